```python
import jax
import jax.numpy as jnp
from jax import lax
import numpy as np


D_MODEL = 1024
BATCH = 8
SEQ = 4096
DEPTH = 2

GRID_W = 64
CTX_LEN = 256
HEAD_DIM = 128
N_Q_HEADS = 8
N_KV_HEADS = 2
Q_PER_KV = N_Q_HEADS // N_KV_HEADS
ATTN_WIDTH = N_Q_HEADS * HEAD_DIM
KV_WIDTH = N_KV_HEADS * HEAD_DIM
WINDOW = 128
BLOCK = 128
BAND = BLOCK + 2 * WINDOW
N_FREQ = HEAD_DIM // 4
ROPE_BASE = 10000.0
D_RNN = D_MODEL
N_RNN_BLOCKS = 8
RNN_BLOCK_W = D_RNN // N_RNN_BLOCKS
CONV_W = 4
CONV_LEFT = 2
LRU_C = 8.0
N_BRANCH = 2
D_FF = ((8 * D_MODEL + 3 * 256 - 1) // (3 * 256)) * 256
IN_SIZES = (D_RNN, D_RNN, ATTN_WIDTH, KV_WIDTH, KV_WIDTH, N_BRANCH * D_MODEL)
IN_WIDTH = sum(IN_SIZES)
MOD_CHUNKS = 6
EPS = 1e-6
NEG_INF = -1e30

kernel_name = 'hybrid_rglru_swa_diffusion_block'


def _rms_norm(x, g):
    xf = x.astype(jnp.float32)
    y = xf * lax.rsqrt(jnp.mean(xf * xf, axis=-1, keepdims=True) + EPS)
    return y.astype(x.dtype) * g


def _split_in(p):
    offs = np.cumsum(IN_SIZES)[:-1].tolist()
    return jnp.split(p, offs, axis=-1)


def _rope_tables(seq_len):
    rows = seq_len // GRID_W
    row = jnp.repeat(jnp.arange(rows, dtype=jnp.int32), GRID_W)
    col = jnp.tile(jnp.arange(GRID_W, dtype=jnp.int32), rows)
    inv = ROPE_BASE ** (-jnp.arange(N_FREQ, dtype=jnp.float32) / N_FREQ)
    ang_r = row.astype(jnp.float32)[:, None] * inv[None, :]
    ang_c = col.astype(jnp.float32)[:, None] * inv[None, :]
    return (jnp.cos(ang_r), jnp.sin(ang_r), jnp.cos(ang_c), jnp.sin(ang_c))


def _rotate(x, cos, sin):
    x1, x2 = jnp.split(x, 2, axis=-1)
    cos = cos[None, :, None, :].astype(x.dtype)
    sin = sin[None, :, None, :].astype(x.dtype)
    return jnp.concatenate([x1 * cos - x2 * sin, x2 * cos + x1 * sin], axis=-1)


def _rope_2d(x, rope):
    cr, sr, cc, sc = rope
    xr, xc = jnp.split(x, 2, axis=-1)
    return jnp.concatenate([_rotate(xr, cr, sr), _rotate(xc, cc, sc)], axis=-1)


def _dwconv_centred(x, w, b):
    s = x.shape[1]
    xp = jnp.pad(x, ((0, 0), (CONV_LEFT, CONV_W - 1 - CONV_LEFT), (0, 0)))
    y = xp[:, 0:s] * w[0]
    for k in range(1, CONV_W):
        y = y + xp[:, k:k + s] * w[k]
    return y + b


def _block_diag(x, w, b):
    xb = x.reshape(x.shape[:-1] + (N_RNN_BLOCKS, RNN_BLOCK_W))
    y = jnp.einsum('bsni,nij->bsnj', xb, w)
    return y.reshape(x.shape) + b


def _lru_coeffs(x, wa, ba, wx, bx, lam):
    r = jax.nn.sigmoid(_block_diag(x, wa, ba)).astype(jnp.float32)
    i = jax.nn.sigmoid(_block_diag(x, wx, bx))
    log_a = LRU_C * r * jax.nn.log_sigmoid(lam.astype(jnp.float32))
    a = jnp.exp(log_a)
    b = jnp.sqrt(-jnp.expm1(2.0 * log_a)) * (i * x).astype(jnp.float32)
    return a, b


def _linear_scan(a, b, h0, reverse):
    def combine(l, r):
        return l[0] * r[0], r[0] * l[1] + r[1]
    a_cum, h = lax.associative_scan(combine, (a, b), axis=1, reverse=reverse)
    if h0 is None:
        return h
    return h + a_cum * h0[:, None, :]


def _rglru_branch(x_lat, x_ctx, conv_w, conv_b, wa, ba, wx, bx, lam, need_ctx):
    xl = _dwconv_centred(x_lat, conv_w, conv_b)
    xc = _dwconv_centred(x_ctx, conv_w, conv_b)
    ac_f, bc_f = _lru_coeffs(xc, wa[0], ba[0], wx[0], bx[0], lam[0])
    ac_b, bc_b = _lru_coeffs(xc, wa[1], ba[1], wx[1], bx[1], lam[1])
    hc_f = _linear_scan(ac_f, bc_f, None, False)
    hc_b = _linear_scan(ac_b, bc_b, None, True)
    al_f, bl_f = _lru_coeffs(xl, wa[0], ba[0], wx[0], bx[0], lam[0])
    al_b, bl_b = _lru_coeffs(xl, wa[1], ba[1], wx[1], bx[1], lam[1])
    hl = (_linear_scan(al_f, bl_f, hc_f[:, -1], False)
          + _linear_scan(al_b, bl_b, hc_b[:, 0], True))
    y_lat = hl.astype(x_lat.dtype)
    y_ctx = (hc_f + hc_b).astype(x_ctx.dtype) if need_ctx else None
    return y_lat, y_ctx


def _band(t, nb):
    n_side = WINDOW // BLOCK
    tp = jnp.pad(t, ((0, 0), (WINDOW, WINDOW), (0, 0), (0, 0)))
    tp = tp.reshape(t.shape[0], nb + 2 * n_side, BLOCK, t.shape[2], t.shape[3])
    return jnp.concatenate([tp[:, j:j + nb] for j in range(2 * n_side + 1)], axis=2)


def _latent_attention(q, k, v, kc, vc, sink):
    bsz, s = q.shape[0], q.shape[1]
    nb = s // BLOCK
    n_ctx = kc.shape[1]
    scale = HEAD_DIM ** -0.5
    qb = q.reshape(bsz, nb, BLOCK, N_KV_HEADS, Q_PER_KV, HEAD_DIM)
    kb = _band(k, nb)
    vb = _band(v, nb)
    s_band = jnp.einsum('bnqkgd,bnpkd->bnkgqp', qb, kb).astype(jnp.float32) * scale
    q_pos = jnp.arange(nb)[:, None] * BLOCK + jnp.arange(BLOCK)[None, :]
    k_pos = jnp.arange(nb)[:, None] * BLOCK - WINDOW + jnp.arange(BAND)[None, :]
    kp = k_pos[:, None, :]
    valid = (jnp.abs(kp - q_pos[:, :, None]) <= WINDOW) & (kp >= 0) & (kp < s)
    s_band = jnp.where(valid[None, :, None, None], s_band, NEG_INF)
    s_ctx = jnp.einsum('bnqkgd,bckd->bnkgqc', qb, kc).astype(jnp.float32) * scale
    s_sink = jnp.broadcast_to(
        sink.astype(jnp.float32).reshape(1, 1, N_KV_HEADS, Q_PER_KV, 1, 1),
        s_band.shape[:-1] + (1,))
    p = jax.nn.softmax(jnp.concatenate([s_band, s_ctx, s_sink], axis=-1), axis=-1)
    p_band = p[..., :BAND].astype(v.dtype)
    p_ctx = p[..., BAND:BAND + n_ctx].astype(v.dtype)
    o = (jnp.einsum('bnkgqp,bnpkd->bnqkgd', p_band, vb)
         + jnp.einsum('bnkgqc,bckd->bnqkgd', p_ctx, vc))
    return o.reshape(bsz, s, ATTN_WIDTH)


def _context_attention(qc, kc, vc, sink):
    bsz, n_ctx = qc.shape[0], qc.shape[1]
    scale = HEAD_DIM ** -0.5
    qg = qc.reshape(bsz, n_ctx, N_KV_HEADS, Q_PER_KV, HEAD_DIM)
    sc = jnp.einsum('bqkgd,bckd->bkgqc', qg, kc).astype(jnp.float32) * scale
    s_sink = jnp.broadcast_to(
        sink.astype(jnp.float32).reshape(1, N_KV_HEADS, Q_PER_KV, 1, 1),
        sc.shape[:-1] + (1,))
    p = jax.nn.softmax(jnp.concatenate([sc, s_sink], axis=-1), axis=-1)
    o = jnp.einsum('bkgqc,bckd->bqkgd', p[..., :n_ctx].astype(vc.dtype), vc)
    return o.reshape(bsz, n_ctx, ATTN_WIDTH)


def _merge(y_rnn, g_rnn, y_attn, gate_logits, w_o_rnn, w_o_attn, w_out):
    ya = (y_rnn * jax.nn.gelu(g_rnn)) @ w_o_rnn
    yb = y_attn @ w_o_attn
    ga, gb = jnp.split(jax.nn.sigmoid(gate_logits), N_BRANCH, axis=-1)
    return (ga * ya + gb * yb) @ w_out


def _mixer(h, hc, rope, w_in, conv_w, conv_b, wa, ba, wx, bx, lam, sink,
           w_o_rnn, w_o_attn, w_out, need_ctx):
    bsz, s = h.shape[0], h.shape[1]
    n_ctx = hc.shape[1]
    xr, gr, q, k, v, gl = _split_in(h @ w_in)
    xrc, grc, qc, kc, vc, glc = _split_in(hc @ w_in)
    y_rnn, y_rnn_c = _rglru_branch(xr, xrc, conv_w, conv_b, wa, ba, wx, bx, lam, need_ctx)
    q = _rope_2d(q.reshape(bsz, s, N_Q_HEADS, HEAD_DIM), rope)
    k = _rope_2d(k.reshape(bsz, s, N_KV_HEADS, HEAD_DIM), rope)
    v = v.reshape(bsz, s, N_KV_HEADS, HEAD_DIM)
    kc = kc.reshape(bsz, n_ctx, N_KV_HEADS, HEAD_DIM)
    vc = vc.reshape(bsz, n_ctx, N_KV_HEADS, HEAD_DIM)
    o = _latent_attention(q, k, v, kc, vc, sink)
    out = _merge(y_rnn, gr, o, gl, w_o_rnn, w_o_attn, w_out)
    out_c = None
    if need_ctx:
        oc = _context_attention(qc.reshape(bsz, n_ctx, N_Q_HEADS, HEAD_DIM), kc, vc, sink)
        out_c = _merge(y_rnn_c, grc, oc, glc, w_o_rnn, w_o_attn, w_out)
    return out, out_c


def _swiglu(h, w_ffn_in, w_ffn_out):
    gate, up = jnp.split(h @ w_ffn_in, 2, axis=-1)
    return (jax.nn.silu(gate) * up) @ w_ffn_out


def _fwd_setup_inputs(seed: int = 0) -> dict:
    key = jax.random.key(seed)
    ks = jax.random.split(key, 24)
    f32 = jnp.float32

    def nrm(k, shape, scale):
        return jax.random.normal(k, shape, f32) * scale

    L = DEPTH
    a0 = jax.random.uniform(ks[15], (L, 2, D_RNN), f32, 0.9, 0.999)
    return {
        'x': nrm(ks[0], (BATCH, SEQ, D_MODEL), 1.0),
        'c': nrm(ks[1], (BATCH, D_MODEL), 1.0),
        'ctx': nrm(ks[2], (BATCH, CTX_LEN, D_MODEL), 1.0),
        'c_ctx': nrm(ks[3], (D_MODEL,), 1.0),
        'w_mod': nrm(ks[4], (L, D_MODEL, MOD_CHUNKS * D_MODEL), 0.5 * D_MODEL ** -0.5),
        'b_mod': nrm(ks[5], (L, MOD_CHUNKS * D_MODEL), 0.02),
        'g_mix_pre': 1.0 + nrm(ks[6], (L, D_MODEL), 0.02),
        'g_mix_post': 1.0 + nrm(ks[7], (L, D_MODEL), 0.02),
        'g_ffn_pre': 1.0 + nrm(ks[8], (L, D_MODEL), 0.02),
        'g_ffn_post': 1.0 + nrm(ks[9], (L, D_MODEL), 0.02),
        'w_in': nrm(ks[10], (L, D_MODEL, IN_WIDTH), D_MODEL ** -0.5),
        'conv_w': nrm(ks[11], (L, CONV_W, D_RNN), CONV_W ** -0.5),
        'conv_b': nrm(ks[12], (L, D_RNN), 0.02),
        'lru_wa': nrm(ks[13], (L, 2, N_RNN_BLOCKS, RNN_BLOCK_W, RNN_BLOCK_W), RNN_BLOCK_W ** -0.5),
        'lru_ba': nrm(ks[14], (L, 2, D_RNN), 0.02),
        'lru_wx': nrm(ks[16], (L, 2, N_RNN_BLOCKS, RNN_BLOCK_W, RNN_BLOCK_W), RNN_BLOCK_W ** -0.5),
        'lru_bx': nrm(ks[17], (L, 2, D_RNN), 0.02),
        'lru_lam': jnp.log(a0) - jnp.log1p(-a0),
        'attn_sink': nrm(ks[18], (L, N_Q_HEADS), 0.5),
        'w_o_rnn': nrm(ks[19], (L, D_RNN, D_MODEL), D_RNN ** -0.5),
        'w_o_attn': nrm(ks[20], (L, ATTN_WIDTH, D_MODEL), ATTN_WIDTH ** -0.5),
        'w_out': nrm(ks[21], (L, D_MODEL, D_MODEL), D_MODEL ** -0.5),
        'w_ffn_in': nrm(ks[22], (L, D_MODEL, 2 * D_FF), D_MODEL ** -0.5),
        'w_ffn_out': nrm(ks[23], (L, D_FF, D_MODEL), D_FF ** -0.5),
    }


def _fwd_reference(x, c, ctx, c_ctx, w_mod, b_mod, g_mix_pre, g_mix_post, g_ffn_pre,
              g_ffn_post, w_in, conv_w, conv_b, lru_wa, lru_ba, lru_wx, lru_bx,
              lru_lam, attn_sink, w_o_rnn, w_o_attn, w_out, w_ffn_in, w_ffn_out):
    rope = _rope_tables(x.shape[1])
    for l in range(DEPTH):
        need_ctx = l < DEPTH - 1
        mod = jax.nn.silu(c) @ w_mod[l] + b_mod[l]
        sh1, sc1, ga1, sh2, sc2, ga2 = jnp.split(mod[:, None, :], MOD_CHUNKS, axis=-1)
        mod_c = jax.nn.silu(c_ctx) @ w_mod[l] + b_mod[l]
        sh1c, sc1c, ga1c, sh2c, sc2c, ga2c = jnp.split(mod_c, MOD_CHUNKS, axis=-1)

        h = _rms_norm(x, g_mix_pre[l]) * (1.0 + sc1) + sh1
        hc = _rms_norm(ctx, g_mix_pre[l]) * (1.0 + sc1c) + sh1c
        m, mc = _mixer(h, hc, rope, w_in[l], conv_w[l], conv_b[l], lru_wa[l], lru_ba[l],
                       lru_wx[l], lru_bx[l], lru_lam[l], attn_sink[l], w_o_rnn[l],
                       w_o_attn[l], w_out[l], need_ctx)
        x = x + ga1 * _rms_norm(m, g_mix_post[l])
        if need_ctx:
            ctx = ctx + ga1c * _rms_norm(mc, g_mix_post[l])

        h = _rms_norm(x, g_ffn_pre[l]) * (1.0 + sc2) + sh2
        x = x + ga2 * _rms_norm(_swiglu(h, w_ffn_in[l], w_ffn_out[l]), g_ffn_post[l])
        if need_ctx:
            hc = _rms_norm(ctx, g_ffn_pre[l]) * (1.0 + sc2c) + sh2c
            ctx = ctx + ga2c * _rms_norm(_swiglu(hc, w_ffn_in[l], w_ffn_out[l]), g_ffn_post[l])
    return x


import jax as _jax
import jax.numpy as _jnp

TWIN_FORMAT = 'train_step'
FWD_PARAMS = ['x', 'c', 'ctx', 'c_ctx', 'w_mod', 'b_mod', 'g_mix_pre', 'g_mix_post', 'g_ffn_pre', 'g_ffn_post', 'w_in', 'conv_w', 'conv_b', 'lru_wa', 'lru_ba', 'lru_wx', 'lru_bx', 'lru_lam', 'attn_sink', 'w_o_rnn', 'w_o_attn', 'w_out', 'w_ffn_in', 'w_ffn_out']
TWIN_WEIGHTS = ['c_ctx', 'w_mod', 'b_mod', 'g_mix_pre', 'g_mix_post', 'g_ffn_pre', 'g_ffn_post', 'w_in', 'conv_w', 'conv_b', 'lru_wa', 'lru_ba', 'lru_wx', 'lru_bx', 'lru_lam', 'attn_sink', 'w_o_rnn', 'w_o_attn', 'w_out', 'w_ffn_in', 'w_ffn_out']
TWIN_DIFF_INPUT = 'x'
TWIN_INPUTS = ['x', 'c', 'ctx', 'c_ctx', 'w_mod', 'b_mod', 'g_mix_pre', 'g_mix_post', 'g_ffn_pre', 'g_ffn_post', 'w_in', 'conv_w', 'conv_b', 'lru_wa', 'lru_ba', 'lru_wx', 'lru_bx', 'lru_lam', 'attn_sink', 'w_o_rnn', 'w_o_attn', 'w_out', 'w_ffn_in', 'w_ffn_out', 'loss_target', 'm_c_ctx', 'm_w_mod', 'm_b_mod', 'm_g_mix_pre', 'm_g_mix_post', 'm_g_ffn_pre', 'm_g_ffn_post', 'm_w_in', 'm_conv_w', 'm_conv_b', 'm_lru_wa', 'm_lru_ba', 'm_lru_wx', 'm_lru_bx', 'm_lru_lam', 'm_attn_sink', 'm_w_o_rnn', 'm_w_o_attn', 'm_w_out', 'm_w_ffn_in', 'm_w_ffn_out', 'v_c_ctx', 'v_w_mod', 'v_b_mod', 'v_g_mix_pre', 'v_g_mix_post', 'v_g_ffn_pre', 'v_g_ffn_post', 'v_w_in', 'v_conv_w', 'v_conv_b', 'v_lru_wa', 'v_lru_ba', 'v_lru_wx', 'v_lru_bx', 'v_lru_lam', 'v_attn_sink', 'v_w_o_rnn', 'v_w_o_attn', 'v_w_out', 'v_w_ffn_in', 'v_w_ffn_out']
TWIN_OUTPUTS = ['loss', 'grad_x', 'grad_c_ctx', 'grad_w_mod', 'grad_b_mod', 'grad_g_mix_pre', 'grad_g_mix_post', 'grad_g_ffn_pre', 'grad_g_ffn_post', 'grad_w_in', 'grad_conv_w', 'grad_conv_b', 'grad_lru_wa', 'grad_lru_ba', 'grad_lru_wx', 'grad_lru_bx', 'grad_lru_lam', 'grad_attn_sink', 'grad_w_o_rnn', 'grad_w_o_attn', 'grad_w_out', 'grad_w_ffn_in', 'grad_w_ffn_out', 'delta_c_ctx', 'delta_w_mod', 'delta_b_mod', 'delta_g_mix_pre', 'delta_g_mix_post', 'delta_g_ffn_pre', 'delta_g_ffn_post', 'delta_w_in', 'delta_conv_w', 'delta_conv_b', 'delta_lru_wa', 'delta_lru_ba', 'delta_lru_wx', 'delta_lru_bx', 'delta_lru_lam', 'delta_attn_sink', 'delta_w_o_rnn', 'delta_w_o_attn', 'delta_w_out', 'delta_w_ffn_in', 'delta_w_ffn_out', 'new_m_c_ctx', 'new_m_w_mod', 'new_m_b_mod', 'new_m_g_mix_pre', 'new_m_g_mix_post', 'new_m_g_ffn_pre', 'new_m_g_ffn_post', 'new_m_w_in', 'new_m_conv_w', 'new_m_conv_b', 'new_m_lru_wa', 'new_m_lru_ba', 'new_m_lru_wx', 'new_m_lru_bx', 'new_m_lru_lam', 'new_m_attn_sink', 'new_m_w_o_rnn', 'new_m_w_o_attn', 'new_m_w_out', 'new_m_w_ffn_in', 'new_m_w_ffn_out', 'new_v_c_ctx', 'new_v_w_mod', 'new_v_b_mod', 'new_v_g_mix_pre', 'new_v_g_mix_post', 'new_v_g_ffn_pre', 'new_v_g_ffn_post', 'new_v_w_in', 'new_v_conv_w', 'new_v_conv_b', 'new_v_lru_wa', 'new_v_lru_ba', 'new_v_lru_wx', 'new_v_lru_bx', 'new_v_lru_lam', 'new_v_attn_sink', 'new_v_w_o_rnn', 'new_v_w_o_attn', 'new_v_w_out', 'new_v_w_ffn_in', 'new_v_w_ffn_out']
TWIN_LEAF_KINDS = {'loss': 'loss', 'grad_x': 'grad_x', 'grad_c_ctx': 'grad_w', 'grad_w_mod': 'grad_w', 'grad_b_mod': 'grad_w', 'grad_g_mix_pre': 'grad_w', 'grad_g_mix_post': 'grad_w', 'grad_g_ffn_pre': 'grad_w', 'grad_g_ffn_post': 'grad_w', 'grad_w_in': 'grad_w', 'grad_conv_w': 'grad_w', 'grad_conv_b': 'grad_w', 'grad_lru_wa': 'grad_w', 'grad_lru_ba': 'grad_w', 'grad_lru_wx': 'grad_w', 'grad_lru_bx': 'grad_w', 'grad_lru_lam': 'grad_w', 'grad_attn_sink': 'grad_w', 'grad_w_o_rnn': 'grad_w', 'grad_w_o_attn': 'grad_w', 'grad_w_out': 'grad_w', 'grad_w_ffn_in': 'grad_w', 'grad_w_ffn_out': 'grad_w', 'delta_c_ctx': 'delta_w', 'delta_w_mod': 'delta_w', 'delta_b_mod': 'delta_w', 'delta_g_mix_pre': 'delta_w', 'delta_g_mix_post': 'delta_w', 'delta_g_ffn_pre': 'delta_w', 'delta_g_ffn_post': 'delta_w', 'delta_w_in': 'delta_w', 'delta_conv_w': 'delta_w', 'delta_conv_b': 'delta_w', 'delta_lru_wa': 'delta_w', 'delta_lru_ba': 'delta_w', 'delta_lru_wx': 'delta_w', 'delta_lru_bx': 'delta_w', 'delta_lru_lam': 'delta_w', 'delta_attn_sink': 'delta_w', 'delta_w_o_rnn': 'delta_w', 'delta_w_o_attn': 'delta_w', 'delta_w_out': 'delta_w', 'delta_w_ffn_in': 'delta_w', 'delta_w_ffn_out': 'delta_w', 'new_m_c_ctx': 'new_m', 'new_m_w_mod': 'new_m', 'new_m_b_mod': 'new_m', 'new_m_g_mix_pre': 'new_m', 'new_m_g_mix_post': 'new_m', 'new_m_g_ffn_pre': 'new_m', 'new_m_g_ffn_post': 'new_m', 'new_m_w_in': 'new_m', 'new_m_conv_w': 'new_m', 'new_m_conv_b': 'new_m', 'new_m_lru_wa': 'new_m', 'new_m_lru_ba': 'new_m', 'new_m_lru_wx': 'new_m', 'new_m_lru_bx': 'new_m', 'new_m_lru_lam': 'new_m', 'new_m_attn_sink': 'new_m', 'new_m_w_o_rnn': 'new_m', 'new_m_w_o_attn': 'new_m', 'new_m_w_out': 'new_m', 'new_m_w_ffn_in': 'new_m', 'new_m_w_ffn_out': 'new_m', 'new_v_c_ctx': 'new_v', 'new_v_w_mod': 'new_v', 'new_v_b_mod': 'new_v', 'new_v_g_mix_pre': 'new_v', 'new_v_g_mix_post': 'new_v', 'new_v_g_ffn_pre': 'new_v', 'new_v_g_ffn_post': 'new_v', 'new_v_w_in': 'new_v', 'new_v_conv_w': 'new_v', 'new_v_conv_b': 'new_v', 'new_v_lru_wa': 'new_v', 'new_v_lru_ba': 'new_v', 'new_v_lru_wx': 'new_v', 'new_v_lru_bx': 'new_v', 'new_v_lru_lam': 'new_v', 'new_v_attn_sink': 'new_v', 'new_v_w_o_rnn': 'new_v', 'new_v_w_o_attn': 'new_v', 'new_v_w_out': 'new_v', 'new_v_w_ffn_in': 'new_v', 'new_v_w_ffn_out': 'new_v'}


def _forward(args):
    return _fwd_reference(*[args[k] for k in FWD_PARAMS])


def _output_shape():
    def fwd():
        inp = _fwd_setup_inputs(0)
        return _fwd_reference(*[inp[k] for k in FWD_PARAMS])
    out = _jax.eval_shape(fwd)
    return out.shape, out.dtype

N_MICROBATCH = 1
ADAM_LR = 0.001
ADAM_B1 = 0.9
ADAM_B2 = 0.999
ADAM_EPS = 1e-08
ADAM_WD = 0.01
ADAM_STEP = 10
PER_EXAMPLE_BATCH_AXIS = {'x': 0, 'c': 0, 'ctx': 0, 'loss_target': 0}
SHARED_INPUTS = []
_WEIGHT_DTYPES = {'c_ctx': _jnp.float32, 'w_mod': _jnp.float32, 'b_mod': _jnp.float32, 'g_mix_pre': _jnp.float32, 'g_mix_post': _jnp.float32, 'g_ffn_pre': _jnp.float32, 'g_ffn_post': _jnp.float32, 'w_in': _jnp.float32, 'conv_w': _jnp.float32, 'conv_b': _jnp.float32, 'lru_wa': _jnp.float32, 'lru_ba': _jnp.float32, 'lru_wx': _jnp.float32, 'lru_bx': _jnp.float32, 'lru_lam': _jnp.float32, 'attn_sink': _jnp.float32, 'w_o_rnn': _jnp.float32, 'w_o_attn': _jnp.float32, 'w_out': _jnp.float32, 'w_ffn_in': _jnp.float32, 'w_ffn_out': _jnp.float32}
MOMENT_SCALE = {'c_ctx': 1.268714e-01, 'w_mod': 1.283728e+00, 'b_mod': 2.708220e+00, 'g_mix_pre': 1.576218e-01, 'g_mix_post': 3.348006e+00, 'g_ffn_pre': 1.458000e-01, 'g_ffn_post': 3.332194e+00, 'w_in': 1.421924e-01, 'conv_w': 2.837505e-01, 'conv_b': 9.443919e-01, 'lru_wa': 1.561915e-02, 'lru_ba': 2.658042e-02, 'lru_wx': 3.128372e-02, 'lru_bx': 5.302985e-02, 'lru_lam': 6.725010e-02, 'attn_sink': 2.191817e-03, 'w_o_rnn': 2.778595e-01, 'w_o_attn': 1.442544e-01, 'w_out': 3.121156e-01, 'w_ffn_in': 7.014799e-02, 'w_ffn_out': 1.275894e-01}


def _to_microbatches(a, axis):
    t = _jnp.moveaxis(a, axis, 0)
    t = t.reshape((N_MICROBATCH, t.shape[0] // N_MICROBATCH) + t.shape[1:])
    return _jnp.moveaxis(t, 1, axis + 1)


def setup_inputs(seed: int = 0) -> dict:
    inp = _fwd_setup_inputs(seed)
    key = _jax.random.fold_in(_jax.random.key(seed), 7919)
    shape, _ = _output_shape()
    out = dict(inp)
    out["loss_target"] = _jax.random.normal(_jax.random.fold_in(key, 0), shape, _jnp.float32)
    for i, name in enumerate(TWIN_WEIGHTS):
        w = inp[name].astype(_jnp.float32)
        if MOMENT_SCALE is None:
            s = _jnp.sqrt(_jnp.mean(_jnp.square(w)) + 1e-30)
        else:
            s = MOMENT_SCALE[name]
        km, kv = _jax.random.split(_jax.random.fold_in(key, i + 1))
        out[name] = w
        out["m_" + name] = s * _jax.random.normal(km, w.shape, _jnp.float32)
        out["v_" + name] = (s * s) * _jax.random.uniform(kv, w.shape, _jnp.float32, 0.5, 1.5)
    if N_MICROBATCH > 1:
        for name, axis in PER_EXAMPLE_BATCH_AXIS.items():
            out[name] = _to_microbatches(out[name], axis)
    return {'x': out['x'], 'c': out['c'], 'ctx': out['ctx'], 'c_ctx': out['c_ctx'], 'w_mod': out['w_mod'], 'b_mod': out['b_mod'], 'g_mix_pre': out['g_mix_pre'], 'g_mix_post': out['g_mix_post'], 'g_ffn_pre': out['g_ffn_pre'], 'g_ffn_post': out['g_ffn_post'], 'w_in': out['w_in'], 'conv_w': out['conv_w'], 'conv_b': out['conv_b'], 'lru_wa': out['lru_wa'], 'lru_ba': out['lru_ba'], 'lru_wx': out['lru_wx'], 'lru_bx': out['lru_bx'], 'lru_lam': out['lru_lam'], 'attn_sink': out['attn_sink'], 'w_o_rnn': out['w_o_rnn'], 'w_o_attn': out['w_o_attn'], 'w_out': out['w_out'], 'w_ffn_in': out['w_ffn_in'], 'w_ffn_out': out['w_ffn_out'], 'loss_target': out['loss_target'], 'm_c_ctx': out['m_c_ctx'], 'm_w_mod': out['m_w_mod'], 'm_b_mod': out['m_b_mod'], 'm_g_mix_pre': out['m_g_mix_pre'], 'm_g_mix_post': out['m_g_mix_post'], 'm_g_ffn_pre': out['m_g_ffn_pre'], 'm_g_ffn_post': out['m_g_ffn_post'], 'm_w_in': out['m_w_in'], 'm_conv_w': out['m_conv_w'], 'm_conv_b': out['m_conv_b'], 'm_lru_wa': out['m_lru_wa'], 'm_lru_ba': out['m_lru_ba'], 'm_lru_wx': out['m_lru_wx'], 'm_lru_bx': out['m_lru_bx'], 'm_lru_lam': out['m_lru_lam'], 'm_attn_sink': out['m_attn_sink'], 'm_w_o_rnn': out['m_w_o_rnn'], 'm_w_o_attn': out['m_w_o_attn'], 'm_w_out': out['m_w_out'], 'm_w_ffn_in': out['m_w_ffn_in'], 'm_w_ffn_out': out['m_w_ffn_out'], 'v_c_ctx': out['v_c_ctx'], 'v_w_mod': out['v_w_mod'], 'v_b_mod': out['v_b_mod'], 'v_g_mix_pre': out['v_g_mix_pre'], 'v_g_mix_post': out['v_g_mix_post'], 'v_g_ffn_pre': out['v_g_ffn_pre'], 'v_g_ffn_post': out['v_g_ffn_post'], 'v_w_in': out['v_w_in'], 'v_conv_w': out['v_conv_w'], 'v_conv_b': out['v_conv_b'], 'v_lru_wa': out['v_lru_wa'], 'v_lru_ba': out['v_lru_ba'], 'v_lru_wx': out['v_lru_wx'], 'v_lru_bx': out['v_lru_bx'], 'v_lru_lam': out['v_lru_lam'], 'v_attn_sink': out['v_attn_sink'], 'v_w_o_rnn': out['v_w_o_rnn'], 'v_w_o_attn': out['v_w_o_attn'], 'v_w_out': out['v_w_out'], 'v_w_ffn_in': out['v_w_ffn_in'], 'v_w_ffn_out': out['v_w_ffn_out']}


def _loss(weights, diff, rest, loss_target):
    with _jax.named_scope("forward"):
        args = {**rest, TWIN_DIFF_INPUT: diff, **{k: w.astype(_WEIGHT_DTYPES[k]) for k, w in weights.items()}}
        y = _forward(args)
    with _jax.named_scope("loss_head"):
        err = _jnp.square(y.astype(_jnp.float32) - loss_target)
        return 0.5 * _jnp.sum(_jnp.mean(err, axis=-1)) if err.ndim else 0.5 * err


def _adamw(w, g, m, v):
    m = ADAM_B1 * m + (1.0 - ADAM_B1) * g
    v = ADAM_B2 * v + (1.0 - ADAM_B2) * _jnp.square(g)
    m_hat = m / (1.0 - ADAM_B1 ** ADAM_STEP)
    v_hat = v / (1.0 - ADAM_B2 ** ADAM_STEP)
    delta = -ADAM_LR * (m_hat / (_jnp.sqrt(v_hat) + ADAM_EPS) + ADAM_WD * w)
    return delta, m, v


def reference(x, c, ctx, c_ctx, w_mod, b_mod, g_mix_pre, g_mix_post, g_ffn_pre, g_ffn_post, w_in, conv_w, conv_b, lru_wa, lru_ba, lru_wx, lru_bx, lru_lam, attn_sink, w_o_rnn, w_o_attn, w_out, w_ffn_in, w_ffn_out, loss_target, m_c_ctx, m_w_mod, m_b_mod, m_g_mix_pre, m_g_mix_post, m_g_ffn_pre, m_g_ffn_post, m_w_in, m_conv_w, m_conv_b, m_lru_wa, m_lru_ba, m_lru_wx, m_lru_bx, m_lru_lam, m_attn_sink, m_w_o_rnn, m_w_o_attn, m_w_out, m_w_ffn_in, m_w_ffn_out, v_c_ctx, v_w_mod, v_b_mod, v_g_mix_pre, v_g_mix_post, v_g_ffn_pre, v_g_ffn_post, v_w_in, v_conv_w, v_conv_b, v_lru_wa, v_lru_ba, v_lru_wx, v_lru_bx, v_lru_lam, v_attn_sink, v_w_o_rnn, v_w_o_attn, v_w_out, v_w_ffn_in, v_w_ffn_out):
    given = dict(x=x, c=c, ctx=ctx, c_ctx=c_ctx, w_mod=w_mod, b_mod=b_mod, g_mix_pre=g_mix_pre, g_mix_post=g_mix_post, g_ffn_pre=g_ffn_pre, g_ffn_post=g_ffn_post, w_in=w_in, conv_w=conv_w, conv_b=conv_b, lru_wa=lru_wa, lru_ba=lru_ba, lru_wx=lru_wx, lru_bx=lru_bx, lru_lam=lru_lam, attn_sink=attn_sink, w_o_rnn=w_o_rnn, w_o_attn=w_o_attn, w_out=w_out, w_ffn_in=w_ffn_in, w_ffn_out=w_ffn_out, loss_target=loss_target, m_c_ctx=m_c_ctx, m_w_mod=m_w_mod, m_b_mod=m_b_mod, m_g_mix_pre=m_g_mix_pre, m_g_mix_post=m_g_mix_post, m_g_ffn_pre=m_g_ffn_pre, m_g_ffn_post=m_g_ffn_post, m_w_in=m_w_in, m_conv_w=m_conv_w, m_conv_b=m_conv_b, m_lru_wa=m_lru_wa, m_lru_ba=m_lru_ba, m_lru_wx=m_lru_wx, m_lru_bx=m_lru_bx, m_lru_lam=m_lru_lam, m_attn_sink=m_attn_sink, m_w_o_rnn=m_w_o_rnn, m_w_o_attn=m_w_o_attn, m_w_out=m_w_out, m_w_ffn_in=m_w_ffn_in, m_w_ffn_out=m_w_ffn_out, v_c_ctx=v_c_ctx, v_w_mod=v_w_mod, v_b_mod=v_b_mod, v_g_mix_pre=v_g_mix_pre, v_g_mix_post=v_g_mix_post, v_g_ffn_pre=v_g_ffn_pre, v_g_ffn_post=v_g_ffn_post, v_w_in=v_w_in, v_conv_w=v_conv_w, v_conv_b=v_conv_b, v_lru_wa=v_lru_wa, v_lru_ba=v_lru_ba, v_lru_wx=v_lru_wx, v_lru_bx=v_lru_bx, v_lru_lam=v_lru_lam, v_attn_sink=v_attn_sink, v_w_o_rnn=v_w_o_rnn, v_w_o_attn=v_w_o_attn, v_w_out=v_w_out, v_w_ffn_in=v_w_ffn_in, v_w_ffn_out=v_w_ffn_out)
    weights = {n: given[n] for n in TWIN_WEIGHTS}
    shared = {n: given[n] for n in SHARED_INPUTS}
    per_example = {n: given[n] for n in ['x', 'c', 'ctx']}
    grad_fn = _jax.value_and_grad(_loss, argnums=(0, 1))

    def one_microbatch(ex, loss_target):
        ex = dict(ex)
        diff = ex.pop(TWIN_DIFF_INPUT)
        return grad_fn(weights, diff, {**shared, **ex}, loss_target)

    if N_MICROBATCH == 1:
        loss, (grad_w, grad_x) = one_microbatch(per_example, given["loss_target"])
    else:
        def body(carry, xs):
            loss_sum, grad_sum = carry
            l_k, (gw_k, gx_k) = one_microbatch(xs[0], xs[1])
            with _jax.named_scope("update"):
                return (loss_sum + l_k, _jax.tree.map(_jnp.add, grad_sum, gw_k)), gx_k

        init = (_jnp.zeros((), _jnp.float32), _jax.tree.map(_jnp.zeros_like, weights))
        (loss, grad_w), grad_x = _jax.lax.scan(body, init, (per_example, given["loss_target"]))
    with _jax.named_scope("update"):
        delta_w, new_m, new_v = {}, {}, {}
        for n in TWIN_WEIGHTS:
            delta_w[n], new_m[n], new_v[n] = _adamw(weights[n], grad_w[n], given["m_" + n], given["v_" + n])
    return (loss, grad_x, *[grad_w[n] for n in TWIN_WEIGHTS], *[delta_w[n] for n in TWIN_WEIGHTS],
            *[new_m[n] for n in TWIN_WEIGHTS], *[new_v[n] for n in TWIN_WEIGHTS])
```

```python
import functools
import math

import numpy as np
import jax
import jax.numpy as jnp
from jax import lax
from jax.experimental import pallas as pl
from jax.experimental.pallas import tpu as pltpu

F32 = jnp.float32
BF16 = jnp.bfloat16

D = 1024
CTX = 256
TR = 256
HEAD = 128
N_Q = 8
N_KV = 2
Q_PER_KV = N_Q // N_KV
GRID_W = 64
N_FREQ = HEAD // 4
ROPE_BASE = 10000.0
N_RNN_BLOCKS = 8
CONV_W = 4
CONV_LEFT = 2
LRU_C = 8.0
D_FF = 2816
IN_W = 5632
P_W = 6144
PAD_BLK = 7
COL_XR, COL_GR, COL_Q, COL_K, COL_V, COL_GL = 0, 1024, 2048, 3072, 3328, 4096
EPS = 1e-6
NEG_INF = -1e30
ATT_SCALE = HEAD ** -0.5
N_DEV = 8
VMEM_LIMIT = 56 * 1024 * 1024

ADAM_LR, ADAM_B1, ADAM_B2, ADAM_EPS, ADAM_WD, ADAM_STEP = 0.001, 0.9, 0.999, 1e-08, 0.01, 10

NN = (((1,), (0,)), ((), ()))
NT = (((1,), (1,)), ((), ()))
TN = (((0,), (0,)), ((), ()))


def _dot(a, b, dims=NN):
    return lax.dot_general(a, b, dims, preferred_element_type=F32)


def _params(sem=("arbitrary",)):
    return pltpu.CompilerParams(dimension_semantics=sem, vmem_limit_bytes=VMEM_LIMIT)


def _full_spec(shape):
    nd = len(shape)
    return pl.BlockSpec(shape, lambda *_: (0,) * nd)


ANY = pl.BlockSpec(memory_space=pl.ANY)


def _ew(name, body, n, row_ins, pars, row_outs, accs=(), alias=None):
    n_ri, n_p, n_ro, n_acc = len(row_ins), len(pars), len(row_outs), len(accs)

    def kern(*refs):
        i = pl.program_id(0)
        ins = refs[:n_ri]
        ps = refs[n_ri:n_ri + n_p]
        outs = refs[n_ri + n_p:n_ri + n_p + n_ro]
        acc = refs[n_ri + n_p + n_ro:]
        if n_acc:
            @pl.when(i == 0)
            def _():
                for a in acc:
                    a[...] = jnp.zeros(a.shape, a.dtype)
        body(i, ins, ps, outs, acc)

    in_specs = [ANY if blk is None else pl.BlockSpec(blk, imap) for (_, blk, imap) in row_ins]
    in_specs += [_full_spec(p.shape) for p in pars]
    out_specs = [pl.BlockSpec(blk, imap) for (_, blk, imap) in row_outs] + [_full_spec(a.shape) for a in accs]
    out_shape = [s for (s, _, _) in row_outs] + list(accs)
    return pl.pallas_call(
        kern, name=name, grid=(n,), in_specs=in_specs, out_specs=out_specs, out_shape=out_shape,
        input_output_aliases=alias or {}, compiler_params=_params(),
    )(*[a for (a, _, _) in row_ins], *pars)


def _rowblk(width, colblk=0, roff=0, tile=TR):
    return (tile, width), (lambda i: (i + roff, colblk))


def _sds(shape, dtype):
    return jax.ShapeDtypeStruct(shape, dtype)


def _skip_pad(j):
    return jnp.where(j >= PAD_BLK, j + 1, j)


def _mm(name, a, b, mode, out_dtype, tm, tn, tk, M, N, K, a_col=None, o_col=None):
    a_col = a_col or (lambda c: c)
    o_col = o_col or (lambda c: c)
    if mode == "NN":
        a_spec = pl.BlockSpec((tm, tk), lambda j, i, k: (i, a_col(k)))
        b_spec = pl.BlockSpec((tk, tn), lambda j, i, k: (k, j))
        dims = NN
    elif mode == "NT":
        a_spec = pl.BlockSpec((tm, tk), lambda j, i, k: (i, a_col(k)))
        b_spec = pl.BlockSpec((tn, tk), lambda j, i, k: (j, k))
        dims = NT
    else:
        a_spec = pl.BlockSpec((tk, tm), lambda j, i, k: (k, a_col(i)))
        b_spec = pl.BlockSpec((tk, tn), lambda j, i, k: (k, j))
        dims = TN
    assert M % tm == 0 and N % tn == 0 and K % tk == 0, (name, M, N, K, tm, tn, tk)
    nk = K // tk

    def kern(a_ref, b_ref, o_ref, *scr):
        part = _dot(a_ref[...], b_ref[...], dims)
        if nk == 1:
            o_ref[...] = part.astype(o_ref.dtype)
        else:
            acc_ref, = scr
            k = pl.program_id(2)

            @pl.when(k == 0)
            def _():
                acc_ref[...] = part

            @pl.when(k > 0)
            def _():
                acc_ref[...] += part

            @pl.when(k == nk - 1)
            def _():
                o_ref[...] = acc_ref[...].astype(o_ref.dtype)

    return kern, a_spec, b_spec, dims, nk


def _mm_call(name, a, b, mode, out_dtype, tm, tn, tk, M, N, K, out_cols=None, a_col=None, o_col=None):
    kern, a_spec, b_spec, _, nk = _mm(name, a, b, mode, out_dtype, tm, tn, tk, M, N, K, a_col, o_col)
    o_col = o_col or (lambda c: c)
    return pl.pallas_call(
        kern, name=name, grid=(N // tn, M // tm, nk), in_specs=[a_spec, b_spec],
        out_specs=pl.BlockSpec((tm, tn), lambda j, i, k: (i, o_col(j))),
        out_shape=_sds((M, out_cols or N), out_dtype),
        scratch_shapes=[] if nk == 1 else [pltpu.VMEM((tm, tn), F32)],
        compiler_params=_params(("arbitrary", "arbitrary", "arbitrary")),
    )(a, b)


def _mm_act(name, a, w, mode, out_dtype=F32):
    rows, K = a.shape
    N = w.shape[1] if mode == "NN" else w.shape[0]
    tk = K if K <= 1024 or K == D_FF else 512
    tn = N if N <= 1024 else D_FF // 2
    return _mm_call(name, a, w, mode, out_dtype, rows // 4, tn, tk, rows, N, K)


def _mm_proj(name, h, w_in_t):
    T = h.shape[0]
    return _mm_call(name, h, w_in_t, "NT", F32, T // 4, 512, D, T, IN_W, D, out_cols=P_W, o_col=_skip_pad)


def _mm_dproj(name, dp, w_in_t):
    T = dp.shape[0]
    return _mm_call(name, dp, w_in_t, "NN", F32, T // 4, D, 512, T, D, IN_W, a_col=_skip_pad)


def _mm_wgrad(name, x, dy, out_dtype=BF16, padded=False):
    rows = x.shape[0]
    M = IN_W if padded else x.shape[1]
    N = dy.shape[1]
    tm = M if M <= 1024 else (512 if padded else 1408)
    return _mm_call(name, x, dy, "TN", out_dtype, tm, N, TR, M, N, rows, a_col=_skip_pad if padded else None)


def _sigmoid(x):
    return 1.0 / (1.0 + jnp.exp(-x))


def _silu(x):
    return x * _sigmoid(x)


def _silu_grad(x):
    s = _sigmoid(x)
    return s * (1.0 + x * (1.0 - s))


_GELU_K = math.sqrt(2.0 / math.pi)


def _gelu(x):
    return 0.5 * x * (1.0 + jnp.tanh(_GELU_K * (x + 0.044715 * x * x * x)))


def _gelu_grad(x):
    t = jnp.tanh(_GELU_K * (x + 0.044715 * x * x * x))
    return 0.5 * (1.0 + t) + 0.5 * x * (1.0 - t * t) * _GELU_K * (1.0 + 3.0 * 0.044715 * x * x)


def _log_sigmoid(x):
    return jnp.minimum(x, 0.0) - jnp.log(1.0 + jnp.exp(-jnp.abs(x)))


def _neg_expm1(x):
    series = -x * (1.0 + x * (0.5 + x * (1.0 / 6.0 + x * (1.0 / 24.0))))
    return jnp.where(x > -0.03, series, 1.0 - jnp.exp(x))


def _rms(x):
    r = lax.rsqrt(jnp.mean(x * x, axis=-1, keepdims=True) + EPS)
    return x * r, r


def _rms_bwd(dy, y, r):
    return r * (dy - y * jnp.mean(dy * y, axis=-1, keepdims=True))


def _modrow(mod_ref, i, chunk):
    lo = mod_ref[0:1, chunk * D:(chunk + 1) * D]
    hi = mod_ref[1:2, chunk * D:(chunk + 1) * D]
    return jnp.where(i == 0, lo, hi)


def _acc_seg(acc_ref, i, val):
    zero = jnp.zeros_like(val)
    acc_ref[0:1, :] += jnp.where(i == 0, val, zero)
    acc_ref[1:2, :] += jnp.where(i == 0, zero, val)


def _colsum(x):
    return jnp.sum(x, axis=0, keepdims=True)


SH1, SC1, GA1, SH2, SC2, GA2 = range(6)


def _normmod_fwd(name, xa, g, mod, c_sh, c_sc):
    T = xa.shape[0]

    def body(i, ins, ps, outs, acc):
        y, _ = _rms(ins[0][...])
        h = (y * ps[0][...]) * (1.0 + _modrow(ps[1], i, c_sc)) + _modrow(ps[1], i, c_sh)
        outs[0][...] = h.astype(BF16)

    return _ew(name, body, T // TR, [(xa, *_rowblk(D))], [g, mod], [(_sds((T, D), BF16), *_rowblk(D))])[0]


def _resid_norm_fwd(name, xin, mat, gpost, mod, c_ga, gnext, modn, c_sh, c_sc):
    T = xin.shape[0]

    def body(i, ins, ps, outs, acc):
        ym, _ = _rms(ins[1][...])
        xo = ins[0][...] + _modrow(ps[1], i, c_ga) * (ym * ps[0][...])
        outs[0][...] = xo
        y, _ = _rms(xo)
        h = (y * ps[2][...]) * (1.0 + _modrow(ps[3], i, c_sc)) + _modrow(ps[3], i, c_sh)
        outs[1][...] = h.astype(BF16)

    return _ew(name, body, T // TR, [(xin, *_rowblk(D)), (mat, *_rowblk(D))], [gpost, mod, gnext, modn],
               [(_sds((T, D), F32), *_rowblk(D)), (_sds((T, D), BF16), *_rowblk(D))])


def _resid_loss_fwd(name, xin, mat, gpost, mod, c_ga, target):
    T = xin.shape[0]

    def body(i, ins, ps, outs, acc):
        ym, _ = _rms(ins[1][...])
        xo = ins[0][...] + _modrow(ps[1], i, c_ga) * (ym * ps[0][...])
        err = xo - ins[2][...]
        lat = i > 0
        outs[0][...] = jnp.where(lat, err * (1.0 / D), 0.0)
        acc[0][...] += jnp.where(lat, _colsum(err * err), 0.0)

    tgt_blk = ((TR, D), lambda i: (jnp.maximum(i - 1, 0), 0))
    dx, sq = _ew(name, body, T // TR, [(xin, *_rowblk(D)), (mat, *_rowblk(D)), (target, *tgt_blk)], [gpost, mod],
                 [(_sds((T, D), F32), *_rowblk(D))], [_sds((1, D), F32)])
    return dx, sq


def _resid_bwd_vals(i, dout, mat, gpost, mod_ref, c_ga, acc_ga, acc_g):
    ym, rm = _rms(mat)
    ga = _modrow(mod_ref, i, c_ga)
    _acc_seg(acc_ga, i, _colsum(dout * (ym * gpost)))
    dn = dout * ga
    acc_g[...] += _colsum(dn * ym)
    return _rms_bwd(dn * gpost, ym, rm)


def _normmod_bwd_vals(i, dh, xin, g, mod_ref, c_sh, c_sc, acc_sh, acc_sc, acc_g):
    y, r = _rms(xin)
    _acc_seg(acc_sc, i, _colsum(dh * (y * g)))
    _acc_seg(acc_sh, i, _colsum(dh))
    dyg = dh * (1.0 + _modrow(mod_ref, i, c_sc))
    acc_g[...] += _colsum(dyg * y)
    return _rms_bwd(dyg * g, y, r)


def _resid_bwd(name, dout, mat, gpost, mod, c_ga):
    T = dout.shape[0]

    def body(i, ins, ps, outs, acc):
        dm = _resid_bwd_vals(i, ins[0][...], ins[1][...], ps[0][...], ps[1], c_ga, acc[0], acc[1])
        outs[0][...] = dm.astype(BF16)

    return _ew(name, body, T // TR, [(dout, *_rowblk(D)), (mat, *_rowblk(D))], [gpost, mod],
               [(_sds((T, D), BF16), *_rowblk(D))], [_sds((2, D), F32), _sds((1, D), F32)])


def _normmod_resid_bwd(name, dh, xin, gpre, mod, c_sh, c_sc, dres, mat, gpost, c_ga):
    T = dh.shape[0]

    def body(i, ins, ps, outs, acc):
        dx = ins[2][...] + _normmod_bwd_vals(i, ins[0][...], ins[1][...], ps[0][...], ps[1], c_sh, c_sc,
                                             acc[0], acc[1], acc[2])
        outs[0][...] = dx
        dm = _resid_bwd_vals(i, dx, ins[3][...], ps[2][...], ps[1], c_ga, acc[3], acc[4])
        outs[1][...] = dm.astype(BF16)

    return _ew(name, body, T // TR, [(dh, *_rowblk(D)), (xin, *_rowblk(D)), (dres, *_rowblk(D)), (mat, *_rowblk(D))],
               [gpre, mod, gpost],
               [(_sds((T, D), F32), *_rowblk(D)), (_sds((T, D), BF16), *_rowblk(D))],
               [_sds((2, D), F32), _sds((2, D), F32), _sds((1, D), F32), _sds((2, D), F32), _sds((1, D), F32)])


def _normmod_bwd(name, dh, xin, gpre, mod, c_sh, c_sc, dres):
    T = dh.shape[0]

    def body(i, ins, ps, outs, acc):
        outs[0][...] = ins[2][...] + _normmod_bwd_vals(i, ins[0][...], ins[1][...], ps[0][...], ps[1], c_sh, c_sc,
                                                       acc[0], acc[1], acc[2])

    return _ew(name, body, T // TR, [(dh, *_rowblk(D)), (xin, *_rowblk(D)), (dres, *_rowblk(D))], [gpre, mod],
               [(_sds((T, D), F32), *_rowblk(D))], [_sds((2, D), F32), _sds((2, D), F32), _sds((1, D), F32)])


def _gate_fwd(name, p, ya, yb):
    T = ya.shape[0]

    def body(i, ins, ps, outs, acc):
        z = _sigmoid(ins[0][...]) * ins[2][...] + _sigmoid(ins[1][...]) * ins[3][...]
        outs[0][...] = z.astype(BF16)

    return _ew(name, body, T // TR,
               [(p, *_rowblk(D, COL_GL // D)), (p, *_rowblk(D, COL_GL // D + 1)), (ya, *_rowblk(D)), (yb, *_rowblk(D))],
               [], [(_sds((T, D), BF16), *_rowblk(D))])[0]


def _gate_bwd(name, p, ya, yb, dz):
    T = ya.shape[0]

    def body(i, ins, ps, outs, acc):
        ga = _sigmoid(ins[0][...])
        gb = _sigmoid(ins[1][...])
        dzv = ins[4][...]
        outs[0][...] = (dzv * ga).astype(BF16)
        outs[1][...] = (dzv * gb).astype(BF16)
        dga = dzv * ins[2][...] * ga * (1.0 - ga)
        dgb = dzv * ins[3][...] * gb * (1.0 - gb)
        outs[2][...] = jnp.concatenate([dga, dgb], axis=1).astype(BF16)

    return _ew(name, body, T // TR,
               [(p, *_rowblk(D, COL_GL // D)), (p, *_rowblk(D, COL_GL // D + 1)), (ya, *_rowblk(D)), (yb, *_rowblk(D)),
                (dz, *_rowblk(D))], [],
               [(_sds((T, D), BF16), *_rowblk(D)), (_sds((T, D), BF16), *_rowblk(D)),
                (_sds((T, P_W), BF16), *_rowblk(2 * D, COL_GL // (2 * D)))])


def _swiglu_fwd(name, f):
    T = f.shape[0]

    def body(i, ins, ps, outs, acc):
        outs[0][...] = (_silu(ins[0][...]) * ins[1][...]).astype(BF16)

    return _ew(name, body, T // TR, [(f, *_rowblk(D_FF, 0)), (f, *_rowblk(D_FF, 1))], [],
               [(_sds((T, D_FF), BF16), *_rowblk(D_FF))])[0]


def _swiglu_bwd(name, f, ds):
    T = f.shape[0]

    def body(i, ins, ps, outs, acc):
        gate, up, dsv = ins[0][...], ins[1][...], ins[2][...]
        dgate = dsv * up * _silu_grad(gate)
        dup = dsv * _silu(gate)
        outs[0][...] = jnp.concatenate([dgate, dup], axis=1).astype(BF16)

    return _ew(name, body, T // TR, [(f, *_rowblk(D_FF, 0)), (f, *_rowblk(D_FF, 1)), (ds, *_rowblk(D_FF))], [],
               [(_sds((T, 2 * D_FF), BF16), *_rowblk(2 * D_FF))])[0]


AB = 128
CTX_BLKS = CTX // AB


def _rope_tables(S):
    pos = jnp.arange(S, dtype=jnp.int32)
    inv = ROPE_BASE ** (-jnp.arange(N_FREQ, dtype=F32) / N_FREQ)
    ang_r = (pos // GRID_W).astype(F32)[:, None] * inv[None, :]
    ang_c = (pos % GRID_W).astype(F32)[:, None] * inv[None, :]
    cos = jnp.concatenate([jnp.cos(ang_r)] * 2 + [jnp.cos(ang_c)] * 2, axis=1)
    sin = jnp.concatenate([-jnp.sin(ang_r), jnp.sin(ang_r), -jnp.sin(ang_c), jnp.sin(ang_c)], axis=1)
    return cos, sin


def _rope(x, cos, sin):
    w = x.shape[1]
    reps = w // HEAD
    lane = lax.broadcasted_iota(jnp.int32, x.shape, 1)
    partner = jnp.where((lane & 63) < 32, pltpu.roll(x, w - 32, 1), pltpu.roll(x, 32, 1))
    return x * jnp.tile(cos, (1, reps)) + partner * jnp.tile(sin, (1, reps))


def _unrope(dx, cos, sin):
    w = dx.shape[1]
    reps = w // HEAD
    lane = lax.broadcasted_iota(jnp.int32, dx.shape, 1)
    t = dx * jnp.tile(sin, (1, reps))
    partner = jnp.where((lane & 63) < 32, pltpu.roll(t, w - 32, 1), pltpu.roll(t, 32, 1))
    return dx * jnp.tile(cos, (1, reps)) + partner


def _qkv_prep(name, p, cos, sin, S):
    T = CTX + S
    nb = S // AB
    KW = N_KV * HEAD

    def kern(q_ref, k_ref, v_ref, cos_ref, sin_ref, qa_ref, kp_ref, vp_ref, kc_ref, vc_ref):
        i = pl.program_id(0)
        cos_v, sin_v = cos_ref[...], sin_ref[...]
        q = q_ref[...]
        qa_ref[...] = jnp.where(i >= CTX_BLKS, _rope(q, cos_v, sin_v), q).astype(BF16)

        @pl.when(i < CTX_BLKS)
        def _():
            kc_ref[...] = k_ref[...].astype(BF16)
            vc_ref[...] = v_ref[...].astype(BF16)
            kp_ref[...] = jnp.zeros(kp_ref.shape, BF16)
            vp_ref[...] = jnp.zeros(vp_ref.shape, BF16)

        @pl.when(i >= CTX_BLKS)
        def _():
            kp_ref[...] = _rope(k_ref[...], cos_v, sin_v).astype(BF16)
            vp_ref[...] = v_ref[...].astype(BF16)

    lat_map = lambda i: (jnp.maximum(i - CTX_BLKS, 0), 0)
    pad_map = lambda i: (jnp.where(i == 0, 0, jnp.where(i == 1, nb + 1, i - 1)), 0)
    ctx_map = lambda i: (jnp.minimum(i, CTX_BLKS - 1), 0)
    return pl.pallas_call(
        kern, name=name, grid=(T // AB,),
        in_specs=[pl.BlockSpec((AB, N_Q * HEAD), lambda i: (i, COL_Q // (N_Q * HEAD))),
                  pl.BlockSpec((AB, KW), lambda i: (i, COL_K // KW)),
                  pl.BlockSpec((AB, KW), lambda i: (i, COL_V // KW)),
                  pl.BlockSpec((AB, HEAD), lat_map), pl.BlockSpec((AB, HEAD), lat_map)],
        out_specs=[pl.BlockSpec((AB, N_Q * HEAD), lambda i: (i, 0)),
                   pl.BlockSpec((AB, KW), pad_map), pl.BlockSpec((AB, KW), pad_map),
                   pl.BlockSpec((AB, KW), ctx_map), pl.BlockSpec((AB, KW), ctx_map)],
        out_shape=[_sds((T, N_Q * HEAD), BF16), _sds((S + 2 * AB, KW), BF16), _sds((S + 2 * AB, KW), BF16),
                   _sds((CTX, KW), BF16), _sds((CTX, KW), BF16)],
        compiler_params=_params(),
    )(p, p, p, cos, sin)


GQ = Q_PER_KV * AB
GW = Q_PER_KV * HEAD


def _stack_heads(blk):
    return jnp.concatenate([blk[:, g * HEAD:(g + 1) * HEAD] for g in range(Q_PER_KV)], axis=0)


def _unstack_heads(x4):
    return jnp.concatenate([x4[g * AB:(g + 1) * AB, :] for g in range(Q_PER_KV)], axis=1)


def _sink_col(sink_ref):
    return jnp.concatenate([jnp.broadcast_to(sink_ref[g:g + 1, 0:1], (AB, 1)) for g in range(Q_PER_KV)], axis=0)


def _band_mask(n, S):
    r = lax.broadcasted_iota(jnp.int32, (GQ, 3 * AB), 0)
    c = lax.broadcasted_iota(jnp.int32, (GQ, 3 * AB), 1)
    d = c - AB - (r & (AB - 1))
    kpos = n * AB - AB + c
    return (jnp.abs(d) <= AB) & (kpos >= 0) & (kpos < S)


def _attn_probs(q4, kc, sink, kb, mask):
    s_ctx = _dot(q4, kc, NT) * ATT_SCALE
    m = jnp.maximum(jnp.max(s_ctx, axis=-1, keepdims=True), sink)
    if kb is not None:
        s_b = jnp.where(mask, _dot(q4, kb, NT) * ATT_SCALE, NEG_INF)
        m = jnp.maximum(m, jnp.max(s_b, axis=-1, keepdims=True))
    p_ctx = jnp.exp(s_ctx - m)
    p_sink = jnp.exp(sink - m)
    l = jnp.sum(p_ctx, axis=-1, keepdims=True) + p_sink
    p_b = None
    if kb is not None:
        p_b = jnp.exp(s_b - m)
        l = l + jnp.sum(p_b, axis=-1, keepdims=True)
    inv = 1.0 / l
    return p_ctx * inv, (None if p_b is None else p_b * inv), p_sink * inv


def _attn_fwd(name, qa, kc, vc, sink4, S, band=None, prev=None):
    T = qa.shape[0]
    has_band = band is not None
    nq = S // AB if has_band else CTX_BLKS
    q_off = CTX_BLKS if has_band else 0

    def kern(*refs):
        q_ref, kc_ref, vc_ref, sink_ref = refs[:4]
        rest = refs[4:]
        if has_band:
            kp_ref, vp_ref = rest[:2]
            rest = rest[2:]
        o_ref = rest[-1]
        n = pl.program_id(1)
        q4 = _stack_heads(q_ref[...])
        sink = _sink_col(sink_ref)
        kb = vb = mask = None
        if has_band:
            start = pl.multiple_of(n * AB, AB)
            kb = kp_ref[pl.ds(start, 3 * AB), :]
            vb = vp_ref[pl.ds(start, 3 * AB), :]
            mask = _band_mask(n, S)
        p_ctx, p_b, _ = _attn_probs(q4, kc_ref[...], sink, kb, mask)
        o4 = _dot(p_ctx.astype(BF16), vc_ref[...])
        if has_band:
            o4 = o4 + _dot(p_b.astype(BF16), vb)
        o_ref[...] = _unstack_heads(o4).astype(BF16)

    in_specs = [pl.BlockSpec((AB, GW), lambda kh, n: (n + q_off, kh)),
                pl.BlockSpec((CTX, HEAD), lambda kh, n: (0, kh)), pl.BlockSpec((CTX, HEAD), lambda kh, n: (0, kh)),
                pl.BlockSpec((None, Q_PER_KV, HEAD), lambda kh, n: (kh, 0, 0))]
    args = [qa, kc, vc, sink4]
    if has_band:
        in_specs += [pl.BlockSpec((S + 2 * AB, HEAD), lambda kh, n: (0, kh))] * 2
        args += list(band)
    alias = {}
    if prev is not None:
        in_specs.append(ANY)
        alias = {len(args): 0}
        args.append(prev)
    return pl.pallas_call(
        kern, name=name, grid=(N_KV, nq), in_specs=in_specs,
        out_specs=pl.BlockSpec((AB, GW), lambda kh, n: (n + q_off, kh)),
        out_shape=_sds((T, N_Q * HEAD), BF16), input_output_aliases=alias,
        compiler_params=_params(("arbitrary", "arbitrary")),
    )(*args)


def _attn_bwd(name, qa, kc, vc, sink4, o_all, do_all, S, band=None, prev_dq=None):
    T = qa.shape[0]
    has_band = band is not None
    nq = S // AB if has_band else CTX_BLKS
    q_off = CTX_BLKS if has_band else 0
    KW = N_KV * HEAD

    def kern(*refs):
        q_ref, kc_ref, vc_ref, sink_ref, o_ref, do_ref = refs[:6]
        rest = refs[6:]
        if has_band:
            kp_ref, vp_ref = rest[:2]
            rest = rest[2:]
        if prev_dq is not None:
            rest = rest[1:]
        dq_ref, dkc_ref, dvc_ref, dsink_ref = rest[:4]
        n = pl.program_id(1)

        @pl.when(n == 0)
        def _():
            dkc_ref[...] = jnp.zeros(dkc_ref.shape, F32)
            dvc_ref[...] = jnp.zeros(dvc_ref.shape, F32)
            dsink_ref[...] = jnp.zeros(dsink_ref.shape, F32)
            if has_band:
                rest[4][...] = jnp.zeros(rest[4].shape, F32)
                rest[5][...] = jnp.zeros(rest[5].shape, F32)

        q4 = _stack_heads(q_ref[...])
        sink = _sink_col(sink_ref)
        kc_v, vc_v = kc_ref[...], vc_ref[...]
        kb = vb = mask = None
        if has_band:
            start = pl.multiple_of(n * AB, AB)
            kb = kp_ref[pl.ds(start, 3 * AB), :]
            vb = vp_ref[pl.ds(start, 3 * AB), :]
            mask = _band_mask(n, S)
        p_ctx, p_b, p_sink = _attn_probs(q4, kc_v, sink, kb, mask)
        do4 = _stack_heads(do_ref[...])
        o4 = _stack_heads(o_ref[...]).astype(F32)
        delta = jnp.sum(do4 * o4, axis=-1, keepdims=True)
        do4b = do4.astype(BF16)
        ds_ctx = (p_ctx * (_dot(do4b, vc_v, NT) - delta)).astype(BF16)
        dq4 = _dot(ds_ctx, kc_v)
        dkc_ref[...] += _dot(ds_ctx, q4, TN) * ATT_SCALE
        dvc_ref[...] += _dot(p_ctx.astype(BF16), do4b, TN)
        if has_band:
            ds_b = (p_b * (_dot(do4b, vb, NT) - delta)).astype(BF16)
            dq4 = dq4 + _dot(ds_b, kb)
            rest[4][pl.ds(start, 3 * AB), :] += _dot(ds_b, q4, TN) * ATT_SCALE
            rest[5][pl.ds(start, 3 * AB), :] += _dot(p_b.astype(BF16), do4b, TN)
        dq_ref[...] = _unstack_heads(dq4 * ATT_SCALE)
        ps = p_sink * delta
        dsink_ref[...] += jnp.concatenate(
            [jnp.broadcast_to(-jnp.sum(ps[g * AB:(g + 1) * AB, :], axis=0, keepdims=True), (1, HEAD))
             for g in range(Q_PER_KV)], axis=0)

    q_spec = pl.BlockSpec((AB, GW), lambda kh, n: (n + q_off, kh))
    c_spec = pl.BlockSpec((CTX, HEAD), lambda kh, n: (0, kh))
    s_spec = pl.BlockSpec((None, Q_PER_KV, HEAD), lambda kh, n: (kh, 0, 0))
    in_specs = [q_spec, c_spec, c_spec, s_spec, q_spec, q_spec]
    args = [qa, kc, vc, sink4, o_all, do_all]
    out_specs = [q_spec, c_spec, c_spec, s_spec]
    out_shape = [_sds((T, N_Q * HEAD), F32), _sds((CTX, KW), F32), _sds((CTX, KW), F32), _sds((N_KV, Q_PER_KV, HEAD), F32)]
    if has_band:
        p_spec = pl.BlockSpec((S + 2 * AB, HEAD), lambda kh, n: (0, kh))
        in_specs += [p_spec, p_spec]
        args += list(band)
        out_specs += [p_spec, p_spec]
        out_shape += [_sds((S + 2 * AB, KW), F32)] * 2
    alias = {}
    if prev_dq is not None:
        in_specs.append(ANY)
        alias = {len(args): 0}
        args.append(prev_dq)
    return pl.pallas_call(
        kern, name=name, grid=(N_KV, nq), in_specs=in_specs, out_specs=out_specs, out_shape=out_shape,
        input_output_aliases=alias, compiler_params=_params(("arbitrary", "arbitrary")),
    )(*args)


def _dqkv_assemble(name, dp, dq_all, dkp, dvp, dkc_l, dvc_l, dkc_c, dvc_c, cos, sin, S):
    T = CTX + S
    KW = N_KV * HEAD
    HALF = N_Q * HEAD // 2

    def kern(dq_ref, dkp_ref, dvp_ref, dkcl_ref, dvcl_ref, dkcc_ref, dvcc_ref, cos_ref, sin_ref, dp_in, out_ref):
        i = pl.program_id(0)
        j = pl.program_id(1)
        lat = i >= CTX_BLKS
        cos_v, sin_v = cos_ref[...], sin_ref[...]

        @pl.when(j < 2)
        def _():
            dq = dq_ref[...]
            out_ref[...] = jnp.where(lat, _unrope(dq, cos_v, sin_v), dq).astype(BF16)

        @pl.when(j == 2)
        def _():
            dk = jnp.where(lat, _unrope(dkp_ref[...], cos_v, sin_v), dkcl_ref[...] + dkcc_ref[...])
            dv = jnp.where(lat, dvp_ref[...], dvcl_ref[...] + dvcc_ref[...])
            out_ref[...] = jnp.concatenate([dk, dv], axis=1).astype(BF16)

    lat_map = lambda i, j: (jnp.maximum(i - CTX_BLKS, 0), 0)
    pad_map = lambda i, j: (jnp.maximum(i - 1, 0), 0)
    ctx_map = lambda i, j: (jnp.minimum(i, CTX_BLKS - 1), 0)
    return pl.pallas_call(
        kern, name=name, grid=(T // AB, 3),
        in_specs=[pl.BlockSpec((AB, HALF), lambda i, j: (i, jnp.minimum(j, 1))),
                  pl.BlockSpec((AB, KW), pad_map), pl.BlockSpec((AB, KW), pad_map),
                  pl.BlockSpec((AB, KW), ctx_map), pl.BlockSpec((AB, KW), ctx_map),
                  pl.BlockSpec((AB, KW), ctx_map), pl.BlockSpec((AB, KW), ctx_map),
                  pl.BlockSpec((AB, HEAD), lat_map), pl.BlockSpec((AB, HEAD), lat_map), ANY],
        out_specs=pl.BlockSpec((AB, HALF), lambda i, j: (i, COL_Q // HALF + j)),
        out_shape=_sds((T, P_W), BF16), input_output_aliases={9: 0},
        compiler_params=_params(("arbitrary", "arbitrary")),
    )(dq_all, dkp, dvp, dkc_l, dvc_l, dkc_c, dvc_c, cos, sin, dp)


RB = 128
CH = 256
HALO = 8
SUB = 8
GRP = 8


def _vscan(a, b, reverse):
    row = lax.broadcasted_iota(jnp.int32, a.shape, 0)
    A, H = a, b
    for s in (1, 2, 4):
        sh = SUB - s if reverse else s
        m = (row < SUB - s) if reverse else (row >= s)
        As = pltpu.roll(A, sh, 0)
        Hs = pltpu.roll(H, sh, 0)
        H = jnp.where(m, A * Hs + H, H)
        A = jnp.where(m, A * As, A)
    return A, H


def _scan_rows(a_ref, b_ref, r0, nrows, reverse, carry, emit):
    ngrp = nrows // (SUB * GRP)
    row = lax.broadcasted_iota(jnp.int32, (SUB, RB), 0)

    def grp(gi, carry):
        g = (ngrp - 1 - gi) if reverse else gi
        base = r0 + g * (SUB * GRP)
        for v in (range(GRP - 1, -1, -1) if reverse else range(GRP)):
            rs = pl.multiple_of(base + v * SUB, SUB)
            A, H = _vscan(a_ref[pl.ds(rs, SUB), :], b_ref[pl.ds(rs, SUB), :], reverse)
            hf = H + A * carry
            if reverse:
                before = jnp.where(row == SUB - 1, carry, pltpu.roll(hf, SUB - 1, 0))
                carry = hf[0:1, :]
            else:
                before = jnp.where(row == 0, carry, pltpu.roll(hf, 1, 0))
                carry = hf[SUB - 1:SUB, :]
            emit(rs, hf, before)
        return carry

    return lax.fori_loop(0, ngrp, grp, carry)


def _conv_taps(ext, t, T, transpose=False):
    lo = jnp.where(t < CTX, 0, CTX)
    hi = jnp.where(t < CTX, CTX, T)
    n = CH + 2 * HALO
    taps = []
    for k in range(CONV_W):
        off = k - CONV_LEFT
        if transpose:
            off = -off
        sh = pltpu.roll(ext, (-off) % n, 0)[HALO:HALO + CH, :]
        valid = (t + off >= lo) & (t + off < hi)
        taps.append(jnp.where(valid, sh, 0.0))
    return taps


def _lru_gates(xl, w4, b4, ls):
    pre = _dot(xl.astype(BF16), w4) + b4
    out = []
    for d in range(2):
        r = _sigmoid(pre[:, d * RB:(d + 1) * RB])
        i = _sigmoid(pre[:, (2 + d) * RB:(3 + d) * RB])
        la = LRU_C * r * ls[d:d + 1, :]
        a = jnp.exp(la)
        mult = jnp.sqrt(_neg_expm1(2.0 * la))
        out.append((r, i, a, mult))
    return out


def _rnn_specs(T):
    col = lambda n, *_: (0, n)
    return dict(
        xr=pl.BlockSpec((T, RB), lambda n, *_: (0, COL_XR // RB + n)),
        gr=pl.BlockSpec((T, RB), lambda n, *_: (0, COL_GR // RB + n)),
        act=pl.BlockSpec((T, RB), col),
        cw=pl.BlockSpec((CONV_W, RB), col), cb=pl.BlockSpec((1, RB), col),
        w4=pl.BlockSpec((None, RB, 4 * RB), lambda n, *_: (n, 0, 0)),
        b4=pl.BlockSpec((None, 1, 4 * RB), lambda n, *_: (n, 0, 0)),
        lam=pl.BlockSpec((2, RB), col))


def _fill_padded(pad_ref, src_ref, T):
    pad_ref[0:HALO, :] = jnp.zeros((HALO, RB), F32)
    pad_ref[HALO + T:2 * HALO + T, :] = jnp.zeros((HALO, RB), F32)
    pad_ref[HALO:HALO + T, :] = src_ref[...]


def _rnn_fwd(name, p, cw, cb, w4, b4, lam, T):
    def kern(xr_ref, gr_ref, cw_ref, cb_ref, w4_ref, b4_ref, lam_ref, u_ref, hpf_ref, hpb_ref,
             xpad, a0, b0, a1, b1, y):
        _fill_padded(xpad, xr_ref, T)
        ls = _log_sigmoid(lam_ref[...])
        w4v, b4v, cwv, cbv = w4_ref[...], b4_ref[...], cw_ref[...], cb_ref[...]

        def chunk(ci, _):
            base = pl.multiple_of(ci * CH, CH)
            t = base + lax.broadcasted_iota(jnp.int32, (CH, 1), 0)
            taps = _conv_taps(xpad[pl.ds(base, CH + 2 * HALO), :], t, T)
            xl = cbv + sum(taps[k] * cwv[k:k + 1, :] for k in range(CONV_W))
            for d, (r, i, a, mult) in enumerate(_lru_gates(xl, w4v, b4v, ls)):
                (a0, a1)[d][pl.ds(base, CH), :] = a
                (b0, b1)[d][pl.ds(base, CH), :] = mult * (i * xl)
            return 0

        lax.fori_loop(0, T // CH, chunk, 0)
        zero = jnp.zeros((1, RB), F32)

        def emit_f(rs, hf, before):
            y[pl.ds(rs, SUB), :] = hf
            hpf_ref[pl.ds(rs, SUB), :] = before

        def emit_b(rs, hf, before):
            y[pl.ds(rs, SUB), :] += hf
            hpb_ref[pl.ds(rs, SUB), :] = before

        _scan_rows(a0, b0, 0, T, False, zero, emit_f)
        c = _scan_rows(a1, b1, 0, CTX, True, zero, emit_b)
        _scan_rows(a1, b1, CTX, T - CTX, True, c, emit_b)

        def finish(ci, _):
            base = pl.multiple_of(ci * CH, CH)
            u_ref[pl.ds(base, CH), :] = (y[pl.ds(base, CH), :] * _gelu(gr_ref[pl.ds(base, CH), :])).astype(BF16)
            return 0

        lax.fori_loop(0, T // CH, finish, 0)

    sp = _rnn_specs(T)
    return pl.pallas_call(
        kern, name=name, grid=(N_RNN_BLOCKS,),
        in_specs=[sp["xr"], sp["gr"], sp["cw"], sp["cb"], sp["w4"], sp["b4"], sp["lam"]],
        out_specs=[sp["act"]] * 3,
        out_shape=[_sds((T, D), BF16), _sds((T, D), F32), _sds((T, D), F32)],
        scratch_shapes=[pltpu.VMEM((T + 2 * HALO, RB), F32)] + [pltpu.VMEM((T, RB), F32)] * 5,
        compiler_params=_params(),
    )(p, p, cw, cb, w4, b4, lam)


def _rnn_bwd(name, p, du, hpf, hpb, dp, cw, cb, w4, b4, lam, T):
    def kern(xr_ref, gr_ref, du_ref, hpf_ref, hpb_ref, cw_ref, cb_ref, w4_ref, b4_ref, lam_ref, dp_in,
             dp_ref, dcw_ref, dcb_ref, dw4_ref, db4_ref, dlam_ref,
             xpad, dxpad, a0, a1, c0, c1, dy, dgr_ref):
        j = pl.program_id(1)

        @pl.when(j == 0)
        def _():
            work(xr_ref, gr_ref, du_ref, hpf_ref, hpb_ref, cw_ref, cb_ref, w4_ref, b4_ref, lam_ref,
                 dp_ref, dgr_ref, dcw_ref, dcb_ref, dw4_ref, db4_ref, dlam_ref, xpad, dxpad, a0, a1, c0, c1, dy)

        @pl.when(j == 1)
        def _():
            dp_ref[...] = dgr_ref[...]

    def work(xr_ref, gr_ref, du_ref, hpf_ref, hpb_ref, cw_ref, cb_ref, w4_ref, b4_ref, lam_ref,
             dxr_ref, dgr_ref, dcw_ref, dcb_ref, dw4_ref, db4_ref, dlam_ref, xpad, dxpad, a0, a1, c0, c1, dy):
        _fill_padded(xpad, xr_ref, T)
        dxpad[0:HALO, :] = jnp.zeros((HALO, RB), F32)
        dxpad[HALO + T:2 * HALO + T, :] = jnp.zeros((HALO, RB), F32)
        lam_v = lam_ref[...]
        ls = _log_sigmoid(lam_v)
        w4v, b4v, cwv, cbv = w4_ref[...], b4_ref[...], cw_ref[...], cb_ref[...]

        def conv_chunk(base):
            t = base + lax.broadcasted_iota(jnp.int32, (CH, 1), 0)
            taps = _conv_taps(xpad[pl.ds(base, CH + 2 * HALO), :], t, T)
            return t, taps, cbv + sum(taps[k] * cwv[k:k + 1, :] for k in range(CONV_W))

        def phase_a(ci, _):
            base = pl.multiple_of(ci * CH, CH)
            rows = pl.ds(base, CH)
            _, _, xl = conv_chunk(base)
            (r0, i0, av0, m0), (r1, i1, av1, m1) = _lru_gates(xl, w4v, b4v, ls)
            yv = (av0 * hpf_ref[rows, :] + m0 * (i0 * xl)) + (av1 * hpb_ref[rows, :] + m1 * (i1 * xl))
            gr = gr_ref[rows, :]
            duv = du_ref[rows, :]
            dyv = duv * _gelu(gr)
            dgr_ref[rows, :] = (duv * yv * _gelu_grad(gr)).astype(BF16)
            dy[rows, :] = dyv
            a0[rows, :] = av0
            a1[rows, :] = av1
            c0[rows, :] = av0 * dyv
            c1[rows, :] = av1 * dyv
            return 0

        lax.fori_loop(0, T // CH, phase_a, 0)
        zero = jnp.zeros((1, RB), F32)

        def emit0(rs, hf, before):
            c0[pl.ds(rs, SUB), :] = dy[pl.ds(rs, SUB), :] + before

        def emit1(rs, hf, before):
            c1[pl.ds(rs, SUB), :] = dy[pl.ds(rs, SUB), :] + before

        _scan_rows(a0, c0, 0, T, True, zero, emit0)
        c = _scan_rows(a1, c1, CTX, T - CTX, False, zero, emit1)
        _scan_rows(a1, c1, 0, CTX, False, c, emit1)

        dw4_ref[...] = jnp.zeros(dw4_ref.shape, F32)
        db4_ref[...] = jnp.zeros(db4_ref.shape, F32)
        dlam_ref[...] = jnp.zeros(dlam_ref.shape, F32)
        dcw_ref[...] = jnp.zeros(dcw_ref.shape, F32)
        dcb_ref[...] = jnp.zeros(dcb_ref.shape, F32)

        def phase_c(ci, _):
            base = pl.multiple_of(ci * CH, CH)
            rows = pl.ds(base, CH)
            _, _, xl = conv_chunk(base)
            gates = _lru_gates(xl, w4v, b4v, ls)
            dxl = jnp.zeros((CH, RB), F32)
            dpre_a, dpre_x, dls = [], [], []
            for d, (r, i, a, mult) in enumerate(gates):
                g = (c0, c1)[d][rows, :]
                hp = (hpf_ref, hpb_ref)[d][rows, :]
                dmult = g * (i * xl)
                di = g * mult * xl
                dxl = dxl + g * mult * i
                dla = g * hp * a - (a * a) * dmult / mult
                dr = dla * (LRU_C * ls[d:d + 1, :])
                dls.append(_colsum(dla * (LRU_C * r)))
                dpre_a.append(dr * r * (1.0 - r))
                dpre_x.append(di * i * (1.0 - i))
            dpre = jnp.concatenate(dpre_a + dpre_x, axis=1)
            dpre_b = dpre.astype(BF16)
            dxl = dxl + _dot(dpre_b, w4v, NT)
            dw4_ref[...] += _dot(xl.astype(BF16), dpre_b, TN)
            db4_ref[...] += _colsum(dpre)
            dlam_ref[...] += jnp.concatenate(dls, axis=0)
            dcb_ref[...] += _colsum(dxl)
            dxpad[pl.ds(pl.multiple_of(base + HALO, HALO), CH), :] = dxl
            return 0

        lax.fori_loop(0, T // CH, phase_c, 0)
        dlam_ref[...] = dlam_ref[...] * _sigmoid(-lam_v)

        def phase_d(ci, _):
            base = pl.multiple_of(ci * CH, CH)
            rows = pl.ds(base, CH)
            t, xtaps, _ = conv_chunk(base)
            dtaps = _conv_taps(dxpad[pl.ds(base, CH + 2 * HALO), :], t, T, transpose=True)
            dxl = dxpad[pl.ds(pl.multiple_of(base + HALO, HALO), CH), :]
            dxr_ref[rows, :] = sum(dtaps[k] * cwv[k:k + 1, :] for k in range(CONV_W)).astype(BF16)
            dcw_ref[...] += jnp.concatenate([_colsum(dxl * xtaps[k]) for k in range(CONV_W)], axis=0)
            return 0

        lax.fori_loop(0, T // CH, phase_d, 0)

    sp = _rnn_specs(T)
    dp_spec = pl.BlockSpec((T, RB), lambda n, j: (0, COL_XR // RB + n + j * (COL_GR - COL_XR) // RB))
    return pl.pallas_call(
        kern, name=name, grid=(N_RNN_BLOCKS, 2),
        in_specs=[sp["xr"], sp["gr"], sp["act"], sp["act"], sp["act"], sp["cw"], sp["cb"], sp["w4"], sp["b4"],
                  sp["lam"], ANY],
        out_specs=[dp_spec, sp["cw"], sp["cb"], sp["w4"], sp["b4"], sp["lam"]],
        out_shape=[_sds((T, P_W), BF16), _sds((CONV_W, D), F32), _sds((1, D), F32),
                   _sds((N_RNN_BLOCKS, RB, 4 * RB), F32), _sds((N_RNN_BLOCKS, 1, 4 * RB), F32), _sds((2, D), F32)],
        scratch_shapes=([pltpu.VMEM((T + 2 * HALO, RB), F32)] * 2 + [pltpu.VMEM((T, RB), F32)] * 5
                        + [pltpu.VMEM((T, RB), BF16)]),
        input_output_aliases={10: 0},
        compiler_params=_params(("arbitrary", "arbitrary")),
    )(p, p, du, hpf, hpb, cw, cb, w4, b4, lam, dp)


def _layer_fwd(l, xa, h, W, rope, S, nxt):
    T = xa.shape[0]
    tag = f"l{l}_"
    cos, sin = rope
    p = _mm_proj(tag + "proj", h, W["win_t"])
    u, hpf, hpb = _rnn_fwd(tag + "rnn_fwd", p, W["cw"], W["cb"], W["w4"], W["b4"], W["lam"], T)
    qa, kp, vp, kc, vc = _qkv_prep(tag + "qkv_prep", p, cos, sin, S)
    o_all = _attn_fwd(tag + "attn_ctx_fwd", qa, kc, vc, W["sink4"], S)
    o_all = _attn_fwd(tag + "attn_lat_fwd", qa, kc, vc, W["sink4"], S, band=(kp, vp), prev=o_all)
    ya = _mm_act(tag + "o_rnn", u, W["wo_rnn"], "NN")
    yb = _mm_act(tag + "o_attn", o_all, W["wo_attn"], "NN")
    z = _gate_fwd(tag + "gate_fwd", p, ya, yb)
    m = _mm_act(tag + "out", z, W["wout"], "NN")
    x1, h2 = _resid_norm_fwd(tag + "mix_resid", xa, m, W["g_mix_post"], W["mod"], GA1, W["g_ffn_pre"], W["mod"], SH2, SC2)
    f = _mm_act(tag + "ffn_in", h2, W["wffn_in_t"], "NT")
    s = _swiglu_fwd(tag + "swiglu_fwd", f)
    e = _mm_act(tag + "ffn_out", s, W["wffn_out"], "NN")
    saved = dict(xa=xa, h=h, p=p, u=u, hpf=hpf, hpb=hpb, qa=qa, kp=kp, vp=vp, kc=kc, vc=vc, o_all=o_all,
                 ya=ya, yb=yb, z=z, m=m, x1=x1, h2=h2, f=f, s=s, e=e)
    if nxt[0] == "norm":
        out = _resid_norm_fwd(tag + "ffn_resid", x1, e, W["g_ffn_post"], W["mod"], GA2, nxt[1], nxt[2], SH1, SC1)
    else:
        out = _resid_loss_fwd(tag + "ffn_resid_loss", x1, e, W["g_ffn_post"], W["mod"], GA2, nxt[1])
    return saved, out


def _layer_bwd(l, dx2, A, W, rope, S):
    T = dx2.shape[0]
    tag = f"l{l}_"
    cos, sin = rope
    G = {}
    de, dga2, G["g_ffn_post"] = _resid_bwd(tag + "ffn_resid_bwd", dx2, A["e"], W["g_ffn_post"], W["mod"], GA2)
    ds = _mm_act(tag + "ffn_out_dx", de, W["wffn_out"], "NT")
    G["wffn_out"] = _mm_wgrad(tag + "ffn_out_dw", A["s"], de)
    df = _swiglu_bwd(tag + "swiglu_bwd", A["f"], ds)
    dh2 = _mm_act(tag + "ffn_in_dx", df, W["wffn_in_t"], "NN")
    G["wffn_in_t"] = _mm_wgrad(tag + "ffn_in_dw", df, A["h2"])
    dx1, dm, dsh2, dsc2, G["g_ffn_pre"], dga1, G["g_mix_post"] = _normmod_resid_bwd(
        tag + "mix_resid_bwd", dh2, A["x1"], W["g_ffn_pre"], W["mod"], SH2, SC2, dx2, A["m"], W["g_mix_post"], GA1)
    dz = _mm_act(tag + "out_dx", dm, W["wout"], "NT")
    G["wout"] = _mm_wgrad(tag + "out_dw", A["z"], dm)
    dya, dyb, dp = _gate_bwd(tag + "gate_bwd", A["p"], A["ya"], A["yb"], dz)
    do = _mm_act(tag + "o_attn_dx", dyb, W["wo_attn"], "NT")
    G["wo_attn"] = _mm_wgrad(tag + "o_attn_dw", A["o_all"], dyb)
    du = _mm_act(tag + "o_rnn_dx", dya, W["wo_rnn"], "NT")
    G["wo_rnn"] = _mm_wgrad(tag + "o_rnn_dw", A["u"], dya)
    dq_all, dkc_c, dvc_c, dsink_c = _attn_bwd(tag + "attn_ctx_bwd", A["qa"], A["kc"], A["vc"], W["sink4"],
                                               A["o_all"], do, S)
    dq_all, dkc_l, dvc_l, dsink_l, dkp, dvp = _attn_bwd(tag + "attn_lat_bwd", A["qa"], A["kc"], A["vc"], W["sink4"],
                                                       A["o_all"], do, S, band=(A["kp"], A["vp"]), prev_dq=dq_all)
    G["sink4"] = dsink_c + dsink_l
    dp = _dqkv_assemble(tag + "dqkv", dp, dq_all, dkp, dvp, dkc_l, dvc_l, dkc_c, dvc_c, cos, sin, S)
    dp, G["cw"], G["cb"], G["w4"], G["b4"], G["lam"] = _rnn_bwd(
        tag + "rnn_bwd", A["p"], du, A["hpf"], A["hpb"], dp, W["cw"], W["cb"], W["w4"], W["b4"], W["lam"], T)
    dh = _mm_dproj(tag + "proj_dx", dp, W["win_t"])
    G["win_t"] = _mm_wgrad(tag + "proj_dw", dp, A["h"], padded=True)
    dxa, dsh1, dsc1, G["g_mix_pre"] = _normmod_bwd(tag + "mix_norm_bwd", dh, A["xa"], W["g_mix_pre"], W["mod"],
                                                   SH1, SC1, dx1)
    G["mod"] = jnp.concatenate([dsh1, dsc1, dga1, dsh2, dsc2, dga2], axis=1)
    return dxa, G


def _local_step(xa, target, Ws, S):
    rope = _rope_tables(S)
    L = len(Ws)
    h = _normmod_fwd("l0_mix_norm", xa, Ws[0]["g_mix_pre"], Ws[0]["mod"], SH1, SC1)
    saved = []
    x = xa
    for l in range(L):
        nxt = ("norm", Ws[l + 1]["g_mix_pre"], Ws[l + 1]["mod"]) if l + 1 < L else ("loss", target)
        A, out = _layer_fwd(l, x, h, Ws[l], rope, S, nxt)
        saved.append(A)
        if l + 1 < L:
            x, h = out
    dx, sq = out
    Gs = [None] * L
    for l in reversed(range(L)):
        dx, Gs[l] = _layer_bwd(l, dx, saved[l], Ws[l], rope, S)
    return sq, dx, Gs


MESH = pl.DeviceIdType.MESH


def _place():
    return lax.axis_index("x"), lax.axis_index("y"), lax.axis_index("c")


def _lin(px, py, pc):
    return 4 * px + 2 * py + pc


def _allgather_small(name, blk):
    m, n = blk.shape

    def body(x_ref, out_ref, send_sems, recv_sems, local_sem):
        x, y, c = _place()
        me, sibling = (x, y, c), (x, y, 1 - c)
        chips = [(1 - x, y), (x, 1 - y), (1 - x, 1 - y)]

        def copy(k, block, to, src=None):
            dst = out_ref.at[_lin(*block)]
            return pltpu.make_async_remote_copy(src_ref=dst if src is None else src, dst_ref=dst,
                                                send_sem=send_sems.at[k], recv_sem=recv_sems.at[k],
                                                device_id=to, device_id_type=MESH)

        mine = pltpu.make_async_copy(x_ref, out_ref.at[_lin(*me)], local_sem)
        mine.start()
        first = [copy(0, me, sibling, src=x_ref)]
        first += [copy(1 + j, me, (*chip, c), src=x_ref) for j, chip in enumerate(chips)]
        for cp in first:
            cp.start()
        passed = [copy(4 + j, (*chip, c), sibling) for j, chip in enumerate(chips)]
        for j, chip in enumerate(chips):
            copy(1 + j, (*chip, c), me).wait_recv()
            passed[j].start()
        copy(0, sibling, me).wait_recv()
        for j, chip in enumerate(chips):
            copy(4 + j, (*chip, 1 - c), me).wait_recv()
        for cp in first + passed:
            cp.wait_send()
        mine.wait()

    return pl.pallas_call(
        body, name=name, out_shape=_sds((N_DEV, m, n), blk.dtype),
        in_specs=[pl.BlockSpec(memory_space=pltpu.VMEM)], out_specs=pl.BlockSpec(memory_space=pltpu.VMEM),
        scratch_shapes=[pltpu.SemaphoreType.DMA((7,)), pltpu.SemaphoreType.DMA((7,)), pltpu.SemaphoreType.DMA],
        compiler_params=pltpu.CompilerParams(vmem_limit_bytes=VMEM_LIMIT),
    )(blk)


def _allgather_hbm(name, shards):
    na = len(shards)

    def body(*refs):
        ins, outs = refs[:na], refs[na:2 * na]
        send_sems, recv_sems, local_sems = refs[2 * na:]
        x, y, c = _place()
        me, sibling = (x, y, c), (x, y, 1 - c)
        chips = [(1 - x, y), (x, 1 - y), (1 - x, 1 - y)]

        def copy(a, k, block, to, from_input=False):
            dst = outs[a].at[_lin(*block)]
            return pltpu.make_async_remote_copy(src_ref=ins[a] if from_input else dst, dst_ref=dst,
                                                send_sem=send_sems.at[a, k], recv_sem=recv_sems.at[a, k],
                                                device_id=to, device_id_type=MESH)

        mine = [pltpu.make_async_copy(ins[a], outs[a].at[_lin(*me)], local_sems.at[a]) for a in range(na)]
        for cp in mine:
            cp.start()
        first = []
        for a in range(na):
            first.append(copy(a, 0, me, sibling, True))
            first += [copy(a, 1 + j, me, (*chip, c), True) for j, chip in enumerate(chips)]
        for cp in first:
            cp.start()
        passed = []
        for j, chip in enumerate(chips):
            for a in range(na):
                copy(a, 1 + j, (*chip, c), me).wait_recv()
                fwd = copy(a, 4 + j, (*chip, c), sibling)
                fwd.start()
                passed.append(fwd)
        for a in range(na):
            copy(a, 0, sibling, me).wait_recv()
            for j, chip in enumerate(chips):
                copy(a, 4 + j, (*chip, 1 - c), me).wait_recv()
        for cp in first + passed:
            cp.wait_send()
        for cp in mine:
            cp.wait()

    return pl.pallas_call(
        body, name=name, out_shape=[_sds((N_DEV, *s.shape), s.dtype) for s in shards],
        in_specs=[ANY] * na, out_specs=[ANY] * na,
        scratch_shapes=[pltpu.SemaphoreType.DMA((na, 7)), pltpu.SemaphoreType.DMA((na, 7)),
                        pltpu.SemaphoreType.DMA((na,))],
    )(*shards)


def _exchange_shards(name, grads, L):
    nw = len(grads)
    na = nw * L
    flat = [g for per_layer in grads for g in per_layer]

    def body(*refs):
        ins, outs = refs[:na], refs[na:na + nw]
        send_sems, recv_sems, local_sems = refs[na + nw:]
        x, y, c = _place()
        me = _lin(x, y, c)
        peers = [(x ^ ((k + 1) >> 2 & 1), y ^ ((k + 1) >> 1 & 1), c ^ ((k + 1) & 1)) for k in range(7)]

        def copy(a, k, src_blk, dst_blk):
            return pltpu.make_async_remote_copy(src_ref=ins[a].at[src_blk], dst_ref=outs[a // L].at[a % L, dst_blk],
                                                send_sem=send_sems.at[a, k], recv_sem=recv_sems.at[a, k],
                                                device_id=peers[k], device_id_type=MESH)

        mine = [pltpu.make_async_copy(ins[a].at[me], outs[a // L].at[a % L, me], local_sems.at[a]) for a in range(na)]
        for cp in mine:
            cp.start()
        sent = [copy(a, k, _lin(*peers[k]), me) for a in range(na) for k in range(7)]
        for cp in sent:
            cp.start()
        for a in range(na):
            for k in range(7):
                copy(a, k, me, _lin(*peers[k])).wait_recv()
        for cp in sent:
            cp.wait_send()
        for cp in mine:
            cp.wait()

    return pl.pallas_call(
        body, name=name, out_shape=[_sds((L, *per_layer[0].shape), per_layer[0].dtype) for per_layer in grads],
        in_specs=[ANY] * na, out_specs=[ANY] * nw,
        scratch_shapes=[pltpu.SemaphoreType.DMA((na, 7)), pltpu.SemaphoreType.DMA((na, 7)),
                        pltpu.SemaphoreType.DMA((na,))],
    )(*flat)


MOD_ROWS = 16
MOD_SHARD = 6 * D // N_DEV
HI = lax.Precision.HIGHEST


def _mod_fwd(name, c9, w_mod, b_shard):
    L = w_mod.shape[0]

    def kern(c_ref, w_ref, b_ref, o_ref):
        o_ref[...] = lax.dot_general(_silu(c_ref[...]), w_ref[...], NN, precision=HI,
                                     preferred_element_type=F32) + b_ref[...]

    return pl.pallas_call(
        kern, name=name, grid=(L,),
        in_specs=[_full_spec(c9.shape), pl.BlockSpec((None, D, MOD_SHARD), lambda l: (l, 0, 0)),
                  pl.BlockSpec((None, 1, MOD_SHARD), lambda l: (l, 0, 0))],
        out_specs=pl.BlockSpec((None, MOD_ROWS, MOD_SHARD), lambda l: (l, 0, 0)),
        out_shape=_sds((L, MOD_ROWS, MOD_SHARD), F32), compiler_params=_params(),
    )(c9, w_mod, b_shard)


def _mod_bwd(name, c9, w_mod, dmod_all, dmod_cols):
    L = w_mod.shape[0]

    def rows9(ref, l):
        own = jnp.concatenate([ref[j, 2 * l + 1:2 * l + 2, :] for j in range(N_DEV)], axis=0)
        ctx = ref[0, 2 * l:2 * l + 1, :]
        for j in range(1, N_DEV):
            ctx = ctx + ref[j, 2 * l:2 * l + 1, :]
        return own, ctx

    def kern(c_ref, w_ref, all_ref, cols_ref, gw_ref, gb_ref, gc_ref):
        l = pl.program_id(0)
        for ll in range(L):
            @pl.when(l == ll)
            def _():
                own, ctx = rows9(all_ref, ll)
                gb_ref[...] = _colsum(own) + ctx
                own_s, ctx_s = rows9(cols_ref, ll)
                r16 = jnp.concatenate([own_s, ctx_s, jnp.zeros((MOD_ROWS - N_DEV - 1, MOD_SHARD), F32)], axis=0)
                gw_ref[...] = lax.dot_general(_silu(c_ref[...]), r16, TN, precision=HI, preferred_element_type=F32)
                part = lax.dot_general(r16, w_ref[...], NT, precision=HI,
                                       preferred_element_type=F32)[N_DEV:N_DEV + 1, :]
                if ll == 0:
                    gc_ref[...] = part
                else:
                    gc_ref[...] += part

    return pl.pallas_call(
        kern, name=name, grid=(L,),
        in_specs=[_full_spec(c9.shape), pl.BlockSpec((None, D, MOD_SHARD), lambda l: (l, 0, 0)),
                  _full_spec(dmod_all.shape), _full_spec(dmod_cols.shape)],
        out_specs=[pl.BlockSpec((None, D, MOD_SHARD), lambda l: (l, 0, 0)),
                   pl.BlockSpec((None, 1, 6 * D), lambda l: (l, 0, 0)), _full_spec((1, D))],
        out_shape=[_sds((L, D, MOD_SHARD), F32), _sds((L, 1, 6 * D), F32), _sds((1, D), F32)],
        compiler_params=_params(),
    )(c9, w_mod, dmod_all, dmod_cols)


_BC1 = 1.0 - ADAM_B1 ** ADAM_STEP
_BC2 = 1.0 - ADAM_B2 ** ADAM_STEP


def _adamw_vals(w, g, m, v):
    m = ADAM_B1 * m + (1.0 - ADAM_B1) * g
    v = ADAM_B2 * v + (1.0 - ADAM_B2) * (g * g)
    delta = -ADAM_LR * ((m / _BC1) / (jnp.sqrt(v / _BC2) + ADAM_EPS) + ADAM_WD * w)
    return delta, m, v


def _adamw(name, w, g, m, v, tile):
    R, C = w.shape
    blk = ((tile, C), lambda i: (i, 0))

    def body(i, ins, ps, outs, acc):
        d, mm, vv = _adamw_vals(ins[0][...], ins[1][...], ins[2][...], ins[3][...])
        outs[0][...] = d
        outs[1][...] = mm
        outs[2][...] = vv

    return _ew(name, body, R // tile, [(a, *blk) for a in (w, g, m, v)], [], [(_sds((R, C), F32), *blk)] * 3)


def _sum_slots(ref):
    g = ref[0].astype(F32)
    for j in range(1, N_DEV):
        g = g + ref[j].astype(F32)
    return g


def _adamw_slots(name, slots, w, m, v, tile, dev_major=False):
    L, R, C = w.shape
    spec = pl.BlockSpec((None, tile, C), lambda l, i: (l, i, 0))
    if dev_major:
        slot_spec = pl.BlockSpec((N_DEV, None, tile, C), lambda l, i: (0, l, i, 0))
    else:
        slot_spec = pl.BlockSpec((None, N_DEV, tile, C), lambda l, i: (l, 0, i, 0))

    def kern(s_ref, w_ref, m_ref, v_ref, g_ref, d_ref, mo_ref, vo_ref):
        g = _sum_slots(s_ref)
        g_ref[...] = g
        d_ref[...], mo_ref[...], vo_ref[...] = _adamw_vals(w_ref[...], g, m_ref[...], v_ref[...])

    return pl.pallas_call(
        kern, name=name, grid=(L, R // tile),
        in_specs=[slot_spec, spec, spec, spec],
        out_specs=[spec] * 4, out_shape=[_sds((L, R, C), F32)] * 4,
        compiler_params=_params(("arbitrary", "arbitrary")),
    )(slots, w, m, v)


def _sum_blocks(name, blocks):
    _, R, C = blocks.shape

    def kern(b_ref, o_ref):
        o_ref[...] = _sum_slots(b_ref)

    return pl.pallas_call(kern, name=name, in_specs=[_full_spec(blocks.shape)], out_specs=_full_spec((R, C)),
                          grid=(1,), out_shape=_sds((R, C), F32), compiler_params=_params())(blocks)


BIG = ("win_t", "wo_rnn", "wo_attn", "wout", "wffn_in_t", "wffn_out")
BIG_SRC = ("w_in", "w_o_rnn", "w_o_attn", "w_out", "w_ffn_in", "w_ffn_out")
BIG_T = (True, False, False, False, True, False)
BIG_TILE = (176, 128, 128, 128, 176, 176)


def _chan_full(g8):
    return jnp.transpose(g8, (1, 0, 2)).reshape(g8.shape[1], D)


def kernel(x, c, ctx, c_ctx, w_mod, b_mod, g_mix_pre, g_mix_post, g_ffn_pre, g_ffn_post, w_in, conv_w, conv_b, lru_wa, lru_ba, lru_wx, lru_bx, lru_lam, attn_sink, w_o_rnn, w_o_attn, w_out, w_ffn_in, w_ffn_out, loss_target, m_c_ctx, m_w_mod, m_b_mod, m_g_mix_pre, m_g_mix_post, m_g_ffn_pre, m_g_ffn_post, m_w_in, m_conv_w, m_conv_b, m_lru_wa, m_lru_ba, m_lru_wx, m_lru_bx, m_lru_lam, m_attn_sink, m_w_o_rnn, m_w_o_attn, m_w_out, m_w_ffn_in, m_w_ffn_out, v_c_ctx, v_w_mod, v_b_mod, v_g_mix_pre, v_g_mix_post, v_g_ffn_pre, v_g_ffn_post, v_w_in, v_conv_w, v_conv_b, v_lru_wa, v_lru_ba, v_lru_wx, v_lru_bx, v_lru_lam, v_attn_sink, v_w_o_rnn, v_w_o_attn, v_w_out, v_w_ffn_in, v_w_ffn_out):
    P = dict(c_ctx=c_ctx, w_mod=w_mod, b_mod=b_mod, g_mix_pre=g_mix_pre, g_mix_post=g_mix_post, g_ffn_pre=g_ffn_pre,
             g_ffn_post=g_ffn_post, w_in=w_in, conv_w=conv_w, conv_b=conv_b, lru_wa=lru_wa, lru_ba=lru_ba,
             lru_wx=lru_wx, lru_bx=lru_bx, lru_lam=lru_lam, attn_sink=attn_sink, w_o_rnn=w_o_rnn, w_o_attn=w_o_attn,
             w_out=w_out, w_ffn_in=w_ffn_in, w_ffn_out=w_ffn_out)
    Mo = dict(c_ctx=m_c_ctx, w_mod=m_w_mod, b_mod=m_b_mod, g_mix_pre=m_g_mix_pre, g_mix_post=m_g_mix_post,
              g_ffn_pre=m_g_ffn_pre, g_ffn_post=m_g_ffn_post, w_in=m_w_in, conv_w=m_conv_w, conv_b=m_conv_b,
              lru_wa=m_lru_wa, lru_ba=m_lru_ba, lru_wx=m_lru_wx, lru_bx=m_lru_bx, lru_lam=m_lru_lam,
              attn_sink=m_attn_sink, w_o_rnn=m_w_o_rnn, w_o_attn=m_w_o_attn, w_out=m_w_out, w_ffn_in=m_w_ffn_in,
              w_ffn_out=m_w_ffn_out)
    Vo = dict(c_ctx=v_c_ctx, w_mod=v_w_mod, b_mod=v_b_mod, g_mix_pre=v_g_mix_pre, g_mix_post=v_g_mix_post,
              g_ffn_pre=v_g_ffn_pre, g_ffn_post=v_g_ffn_post, w_in=v_w_in, conv_w=v_conv_w, conv_b=v_conv_b,
              lru_wa=v_lru_wa, lru_ba=v_lru_ba, lru_wx=v_lru_wx, lru_bx=v_lru_bx, lru_lam=v_lru_lam,
              attn_sink=v_attn_sink, w_o_rnn=v_w_o_rnn, w_o_attn=v_w_o_attn, w_out=v_w_out, w_ffn_in=v_w_ffn_in,
              w_ffn_out=v_w_ffn_out)
    L = w_in.shape[0]
    S = x.shape[1]
    me = _lin(*_place())

    small = jnp.concatenate([c.reshape(8, 128), conv_w.reshape(L * CONV_W, 128), lru_ba.reshape(2 * L, 128),
                             lru_bx.reshape(2 * L, 128), lru_lam.reshape(2 * L, 128), jnp.zeros((4, 128), F32)], axis=0)
    small_all = _allgather_small("ag_small", small)
    c_all = small_all[:, 0:8].reshape(N_DEV, D)
    conv_w_f = _chan_full(small_all[:, 8:16]).reshape(L, CONV_W, D)
    lru_ba_f = _chan_full(small_all[:, 16:20]).reshape(L, 2, D)
    lru_bx_f = _chan_full(small_all[:, 20:24]).reshape(L, 2, D)
    lru_lam_f = _chan_full(small_all[:, 24:28]).reshape(L, 2, D)

    c9 = jnp.concatenate([c_all, c_ctx[None], jnp.zeros((MOD_ROWS - N_DEV - 1, D), F32)], axis=0)
    b_shard = lax.dynamic_slice_in_dim(b_mod, me * MOD_SHARD, MOD_SHARD, axis=1)[:, None, :]
    mod_part = _mod_fwd("mod_fwd", c9, w_mod, b_shard)
    mod_all = _allgather_small("ag_mod", mod_part.reshape(L * MOD_ROWS, MOD_SHARD))
    mod_all = jnp.transpose(mod_all.reshape(N_DEV, L, MOD_ROWS, MOD_SHARD), (1, 2, 0, 3)).reshape(L, MOD_ROWS, 6 * D)
    own_row = lax.dynamic_index_in_dim(mod_all, me, axis=1, keepdims=False)
    modrows = jnp.stack([mod_all[:, N_DEV], own_row], axis=1)

    shards = []
    for l in range(L):
        for src, tr in zip(BIG_SRC, BIG_T):
            w = P[src][l]
            shards.append((w.T if tr else w).astype(BF16))
    full = _allgather_hbm("ag_weights", shards)
    Ws = []
    for l in range(L):
        W = {k: full[l * len(BIG) + i].reshape(-1, D) for i, k in enumerate(BIG)}
        W.update(
            cw=conv_w_f[l], cb=conv_b[l][None],
            w4=jnp.concatenate([lru_wa[l, 0], lru_wa[l, 1], lru_wx[l, 0], lru_wx[l, 1]], axis=-1).astype(BF16),
            b4=jnp.concatenate([lru_ba_f[l, 0].reshape(N_RNN_BLOCKS, 1, RB), lru_ba_f[l, 1].reshape(N_RNN_BLOCKS, 1, RB),
                                lru_bx_f[l, 0].reshape(N_RNN_BLOCKS, 1, RB), lru_bx_f[l, 1].reshape(N_RNN_BLOCKS, 1, RB)],
                               axis=-1),
            lam=lru_lam_f[l], sink4=jnp.broadcast_to(attn_sink[l].reshape(N_KV, Q_PER_KV, 1), (N_KV, Q_PER_KV, HEAD)),
            g_mix_pre=g_mix_pre[l][None], g_mix_post=g_mix_post[l][None], g_ffn_pre=g_ffn_pre[l][None],
            g_ffn_post=g_ffn_post[l][None], mod=modrows[l])
        Ws.append(W)

    xa = jnp.concatenate([ctx[0], x[0]], axis=0)
    sq, dxa, Gs = _local_step(xa, loss_target[0], Ws, S)
    loss = lax.psum((0.5 / D) * jnp.sum(sq), ("x", "y", "c"))
    grad_x = dxa[CTX:][None]

    dmod = jnp.concatenate([Gs[l]["mod"] for l in range(L)] + [jnp.zeros((8 - 2 * L, 6 * D), F32)], axis=0)
    dmod_all = _allgather_small("ag_dmod", dmod)
    dmod_cols = lax.dynamic_slice_in_dim(dmod_all, me * MOD_SHARD, MOD_SHARD, axis=2)
    g_w_mod, g_b_mod, dsc_part = _mod_bwd("mod_bwd", c9, w_mod, dmod_all, dmod_cols)
    g_b_mod = g_b_mod[:, 0]

    def rows(name, shape):
        return jnp.concatenate([Gs[l][name].reshape(shape) for l in range(L)], axis=0)

    b4g = [Gs[l]["b4"].reshape(N_RNN_BLOCKS, 4, RB) for l in range(L)]
    sink_row = jnp.concatenate([Gs[l]["sink4"][:, :, 0].reshape(1, N_Q) for l in range(L)]
                               + [jnp.zeros((1, D - L * N_Q), F32)], axis=1)
    small_g = jnp.concatenate(
        [rows("g_mix_pre", (1, D)), rows("g_mix_post", (1, D)), rows("g_ffn_pre", (1, D)), rows("g_ffn_post", (1, D)),
         rows("cb", (1, D)), rows("cw", (CONV_W, D))]
        + [b4g[l][:, d].reshape(1, D) for l in range(L) for d in range(2)]
        + [b4g[l][:, 2 + d].reshape(1, D) for l in range(L) for d in range(2)]
        + [rows("lam", (2, D)), sink_row, dsc_part], axis=0)
    n_small = small_g.shape[0]
    small_tot = _sum_blocks("sum_small", _allgather_small("ag_small_grads", small_g))
    o = 0
    G = {}
    for name in ("g_mix_pre", "g_mix_post", "g_ffn_pre", "g_ffn_post", "conv_b"):
        G[name] = small_tot[o:o + L]
        o += L
    G["conv_w"] = small_tot[o:o + L * CONV_W].reshape(L, CONV_W, D)
    o += L * CONV_W
    for name in ("lru_ba", "lru_bx", "lru_lam"):
        G[name] = small_tot[o:o + 2 * L].reshape(L, 2, D)
        o += 2 * L
    G["attn_sink"] = small_tot[o, :L * N_Q].reshape(L, N_Q)
    sg = jax.nn.sigmoid(c_ctx)
    G["c_ctx"] = small_tot[o + 1] * (sg * (1.0 + c_ctx * (1.0 - sg)))
    G["b_mod"] = g_b_mod
    G["w_mod"] = g_w_mod

    w4g = jnp.concatenate([Gs[l]["w4"].reshape(N_RNN_BLOCKS * RB, 4 * RB) for l in range(L)], axis=0).astype(BF16)
    w4_slots, = _allgather_hbm("ag_gate_grads", [w4g])

    big_slots = _exchange_shards("exchange_grads", [[Gs[l][k].reshape(N_DEV, -1, D) for l in range(L)] for k in BIG], L)

    out_g, out_d, out_m, out_v = {}, {}, {}, {}

    def put(name, res, shape=None):
        g, d, m, v = res
        for dst, val in ((out_g, g), (out_d, d), (out_m, m), (out_v, v)):
            dst[name] = val if shape is None else val.reshape(shape)

    for k, src, tr, tile, slots in zip(BIG, BIG_SRC, BIG_T, BIG_TILE, big_slots):
        lay = (lambda a: jnp.swapaxes(a, 1, 2)) if tr else (lambda a: a)
        res = _adamw_slots("adamw_" + src, slots, lay(P[src]), lay(Mo[src]), lay(Vo[src]), tile)
        put(src, [lay(r) for r in res])
    res = _adamw("adamw_w_mod", w_mod.reshape(L * D, MOD_SHARD), g_w_mod.reshape(L * D, MOD_SHARD),
                 m_w_mod.reshape(L * D, MOD_SHARD), v_w_mod.reshape(L * D, MOD_SHARD), 256)
    put("w_mod", (g_w_mod,) + tuple(res), w_mod.shape)
    def fuse4(wa, wx):
        return jnp.concatenate([wa[:, 0], wa[:, 1], wx[:, 0], wx[:, 1]], axis=-1).reshape(L, N_RNN_BLOCKS * RB, 4 * RB)

    res = _adamw_slots("adamw_gates", w4_slots.reshape(N_DEV, L, N_RNN_BLOCKS * RB, 4 * RB),
                       fuse4(lru_wa, lru_wx), fuse4(m_lru_wa, m_lru_wx), fuse4(v_lru_wa, v_lru_wx), 256, dev_major=True)
    res = [r.reshape(L, N_RNN_BLOCKS, RB, 4, RB) for r in res]
    put("lru_wa", [jnp.stack([r[:, :, :, 0], r[:, :, :, 1]], axis=1) for r in res])
    put("lru_wx", [jnp.stack([r[:, :, :, 2], r[:, :, :, 3]], axis=1) for r in res])
    rep = ("g_mix_pre", "g_mix_post", "g_ffn_pre", "g_ffn_post", "conv_b", "b_mod")

    def pack_rep(T_):
        sink = jnp.concatenate([T_["attn_sink"].reshape(1, L * N_Q), jnp.zeros((1, D - L * N_Q), F32)], axis=1)
        return jnp.concatenate([T_[n].reshape(-1, D) for n in rep] + [sink, T_["c_ctx"][None]], axis=0)

    pk = [pack_rep(T_) for T_ in (P, G, Mo, Vo)]
    n_rep = pk[0].shape[0]
    res = _adamw("adamw_replicated", *[jnp.pad(a, ((0, 24 - n_rep), (0, 0))) for a in pk], 24)
    res = (pk[1],) + tuple(r[:n_rep] for r in res)
    o = 0
    for n in rep:
        k = P[n].size // D
        put(n, [r[o:o + k] for r in res], P[n].shape)
        o += k
    put("attn_sink", [r[o, :L * N_Q] for r in res], attn_sink.shape)
    put("c_ctx", [r[o + 1] for r in res], c_ctx.shape)
    chan = ("conv_w", "lru_ba", "lru_bx", "lru_lam")
    g_own = {n: lax.dynamic_slice_in_dim(G[n], me * RB, RB, axis=2) for n in chan}

    def pack_chan(T_):
        return jnp.concatenate([T_[n].reshape(-1, RB) for n in chan], axis=0)

    pk = [pack_chan(T_) for T_ in (P, g_own, Mo, Vo)]
    n_ch = pk[0].shape[0]
    res = _adamw("adamw_channels", *[jnp.pad(a, ((0, 24 - n_ch), (0, 0))) for a in pk], 24)
    res = (pk[1],) + tuple(r[:n_ch] for r in res)
    o = 0
    for n in chan:
        k = P[n].size // RB
        put(n, [r[o:o + k] for r in res], P[n].shape)
        o += k

    order = ("c_ctx", "w_mod", "b_mod", "g_mix_pre", "g_mix_post", "g_ffn_pre", "g_ffn_post", "w_in", "conv_w", "conv_b",
             "lru_wa", "lru_ba", "lru_wx", "lru_bx", "lru_lam", "attn_sink", "w_o_rnn", "w_o_attn", "w_out", "w_ffn_in",
             "w_ffn_out")
    return (loss, grad_x, *[out_g[n] for n in order], *[out_d[n] for n in order], *[out_m[n] for n in order],
            *[out_v[n] for n in order])
```

```python
import functools
import math

import numpy as np
import jax
import jax.numpy as jnp
from jax import lax
from jax.experimental import pallas as pl
from jax.experimental.pallas import tpu as pltpu

F32 = jnp.float32
BF16 = jnp.bfloat16

D = 1024
CTX = 256
TR = 256
HEAD = 128
N_Q = 8
N_KV = 2
Q_PER_KV = N_Q // N_KV
GRID_W = 64
N_FREQ = HEAD // 4
ROPE_BASE = 10000.0
N_RNN_BLOCKS = 8
CONV_W = 4
CONV_LEFT = 2
LRU_C = 8.0
D_FF = 2816
IN_W = 5632
P_W = IN_W
COL_XR, COL_GR, COL_Q, COL_K, COL_V, COL_GL = 0, 1024, 2048, 3072, 3328, 3584
GLB = 512
EPS = 1e-6
NEG_INF = -1e30
ATT_SCALE = HEAD ** -0.5
N_DEV = 8
VMEM_LIMIT = 56 * 1024 * 1024

ADAM_LR, ADAM_B1, ADAM_B2, ADAM_EPS, ADAM_WD, ADAM_STEP = 0.001, 0.9, 0.999, 1e-08, 0.01, 10

NN = (((1,), (0,)), ((), ()))
NT = (((1,), (1,)), ((), ()))
TN = (((0,), (0,)), ((), ()))


def _dot(a, b, dims=NN):
    return lax.dot_general(a, b, dims, preferred_element_type=F32)


def _params(sem=("arbitrary",)):
    return pltpu.CompilerParams(dimension_semantics=sem, vmem_limit_bytes=VMEM_LIMIT)


def _full_spec(shape):
    nd = len(shape)
    return pl.BlockSpec(shape, lambda *_: (0,) * nd)


ANY = pl.BlockSpec(memory_space=pl.ANY)


def _ew(name, body, n, row_ins, pars, row_outs, accs=(), alias=None):
    n_ri, n_p, n_ro, n_acc = len(row_ins), len(pars), len(row_outs), len(accs)

    def kern(*refs):
        i = pl.program_id(0)
        ins = refs[:n_ri]
        ps = refs[n_ri:n_ri + n_p]
        outs = refs[n_ri + n_p:n_ri + n_p + n_ro]
        acc = refs[n_ri + n_p + n_ro:]
        if n_acc:
            @pl.when(i == 0)
            def _():
                for a in acc:
                    a[...] = jnp.zeros(a.shape, a.dtype)
        body(i, ins, ps, outs, acc)

    in_specs = [ANY if blk is None else pl.BlockSpec(blk, imap) for (_, blk, imap) in row_ins]
    in_specs += [_full_spec(p.shape) for p in pars]
    out_specs = [pl.BlockSpec(blk, imap) for (_, blk, imap) in row_outs] + [_full_spec(a.shape) for a in accs]
    out_shape = [s for (s, _, _) in row_outs] + list(accs)
    return pl.pallas_call(
        kern, name=name, grid=(n,), in_specs=in_specs, out_specs=out_specs, out_shape=out_shape,
        input_output_aliases=alias or {}, compiler_params=_params(),
    )(*[a for (a, _, _) in row_ins], *pars)


def _rowblk(width, colblk=0, roff=0, tile=TR):
    return (tile, width), (lambda i: (i + roff, colblk))


def _sds(shape, dtype):
    return jax.ShapeDtypeStruct(shape, dtype)


def _mm_call(name, a, b, mode, out_dtype, tm, tn, rows_outer=True, single_b=False):
    if mode == "TN":
        (K, M), N = a.shape, b.shape[1]
    else:
        (M, K), N = a.shape, (b.shape[1] if mode == "NN" else b.shape[0])
    assert M % tm == 0 and N % tn == 0, (name, M, N, K, tm, tn)
    ij = (lambda g0, g1: (g0, g1)) if rows_outer else (lambda g0, g1: (g1, g0))
    grid = (M // tm, N // tn) if rows_outer else (N // tn, M // tm)
    if mode == "TN":
        a_spec = pl.BlockSpec((K, tm), lambda g0, g1: (0, ij(g0, g1)[0]))
    else:
        a_spec = pl.BlockSpec((tm, K), lambda g0, g1: (ij(g0, g1)[0], 0))
    b_blk, b_map = ((tn, K), lambda g0, g1: (ij(g0, g1)[1], 0)) if mode == "NT" else \
                   ((K, tn), lambda g0, g1: (0, ij(g0, g1)[1]))
    b_spec = pl.BlockSpec(b_blk, b_map, pipeline_mode=pl.Buffered(1)) if single_b else pl.BlockSpec(b_blk, b_map)
    dims = {"NN": NN, "NT": NT, "TN": TN}[mode]

    def kern(a_ref, b_ref, o_ref):
        o_ref[...] = _dot(a_ref[...], b_ref[...], dims).astype(o_ref.dtype)

    return pl.pallas_call(
        kern, name=name, grid=grid, in_specs=[a_spec, b_spec],
        out_specs=pl.BlockSpec((tm, tn), lambda g0, g1: ij(g0, g1)),
        out_shape=_sds((M, N), out_dtype), compiler_params=_params(("arbitrary", "arbitrary")),
    )(a, b)


def _mm_act(name, a, w, mode, out_dtype=F32):
    rows, K = a.shape
    N = w.shape[1] if mode == "NN" else w.shape[0]
    if K > D_FF:
        return _mm_call(name, a, w, mode, out_dtype, rows // 8, N, single_b=True)
    tn = N if N <= 1024 else 1408
    return _mm_call(name, a, w, mode, out_dtype, rows // 4, tn)


def _mm_wgrad(name, x, dy, out_dtype=BF16):
    M = x.shape[1]
    tm = 1408 if M == D_FF else 512
    return _mm_call(name, x, dy, "TN", out_dtype, tm, dy.shape[1], single_b=True)


def _sigmoid(x):
    return 0.5 * jnp.tanh(0.5 * x) + 0.5


def _silu(x):
    return x * _sigmoid(x)


def _silu_grad(x):
    s = _sigmoid(x)
    return s * (1.0 + x * (1.0 - s))


_GELU_K = math.sqrt(2.0 / math.pi)


def _gelu(x):
    return 0.5 * x * (1.0 + jnp.tanh(_GELU_K * (x + 0.044715 * x * x * x)))


def _gelu_grad(x):
    t = jnp.tanh(_GELU_K * (x + 0.044715 * x * x * x))
    return 0.5 * (1.0 + t) + 0.5 * x * (1.0 - t * t) * _GELU_K * (1.0 + 3.0 * 0.044715 * x * x)


def _log_sigmoid(x):
    return jnp.minimum(x, 0.0) - jnp.log(1.0 + jnp.exp(-jnp.abs(x)))


def _neg_expm1(x):
    series = -x * (1.0 + x * (0.5 + x * (1.0 / 6.0 + x * (1.0 / 24.0))))
    return jnp.where(x > -0.03, series, 1.0 - jnp.exp(x))


def _rms(x):
    r = lax.rsqrt(jnp.mean(x * x, axis=-1, keepdims=True) + EPS)
    return x * r, r


def _rms_bwd(dy, y, r):
    return r * (dy - y * jnp.mean(dy * y, axis=-1, keepdims=True))


def _modrow(mod_ref, i, chunk):
    lo = mod_ref[0:1, chunk * D:(chunk + 1) * D]
    hi = mod_ref[1:2, chunk * D:(chunk + 1) * D]
    return jnp.where(i == 0, lo, hi)


def _acc_seg(acc_ref, i, val):
    zero = jnp.zeros_like(val)
    acc_ref[0:1, :] += jnp.where(i == 0, val, zero)
    acc_ref[1:2, :] += jnp.where(i == 0, zero, val)


def _colsum(x):
    return jnp.sum(x, axis=0, keepdims=True)


SH1, SC1, GA1, SH2, SC2, GA2 = range(6)


def _normmod_fwd(name, xa, g, mod, c_sh, c_sc):
    T = xa.shape[0]

    def body(i, ins, ps, outs, acc):
        y, _ = _rms(ins[0][...])
        h = (y * ps[0][...]) * (1.0 + _modrow(ps[1], i, c_sc)) + _modrow(ps[1], i, c_sh)
        outs[0][...] = h.astype(BF16)

    return _ew(name, body, T // TR, [(xa, *_rowblk(D))], [g, mod], [(_sds((T, D), BF16), *_rowblk(D))])[0]


def _resid_norm_fwd(name, xin, mat, gpost, mod, c_ga, gnext, modn, c_sh, c_sc):
    T = xin.shape[0]

    def body(i, ins, ps, outs, acc):
        ym, _ = _rms(ins[1][...])
        xo = ins[0][...] + _modrow(ps[1], i, c_ga) * (ym * ps[0][...])
        outs[0][...] = xo
        y, _ = _rms(xo)
        h = (y * ps[2][...]) * (1.0 + _modrow(ps[3], i, c_sc)) + _modrow(ps[3], i, c_sh)
        outs[1][...] = h.astype(BF16)

    return _ew(name, body, T // TR, [(xin, *_rowblk(D)), (mat, *_rowblk(D))], [gpost, mod, gnext, modn],
               [(_sds((T, D), F32), *_rowblk(D)), (_sds((T, D), BF16), *_rowblk(D))])


def _resid_loss_fwd(name, xin, mat, gpost, mod, c_ga, target):
    T = xin.shape[0]

    def body(i, ins, ps, outs, acc):
        ym, _ = _rms(ins[1][...])
        xo = ins[0][...] + _modrow(ps[1], i, c_ga) * (ym * ps[0][...])
        err = xo - ins[2][...]
        lat = i > 0
        outs[0][...] = jnp.where(lat, err * (1.0 / D), 0.0)
        acc[0][...] += jnp.where(lat, _colsum(err * err), 0.0)

    tgt_blk = ((TR, D), lambda i: (jnp.maximum(i - 1, 0), 0))
    dx, sq = _ew(name, body, T // TR, [(xin, *_rowblk(D)), (mat, *_rowblk(D)), (target, *tgt_blk)], [gpost, mod],
                 [(_sds((T, D), F32), *_rowblk(D))], [_sds((1, D), F32)])
    return dx, sq


def _resid_bwd_vals(i, dout, mat, gpost, mod_ref, c_ga, acc_ga, acc_g):
    ym, rm = _rms(mat)
    ga = _modrow(mod_ref, i, c_ga)
    _acc_seg(acc_ga, i, _colsum(dout * (ym * gpost)))
    dn = dout * ga
    acc_g[...] += _colsum(dn * ym)
    return _rms_bwd(dn * gpost, ym, rm)


def _normmod_bwd_vals(i, dh, xin, g, mod_ref, c_sh, c_sc, acc_sh, acc_sc, acc_g):
    y, r = _rms(xin)
    _acc_seg(acc_sc, i, _colsum(dh * (y * g)))
    _acc_seg(acc_sh, i, _colsum(dh))
    dyg = dh * (1.0 + _modrow(mod_ref, i, c_sc))
    acc_g[...] += _colsum(dyg * y)
    return _rms_bwd(dyg * g, y, r)


def _resid_bwd(name, dout, mat, gpost, mod, c_ga):
    T = dout.shape[0]

    def body(i, ins, ps, outs, acc):
        dm = _resid_bwd_vals(i, ins[0][...], ins[1][...], ps[0][...], ps[1], c_ga, acc[0], acc[1])
        outs[0][...] = dm.astype(BF16)

    return _ew(name, body, T // TR, [(dout, *_rowblk(D)), (mat, *_rowblk(D))], [gpost, mod],
               [(_sds((T, D), BF16), *_rowblk(D))], [_sds((2, D), F32), _sds((1, D), F32)])


def _normmod_resid_bwd(name, dh, xin, gpre, mod, c_sh, c_sc, dres, mat, gpost, c_ga):
    T = dh.shape[0]

    def body(i, ins, ps, outs, acc):
        dx = ins[2][...] + _normmod_bwd_vals(i, ins[0][...], ins[1][...], ps[0][...], ps[1], c_sh, c_sc,
                                             acc[0], acc[1], acc[2])
        outs[0][...] = dx
        dm = _resid_bwd_vals(i, dx, ins[3][...], ps[2][...], ps[1], c_ga, acc[3], acc[4])
        outs[1][...] = dm.astype(BF16)

    return _ew(name, body, T // TR, [(dh, *_rowblk(D)), (xin, *_rowblk(D)), (dres, *_rowblk(D)), (mat, *_rowblk(D))],
               [gpre, mod, gpost],
               [(_sds((T, D), F32), *_rowblk(D)), (_sds((T, D), BF16), *_rowblk(D))],
               [_sds((2, D), F32), _sds((2, D), F32), _sds((1, D), F32), _sds((2, D), F32), _sds((1, D), F32)])


def _normmod_bwd(name, dh, xin, gpre, mod, c_sh, c_sc, dres):
    T = dh.shape[0]

    def body(i, ins, ps, outs, acc):
        outs[0][...] = ins[2][...] + _normmod_bwd_vals(i, ins[0][...], ins[1][...], ps[0][...], ps[1], c_sh, c_sc,
                                                       acc[0], acc[1], acc[2])

    return _ew(name, body, T // TR, [(dh, *_rowblk(D)), (xin, *_rowblk(D)), (dres, *_rowblk(D))], [gpre, mod],
               [(_sds((T, D), F32), *_rowblk(D))], [_sds((2, D), F32), _sds((2, D), F32), _sds((1, D), F32)])


def _gate_fwd(name, p, ya, yb):
    T = ya.shape[0]

    def body(i, ins, ps, outs, acc):
        gl = [r[...].astype(F32) for r in ins[:4]]
        ga = _sigmoid(jnp.concatenate(gl[:2], axis=1))
        gb = _sigmoid(jnp.concatenate(gl[2:], axis=1))
        outs[0][...] = (ga * ins[4][...] + gb * ins[5][...]).astype(BF16)

    return _ew(name, body, T // TR,
               [(p, *_rowblk(GLB, COL_GL // GLB + q)) for q in range(4)] + [(ya, *_rowblk(D)), (yb, *_rowblk(D))],
               [], [(_sds((T, D), BF16), *_rowblk(D))])[0]


def _gate_bwd(name, p, ya, yb, dz):
    T = ya.shape[0]

    def kern(gl_ref, ya_ref, yb_ref, dz_ref, dya_ref, dyb_ref, dp_ref):
        j = pl.program_id(1)
        g = _sigmoid(gl_ref[...].astype(F32))
        dzv = dz_ref[...]
        dbranch = (dzv * g).astype(BF16)
        dg = dzv * g * (1.0 - g)

        @pl.when(j < 2)
        def _():
            dya_ref[...] = dbranch
            dp_ref[...] = (dg * ya_ref[...]).astype(BF16)

        @pl.when(j >= 2)
        def _():
            dyb_ref[...] = dbranch
            dp_ref[...] = (dg * yb_ref[...]).astype(BF16)

    half = pl.BlockSpec((TR, GLB), lambda i, j: (i, j % 2))
    return pl.pallas_call(
        kern, name=name, grid=(T // TR, 4),
        in_specs=[pl.BlockSpec((TR, GLB), lambda i, j: (i, COL_GL // GLB + j)), half, half, half],
        out_specs=[pl.BlockSpec((TR, GLB), lambda i, j: (i, jnp.minimum(j, 1))),
                   pl.BlockSpec((TR, GLB), lambda i, j: (i, jnp.maximum(j - 2, 0))),
                   pl.BlockSpec((TR, GLB), lambda i, j: (i, COL_GL // GLB + j))],
        out_shape=[_sds((T, D), BF16), _sds((T, D), BF16), _sds((T, P_W), BF16)],
        compiler_params=_params(("arbitrary", "arbitrary")),
    )(p, ya, yb, dz)


def _swiglu_fwd(name, f):
    T = f.shape[0]

    def body(i, ins, ps, outs, acc):
        outs[0][...] = (_silu(ins[0][...].astype(F32)) * ins[1][...].astype(F32)).astype(BF16)

    return _ew(name, body, T // TR, [(f, *_rowblk(D_FF, 0)), (f, *_rowblk(D_FF, 1))], [],
               [(_sds((T, D_FF), BF16), *_rowblk(D_FF))])[0]


def _swiglu_bwd(name, f, ds):
    T = f.shape[0]

    def body(i, ins, ps, outs, acc):
        gate, up, dsv = ins[0][...].astype(F32), ins[1][...].astype(F32), ins[2][...].astype(F32)
        dgate = dsv * up * _silu_grad(gate)
        dup = dsv * _silu(gate)
        outs[0][...] = jnp.concatenate([dgate, dup], axis=1).astype(BF16)

    return _ew(name, body, T // TR, [(f, *_rowblk(D_FF, 0)), (f, *_rowblk(D_FF, 1)), (ds, *_rowblk(D_FF))], [],
               [(_sds((T, 2 * D_FF), BF16), *_rowblk(2 * D_FF))])[0]


AB = 128
CTX_BLKS = CTX // AB


def _rope_tables(S):
    pos = jnp.arange(S, dtype=jnp.int32)
    inv = ROPE_BASE ** (-jnp.arange(N_FREQ, dtype=F32) / N_FREQ)
    ang_r = (pos // GRID_W).astype(F32)[:, None] * inv[None, :]
    ang_c = (pos % GRID_W).astype(F32)[:, None] * inv[None, :]
    cos = jnp.concatenate([jnp.cos(ang_r)] * 2 + [jnp.cos(ang_c)] * 2, axis=1)
    sin = jnp.concatenate([-jnp.sin(ang_r), jnp.sin(ang_r), -jnp.sin(ang_c), jnp.sin(ang_c)], axis=1)
    return cos, sin


def _rope(x, cos, sin):
    w = x.shape[1]
    reps = w // HEAD
    lane = lax.broadcasted_iota(jnp.int32, x.shape, 1)
    partner = jnp.where((lane & 63) < 32, pltpu.roll(x, w - 32, 1), pltpu.roll(x, 32, 1))
    return x * jnp.tile(cos, (1, reps)) + partner * jnp.tile(sin, (1, reps))


def _unrope(dx, cos, sin):
    w = dx.shape[1]
    reps = w // HEAD
    lane = lax.broadcasted_iota(jnp.int32, dx.shape, 1)
    t = dx * jnp.tile(sin, (1, reps))
    partner = jnp.where((lane & 63) < 32, pltpu.roll(t, w - 32, 1), pltpu.roll(t, 32, 1))
    return dx * jnp.tile(cos, (1, reps)) + partner


def _qkv_prep(name, p, cos, sin, S):
    T = CTX + S
    nb = S // AB
    KW = N_KV * HEAD

    def kern(q_ref, k_ref, v_ref, cos_ref, sin_ref, qa_ref, kp_ref, vp_ref, kc_ref, vc_ref):
        i = pl.program_id(0)
        cos_v, sin_v = cos_ref[...], sin_ref[...]
        @pl.when(i < CTX_BLKS)
        def _():
            qa_ref[...] = q_ref[...]
            kc_ref[...] = k_ref[...]
            vc_ref[...] = v_ref[...]
            kp_ref[...] = jnp.zeros(kp_ref.shape, BF16)
            vp_ref[...] = jnp.zeros(vp_ref.shape, BF16)

        @pl.when(i >= CTX_BLKS)
        def _():
            qa_ref[...] = _rope(q_ref[...].astype(F32), cos_v, sin_v).astype(BF16)
            kp_ref[...] = _rope(k_ref[...].astype(F32), cos_v, sin_v).astype(BF16)
            vp_ref[...] = v_ref[...]

    lat_map = lambda i: (jnp.maximum(i - CTX_BLKS, 0), 0)
    pad_map = lambda i: (jnp.where(i == 0, 0, jnp.where(i == 1, nb + 1, i - 1)), 0)
    ctx_map = lambda i: (jnp.minimum(i, CTX_BLKS - 1), 0)
    return pl.pallas_call(
        kern, name=name, grid=(T // AB,),
        in_specs=[pl.BlockSpec((AB, N_Q * HEAD), lambda i: (i, COL_Q // (N_Q * HEAD))),
                  pl.BlockSpec((AB, KW), lambda i: (i, COL_K // KW)),
                  pl.BlockSpec((AB, KW), lambda i: (i, COL_V // KW)),
                  pl.BlockSpec((AB, HEAD), lat_map), pl.BlockSpec((AB, HEAD), lat_map)],
        out_specs=[pl.BlockSpec((AB, N_Q * HEAD), lambda i: (i, 0)),
                   pl.BlockSpec((AB, KW), pad_map), pl.BlockSpec((AB, KW), pad_map),
                   pl.BlockSpec((AB, KW), ctx_map), pl.BlockSpec((AB, KW), ctx_map)],
        out_shape=[_sds((T, N_Q * HEAD), BF16), _sds((S + 2 * AB, KW), BF16), _sds((S + 2 * AB, KW), BF16),
                   _sds((CTX, KW), BF16), _sds((CTX, KW), BF16)],
        compiler_params=_params(),
    )(p, p, p, cos, sin)


GQ = Q_PER_KV * AB
GW = Q_PER_KV * HEAD


def _stack_heads(blk):
    return jnp.concatenate([blk[:, g * HEAD:(g + 1) * HEAD] for g in range(Q_PER_KV)], axis=0)


def _unstack_heads(x4):
    return jnp.concatenate([x4[g * AB:(g + 1) * AB, :] for g in range(Q_PER_KV)], axis=1)


def _sink_col(sink_ref):
    return jnp.concatenate([jnp.broadcast_to(sink_ref[g:g + 1, 0:1], (AB, 1)) for g in range(Q_PER_KV)], axis=0)


def _band_mask(n, S):
    r = lax.broadcasted_iota(jnp.int32, (GQ, 3 * AB), 0)
    c = lax.broadcasted_iota(jnp.int32, (GQ, 3 * AB), 1)
    d = c - AB - (r & (AB - 1))
    kpos = n * AB - AB + c
    return (jnp.abs(d) <= AB) & (kpos >= 0) & (kpos < S)


def _attn_probs(q4, kc, sink, kb, mask):
    s_ctx = _dot(q4, kc, NT) * ATT_SCALE
    m = jnp.maximum(jnp.max(s_ctx, axis=-1, keepdims=True), sink)
    if kb is not None:
        s_b = jnp.where(mask, _dot(q4, kb, NT) * ATT_SCALE, NEG_INF)
        m = jnp.maximum(m, jnp.max(s_b, axis=-1, keepdims=True))
    p_ctx = jnp.exp(s_ctx - m)
    p_sink = jnp.exp(sink - m)
    l = jnp.sum(p_ctx, axis=-1, keepdims=True) + p_sink
    p_b = None
    if kb is not None:
        p_b = jnp.exp(s_b - m)
        l = l + jnp.sum(p_b, axis=-1, keepdims=True)
    inv = 1.0 / l
    return p_ctx * inv, (None if p_b is None else p_b * inv), p_sink * inv


def _attn_fwd(name, qa, kc, vc, sink4, S, band=None, prev=None):
    T = qa.shape[0]
    has_band = band is not None
    nq = S // AB if has_band else CTX_BLKS
    q_off = CTX_BLKS if has_band else 0

    def kern(*refs):
        q_ref, kc_ref, vc_ref, sink_ref = refs[:4]
        rest = refs[4:]
        if has_band:
            kp_ref, vp_ref = rest[:2]
            rest = rest[2:]
        o_ref = rest[-1]
        n = pl.program_id(1)
        q4 = _stack_heads(q_ref[...])
        sink = _sink_col(sink_ref)
        kb = vb = mask = None
        if has_band:
            start = pl.multiple_of(n * AB, AB)
            kb = kp_ref[pl.ds(start, 3 * AB), :]
            vb = vp_ref[pl.ds(start, 3 * AB), :]
            mask = _band_mask(n, S)
        p_ctx, p_b, _ = _attn_probs(q4, kc_ref[...], sink, kb, mask)
        o4 = _dot(p_ctx.astype(BF16), vc_ref[...])
        if has_band:
            o4 = o4 + _dot(p_b.astype(BF16), vb)
        o_ref[...] = _unstack_heads(o4).astype(BF16)

    in_specs = [pl.BlockSpec((AB, GW), lambda kh, n: (n + q_off, kh)),
                pl.BlockSpec((CTX, HEAD), lambda kh, n: (0, kh)), pl.BlockSpec((CTX, HEAD), lambda kh, n: (0, kh)),
                pl.BlockSpec((None, Q_PER_KV, HEAD), lambda kh, n: (kh, 0, 0))]
    args = [qa, kc, vc, sink4]
    if has_band:
        in_specs += [pl.BlockSpec((S + 2 * AB, HEAD), lambda kh, n: (0, kh))] * 2
        args += list(band)
    alias = {}
    if prev is not None:
        in_specs.append(ANY)
        alias = {len(args): 0}
        args.append(prev)
    return pl.pallas_call(
        kern, name=name, grid=(N_KV, nq), in_specs=in_specs,
        out_specs=pl.BlockSpec((AB, GW), lambda kh, n: (n + q_off, kh)),
        out_shape=_sds((T, N_Q * HEAD), BF16), input_output_aliases=alias,
        compiler_params=_params(("arbitrary", "arbitrary")),
    )(*args)


def _attn_bwd(name, qa, kc, vc, sink4, o_all, do_all, S, band=None, prev_dq=None):
    T = qa.shape[0]
    has_band = band is not None
    nq = S // AB if has_band else CTX_BLKS
    q_off = CTX_BLKS if has_band else 0
    KW = N_KV * HEAD

    def kern(*refs):
        q_ref, kc_ref, vc_ref, sink_ref, o_ref, do_ref = refs[:6]
        rest = refs[6:]
        if has_band:
            kp_ref, vp_ref = rest[:2]
            rest = rest[2:]
        if prev_dq is not None:
            rest = rest[1:]
        dq_ref, dkc_ref, dvc_ref, dsink_ref = rest[:4]
        n = pl.program_id(1)

        @pl.when(n == 0)
        def _():
            dkc_ref[...] = jnp.zeros(dkc_ref.shape, F32)
            dvc_ref[...] = jnp.zeros(dvc_ref.shape, F32)
            dsink_ref[...] = jnp.zeros(dsink_ref.shape, F32)
            if has_band:
                rest[4][...] = jnp.zeros(rest[4].shape, F32)
                rest[5][...] = jnp.zeros(rest[5].shape, F32)

        q4 = _stack_heads(q_ref[...])
        sink = _sink_col(sink_ref)
        kc_v, vc_v = kc_ref[...], vc_ref[...]
        kb = vb = mask = None
        if has_band:
            start = pl.multiple_of(n * AB, AB)
            kb = kp_ref[pl.ds(start, 3 * AB), :]
            vb = vp_ref[pl.ds(start, 3 * AB), :]
            mask = _band_mask(n, S)
        p_ctx, p_b, p_sink = _attn_probs(q4, kc_v, sink, kb, mask)
        do4 = _stack_heads(do_ref[...])
        o4 = _stack_heads(o_ref[...]).astype(F32)
        delta = jnp.sum(do4 * o4, axis=-1, keepdims=True)
        do4b = do4.astype(BF16)
        ds_ctx = (p_ctx * (_dot(do4b, vc_v, NT) - delta)).astype(BF16)
        dq4 = _dot(ds_ctx, kc_v)
        dkc_ref[...] += _dot(ds_ctx, q4, TN) * ATT_SCALE
        dvc_ref[...] += _dot(p_ctx.astype(BF16), do4b, TN)
        if has_band:
            ds_b = (p_b * (_dot(do4b, vb, NT) - delta)).astype(BF16)
            dq4 = dq4 + _dot(ds_b, kb)
            rest[4][pl.ds(start, 3 * AB), :] += _dot(ds_b, q4, TN) * ATT_SCALE
            rest[5][pl.ds(start, 3 * AB), :] += _dot(p_b.astype(BF16), do4b, TN)
        dq_ref[...] = _unstack_heads(dq4 * ATT_SCALE)
        ps = p_sink * delta
        dsink_ref[...] += jnp.concatenate(
            [jnp.broadcast_to(-jnp.sum(ps[g * AB:(g + 1) * AB, :], axis=0, keepdims=True), (1, HEAD))
             for g in range(Q_PER_KV)], axis=0)

    q_spec = pl.BlockSpec((AB, GW), lambda kh, n: (n + q_off, kh))
    c_spec = pl.BlockSpec((CTX, HEAD), lambda kh, n: (0, kh))
    s_spec = pl.BlockSpec((None, Q_PER_KV, HEAD), lambda kh, n: (kh, 0, 0))
    in_specs = [q_spec, c_spec, c_spec, s_spec, q_spec, q_spec]
    args = [qa, kc, vc, sink4, o_all, do_all]
    out_specs = [q_spec, c_spec, c_spec, s_spec]
    out_shape = [_sds((T, N_Q * HEAD), F32), _sds((CTX, KW), F32), _sds((CTX, KW), F32), _sds((N_KV, Q_PER_KV, HEAD), F32)]
    if has_band:
        p_spec = pl.BlockSpec((S + 2 * AB, HEAD), lambda kh, n: (0, kh))
        in_specs += [p_spec, p_spec]
        args += list(band)
        out_specs += [p_spec, p_spec]
        out_shape += [_sds((S + 2 * AB, KW), F32)] * 2
    alias = {}
    if prev_dq is not None:
        in_specs.append(ANY)
        alias = {len(args): 0}
        args.append(prev_dq)
    return pl.pallas_call(
        kern, name=name, grid=(N_KV, nq), in_specs=in_specs, out_specs=out_specs, out_shape=out_shape,
        input_output_aliases=alias, compiler_params=_params(("arbitrary", "arbitrary")),
    )(*args)


def _dqkv_assemble(name, dp, dq_all, dkp, dvp, dkc_l, dvc_l, dkc_c, dvc_c, cos, sin, S):
    T = CTX + S
    KW = N_KV * HEAD
    HALF = N_Q * HEAD // 2

    def kern(dq_ref, dkp_ref, dvp_ref, dkcl_ref, dvcl_ref, dkcc_ref, dvcc_ref, cos_ref, sin_ref, dp_in, out_ref):
        i = pl.program_id(0)
        j = pl.program_id(1)
        lat = i >= CTX_BLKS
        cos_v, sin_v = cos_ref[...], sin_ref[...]

        @pl.when(j < 2)
        def _():
            dq = dq_ref[...]
            out_ref[...] = jnp.where(lat, _unrope(dq, cos_v, sin_v), dq).astype(BF16)

        @pl.when(j == 2)
        def _():
            dk = jnp.where(lat, _unrope(dkp_ref[...], cos_v, sin_v), dkcl_ref[...] + dkcc_ref[...])
            dv = jnp.where(lat, dvp_ref[...], dvcl_ref[...] + dvcc_ref[...])
            out_ref[...] = jnp.concatenate([dk, dv], axis=1).astype(BF16)

    lat_map = lambda i, j: (jnp.maximum(i - CTX_BLKS, 0), 0)
    pad_map = lambda i, j: (jnp.maximum(i - 1, 0), 0)
    ctx_map = lambda i, j: (jnp.minimum(i, CTX_BLKS - 1), 0)
    return pl.pallas_call(
        kern, name=name, grid=(T // AB, 3),
        in_specs=[pl.BlockSpec((AB, HALF), lambda i, j: (i, jnp.minimum(j, 1))),
                  pl.BlockSpec((AB, KW), pad_map), pl.BlockSpec((AB, KW), pad_map),
                  pl.BlockSpec((AB, KW), ctx_map), pl.BlockSpec((AB, KW), ctx_map),
                  pl.BlockSpec((AB, KW), ctx_map), pl.BlockSpec((AB, KW), ctx_map),
                  pl.BlockSpec((AB, HEAD), lat_map), pl.BlockSpec((AB, HEAD), lat_map), ANY],
        out_specs=pl.BlockSpec((AB, HALF), lambda i, j: (i, COL_Q // HALF + j)),
        out_shape=_sds((T, P_W), BF16), input_output_aliases={9: 0},
        compiler_params=_params(("arbitrary", "arbitrary")),
    )(dq_all, dkp, dvp, dkc_l, dvc_l, dkc_c, dvc_c, cos, sin, dp)


RB = 128
CH = 256
HALO = 8
SUB = 8
GRP = 8


def _vscan(a, b, reverse):
    row = lax.broadcasted_iota(jnp.int32, a.shape, 0)
    A, H = a, b
    for s in (1, 2, 4):
        sh = SUB - s if reverse else s
        m = (row < SUB - s) if reverse else (row >= s)
        As = pltpu.roll(A, sh, 0)
        Hs = pltpu.roll(H, sh, 0)
        H = jnp.where(m, A * Hs + H, H)
        A = jnp.where(m, A * As, A)
    return A, H


def _scan_rows(a_ref, b_ref, r0, nrows, reverse, carry, emit):
    ngrp = nrows // (SUB * GRP)
    row = lax.broadcasted_iota(jnp.int32, (SUB, RB), 0)

    def grp(gi, carry):
        g = (ngrp - 1 - gi) if reverse else gi
        base = r0 + g * (SUB * GRP)
        for v in (range(GRP - 1, -1, -1) if reverse else range(GRP)):
            rs = pl.multiple_of(base + v * SUB, SUB)
            A, H = _vscan(a_ref[pl.ds(rs, SUB), :], b_ref[pl.ds(rs, SUB), :], reverse)
            hf = H + A * carry
            if reverse:
                before = jnp.where(row == SUB - 1, carry, pltpu.roll(hf, SUB - 1, 0))
                carry = hf[0:1, :]
            else:
                before = jnp.where(row == 0, carry, pltpu.roll(hf, 1, 0))
                carry = hf[SUB - 1:SUB, :]
            emit(rs, hf, before)
        return carry

    return lax.fori_loop(0, ngrp, grp, carry)


def _conv_taps(ext, t, T, transpose=False):
    lo = jnp.where(t < CTX, 0, CTX)
    hi = jnp.where(t < CTX, CTX, T)
    n = CH + 2 * HALO
    taps = []
    for k in range(CONV_W):
        off = k - CONV_LEFT
        if transpose:
            off = -off
        sh = pltpu.roll(ext, (-off) % n, 0)[HALO:HALO + CH, :]
        valid = (t + off >= lo) & (t + off < hi)
        taps.append(jnp.where(valid, sh, 0.0))
    return taps


def _lru_gates(xl, w4, b4, ls):
    pre = _dot(xl.astype(BF16), w4) + b4
    out = []
    for d in range(2):
        r = _sigmoid(pre[:, d * RB:(d + 1) * RB])
        i = _sigmoid(pre[:, (2 + d) * RB:(3 + d) * RB])
        la = LRU_C * r * ls[d:d + 1, :]
        a = jnp.exp(la)
        mult = jnp.sqrt(_neg_expm1(2.0 * la))
        out.append((r, i, a, mult))
    return out


def _rnn_specs(T):
    col = lambda n, *_: (0, n)
    return dict(
        xr=pl.BlockSpec((T, RB), lambda n, *_: (0, COL_XR // RB + n)),
        gr=pl.BlockSpec((T, RB), lambda n, *_: (0, COL_GR // RB + n)),
        act=pl.BlockSpec((T, RB), col),
        cw=pl.BlockSpec((CONV_W, RB), col), cb=pl.BlockSpec((1, RB), col),
        w4=pl.BlockSpec((None, RB, 4 * RB), lambda n, *_: (n, 0, 0)),
        b4=pl.BlockSpec((None, 1, 4 * RB), lambda n, *_: (n, 0, 0)),
        lam=pl.BlockSpec((2, RB), col))


def _fill_padded(pad_ref, src_ref, T):
    pad_ref[0:HALO, :] = jnp.zeros((HALO, RB), F32)
    pad_ref[HALO + T:2 * HALO + T, :] = jnp.zeros((HALO, RB), F32)
    pad_ref[HALO:HALO + T, :] = src_ref[...].astype(F32)


def _rnn_fwd(name, p, cw, cb, w4, b4, lam, T):
    def kern(xr_ref, gr_ref, cw_ref, cb_ref, w4_ref, b4_ref, lam_ref, u_ref, hpf_ref, hpb_ref,
             xpad, a0, b0, a1, b1, y):
        _fill_padded(xpad, xr_ref, T)
        ls = _log_sigmoid(lam_ref[...])
        w4v, b4v, cwv, cbv = w4_ref[...], b4_ref[...], cw_ref[...], cb_ref[...]

        def chunk(ci, _):
            base = pl.multiple_of(ci * CH, CH)
            t = base + lax.broadcasted_iota(jnp.int32, (CH, 1), 0)
            taps = _conv_taps(xpad[pl.ds(base, CH + 2 * HALO), :], t, T)
            xl = cbv + sum(taps[k] * cwv[k:k + 1, :] for k in range(CONV_W))
            for d, (r, i, a, mult) in enumerate(_lru_gates(xl, w4v, b4v, ls)):
                (a0, a1)[d][pl.ds(base, CH), :] = a
                (b0, b1)[d][pl.ds(base, CH), :] = mult * (i * xl)
            return 0

        lax.fori_loop(0, T // CH, chunk, 0)
        zero = jnp.zeros((1, RB), F32)

        def emit_f(rs, hf, before):
            y[pl.ds(rs, SUB), :] = hf
            hpf_ref[pl.ds(rs, SUB), :] = before

        def emit_b(rs, hf, before):
            y[pl.ds(rs, SUB), :] += hf
            hpb_ref[pl.ds(rs, SUB), :] = before

        _scan_rows(a0, b0, 0, T, False, zero, emit_f)
        c = _scan_rows(a1, b1, 0, CTX, True, zero, emit_b)
        _scan_rows(a1, b1, CTX, T - CTX, True, c, emit_b)

        def finish(ci, _):
            base = pl.multiple_of(ci * CH, CH)
            gr = gr_ref[pl.ds(base, CH), :].astype(F32)
            u_ref[pl.ds(base, CH), :] = (y[pl.ds(base, CH), :] * _gelu(gr)).astype(BF16)
            return 0

        lax.fori_loop(0, T // CH, finish, 0)

    sp = _rnn_specs(T)
    return pl.pallas_call(
        kern, name=name, grid=(N_RNN_BLOCKS,),
        in_specs=[sp["xr"], sp["gr"], sp["cw"], sp["cb"], sp["w4"], sp["b4"], sp["lam"]],
        out_specs=[sp["act"]] * 3,
        out_shape=[_sds((T, D), BF16), _sds((T, D), F32), _sds((T, D), F32)],
        scratch_shapes=[pltpu.VMEM((T + 2 * HALO, RB), F32)] + [pltpu.VMEM((T, RB), F32)] * 5,
        compiler_params=_params(),
    )(p, p, cw, cb, w4, b4, lam)


def _rnn_bwd(name, p, du, hpf, hpb, dp, cw, cb, w4, b4, lam, T):
    def kern(xr_ref, gr_ref, du_ref, hpf_ref, hpb_ref, cw_ref, cb_ref, w4_ref, b4_ref, lam_ref, dp_in,
             dp_ref, dcw_ref, dcb_ref, dw4_ref, db4_ref, dlam_ref,
             xpad, dxpad, a0, a1, c0, c1, dy, dgr_ref):
        j = pl.program_id(1)

        @pl.when(j == 0)
        def _():
            work(xr_ref, gr_ref, du_ref, hpf_ref, hpb_ref, cw_ref, cb_ref, w4_ref, b4_ref, lam_ref,
                 dp_ref, dgr_ref, dcw_ref, dcb_ref, dw4_ref, db4_ref, dlam_ref, xpad, dxpad, a0, a1, c0, c1, dy)

        @pl.when(j == 1)
        def _():
            dp_ref[...] = dgr_ref[...]

    def work(xr_ref, gr_ref, du_ref, hpf_ref, hpb_ref, cw_ref, cb_ref, w4_ref, b4_ref, lam_ref,
             dxr_ref, dgr_ref, dcw_ref, dcb_ref, dw4_ref, db4_ref, dlam_ref, xpad, dxpad, a0, a1, c0, c1, dy):
        _fill_padded(xpad, xr_ref, T)
        dxpad[0:HALO, :] = jnp.zeros((HALO, RB), F32)
        dxpad[HALO + T:2 * HALO + T, :] = jnp.zeros((HALO, RB), F32)
        lam_v = lam_ref[...]
        ls = _log_sigmoid(lam_v)
        w4v, b4v, cwv, cbv = w4_ref[...], b4_ref[...], cw_ref[...], cb_ref[...]

        def conv_chunk(base):
            t = base + lax.broadcasted_iota(jnp.int32, (CH, 1), 0)
            taps = _conv_taps(xpad[pl.ds(base, CH + 2 * HALO), :], t, T)
            return t, taps, cbv + sum(taps[k] * cwv[k:k + 1, :] for k in range(CONV_W))

        def phase_a(ci, _):
            base = pl.multiple_of(ci * CH, CH)
            rows = pl.ds(base, CH)
            _, _, xl = conv_chunk(base)
            (r0, i0, av0, m0), (r1, i1, av1, m1) = _lru_gates(xl, w4v, b4v, ls)
            yv = (av0 * hpf_ref[rows, :] + m0 * (i0 * xl)) + (av1 * hpb_ref[rows, :] + m1 * (i1 * xl))
            gr = gr_ref[rows, :].astype(F32)
            duv = du_ref[rows, :]
            dyv = duv * _gelu(gr)
            dgr_ref[rows, :] = (duv * yv * _gelu_grad(gr)).astype(BF16)
            dy[rows, :] = dyv
            a0[rows, :] = av0
            a1[rows, :] = av1
            c0[rows, :] = av0 * dyv
            c1[rows, :] = av1 * dyv
            return 0

        lax.fori_loop(0, T // CH, phase_a, 0)
        zero = jnp.zeros((1, RB), F32)

        def emit0(rs, hf, before):
            c0[pl.ds(rs, SUB), :] = dy[pl.ds(rs, SUB), :] + before

        def emit1(rs, hf, before):
            c1[pl.ds(rs, SUB), :] = dy[pl.ds(rs, SUB), :] + before

        _scan_rows(a0, c0, 0, T, True, zero, emit0)
        c = _scan_rows(a1, c1, CTX, T - CTX, False, zero, emit1)
        _scan_rows(a1, c1, 0, CTX, False, c, emit1)

        dw4_ref[...] = jnp.zeros(dw4_ref.shape, F32)
        db4_ref[...] = jnp.zeros(db4_ref.shape, F32)
        dlam_ref[...] = jnp.zeros(dlam_ref.shape, F32)
        dcw_ref[...] = jnp.zeros(dcw_ref.shape, F32)
        dcb_ref[...] = jnp.zeros(dcb_ref.shape, F32)

        def phase_c(ci, _):
            base = pl.multiple_of(ci * CH, CH)
            rows = pl.ds(base, CH)
            _, _, xl = conv_chunk(base)
            gates = _lru_gates(xl, w4v, b4v, ls)
            dxl = jnp.zeros((CH, RB), F32)
            dpre_a, dpre_x, dls = [], [], []
            for d, (r, i, a, mult) in enumerate(gates):
                g = (c0, c1)[d][rows, :]
                hp = (hpf_ref, hpb_ref)[d][rows, :]
                dmult = g * (i * xl)
                di = g * mult * xl
                dxl = dxl + g * mult * i
                dla = g * hp * a - (a * a) * dmult / mult
                dr = dla * (LRU_C * ls[d:d + 1, :])
                dls.append(_colsum(dla * (LRU_C * r)))
                dpre_a.append(dr * r * (1.0 - r))
                dpre_x.append(di * i * (1.0 - i))
            dpre = jnp.concatenate(dpre_a + dpre_x, axis=1)
            dpre_b = dpre.astype(BF16)
            dxl = dxl + _dot(dpre_b, w4v, NT)
            dw4_ref[...] += _dot(xl.astype(BF16), dpre_b, TN)
            db4_ref[...] += _colsum(dpre)
            dlam_ref[...] += jnp.concatenate(dls, axis=0)
            dcb_ref[...] += _colsum(dxl)
            dxpad[pl.ds(pl.multiple_of(base + HALO, HALO), CH), :] = dxl
            return 0

        lax.fori_loop(0, T // CH, phase_c, 0)
        dlam_ref[...] = dlam_ref[...] * _sigmoid(-lam_v)

        def phase_d(ci, _):
            base = pl.multiple_of(ci * CH, CH)
            rows = pl.ds(base, CH)
            t, xtaps, _ = conv_chunk(base)
            dtaps = _conv_taps(dxpad[pl.ds(base, CH + 2 * HALO), :], t, T, transpose=True)
            dxl = dxpad[pl.ds(pl.multiple_of(base + HALO, HALO), CH), :]
            dxr_ref[rows, :] = sum(dtaps[k] * cwv[k:k + 1, :] for k in range(CONV_W)).astype(BF16)
            dcw_ref[...] += jnp.concatenate([_colsum(dxl * xtaps[k]) for k in range(CONV_W)], axis=0)
            return 0

        lax.fori_loop(0, T // CH, phase_d, 0)

    sp = _rnn_specs(T)
    dp_spec = pl.BlockSpec((T, RB), lambda n, j: (0, COL_XR // RB + n + j * (COL_GR - COL_XR) // RB))
    return pl.pallas_call(
        kern, name=name, grid=(N_RNN_BLOCKS, 2),
        in_specs=[sp["xr"], sp["gr"], sp["act"], sp["act"], sp["act"], sp["cw"], sp["cb"], sp["w4"], sp["b4"],
                  sp["lam"], ANY],
        out_specs=[dp_spec, sp["cw"], sp["cb"], sp["w4"], sp["b4"], sp["lam"]],
        out_shape=[_sds((T, P_W), BF16), _sds((CONV_W, D), F32), _sds((1, D), F32),
                   _sds((N_RNN_BLOCKS, RB, 4 * RB), F32), _sds((N_RNN_BLOCKS, 1, 4 * RB), F32), _sds((2, D), F32)],
        scratch_shapes=([pltpu.VMEM((T + 2 * HALO, RB), F32)] * 2 + [pltpu.VMEM((T, RB), F32)] * 5
                        + [pltpu.VMEM((T, RB), BF16)]),
        input_output_aliases={10: 0},
        compiler_params=_params(("arbitrary", "arbitrary")),
    )(p, p, du, hpf, hpb, cw, cb, w4, b4, lam, dp)


def _layer_fwd(l, xa, h, W, rope, S, nxt):
    T = xa.shape[0]
    tag = f"l{l}_"
    cos, sin = rope
    p = _mm_act(tag + "proj", h, W["win_t"], "NT", BF16)
    u, hpf, hpb = _rnn_fwd(tag + "rnn_fwd", p, W["cw"], W["cb"], W["w4"], W["b4"], W["lam"], T)
    qa, kp, vp, kc, vc = _qkv_prep(tag + "qkv_prep", p, cos, sin, S)
    o_all = _attn_fwd(tag + "attn_ctx_fwd", qa, kc, vc, W["sink4"], S)
    o_all = _attn_fwd(tag + "attn_lat_fwd", qa, kc, vc, W["sink4"], S, band=(kp, vp), prev=o_all)
    ya = _mm_act(tag + "o_rnn", u, W["wo_rnn"], "NN")
    yb = _mm_act(tag + "o_attn", o_all, W["wo_attn"], "NN")
    z = _gate_fwd(tag + "gate_fwd", p, ya, yb)
    m = _mm_act(tag + "out", z, W["wout"], "NN")
    x1, h2 = _resid_norm_fwd(tag + "mix_resid", xa, m, W["g_mix_post"], W["mod"], GA1, W["g_ffn_pre"], W["mod"], SH2, SC2)
    f = _mm_act(tag + "ffn_in", h2, W["wffn_in_t"], "NT", BF16)
    s = _swiglu_fwd(tag + "swiglu_fwd", f)
    e = _mm_act(tag + "ffn_out", s, W["wffn_out"], "NN")
    saved = dict(xa=xa, h=h, p=p, u=u, hpf=hpf, hpb=hpb, qa=qa, kp=kp, vp=vp, kc=kc, vc=vc, o_all=o_all,
                 ya=ya, yb=yb, z=z, m=m, x1=x1, h2=h2, f=f, s=s, e=e)
    if nxt[0] == "norm":
        out = _resid_norm_fwd(tag + "ffn_resid", x1, e, W["g_ffn_post"], W["mod"], GA2, nxt[1], nxt[2], SH1, SC1)
    else:
        out = _resid_loss_fwd(tag + "ffn_resid_loss", x1, e, W["g_ffn_post"], W["mod"], GA2, nxt[1])
    return saved, out


def _layer_bwd(l, dx2, A, W, rope, S):
    T = dx2.shape[0]
    tag = f"l{l}_"
    cos, sin = rope
    G = {}
    de, dga2, G["g_ffn_post"] = _resid_bwd(tag + "ffn_resid_bwd", dx2, A["e"], W["g_ffn_post"], W["mod"], GA2)
    ds = _mm_act(tag + "ffn_out_dx", de, W["wffn_out"], "NT", BF16)
    G["wffn_out"] = _mm_wgrad(tag + "ffn_out_dw", A["s"], de)
    df = _swiglu_bwd(tag + "swiglu_bwd", A["f"], ds)
    dh2 = _mm_act(tag + "ffn_in_dx", df, W["wffn_in_t"], "NN")
    G["wffn_in_t"] = _mm_wgrad(tag + "ffn_in_dw", df, A["h2"])
    dx1, dm, dsh2, dsc2, G["g_ffn_pre"], dga1, G["g_mix_post"] = _normmod_resid_bwd(
        tag + "mix_resid_bwd", dh2, A["x1"], W["g_ffn_pre"], W["mod"], SH2, SC2, dx2, A["m"], W["g_mix_post"], GA1)
    dz = _mm_act(tag + "out_dx", dm, W["wout"], "NT")
    G["wout"] = _mm_wgrad(tag + "out_dw", A["z"], dm)
    dya, dyb, dp = _gate_bwd(tag + "gate_bwd", A["p"], A["ya"], A["yb"], dz)
    do = _mm_act(tag + "o_attn_dx", dyb, W["wo_attn"], "NT")
    G["wo_attn"] = _mm_wgrad(tag + "o_attn_dw", A["o_all"], dyb)
    du = _mm_act(tag + "o_rnn_dx", dya, W["wo_rnn"], "NT")
    G["wo_rnn"] = _mm_wgrad(tag + "o_rnn_dw", A["u"], dya)
    dq_all, dkc_c, dvc_c, dsink_c = _attn_bwd(tag + "attn_ctx_bwd", A["qa"], A["kc"], A["vc"], W["sink4"],
                                               A["o_all"], do, S)
    dq_all, dkc_l, dvc_l, dsink_l, dkp, dvp = _attn_bwd(tag + "attn_lat_bwd", A["qa"], A["kc"], A["vc"], W["sink4"],
                                                       A["o_all"], do, S, band=(A["kp"], A["vp"]), prev_dq=dq_all)
    G["sink4"] = dsink_c + dsink_l
    dp = _dqkv_assemble(tag + "dqkv", dp, dq_all, dkp, dvp, dkc_l, dvc_l, dkc_c, dvc_c, cos, sin, S)
    dp, G["cw"], G["cb"], G["w4"], G["b4"], G["lam"] = _rnn_bwd(
        tag + "rnn_bwd", A["p"], du, A["hpf"], A["hpb"], dp, W["cw"], W["cb"], W["w4"], W["b4"], W["lam"], T)
    dh = _mm_act(tag + "proj_dx", dp, W["win_t"], "NN")
    G["win_t"] = _mm_wgrad(tag + "proj_dw", dp, A["h"])
    dxa, dsh1, dsc1, G["g_mix_pre"] = _normmod_bwd(tag + "mix_norm_bwd", dh, A["xa"], W["g_mix_pre"], W["mod"],
                                                   SH1, SC1, dx1)
    G["mod"] = jnp.concatenate([dsh1, dsc1, dga1, dsh2, dsc2, dga2], axis=1)
    return dxa, G


def _local_step(xa, target, Ws, S):
    rope = _rope_tables(S)
    L = len(Ws)
    h = _normmod_fwd("l0_mix_norm", xa, Ws[0]["g_mix_pre"], Ws[0]["mod"], SH1, SC1)
    saved = []
    x = xa
    for l in range(L):
        nxt = ("norm", Ws[l + 1]["g_mix_pre"], Ws[l + 1]["mod"]) if l + 1 < L else ("loss", target)
        A, out = _layer_fwd(l, x, h, Ws[l], rope, S, nxt)
        saved.append(A)
        if l + 1 < L:
            x, h = out
    dx, sq = out
    Gs = [None] * L
    for l in reversed(range(L)):
        dx, Gs[l] = _layer_bwd(l, dx, saved[l], Ws[l], rope, S)
    return sq, dx, Gs


MESH = pl.DeviceIdType.MESH


def _place():
    return lax.axis_index("x"), lax.axis_index("y"), lax.axis_index("c")


def _lin(px, py, pc):
    return 4 * px + 2 * py + pc


def _allgather_small(name, blk):
    m, n = blk.shape

    def body(x_ref, out_ref, send_sems, recv_sems, local_sem):
        x, y, c = _place()
        me, sibling = (x, y, c), (x, y, 1 - c)
        chips = [(1 - x, y), (x, 1 - y), (1 - x, 1 - y)]

        def copy(k, block, to, src=None):
            dst = out_ref.at[_lin(*block)]
            return pltpu.make_async_remote_copy(src_ref=dst if src is None else src, dst_ref=dst,
                                                send_sem=send_sems.at[k], recv_sem=recv_sems.at[k],
                                                device_id=to, device_id_type=MESH)

        mine = pltpu.make_async_copy(x_ref, out_ref.at[_lin(*me)], local_sem)
        mine.start()
        first = [copy(0, me, sibling, src=x_ref)]
        first += [copy(1 + j, me, (*chip, c), src=x_ref) for j, chip in enumerate(chips)]
        for cp in first:
            cp.start()
        passed = [copy(4 + j, (*chip, c), sibling) for j, chip in enumerate(chips)]
        for j, chip in enumerate(chips):
            copy(1 + j, (*chip, c), me).wait_recv()
            passed[j].start()
        copy(0, sibling, me).wait_recv()
        for j, chip in enumerate(chips):
            copy(4 + j, (*chip, 1 - c), me).wait_recv()
        for cp in first + passed:
            cp.wait_send()
        mine.wait()

    return pl.pallas_call(
        body, name=name, out_shape=_sds((N_DEV, m, n), blk.dtype),
        in_specs=[pl.BlockSpec(memory_space=pltpu.VMEM)], out_specs=pl.BlockSpec(memory_space=pltpu.VMEM),
        scratch_shapes=[pltpu.SemaphoreType.DMA((7,)), pltpu.SemaphoreType.DMA((7,)), pltpu.SemaphoreType.DMA],
        compiler_params=pltpu.CompilerParams(vmem_limit_bytes=VMEM_LIMIT),
    )(blk)


def _allgather_hbm(name, shards):
    na = len(shards)

    def body(*refs):
        ins, outs = refs[:na], refs[na:2 * na]
        send_sems, recv_sems, local_sems = refs[2 * na:]
        x, y, c = _place()
        me, sibling = (x, y, c), (x, y, 1 - c)
        chips = [(1 - x, y), (x, 1 - y), (1 - x, 1 - y)]

        def copy(a, k, block, to, from_input=False):
            dst = outs[a].at[_lin(*block)]
            return pltpu.make_async_remote_copy(src_ref=ins[a] if from_input else dst, dst_ref=dst,
                                                send_sem=send_sems.at[a, k], recv_sem=recv_sems.at[a, k],
                                                device_id=to, device_id_type=MESH)

        mine = [pltpu.make_async_copy(ins[a], outs[a].at[_lin(*me)], local_sems.at[a]) for a in range(na)]
        for cp in mine:
            cp.start()
        first = []
        for a in range(na):
            first.append(copy(a, 0, me, sibling, True))
            first += [copy(a, 1 + j, me, (*chip, c), True) for j, chip in enumerate(chips)]
        for cp in first:
            cp.start()
        passed = []
        for j, chip in enumerate(chips):
            for a in range(na):
                copy(a, 1 + j, (*chip, c), me).wait_recv()
                fwd = copy(a, 4 + j, (*chip, c), sibling)
                fwd.start()
                passed.append(fwd)
        for a in range(na):
            copy(a, 0, sibling, me).wait_recv()
            for j, chip in enumerate(chips):
                copy(a, 4 + j, (*chip, 1 - c), me).wait_recv()
        for cp in first + passed:
            cp.wait_send()
        for cp in mine:
            cp.wait()

    return pl.pallas_call(
        body, name=name, out_shape=[_sds((N_DEV, *s.shape), s.dtype) for s in shards],
        in_specs=[ANY] * na, out_specs=[ANY] * na,
        scratch_shapes=[pltpu.SemaphoreType.DMA((na, 7)), pltpu.SemaphoreType.DMA((na, 7)),
                        pltpu.SemaphoreType.DMA((na,))],
    )(*shards)


def _exchange_shards(name, grads, L):
    nw = len(grads)
    na = nw * L
    flat = [g for per_layer in grads for g in per_layer]

    def body(*refs):
        ins, outs = refs[:na], refs[na:na + nw]
        send_sems, recv_sems, local_sems = refs[na + nw:]
        x, y, c = _place()
        me = _lin(x, y, c)
        peers = [(x ^ ((k + 1) >> 2 & 1), y ^ ((k + 1) >> 1 & 1), c ^ ((k + 1) & 1)) for k in range(7)]

        def copy(a, k, src_blk, dst_blk):
            return pltpu.make_async_remote_copy(src_ref=ins[a].at[src_blk], dst_ref=outs[a // L].at[a % L, dst_blk],
                                                send_sem=send_sems.at[a, k], recv_sem=recv_sems.at[a, k],
                                                device_id=peers[k], device_id_type=MESH)

        mine = [pltpu.make_async_copy(ins[a].at[me], outs[a // L].at[a % L, me], local_sems.at[a]) for a in range(na)]
        for cp in mine:
            cp.start()
        sent = [copy(a, k, _lin(*peers[k]), me) for a in range(na) for k in range(7)]
        for cp in sent:
            cp.start()
        for a in range(na):
            for k in range(7):
                copy(a, k, me, _lin(*peers[k])).wait_recv()
        for cp in sent:
            cp.wait_send()
        for cp in mine:
            cp.wait()

    return pl.pallas_call(
        body, name=name, out_shape=[_sds((L, *per_layer[0].shape), per_layer[0].dtype) for per_layer in grads],
        in_specs=[ANY] * na, out_specs=[ANY] * nw,
        scratch_shapes=[pltpu.SemaphoreType.DMA((na, 7)), pltpu.SemaphoreType.DMA((na, 7)),
                        pltpu.SemaphoreType.DMA((na,))],
    )(*flat)


MOD_ROWS = 16
MOD_SHARD = 6 * D // N_DEV
HI = lax.Precision.HIGHEST


def _mod_fwd(name, c9, w_mod, b_shard):
    L = w_mod.shape[0]

    def kern(c_ref, w_ref, b_ref, o_ref):
        o_ref[...] = lax.dot_general(_silu(c_ref[...]), w_ref[...], NN, precision=HI,
                                     preferred_element_type=F32) + b_ref[...]

    return pl.pallas_call(
        kern, name=name, grid=(L,),
        in_specs=[_full_spec(c9.shape), pl.BlockSpec((None, D, MOD_SHARD), lambda l: (l, 0, 0)),
                  pl.BlockSpec((None, 1, MOD_SHARD), lambda l: (l, 0, 0))],
        out_specs=pl.BlockSpec((None, MOD_ROWS, MOD_SHARD), lambda l: (l, 0, 0)),
        out_shape=_sds((L, MOD_ROWS, MOD_SHARD), F32), compiler_params=_params(),
    )(c9, w_mod, b_shard)


def _mod_bwd(name, c9, w_mod, dmod_all, dmod_cols):
    L = w_mod.shape[0]

    def rows9(ref, l):
        own = jnp.concatenate([ref[j, 2 * l + 1:2 * l + 2, :] for j in range(N_DEV)], axis=0)
        ctx = ref[0, 2 * l:2 * l + 1, :]
        for j in range(1, N_DEV):
            ctx = ctx + ref[j, 2 * l:2 * l + 1, :]
        return own, ctx

    def kern(c_ref, w_ref, all_ref, cols_ref, gw_ref, gb_ref, gc_ref):
        l = pl.program_id(0)
        for ll in range(L):
            @pl.when(l == ll)
            def _():
                own, ctx = rows9(all_ref, ll)
                gb_ref[...] = _colsum(own) + ctx
                own_s, ctx_s = rows9(cols_ref, ll)
                r16 = jnp.concatenate([own_s, ctx_s, jnp.zeros((MOD_ROWS - N_DEV - 1, MOD_SHARD), F32)], axis=0)
                gw_ref[...] = lax.dot_general(_silu(c_ref[...]), r16, TN, precision=HI, preferred_element_type=F32)
                part = lax.dot_general(r16, w_ref[...], NT, precision=HI,
                                       preferred_element_type=F32)[N_DEV:N_DEV + 1, :]
                if ll == 0:
                    gc_ref[...] = part
                else:
                    gc_ref[...] += part

    return pl.pallas_call(
        kern, name=name, grid=(L,),
        in_specs=[_full_spec(c9.shape), pl.BlockSpec((None, D, MOD_SHARD), lambda l: (l, 0, 0)),
                  _full_spec(dmod_all.shape), _full_spec(dmod_cols.shape)],
        out_specs=[pl.BlockSpec((None, D, MOD_SHARD), lambda l: (l, 0, 0)),
                   pl.BlockSpec((None, 1, 6 * D), lambda l: (l, 0, 0)), _full_spec((1, D))],
        out_shape=[_sds((L, D, MOD_SHARD), F32), _sds((L, 1, 6 * D), F32), _sds((1, D), F32)],
        compiler_params=_params(),
    )(c9, w_mod, dmod_all, dmod_cols)


_BC1 = 1.0 - ADAM_B1 ** ADAM_STEP
_BC2 = 1.0 - ADAM_B2 ** ADAM_STEP


def _adamw_vals(w, g, m, v):
    m = ADAM_B1 * m + (1.0 - ADAM_B1) * g
    v = ADAM_B2 * v + (1.0 - ADAM_B2) * (g * g)
    delta = -ADAM_LR * ((m / _BC1) / (jnp.sqrt(v / _BC2) + ADAM_EPS) + ADAM_WD * w)
    return delta, m, v


def _adamw(name, w, g, m, v, tile):
    R, C = w.shape
    blk = ((tile, C), lambda i: (i, 0))

    def body(i, ins, ps, outs, acc):
        d, mm, vv = _adamw_vals(ins[0][...], ins[1][...], ins[2][...], ins[3][...])
        outs[0][...] = d
        outs[1][...] = mm
        outs[2][...] = vv

    return _ew(name, body, R // tile, [(a, *blk) for a in (w, g, m, v)], [], [(_sds((R, C), F32), *blk)] * 3)


def _sum_slots(ref):
    g = ref[0].astype(F32)
    for j in range(1, N_DEV):
        g = g + ref[j].astype(F32)
    return g


def _adamw_slots(name, slots, w, m, v, tile, dev_major=False):
    L, R, C = w.shape
    spec = pl.BlockSpec((None, tile, C), lambda l, i: (l, i, 0))
    if dev_major:
        slot_spec = pl.BlockSpec((N_DEV, None, tile, C), lambda l, i: (0, l, i, 0))
    else:
        slot_spec = pl.BlockSpec((None, N_DEV, tile, C), lambda l, i: (l, 0, i, 0))

    def kern(s_ref, w_ref, m_ref, v_ref, g_ref, d_ref, mo_ref, vo_ref):
        g = _sum_slots(s_ref)
        g_ref[...] = g
        d_ref[...], mo_ref[...], vo_ref[...] = _adamw_vals(w_ref[...], g, m_ref[...], v_ref[...])

    return pl.pallas_call(
        kern, name=name, grid=(L, R // tile),
        in_specs=[slot_spec, spec, spec, spec],
        out_specs=[spec] * 4, out_shape=[_sds((L, R, C), F32)] * 4,
        compiler_params=_params(("arbitrary", "arbitrary")),
    )(slots, w, m, v)


def _sum_blocks(name, blocks):
    _, R, C = blocks.shape

    def kern(b_ref, o_ref):
        o_ref[...] = _sum_slots(b_ref)

    return pl.pallas_call(kern, name=name, in_specs=[_full_spec(blocks.shape)], out_specs=_full_spec((R, C)),
                          grid=(1,), out_shape=_sds((R, C), F32), compiler_params=_params())(blocks)


BIG = ("win_t", "wo_rnn", "wo_attn", "wout", "wffn_in_t", "wffn_out")
BIG_SRC = ("w_in", "w_o_rnn", "w_o_attn", "w_out", "w_ffn_in", "w_ffn_out")
BIG_T = (True, False, False, False, True, False)
BIG_TILE = (176, 128, 128, 128, 176, 176)


def _chan_full(g8):
    return jnp.transpose(g8, (1, 0, 2)).reshape(g8.shape[1], D)


def kernel(x, c, ctx, c_ctx, w_mod, b_mod, g_mix_pre, g_mix_post, g_ffn_pre, g_ffn_post, w_in, conv_w, conv_b, lru_wa, lru_ba, lru_wx, lru_bx, lru_lam, attn_sink, w_o_rnn, w_o_attn, w_out, w_ffn_in, w_ffn_out, loss_target, m_c_ctx, m_w_mod, m_b_mod, m_g_mix_pre, m_g_mix_post, m_g_ffn_pre, m_g_ffn_post, m_w_in, m_conv_w, m_conv_b, m_lru_wa, m_lru_ba, m_lru_wx, m_lru_bx, m_lru_lam, m_attn_sink, m_w_o_rnn, m_w_o_attn, m_w_out, m_w_ffn_in, m_w_ffn_out, v_c_ctx, v_w_mod, v_b_mod, v_g_mix_pre, v_g_mix_post, v_g_ffn_pre, v_g_ffn_post, v_w_in, v_conv_w, v_conv_b, v_lru_wa, v_lru_ba, v_lru_wx, v_lru_bx, v_lru_lam, v_attn_sink, v_w_o_rnn, v_w_o_attn, v_w_out, v_w_ffn_in, v_w_ffn_out):
    P = dict(c_ctx=c_ctx, w_mod=w_mod, b_mod=b_mod, g_mix_pre=g_mix_pre, g_mix_post=g_mix_post, g_ffn_pre=g_ffn_pre,
             g_ffn_post=g_ffn_post, w_in=w_in, conv_w=conv_w, conv_b=conv_b, lru_wa=lru_wa, lru_ba=lru_ba,
             lru_wx=lru_wx, lru_bx=lru_bx, lru_lam=lru_lam, attn_sink=attn_sink, w_o_rnn=w_o_rnn, w_o_attn=w_o_attn,
             w_out=w_out, w_ffn_in=w_ffn_in, w_ffn_out=w_ffn_out)
    Mo = dict(c_ctx=m_c_ctx, w_mod=m_w_mod, b_mod=m_b_mod, g_mix_pre=m_g_mix_pre, g_mix_post=m_g_mix_post,
              g_ffn_pre=m_g_ffn_pre, g_ffn_post=m_g_ffn_post, w_in=m_w_in, conv_w=m_conv_w, conv_b=m_conv_b,
              lru_wa=m_lru_wa, lru_ba=m_lru_ba, lru_wx=m_lru_wx, lru_bx=m_lru_bx, lru_lam=m_lru_lam,
              attn_sink=m_attn_sink, w_o_rnn=m_w_o_rnn, w_o_attn=m_w_o_attn, w_out=m_w_out, w_ffn_in=m_w_ffn_in,
              w_ffn_out=m_w_ffn_out)
    Vo = dict(c_ctx=v_c_ctx, w_mod=v_w_mod, b_mod=v_b_mod, g_mix_pre=v_g_mix_pre, g_mix_post=v_g_mix_post,
              g_ffn_pre=v_g_ffn_pre, g_ffn_post=v_g_ffn_post, w_in=v_w_in, conv_w=v_conv_w, conv_b=v_conv_b,
              lru_wa=v_lru_wa, lru_ba=v_lru_ba, lru_wx=v_lru_wx, lru_bx=v_lru_bx, lru_lam=v_lru_lam,
              attn_sink=v_attn_sink, w_o_rnn=v_w_o_rnn, w_o_attn=v_w_o_attn, w_out=v_w_out, w_ffn_in=v_w_ffn_in,
              w_ffn_out=v_w_ffn_out)
    L = w_in.shape[0]
    S = x.shape[1]
    me = _lin(*_place())

    small = jnp.concatenate([c.reshape(8, 128), conv_w.reshape(L * CONV_W, 128), lru_ba.reshape(2 * L, 128),
                             lru_bx.reshape(2 * L, 128), lru_lam.reshape(2 * L, 128), jnp.zeros((4, 128), F32)], axis=0)
    small_all = _allgather_small("ag_small", small)
    c_all = small_all[:, 0:8].reshape(N_DEV, D)
    conv_w_f = _chan_full(small_all[:, 8:16]).reshape(L, CONV_W, D)
    lru_ba_f = _chan_full(small_all[:, 16:20]).reshape(L, 2, D)
    lru_bx_f = _chan_full(small_all[:, 20:24]).reshape(L, 2, D)
    lru_lam_f = _chan_full(small_all[:, 24:28]).reshape(L, 2, D)

    c9 = jnp.concatenate([c_all, c_ctx[None], jnp.zeros((MOD_ROWS - N_DEV - 1, D), F32)], axis=0)
    b_shard = lax.dynamic_slice_in_dim(b_mod, me * MOD_SHARD, MOD_SHARD, axis=1)[:, None, :]
    mod_part = _mod_fwd("mod_fwd", c9, w_mod, b_shard)
    mod_all = _allgather_small("ag_mod", mod_part.reshape(L * MOD_ROWS, MOD_SHARD))
    mod_all = jnp.transpose(mod_all.reshape(N_DEV, L, MOD_ROWS, MOD_SHARD), (1, 2, 0, 3)).reshape(L, MOD_ROWS, 6 * D)
    own_row = lax.dynamic_index_in_dim(mod_all, me, axis=1, keepdims=False)
    modrows = jnp.stack([mod_all[:, N_DEV], own_row], axis=1)

    shards = []
    for l in range(L):
        for src, tr in zip(BIG_SRC, BIG_T):
            w = P[src][l]
            shards.append((w.T if tr else w).astype(BF16))
    full = _allgather_hbm("ag_weights", shards)
    Ws = []
    for l in range(L):
        W = {k: full[l * len(BIG) + i].reshape(-1, D) for i, k in enumerate(BIG)}
        W.update(
            cw=conv_w_f[l], cb=conv_b[l][None],
            w4=jnp.concatenate([lru_wa[l, 0], lru_wa[l, 1], lru_wx[l, 0], lru_wx[l, 1]], axis=-1).astype(BF16),
            b4=jnp.concatenate([lru_ba_f[l, 0].reshape(N_RNN_BLOCKS, 1, RB), lru_ba_f[l, 1].reshape(N_RNN_BLOCKS, 1, RB),
                                lru_bx_f[l, 0].reshape(N_RNN_BLOCKS, 1, RB), lru_bx_f[l, 1].reshape(N_RNN_BLOCKS, 1, RB)],
                               axis=-1),
            lam=lru_lam_f[l], sink4=jnp.broadcast_to(attn_sink[l].reshape(N_KV, Q_PER_KV, 1), (N_KV, Q_PER_KV, HEAD)),
            g_mix_pre=g_mix_pre[l][None], g_mix_post=g_mix_post[l][None], g_ffn_pre=g_ffn_pre[l][None],
            g_ffn_post=g_ffn_post[l][None], mod=modrows[l])
        Ws.append(W)

    xa = jnp.concatenate([ctx[0], x[0]], axis=0)
    sq, dxa, Gs = _local_step(xa, loss_target[0], Ws, S)
    loss = lax.psum((0.5 / D) * jnp.sum(sq), ("x", "y", "c"))
    grad_x = dxa[CTX:][None]

    dmod = jnp.concatenate([Gs[l]["mod"] for l in range(L)] + [jnp.zeros((8 - 2 * L, 6 * D), F32)], axis=0)
    dmod_all = _allgather_small("ag_dmod", dmod)
    dmod_cols = lax.dynamic_slice_in_dim(dmod_all, me * MOD_SHARD, MOD_SHARD, axis=2)
    g_w_mod, g_b_mod, dsc_part = _mod_bwd("mod_bwd", c9, w_mod, dmod_all, dmod_cols)
    g_b_mod = g_b_mod[:, 0]

    def rows(name, shape):
        return jnp.concatenate([Gs[l][name].reshape(shape) for l in range(L)], axis=0)

    b4g = [Gs[l]["b4"].reshape(N_RNN_BLOCKS, 4, RB) for l in range(L)]
    sink_row = jnp.concatenate([Gs[l]["sink4"][:, :, 0].reshape(1, N_Q) for l in range(L)]
                               + [jnp.zeros((1, D - L * N_Q), F32)], axis=1)
    small_g = jnp.concatenate(
        [rows("g_mix_pre", (1, D)), rows("g_mix_post", (1, D)), rows("g_ffn_pre", (1, D)), rows("g_ffn_post", (1, D)),
         rows("cb", (1, D)), rows("cw", (CONV_W, D))]
        + [b4g[l][:, d].reshape(1, D) for l in range(L) for d in range(2)]
        + [b4g[l][:, 2 + d].reshape(1, D) for l in range(L) for d in range(2)]
        + [rows("lam", (2, D)), sink_row, dsc_part], axis=0)
    n_small = small_g.shape[0]
    small_tot = _sum_blocks("sum_small", _allgather_small("ag_small_grads", small_g))
    o = 0
    G = {}
    for name in ("g_mix_pre", "g_mix_post", "g_ffn_pre", "g_ffn_post", "conv_b"):
        G[name] = small_tot[o:o + L]
        o += L
    G["conv_w"] = small_tot[o:o + L * CONV_W].reshape(L, CONV_W, D)
    o += L * CONV_W
    for name in ("lru_ba", "lru_bx", "lru_lam"):
        G[name] = small_tot[o:o + 2 * L].reshape(L, 2, D)
        o += 2 * L
    G["attn_sink"] = small_tot[o, :L * N_Q].reshape(L, N_Q)
    sg = jax.nn.sigmoid(c_ctx)
    G["c_ctx"] = small_tot[o + 1] * (sg * (1.0 + c_ctx * (1.0 - sg)))
    G["b_mod"] = g_b_mod
    G["w_mod"] = g_w_mod

    w4g = jnp.concatenate([Gs[l]["w4"].reshape(N_RNN_BLOCKS * RB, 4 * RB) for l in range(L)], axis=0).astype(BF16)
    w4_slots, = _allgather_hbm("ag_gate_grads", [w4g])

    big_slots = _exchange_shards("exchange_grads", [[Gs[l][k].reshape(N_DEV, -1, D) for l in range(L)] for k in BIG], L)

    out_g, out_d, out_m, out_v = {}, {}, {}, {}

    def put(name, res, shape=None):
        g, d, m, v = res
        for dst, val in ((out_g, g), (out_d, d), (out_m, m), (out_v, v)):
            dst[name] = val if shape is None else val.reshape(shape)

    for k, src, tr, tile, slots in zip(BIG, BIG_SRC, BIG_T, BIG_TILE, big_slots):
        lay = (lambda a: jnp.swapaxes(a, 1, 2)) if tr else (lambda a: a)
        res = _adamw_slots("adamw_" + src, slots, lay(P[src]), lay(Mo[src]), lay(Vo[src]), tile)
        put(src, [lay(r) for r in res])
    res = _adamw("adamw_w_mod", w_mod.reshape(L * D, MOD_SHARD), g_w_mod.reshape(L * D, MOD_SHARD),
                 m_w_mod.reshape(L * D, MOD_SHARD), v_w_mod.reshape(L * D, MOD_SHARD), 256)
    put("w_mod", (g_w_mod,) + tuple(res), w_mod.shape)
    def fuse4(wa, wx):
        return jnp.concatenate([wa[:, 0], wa[:, 1], wx[:, 0], wx[:, 1]], axis=-1).reshape(L, N_RNN_BLOCKS * RB, 4 * RB)

    res = _adamw_slots("adamw_gates", w4_slots.reshape(N_DEV, L, N_RNN_BLOCKS * RB, 4 * RB),
                       fuse4(lru_wa, lru_wx), fuse4(m_lru_wa, m_lru_wx), fuse4(v_lru_wa, v_lru_wx), 256, dev_major=True)
    res = [r.reshape(L, N_RNN_BLOCKS, RB, 4, RB) for r in res]
    put("lru_wa", [jnp.stack([r[:, :, :, 0], r[:, :, :, 1]], axis=1) for r in res])
    put("lru_wx", [jnp.stack([r[:, :, :, 2], r[:, :, :, 3]], axis=1) for r in res])
    rep = ("g_mix_pre", "g_mix_post", "g_ffn_pre", "g_ffn_post", "conv_b", "b_mod")

    def pack_rep(T_):
        sink = jnp.concatenate([T_["attn_sink"].reshape(1, L * N_Q), jnp.zeros((1, D - L * N_Q), F32)], axis=1)
        return jnp.concatenate([T_[n].reshape(-1, D) for n in rep] + [sink, T_["c_ctx"][None]], axis=0)

    pk = [pack_rep(T_) for T_ in (P, G, Mo, Vo)]
    n_rep = pk[0].shape[0]
    res = _adamw("adamw_replicated", *[jnp.pad(a, ((0, 24 - n_rep), (0, 0))) for a in pk], 24)
    res = (pk[1],) + tuple(r[:n_rep] for r in res)
    o = 0
    for n in rep:
        k = P[n].size // D
        put(n, [r[o:o + k] for r in res], P[n].shape)
        o += k
    put("attn_sink", [r[o, :L * N_Q] for r in res], attn_sink.shape)
    put("c_ctx", [r[o + 1] for r in res], c_ctx.shape)
    chan = ("conv_w", "lru_ba", "lru_bx", "lru_lam")
    g_own = {n: lax.dynamic_slice_in_dim(G[n], me * RB, RB, axis=2) for n in chan}

    def pack_chan(T_):
        return jnp.concatenate([T_[n].reshape(-1, RB) for n in chan], axis=0)

    pk = [pack_chan(T_) for T_ in (P, g_own, Mo, Vo)]
    n_ch = pk[0].shape[0]
    res = _adamw("adamw_channels", *[jnp.pad(a, ((0, 24 - n_ch), (0, 0))) for a in pk], 24)
    res = (pk[1],) + tuple(r[:n_ch] for r in res)
    o = 0
    for n in chan:
        k = P[n].size // RB
        put(n, [r[o:o + k] for r in res], P[n].shape)
        o += k

    order = ("c_ctx", "w_mod", "b_mod", "g_mix_pre", "g_mix_post", "g_ffn_pre", "g_ffn_post", "w_in", "conv_w", "conv_b",
             "lru_wa", "lru_ba", "lru_wx", "lru_bx", "lru_lam", "attn_sink", "w_o_rnn", "w_o_attn", "w_out", "w_ffn_in",
             "w_ffn_out")
    return (loss, grad_x, *[out_g[n] for n in order], *[out_d[n] for n in order], *[out_m[n] for n in order],
            *[out_v[n] for n in order])
```

```python
import functools
import math

import numpy as np
import jax
import jax.numpy as jnp
from jax import lax
from jax.experimental import pallas as pl
from jax.experimental.pallas import tpu as pltpu

F32 = jnp.float32
BF16 = jnp.bfloat16

D = 1024
CTX = 256
TR = 256
HEAD = 128
N_Q = 8
N_KV = 2
Q_PER_KV = N_Q // N_KV
GRID_W = 64
N_FREQ = HEAD // 4
ROPE_BASE = 10000.0
N_RNN_BLOCKS = 8
CONV_W = 4
CONV_LEFT = 2
LRU_C = 8.0
D_FF = 2816
IN_W = 5632
P_W = IN_W
COL_XR, COL_GR, COL_Q, COL_K, COL_V, COL_GL = 0, 1024, 2048, 3072, 3328, 3584
GLB = 512
EPS = 1e-6
NEG_INF = -1e30
ATT_SCALE = HEAD ** -0.5
N_DEV = 8
VMEM_LIMIT = 56 * 1024 * 1024

ADAM_LR, ADAM_B1, ADAM_B2, ADAM_EPS, ADAM_WD, ADAM_STEP = 0.001, 0.9, 0.999, 1e-08, 0.01, 10

NN = (((1,), (0,)), ((), ()))
NT = (((1,), (1,)), ((), ()))
TN = (((0,), (0,)), ((), ()))


def _dot(a, b, dims=NN):
    return lax.dot_general(a, b, dims, preferred_element_type=F32)


def _params(sem=("arbitrary",)):
    return pltpu.CompilerParams(dimension_semantics=sem, vmem_limit_bytes=VMEM_LIMIT)


def _full_spec(shape):
    nd = len(shape)
    return pl.BlockSpec(shape, lambda *_: (0,) * nd)


ANY = pl.BlockSpec(memory_space=pl.ANY)


def _ew(name, body, n, row_ins, pars, row_outs, accs=(), alias=None):
    n_ri, n_p, n_ro, n_acc = len(row_ins), len(pars), len(row_outs), len(accs)

    def kern(*refs):
        i = pl.program_id(0)
        ins = refs[:n_ri]
        ps = refs[n_ri:n_ri + n_p]
        outs = refs[n_ri + n_p:n_ri + n_p + n_ro]
        acc = refs[n_ri + n_p + n_ro:]
        if n_acc:
            @pl.when(i == 0)
            def _():
                for a in acc:
                    a[...] = jnp.zeros(a.shape, a.dtype)
        body(i, ins, ps, outs, acc)

    in_specs = [ANY if blk is None else pl.BlockSpec(blk, imap) for (_, blk, imap) in row_ins]
    in_specs += [_full_spec(p.shape) for p in pars]
    out_specs = [pl.BlockSpec(blk, imap) for (_, blk, imap) in row_outs] + [_full_spec(a.shape) for a in accs]
    out_shape = [s for (s, _, _) in row_outs] + list(accs)
    return pl.pallas_call(
        kern, name=name, grid=(n,), in_specs=in_specs, out_specs=out_specs, out_shape=out_shape,
        input_output_aliases=alias or {}, compiler_params=_params(),
    )(*[a for (a, _, _) in row_ins], *pars)


def _rowblk(width, colblk=0, roff=0, tile=TR):
    return (tile, width), (lambda i: (i + roff, colblk))


def _sds(shape, dtype):
    return jax.ShapeDtypeStruct(shape, dtype)


class _Carry:
    def __init__(self, jobs):
        self.jobs = list(jobs)
        self.arrays = [a for _, a in self.jobs]
        self.out_shapes = [_sds(a.shape if kind == "scatter" else (N_DEV, *a.shape), a.dtype) for kind, a in self.jobs]
        n = len(self.jobs)
        self.scratch = [pltpu.SemaphoreType.DMA((n, 7)), pltpu.SemaphoreType.DMA((n, 7)), pltpu.SemaphoreType.DMA((n,))]

    def _copies(self, ins, outs, sems):
        send_sems, recv_sems, local_sems = sems
        x, y, c = _place()
        me = _lin(x, y, c)
        peers = [(x ^ ((k + 1) >> 2 & 1), y ^ ((k + 1) >> 1 & 1), c ^ ((k + 1) & 1)) for k in range(7)]
        sends, recvs, local = [], [], []
        for a, (kind, _) in enumerate(self.jobs):
            mine = ins[a].at[me] if kind == "scatter" else ins[a]
            local.append(pltpu.make_async_copy(mine, outs[a].at[me], local_sems.at[a]))
            for k in range(7):
                theirs = _lin(*peers[k])
                src = ins[a].at[theirs] if kind == "scatter" else ins[a]
                common = dict(send_sem=send_sems.at[a, k], recv_sem=recv_sems.at[a, k], device_id=peers[k],
                              device_id_type=MESH)
                sends.append(pltpu.make_async_remote_copy(src_ref=src, dst_ref=outs[a].at[me], **common))
                recvs.append(pltpu.make_async_remote_copy(src_ref=src, dst_ref=outs[a].at[theirs], **common))
        return sends, recvs, local

    def start(self, ins, outs, sems):
        sends, _, local = self._copies(ins, outs, sems)
        for cp in local + sends:
            cp.start()

    def wait(self, ins, outs, sems):
        sends, recvs, local = self._copies(ins, outs, sems)
        for cp in recvs:
            cp.wait_recv()
        for cp in sends:
            cp.wait_send()
        for cp in local:
            cp.wait()


def _carried(kern, carry, n_in, n_out, first, last):
    if carry is None:
        return kern
    nc = len(carry.jobs)

    def wrapped(*refs):
        ins, cin = refs[:n_in], refs[n_in:n_in + nc]
        outs, cout = refs[n_in + nc:n_in + nc + n_out], refs[n_in + nc + n_out:n_in + 2 * nc + n_out]
        scr, sems = refs[n_in + 2 * nc + n_out:-3], refs[-3:]

        @pl.when(first())
        def _():
            carry.start(cin, cout, sems)

        kern(*ins, *outs, *scr)

        @pl.when(last())
        def _():
            carry.wait(cin, cout, sems)

    return wrapped


def _carry_args(carry):
    if carry is None:
        return [], [], [], [], []
    n = len(carry.jobs)
    return [ANY] * n, carry.arrays, [ANY] * n, carry.out_shapes, carry.scratch


def _grid_ends(dims):
    first = lambda: functools.reduce(jnp.logical_and, [pl.program_id(d) == 0 for d in range(len(dims))])
    last = lambda: functools.reduce(jnp.logical_and, [pl.program_id(d) == n - 1 for d, n in enumerate(dims)])
    return first, last


def _mm_call(name, a, b, mode, out_dtype, tm, tn, rows_outer=True, single_b=False, carry=None):
    if mode == "TN":
        (K, M), N = a.shape, b.shape[1]
    else:
        (M, K), N = a.shape, (b.shape[1] if mode == "NN" else b.shape[0])
    assert M % tm == 0 and N % tn == 0, (name, M, N, K, tm, tn)
    ij = (lambda g0, g1: (g0, g1)) if rows_outer else (lambda g0, g1: (g1, g0))
    grid = (M // tm, N // tn) if rows_outer else (N // tn, M // tm)
    if mode == "TN":
        a_spec = pl.BlockSpec((K, tm), lambda g0, g1: (0, ij(g0, g1)[0]))
    else:
        a_spec = pl.BlockSpec((tm, K), lambda g0, g1: (ij(g0, g1)[0], 0))
    b_blk, b_map = ((tn, K), lambda g0, g1: (ij(g0, g1)[1], 0)) if mode == "NT" else \
                   ((K, tn), lambda g0, g1: (0, ij(g0, g1)[1]))
    b_spec = pl.BlockSpec(b_blk, b_map, pipeline_mode=pl.Buffered(1)) if single_b else pl.BlockSpec(b_blk, b_map)
    dims = {"NN": NN, "NT": NT, "TN": TN}[mode]

    def kern(a_ref, b_ref, o_ref):
        o_ref[...] = _dot(a_ref[...], b_ref[...], dims).astype(o_ref.dtype)

    ci, ca, co, cs, cscr = _carry_args(carry)
    res = pl.pallas_call(
        _carried(kern, carry, 2, 1, *_grid_ends(grid)), name=name, grid=grid, in_specs=[a_spec, b_spec] + ci,
        out_specs=[pl.BlockSpec((tm, tn), lambda g0, g1: ij(g0, g1))] + co,
        out_shape=[_sds((M, N), out_dtype)] + cs, scratch_shapes=cscr,
        compiler_params=_params(("arbitrary", "arbitrary")),
    )(a, b, *ca)
    return res[0] if carry is None else (res[0], res[1:])


def _mm_act(name, a, w, mode, out_dtype=F32, carry=None):
    rows, K = a.shape
    N = w.shape[1] if mode == "NN" else w.shape[0]
    if K > D_FF:
        return _mm_call(name, a, w, mode, out_dtype, rows // 8, N, single_b=True, carry=carry)
    tn = N if N <= 1024 else 1408
    return _mm_call(name, a, w, mode, out_dtype, rows // 4, tn, carry=carry)


def _mm_wgrad(name, x, dy, out_dtype=BF16, carry=None):
    M = x.shape[1]
    tm = 1408 if M == D_FF else 512
    return _mm_call(name, x, dy, "TN", out_dtype, tm, dy.shape[1], single_b=True, carry=carry)


def _sigmoid(x):
    return 0.5 * jnp.tanh(0.5 * x) + 0.5


def _silu(x):
    return x * _sigmoid(x)


def _silu_grad(x):
    s = _sigmoid(x)
    return s * (1.0 + x * (1.0 - s))


_GELU_K = math.sqrt(2.0 / math.pi)


def _gelu(x):
    return 0.5 * x * (1.0 + jnp.tanh(_GELU_K * (x + 0.044715 * x * x * x)))


def _gelu_grad(x):
    t = jnp.tanh(_GELU_K * (x + 0.044715 * x * x * x))
    return 0.5 * (1.0 + t) + 0.5 * x * (1.0 - t * t) * _GELU_K * (1.0 + 3.0 * 0.044715 * x * x)


def _log_sigmoid(x):
    return jnp.minimum(x, 0.0) - jnp.log(1.0 + jnp.exp(-jnp.abs(x)))


def _neg_expm1(x):
    series = -x * (1.0 + x * (0.5 + x * (1.0 / 6.0 + x * (1.0 / 24.0))))
    return jnp.where(x > -0.03, series, 1.0 - jnp.exp(x))


def _rms(x):
    r = lax.rsqrt(jnp.mean(x * x, axis=-1, keepdims=True) + EPS)
    return x * r, r


def _rms_bwd(dy, y, r):
    return r * (dy - y * jnp.mean(dy * y, axis=-1, keepdims=True))


def _modrow(mod_ref, i, chunk):
    lo = mod_ref[0:1, chunk * D:(chunk + 1) * D]
    hi = mod_ref[1:2, chunk * D:(chunk + 1) * D]
    return jnp.where(i == 0, lo, hi)


def _acc_seg(acc_ref, i, val):
    zero = jnp.zeros_like(val)
    acc_ref[0:1, :] += jnp.where(i == 0, val, zero)
    acc_ref[1:2, :] += jnp.where(i == 0, zero, val)


def _colsum(x):
    return jnp.sum(x, axis=0, keepdims=True)


SH1, SC1, GA1, SH2, SC2, GA2 = range(6)


def _normmod_fwd(name, xa, g, mod, c_sh, c_sc):
    T = xa.shape[0]

    def body(i, ins, ps, outs, acc):
        y, _ = _rms(ins[0][...])
        h = (y * ps[0][...]) * (1.0 + _modrow(ps[1], i, c_sc)) + _modrow(ps[1], i, c_sh)
        outs[0][...] = h.astype(BF16)

    return _ew(name, body, T // TR, [(xa, *_rowblk(D))], [g, mod], [(_sds((T, D), BF16), *_rowblk(D))])[0]


def _resid_norm_fwd(name, xin, mat, gpost, mod, c_ga, gnext, modn, c_sh, c_sc):
    T = xin.shape[0]

    def body(i, ins, ps, outs, acc):
        ym, _ = _rms(ins[1][...])
        xo = ins[0][...] + _modrow(ps[1], i, c_ga) * (ym * ps[0][...])
        outs[0][...] = xo
        y, _ = _rms(xo)
        h = (y * ps[2][...]) * (1.0 + _modrow(ps[3], i, c_sc)) + _modrow(ps[3], i, c_sh)
        outs[1][...] = h.astype(BF16)

    return _ew(name, body, T // TR, [(xin, *_rowblk(D)), (mat, *_rowblk(D))], [gpost, mod, gnext, modn],
               [(_sds((T, D), F32), *_rowblk(D)), (_sds((T, D), BF16), *_rowblk(D))])


def _resid_loss_fwd(name, xin, mat, gpost, mod, c_ga, target):
    T = xin.shape[0]

    def body(i, ins, ps, outs, acc):
        ym, _ = _rms(ins[1][...])
        xo = ins[0][...] + _modrow(ps[1], i, c_ga) * (ym * ps[0][...])
        err = xo - ins[2][...]
        lat = i > 0
        outs[0][...] = jnp.where(lat, err * (1.0 / D), 0.0)
        acc[0][...] += jnp.where(lat, _colsum(err * err), 0.0)

    tgt_blk = ((TR, D), lambda i: (jnp.maximum(i - 1, 0), 0))
    dx, sq = _ew(name, body, T // TR, [(xin, *_rowblk(D)), (mat, *_rowblk(D)), (target, *tgt_blk)], [gpost, mod],
                 [(_sds((T, D), F32), *_rowblk(D))], [_sds((1, D), F32)])
    return dx, sq


def _resid_bwd_vals(i, dout, mat, gpost, mod_ref, c_ga, acc_ga, acc_g):
    ym, rm = _rms(mat)
    ga = _modrow(mod_ref, i, c_ga)
    _acc_seg(acc_ga, i, _colsum(dout * (ym * gpost)))
    dn = dout * ga
    acc_g[...] += _colsum(dn * ym)
    return _rms_bwd(dn * gpost, ym, rm)


def _normmod_bwd_vals(i, dh, xin, g, mod_ref, c_sh, c_sc, acc_sh, acc_sc, acc_g):
    y, r = _rms(xin)
    _acc_seg(acc_sc, i, _colsum(dh * (y * g)))
    _acc_seg(acc_sh, i, _colsum(dh))
    dyg = dh * (1.0 + _modrow(mod_ref, i, c_sc))
    acc_g[...] += _colsum(dyg * y)
    return _rms_bwd(dyg * g, y, r)


def _resid_bwd(name, dout, mat, gpost, mod, c_ga):
    T = dout.shape[0]

    def body(i, ins, ps, outs, acc):
        dm = _resid_bwd_vals(i, ins[0][...], ins[1][...], ps[0][...], ps[1], c_ga, acc[0], acc[1])
        outs[0][...] = dm.astype(BF16)

    return _ew(name, body, T // TR, [(dout, *_rowblk(D)), (mat, *_rowblk(D))], [gpost, mod],
               [(_sds((T, D), BF16), *_rowblk(D))], [_sds((2, D), F32), _sds((1, D), F32)])


def _normmod_resid_bwd(name, dh, xin, gpre, mod, c_sh, c_sc, dres, mat, gpost, c_ga):
    T = dh.shape[0]

    def body(i, ins, ps, outs, acc):
        dx = ins[2][...] + _normmod_bwd_vals(i, ins[0][...], ins[1][...], ps[0][...], ps[1], c_sh, c_sc,
                                             acc[0], acc[1], acc[2])
        outs[0][...] = dx
        dm = _resid_bwd_vals(i, dx, ins[3][...], ps[2][...], ps[1], c_ga, acc[3], acc[4])
        outs[1][...] = dm.astype(BF16)

    return _ew(name, body, T // TR, [(dh, *_rowblk(D)), (xin, *_rowblk(D)), (dres, *_rowblk(D)), (mat, *_rowblk(D))],
               [gpre, mod, gpost],
               [(_sds((T, D), F32), *_rowblk(D)), (_sds((T, D), BF16), *_rowblk(D))],
               [_sds((2, D), F32), _sds((2, D), F32), _sds((1, D), F32), _sds((2, D), F32), _sds((1, D), F32)])


def _normmod_bwd(name, dh, xin, gpre, mod, c_sh, c_sc, dres):
    T = dh.shape[0]

    def body(i, ins, ps, outs, acc):
        outs[0][...] = ins[2][...] + _normmod_bwd_vals(i, ins[0][...], ins[1][...], ps[0][...], ps[1], c_sh, c_sc,
                                                       acc[0], acc[1], acc[2])

    return _ew(name, body, T // TR, [(dh, *_rowblk(D)), (xin, *_rowblk(D)), (dres, *_rowblk(D))], [gpre, mod],
               [(_sds((T, D), F32), *_rowblk(D))], [_sds((2, D), F32), _sds((2, D), F32), _sds((1, D), F32)])


def _gate_fwd(name, p, ya, yb):
    T = ya.shape[0]

    def body(i, ins, ps, outs, acc):
        gl = [r[...].astype(F32) for r in ins[:4]]
        ga = _sigmoid(jnp.concatenate(gl[:2], axis=1))
        gb = _sigmoid(jnp.concatenate(gl[2:], axis=1))
        outs[0][...] = (ga * ins[4][...] + gb * ins[5][...]).astype(BF16)

    return _ew(name, body, T // TR,
               [(p, *_rowblk(GLB, COL_GL // GLB + q)) for q in range(4)] + [(ya, *_rowblk(D)), (yb, *_rowblk(D))],
               [], [(_sds((T, D), BF16), *_rowblk(D))])[0]


def _gate_bwd(name, p, ya, yb, dz):
    T = ya.shape[0]

    def kern(gl_ref, ya_ref, yb_ref, dz_ref, dya_ref, dyb_ref, dp_ref):
        j = pl.program_id(1)
        g = _sigmoid(gl_ref[...].astype(F32))
        dzv = dz_ref[...]
        dbranch = (dzv * g).astype(BF16)
        dg = dzv * g * (1.0 - g)

        @pl.when(j < 2)
        def _():
            dya_ref[...] = dbranch
            dp_ref[...] = (dg * ya_ref[...]).astype(BF16)

        @pl.when(j >= 2)
        def _():
            dyb_ref[...] = dbranch
            dp_ref[...] = (dg * yb_ref[...]).astype(BF16)

    half = pl.BlockSpec((TR, GLB), lambda i, j: (i, j % 2))
    return pl.pallas_call(
        kern, name=name, grid=(T // TR, 4),
        in_specs=[pl.BlockSpec((TR, GLB), lambda i, j: (i, COL_GL // GLB + j)), half, half, half],
        out_specs=[pl.BlockSpec((TR, GLB), lambda i, j: (i, jnp.minimum(j, 1))),
                   pl.BlockSpec((TR, GLB), lambda i, j: (i, jnp.maximum(j - 2, 0))),
                   pl.BlockSpec((TR, GLB), lambda i, j: (i, COL_GL // GLB + j))],
        out_shape=[_sds((T, D), BF16), _sds((T, D), BF16), _sds((T, P_W), BF16)],
        compiler_params=_params(("arbitrary", "arbitrary")),
    )(p, ya, yb, dz)


def _swiglu_fwd(name, f):
    T = f.shape[0]

    def body(i, ins, ps, outs, acc):
        outs[0][...] = (_silu(ins[0][...].astype(F32)) * ins[1][...].astype(F32)).astype(BF16)

    return _ew(name, body, T // TR, [(f, *_rowblk(D_FF, 0)), (f, *_rowblk(D_FF, 1))], [],
               [(_sds((T, D_FF), BF16), *_rowblk(D_FF))])[0]


def _swiglu_bwd(name, f, ds):
    T = f.shape[0]

    def body(i, ins, ps, outs, acc):
        gate, up, dsv = ins[0][...].astype(F32), ins[1][...].astype(F32), ins[2][...].astype(F32)
        dgate = dsv * up * _silu_grad(gate)
        dup = dsv * _silu(gate)
        outs[0][...] = jnp.concatenate([dgate, dup], axis=1).astype(BF16)

    return _ew(name, body, T // TR, [(f, *_rowblk(D_FF, 0)), (f, *_rowblk(D_FF, 1)), (ds, *_rowblk(D_FF))], [],
               [(_sds((T, 2 * D_FF), BF16), *_rowblk(2 * D_FF))])[0]


AB = 128
CTX_BLKS = CTX // AB


def _rope_tables(S):
    pos = jnp.arange(S, dtype=jnp.int32)
    inv = ROPE_BASE ** (-jnp.arange(N_FREQ, dtype=F32) / N_FREQ)
    ang_r = (pos // GRID_W).astype(F32)[:, None] * inv[None, :]
    ang_c = (pos % GRID_W).astype(F32)[:, None] * inv[None, :]
    cos = jnp.concatenate([jnp.cos(ang_r)] * 2 + [jnp.cos(ang_c)] * 2, axis=1)
    sin = jnp.concatenate([-jnp.sin(ang_r), jnp.sin(ang_r), -jnp.sin(ang_c), jnp.sin(ang_c)], axis=1)
    return cos, sin


def _rope(x, cos, sin):
    w = x.shape[1]
    reps = w // HEAD
    lane = lax.broadcasted_iota(jnp.int32, x.shape, 1)
    partner = jnp.where((lane & 63) < 32, pltpu.roll(x, w - 32, 1), pltpu.roll(x, 32, 1))
    return x * jnp.tile(cos, (1, reps)) + partner * jnp.tile(sin, (1, reps))


def _unrope(dx, cos, sin):
    w = dx.shape[1]
    reps = w // HEAD
    lane = lax.broadcasted_iota(jnp.int32, dx.shape, 1)
    t = dx * jnp.tile(sin, (1, reps))
    partner = jnp.where((lane & 63) < 32, pltpu.roll(t, w - 32, 1), pltpu.roll(t, 32, 1))
    return dx * jnp.tile(cos, (1, reps)) + partner


def _qkv_prep(name, p, cos, sin, S):
    T = CTX + S
    nb = S // AB
    KW = N_KV * HEAD

    def kern(q_ref, k_ref, v_ref, cos_ref, sin_ref, qa_ref, kp_ref, vp_ref, kc_ref, vc_ref):
        i = pl.program_id(0)
        cos_v, sin_v = cos_ref[...], sin_ref[...]
        @pl.when(i < CTX_BLKS)
        def _():
            qa_ref[...] = q_ref[...]
            kc_ref[...] = k_ref[...]
            vc_ref[...] = v_ref[...]
            kp_ref[...] = jnp.zeros(kp_ref.shape, BF16)
            vp_ref[...] = jnp.zeros(vp_ref.shape, BF16)

        @pl.when(i >= CTX_BLKS)
        def _():
            qa_ref[...] = _rope(q_ref[...].astype(F32), cos_v, sin_v).astype(BF16)
            kp_ref[...] = _rope(k_ref[...].astype(F32), cos_v, sin_v).astype(BF16)
            vp_ref[...] = v_ref[...]

    lat_map = lambda i: (jnp.maximum(i - CTX_BLKS, 0), 0)
    pad_map = lambda i: (jnp.where(i == 0, 0, jnp.where(i == 1, nb + 1, i - 1)), 0)
    ctx_map = lambda i: (jnp.minimum(i, CTX_BLKS - 1), 0)
    return pl.pallas_call(
        kern, name=name, grid=(T // AB,),
        in_specs=[pl.BlockSpec((AB, N_Q * HEAD), lambda i: (i, COL_Q // (N_Q * HEAD))),
                  pl.BlockSpec((AB, KW), lambda i: (i, COL_K // KW)),
                  pl.BlockSpec((AB, KW), lambda i: (i, COL_V // KW)),
                  pl.BlockSpec((AB, HEAD), lat_map), pl.BlockSpec((AB, HEAD), lat_map)],
        out_specs=[pl.BlockSpec((AB, N_Q * HEAD), lambda i: (i, 0)),
                   pl.BlockSpec((AB, KW), pad_map), pl.BlockSpec((AB, KW), pad_map),
                   pl.BlockSpec((AB, KW), ctx_map), pl.BlockSpec((AB, KW), ctx_map)],
        out_shape=[_sds((T, N_Q * HEAD), BF16), _sds((S + 2 * AB, KW), BF16), _sds((S + 2 * AB, KW), BF16),
                   _sds((CTX, KW), BF16), _sds((CTX, KW), BF16)],
        compiler_params=_params(),
    )(p, p, p, cos, sin)


GQ = Q_PER_KV * AB
GW = Q_PER_KV * HEAD


def _stack_heads(blk):
    return jnp.concatenate([blk[:, g * HEAD:(g + 1) * HEAD] for g in range(Q_PER_KV)], axis=0)


def _unstack_heads(x4):
    return jnp.concatenate([x4[g * AB:(g + 1) * AB, :] for g in range(Q_PER_KV)], axis=1)


def _sink_col(sink_ref):
    return jnp.concatenate([jnp.broadcast_to(sink_ref[g:g + 1, 0:1], (AB, 1)) for g in range(Q_PER_KV)], axis=0)


def _band_mask(n, S):
    r = lax.broadcasted_iota(jnp.int32, (GQ, 3 * AB), 0)
    c = lax.broadcasted_iota(jnp.int32, (GQ, 3 * AB), 1)
    d = c - AB - (r & (AB - 1))
    kpos = n * AB - AB + c
    return (jnp.abs(d) <= AB) & (kpos >= 0) & (kpos < S)


def _attn_probs(q4, kc, sink, kb, mask):
    s_ctx = _dot(q4, kc, NT) * ATT_SCALE
    m = jnp.maximum(jnp.max(s_ctx, axis=-1, keepdims=True), sink)
    if kb is not None:
        s_b = jnp.where(mask, _dot(q4, kb, NT) * ATT_SCALE, NEG_INF)
        m = jnp.maximum(m, jnp.max(s_b, axis=-1, keepdims=True))
    p_ctx = jnp.exp(s_ctx - m)
    p_sink = jnp.exp(sink - m)
    l = jnp.sum(p_ctx, axis=-1, keepdims=True) + p_sink
    p_b = None
    if kb is not None:
        p_b = jnp.exp(s_b - m)
        l = l + jnp.sum(p_b, axis=-1, keepdims=True)
    inv = 1.0 / l
    return p_ctx * inv, (None if p_b is None else p_b * inv), p_sink * inv


def _attn_fwd(name, qa, kc, vc, sink4, S, band=None, prev=None, carry=None):
    T = qa.shape[0]
    has_band = band is not None
    nq = S // AB if has_band else CTX_BLKS
    q_off = CTX_BLKS if has_band else 0

    def kern(*refs):
        q_ref, kc_ref, vc_ref, sink_ref = refs[:4]
        rest = refs[4:]
        if has_band:
            kp_ref, vp_ref = rest[:2]
            rest = rest[2:]
        o_ref = rest[-1]
        n = pl.program_id(1)
        q4 = _stack_heads(q_ref[...])
        sink = _sink_col(sink_ref)
        kb = vb = mask = None
        if has_band:
            start = pl.multiple_of(n * AB, AB)
            kb = kp_ref[pl.ds(start, 3 * AB), :]
            vb = vp_ref[pl.ds(start, 3 * AB), :]
            mask = _band_mask(n, S)
        p_ctx, p_b, _ = _attn_probs(q4, kc_ref[...], sink, kb, mask)
        o4 = _dot(p_ctx.astype(BF16), vc_ref[...])
        if has_band:
            o4 = o4 + _dot(p_b.astype(BF16), vb)
        o_ref[...] = _unstack_heads(o4).astype(BF16)

    in_specs = [pl.BlockSpec((AB, GW), lambda kh, n: (n + q_off, kh)),
                pl.BlockSpec((CTX, HEAD), lambda kh, n: (0, kh)), pl.BlockSpec((CTX, HEAD), lambda kh, n: (0, kh)),
                pl.BlockSpec((None, Q_PER_KV, HEAD), lambda kh, n: (kh, 0, 0))]
    args = [qa, kc, vc, sink4]
    if has_band:
        in_specs += [pl.BlockSpec((S + 2 * AB, HEAD), lambda kh, n: (0, kh))] * 2
        args += list(band)
    alias = {}
    if prev is not None:
        in_specs.append(ANY)
        alias = {len(args): 0}
        args.append(prev)
    ci, ca, co, cs, cscr = _carry_args(carry)
    res = pl.pallas_call(
        _carried(kern, carry, len(args), 1, *_grid_ends((N_KV, nq))), name=name, grid=(N_KV, nq),
        in_specs=in_specs + ci,
        out_specs=[pl.BlockSpec((AB, GW), lambda kh, n: (n + q_off, kh))] + co,
        out_shape=[_sds((T, N_Q * HEAD), BF16)] + cs, input_output_aliases=alias, scratch_shapes=cscr,
        compiler_params=_params(("arbitrary", "arbitrary")),
    )(*args, *ca)
    return res[0] if carry is None else (res[0], res[1:])


def _attn_bwd(name, qa, kc, vc, sink4, o_all, do_all, S, band=None, prev_dq=None, carry=None):
    T = qa.shape[0]
    has_band = band is not None
    nq = S // AB if has_band else CTX_BLKS
    q_off = CTX_BLKS if has_band else 0
    KW = N_KV * HEAD

    def kern(*refs):
        q_ref, kc_ref, vc_ref, sink_ref, o_ref, do_ref = refs[:6]
        rest = refs[6:]
        if has_band:
            kp_ref, vp_ref = rest[:2]
            rest = rest[2:]
        if prev_dq is not None:
            rest = rest[1:]
        dq_ref, dkc_ref, dvc_ref, dsink_ref = rest[:4]
        n = pl.program_id(1)

        @pl.when(n == 0)
        def _():
            dkc_ref[...] = jnp.zeros(dkc_ref.shape, F32)
            dvc_ref[...] = jnp.zeros(dvc_ref.shape, F32)
            dsink_ref[...] = jnp.zeros(dsink_ref.shape, F32)
            if has_band:
                rest[4][...] = jnp.zeros(rest[4].shape, F32)
                rest[5][...] = jnp.zeros(rest[5].shape, F32)

        q4 = _stack_heads(q_ref[...])
        sink = _sink_col(sink_ref)
        kc_v, vc_v = kc_ref[...], vc_ref[...]
        kb = vb = mask = None
        if has_band:
            start = pl.multiple_of(n * AB, AB)
            kb = kp_ref[pl.ds(start, 3 * AB), :]
            vb = vp_ref[pl.ds(start, 3 * AB), :]
            mask = _band_mask(n, S)
        p_ctx, p_b, p_sink = _attn_probs(q4, kc_v, sink, kb, mask)
        do4 = _stack_heads(do_ref[...])
        o4 = _stack_heads(o_ref[...]).astype(F32)
        delta = jnp.sum(do4 * o4, axis=-1, keepdims=True)
        do4b = do4.astype(BF16)
        ds_ctx = (p_ctx * (_dot(do4b, vc_v, NT) - delta)).astype(BF16)
        dq4 = _dot(ds_ctx, kc_v)
        dkc_ref[...] += _dot(ds_ctx, q4, TN) * ATT_SCALE
        dvc_ref[...] += _dot(p_ctx.astype(BF16), do4b, TN)
        if has_band:
            ds_b = (p_b * (_dot(do4b, vb, NT) - delta)).astype(BF16)
            dq4 = dq4 + _dot(ds_b, kb)
            rest[4][pl.ds(start, 3 * AB), :] += _dot(ds_b, q4, TN) * ATT_SCALE
            rest[5][pl.ds(start, 3 * AB), :] += _dot(p_b.astype(BF16), do4b, TN)
        dq_ref[...] = _unstack_heads(dq4 * ATT_SCALE)
        ps = p_sink * delta
        dsink_ref[...] += jnp.concatenate(
            [jnp.broadcast_to(-jnp.sum(ps[g * AB:(g + 1) * AB, :], axis=0, keepdims=True), (1, HEAD))
             for g in range(Q_PER_KV)], axis=0)

    q_spec = pl.BlockSpec((AB, GW), lambda kh, n: (n + q_off, kh))
    c_spec = pl.BlockSpec((CTX, HEAD), lambda kh, n: (0, kh))
    s_spec = pl.BlockSpec((None, Q_PER_KV, HEAD), lambda kh, n: (kh, 0, 0))
    in_specs = [q_spec, c_spec, c_spec, s_spec, q_spec, q_spec]
    args = [qa, kc, vc, sink4, o_all, do_all]
    out_specs = [q_spec, c_spec, c_spec, s_spec]
    out_shape = [_sds((T, N_Q * HEAD), F32), _sds((CTX, KW), F32), _sds((CTX, KW), F32), _sds((N_KV, Q_PER_KV, HEAD), F32)]
    if has_band:
        p_spec = pl.BlockSpec((S + 2 * AB, HEAD), lambda kh, n: (0, kh))
        in_specs += [p_spec, p_spec]
        args += list(band)
        out_specs += [p_spec, p_spec]
        out_shape += [_sds((S + 2 * AB, KW), F32)] * 2
    alias = {}
    if prev_dq is not None:
        in_specs.append(ANY)
        alias = {len(args): 0}
        args.append(prev_dq)
    ci, ca, co, cs, cscr = _carry_args(carry)
    n_out = len(out_specs)
    res = pl.pallas_call(
        _carried(kern, carry, len(args), n_out, *_grid_ends((N_KV, nq))), name=name, grid=(N_KV, nq),
        in_specs=in_specs + ci, out_specs=out_specs + co, out_shape=out_shape + cs, scratch_shapes=cscr,
        input_output_aliases=alias, compiler_params=_params(("arbitrary", "arbitrary")),
    )(*args, *ca)
    return res if carry is None else (res[:n_out], res[n_out:])


def _dqkv_assemble(name, dp, dq_all, dkp, dvp, dkc_l, dvc_l, dkc_c, dvc_c, cos, sin, S):
    T = CTX + S
    KW = N_KV * HEAD
    HALF = N_Q * HEAD // 2

    def kern(dq_ref, dkp_ref, dvp_ref, dkcl_ref, dvcl_ref, dkcc_ref, dvcc_ref, cos_ref, sin_ref, dp_in, out_ref):
        i = pl.program_id(0)
        j = pl.program_id(1)
        lat = i >= CTX_BLKS
        cos_v, sin_v = cos_ref[...], sin_ref[...]

        @pl.when(j < 2)
        def _():
            dq = dq_ref[...]
            out_ref[...] = jnp.where(lat, _unrope(dq, cos_v, sin_v), dq).astype(BF16)

        @pl.when(j == 2)
        def _():
            dk = jnp.where(lat, _unrope(dkp_ref[...], cos_v, sin_v), dkcl_ref[...] + dkcc_ref[...])
            dv = jnp.where(lat, dvp_ref[...], dvcl_ref[...] + dvcc_ref[...])
            out_ref[...] = jnp.concatenate([dk, dv], axis=1).astype(BF16)

    lat_map = lambda i, j: (jnp.maximum(i - CTX_BLKS, 0), 0)
    pad_map = lambda i, j: (jnp.maximum(i - 1, 0), 0)
    ctx_map = lambda i, j: (jnp.minimum(i, CTX_BLKS - 1), 0)
    return pl.pallas_call(
        kern, name=name, grid=(T // AB, 3),
        in_specs=[pl.BlockSpec((AB, HALF), lambda i, j: (i, jnp.minimum(j, 1))),
                  pl.BlockSpec((AB, KW), pad_map), pl.BlockSpec((AB, KW), pad_map),
                  pl.BlockSpec((AB, KW), ctx_map), pl.BlockSpec((AB, KW), ctx_map),
                  pl.BlockSpec((AB, KW), ctx_map), pl.BlockSpec((AB, KW), ctx_map),
                  pl.BlockSpec((AB, HEAD), lat_map), pl.BlockSpec((AB, HEAD), lat_map), ANY],
        out_specs=pl.BlockSpec((AB, HALF), lambda i, j: (i, COL_Q // HALF + j)),
        out_shape=_sds((T, P_W), BF16), input_output_aliases={9: 0},
        compiler_params=_params(("arbitrary", "arbitrary")),
    )(dq_all, dkp, dvp, dkc_l, dvc_l, dkc_c, dvc_c, cos, sin, dp)


RB = 128
CH = 256
HALO = 8
SUB = 8
GRP = 8


def _vscan(a, b, reverse):
    row = lax.broadcasted_iota(jnp.int32, a.shape, 0)
    A, H = a, b
    for s in (1, 2, 4):
        sh = SUB - s if reverse else s
        m = (row < SUB - s) if reverse else (row >= s)
        As = pltpu.roll(A, sh, 0)
        Hs = pltpu.roll(H, sh, 0)
        H = jnp.where(m, A * Hs + H, H)
        A = jnp.where(m, A * As, A)
    return A, H


def _scan_rows(a_ref, b_ref, r0, nrows, reverse, carry, emit):
    ngrp = nrows // (SUB * GRP)
    row = lax.broadcasted_iota(jnp.int32, (SUB, RB), 0)

    def grp(gi, carry):
        g = (ngrp - 1 - gi) if reverse else gi
        base = r0 + g * (SUB * GRP)
        for v in (range(GRP - 1, -1, -1) if reverse else range(GRP)):
            rs = pl.multiple_of(base + v * SUB, SUB)
            A, H = _vscan(a_ref[pl.ds(rs, SUB), :], b_ref[pl.ds(rs, SUB), :], reverse)
            hf = H + A * carry
            if reverse:
                before = jnp.where(row == SUB - 1, carry, pltpu.roll(hf, SUB - 1, 0))
                carry = hf[0:1, :]
            else:
                before = jnp.where(row == 0, carry, pltpu.roll(hf, 1, 0))
                carry = hf[SUB - 1:SUB, :]
            emit(rs, hf, before)
        return carry

    return lax.fori_loop(0, ngrp, grp, carry)


def _conv_taps(ext, t, T, transpose=False):
    lo = jnp.where(t < CTX, 0, CTX)
    hi = jnp.where(t < CTX, CTX, T)
    n = CH + 2 * HALO
    taps = []
    for k in range(CONV_W):
        off = k - CONV_LEFT
        if transpose:
            off = -off
        sh = pltpu.roll(ext, (-off) % n, 0)[HALO:HALO + CH, :]
        valid = (t + off >= lo) & (t + off < hi)
        taps.append(jnp.where(valid, sh, 0.0))
    return taps


def _lru_gates(xl, w4, b4, ls):
    pre = _dot(xl.astype(BF16), w4) + b4
    out = []
    for d in range(2):
        r = _sigmoid(pre[:, d * RB:(d + 1) * RB])
        i = _sigmoid(pre[:, (2 + d) * RB:(3 + d) * RB])
        la = LRU_C * r * ls[d:d + 1, :]
        a = jnp.exp(la)
        mult = jnp.sqrt(_neg_expm1(2.0 * la))
        out.append((r, i, a, mult))
    return out


def _rnn_specs(T):
    col = lambda n, *_: (0, n)
    return dict(
        xr=pl.BlockSpec((T, RB), lambda n, *_: (0, COL_XR // RB + n)),
        gr=pl.BlockSpec((T, RB), lambda n, *_: (0, COL_GR // RB + n)),
        act=pl.BlockSpec((T, RB), col),
        cw=pl.BlockSpec((CONV_W, RB), col), cb=pl.BlockSpec((1, RB), col),
        w4=pl.BlockSpec((None, RB, 4 * RB), lambda n, *_: (n, 0, 0)),
        b4=pl.BlockSpec((None, 1, 4 * RB), lambda n, *_: (n, 0, 0)),
        lam=pl.BlockSpec((2, RB), col))


def _fill_padded(pad_ref, src_ref, T):
    pad_ref[0:HALO, :] = jnp.zeros((HALO, RB), F32)
    pad_ref[HALO + T:2 * HALO + T, :] = jnp.zeros((HALO, RB), F32)
    pad_ref[HALO:HALO + T, :] = src_ref[...].astype(F32)


def _rnn_fwd(name, p, cw, cb, w4, b4, lam, T, carry=None):
    def kern(xr_ref, gr_ref, cw_ref, cb_ref, w4_ref, b4_ref, lam_ref, u_ref, hpf_ref, hpb_ref,
             xpad, a0, b0, a1, b1, y):
        _fill_padded(xpad, xr_ref, T)
        ls = _log_sigmoid(lam_ref[...])
        w4v, b4v, cwv, cbv = w4_ref[...], b4_ref[...], cw_ref[...], cb_ref[...]

        def chunk(ci, _):
            base = pl.multiple_of(ci * CH, CH)
            t = base + lax.broadcasted_iota(jnp.int32, (CH, 1), 0)
            taps = _conv_taps(xpad[pl.ds(base, CH + 2 * HALO), :], t, T)
            xl = cbv + sum(taps[k] * cwv[k:k + 1, :] for k in range(CONV_W))
            for d, (r, i, a, mult) in enumerate(_lru_gates(xl, w4v, b4v, ls)):
                (a0, a1)[d][pl.ds(base, CH), :] = a
                (b0, b1)[d][pl.ds(base, CH), :] = mult * (i * xl)
            return 0

        lax.fori_loop(0, T // CH, chunk, 0)
        zero = jnp.zeros((1, RB), F32)

        def emit_f(rs, hf, before):
            y[pl.ds(rs, SUB), :] = hf
            hpf_ref[pl.ds(rs, SUB), :] = before

        def emit_b(rs, hf, before):
            y[pl.ds(rs, SUB), :] += hf
            hpb_ref[pl.ds(rs, SUB), :] = before

        _scan_rows(a0, b0, 0, T, False, zero, emit_f)
        c = _scan_rows(a1, b1, 0, CTX, True, zero, emit_b)
        _scan_rows(a1, b1, CTX, T - CTX, True, c, emit_b)

        def finish(ci, _):
            base = pl.multiple_of(ci * CH, CH)
            gr = gr_ref[pl.ds(base, CH), :].astype(F32)
            u_ref[pl.ds(base, CH), :] = (y[pl.ds(base, CH), :] * _gelu(gr)).astype(BF16)
            return 0

        lax.fori_loop(0, T // CH, finish, 0)

    sp = _rnn_specs(T)
    ci, ca, co, cs, cscr = _carry_args(carry)
    res = pl.pallas_call(
        _carried(kern, carry, 7, 3, *_grid_ends((N_RNN_BLOCKS,))), name=name, grid=(N_RNN_BLOCKS,),
        in_specs=[sp["xr"], sp["gr"], sp["cw"], sp["cb"], sp["w4"], sp["b4"], sp["lam"]] + ci,
        out_specs=[sp["act"]] * 3 + co,
        out_shape=[_sds((T, D), BF16), _sds((T, D), F32), _sds((T, D), F32)] + cs,
        scratch_shapes=[pltpu.VMEM((T + 2 * HALO, RB), F32)] + [pltpu.VMEM((T, RB), F32)] * 5 + cscr,
        compiler_params=_params(),
    )(p, p, cw, cb, w4, b4, lam, *ca)
    return res if carry is None else (res[:3], res[3:])


def _rnn_bwd(name, p, du, hpf, hpb, dp, cw, cb, w4, b4, lam, T, carry=None):
    def kern(xr_ref, gr_ref, du_ref, hpf_ref, hpb_ref, cw_ref, cb_ref, w4_ref, b4_ref, lam_ref, dp_in,
             dp_ref, dcw_ref, dcb_ref, dw4_ref, db4_ref, dlam_ref,
             xpad, dxpad, a0, a1, c0, c1, dy, dgr_ref):
        j = pl.program_id(1)

        @pl.when(j == 0)
        def _():
            work(xr_ref, gr_ref, du_ref, hpf_ref, hpb_ref, cw_ref, cb_ref, w4_ref, b4_ref, lam_ref,
                 dp_ref, dgr_ref, dcw_ref, dcb_ref, dw4_ref, db4_ref, dlam_ref, xpad, dxpad, a0, a1, c0, c1, dy)

        @pl.when(j == 1)
        def _():
            dp_ref[...] = dgr_ref[...]

    def work(xr_ref, gr_ref, du_ref, hpf_ref, hpb_ref, cw_ref, cb_ref, w4_ref, b4_ref, lam_ref,
             dxr_ref, dgr_ref, dcw_ref, dcb_ref, dw4_ref, db4_ref, dlam_ref, xpad, dxpad, a0, a1, c0, c1, dy):
        _fill_padded(xpad, xr_ref, T)
        dxpad[0:HALO, :] = jnp.zeros((HALO, RB), F32)
        dxpad[HALO + T:2 * HALO + T, :] = jnp.zeros((HALO, RB), F32)
        lam_v = lam_ref[...]
        ls = _log_sigmoid(lam_v)
        w4v, b4v, cwv, cbv = w4_ref[...], b4_ref[...], cw_ref[...], cb_ref[...]

        def conv_chunk(base):
            t = base + lax.broadcasted_iota(jnp.int32, (CH, 1), 0)
            taps = _conv_taps(xpad[pl.ds(base, CH + 2 * HALO), :], t, T)
            return t, taps, cbv + sum(taps[k] * cwv[k:k + 1, :] for k in range(CONV_W))

        def phase_a(ci, _):
            base = pl.multiple_of(ci * CH, CH)
            rows = pl.ds(base, CH)
            _, _, xl = conv_chunk(base)
            (r0, i0, av0, m0), (r1, i1, av1, m1) = _lru_gates(xl, w4v, b4v, ls)
            yv = (av0 * hpf_ref[rows, :] + m0 * (i0 * xl)) + (av1 * hpb_ref[rows, :] + m1 * (i1 * xl))
            gr = gr_ref[rows, :].astype(F32)
            duv = du_ref[rows, :]
            dyv = duv * _gelu(gr)
            dgr_ref[rows, :] = (duv * yv * _gelu_grad(gr)).astype(BF16)
            dy[rows, :] = dyv
            a0[rows, :] = av0
            a1[rows, :] = av1
            c0[rows, :] = av0 * dyv
            c1[rows, :] = av1 * dyv
            return 0

        lax.fori_loop(0, T // CH, phase_a, 0)
        zero = jnp.zeros((1, RB), F32)

        def emit0(rs, hf, before):
            c0[pl.ds(rs, SUB), :] = dy[pl.ds(rs, SUB), :] + before

        def emit1(rs, hf, before):
            c1[pl.ds(rs, SUB), :] = dy[pl.ds(rs, SUB), :] + before

        _scan_rows(a0, c0, 0, T, True, zero, emit0)
        c = _scan_rows(a1, c1, CTX, T - CTX, False, zero, emit1)
        _scan_rows(a1, c1, 0, CTX, False, c, emit1)

        dw4_ref[...] = jnp.zeros(dw4_ref.shape, F32)
        db4_ref[...] = jnp.zeros(db4_ref.shape, F32)
        dlam_ref[...] = jnp.zeros(dlam_ref.shape, F32)
        dcw_ref[...] = jnp.zeros(dcw_ref.shape, F32)
        dcb_ref[...] = jnp.zeros(dcb_ref.shape, F32)

        def phase_c(ci, _):
            base = pl.multiple_of(ci * CH, CH)
            rows = pl.ds(base, CH)
            _, _, xl = conv_chunk(base)
            gates = _lru_gates(xl, w4v, b4v, ls)
            dxl = jnp.zeros((CH, RB), F32)
            dpre_a, dpre_x, dls = [], [], []
            for d, (r, i, a, mult) in enumerate(gates):
                g = (c0, c1)[d][rows, :]
                hp = (hpf_ref, hpb_ref)[d][rows, :]
                dmult = g * (i * xl)
                di = g * mult * xl
                dxl = dxl + g * mult * i
                dla = g * hp * a - (a * a) * dmult / mult
                dr = dla * (LRU_C * ls[d:d + 1, :])
                dls.append(_colsum(dla * (LRU_C * r)))
                dpre_a.append(dr * r * (1.0 - r))
                dpre_x.append(di * i * (1.0 - i))
            dpre = jnp.concatenate(dpre_a + dpre_x, axis=1)
            dpre_b = dpre.astype(BF16)
            dxl = dxl + _dot(dpre_b, w4v, NT)
            dw4_ref[...] += _dot(xl.astype(BF16), dpre_b, TN)
            db4_ref[...] += _colsum(dpre)
            dlam_ref[...] += jnp.concatenate(dls, axis=0)
            dcb_ref[...] += _colsum(dxl)
            dxpad[pl.ds(pl.multiple_of(base + HALO, HALO), CH), :] = dxl
            return 0

        lax.fori_loop(0, T // CH, phase_c, 0)
        dlam_ref[...] = dlam_ref[...] * _sigmoid(-lam_v)

        def phase_d(ci, _):
            base = pl.multiple_of(ci * CH, CH)
            rows = pl.ds(base, CH)
            t, xtaps, _ = conv_chunk(base)
            dtaps = _conv_taps(dxpad[pl.ds(base, CH + 2 * HALO), :], t, T, transpose=True)
            dxl = dxpad[pl.ds(pl.multiple_of(base + HALO, HALO), CH), :]
            dxr_ref[rows, :] = sum(dtaps[k] * cwv[k:k + 1, :] for k in range(CONV_W)).astype(BF16)
            dcw_ref[...] += jnp.concatenate([_colsum(dxl * xtaps[k]) for k in range(CONV_W)], axis=0)
            return 0

        lax.fori_loop(0, T // CH, phase_d, 0)

    sp = _rnn_specs(T)
    dp_spec = pl.BlockSpec((T, RB), lambda n, j: (0, COL_XR // RB + n + j * (COL_GR - COL_XR) // RB))
    ci, ca, co, cs, cscr = _carry_args(carry)
    res = pl.pallas_call(
        _carried(kern, carry, 11, 6, *_grid_ends((N_RNN_BLOCKS, 2))), name=name, grid=(N_RNN_BLOCKS, 2),
        in_specs=[sp["xr"], sp["gr"], sp["act"], sp["act"], sp["act"], sp["cw"], sp["cb"], sp["w4"], sp["b4"],
                  sp["lam"], ANY] + ci,
        out_specs=[dp_spec, sp["cw"], sp["cb"], sp["w4"], sp["b4"], sp["lam"]] + co,
        out_shape=[_sds((T, P_W), BF16), _sds((CONV_W, D), F32), _sds((1, D), F32),
                   _sds((N_RNN_BLOCKS, RB, 4 * RB), F32), _sds((N_RNN_BLOCKS, 1, 4 * RB), F32), _sds((2, D), F32)] + cs,
        scratch_shapes=([pltpu.VMEM((T + 2 * HALO, RB), F32)] * 2 + [pltpu.VMEM((T, RB), F32)] * 5
                        + [pltpu.VMEM((T, RB), BF16)] + cscr),
        input_output_aliases={10: 0},
        compiler_params=_params(("arbitrary", "arbitrary")),
    )(p, p, du, hpf, hpb, cw, cb, w4, b4, lam, dp, *ca)
    return res if carry is None else (res[:6], res[6:])


class _Plan:
    def __init__(self, shards, Ws):
        L = len(Ws)
        self.shards, self.Ws = shards, Ws
        self.Gs = [None] * L
        self.slots = [dict() for _ in range(L)]
        self.gate_slots = [None] * L
        self.table = {}
        for l in range(L):
            t = f"l{l}_"
            self.table[t + "proj"] = [("gather", l, k) for k in ("wo_rnn", "wo_attn", "wout")]
            self.table[t + "rnn_fwd"] = [("gather", l, "wffn_in_t")]
            self.table[t + "attn_lat_fwd"] = [("gather", l + 1, "win_t")] if l + 1 < L else []
            self.table[t + "ffn_in"] = [("gather", l, "wffn_out")]
            self.table[t + "ffn_in_dx"] = [("scatter", l, "wffn_out")]
            self.table[t + "attn_lat_bwd"] = [("scatter", l, "wffn_in_t")]
            self.table[t + "rnn_bwd"] = ([("scatter", l, k) for k in ("wout", "wo_attn", "wo_rnn")]
                                         + ([("scatter", l + 1, "win_t")] if l + 1 < L else []))
            self.table[t + "proj_dx"] = [("gates", l, "w4")]

    def carry(self, name):
        jobs = []
        for kind, l, k in self.table.get(name, []):
            if kind == "gather":
                jobs.append(("gather", self.shards[l][k]))
            elif kind == "scatter":
                jobs.append(("scatter", self.Gs[l][k].reshape(N_DEV, -1, D)))
            else:
                jobs.append(("gather", self.Gs[l]["w4"].reshape(N_RNN_BLOCKS * RB, 4 * RB).astype(BF16)))
        return _Carry(jobs) if jobs else None

    def done(self, name, got):
        for (kind, l, k), res in zip(self.table[name], got):
            if kind == "gather":
                self.Ws[l][k] = res.reshape(-1, D)
            elif kind == "scatter":
                self.slots[l][k] = res
            else:
                self.gate_slots[l] = res


def _run(X, fn, name, *args, **kw):
    carry = None if X is None else X.carry(name)
    if carry is None:
        return fn(name, *args, **kw)
    out, got = fn(name, *args, carry=carry, **kw)
    X.done(name, got)
    return out


def _layer_fwd(l, xa, h, W, rope, S, nxt, X=None):
    T = xa.shape[0]
    tag = f"l{l}_"
    cos, sin = rope
    p = _run(X, _mm_act, tag + "proj", h, W["win_t"], "NT", BF16)
    u, hpf, hpb = _run(X, _rnn_fwd, tag + "rnn_fwd", p, W["cw"], W["cb"], W["w4"], W["b4"], W["lam"], T)
    qa, kp, vp, kc, vc = _qkv_prep(tag + "qkv_prep", p, cos, sin, S)
    o_all = _attn_fwd(tag + "attn_ctx_fwd", qa, kc, vc, W["sink4"], S)
    o_all = _run(X, _attn_fwd, tag + "attn_lat_fwd", qa, kc, vc, W["sink4"], S, band=(kp, vp), prev=o_all)
    ya = _mm_act(tag + "o_rnn", u, W["wo_rnn"], "NN")
    yb = _mm_act(tag + "o_attn", o_all, W["wo_attn"], "NN")
    z = _gate_fwd(tag + "gate_fwd", p, ya, yb)
    m = _mm_act(tag + "out", z, W["wout"], "NN")
    x1, h2 = _resid_norm_fwd(tag + "mix_resid", xa, m, W["g_mix_post"], W["mod"], GA1, W["g_ffn_pre"], W["mod"], SH2, SC2)
    f = _run(X, _mm_act, tag + "ffn_in", h2, W["wffn_in_t"], "NT", BF16)
    s = _swiglu_fwd(tag + "swiglu_fwd", f)
    e = _mm_act(tag + "ffn_out", s, W["wffn_out"], "NN")
    saved = dict(xa=xa, h=h, p=p, u=u, hpf=hpf, hpb=hpb, qa=qa, kp=kp, vp=vp, kc=kc, vc=vc, o_all=o_all,
                 ya=ya, yb=yb, z=z, m=m, x1=x1, h2=h2, f=f, s=s, e=e)
    if nxt[0] == "norm":
        out = _resid_norm_fwd(tag + "ffn_resid", x1, e, W["g_ffn_post"], W["mod"], GA2, nxt[1], nxt[2], SH1, SC1)
    else:
        out = _resid_loss_fwd(tag + "ffn_resid_loss", x1, e, W["g_ffn_post"], W["mod"], GA2, nxt[1])
    return saved, out


def _layer_bwd(l, dx2, A, W, rope, S, X=None):
    T = dx2.shape[0]
    tag = f"l{l}_"
    cos, sin = rope
    G = {}
    if X is not None:
        X.Gs[l] = G
    de, dga2, G["g_ffn_post"] = _resid_bwd(tag + "ffn_resid_bwd", dx2, A["e"], W["g_ffn_post"], W["mod"], GA2)
    ds = _mm_act(tag + "ffn_out_dx", de, W["wffn_out"], "NT", BF16)
    G["wffn_out"] = _mm_wgrad(tag + "ffn_out_dw", A["s"], de)
    df = _swiglu_bwd(tag + "swiglu_bwd", A["f"], ds)
    dh2 = _run(X, _mm_act, tag + "ffn_in_dx", df, W["wffn_in_t"], "NN")
    G["wffn_in_t"] = _mm_wgrad(tag + "ffn_in_dw", df, A["h2"])
    dx1, dm, dsh2, dsc2, G["g_ffn_pre"], dga1, G["g_mix_post"] = _normmod_resid_bwd(
        tag + "mix_resid_bwd", dh2, A["x1"], W["g_ffn_pre"], W["mod"], SH2, SC2, dx2, A["m"], W["g_mix_post"], GA1)
    dz = _mm_act(tag + "out_dx", dm, W["wout"], "NT")
    G["wout"] = _mm_wgrad(tag + "out_dw", A["z"], dm)
    dya, dyb, dp = _gate_bwd(tag + "gate_bwd", A["p"], A["ya"], A["yb"], dz)
    do = _mm_act(tag + "o_attn_dx", dyb, W["wo_attn"], "NT")
    G["wo_attn"] = _mm_wgrad(tag + "o_attn_dw", A["o_all"], dyb)
    du = _mm_act(tag + "o_rnn_dx", dya, W["wo_rnn"], "NT")
    G["wo_rnn"] = _mm_wgrad(tag + "o_rnn_dw", A["u"], dya)
    dq_all, dkc_c, dvc_c, dsink_c = _attn_bwd(tag + "attn_ctx_bwd", A["qa"], A["kc"], A["vc"], W["sink4"],
                                               A["o_all"], do, S)
    dq_all, dkc_l, dvc_l, dsink_l, dkp, dvp = _run(
        X, _attn_bwd, tag + "attn_lat_bwd", A["qa"], A["kc"], A["vc"], W["sink4"], A["o_all"], do, S,
        band=(A["kp"], A["vp"]), prev_dq=dq_all)
    G["sink4"] = dsink_c + dsink_l
    dp = _dqkv_assemble(tag + "dqkv", dp, dq_all, dkp, dvp, dkc_l, dvc_l, dkc_c, dvc_c, cos, sin, S)
    dp, G["cw"], G["cb"], G["w4"], G["b4"], G["lam"] = _run(
        X, _rnn_bwd, tag + "rnn_bwd", A["p"], du, A["hpf"], A["hpb"], dp, W["cw"], W["cb"], W["w4"], W["b4"], W["lam"], T)
    dh = _run(X, _mm_act, tag + "proj_dx", dp, W["win_t"], "NN")
    G["win_t"] = _mm_wgrad(tag + "proj_dw", dp, A["h"])
    dxa, dsh1, dsc1, G["g_mix_pre"] = _normmod_bwd(tag + "mix_norm_bwd", dh, A["xa"], W["g_mix_pre"], W["mod"],
                                                   SH1, SC1, dx1)
    G["mod"] = jnp.concatenate([dsh1, dsc1, dga1, dsh2, dsc2, dga2], axis=1)
    return dxa, G


def _local_step(xa, target, Ws, S, X=None):
    rope = _rope_tables(S)
    L = len(Ws)
    h = _normmod_fwd("l0_mix_norm", xa, Ws[0]["g_mix_pre"], Ws[0]["mod"], SH1, SC1)
    saved = []
    x = xa
    for l in range(L):
        nxt = ("norm", Ws[l + 1]["g_mix_pre"], Ws[l + 1]["mod"]) if l + 1 < L else ("loss", target)
        A, out = _layer_fwd(l, x, h, Ws[l], rope, S, nxt, X)
        saved.append(A)
        if l + 1 < L:
            x, h = out
    dx, sq = out
    Gs = [None] * L
    for l in reversed(range(L)):
        dx, Gs[l] = _layer_bwd(l, dx, saved[l], Ws[l], rope, S, X)
    return sq, dx, Gs


MESH = pl.DeviceIdType.MESH


def _place():
    return lax.axis_index("x"), lax.axis_index("y"), lax.axis_index("c")


def _lin(px, py, pc):
    return 4 * px + 2 * py + pc


def _allgather_small(name, blk):
    m, n = blk.shape

    def body(x_ref, out_ref, send_sems, recv_sems, local_sem):
        x, y, c = _place()
        me, sibling = (x, y, c), (x, y, 1 - c)
        chips = [(1 - x, y), (x, 1 - y), (1 - x, 1 - y)]

        def copy(k, block, to, src=None):
            dst = out_ref.at[_lin(*block)]
            return pltpu.make_async_remote_copy(src_ref=dst if src is None else src, dst_ref=dst,
                                                send_sem=send_sems.at[k], recv_sem=recv_sems.at[k],
                                                device_id=to, device_id_type=MESH)

        mine = pltpu.make_async_copy(x_ref, out_ref.at[_lin(*me)], local_sem)
        mine.start()
        first = [copy(0, me, sibling, src=x_ref)]
        first += [copy(1 + j, me, (*chip, c), src=x_ref) for j, chip in enumerate(chips)]
        for cp in first:
            cp.start()
        passed = [copy(4 + j, (*chip, c), sibling) for j, chip in enumerate(chips)]
        for j, chip in enumerate(chips):
            copy(1 + j, (*chip, c), me).wait_recv()
            passed[j].start()
        copy(0, sibling, me).wait_recv()
        for j, chip in enumerate(chips):
            copy(4 + j, (*chip, 1 - c), me).wait_recv()
        for cp in first + passed:
            cp.wait_send()
        mine.wait()

    return pl.pallas_call(
        body, name=name, out_shape=_sds((N_DEV, m, n), blk.dtype),
        in_specs=[pl.BlockSpec(memory_space=pltpu.VMEM)], out_specs=pl.BlockSpec(memory_space=pltpu.VMEM),
        scratch_shapes=[pltpu.SemaphoreType.DMA((7,)), pltpu.SemaphoreType.DMA((7,)), pltpu.SemaphoreType.DMA],
        compiler_params=pltpu.CompilerParams(vmem_limit_bytes=VMEM_LIMIT),
    )(blk)


def _allgather_hbm(name, shards):
    na = len(shards)

    def body(*refs):
        ins, outs = refs[:na], refs[na:2 * na]
        send_sems, recv_sems, local_sems = refs[2 * na:]
        x, y, c = _place()
        me, sibling = (x, y, c), (x, y, 1 - c)
        chips = [(1 - x, y), (x, 1 - y), (1 - x, 1 - y)]

        def copy(a, k, block, to, from_input=False):
            dst = outs[a].at[_lin(*block)]
            return pltpu.make_async_remote_copy(src_ref=ins[a] if from_input else dst, dst_ref=dst,
                                                send_sem=send_sems.at[a, k], recv_sem=recv_sems.at[a, k],
                                                device_id=to, device_id_type=MESH)

        mine = [pltpu.make_async_copy(ins[a], outs[a].at[_lin(*me)], local_sems.at[a]) for a in range(na)]
        for cp in mine:
            cp.start()
        first = []
        for a in range(na):
            first.append(copy(a, 0, me, sibling, True))
            first += [copy(a, 1 + j, me, (*chip, c), True) for j, chip in enumerate(chips)]
        for cp in first:
            cp.start()
        passed = []
        for j, chip in enumerate(chips):
            for a in range(na):
                copy(a, 1 + j, (*chip, c), me).wait_recv()
                fwd = copy(a, 4 + j, (*chip, c), sibling)
                fwd.start()
                passed.append(fwd)
        for a in range(na):
            copy(a, 0, sibling, me).wait_recv()
            for j, chip in enumerate(chips):
                copy(a, 4 + j, (*chip, 1 - c), me).wait_recv()
        for cp in first + passed:
            cp.wait_send()
        for cp in mine:
            cp.wait()

    return pl.pallas_call(
        body, name=name, out_shape=[_sds((N_DEV, *s.shape), s.dtype) for s in shards],
        in_specs=[ANY] * na, out_specs=[ANY] * na,
        scratch_shapes=[pltpu.SemaphoreType.DMA((na, 7)), pltpu.SemaphoreType.DMA((na, 7)),
                        pltpu.SemaphoreType.DMA((na,))],
    )(*shards)


def _exchange_shards(name, grads, L):
    nw = len(grads)
    na = nw * L
    flat = [g for per_layer in grads for g in per_layer]

    def body(*refs):
        ins, outs = refs[:na], refs[na:na + nw]
        send_sems, recv_sems, local_sems = refs[na + nw:]
        x, y, c = _place()
        me = _lin(x, y, c)
        peers = [(x ^ ((k + 1) >> 2 & 1), y ^ ((k + 1) >> 1 & 1), c ^ ((k + 1) & 1)) for k in range(7)]

        def copy(a, k, src_blk, dst_blk):
            return pltpu.make_async_remote_copy(src_ref=ins[a].at[src_blk], dst_ref=outs[a // L].at[a % L, dst_blk],
                                                send_sem=send_sems.at[a, k], recv_sem=recv_sems.at[a, k],
                                                device_id=peers[k], device_id_type=MESH)

        mine = [pltpu.make_async_copy(ins[a].at[me], outs[a // L].at[a % L, me], local_sems.at[a]) for a in range(na)]
        for cp in mine:
            cp.start()
        sent = [copy(a, k, _lin(*peers[k]), me) for a in range(na) for k in range(7)]
        for cp in sent:
            cp.start()
        for a in range(na):
            for k in range(7):
                copy(a, k, me, _lin(*peers[k])).wait_recv()
        for cp in sent:
            cp.wait_send()
        for cp in mine:
            cp.wait()

    return pl.pallas_call(
        body, name=name, out_shape=[_sds((L, *per_layer[0].shape), per_layer[0].dtype) for per_layer in grads],
        in_specs=[ANY] * na, out_specs=[ANY] * nw,
        scratch_shapes=[pltpu.SemaphoreType.DMA((na, 7)), pltpu.SemaphoreType.DMA((na, 7)),
                        pltpu.SemaphoreType.DMA((na,))],
    )(*flat)


MOD_ROWS = 16
MOD_SHARD = 6 * D // N_DEV
HI = lax.Precision.HIGHEST


def _mod_fwd(name, c9, w_mod, b_shard):
    L = w_mod.shape[0]

    def kern(c_ref, w_ref, b_ref, o_ref):
        o_ref[...] = lax.dot_general(_silu(c_ref[...]), w_ref[...], NN, precision=HI,
                                     preferred_element_type=F32) + b_ref[...]

    return pl.pallas_call(
        kern, name=name, grid=(L,),
        in_specs=[_full_spec(c9.shape), pl.BlockSpec((None, D, MOD_SHARD), lambda l: (l, 0, 0)),
                  pl.BlockSpec((None, 1, MOD_SHARD), lambda l: (l, 0, 0))],
        out_specs=pl.BlockSpec((None, MOD_ROWS, MOD_SHARD), lambda l: (l, 0, 0)),
        out_shape=_sds((L, MOD_ROWS, MOD_SHARD), F32), compiler_params=_params(),
    )(c9, w_mod, b_shard)


def _mod_bwd(name, c9, w_mod, dmod_all, dmod_cols):
    L = w_mod.shape[0]

    def rows9(ref, l):
        own = jnp.concatenate([ref[j, 2 * l + 1:2 * l + 2, :] for j in range(N_DEV)], axis=0)
        ctx = ref[0, 2 * l:2 * l + 1, :]
        for j in range(1, N_DEV):
            ctx = ctx + ref[j, 2 * l:2 * l + 1, :]
        return own, ctx

    def kern(c_ref, w_ref, all_ref, cols_ref, gw_ref, gb_ref, gc_ref):
        l = pl.program_id(0)
        for ll in range(L):
            @pl.when(l == ll)
            def _():
                own, ctx = rows9(all_ref, ll)
                gb_ref[...] = _colsum(own) + ctx
                own_s, ctx_s = rows9(cols_ref, ll)
                r16 = jnp.concatenate([own_s, ctx_s, jnp.zeros((MOD_ROWS - N_DEV - 1, MOD_SHARD), F32)], axis=0)
                gw_ref[...] = lax.dot_general(_silu(c_ref[...]), r16, TN, precision=HI, preferred_element_type=F32)
                part = lax.dot_general(r16, w_ref[...], NT, precision=HI,
                                       preferred_element_type=F32)[N_DEV:N_DEV + 1, :]
                if ll == 0:
                    gc_ref[...] = part
                else:
                    gc_ref[...] += part

    return pl.pallas_call(
        kern, name=name, grid=(L,),
        in_specs=[_full_spec(c9.shape), pl.BlockSpec((None, D, MOD_SHARD), lambda l: (l, 0, 0)),
                  _full_spec(dmod_all.shape), _full_spec(dmod_cols.shape)],
        out_specs=[pl.BlockSpec((None, D, MOD_SHARD), lambda l: (l, 0, 0)),
                   pl.BlockSpec((None, 1, 6 * D), lambda l: (l, 0, 0)), _full_spec((1, D))],
        out_shape=[_sds((L, D, MOD_SHARD), F32), _sds((L, 1, 6 * D), F32), _sds((1, D), F32)],
        compiler_params=_params(),
    )(c9, w_mod, dmod_all, dmod_cols)


_BC1 = 1.0 - ADAM_B1 ** ADAM_STEP
_BC2 = 1.0 - ADAM_B2 ** ADAM_STEP


def _adamw_vals(w, g, m, v):
    m = ADAM_B1 * m + (1.0 - ADAM_B1) * g
    v = ADAM_B2 * v + (1.0 - ADAM_B2) * (g * g)
    delta = -ADAM_LR * ((m / _BC1) / (jnp.sqrt(v / _BC2) + ADAM_EPS) + ADAM_WD * w)
    return delta, m, v


def _adamw(name, w, g, m, v, tile):
    R, C = w.shape
    blk = ((tile, C), lambda i: (i, 0))

    def body(i, ins, ps, outs, acc):
        d, mm, vv = _adamw_vals(ins[0][...], ins[1][...], ins[2][...], ins[3][...])
        outs[0][...] = d
        outs[1][...] = mm
        outs[2][...] = vv

    return _ew(name, body, R // tile, [(a, *blk) for a in (w, g, m, v)], [], [(_sds((R, C), F32), *blk)] * 3)


def _sum_slots(ref):
    g = ref[0].astype(F32)
    for j in range(1, N_DEV):
        g = g + ref[j].astype(F32)
    return g


def _adamw_slots(name, slots, w, m, v, tile):
    L, R, C = w.shape
    n = R // tile
    spec = pl.BlockSpec((None, tile, C), lambda l, i: (l, i, 0))

    def slot_spec(ll):
        return pl.BlockSpec((N_DEV, tile, C), lambda l, i: (0, jnp.where(l == ll, i, jnp.where(l < ll, 0, n - 1)), 0))

    def kern(*refs):
        s_refs = refs[:L]
        w_ref, m_ref, v_ref, g_ref, d_ref, mo_ref, vo_ref = refs[L:]
        l = pl.program_id(0)
        for ll in range(L):
            @pl.when(l == ll)
            def _():
                g = _sum_slots(s_refs[ll])
                g_ref[...] = g
                d_ref[...], mo_ref[...], vo_ref[...] = _adamw_vals(w_ref[...], g, m_ref[...], v_ref[...])

    return pl.pallas_call(
        kern, name=name, grid=(L, n),
        in_specs=[slot_spec(ll) for ll in range(L)] + [spec, spec, spec],
        out_specs=[spec] * 4, out_shape=[_sds((L, R, C), F32)] * 4,
        compiler_params=_params(("arbitrary", "arbitrary")),
    )(*slots, w, m, v)


def _sum_blocks(name, blocks):
    _, R, C = blocks.shape

    def kern(b_ref, o_ref):
        o_ref[...] = _sum_slots(b_ref)

    return pl.pallas_call(kern, name=name, in_specs=[_full_spec(blocks.shape)], out_specs=_full_spec((R, C)),
                          grid=(1,), out_shape=_sds((R, C), F32), compiler_params=_params())(blocks)


BIG = ("win_t", "wo_rnn", "wo_attn", "wout", "wffn_in_t", "wffn_out")
BIG_SRC = ("w_in", "w_o_rnn", "w_o_attn", "w_out", "w_ffn_in", "w_ffn_out")
BIG_T = (True, False, False, False, True, False)
BIG_TILE = (176, 128, 128, 128, 176, 176)


def _chan_full(g8):
    return jnp.transpose(g8, (1, 0, 2)).reshape(g8.shape[1], D)


def kernel(x, c, ctx, c_ctx, w_mod, b_mod, g_mix_pre, g_mix_post, g_ffn_pre, g_ffn_post, w_in, conv_w, conv_b, lru_wa, lru_ba, lru_wx, lru_bx, lru_lam, attn_sink, w_o_rnn, w_o_attn, w_out, w_ffn_in, w_ffn_out, loss_target, m_c_ctx, m_w_mod, m_b_mod, m_g_mix_pre, m_g_mix_post, m_g_ffn_pre, m_g_ffn_post, m_w_in, m_conv_w, m_conv_b, m_lru_wa, m_lru_ba, m_lru_wx, m_lru_bx, m_lru_lam, m_attn_sink, m_w_o_rnn, m_w_o_attn, m_w_out, m_w_ffn_in, m_w_ffn_out, v_c_ctx, v_w_mod, v_b_mod, v_g_mix_pre, v_g_mix_post, v_g_ffn_pre, v_g_ffn_post, v_w_in, v_conv_w, v_conv_b, v_lru_wa, v_lru_ba, v_lru_wx, v_lru_bx, v_lru_lam, v_attn_sink, v_w_o_rnn, v_w_o_attn, v_w_out, v_w_ffn_in, v_w_ffn_out):
    P = dict(c_ctx=c_ctx, w_mod=w_mod, b_mod=b_mod, g_mix_pre=g_mix_pre, g_mix_post=g_mix_post, g_ffn_pre=g_ffn_pre,
             g_ffn_post=g_ffn_post, w_in=w_in, conv_w=conv_w, conv_b=conv_b, lru_wa=lru_wa, lru_ba=lru_ba,
             lru_wx=lru_wx, lru_bx=lru_bx, lru_lam=lru_lam, attn_sink=attn_sink, w_o_rnn=w_o_rnn, w_o_attn=w_o_attn,
             w_out=w_out, w_ffn_in=w_ffn_in, w_ffn_out=w_ffn_out)
    Mo = dict(c_ctx=m_c_ctx, w_mod=m_w_mod, b_mod=m_b_mod, g_mix_pre=m_g_mix_pre, g_mix_post=m_g_mix_post,
              g_ffn_pre=m_g_ffn_pre, g_ffn_post=m_g_ffn_post, w_in=m_w_in, conv_w=m_conv_w, conv_b=m_conv_b,
              lru_wa=m_lru_wa, lru_ba=m_lru_ba, lru_wx=m_lru_wx, lru_bx=m_lru_bx, lru_lam=m_lru_lam,
              attn_sink=m_attn_sink, w_o_rnn=m_w_o_rnn, w_o_attn=m_w_o_attn, w_out=m_w_out, w_ffn_in=m_w_ffn_in,
              w_ffn_out=m_w_ffn_out)
    Vo = dict(c_ctx=v_c_ctx, w_mod=v_w_mod, b_mod=v_b_mod, g_mix_pre=v_g_mix_pre, g_mix_post=v_g_mix_post,
              g_ffn_pre=v_g_ffn_pre, g_ffn_post=v_g_ffn_post, w_in=v_w_in, conv_w=v_conv_w, conv_b=v_conv_b,
              lru_wa=v_lru_wa, lru_ba=v_lru_ba, lru_wx=v_lru_wx, lru_bx=v_lru_bx, lru_lam=v_lru_lam,
              attn_sink=v_attn_sink, w_o_rnn=v_w_o_rnn, w_o_attn=v_w_o_attn, w_out=v_w_out, w_ffn_in=v_w_ffn_in,
              w_ffn_out=v_w_ffn_out)
    L = w_in.shape[0]
    S = x.shape[1]
    me = _lin(*_place())

    small = jnp.concatenate([c.reshape(8, 128), conv_w.reshape(L * CONV_W, 128), lru_ba.reshape(2 * L, 128),
                             lru_bx.reshape(2 * L, 128), lru_lam.reshape(2 * L, 128), jnp.zeros((4, 128), F32)], axis=0)
    small_all = _allgather_small("ag_small", small)
    c_all = small_all[:, 0:8].reshape(N_DEV, D)
    conv_w_f = _chan_full(small_all[:, 8:16]).reshape(L, CONV_W, D)
    lru_ba_f = _chan_full(small_all[:, 16:20]).reshape(L, 2, D)
    lru_bx_f = _chan_full(small_all[:, 20:24]).reshape(L, 2, D)
    lru_lam_f = _chan_full(small_all[:, 24:28]).reshape(L, 2, D)

    c9 = jnp.concatenate([c_all, c_ctx[None], jnp.zeros((MOD_ROWS - N_DEV - 1, D), F32)], axis=0)
    b_shard = lax.dynamic_slice_in_dim(b_mod, me * MOD_SHARD, MOD_SHARD, axis=1)[:, None, :]
    mod_part = _mod_fwd("mod_fwd", c9, w_mod, b_shard)
    mod_all = _allgather_small("ag_mod", mod_part.reshape(L * MOD_ROWS, MOD_SHARD))
    mod_all = jnp.transpose(mod_all.reshape(N_DEV, L, MOD_ROWS, MOD_SHARD), (1, 2, 0, 3)).reshape(L, MOD_ROWS, 6 * D)
    own_row = lax.dynamic_index_in_dim(mod_all, me, axis=1, keepdims=False)
    modrows = jnp.stack([mod_all[:, N_DEV], own_row], axis=1)

    shards = [{k: (P[src][l].T if tr else P[src][l]).astype(BF16) for k, src, tr in zip(BIG, BIG_SRC, BIG_T)}
              for l in range(L)]
    win0, = _allgather_hbm("ag_w_in0", [shards[0]["win_t"]])
    Ws = []
    for l in range(L):
        W = {"win_t": win0.reshape(-1, D)} if l == 0 else {}
        W.update(
            cw=conv_w_f[l], cb=conv_b[l][None],
            w4=jnp.concatenate([lru_wa[l, 0], lru_wa[l, 1], lru_wx[l, 0], lru_wx[l, 1]], axis=-1).astype(BF16),
            b4=jnp.concatenate([lru_ba_f[l, 0].reshape(N_RNN_BLOCKS, 1, RB), lru_ba_f[l, 1].reshape(N_RNN_BLOCKS, 1, RB),
                                lru_bx_f[l, 0].reshape(N_RNN_BLOCKS, 1, RB), lru_bx_f[l, 1].reshape(N_RNN_BLOCKS, 1, RB)],
                               axis=-1),
            lam=lru_lam_f[l], sink4=jnp.broadcast_to(attn_sink[l].reshape(N_KV, Q_PER_KV, 1), (N_KV, Q_PER_KV, HEAD)),
            g_mix_pre=g_mix_pre[l][None], g_mix_post=g_mix_post[l][None], g_ffn_pre=g_ffn_pre[l][None],
            g_ffn_post=g_ffn_post[l][None], mod=modrows[l])
        Ws.append(W)

    xa = jnp.concatenate([ctx[0], x[0]], axis=0)
    plan = _Plan(shards, Ws)
    sq, dxa, Gs = _local_step(xa, loss_target[0], Ws, S, plan)
    loss = lax.psum((0.5 / D) * jnp.sum(sq), ("x", "y", "c"))
    grad_x = dxa[CTX:][None]

    dmod = jnp.concatenate([Gs[l]["mod"] for l in range(L)] + [jnp.zeros((8 - 2 * L, 6 * D), F32)], axis=0)
    dmod_all = _allgather_small("ag_dmod", dmod)
    dmod_cols = lax.dynamic_slice_in_dim(dmod_all, me * MOD_SHARD, MOD_SHARD, axis=2)
    g_w_mod, g_b_mod, dsc_part = _mod_bwd("mod_bwd", c9, w_mod, dmod_all, dmod_cols)
    g_b_mod = g_b_mod[:, 0]

    def rows(name, shape):
        return jnp.concatenate([Gs[l][name].reshape(shape) for l in range(L)], axis=0)

    b4g = [Gs[l]["b4"].reshape(N_RNN_BLOCKS, 4, RB) for l in range(L)]
    sink_row = jnp.concatenate([Gs[l]["sink4"][:, :, 0].reshape(1, N_Q) for l in range(L)]
                               + [jnp.zeros((1, D - L * N_Q), F32)], axis=1)
    small_g = jnp.concatenate(
        [rows("g_mix_pre", (1, D)), rows("g_mix_post", (1, D)), rows("g_ffn_pre", (1, D)), rows("g_ffn_post", (1, D)),
         rows("cb", (1, D)), rows("cw", (CONV_W, D))]
        + [b4g[l][:, d].reshape(1, D) for l in range(L) for d in range(2)]
        + [b4g[l][:, 2 + d].reshape(1, D) for l in range(L) for d in range(2)]
        + [rows("lam", (2, D)), sink_row, dsc_part], axis=0)
    n_small = small_g.shape[0]
    small_tot = _sum_blocks("sum_small", _allgather_small("ag_small_grads", small_g))
    o = 0
    G = {}
    for name in ("g_mix_pre", "g_mix_post", "g_ffn_pre", "g_ffn_post", "conv_b"):
        G[name] = small_tot[o:o + L]
        o += L
    G["conv_w"] = small_tot[o:o + L * CONV_W].reshape(L, CONV_W, D)
    o += L * CONV_W
    for name in ("lru_ba", "lru_bx", "lru_lam"):
        G[name] = small_tot[o:o + 2 * L].reshape(L, 2, D)
        o += 2 * L
    G["attn_sink"] = small_tot[o, :L * N_Q].reshape(L, N_Q)
    sg = jax.nn.sigmoid(c_ctx)
    G["c_ctx"] = small_tot[o + 1] * (sg * (1.0 + c_ctx * (1.0 - sg)))
    G["b_mod"] = g_b_mod
    G["w_mod"] = g_w_mod

    last_slots, = _exchange_shards("exchange_w_in0", [[Gs[0]["win_t"].reshape(N_DEV, -1, D)]], 1)
    plan.slots[0]["win_t"] = last_slots[0]

    out_g, out_d, out_m, out_v = {}, {}, {}, {}

    def put(name, res, shape=None):
        g, d, m, v = res
        for dst, val in ((out_g, g), (out_d, d), (out_m, m), (out_v, v)):
            dst[name] = val if shape is None else val.reshape(shape)

    for k, src, tr, tile in zip(BIG, BIG_SRC, BIG_T, BIG_TILE):
        lay = (lambda a: jnp.swapaxes(a, 1, 2)) if tr else (lambda a: a)
        res = _adamw_slots("adamw_" + src, [plan.slots[l][k] for l in range(L)], lay(P[src]), lay(Mo[src]),
                           lay(Vo[src]), tile)
        put(src, [lay(r) for r in res])
    res = _adamw("adamw_w_mod", w_mod.reshape(L * D, MOD_SHARD), g_w_mod.reshape(L * D, MOD_SHARD),
                 m_w_mod.reshape(L * D, MOD_SHARD), v_w_mod.reshape(L * D, MOD_SHARD), 256)
    put("w_mod", (g_w_mod,) + tuple(res), w_mod.shape)
    def fuse4(wa, wx):
        return jnp.concatenate([wa[:, 0], wa[:, 1], wx[:, 0], wx[:, 1]], axis=-1).reshape(L, N_RNN_BLOCKS * RB, 4 * RB)

    res = _adamw_slots("adamw_gates", plan.gate_slots,
                       fuse4(lru_wa, lru_wx), fuse4(m_lru_wa, m_lru_wx), fuse4(v_lru_wa, v_lru_wx), 256)
    res = [r.reshape(L, N_RNN_BLOCKS, RB, 4, RB) for r in res]
    put("lru_wa", [jnp.stack([r[:, :, :, 0], r[:, :, :, 1]], axis=1) for r in res])
    put("lru_wx", [jnp.stack([r[:, :, :, 2], r[:, :, :, 3]], axis=1) for r in res])
    rep = ("g_mix_pre", "g_mix_post", "g_ffn_pre", "g_ffn_post", "conv_b", "b_mod")

    def pack_rep(T_):
        sink = jnp.concatenate([T_["attn_sink"].reshape(1, L * N_Q), jnp.zeros((1, D - L * N_Q), F32)], axis=1)
        return jnp.concatenate([T_[n].reshape(-1, D) for n in rep] + [sink, T_["c_ctx"][None]], axis=0)

    pk = [pack_rep(T_) for T_ in (P, G, Mo, Vo)]
    n_rep = pk[0].shape[0]
    res = _adamw("adamw_replicated", *[jnp.pad(a, ((0, 24 - n_rep), (0, 0))) for a in pk], 24)
    res = (pk[1],) + tuple(r[:n_rep] for r in res)
    o = 0
    for n in rep:
        k = P[n].size // D
        put(n, [r[o:o + k] for r in res], P[n].shape)
        o += k
    put("attn_sink", [r[o, :L * N_Q] for r in res], attn_sink.shape)
    put("c_ctx", [r[o + 1] for r in res], c_ctx.shape)
    chan = ("conv_w", "lru_ba", "lru_bx", "lru_lam")
    g_own = {n: lax.dynamic_slice_in_dim(G[n], me * RB, RB, axis=2) for n in chan}

    def pack_chan(T_):
        return jnp.concatenate([T_[n].reshape(-1, RB) for n in chan], axis=0)

    pk = [pack_chan(T_) for T_ in (P, g_own, Mo, Vo)]
    n_ch = pk[0].shape[0]
    res = _adamw("adamw_channels", *[jnp.pad(a, ((0, 24 - n_ch), (0, 0))) for a in pk], 24)
    res = (pk[1],) + tuple(r[:n_ch] for r in res)
    o = 0
    for n in chan:
        k = P[n].size // RB
        put(n, [r[o:o + k] for r in res], P[n].shape)
        o += k

    order = ("c_ctx", "w_mod", "b_mod", "g_mix_pre", "g_mix_post", "g_ffn_pre", "g_ffn_post", "w_in", "conv_w", "conv_b",
             "lru_wa", "lru_ba", "lru_wx", "lru_bx", "lru_lam", "attn_sink", "w_o_rnn", "w_o_attn", "w_out", "w_ffn_in",
             "w_ffn_out")
    return (loss, grad_x, *[out_g[n] for n in order], *[out_d[n] for n in order], *[out_m[n] for n in order],
            *[out_v[n] for n in order])
```

```python
import functools
import math

import numpy as np
import jax
import jax.numpy as jnp
from jax import lax
from jax.experimental import pallas as pl
from jax.experimental.pallas import tpu as pltpu

F32 = jnp.float32
BF16 = jnp.bfloat16

D = 1024
CTX = 256
TR = 256
HEAD = 128
N_Q = 8
N_KV = 2
Q_PER_KV = N_Q // N_KV
GRID_W = 64
N_FREQ = HEAD // 4
ROPE_BASE = 10000.0
N_RNN_BLOCKS = 8
CONV_W = 4
CONV_LEFT = 2
LRU_C = 8.0
D_FF = 2816
IN_W = 5632
P_W = IN_W
COL_XR, COL_GR, COL_Q, COL_K, COL_V, COL_GL = 0, 1024, 2048, 3072, 3328, 3584
GLB = 512
EPS = 1e-6
NEG_INF = -1e30
ATT_SCALE = HEAD ** -0.5
N_DEV = 8
VMEM_LIMIT = 56 * 1024 * 1024

ADAM_LR, ADAM_B1, ADAM_B2, ADAM_EPS, ADAM_WD, ADAM_STEP = 0.001, 0.9, 0.999, 1e-08, 0.01, 10

NN = (((1,), (0,)), ((), ()))
NT = (((1,), (1,)), ((), ()))
TN = (((0,), (0,)), ((), ()))


def _dot(a, b, dims=NN):
    return lax.dot_general(a, b, dims, preferred_element_type=F32)


def _params(sem=("arbitrary",)):
    return pltpu.CompilerParams(dimension_semantics=sem, vmem_limit_bytes=VMEM_LIMIT)


def _full_spec(shape):
    nd = len(shape)
    return pl.BlockSpec(shape, lambda *_: (0,) * nd)


ANY = pl.BlockSpec(memory_space=pl.ANY)


def _ew(name, body, n, row_ins, pars, row_outs, accs=(), alias=None):
    n_ri, n_p, n_ro, n_acc = len(row_ins), len(pars), len(row_outs), len(accs)

    def kern(*refs):
        i = pl.program_id(0)
        ins = refs[:n_ri]
        ps = refs[n_ri:n_ri + n_p]
        outs = refs[n_ri + n_p:n_ri + n_p + n_ro]
        acc = refs[n_ri + n_p + n_ro:]
        if n_acc:
            @pl.when(i == 0)
            def _():
                for a in acc:
                    a[...] = jnp.zeros(a.shape, a.dtype)
        body(i, ins, ps, outs, acc)

    in_specs = [ANY if blk is None else pl.BlockSpec(blk, imap) for (_, blk, imap) in row_ins]
    in_specs += [_full_spec(p.shape) for p in pars]
    out_specs = [pl.BlockSpec(blk, imap) for (_, blk, imap) in row_outs] + [_full_spec(a.shape) for a in accs]
    out_shape = [s for (s, _, _) in row_outs] + list(accs)
    return pl.pallas_call(
        kern, name=name, grid=(n,), in_specs=in_specs, out_specs=out_specs, out_shape=out_shape,
        input_output_aliases=alias or {}, compiler_params=_params(),
    )(*[a for (a, _, _) in row_ins], *pars)


def _rowblk(width, colblk=0, roff=0, tile=TR):
    return (tile, width), (lambda i: (i + roff, colblk))


def _sds(shape, dtype):
    return jax.ShapeDtypeStruct(shape, dtype)


class _Carry:
    SAME_CORE = (1, 3, 5)

    def __init__(self, jobs):
        self.jobs = list(jobs)
        self.arrays = [a for _, a in self.jobs]
        self.out_shapes = [_sds(a.shape if kind == "scatter" else (N_DEV, *a.shape), a.dtype) for kind, a in self.jobs]
        n = len(self.jobs)
        self.scratch = [pltpu.SemaphoreType.DMA((n, 7)), pltpu.SemaphoreType.DMA((n, 7)), pltpu.SemaphoreType.DMA((n,))]

    def _setup(self, sems):
        send_sems, recv_sems, local_sems = sems
        x, y, c = _place()
        me = _lin(x, y, c)
        peers = [(x ^ ((k + 1) >> 2 & 1), y ^ ((k + 1) >> 1 & 1), c ^ ((k + 1) & 1)) for k in range(7)]

        def copy(a, k, sem_k, src, dst):
            return pltpu.make_async_remote_copy(src_ref=src, dst_ref=dst, send_sem=send_sems.at[a, sem_k],
                                                recv_sem=recv_sems.at[a, sem_k], device_id=peers[k], device_id_type=MESH)

        return me, [_lin(*p) for p in peers], copy, local_sems

    def _local(self, a, kind, ins, outs, me, local_sems):
        return pltpu.make_async_copy(ins[a].at[me] if kind == "scatter" else ins[a], outs[a].at[me], local_sems.at[a])

    def start(self, ins, outs, sems):
        me, theirs, copy, local_sems = self._setup(sems)
        for a, (kind, _) in enumerate(self.jobs):
            self._local(a, kind, ins, outs, me, local_sems).start()
            if kind == "scatter":
                for k in range(7):
                    copy(a, k, k, ins[a].at[theirs[k]], outs[a].at[me]).start()
            else:
                for k in (0,) + self.SAME_CORE:
                    copy(a, k, k, ins[a], outs[a].at[me]).start()

    def wait(self, ins, outs, sems):
        me, theirs, copy, local_sems = self._setup(sems)
        for a, (kind, _) in enumerate(self.jobs):
            if kind == "scatter":
                for k in range(7):
                    copy(a, k, k, ins[a].at[me], outs[a].at[theirs[k]]).wait_recv()
                for k in range(7):
                    copy(a, k, k, ins[a].at[theirs[k]], outs[a].at[me]).wait_send()
            else:
                for k in self.SAME_CORE:
                    blk = outs[a].at[theirs[k]]
                    copy(a, k, k, ins[a], blk).wait_recv()
                    copy(a, 0, k + 1, blk, blk).start()
                copy(a, 0, 0, ins[a], outs[a].at[theirs[0]]).wait_recv()
                for k in self.SAME_CORE:
                    copy(a, 0, k + 1, ins[a], outs[a].at[theirs[k + 1]]).wait_recv()
                for k in (0,) + self.SAME_CORE:
                    copy(a, k, k, ins[a], outs[a].at[me]).wait_send()
                for k in self.SAME_CORE:
                    blk = outs[a].at[theirs[k]]
                    copy(a, 0, k + 1, blk, blk).wait_send()
            self._local(a, kind, ins, outs, me, local_sems).wait()


def _carried(kern, carry, n_in, n_out, first, last):
    if carry is None:
        return kern
    nc = len(carry.jobs)

    def wrapped(*refs):
        ins, cin = refs[:n_in], refs[n_in:n_in + nc]
        outs, cout = refs[n_in + nc:n_in + nc + n_out], refs[n_in + nc + n_out:n_in + 2 * nc + n_out]
        scr, sems = refs[n_in + 2 * nc + n_out:-3], refs[-3:]

        @pl.when(first())
        def _():
            carry.start(cin, cout, sems)

        kern(*ins, *outs, *scr)

        @pl.when(last())
        def _():
            carry.wait(cin, cout, sems)

    return wrapped


def _carry_args(carry):
    if carry is None:
        return [], [], [], [], []
    n = len(carry.jobs)
    return [ANY] * n, carry.arrays, [ANY] * n, carry.out_shapes, carry.scratch


def _grid_ends(dims):
    first = lambda: functools.reduce(jnp.logical_and, [pl.program_id(d) == 0 for d in range(len(dims))])
    last = lambda: functools.reduce(jnp.logical_and, [pl.program_id(d) == n - 1 for d, n in enumerate(dims)])
    return first, last


def _mm_call(name, a, b, mode, out_dtype, tm, tn, rows_outer=True, single_b=False, carry=None):
    if mode == "TN":
        (K, M), N = a.shape, b.shape[1]
    else:
        (M, K), N = a.shape, (b.shape[1] if mode == "NN" else b.shape[0])
    assert M % tm == 0 and N % tn == 0, (name, M, N, K, tm, tn)
    ij = (lambda g0, g1: (g0, g1)) if rows_outer else (lambda g0, g1: (g1, g0))
    grid = (M // tm, N // tn) if rows_outer else (N // tn, M // tm)
    if mode == "TN":
        a_spec = pl.BlockSpec((K, tm), lambda g0, g1: (0, ij(g0, g1)[0]))
    else:
        a_spec = pl.BlockSpec((tm, K), lambda g0, g1: (ij(g0, g1)[0], 0))
    b_blk, b_map = ((tn, K), lambda g0, g1: (ij(g0, g1)[1], 0)) if mode == "NT" else \
                   ((K, tn), lambda g0, g1: (0, ij(g0, g1)[1]))
    b_spec = pl.BlockSpec(b_blk, b_map, pipeline_mode=pl.Buffered(1)) if single_b else pl.BlockSpec(b_blk, b_map)
    dims = {"NN": NN, "NT": NT, "TN": TN}[mode]

    def kern(a_ref, b_ref, o_ref):
        o_ref[...] = _dot(a_ref[...], b_ref[...], dims).astype(o_ref.dtype)

    ci, ca, co, cs, cscr = _carry_args(carry)
    res = pl.pallas_call(
        _carried(kern, carry, 2, 1, *_grid_ends(grid)), name=name, grid=grid, in_specs=[a_spec, b_spec] + ci,
        out_specs=[pl.BlockSpec((tm, tn), lambda g0, g1: ij(g0, g1))] + co,
        out_shape=[_sds((M, N), out_dtype)] + cs, scratch_shapes=cscr,
        compiler_params=_params(("arbitrary", "arbitrary")),
    )(a, b, *ca)
    return res[0] if carry is None else (res[0], res[1:])


def _mm_act(name, a, w, mode, out_dtype=F32, carry=None):
    rows, K = a.shape
    N = w.shape[1] if mode == "NN" else w.shape[0]
    if K > D_FF:
        return _mm_call(name, a, w, mode, out_dtype, rows // 8, N, single_b=True, carry=carry)
    tn = N if N <= 1024 else 1408
    return _mm_call(name, a, w, mode, out_dtype, rows // 4, tn, carry=carry)


def _mm_wgrad(name, x, dy, out_dtype=BF16, carry=None):
    M = x.shape[1]
    tm = 1408 if M == D_FF else 512
    return _mm_call(name, x, dy, "TN", out_dtype, tm, dy.shape[1], single_b=True, carry=carry)


def _sigmoid(x):
    return 0.5 * jnp.tanh(0.5 * x) + 0.5


def _silu(x):
    return x * _sigmoid(x)


def _silu_grad(x):
    s = _sigmoid(x)
    return s * (1.0 + x * (1.0 - s))


_GELU_K = math.sqrt(2.0 / math.pi)


def _gelu(x):
    return 0.5 * x * (1.0 + jnp.tanh(_GELU_K * (x + 0.044715 * x * x * x)))


def _gelu_grad(x):
    t = jnp.tanh(_GELU_K * (x + 0.044715 * x * x * x))
    return 0.5 * (1.0 + t) + 0.5 * x * (1.0 - t * t) * _GELU_K * (1.0 + 3.0 * 0.044715 * x * x)


def _log_sigmoid(x):
    return jnp.minimum(x, 0.0) - jnp.log(1.0 + jnp.exp(-jnp.abs(x)))


def _rms(x):
    r = lax.rsqrt(jnp.mean(x * x, axis=-1, keepdims=True) + EPS)
    return x * r, r


def _rms_bwd(dy, y, r):
    return r * (dy - y * jnp.mean(dy * y, axis=-1, keepdims=True))


def _modrow(mod_ref, i, chunk):
    lo = mod_ref[0:1, chunk * D:(chunk + 1) * D]
    hi = mod_ref[1:2, chunk * D:(chunk + 1) * D]
    return jnp.where(i == 0, lo, hi)


def _acc_seg(acc_ref, i, val):
    zero = jnp.zeros_like(val)
    acc_ref[0:1, :] += jnp.where(i == 0, val, zero)
    acc_ref[1:2, :] += jnp.where(i == 0, zero, val)


def _colsum(x):
    return jnp.sum(x, axis=0, keepdims=True)


SH1, SC1, GA1, SH2, SC2, GA2 = range(6)


def _normmod_fwd(name, xa, g, mod, c_sh, c_sc):
    T = xa.shape[0]

    def body(i, ins, ps, outs, acc):
        y, _ = _rms(ins[0][...])
        h = (y * ps[0][...]) * (1.0 + _modrow(ps[1], i, c_sc)) + _modrow(ps[1], i, c_sh)
        outs[0][...] = h.astype(BF16)

    return _ew(name, body, T // TR, [(xa, *_rowblk(D))], [g, mod], [(_sds((T, D), BF16), *_rowblk(D))])[0]


def _resid_norm_fwd(name, xin, mat, gpost, mod, c_ga, gnext, modn, c_sh, c_sc):
    T = xin.shape[0]

    def body(i, ins, ps, outs, acc):
        ym, _ = _rms(ins[1][...])
        xo = ins[0][...] + _modrow(ps[1], i, c_ga) * (ym * ps[0][...])
        outs[0][...] = xo
        y, _ = _rms(xo)
        h = (y * ps[2][...]) * (1.0 + _modrow(ps[3], i, c_sc)) + _modrow(ps[3], i, c_sh)
        outs[1][...] = h.astype(BF16)

    return _ew(name, body, T // TR, [(xin, *_rowblk(D)), (mat, *_rowblk(D))], [gpost, mod, gnext, modn],
               [(_sds((T, D), F32), *_rowblk(D)), (_sds((T, D), BF16), *_rowblk(D))])


def _resid_loss_fwd(name, xin, mat, gpost, mod, c_ga, target):
    T = xin.shape[0]

    def body(i, ins, ps, outs, acc):
        ym, _ = _rms(ins[1][...])
        xo = ins[0][...] + _modrow(ps[1], i, c_ga) * (ym * ps[0][...])
        err = xo - ins[2][...]
        lat = i > 0
        outs[0][...] = jnp.where(lat, err * (1.0 / D), 0.0)
        acc[0][...] += jnp.where(lat, _colsum(err * err), 0.0)

    tgt_blk = ((TR, D), lambda i: (jnp.maximum(i - 1, 0), 0))
    dx, sq = _ew(name, body, T // TR, [(xin, *_rowblk(D)), (mat, *_rowblk(D)), (target, *tgt_blk)], [gpost, mod],
                 [(_sds((T, D), F32), *_rowblk(D))], [_sds((1, D), F32)])
    return dx, sq


def _resid_bwd_vals(i, dout, mat, gpost, mod_ref, c_ga, acc_ga, acc_g):
    ym, rm = _rms(mat)
    ga = _modrow(mod_ref, i, c_ga)
    _acc_seg(acc_ga, i, _colsum(dout * (ym * gpost)))
    dn = dout * ga
    acc_g[...] += _colsum(dn * ym)
    return _rms_bwd(dn * gpost, ym, rm)


def _normmod_bwd_vals(i, dh, xin, g, mod_ref, c_sh, c_sc, acc_sh, acc_sc, acc_g):
    y, r = _rms(xin)
    _acc_seg(acc_sc, i, _colsum(dh * (y * g)))
    _acc_seg(acc_sh, i, _colsum(dh))
    dyg = dh * (1.0 + _modrow(mod_ref, i, c_sc))
    acc_g[...] += _colsum(dyg * y)
    return _rms_bwd(dyg * g, y, r)


def _resid_bwd(name, dout, mat, gpost, mod, c_ga):
    T = dout.shape[0]

    def body(i, ins, ps, outs, acc):
        dm = _resid_bwd_vals(i, ins[0][...], ins[1][...], ps[0][...], ps[1], c_ga, acc[0], acc[1])
        outs[0][...] = dm.astype(BF16)

    return _ew(name, body, T // TR, [(dout, *_rowblk(D)), (mat, *_rowblk(D))], [gpost, mod],
               [(_sds((T, D), BF16), *_rowblk(D))], [_sds((2, D), F32), _sds((1, D), F32)])


def _normmod_resid_bwd(name, dh, xin, gpre, mod, c_sh, c_sc, dres, mat, gpost, c_ga):
    T = dh.shape[0]

    def body(i, ins, ps, outs, acc):
        dx = ins[2][...] + _normmod_bwd_vals(i, ins[0][...], ins[1][...], ps[0][...], ps[1], c_sh, c_sc,
                                             acc[0], acc[1], acc[2])
        outs[0][...] = dx
        dm = _resid_bwd_vals(i, dx, ins[3][...], ps[2][...], ps[1], c_ga, acc[3], acc[4])
        outs[1][...] = dm.astype(BF16)

    return _ew(name, body, T // TR, [(dh, *_rowblk(D)), (xin, *_rowblk(D)), (dres, *_rowblk(D)), (mat, *_rowblk(D))],
               [gpre, mod, gpost],
               [(_sds((T, D), F32), *_rowblk(D)), (_sds((T, D), BF16), *_rowblk(D))],
               [_sds((2, D), F32), _sds((2, D), F32), _sds((1, D), F32), _sds((2, D), F32), _sds((1, D), F32)])


def _normmod_bwd(name, dh, xin, gpre, mod, c_sh, c_sc, dres):
    T = dh.shape[0]

    def body(i, ins, ps, outs, acc):
        outs[0][...] = ins[2][...] + _normmod_bwd_vals(i, ins[0][...], ins[1][...], ps[0][...], ps[1], c_sh, c_sc,
                                                       acc[0], acc[1], acc[2])

    return _ew(name, body, T // TR, [(dh, *_rowblk(D)), (xin, *_rowblk(D)), (dres, *_rowblk(D))], [gpre, mod],
               [(_sds((T, D), F32), *_rowblk(D))], [_sds((2, D), F32), _sds((2, D), F32), _sds((1, D), F32)])


def _gate_fwd(name, p, ya, yb):
    T = ya.shape[0]

    def body(i, ins, ps, outs, acc):
        gl = [r[...].astype(F32) for r in ins[:4]]
        ga = _sigmoid(jnp.concatenate(gl[:2], axis=1))
        gb = _sigmoid(jnp.concatenate(gl[2:], axis=1))
        outs[0][...] = (ga * ins[4][...] + gb * ins[5][...]).astype(BF16)

    return _ew(name, body, T // TR,
               [(p, *_rowblk(GLB, COL_GL // GLB + q)) for q in range(4)] + [(ya, *_rowblk(D)), (yb, *_rowblk(D))],
               [], [(_sds((T, D), BF16), *_rowblk(D))])[0]


def _gate_bwd(name, p, ya, yb, dz):
    T = ya.shape[0]

    def kern(gl_ref, ya_ref, yb_ref, dz_ref, dya_ref, dyb_ref, dp_ref):
        j = pl.program_id(1)
        g = _sigmoid(gl_ref[...].astype(F32))
        dzv = dz_ref[...]
        dbranch = (dzv * g).astype(BF16)
        dg = dzv * g * (1.0 - g)

        @pl.when(j < 2)
        def _():
            dya_ref[...] = dbranch
            dp_ref[...] = (dg * ya_ref[...]).astype(BF16)

        @pl.when(j >= 2)
        def _():
            dyb_ref[...] = dbranch
            dp_ref[...] = (dg * yb_ref[...]).astype(BF16)

    rt = T // 4
    first = pl.BlockSpec((rt, GLB), lambda i, j: (i, jnp.minimum(j, 1)))
    second = pl.BlockSpec((rt, GLB), lambda i, j: (i, jnp.maximum(j - 2, 0)))
    return pl.pallas_call(
        kern, name=name, grid=(4, 4),
        in_specs=[pl.BlockSpec((rt, GLB), lambda i, j: (i, COL_GL // GLB + j)), first, second,
                  pl.BlockSpec((rt, GLB), lambda i, j: (i, j % 2))],
        out_specs=[first, second, pl.BlockSpec((rt, GLB), lambda i, j: (i, COL_GL // GLB + j))],
        out_shape=[_sds((T, D), BF16), _sds((T, D), BF16), _sds((T, P_W), BF16)],
        compiler_params=_params(("arbitrary", "arbitrary")),
    )(p, ya, yb, dz)


def _swiglu_fwd(name, f):
    T = f.shape[0]

    def body(i, ins, ps, outs, acc):
        outs[0][...] = (_silu(ins[0][...].astype(F32)) * ins[1][...].astype(F32)).astype(BF16)

    return _ew(name, body, T // TR, [(f, *_rowblk(D_FF, 0)), (f, *_rowblk(D_FF, 1))], [],
               [(_sds((T, D_FF), BF16), *_rowblk(D_FF))])[0]


def _swiglu_bwd(name, f, ds):
    T = f.shape[0]

    def body(i, ins, ps, outs, acc):
        gate, up, dsv = ins[0][...].astype(F32), ins[1][...].astype(F32), ins[2][...].astype(F32)
        dgate = dsv * up * _silu_grad(gate)
        dup = dsv * _silu(gate)
        outs[0][...] = jnp.concatenate([dgate, dup], axis=1).astype(BF16)

    return _ew(name, body, T // TR, [(f, *_rowblk(D_FF, 0)), (f, *_rowblk(D_FF, 1)), (ds, *_rowblk(D_FF))], [],
               [(_sds((T, 2 * D_FF), BF16), *_rowblk(2 * D_FF))])[0]


AB = 128
CTX_BLKS = CTX // AB


def _rope_tables(S):
    pos = jnp.arange(S, dtype=jnp.int32)
    inv = ROPE_BASE ** (-jnp.arange(N_FREQ, dtype=F32) / N_FREQ)
    ang_r = (pos // GRID_W).astype(F32)[:, None] * inv[None, :]
    ang_c = (pos % GRID_W).astype(F32)[:, None] * inv[None, :]
    cos = jnp.concatenate([jnp.cos(ang_r)] * 2 + [jnp.cos(ang_c)] * 2, axis=1)
    sin = jnp.concatenate([-jnp.sin(ang_r), jnp.sin(ang_r), -jnp.sin(ang_c), jnp.sin(ang_c)], axis=1)
    return cos, sin


def _rope(x, cos, sin):
    w = x.shape[1]
    reps = w // HEAD
    lane = lax.broadcasted_iota(jnp.int32, x.shape, 1)
    partner = jnp.where((lane & 63) < 32, pltpu.roll(x, w - 32, 1), pltpu.roll(x, 32, 1))
    return x * jnp.tile(cos, (1, reps)) + partner * jnp.tile(sin, (1, reps))


def _unrope(dx, cos, sin):
    w = dx.shape[1]
    reps = w // HEAD
    lane = lax.broadcasted_iota(jnp.int32, dx.shape, 1)
    t = dx * jnp.tile(sin, (1, reps))
    partner = jnp.where((lane & 63) < 32, pltpu.roll(t, w - 32, 1), pltpu.roll(t, 32, 1))
    return dx * jnp.tile(cos, (1, reps)) + partner


def _qkv_prep(name, p, cos, sin, S):
    T = CTX + S
    nb = S // AB
    KW = N_KV * HEAD

    def kern(q_ref, k_ref, v_ref, cos_ref, sin_ref, qa_ref, kp_ref, vp_ref, kc_ref, vc_ref):
        i = pl.program_id(0)
        cos_v, sin_v = cos_ref[...], sin_ref[...]
        @pl.when(i < CTX_BLKS)
        def _():
            qa_ref[...] = q_ref[...]
            kc_ref[...] = k_ref[...]
            vc_ref[...] = v_ref[...]
            kp_ref[...] = jnp.zeros(kp_ref.shape, BF16)
            vp_ref[...] = jnp.zeros(vp_ref.shape, BF16)

        @pl.when(i >= CTX_BLKS)
        def _():
            qa_ref[...] = _rope(q_ref[...].astype(F32), cos_v, sin_v).astype(BF16)
            kp_ref[...] = _rope(k_ref[...].astype(F32), cos_v, sin_v).astype(BF16)
            vp_ref[...] = v_ref[...]

    lat_map = lambda i: (jnp.maximum(i - CTX_BLKS, 0), 0)
    pad_map = lambda i: (jnp.where(i == 0, 0, jnp.where(i == 1, nb + 1, i - 1)), 0)
    ctx_map = lambda i: (jnp.minimum(i, CTX_BLKS - 1), 0)
    return pl.pallas_call(
        kern, name=name, grid=(T // AB,),
        in_specs=[pl.BlockSpec((AB, N_Q * HEAD), lambda i: (i, COL_Q // (N_Q * HEAD))),
                  pl.BlockSpec((AB, KW), lambda i: (i, COL_K // KW)),
                  pl.BlockSpec((AB, KW), lambda i: (i, COL_V // KW)),
                  pl.BlockSpec((AB, HEAD), lat_map), pl.BlockSpec((AB, HEAD), lat_map)],
        out_specs=[pl.BlockSpec((AB, N_Q * HEAD), lambda i: (i, 0)),
                   pl.BlockSpec((AB, KW), pad_map), pl.BlockSpec((AB, KW), pad_map),
                   pl.BlockSpec((AB, KW), ctx_map), pl.BlockSpec((AB, KW), ctx_map)],
        out_shape=[_sds((T, N_Q * HEAD), BF16), _sds((S + 2 * AB, KW), BF16), _sds((S + 2 * AB, KW), BF16),
                   _sds((CTX, KW), BF16), _sds((CTX, KW), BF16)],
        compiler_params=_params(),
    )(p, p, p, cos, sin)


GQ = Q_PER_KV * AB
GW = Q_PER_KV * HEAD


def _stack_heads(blk):
    return jnp.concatenate([blk[:, g * HEAD:(g + 1) * HEAD] for g in range(Q_PER_KV)], axis=0)


def _unstack_heads(x4):
    return jnp.concatenate([x4[g * AB:(g + 1) * AB, :] for g in range(Q_PER_KV)], axis=1)


def _sink_col(sink_ref):
    return jnp.concatenate([jnp.broadcast_to(sink_ref[g:g + 1, 0:1], (AB, 1)) for g in range(Q_PER_KV)], axis=0)


def _band_mask(n, S):
    r = lax.broadcasted_iota(jnp.int32, (GQ, 3 * AB), 0)
    c = lax.broadcasted_iota(jnp.int32, (GQ, 3 * AB), 1)
    d = c - AB - (r & (AB - 1))
    kpos = n * AB - AB + c
    return (jnp.abs(d) <= AB) & (kpos >= 0) & (kpos < S)


def _attn_probs(q4, kc, sink, kb, mask):
    s_ctx = _dot(q4, kc, NT) * ATT_SCALE
    m = jnp.maximum(jnp.max(s_ctx, axis=-1, keepdims=True), sink)
    if kb is not None:
        s_b = jnp.where(mask, _dot(q4, kb, NT) * ATT_SCALE, NEG_INF)
        m = jnp.maximum(m, jnp.max(s_b, axis=-1, keepdims=True))
    p_ctx = jnp.exp(s_ctx - m)
    p_sink = jnp.exp(sink - m)
    l = jnp.sum(p_ctx, axis=-1, keepdims=True) + p_sink
    p_b = None
    if kb is not None:
        p_b = jnp.exp(s_b - m)
        l = l + jnp.sum(p_b, axis=-1, keepdims=True)
    inv = 1.0 / l
    return p_ctx * inv, (None if p_b is None else p_b * inv), p_sink * inv


def _attn_fwd(name, qa, kc, vc, sink4, S, band=None, prev=None, carry=None):
    T = qa.shape[0]
    has_band = band is not None
    nq = S // AB if has_band else CTX_BLKS
    q_off = CTX_BLKS if has_band else 0

    def kern(*refs):
        q_ref, kc_ref, vc_ref, sink_ref = refs[:4]
        rest = refs[4:]
        if has_band:
            kp_ref, vp_ref = rest[:2]
            rest = rest[2:]
        o_ref = rest[-1]
        n = pl.program_id(1)
        q4 = _stack_heads(q_ref[...])
        sink = _sink_col(sink_ref)
        kb = vb = mask = None
        if has_band:
            start = pl.multiple_of(n * AB, AB)
            kb = kp_ref[pl.ds(start, 3 * AB), :]
            vb = vp_ref[pl.ds(start, 3 * AB), :]
            mask = _band_mask(n, S)
        p_ctx, p_b, _ = _attn_probs(q4, kc_ref[...], sink, kb, mask)
        o4 = _dot(p_ctx.astype(BF16), vc_ref[...])
        if has_band:
            o4 = o4 + _dot(p_b.astype(BF16), vb)
        o_ref[...] = _unstack_heads(o4).astype(BF16)

    in_specs = [pl.BlockSpec((AB, GW), lambda kh, n: (n + q_off, kh)),
                pl.BlockSpec((CTX, HEAD), lambda kh, n: (0, kh)), pl.BlockSpec((CTX, HEAD), lambda kh, n: (0, kh)),
                pl.BlockSpec((None, Q_PER_KV, HEAD), lambda kh, n: (kh, 0, 0))]
    args = [qa, kc, vc, sink4]
    if has_band:
        in_specs += [pl.BlockSpec((S + 2 * AB, HEAD), lambda kh, n: (0, kh))] * 2
        args += list(band)
    alias = {}
    if prev is not None:
        in_specs.append(ANY)
        alias = {len(args): 0}
        args.append(prev)
    ci, ca, co, cs, cscr = _carry_args(carry)
    res = pl.pallas_call(
        _carried(kern, carry, len(args), 1, *_grid_ends((N_KV, nq))), name=name, grid=(N_KV, nq),
        in_specs=in_specs + ci,
        out_specs=[pl.BlockSpec((AB, GW), lambda kh, n: (n + q_off, kh))] + co,
        out_shape=[_sds((T, N_Q * HEAD), BF16)] + cs, input_output_aliases=alias, scratch_shapes=cscr,
        compiler_params=_params(("arbitrary", "arbitrary")),
    )(*args, *ca)
    return res[0] if carry is None else (res[0], res[1:])


def _attn_bwd(name, qa, kc, vc, sink4, o_all, do_all, S, band=None, prev_dq=None, carry=None):
    T = qa.shape[0]
    has_band = band is not None
    nq = S // AB if has_band else CTX_BLKS
    q_off = CTX_BLKS if has_band else 0
    KW = N_KV * HEAD

    def kern(*refs):
        q_ref, kc_ref, vc_ref, sink_ref, o_ref, do_ref = refs[:6]
        rest = refs[6:]
        if has_band:
            kp_ref, vp_ref = rest[:2]
            rest = rest[2:]
        if prev_dq is not None:
            rest = rest[1:]
        dq_ref, dkc_ref, dvc_ref, dsink_ref = rest[:4]
        n = pl.program_id(1)

        @pl.when(n == 0)
        def _():
            dkc_ref[...] = jnp.zeros(dkc_ref.shape, F32)
            dvc_ref[...] = jnp.zeros(dvc_ref.shape, F32)
            dsink_ref[...] = jnp.zeros(dsink_ref.shape, F32)
            if has_band:
                rest[4][...] = jnp.zeros(rest[4].shape, F32)
                rest[5][...] = jnp.zeros(rest[5].shape, F32)

        q4 = _stack_heads(q_ref[...])
        sink = _sink_col(sink_ref)
        kc_v, vc_v = kc_ref[...], vc_ref[...]
        kb = vb = mask = None
        if has_band:
            start = pl.multiple_of(n * AB, AB)
            kb = kp_ref[pl.ds(start, 3 * AB), :]
            vb = vp_ref[pl.ds(start, 3 * AB), :]
            mask = _band_mask(n, S)
        p_ctx, p_b, p_sink = _attn_probs(q4, kc_v, sink, kb, mask)
        do4 = _stack_heads(do_ref[...])
        o4 = _stack_heads(o_ref[...]).astype(F32)
        delta = jnp.sum(do4 * o4, axis=-1, keepdims=True)
        do4b = do4.astype(BF16)
        ds_ctx = (p_ctx * (_dot(do4b, vc_v, NT) - delta)).astype(BF16)
        dq4 = _dot(ds_ctx, kc_v)
        dkc_ref[...] += _dot(ds_ctx, q4, TN) * ATT_SCALE
        dvc_ref[...] += _dot(p_ctx.astype(BF16), do4b, TN)
        if has_band:
            ds_b = (p_b * (_dot(do4b, vb, NT) - delta)).astype(BF16)
            dq4 = dq4 + _dot(ds_b, kb)
            rest[4][pl.ds(start, 3 * AB), :] += _dot(ds_b, q4, TN) * ATT_SCALE
            rest[5][pl.ds(start, 3 * AB), :] += _dot(p_b.astype(BF16), do4b, TN)
        dq_ref[...] = _unstack_heads(dq4 * ATT_SCALE)
        ps = p_sink * delta
        dsink_ref[...] += jnp.concatenate(
            [jnp.broadcast_to(-jnp.sum(ps[g * AB:(g + 1) * AB, :], axis=0, keepdims=True), (1, HEAD))
             for g in range(Q_PER_KV)], axis=0)

    q_spec = pl.BlockSpec((AB, GW), lambda kh, n: (n + q_off, kh))
    c_spec = pl.BlockSpec((CTX, HEAD), lambda kh, n: (0, kh))
    s_spec = pl.BlockSpec((None, Q_PER_KV, HEAD), lambda kh, n: (kh, 0, 0))
    in_specs = [q_spec, c_spec, c_spec, s_spec, q_spec, q_spec]
    args = [qa, kc, vc, sink4, o_all, do_all]
    out_specs = [q_spec, c_spec, c_spec, s_spec]
    out_shape = [_sds((T, N_Q * HEAD), F32), _sds((CTX, KW), F32), _sds((CTX, KW), F32), _sds((N_KV, Q_PER_KV, HEAD), F32)]
    if has_band:
        p_spec = pl.BlockSpec((S + 2 * AB, HEAD), lambda kh, n: (0, kh))
        in_specs += [p_spec, p_spec]
        args += list(band)
        out_specs += [p_spec, p_spec]
        out_shape += [_sds((S + 2 * AB, KW), F32)] * 2
    alias = {}
    if prev_dq is not None:
        in_specs.append(ANY)
        alias = {len(args): 0}
        args.append(prev_dq)
    ci, ca, co, cs, cscr = _carry_args(carry)
    n_out = len(out_specs)
    res = pl.pallas_call(
        _carried(kern, carry, len(args), n_out, *_grid_ends((N_KV, nq))), name=name, grid=(N_KV, nq),
        in_specs=in_specs + ci, out_specs=out_specs + co, out_shape=out_shape + cs, scratch_shapes=cscr,
        input_output_aliases=alias, compiler_params=_params(("arbitrary", "arbitrary")),
    )(*args, *ca)
    return res if carry is None else (res[:n_out], res[n_out:])


def _dqkv_assemble(name, dp, dq_all, dkp, dvp, dkc_l, dvc_l, dkc_c, dvc_c, cos, sin, S):
    T = CTX + S
    KW = N_KV * HEAD
    HALF = N_Q * HEAD // 2

    def kern(dq_ref, dkp_ref, dvp_ref, dkcl_ref, dvcl_ref, dkcc_ref, dvcc_ref, cos_ref, sin_ref, dp_in, out_ref):
        i = pl.program_id(0)
        j = pl.program_id(1)
        lat = i >= CTX_BLKS
        cos_v, sin_v = cos_ref[...], sin_ref[...]

        @pl.when(j < 2)
        def _():
            dq = dq_ref[...]
            out_ref[...] = jnp.where(lat, _unrope(dq, cos_v, sin_v), dq).astype(BF16)

        @pl.when(j == 2)
        def _():
            dk = jnp.where(lat, _unrope(dkp_ref[...], cos_v, sin_v), dkcl_ref[...] + dkcc_ref[...])
            dv = jnp.where(lat, dvp_ref[...], dvcl_ref[...] + dvcc_ref[...])
            out_ref[...] = jnp.concatenate([dk, dv], axis=1).astype(BF16)

    lat_map = lambda i, j: (jnp.maximum(i - CTX_BLKS, 0), 0)
    pad_map = lambda i, j: (jnp.maximum(i - 1, 0), 0)
    ctx_map = lambda i, j: (jnp.minimum(i, CTX_BLKS - 1), 0)
    return pl.pallas_call(
        kern, name=name, grid=(T // AB, 3),
        in_specs=[pl.BlockSpec((AB, HALF), lambda i, j: (i, jnp.minimum(j, 1))),
                  pl.BlockSpec((AB, KW), pad_map), pl.BlockSpec((AB, KW), pad_map),
                  pl.BlockSpec((AB, KW), ctx_map), pl.BlockSpec((AB, KW), ctx_map),
                  pl.BlockSpec((AB, KW), ctx_map), pl.BlockSpec((AB, KW), ctx_map),
                  pl.BlockSpec((AB, HEAD), lat_map), pl.BlockSpec((AB, HEAD), lat_map), ANY],
        out_specs=pl.BlockSpec((AB, HALF), lambda i, j: (i, COL_Q // HALF + j)),
        out_shape=_sds((T, P_W), BF16), input_output_aliases={9: 0},
        compiler_params=_params(("arbitrary", "arbitrary")),
    )(dq_all, dkp, dvp, dkc_l, dvc_l, dkc_c, dvc_c, cos, sin, dp)


RB = 128
CH = 256
HALO = 8
SUB = 8
GRP = 8


def _vscan(a, b, reverse):
    row = lax.broadcasted_iota(jnp.int32, a.shape, 0)
    A, H = a, b
    for s in (1, 2, 4):
        sh = SUB - s if reverse else s
        m = (row < SUB - s) if reverse else (row >= s)
        As = pltpu.roll(A, sh, 0)
        Hs = pltpu.roll(H, sh, 0)
        H = jnp.where(m, A * Hs + H, H)
        A = jnp.where(m, A * As, A)
    return A, H


def _scan_rows(a_ref, b_ref, r0, nrows, reverse, carry, emit):
    ngrp = nrows // (SUB * GRP)
    row = lax.broadcasted_iota(jnp.int32, (SUB, RB), 0)

    def grp(gi, carry):
        g = (ngrp - 1 - gi) if reverse else gi
        base = r0 + g * (SUB * GRP)
        for v in (range(GRP - 1, -1, -1) if reverse else range(GRP)):
            rs = pl.multiple_of(base + v * SUB, SUB)
            A, H = _vscan(a_ref[pl.ds(rs, SUB), :], b_ref[pl.ds(rs, SUB), :], reverse)
            hf = H + A * carry
            if reverse:
                before = jnp.where(row == SUB - 1, carry, pltpu.roll(hf, SUB - 1, 0))
                carry = hf[0:1, :]
            else:
                before = jnp.where(row == 0, carry, pltpu.roll(hf, 1, 0))
                carry = hf[SUB - 1:SUB, :]
            emit(rs, hf, before)
        return carry

    return lax.fori_loop(0, ngrp, grp, carry)


def _pad_start(ci):
    return pl.multiple_of(ci * CH + HALO * jnp.minimum(ci, 1), HALO)


def _conv_taps(ext, transpose=False):
    n = CH + 2 * HALO
    taps = []
    for k in range(CONV_W):
        off = CONV_LEFT - k if transpose else k - CONV_LEFT
        taps.append(ext[HALO:HALO + CH, :] if off == 0 else pltpu.roll(ext, (-off) % n, 0)[HALO:HALO + CH, :])
    return taps


def _lru_gates(xl, w4, b4, ls):
    pre = _dot(xl.astype(BF16), w4) + b4
    out = []
    for d in range(2):
        r = _sigmoid(pre[:, d * RB:(d + 1) * RB])
        i = _sigmoid(pre[:, (2 + d) * RB:(3 + d) * RB])
        la = LRU_C * r * ls[d:d + 1, :]
        a = jnp.exp(la)
        x2 = 2.0 * la
        series = -x2 * (1.0 + x2 * (0.5 + x2 * (1.0 / 6.0 + x2 * (1.0 / 24.0))))
        q = jnp.where(x2 > -0.03, series, 1.0 - a * a)
        out.append((r, i, a, q))
    return out


def _rnn_specs(T):
    col = lambda n, *_: (0, n)
    return dict(
        xr=pl.BlockSpec((T, RB), lambda n, *_: (0, COL_XR // RB + n)),
        gr=pl.BlockSpec((T, RB), lambda n, *_: (0, COL_GR // RB + n)),
        act=pl.BlockSpec((T, RB), col),
        cw=pl.BlockSpec((CONV_W, RB), col), cb=pl.BlockSpec((1, RB), col),
        w4=pl.BlockSpec((None, RB, 4 * RB), lambda n, *_: (n, 0, 0)),
        b4=pl.BlockSpec((None, 1, 4 * RB), lambda n, *_: (n, 0, 0)),
        lam=pl.BlockSpec((2, RB), col))


PAD_ROWS = 3 * HALO


def _zero_pads(pad_ref, T):
    for r in (0, HALO + CTX, 2 * HALO + T):
        pad_ref[r:r + HALO, :] = jnp.zeros((HALO, RB), F32)


def _fill_padded(pad_ref, src_ref, T):
    _zero_pads(pad_ref, T)
    pad_ref[HALO:HALO + CTX, :] = src_ref[0:CTX, :].astype(F32)
    pad_ref[2 * HALO + CTX:2 * HALO + T, :] = src_ref[CTX:T, :].astype(F32)


def _pad_rows(ci):
    return pl.ds(pl.multiple_of(ci * CH + HALO + HALO * jnp.minimum(ci, 1), HALO), CH)


def _rnn_fwd(name, p, cw, cb, w4, b4, lam, T, carry=None):
    def kern(xr_ref, gr_ref, cw_ref, cb_ref, w4_ref, b4_ref, lam_ref, u_ref, hpf_ref, hpb_ref,
             xpad, a0, b0, a1, b1, y):
        _fill_padded(xpad, xr_ref, T)
        ls = _log_sigmoid(lam_ref[...])
        w4v, b4v, cwv, cbv = w4_ref[...], b4_ref[...], cw_ref[...], cb_ref[...]

        def chunk(ci, _):
            base = pl.multiple_of(ci * CH, CH)
            taps = _conv_taps(xpad[pl.ds(_pad_start(ci), CH + 2 * HALO), :])
            xl = cbv + sum(taps[k] * cwv[k:k + 1, :] for k in range(CONV_W))
            for d, (r, i, a, q) in enumerate(_lru_gates(xl, w4v, b4v, ls)):
                (a0, a1)[d][pl.ds(base, CH), :] = a
                (b0, b1)[d][pl.ds(base, CH), :] = jnp.sqrt(q) * (i * xl)
            return 0

        lax.fori_loop(0, T // CH, chunk, 0)
        zero = jnp.zeros((1, RB), F32)

        def emit_f(rs, hf, before):
            y[pl.ds(rs, SUB), :] = hf
            hpf_ref[pl.ds(rs, SUB), :] = before

        def emit_b(rs, hf, before):
            y[pl.ds(rs, SUB), :] += hf
            hpb_ref[pl.ds(rs, SUB), :] = before

        _scan_rows(a0, b0, 0, T, False, zero, emit_f)
        c = _scan_rows(a1, b1, 0, CTX, True, zero, emit_b)
        _scan_rows(a1, b1, CTX, T - CTX, True, c, emit_b)

        def finish(ci, _):
            base = pl.multiple_of(ci * CH, CH)
            gr = gr_ref[pl.ds(base, CH), :].astype(F32)
            u_ref[pl.ds(base, CH), :] = (y[pl.ds(base, CH), :] * _gelu(gr)).astype(BF16)
            return 0

        lax.fori_loop(0, T // CH, finish, 0)

    sp = _rnn_specs(T)
    ci, ca, co, cs, cscr = _carry_args(carry)
    res = pl.pallas_call(
        _carried(kern, carry, 7, 3, *_grid_ends((N_RNN_BLOCKS,))), name=name, grid=(N_RNN_BLOCKS,),
        in_specs=[sp["xr"], sp["gr"], sp["cw"], sp["cb"], sp["w4"], sp["b4"], sp["lam"]] + ci,
        out_specs=[sp["act"]] * 3 + co,
        out_shape=[_sds((T, D), BF16), _sds((T, D), F32), _sds((T, D), F32)] + cs,
        scratch_shapes=[pltpu.VMEM((T + PAD_ROWS, RB), F32)] + [pltpu.VMEM((T, RB), F32)] * 5 + cscr,
        compiler_params=_params(),
    )(p, p, cw, cb, w4, b4, lam, *ca)
    return res if carry is None else (res[:3], res[3:])


def _rnn_bwd(name, p, du, hpf, hpb, dp, cw, cb, w4, b4, lam, T, carry=None):
    def kern(xr_ref, gr_ref, du_ref, hpf_ref, hpb_ref, cw_ref, cb_ref, w4_ref, b4_ref, lam_ref, dp_in,
             dp_ref, dcw_ref, dcb_ref, dw4_ref, db4_ref, dlam_ref,
             xpad, dxpad, a0, a1, c0, c1, dy, dgr_ref):
        j = pl.program_id(1)

        @pl.when(j == 0)
        def _():
            work(xr_ref, gr_ref, du_ref, hpf_ref, hpb_ref, cw_ref, cb_ref, w4_ref, b4_ref, lam_ref,
                 dp_ref, dgr_ref, dcw_ref, dcb_ref, dw4_ref, db4_ref, dlam_ref, xpad, dxpad, a0, a1, c0, c1, dy)

        @pl.when(j == 1)
        def _():
            dp_ref[...] = dgr_ref[...]

    def work(xr_ref, gr_ref, du_ref, hpf_ref, hpb_ref, cw_ref, cb_ref, w4_ref, b4_ref, lam_ref,
             dxr_ref, dgr_ref, dcw_ref, dcb_ref, dw4_ref, db4_ref, dlam_ref, xpad, dxpad, a0, a1, c0, c1, dy):
        _fill_padded(xpad, xr_ref, T)
        _zero_pads(dxpad, T)
        lam_v = lam_ref[...]
        ls = _log_sigmoid(lam_v)
        w4v, b4v, cwv, cbv = w4_ref[...], b4_ref[...], cw_ref[...], cb_ref[...]

        def conv_chunk(ci):
            taps = _conv_taps(xpad[pl.ds(_pad_start(ci), CH + 2 * HALO), :])
            return taps, cbv + sum(taps[k] * cwv[k:k + 1, :] for k in range(CONV_W))

        def phase_a(ci, _):
            base = pl.multiple_of(ci * CH, CH)
            rows = pl.ds(base, CH)
            _, xl = conv_chunk(ci)
            (r0, i0, av0, q0), (r1, i1, av1, q1) = _lru_gates(xl, w4v, b4v, ls)
            yv = ((av0 * hpf_ref[rows, :] + jnp.sqrt(q0) * (i0 * xl))
                  + (av1 * hpb_ref[rows, :] + jnp.sqrt(q1) * (i1 * xl)))
            gr = gr_ref[rows, :].astype(F32)
            duv = du_ref[rows, :]
            dyv = duv * _gelu(gr)
            dgr_ref[rows, :] = (duv * yv * _gelu_grad(gr)).astype(BF16)
            dy[rows, :] = dyv
            a0[rows, :] = av0
            a1[rows, :] = av1
            c0[rows, :] = av0 * dyv
            c1[rows, :] = av1 * dyv
            return 0

        lax.fori_loop(0, T // CH, phase_a, 0)
        zero = jnp.zeros((1, RB), F32)

        def emit0(rs, hf, before):
            c0[pl.ds(rs, SUB), :] = dy[pl.ds(rs, SUB), :] + before

        def emit1(rs, hf, before):
            c1[pl.ds(rs, SUB), :] = dy[pl.ds(rs, SUB), :] + before

        _scan_rows(a0, c0, 0, T, True, zero, emit0)
        c = _scan_rows(a1, c1, CTX, T - CTX, False, zero, emit1)
        _scan_rows(a1, c1, 0, CTX, False, c, emit1)

        dw4_ref[...] = jnp.zeros(dw4_ref.shape, F32)
        db4_ref[...] = jnp.zeros(db4_ref.shape, F32)
        dlam_ref[...] = jnp.zeros(dlam_ref.shape, F32)
        dcw_ref[...] = jnp.zeros(dcw_ref.shape, F32)
        dcb_ref[...] = jnp.zeros(dcb_ref.shape, F32)

        def phase_c(ci, _):
            base = pl.multiple_of(ci * CH, CH)
            rows = pl.ds(base, CH)
            _, xl = conv_chunk(ci)
            gates = _lru_gates(xl, w4v, b4v, ls)
            dxl = jnp.zeros((CH, RB), F32)
            dpre_a, dpre_x, dls = [], [], []
            for d, (r, i, a, q) in enumerate(gates):
                g = (c0, c1)[d][rows, :]
                hp = (hpf_ref, hpb_ref)[d][rows, :]
                gm = g * jnp.sqrt(q)
                di = gm * xl
                dxl = dxl + gm * i
                dla = a * (g * hp - a * (g * (i * xl)) * lax.rsqrt(q))
                dr = dla * (LRU_C * ls[d:d + 1, :])
                dls.append(_colsum(dla * (LRU_C * r)))
                dpre_a.append(dr * r * (1.0 - r))
                dpre_x.append(di * i * (1.0 - i))
            dpre = jnp.concatenate(dpre_a + dpre_x, axis=1)
            dpre_b = dpre.astype(BF16)
            dxl = dxl + _dot(dpre_b, w4v, NT)
            dw4_ref[...] += _dot(xl.astype(BF16), dpre_b, TN)
            db4_ref[...] += _colsum(dpre)
            dlam_ref[...] += jnp.concatenate(dls, axis=0)
            dcb_ref[...] += _colsum(dxl)
            dxpad[_pad_rows(ci), :] = dxl
            return 0

        lax.fori_loop(0, T // CH, phase_c, 0)
        dlam_ref[...] = dlam_ref[...] * _sigmoid(-lam_v)

        def phase_d(ci, _):
            base = pl.multiple_of(ci * CH, CH)
            rows = pl.ds(base, CH)
            xtaps, _ = conv_chunk(ci)
            dtaps = _conv_taps(dxpad[pl.ds(_pad_start(ci), CH + 2 * HALO), :], transpose=True)
            dxl = dxpad[_pad_rows(ci), :]
            dxr_ref[rows, :] = sum(dtaps[k] * cwv[k:k + 1, :] for k in range(CONV_W)).astype(BF16)
            dcw_ref[...] += jnp.concatenate([_colsum(dxl * xtaps[k]) for k in range(CONV_W)], axis=0)
            return 0

        lax.fori_loop(0, T // CH, phase_d, 0)

    sp = _rnn_specs(T)
    dp_spec = pl.BlockSpec((T, RB), lambda n, j: (0, COL_XR // RB + n + j * (COL_GR - COL_XR) // RB))
    ci, ca, co, cs, cscr = _carry_args(carry)
    res = pl.pallas_call(
        _carried(kern, carry, 11, 6, *_grid_ends((N_RNN_BLOCKS, 2))), name=name, grid=(N_RNN_BLOCKS, 2),
        in_specs=[sp["xr"], sp["gr"], sp["act"], sp["act"], sp["act"], sp["cw"], sp["cb"], sp["w4"], sp["b4"],
                  sp["lam"], ANY] + ci,
        out_specs=[dp_spec, sp["cw"], sp["cb"], sp["w4"], sp["b4"], sp["lam"]] + co,
        out_shape=[_sds((T, P_W), BF16), _sds((CONV_W, D), F32), _sds((1, D), F32),
                   _sds((N_RNN_BLOCKS, RB, 4 * RB), F32), _sds((N_RNN_BLOCKS, 1, 4 * RB), F32), _sds((2, D), F32)] + cs,
        scratch_shapes=([pltpu.VMEM((T + PAD_ROWS, RB), F32)] * 2 + [pltpu.VMEM((T, RB), F32)] * 5
                        + [pltpu.VMEM((T, RB), BF16)] + cscr),
        input_output_aliases={10: 0},
        compiler_params=_params(("arbitrary", "arbitrary")),
    )(p, p, du, hpf, hpb, cw, cb, w4, b4, lam, dp, *ca)
    return res if carry is None else (res[:6], res[6:])


class _Plan:
    def __init__(self, shards, Ws):
        L = len(Ws)
        self.shards, self.Ws = shards, Ws
        self.Gs = [None] * L
        self.slots = [dict() for _ in range(L)]
        self.gate_slots = [None] * L
        self.table = {}
        for l in range(L):
            t = f"l{l}_"
            self.table[t + "proj"] = [("gather", l, k) for k in ("wo_rnn", "wo_attn", "wout")]
            self.table[t + "rnn_fwd"] = [("gather", l, "wffn_in_t")]
            self.table[t + "attn_lat_fwd"] = [("gather", l + 1, "win_t")] if l + 1 < L else []
            self.table[t + "ffn_in"] = [("gather", l, "wffn_out")]
            self.table[t + "ffn_in_dx"] = [("scatter", l, "wffn_out")]
            self.table[t + "attn_lat_bwd"] = [("scatter", l, "wffn_in_t")]
            self.table[t + "rnn_bwd"] = ([("scatter", l, k) for k in ("wout", "wo_attn", "wo_rnn")]
                                         + ([("scatter", l + 1, "win_t"), ("gates", l + 1, "w4")] if l + 1 < L else []))
        self.table["l0_proj_dw_b"] = [("scatter", 0, "win_t_a"), ("gates", 0, "w4")]

    def carry(self, name):
        jobs = []
        for kind, l, k in self.table.get(name, []):
            if kind == "gather":
                jobs.append(("gather", self.shards[l][k]))
            elif kind == "scatter":
                jobs.append(("scatter", self.Gs[l][k].reshape(N_DEV, -1, self.Gs[l][k].shape[-1])))
            else:
                jobs.append(("gather", self.Gs[l]["w4"].reshape(N_RNN_BLOCKS * RB, 4 * RB).astype(BF16)))
        return _Carry(jobs) if jobs else None

    def done(self, name, got):
        for (kind, l, k), res in zip(self.table[name], got):
            if kind == "gather":
                self.Ws[l][k] = res.reshape(-1, D)
            elif kind == "scatter":
                self.slots[l][k] = res
            else:
                self.gate_slots[l] = res


def _run(X, fn, name, *args, **kw):
    carry = None if X is None else X.carry(name)
    if carry is None:
        return fn(name, *args, **kw)
    out, got = fn(name, *args, carry=carry, **kw)
    X.done(name, got)
    return out


def _layer_fwd(l, xa, h, W, rope, S, nxt, X=None):
    T = xa.shape[0]
    tag = f"l{l}_"
    cos, sin = rope
    p = _run(X, _mm_act, tag + "proj", h, W["win_t"], "NT", BF16)
    u, hpf, hpb = _run(X, _rnn_fwd, tag + "rnn_fwd", p, W["cw"], W["cb"], W["w4"], W["b4"], W["lam"], T)
    qa, kp, vp, kc, vc = _qkv_prep(tag + "qkv_prep", p, cos, sin, S)
    o_all = _attn_fwd(tag + "attn_ctx_fwd", qa, kc, vc, W["sink4"], S)
    o_all = _run(X, _attn_fwd, tag + "attn_lat_fwd", qa, kc, vc, W["sink4"], S, band=(kp, vp), prev=o_all)
    ya = _mm_act(tag + "o_rnn", u, W["wo_rnn"], "NN")
    yb = _mm_act(tag + "o_attn", o_all, W["wo_attn"], "NN")
    z = _gate_fwd(tag + "gate_fwd", p, ya, yb)
    m = _mm_act(tag + "out", z, W["wout"], "NN")
    x1, h2 = _resid_norm_fwd(tag + "mix_resid", xa, m, W["g_mix_post"], W["mod"], GA1, W["g_ffn_pre"], W["mod"], SH2, SC2)
    f = _run(X, _mm_act, tag + "ffn_in", h2, W["wffn_in_t"], "NT", BF16)
    s = _swiglu_fwd(tag + "swiglu_fwd", f)
    e = _mm_act(tag + "ffn_out", s, W["wffn_out"], "NN")
    saved = dict(xa=xa, h=h, p=p, u=u, hpf=hpf, hpb=hpb, qa=qa, kp=kp, vp=vp, kc=kc, vc=vc, o_all=o_all,
                 ya=ya, yb=yb, z=z, m=m, x1=x1, h2=h2, f=f, s=s, e=e)
    if nxt[0] == "norm":
        out = _resid_norm_fwd(tag + "ffn_resid", x1, e, W["g_ffn_post"], W["mod"], GA2, nxt[1], nxt[2], SH1, SC1)
    else:
        out = _resid_loss_fwd(tag + "ffn_resid_loss", x1, e, W["g_ffn_post"], W["mod"], GA2, nxt[1])
    return saved, out


def _layer_bwd(l, dx2, A, W, rope, S, X=None):
    T = dx2.shape[0]
    tag = f"l{l}_"
    cos, sin = rope
    G = {}
    if X is not None:
        X.Gs[l] = G
    de, dga2, G["g_ffn_post"] = _resid_bwd(tag + "ffn_resid_bwd", dx2, A["e"], W["g_ffn_post"], W["mod"], GA2)
    ds = _mm_act(tag + "ffn_out_dx", de, W["wffn_out"], "NT", BF16)
    G["wffn_out"] = _mm_wgrad(tag + "ffn_out_dw", A["s"], de)
    df = _swiglu_bwd(tag + "swiglu_bwd", A["f"], ds)
    dh2 = _run(X, _mm_act, tag + "ffn_in_dx", df, W["wffn_in_t"], "NN")
    G["wffn_in_t"] = _mm_wgrad(tag + "ffn_in_dw", df, A["h2"])
    dx1, dm, dsh2, dsc2, G["g_ffn_pre"], dga1, G["g_mix_post"] = _normmod_resid_bwd(
        tag + "mix_resid_bwd", dh2, A["x1"], W["g_ffn_pre"], W["mod"], SH2, SC2, dx2, A["m"], W["g_mix_post"], GA1)
    dz = _mm_act(tag + "out_dx", dm, W["wout"], "NT")
    G["wout"] = _mm_wgrad(tag + "out_dw", A["z"], dm)
    dya, dyb, dp = _gate_bwd(tag + "gate_bwd", A["p"], A["ya"], A["yb"], dz)
    do = _mm_act(tag + "o_attn_dx", dyb, W["wo_attn"], "NT")
    G["wo_attn"] = _mm_wgrad(tag + "o_attn_dw", A["o_all"], dyb)
    du = _mm_act(tag + "o_rnn_dx", dya, W["wo_rnn"], "NT")
    G["wo_rnn"] = _mm_wgrad(tag + "o_rnn_dw", A["u"], dya)
    dq_all, dkc_c, dvc_c, dsink_c = _attn_bwd(tag + "attn_ctx_bwd", A["qa"], A["kc"], A["vc"], W["sink4"],
                                               A["o_all"], do, S)
    dq_all, dkc_l, dvc_l, dsink_l, dkp, dvp = _run(
        X, _attn_bwd, tag + "attn_lat_bwd", A["qa"], A["kc"], A["vc"], W["sink4"], A["o_all"], do, S,
        band=(A["kp"], A["vp"]), prev_dq=dq_all)
    G["sink4"] = dsink_c + dsink_l
    dp = _dqkv_assemble(tag + "dqkv", dp, dq_all, dkp, dvp, dkc_l, dvc_l, dkc_c, dvc_c, cos, sin, S)
    dp, G["cw"], G["cb"], G["w4"], G["b4"], G["lam"] = _run(
        X, _rnn_bwd, tag + "rnn_bwd", A["p"], du, A["hpf"], A["hpb"], dp, W["cw"], W["cb"], W["w4"], W["b4"], W["lam"], T)
    dh = _mm_act(tag + "proj_dx", dp, W["win_t"], "NN")
    if X is not None and l == 0:
        G["win_t_a"] = _mm_wgrad(tag + "proj_dw_a", dp, A["h"][:, :D // 2])
        G["win_t_b"] = _run(X, _mm_wgrad, tag + "proj_dw_b", dp, A["h"][:, D // 2:])
    else:
        G["win_t"] = _mm_wgrad(tag + "proj_dw", dp, A["h"])
    dxa, dsh1, dsc1, G["g_mix_pre"] = _normmod_bwd(tag + "mix_norm_bwd", dh, A["xa"], W["g_mix_pre"], W["mod"],
                                                   SH1, SC1, dx1)
    G["mod"] = jnp.concatenate([dsh1, dsc1, dga1, dsh2, dsc2, dga2], axis=1)
    return dxa, G


def _local_step(xa, target, Ws, S, X=None):
    rope = _rope_tables(S)
    L = len(Ws)
    h = _normmod_fwd("l0_mix_norm", xa, Ws[0]["g_mix_pre"], Ws[0]["mod"], SH1, SC1)
    saved = []
    x = xa
    for l in range(L):
        nxt = ("norm", Ws[l + 1]["g_mix_pre"], Ws[l + 1]["mod"]) if l + 1 < L else ("loss", target)
        A, out = _layer_fwd(l, x, h, Ws[l], rope, S, nxt, X)
        saved.append(A)
        if l + 1 < L:
            x, h = out
    dx, sq = out
    Gs = [None] * L
    for l in reversed(range(L)):
        dx, Gs[l] = _layer_bwd(l, dx, saved[l], Ws[l], rope, S, X)
    return sq, dx, Gs


MESH = pl.DeviceIdType.MESH


def _place():
    return lax.axis_index("x"), lax.axis_index("y"), lax.axis_index("c")


def _lin(px, py, pc):
    return 4 * px + 2 * py + pc


def _allgather_small(name, blk):
    m, n = blk.shape

    def body(x_ref, out_ref, send_sems, recv_sems, local_sem):
        x, y, c = _place()
        me, sibling = (x, y, c), (x, y, 1 - c)
        chips = [(1 - x, y), (x, 1 - y), (1 - x, 1 - y)]

        def copy(k, block, to, src=None):
            dst = out_ref.at[_lin(*block)]
            return pltpu.make_async_remote_copy(src_ref=dst if src is None else src, dst_ref=dst,
                                                send_sem=send_sems.at[k], recv_sem=recv_sems.at[k],
                                                device_id=to, device_id_type=MESH)

        mine = pltpu.make_async_copy(x_ref, out_ref.at[_lin(*me)], local_sem)
        mine.start()
        first = [copy(0, me, sibling, src=x_ref)]
        first += [copy(1 + j, me, (*chip, c), src=x_ref) for j, chip in enumerate(chips)]
        for cp in first:
            cp.start()
        passed = [copy(4 + j, (*chip, c), sibling) for j, chip in enumerate(chips)]
        for j, chip in enumerate(chips):
            copy(1 + j, (*chip, c), me).wait_recv()
            passed[j].start()
        copy(0, sibling, me).wait_recv()
        for j, chip in enumerate(chips):
            copy(4 + j, (*chip, 1 - c), me).wait_recv()
        for cp in first + passed:
            cp.wait_send()
        mine.wait()

    return pl.pallas_call(
        body, name=name, out_shape=_sds((N_DEV, m, n), blk.dtype),
        in_specs=[pl.BlockSpec(memory_space=pltpu.VMEM)], out_specs=pl.BlockSpec(memory_space=pltpu.VMEM),
        scratch_shapes=[pltpu.SemaphoreType.DMA((7,)), pltpu.SemaphoreType.DMA((7,)), pltpu.SemaphoreType.DMA],
        compiler_params=pltpu.CompilerParams(vmem_limit_bytes=VMEM_LIMIT),
    )(blk)


def _allgather_hbm(name, shards):
    na = len(shards)

    def body(*refs):
        ins, outs = refs[:na], refs[na:2 * na]
        send_sems, recv_sems, local_sems = refs[2 * na:]
        x, y, c = _place()
        me, sibling = (x, y, c), (x, y, 1 - c)
        chips = [(1 - x, y), (x, 1 - y), (1 - x, 1 - y)]

        def copy(a, k, block, to, from_input=False):
            dst = outs[a].at[_lin(*block)]
            return pltpu.make_async_remote_copy(src_ref=ins[a] if from_input else dst, dst_ref=dst,
                                                send_sem=send_sems.at[a, k], recv_sem=recv_sems.at[a, k],
                                                device_id=to, device_id_type=MESH)

        mine = [pltpu.make_async_copy(ins[a], outs[a].at[_lin(*me)], local_sems.at[a]) for a in range(na)]
        for cp in mine:
            cp.start()
        first = []
        for a in range(na):
            first.append(copy(a, 0, me, sibling, True))
            first += [copy(a, 1 + j, me, (*chip, c), True) for j, chip in enumerate(chips)]
        for cp in first:
            cp.start()
        passed = []
        for j, chip in enumerate(chips):
            for a in range(na):
                copy(a, 1 + j, (*chip, c), me).wait_recv()
                fwd = copy(a, 4 + j, (*chip, c), sibling)
                fwd.start()
                passed.append(fwd)
        for a in range(na):
            copy(a, 0, sibling, me).wait_recv()
            for j, chip in enumerate(chips):
                copy(a, 4 + j, (*chip, 1 - c), me).wait_recv()
        for cp in first + passed:
            cp.wait_send()
        for cp in mine:
            cp.wait()

    return pl.pallas_call(
        body, name=name, out_shape=[_sds((N_DEV, *s.shape), s.dtype) for s in shards],
        in_specs=[ANY] * na, out_specs=[ANY] * na,
        scratch_shapes=[pltpu.SemaphoreType.DMA((na, 7)), pltpu.SemaphoreType.DMA((na, 7)),
                        pltpu.SemaphoreType.DMA((na,))],
    )(*shards)


def _exchange_shards(name, grads, L):
    nw = len(grads)
    na = nw * L
    flat = [g for per_layer in grads for g in per_layer]

    def body(*refs):
        ins, outs = refs[:na], refs[na:na + nw]
        send_sems, recv_sems, local_sems = refs[na + nw:]
        x, y, c = _place()
        me = _lin(x, y, c)
        peers = [(x ^ ((k + 1) >> 2 & 1), y ^ ((k + 1) >> 1 & 1), c ^ ((k + 1) & 1)) for k in range(7)]

        def copy(a, k, src_blk, dst_blk):
            return pltpu.make_async_remote_copy(src_ref=ins[a].at[src_blk], dst_ref=outs[a // L].at[a % L, dst_blk],
                                                send_sem=send_sems.at[a, k], recv_sem=recv_sems.at[a, k],
                                                device_id=peers[k], device_id_type=MESH)

        mine = [pltpu.make_async_copy(ins[a].at[me], outs[a // L].at[a % L, me], local_sems.at[a]) for a in range(na)]
        for cp in mine:
            cp.start()
        sent = [copy(a, k, _lin(*peers[k]), me) for a in range(na) for k in range(7)]
        for cp in sent:
            cp.start()
        for a in range(na):
            for k in range(7):
                copy(a, k, me, _lin(*peers[k])).wait_recv()
        for cp in sent:
            cp.wait_send()
        for cp in mine:
            cp.wait()

    return pl.pallas_call(
        body, name=name, out_shape=[_sds((L, *per_layer[0].shape), per_layer[0].dtype) for per_layer in grads],
        in_specs=[ANY] * na, out_specs=[ANY] * nw,
        scratch_shapes=[pltpu.SemaphoreType.DMA((na, 7)), pltpu.SemaphoreType.DMA((na, 7)),
                        pltpu.SemaphoreType.DMA((na,))],
    )(*flat)


MOD_ROWS = 16
MOD_SHARD = 6 * D // N_DEV
HI = lax.Precision.HIGHEST


def _mod_fwd(name, c9, w_mod, b_shard):
    L = w_mod.shape[0]

    def kern(c_ref, w_ref, b_ref, o_ref):
        o_ref[...] = lax.dot_general(_silu(c_ref[...]), w_ref[...], NN, precision=HI,
                                     preferred_element_type=F32) + b_ref[...]

    return pl.pallas_call(
        kern, name=name, grid=(L,),
        in_specs=[_full_spec(c9.shape), pl.BlockSpec((None, D, MOD_SHARD), lambda l: (l, 0, 0)),
                  pl.BlockSpec((None, 1, MOD_SHARD), lambda l: (l, 0, 0))],
        out_specs=pl.BlockSpec((None, MOD_ROWS, MOD_SHARD), lambda l: (l, 0, 0)),
        out_shape=_sds((L, MOD_ROWS, MOD_SHARD), F32), compiler_params=_params(),
    )(c9, w_mod, b_shard)


def _mod_bwd(name, c9, w_mod, dmod_all, dmod_cols):
    L = w_mod.shape[0]

    def rows9(ref, l):
        own = jnp.concatenate([ref[j, 2 * l + 1:2 * l + 2, :] for j in range(N_DEV)], axis=0)
        ctx = ref[0, 2 * l:2 * l + 1, :]
        for j in range(1, N_DEV):
            ctx = ctx + ref[j, 2 * l:2 * l + 1, :]
        return own, ctx

    def kern(c_ref, w_ref, all_ref, cols_ref, gw_ref, gb_ref, gc_ref):
        l = pl.program_id(0)
        for ll in range(L):
            @pl.when(l == ll)
            def _():
                own, ctx = rows9(all_ref, ll)
                gb_ref[...] = _colsum(own) + ctx
                own_s, ctx_s = rows9(cols_ref, ll)
                r16 = jnp.concatenate([own_s, ctx_s, jnp.zeros((MOD_ROWS - N_DEV - 1, MOD_SHARD), F32)], axis=0)
                gw_ref[...] = lax.dot_general(_silu(c_ref[...]), r16, TN, precision=HI, preferred_element_type=F32)
                part = lax.dot_general(r16, w_ref[...], NT, precision=HI,
                                       preferred_element_type=F32)[N_DEV:N_DEV + 1, :]
                if ll == 0:
                    gc_ref[...] = part
                else:
                    gc_ref[...] += part

    return pl.pallas_call(
        kern, name=name, grid=(L,),
        in_specs=[_full_spec(c9.shape), pl.BlockSpec((None, D, MOD_SHARD), lambda l: (l, 0, 0)),
                  _full_spec(dmod_all.shape), _full_spec(dmod_cols.shape)],
        out_specs=[pl.BlockSpec((None, D, MOD_SHARD), lambda l: (l, 0, 0)),
                   pl.BlockSpec((None, 1, 6 * D), lambda l: (l, 0, 0)), _full_spec((1, D))],
        out_shape=[_sds((L, D, MOD_SHARD), F32), _sds((L, 1, 6 * D), F32), _sds((1, D), F32)],
        compiler_params=_params(),
    )(c9, w_mod, dmod_all, dmod_cols)


_BC1 = 1.0 - ADAM_B1 ** ADAM_STEP
_BC2 = 1.0 - ADAM_B2 ** ADAM_STEP


def _adamw_vals(w, g, m, v):
    m = ADAM_B1 * m + (1.0 - ADAM_B1) * g
    v = ADAM_B2 * v + (1.0 - ADAM_B2) * (g * g)
    delta = -ADAM_LR * ((m / _BC1) / (jnp.sqrt(v / _BC2) + ADAM_EPS) + ADAM_WD * w)
    return delta, m, v


def _adamw(name, w, g, m, v, tile):
    R, C = w.shape
    blk = ((tile, C), lambda i: (i, 0))

    def body(i, ins, ps, outs, acc):
        d, mm, vv = _adamw_vals(ins[0][...], ins[1][...], ins[2][...], ins[3][...])
        outs[0][...] = d
        outs[1][...] = mm
        outs[2][...] = vv

    return _ew(name, body, R // tile, [(a, *blk) for a in (w, g, m, v)], [], [(_sds((R, C), F32), *blk)] * 3)


def _sum_slots(ref):
    g = ref[0].astype(F32)
    for j in range(1, N_DEV):
        g = g + ref[j].astype(F32)
    return g


def _adamw_slots(name, slots, w, m, v, tile):
    L, R, C = w.shape
    n = R // tile
    spec = pl.BlockSpec((None, tile, C), lambda l, i: (l, i, 0))
    pieces = [s if isinstance(s, (list, tuple)) else [s] for s in slots]
    layer_of = [ll for ll, ps in enumerate(pieces) for _ in ps]
    flat = [p for ps in pieces for p in ps]

    def slot_spec(ll, cols):
        return pl.BlockSpec((N_DEV, tile, cols),
                            lambda l, i: (0, jnp.where(l == ll, i, jnp.where(l < ll, 0, n - 1)), 0))

    def kern(*refs):
        s_refs = refs[:len(flat)]
        w_ref, m_ref, v_ref, g_ref, d_ref, mo_ref, vo_ref = refs[len(flat):]
        l = pl.program_id(0)
        for ll in range(L):
            @pl.when(l == ll)
            def _():
                parts = [_sum_slots(r) for r, lr in zip(s_refs, layer_of) if lr == ll]
                g = parts[0] if len(parts) == 1 else jnp.concatenate(parts, axis=1)
                g_ref[...] = g
                d_ref[...], mo_ref[...], vo_ref[...] = _adamw_vals(w_ref[...], g, m_ref[...], v_ref[...])

    return pl.pallas_call(
        kern, name=name, grid=(L, n),
        in_specs=[slot_spec(ll, p.shape[-1]) for ll, p in zip(layer_of, flat)] + [spec, spec, spec],
        out_specs=[spec] * 4, out_shape=[_sds((L, R, C), F32)] * 4,
        compiler_params=_params(("arbitrary", "arbitrary")),
    )(*flat, w, m, v)


def _sum_blocks(name, blocks):
    _, R, C = blocks.shape

    def kern(b_ref, o_ref):
        o_ref[...] = _sum_slots(b_ref)

    return pl.pallas_call(kern, name=name, in_specs=[_full_spec(blocks.shape)], out_specs=_full_spec((R, C)),
                          grid=(1,), out_shape=_sds((R, C), F32), compiler_params=_params())(blocks)


BIG = ("win_t", "wo_rnn", "wo_attn", "wout", "wffn_in_t", "wffn_out")
BIG_SRC = ("w_in", "w_o_rnn", "w_o_attn", "w_out", "w_ffn_in", "w_ffn_out")
BIG_T = (True, False, False, False, True, False)
BIG_TILE = (176, 128, 128, 128, 176, 176)


def _chan_full(g8):
    return jnp.transpose(g8, (1, 0, 2)).reshape(g8.shape[1], D)


def kernel(x, c, ctx, c_ctx, w_mod, b_mod, g_mix_pre, g_mix_post, g_ffn_pre, g_ffn_post, w_in, conv_w, conv_b, lru_wa, lru_ba, lru_wx, lru_bx, lru_lam, attn_sink, w_o_rnn, w_o_attn, w_out, w_ffn_in, w_ffn_out, loss_target, m_c_ctx, m_w_mod, m_b_mod, m_g_mix_pre, m_g_mix_post, m_g_ffn_pre, m_g_ffn_post, m_w_in, m_conv_w, m_conv_b, m_lru_wa, m_lru_ba, m_lru_wx, m_lru_bx, m_lru_lam, m_attn_sink, m_w_o_rnn, m_w_o_attn, m_w_out, m_w_ffn_in, m_w_ffn_out, v_c_ctx, v_w_mod, v_b_mod, v_g_mix_pre, v_g_mix_post, v_g_ffn_pre, v_g_ffn_post, v_w_in, v_conv_w, v_conv_b, v_lru_wa, v_lru_ba, v_lru_wx, v_lru_bx, v_lru_lam, v_attn_sink, v_w_o_rnn, v_w_o_attn, v_w_out, v_w_ffn_in, v_w_ffn_out):
    P = dict(c_ctx=c_ctx, w_mod=w_mod, b_mod=b_mod, g_mix_pre=g_mix_pre, g_mix_post=g_mix_post, g_ffn_pre=g_ffn_pre,
             g_ffn_post=g_ffn_post, w_in=w_in, conv_w=conv_w, conv_b=conv_b, lru_wa=lru_wa, lru_ba=lru_ba,
             lru_wx=lru_wx, lru_bx=lru_bx, lru_lam=lru_lam, attn_sink=attn_sink, w_o_rnn=w_o_rnn, w_o_attn=w_o_attn,
             w_out=w_out, w_ffn_in=w_ffn_in, w_ffn_out=w_ffn_out)
    Mo = dict(c_ctx=m_c_ctx, w_mod=m_w_mod, b_mod=m_b_mod, g_mix_pre=m_g_mix_pre, g_mix_post=m_g_mix_post,
              g_ffn_pre=m_g_ffn_pre, g_ffn_post=m_g_ffn_post, w_in=m_w_in, conv_w=m_conv_w, conv_b=m_conv_b,
              lru_wa=m_lru_wa, lru_ba=m_lru_ba, lru_wx=m_lru_wx, lru_bx=m_lru_bx, lru_lam=m_lru_lam,
              attn_sink=m_attn_sink, w_o_rnn=m_w_o_rnn, w_o_attn=m_w_o_attn, w_out=m_w_out, w_ffn_in=m_w_ffn_in,
              w_ffn_out=m_w_ffn_out)
    Vo = dict(c_ctx=v_c_ctx, w_mod=v_w_mod, b_mod=v_b_mod, g_mix_pre=v_g_mix_pre, g_mix_post=v_g_mix_post,
              g_ffn_pre=v_g_ffn_pre, g_ffn_post=v_g_ffn_post, w_in=v_w_in, conv_w=v_conv_w, conv_b=v_conv_b,
              lru_wa=v_lru_wa, lru_ba=v_lru_ba, lru_wx=v_lru_wx, lru_bx=v_lru_bx, lru_lam=v_lru_lam,
              attn_sink=v_attn_sink, w_o_rnn=v_w_o_rnn, w_o_attn=v_w_o_attn, w_out=v_w_out, w_ffn_in=v_w_ffn_in,
              w_ffn_out=v_w_ffn_out)
    L = w_in.shape[0]
    S = x.shape[1]
    me = _lin(*_place())

    small = jnp.concatenate([c.reshape(8, 128), conv_w.reshape(L * CONV_W, 128), lru_ba.reshape(2 * L, 128),
                             lru_bx.reshape(2 * L, 128), lru_lam.reshape(2 * L, 128), jnp.zeros((4, 128), F32)], axis=0)
    small_all = _allgather_small("ag_small", small)
    c_all = small_all[:, 0:8].reshape(N_DEV, D)
    conv_w_f = _chan_full(small_all[:, 8:16]).reshape(L, CONV_W, D)
    lru_ba_f = _chan_full(small_all[:, 16:20]).reshape(L, 2, D)
    lru_bx_f = _chan_full(small_all[:, 20:24]).reshape(L, 2, D)
    lru_lam_f = _chan_full(small_all[:, 24:28]).reshape(L, 2, D)

    c9 = jnp.concatenate([c_all, c_ctx[None], jnp.zeros((MOD_ROWS - N_DEV - 1, D), F32)], axis=0)
    b_shard = lax.dynamic_slice_in_dim(b_mod, me * MOD_SHARD, MOD_SHARD, axis=1)[:, None, :]
    mod_part = _mod_fwd("mod_fwd", c9, w_mod, b_shard)
    mod_all = _allgather_small("ag_mod", mod_part.reshape(L * MOD_ROWS, MOD_SHARD))
    mod_all = jnp.transpose(mod_all.reshape(N_DEV, L, MOD_ROWS, MOD_SHARD), (1, 2, 0, 3)).reshape(L, MOD_ROWS, 6 * D)
    own_row = lax.dynamic_index_in_dim(mod_all, me, axis=1, keepdims=False)
    modrows = jnp.stack([mod_all[:, N_DEV], own_row], axis=1)

    shards = [{k: (P[src][l].T if tr else P[src][l]).astype(BF16) for k, src, tr in zip(BIG, BIG_SRC, BIG_T)}
              for l in range(L)]
    win0, = _allgather_hbm("ag_w_in0", [shards[0]["win_t"]])
    Ws = []
    for l in range(L):
        W = {"win_t": win0.reshape(-1, D)} if l == 0 else {}
        W.update(
            cw=conv_w_f[l], cb=conv_b[l][None],
            w4=jnp.concatenate([lru_wa[l, 0], lru_wa[l, 1], lru_wx[l, 0], lru_wx[l, 1]], axis=-1).astype(BF16),
            b4=jnp.concatenate([lru_ba_f[l, 0].reshape(N_RNN_BLOCKS, 1, RB), lru_ba_f[l, 1].reshape(N_RNN_BLOCKS, 1, RB),
                                lru_bx_f[l, 0].reshape(N_RNN_BLOCKS, 1, RB), lru_bx_f[l, 1].reshape(N_RNN_BLOCKS, 1, RB)],
                               axis=-1),
            lam=lru_lam_f[l], sink4=jnp.broadcast_to(attn_sink[l].reshape(N_KV, Q_PER_KV, 1), (N_KV, Q_PER_KV, HEAD)),
            g_mix_pre=g_mix_pre[l][None], g_mix_post=g_mix_post[l][None], g_ffn_pre=g_ffn_pre[l][None],
            g_ffn_post=g_ffn_post[l][None], mod=modrows[l])
        Ws.append(W)

    xa = jnp.concatenate([ctx[0], x[0]], axis=0)
    plan = _Plan(shards, Ws)
    sq, dxa, Gs = _local_step(xa, loss_target[0], Ws, S, plan)
    loss = lax.psum((0.5 / D) * jnp.sum(sq), ("x", "y", "c"))
    grad_x = dxa[CTX:][None]

    dmod = jnp.concatenate([Gs[l]["mod"] for l in range(L)] + [jnp.zeros((8 - 2 * L, 6 * D), F32)], axis=0)
    dmod_all = _allgather_small("ag_dmod", dmod)
    dmod_cols = lax.dynamic_slice_in_dim(dmod_all, me * MOD_SHARD, MOD_SHARD, axis=2)
    g_w_mod, g_b_mod, dsc_part = _mod_bwd("mod_bwd", c9, w_mod, dmod_all, dmod_cols)
    g_b_mod = g_b_mod[:, 0]

    def rows(name, shape):
        return jnp.concatenate([Gs[l][name].reshape(shape) for l in range(L)], axis=0)

    b4g = [Gs[l]["b4"].reshape(N_RNN_BLOCKS, 4, RB) for l in range(L)]
    sink_row = jnp.concatenate([Gs[l]["sink4"][:, :, 0].reshape(1, N_Q) for l in range(L)]
                               + [jnp.zeros((1, D - L * N_Q), F32)], axis=1)
    small_g = jnp.concatenate(
        [rows("g_mix_pre", (1, D)), rows("g_mix_post", (1, D)), rows("g_ffn_pre", (1, D)), rows("g_ffn_post", (1, D)),
         rows("cb", (1, D)), rows("cw", (CONV_W, D))]
        + [b4g[l][:, d].reshape(1, D) for l in range(L) for d in range(2)]
        + [b4g[l][:, 2 + d].reshape(1, D) for l in range(L) for d in range(2)]
        + [rows("lam", (2, D)), sink_row, dsc_part], axis=0)
    n_small = small_g.shape[0]
    small_tot = _sum_blocks("sum_small", _allgather_small("ag_small_grads", small_g))
    o = 0
    G = {}
    for name in ("g_mix_pre", "g_mix_post", "g_ffn_pre", "g_ffn_post", "conv_b"):
        G[name] = small_tot[o:o + L]
        o += L
    G["conv_w"] = small_tot[o:o + L * CONV_W].reshape(L, CONV_W, D)
    o += L * CONV_W
    for name in ("lru_ba", "lru_bx", "lru_lam"):
        G[name] = small_tot[o:o + 2 * L].reshape(L, 2, D)
        o += 2 * L
    G["attn_sink"] = small_tot[o, :L * N_Q].reshape(L, N_Q)
    sg = jax.nn.sigmoid(c_ctx)
    G["c_ctx"] = small_tot[o + 1] * (sg * (1.0 + c_ctx * (1.0 - sg)))
    G["b_mod"] = g_b_mod
    G["w_mod"] = g_w_mod

    last_slots, = _exchange_shards("exchange_w_in0", [[Gs[0]["win_t_b"].reshape(N_DEV, -1, D // 2)]], 1)
    plan.slots[0]["win_t"] = [plan.slots[0]["win_t_a"], last_slots[0]]

    out_g, out_d, out_m, out_v = {}, {}, {}, {}

    def put(name, res, shape=None):
        g, d, m, v = res
        for dst, val in ((out_g, g), (out_d, d), (out_m, m), (out_v, v)):
            dst[name] = val if shape is None else val.reshape(shape)

    for k, src, tr, tile in zip(BIG, BIG_SRC, BIG_T, BIG_TILE):
        lay = (lambda a: jnp.swapaxes(a, 1, 2)) if tr else (lambda a: a)
        res = _adamw_slots("adamw_" + src, [plan.slots[l][k] for l in range(L)], lay(P[src]), lay(Mo[src]),
                           lay(Vo[src]), tile)
        put(src, [lay(r) for r in res])
    res = _adamw("adamw_w_mod", w_mod.reshape(L * D, MOD_SHARD), g_w_mod.reshape(L * D, MOD_SHARD),
                 m_w_mod.reshape(L * D, MOD_SHARD), v_w_mod.reshape(L * D, MOD_SHARD), 256)
    put("w_mod", (g_w_mod,) + tuple(res), w_mod.shape)
    def fuse4(wa, wx):
        return jnp.concatenate([wa[:, 0], wa[:, 1], wx[:, 0], wx[:, 1]], axis=-1).reshape(L, N_RNN_BLOCKS * RB, 4 * RB)

    res = _adamw_slots("adamw_gates", plan.gate_slots,
                       fuse4(lru_wa, lru_wx), fuse4(m_lru_wa, m_lru_wx), fuse4(v_lru_wa, v_lru_wx), 256)
    res = [r.reshape(L, N_RNN_BLOCKS, RB, 4, RB) for r in res]
    put("lru_wa", [jnp.stack([r[:, :, :, 0], r[:, :, :, 1]], axis=1) for r in res])
    put("lru_wx", [jnp.stack([r[:, :, :, 2], r[:, :, :, 3]], axis=1) for r in res])
    rep = ("g_mix_pre", "g_mix_post", "g_ffn_pre", "g_ffn_post", "conv_b", "b_mod")

    def pack_rep(T_):
        sink = jnp.concatenate([T_["attn_sink"].reshape(1, L * N_Q), jnp.zeros((1, D - L * N_Q), F32)], axis=1)
        return jnp.concatenate([T_[n].reshape(-1, D) for n in rep] + [sink, T_["c_ctx"][None]], axis=0)

    pk = [pack_rep(T_) for T_ in (P, G, Mo, Vo)]
    n_rep = pk[0].shape[0]
    res = _adamw("adamw_replicated", *[jnp.pad(a, ((0, 24 - n_rep), (0, 0))) for a in pk], 24)
    res = (pk[1],) + tuple(r[:n_rep] for r in res)
    o = 0
    for n in rep:
        k = P[n].size // D
        put(n, [r[o:o + k] for r in res], P[n].shape)
        o += k
    put("attn_sink", [r[o, :L * N_Q] for r in res], attn_sink.shape)
    put("c_ctx", [r[o + 1] for r in res], c_ctx.shape)
    chan = ("conv_w", "lru_ba", "lru_bx", "lru_lam")
    g_own = {n: lax.dynamic_slice_in_dim(G[n], me * RB, RB, axis=2) for n in chan}

    def pack_chan(T_):
        return jnp.concatenate([T_[n].reshape(-1, RB) for n in chan], axis=0)

    pk = [pack_chan(T_) for T_ in (P, g_own, Mo, Vo)]
    n_ch = pk[0].shape[0]
    res = _adamw("adamw_channels", *[jnp.pad(a, ((0, 24 - n_ch), (0, 0))) for a in pk], 24)
    res = (pk[1],) + tuple(r[:n_ch] for r in res)
    o = 0
    for n in chan:
        k = P[n].size // RB
        put(n, [r[o:o + k] for r in res], P[n].shape)
        o += k

    order = ("c_ctx", "w_mod", "b_mod", "g_mix_pre", "g_mix_post", "g_ffn_pre", "g_ffn_post", "w_in", "conv_w", "conv_b",
             "lru_wa", "lru_ba", "lru_wx", "lru_bx", "lru_lam", "attn_sink", "w_o_rnn", "w_o_attn", "w_out", "w_ffn_in",
             "w_ffn_out")
    return (loss, grad_x, *[out_g[n] for n in order], *[out_d[n] for n in order], *[out_m[n] for n in order],
            *[out_v[n] for n in order])
```

```python
import functools
import math

import numpy as np
import jax
import jax.numpy as jnp
from jax import lax
from jax.experimental import pallas as pl
from jax.experimental.pallas import tpu as pltpu

F32 = jnp.float32
BF16 = jnp.bfloat16

D = 1024
CTX = 256
TR = 256
HEAD = 128
N_Q = 8
N_KV = 2
Q_PER_KV = N_Q // N_KV
GRID_W = 64
N_FREQ = HEAD // 4
ROPE_BASE = 10000.0
N_RNN_BLOCKS = 8
CONV_W = 4
CONV_LEFT = 2
LRU_C = 8.0
D_FF = 2816
IN_W = 5632
P_W = IN_W
COL_XR, COL_GR, COL_Q, COL_K, COL_V, COL_GL = 0, 1024, 2048, 3072, 3328, 3584
GLB = 512
EPS = 1e-6
NEG_INF = -1e30
ATT_SCALE = HEAD ** -0.5
N_DEV = 8
VMEM_LIMIT = 56 * 1024 * 1024

ADAM_LR, ADAM_B1, ADAM_B2, ADAM_EPS, ADAM_WD, ADAM_STEP = 0.001, 0.9, 0.999, 1e-08, 0.01, 10

NN = (((1,), (0,)), ((), ()))
NT = (((1,), (1,)), ((), ()))
TN = (((0,), (0,)), ((), ()))


def _dot(a, b, dims=NN):
    return lax.dot_general(a, b, dims, preferred_element_type=F32)


def _params(sem=("arbitrary",)):
    return pltpu.CompilerParams(dimension_semantics=sem, vmem_limit_bytes=VMEM_LIMIT)


def _full_spec(shape):
    nd = len(shape)
    return pl.BlockSpec(shape, lambda *_: (0,) * nd)


ANY = pl.BlockSpec(memory_space=pl.ANY)


def _ew(name, body, n, row_ins, pars, row_outs, accs=(), alias=None):
    n_ri, n_p, n_ro, n_acc = len(row_ins), len(pars), len(row_outs), len(accs)

    def kern(*refs):
        i = pl.program_id(0)
        ins = refs[:n_ri]
        ps = refs[n_ri:n_ri + n_p]
        outs = refs[n_ri + n_p:n_ri + n_p + n_ro]
        acc = refs[n_ri + n_p + n_ro:]
        if n_acc:
            @pl.when(i == 0)
            def _():
                for a in acc:
                    a[...] = jnp.zeros(a.shape, a.dtype)
        body(i, ins, ps, outs, acc)

    in_specs = [ANY if blk is None else pl.BlockSpec(blk, imap) for (_, blk, imap) in row_ins]
    in_specs += [_full_spec(p.shape) for p in pars]
    out_specs = [pl.BlockSpec(blk, imap) for (_, blk, imap) in row_outs] + [_full_spec(a.shape) for a in accs]
    out_shape = [s for (s, _, _) in row_outs] + list(accs)
    return pl.pallas_call(
        kern, name=name, grid=(n,), in_specs=in_specs, out_specs=out_specs, out_shape=out_shape,
        input_output_aliases=alias or {}, compiler_params=_params(),
    )(*[a for (a, _, _) in row_ins], *pars)


def _rowblk(width, colblk=0, roff=0, tile=TR):
    return (tile, width), (lambda i: (i + roff, colblk))


def _sds(shape, dtype):
    return jax.ShapeDtypeStruct(shape, dtype)


class _Carry:
    SAME_CORE = (1, 3, 5)

    def __init__(self, jobs):
        self.jobs = list(jobs)
        self.arrays = [a for _, a in self.jobs]
        self.out_shapes = [_sds(a.shape if kind == "scatter" else (N_DEV, *a.shape), a.dtype) for kind, a in self.jobs]
        n = len(self.jobs)
        self.scratch = [pltpu.SemaphoreType.DMA((n, 7)), pltpu.SemaphoreType.DMA((n, 7)), pltpu.SemaphoreType.DMA((n,))]

    def _setup(self, sems):
        send_sems, recv_sems, local_sems = sems
        x, y, c = _place()
        me = _lin(x, y, c)
        peers = [(x ^ ((k + 1) >> 2 & 1), y ^ ((k + 1) >> 1 & 1), c ^ ((k + 1) & 1)) for k in range(7)]

        def copy(a, k, sem_k, src, dst):
            return pltpu.make_async_remote_copy(src_ref=src, dst_ref=dst, send_sem=send_sems.at[a, sem_k],
                                                recv_sem=recv_sems.at[a, sem_k], device_id=peers[k], device_id_type=MESH)

        return me, [_lin(*p) for p in peers], copy, local_sems

    def _local(self, a, kind, ins, outs, me, local_sems):
        return pltpu.make_async_copy(ins[a].at[me] if kind == "scatter" else ins[a], outs[a].at[me], local_sems.at[a])

    def start(self, ins, outs, sems):
        me, theirs, copy, local_sems = self._setup(sems)
        for a, (kind, _) in enumerate(self.jobs):
            self._local(a, kind, ins, outs, me, local_sems).start()
            if kind == "scatter":
                for k in range(7):
                    copy(a, k, k, ins[a].at[theirs[k]], outs[a].at[me]).start()
            else:
                for k in (0,) + self.SAME_CORE:
                    copy(a, k, k, ins[a], outs[a].at[me]).start()

    def wait(self, ins, outs, sems):
        me, theirs, copy, local_sems = self._setup(sems)
        for a, (kind, _) in enumerate(self.jobs):
            if kind == "scatter":
                for k in range(7):
                    copy(a, k, k, ins[a].at[me], outs[a].at[theirs[k]]).wait_recv()
                for k in range(7):
                    copy(a, k, k, ins[a].at[theirs[k]], outs[a].at[me]).wait_send()
            else:
                for k in self.SAME_CORE:
                    blk = outs[a].at[theirs[k]]
                    copy(a, k, k, ins[a], blk).wait_recv()
                    copy(a, 0, k + 1, blk, blk).start()
                copy(a, 0, 0, ins[a], outs[a].at[theirs[0]]).wait_recv()
                for k in self.SAME_CORE:
                    copy(a, 0, k + 1, ins[a], outs[a].at[theirs[k + 1]]).wait_recv()
                for k in (0,) + self.SAME_CORE:
                    copy(a, k, k, ins[a], outs[a].at[me]).wait_send()
                for k in self.SAME_CORE:
                    blk = outs[a].at[theirs[k]]
                    copy(a, 0, k + 1, blk, blk).wait_send()
            self._local(a, kind, ins, outs, me, local_sems).wait()


def _carried(kern, carry, n_in, n_out, first, last):
    if carry is None:
        return kern
    nc = len(carry.jobs)

    def wrapped(*refs):
        ins, cin = refs[:n_in], refs[n_in:n_in + nc]
        outs, cout = refs[n_in + nc:n_in + nc + n_out], refs[n_in + nc + n_out:n_in + 2 * nc + n_out]
        scr, sems = refs[n_in + 2 * nc + n_out:-3], refs[-3:]

        @pl.when(first())
        def _():
            carry.start(cin, cout, sems)

        kern(*ins, *outs, *scr)

        @pl.when(last())
        def _():
            carry.wait(cin, cout, sems)

    return wrapped


def _carry_args(carry):
    if carry is None:
        return [], [], [], [], []
    n = len(carry.jobs)
    return [ANY] * n, carry.arrays, [ANY] * n, carry.out_shapes, carry.scratch


def _grid_ends(dims):
    first = lambda: functools.reduce(jnp.logical_and, [pl.program_id(d) == 0 for d in range(len(dims))])
    last = lambda: functools.reduce(jnp.logical_and, [pl.program_id(d) == n - 1 for d, n in enumerate(dims)])
    return first, last


def _mm_call(name, a, b, mode, out_dtype, tm, tn, rows_outer=True, single_b=False, carry=None):
    if mode == "TN":
        (K, M), N = a.shape, b.shape[1]
    else:
        (M, K), N = a.shape, (b.shape[1] if mode == "NN" else b.shape[0])
    assert M % tm == 0 and N % tn == 0, (name, M, N, K, tm, tn)
    ij = (lambda g0, g1: (g0, g1)) if rows_outer else (lambda g0, g1: (g1, g0))
    grid = (M // tm, N // tn) if rows_outer else (N // tn, M // tm)
    if mode == "TN":
        a_spec = pl.BlockSpec((K, tm), lambda g0, g1: (0, ij(g0, g1)[0]))
    else:
        a_spec = pl.BlockSpec((tm, K), lambda g0, g1: (ij(g0, g1)[0], 0))
    b_blk, b_map = ((tn, K), lambda g0, g1: (ij(g0, g1)[1], 0)) if mode == "NT" else \
                   ((K, tn), lambda g0, g1: (0, ij(g0, g1)[1]))
    b_spec = pl.BlockSpec(b_blk, b_map, pipeline_mode=pl.Buffered(1)) if single_b else pl.BlockSpec(b_blk, b_map)
    dims = {"NN": NN, "NT": NT, "TN": TN}[mode]

    def kern(a_ref, b_ref, o_ref):
        o_ref[...] = _dot(a_ref[...], b_ref[...], dims).astype(o_ref.dtype)

    ci, ca, co, cs, cscr = _carry_args(carry)
    res = pl.pallas_call(
        _carried(kern, carry, 2, 1, *_grid_ends(grid)), name=name, grid=grid, in_specs=[a_spec, b_spec] + ci,
        out_specs=[pl.BlockSpec((tm, tn), lambda g0, g1: ij(g0, g1))] + co,
        out_shape=[_sds((M, N), out_dtype)] + cs, scratch_shapes=cscr,
        compiler_params=_params(("arbitrary", "arbitrary")),
    )(a, b, *ca)
    return res[0] if carry is None else (res[0], res[1:])


def _mm_act(name, a, w, mode, out_dtype=F32, carry=None):
    rows, K = a.shape
    N = w.shape[1] if mode == "NN" else w.shape[0]
    if K > D_FF:
        return _mm_call(name, a, w, mode, out_dtype, rows // 8, N, single_b=True, carry=carry)
    tn = N if N <= 1024 else 1408
    return _mm_call(name, a, w, mode, out_dtype, rows // 4, tn, carry=carry)


def _mm_wgrad(name, x, dy, out_dtype=BF16, carry=None):
    M = x.shape[1]
    tm = 1408 if M == D_FF else 512
    return _mm_call(name, x, dy, "TN", out_dtype, tm, dy.shape[1], single_b=True, carry=carry)


def _sigmoid(x):
    return 0.5 * jnp.tanh(0.5 * x) + 0.5


def _silu(x):
    return x * _sigmoid(x)


def _silu_grad(x):
    s = _sigmoid(x)
    return s * (1.0 + x * (1.0 - s))


_GELU_K = math.sqrt(2.0 / math.pi)


def _gelu(x):
    return 0.5 * x * (1.0 + jnp.tanh(_GELU_K * (x + 0.044715 * x * x * x)))


def _gelu_grad(x):
    t = jnp.tanh(_GELU_K * (x + 0.044715 * x * x * x))
    return 0.5 * (1.0 + t) + 0.5 * x * (1.0 - t * t) * _GELU_K * (1.0 + 3.0 * 0.044715 * x * x)


def _log_sigmoid(x):
    return jnp.minimum(x, 0.0) - jnp.log(1.0 + jnp.exp(-jnp.abs(x)))


def _rms(x):
    r = lax.rsqrt(jnp.mean(x * x, axis=-1, keepdims=True) + EPS)
    return x * r, r


def _rms_bwd(dy, y, r):
    return r * (dy - y * jnp.mean(dy * y, axis=-1, keepdims=True))


def _modrow(mod_ref, i, chunk):
    lo = mod_ref[0:1, chunk * D:(chunk + 1) * D]
    hi = mod_ref[1:2, chunk * D:(chunk + 1) * D]
    return jnp.where(i == 0, lo, hi)


def _acc_seg(acc_ref, i, val):
    zero = jnp.zeros_like(val)
    acc_ref[0:1, :] += jnp.where(i == 0, val, zero)
    acc_ref[1:2, :] += jnp.where(i == 0, zero, val)


def _colsum(x):
    return jnp.sum(x, axis=0, keepdims=True)


SH1, SC1, GA1, SH2, SC2, GA2 = range(6)


def _normmod_fwd(name, xa, g, mod, c_sh, c_sc):
    T = xa.shape[0]

    def body(i, ins, ps, outs, acc):
        y, _ = _rms(ins[0][...])
        h = (y * ps[0][...]) * (1.0 + _modrow(ps[1], i, c_sc)) + _modrow(ps[1], i, c_sh)
        outs[0][...] = h.astype(BF16)

    return _ew(name, body, T // TR, [(xa, *_rowblk(D))], [g, mod], [(_sds((T, D), BF16), *_rowblk(D))])[0]


def _resid_norm_fwd(name, xin, mat, gpost, mod, c_ga, gnext, modn, c_sh, c_sc):
    T = xin.shape[0]

    def body(i, ins, ps, outs, acc):
        ym, _ = _rms(ins[1][...])
        xo = ins[0][...] + _modrow(ps[1], i, c_ga) * (ym * ps[0][...])
        outs[0][...] = xo
        y, _ = _rms(xo)
        h = (y * ps[2][...]) * (1.0 + _modrow(ps[3], i, c_sc)) + _modrow(ps[3], i, c_sh)
        outs[1][...] = h.astype(BF16)

    return _ew(name, body, T // TR, [(xin, *_rowblk(D)), (mat, *_rowblk(D))], [gpost, mod, gnext, modn],
               [(_sds((T, D), F32), *_rowblk(D)), (_sds((T, D), BF16), *_rowblk(D))])


def _resid_loss_fwd(name, xin, mat, gpost, mod, c_ga, target):
    T = xin.shape[0]

    def body(i, ins, ps, outs, acc):
        ym, _ = _rms(ins[1][...])
        xo = ins[0][...] + _modrow(ps[1], i, c_ga) * (ym * ps[0][...])
        err = xo - ins[2][...]
        lat = i > 0
        outs[0][...] = jnp.where(lat, err * (1.0 / D), 0.0)
        acc[0][...] += jnp.where(lat, _colsum(err * err), 0.0)

    tgt_blk = ((TR, D), lambda i: (jnp.maximum(i - 1, 0), 0))
    dx, sq = _ew(name, body, T // TR, [(xin, *_rowblk(D)), (mat, *_rowblk(D)), (target, *tgt_blk)], [gpost, mod],
                 [(_sds((T, D), F32), *_rowblk(D))], [_sds((1, D), F32)])
    return dx, sq


def _resid_bwd_vals(i, dout, mat, gpost, mod_ref, c_ga, acc_ga, acc_g):
    ym, rm = _rms(mat)
    ga = _modrow(mod_ref, i, c_ga)
    _acc_seg(acc_ga, i, _colsum(dout * (ym * gpost)))
    dn = dout * ga
    acc_g[...] += _colsum(dn * ym)
    return _rms_bwd(dn * gpost, ym, rm)


def _normmod_bwd_vals(i, dh, xin, g, mod_ref, c_sh, c_sc, acc_sh, acc_sc, acc_g):
    y, r = _rms(xin)
    _acc_seg(acc_sc, i, _colsum(dh * (y * g)))
    _acc_seg(acc_sh, i, _colsum(dh))
    dyg = dh * (1.0 + _modrow(mod_ref, i, c_sc))
    acc_g[...] += _colsum(dyg * y)
    return _rms_bwd(dyg * g, y, r)


def _resid_bwd(name, dout, mat, gpost, mod, c_ga):
    T = dout.shape[0]

    def body(i, ins, ps, outs, acc):
        dm = _resid_bwd_vals(i, ins[0][...], ins[1][...], ps[0][...], ps[1], c_ga, acc[0], acc[1])
        outs[0][...] = dm.astype(BF16)

    return _ew(name, body, T // TR, [(dout, *_rowblk(D)), (mat, *_rowblk(D))], [gpost, mod],
               [(_sds((T, D), BF16), *_rowblk(D))], [_sds((2, D), F32), _sds((1, D), F32)])


def _normmod_resid_bwd(name, dh, xin, gpre, mod, c_sh, c_sc, dres, mat, gpost, c_ga):
    T = dh.shape[0]

    def body(i, ins, ps, outs, acc):
        dx = ins[2][...] + _normmod_bwd_vals(i, ins[0][...], ins[1][...], ps[0][...], ps[1], c_sh, c_sc,
                                             acc[0], acc[1], acc[2])
        outs[0][...] = dx
        dm = _resid_bwd_vals(i, dx, ins[3][...], ps[2][...], ps[1], c_ga, acc[3], acc[4])
        outs[1][...] = dm.astype(BF16)

    return _ew(name, body, T // TR, [(dh, *_rowblk(D)), (xin, *_rowblk(D)), (dres, *_rowblk(D)), (mat, *_rowblk(D))],
               [gpre, mod, gpost],
               [(_sds((T, D), F32), *_rowblk(D)), (_sds((T, D), BF16), *_rowblk(D))],
               [_sds((2, D), F32), _sds((2, D), F32), _sds((1, D), F32), _sds((2, D), F32), _sds((1, D), F32)])


def _normmod_bwd(name, dh, xin, gpre, mod, c_sh, c_sc, dres):
    T = dh.shape[0]

    def body(i, ins, ps, outs, acc):
        outs[0][...] = ins[2][...] + _normmod_bwd_vals(i, ins[0][...], ins[1][...], ps[0][...], ps[1], c_sh, c_sc,
                                                       acc[0], acc[1], acc[2])

    return _ew(name, body, T // TR, [(dh, *_rowblk(D)), (xin, *_rowblk(D)), (dres, *_rowblk(D))], [gpre, mod],
               [(_sds((T, D), F32), *_rowblk(D))], [_sds((2, D), F32), _sds((2, D), F32), _sds((1, D), F32)])


def _gate_fwd(name, p, ya, yb):
    T = ya.shape[0]

    def body(i, ins, ps, outs, acc):
        gl = [r[...].astype(F32) for r in ins[:4]]
        ga = _sigmoid(jnp.concatenate(gl[:2], axis=1))
        gb = _sigmoid(jnp.concatenate(gl[2:], axis=1))
        outs[0][...] = (ga * ins[4][...] + gb * ins[5][...]).astype(BF16)

    return _ew(name, body, T // TR,
               [(p, *_rowblk(GLB, COL_GL // GLB + q)) for q in range(4)] + [(ya, *_rowblk(D)), (yb, *_rowblk(D))],
               [], [(_sds((T, D), BF16), *_rowblk(D))])[0]


def _gate_bwd(name, p, ya, yb, dz):
    T = ya.shape[0]

    def kern(gl_ref, ya_ref, yb_ref, dz_ref, dya_ref, dyb_ref, dp_ref):
        j = pl.program_id(1)
        g = _sigmoid(gl_ref[...].astype(F32))
        dzv = dz_ref[...]
        dbranch = (dzv * g).astype(BF16)
        dg = dzv * g * (1.0 - g)

        @pl.when(j < 2)
        def _():
            dya_ref[...] = dbranch
            dp_ref[...] = (dg * ya_ref[...]).astype(BF16)

        @pl.when(j >= 2)
        def _():
            dyb_ref[...] = dbranch
            dp_ref[...] = (dg * yb_ref[...]).astype(BF16)

    rt = T // 4
    first = pl.BlockSpec((rt, GLB), lambda i, j: (i, jnp.minimum(j, 1)))
    second = pl.BlockSpec((rt, GLB), lambda i, j: (i, jnp.maximum(j - 2, 0)))
    return pl.pallas_call(
        kern, name=name, grid=(4, 4),
        in_specs=[pl.BlockSpec((rt, GLB), lambda i, j: (i, COL_GL // GLB + j)), first, second,
                  pl.BlockSpec((rt, GLB), lambda i, j: (i, j % 2))],
        out_specs=[first, second, pl.BlockSpec((rt, GLB), lambda i, j: (i, COL_GL // GLB + j))],
        out_shape=[_sds((T, D), BF16), _sds((T, D), BF16), _sds((T, P_W), BF16)],
        compiler_params=_params(("arbitrary", "arbitrary")),
    )(p, ya, yb, dz)


def _swiglu_fwd(name, f):
    T = f.shape[0]

    def body(i, ins, ps, outs, acc):
        outs[0][...] = (_silu(ins[0][...].astype(F32)) * ins[1][...].astype(F32)).astype(BF16)

    return _ew(name, body, T // TR, [(f, *_rowblk(D_FF, 0)), (f, *_rowblk(D_FF, 1))], [],
               [(_sds((T, D_FF), BF16), *_rowblk(D_FF))])[0]


def _swiglu_bwd(name, f, ds):
    T = f.shape[0]

    def body(i, ins, ps, outs, acc):
        gate, up, dsv = ins[0][...].astype(F32), ins[1][...].astype(F32), ins[2][...].astype(F32)
        dgate = dsv * up * _silu_grad(gate)
        dup = dsv * _silu(gate)
        outs[0][...] = jnp.concatenate([dgate, dup], axis=1).astype(BF16)

    return _ew(name, body, T // TR, [(f, *_rowblk(D_FF, 0)), (f, *_rowblk(D_FF, 1)), (ds, *_rowblk(D_FF))], [],
               [(_sds((T, 2 * D_FF), BF16), *_rowblk(2 * D_FF))])[0]


AB = 128
CTX_BLKS = CTX // AB


def _rope_tables(S):
    pos = jnp.arange(S, dtype=jnp.int32)
    inv = ROPE_BASE ** (-jnp.arange(N_FREQ, dtype=F32) / N_FREQ)
    ang_r = (pos // GRID_W).astype(F32)[:, None] * inv[None, :]
    ang_c = (pos % GRID_W).astype(F32)[:, None] * inv[None, :]
    cos = jnp.concatenate([jnp.cos(ang_r)] * 2 + [jnp.cos(ang_c)] * 2, axis=1)
    sin = jnp.concatenate([-jnp.sin(ang_r), jnp.sin(ang_r), -jnp.sin(ang_c), jnp.sin(ang_c)], axis=1)
    return cos, sin


def _rope(x, cos, sin):
    w = x.shape[1]
    reps = w // HEAD
    lane = lax.broadcasted_iota(jnp.int32, x.shape, 1)
    partner = jnp.where((lane & 63) < 32, pltpu.roll(x, w - 32, 1), pltpu.roll(x, 32, 1))
    return x * jnp.tile(cos, (1, reps)) + partner * jnp.tile(sin, (1, reps))


def _unrope(dx, cos, sin):
    w = dx.shape[1]
    reps = w // HEAD
    lane = lax.broadcasted_iota(jnp.int32, dx.shape, 1)
    t = dx * jnp.tile(sin, (1, reps))
    partner = jnp.where((lane & 63) < 32, pltpu.roll(t, w - 32, 1), pltpu.roll(t, 32, 1))
    return dx * jnp.tile(cos, (1, reps)) + partner


def _qkv_prep(name, p, cos, sin, S):
    T = CTX + S
    nb = S // AB
    KW = N_KV * HEAD

    def with_ones(v):
        ones = jnp.ones((AB, HEAD), BF16)
        return jnp.concatenate([v[:, kh * HEAD:(kh + 1) * HEAD] if part == 0 else ones
                                for kh in range(N_KV) for part in range(2)], axis=1)

    def kern(q_ref, k_ref, v_ref, cos_ref, sin_ref, qa_ref, kp_ref, vp_ref, kc_ref, vc_ref):
        i = pl.program_id(0)
        cos_v, sin_v = cos_ref[...], sin_ref[...]
        @pl.when(i < CTX_BLKS)
        def _():
            qa_ref[...] = (q_ref[...].astype(F32) * ATT_SCALE).astype(BF16)
            kc_ref[...] = k_ref[...]
            vc_ref[...] = with_ones(v_ref[...])
            kp_ref[...] = jnp.zeros(kp_ref.shape, BF16)
            vp_ref[...] = jnp.zeros(vp_ref.shape, BF16)

        @pl.when(i >= CTX_BLKS)
        def _():
            qa_ref[...] = (_rope(q_ref[...].astype(F32), cos_v, sin_v) * ATT_SCALE).astype(BF16)
            kp_ref[...] = _rope(k_ref[...].astype(F32), cos_v, sin_v).astype(BF16)
            vp_ref[...] = with_ones(v_ref[...])

    lat_map = lambda i: (jnp.maximum(i - CTX_BLKS, 0), 0)
    pad_map = lambda i: (jnp.where(i == 0, 0, jnp.where(i == 1, nb + 1, i - 1)), 0)
    ctx_map = lambda i: (jnp.minimum(i, CTX_BLKS - 1), 0)
    return pl.pallas_call(
        kern, name=name, grid=(T // AB,),
        in_specs=[pl.BlockSpec((AB, N_Q * HEAD), lambda i: (i, COL_Q // (N_Q * HEAD))),
                  pl.BlockSpec((AB, KW), lambda i: (i, COL_K // KW)),
                  pl.BlockSpec((AB, KW), lambda i: (i, COL_V // KW)),
                  pl.BlockSpec((AB, HEAD), lat_map), pl.BlockSpec((AB, HEAD), lat_map)],
        out_specs=[pl.BlockSpec((AB, N_Q * HEAD), lambda i: (i, 0)),
                   pl.BlockSpec((AB, KW), pad_map), pl.BlockSpec((AB, 2 * KW), pad_map),
                   pl.BlockSpec((AB, KW), ctx_map), pl.BlockSpec((AB, 2 * KW), ctx_map)],
        out_shape=[_sds((T, N_Q * HEAD), BF16), _sds((S + 2 * AB, KW), BF16), _sds((S + 2 * AB, 2 * KW), BF16),
                   _sds((CTX, KW), BF16), _sds((CTX, 2 * KW), BF16)],
        compiler_params=_params(),
    )(p, p, p, cos, sin)


GW = Q_PER_KV * HEAD


def _band_bias(S):
    r = jnp.arange(AB, dtype=jnp.int32)[:, None]
    c = jnp.arange(3 * AB, dtype=jnp.int32)[None, :]
    near = jnp.abs(c - AB - r) <= AB
    valid = jnp.stack([near & (c >= AB), near, near & (c < 2 * AB)])
    return jnp.where(valid, 0.0, NEG_INF).astype(F32)


def _bias_spec(S):
    nb = S // AB
    return pl.BlockSpec((None, AB, 3 * AB), lambda kh, n: (jnp.where(n == 0, 0, jnp.where(n == nb - 1, 2, 1)), 0, 0))


def _head_probs(q, sink, kc, vce, kb, vbe, bias):
    s_c = _dot(q, kc, NT)
    m = jnp.maximum(jnp.max(s_c, axis=-1, keepdims=True), sink)
    if kb is not None:
        s_b = _dot(q, kb, NT) + bias
        m = jnp.maximum(m, jnp.max(s_b, axis=-1, keepdims=True))
    p_c = jnp.exp(s_c - m).astype(BF16)
    acc = _dot(p_c, vce)
    p_b = None
    if kb is not None:
        p_b = jnp.exp(s_b - m).astype(BF16)
        acc = acc + _dot(p_b, vbe)
    return p_c, p_b, m, acc


def _attn_fwd(name, qa, kc, vc, sink4, S, band=None, prev=None, carry=None):
    T = qa.shape[0]
    has_band = band is not None
    nq = S // AB if has_band else CTX_BLKS
    q_off = CTX_BLKS if has_band else 0

    def kern(*refs):
        q_ref, kc_ref, vc_ref, sink_ref = refs[:4]
        rest = refs[4:]
        o_ref = rest[-1]
        n = pl.program_id(1)
        kc_v, vce = kc_ref[...], vc_ref[...]
        kb = vbe = bias = None
        if has_band:
            kp_ref, vp_ref, bias_ref = rest[:3]
            start = pl.multiple_of(n * AB, AB)
            kb = kp_ref[pl.ds(start, 3 * AB), :]
            vbe = vp_ref[pl.ds(start, 3 * AB), :]
            bias = bias_ref[...]
        outs = []
        for g in range(Q_PER_KV):
            sink = sink_ref[g:g + 1, 0:1]
            _, _, m, acc = _head_probs(q_ref[:, g * HEAD:(g + 1) * HEAD], sink, kc_v, vce, kb, vbe, bias)
            l = acc[:, HEAD:] + jnp.exp(sink - m)
            outs.append(acc[:, :HEAD] / l)
        o_ref[...] = jnp.concatenate(outs, axis=1).astype(BF16)

    in_specs = [pl.BlockSpec((AB, GW), lambda kh, n: (n + q_off, kh)),
                pl.BlockSpec((CTX, HEAD), lambda kh, n: (0, kh)), pl.BlockSpec((CTX, 2 * HEAD), lambda kh, n: (0, kh)),
                pl.BlockSpec((None, Q_PER_KV, HEAD), lambda kh, n: (kh, 0, 0))]
    args = [qa, kc, vc, sink4]
    if has_band:
        in_specs += [pl.BlockSpec((S + 2 * AB, HEAD), lambda kh, n: (0, kh)),
                     pl.BlockSpec((S + 2 * AB, 2 * HEAD), lambda kh, n: (0, kh)), _bias_spec(S)]
        args += list(band)
    alias = {}
    if prev is not None:
        in_specs.append(ANY)
        alias = {len(args): 0}
        args.append(prev)
    ci, ca, co, cs, cscr = _carry_args(carry)
    res = pl.pallas_call(
        _carried(kern, carry, len(args), 1, *_grid_ends((N_KV, nq))), name=name, grid=(N_KV, nq),
        in_specs=in_specs + ci,
        out_specs=[pl.BlockSpec((AB, GW), lambda kh, n: (n + q_off, kh))] + co,
        out_shape=[_sds((T, N_Q * HEAD), BF16)] + cs, input_output_aliases=alias, scratch_shapes=cscr,
        compiler_params=_params(("arbitrary", "arbitrary")),
    )(*args, *ca)
    return res[0] if carry is None else (res[0], res[1:])


def _attn_bwd(name, qa, kc, vc, sink4, o_all, do_all, S, band=None, prev_dq=None, carry=None):
    T = qa.shape[0]
    has_band = band is not None
    nq = S // AB if has_band else CTX_BLKS
    q_off = CTX_BLKS if has_band else 0
    KW = N_KV * HEAD

    def kern(*refs):
        q_ref, kc_ref, vc_ref, sink_ref, o_ref, do_ref = refs[:6]
        rest = refs[6:]
        if has_band:
            kp_ref, vp_ref, bias_ref = rest[:3]
            rest = rest[3:]
        if prev_dq is not None:
            rest = rest[1:]
        dq_ref, dkc_ref, dvc_ref, dsink_ref = rest[:4]
        n = pl.program_id(1)

        @pl.when(n == 0)
        def _():
            dkc_ref[...] = jnp.zeros(dkc_ref.shape, F32)
            dvc_ref[...] = jnp.zeros(dvc_ref.shape, F32)
            dsink_ref[...] = jnp.zeros(dsink_ref.shape, F32)
            if has_band:
                rest[4][...] = jnp.zeros(rest[4].shape, F32)
                rest[5][...] = jnp.zeros(rest[5].shape, F32)

        kc_v, vce = kc_ref[...], vc_ref[...]
        vc_v = vce[:, :HEAD]
        kb = vbe = vb = bias = None
        if has_band:
            start = pl.multiple_of(n * AB, AB)
            kb = kp_ref[pl.ds(start, 3 * AB), :]
            vbe = vp_ref[pl.ds(start, 3 * AB), :]
            vb = vbe[:, :HEAD]
            bias = bias_ref[...]
        stack = lambda ref: jnp.concatenate([ref[:, g * HEAD:(g + 1) * HEAD] for g in range(Q_PER_KV)], axis=0)
        q4, do4 = stack(q_ref), stack(do_ref)
        sink = jnp.concatenate([jnp.broadcast_to(sink_ref[g:g + 1, 0:1], (AB, 1)) for g in range(Q_PER_KV)], axis=0)
        s_c = _dot(q4, kc_v, NT)
        m = jnp.maximum(jnp.max(s_c, axis=-1, keepdims=True), sink)
        if has_band:
            s_b = _dot(q4, kb, NT) + jnp.tile(bias, (Q_PER_KV, 1))
            m = jnp.maximum(m, jnp.max(s_b, axis=-1, keepdims=True))
        p_c = jnp.exp(s_c - m).astype(BF16).astype(F32)
        p_sink = jnp.exp(sink - m)
        l = jnp.sum(p_c, axis=-1, keepdims=True) + p_sink
        if has_band:
            p_b = jnp.exp(s_b - m).astype(BF16).astype(F32)
            l = l + jnp.sum(p_b, axis=-1, keepdims=True)
        inv = 1.0 / l
        delta = jnp.sum(do4 * stack(o_ref).astype(F32), axis=-1, keepdims=True)
        do4b = do4.astype(BF16)
        pn_c = (p_c * inv).astype(BF16)
        ds_c = (p_c * inv * (_dot(do4b, vc_v, NT) - delta)).astype(BF16)
        dq4 = _dot(ds_c, kc_v)
        dkc_ref[...] += _dot(ds_c, q4, TN)
        dvc_ref[...] += _dot(pn_c, do4b, TN)
        if has_band:
            pn_b = (p_b * inv).astype(BF16)
            ds_b = (p_b * inv * (_dot(do4b, vb, NT) - delta)).astype(BF16)
            dq4 = dq4 + _dot(ds_b, kb)
            rest[4][pl.ds(start, 3 * AB), :] += _dot(ds_b, q4, TN)
            rest[5][pl.ds(start, 3 * AB), :] += _dot(pn_b, do4b, TN)
        dq4 = dq4 * ATT_SCALE
        dq_ref[...] = jnp.concatenate([dq4[g * AB:(g + 1) * AB, :] for g in range(Q_PER_KV)], axis=1)
        ps = p_sink * inv * delta
        dsink_ref[...] += jnp.concatenate(
            [jnp.broadcast_to(-jnp.sum(ps[g * AB:(g + 1) * AB, :], axis=0, keepdims=True), (1, HEAD))
             for g in range(Q_PER_KV)], axis=0)

    q_spec = pl.BlockSpec((AB, GW), lambda kh, n: (n + q_off, kh))
    c_spec = pl.BlockSpec((CTX, HEAD), lambda kh, n: (0, kh))
    ce_spec = pl.BlockSpec((CTX, 2 * HEAD), lambda kh, n: (0, kh))
    s_spec = pl.BlockSpec((None, Q_PER_KV, HEAD), lambda kh, n: (kh, 0, 0))
    in_specs = [q_spec, c_spec, ce_spec, s_spec, q_spec, q_spec]
    args = [qa, kc, vc, sink4, o_all, do_all]
    out_specs = [q_spec, c_spec, c_spec, s_spec]
    out_shape = [_sds((T, N_Q * HEAD), F32), _sds((CTX, KW), F32), _sds((CTX, KW), F32), _sds((N_KV, Q_PER_KV, HEAD), F32)]
    if has_band:
        p_spec = pl.BlockSpec((S + 2 * AB, HEAD), lambda kh, n: (0, kh))
        in_specs += [p_spec, pl.BlockSpec((S + 2 * AB, 2 * HEAD), lambda kh, n: (0, kh)), _bias_spec(S)]
        args += list(band)
        out_specs += [p_spec, p_spec]
        out_shape += [_sds((S + 2 * AB, KW), F32)] * 2
    alias = {}
    if prev_dq is not None:
        in_specs.append(ANY)
        alias = {len(args): 0}
        args.append(prev_dq)
    ci, ca, co, cs, cscr = _carry_args(carry)
    n_out = len(out_specs)
    res = pl.pallas_call(
        _carried(kern, carry, len(args), n_out, *_grid_ends((N_KV, nq))), name=name, grid=(N_KV, nq),
        in_specs=in_specs + ci, out_specs=out_specs + co, out_shape=out_shape + cs, scratch_shapes=cscr,
        input_output_aliases=alias, compiler_params=_params(("arbitrary", "arbitrary")),
    )(*args, *ca)
    return res if carry is None else (res[:n_out], res[n_out:])


def _dqkv_assemble(name, dp, dq_all, dkp, dvp, dkc_l, dvc_l, dkc_c, dvc_c, cos, sin, S):
    T = CTX + S
    KW = N_KV * HEAD
    HALF = N_Q * HEAD // 2

    def kern(dq_ref, dkp_ref, dvp_ref, dkcl_ref, dvcl_ref, dkcc_ref, dvcc_ref, cos_ref, sin_ref, dp_in, out_ref):
        i = pl.program_id(0)
        j = pl.program_id(1)
        lat = i >= CTX_BLKS
        cos_v, sin_v = cos_ref[...], sin_ref[...]

        @pl.when(j < 2)
        def _():
            dq = dq_ref[...]
            out_ref[...] = jnp.where(lat, _unrope(dq, cos_v, sin_v), dq).astype(BF16)

        @pl.when(j == 2)
        def _():
            dk = jnp.where(lat, _unrope(dkp_ref[...], cos_v, sin_v), dkcl_ref[...] + dkcc_ref[...])
            dv = jnp.where(lat, dvp_ref[...], dvcl_ref[...] + dvcc_ref[...])
            out_ref[...] = jnp.concatenate([dk, dv], axis=1).astype(BF16)

    lat_map = lambda i, j: (jnp.maximum(i - CTX_BLKS, 0), 0)
    pad_map = lambda i, j: (jnp.maximum(i - 1, 0), 0)
    ctx_map = lambda i, j: (jnp.minimum(i, CTX_BLKS - 1), 0)
    return pl.pallas_call(
        kern, name=name, grid=(T // AB, 3),
        in_specs=[pl.BlockSpec((AB, HALF), lambda i, j: (i, jnp.minimum(j, 1))),
                  pl.BlockSpec((AB, KW), pad_map), pl.BlockSpec((AB, KW), pad_map),
                  pl.BlockSpec((AB, KW), ctx_map), pl.BlockSpec((AB, KW), ctx_map),
                  pl.BlockSpec((AB, KW), ctx_map), pl.BlockSpec((AB, KW), ctx_map),
                  pl.BlockSpec((AB, HEAD), lat_map), pl.BlockSpec((AB, HEAD), lat_map), ANY],
        out_specs=pl.BlockSpec((AB, HALF), lambda i, j: (i, COL_Q // HALF + j)),
        out_shape=_sds((T, P_W), BF16), input_output_aliases={9: 0},
        compiler_params=_params(("arbitrary", "arbitrary")),
    )(dq_all, dkp, dvp, dkc_l, dvc_l, dkc_c, dvc_c, cos, sin, dp)


RB = 128
CH = 256
HALO = 8
SUB = 8
GRP = 8


def _vscan(a, b, reverse):
    row = lax.broadcasted_iota(jnp.int32, a.shape, 0)
    A, H = a, b
    for s in (1, 2, 4):
        sh = SUB - s if reverse else s
        m = (row < SUB - s) if reverse else (row >= s)
        As = pltpu.roll(A, sh, 0)
        Hs = pltpu.roll(H, sh, 0)
        H = jnp.where(m, A * Hs + H, H)
        A = jnp.where(m, A * As, A)
    return A, H


def _scan_rows(a_ref, b_ref, r0, nrows, reverse, carry, emit):
    ngrp = nrows // (SUB * GRP)
    row = lax.broadcasted_iota(jnp.int32, (SUB, RB), 0)

    def grp(gi, carry):
        g = (ngrp - 1 - gi) if reverse else gi
        base = r0 + g * (SUB * GRP)
        for v in (range(GRP - 1, -1, -1) if reverse else range(GRP)):
            rs = pl.multiple_of(base + v * SUB, SUB)
            A, H = _vscan(a_ref[pl.ds(rs, SUB), :], b_ref[pl.ds(rs, SUB), :], reverse)
            hf = H + A * carry
            if reverse:
                before = jnp.where(row == SUB - 1, carry, pltpu.roll(hf, SUB - 1, 0))
                carry = hf[0:1, :]
            else:
                before = jnp.where(row == 0, carry, pltpu.roll(hf, 1, 0))
                carry = hf[SUB - 1:SUB, :]
            emit(rs, hf, before)
        return carry

    return lax.fori_loop(0, ngrp, grp, carry)


def _pad_start(ci):
    return pl.multiple_of(ci * CH + HALO * jnp.minimum(ci, 1), HALO)


def _conv_taps(ext, transpose=False):
    n = CH + 2 * HALO
    taps = []
    for k in range(CONV_W):
        off = CONV_LEFT - k if transpose else k - CONV_LEFT
        taps.append(ext[HALO:HALO + CH, :] if off == 0 else pltpu.roll(ext, (-off) % n, 0)[HALO:HALO + CH, :])
    return taps


def _lru_gates(xl, w4, b4, ls):
    pre = _dot(xl.astype(BF16), w4) + b4
    out = []
    for d in range(2):
        r = _sigmoid(pre[:, d * RB:(d + 1) * RB])
        i = _sigmoid(pre[:, (2 + d) * RB:(3 + d) * RB])
        la = LRU_C * r * ls[d:d + 1, :]
        a = jnp.exp(la)
        q = -jnp.tanh(la) * (1.0 + a * a)
        out.append((r, i, a, q))
    return out


def _rnn_specs(T):
    col = lambda n, *_: (0, n)
    return dict(
        xr=pl.BlockSpec((T, RB), lambda n, *_: (0, COL_XR // RB + n)),
        gr=pl.BlockSpec((T, RB), lambda n, *_: (0, COL_GR // RB + n)),
        act=pl.BlockSpec((T, RB), col),
        cw=pl.BlockSpec((CONV_W, RB), col), cb=pl.BlockSpec((1, RB), col),
        w4=pl.BlockSpec((None, RB, 4 * RB), lambda n, *_: (n, 0, 0)),
        b4=pl.BlockSpec((None, 1, 4 * RB), lambda n, *_: (n, 0, 0)),
        lam=pl.BlockSpec((2, RB), col))


PAD_ROWS = 3 * HALO


def _zero_pads(pad_ref, T):
    for r in (0, HALO + CTX, 2 * HALO + T):
        pad_ref[r:r + HALO, :] = jnp.zeros((HALO, RB), F32)


def _fill_padded(pad_ref, src_ref, T):
    _zero_pads(pad_ref, T)
    pad_ref[HALO:HALO + CTX, :] = src_ref[0:CTX, :].astype(F32)
    pad_ref[2 * HALO + CTX:2 * HALO + T, :] = src_ref[CTX:T, :].astype(F32)


def _pad_rows(ci):
    return pl.ds(pl.multiple_of(ci * CH + HALO + HALO * jnp.minimum(ci, 1), HALO), CH)


def _rnn_fwd(name, p, cw, cb, w4, b4, lam, T, carry=None):
    def kern(xr_ref, gr_ref, cw_ref, cb_ref, w4_ref, b4_ref, lam_ref, u_ref, hpf_ref, hpb_ref,
             xpad, a0, b0, a1, b1, y):
        _fill_padded(xpad, xr_ref, T)
        ls = _log_sigmoid(lam_ref[...])
        w4v, b4v, cwv, cbv = w4_ref[...], b4_ref[...], cw_ref[...], cb_ref[...]

        def chunk(ci, _):
            base = pl.multiple_of(ci * CH, CH)
            taps = _conv_taps(xpad[pl.ds(_pad_start(ci), CH + 2 * HALO), :])
            xl = cbv + sum(taps[k] * cwv[k:k + 1, :] for k in range(CONV_W))
            for d, (r, i, a, q) in enumerate(_lru_gates(xl, w4v, b4v, ls)):
                (a0, a1)[d][pl.ds(base, CH), :] = a
                (b0, b1)[d][pl.ds(base, CH), :] = jnp.sqrt(q) * (i * xl)
            return 0

        lax.fori_loop(0, T // CH, chunk, 0)
        zero = jnp.zeros((1, RB), F32)

        def emit_f(rs, hf, before):
            y[pl.ds(rs, SUB), :] = hf
            hpf_ref[pl.ds(rs, SUB), :] = before

        def emit_b(rs, hf, before):
            y[pl.ds(rs, SUB), :] += hf
            hpb_ref[pl.ds(rs, SUB), :] = before

        _scan_rows(a0, b0, 0, T, False, zero, emit_f)
        c = _scan_rows(a1, b1, 0, CTX, True, zero, emit_b)
        _scan_rows(a1, b1, CTX, T - CTX, True, c, emit_b)

        def finish(ci, _):
            base = pl.multiple_of(ci * CH, CH)
            gr = gr_ref[pl.ds(base, CH), :].astype(F32)
            u_ref[pl.ds(base, CH), :] = (y[pl.ds(base, CH), :] * _gelu(gr)).astype(BF16)
            return 0

        lax.fori_loop(0, T // CH, finish, 0)

    sp = _rnn_specs(T)
    ci, ca, co, cs, cscr = _carry_args(carry)
    res = pl.pallas_call(
        _carried(kern, carry, 7, 3, *_grid_ends((N_RNN_BLOCKS,))), name=name, grid=(N_RNN_BLOCKS,),
        in_specs=[sp["xr"], sp["gr"], sp["cw"], sp["cb"], sp["w4"], sp["b4"], sp["lam"]] + ci,
        out_specs=[sp["act"]] * 3 + co,
        out_shape=[_sds((T, D), BF16), _sds((T, D), F32), _sds((T, D), F32)] + cs,
        scratch_shapes=[pltpu.VMEM((T + PAD_ROWS, RB), F32)] + [pltpu.VMEM((T, RB), F32)] * 5 + cscr,
        compiler_params=_params(),
    )(p, p, cw, cb, w4, b4, lam, *ca)
    return res if carry is None else (res[:3], res[3:])


def _rnn_bwd(name, p, du, hpf, hpb, dp, cw, cb, w4, b4, lam, T, carry=None):
    def kern(xr_ref, gr_ref, du_ref, hpf_ref, hpb_ref, cw_ref, cb_ref, w4_ref, b4_ref, lam_ref, dp_in,
             dp_ref, dcw_ref, dcb_ref, dw4_ref, db4_ref, dlam_ref,
             xpad, dxpad, a0, a1, c0, c1, dy, dgr_ref):
        j = pl.program_id(1)

        @pl.when(j == 0)
        def _():
            work(xr_ref, gr_ref, du_ref, hpf_ref, hpb_ref, cw_ref, cb_ref, w4_ref, b4_ref, lam_ref,
                 dp_ref, dgr_ref, dcw_ref, dcb_ref, dw4_ref, db4_ref, dlam_ref, xpad, dxpad, a0, a1, c0, c1, dy)

        @pl.when(j == 1)
        def _():
            dp_ref[...] = dgr_ref[...]

    def work(xr_ref, gr_ref, du_ref, hpf_ref, hpb_ref, cw_ref, cb_ref, w4_ref, b4_ref, lam_ref,
             dxr_ref, dgr_ref, dcw_ref, dcb_ref, dw4_ref, db4_ref, dlam_ref, xpad, dxpad, a0, a1, c0, c1, dy):
        _fill_padded(xpad, xr_ref, T)
        _zero_pads(dxpad, T)
        lam_v = lam_ref[...]
        ls = _log_sigmoid(lam_v)
        w4v, b4v, cwv, cbv = w4_ref[...], b4_ref[...], cw_ref[...], cb_ref[...]

        def conv_chunk(ci):
            taps = _conv_taps(xpad[pl.ds(_pad_start(ci), CH + 2 * HALO), :])
            return taps, cbv + sum(taps[k] * cwv[k:k + 1, :] for k in range(CONV_W))

        def phase_a(ci, _):
            base = pl.multiple_of(ci * CH, CH)
            rows = pl.ds(base, CH)
            _, xl = conv_chunk(ci)
            (r0, i0, av0, q0), (r1, i1, av1, q1) = _lru_gates(xl, w4v, b4v, ls)
            yv = ((av0 * hpf_ref[rows, :] + jnp.sqrt(q0) * (i0 * xl))
                  + (av1 * hpb_ref[rows, :] + jnp.sqrt(q1) * (i1 * xl)))
            gr = gr_ref[rows, :].astype(F32)
            duv = du_ref[rows, :]
            dyv = duv * _gelu(gr)
            dgr_ref[rows, :] = (duv * yv * _gelu_grad(gr)).astype(BF16)
            dy[rows, :] = dyv
            a0[rows, :] = av0
            a1[rows, :] = av1
            c0[rows, :] = av0 * dyv
            c1[rows, :] = av1 * dyv
            return 0

        lax.fori_loop(0, T // CH, phase_a, 0)
        zero = jnp.zeros((1, RB), F32)

        def emit0(rs, hf, before):
            c0[pl.ds(rs, SUB), :] = dy[pl.ds(rs, SUB), :] + before

        def emit1(rs, hf, before):
            c1[pl.ds(rs, SUB), :] = dy[pl.ds(rs, SUB), :] + before

        _scan_rows(a0, c0, 0, T, True, zero, emit0)
        c = _scan_rows(a1, c1, CTX, T - CTX, False, zero, emit1)
        _scan_rows(a1, c1, 0, CTX, False, c, emit1)

        dw4_ref[...] = jnp.zeros(dw4_ref.shape, F32)
        db4_ref[...] = jnp.zeros(db4_ref.shape, F32)
        dlam_ref[...] = jnp.zeros(dlam_ref.shape, F32)
        dcw_ref[...] = jnp.zeros(dcw_ref.shape, F32)
        dcb_ref[...] = jnp.zeros(dcb_ref.shape, F32)

        def phase_c(ci, _):
            base = pl.multiple_of(ci * CH, CH)
            rows = pl.ds(base, CH)
            _, xl = conv_chunk(ci)
            gates = _lru_gates(xl, w4v, b4v, ls)
            dxl = jnp.zeros((CH, RB), F32)
            dpre_a, dpre_x, dls = [], [], []
            for d, (r, i, a, q) in enumerate(gates):
                g = (c0, c1)[d][rows, :]
                hp = (hpf_ref, hpb_ref)[d][rows, :]
                gm = g * jnp.sqrt(q)
                di = gm * xl
                dxl = dxl + gm * i
                dla = a * (g * hp - a * (g * (i * xl)) * lax.rsqrt(q))
                dr = dla * (LRU_C * ls[d:d + 1, :])
                dls.append(_colsum(dla * (LRU_C * r)))
                dpre_a.append(dr * r * (1.0 - r))
                dpre_x.append(di * i * (1.0 - i))
            dpre = jnp.concatenate(dpre_a + dpre_x, axis=1)
            dpre_b = dpre.astype(BF16)
            dxl = dxl + _dot(dpre_b, w4v, NT)
            dw4_ref[...] += _dot(xl.astype(BF16), dpre_b, TN)
            db4_ref[...] += _colsum(dpre)
            dlam_ref[...] += jnp.concatenate(dls, axis=0)
            dcb_ref[...] += _colsum(dxl)
            dxpad[_pad_rows(ci), :] = dxl
            return 0

        lax.fori_loop(0, T // CH, phase_c, 0)
        dlam_ref[...] = dlam_ref[...] * _sigmoid(-lam_v)

        def phase_d(ci, _):
            base = pl.multiple_of(ci * CH, CH)
            rows = pl.ds(base, CH)
            xtaps, _ = conv_chunk(ci)
            dtaps = _conv_taps(dxpad[pl.ds(_pad_start(ci), CH + 2 * HALO), :], transpose=True)
            dxl = dxpad[_pad_rows(ci), :]
            dxr_ref[rows, :] = sum(dtaps[k] * cwv[k:k + 1, :] for k in range(CONV_W)).astype(BF16)
            dcw_ref[...] += jnp.concatenate([_colsum(dxl * xtaps[k]) for k in range(CONV_W)], axis=0)
            return 0

        lax.fori_loop(0, T // CH, phase_d, 0)

    sp = _rnn_specs(T)
    dp_spec = pl.BlockSpec((T, RB), lambda n, j: (0, COL_XR // RB + n + j * (COL_GR - COL_XR) // RB))
    ci, ca, co, cs, cscr = _carry_args(carry)
    res = pl.pallas_call(
        _carried(kern, carry, 11, 6, *_grid_ends((N_RNN_BLOCKS, 2))), name=name, grid=(N_RNN_BLOCKS, 2),
        in_specs=[sp["xr"], sp["gr"], sp["act"], sp["act"], sp["act"], sp["cw"], sp["cb"], sp["w4"], sp["b4"],
                  sp["lam"], ANY] + ci,
        out_specs=[dp_spec, sp["cw"], sp["cb"], sp["w4"], sp["b4"], sp["lam"]] + co,
        out_shape=[_sds((T, P_W), BF16), _sds((CONV_W, D), F32), _sds((1, D), F32),
                   _sds((N_RNN_BLOCKS, RB, 4 * RB), F32), _sds((N_RNN_BLOCKS, 1, 4 * RB), F32), _sds((2, D), F32)] + cs,
        scratch_shapes=([pltpu.VMEM((T + PAD_ROWS, RB), F32)] * 2 + [pltpu.VMEM((T, RB), F32)] * 5
                        + [pltpu.VMEM((T, RB), BF16)] + cscr),
        input_output_aliases={10: 0},
        compiler_params=_params(("arbitrary", "arbitrary")),
    )(p, p, du, hpf, hpb, cw, cb, w4, b4, lam, dp, *ca)
    return res if carry is None else (res[:6], res[6:])


class _Plan:
    def __init__(self, shards, Ws):
        L = len(Ws)
        self.shards, self.Ws = shards, Ws
        self.Gs = [None] * L
        self.slots = [dict() for _ in range(L)]
        self.gate_slots = [None] * L
        self.table = {}
        for l in range(L):
            t = f"l{l}_"
            self.table[t + "proj"] = [("gather", l, k) for k in ("wo_rnn", "wo_attn", "wout")]
            self.table[t + "rnn_fwd"] = [("gather", l, "wffn_in_t")]
            self.table[t + "attn_lat_fwd"] = [("gather", l + 1, "win_t")] if l + 1 < L else []
            self.table[t + "ffn_in"] = [("gather", l, "wffn_out")]
            self.table[t + "ffn_in_dx"] = [("scatter", l, "wffn_out")]
            self.table[t + "attn_lat_bwd"] = [("scatter", l, "wffn_in_t")]
            self.table[t + "rnn_bwd"] = ([("scatter", l, k) for k in ("wout", "wo_attn", "wo_rnn")]
                                         + ([("scatter", l + 1, "win_t"), ("gates", l + 1, "w4")] if l + 1 < L else []))
        self.table["l0_proj_dx"] = [("scatter", 0, "win_t_a")]
        self.table["l0_proj_dw_b"] = [("gates", 0, "w4")]

    def carry(self, name):
        jobs = []
        for kind, l, k in self.table.get(name, []):
            if kind == "gather":
                jobs.append(("gather", self.shards[l][k]))
            elif kind == "scatter":
                jobs.append(("scatter", self.Gs[l][k].reshape(N_DEV, -1, self.Gs[l][k].shape[-1])))
            else:
                jobs.append(("gather", self.Gs[l]["w4"].reshape(N_RNN_BLOCKS * RB, 4 * RB).astype(BF16)))
        return _Carry(jobs) if jobs else None

    def done(self, name, got):
        for (kind, l, k), res in zip(self.table[name], got):
            if kind == "gather":
                self.Ws[l][k] = res.reshape(-1, D)
            elif kind == "scatter":
                self.slots[l][k] = res
            else:
                self.gate_slots[l] = res


def _run(X, fn, name, *args, **kw):
    carry = None if X is None else X.carry(name)
    if carry is None:
        return fn(name, *args, **kw)
    out, got = fn(name, *args, carry=carry, **kw)
    X.done(name, got)
    return out


def _layer_fwd(l, xa, h, W, rope, S, nxt, X=None):
    T = xa.shape[0]
    tag = f"l{l}_"
    cos, sin, bias = rope
    p = _run(X, _mm_act, tag + "proj", h, W["win_t"], "NT", BF16)
    u, hpf, hpb = _run(X, _rnn_fwd, tag + "rnn_fwd", p, W["cw"], W["cb"], W["w4"], W["b4"], W["lam"], T)
    qa, kp, vp, kc, vc = _qkv_prep(tag + "qkv_prep", p, cos, sin, S)
    o_all = _attn_fwd(tag + "attn_ctx_fwd", qa, kc, vc, W["sink4"], S)
    o_all = _run(X, _attn_fwd, tag + "attn_lat_fwd", qa, kc, vc, W["sink4"], S, band=(kp, vp, bias), prev=o_all)
    ya = _mm_act(tag + "o_rnn", u, W["wo_rnn"], "NN")
    yb = _mm_act(tag + "o_attn", o_all, W["wo_attn"], "NN")
    z = _gate_fwd(tag + "gate_fwd", p, ya, yb)
    m = _mm_act(tag + "out", z, W["wout"], "NN")
    x1, h2 = _resid_norm_fwd(tag + "mix_resid", xa, m, W["g_mix_post"], W["mod"], GA1, W["g_ffn_pre"], W["mod"], SH2, SC2)
    f = _run(X, _mm_act, tag + "ffn_in", h2, W["wffn_in_t"], "NT", BF16)
    s = _swiglu_fwd(tag + "swiglu_fwd", f)
    e = _mm_act(tag + "ffn_out", s, W["wffn_out"], "NN")
    saved = dict(xa=xa, h=h, p=p, u=u, hpf=hpf, hpb=hpb, qa=qa, kp=kp, vp=vp, kc=kc, vc=vc, o_all=o_all,
                 ya=ya, yb=yb, z=z, m=m, x1=x1, h2=h2, f=f, s=s, e=e)
    if nxt[0] == "norm":
        out = _resid_norm_fwd(tag + "ffn_resid", x1, e, W["g_ffn_post"], W["mod"], GA2, nxt[1], nxt[2], SH1, SC1)
    else:
        out = _resid_loss_fwd(tag + "ffn_resid_loss", x1, e, W["g_ffn_post"], W["mod"], GA2, nxt[1])
    return saved, out


def _layer_bwd(l, dx2, A, W, rope, S, X=None):
    T = dx2.shape[0]
    tag = f"l{l}_"
    cos, sin, bias = rope
    G = {}
    if X is not None:
        X.Gs[l] = G
    de, dga2, G["g_ffn_post"] = _resid_bwd(tag + "ffn_resid_bwd", dx2, A["e"], W["g_ffn_post"], W["mod"], GA2)
    ds = _mm_act(tag + "ffn_out_dx", de, W["wffn_out"], "NT", BF16)
    G["wffn_out"] = _mm_wgrad(tag + "ffn_out_dw", A["s"], de)
    df = _swiglu_bwd(tag + "swiglu_bwd", A["f"], ds)
    dh2 = _run(X, _mm_act, tag + "ffn_in_dx", df, W["wffn_in_t"], "NN")
    G["wffn_in_t"] = _mm_wgrad(tag + "ffn_in_dw", df, A["h2"])
    dx1, dm, dsh2, dsc2, G["g_ffn_pre"], dga1, G["g_mix_post"] = _normmod_resid_bwd(
        tag + "mix_resid_bwd", dh2, A["x1"], W["g_ffn_pre"], W["mod"], SH2, SC2, dx2, A["m"], W["g_mix_post"], GA1)
    dz = _mm_act(tag + "out_dx", dm, W["wout"], "NT")
    G["wout"] = _mm_wgrad(tag + "out_dw", A["z"], dm)
    dya, dyb, dp = _gate_bwd(tag + "gate_bwd", A["p"], A["ya"], A["yb"], dz)
    do = _mm_act(tag + "o_attn_dx", dyb, W["wo_attn"], "NT")
    G["wo_attn"] = _mm_wgrad(tag + "o_attn_dw", A["o_all"], dyb)
    du = _mm_act(tag + "o_rnn_dx", dya, W["wo_rnn"], "NT")
    G["wo_rnn"] = _mm_wgrad(tag + "o_rnn_dw", A["u"], dya)
    dq_all, dkc_c, dvc_c, dsink_c = _attn_bwd(tag + "attn_ctx_bwd", A["qa"], A["kc"], A["vc"], W["sink4"],
                                               A["o_all"], do, S)
    dq_all, dkc_l, dvc_l, dsink_l, dkp, dvp = _run(
        X, _attn_bwd, tag + "attn_lat_bwd", A["qa"], A["kc"], A["vc"], W["sink4"], A["o_all"], do, S,
        band=(A["kp"], A["vp"], bias), prev_dq=dq_all)
    G["sink4"] = dsink_c + dsink_l
    dp = _dqkv_assemble(tag + "dqkv", dp, dq_all, dkp, dvp, dkc_l, dvc_l, dkc_c, dvc_c, cos, sin, S)
    dp, G["cw"], G["cb"], G["w4"], G["b4"], G["lam"] = _run(
        X, _rnn_bwd, tag + "rnn_bwd", A["p"], du, A["hpf"], A["hpb"], dp, W["cw"], W["cb"], W["w4"], W["b4"], W["lam"], T)
    if X is not None and l == 0:
        G["win_t_a"] = _mm_wgrad(tag + "proj_dw_a", dp, A["h"][:, :D // 2])
        dh = _run(X, _mm_act, tag + "proj_dx", dp, W["win_t"], "NN")
        G["win_t_b"] = _run(X, _mm_wgrad, tag + "proj_dw_b", dp, A["h"][:, D // 2:])
    else:
        dh = _mm_act(tag + "proj_dx", dp, W["win_t"], "NN")
        G["win_t"] = _mm_wgrad(tag + "proj_dw", dp, A["h"])
    dxa, dsh1, dsc1, G["g_mix_pre"] = _normmod_bwd(tag + "mix_norm_bwd", dh, A["xa"], W["g_mix_pre"], W["mod"],
                                                   SH1, SC1, dx1)
    G["mod"] = jnp.concatenate([dsh1, dsc1, dga1, dsh2, dsc2, dga2], axis=1)
    return dxa, G


def _local_step(xa, target, Ws, S, X=None):
    rope = (*_rope_tables(S), _band_bias(S))
    L = len(Ws)
    h = _normmod_fwd("l0_mix_norm", xa, Ws[0]["g_mix_pre"], Ws[0]["mod"], SH1, SC1)
    saved = []
    x = xa
    for l in range(L):
        nxt = ("norm", Ws[l + 1]["g_mix_pre"], Ws[l + 1]["mod"]) if l + 1 < L else ("loss", target)
        A, out = _layer_fwd(l, x, h, Ws[l], rope, S, nxt, X)
        saved.append(A)
        if l + 1 < L:
            x, h = out
    dx, sq = out
    Gs = [None] * L
    for l in reversed(range(L)):
        dx, Gs[l] = _layer_bwd(l, dx, saved[l], Ws[l], rope, S, X)
    return sq, dx, Gs


MESH = pl.DeviceIdType.MESH


def _place():
    return lax.axis_index("x"), lax.axis_index("y"), lax.axis_index("c")


def _lin(px, py, pc):
    return 4 * px + 2 * py + pc


def _allgather_small(name, blk):
    m, n = blk.shape

    def body(x_ref, out_ref, send_sems, recv_sems, local_sem):
        x, y, c = _place()
        me, sibling = (x, y, c), (x, y, 1 - c)
        chips = [(1 - x, y), (x, 1 - y), (1 - x, 1 - y)]

        def copy(k, block, to, src=None):
            dst = out_ref.at[_lin(*block)]
            return pltpu.make_async_remote_copy(src_ref=dst if src is None else src, dst_ref=dst,
                                                send_sem=send_sems.at[k], recv_sem=recv_sems.at[k],
                                                device_id=to, device_id_type=MESH)

        mine = pltpu.make_async_copy(x_ref, out_ref.at[_lin(*me)], local_sem)
        mine.start()
        first = [copy(0, me, sibling, src=x_ref)]
        first += [copy(1 + j, me, (*chip, c), src=x_ref) for j, chip in enumerate(chips)]
        for cp in first:
            cp.start()
        passed = [copy(4 + j, (*chip, c), sibling) for j, chip in enumerate(chips)]
        for j, chip in enumerate(chips):
            copy(1 + j, (*chip, c), me).wait_recv()
            passed[j].start()
        copy(0, sibling, me).wait_recv()
        for j, chip in enumerate(chips):
            copy(4 + j, (*chip, 1 - c), me).wait_recv()
        for cp in first + passed:
            cp.wait_send()
        mine.wait()

    return pl.pallas_call(
        body, name=name, out_shape=_sds((N_DEV, m, n), blk.dtype),
        in_specs=[pl.BlockSpec(memory_space=pltpu.VMEM)], out_specs=pl.BlockSpec(memory_space=pltpu.VMEM),
        scratch_shapes=[pltpu.SemaphoreType.DMA((7,)), pltpu.SemaphoreType.DMA((7,)), pltpu.SemaphoreType.DMA],
        compiler_params=pltpu.CompilerParams(vmem_limit_bytes=VMEM_LIMIT),
    )(blk)


def _allgather_hbm(name, shards):
    na = len(shards)

    def body(*refs):
        ins, outs = refs[:na], refs[na:2 * na]
        send_sems, recv_sems, local_sems = refs[2 * na:]
        x, y, c = _place()
        me, sibling = (x, y, c), (x, y, 1 - c)
        chips = [(1 - x, y), (x, 1 - y), (1 - x, 1 - y)]

        def copy(a, k, block, to, from_input=False):
            dst = outs[a].at[_lin(*block)]
            return pltpu.make_async_remote_copy(src_ref=ins[a] if from_input else dst, dst_ref=dst,
                                                send_sem=send_sems.at[a, k], recv_sem=recv_sems.at[a, k],
                                                device_id=to, device_id_type=MESH)

        mine = [pltpu.make_async_copy(ins[a], outs[a].at[_lin(*me)], local_sems.at[a]) for a in range(na)]
        for cp in mine:
            cp.start()
        first = []
        for a in range(na):
            first.append(copy(a, 0, me, sibling, True))
            first += [copy(a, 1 + j, me, (*chip, c), True) for j, chip in enumerate(chips)]
        for cp in first:
            cp.start()
        passed = []
        for j, chip in enumerate(chips):
            for a in range(na):
                copy(a, 1 + j, (*chip, c), me).wait_recv()
                fwd = copy(a, 4 + j, (*chip, c), sibling)
                fwd.start()
                passed.append(fwd)
        for a in range(na):
            copy(a, 0, sibling, me).wait_recv()
            for j, chip in enumerate(chips):
                copy(a, 4 + j, (*chip, 1 - c), me).wait_recv()
        for cp in first + passed:
            cp.wait_send()
        for cp in mine:
            cp.wait()

    return pl.pallas_call(
        body, name=name, out_shape=[_sds((N_DEV, *s.shape), s.dtype) for s in shards],
        in_specs=[ANY] * na, out_specs=[ANY] * na,
        scratch_shapes=[pltpu.SemaphoreType.DMA((na, 7)), pltpu.SemaphoreType.DMA((na, 7)),
                        pltpu.SemaphoreType.DMA((na,))],
    )(*shards)


def _exchange_shards(name, grads, L):
    nw = len(grads)
    na = nw * L
    flat = [g for per_layer in grads for g in per_layer]

    def body(*refs):
        ins, outs = refs[:na], refs[na:na + nw]
        send_sems, recv_sems, local_sems = refs[na + nw:]
        x, y, c = _place()
        me = _lin(x, y, c)
        peers = [(x ^ ((k + 1) >> 2 & 1), y ^ ((k + 1) >> 1 & 1), c ^ ((k + 1) & 1)) for k in range(7)]

        def copy(a, k, src_blk, dst_blk):
            return pltpu.make_async_remote_copy(src_ref=ins[a].at[src_blk], dst_ref=outs[a // L].at[a % L, dst_blk],
                                                send_sem=send_sems.at[a, k], recv_sem=recv_sems.at[a, k],
                                                device_id=peers[k], device_id_type=MESH)

        mine = [pltpu.make_async_copy(ins[a].at[me], outs[a // L].at[a % L, me], local_sems.at[a]) for a in range(na)]
        for cp in mine:
            cp.start()
        sent = [copy(a, k, _lin(*peers[k]), me) for a in range(na) for k in range(7)]
        for cp in sent:
            cp.start()
        for a in range(na):
            for k in range(7):
                copy(a, k, me, _lin(*peers[k])).wait_recv()
        for cp in sent:
            cp.wait_send()
        for cp in mine:
            cp.wait()

    return pl.pallas_call(
        body, name=name, out_shape=[_sds((L, *per_layer[0].shape), per_layer[0].dtype) for per_layer in grads],
        in_specs=[ANY] * na, out_specs=[ANY] * nw,
        scratch_shapes=[pltpu.SemaphoreType.DMA((na, 7)), pltpu.SemaphoreType.DMA((na, 7)),
                        pltpu.SemaphoreType.DMA((na,))],
    )(*flat)


MOD_ROWS = 16
MOD_SHARD = 6 * D // N_DEV
HI = lax.Precision.HIGHEST


def _mod_fwd(name, c9, w_mod, b_shard):
    L = w_mod.shape[0]

    def kern(c_ref, w_ref, b_ref, o_ref):
        o_ref[...] = lax.dot_general(_silu(c_ref[...]), w_ref[...], NN, precision=HI,
                                     preferred_element_type=F32) + b_ref[...]

    return pl.pallas_call(
        kern, name=name, grid=(L,),
        in_specs=[_full_spec(c9.shape), pl.BlockSpec((None, D, MOD_SHARD), lambda l: (l, 0, 0)),
                  pl.BlockSpec((None, 1, MOD_SHARD), lambda l: (l, 0, 0))],
        out_specs=pl.BlockSpec((None, MOD_ROWS, MOD_SHARD), lambda l: (l, 0, 0)),
        out_shape=_sds((L, MOD_ROWS, MOD_SHARD), F32), compiler_params=_params(),
    )(c9, w_mod, b_shard)


def _mod_bwd(name, c9, w_mod, dmod_all, dmod_cols):
    L = w_mod.shape[0]

    def rows9(ref, l):
        own = jnp.concatenate([ref[j, 2 * l + 1:2 * l + 2, :] for j in range(N_DEV)], axis=0)
        ctx = ref[0, 2 * l:2 * l + 1, :]
        for j in range(1, N_DEV):
            ctx = ctx + ref[j, 2 * l:2 * l + 1, :]
        return own, ctx

    def kern(c_ref, w_ref, all_ref, cols_ref, gw_ref, gb_ref, gc_ref):
        l = pl.program_id(0)
        for ll in range(L):
            @pl.when(l == ll)
            def _():
                own, ctx = rows9(all_ref, ll)
                gb_ref[...] = _colsum(own) + ctx
                own_s, ctx_s = rows9(cols_ref, ll)
                r16 = jnp.concatenate([own_s, ctx_s, jnp.zeros((MOD_ROWS - N_DEV - 1, MOD_SHARD), F32)], axis=0)
                gw_ref[...] = lax.dot_general(_silu(c_ref[...]), r16, TN, precision=HI, preferred_element_type=F32)
                part = lax.dot_general(r16, w_ref[...], NT, precision=HI,
                                       preferred_element_type=F32)[N_DEV:N_DEV + 1, :]
                if ll == 0:
                    gc_ref[...] = part
                else:
                    gc_ref[...] += part

    return pl.pallas_call(
        kern, name=name, grid=(L,),
        in_specs=[_full_spec(c9.shape), pl.BlockSpec((None, D, MOD_SHARD), lambda l: (l, 0, 0)),
                  _full_spec(dmod_all.shape), _full_spec(dmod_cols.shape)],
        out_specs=[pl.BlockSpec((None, D, MOD_SHARD), lambda l: (l, 0, 0)),
                   pl.BlockSpec((None, 1, 6 * D), lambda l: (l, 0, 0)), _full_spec((1, D))],
        out_shape=[_sds((L, D, MOD_SHARD), F32), _sds((L, 1, 6 * D), F32), _sds((1, D), F32)],
        compiler_params=_params(),
    )(c9, w_mod, dmod_all, dmod_cols)


_BC1 = 1.0 - ADAM_B1 ** ADAM_STEP
_BC2 = 1.0 - ADAM_B2 ** ADAM_STEP


def _adamw_vals(w, g, m, v):
    m = ADAM_B1 * m + (1.0 - ADAM_B1) * g
    v = ADAM_B2 * v + (1.0 - ADAM_B2) * (g * g)
    delta = -ADAM_LR * ((m / _BC1) / (jnp.sqrt(v / _BC2) + ADAM_EPS) + ADAM_WD * w)
    return delta, m, v


def _adamw(name, w, g, m, v, tile):
    R, C = w.shape
    blk = ((tile, C), lambda i: (i, 0))

    def body(i, ins, ps, outs, acc):
        d, mm, vv = _adamw_vals(ins[0][...], ins[1][...], ins[2][...], ins[3][...])
        outs[0][...] = d
        outs[1][...] = mm
        outs[2][...] = vv

    return _ew(name, body, R // tile, [(a, *blk) for a in (w, g, m, v)], [], [(_sds((R, C), F32), *blk)] * 3)


def _sum_slots(ref):
    g = ref[0].astype(F32)
    for j in range(1, N_DEV):
        g = g + ref[j].astype(F32)
    return g


def _adamw_slots(name, slots, w, m, v, tile):
    L, R, C = w.shape
    n = R // tile
    spec = pl.BlockSpec((None, tile, C), lambda l, i: (l, i, 0))
    pieces = [s if isinstance(s, (list, tuple)) else [s] for s in slots]
    layer_of = [ll for ll, ps in enumerate(pieces) for _ in ps]
    flat = [p for ps in pieces for p in ps]

    def slot_spec(ll, cols):
        return pl.BlockSpec((N_DEV, tile, cols),
                            lambda l, i: (0, jnp.where(l == ll, i, jnp.where(l < ll, 0, n - 1)), 0))

    def kern(*refs):
        s_refs = refs[:len(flat)]
        w_ref, m_ref, v_ref, g_ref, d_ref, mo_ref, vo_ref = refs[len(flat):]
        l = pl.program_id(0)
        for ll in range(L):
            @pl.when(l == ll)
            def _():
                parts = [_sum_slots(r) for r, lr in zip(s_refs, layer_of) if lr == ll]
                g = parts[0] if len(parts) == 1 else jnp.concatenate(parts, axis=1)
                g_ref[...] = g
                d_ref[...], mo_ref[...], vo_ref[...] = _adamw_vals(w_ref[...], g, m_ref[...], v_ref[...])

    return pl.pallas_call(
        kern, name=name, grid=(L, n),
        in_specs=[slot_spec(ll, p.shape[-1]) for ll, p in zip(layer_of, flat)] + [spec, spec, spec],
        out_specs=[spec] * 4, out_shape=[_sds((L, R, C), F32)] * 4,
        compiler_params=_params(("arbitrary", "arbitrary")),
    )(*flat, w, m, v)


def _sum_blocks(name, blocks):
    _, R, C = blocks.shape

    def kern(b_ref, o_ref):
        o_ref[...] = _sum_slots(b_ref)

    return pl.pallas_call(kern, name=name, in_specs=[_full_spec(blocks.shape)], out_specs=_full_spec((R, C)),
                          grid=(1,), out_shape=_sds((R, C), F32), compiler_params=_params())(blocks)


BIG = ("win_t", "wo_rnn", "wo_attn", "wout", "wffn_in_t", "wffn_out")
BIG_SRC = ("w_in", "w_o_rnn", "w_o_attn", "w_out", "w_ffn_in", "w_ffn_out")
BIG_T = (True, False, False, False, True, False)
BIG_TILE = (176, 128, 128, 128, 176, 176)


def _chan_full(g8):
    return jnp.transpose(g8, (1, 0, 2)).reshape(g8.shape[1], D)


def kernel(x, c, ctx, c_ctx, w_mod, b_mod, g_mix_pre, g_mix_post, g_ffn_pre, g_ffn_post, w_in, conv_w, conv_b, lru_wa, lru_ba, lru_wx, lru_bx, lru_lam, attn_sink, w_o_rnn, w_o_attn, w_out, w_ffn_in, w_ffn_out, loss_target, m_c_ctx, m_w_mod, m_b_mod, m_g_mix_pre, m_g_mix_post, m_g_ffn_pre, m_g_ffn_post, m_w_in, m_conv_w, m_conv_b, m_lru_wa, m_lru_ba, m_lru_wx, m_lru_bx, m_lru_lam, m_attn_sink, m_w_o_rnn, m_w_o_attn, m_w_out, m_w_ffn_in, m_w_ffn_out, v_c_ctx, v_w_mod, v_b_mod, v_g_mix_pre, v_g_mix_post, v_g_ffn_pre, v_g_ffn_post, v_w_in, v_conv_w, v_conv_b, v_lru_wa, v_lru_ba, v_lru_wx, v_lru_bx, v_lru_lam, v_attn_sink, v_w_o_rnn, v_w_o_attn, v_w_out, v_w_ffn_in, v_w_ffn_out):
    P = dict(c_ctx=c_ctx, w_mod=w_mod, b_mod=b_mod, g_mix_pre=g_mix_pre, g_mix_post=g_mix_post, g_ffn_pre=g_ffn_pre,
             g_ffn_post=g_ffn_post, w_in=w_in, conv_w=conv_w, conv_b=conv_b, lru_wa=lru_wa, lru_ba=lru_ba,
             lru_wx=lru_wx, lru_bx=lru_bx, lru_lam=lru_lam, attn_sink=attn_sink, w_o_rnn=w_o_rnn, w_o_attn=w_o_attn,
             w_out=w_out, w_ffn_in=w_ffn_in, w_ffn_out=w_ffn_out)
    Mo = dict(c_ctx=m_c_ctx, w_mod=m_w_mod, b_mod=m_b_mod, g_mix_pre=m_g_mix_pre, g_mix_post=m_g_mix_post,
              g_ffn_pre=m_g_ffn_pre, g_ffn_post=m_g_ffn_post, w_in=m_w_in, conv_w=m_conv_w, conv_b=m_conv_b,
              lru_wa=m_lru_wa, lru_ba=m_lru_ba, lru_wx=m_lru_wx, lru_bx=m_lru_bx, lru_lam=m_lru_lam,
              attn_sink=m_attn_sink, w_o_rnn=m_w_o_rnn, w_o_attn=m_w_o_attn, w_out=m_w_out, w_ffn_in=m_w_ffn_in,
              w_ffn_out=m_w_ffn_out)
    Vo = dict(c_ctx=v_c_ctx, w_mod=v_w_mod, b_mod=v_b_mod, g_mix_pre=v_g_mix_pre, g_mix_post=v_g_mix_post,
              g_ffn_pre=v_g_ffn_pre, g_ffn_post=v_g_ffn_post, w_in=v_w_in, conv_w=v_conv_w, conv_b=v_conv_b,
              lru_wa=v_lru_wa, lru_ba=v_lru_ba, lru_wx=v_lru_wx, lru_bx=v_lru_bx, lru_lam=v_lru_lam,
              attn_sink=v_attn_sink, w_o_rnn=v_w_o_rnn, w_o_attn=v_w_o_attn, w_out=v_w_out, w_ffn_in=v_w_ffn_in,
              w_ffn_out=v_w_ffn_out)
    L = w_in.shape[0]
    S = x.shape[1]
    me = _lin(*_place())

    small = jnp.concatenate([c.reshape(8, 128), conv_w.reshape(L * CONV_W, 128), lru_ba.reshape(2 * L, 128),
                             lru_bx.reshape(2 * L, 128), lru_lam.reshape(2 * L, 128), jnp.zeros((4, 128), F32)], axis=0)
    small_all = _allgather_small("ag_small", small)
    c_all = small_all[:, 0:8].reshape(N_DEV, D)
    conv_w_f = _chan_full(small_all[:, 8:16]).reshape(L, CONV_W, D)
    lru_ba_f = _chan_full(small_all[:, 16:20]).reshape(L, 2, D)
    lru_bx_f = _chan_full(small_all[:, 20:24]).reshape(L, 2, D)
    lru_lam_f = _chan_full(small_all[:, 24:28]).reshape(L, 2, D)

    c9 = jnp.concatenate([c_all, c_ctx[None], jnp.zeros((MOD_ROWS - N_DEV - 1, D), F32)], axis=0)
    b_shard = lax.dynamic_slice_in_dim(b_mod, me * MOD_SHARD, MOD_SHARD, axis=1)[:, None, :]
    mod_part = _mod_fwd("mod_fwd", c9, w_mod, b_shard)
    mod_all = _allgather_small("ag_mod", mod_part.reshape(L * MOD_ROWS, MOD_SHARD))
    mod_all = jnp.transpose(mod_all.reshape(N_DEV, L, MOD_ROWS, MOD_SHARD), (1, 2, 0, 3)).reshape(L, MOD_ROWS, 6 * D)
    own_row = lax.dynamic_index_in_dim(mod_all, me, axis=1, keepdims=False)
    modrows = jnp.stack([mod_all[:, N_DEV], own_row], axis=1)

    shards = [{k: (P[src][l].T if tr else P[src][l]).astype(BF16) for k, src, tr in zip(BIG, BIG_SRC, BIG_T)}
              for l in range(L)]
    win0, = _allgather_hbm("ag_w_in0", [shards[0]["win_t"]])
    Ws = []
    for l in range(L):
        W = {"win_t": win0.reshape(-1, D)} if l == 0 else {}
        W.update(
            cw=conv_w_f[l], cb=conv_b[l][None],
            w4=jnp.concatenate([lru_wa[l, 0], lru_wa[l, 1], lru_wx[l, 0], lru_wx[l, 1]], axis=-1).astype(BF16),
            b4=jnp.concatenate([lru_ba_f[l, 0].reshape(N_RNN_BLOCKS, 1, RB), lru_ba_f[l, 1].reshape(N_RNN_BLOCKS, 1, RB),
                                lru_bx_f[l, 0].reshape(N_RNN_BLOCKS, 1, RB), lru_bx_f[l, 1].reshape(N_RNN_BLOCKS, 1, RB)],
                               axis=-1),
            lam=lru_lam_f[l], sink4=jnp.broadcast_to(attn_sink[l].reshape(N_KV, Q_PER_KV, 1), (N_KV, Q_PER_KV, HEAD)),
            g_mix_pre=g_mix_pre[l][None], g_mix_post=g_mix_post[l][None], g_ffn_pre=g_ffn_pre[l][None],
            g_ffn_post=g_ffn_post[l][None], mod=modrows[l])
        Ws.append(W)

    xa = jnp.concatenate([ctx[0], x[0]], axis=0)
    plan = _Plan(shards, Ws)
    sq, dxa, Gs = _local_step(xa, loss_target[0], Ws, S, plan)
    loss = lax.psum((0.5 / D) * jnp.sum(sq), ("x", "y", "c"))
    grad_x = dxa[CTX:][None]

    dmod = jnp.concatenate([Gs[l]["mod"] for l in range(L)] + [jnp.zeros((8 - 2 * L, 6 * D), F32)], axis=0)
    dmod_all = _allgather_small("ag_dmod", dmod)
    dmod_cols = lax.dynamic_slice_in_dim(dmod_all, me * MOD_SHARD, MOD_SHARD, axis=2)
    g_w_mod, g_b_mod, dsc_part = _mod_bwd("mod_bwd", c9, w_mod, dmod_all, dmod_cols)
    g_b_mod = g_b_mod[:, 0]

    def rows(name, shape):
        return jnp.concatenate([Gs[l][name].reshape(shape) for l in range(L)], axis=0)

    b4g = [Gs[l]["b4"].reshape(N_RNN_BLOCKS, 4, RB) for l in range(L)]
    sink_row = jnp.concatenate([Gs[l]["sink4"][:, :, 0].reshape(1, N_Q) for l in range(L)]
                               + [jnp.zeros((1, D - L * N_Q), F32)], axis=1)
    small_g = jnp.concatenate(
        [rows("g_mix_pre", (1, D)), rows("g_mix_post", (1, D)), rows("g_ffn_pre", (1, D)), rows("g_ffn_post", (1, D)),
         rows("cb", (1, D)), rows("cw", (CONV_W, D))]
        + [b4g[l][:, d].reshape(1, D) for l in range(L) for d in range(2)]
        + [b4g[l][:, 2 + d].reshape(1, D) for l in range(L) for d in range(2)]
        + [rows("lam", (2, D)), sink_row, dsc_part], axis=0)
    n_small = small_g.shape[0]
    small_tot = _sum_blocks("sum_small", _allgather_small("ag_small_grads", small_g))
    o = 0
    G = {}
    for name in ("g_mix_pre", "g_mix_post", "g_ffn_pre", "g_ffn_post", "conv_b"):
        G[name] = small_tot[o:o + L]
        o += L
    G["conv_w"] = small_tot[o:o + L * CONV_W].reshape(L, CONV_W, D)
    o += L * CONV_W
    for name in ("lru_ba", "lru_bx", "lru_lam"):
        G[name] = small_tot[o:o + 2 * L].reshape(L, 2, D)
        o += 2 * L
    G["attn_sink"] = small_tot[o, :L * N_Q].reshape(L, N_Q)
    sg = jax.nn.sigmoid(c_ctx)
    G["c_ctx"] = small_tot[o + 1] * (sg * (1.0 + c_ctx * (1.0 - sg)))
    G["b_mod"] = g_b_mod
    G["w_mod"] = g_w_mod

    last_slots, = _exchange_shards("exchange_w_in0", [[Gs[0]["win_t_b"].reshape(N_DEV, -1, D // 2)]], 1)
    plan.slots[0]["win_t"] = [plan.slots[0]["win_t_a"], last_slots[0]]

    out_g, out_d, out_m, out_v = {}, {}, {}, {}

    def put(name, res, shape=None):
        g, d, m, v = res
        for dst, val in ((out_g, g), (out_d, d), (out_m, m), (out_v, v)):
            dst[name] = val if shape is None else val.reshape(shape)

    for k, src, tr, tile in zip(BIG, BIG_SRC, BIG_T, BIG_TILE):
        lay = (lambda a: jnp.swapaxes(a, 1, 2)) if tr else (lambda a: a)
        res = _adamw_slots("adamw_" + src, [plan.slots[l][k] for l in range(L)], lay(P[src]), lay(Mo[src]),
                           lay(Vo[src]), tile)
        put(src, [lay(r) for r in res])
    res = _adamw("adamw_w_mod", w_mod.reshape(L * D, MOD_SHARD), g_w_mod.reshape(L * D, MOD_SHARD),
                 m_w_mod.reshape(L * D, MOD_SHARD), v_w_mod.reshape(L * D, MOD_SHARD), 256)
    put("w_mod", (g_w_mod,) + tuple(res), w_mod.shape)
    def fuse4(wa, wx):
        return jnp.concatenate([wa[:, 0], wa[:, 1], wx[:, 0], wx[:, 1]], axis=-1).reshape(L, N_RNN_BLOCKS * RB, 4 * RB)

    res = _adamw_slots("adamw_gates", plan.gate_slots,
                       fuse4(lru_wa, lru_wx), fuse4(m_lru_wa, m_lru_wx), fuse4(v_lru_wa, v_lru_wx), 256)
    res = [r.reshape(L, N_RNN_BLOCKS, RB, 4, RB) for r in res]
    put("lru_wa", [jnp.stack([r[:, :, :, 0], r[:, :, :, 1]], axis=1) for r in res])
    put("lru_wx", [jnp.stack([r[:, :, :, 2], r[:, :, :, 3]], axis=1) for r in res])
    rep = ("g_mix_pre", "g_mix_post", "g_ffn_pre", "g_ffn_post", "conv_b", "b_mod")

    def pack_rep(T_):
        sink = jnp.concatenate([T_["attn_sink"].reshape(1, L * N_Q), jnp.zeros((1, D - L * N_Q), F32)], axis=1)
        return jnp.concatenate([T_[n].reshape(-1, D) for n in rep] + [sink, T_["c_ctx"][None]], axis=0)

    pk = [pack_rep(T_) for T_ in (P, G, Mo, Vo)]
    n_rep = pk[0].shape[0]
    res = _adamw("adamw_replicated", *[jnp.pad(a, ((0, 24 - n_rep), (0, 0))) for a in pk], 24)
    res = (pk[1],) + tuple(r[:n_rep] for r in res)
    o = 0
    for n in rep:
        k = P[n].size // D
        put(n, [r[o:o + k] for r in res], P[n].shape)
        o += k
    put("attn_sink", [r[o, :L * N_Q] for r in res], attn_sink.shape)
    put("c_ctx", [r[o + 1] for r in res], c_ctx.shape)
    chan = ("conv_w", "lru_ba", "lru_bx", "lru_lam")
    g_own = {n: lax.dynamic_slice_in_dim(G[n], me * RB, RB, axis=2) for n in chan}

    def pack_chan(T_):
        return jnp.concatenate([T_[n].reshape(-1, RB) for n in chan], axis=0)

    pk = [pack_chan(T_) for T_ in (P, g_own, Mo, Vo)]
    n_ch = pk[0].shape[0]
    res = _adamw("adamw_channels", *[jnp.pad(a, ((0, 24 - n_ch), (0, 0))) for a in pk], 24)
    res = (pk[1],) + tuple(r[:n_ch] for r in res)
    o = 0
    for n in chan:
        k = P[n].size // RB
        put(n, [r[o:o + k] for r in res], P[n].shape)
        o += k

    order = ("c_ctx", "w_mod", "b_mod", "g_mix_pre", "g_mix_post", "g_ffn_pre", "g_ffn_post", "w_in", "conv_w", "conv_b",
             "lru_wa", "lru_ba", "lru_wx", "lru_bx", "lru_lam", "attn_sink", "w_o_rnn", "w_o_attn", "w_out", "w_ffn_in",
             "w_ffn_out")
    return (loss, grad_x, *[out_g[n] for n in order], *[out_d[n] for n in order], *[out_m[n] for n in order],
            *[out_v[n] for n in order])
```

```python
import functools
import math

import numpy as np
import jax
import jax.numpy as jnp
from jax import lax
from jax.experimental import pallas as pl
from jax.experimental.pallas import tpu as pltpu

F32 = jnp.float32
BF16 = jnp.bfloat16

D = 1024
CTX = 256
TR = 256
HEAD = 128
N_Q = 8
N_KV = 2
Q_PER_KV = N_Q // N_KV
GRID_W = 64
N_FREQ = HEAD // 4
ROPE_BASE = 10000.0
N_RNN_BLOCKS = 8
CONV_W = 4
CONV_LEFT = 2
LRU_C = 8.0
D_FF = 2816
IN_W = 5632
P_W = IN_W
COL_XR, COL_GR, COL_Q, COL_K, COL_V, COL_GL = 0, 1024, 2048, 3072, 3328, 3584
GLB = 512
EPS = 1e-6
NEG_INF = -1e30
ATT_SCALE = HEAD ** -0.5
N_DEV = 8
VMEM_LIMIT = 56 * 1024 * 1024

ADAM_LR, ADAM_B1, ADAM_B2, ADAM_EPS, ADAM_WD, ADAM_STEP = 0.001, 0.9, 0.999, 1e-08, 0.01, 10

NN = (((1,), (0,)), ((), ()))
NT = (((1,), (1,)), ((), ()))
TN = (((0,), (0,)), ((), ()))


def _dot(a, b, dims=NN):
    return lax.dot_general(a, b, dims, preferred_element_type=F32)


def _params(sem=("arbitrary",)):
    return pltpu.CompilerParams(dimension_semantics=sem, vmem_limit_bytes=VMEM_LIMIT)


def _full_spec(shape):
    nd = len(shape)
    return pl.BlockSpec(shape, lambda *_: (0,) * nd)


ANY = pl.BlockSpec(memory_space=pl.ANY)


def _ew(name, body, n, row_ins, pars, row_outs, accs=(), alias=None):
    n_ri, n_p, n_ro, n_acc = len(row_ins), len(pars), len(row_outs), len(accs)

    def kern(*refs):
        i = pl.program_id(0)
        ins = refs[:n_ri]
        ps = refs[n_ri:n_ri + n_p]
        outs = refs[n_ri + n_p:n_ri + n_p + n_ro]
        acc = refs[n_ri + n_p + n_ro:]
        if n_acc:
            @pl.when(i == 0)
            def _():
                for a in acc:
                    a[...] = jnp.zeros(a.shape, a.dtype)
        body(i, ins, ps, outs, acc)

    in_specs = [ANY if blk is None else pl.BlockSpec(blk, imap) for (_, blk, imap) in row_ins]
    in_specs += [_full_spec(p.shape) for p in pars]
    out_specs = [pl.BlockSpec(blk, imap) for (_, blk, imap) in row_outs] + [_full_spec(a.shape) for a in accs]
    out_shape = [s for (s, _, _) in row_outs] + list(accs)
    return pl.pallas_call(
        kern, name=name, grid=(n,), in_specs=in_specs, out_specs=out_specs, out_shape=out_shape,
        input_output_aliases=alias or {}, compiler_params=_params(),
    )(*[a for (a, _, _) in row_ins], *pars)


def _rowblk(width, colblk=0, roff=0, tile=TR):
    return (tile, width), (lambda i: (i + roff, colblk))


def _sds(shape, dtype):
    return jax.ShapeDtypeStruct(shape, dtype)


class _Carry:
    SAME_CORE = (1, 3, 5)

    def __init__(self, jobs):
        self.jobs = list(jobs)
        self.arrays = [a for _, a in self.jobs]
        self.out_shapes = [_sds(a.shape if kind == "scatter" else (N_DEV, *a.shape), a.dtype) for kind, a in self.jobs]
        n = len(self.jobs)
        self.scratch = [pltpu.SemaphoreType.DMA((n, 7)), pltpu.SemaphoreType.DMA((n, 7)), pltpu.SemaphoreType.DMA((n,))]

    def _setup(self, sems):
        send_sems, recv_sems, local_sems = sems
        x, y, c = _place()
        me = _lin(x, y, c)
        peers = [(x ^ ((k + 1) >> 2 & 1), y ^ ((k + 1) >> 1 & 1), c ^ ((k + 1) & 1)) for k in range(7)]

        def copy(a, k, sem_k, src, dst):
            return pltpu.make_async_remote_copy(src_ref=src, dst_ref=dst, send_sem=send_sems.at[a, sem_k],
                                                recv_sem=recv_sems.at[a, sem_k], device_id=peers[k], device_id_type=MESH)

        return me, [_lin(*p) for p in peers], copy, local_sems

    def _local(self, a, kind, ins, outs, me, local_sems):
        return pltpu.make_async_copy(ins[a].at[me] if kind == "scatter" else ins[a], outs[a].at[me], local_sems.at[a])

    def start(self, ins, outs, sems):
        me, theirs, copy, local_sems = self._setup(sems)
        for a, (kind, _) in enumerate(self.jobs):
            self._local(a, kind, ins, outs, me, local_sems).start()
            if kind == "scatter":
                for k in range(7):
                    copy(a, k, k, ins[a].at[theirs[k]], outs[a].at[me]).start()
            else:
                for k in (0,) + self.SAME_CORE:
                    copy(a, k, k, ins[a], outs[a].at[me]).start()

    def wait(self, ins, outs, sems):
        me, theirs, copy, local_sems = self._setup(sems)
        for a, (kind, _) in enumerate(self.jobs):
            if kind == "scatter":
                for k in range(7):
                    copy(a, k, k, ins[a].at[me], outs[a].at[theirs[k]]).wait_recv()
                for k in range(7):
                    copy(a, k, k, ins[a].at[theirs[k]], outs[a].at[me]).wait_send()
            else:
                for k in self.SAME_CORE:
                    blk = outs[a].at[theirs[k]]
                    copy(a, k, k, ins[a], blk).wait_recv()
                    copy(a, 0, k + 1, blk, blk).start()
                copy(a, 0, 0, ins[a], outs[a].at[theirs[0]]).wait_recv()
                for k in self.SAME_CORE:
                    copy(a, 0, k + 1, ins[a], outs[a].at[theirs[k + 1]]).wait_recv()
                for k in (0,) + self.SAME_CORE:
                    copy(a, k, k, ins[a], outs[a].at[me]).wait_send()
                for k in self.SAME_CORE:
                    blk = outs[a].at[theirs[k]]
                    copy(a, 0, k + 1, blk, blk).wait_send()
            self._local(a, kind, ins, outs, me, local_sems).wait()


def _carried(kern, carry, n_in, n_out, first, last):
    if carry is None:
        return kern
    nc = len(carry.jobs)

    def wrapped(*refs):
        ins, cin = refs[:n_in], refs[n_in:n_in + nc]
        outs, cout = refs[n_in + nc:n_in + nc + n_out], refs[n_in + nc + n_out:n_in + 2 * nc + n_out]
        scr, sems = refs[n_in + 2 * nc + n_out:-3], refs[-3:]

        @pl.when(first())
        def _():
            carry.start(cin, cout, sems)

        kern(*ins, *outs, *scr)

        @pl.when(last())
        def _():
            carry.wait(cin, cout, sems)

    return wrapped


def _carry_args(carry):
    if carry is None:
        return [], [], [], [], []
    n = len(carry.jobs)
    return [ANY] * n, carry.arrays, [ANY] * n, carry.out_shapes, carry.scratch


def _grid_ends(dims):
    first = lambda: functools.reduce(jnp.logical_and, [pl.program_id(d) == 0 for d in range(len(dims))])
    last = lambda: functools.reduce(jnp.logical_and, [pl.program_id(d) == n - 1 for d, n in enumerate(dims)])
    return first, last


def _mm_call(name, a, b, mode, out_dtype, tm, tn, rows_outer=True, single_b=False, carry=None):
    if mode == "TN":
        (K, M), N = a.shape, b.shape[1]
    else:
        (M, K), N = a.shape, (b.shape[1] if mode == "NN" else b.shape[0])
    assert M % tm == 0 and N % tn == 0, (name, M, N, K, tm, tn)
    ij = (lambda g0, g1: (g0, g1)) if rows_outer else (lambda g0, g1: (g1, g0))
    grid = (M // tm, N // tn) if rows_outer else (N // tn, M // tm)
    if mode == "TN":
        a_spec = pl.BlockSpec((K, tm), lambda g0, g1: (0, ij(g0, g1)[0]))
    else:
        a_spec = pl.BlockSpec((tm, K), lambda g0, g1: (ij(g0, g1)[0], 0))
    b_blk, b_map = ((tn, K), lambda g0, g1: (ij(g0, g1)[1], 0)) if mode == "NT" else \
                   ((K, tn), lambda g0, g1: (0, ij(g0, g1)[1]))
    b_spec = pl.BlockSpec(b_blk, b_map, pipeline_mode=pl.Buffered(1)) if single_b else pl.BlockSpec(b_blk, b_map)
    dims = {"NN": NN, "NT": NT, "TN": TN}[mode]

    def kern(a_ref, b_ref, o_ref):
        o_ref[...] = _dot(a_ref[...], b_ref[...], dims).astype(o_ref.dtype)

    ci, ca, co, cs, cscr = _carry_args(carry)
    res = pl.pallas_call(
        _carried(kern, carry, 2, 1, *_grid_ends(grid)), name=name, grid=grid, in_specs=[a_spec, b_spec] + ci,
        out_specs=[pl.BlockSpec((tm, tn), lambda g0, g1: ij(g0, g1))] + co,
        out_shape=[_sds((M, N), out_dtype)] + cs, scratch_shapes=cscr,
        compiler_params=_params(("arbitrary", "arbitrary")),
    )(a, b, *ca)
    return res[0] if carry is None else (res[0], res[1:])


def _mm_act(name, a, w, mode, out_dtype=BF16, carry=None):
    rows, K = a.shape
    N = w.shape[1] if mode == "NN" else w.shape[0]
    if K > D_FF:
        return _mm_call(name, a, w, mode, out_dtype, rows // 8, N, single_b=True, carry=carry)
    tn = N if N <= 1024 else 1408
    return _mm_call(name, a, w, mode, out_dtype, rows // 4, tn, carry=carry)


def _mm_wgrad(name, x, dy, out_dtype=BF16, carry=None):
    M = x.shape[1]
    tm = 1408 if M == D_FF else 512
    return _mm_call(name, x, dy, "TN", out_dtype, tm, dy.shape[1], single_b=True, carry=carry)


def _sigmoid(x):
    return 0.5 * jnp.tanh(0.5 * x) + 0.5


def _silu(x):
    return x * _sigmoid(x)


def _silu_grad(x):
    s = _sigmoid(x)
    return s * (1.0 + x * (1.0 - s))


_GELU_K = math.sqrt(2.0 / math.pi)


def _gelu(x):
    return 0.5 * x * (1.0 + jnp.tanh(_GELU_K * (x + 0.044715 * x * x * x)))


def _gelu_grad(x):
    t = jnp.tanh(_GELU_K * (x + 0.044715 * x * x * x))
    return 0.5 * (1.0 + t) + 0.5 * x * (1.0 - t * t) * _GELU_K * (1.0 + 3.0 * 0.044715 * x * x)


def _log_sigmoid(x):
    return jnp.minimum(x, 0.0) - jnp.log(1.0 + jnp.exp(-jnp.abs(x)))


def _rms(x):
    x = x.astype(F32)
    r = lax.rsqrt(jnp.mean(x * x, axis=-1, keepdims=True) + EPS)
    return x * r, r


def _rms_bwd(dy, y, r):
    return r * (dy - y * jnp.mean(dy * y, axis=-1, keepdims=True))


def _modrow(mod_ref, i, chunk):
    lo = mod_ref[0:1, chunk * D:(chunk + 1) * D]
    hi = mod_ref[1:2, chunk * D:(chunk + 1) * D]
    return jnp.where(i == 0, lo, hi)


def _acc_seg(acc_ref, i, val):
    zero = jnp.zeros_like(val)
    acc_ref[0:1, :] += jnp.where(i == 0, val, zero)
    acc_ref[1:2, :] += jnp.where(i == 0, zero, val)


def _colsum(x):
    return jnp.sum(x, axis=0, keepdims=True)


SH1, SC1, GA1, SH2, SC2, GA2 = range(6)


def _normmod_fwd(name, xa, g, mod, c_sh, c_sc):
    T = xa.shape[0]

    def body(i, ins, ps, outs, acc):
        y, _ = _rms(ins[0][...])
        h = (y * ps[0][...]) * (1.0 + _modrow(ps[1], i, c_sc)) + _modrow(ps[1], i, c_sh)
        outs[0][...] = h.astype(BF16)

    return _ew(name, body, T // TR, [(xa, *_rowblk(D))], [g, mod], [(_sds((T, D), BF16), *_rowblk(D))])[0]


def _resid_norm_fwd(name, xin, mat, gpost, mod, c_ga, gnext, modn, c_sh, c_sc):
    T = xin.shape[0]

    def body(i, ins, ps, outs, acc):
        ym, _ = _rms(ins[1][...])
        xo = ins[0][...] + _modrow(ps[1], i, c_ga) * (ym * ps[0][...])
        outs[0][...] = xo
        y, _ = _rms(xo)
        h = (y * ps[2][...]) * (1.0 + _modrow(ps[3], i, c_sc)) + _modrow(ps[3], i, c_sh)
        outs[1][...] = h.astype(BF16)

    return _ew(name, body, T // TR, [(xin, *_rowblk(D)), (mat, *_rowblk(D))], [gpost, mod, gnext, modn],
               [(_sds((T, D), F32), *_rowblk(D)), (_sds((T, D), BF16), *_rowblk(D))])


def _resid_loss_fwd(name, xin, mat, gpost, mod, c_ga, target):
    T = xin.shape[0]

    def body(i, ins, ps, outs, acc):
        ym, _ = _rms(ins[1][...])
        xo = ins[0][...] + _modrow(ps[1], i, c_ga) * (ym * ps[0][...])
        err = xo - ins[2][...]
        lat = i > 0
        outs[0][...] = jnp.where(lat, err * (1.0 / D), 0.0)
        acc[0][...] += jnp.where(lat, _colsum(err * err), 0.0)

    tgt_blk = ((TR, D), lambda i: (jnp.maximum(i - 1, 0), 0))
    dx, sq = _ew(name, body, T // TR, [(xin, *_rowblk(D)), (mat, *_rowblk(D)), (target, *tgt_blk)], [gpost, mod],
                 [(_sds((T, D), F32), *_rowblk(D))], [_sds((1, D), F32)])
    return dx, sq


def _resid_bwd_vals(i, dout, mat, gpost, mod_ref, c_ga, acc_ga, acc_g):
    ym, rm = _rms(mat)
    ga = _modrow(mod_ref, i, c_ga)
    _acc_seg(acc_ga, i, _colsum(dout * (ym * gpost)))
    dn = dout * ga
    acc_g[...] += _colsum(dn * ym)
    return _rms_bwd(dn * gpost, ym, rm)


def _normmod_bwd_vals(i, dh, xin, g, mod_ref, c_sh, c_sc, acc_sh, acc_sc, acc_g):
    dh = dh.astype(F32)
    y, r = _rms(xin)
    _acc_seg(acc_sc, i, _colsum(dh * (y * g)))
    _acc_seg(acc_sh, i, _colsum(dh))
    dyg = dh * (1.0 + _modrow(mod_ref, i, c_sc))
    acc_g[...] += _colsum(dyg * y)
    return _rms_bwd(dyg * g, y, r)


def _resid_bwd(name, dout, mat, gpost, mod, c_ga):
    T = dout.shape[0]

    def body(i, ins, ps, outs, acc):
        dm = _resid_bwd_vals(i, ins[0][...], ins[1][...], ps[0][...], ps[1], c_ga, acc[0], acc[1])
        outs[0][...] = dm.astype(BF16)

    return _ew(name, body, T // TR, [(dout, *_rowblk(D)), (mat, *_rowblk(D))], [gpost, mod],
               [(_sds((T, D), BF16), *_rowblk(D))], [_sds((2, D), F32), _sds((1, D), F32)])


def _normmod_resid_bwd(name, dh, xin, gpre, mod, c_sh, c_sc, dres, mat, gpost, c_ga):
    T = dh.shape[0]

    def body(i, ins, ps, outs, acc):
        dx = ins[2][...] + _normmod_bwd_vals(i, ins[0][...], ins[1][...], ps[0][...], ps[1], c_sh, c_sc,
                                             acc[0], acc[1], acc[2])
        outs[0][...] = dx
        dm = _resid_bwd_vals(i, dx, ins[3][...], ps[2][...], ps[1], c_ga, acc[3], acc[4])
        outs[1][...] = dm.astype(BF16)

    return _ew(name, body, T // TR, [(dh, *_rowblk(D)), (xin, *_rowblk(D)), (dres, *_rowblk(D)), (mat, *_rowblk(D))],
               [gpre, mod, gpost],
               [(_sds((T, D), F32), *_rowblk(D)), (_sds((T, D), BF16), *_rowblk(D))],
               [_sds((2, D), F32), _sds((2, D), F32), _sds((1, D), F32), _sds((2, D), F32), _sds((1, D), F32)])


def _normmod_bwd(name, dh, xin, gpre, mod, c_sh, c_sc, dres):
    T = dh.shape[0]

    def body(i, ins, ps, outs, acc):
        outs[0][...] = ins[2][...] + _normmod_bwd_vals(i, ins[0][...], ins[1][...], ps[0][...], ps[1], c_sh, c_sc,
                                                       acc[0], acc[1], acc[2])

    return _ew(name, body, T // TR, [(dh, *_rowblk(D)), (xin, *_rowblk(D)), (dres, *_rowblk(D))], [gpre, mod],
               [(_sds((T, D), F32), *_rowblk(D))], [_sds((2, D), F32), _sds((2, D), F32), _sds((1, D), F32)])


def _gate_fwd(name, p, ya, yb):
    T = ya.shape[0]

    def body(i, ins, ps, outs, acc):
        gl = [r[...].astype(F32) for r in ins[:4]]
        ga = _sigmoid(jnp.concatenate(gl[:2], axis=1))
        gb = _sigmoid(jnp.concatenate(gl[2:], axis=1))
        outs[0][...] = (ga * ins[4][...].astype(F32) + gb * ins[5][...].astype(F32)).astype(BF16)

    return _ew(name, body, T // TR,
               [(p, *_rowblk(GLB, COL_GL // GLB + q)) for q in range(4)] + [(ya, *_rowblk(D)), (yb, *_rowblk(D))],
               [], [(_sds((T, D), BF16), *_rowblk(D))])[0]


def _gate_bwd(name, p, ya, yb, dz):
    T = ya.shape[0]

    def kern(gl_ref, ya_ref, yb_ref, dz_ref, dya_ref, dyb_ref, dp_ref):
        j = pl.program_id(1)
        g = _sigmoid(gl_ref[...].astype(F32))
        dzv = dz_ref[...].astype(F32)
        dbranch = (dzv * g).astype(BF16)
        dg = dzv * g * (1.0 - g)

        @pl.when(j < 2)
        def _():
            dya_ref[...] = dbranch
            dp_ref[...] = (dg * ya_ref[...].astype(F32)).astype(BF16)

        @pl.when(j >= 2)
        def _():
            dyb_ref[...] = dbranch
            dp_ref[...] = (dg * yb_ref[...].astype(F32)).astype(BF16)

    rt = T // 4
    first = pl.BlockSpec((rt, GLB), lambda i, j: (i, jnp.minimum(j, 1)))
    second = pl.BlockSpec((rt, GLB), lambda i, j: (i, jnp.maximum(j - 2, 0)))
    return pl.pallas_call(
        kern, name=name, grid=(4, 4),
        in_specs=[pl.BlockSpec((rt, GLB), lambda i, j: (i, COL_GL // GLB + j)), first, second,
                  pl.BlockSpec((rt, GLB), lambda i, j: (i, j % 2))],
        out_specs=[first, second, pl.BlockSpec((rt, GLB), lambda i, j: (i, COL_GL // GLB + j))],
        out_shape=[_sds((T, D), BF16), _sds((T, D), BF16), _sds((T, P_W), BF16)],
        compiler_params=_params(("arbitrary", "arbitrary")),
    )(p, ya, yb, dz)


def _swiglu_fwd(name, f):
    T = f.shape[0]

    def body(i, ins, ps, outs, acc):
        outs[0][...] = (_silu(ins[0][...].astype(F32)) * ins[1][...].astype(F32)).astype(BF16)

    return _ew(name, body, T // TR, [(f, *_rowblk(D_FF, 0)), (f, *_rowblk(D_FF, 1))], [],
               [(_sds((T, D_FF), BF16), *_rowblk(D_FF))])[0]


def _swiglu_bwd(name, f, ds):
    T = f.shape[0]

    def body(i, ins, ps, outs, acc):
        gate, up, dsv = ins[0][...].astype(F32), ins[1][...].astype(F32), ins[2][...].astype(F32)
        dgate = dsv * up * _silu_grad(gate)
        dup = dsv * _silu(gate)
        outs[0][...] = jnp.concatenate([dgate, dup], axis=1).astype(BF16)

    return _ew(name, body, T // TR, [(f, *_rowblk(D_FF, 0)), (f, *_rowblk(D_FF, 1)), (ds, *_rowblk(D_FF))], [],
               [(_sds((T, 2 * D_FF), BF16), *_rowblk(2 * D_FF))])[0]


AB = 128
CTX_BLKS = CTX // AB


def _rope_tables(S):
    pos = jnp.arange(S, dtype=jnp.int32)
    inv = ROPE_BASE ** (-jnp.arange(N_FREQ, dtype=F32) / N_FREQ)
    ang_r = (pos // GRID_W).astype(F32)[:, None] * inv[None, :]
    ang_c = (pos % GRID_W).astype(F32)[:, None] * inv[None, :]
    cos = jnp.concatenate([jnp.cos(ang_r)] * 2 + [jnp.cos(ang_c)] * 2, axis=1)
    sin = jnp.concatenate([-jnp.sin(ang_r), jnp.sin(ang_r), -jnp.sin(ang_c), jnp.sin(ang_c)], axis=1)
    return cos, sin


def _rope(x, cos, sin):
    w = x.shape[1]
    reps = w // HEAD
    lane = lax.broadcasted_iota(jnp.int32, x.shape, 1)
    partner = jnp.where((lane & 63) < 32, pltpu.roll(x, w - 32, 1), pltpu.roll(x, 32, 1))
    return x * jnp.tile(cos, (1, reps)) + partner * jnp.tile(sin, (1, reps))


def _unrope(dx, cos, sin):
    w = dx.shape[1]
    reps = w // HEAD
    lane = lax.broadcasted_iota(jnp.int32, dx.shape, 1)
    t = dx * jnp.tile(sin, (1, reps))
    partner = jnp.where((lane & 63) < 32, pltpu.roll(t, w - 32, 1), pltpu.roll(t, 32, 1))
    return dx * jnp.tile(cos, (1, reps)) + partner


def _qkv_prep(name, p, cos, sin, S):
    T = CTX + S
    nt = T // AB
    KW = N_KV * HEAD

    def with_ones(v):
        ones = jnp.ones((AB, HEAD), BF16)
        return jnp.concatenate([v[:, kh * HEAD:(kh + 1) * HEAD] if part == 0 else ones
                                for kh in range(N_KV) for part in range(2)], axis=1)

    def kern(q_ref, k_ref, v_ref, cos_ref, sin_ref, qa_ref, kp_ref, vp_ref, kc_ref, vc_ref):
        i = pl.program_id(0)
        cos_v, sin_v = cos_ref[...], sin_ref[...]
        @pl.when(i < CTX_BLKS)
        def _():
            qa_ref[...] = (q_ref[...].astype(F32) * ATT_SCALE).astype(BF16)
            kc_ref[...] = k_ref[...]
            vc_ref[...] = with_ones(v_ref[...])

        @pl.when((i < CTX_BLKS) | (i >= nt))
        def _():
            kp_ref[...] = jnp.zeros(kp_ref.shape, BF16)
            vp_ref[...] = jnp.zeros(vp_ref.shape, BF16)

        @pl.when((i >= CTX_BLKS) & (i < nt))
        def _():
            qa_ref[...] = (_rope(q_ref[...].astype(F32), cos_v, sin_v) * ATT_SCALE).astype(BF16)
            kp_ref[...] = _rope(k_ref[...].astype(F32), cos_v, sin_v).astype(BF16)
            vp_ref[...] = with_ones(v_ref[...])

    tok = lambda i: jnp.minimum(i, nt - 1)
    lat_map = lambda i: (jnp.clip(i - CTX_BLKS, 0, nt - CTX_BLKS - 1), 0)
    ctx_map = lambda i: (jnp.minimum(i, CTX_BLKS - 1), 0)
    return pl.pallas_call(
        kern, name=name, grid=(nt + CTX_BLKS,),
        in_specs=[pl.BlockSpec((AB, N_Q * HEAD), lambda i: (tok(i), COL_Q // (N_Q * HEAD))),
                  pl.BlockSpec((AB, KW), lambda i: (tok(i), COL_K // KW)),
                  pl.BlockSpec((AB, KW), lambda i: (tok(i), COL_V // KW)),
                  pl.BlockSpec((AB, HEAD), lat_map), pl.BlockSpec((AB, HEAD), lat_map)],
        out_specs=[pl.BlockSpec((AB, N_Q * HEAD), lambda i: (tok(i), 0)),
                   pl.BlockSpec((AB, KW), lambda i: (i, 0)), pl.BlockSpec((AB, 2 * KW), lambda i: (i, 0)),
                   pl.BlockSpec((AB, KW), ctx_map), pl.BlockSpec((AB, 2 * KW), ctx_map)],
        out_shape=[_sds((T, N_Q * HEAD), BF16), _sds((S + 2 * CTX, KW), BF16), _sds((S + 2 * CTX, 2 * KW), BF16),
                   _sds((CTX, KW), BF16), _sds((CTX, 2 * KW), BF16)],
        compiler_params=_params(),
    )(p, p, p, cos, sin)


GW = Q_PER_KV * HEAD


def _band_bias(S):
    r = jnp.arange(AB, dtype=jnp.int32)[:, None]
    c = jnp.arange(3 * AB, dtype=jnp.int32)[None, :]
    near = jnp.abs(c - AB - r) <= AB
    valid = jnp.stack([near & (c >= AB), near, near & (c < 2 * AB)])
    return jnp.where(valid, 0.0, NEG_INF).astype(F32)


def _bias_spec(S):
    nb = S // AB
    return pl.BlockSpec((None, AB, 3 * AB), lambda kh, n: (jnp.where(n == 0, 0, jnp.where(n == nb - 1, 2, 1)), 0, 0))


def _head_probs(q, sink, kc, vce, kb, vbe, bias):
    s_c = _dot(q, kc, NT)
    m = jnp.maximum(jnp.max(s_c, axis=-1, keepdims=True), sink)
    if kb is not None:
        s_b = _dot(q, kb, NT) + bias
        m = jnp.maximum(m, jnp.max(s_b, axis=-1, keepdims=True))
    p_c = jnp.exp(s_c - m).astype(BF16)
    acc = _dot(p_c, vce)
    p_b = None
    if kb is not None:
        p_b = jnp.exp(s_b - m).astype(BF16)
        acc = acc + _dot(p_b, vbe)
    return p_c, p_b, m, acc


def _attn_fwd(name, qa, kc, vc, sink4, S, band=None, prev=None, carry=None):
    T = qa.shape[0]
    has_band = band is not None
    nq = S // AB if has_band else CTX_BLKS
    q_off = CTX_BLKS if has_band else 0

    def kern(*refs):
        q_ref, kc_ref, vc_ref, sink_ref = refs[:4]
        rest = refs[4:]
        o_ref = rest[-1]
        n = pl.program_id(1)
        kc_v, vce = kc_ref[...], vc_ref[...]
        kb = vbe = bias = None
        if has_band:
            kp_ref, vp_ref, bias_ref = rest[:3]
            start = pl.multiple_of(n * AB + (CTX - AB), AB)
            kb = kp_ref[pl.ds(start, 3 * AB), :]
            vbe = vp_ref[pl.ds(start, 3 * AB), :]
            bias = bias_ref[...]
        outs = []
        for g in range(Q_PER_KV):
            sink = sink_ref[g:g + 1, 0:1]
            _, _, m, acc = _head_probs(q_ref[:, g * HEAD:(g + 1) * HEAD], sink, kc_v, vce, kb, vbe, bias)
            l = acc[:, HEAD:] + jnp.exp(sink - m)
            outs.append(acc[:, :HEAD] / l)
        o_ref[...] = jnp.concatenate(outs, axis=1).astype(BF16)

    in_specs = [pl.BlockSpec((AB, GW), lambda kh, n: (n + q_off, kh)),
                pl.BlockSpec((CTX, HEAD), lambda kh, n: (0, kh)), pl.BlockSpec((CTX, 2 * HEAD), lambda kh, n: (0, kh)),
                pl.BlockSpec((None, Q_PER_KV, HEAD), lambda kh, n: (kh, 0, 0))]
    args = [qa, kc, vc, sink4]
    if has_band:
        in_specs += [pl.BlockSpec((S + 2 * CTX, HEAD), lambda kh, n: (0, kh)),
                     pl.BlockSpec((S + 2 * CTX, 2 * HEAD), lambda kh, n: (0, kh)), _bias_spec(S)]
        args += list(band)
    alias = {}
    if prev is not None:
        in_specs.append(ANY)
        alias = {len(args): 0}
        args.append(prev)
    ci, ca, co, cs, cscr = _carry_args(carry)
    res = pl.pallas_call(
        _carried(kern, carry, len(args), 1, *_grid_ends((N_KV, nq))), name=name, grid=(N_KV, nq),
        in_specs=in_specs + ci,
        out_specs=[pl.BlockSpec((AB, GW), lambda kh, n: (n + q_off, kh))] + co,
        out_shape=[_sds((T, N_Q * HEAD), BF16)] + cs, input_output_aliases=alias, scratch_shapes=cscr,
        compiler_params=_params(("arbitrary", "arbitrary")),
    )(*args, *ca)
    return res[0] if carry is None else (res[0], res[1:])


def _attn_bwd(name, qa, kc, vc, sink4, o_all, do_all, S, band=None, prev_dq=None, carry=None):
    T = qa.shape[0]
    has_band = band is not None
    nq = S // AB if has_band else CTX_BLKS
    q_off = CTX_BLKS if has_band else 0
    KW = N_KV * HEAD

    def kern(*refs):
        q_ref, kc_ref, vc_ref, sink_ref, o_ref, do_ref = refs[:6]
        rest = refs[6:]
        if has_band:
            kp_ref, vp_ref, bias_ref = rest[:3]
            rest = rest[3:]
        if prev_dq is not None:
            rest = rest[1:]
        dq_ref, dkc_ref, dvc_ref, dsink_ref = rest[:4]
        n = pl.program_id(1)

        @pl.when(n == 0)
        def _():
            dkc_ref[...] = jnp.zeros(dkc_ref.shape, F32)
            dvc_ref[...] = jnp.zeros(dvc_ref.shape, F32)
            dsink_ref[...] = jnp.zeros(dsink_ref.shape, F32)
            if has_band:
                rest[4][...] = jnp.zeros(rest[4].shape, F32)
                rest[5][...] = jnp.zeros(rest[5].shape, F32)

        kc_v, vce = kc_ref[...], vc_ref[...]
        vc_v = vce[:, :HEAD]
        kb = vbe = vb = bias = None
        if has_band:
            start = pl.multiple_of(n * AB + (CTX - AB), AB)
            kb = kp_ref[pl.ds(start, 3 * AB), :]
            vbe = vp_ref[pl.ds(start, 3 * AB), :]
            vb = vbe[:, :HEAD]
            bias = bias_ref[...]
        stack = lambda ref: jnp.concatenate([ref[:, g * HEAD:(g + 1) * HEAD] for g in range(Q_PER_KV)], axis=0)
        q4, do4 = stack(q_ref), stack(do_ref)
        sink = jnp.concatenate([jnp.broadcast_to(sink_ref[g:g + 1, 0:1], (AB, 1)) for g in range(Q_PER_KV)], axis=0)
        s_c = _dot(q4, kc_v, NT)
        m = jnp.maximum(jnp.max(s_c, axis=-1, keepdims=True), sink)
        if has_band:
            s_b = _dot(q4, kb, NT) + jnp.tile(bias, (Q_PER_KV, 1))
            m = jnp.maximum(m, jnp.max(s_b, axis=-1, keepdims=True))
        p_c = jnp.exp(s_c - m).astype(BF16).astype(F32)
        p_sink = jnp.exp(sink - m)
        l = jnp.sum(p_c, axis=-1, keepdims=True) + p_sink
        if has_band:
            p_b = jnp.exp(s_b - m).astype(BF16).astype(F32)
            l = l + jnp.sum(p_b, axis=-1, keepdims=True)
        inv = 1.0 / l
        delta = jnp.sum(do4.astype(F32) * stack(o_ref).astype(F32), axis=-1, keepdims=True)
        do4b = do4.astype(BF16)
        pn_c = (p_c * inv).astype(BF16)
        ds_c = (p_c * inv * (_dot(do4b, vc_v, NT) - delta)).astype(BF16)
        dq4 = _dot(ds_c, kc_v)
        dkc_ref[...] += _dot(ds_c, q4, TN)
        dvc_ref[...] += _dot(pn_c, do4b, TN)
        if has_band:
            pn_b = (p_b * inv).astype(BF16)
            ds_b = (p_b * inv * (_dot(do4b, vb, NT) - delta)).astype(BF16)
            dq4 = dq4 + _dot(ds_b, kb)
            rest[4][pl.ds(start, 3 * AB), :] += _dot(ds_b, q4, TN)
            rest[5][pl.ds(start, 3 * AB), :] += _dot(pn_b, do4b, TN)
        dq4 = dq4 * ATT_SCALE
        dq_ref[...] = jnp.concatenate([dq4[g * AB:(g + 1) * AB, :] for g in range(Q_PER_KV)], axis=1)
        ps = p_sink * inv * delta
        dsink_ref[...] += jnp.concatenate(
            [jnp.broadcast_to(-jnp.sum(ps[g * AB:(g + 1) * AB, :], axis=0, keepdims=True), (1, HEAD))
             for g in range(Q_PER_KV)], axis=0)

    q_spec = pl.BlockSpec((AB, GW), lambda kh, n: (n + q_off, kh))
    c_spec = pl.BlockSpec((CTX, HEAD), lambda kh, n: (0, kh))
    ce_spec = pl.BlockSpec((CTX, 2 * HEAD), lambda kh, n: (0, kh))
    s_spec = pl.BlockSpec((None, Q_PER_KV, HEAD), lambda kh, n: (kh, 0, 0))
    in_specs = [q_spec, c_spec, ce_spec, s_spec, q_spec, q_spec]
    args = [qa, kc, vc, sink4, o_all, do_all]
    out_specs = [q_spec, c_spec, c_spec, s_spec]
    out_shape = [_sds((T, N_Q * HEAD), F32), _sds((CTX, KW), F32), _sds((CTX, KW), F32), _sds((N_KV, Q_PER_KV, HEAD), F32)]
    if has_band:
        p_spec = pl.BlockSpec((S + 2 * CTX, HEAD), lambda kh, n: (0, kh))
        in_specs += [p_spec, pl.BlockSpec((S + 2 * CTX, 2 * HEAD), lambda kh, n: (0, kh)), _bias_spec(S)]
        args += list(band)
        out_specs += [p_spec, p_spec]
        out_shape += [_sds((S + 2 * CTX, KW), F32)] * 2
    alias = {}
    if prev_dq is not None:
        in_specs.append(ANY)
        alias = {len(args): 0}
        args.append(prev_dq)
    ci, ca, co, cs, cscr = _carry_args(carry)
    n_out = len(out_specs)
    res = pl.pallas_call(
        _carried(kern, carry, len(args), n_out, *_grid_ends((N_KV, nq))), name=name, grid=(N_KV, nq),
        in_specs=in_specs + ci, out_specs=out_specs + co, out_shape=out_shape + cs, scratch_shapes=cscr,
        input_output_aliases=alias, compiler_params=_params(("arbitrary", "arbitrary")),
    )(*args, *ca)
    return res if carry is None else (res[:n_out], res[n_out:])


def _dqkv_assemble(name, dp, dq_all, dkp, dvp, dkc_l, dvc_l, dkc_c, dvc_c, cos, sin, S):
    T = CTX + S
    KW = N_KV * HEAD
    HALF = N_Q * HEAD // 2

    def kern(dq_ref, dkp_ref, dvp_ref, dkcl_ref, dvcl_ref, dkcc_ref, dvcc_ref, cos_ref, sin_ref, dp_in, out_ref):
        i = pl.program_id(0)
        j = pl.program_id(1)
        cos_v, sin_v = cos_ref[...], sin_ref[...]

        @pl.when((j < 2) & (i == 0))
        def _():
            out_ref[...] = dq_ref[...].astype(BF16)

        @pl.when((j < 2) & (i > 0))
        def _():
            out_ref[...] = _unrope(dq_ref[...], cos_v, sin_v).astype(BF16)

        @pl.when((j == 2) & (i == 0))
        def _():
            out_ref[...] = jnp.concatenate([dkcl_ref[...] + dkcc_ref[...], dvcl_ref[...] + dvcc_ref[...]],
                                           axis=1).astype(BF16)

        @pl.when((j == 2) & (i > 0))
        def _():
            out_ref[...] = jnp.concatenate([_unrope(dkp_ref[...], cos_v, sin_v), dvp_ref[...]], axis=1).astype(BF16)

    same = lambda i, j: (i, 0)
    lat_map = lambda i, j: (jnp.maximum(i - 1, 0), 0)
    ctx_map = lambda i, j: (0, 0)
    return pl.pallas_call(
        kern, name=name, grid=(T // TR, 3),
        in_specs=[pl.BlockSpec((TR, HALF), lambda i, j: (i, jnp.minimum(j, 1))),
                  pl.BlockSpec((TR, KW), same), pl.BlockSpec((TR, KW), same),
                  pl.BlockSpec((CTX, KW), ctx_map), pl.BlockSpec((CTX, KW), ctx_map),
                  pl.BlockSpec((CTX, KW), ctx_map), pl.BlockSpec((CTX, KW), ctx_map),
                  pl.BlockSpec((TR, HEAD), lat_map), pl.BlockSpec((TR, HEAD), lat_map), ANY],
        out_specs=pl.BlockSpec((TR, HALF), lambda i, j: (i, COL_Q // HALF + j)),
        out_shape=_sds((T, P_W), BF16), input_output_aliases={9: 0},
        compiler_params=_params(("arbitrary", "arbitrary")),
    )(dq_all, dkp, dvp, dkc_l, dvc_l, dkc_c, dvc_c, cos, sin, dp)


RB = 128
CH = 256
HALO = 8
SUB = 8
GRP = 8


def _vscan(a, b, reverse):
    row = lax.broadcasted_iota(jnp.int32, a.shape, 0)
    A, H = a, b
    for s in (1, 2, 4):
        sh = SUB - s if reverse else s
        m = (row < SUB - s) if reverse else (row >= s)
        As = pltpu.roll(A, sh, 0)
        Hs = pltpu.roll(H, sh, 0)
        H = jnp.where(m, A * Hs + H, H)
        A = jnp.where(m, A * As, A)
    return A, H


def _scan_rows(a_ref, b_ref, r0, nrows, reverse, carry, emit):
    ngrp = nrows // (SUB * GRP)
    row = lax.broadcasted_iota(jnp.int32, (SUB, RB), 0)

    def grp(gi, carry):
        g = (ngrp - 1 - gi) if reverse else gi
        base = r0 + g * (SUB * GRP)
        for v in (range(GRP - 1, -1, -1) if reverse else range(GRP)):
            rs = pl.multiple_of(base + v * SUB, SUB)
            A, H = _vscan(a_ref[pl.ds(rs, SUB), :], b_ref[pl.ds(rs, SUB), :], reverse)
            hf = H + A * carry
            if reverse:
                before = jnp.where(row == SUB - 1, carry, pltpu.roll(hf, SUB - 1, 0))
                carry = hf[0:1, :]
            else:
                before = jnp.where(row == 0, carry, pltpu.roll(hf, 1, 0))
                carry = hf[SUB - 1:SUB, :]
            emit(rs, hf, before)
        return carry

    return lax.fori_loop(0, ngrp, grp, carry)


def _pad_start(ci):
    return pl.multiple_of(ci * CH + HALO * jnp.minimum(ci, 1), HALO)


def _conv_taps(ext, transpose=False):
    n = CH + 2 * HALO
    taps = []
    for k in range(CONV_W):
        off = CONV_LEFT - k if transpose else k - CONV_LEFT
        taps.append(ext[HALO:HALO + CH, :] if off == 0 else pltpu.roll(ext, (-off) % n, 0)[HALO:HALO + CH, :])
    return taps


def _lru_gates(xl, w4, b4, ls):
    pre = _dot(xl.astype(BF16), w4) + b4
    out = []
    for d in range(2):
        r = _sigmoid(pre[:, d * RB:(d + 1) * RB])
        i = _sigmoid(pre[:, (2 + d) * RB:(3 + d) * RB])
        la = LRU_C * r * ls[d:d + 1, :]
        a = jnp.exp(la)
        q = -jnp.tanh(la) * (1.0 + a * a)
        out.append((r, i, a, q))
    return out


def _rnn_specs(T):
    col = lambda n, *_: (0, n)
    return dict(
        xr=pl.BlockSpec((T, RB), lambda n, *_: (0, COL_XR // RB + n)),
        gr=pl.BlockSpec((T, RB), lambda n, *_: (0, COL_GR // RB + n)),
        act=pl.BlockSpec((T, RB), col),
        cw=pl.BlockSpec((CONV_W, RB), col), cb=pl.BlockSpec((1, RB), col),
        w4=pl.BlockSpec((None, RB, 4 * RB), lambda n, *_: (n, 0, 0)),
        b4=pl.BlockSpec((None, 1, 4 * RB), lambda n, *_: (n, 0, 0)),
        lam=pl.BlockSpec((2, RB), col))


PAD_ROWS = 3 * HALO


def _zero_pads(pad_ref, T):
    for r in (0, HALO + CTX, 2 * HALO + T):
        pad_ref[r:r + HALO, :] = jnp.zeros((HALO, RB), F32)


def _fill_padded(pad_ref, src_ref, T):
    _zero_pads(pad_ref, T)
    pad_ref[HALO:HALO + CTX, :] = src_ref[0:CTX, :].astype(F32)
    pad_ref[2 * HALO + CTX:2 * HALO + T, :] = src_ref[CTX:T, :].astype(F32)


def _pad_rows(ci):
    return pl.ds(pl.multiple_of(ci * CH + HALO + HALO * jnp.minimum(ci, 1), HALO), CH)


def _rnn_fwd(name, p, cw, cb, w4, b4, lam, T, carry=None):
    def kern(xr_ref, gr_ref, cw_ref, cb_ref, w4_ref, b4_ref, lam_ref, u_ref, hpf_ref, hpb_ref,
             xpad, a0, b0, a1, b1, y):
        _fill_padded(xpad, xr_ref, T)
        ls = _log_sigmoid(lam_ref[...])
        w4v, b4v, cwv, cbv = w4_ref[...], b4_ref[...], cw_ref[...], cb_ref[...]

        def chunk(ci, _):
            base = pl.multiple_of(ci * CH, CH)
            taps = _conv_taps(xpad[pl.ds(_pad_start(ci), CH + 2 * HALO), :])
            xl = cbv + sum(taps[k] * cwv[k:k + 1, :] for k in range(CONV_W))
            for d, (r, i, a, q) in enumerate(_lru_gates(xl, w4v, b4v, ls)):
                (a0, a1)[d][pl.ds(base, CH), :] = a
                (b0, b1)[d][pl.ds(base, CH), :] = jnp.sqrt(q) * (i * xl)
            return 0

        lax.fori_loop(0, T // CH, chunk, 0)
        zero = jnp.zeros((1, RB), F32)

        def emit_f(rs, hf, before):
            y[pl.ds(rs, SUB), :] = hf
            hpf_ref[pl.ds(rs, SUB), :] = before

        def emit_b(rs, hf, before):
            y[pl.ds(rs, SUB), :] += hf
            hpb_ref[pl.ds(rs, SUB), :] = before

        _scan_rows(a0, b0, 0, T, False, zero, emit_f)
        c = _scan_rows(a1, b1, 0, CTX, True, zero, emit_b)
        _scan_rows(a1, b1, CTX, T - CTX, True, c, emit_b)

        def finish(ci, _):
            base = pl.multiple_of(ci * CH, CH)
            gr = gr_ref[pl.ds(base, CH), :].astype(F32)
            u_ref[pl.ds(base, CH), :] = (y[pl.ds(base, CH), :] * _gelu(gr)).astype(BF16)
            return 0

        lax.fori_loop(0, T // CH, finish, 0)

    sp = _rnn_specs(T)
    ci, ca, co, cs, cscr = _carry_args(carry)
    res = pl.pallas_call(
        _carried(kern, carry, 7, 3, *_grid_ends((N_RNN_BLOCKS,))), name=name, grid=(N_RNN_BLOCKS,),
        in_specs=[sp["xr"], sp["gr"], sp["cw"], sp["cb"], sp["w4"], sp["b4"], sp["lam"]] + ci,
        out_specs=[sp["act"]] * 3 + co,
        out_shape=[_sds((T, D), BF16), _sds((T, D), F32), _sds((T, D), F32)] + cs,
        scratch_shapes=[pltpu.VMEM((T + PAD_ROWS, RB), F32)] + [pltpu.VMEM((T, RB), F32)] * 5 + cscr,
        compiler_params=_params(),
    )(p, p, cw, cb, w4, b4, lam, *ca)
    return res if carry is None else (res[:3], res[3:])


def _rnn_bwd(name, p, du, hpf, hpb, dp, cw, cb, w4, b4, lam, T, carry=None):
    def kern(xr_ref, gr_ref, du_ref, hpf_ref, hpb_ref, cw_ref, cb_ref, w4_ref, b4_ref, lam_ref, dp_in,
             dp_ref, dcw_ref, dcb_ref, dw4_ref, db4_ref, dlam_ref,
             xpad, dxpad, a0, a1, c0, c1, dy, dgr_ref):
        j = pl.program_id(1)

        @pl.when(j == 0)
        def _():
            work(xr_ref, gr_ref, du_ref, hpf_ref, hpb_ref, cw_ref, cb_ref, w4_ref, b4_ref, lam_ref,
                 dp_ref, dgr_ref, dcw_ref, dcb_ref, dw4_ref, db4_ref, dlam_ref, xpad, dxpad, a0, a1, c0, c1, dy)

        @pl.when(j == 1)
        def _():
            dp_ref[...] = dgr_ref[...]

    def work(xr_ref, gr_ref, du_ref, hpf_ref, hpb_ref, cw_ref, cb_ref, w4_ref, b4_ref, lam_ref,
             dxr_ref, dgr_ref, dcw_ref, dcb_ref, dw4_ref, db4_ref, dlam_ref, xpad, dxpad, a0, a1, c0, c1, dy):
        _fill_padded(xpad, xr_ref, T)
        _zero_pads(dxpad, T)
        lam_v = lam_ref[...]
        ls = _log_sigmoid(lam_v)
        w4v, b4v, cwv, cbv = w4_ref[...], b4_ref[...], cw_ref[...], cb_ref[...]

        def conv_chunk(ci):
            taps = _conv_taps(xpad[pl.ds(_pad_start(ci), CH + 2 * HALO), :])
            return taps, cbv + sum(taps[k] * cwv[k:k + 1, :] for k in range(CONV_W))

        def phase_a(ci, _):
            base = pl.multiple_of(ci * CH, CH)
            rows = pl.ds(base, CH)
            _, xl = conv_chunk(ci)
            (r0, i0, av0, q0), (r1, i1, av1, q1) = _lru_gates(xl, w4v, b4v, ls)
            yv = ((av0 * hpf_ref[rows, :] + jnp.sqrt(q0) * (i0 * xl))
                  + (av1 * hpb_ref[rows, :] + jnp.sqrt(q1) * (i1 * xl)))
            gr = gr_ref[rows, :].astype(F32)
            duv = du_ref[rows, :].astype(F32)
            dyv = duv * _gelu(gr)
            dgr_ref[rows, :] = (duv * yv * _gelu_grad(gr)).astype(BF16)
            dy[rows, :] = dyv
            a0[rows, :] = av0
            a1[rows, :] = av1
            c0[rows, :] = av0 * dyv
            c1[rows, :] = av1 * dyv
            return 0

        lax.fori_loop(0, T // CH, phase_a, 0)
        zero = jnp.zeros((1, RB), F32)

        def emit0(rs, hf, before):
            c0[pl.ds(rs, SUB), :] = dy[pl.ds(rs, SUB), :] + before

        def emit1(rs, hf, before):
            c1[pl.ds(rs, SUB), :] = dy[pl.ds(rs, SUB), :] + before

        _scan_rows(a0, c0, 0, T, True, zero, emit0)
        c = _scan_rows(a1, c1, CTX, T - CTX, False, zero, emit1)
        _scan_rows(a1, c1, 0, CTX, False, c, emit1)

        dw4_ref[...] = jnp.zeros(dw4_ref.shape, F32)
        db4_ref[...] = jnp.zeros(db4_ref.shape, F32)
        dlam_ref[...] = jnp.zeros(dlam_ref.shape, F32)
        dcw_ref[...] = jnp.zeros(dcw_ref.shape, F32)
        dcb_ref[...] = jnp.zeros(dcb_ref.shape, F32)

        def phase_c(ci, _):
            base = pl.multiple_of(ci * CH, CH)
            rows = pl.ds(base, CH)
            _, xl = conv_chunk(ci)
            gates = _lru_gates(xl, w4v, b4v, ls)
            dxl = jnp.zeros((CH, RB), F32)
            dpre_a, dpre_x, dls = [], [], []
            for d, (r, i, a, q) in enumerate(gates):
                g = (c0, c1)[d][rows, :]
                hp = (hpf_ref, hpb_ref)[d][rows, :]
                gm = g * jnp.sqrt(q)
                di = gm * xl
                dxl = dxl + gm * i
                dla = a * (g * hp - a * (g * (i * xl)) * lax.rsqrt(q))
                dr = dla * (LRU_C * ls[d:d + 1, :])
                dls.append(_colsum(dla * (LRU_C * r)))
                dpre_a.append(dr * r * (1.0 - r))
                dpre_x.append(di * i * (1.0 - i))
            dpre = jnp.concatenate(dpre_a + dpre_x, axis=1)
            dpre_b = dpre.astype(BF16)
            dxl = dxl + _dot(dpre_b, w4v, NT)
            dw4_ref[...] += _dot(xl.astype(BF16), dpre_b, TN)
            db4_ref[...] += _colsum(dpre)
            dlam_ref[...] += jnp.concatenate(dls, axis=0)
            dcb_ref[...] += _colsum(dxl)
            dxpad[_pad_rows(ci), :] = dxl
            return 0

        lax.fori_loop(0, T // CH, phase_c, 0)
        dlam_ref[...] = dlam_ref[...] * _sigmoid(-lam_v)

        def phase_d(ci, _):
            base = pl.multiple_of(ci * CH, CH)
            rows = pl.ds(base, CH)
            xtaps, _ = conv_chunk(ci)
            dtaps = _conv_taps(dxpad[pl.ds(_pad_start(ci), CH + 2 * HALO), :], transpose=True)
            dxl = dxpad[_pad_rows(ci), :]
            dxr_ref[rows, :] = sum(dtaps[k] * cwv[k:k + 1, :] for k in range(CONV_W)).astype(BF16)
            dcw_ref[...] += jnp.concatenate([_colsum(dxl * xtaps[k]) for k in range(CONV_W)], axis=0)
            return 0

        lax.fori_loop(0, T // CH, phase_d, 0)

    sp = _rnn_specs(T)
    dp_spec = pl.BlockSpec((T, RB), lambda n, j: (0, COL_XR // RB + n + j * (COL_GR - COL_XR) // RB))
    ci, ca, co, cs, cscr = _carry_args(carry)
    res = pl.pallas_call(
        _carried(kern, carry, 11, 6, *_grid_ends((N_RNN_BLOCKS, 2))), name=name, grid=(N_RNN_BLOCKS, 2),
        in_specs=[sp["xr"], sp["gr"], sp["act"], sp["act"], sp["act"], sp["cw"], sp["cb"], sp["w4"], sp["b4"],
                  sp["lam"], ANY] + ci,
        out_specs=[dp_spec, sp["cw"], sp["cb"], sp["w4"], sp["b4"], sp["lam"]] + co,
        out_shape=[_sds((T, P_W), BF16), _sds((CONV_W, D), F32), _sds((1, D), F32),
                   _sds((N_RNN_BLOCKS, RB, 4 * RB), F32), _sds((N_RNN_BLOCKS, 1, 4 * RB), F32), _sds((2, D), F32)] + cs,
        scratch_shapes=([pltpu.VMEM((T + PAD_ROWS, RB), F32)] * 2 + [pltpu.VMEM((T, RB), F32)] * 5
                        + [pltpu.VMEM((T, RB), BF16)] + cscr),
        input_output_aliases={10: 0},
        compiler_params=_params(("arbitrary", "arbitrary")),
    )(p, p, du, hpf, hpb, cw, cb, w4, b4, lam, dp, *ca)
    return res if carry is None else (res[:6], res[6:])


class _Plan:
    def __init__(self, shards, Ws):
        L = len(Ws)
        self.shards, self.Ws = shards, Ws
        self.Gs = [None] * L
        self.slots = [dict() for _ in range(L)]
        self.gate_slots = [None] * L
        self.table = {}
        for l in range(L):
            t = f"l{l}_"
            self.table[t + "proj"] = [("gather", l, k) for k in ("wo_rnn", "wo_attn", "wout")]
            self.table[t + "rnn_fwd"] = [("gather", l, "wffn_in_t")]
            self.table[t + "attn_lat_fwd"] = [("gather", l + 1, "win_t")] if l + 1 < L else []
            self.table[t + "ffn_in"] = [("gather", l, "wffn_out")]
            self.table[t + "ffn_in_dx"] = [("scatter", l, "wffn_out")]
            self.table[t + "attn_lat_bwd"] = [("scatter", l, "wffn_in_t")]
            self.table[t + "rnn_bwd"] = ([("scatter", l, k) for k in ("wout", "wo_attn", "wo_rnn")]
                                         + ([("scatter", l + 1, "win_t"), ("gates", l + 1, "w4")] if l + 1 < L else []))
        self.table["l0_proj_dx"] = [("scatter", 0, "win_t_a")]
        self.table["l0_proj_dw_b"] = [("gates", 0, "w4")]

    def carry(self, name):
        jobs = []
        for kind, l, k in self.table.get(name, []):
            if kind == "gather":
                jobs.append(("gather", self.shards[l][k]))
            elif kind == "scatter":
                jobs.append(("scatter", self.Gs[l][k].reshape(N_DEV, -1, self.Gs[l][k].shape[-1])))
            else:
                jobs.append(("gather", self.Gs[l]["w4"].reshape(N_RNN_BLOCKS * RB, 4 * RB).astype(BF16)))
        return _Carry(jobs) if jobs else None

    def done(self, name, got):
        for (kind, l, k), res in zip(self.table[name], got):
            if kind == "gather":
                self.Ws[l][k] = res.reshape(-1, D)
            elif kind == "scatter":
                self.slots[l][k] = res
            else:
                self.gate_slots[l] = res


def _run(X, fn, name, *args, **kw):
    carry = None if X is None else X.carry(name)
    if carry is None:
        return fn(name, *args, **kw)
    out, got = fn(name, *args, carry=carry, **kw)
    X.done(name, got)
    return out


def _layer_fwd(l, xa, h, W, rope, S, nxt, X=None):
    T = xa.shape[0]
    tag = f"l{l}_"
    cos, sin, bias = rope
    p = _run(X, _mm_act, tag + "proj", h, W["win_t"], "NT", BF16)
    u, hpf, hpb = _run(X, _rnn_fwd, tag + "rnn_fwd", p, W["cw"], W["cb"], W["w4"], W["b4"], W["lam"], T)
    qa, kp, vp, kc, vc = _qkv_prep(tag + "qkv_prep", p, cos, sin, S)
    o_all = _attn_fwd(tag + "attn_ctx_fwd", qa, kc, vc, W["sink4"], S)
    o_all = _run(X, _attn_fwd, tag + "attn_lat_fwd", qa, kc, vc, W["sink4"], S, band=(kp, vp, bias), prev=o_all)
    ya = _mm_act(tag + "o_rnn", u, W["wo_rnn"], "NN")
    yb = _mm_act(tag + "o_attn", o_all, W["wo_attn"], "NN")
    z = _gate_fwd(tag + "gate_fwd", p, ya, yb)
    m = _mm_act(tag + "out", z, W["wout"], "NN")
    x1, h2 = _resid_norm_fwd(tag + "mix_resid", xa, m, W["g_mix_post"], W["mod"], GA1, W["g_ffn_pre"], W["mod"], SH2, SC2)
    f = _run(X, _mm_act, tag + "ffn_in", h2, W["wffn_in_t"], "NT", BF16)
    s = _swiglu_fwd(tag + "swiglu_fwd", f)
    e = _mm_act(tag + "ffn_out", s, W["wffn_out"], "NN")
    saved = dict(xa=xa, h=h, p=p, u=u, hpf=hpf, hpb=hpb, qa=qa, kp=kp, vp=vp, kc=kc, vc=vc, o_all=o_all,
                 ya=ya, yb=yb, z=z, m=m, x1=x1, h2=h2, f=f, s=s, e=e)
    if nxt[0] == "norm":
        out = _resid_norm_fwd(tag + "ffn_resid", x1, e, W["g_ffn_post"], W["mod"], GA2, nxt[1], nxt[2], SH1, SC1)
    else:
        out = _resid_loss_fwd(tag + "ffn_resid_loss", x1, e, W["g_ffn_post"], W["mod"], GA2, nxt[1])
    return saved, out


def _layer_bwd(l, dx2, A, W, rope, S, X=None):
    T = dx2.shape[0]
    tag = f"l{l}_"
    cos, sin, bias = rope
    G = {}
    if X is not None:
        X.Gs[l] = G
    de, dga2, G["g_ffn_post"] = _resid_bwd(tag + "ffn_resid_bwd", dx2, A["e"], W["g_ffn_post"], W["mod"], GA2)
    ds = _mm_act(tag + "ffn_out_dx", de, W["wffn_out"], "NT", BF16)
    G["wffn_out"] = _mm_wgrad(tag + "ffn_out_dw", A["s"], de)
    df = _swiglu_bwd(tag + "swiglu_bwd", A["f"], ds)
    dh2 = _run(X, _mm_act, tag + "ffn_in_dx", df, W["wffn_in_t"], "NN")
    G["wffn_in_t"] = _mm_wgrad(tag + "ffn_in_dw", df, A["h2"])
    dx1, dm, dsh2, dsc2, G["g_ffn_pre"], dga1, G["g_mix_post"] = _normmod_resid_bwd(
        tag + "mix_resid_bwd", dh2, A["x1"], W["g_ffn_pre"], W["mod"], SH2, SC2, dx2, A["m"], W["g_mix_post"], GA1)
    dz = _mm_act(tag + "out_dx", dm, W["wout"], "NT")
    G["wout"] = _mm_wgrad(tag + "out_dw", A["z"], dm)
    dya, dyb, dp = _gate_bwd(tag + "gate_bwd", A["p"], A["ya"], A["yb"], dz)
    do = _mm_act(tag + "o_attn_dx", dyb, W["wo_attn"], "NT")
    G["wo_attn"] = _mm_wgrad(tag + "o_attn_dw", A["o_all"], dyb)
    du = _mm_act(tag + "o_rnn_dx", dya, W["wo_rnn"], "NT")
    G["wo_rnn"] = _mm_wgrad(tag + "o_rnn_dw", A["u"], dya)
    dq_all, dkc_c, dvc_c, dsink_c = _attn_bwd(tag + "attn_ctx_bwd", A["qa"], A["kc"], A["vc"], W["sink4"],
                                               A["o_all"], do, S)
    dq_all, dkc_l, dvc_l, dsink_l, dkp, dvp = _run(
        X, _attn_bwd, tag + "attn_lat_bwd", A["qa"], A["kc"], A["vc"], W["sink4"], A["o_all"], do, S,
        band=(A["kp"], A["vp"], bias), prev_dq=dq_all)
    G["sink4"] = dsink_c + dsink_l
    dp = _dqkv_assemble(tag + "dqkv", dp, dq_all, dkp, dvp, dkc_l, dvc_l, dkc_c, dvc_c, cos, sin, S)
    dp, G["cw"], G["cb"], G["w4"], G["b4"], G["lam"] = _run(
        X, _rnn_bwd, tag + "rnn_bwd", A["p"], du, A["hpf"], A["hpb"], dp, W["cw"], W["cb"], W["w4"], W["b4"], W["lam"], T)
    if X is not None and l == 0:
        G["win_t_a"] = _mm_wgrad(tag + "proj_dw_a", dp, A["h"][:, :D // 2])
        dh = _run(X, _mm_act, tag + "proj_dx", dp, W["win_t"], "NN")
        G["win_t_b"] = _run(X, _mm_wgrad, tag + "proj_dw_b", dp, A["h"][:, D // 2:])
    else:
        dh = _mm_act(tag + "proj_dx", dp, W["win_t"], "NN")
        G["win_t"] = _mm_wgrad(tag + "proj_dw", dp, A["h"])
    dxa, dsh1, dsc1, G["g_mix_pre"] = _normmod_bwd(tag + "mix_norm_bwd", dh, A["xa"], W["g_mix_pre"], W["mod"],
                                                   SH1, SC1, dx1)
    G["mod"] = jnp.concatenate([dsh1, dsc1, dga1, dsh2, dsc2, dga2], axis=1)
    return dxa, G


def _local_step(xa, target, Ws, S, X=None):
    rope = (*_rope_tables(S), _band_bias(S))
    L = len(Ws)
    h = _normmod_fwd("l0_mix_norm", xa, Ws[0]["g_mix_pre"], Ws[0]["mod"], SH1, SC1)
    saved = []
    x = xa
    for l in range(L):
        nxt = ("norm", Ws[l + 1]["g_mix_pre"], Ws[l + 1]["mod"]) if l + 1 < L else ("loss", target)
        A, out = _layer_fwd(l, x, h, Ws[l], rope, S, nxt, X)
        saved.append(A)
        if l + 1 < L:
            x, h = out
    dx, sq = out
    Gs = [None] * L
    for l in reversed(range(L)):
        dx, Gs[l] = _layer_bwd(l, dx, saved[l], Ws[l], rope, S, X)
    return sq, dx, Gs


MESH = pl.DeviceIdType.MESH


def _place():
    return lax.axis_index("x"), lax.axis_index("y"), lax.axis_index("c")


def _lin(px, py, pc):
    return 4 * px + 2 * py + pc


def _allgather_small(name, blk):
    m, n = blk.shape

    def body(x_ref, out_ref, send_sems, recv_sems, local_sem):
        x, y, c = _place()
        me, sibling = (x, y, c), (x, y, 1 - c)
        chips = [(1 - x, y), (x, 1 - y), (1 - x, 1 - y)]

        def copy(k, block, to, src=None):
            dst = out_ref.at[_lin(*block)]
            return pltpu.make_async_remote_copy(src_ref=dst if src is None else src, dst_ref=dst,
                                                send_sem=send_sems.at[k], recv_sem=recv_sems.at[k],
                                                device_id=to, device_id_type=MESH)

        mine = pltpu.make_async_copy(x_ref, out_ref.at[_lin(*me)], local_sem)
        mine.start()
        first = [copy(0, me, sibling, src=x_ref)]
        first += [copy(1 + j, me, (*chip, c), src=x_ref) for j, chip in enumerate(chips)]
        for cp in first:
            cp.start()
        passed = [copy(4 + j, (*chip, c), sibling) for j, chip in enumerate(chips)]
        for j, chip in enumerate(chips):
            copy(1 + j, (*chip, c), me).wait_recv()
            passed[j].start()
        copy(0, sibling, me).wait_recv()
        for j, chip in enumerate(chips):
            copy(4 + j, (*chip, 1 - c), me).wait_recv()
        for cp in first + passed:
            cp.wait_send()
        mine.wait()

    return pl.pallas_call(
        body, name=name, out_shape=_sds((N_DEV, m, n), blk.dtype),
        in_specs=[pl.BlockSpec(memory_space=pltpu.VMEM)], out_specs=pl.BlockSpec(memory_space=pltpu.VMEM),
        scratch_shapes=[pltpu.SemaphoreType.DMA((7,)), pltpu.SemaphoreType.DMA((7,)), pltpu.SemaphoreType.DMA],
        compiler_params=pltpu.CompilerParams(vmem_limit_bytes=VMEM_LIMIT),
    )(blk)


def _allgather_hbm(name, shards):
    na = len(shards)

    def body(*refs):
        ins, outs = refs[:na], refs[na:2 * na]
        send_sems, recv_sems, local_sems = refs[2 * na:]
        x, y, c = _place()
        me, sibling = (x, y, c), (x, y, 1 - c)
        chips = [(1 - x, y), (x, 1 - y), (1 - x, 1 - y)]

        def copy(a, k, block, to, from_input=False):
            dst = outs[a].at[_lin(*block)]
            return pltpu.make_async_remote_copy(src_ref=ins[a] if from_input else dst, dst_ref=dst,
                                                send_sem=send_sems.at[a, k], recv_sem=recv_sems.at[a, k],
                                                device_id=to, device_id_type=MESH)

        mine = [pltpu.make_async_copy(ins[a], outs[a].at[_lin(*me)], local_sems.at[a]) for a in range(na)]
        for cp in mine:
            cp.start()
        first = []
        for a in range(na):
            first.append(copy(a, 0, me, sibling, True))
            first += [copy(a, 1 + j, me, (*chip, c), True) for j, chip in enumerate(chips)]
        for cp in first:
            cp.start()
        passed = []
        for j, chip in enumerate(chips):
            for a in range(na):
                copy(a, 1 + j, (*chip, c), me).wait_recv()
                fwd = copy(a, 4 + j, (*chip, c), sibling)
                fwd.start()
                passed.append(fwd)
        for a in range(na):
            copy(a, 0, sibling, me).wait_recv()
            for j, chip in enumerate(chips):
                copy(a, 4 + j, (*chip, 1 - c), me).wait_recv()
        for cp in first + passed:
            cp.wait_send()
        for cp in mine:
            cp.wait()

    return pl.pallas_call(
        body, name=name, out_shape=[_sds((N_DEV, *s.shape), s.dtype) for s in shards],
        in_specs=[ANY] * na, out_specs=[ANY] * na,
        scratch_shapes=[pltpu.SemaphoreType.DMA((na, 7)), pltpu.SemaphoreType.DMA((na, 7)),
                        pltpu.SemaphoreType.DMA((na,))],
    )(*shards)


def _exchange_shards(name, grads, L):
    nw = len(grads)
    na = nw * L
    flat = [g for per_layer in grads for g in per_layer]

    def body(*refs):
        ins, outs = refs[:na], refs[na:na + nw]
        send_sems, recv_sems, local_sems = refs[na + nw:]
        x, y, c = _place()
        me = _lin(x, y, c)
        peers = [(x ^ ((k + 1) >> 2 & 1), y ^ ((k + 1) >> 1 & 1), c ^ ((k + 1) & 1)) for k in range(7)]

        def copy(a, k, src_blk, dst_blk):
            return pltpu.make_async_remote_copy(src_ref=ins[a].at[src_blk], dst_ref=outs[a // L].at[a % L, dst_blk],
                                                send_sem=send_sems.at[a, k], recv_sem=recv_sems.at[a, k],
                                                device_id=peers[k], device_id_type=MESH)

        mine = [pltpu.make_async_copy(ins[a].at[me], outs[a // L].at[a % L, me], local_sems.at[a]) for a in range(na)]
        for cp in mine:
            cp.start()
        sent = [copy(a, k, _lin(*peers[k]), me) for a in range(na) for k in range(7)]
        for cp in sent:
            cp.start()
        for a in range(na):
            for k in range(7):
                copy(a, k, me, _lin(*peers[k])).wait_recv()
        for cp in sent:
            cp.wait_send()
        for cp in mine:
            cp.wait()

    return pl.pallas_call(
        body, name=name, out_shape=[_sds((L, *per_layer[0].shape), per_layer[0].dtype) for per_layer in grads],
        in_specs=[ANY] * na, out_specs=[ANY] * nw,
        scratch_shapes=[pltpu.SemaphoreType.DMA((na, 7)), pltpu.SemaphoreType.DMA((na, 7)),
                        pltpu.SemaphoreType.DMA((na,))],
    )(*flat)


MOD_ROWS = 16
MOD_SHARD = 6 * D // N_DEV
HI = lax.Precision.HIGHEST


def _mod_fwd(name, c9, w_mod, b_shard):
    L = w_mod.shape[0]

    def kern(c_ref, w_ref, b_ref, o_ref):
        o_ref[...] = lax.dot_general(_silu(c_ref[...]), w_ref[...], NN, precision=HI,
                                     preferred_element_type=F32) + b_ref[...]

    return pl.pallas_call(
        kern, name=name, grid=(L,),
        in_specs=[_full_spec(c9.shape), pl.BlockSpec((None, D, MOD_SHARD), lambda l: (l, 0, 0)),
                  pl.BlockSpec((None, 1, MOD_SHARD), lambda l: (l, 0, 0))],
        out_specs=pl.BlockSpec((None, MOD_ROWS, MOD_SHARD), lambda l: (l, 0, 0)),
        out_shape=_sds((L, MOD_ROWS, MOD_SHARD), F32), compiler_params=_params(),
    )(c9, w_mod, b_shard)


def _mod_bwd(name, c9, w_mod, dmod_all, dmod_cols):
    L = w_mod.shape[0]

    def rows9(ref, l):
        own = jnp.concatenate([ref[j, 2 * l + 1:2 * l + 2, :] for j in range(N_DEV)], axis=0)
        ctx = ref[0, 2 * l:2 * l + 1, :]
        for j in range(1, N_DEV):
            ctx = ctx + ref[j, 2 * l:2 * l + 1, :]
        return own, ctx

    def kern(c_ref, w_ref, all_ref, cols_ref, gw_ref, gb_ref, gc_ref):
        l = pl.program_id(0)
        for ll in range(L):
            @pl.when(l == ll)
            def _():
                own, ctx = rows9(all_ref, ll)
                gb_ref[...] = _colsum(own) + ctx
                own_s, ctx_s = rows9(cols_ref, ll)
                r16 = jnp.concatenate([own_s, ctx_s, jnp.zeros((MOD_ROWS - N_DEV - 1, MOD_SHARD), F32)], axis=0)
                gw_ref[...] = lax.dot_general(_silu(c_ref[...]), r16, TN, precision=HI, preferred_element_type=F32)
                part = lax.dot_general(r16, w_ref[...], NT, precision=HI,
                                       preferred_element_type=F32)[N_DEV:N_DEV + 1, :]
                if ll == 0:
                    gc_ref[...] = part
                else:
                    gc_ref[...] += part

    return pl.pallas_call(
        kern, name=name, grid=(L,),
        in_specs=[_full_spec(c9.shape), pl.BlockSpec((None, D, MOD_SHARD), lambda l: (l, 0, 0)),
                  _full_spec(dmod_all.shape), _full_spec(dmod_cols.shape)],
        out_specs=[pl.BlockSpec((None, D, MOD_SHARD), lambda l: (l, 0, 0)),
                   pl.BlockSpec((None, 1, 6 * D), lambda l: (l, 0, 0)), _full_spec((1, D))],
        out_shape=[_sds((L, D, MOD_SHARD), F32), _sds((L, 1, 6 * D), F32), _sds((1, D), F32)],
        compiler_params=_params(),
    )(c9, w_mod, dmod_all, dmod_cols)


_BC1 = 1.0 - ADAM_B1 ** ADAM_STEP
_BC2 = 1.0 - ADAM_B2 ** ADAM_STEP


def _adamw_vals(w, g, m, v):
    m = ADAM_B1 * m + (1.0 - ADAM_B1) * g
    v = ADAM_B2 * v + (1.0 - ADAM_B2) * (g * g)
    delta = -ADAM_LR * ((m / _BC1) / (jnp.sqrt(v / _BC2) + ADAM_EPS) + ADAM_WD * w)
    return delta, m, v


def _adamw(name, w, g, m, v, tile):
    R, C = w.shape
    blk = ((tile, C), lambda i: (i, 0))

    def body(i, ins, ps, outs, acc):
        d, mm, vv = _adamw_vals(ins[0][...], ins[1][...], ins[2][...], ins[3][...])
        outs[0][...] = d
        outs[1][...] = mm
        outs[2][...] = vv

    return _ew(name, body, R // tile, [(a, *blk) for a in (w, g, m, v)], [], [(_sds((R, C), F32), *blk)] * 3)


def _sum_slots(ref):
    g = ref[0].astype(F32)
    for j in range(1, N_DEV):
        g = g + ref[j].astype(F32)
    return g


def _adamw_slots(name, slots, w, m, v, tile):
    L, R, C = w.shape
    n = R // tile
    spec = pl.BlockSpec((None, tile, C), lambda l, i: (l, i, 0))
    pieces = [s if isinstance(s, (list, tuple)) else [s] for s in slots]
    layer_of = [ll for ll, ps in enumerate(pieces) for _ in ps]
    flat = [p for ps in pieces for p in ps]

    def slot_spec(ll, cols):
        return pl.BlockSpec((N_DEV, tile, cols),
                            lambda l, i: (0, jnp.where(l == ll, i, jnp.where(l < ll, 0, n - 1)), 0))

    def kern(*refs):
        s_refs = refs[:len(flat)]
        w_ref, m_ref, v_ref, g_ref, d_ref, mo_ref, vo_ref = refs[len(flat):]
        l = pl.program_id(0)
        for ll in range(L):
            @pl.when(l == ll)
            def _():
                parts = [_sum_slots(r) for r, lr in zip(s_refs, layer_of) if lr == ll]
                g = parts[0] if len(parts) == 1 else jnp.concatenate(parts, axis=1)
                g_ref[...] = g
                d_ref[...], mo_ref[...], vo_ref[...] = _adamw_vals(w_ref[...], g, m_ref[...], v_ref[...])

    return pl.pallas_call(
        kern, name=name, grid=(L, n),
        in_specs=[slot_spec(ll, p.shape[-1]) for ll, p in zip(layer_of, flat)] + [spec, spec, spec],
        out_specs=[spec] * 4, out_shape=[_sds((L, R, C), F32)] * 4,
        compiler_params=_params(("arbitrary", "arbitrary")),
    )(*flat, w, m, v)


def _sum_blocks(name, blocks):
    _, R, C = blocks.shape

    def kern(b_ref, o_ref):
        o_ref[...] = _sum_slots(b_ref)

    return pl.pallas_call(kern, name=name, in_specs=[_full_spec(blocks.shape)], out_specs=_full_spec((R, C)),
                          grid=(1,), out_shape=_sds((R, C), F32), compiler_params=_params())(blocks)


BIG = ("win_t", "wo_rnn", "wo_attn", "wout", "wffn_in_t", "wffn_out")
BIG_SRC = ("w_in", "w_o_rnn", "w_o_attn", "w_out", "w_ffn_in", "w_ffn_out")
BIG_T = (True, False, False, False, True, False)
BIG_TILE = (176, 128, 128, 128, 176, 176)


def _chan_full(g8):
    return jnp.transpose(g8, (1, 0, 2)).reshape(g8.shape[1], D)


def kernel(x, c, ctx, c_ctx, w_mod, b_mod, g_mix_pre, g_mix_post, g_ffn_pre, g_ffn_post, w_in, conv_w, conv_b, lru_wa, lru_ba, lru_wx, lru_bx, lru_lam, attn_sink, w_o_rnn, w_o_attn, w_out, w_ffn_in, w_ffn_out, loss_target, m_c_ctx, m_w_mod, m_b_mod, m_g_mix_pre, m_g_mix_post, m_g_ffn_pre, m_g_ffn_post, m_w_in, m_conv_w, m_conv_b, m_lru_wa, m_lru_ba, m_lru_wx, m_lru_bx, m_lru_lam, m_attn_sink, m_w_o_rnn, m_w_o_attn, m_w_out, m_w_ffn_in, m_w_ffn_out, v_c_ctx, v_w_mod, v_b_mod, v_g_mix_pre, v_g_mix_post, v_g_ffn_pre, v_g_ffn_post, v_w_in, v_conv_w, v_conv_b, v_lru_wa, v_lru_ba, v_lru_wx, v_lru_bx, v_lru_lam, v_attn_sink, v_w_o_rnn, v_w_o_attn, v_w_out, v_w_ffn_in, v_w_ffn_out):
    P = dict(c_ctx=c_ctx, w_mod=w_mod, b_mod=b_mod, g_mix_pre=g_mix_pre, g_mix_post=g_mix_post, g_ffn_pre=g_ffn_pre,
             g_ffn_post=g_ffn_post, w_in=w_in, conv_w=conv_w, conv_b=conv_b, lru_wa=lru_wa, lru_ba=lru_ba,
             lru_wx=lru_wx, lru_bx=lru_bx, lru_lam=lru_lam, attn_sink=attn_sink, w_o_rnn=w_o_rnn, w_o_attn=w_o_attn,
             w_out=w_out, w_ffn_in=w_ffn_in, w_ffn_out=w_ffn_out)
    Mo = dict(c_ctx=m_c_ctx, w_mod=m_w_mod, b_mod=m_b_mod, g_mix_pre=m_g_mix_pre, g_mix_post=m_g_mix_post,
              g_ffn_pre=m_g_ffn_pre, g_ffn_post=m_g_ffn_post, w_in=m_w_in, conv_w=m_conv_w, conv_b=m_conv_b,
              lru_wa=m_lru_wa, lru_ba=m_lru_ba, lru_wx=m_lru_wx, lru_bx=m_lru_bx, lru_lam=m_lru_lam,
              attn_sink=m_attn_sink, w_o_rnn=m_w_o_rnn, w_o_attn=m_w_o_attn, w_out=m_w_out, w_ffn_in=m_w_ffn_in,
              w_ffn_out=m_w_ffn_out)
    Vo = dict(c_ctx=v_c_ctx, w_mod=v_w_mod, b_mod=v_b_mod, g_mix_pre=v_g_mix_pre, g_mix_post=v_g_mix_post,
              g_ffn_pre=v_g_ffn_pre, g_ffn_post=v_g_ffn_post, w_in=v_w_in, conv_w=v_conv_w, conv_b=v_conv_b,
              lru_wa=v_lru_wa, lru_ba=v_lru_ba, lru_wx=v_lru_wx, lru_bx=v_lru_bx, lru_lam=v_lru_lam,
              attn_sink=v_attn_sink, w_o_rnn=v_w_o_rnn, w_o_attn=v_w_o_attn, w_out=v_w_out, w_ffn_in=v_w_ffn_in,
              w_ffn_out=v_w_ffn_out)
    L = w_in.shape[0]
    S = x.shape[1]
    me = _lin(*_place())

    small = jnp.concatenate([c.reshape(8, 128), conv_w.reshape(L * CONV_W, 128), lru_ba.reshape(2 * L, 128),
                             lru_bx.reshape(2 * L, 128), lru_lam.reshape(2 * L, 128), jnp.zeros((4, 128), F32)], axis=0)
    small_all = _allgather_small("ag_small", small)
    c_all = small_all[:, 0:8].reshape(N_DEV, D)
    conv_w_f = _chan_full(small_all[:, 8:16]).reshape(L, CONV_W, D)
    lru_ba_f = _chan_full(small_all[:, 16:20]).reshape(L, 2, D)
    lru_bx_f = _chan_full(small_all[:, 20:24]).reshape(L, 2, D)
    lru_lam_f = _chan_full(small_all[:, 24:28]).reshape(L, 2, D)

    c9 = jnp.concatenate([c_all, c_ctx[None], jnp.zeros((MOD_ROWS - N_DEV - 1, D), F32)], axis=0)
    b_shard = lax.dynamic_slice_in_dim(b_mod, me * MOD_SHARD, MOD_SHARD, axis=1)[:, None, :]
    mod_part = _mod_fwd("mod_fwd", c9, w_mod, b_shard)
    mod_all = _allgather_small("ag_mod", mod_part.reshape(L * MOD_ROWS, MOD_SHARD))
    mod_all = jnp.transpose(mod_all.reshape(N_DEV, L, MOD_ROWS, MOD_SHARD), (1, 2, 0, 3)).reshape(L, MOD_ROWS, 6 * D)
    own_row = lax.dynamic_index_in_dim(mod_all, me, axis=1, keepdims=False)
    modrows = jnp.stack([mod_all[:, N_DEV], own_row], axis=1)

    shards = [{k: (P[src][l].T if tr else P[src][l]).astype(BF16) for k, src, tr in zip(BIG, BIG_SRC, BIG_T)}
              for l in range(L)]
    win0, = _allgather_hbm("ag_w_in0", [shards[0]["win_t"]])
    Ws = []
    for l in range(L):
        W = {"win_t": win0.reshape(-1, D)} if l == 0 else {}
        W.update(
            cw=conv_w_f[l], cb=conv_b[l][None],
            w4=jnp.concatenate([lru_wa[l, 0], lru_wa[l, 1], lru_wx[l, 0], lru_wx[l, 1]], axis=-1).astype(BF16),
            b4=jnp.concatenate([lru_ba_f[l, 0].reshape(N_RNN_BLOCKS, 1, RB), lru_ba_f[l, 1].reshape(N_RNN_BLOCKS, 1, RB),
                                lru_bx_f[l, 0].reshape(N_RNN_BLOCKS, 1, RB), lru_bx_f[l, 1].reshape(N_RNN_BLOCKS, 1, RB)],
                               axis=-1),
            lam=lru_lam_f[l], sink4=jnp.broadcast_to(attn_sink[l].reshape(N_KV, Q_PER_KV, 1), (N_KV, Q_PER_KV, HEAD)),
            g_mix_pre=g_mix_pre[l][None], g_mix_post=g_mix_post[l][None], g_ffn_pre=g_ffn_pre[l][None],
            g_ffn_post=g_ffn_post[l][None], mod=modrows[l])
        Ws.append(W)

    xa = jnp.concatenate([ctx[0], x[0]], axis=0)
    plan = _Plan(shards, Ws)
    sq, dxa, Gs = _local_step(xa, loss_target[0], Ws, S, plan)
    loss = lax.psum((0.5 / D) * jnp.sum(sq), ("x", "y", "c"))
    grad_x = dxa[CTX:][None]

    dmod = jnp.concatenate([Gs[l]["mod"] for l in range(L)] + [jnp.zeros((8 - 2 * L, 6 * D), F32)], axis=0)
    dmod_all = _allgather_small("ag_dmod", dmod)
    dmod_cols = lax.dynamic_slice_in_dim(dmod_all, me * MOD_SHARD, MOD_SHARD, axis=2)
    g_w_mod, g_b_mod, dsc_part = _mod_bwd("mod_bwd", c9, w_mod, dmod_all, dmod_cols)
    g_b_mod = g_b_mod[:, 0]

    def rows(name, shape):
        return jnp.concatenate([Gs[l][name].reshape(shape) for l in range(L)], axis=0)

    b4g = [Gs[l]["b4"].reshape(N_RNN_BLOCKS, 4, RB) for l in range(L)]
    sink_row = jnp.concatenate([Gs[l]["sink4"][:, :, 0].reshape(1, N_Q) for l in range(L)]
                               + [jnp.zeros((1, D - L * N_Q), F32)], axis=1)
    small_g = jnp.concatenate(
        [rows("g_mix_pre", (1, D)), rows("g_mix_post", (1, D)), rows("g_ffn_pre", (1, D)), rows("g_ffn_post", (1, D)),
         rows("cb", (1, D)), rows("cw", (CONV_W, D))]
        + [b4g[l][:, d].reshape(1, D) for l in range(L) for d in range(2)]
        + [b4g[l][:, 2 + d].reshape(1, D) for l in range(L) for d in range(2)]
        + [rows("lam", (2, D)), sink_row, dsc_part], axis=0)
    n_small = small_g.shape[0]
    small_tot = _sum_blocks("sum_small", _allgather_small("ag_small_grads", small_g))
    o = 0
    G = {}
    for name in ("g_mix_pre", "g_mix_post", "g_ffn_pre", "g_ffn_post", "conv_b"):
        G[name] = small_tot[o:o + L]
        o += L
    G["conv_w"] = small_tot[o:o + L * CONV_W].reshape(L, CONV_W, D)
    o += L * CONV_W
    for name in ("lru_ba", "lru_bx", "lru_lam"):
        G[name] = small_tot[o:o + 2 * L].reshape(L, 2, D)
        o += 2 * L
    G["attn_sink"] = small_tot[o, :L * N_Q].reshape(L, N_Q)
    sg = jax.nn.sigmoid(c_ctx)
    G["c_ctx"] = small_tot[o + 1] * (sg * (1.0 + c_ctx * (1.0 - sg)))
    G["b_mod"] = g_b_mod
    G["w_mod"] = g_w_mod

    last_slots, = _exchange_shards("exchange_w_in0", [[Gs[0]["win_t_b"].reshape(N_DEV, -1, D // 2)]], 1)
    plan.slots[0]["win_t"] = [plan.slots[0]["win_t_a"], last_slots[0]]

    out_g, out_d, out_m, out_v = {}, {}, {}, {}

    def put(name, res, shape=None):
        g, d, m, v = res
        for dst, val in ((out_g, g), (out_d, d), (out_m, m), (out_v, v)):
            dst[name] = val if shape is None else val.reshape(shape)

    for k, src, tr, tile in zip(BIG, BIG_SRC, BIG_T, BIG_TILE):
        lay = (lambda a: jnp.swapaxes(a, 1, 2)) if tr else (lambda a: a)
        res = _adamw_slots("adamw_" + src, [plan.slots[l][k] for l in range(L)], lay(P[src]), lay(Mo[src]),
                           lay(Vo[src]), tile)
        put(src, [lay(r) for r in res])
    res = _adamw("adamw_w_mod", w_mod.reshape(L * D, MOD_SHARD), g_w_mod.reshape(L * D, MOD_SHARD),
                 m_w_mod.reshape(L * D, MOD_SHARD), v_w_mod.reshape(L * D, MOD_SHARD), 256)
    put("w_mod", (g_w_mod,) + tuple(res), w_mod.shape)
    def fuse4(wa, wx):
        return jnp.concatenate([wa[:, 0], wa[:, 1], wx[:, 0], wx[:, 1]], axis=-1).reshape(L, N_RNN_BLOCKS * RB, 4 * RB)

    res = _adamw_slots("adamw_gates", plan.gate_slots,
                       fuse4(lru_wa, lru_wx), fuse4(m_lru_wa, m_lru_wx), fuse4(v_lru_wa, v_lru_wx), 256)
    res = [r.reshape(L, N_RNN_BLOCKS, RB, 4, RB) for r in res]
    put("lru_wa", [jnp.stack([r[:, :, :, 0], r[:, :, :, 1]], axis=1) for r in res])
    put("lru_wx", [jnp.stack([r[:, :, :, 2], r[:, :, :, 3]], axis=1) for r in res])
    rep = ("g_mix_pre", "g_mix_post", "g_ffn_pre", "g_ffn_post", "conv_b", "b_mod")

    def pack_rep(T_):
        sink = jnp.concatenate([T_["attn_sink"].reshape(1, L * N_Q), jnp.zeros((1, D - L * N_Q), F32)], axis=1)
        return jnp.concatenate([T_[n].reshape(-1, D) for n in rep] + [sink, T_["c_ctx"][None]], axis=0)

    pk = [pack_rep(T_) for T_ in (P, G, Mo, Vo)]
    n_rep = pk[0].shape[0]
    res = _adamw("adamw_replicated", *[jnp.pad(a, ((0, 24 - n_rep), (0, 0))) for a in pk], 24)
    res = (pk[1],) + tuple(r[:n_rep] for r in res)
    o = 0
    for n in rep:
        k = P[n].size // D
        put(n, [r[o:o + k] for r in res], P[n].shape)
        o += k
    put("attn_sink", [r[o, :L * N_Q] for r in res], attn_sink.shape)
    put("c_ctx", [r[o + 1] for r in res], c_ctx.shape)
    chan = ("conv_w", "lru_ba", "lru_bx", "lru_lam")
    g_own = {n: lax.dynamic_slice_in_dim(G[n], me * RB, RB, axis=2) for n in chan}

    def pack_chan(T_):
        return jnp.concatenate([T_[n].reshape(-1, RB) for n in chan], axis=0)

    pk = [pack_chan(T_) for T_ in (P, g_own, Mo, Vo)]
    n_ch = pk[0].shape[0]
    res = _adamw("adamw_channels", *[jnp.pad(a, ((0, 24 - n_ch), (0, 0))) for a in pk], 24)
    res = (pk[1],) + tuple(r[:n_ch] for r in res)
    o = 0
    for n in chan:
        k = P[n].size // RB
        put(n, [r[o:o + k] for r in res], P[n].shape)
        o += k

    order = ("c_ctx", "w_mod", "b_mod", "g_mix_pre", "g_mix_post", "g_ffn_pre", "g_ffn_post", "w_in", "conv_w", "conv_b",
             "lru_wa", "lru_ba", "lru_wx", "lru_bx", "lru_lam", "attn_sink", "w_o_rnn", "w_o_attn", "w_out", "w_ffn_in",
             "w_ffn_out")
    return (loss, grad_x, *[out_g[n] for n in order], *[out_d[n] for n in order], *[out_m[n] for n in order],
            *[out_v[n] for n in order])
```

```python
import functools
import math

import numpy as np
import jax
import jax.numpy as jnp
from jax import lax
from jax.experimental import pallas as pl
from jax.experimental.pallas import tpu as pltpu

F32 = jnp.float32
BF16 = jnp.bfloat16

D = 1024
CTX = 256
TR = 256
HEAD = 128
N_Q = 8
N_KV = 2
Q_PER_KV = N_Q // N_KV
GRID_W = 64
N_FREQ = HEAD // 4
ROPE_BASE = 10000.0
N_RNN_BLOCKS = 8
CONV_W = 4
CONV_LEFT = 2
LRU_C = 8.0
D_FF = 2816
IN_W = 5632
P_W = IN_W
COL_XR, COL_GR, COL_Q, COL_K, COL_V, COL_GL = 0, 1024, 2048, 3072, 3328, 3584
GLB = 512
EPS = 1e-6
NEG_INF = -1e30
ATT_SCALE = HEAD ** -0.5
N_DEV = 8
VMEM_LIMIT = 56 * 1024 * 1024

ADAM_LR, ADAM_B1, ADAM_B2, ADAM_EPS, ADAM_WD, ADAM_STEP = 0.001, 0.9, 0.999, 1e-08, 0.01, 10

NN = (((1,), (0,)), ((), ()))
NT = (((1,), (1,)), ((), ()))
TN = (((0,), (0,)), ((), ()))


def _dot(a, b, dims=NN):
    return lax.dot_general(a, b, dims, preferred_element_type=F32)


def _params(sem=("arbitrary",)):
    return pltpu.CompilerParams(dimension_semantics=sem, vmem_limit_bytes=VMEM_LIMIT)


def _full_spec(shape):
    nd = len(shape)
    return pl.BlockSpec(shape, lambda *_: (0,) * nd)


ANY = pl.BlockSpec(memory_space=pl.ANY)


def _ew(name, body, n, row_ins, pars, row_outs, accs=(), alias=None):
    n_ri, n_p, n_ro, n_acc = len(row_ins), len(pars), len(row_outs), len(accs)

    def kern(*refs):
        i = pl.program_id(0)
        ins = refs[:n_ri]
        ps = refs[n_ri:n_ri + n_p]
        outs = refs[n_ri + n_p:n_ri + n_p + n_ro]
        acc = refs[n_ri + n_p + n_ro:]
        if n_acc:
            @pl.when(i == 0)
            def _():
                for a in acc:
                    a[...] = jnp.zeros(a.shape, a.dtype)
        body(i, ins, ps, outs, acc)

    in_specs = [ANY if blk is None else pl.BlockSpec(blk, imap) for (_, blk, imap) in row_ins]
    in_specs += [_full_spec(p.shape) for p in pars]
    out_specs = [pl.BlockSpec(blk, imap) for (_, blk, imap) in row_outs] + [_full_spec(a.shape) for a in accs]
    out_shape = [s for (s, _, _) in row_outs] + list(accs)
    return pl.pallas_call(
        kern, name=name, grid=(n,), in_specs=in_specs, out_specs=out_specs, out_shape=out_shape,
        input_output_aliases=alias or {}, compiler_params=_params(),
    )(*[a for (a, _, _) in row_ins], *pars)


def _rowblk(width, colblk=0, roff=0, tile=TR):
    return (tile, width), (lambda i: (i + roff, colblk))


def _sds(shape, dtype):
    return jax.ShapeDtypeStruct(shape, dtype)


class _Carry:
    SAME_CORE = (1, 3, 5)

    def __init__(self, jobs):
        self.jobs = list(jobs)
        self.arrays = [a for _, a in self.jobs]
        self.out_shapes = [_sds(a.shape if kind == "scatter" else (N_DEV, *a.shape), a.dtype) for kind, a in self.jobs]
        n = len(self.jobs)
        self.scratch = [pltpu.SemaphoreType.DMA((n, 7)), pltpu.SemaphoreType.DMA((n, 7)), pltpu.SemaphoreType.DMA((n,))]

    def _setup(self, sems):
        send_sems, recv_sems, local_sems = sems
        x, y, c = _place()
        me = _lin(x, y, c)
        peers = [(x ^ ((k + 1) >> 2 & 1), y ^ ((k + 1) >> 1 & 1), c ^ ((k + 1) & 1)) for k in range(7)]

        def copy(a, k, sem_k, src, dst):
            return pltpu.make_async_remote_copy(src_ref=src, dst_ref=dst, send_sem=send_sems.at[a, sem_k],
                                                recv_sem=recv_sems.at[a, sem_k], device_id=peers[k], device_id_type=MESH)

        return me, [_lin(*p) for p in peers], copy, local_sems

    def _local(self, a, kind, ins, outs, me, local_sems):
        return pltpu.make_async_copy(ins[a].at[me] if kind == "scatter" else ins[a], outs[a].at[me], local_sems.at[a])

    def start(self, ins, outs, sems):
        me, theirs, copy, local_sems = self._setup(sems)
        for a, (kind, _) in enumerate(self.jobs):
            self._local(a, kind, ins, outs, me, local_sems).start()
            if kind == "scatter":
                for k in range(7):
                    copy(a, k, k, ins[a].at[theirs[k]], outs[a].at[me]).start()
            else:
                for k in (0,) + self.SAME_CORE:
                    copy(a, k, k, ins[a], outs[a].at[me]).start()

    def wait(self, ins, outs, sems):
        me, theirs, copy, local_sems = self._setup(sems)
        for a, (kind, _) in enumerate(self.jobs):
            if kind == "scatter":
                for k in range(7):
                    copy(a, k, k, ins[a].at[me], outs[a].at[theirs[k]]).wait_recv()
                for k in range(7):
                    copy(a, k, k, ins[a].at[theirs[k]], outs[a].at[me]).wait_send()
            else:
                for k in self.SAME_CORE:
                    blk = outs[a].at[theirs[k]]
                    copy(a, k, k, ins[a], blk).wait_recv()
                    copy(a, 0, k + 1, blk, blk).start()
                copy(a, 0, 0, ins[a], outs[a].at[theirs[0]]).wait_recv()
                for k in self.SAME_CORE:
                    copy(a, 0, k + 1, ins[a], outs[a].at[theirs[k + 1]]).wait_recv()
                for k in (0,) + self.SAME_CORE:
                    copy(a, k, k, ins[a], outs[a].at[me]).wait_send()
                for k in self.SAME_CORE:
                    blk = outs[a].at[theirs[k]]
                    copy(a, 0, k + 1, blk, blk).wait_send()
            self._local(a, kind, ins, outs, me, local_sems).wait()


def _carried(kern, carry, n_in, n_out, first, last):
    if carry is None:
        return kern
    nc = len(carry.jobs)

    def wrapped(*refs):
        ins, cin = refs[:n_in], refs[n_in:n_in + nc]
        outs, cout = refs[n_in + nc:n_in + nc + n_out], refs[n_in + nc + n_out:n_in + 2 * nc + n_out]
        scr, sems = refs[n_in + 2 * nc + n_out:-3], refs[-3:]

        @pl.when(first())
        def _():
            carry.start(cin, cout, sems)

        kern(*ins, *outs, *scr)

        @pl.when(last())
        def _():
            carry.wait(cin, cout, sems)

    return wrapped


def _carry_args(carry):
    if carry is None:
        return [], [], [], [], []
    n = len(carry.jobs)
    return [ANY] * n, carry.arrays, [ANY] * n, carry.out_shapes, carry.scratch


def _grid_ends(dims):
    first = lambda: functools.reduce(jnp.logical_and, [pl.program_id(d) == 0 for d in range(len(dims))])
    last = lambda: functools.reduce(jnp.logical_and, [pl.program_id(d) == n - 1 for d, n in enumerate(dims)])
    return first, last


def _mm_call(name, a, b, mode, out_dtype, tm, tn, rows_outer=True, single_b=False, carry=None):
    if mode == "TN":
        (K, M), N = a.shape, b.shape[1]
    else:
        (M, K), N = a.shape, (b.shape[1] if mode == "NN" else b.shape[0])
    assert M % tm == 0 and N % tn == 0, (name, M, N, K, tm, tn)
    ij = (lambda g0, g1: (g0, g1)) if rows_outer else (lambda g0, g1: (g1, g0))
    grid = (M // tm, N // tn) if rows_outer else (N // tn, M // tm)
    if mode == "TN":
        a_spec = pl.BlockSpec((K, tm), lambda g0, g1: (0, ij(g0, g1)[0]))
    else:
        a_spec = pl.BlockSpec((tm, K), lambda g0, g1: (ij(g0, g1)[0], 0))
    b_blk, b_map = ((tn, K), lambda g0, g1: (ij(g0, g1)[1], 0)) if mode == "NT" else \
                   ((K, tn), lambda g0, g1: (0, ij(g0, g1)[1]))
    b_spec = pl.BlockSpec(b_blk, b_map, pipeline_mode=pl.Buffered(1)) if single_b else pl.BlockSpec(b_blk, b_map)
    dims = {"NN": NN, "NT": NT, "TN": TN}[mode]

    def kern(a_ref, b_ref, o_ref):
        o_ref[...] = _dot(a_ref[...], b_ref[...], dims).astype(o_ref.dtype)

    ci, ca, co, cs, cscr = _carry_args(carry)
    res = pl.pallas_call(
        _carried(kern, carry, 2, 1, *_grid_ends(grid)), name=name, grid=grid, in_specs=[a_spec, b_spec] + ci,
        out_specs=[pl.BlockSpec((tm, tn), lambda g0, g1: ij(g0, g1))] + co,
        out_shape=[_sds((M, N), out_dtype)] + cs, scratch_shapes=cscr,
        compiler_params=_params(("arbitrary", "arbitrary")),
    )(a, b, *ca)
    return res[0] if carry is None else (res[0], res[1:])


def _mm_act(name, a, w, mode, out_dtype=BF16, carry=None):
    rows, K = a.shape
    N = w.shape[1] if mode == "NN" else w.shape[0]
    if K > D_FF:
        return _mm_call(name, a, w, mode, out_dtype, rows // 8, N, single_b=True, carry=carry)
    tn = N if N <= 1024 else 1408
    return _mm_call(name, a, w, mode, out_dtype, rows // 4, tn, carry=carry)


def _mm_wgrad(name, x, dy, out_dtype=BF16, carry=None):
    M = x.shape[1]
    tm = 1408 if M == D_FF else 512
    return _mm_call(name, x, dy, "TN", out_dtype, tm, dy.shape[1], single_b=True, carry=carry)


def _sigmoid(x):
    return 0.5 * jnp.tanh(0.5 * x) + 0.5


def _silu(x):
    return x * _sigmoid(x)


def _silu_grad(x):
    s = _sigmoid(x)
    return s * (1.0 + x * (1.0 - s))


_GELU_K = math.sqrt(2.0 / math.pi)


def _gelu(x):
    return 0.5 * x * (1.0 + jnp.tanh(_GELU_K * (x + 0.044715 * x * x * x)))


def _gelu_grad(x):
    t = jnp.tanh(_GELU_K * (x + 0.044715 * x * x * x))
    return 0.5 * (1.0 + t) + 0.5 * x * (1.0 - t * t) * _GELU_K * (1.0 + 3.0 * 0.044715 * x * x)


def _log_sigmoid(x):
    return jnp.minimum(x, 0.0) - jnp.log(1.0 + jnp.exp(-jnp.abs(x)))


def _rms(x):
    x = x.astype(F32)
    r = lax.rsqrt(jnp.mean(x * x, axis=-1, keepdims=True) + EPS)
    return x * r, r


def _rms_bwd(dy, y, r):
    return r * (dy - y * jnp.mean(dy * y, axis=-1, keepdims=True))


def _modrow(mod_ref, i, chunk):
    lo = mod_ref[0:1, chunk * D:(chunk + 1) * D]
    hi = mod_ref[1:2, chunk * D:(chunk + 1) * D]
    return jnp.where(i == 0, lo, hi)


def _acc_seg(acc_ref, i, val):
    zero = jnp.zeros_like(val)
    acc_ref[0:1, :] += jnp.where(i == 0, val, zero)
    acc_ref[1:2, :] += jnp.where(i == 0, zero, val)


def _colsum(x):
    return jnp.sum(x, axis=0, keepdims=True)


SH1, SC1, GA1, SH2, SC2, GA2 = range(6)


def _normmod_fwd(name, xa, g, mod, c_sh, c_sc):
    T = xa.shape[0]

    def body(i, ins, ps, outs, acc):
        y, _ = _rms(ins[0][...])
        h = (y * ps[0][...]) * (1.0 + _modrow(ps[1], i, c_sc)) + _modrow(ps[1], i, c_sh)
        outs[0][...] = h.astype(BF16)

    return _ew(name, body, T // TR, [(xa, *_rowblk(D))], [g, mod], [(_sds((T, D), BF16), *_rowblk(D))])[0]


def _resid_norm_fwd(name, xin, mat, gpost, mod, c_ga, gnext, modn, c_sh, c_sc):
    T = xin.shape[0]

    def body(i, ins, ps, outs, acc):
        ym, _ = _rms(ins[1][...])
        xo = ins[0][...] + _modrow(ps[1], i, c_ga) * (ym * ps[0][...])
        outs[0][...] = xo
        y, _ = _rms(xo)
        h = (y * ps[2][...]) * (1.0 + _modrow(ps[3], i, c_sc)) + _modrow(ps[3], i, c_sh)
        outs[1][...] = h.astype(BF16)

    return _ew(name, body, T // TR, [(xin, *_rowblk(D)), (mat, *_rowblk(D))], [gpost, mod, gnext, modn],
               [(_sds((T, D), F32), *_rowblk(D)), (_sds((T, D), BF16), *_rowblk(D))])


def _resid_loss_fwd(name, xin, mat, gpost, mod, c_ga, target):
    T = xin.shape[0]

    def body(i, ins, ps, outs, acc):
        ym, _ = _rms(ins[1][...])
        xo = ins[0][...] + _modrow(ps[1], i, c_ga) * (ym * ps[0][...])
        err = xo - ins[2][...]
        lat = i > 0
        outs[0][...] = jnp.where(lat, err * (1.0 / D), 0.0)
        acc[0][...] += jnp.where(lat, _colsum(err * err), 0.0)

    tgt_blk = ((TR, D), lambda i: (jnp.maximum(i - 1, 0), 0))
    dx, sq = _ew(name, body, T // TR, [(xin, *_rowblk(D)), (mat, *_rowblk(D)), (target, *tgt_blk)], [gpost, mod],
                 [(_sds((T, D), F32), *_rowblk(D))], [_sds((1, D), F32)])
    return dx, sq


def _resid_bwd_vals(i, dout, mat, gpost, mod_ref, c_ga, acc_ga, acc_g):
    ym, rm = _rms(mat)
    ga = _modrow(mod_ref, i, c_ga)
    _acc_seg(acc_ga, i, _colsum(dout * (ym * gpost)))
    dn = dout * ga
    acc_g[...] += _colsum(dn * ym)
    return _rms_bwd(dn * gpost, ym, rm)


def _normmod_bwd_vals(i, dh, xin, g, mod_ref, c_sh, c_sc, acc_sh, acc_sc, acc_g):
    dh = dh.astype(F32)
    y, r = _rms(xin)
    _acc_seg(acc_sc, i, _colsum(dh * (y * g)))
    _acc_seg(acc_sh, i, _colsum(dh))
    dyg = dh * (1.0 + _modrow(mod_ref, i, c_sc))
    acc_g[...] += _colsum(dyg * y)
    return _rms_bwd(dyg * g, y, r)


def _resid_bwd(name, dout, mat, gpost, mod, c_ga):
    T = dout.shape[0]

    def body(i, ins, ps, outs, acc):
        dm = _resid_bwd_vals(i, ins[0][...], ins[1][...], ps[0][...], ps[1], c_ga, acc[0], acc[1])
        outs[0][...] = dm.astype(BF16)

    return _ew(name, body, T // TR, [(dout, *_rowblk(D)), (mat, *_rowblk(D))], [gpost, mod],
               [(_sds((T, D), BF16), *_rowblk(D))], [_sds((2, D), F32), _sds((1, D), F32)])


def _normmod_resid_bwd(name, dh, xin, gpre, mod, c_sh, c_sc, dres, mat, gpost, c_ga):
    T = dh.shape[0]

    def body(i, ins, ps, outs, acc):
        dx = ins[2][...] + _normmod_bwd_vals(i, ins[0][...], ins[1][...], ps[0][...], ps[1], c_sh, c_sc,
                                             acc[0], acc[1], acc[2])
        outs[0][...] = dx
        dm = _resid_bwd_vals(i, dx, ins[3][...], ps[2][...], ps[1], c_ga, acc[3], acc[4])
        outs[1][...] = dm.astype(BF16)

    return _ew(name, body, T // TR, [(dh, *_rowblk(D)), (xin, *_rowblk(D)), (dres, *_rowblk(D)), (mat, *_rowblk(D))],
               [gpre, mod, gpost],
               [(_sds((T, D), F32), *_rowblk(D)), (_sds((T, D), BF16), *_rowblk(D))],
               [_sds((2, D), F32), _sds((2, D), F32), _sds((1, D), F32), _sds((2, D), F32), _sds((1, D), F32)])


def _normmod_bwd(name, dh, xin, gpre, mod, c_sh, c_sc, dres):
    T = dh.shape[0]

    def body(i, ins, ps, outs, acc):
        outs[0][...] = ins[2][...] + _normmod_bwd_vals(i, ins[0][...], ins[1][...], ps[0][...], ps[1], c_sh, c_sc,
                                                       acc[0], acc[1], acc[2])

    return _ew(name, body, T // TR, [(dh, *_rowblk(D)), (xin, *_rowblk(D)), (dres, *_rowblk(D))], [gpre, mod],
               [(_sds((T, D), F32), *_rowblk(D))], [_sds((2, D), F32), _sds((2, D), F32), _sds((1, D), F32)])


def _gate_fwd(name, p, ya, yb):
    T = ya.shape[0]

    def body(i, ins, ps, outs, acc):
        gl = [r[...].astype(F32) for r in ins[:4]]
        ga = _sigmoid(jnp.concatenate(gl[:2], axis=1))
        gb = _sigmoid(jnp.concatenate(gl[2:], axis=1))
        outs[0][...] = (ga * ins[4][...].astype(F32) + gb * ins[5][...].astype(F32)).astype(BF16)

    return _ew(name, body, T // TR,
               [(p, *_rowblk(GLB, COL_GL // GLB + q)) for q in range(4)] + [(ya, *_rowblk(D)), (yb, *_rowblk(D))],
               [], [(_sds((T, D), BF16), *_rowblk(D))])[0]


def _gate_bwd(name, p, ya, yb, dz):
    T = ya.shape[0]

    def kern(gl_ref, ya_ref, yb_ref, dz_ref, dya_ref, dyb_ref, dp_ref):
        j = pl.program_id(1)
        g = _sigmoid(gl_ref[...].astype(F32))
        dzv = dz_ref[...].astype(F32)
        dbranch = (dzv * g).astype(BF16)
        dg = dzv * g * (1.0 - g)

        @pl.when(j < 2)
        def _():
            dya_ref[...] = dbranch
            dp_ref[...] = (dg * ya_ref[...].astype(F32)).astype(BF16)

        @pl.when(j >= 2)
        def _():
            dyb_ref[...] = dbranch
            dp_ref[...] = (dg * yb_ref[...].astype(F32)).astype(BF16)

    rt = T // 4
    first = pl.BlockSpec((rt, GLB), lambda i, j: (i, jnp.minimum(j, 1)))
    second = pl.BlockSpec((rt, GLB), lambda i, j: (i, jnp.maximum(j - 2, 0)))
    return pl.pallas_call(
        kern, name=name, grid=(4, 4),
        in_specs=[pl.BlockSpec((rt, GLB), lambda i, j: (i, COL_GL // GLB + j)), first, second,
                  pl.BlockSpec((rt, GLB), lambda i, j: (i, j % 2))],
        out_specs=[first, second, pl.BlockSpec((rt, GLB), lambda i, j: (i, COL_GL // GLB + j))],
        out_shape=[_sds((T, D), BF16), _sds((T, D), BF16), _sds((T, P_W), BF16)],
        compiler_params=_params(("arbitrary", "arbitrary")),
    )(p, ya, yb, dz)


def _swiglu_fwd(name, f):
    T = f.shape[0]

    def body(i, ins, ps, outs, acc):
        outs[0][...] = (_silu(ins[0][...].astype(F32)) * ins[1][...].astype(F32)).astype(BF16)

    return _ew(name, body, T // TR, [(f, *_rowblk(D_FF, 0)), (f, *_rowblk(D_FF, 1))], [],
               [(_sds((T, D_FF), BF16), *_rowblk(D_FF))])[0]


def _swiglu_bwd(name, f, ds):
    T = f.shape[0]

    def body(i, ins, ps, outs, acc):
        gate, up, dsv = ins[0][...].astype(F32), ins[1][...].astype(F32), ins[2][...].astype(F32)
        dgate = dsv * up * _silu_grad(gate)
        dup = dsv * _silu(gate)
        outs[0][...] = jnp.concatenate([dgate, dup], axis=1).astype(BF16)

    return _ew(name, body, T // TR, [(f, *_rowblk(D_FF, 0)), (f, *_rowblk(D_FF, 1)), (ds, *_rowblk(D_FF))], [],
               [(_sds((T, 2 * D_FF), BF16), *_rowblk(2 * D_FF))])[0]


AB = 128
CTX_BLKS = CTX // AB


def _rope_tables(S):
    pos = jnp.arange(S, dtype=jnp.int32)
    inv = ROPE_BASE ** (-jnp.arange(N_FREQ, dtype=F32) / N_FREQ)
    ang_r = (pos // GRID_W).astype(F32)[:, None] * inv[None, :]
    ang_c = (pos % GRID_W).astype(F32)[:, None] * inv[None, :]
    cos = jnp.concatenate([jnp.cos(ang_r)] * 2 + [jnp.cos(ang_c)] * 2, axis=1)
    sin = jnp.concatenate([-jnp.sin(ang_r), jnp.sin(ang_r), -jnp.sin(ang_c), jnp.sin(ang_c)], axis=1)
    return cos, sin


def _rope(x, cos, sin):
    w = x.shape[1]
    reps = w // HEAD
    lane = lax.broadcasted_iota(jnp.int32, x.shape, 1)
    partner = jnp.where((lane & 63) < 32, pltpu.roll(x, w - 32, 1), pltpu.roll(x, 32, 1))
    return x * jnp.tile(cos, (1, reps)) + partner * jnp.tile(sin, (1, reps))


def _unrope(dx, cos, sin):
    w = dx.shape[1]
    reps = w // HEAD
    lane = lax.broadcasted_iota(jnp.int32, dx.shape, 1)
    t = dx * jnp.tile(sin, (1, reps))
    partner = jnp.where((lane & 63) < 32, pltpu.roll(t, w - 32, 1), pltpu.roll(t, 32, 1))
    return dx * jnp.tile(cos, (1, reps)) + partner


def _qkv_prep(name, p, cos, sin, S):
    T = CTX + S
    nt = T // AB
    KW = N_KV * HEAD

    def with_ones(v):
        ones = jnp.ones((AB, HEAD), BF16)
        return jnp.concatenate([v[:, kh * HEAD:(kh + 1) * HEAD] if part == 0 else ones
                                for kh in range(N_KV) for part in range(2)], axis=1)

    def kern(q_ref, k_ref, v_ref, cos_ref, sin_ref, qa_ref, kp_ref, vp_ref, kc_ref, vc_ref):
        i = pl.program_id(0)
        cos_v, sin_v = cos_ref[...], sin_ref[...]
        @pl.when(i < CTX_BLKS)
        def _():
            qa_ref[...] = (q_ref[...].astype(F32) * ATT_SCALE).astype(BF16)
            kc_ref[...] = k_ref[...]
            vc_ref[...] = with_ones(v_ref[...])

        @pl.when((i < CTX_BLKS) | (i >= nt))
        def _():
            kp_ref[...] = jnp.zeros(kp_ref.shape, BF16)
            vp_ref[...] = jnp.zeros(vp_ref.shape, BF16)

        @pl.when((i >= CTX_BLKS) & (i < nt))
        def _():
            qa_ref[...] = (_rope(q_ref[...].astype(F32), cos_v, sin_v) * ATT_SCALE).astype(BF16)
            kp_ref[...] = _rope(k_ref[...].astype(F32), cos_v, sin_v).astype(BF16)
            vp_ref[...] = with_ones(v_ref[...])

    tok = lambda i: jnp.minimum(i, nt - 1)
    lat_map = lambda i: (jnp.clip(i - CTX_BLKS, 0, nt - CTX_BLKS - 1), 0)
    ctx_map = lambda i: (jnp.minimum(i, CTX_BLKS - 1), 0)
    return pl.pallas_call(
        kern, name=name, grid=(nt + CTX_BLKS,),
        in_specs=[pl.BlockSpec((AB, N_Q * HEAD), lambda i: (tok(i), COL_Q // (N_Q * HEAD))),
                  pl.BlockSpec((AB, KW), lambda i: (tok(i), COL_K // KW)),
                  pl.BlockSpec((AB, KW), lambda i: (tok(i), COL_V // KW)),
                  pl.BlockSpec((AB, HEAD), lat_map), pl.BlockSpec((AB, HEAD), lat_map)],
        out_specs=[pl.BlockSpec((AB, N_Q * HEAD), lambda i: (tok(i), 0)),
                   pl.BlockSpec((AB, KW), lambda i: (i, 0)), pl.BlockSpec((AB, 2 * KW), lambda i: (i, 0)),
                   pl.BlockSpec((AB, KW), ctx_map), pl.BlockSpec((AB, 2 * KW), ctx_map)],
        out_shape=[_sds((T, N_Q * HEAD), BF16), _sds((S + 2 * CTX, KW), BF16), _sds((S + 2 * CTX, 2 * KW), BF16),
                   _sds((CTX, KW), BF16), _sds((CTX, 2 * KW), BF16)],
        compiler_params=_params(),
    )(p, p, p, cos, sin)


GW = Q_PER_KV * HEAD


def _band_bias(S):
    r = jnp.arange(AB, dtype=jnp.int32)[:, None]
    c = jnp.arange(3 * AB, dtype=jnp.int32)[None, :]
    near = jnp.abs(c - AB - r) <= AB
    valid = jnp.stack([near & (c >= AB), near, near & (c < 2 * AB)])
    return jnp.where(valid, 0.0, NEG_INF).astype(F32)


def _bias_spec(S):
    nb = S // AB
    return pl.BlockSpec((None, AB, 3 * AB), lambda kh, n: (jnp.where(n == 0, 0, jnp.where(n == nb - 1, 2, 1)), 0, 0))


def _head_probs(q, sink, kc, vce, kb, vbe, bias):
    s_c = _dot(q, kc, NT)
    m = jnp.maximum(jnp.max(s_c, axis=-1, keepdims=True), sink)
    if kb is not None:
        s_b = _dot(q, kb, NT) + bias
        m = jnp.maximum(m, jnp.max(s_b, axis=-1, keepdims=True))
    p_c = jnp.exp(s_c - m).astype(BF16)
    acc = _dot(p_c, vce)
    p_b = None
    if kb is not None:
        p_b = jnp.exp(s_b - m).astype(BF16)
        acc = acc + _dot(p_b, vbe)
    return p_c, p_b, m, acc


def _attn_fwd(name, qa, kc, vc, sink4, S, band=None, prev=None, carry=None):
    T = qa.shape[0]
    has_band = band is not None
    nq = S // AB if has_band else CTX_BLKS
    q_off = CTX_BLKS if has_band else 0

    def kern(*refs):
        q_ref, kc_ref, vc_ref, sink_ref = refs[:4]
        rest = refs[4:]
        o_ref = rest[-1]
        n = pl.program_id(1)
        kc_v, vce = kc_ref[...], vc_ref[...]
        kb = vbe = bias = None
        if has_band:
            kp_ref, vp_ref, bias_ref = rest[:3]
            start = pl.multiple_of(n * AB + (CTX - AB), AB)
            kb = kp_ref[pl.ds(start, 3 * AB), :]
            vbe = vp_ref[pl.ds(start, 3 * AB), :]
            bias = bias_ref[...]
        outs = []
        for g in range(Q_PER_KV):
            sink = sink_ref[g:g + 1, 0:1]
            _, _, m, acc = _head_probs(q_ref[:, g * HEAD:(g + 1) * HEAD], sink, kc_v, vce, kb, vbe, bias)
            l = acc[:, HEAD:] + jnp.exp(sink - m)
            outs.append(acc[:, :HEAD] / l)
        o_ref[...] = jnp.concatenate(outs, axis=1).astype(BF16)

    in_specs = [pl.BlockSpec((AB, GW), lambda kh, n: (n + q_off, kh)),
                pl.BlockSpec((CTX, HEAD), lambda kh, n: (0, kh)), pl.BlockSpec((CTX, 2 * HEAD), lambda kh, n: (0, kh)),
                pl.BlockSpec((None, Q_PER_KV, HEAD), lambda kh, n: (kh, 0, 0))]
    args = [qa, kc, vc, sink4]
    if has_band:
        in_specs += [pl.BlockSpec((S + 2 * CTX, HEAD), lambda kh, n: (0, kh)),
                     pl.BlockSpec((S + 2 * CTX, 2 * HEAD), lambda kh, n: (0, kh)), _bias_spec(S)]
        args += list(band)
    alias = {}
    if prev is not None:
        in_specs.append(ANY)
        alias = {len(args): 0}
        args.append(prev)
    ci, ca, co, cs, cscr = _carry_args(carry)
    res = pl.pallas_call(
        _carried(kern, carry, len(args), 1, *_grid_ends((N_KV, nq))), name=name, grid=(N_KV, nq),
        in_specs=in_specs + ci,
        out_specs=[pl.BlockSpec((AB, GW), lambda kh, n: (n + q_off, kh))] + co,
        out_shape=[_sds((T, N_Q * HEAD), BF16)] + cs, input_output_aliases=alias, scratch_shapes=cscr,
        compiler_params=_params(("arbitrary", "arbitrary")),
    )(*args, *ca)
    return res[0] if carry is None else (res[0], res[1:])


def _attn_bwd(name, qa, kc, vc, sink4, o_all, do_all, S, band=None, prev_dq=None, carry=None):
    T = qa.shape[0]
    has_band = band is not None
    nq = S // AB if has_band else CTX_BLKS
    q_off = CTX_BLKS if has_band else 0
    KW = N_KV * HEAD

    def kern(*refs):
        q_ref, kc_ref, vc_ref, sink_ref, o_ref, do_ref = refs[:6]
        rest = refs[6:]
        if has_band:
            kp_ref, vp_ref, bias_ref = rest[:3]
            rest = rest[3:]
        if prev_dq is not None:
            rest = rest[1:]
        dq_ref, dkc_ref, dvc_ref, dsink_ref = rest[:4]
        n = pl.program_id(1)

        @pl.when(n == 0)
        def _():
            dkc_ref[...] = jnp.zeros(dkc_ref.shape, F32)
            dvc_ref[...] = jnp.zeros(dvc_ref.shape, F32)
            dsink_ref[...] = jnp.zeros(dsink_ref.shape, F32)
            if has_band:
                rest[4][...] = jnp.zeros(rest[4].shape, F32)
                rest[5][...] = jnp.zeros(rest[5].shape, F32)

        kc_v, vce = kc_ref[...], vc_ref[...]
        vc_v = vce[:, :HEAD]
        kb = vbe = vb = bias = None
        if has_band:
            start = pl.multiple_of(n * AB + (CTX - AB), AB)
            kb = kp_ref[pl.ds(start, 3 * AB), :]
            vbe = vp_ref[pl.ds(start, 3 * AB), :]
            vb = vbe[:, :HEAD]
            bias = bias_ref[...]
        stack = lambda ref: jnp.concatenate([ref[:, g * HEAD:(g + 1) * HEAD] for g in range(Q_PER_KV)], axis=0)
        q4, do4 = stack(q_ref), stack(do_ref)
        sink = jnp.concatenate([jnp.broadcast_to(sink_ref[g:g + 1, 0:1], (AB, 1)) for g in range(Q_PER_KV)], axis=0)
        s_c = _dot(q4, kc_v, NT)
        m = jnp.maximum(jnp.max(s_c, axis=-1, keepdims=True), sink)
        if has_band:
            s_b = _dot(q4, kb, NT) + jnp.tile(bias, (Q_PER_KV, 1))
            m = jnp.maximum(m, jnp.max(s_b, axis=-1, keepdims=True))
        p_c = jnp.exp(s_c - m).astype(BF16).astype(F32)
        p_sink = jnp.exp(sink - m)
        l = jnp.sum(p_c, axis=-1, keepdims=True) + p_sink
        if has_band:
            p_b = jnp.exp(s_b - m).astype(BF16).astype(F32)
            l = l + jnp.sum(p_b, axis=-1, keepdims=True)
        inv = 1.0 / l
        delta = jnp.sum(do4.astype(F32) * stack(o_ref).astype(F32), axis=-1, keepdims=True)
        do4b = do4.astype(BF16)
        pn_c = (p_c * inv).astype(BF16)
        ds_c = (p_c * inv * (_dot(do4b, vc_v, NT) - delta)).astype(BF16)
        dq4 = _dot(ds_c, kc_v)
        dkc_ref[...] += _dot(ds_c, q4, TN)
        dvc_ref[...] += _dot(pn_c, do4b, TN)
        if has_band:
            pn_b = (p_b * inv).astype(BF16)
            ds_b = (p_b * inv * (_dot(do4b, vb, NT) - delta)).astype(BF16)
            dq4 = dq4 + _dot(ds_b, kb)
            rest[4][pl.ds(start, 3 * AB), :] += _dot(ds_b, q4, TN)
            rest[5][pl.ds(start, 3 * AB), :] += _dot(pn_b, do4b, TN)
        dq4 = dq4 * ATT_SCALE
        dq_ref[...] = jnp.concatenate([dq4[g * AB:(g + 1) * AB, :] for g in range(Q_PER_KV)], axis=1)
        ps = p_sink * inv * delta
        dsink_ref[...] += jnp.concatenate(
            [jnp.broadcast_to(-jnp.sum(ps[g * AB:(g + 1) * AB, :], axis=0, keepdims=True), (1, HEAD))
             for g in range(Q_PER_KV)], axis=0)

    q_spec = pl.BlockSpec((AB, GW), lambda kh, n: (n + q_off, kh))
    c_spec = pl.BlockSpec((CTX, HEAD), lambda kh, n: (0, kh))
    ce_spec = pl.BlockSpec((CTX, 2 * HEAD), lambda kh, n: (0, kh))
    s_spec = pl.BlockSpec((None, Q_PER_KV, HEAD), lambda kh, n: (kh, 0, 0))
    in_specs = [q_spec, c_spec, ce_spec, s_spec, q_spec, q_spec]
    args = [qa, kc, vc, sink4, o_all, do_all]
    out_specs = [q_spec, c_spec, c_spec, s_spec]
    out_shape = [_sds((T, N_Q * HEAD), F32), _sds((CTX, KW), F32), _sds((CTX, KW), F32), _sds((N_KV, Q_PER_KV, HEAD), F32)]
    if has_band:
        p_spec = pl.BlockSpec((S + 2 * CTX, HEAD), lambda kh, n: (0, kh))
        in_specs += [p_spec, pl.BlockSpec((S + 2 * CTX, 2 * HEAD), lambda kh, n: (0, kh)), _bias_spec(S)]
        args += list(band)
        out_specs += [p_spec, p_spec]
        out_shape += [_sds((S + 2 * CTX, KW), F32)] * 2
    alias = {}
    if prev_dq is not None:
        in_specs.append(ANY)
        alias = {len(args): 0}
        args.append(prev_dq)
    ci, ca, co, cs, cscr = _carry_args(carry)
    n_out = len(out_specs)
    res = pl.pallas_call(
        _carried(kern, carry, len(args), n_out, *_grid_ends((N_KV, nq))), name=name, grid=(N_KV, nq),
        in_specs=in_specs + ci, out_specs=out_specs + co, out_shape=out_shape + cs, scratch_shapes=cscr,
        input_output_aliases=alias, compiler_params=_params(("arbitrary", "arbitrary")),
    )(*args, *ca)
    return res if carry is None else (res[:n_out], res[n_out:])


def _dqkv_assemble(name, dp, dq_all, dkp, dvp, dkc_l, dvc_l, dkc_c, dvc_c, cos, sin, S):
    T = CTX + S
    KW = N_KV * HEAD
    HALF = N_Q * HEAD // 2

    def kern(dq_ref, dkp_ref, dvp_ref, dkcl_ref, dvcl_ref, dkcc_ref, dvcc_ref, cos_ref, sin_ref, dp_in, out_ref):
        i = pl.program_id(0)
        j = pl.program_id(1)
        cos_v, sin_v = cos_ref[...], sin_ref[...]

        @pl.when((j < 2) & (i == 0))
        def _():
            out_ref[...] = dq_ref[...].astype(BF16)

        @pl.when((j < 2) & (i > 0))
        def _():
            out_ref[...] = _unrope(dq_ref[...], cos_v, sin_v).astype(BF16)

        @pl.when((j == 2) & (i == 0))
        def _():
            out_ref[...] = jnp.concatenate([dkcl_ref[...] + dkcc_ref[...], dvcl_ref[...] + dvcc_ref[...]],
                                           axis=1).astype(BF16)

        @pl.when((j == 2) & (i > 0))
        def _():
            out_ref[...] = jnp.concatenate([_unrope(dkp_ref[...], cos_v, sin_v), dvp_ref[...]], axis=1).astype(BF16)

    same = lambda i, j: (i, 0)
    lat_map = lambda i, j: (jnp.maximum(i - 1, 0), 0)
    ctx_map = lambda i, j: (0, 0)
    return pl.pallas_call(
        kern, name=name, grid=(T // TR, 3),
        in_specs=[pl.BlockSpec((TR, HALF), lambda i, j: (i, jnp.minimum(j, 1))),
                  pl.BlockSpec((TR, KW), same), pl.BlockSpec((TR, KW), same),
                  pl.BlockSpec((CTX, KW), ctx_map), pl.BlockSpec((CTX, KW), ctx_map),
                  pl.BlockSpec((CTX, KW), ctx_map), pl.BlockSpec((CTX, KW), ctx_map),
                  pl.BlockSpec((TR, HEAD), lat_map), pl.BlockSpec((TR, HEAD), lat_map), ANY],
        out_specs=pl.BlockSpec((TR, HALF), lambda i, j: (i, COL_Q // HALF + j)),
        out_shape=_sds((T, P_W), BF16), input_output_aliases={9: 0},
        compiler_params=_params(("arbitrary", "arbitrary")),
    )(dq_all, dkp, dvp, dkc_l, dvc_l, dkc_c, dvc_c, cos, sin, dp)


RB = 128
CH = 256
HALO = 8
SUB = 8
GRP = 8


def _vscan(a, b, reverse):
    row = lax.broadcasted_iota(jnp.int32, a.shape, 0)
    A, H = a, b
    for s in (1, 2, 4):
        sh = SUB - s if reverse else s
        m = (row < SUB - s) if reverse else (row >= s)
        As = pltpu.roll(A, sh, 0)
        Hs = pltpu.roll(H, sh, 0)
        H = jnp.where(m, A * Hs + H, H)
        A = jnp.where(m, A * As, A)
    return A, H


def _scan_rows(a_ref, b_ref, r0, nrows, reverse, carry, emit):
    ngrp = nrows // (SUB * GRP)
    row = lax.broadcasted_iota(jnp.int32, (SUB, RB), 0)

    def grp(gi, carry):
        g = (ngrp - 1 - gi) if reverse else gi
        base = r0 + g * (SUB * GRP)
        for v in (range(GRP - 1, -1, -1) if reverse else range(GRP)):
            rs = pl.multiple_of(base + v * SUB, SUB)
            A, H = _vscan(a_ref[pl.ds(rs, SUB), :], b_ref[pl.ds(rs, SUB), :], reverse)
            hf = H + A * carry
            if reverse:
                before = jnp.where(row == SUB - 1, carry, pltpu.roll(hf, SUB - 1, 0))
                carry = hf[0:1, :]
            else:
                before = jnp.where(row == 0, carry, pltpu.roll(hf, 1, 0))
                carry = hf[SUB - 1:SUB, :]
            emit(rs, hf, before)
        return carry

    return lax.fori_loop(0, ngrp, grp, carry)


def _pad_start(ci):
    return pl.multiple_of(ci * CH + HALO * jnp.minimum(ci, 1), HALO)


def _conv_taps(ext, transpose=False):
    n = CH + 2 * HALO
    taps = []
    for k in range(CONV_W):
        off = CONV_LEFT - k if transpose else k - CONV_LEFT
        taps.append(ext[HALO:HALO + CH, :] if off == 0 else pltpu.roll(ext, (-off) % n, 0)[HALO:HALO + CH, :])
    return taps


def _lru_gates(xl, w4, b4, ls):
    pre = _dot(xl.astype(BF16), w4) + b4
    out = []
    for d in range(2):
        r = _sigmoid(pre[:, d * RB:(d + 1) * RB])
        i = _sigmoid(pre[:, (2 + d) * RB:(3 + d) * RB])
        la = LRU_C * r * ls[d:d + 1, :]
        a = jnp.exp(la)
        q = -jnp.tanh(la) * (1.0 + a * a)
        out.append((r, i, a, q))
    return out


def _rnn_specs(T):
    col = lambda n, *_: (0, n)
    return dict(
        xr=pl.BlockSpec((T, RB), lambda n, *_: (0, COL_XR // RB + n)),
        gr=pl.BlockSpec((T, RB), lambda n, *_: (0, COL_GR // RB + n)),
        act=pl.BlockSpec((T, RB), col),
        cw=pl.BlockSpec((CONV_W, RB), col), cb=pl.BlockSpec((1, RB), col),
        w4=pl.BlockSpec((None, RB, 4 * RB), lambda n, *_: (n, 0, 0)),
        b4=pl.BlockSpec((None, 1, 4 * RB), lambda n, *_: (n, 0, 0)),
        lam=pl.BlockSpec((2, RB), col))


PAD_ROWS = 3 * HALO


def _zero_pads(pad_ref, T):
    for r in (0, HALO + CTX, 2 * HALO + T):
        pad_ref[r:r + HALO, :] = jnp.zeros((HALO, RB), F32)


def _fill_padded(pad_ref, src_ref, T):
    _zero_pads(pad_ref, T)
    pad_ref[HALO:HALO + CTX, :] = src_ref[0:CTX, :].astype(F32)
    pad_ref[2 * HALO + CTX:2 * HALO + T, :] = src_ref[CTX:T, :].astype(F32)


def _pad_rows(ci):
    return pl.ds(pl.multiple_of(ci * CH + HALO + HALO * jnp.minimum(ci, 1), HALO), CH)


def _rnn_fwd(name, p, cw, cb, w4, b4, lam, T, carry=None):
    def kern(xr_ref, gr_ref, cw_ref, cb_ref, w4_ref, b4_ref, lam_ref,
             u_ref, a0, a1, yo_ref, hpf_ref, hpb_ref, r0_ref, r1_ref, i0_ref, i1_ref, xpad, b0, b1, y):
        _fill_padded(xpad, xr_ref, T)
        ls = _log_sigmoid(lam_ref[...])
        w4v, b4v, cwv, cbv = w4_ref[...], b4_ref[...], cw_ref[...], cb_ref[...]

        def chunk(ci, _):
            rows = pl.ds(pl.multiple_of(ci * CH, CH), CH)
            taps = _conv_taps(xpad[pl.ds(_pad_start(ci), CH + 2 * HALO), :])
            xl = cbv + sum(taps[k] * cwv[k:k + 1, :] for k in range(CONV_W))
            for d, (r, i, a, q) in enumerate(_lru_gates(xl, w4v, b4v, ls)):
                (a0, a1)[d][rows, :] = a
                (b0, b1)[d][rows, :] = jnp.sqrt(q) * (i * xl)
                (r0_ref, r1_ref)[d][rows, :] = r.astype(BF16)
                (i0_ref, i1_ref)[d][rows, :] = i.astype(BF16)
            return 0

        lax.fori_loop(0, T // CH, chunk, 0)
        zero = jnp.zeros((1, RB), F32)

        def emit_f(rs, hf, before):
            y[pl.ds(rs, SUB), :] = hf
            b0[pl.ds(rs, SUB), :] = before

        def emit_b(rs, hf, before):
            y[pl.ds(rs, SUB), :] += hf
            b1[pl.ds(rs, SUB), :] = before

        _scan_rows(a0, b0, 0, T, False, zero, emit_f)
        c = _scan_rows(a1, b1, 0, CTX, True, zero, emit_b)
        _scan_rows(a1, b1, CTX, T - CTX, True, c, emit_b)

        def finish(ci, _):
            rows = pl.ds(pl.multiple_of(ci * CH, CH), CH)
            yv = y[rows, :]
            u_ref[rows, :] = (yv * _gelu(gr_ref[rows, :].astype(F32))).astype(BF16)
            yo_ref[rows, :] = yv.astype(BF16)
            hpf_ref[rows, :] = b0[rows, :].astype(BF16)
            hpb_ref[rows, :] = b1[rows, :].astype(BF16)
            return 0

        lax.fori_loop(0, T // CH, finish, 0)

    sp = _rnn_specs(T)
    ci, ca, co, cs, cscr = _carry_args(carry)
    dts = [BF16, F32, F32] + [BF16] * 7
    res = pl.pallas_call(
        _carried(kern, carry, 7, 10, *_grid_ends((N_RNN_BLOCKS,))), name=name, grid=(N_RNN_BLOCKS,),
        in_specs=[sp["xr"], sp["gr"], sp["cw"], sp["cb"], sp["w4"], sp["b4"], sp["lam"]] + ci,
        out_specs=[sp["act"]] * 10 + co,
        out_shape=[_sds((T, D), dt) for dt in dts] + cs,
        scratch_shapes=[pltpu.VMEM((T + PAD_ROWS, RB), F32)] + [pltpu.VMEM((T, RB), F32)] * 3 + cscr,
        compiler_params=_params(),
    )(p, p, cw, cb, w4, b4, lam, *ca)
    return res if carry is None else (res[:10], res[10:])


def _rnn_bwd(name, p, du, saved, dp, cw, cb, w4, b4, lam, T, carry=None):
    def kern(xr_ref, gr_ref, du_ref, a0, a1, y_ref, hpf_ref, hpb_ref, r0_ref, r1_ref, i0_ref, i1_ref,
             cw_ref, cb_ref, w4_ref, b4_ref, lam_ref, dp_in,
             dp_ref, dcw_ref, dcb_ref, dw4_ref, db4_ref, dlam_ref,
             xpad, dxpad, c0, c1, dy, dgr_ref):
        j = pl.program_id(1)

        @pl.when(j == 0)
        def _():
            work(xr_ref, gr_ref, du_ref, a0, a1, y_ref, (hpf_ref, hpb_ref), (r0_ref, r1_ref), (i0_ref, i1_ref),
                 cw_ref, cb_ref, w4_ref, lam_ref, dp_ref, dgr_ref, dcw_ref, dcb_ref, dw4_ref, db4_ref, dlam_ref,
                 xpad, dxpad, c0, c1, dy)

        @pl.when(j == 1)
        def _():
            dp_ref[...] = dgr_ref[...]

    def work(xr_ref, gr_ref, du_ref, a0, a1, y_ref, hp_refs, r_refs, i_refs, cw_ref, cb_ref, w4_ref, lam_ref,
             dxr_ref, dgr_ref, dcw_ref, dcb_ref, dw4_ref, db4_ref, dlam_ref, xpad, dxpad, c0, c1, dy):
        _fill_padded(xpad, xr_ref, T)
        _zero_pads(dxpad, T)
        lam_v = lam_ref[...]
        ls = _log_sigmoid(lam_v)
        w4v, cwv, cbv = w4_ref[...], cw_ref[...], cb_ref[...]

        def conv_chunk(ci):
            taps = _conv_taps(xpad[pl.ds(_pad_start(ci), CH + 2 * HALO), :])
            return taps, cbv + sum(taps[k] * cwv[k:k + 1, :] for k in range(CONV_W))

        def phase_a(ci, _):
            rows = pl.ds(pl.multiple_of(ci * CH, CH), CH)
            gr = gr_ref[rows, :].astype(F32)
            duv = du_ref[rows, :].astype(F32)
            dyv = duv * _gelu(gr)
            dgr_ref[rows, :] = (duv * y_ref[rows, :].astype(F32) * _gelu_grad(gr)).astype(BF16)
            dy[rows, :] = dyv
            c0[rows, :] = a0[rows, :] * dyv
            c1[rows, :] = a1[rows, :] * dyv
            return 0

        lax.fori_loop(0, T // CH, phase_a, 0)
        zero = jnp.zeros((1, RB), F32)

        def emit0(rs, hf, before):
            c0[pl.ds(rs, SUB), :] = dy[pl.ds(rs, SUB), :] + before

        def emit1(rs, hf, before):
            c1[pl.ds(rs, SUB), :] = dy[pl.ds(rs, SUB), :] + before

        _scan_rows(a0, c0, 0, T, True, zero, emit0)
        c = _scan_rows(a1, c1, CTX, T - CTX, False, zero, emit1)
        _scan_rows(a1, c1, 0, CTX, False, c, emit1)

        dw4_ref[...] = jnp.zeros(dw4_ref.shape, F32)
        db4_ref[...] = jnp.zeros(db4_ref.shape, F32)
        dlam_ref[...] = jnp.zeros(dlam_ref.shape, F32)
        dcw_ref[...] = jnp.zeros(dcw_ref.shape, F32)
        dcb_ref[...] = jnp.zeros(dcb_ref.shape, F32)

        def phase_c(ci, _):
            base = pl.multiple_of(ci * CH, CH)
            rows = pl.ds(base, CH)
            _, xl = conv_chunk(ci)
            dxl = jnp.zeros((CH, RB), F32)
            dpre_a, dpre_x, dls = [], [], []
            for d in range(2):
                a = (a0, a1)[d][rows, :]
                r = r_refs[d][rows, :].astype(F32)
                i = i_refs[d][rows, :].astype(F32)
                q = -jnp.tanh(LRU_C * r * ls[d:d + 1, :]) * (1.0 + a * a)
                g = (c0, c1)[d][rows, :]
                hp = hp_refs[d][rows, :].astype(F32)
                gm = g * jnp.sqrt(q)
                di = gm * xl
                dxl = dxl + gm * i
                dla = a * (g * hp - a * (g * (i * xl)) * lax.rsqrt(q))
                dr = dla * (LRU_C * ls[d:d + 1, :])
                dls.append(_colsum(dla * (LRU_C * r)))
                dpre_a.append(dr * r * (1.0 - r))
                dpre_x.append(di * i * (1.0 - i))
            dpre = jnp.concatenate(dpre_a + dpre_x, axis=1)
            dpre_b = dpre.astype(BF16)
            dxl = dxl + _dot(dpre_b, w4v, NT)
            dw4_ref[...] += _dot(xl.astype(BF16), dpre_b, TN)
            db4_ref[...] += _colsum(dpre)
            dlam_ref[...] += jnp.concatenate(dls, axis=0)
            dcb_ref[...] += _colsum(dxl)
            dxpad[_pad_rows(ci), :] = dxl
            return 0

        lax.fori_loop(0, T // CH, phase_c, 0)
        dlam_ref[...] = dlam_ref[...] * _sigmoid(-lam_v)

        def phase_d(ci, _):
            base = pl.multiple_of(ci * CH, CH)
            rows = pl.ds(base, CH)
            xtaps, _ = conv_chunk(ci)
            dtaps = _conv_taps(dxpad[pl.ds(_pad_start(ci), CH + 2 * HALO), :], transpose=True)
            dxl = dxpad[_pad_rows(ci), :]
            dxr_ref[rows, :] = sum(dtaps[k] * cwv[k:k + 1, :] for k in range(CONV_W)).astype(BF16)
            dcw_ref[...] += jnp.concatenate([_colsum(dxl * xtaps[k]) for k in range(CONV_W)], axis=0)
            return 0

        lax.fori_loop(0, T // CH, phase_d, 0)

    sp = _rnn_specs(T)
    dp_spec = pl.BlockSpec((T, RB), lambda n, j: (0, COL_XR // RB + n + j * (COL_GR - COL_XR) // RB))
    ci, ca, co, cs, cscr = _carry_args(carry)
    n_in = 3 + len(saved) + 5 + 1
    res = pl.pallas_call(
        _carried(kern, carry, n_in, 6, *_grid_ends((N_RNN_BLOCKS, 2))), name=name, grid=(N_RNN_BLOCKS, 2),
        in_specs=[sp["xr"], sp["gr"]] + [sp["act"]] * (1 + len(saved)) + [sp["cw"], sp["cb"], sp["w4"], sp["b4"],
                                                                           sp["lam"], ANY] + ci,
        out_specs=[dp_spec, sp["cw"], sp["cb"], sp["w4"], sp["b4"], sp["lam"]] + co,
        out_shape=[_sds((T, P_W), BF16), _sds((CONV_W, D), F32), _sds((1, D), F32),
                   _sds((N_RNN_BLOCKS, RB, 4 * RB), F32), _sds((N_RNN_BLOCKS, 1, 4 * RB), F32), _sds((2, D), F32)] + cs,
        scratch_shapes=([pltpu.VMEM((T + PAD_ROWS, RB), F32)] * 2 + [pltpu.VMEM((T, RB), F32)] * 3
                        + [pltpu.VMEM((T, RB), BF16)] + cscr),
        input_output_aliases={n_in - 1: 0},
        compiler_params=_params(("arbitrary", "arbitrary")),
    )(p, p, du, *saved, cw, cb, w4, b4, lam, dp, *ca)
    return res if carry is None else (res[:6], res[6:])


class _Plan:
    def __init__(self, shards, Ws):
        L = len(Ws)
        self.shards, self.Ws = shards, Ws
        self.Gs = [None] * L
        self.slots = [dict() for _ in range(L)]
        self.gate_slots = [None] * L
        self.table = {}
        for l in range(L):
            t = f"l{l}_"
            self.table[t + "proj"] = [("gather", l, k) for k in ("wo_rnn", "wo_attn", "wout")]
            self.table[t + "rnn_fwd"] = [("gather", l, "wffn_in_t")]
            self.table[t + "attn_lat_fwd"] = [("gather", l + 1, "win_t")] if l + 1 < L else []
            self.table[t + "ffn_in"] = [("gather", l, "wffn_out")]
            self.table[t + "ffn_in_dx"] = [("scatter", l, "wffn_out")]
            self.table[t + "attn_lat_bwd"] = [("scatter", l, "wffn_in_t")]
            self.table[t + "rnn_bwd"] = ([("scatter", l, k) for k in ("wout", "wo_attn", "wo_rnn")]
                                         + ([("scatter", l + 1, "win_t"), ("gates", l + 1, "w4")] if l + 1 < L else []))
        self.table["l0_proj_dx"] = [("scatter", 0, "win_t_a")]
        self.table["l0_proj_dw_b"] = [("gates", 0, "w4")]

    def carry(self, name):
        jobs = []
        for kind, l, k in self.table.get(name, []):
            if kind == "gather":
                jobs.append(("gather", self.shards[l][k]))
            elif kind == "scatter":
                jobs.append(("scatter", self.Gs[l][k].reshape(N_DEV, -1, self.Gs[l][k].shape[-1])))
            else:
                jobs.append(("gather", self.Gs[l]["w4"].reshape(N_RNN_BLOCKS * RB, 4 * RB).astype(BF16)))
        return _Carry(jobs) if jobs else None

    def done(self, name, got):
        for (kind, l, k), res in zip(self.table[name], got):
            if kind == "gather":
                self.Ws[l][k] = res.reshape(-1, D)
            elif kind == "scatter":
                self.slots[l][k] = res
            else:
                self.gate_slots[l] = res


def _run(X, fn, name, *args, **kw):
    carry = None if X is None else X.carry(name)
    if carry is None:
        return fn(name, *args, **kw)
    out, got = fn(name, *args, carry=carry, **kw)
    X.done(name, got)
    return out


def _layer_fwd(l, xa, h, W, rope, S, nxt, X=None):
    T = xa.shape[0]
    tag = f"l{l}_"
    cos, sin, bias = rope
    p = _run(X, _mm_act, tag + "proj", h, W["win_t"], "NT", BF16)
    u, *rnn_saved = _run(X, _rnn_fwd, tag + "rnn_fwd", p, W["cw"], W["cb"], W["w4"], W["b4"], W["lam"], T)
    qa, kp, vp, kc, vc = _qkv_prep(tag + "qkv_prep", p, cos, sin, S)
    o_all = _attn_fwd(tag + "attn_ctx_fwd", qa, kc, vc, W["sink4"], S)
    o_all = _run(X, _attn_fwd, tag + "attn_lat_fwd", qa, kc, vc, W["sink4"], S, band=(kp, vp, bias), prev=o_all)
    ya = _mm_act(tag + "o_rnn", u, W["wo_rnn"], "NN")
    yb = _mm_act(tag + "o_attn", o_all, W["wo_attn"], "NN")
    z = _gate_fwd(tag + "gate_fwd", p, ya, yb)
    m = _mm_act(tag + "out", z, W["wout"], "NN")
    x1, h2 = _resid_norm_fwd(tag + "mix_resid", xa, m, W["g_mix_post"], W["mod"], GA1, W["g_ffn_pre"], W["mod"], SH2, SC2)
    f = _run(X, _mm_act, tag + "ffn_in", h2, W["wffn_in_t"], "NT", BF16)
    s = _swiglu_fwd(tag + "swiglu_fwd", f)
    e = _mm_act(tag + "ffn_out", s, W["wffn_out"], "NN")
    saved = dict(xa=xa, h=h, p=p, u=u, rnn=rnn_saved, qa=qa, kp=kp, vp=vp, kc=kc, vc=vc, o_all=o_all,
                 ya=ya, yb=yb, z=z, m=m, x1=x1, h2=h2, f=f, s=s, e=e)
    if nxt[0] == "norm":
        out = _resid_norm_fwd(tag + "ffn_resid", x1, e, W["g_ffn_post"], W["mod"], GA2, nxt[1], nxt[2], SH1, SC1)
    else:
        out = _resid_loss_fwd(tag + "ffn_resid_loss", x1, e, W["g_ffn_post"], W["mod"], GA2, nxt[1])
    return saved, out


def _layer_bwd(l, dx2, A, W, rope, S, X=None):
    T = dx2.shape[0]
    tag = f"l{l}_"
    cos, sin, bias = rope
    G = {}
    if X is not None:
        X.Gs[l] = G
    de, dga2, G["g_ffn_post"] = _resid_bwd(tag + "ffn_resid_bwd", dx2, A["e"], W["g_ffn_post"], W["mod"], GA2)
    ds = _mm_act(tag + "ffn_out_dx", de, W["wffn_out"], "NT", BF16)
    G["wffn_out"] = _mm_wgrad(tag + "ffn_out_dw", A["s"], de)
    df = _swiglu_bwd(tag + "swiglu_bwd", A["f"], ds)
    dh2 = _run(X, _mm_act, tag + "ffn_in_dx", df, W["wffn_in_t"], "NN")
    G["wffn_in_t"] = _mm_wgrad(tag + "ffn_in_dw", df, A["h2"])
    dx1, dm, dsh2, dsc2, G["g_ffn_pre"], dga1, G["g_mix_post"] = _normmod_resid_bwd(
        tag + "mix_resid_bwd", dh2, A["x1"], W["g_ffn_pre"], W["mod"], SH2, SC2, dx2, A["m"], W["g_mix_post"], GA1)
    dz = _mm_act(tag + "out_dx", dm, W["wout"], "NT")
    G["wout"] = _mm_wgrad(tag + "out_dw", A["z"], dm)
    dya, dyb, dp = _gate_bwd(tag + "gate_bwd", A["p"], A["ya"], A["yb"], dz)
    do = _mm_act(tag + "o_attn_dx", dyb, W["wo_attn"], "NT")
    G["wo_attn"] = _mm_wgrad(tag + "o_attn_dw", A["o_all"], dyb)
    du = _mm_act(tag + "o_rnn_dx", dya, W["wo_rnn"], "NT")
    G["wo_rnn"] = _mm_wgrad(tag + "o_rnn_dw", A["u"], dya)
    dq_all, dkc_c, dvc_c, dsink_c = _attn_bwd(tag + "attn_ctx_bwd", A["qa"], A["kc"], A["vc"], W["sink4"],
                                               A["o_all"], do, S)
    dq_all, dkc_l, dvc_l, dsink_l, dkp, dvp = _run(
        X, _attn_bwd, tag + "attn_lat_bwd", A["qa"], A["kc"], A["vc"], W["sink4"], A["o_all"], do, S,
        band=(A["kp"], A["vp"], bias), prev_dq=dq_all)
    G["sink4"] = dsink_c + dsink_l
    dp = _dqkv_assemble(tag + "dqkv", dp, dq_all, dkp, dvp, dkc_l, dvc_l, dkc_c, dvc_c, cos, sin, S)
    dp, G["cw"], G["cb"], G["w4"], G["b4"], G["lam"] = _run(
        X, _rnn_bwd, tag + "rnn_bwd", A["p"], du, A["rnn"], dp, W["cw"], W["cb"], W["w4"], W["b4"], W["lam"], T)
    if X is not None and l == 0:
        G["win_t_a"] = _mm_wgrad(tag + "proj_dw_a", dp, A["h"][:, :D // 2])
        dh = _run(X, _mm_act, tag + "proj_dx", dp, W["win_t"], "NN")
        G["win_t_b"] = _run(X, _mm_wgrad, tag + "proj_dw_b", dp, A["h"][:, D // 2:])
    else:
        dh = _mm_act(tag + "proj_dx", dp, W["win_t"], "NN")
        G["win_t"] = _mm_wgrad(tag + "proj_dw", dp, A["h"])
    dxa, dsh1, dsc1, G["g_mix_pre"] = _normmod_bwd(tag + "mix_norm_bwd", dh, A["xa"], W["g_mix_pre"], W["mod"],
                                                   SH1, SC1, dx1)
    G["mod"] = jnp.concatenate([dsh1, dsc1, dga1, dsh2, dsc2, dga2], axis=1)
    return dxa, G


def _local_step(xa, target, Ws, S, X=None):
    rope = (*_rope_tables(S), _band_bias(S))
    L = len(Ws)
    h = _normmod_fwd("l0_mix_norm", xa, Ws[0]["g_mix_pre"], Ws[0]["mod"], SH1, SC1)
    saved = []
    x = xa
    for l in range(L):
        nxt = ("norm", Ws[l + 1]["g_mix_pre"], Ws[l + 1]["mod"]) if l + 1 < L else ("loss", target)
        A, out = _layer_fwd(l, x, h, Ws[l], rope, S, nxt, X)
        saved.append(A)
        if l + 1 < L:
            x, h = out
    dx, sq = out
    Gs = [None] * L
    for l in reversed(range(L)):
        dx, Gs[l] = _layer_bwd(l, dx, saved[l], Ws[l], rope, S, X)
    return sq, dx, Gs


MESH = pl.DeviceIdType.MESH


def _place():
    return lax.axis_index("x"), lax.axis_index("y"), lax.axis_index("c")


def _lin(px, py, pc):
    return 4 * px + 2 * py + pc


def _allgather_small(name, blk):
    m, n = blk.shape

    def body(x_ref, out_ref, send_sems, recv_sems, local_sem):
        x, y, c = _place()
        me, sibling = (x, y, c), (x, y, 1 - c)
        chips = [(1 - x, y), (x, 1 - y), (1 - x, 1 - y)]

        def copy(k, block, to, src=None):
            dst = out_ref.at[_lin(*block)]
            return pltpu.make_async_remote_copy(src_ref=dst if src is None else src, dst_ref=dst,
                                                send_sem=send_sems.at[k], recv_sem=recv_sems.at[k],
                                                device_id=to, device_id_type=MESH)

        mine = pltpu.make_async_copy(x_ref, out_ref.at[_lin(*me)], local_sem)
        mine.start()
        first = [copy(0, me, sibling, src=x_ref)]
        first += [copy(1 + j, me, (*chip, c), src=x_ref) for j, chip in enumerate(chips)]
        for cp in first:
            cp.start()
        passed = [copy(4 + j, (*chip, c), sibling) for j, chip in enumerate(chips)]
        for j, chip in enumerate(chips):
            copy(1 + j, (*chip, c), me).wait_recv()
            passed[j].start()
        copy(0, sibling, me).wait_recv()
        for j, chip in enumerate(chips):
            copy(4 + j, (*chip, 1 - c), me).wait_recv()
        for cp in first + passed:
            cp.wait_send()
        mine.wait()

    return pl.pallas_call(
        body, name=name, out_shape=_sds((N_DEV, m, n), blk.dtype),
        in_specs=[pl.BlockSpec(memory_space=pltpu.VMEM)], out_specs=pl.BlockSpec(memory_space=pltpu.VMEM),
        scratch_shapes=[pltpu.SemaphoreType.DMA((7,)), pltpu.SemaphoreType.DMA((7,)), pltpu.SemaphoreType.DMA],
        compiler_params=pltpu.CompilerParams(vmem_limit_bytes=VMEM_LIMIT),
    )(blk)


def _allgather_hbm(name, shards):
    na = len(shards)

    def body(*refs):
        ins, outs = refs[:na], refs[na:2 * na]
        send_sems, recv_sems, local_sems = refs[2 * na:]
        x, y, c = _place()
        me, sibling = (x, y, c), (x, y, 1 - c)
        chips = [(1 - x, y), (x, 1 - y), (1 - x, 1 - y)]

        def copy(a, k, block, to, from_input=False):
            dst = outs[a].at[_lin(*block)]
            return pltpu.make_async_remote_copy(src_ref=ins[a] if from_input else dst, dst_ref=dst,
                                                send_sem=send_sems.at[a, k], recv_sem=recv_sems.at[a, k],
                                                device_id=to, device_id_type=MESH)

        mine = [pltpu.make_async_copy(ins[a], outs[a].at[_lin(*me)], local_sems.at[a]) for a in range(na)]
        for cp in mine:
            cp.start()
        first = []
        for a in range(na):
            first.append(copy(a, 0, me, sibling, True))
            first += [copy(a, 1 + j, me, (*chip, c), True) for j, chip in enumerate(chips)]
        for cp in first:
            cp.start()
        passed = []
        for j, chip in enumerate(chips):
            for a in range(na):
                copy(a, 1 + j, (*chip, c), me).wait_recv()
                fwd = copy(a, 4 + j, (*chip, c), sibling)
                fwd.start()
                passed.append(fwd)
        for a in range(na):
            copy(a, 0, sibling, me).wait_recv()
            for j, chip in enumerate(chips):
                copy(a, 4 + j, (*chip, 1 - c), me).wait_recv()
        for cp in first + passed:
            cp.wait_send()
        for cp in mine:
            cp.wait()

    return pl.pallas_call(
        body, name=name, out_shape=[_sds((N_DEV, *s.shape), s.dtype) for s in shards],
        in_specs=[ANY] * na, out_specs=[ANY] * na,
        scratch_shapes=[pltpu.SemaphoreType.DMA((na, 7)), pltpu.SemaphoreType.DMA((na, 7)),
                        pltpu.SemaphoreType.DMA((na,))],
    )(*shards)


def _exchange_shards(name, grads, L):
    nw = len(grads)
    na = nw * L
    flat = [g for per_layer in grads for g in per_layer]

    def body(*refs):
        ins, outs = refs[:na], refs[na:na + nw]
        send_sems, recv_sems, local_sems = refs[na + nw:]
        x, y, c = _place()
        me = _lin(x, y, c)
        peers = [(x ^ ((k + 1) >> 2 & 1), y ^ ((k + 1) >> 1 & 1), c ^ ((k + 1) & 1)) for k in range(7)]

        def copy(a, k, src_blk, dst_blk):
            return pltpu.make_async_remote_copy(src_ref=ins[a].at[src_blk], dst_ref=outs[a // L].at[a % L, dst_blk],
                                                send_sem=send_sems.at[a, k], recv_sem=recv_sems.at[a, k],
                                                device_id=peers[k], device_id_type=MESH)

        mine = [pltpu.make_async_copy(ins[a].at[me], outs[a // L].at[a % L, me], local_sems.at[a]) for a in range(na)]
        for cp in mine:
            cp.start()
        sent = [copy(a, k, _lin(*peers[k]), me) for a in range(na) for k in range(7)]
        for cp in sent:
            cp.start()
        for a in range(na):
            for k in range(7):
                copy(a, k, me, _lin(*peers[k])).wait_recv()
        for cp in sent:
            cp.wait_send()
        for cp in mine:
            cp.wait()

    return pl.pallas_call(
        body, name=name, out_shape=[_sds((L, *per_layer[0].shape), per_layer[0].dtype) for per_layer in grads],
        in_specs=[ANY] * na, out_specs=[ANY] * nw,
        scratch_shapes=[pltpu.SemaphoreType.DMA((na, 7)), pltpu.SemaphoreType.DMA((na, 7)),
                        pltpu.SemaphoreType.DMA((na,))],
    )(*flat)


MOD_ROWS = 16
MOD_SHARD = 6 * D // N_DEV
HI = lax.Precision.HIGHEST


def _mod_fwd(name, c9, w_mod, b_shard):
    L = w_mod.shape[0]

    def kern(c_ref, w_ref, b_ref, o_ref):
        o_ref[...] = lax.dot_general(_silu(c_ref[...]), w_ref[...], NN, precision=HI,
                                     preferred_element_type=F32) + b_ref[...]

    return pl.pallas_call(
        kern, name=name, grid=(L,),
        in_specs=[_full_spec(c9.shape), pl.BlockSpec((None, D, MOD_SHARD), lambda l: (l, 0, 0)),
                  pl.BlockSpec((None, 1, MOD_SHARD), lambda l: (l, 0, 0))],
        out_specs=pl.BlockSpec((None, MOD_ROWS, MOD_SHARD), lambda l: (l, 0, 0)),
        out_shape=_sds((L, MOD_ROWS, MOD_SHARD), F32), compiler_params=_params(),
    )(c9, w_mod, b_shard)


def _mod_bwd(name, c9, w_mod, dmod_all, dmod_cols):
    L = w_mod.shape[0]

    def rows9(ref, l):
        own = jnp.concatenate([ref[j, 2 * l + 1:2 * l + 2, :] for j in range(N_DEV)], axis=0)
        ctx = ref[0, 2 * l:2 * l + 1, :]
        for j in range(1, N_DEV):
            ctx = ctx + ref[j, 2 * l:2 * l + 1, :]
        return own, ctx

    def kern(c_ref, w_ref, all_ref, cols_ref, gw_ref, gb_ref, gc_ref):
        l = pl.program_id(0)
        for ll in range(L):
            @pl.when(l == ll)
            def _():
                own, ctx = rows9(all_ref, ll)
                gb_ref[...] = _colsum(own) + ctx
                own_s, ctx_s = rows9(cols_ref, ll)
                r16 = jnp.concatenate([own_s, ctx_s, jnp.zeros((MOD_ROWS - N_DEV - 1, MOD_SHARD), F32)], axis=0)
                gw_ref[...] = lax.dot_general(_silu(c_ref[...]), r16, TN, precision=HI, preferred_element_type=F32)
                part = lax.dot_general(r16, w_ref[...], NT, precision=HI,
                                       preferred_element_type=F32)[N_DEV:N_DEV + 1, :]
                if ll == 0:
                    gc_ref[...] = part
                else:
                    gc_ref[...] += part

    return pl.pallas_call(
        kern, name=name, grid=(L,),
        in_specs=[_full_spec(c9.shape), pl.BlockSpec((None, D, MOD_SHARD), lambda l: (l, 0, 0)),
                  _full_spec(dmod_all.shape), _full_spec(dmod_cols.shape)],
        out_specs=[pl.BlockSpec((None, D, MOD_SHARD), lambda l: (l, 0, 0)),
                   pl.BlockSpec((None, 1, 6 * D), lambda l: (l, 0, 0)), _full_spec((1, D))],
        out_shape=[_sds((L, D, MOD_SHARD), F32), _sds((L, 1, 6 * D), F32), _sds((1, D), F32)],
        compiler_params=_params(),
    )(c9, w_mod, dmod_all, dmod_cols)


_BC1 = 1.0 - ADAM_B1 ** ADAM_STEP
_BC2 = 1.0 - ADAM_B2 ** ADAM_STEP


def _adamw_vals(w, g, m, v):
    m = ADAM_B1 * m + (1.0 - ADAM_B1) * g
    v = ADAM_B2 * v + (1.0 - ADAM_B2) * (g * g)
    delta = -ADAM_LR * ((m / _BC1) / (jnp.sqrt(v / _BC2) + ADAM_EPS) + ADAM_WD * w)
    return delta, m, v


def _adamw(name, w, g, m, v, tile):
    R, C = w.shape
    blk = ((tile, C), lambda i: (i, 0))

    def body(i, ins, ps, outs, acc):
        d, mm, vv = _adamw_vals(ins[0][...], ins[1][...], ins[2][...], ins[3][...])
        outs[0][...] = d
        outs[1][...] = mm
        outs[2][...] = vv

    return _ew(name, body, R // tile, [(a, *blk) for a in (w, g, m, v)], [], [(_sds((R, C), F32), *blk)] * 3)


def _sum_slots(ref):
    g = ref[0].astype(F32)
    for j in range(1, N_DEV):
        g = g + ref[j].astype(F32)
    return g


def _adamw_slots(name, slots, w, m, v, tile):
    L, R, C = w.shape
    n = R // tile
    spec = pl.BlockSpec((None, tile, C), lambda l, i: (l, i, 0))
    pieces = [s if isinstance(s, (list, tuple)) else [s] for s in slots]
    layer_of = [ll for ll, ps in enumerate(pieces) for _ in ps]
    flat = [p for ps in pieces for p in ps]

    def slot_spec(ll, cols):
        return pl.BlockSpec((N_DEV, tile, cols),
                            lambda l, i: (0, jnp.where(l == ll, i, jnp.where(l < ll, 0, n - 1)), 0))

    def kern(*refs):
        s_refs = refs[:len(flat)]
        w_ref, m_ref, v_ref, g_ref, d_ref, mo_ref, vo_ref = refs[len(flat):]
        l = pl.program_id(0)
        for ll in range(L):
            @pl.when(l == ll)
            def _():
                parts = [_sum_slots(r) for r, lr in zip(s_refs, layer_of) if lr == ll]
                g = parts[0] if len(parts) == 1 else jnp.concatenate(parts, axis=1)
                g_ref[...] = g
                d_ref[...], mo_ref[...], vo_ref[...] = _adamw_vals(w_ref[...], g, m_ref[...], v_ref[...])

    return pl.pallas_call(
        kern, name=name, grid=(L, n),
        in_specs=[slot_spec(ll, p.shape[-1]) for ll, p in zip(layer_of, flat)] + [spec, spec, spec],
        out_specs=[spec] * 4, out_shape=[_sds((L, R, C), F32)] * 4,
        compiler_params=_params(("arbitrary", "arbitrary")),
    )(*flat, w, m, v)


def _sum_blocks(name, blocks):
    _, R, C = blocks.shape

    def kern(b_ref, o_ref):
        o_ref[...] = _sum_slots(b_ref)

    return pl.pallas_call(kern, name=name, in_specs=[_full_spec(blocks.shape)], out_specs=_full_spec((R, C)),
                          grid=(1,), out_shape=_sds((R, C), F32), compiler_params=_params())(blocks)


BIG = ("win_t", "wo_rnn", "wo_attn", "wout", "wffn_in_t", "wffn_out")
BIG_SRC = ("w_in", "w_o_rnn", "w_o_attn", "w_out", "w_ffn_in", "w_ffn_out")
BIG_T = (True, False, False, False, True, False)
BIG_TILE = (176, 128, 128, 128, 176, 176)


def _chan_full(g8):
    return jnp.transpose(g8, (1, 0, 2)).reshape(g8.shape[1], D)


def kernel(x, c, ctx, c_ctx, w_mod, b_mod, g_mix_pre, g_mix_post, g_ffn_pre, g_ffn_post, w_in, conv_w, conv_b, lru_wa, lru_ba, lru_wx, lru_bx, lru_lam, attn_sink, w_o_rnn, w_o_attn, w_out, w_ffn_in, w_ffn_out, loss_target, m_c_ctx, m_w_mod, m_b_mod, m_g_mix_pre, m_g_mix_post, m_g_ffn_pre, m_g_ffn_post, m_w_in, m_conv_w, m_conv_b, m_lru_wa, m_lru_ba, m_lru_wx, m_lru_bx, m_lru_lam, m_attn_sink, m_w_o_rnn, m_w_o_attn, m_w_out, m_w_ffn_in, m_w_ffn_out, v_c_ctx, v_w_mod, v_b_mod, v_g_mix_pre, v_g_mix_post, v_g_ffn_pre, v_g_ffn_post, v_w_in, v_conv_w, v_conv_b, v_lru_wa, v_lru_ba, v_lru_wx, v_lru_bx, v_lru_lam, v_attn_sink, v_w_o_rnn, v_w_o_attn, v_w_out, v_w_ffn_in, v_w_ffn_out):
    P = dict(c_ctx=c_ctx, w_mod=w_mod, b_mod=b_mod, g_mix_pre=g_mix_pre, g_mix_post=g_mix_post, g_ffn_pre=g_ffn_pre,
             g_ffn_post=g_ffn_post, w_in=w_in, conv_w=conv_w, conv_b=conv_b, lru_wa=lru_wa, lru_ba=lru_ba,
             lru_wx=lru_wx, lru_bx=lru_bx, lru_lam=lru_lam, attn_sink=attn_sink, w_o_rnn=w_o_rnn, w_o_attn=w_o_attn,
             w_out=w_out, w_ffn_in=w_ffn_in, w_ffn_out=w_ffn_out)
    Mo = dict(c_ctx=m_c_ctx, w_mod=m_w_mod, b_mod=m_b_mod, g_mix_pre=m_g_mix_pre, g_mix_post=m_g_mix_post,
              g_ffn_pre=m_g_ffn_pre, g_ffn_post=m_g_ffn_post, w_in=m_w_in, conv_w=m_conv_w, conv_b=m_conv_b,
              lru_wa=m_lru_wa, lru_ba=m_lru_ba, lru_wx=m_lru_wx, lru_bx=m_lru_bx, lru_lam=m_lru_lam,
              attn_sink=m_attn_sink, w_o_rnn=m_w_o_rnn, w_o_attn=m_w_o_attn, w_out=m_w_out, w_ffn_in=m_w_ffn_in,
              w_ffn_out=m_w_ffn_out)
    Vo = dict(c_ctx=v_c_ctx, w_mod=v_w_mod, b_mod=v_b_mod, g_mix_pre=v_g_mix_pre, g_mix_post=v_g_mix_post,
              g_ffn_pre=v_g_ffn_pre, g_ffn_post=v_g_ffn_post, w_in=v_w_in, conv_w=v_conv_w, conv_b=v_conv_b,
              lru_wa=v_lru_wa, lru_ba=v_lru_ba, lru_wx=v_lru_wx, lru_bx=v_lru_bx, lru_lam=v_lru_lam,
              attn_sink=v_attn_sink, w_o_rnn=v_w_o_rnn, w_o_attn=v_w_o_attn, w_out=v_w_out, w_ffn_in=v_w_ffn_in,
              w_ffn_out=v_w_ffn_out)
    L = w_in.shape[0]
    S = x.shape[1]
    me = _lin(*_place())

    small = jnp.concatenate([c.reshape(8, 128), conv_w.reshape(L * CONV_W, 128), lru_ba.reshape(2 * L, 128),
                             lru_bx.reshape(2 * L, 128), lru_lam.reshape(2 * L, 128), jnp.zeros((4, 128), F32)], axis=0)
    small_all = _allgather_small("ag_small", small)
    c_all = small_all[:, 0:8].reshape(N_DEV, D)
    conv_w_f = _chan_full(small_all[:, 8:16]).reshape(L, CONV_W, D)
    lru_ba_f = _chan_full(small_all[:, 16:20]).reshape(L, 2, D)
    lru_bx_f = _chan_full(small_all[:, 20:24]).reshape(L, 2, D)
    lru_lam_f = _chan_full(small_all[:, 24:28]).reshape(L, 2, D)

    c9 = jnp.concatenate([c_all, c_ctx[None], jnp.zeros((MOD_ROWS - N_DEV - 1, D), F32)], axis=0)
    b_shard = lax.dynamic_slice_in_dim(b_mod, me * MOD_SHARD, MOD_SHARD, axis=1)[:, None, :]
    mod_part = _mod_fwd("mod_fwd", c9, w_mod, b_shard)
    mod_all = _allgather_small("ag_mod", mod_part.reshape(L * MOD_ROWS, MOD_SHARD))
    mod_all = jnp.transpose(mod_all.reshape(N_DEV, L, MOD_ROWS, MOD_SHARD), (1, 2, 0, 3)).reshape(L, MOD_ROWS, 6 * D)
    own_row = lax.dynamic_index_in_dim(mod_all, me, axis=1, keepdims=False)
    modrows = jnp.stack([mod_all[:, N_DEV], own_row], axis=1)

    shards = [{k: (P[src][l].T if tr else P[src][l]).astype(BF16) for k, src, tr in zip(BIG, BIG_SRC, BIG_T)}
              for l in range(L)]
    win0, = _allgather_hbm("ag_w_in0", [shards[0]["win_t"]])
    Ws = []
    for l in range(L):
        W = {"win_t": win0.reshape(-1, D)} if l == 0 else {}
        W.update(
            cw=conv_w_f[l], cb=conv_b[l][None],
            w4=jnp.concatenate([lru_wa[l, 0], lru_wa[l, 1], lru_wx[l, 0], lru_wx[l, 1]], axis=-1).astype(BF16),
            b4=jnp.concatenate([lru_ba_f[l, 0].reshape(N_RNN_BLOCKS, 1, RB), lru_ba_f[l, 1].reshape(N_RNN_BLOCKS, 1, RB),
                                lru_bx_f[l, 0].reshape(N_RNN_BLOCKS, 1, RB), lru_bx_f[l, 1].reshape(N_RNN_BLOCKS, 1, RB)],
                               axis=-1),
            lam=lru_lam_f[l], sink4=jnp.broadcast_to(attn_sink[l].reshape(N_KV, Q_PER_KV, 1), (N_KV, Q_PER_KV, HEAD)),
            g_mix_pre=g_mix_pre[l][None], g_mix_post=g_mix_post[l][None], g_ffn_pre=g_ffn_pre[l][None],
            g_ffn_post=g_ffn_post[l][None], mod=modrows[l])
        Ws.append(W)

    xa = jnp.concatenate([ctx[0], x[0]], axis=0)
    plan = _Plan(shards, Ws)
    sq, dxa, Gs = _local_step(xa, loss_target[0], Ws, S, plan)
    loss = lax.psum((0.5 / D) * jnp.sum(sq), ("x", "y", "c"))
    grad_x = dxa[CTX:][None]

    dmod = jnp.concatenate([Gs[l]["mod"] for l in range(L)] + [jnp.zeros((8 - 2 * L, 6 * D), F32)], axis=0)
    dmod_all = _allgather_small("ag_dmod", dmod)
    dmod_cols = lax.dynamic_slice_in_dim(dmod_all, me * MOD_SHARD, MOD_SHARD, axis=2)
    g_w_mod, g_b_mod, dsc_part = _mod_bwd("mod_bwd", c9, w_mod, dmod_all, dmod_cols)
    g_b_mod = g_b_mod[:, 0]

    def rows(name, shape):
        return jnp.concatenate([Gs[l][name].reshape(shape) for l in range(L)], axis=0)

    b4g = [Gs[l]["b4"].reshape(N_RNN_BLOCKS, 4, RB) for l in range(L)]
    sink_row = jnp.concatenate([Gs[l]["sink4"][:, :, 0].reshape(1, N_Q) for l in range(L)]
                               + [jnp.zeros((1, D - L * N_Q), F32)], axis=1)
    small_g = jnp.concatenate(
        [rows("g_mix_pre", (1, D)), rows("g_mix_post", (1, D)), rows("g_ffn_pre", (1, D)), rows("g_ffn_post", (1, D)),
         rows("cb", (1, D)), rows("cw", (CONV_W, D))]
        + [b4g[l][:, d].reshape(1, D) for l in range(L) for d in range(2)]
        + [b4g[l][:, 2 + d].reshape(1, D) for l in range(L) for d in range(2)]
        + [rows("lam", (2, D)), sink_row, dsc_part], axis=0)
    n_small = small_g.shape[0]
    small_tot = _sum_blocks("sum_small", _allgather_small("ag_small_grads", small_g))
    o = 0
    G = {}
    for name in ("g_mix_pre", "g_mix_post", "g_ffn_pre", "g_ffn_post", "conv_b"):
        G[name] = small_tot[o:o + L]
        o += L
    G["conv_w"] = small_tot[o:o + L * CONV_W].reshape(L, CONV_W, D)
    o += L * CONV_W
    for name in ("lru_ba", "lru_bx", "lru_lam"):
        G[name] = small_tot[o:o + 2 * L].reshape(L, 2, D)
        o += 2 * L
    G["attn_sink"] = small_tot[o, :L * N_Q].reshape(L, N_Q)
    sg = jax.nn.sigmoid(c_ctx)
    G["c_ctx"] = small_tot[o + 1] * (sg * (1.0 + c_ctx * (1.0 - sg)))
    G["b_mod"] = g_b_mod
    G["w_mod"] = g_w_mod

    last_slots, = _exchange_shards("exchange_w_in0", [[Gs[0]["win_t_b"].reshape(N_DEV, -1, D // 2)]], 1)
    plan.slots[0]["win_t"] = [plan.slots[0]["win_t_a"], last_slots[0]]

    out_g, out_d, out_m, out_v = {}, {}, {}, {}

    def put(name, res, shape=None):
        g, d, m, v = res
        for dst, val in ((out_g, g), (out_d, d), (out_m, m), (out_v, v)):
            dst[name] = val if shape is None else val.reshape(shape)

    for k, src, tr, tile in zip(BIG, BIG_SRC, BIG_T, BIG_TILE):
        lay = (lambda a: jnp.swapaxes(a, 1, 2)) if tr else (lambda a: a)
        res = _adamw_slots("adamw_" + src, [plan.slots[l][k] for l in range(L)], lay(P[src]), lay(Mo[src]),
                           lay(Vo[src]), tile)
        put(src, [lay(r) for r in res])
    res = _adamw("adamw_w_mod", w_mod.reshape(L * D, MOD_SHARD), g_w_mod.reshape(L * D, MOD_SHARD),
                 m_w_mod.reshape(L * D, MOD_SHARD), v_w_mod.reshape(L * D, MOD_SHARD), 256)
    put("w_mod", (g_w_mod,) + tuple(res), w_mod.shape)
    def fuse4(wa, wx):
        return jnp.concatenate([wa[:, 0], wa[:, 1], wx[:, 0], wx[:, 1]], axis=-1).reshape(L, N_RNN_BLOCKS * RB, 4 * RB)

    res = _adamw_slots("adamw_gates", plan.gate_slots,
                       fuse4(lru_wa, lru_wx), fuse4(m_lru_wa, m_lru_wx), fuse4(v_lru_wa, v_lru_wx), 256)
    res = [r.reshape(L, N_RNN_BLOCKS, RB, 4, RB) for r in res]
    put("lru_wa", [jnp.stack([r[:, :, :, 0], r[:, :, :, 1]], axis=1) for r in res])
    put("lru_wx", [jnp.stack([r[:, :, :, 2], r[:, :, :, 3]], axis=1) for r in res])
    rep = ("g_mix_pre", "g_mix_post", "g_ffn_pre", "g_ffn_post", "conv_b", "b_mod")

    def pack_rep(T_):
        sink = jnp.concatenate([T_["attn_sink"].reshape(1, L * N_Q), jnp.zeros((1, D - L * N_Q), F32)], axis=1)
        return jnp.concatenate([T_[n].reshape(-1, D) for n in rep] + [sink, T_["c_ctx"][None]], axis=0)

    pk = [pack_rep(T_) for T_ in (P, G, Mo, Vo)]
    n_rep = pk[0].shape[0]
    res = _adamw("adamw_replicated", *[jnp.pad(a, ((0, 24 - n_rep), (0, 0))) for a in pk], 24)
    res = (pk[1],) + tuple(r[:n_rep] for r in res)
    o = 0
    for n in rep:
        k = P[n].size // D
        put(n, [r[o:o + k] for r in res], P[n].shape)
        o += k
    put("attn_sink", [r[o, :L * N_Q] for r in res], attn_sink.shape)
    put("c_ctx", [r[o + 1] for r in res], c_ctx.shape)
    chan = ("conv_w", "lru_ba", "lru_bx", "lru_lam")
    g_own = {n: lax.dynamic_slice_in_dim(G[n], me * RB, RB, axis=2) for n in chan}

    def pack_chan(T_):
        return jnp.concatenate([T_[n].reshape(-1, RB) for n in chan], axis=0)

    pk = [pack_chan(T_) for T_ in (P, g_own, Mo, Vo)]
    n_ch = pk[0].shape[0]
    res = _adamw("adamw_channels", *[jnp.pad(a, ((0, 24 - n_ch), (0, 0))) for a in pk], 24)
    res = (pk[1],) + tuple(r[:n_ch] for r in res)
    o = 0
    for n in chan:
        k = P[n].size // RB
        put(n, [r[o:o + k] for r in res], P[n].shape)
        o += k

    order = ("c_ctx", "w_mod", "b_mod", "g_mix_pre", "g_mix_post", "g_ffn_pre", "g_ffn_post", "w_in", "conv_w", "conv_b",
             "lru_wa", "lru_ba", "lru_wx", "lru_bx", "lru_lam", "attn_sink", "w_o_rnn", "w_o_attn", "w_out", "w_ffn_in",
             "w_ffn_out")
    return (loss, grad_x, *[out_g[n] for n in order], *[out_d[n] for n in order], *[out_m[n] for n in order],
            *[out_v[n] for n in order])
```

```python
import functools
import math

import numpy as np
import jax
import jax.numpy as jnp
from jax import lax
from jax.experimental import pallas as pl
from jax.experimental.pallas import tpu as pltpu

F32 = jnp.float32
BF16 = jnp.bfloat16

D = 1024
CTX = 256
TR = 256
HEAD = 128
N_Q = 8
N_KV = 2
Q_PER_KV = N_Q // N_KV
GRID_W = 64
N_FREQ = HEAD // 4
ROPE_BASE = 10000.0
N_RNN_BLOCKS = 8
CONV_W = 4
CONV_LEFT = 2
LRU_C = 8.0
D_FF = 2816
IN_W = 5632
P_W = IN_W
COL_XR, COL_GR, COL_Q, COL_K, COL_V, COL_GL = 0, 1024, 2048, 3072, 3328, 3584
GLB = 512
EPS = 1e-6
NEG_INF = -1e30
ATT_SCALE = HEAD ** -0.5
N_DEV = 8
VMEM_LIMIT = 56 * 1024 * 1024

ADAM_LR, ADAM_B1, ADAM_B2, ADAM_EPS, ADAM_WD, ADAM_STEP = 0.001, 0.9, 0.999, 1e-08, 0.01, 10

NN = (((1,), (0,)), ((), ()))
NT = (((1,), (1,)), ((), ()))
TN = (((0,), (0,)), ((), ()))


def _dot(a, b, dims=NN):
    return lax.dot_general(a, b, dims, preferred_element_type=F32)


def _params(sem=("arbitrary",)):
    return pltpu.CompilerParams(dimension_semantics=sem, vmem_limit_bytes=VMEM_LIMIT)


def _full_spec(shape):
    nd = len(shape)
    return pl.BlockSpec(shape, lambda *_: (0,) * nd)


ANY = pl.BlockSpec(memory_space=pl.ANY)


def _ew(name, body, n, row_ins, pars, row_outs, accs=(), alias=None):
    n_ri, n_p, n_ro, n_acc = len(row_ins), len(pars), len(row_outs), len(accs)

    def kern(*refs):
        i = pl.program_id(0)
        ins = refs[:n_ri]
        ps = refs[n_ri:n_ri + n_p]
        outs = refs[n_ri + n_p:n_ri + n_p + n_ro]
        acc = refs[n_ri + n_p + n_ro:]
        if n_acc:
            @pl.when(i == 0)
            def _():
                for a in acc:
                    a[...] = jnp.zeros(a.shape, a.dtype)
        body(i, ins, ps, outs, acc)

    in_specs = [ANY if blk is None else pl.BlockSpec(blk, imap) for (_, blk, imap) in row_ins]
    in_specs += [_full_spec(p.shape) for p in pars]
    out_specs = [pl.BlockSpec(blk, imap) for (_, blk, imap) in row_outs] + [_full_spec(a.shape) for a in accs]
    out_shape = [s for (s, _, _) in row_outs] + list(accs)
    return pl.pallas_call(
        kern, name=name, grid=(n,), in_specs=in_specs, out_specs=out_specs, out_shape=out_shape,
        input_output_aliases=alias or {}, compiler_params=_params(),
    )(*[a for (a, _, _) in row_ins], *pars)


def _rowblk(width, colblk=0, roff=0, tile=TR):
    return (tile, width), (lambda i: (i + roff, colblk))


def _sds(shape, dtype):
    return jax.ShapeDtypeStruct(shape, dtype)


class _Carry:
    SAME_CORE = (1, 3, 5)

    def __init__(self, jobs):
        self.jobs = list(jobs)
        self.arrays = [a for _, a in self.jobs]
        self.out_shapes = [_sds(a.shape if kind == "scatter" else (N_DEV, *a.shape), a.dtype) for kind, a in self.jobs]
        n = len(self.jobs)
        self.scratch = [pltpu.SemaphoreType.DMA((n, 7)), pltpu.SemaphoreType.DMA((n, 7)), pltpu.SemaphoreType.DMA((n,))]

    def _setup(self, sems):
        send_sems, recv_sems, local_sems = sems
        x, y, c = _place()
        me = _lin(x, y, c)
        peers = [(x ^ ((k + 1) >> 2 & 1), y ^ ((k + 1) >> 1 & 1), c ^ ((k + 1) & 1)) for k in range(7)]

        def copy(a, k, sem_k, src, dst):
            return pltpu.make_async_remote_copy(src_ref=src, dst_ref=dst, send_sem=send_sems.at[a, sem_k],
                                                recv_sem=recv_sems.at[a, sem_k], device_id=peers[k], device_id_type=MESH)

        return me, [_lin(*p) for p in peers], copy, local_sems

    def _local(self, a, kind, ins, outs, me, local_sems):
        return pltpu.make_async_copy(ins[a].at[me] if kind == "scatter" else ins[a], outs[a].at[me], local_sems.at[a])

    def start(self, ins, outs, sems):
        me, theirs, copy, local_sems = self._setup(sems)
        for a, (kind, _) in enumerate(self.jobs):
            self._local(a, kind, ins, outs, me, local_sems).start()
            if kind == "scatter":
                for k in range(7):
                    copy(a, k, k, ins[a].at[theirs[k]], outs[a].at[me]).start()
            else:
                for k in (0,) + self.SAME_CORE:
                    copy(a, k, k, ins[a], outs[a].at[me]).start()

    def wait(self, ins, outs, sems):
        me, theirs, copy, local_sems = self._setup(sems)
        for a, (kind, _) in enumerate(self.jobs):
            if kind == "scatter":
                for k in range(7):
                    copy(a, k, k, ins[a].at[me], outs[a].at[theirs[k]]).wait_recv()
                for k in range(7):
                    copy(a, k, k, ins[a].at[theirs[k]], outs[a].at[me]).wait_send()
            else:
                for k in self.SAME_CORE:
                    blk = outs[a].at[theirs[k]]
                    copy(a, k, k, ins[a], blk).wait_recv()
                    copy(a, 0, k + 1, blk, blk).start()
                copy(a, 0, 0, ins[a], outs[a].at[theirs[0]]).wait_recv()
                for k in self.SAME_CORE:
                    copy(a, 0, k + 1, ins[a], outs[a].at[theirs[k + 1]]).wait_recv()
                for k in (0,) + self.SAME_CORE:
                    copy(a, k, k, ins[a], outs[a].at[me]).wait_send()
                for k in self.SAME_CORE:
                    blk = outs[a].at[theirs[k]]
                    copy(a, 0, k + 1, blk, blk).wait_send()
            self._local(a, kind, ins, outs, me, local_sems).wait()


def _carried(kern, carry, n_in, n_out, first, last):
    if carry is None:
        return kern
    nc = len(carry.jobs)

    def wrapped(*refs):
        ins, cin = refs[:n_in], refs[n_in:n_in + nc]
        outs, cout = refs[n_in + nc:n_in + nc + n_out], refs[n_in + nc + n_out:n_in + 2 * nc + n_out]
        scr, sems = refs[n_in + 2 * nc + n_out:-3], refs[-3:]

        @pl.when(first())
        def _():
            carry.start(cin, cout, sems)

        kern(*ins, *outs, *scr)

        @pl.when(last())
        def _():
            carry.wait(cin, cout, sems)

    return wrapped


def _carry_args(carry):
    if carry is None:
        return [], [], [], [], []
    n = len(carry.jobs)
    return [ANY] * n, carry.arrays, [ANY] * n, carry.out_shapes, carry.scratch


def _grid_ends(dims):
    first = lambda: functools.reduce(jnp.logical_and, [pl.program_id(d) == 0 for d in range(len(dims))])
    last = lambda: functools.reduce(jnp.logical_and, [pl.program_id(d) == n - 1 for d, n in enumerate(dims)])
    return first, last


def _mm_call(name, a, b, mode, out_dtype, tm, tn, rows_outer=True, single_b=False, carry=None):
    if mode == "TN":
        (K, M), N = a.shape, b.shape[1]
    else:
        (M, K), N = a.shape, (b.shape[1] if mode == "NN" else b.shape[0])
    assert M % tm == 0 and N % tn == 0, (name, M, N, K, tm, tn)
    ij = (lambda g0, g1: (g0, g1)) if rows_outer else (lambda g0, g1: (g1, g0))
    grid = (M // tm, N // tn) if rows_outer else (N // tn, M // tm)
    if mode == "TN":
        a_spec = pl.BlockSpec((K, tm), lambda g0, g1: (0, ij(g0, g1)[0]))
    else:
        a_spec = pl.BlockSpec((tm, K), lambda g0, g1: (ij(g0, g1)[0], 0))
    b_blk, b_map = ((tn, K), lambda g0, g1: (ij(g0, g1)[1], 0)) if mode == "NT" else \
                   ((K, tn), lambda g0, g1: (0, ij(g0, g1)[1]))
    b_spec = pl.BlockSpec(b_blk, b_map, pipeline_mode=pl.Buffered(1)) if single_b else pl.BlockSpec(b_blk, b_map)
    dims = {"NN": NN, "NT": NT, "TN": TN}[mode]

    def kern(a_ref, b_ref, o_ref):
        o_ref[...] = _dot(a_ref[...], b_ref[...], dims).astype(o_ref.dtype)

    ci, ca, co, cs, cscr = _carry_args(carry)
    res = pl.pallas_call(
        _carried(kern, carry, 2, 1, *_grid_ends(grid)), name=name, grid=grid, in_specs=[a_spec, b_spec] + ci,
        out_specs=[pl.BlockSpec((tm, tn), lambda g0, g1: ij(g0, g1))] + co,
        out_shape=[_sds((M, N), out_dtype)] + cs, scratch_shapes=cscr,
        compiler_params=_params(("arbitrary", "arbitrary")),
    )(a, b, *ca)
    return res[0] if carry is None else (res[0], res[1:])


def _mm_act(name, a, w, mode, out_dtype=BF16, carry=None):
    rows, K = a.shape
    N = w.shape[1] if mode == "NN" else w.shape[0]
    if K > D_FF:
        return _mm_call(name, a, w, mode, out_dtype, rows // 8, N, single_b=True, carry=carry)
    tn = N if N <= 1024 else 1408
    return _mm_call(name, a, w, mode, out_dtype, rows // 4, tn, carry=carry)


def _mm_wgrad(name, x, dy, out_dtype=BF16, carry=None):
    M = x.shape[1]
    tm = 1408 if M == D_FF else 512
    return _mm_call(name, x, dy, "TN", out_dtype, tm, dy.shape[1], single_b=True, carry=carry)


def _sigmoid(x):
    return 0.5 * jnp.tanh(0.5 * x) + 0.5


def _silu(x):
    return x * _sigmoid(x)


def _silu_grad(x):
    s = _sigmoid(x)
    return s * (1.0 + x * (1.0 - s))


_GELU_K = math.sqrt(2.0 / math.pi)


def _gelu(x):
    return 0.5 * x * (1.0 + jnp.tanh(_GELU_K * (x + 0.044715 * x * x * x)))


def _gelu_grad(x):
    t = jnp.tanh(_GELU_K * (x + 0.044715 * x * x * x))
    return 0.5 * (1.0 + t) + 0.5 * x * (1.0 - t * t) * _GELU_K * (1.0 + 3.0 * 0.044715 * x * x)


def _log_sigmoid(x):
    return jnp.minimum(x, 0.0) - jnp.log(1.0 + jnp.exp(-jnp.abs(x)))


def _rms(x):
    x = x.astype(F32)
    r = lax.rsqrt(jnp.mean(x * x, axis=-1, keepdims=True) + EPS)
    return x * r, r


def _rms_bwd(dy, y, r):
    return r * (dy - y * jnp.mean(dy * y, axis=-1, keepdims=True))


def _modrow(mod_ref, i, chunk):
    lo = mod_ref[0:1, chunk * D:(chunk + 1) * D]
    hi = mod_ref[1:2, chunk * D:(chunk + 1) * D]
    return jnp.where(i == 0, lo, hi)


def _acc_seg(acc_ref, i, val):
    zero = jnp.zeros_like(val)
    acc_ref[0:1, :] += jnp.where(i == 0, val, zero)
    acc_ref[1:2, :] += jnp.where(i == 0, zero, val)


def _colsum(x):
    return jnp.sum(x, axis=0, keepdims=True)


SH1, SC1, GA1, SH2, SC2, GA2 = range(6)


def _normmod_fwd(name, xa, g, mod, c_sh, c_sc):
    T = xa.shape[0]

    def body(i, ins, ps, outs, acc):
        y, _ = _rms(ins[0][...])
        h = (y * ps[0][...]) * (1.0 + _modrow(ps[1], i, c_sc)) + _modrow(ps[1], i, c_sh)
        outs[0][...] = h.astype(BF16)

    return _ew(name, body, T // TR, [(xa, *_rowblk(D))], [g, mod], [(_sds((T, D), BF16), *_rowblk(D))])[0]


def _modrows(mod_ref, row0, n, chunk):
    t = row0 + lax.broadcasted_iota(jnp.int32, (n, 1), 0)
    return jnp.where(t < CTX, mod_ref[0:1, chunk * D:(chunk + 1) * D], mod_ref[1:2, chunk * D:(chunk + 1) * D])


def _loss_resid_bwd(name, x_out, target, mat, gpost, mod, c_ga):
    T = x_out.shape[0]

    def body(i, ins, ps, outs, acc):
        err = ins[0][...] - ins[1][...]
        lat = i > 0
        dx = jnp.where(lat, err * (1.0 / D), 0.0)
        outs[0][...] = dx
        acc[2][...] += jnp.where(lat, _colsum(err * err), 0.0)
        outs[1][...] = _resid_bwd_vals(i, dx, ins[2][...], ps[0][...], ps[1], c_ga, acc[0], acc[1]).astype(BF16)

    tgt_blk = ((TR, D), lambda i: (jnp.maximum(i - 1, 0), 0))
    return _ew(name, body, T // TR, [(x_out, *_rowblk(D)), (target, *tgt_blk), (mat, *_rowblk(D))], [gpost, mod],
               [(_sds((T, D), F32), *_rowblk(D)), (_sds((T, D), BF16), *_rowblk(D))],
               [_sds((2, D), F32), _sds((1, D), F32), _sds((1, D), F32)])


def _resid_bwd_vals(i, dout, mat, gpost, mod_ref, c_ga, acc_ga, acc_g):
    ym, rm = _rms(mat)
    ga = _modrow(mod_ref, i, c_ga)
    _acc_seg(acc_ga, i, _colsum(dout * (ym * gpost)))
    dn = dout * ga
    acc_g[...] += _colsum(dn * ym)
    return _rms_bwd(dn * gpost, ym, rm)


def _normmod_bwd_vals(i, dh, xin, g, mod_ref, c_sh, c_sc, acc_sh, acc_sc, acc_g):
    dh = dh.astype(F32)
    y, r = _rms(xin)
    _acc_seg(acc_sc, i, _colsum(dh * (y * g)))
    _acc_seg(acc_sh, i, _colsum(dh))
    dyg = dh * (1.0 + _modrow(mod_ref, i, c_sc))
    acc_g[...] += _colsum(dyg * y)
    return _rms_bwd(dyg * g, y, r)


def _resid_bwd(name, dout, mat, gpost, mod, c_ga):
    T = dout.shape[0]

    def body(i, ins, ps, outs, acc):
        dm = _resid_bwd_vals(i, ins[0][...], ins[1][...], ps[0][...], ps[1], c_ga, acc[0], acc[1])
        outs[0][...] = dm.astype(BF16)

    return _ew(name, body, T // TR, [(dout, *_rowblk(D)), (mat, *_rowblk(D))], [gpost, mod],
               [(_sds((T, D), BF16), *_rowblk(D))], [_sds((2, D), F32), _sds((1, D), F32)])


def _normmod_resid_bwd(name, dh, xin, gpre, mod, c_sh, c_sc, dres, mat, gpost, c_ga):
    T = dh.shape[0]

    def body(i, ins, ps, outs, acc):
        dx = ins[2][...] + _normmod_bwd_vals(i, ins[0][...], ins[1][...], ps[0][...], ps[1], c_sh, c_sc,
                                             acc[0], acc[1], acc[2])
        outs[0][...] = dx
        dm = _resid_bwd_vals(i, dx, ins[3][...], ps[2][...], ps[1], c_ga, acc[3], acc[4])
        outs[1][...] = dm.astype(BF16)

    return _ew(name, body, T // TR, [(dh, *_rowblk(D)), (xin, *_rowblk(D)), (dres, *_rowblk(D)), (mat, *_rowblk(D))],
               [gpre, mod, gpost],
               [(_sds((T, D), F32), *_rowblk(D)), (_sds((T, D), BF16), *_rowblk(D))],
               [_sds((2, D), F32), _sds((2, D), F32), _sds((1, D), F32), _sds((2, D), F32), _sds((1, D), F32)])


def _normmod_bwd(name, dh, xin, gpre, mod, c_sh, c_sc, dres):
    T = dh.shape[0]

    def body(i, ins, ps, outs, acc):
        outs[0][...] = ins[2][...] + _normmod_bwd_vals(i, ins[0][...], ins[1][...], ps[0][...], ps[1], c_sh, c_sc,
                                                       acc[0], acc[1], acc[2])

    return _ew(name, body, T // TR, [(dh, *_rowblk(D)), (xin, *_rowblk(D)), (dres, *_rowblk(D))], [gpre, mod],
               [(_sds((T, D), F32), *_rowblk(D))], [_sds((2, D), F32), _sds((2, D), F32), _sds((1, D), F32)])


def _gate_bwd(name, p, ya, yb, dz):
    T = ya.shape[0]

    def kern(gl_ref, ya_ref, yb_ref, dz_ref, dya_ref, dyb_ref, dp_ref):
        j = pl.program_id(1)
        g = _sigmoid(gl_ref[...].astype(F32))
        dzv = dz_ref[...].astype(F32)
        dbranch = (dzv * g).astype(BF16)
        dg = dzv * g * (1.0 - g)

        @pl.when(j < 2)
        def _():
            dya_ref[...] = dbranch
            dp_ref[...] = (dg * ya_ref[...].astype(F32)).astype(BF16)

        @pl.when(j >= 2)
        def _():
            dyb_ref[...] = dbranch
            dp_ref[...] = (dg * yb_ref[...].astype(F32)).astype(BF16)

    rt = T // 4
    first = pl.BlockSpec((rt, GLB), lambda i, j: (i, jnp.minimum(j, 1)))
    second = pl.BlockSpec((rt, GLB), lambda i, j: (i, jnp.maximum(j - 2, 0)))
    return pl.pallas_call(
        kern, name=name, grid=(4, 4),
        in_specs=[pl.BlockSpec((rt, GLB), lambda i, j: (i, COL_GL // GLB + j)), first, second,
                  pl.BlockSpec((rt, GLB), lambda i, j: (i, j % 2))],
        out_specs=[first, second, pl.BlockSpec((rt, GLB), lambda i, j: (i, COL_GL // GLB + j))],
        out_shape=[_sds((T, D), BF16), _sds((T, D), BF16), _sds((T, P_W), BF16)],
        compiler_params=_params(("arbitrary", "arbitrary")),
    )(p, ya, yb, dz)


def _swiglu_bwd(name, fg, fu, ds):
    T = fg.shape[0]

    def body(i, ins, ps, outs, acc):
        gate, up, dsv = ins[0][...].astype(F32), ins[1][...].astype(F32), ins[2][...].astype(F32)
        dgate = dsv * up * _silu_grad(gate)
        dup = dsv * _silu(gate)
        outs[0][...] = jnp.concatenate([dgate, dup], axis=1).astype(BF16)

    return _ew(name, body, T // TR, [(fg, *_rowblk(D_FF)), (fu, *_rowblk(D_FF)), (ds, *_rowblk(D_FF))], [],
               [(_sds((T, 2 * D_FF), BF16), *_rowblk(2 * D_FF))])[0]


FT = 1408


def _ffn_in_fused(name, h2, w_t, carry=None):
    T = h2.shape[0]
    tm, nj = T // 4, D_FF // FT

    def kern(a_ref, bg_ref, bu_ref, fg_ref, fu_ref, s_ref):
        a = a_ref[...]
        g = _dot(a, bg_ref[...], NT)
        u = _dot(a, bu_ref[...], NT)
        fg_ref[...] = g.astype(BF16)
        fu_ref[...] = u.astype(BF16)
        s_ref[...] = (_silu(g) * u).astype(BF16)

    o_spec = pl.BlockSpec((tm, FT), lambda i, j: (i, j))
    ci, ca, co, cs, cscr = _carry_args(carry)
    res = pl.pallas_call(
        _carried(kern, carry, 3, 3, *_grid_ends((4, nj))), name=name, grid=(4, nj),
        in_specs=[pl.BlockSpec((tm, D), lambda i, j: (i, 0)), pl.BlockSpec((FT, D), lambda i, j: (j, 0)),
                  pl.BlockSpec((FT, D), lambda i, j: (j + nj, 0))] + ci,
        out_specs=[o_spec] * 3 + co, out_shape=[_sds((T, D_FF), BF16)] * 3 + cs, scratch_shapes=cscr,
        compiler_params=_params(("arbitrary", "arbitrary")),
    )(h2, w_t, w_t, *ca)
    return res if carry is None else (res[:3], res[3:])


def _norm_chain(i, tm, xin, mat, gpost, mod_ref, c_ga, gnext, modn_ref, c_sh, c_sc):
    ym, _ = _rms(mat.astype(BF16))
    xo = xin + _modrows(mod_ref, i * tm, tm, c_ga) * (ym * gpost)
    y, _ = _rms(xo)
    h = (y * gnext) * (1.0 + _modrows(modn_ref, i * tm, tm, c_sc)) + _modrows(modn_ref, i * tm, tm, c_sh)
    return xo, h.astype(BF16)


def _out_fused(name, p, ya, yb, xa, w_out, gpost, mod, gnext):
    T = ya.shape[0]
    tm = T // 8

    def kern(g0, g1, g2, g3, ya_ref, yb_ref, xa_ref, w_ref, gpost_ref, mod_ref, gnext_ref, z_ref, m_ref, x1_ref, h2_ref):
        i = pl.program_id(0)
        ga = _sigmoid(jnp.concatenate([g0[...], g1[...]], axis=1).astype(F32))
        gb = _sigmoid(jnp.concatenate([g2[...], g3[...]], axis=1).astype(F32))
        z = (ga * ya_ref[...].astype(F32) + gb * yb_ref[...].astype(F32)).astype(BF16)
        z_ref[...] = z
        m = _dot(z, w_ref[...])
        m_ref[...] = m.astype(BF16)
        x1_ref[...], h2_ref[...] = _norm_chain(i, tm, xa_ref[...], m, gpost_ref[...], mod_ref, GA1,
                                               gnext_ref[...], mod_ref, SH2, SC2)

    row = lambda w: pl.BlockSpec((tm, w), lambda i: (i, 0))
    return pl.pallas_call(
        kern, name=name, grid=(T // tm,),
        in_specs=[pl.BlockSpec((tm, GLB), lambda i, q=q: (i, COL_GL // GLB + q)) for q in range(4)]
                 + [row(D), row(D), row(D), _full_spec(w_out.shape), _full_spec(gpost.shape), _full_spec(mod.shape),
                    _full_spec(gnext.shape)],
        out_specs=[row(D)] * 4,
        out_shape=[_sds((T, D), BF16), _sds((T, D), BF16), _sds((T, D), F32), _sds((T, D), BF16)],
        compiler_params=_params(),
    )(p, p, p, p, ya, yb, xa, w_out, gpost, mod, gnext)


def _ffn_out_fused(name, s, w, x1, gpost, mod, nxt=None):
    T = s.shape[0]
    tm = T // 8

    def kern(s_ref, w_ref, x1_ref, gpost_ref, mod_ref, *rest):
        i = pl.program_id(0)
        e = _dot(s_ref[...], w_ref[...])
        if nxt is None:
            e_ref, xo_ref = rest
            ym, _ = _rms(e.astype(BF16))
            xo_ref[...] = x1_ref[...] + _modrows(mod_ref, i * tm, tm, GA2) * (ym * gpost_ref[...])
        else:
            gnext_ref, modn_ref, e_ref, xo_ref, h_ref = rest
            xo_ref[...], h_ref[...] = _norm_chain(i, tm, x1_ref[...], e, gpost_ref[...], mod_ref, GA2,
                                                  gnext_ref[...], modn_ref, SH1, SC1)
        e_ref[...] = e.astype(BF16)

    row = lambda w_: pl.BlockSpec((tm, w_), lambda i: (i, 0))
    extra = [] if nxt is None else list(nxt)
    return pl.pallas_call(
        kern, name=name, grid=(T // tm,),
        in_specs=[row(D_FF), _full_spec(w.shape), row(D), _full_spec(gpost.shape), _full_spec(mod.shape)]
                 + [_full_spec(a.shape) for a in extra],
        out_specs=[row(D)] * (2 if nxt is None else 3),
        out_shape=[_sds((T, D), BF16), _sds((T, D), F32)] + ([] if nxt is None else [_sds((T, D), BF16)]),
        compiler_params=_params(),
    )(s, w, x1, gpost, mod, *extra)


AB = 128
CTX_BLKS = CTX // AB


def _rope_tables(S):
    pos = jnp.arange(S, dtype=jnp.int32)
    inv = ROPE_BASE ** (-jnp.arange(N_FREQ, dtype=F32) / N_FREQ)
    ang_r = (pos // GRID_W).astype(F32)[:, None] * inv[None, :]
    ang_c = (pos % GRID_W).astype(F32)[:, None] * inv[None, :]
    cos = jnp.concatenate([jnp.cos(ang_r)] * 2 + [jnp.cos(ang_c)] * 2, axis=1)
    sin = jnp.concatenate([-jnp.sin(ang_r), jnp.sin(ang_r), -jnp.sin(ang_c), jnp.sin(ang_c)], axis=1)
    return cos, sin


def _rope(x, cos, sin):
    w = x.shape[1]
    reps = w // HEAD
    lane = lax.broadcasted_iota(jnp.int32, x.shape, 1)
    partner = jnp.where((lane & 63) < 32, pltpu.roll(x, w - 32, 1), pltpu.roll(x, 32, 1))
    return x * jnp.tile(cos, (1, reps)) + partner * jnp.tile(sin, (1, reps))


def _unrope(dx, cos, sin):
    w = dx.shape[1]
    reps = w // HEAD
    lane = lax.broadcasted_iota(jnp.int32, dx.shape, 1)
    t = dx * jnp.tile(sin, (1, reps))
    partner = jnp.where((lane & 63) < 32, pltpu.roll(t, w - 32, 1), pltpu.roll(t, 32, 1))
    return dx * jnp.tile(cos, (1, reps)) + partner


def _qkv_prep(name, p, cos, sin, S):
    T = CTX + S
    nt = T // AB
    KW = N_KV * HEAD

    def with_ones(v):
        ones = jnp.ones((AB, HEAD), BF16)
        return jnp.concatenate([v[:, kh * HEAD:(kh + 1) * HEAD] if part == 0 else ones
                                for kh in range(N_KV) for part in range(2)], axis=1)

    def kern(q_ref, k_ref, v_ref, cos_ref, sin_ref, qa_ref, kp_ref, vp_ref, kc_ref, vc_ref):
        i = pl.program_id(0)
        cos_v, sin_v = cos_ref[...], sin_ref[...]
        @pl.when(i < CTX_BLKS)
        def _():
            qa_ref[...] = (q_ref[...].astype(F32) * ATT_SCALE).astype(BF16)
            kc_ref[...] = k_ref[...]
            vc_ref[...] = with_ones(v_ref[...])

        @pl.when((i < CTX_BLKS) | (i >= nt))
        def _():
            kp_ref[...] = jnp.zeros(kp_ref.shape, BF16)
            vp_ref[...] = jnp.zeros(vp_ref.shape, BF16)

        @pl.when((i >= CTX_BLKS) & (i < nt))
        def _():
            qa_ref[...] = (_rope(q_ref[...].astype(F32), cos_v, sin_v) * ATT_SCALE).astype(BF16)
            kp_ref[...] = _rope(k_ref[...].astype(F32), cos_v, sin_v).astype(BF16)
            vp_ref[...] = with_ones(v_ref[...])

    tok = lambda i: jnp.minimum(i, nt - 1)
    lat_map = lambda i: (jnp.clip(i - CTX_BLKS, 0, nt - CTX_BLKS - 1), 0)
    ctx_map = lambda i: (jnp.minimum(i, CTX_BLKS - 1), 0)
    return pl.pallas_call(
        kern, name=name, grid=(nt + CTX_BLKS,),
        in_specs=[pl.BlockSpec((AB, N_Q * HEAD), lambda i: (tok(i), COL_Q // (N_Q * HEAD))),
                  pl.BlockSpec((AB, KW), lambda i: (tok(i), COL_K // KW)),
                  pl.BlockSpec((AB, KW), lambda i: (tok(i), COL_V // KW)),
                  pl.BlockSpec((AB, HEAD), lat_map), pl.BlockSpec((AB, HEAD), lat_map)],
        out_specs=[pl.BlockSpec((AB, N_Q * HEAD), lambda i: (tok(i), 0)),
                   pl.BlockSpec((AB, KW), lambda i: (i, 0)), pl.BlockSpec((AB, 2 * KW), lambda i: (i, 0)),
                   pl.BlockSpec((AB, KW), ctx_map), pl.BlockSpec((AB, 2 * KW), ctx_map)],
        out_shape=[_sds((T, N_Q * HEAD), BF16), _sds((S + 2 * CTX, KW), BF16), _sds((S + 2 * CTX, 2 * KW), BF16),
                   _sds((CTX, KW), BF16), _sds((CTX, 2 * KW), BF16)],
        compiler_params=_params(),
    )(p, p, p, cos, sin)


GW = Q_PER_KV * HEAD


def _band_bias(S):
    r = jnp.arange(AB, dtype=jnp.int32)[:, None]
    c = jnp.arange(3 * AB, dtype=jnp.int32)[None, :]
    near = jnp.abs(c - AB - r) <= AB
    valid = jnp.stack([near & (c >= AB), near, near & (c < 2 * AB)])
    return jnp.where(valid, 0.0, NEG_INF).astype(F32)


def _bias_spec(S):
    nb = S // AB
    return pl.BlockSpec((None, AB, 3 * AB), lambda kh, n: (jnp.where(n == 0, 0, jnp.where(n == nb - 1, 2, 1)), 0, 0))


def _head_probs(q, sink, kc, vce, kb, vbe, bias):
    s_c = _dot(q, kc, NT)
    m = jnp.maximum(jnp.max(s_c, axis=-1, keepdims=True), sink)
    if kb is not None:
        s_b = _dot(q, kb, NT) + bias
        m = jnp.maximum(m, jnp.max(s_b, axis=-1, keepdims=True))
    p_c = jnp.exp(s_c - m).astype(BF16)
    acc = _dot(p_c, vce)
    p_b = None
    if kb is not None:
        p_b = jnp.exp(s_b - m).astype(BF16)
        acc = acc + _dot(p_b, vbe)
    return p_c, p_b, m, acc


def _attn_fwd(name, qa, kc, vc, sink4, S, band=None, prev=None, carry=None):
    T = qa.shape[0]
    has_band = band is not None
    nq = S // AB if has_band else CTX_BLKS
    q_off = CTX_BLKS if has_band else 0

    def kern(*refs):
        q_ref, kc_ref, vc_ref, sink_ref = refs[:4]
        rest = refs[4:]
        o_ref = rest[-1]
        n = pl.program_id(1)
        kc_v, vce = kc_ref[...], vc_ref[...]
        kb = vbe = bias = None
        if has_band:
            kp_ref, vp_ref, bias_ref = rest[:3]
            start = pl.multiple_of(n * AB + (CTX - AB), AB)
            kb = kp_ref[pl.ds(start, 3 * AB), :]
            vbe = vp_ref[pl.ds(start, 3 * AB), :]
            bias = bias_ref[...]
        outs = []
        for g in range(Q_PER_KV):
            sink = sink_ref[g:g + 1, 0:1]
            _, _, m, acc = _head_probs(q_ref[:, g * HEAD:(g + 1) * HEAD], sink, kc_v, vce, kb, vbe, bias)
            l = acc[:, HEAD:] + jnp.exp(sink - m)
            outs.append(acc[:, :HEAD] / l)
        o_ref[...] = jnp.concatenate(outs, axis=1).astype(BF16)

    in_specs = [pl.BlockSpec((AB, GW), lambda kh, n: (n + q_off, kh)),
                pl.BlockSpec((CTX, HEAD), lambda kh, n: (0, kh)), pl.BlockSpec((CTX, 2 * HEAD), lambda kh, n: (0, kh)),
                pl.BlockSpec((None, Q_PER_KV, HEAD), lambda kh, n: (kh, 0, 0))]
    args = [qa, kc, vc, sink4]
    if has_band:
        in_specs += [pl.BlockSpec((S + 2 * CTX, HEAD), lambda kh, n: (0, kh)),
                     pl.BlockSpec((S + 2 * CTX, 2 * HEAD), lambda kh, n: (0, kh)), _bias_spec(S)]
        args += list(band)
    alias = {}
    if prev is not None:
        in_specs.append(ANY)
        alias = {len(args): 0}
        args.append(prev)
    ci, ca, co, cs, cscr = _carry_args(carry)
    res = pl.pallas_call(
        _carried(kern, carry, len(args), 1, *_grid_ends((N_KV, nq))), name=name, grid=(N_KV, nq),
        in_specs=in_specs + ci,
        out_specs=[pl.BlockSpec((AB, GW), lambda kh, n: (n + q_off, kh))] + co,
        out_shape=[_sds((T, N_Q * HEAD), BF16)] + cs, input_output_aliases=alias, scratch_shapes=cscr,
        compiler_params=_params(("arbitrary", "arbitrary")),
    )(*args, *ca)
    return res[0] if carry is None else (res[0], res[1:])


def _attn_bwd(name, qa, kc, vc, sink4, o_all, do_all, S, band=None, prev_dq=None, carry=None):
    T = qa.shape[0]
    has_band = band is not None
    nq = S // AB if has_band else CTX_BLKS
    q_off = CTX_BLKS if has_band else 0
    KW = N_KV * HEAD

    def kern(*refs):
        q_ref, kc_ref, vc_ref, sink_ref, o_ref, do_ref = refs[:6]
        rest = refs[6:]
        if has_band:
            kp_ref, vp_ref, bias_ref = rest[:3]
            rest = rest[3:]
        if prev_dq is not None:
            rest = rest[1:]
        dq_ref, dkc_ref, dvc_ref, dsink_ref = rest[:4]
        n = pl.program_id(1)

        @pl.when(n == 0)
        def _():
            dkc_ref[...] = jnp.zeros(dkc_ref.shape, F32)
            dvc_ref[...] = jnp.zeros(dvc_ref.shape, F32)
            dsink_ref[...] = jnp.zeros(dsink_ref.shape, F32)
            if has_band:
                rest[4][...] = jnp.zeros(rest[4].shape, F32)
                rest[5][...] = jnp.zeros(rest[5].shape, F32)

        kc_v, vce = kc_ref[...], vc_ref[...]
        vc_v = vce[:, :HEAD]
        kb = vbe = vb = bias = None
        if has_band:
            start = pl.multiple_of(n * AB + (CTX - AB), AB)
            kb = kp_ref[pl.ds(start, 3 * AB), :]
            vbe = vp_ref[pl.ds(start, 3 * AB), :]
            vb = vbe[:, :HEAD]
            bias = bias_ref[...]
        stack = lambda ref: jnp.concatenate([ref[:, g * HEAD:(g + 1) * HEAD] for g in range(Q_PER_KV)], axis=0)
        q4, do4 = stack(q_ref), stack(do_ref)
        sink = jnp.concatenate([jnp.broadcast_to(sink_ref[g:g + 1, 0:1], (AB, 1)) for g in range(Q_PER_KV)], axis=0)
        s_c = _dot(q4, kc_v, NT)
        m = jnp.maximum(jnp.max(s_c, axis=-1, keepdims=True), sink)
        if has_band:
            s_b = _dot(q4, kb, NT) + jnp.tile(bias, (Q_PER_KV, 1))
            m = jnp.maximum(m, jnp.max(s_b, axis=-1, keepdims=True))
        p_c = jnp.exp(s_c - m).astype(BF16).astype(F32)
        p_sink = jnp.exp(sink - m)
        l = jnp.sum(p_c, axis=-1, keepdims=True) + p_sink
        if has_band:
            p_b = jnp.exp(s_b - m).astype(BF16).astype(F32)
            l = l + jnp.sum(p_b, axis=-1, keepdims=True)
        inv = 1.0 / l
        delta = jnp.sum(do4.astype(F32) * stack(o_ref).astype(F32), axis=-1, keepdims=True)
        do4b = do4.astype(BF16)
        pn_c = (p_c * inv).astype(BF16)
        ds_c = (p_c * inv * (_dot(do4b, vc_v, NT) - delta)).astype(BF16)
        dq4 = _dot(ds_c, kc_v)
        dkc_ref[...] += _dot(ds_c, q4, TN)
        dvc_ref[...] += _dot(pn_c, do4b, TN)
        if has_band:
            pn_b = (p_b * inv).astype(BF16)
            ds_b = (p_b * inv * (_dot(do4b, vb, NT) - delta)).astype(BF16)
            dq4 = dq4 + _dot(ds_b, kb)
            rest[4][pl.ds(start, 3 * AB), :] += _dot(ds_b, q4, TN)
            rest[5][pl.ds(start, 3 * AB), :] += _dot(pn_b, do4b, TN)
        dq4 = dq4 * ATT_SCALE
        dq_ref[...] = jnp.concatenate([dq4[g * AB:(g + 1) * AB, :] for g in range(Q_PER_KV)], axis=1)
        ps = p_sink * inv * delta
        dsink_ref[...] += jnp.concatenate(
            [jnp.broadcast_to(-jnp.sum(ps[g * AB:(g + 1) * AB, :], axis=0, keepdims=True), (1, HEAD))
             for g in range(Q_PER_KV)], axis=0)

    q_spec = pl.BlockSpec((AB, GW), lambda kh, n: (n + q_off, kh))
    c_spec = pl.BlockSpec((CTX, HEAD), lambda kh, n: (0, kh))
    ce_spec = pl.BlockSpec((CTX, 2 * HEAD), lambda kh, n: (0, kh))
    s_spec = pl.BlockSpec((None, Q_PER_KV, HEAD), lambda kh, n: (kh, 0, 0))
    in_specs = [q_spec, c_spec, ce_spec, s_spec, q_spec, q_spec]
    args = [qa, kc, vc, sink4, o_all, do_all]
    out_specs = [q_spec, c_spec, c_spec, s_spec]
    out_shape = [_sds((T, N_Q * HEAD), F32), _sds((CTX, KW), F32), _sds((CTX, KW), F32), _sds((N_KV, Q_PER_KV, HEAD), F32)]
    if has_band:
        p_spec = pl.BlockSpec((S + 2 * CTX, HEAD), lambda kh, n: (0, kh))
        in_specs += [p_spec, pl.BlockSpec((S + 2 * CTX, 2 * HEAD), lambda kh, n: (0, kh)), _bias_spec(S)]
        args += list(band)
        out_specs += [p_spec, p_spec]
        out_shape += [_sds((S + 2 * CTX, KW), F32)] * 2
    alias = {}
    if prev_dq is not None:
        in_specs.append(ANY)
        alias = {len(args): 0}
        args.append(prev_dq)
    ci, ca, co, cs, cscr = _carry_args(carry)
    n_out = len(out_specs)
    res = pl.pallas_call(
        _carried(kern, carry, len(args), n_out, *_grid_ends((N_KV, nq))), name=name, grid=(N_KV, nq),
        in_specs=in_specs + ci, out_specs=out_specs + co, out_shape=out_shape + cs, scratch_shapes=cscr,
        input_output_aliases=alias, compiler_params=_params(("arbitrary", "arbitrary")),
    )(*args, *ca)
    return res if carry is None else (res[:n_out], res[n_out:])


def _dqkv_assemble(name, dp, dq_all, dkp, dvp, dkc_l, dvc_l, dkc_c, dvc_c, cos, sin, S):
    T = CTX + S
    KW = N_KV * HEAD
    HALF = N_Q * HEAD // 2

    def kern(dq_ref, dkp_ref, dvp_ref, dkcl_ref, dvcl_ref, dkcc_ref, dvcc_ref, cos_ref, sin_ref, dp_in, out_ref):
        i = pl.program_id(0)
        j = pl.program_id(1)
        cos_v, sin_v = cos_ref[...], sin_ref[...]

        @pl.when((j < 2) & (i == 0))
        def _():
            out_ref[...] = dq_ref[...].astype(BF16)

        @pl.when((j < 2) & (i > 0))
        def _():
            out_ref[...] = _unrope(dq_ref[...], cos_v, sin_v).astype(BF16)

        @pl.when((j == 2) & (i == 0))
        def _():
            out_ref[...] = jnp.concatenate([dkcl_ref[...] + dkcc_ref[...], dvcl_ref[...] + dvcc_ref[...]],
                                           axis=1).astype(BF16)

        @pl.when((j == 2) & (i > 0))
        def _():
            out_ref[...] = jnp.concatenate([_unrope(dkp_ref[...], cos_v, sin_v), dvp_ref[...]], axis=1).astype(BF16)

    same = lambda i, j: (i, 0)
    lat_map = lambda i, j: (jnp.maximum(i - 1, 0), 0)
    ctx_map = lambda i, j: (0, 0)
    return pl.pallas_call(
        kern, name=name, grid=(T // TR, 3),
        in_specs=[pl.BlockSpec((TR, HALF), lambda i, j: (i, jnp.minimum(j, 1))),
                  pl.BlockSpec((TR, KW), same), pl.BlockSpec((TR, KW), same),
                  pl.BlockSpec((CTX, KW), ctx_map), pl.BlockSpec((CTX, KW), ctx_map),
                  pl.BlockSpec((CTX, KW), ctx_map), pl.BlockSpec((CTX, KW), ctx_map),
                  pl.BlockSpec((TR, HEAD), lat_map), pl.BlockSpec((TR, HEAD), lat_map), ANY],
        out_specs=pl.BlockSpec((TR, HALF), lambda i, j: (i, COL_Q // HALF + j)),
        out_shape=_sds((T, P_W), BF16), input_output_aliases={9: 0},
        compiler_params=_params(("arbitrary", "arbitrary")),
    )(dq_all, dkp, dvp, dkc_l, dvc_l, dkc_c, dvc_c, cos, sin, dp)


RB = 128
CH = 256
HALO = 8
SUB = 8
GRP = 8


def _vscan(a, b, reverse):
    row = lax.broadcasted_iota(jnp.int32, a.shape, 0)
    A, H = a, b
    for s in (1, 2, 4):
        sh = SUB - s if reverse else s
        m = (row < SUB - s) if reverse else (row >= s)
        As = pltpu.roll(A, sh, 0)
        Hs = pltpu.roll(H, sh, 0)
        H = jnp.where(m, A * Hs + H, H)
        A = jnp.where(m, A * As, A)
    return A, H


def _scan_rows(a_ref, b_ref, r0, nrows, reverse, carry, emit):
    ngrp = nrows // (SUB * GRP)
    row = lax.broadcasted_iota(jnp.int32, (SUB, RB), 0)

    def grp(gi, carry):
        g = (ngrp - 1 - gi) if reverse else gi
        base = r0 + g * (SUB * GRP)
        for v in (range(GRP - 1, -1, -1) if reverse else range(GRP)):
            rs = pl.multiple_of(base + v * SUB, SUB)
            A, H = _vscan(a_ref[pl.ds(rs, SUB), :], b_ref[pl.ds(rs, SUB), :], reverse)
            hf = H + A * carry
            if reverse:
                before = jnp.where(row == SUB - 1, carry, pltpu.roll(hf, SUB - 1, 0))
                carry = hf[0:1, :]
            else:
                before = jnp.where(row == 0, carry, pltpu.roll(hf, 1, 0))
                carry = hf[SUB - 1:SUB, :]
            emit(rs, hf, before)
        return carry

    return lax.fori_loop(0, ngrp, grp, carry)


def _pad_start(ci):
    return pl.multiple_of(ci * CH + HALO * jnp.minimum(ci, 1), HALO)


def _conv_taps(ext, transpose=False):
    n = CH + 2 * HALO
    taps = []
    for k in range(CONV_W):
        off = CONV_LEFT - k if transpose else k - CONV_LEFT
        taps.append(ext[HALO:HALO + CH, :] if off == 0 else pltpu.roll(ext, (-off) % n, 0)[HALO:HALO + CH, :])
    return taps


def _lru_gates(xl, w4, b4, ls):
    pre = _dot(xl.astype(BF16), w4) + b4
    out = []
    for d in range(2):
        r = _sigmoid(pre[:, d * RB:(d + 1) * RB])
        i = _sigmoid(pre[:, (2 + d) * RB:(3 + d) * RB])
        la = LRU_C * r * ls[d:d + 1, :]
        a = jnp.exp(la)
        q = -jnp.tanh(la) * (1.0 + a * a)
        out.append((r, i, a, q))
    return out


def _rnn_specs(T):
    col = lambda n, *_: (0, n)
    return dict(
        xr=pl.BlockSpec((T, RB), lambda n, *_: (0, COL_XR // RB + n)),
        gr=pl.BlockSpec((T, RB), lambda n, *_: (0, COL_GR // RB + n)),
        act=pl.BlockSpec((T, RB), col),
        cw=pl.BlockSpec((CONV_W, RB), col), cb=pl.BlockSpec((1, RB), col),
        w4=pl.BlockSpec((None, RB, 4 * RB), lambda n, *_: (n, 0, 0)),
        b4=pl.BlockSpec((None, 1, 4 * RB), lambda n, *_: (n, 0, 0)),
        lam=pl.BlockSpec((2, RB), col))


PAD_ROWS = 3 * HALO


def _zero_pads(pad_ref, T):
    for r in (0, HALO + CTX, 2 * HALO + T):
        pad_ref[r:r + HALO, :] = jnp.zeros((HALO, RB), F32)


def _fill_padded(pad_ref, src_ref, T):
    _zero_pads(pad_ref, T)
    pad_ref[HALO:HALO + CTX, :] = src_ref[0:CTX, :].astype(F32)
    pad_ref[2 * HALO + CTX:2 * HALO + T, :] = src_ref[CTX:T, :].astype(F32)


def _pad_rows(ci):
    return pl.ds(pl.multiple_of(ci * CH + HALO + HALO * jnp.minimum(ci, 1), HALO), CH)


def _rnn_fwd(name, p, cw, cb, w4, b4, lam, T, carry=None):
    def kern(xr_ref, gr_ref, cw_ref, cb_ref, w4_ref, b4_ref, lam_ref,
             u_ref, a0, a1, yo_ref, hpf_ref, hpb_ref, r0_ref, r1_ref, i0_ref, i1_ref, xpad, b0, b1, y):
        _fill_padded(xpad, xr_ref, T)
        ls = _log_sigmoid(lam_ref[...])
        w4v, b4v, cwv, cbv = w4_ref[...], b4_ref[...], cw_ref[...], cb_ref[...]

        def chunk(ci, _):
            rows = pl.ds(pl.multiple_of(ci * CH, CH), CH)
            taps = _conv_taps(xpad[pl.ds(_pad_start(ci), CH + 2 * HALO), :])
            xl = cbv + sum(taps[k] * cwv[k:k + 1, :] for k in range(CONV_W))
            for d, (r, i, a, q) in enumerate(_lru_gates(xl, w4v, b4v, ls)):
                (a0, a1)[d][rows, :] = a
                (b0, b1)[d][rows, :] = jnp.sqrt(q) * (i * xl)
                (r0_ref, r1_ref)[d][rows, :] = r.astype(BF16)
                (i0_ref, i1_ref)[d][rows, :] = i.astype(BF16)
            return 0

        lax.fori_loop(0, T // CH, chunk, 0)
        zero = jnp.zeros((1, RB), F32)

        def emit_f(rs, hf, before):
            y[pl.ds(rs, SUB), :] = hf
            b0[pl.ds(rs, SUB), :] = before

        def emit_b(rs, hf, before):
            y[pl.ds(rs, SUB), :] += hf
            b1[pl.ds(rs, SUB), :] = before

        _scan_rows(a0, b0, 0, T, False, zero, emit_f)
        c = _scan_rows(a1, b1, 0, CTX, True, zero, emit_b)
        _scan_rows(a1, b1, CTX, T - CTX, True, c, emit_b)

        def finish(ci, _):
            rows = pl.ds(pl.multiple_of(ci * CH, CH), CH)
            yv = y[rows, :]
            u_ref[rows, :] = (yv * _gelu(gr_ref[rows, :].astype(F32))).astype(BF16)
            yo_ref[rows, :] = yv.astype(BF16)
            hpf_ref[rows, :] = b0[rows, :].astype(BF16)
            hpb_ref[rows, :] = b1[rows, :].astype(BF16)
            return 0

        lax.fori_loop(0, T // CH, finish, 0)

    sp = _rnn_specs(T)
    ci, ca, co, cs, cscr = _carry_args(carry)
    dts = [BF16, F32, F32] + [BF16] * 7
    res = pl.pallas_call(
        _carried(kern, carry, 7, 10, *_grid_ends((N_RNN_BLOCKS,))), name=name, grid=(N_RNN_BLOCKS,),
        in_specs=[sp["xr"], sp["gr"], sp["cw"], sp["cb"], sp["w4"], sp["b4"], sp["lam"]] + ci,
        out_specs=[sp["act"]] * 10 + co,
        out_shape=[_sds((T, D), dt) for dt in dts] + cs,
        scratch_shapes=[pltpu.VMEM((T + PAD_ROWS, RB), F32)] + [pltpu.VMEM((T, RB), F32)] * 3 + cscr,
        compiler_params=_params(),
    )(p, p, cw, cb, w4, b4, lam, *ca)
    return res if carry is None else (res[:10], res[10:])


def _rnn_bwd(name, p, du, saved, dp, cw, cb, w4, b4, lam, T, carry=None):
    def kern(xr_ref, gr_ref, du_ref, a0, a1, y_ref, hpf_ref, hpb_ref, r0_ref, r1_ref, i0_ref, i1_ref,
             cw_ref, cb_ref, w4_ref, b4_ref, lam_ref, dp_in,
             dp_ref, dcw_ref, dcb_ref, dw4_ref, db4_ref, dlam_ref,
             xpad, dxpad, c0, c1, dy, dgr_ref):
        j = pl.program_id(1)

        @pl.when(j == 0)
        def _():
            work(xr_ref, gr_ref, du_ref, a0, a1, y_ref, (hpf_ref, hpb_ref), (r0_ref, r1_ref), (i0_ref, i1_ref),
                 cw_ref, cb_ref, w4_ref, lam_ref, dp_ref, dgr_ref, dcw_ref, dcb_ref, dw4_ref, db4_ref, dlam_ref,
                 xpad, dxpad, c0, c1, dy)

        @pl.when(j == 1)
        def _():
            dp_ref[...] = dgr_ref[...]

    def work(xr_ref, gr_ref, du_ref, a0, a1, y_ref, hp_refs, r_refs, i_refs, cw_ref, cb_ref, w4_ref, lam_ref,
             dxr_ref, dgr_ref, dcw_ref, dcb_ref, dw4_ref, db4_ref, dlam_ref, xpad, dxpad, c0, c1, dy):
        _fill_padded(xpad, xr_ref, T)
        _zero_pads(dxpad, T)
        lam_v = lam_ref[...]
        ls = _log_sigmoid(lam_v)
        w4v, cwv, cbv = w4_ref[...], cw_ref[...], cb_ref[...]

        def conv_chunk(ci):
            taps = _conv_taps(xpad[pl.ds(_pad_start(ci), CH + 2 * HALO), :])
            return taps, cbv + sum(taps[k] * cwv[k:k + 1, :] for k in range(CONV_W))

        def phase_a(ci, _):
            rows = pl.ds(pl.multiple_of(ci * CH, CH), CH)
            gr = gr_ref[rows, :].astype(F32)
            duv = du_ref[rows, :].astype(F32)
            dyv = duv * _gelu(gr)
            dgr_ref[rows, :] = (duv * y_ref[rows, :].astype(F32) * _gelu_grad(gr)).astype(BF16)
            dy[rows, :] = dyv
            c0[rows, :] = a0[rows, :] * dyv
            c1[rows, :] = a1[rows, :] * dyv
            return 0

        lax.fori_loop(0, T // CH, phase_a, 0)
        zero = jnp.zeros((1, RB), F32)

        def emit0(rs, hf, before):
            c0[pl.ds(rs, SUB), :] = dy[pl.ds(rs, SUB), :] + before

        def emit1(rs, hf, before):
            c1[pl.ds(rs, SUB), :] = dy[pl.ds(rs, SUB), :] + before

        _scan_rows(a0, c0, 0, T, True, zero, emit0)
        c = _scan_rows(a1, c1, CTX, T - CTX, False, zero, emit1)
        _scan_rows(a1, c1, 0, CTX, False, c, emit1)

        dw4_ref[...] = jnp.zeros(dw4_ref.shape, F32)
        db4_ref[...] = jnp.zeros(db4_ref.shape, F32)
        dlam_ref[...] = jnp.zeros(dlam_ref.shape, F32)
        dcw_ref[...] = jnp.zeros(dcw_ref.shape, F32)
        dcb_ref[...] = jnp.zeros(dcb_ref.shape, F32)

        def phase_c(ci, _):
            base = pl.multiple_of(ci * CH, CH)
            rows = pl.ds(base, CH)
            _, xl = conv_chunk(ci)
            dxl = jnp.zeros((CH, RB), F32)
            dpre_a, dpre_x, dls = [], [], []
            for d in range(2):
                a = (a0, a1)[d][rows, :]
                r = r_refs[d][rows, :].astype(F32)
                i = i_refs[d][rows, :].astype(F32)
                q = -jnp.tanh(LRU_C * r * ls[d:d + 1, :]) * (1.0 + a * a)
                g = (c0, c1)[d][rows, :]
                hp = hp_refs[d][rows, :].astype(F32)
                gm = g * jnp.sqrt(q)
                di = gm * xl
                dxl = dxl + gm * i
                dla = a * (g * hp - a * (g * (i * xl)) * lax.rsqrt(q))
                dr = dla * (LRU_C * ls[d:d + 1, :])
                dls.append(_colsum(dla * (LRU_C * r)))
                dpre_a.append(dr * r * (1.0 - r))
                dpre_x.append(di * i * (1.0 - i))
            dpre = jnp.concatenate(dpre_a + dpre_x, axis=1)
            dpre_b = dpre.astype(BF16)
            dxl = dxl + _dot(dpre_b, w4v, NT)
            dw4_ref[...] += _dot(xl.astype(BF16), dpre_b, TN)
            db4_ref[...] += _colsum(dpre)
            dlam_ref[...] += jnp.concatenate(dls, axis=0)
            dcb_ref[...] += _colsum(dxl)
            dxpad[_pad_rows(ci), :] = dxl
            return 0

        lax.fori_loop(0, T // CH, phase_c, 0)
        dlam_ref[...] = dlam_ref[...] * _sigmoid(-lam_v)

        def phase_d(ci, _):
            base = pl.multiple_of(ci * CH, CH)
            rows = pl.ds(base, CH)
            xtaps, _ = conv_chunk(ci)
            dtaps = _conv_taps(dxpad[pl.ds(_pad_start(ci), CH + 2 * HALO), :], transpose=True)
            dxl = dxpad[_pad_rows(ci), :]
            dxr_ref[rows, :] = sum(dtaps[k] * cwv[k:k + 1, :] for k in range(CONV_W)).astype(BF16)
            dcw_ref[...] += jnp.concatenate([_colsum(dxl * xtaps[k]) for k in range(CONV_W)], axis=0)
            return 0

        lax.fori_loop(0, T // CH, phase_d, 0)

    sp = _rnn_specs(T)
    dp_spec = pl.BlockSpec((T, RB), lambda n, j: (0, COL_XR // RB + n + j * (COL_GR - COL_XR) // RB))
    ci, ca, co, cs, cscr = _carry_args(carry)
    n_in = 3 + len(saved) + 5 + 1
    res = pl.pallas_call(
        _carried(kern, carry, n_in, 6, *_grid_ends((N_RNN_BLOCKS, 2))), name=name, grid=(N_RNN_BLOCKS, 2),
        in_specs=[sp["xr"], sp["gr"]] + [sp["act"]] * (1 + len(saved)) + [sp["cw"], sp["cb"], sp["w4"], sp["b4"],
                                                                           sp["lam"], ANY] + ci,
        out_specs=[dp_spec, sp["cw"], sp["cb"], sp["w4"], sp["b4"], sp["lam"]] + co,
        out_shape=[_sds((T, P_W), BF16), _sds((CONV_W, D), F32), _sds((1, D), F32),
                   _sds((N_RNN_BLOCKS, RB, 4 * RB), F32), _sds((N_RNN_BLOCKS, 1, 4 * RB), F32), _sds((2, D), F32)] + cs,
        scratch_shapes=([pltpu.VMEM((T + PAD_ROWS, RB), F32)] * 2 + [pltpu.VMEM((T, RB), F32)] * 3
                        + [pltpu.VMEM((T, RB), BF16)] + cscr),
        input_output_aliases={n_in - 1: 0},
        compiler_params=_params(("arbitrary", "arbitrary")),
    )(p, p, du, *saved, cw, cb, w4, b4, lam, dp, *ca)
    return res if carry is None else (res[:6], res[6:])


class _Plan:
    def __init__(self, shards, Ws):
        L = len(Ws)
        self.shards, self.Ws = shards, Ws
        self.Gs = [None] * L
        self.slots = [dict() for _ in range(L)]
        self.gate_slots = [None] * L
        self.table = {}
        for l in range(L):
            t = f"l{l}_"
            self.table[t + "proj"] = [("gather", l, k) for k in ("wo_rnn", "wo_attn", "wout")]
            self.table[t + "rnn_fwd"] = [("gather", l, "wffn_in_t")]
            self.table[t + "attn_lat_fwd"] = [("gather", l + 1, "win_t")] if l + 1 < L else []
            self.table[t + "ffn_in"] = [("gather", l, "wffn_out")]
            self.table[t + "ffn_in_dx"] = [("scatter", l, "wffn_out")]
            self.table[t + "attn_lat_bwd"] = [("scatter", l, "wffn_in_t")]
            self.table[t + "ffn_in_dw"] = [("gates", l + 1, "w4")] if l + 1 < L else []
            self.table[t + "rnn_bwd"] = ([("scatter", l, k) for k in ("wout", "wo_attn", "wo_rnn")]
                                         + ([("scatter", l + 1, "win_t")] if l + 1 < L else []))
        self.table["l0_proj_dx"] = [("scatter", 0, "win_t_a")]
        self.table["l0_proj_dw_b"] = [("gates", 0, "w4")]

    def carry(self, name):
        jobs = []
        for kind, l, k in self.table.get(name, []):
            if kind == "gather":
                jobs.append(("gather", self.shards[l][k]))
            elif kind == "scatter":
                jobs.append(("scatter", self.Gs[l][k].reshape(N_DEV, -1, self.Gs[l][k].shape[-1])))
            else:
                jobs.append(("gather", self.Gs[l]["w4"].reshape(N_RNN_BLOCKS * RB, 4 * RB).astype(BF16)))
        return _Carry(jobs) if jobs else None

    def done(self, name, got):
        for (kind, l, k), res in zip(self.table[name], got):
            if kind == "gather":
                self.Ws[l][k] = res.reshape(-1, D)
            elif kind == "scatter":
                self.slots[l][k] = res
            else:
                self.gate_slots[l] = res


def _run(X, fn, name, *args, **kw):
    carry = None if X is None else X.carry(name)
    if carry is None:
        return fn(name, *args, **kw)
    out, got = fn(name, *args, carry=carry, **kw)
    X.done(name, got)
    return out


def _layer_fwd(l, xa, h, W, rope, S, nxt, X=None):
    T = xa.shape[0]
    tag = f"l{l}_"
    cos, sin, bias = rope
    p = _run(X, _mm_act, tag + "proj", h, W["win_t"], "NT", BF16)
    u, *rnn_saved = _run(X, _rnn_fwd, tag + "rnn_fwd", p, W["cw"], W["cb"], W["w4"], W["b4"], W["lam"], T)
    qa, kp, vp, kc, vc = _qkv_prep(tag + "qkv_prep", p, cos, sin, S)
    o_all = _attn_fwd(tag + "attn_ctx_fwd", qa, kc, vc, W["sink4"], S)
    o_all = _run(X, _attn_fwd, tag + "attn_lat_fwd", qa, kc, vc, W["sink4"], S, band=(kp, vp, bias), prev=o_all)
    ya = _mm_act(tag + "o_rnn", u, W["wo_rnn"], "NN")
    yb = _mm_act(tag + "o_attn", o_all, W["wo_attn"], "NN")
    z, m, x1, h2 = _out_fused(tag + "out", p, ya, yb, xa, W["wout"], W["g_mix_post"], W["mod"], W["g_ffn_pre"])
    fg, fu, s = _run(X, _ffn_in_fused, tag + "ffn_in", h2, W["wffn_in_t"])
    e, *out = _ffn_out_fused(tag + "ffn_out", s, W["wffn_out"], x1, W["g_ffn_post"], W["mod"], nxt)
    saved = dict(xa=xa, h=h, p=p, u=u, rnn=rnn_saved, qa=qa, kp=kp, vp=vp, kc=kc, vc=vc, o_all=o_all,
                 ya=ya, yb=yb, z=z, m=m, x1=x1, h2=h2, fg=fg, fu=fu, s=s, e=e)
    return saved, out


def _layer_bwd(l, dx2, A, W, rope, S, X=None, loss_of=None):
    T = A["xa"].shape[0]
    tag = f"l{l}_"
    cos, sin, bias = rope
    G = {}
    if X is not None:
        X.Gs[l] = G
    if loss_of is None:
        de, dga2, G["g_ffn_post"] = _resid_bwd(tag + "ffn_resid_bwd", dx2, A["e"], W["g_ffn_post"], W["mod"], GA2)
    else:
        dx2, de, dga2, G["g_ffn_post"], G["sq"] = _loss_resid_bwd(tag + "loss_ffn_resid_bwd", *loss_of, A["e"],
                                                                  W["g_ffn_post"], W["mod"], GA2)
    ds = _mm_act(tag + "ffn_out_dx", de, W["wffn_out"], "NT", BF16)
    G["wffn_out"] = _mm_wgrad(tag + "ffn_out_dw", A["s"], de)
    df = _swiglu_bwd(tag + "swiglu_bwd", A["fg"], A["fu"], ds)
    dh2 = _run(X, _mm_act, tag + "ffn_in_dx", df, W["wffn_in_t"], "NN")
    G["wffn_in_t"] = _run(X, _mm_wgrad, tag + "ffn_in_dw", df, A["h2"])
    dx1, dm, dsh2, dsc2, G["g_ffn_pre"], dga1, G["g_mix_post"] = _normmod_resid_bwd(
        tag + "mix_resid_bwd", dh2, A["x1"], W["g_ffn_pre"], W["mod"], SH2, SC2, dx2, A["m"], W["g_mix_post"], GA1)
    dz = _mm_act(tag + "out_dx", dm, W["wout"], "NT")
    G["wout"] = _mm_wgrad(tag + "out_dw", A["z"], dm)
    dya, dyb, dp = _gate_bwd(tag + "gate_bwd", A["p"], A["ya"], A["yb"], dz)
    do = _mm_act(tag + "o_attn_dx", dyb, W["wo_attn"], "NT")
    G["wo_attn"] = _mm_wgrad(tag + "o_attn_dw", A["o_all"], dyb)
    du = _mm_act(tag + "o_rnn_dx", dya, W["wo_rnn"], "NT")
    G["wo_rnn"] = _mm_wgrad(tag + "o_rnn_dw", A["u"], dya)
    dq_all, dkc_c, dvc_c, dsink_c = _attn_bwd(tag + "attn_ctx_bwd", A["qa"], A["kc"], A["vc"], W["sink4"],
                                               A["o_all"], do, S)
    dq_all, dkc_l, dvc_l, dsink_l, dkp, dvp = _run(
        X, _attn_bwd, tag + "attn_lat_bwd", A["qa"], A["kc"], A["vc"], W["sink4"], A["o_all"], do, S,
        band=(A["kp"], A["vp"], bias), prev_dq=dq_all)
    G["sink4"] = dsink_c + dsink_l
    dp = _dqkv_assemble(tag + "dqkv", dp, dq_all, dkp, dvp, dkc_l, dvc_l, dkc_c, dvc_c, cos, sin, S)
    dp, G["cw"], G["cb"], G["w4"], G["b4"], G["lam"] = _run(
        X, _rnn_bwd, tag + "rnn_bwd", A["p"], du, A["rnn"], dp, W["cw"], W["cb"], W["w4"], W["b4"], W["lam"], T)
    if X is not None and l == 0:
        G["win_t_a"] = _mm_wgrad(tag + "proj_dw_a", dp, A["h"][:, :D // 2])
        dh = _run(X, _mm_act, tag + "proj_dx", dp, W["win_t"], "NN")
        G["win_t_b"] = _run(X, _mm_wgrad, tag + "proj_dw_b", dp, A["h"][:, D // 2:])
    else:
        dh = _mm_act(tag + "proj_dx", dp, W["win_t"], "NN")
        G["win_t"] = _mm_wgrad(tag + "proj_dw", dp, A["h"])
    dxa, dsh1, dsc1, G["g_mix_pre"] = _normmod_bwd(tag + "mix_norm_bwd", dh, A["xa"], W["g_mix_pre"], W["mod"],
                                                   SH1, SC1, dx1)
    G["mod"] = jnp.concatenate([dsh1, dsc1, dga1, dsh2, dsc2, dga2], axis=1)
    return dxa, G


def _local_step(xa, target, Ws, S, X=None):
    rope = (*_rope_tables(S), _band_bias(S))
    L = len(Ws)
    h = _normmod_fwd("l0_mix_norm", xa, Ws[0]["g_mix_pre"], Ws[0]["mod"], SH1, SC1)
    saved = []
    x = xa
    for l in range(L):
        nxt = (Ws[l + 1]["g_mix_pre"], Ws[l + 1]["mod"]) if l + 1 < L else None
        A, out = _layer_fwd(l, x, h, Ws[l], rope, S, nxt, X)
        saved.append(A)
        if l + 1 < L:
            x, h = out
    Gs = [None] * L
    dx = None
    for l in reversed(range(L)):
        dx, Gs[l] = _layer_bwd(l, dx, saved[l], Ws[l], rope, S, X, loss_of=(out[0], target) if l == L - 1 else None)
    return Gs[L - 1]["sq"], dx, Gs


MESH = pl.DeviceIdType.MESH


def _place():
    return lax.axis_index("x"), lax.axis_index("y"), lax.axis_index("c")


def _lin(px, py, pc):
    return 4 * px + 2 * py + pc


def _allgather_small(name, blk):
    m, n = blk.shape

    def body(x_ref, out_ref, send_sems, recv_sems, local_sem):
        x, y, c = _place()
        me, sibling = (x, y, c), (x, y, 1 - c)
        chips = [(1 - x, y), (x, 1 - y), (1 - x, 1 - y)]

        def copy(k, block, to, src=None):
            dst = out_ref.at[_lin(*block)]
            return pltpu.make_async_remote_copy(src_ref=dst if src is None else src, dst_ref=dst,
                                                send_sem=send_sems.at[k], recv_sem=recv_sems.at[k],
                                                device_id=to, device_id_type=MESH)

        mine = pltpu.make_async_copy(x_ref, out_ref.at[_lin(*me)], local_sem)
        mine.start()
        first = [copy(0, me, sibling, src=x_ref)]
        first += [copy(1 + j, me, (*chip, c), src=x_ref) for j, chip in enumerate(chips)]
        for cp in first:
            cp.start()
        passed = [copy(4 + j, (*chip, c), sibling) for j, chip in enumerate(chips)]
        for j, chip in enumerate(chips):
            copy(1 + j, (*chip, c), me).wait_recv()
            passed[j].start()
        copy(0, sibling, me).wait_recv()
        for j, chip in enumerate(chips):
            copy(4 + j, (*chip, 1 - c), me).wait_recv()
        for cp in first + passed:
            cp.wait_send()
        mine.wait()

    return pl.pallas_call(
        body, name=name, out_shape=_sds((N_DEV, m, n), blk.dtype),
        in_specs=[pl.BlockSpec(memory_space=pltpu.VMEM)], out_specs=pl.BlockSpec(memory_space=pltpu.VMEM),
        scratch_shapes=[pltpu.SemaphoreType.DMA((7,)), pltpu.SemaphoreType.DMA((7,)), pltpu.SemaphoreType.DMA],
        compiler_params=pltpu.CompilerParams(vmem_limit_bytes=VMEM_LIMIT),
    )(blk)


def _allgather_hbm(name, shards):
    na = len(shards)

    def body(*refs):
        ins, outs = refs[:na], refs[na:2 * na]
        send_sems, recv_sems, local_sems = refs[2 * na:]
        x, y, c = _place()
        me, sibling = (x, y, c), (x, y, 1 - c)
        chips = [(1 - x, y), (x, 1 - y), (1 - x, 1 - y)]

        def copy(a, k, block, to, from_input=False):
            dst = outs[a].at[_lin(*block)]
            return pltpu.make_async_remote_copy(src_ref=ins[a] if from_input else dst, dst_ref=dst,
                                                send_sem=send_sems.at[a, k], recv_sem=recv_sems.at[a, k],
                                                device_id=to, device_id_type=MESH)

        mine = [pltpu.make_async_copy(ins[a], outs[a].at[_lin(*me)], local_sems.at[a]) for a in range(na)]
        for cp in mine:
            cp.start()
        first = []
        for a in range(na):
            first.append(copy(a, 0, me, sibling, True))
            first += [copy(a, 1 + j, me, (*chip, c), True) for j, chip in enumerate(chips)]
        for cp in first:
            cp.start()
        passed = []
        for j, chip in enumerate(chips):
            for a in range(na):
                copy(a, 1 + j, (*chip, c), me).wait_recv()
                fwd = copy(a, 4 + j, (*chip, c), sibling)
                fwd.start()
                passed.append(fwd)
        for a in range(na):
            copy(a, 0, sibling, me).wait_recv()
            for j, chip in enumerate(chips):
                copy(a, 4 + j, (*chip, 1 - c), me).wait_recv()
        for cp in first + passed:
            cp.wait_send()
        for cp in mine:
            cp.wait()

    return pl.pallas_call(
        body, name=name, out_shape=[_sds((N_DEV, *s.shape), s.dtype) for s in shards],
        in_specs=[ANY] * na, out_specs=[ANY] * na,
        scratch_shapes=[pltpu.SemaphoreType.DMA((na, 7)), pltpu.SemaphoreType.DMA((na, 7)),
                        pltpu.SemaphoreType.DMA((na,))],
    )(*shards)


def _exchange_shards(name, grads, L):
    nw = len(grads)
    na = nw * L
    flat = [g for per_layer in grads for g in per_layer]

    def body(*refs):
        ins, outs = refs[:na], refs[na:na + nw]
        send_sems, recv_sems, local_sems = refs[na + nw:]
        x, y, c = _place()
        me = _lin(x, y, c)
        peers = [(x ^ ((k + 1) >> 2 & 1), y ^ ((k + 1) >> 1 & 1), c ^ ((k + 1) & 1)) for k in range(7)]

        def copy(a, k, src_blk, dst_blk):
            return pltpu.make_async_remote_copy(src_ref=ins[a].at[src_blk], dst_ref=outs[a // L].at[a % L, dst_blk],
                                                send_sem=send_sems.at[a, k], recv_sem=recv_sems.at[a, k],
                                                device_id=peers[k], device_id_type=MESH)

        mine = [pltpu.make_async_copy(ins[a].at[me], outs[a // L].at[a % L, me], local_sems.at[a]) for a in range(na)]
        for cp in mine:
            cp.start()
        sent = [copy(a, k, _lin(*peers[k]), me) for a in range(na) for k in range(7)]
        for cp in sent:
            cp.start()
        for a in range(na):
            for k in range(7):
                copy(a, k, me, _lin(*peers[k])).wait_recv()
        for cp in sent:
            cp.wait_send()
        for cp in mine:
            cp.wait()

    return pl.pallas_call(
        body, name=name, out_shape=[_sds((L, *per_layer[0].shape), per_layer[0].dtype) for per_layer in grads],
        in_specs=[ANY] * na, out_specs=[ANY] * nw,
        scratch_shapes=[pltpu.SemaphoreType.DMA((na, 7)), pltpu.SemaphoreType.DMA((na, 7)),
                        pltpu.SemaphoreType.DMA((na,))],
    )(*flat)


MOD_ROWS = 16
MOD_SHARD = 6 * D // N_DEV
HI = lax.Precision.HIGHEST


def _mod_fwd(name, c9, w_mod, b_shard):
    L = w_mod.shape[0]

    def kern(c_ref, w_ref, b_ref, o_ref):
        o_ref[...] = lax.dot_general(_silu(c_ref[...]), w_ref[...], NN, precision=HI,
                                     preferred_element_type=F32) + b_ref[...]

    return pl.pallas_call(
        kern, name=name, grid=(L,),
        in_specs=[_full_spec(c9.shape), pl.BlockSpec((None, D, MOD_SHARD), lambda l: (l, 0, 0)),
                  pl.BlockSpec((None, 1, MOD_SHARD), lambda l: (l, 0, 0))],
        out_specs=pl.BlockSpec((None, MOD_ROWS, MOD_SHARD), lambda l: (l, 0, 0)),
        out_shape=_sds((L, MOD_ROWS, MOD_SHARD), F32), compiler_params=_params(),
    )(c9, w_mod, b_shard)


def _mod_bwd(name, c9, w_mod, dmod_all, dmod_cols):
    L = w_mod.shape[0]

    def rows9(ref, l):
        own = jnp.concatenate([ref[j, 2 * l + 1:2 * l + 2, :] for j in range(N_DEV)], axis=0)
        ctx = ref[0, 2 * l:2 * l + 1, :]
        for j in range(1, N_DEV):
            ctx = ctx + ref[j, 2 * l:2 * l + 1, :]
        return own, ctx

    def kern(c_ref, w_ref, all_ref, cols_ref, gw_ref, gb_ref, gc_ref):
        l = pl.program_id(0)
        for ll in range(L):
            @pl.when(l == ll)
            def _():
                own, ctx = rows9(all_ref, ll)
                gb_ref[...] = _colsum(own) + ctx
                own_s, ctx_s = rows9(cols_ref, ll)
                r16 = jnp.concatenate([own_s, ctx_s, jnp.zeros((MOD_ROWS - N_DEV - 1, MOD_SHARD), F32)], axis=0)
                gw_ref[...] = lax.dot_general(_silu(c_ref[...]), r16, TN, precision=HI, preferred_element_type=F32)
                part = lax.dot_general(r16, w_ref[...], NT, precision=HI,
                                       preferred_element_type=F32)[N_DEV:N_DEV + 1, :]
                if ll == 0:
                    gc_ref[...] = part
                else:
                    gc_ref[...] += part

    return pl.pallas_call(
        kern, name=name, grid=(L,),
        in_specs=[_full_spec(c9.shape), pl.BlockSpec((None, D, MOD_SHARD), lambda l: (l, 0, 0)),
                  _full_spec(dmod_all.shape), _full_spec(dmod_cols.shape)],
        out_specs=[pl.BlockSpec((None, D, MOD_SHARD), lambda l: (l, 0, 0)),
                   pl.BlockSpec((None, 1, 6 * D), lambda l: (l, 0, 0)), _full_spec((1, D))],
        out_shape=[_sds((L, D, MOD_SHARD), F32), _sds((L, 1, 6 * D), F32), _sds((1, D), F32)],
        compiler_params=_params(),
    )(c9, w_mod, dmod_all, dmod_cols)


_BC1 = 1.0 - ADAM_B1 ** ADAM_STEP
_BC2 = 1.0 - ADAM_B2 ** ADAM_STEP


def _adamw_vals(w, g, m, v):
    m = ADAM_B1 * m + (1.0 - ADAM_B1) * g
    v = ADAM_B2 * v + (1.0 - ADAM_B2) * (g * g)
    delta = -ADAM_LR * ((m / _BC1) / (jnp.sqrt(v / _BC2) + ADAM_EPS) + ADAM_WD * w)
    return delta, m, v


def _adamw(name, w, g, m, v, tile):
    R, C = w.shape
    blk = ((tile, C), lambda i: (i, 0))

    def body(i, ins, ps, outs, acc):
        d, mm, vv = _adamw_vals(ins[0][...], ins[1][...], ins[2][...], ins[3][...])
        outs[0][...] = d
        outs[1][...] = mm
        outs[2][...] = vv

    return _ew(name, body, R // tile, [(a, *blk) for a in (w, g, m, v)], [], [(_sds((R, C), F32), *blk)] * 3)


def _sum_slots(ref):
    g = ref[0].astype(F32)
    for j in range(1, N_DEV):
        g = g + ref[j].astype(F32)
    return g


def _adamw_slots(name, slots, w, m, v, tile):
    L, R, C = w.shape
    n = R // tile
    spec = pl.BlockSpec((None, tile, C), lambda l, i: (l, i, 0))
    pieces = [s if isinstance(s, (list, tuple)) else [s] for s in slots]
    layer_of = [ll for ll, ps in enumerate(pieces) for _ in ps]
    flat = [p for ps in pieces for p in ps]

    def slot_spec(ll, cols):
        return pl.BlockSpec((N_DEV, tile, cols),
                            lambda l, i: (0, jnp.where(l == ll, i, jnp.where(l < ll, 0, n - 1)), 0))

    def kern(*refs):
        s_refs = refs[:len(flat)]
        w_ref, m_ref, v_ref, g_ref, d_ref, mo_ref, vo_ref = refs[len(flat):]
        l = pl.program_id(0)
        for ll in range(L):
            @pl.when(l == ll)
            def _():
                parts = [_sum_slots(r) for r, lr in zip(s_refs, layer_of) if lr == ll]
                g = parts[0] if len(parts) == 1 else jnp.concatenate(parts, axis=1)
                g_ref[...] = g
                d_ref[...], mo_ref[...], vo_ref[...] = _adamw_vals(w_ref[...], g, m_ref[...], v_ref[...])

    return pl.pallas_call(
        kern, name=name, grid=(L, n),
        in_specs=[slot_spec(ll, p.shape[-1]) for ll, p in zip(layer_of, flat)] + [spec, spec, spec],
        out_specs=[spec] * 4, out_shape=[_sds((L, R, C), F32)] * 4,
        compiler_params=_params(("arbitrary", "arbitrary")),
    )(*flat, w, m, v)


def _sum_blocks(name, blocks):
    _, R, C = blocks.shape

    def kern(b_ref, o_ref):
        o_ref[...] = _sum_slots(b_ref)

    return pl.pallas_call(kern, name=name, in_specs=[_full_spec(blocks.shape)], out_specs=_full_spec((R, C)),
                          grid=(1,), out_shape=_sds((R, C), F32), compiler_params=_params())(blocks)


BIG = ("win_t", "wo_rnn", "wo_attn", "wout", "wffn_in_t", "wffn_out")
BIG_SRC = ("w_in", "w_o_rnn", "w_o_attn", "w_out", "w_ffn_in", "w_ffn_out")
BIG_T = (True, False, False, False, True, False)
BIG_TILE = (176, 128, 128, 128, 176, 176)


def _chan_full(g8):
    return jnp.transpose(g8, (1, 0, 2)).reshape(g8.shape[1], D)


def kernel(x, c, ctx, c_ctx, w_mod, b_mod, g_mix_pre, g_mix_post, g_ffn_pre, g_ffn_post, w_in, conv_w, conv_b, lru_wa, lru_ba, lru_wx, lru_bx, lru_lam, attn_sink, w_o_rnn, w_o_attn, w_out, w_ffn_in, w_ffn_out, loss_target, m_c_ctx, m_w_mod, m_b_mod, m_g_mix_pre, m_g_mix_post, m_g_ffn_pre, m_g_ffn_post, m_w_in, m_conv_w, m_conv_b, m_lru_wa, m_lru_ba, m_lru_wx, m_lru_bx, m_lru_lam, m_attn_sink, m_w_o_rnn, m_w_o_attn, m_w_out, m_w_ffn_in, m_w_ffn_out, v_c_ctx, v_w_mod, v_b_mod, v_g_mix_pre, v_g_mix_post, v_g_ffn_pre, v_g_ffn_post, v_w_in, v_conv_w, v_conv_b, v_lru_wa, v_lru_ba, v_lru_wx, v_lru_bx, v_lru_lam, v_attn_sink, v_w_o_rnn, v_w_o_attn, v_w_out, v_w_ffn_in, v_w_ffn_out):
    P = dict(c_ctx=c_ctx, w_mod=w_mod, b_mod=b_mod, g_mix_pre=g_mix_pre, g_mix_post=g_mix_post, g_ffn_pre=g_ffn_pre,
             g_ffn_post=g_ffn_post, w_in=w_in, conv_w=conv_w, conv_b=conv_b, lru_wa=lru_wa, lru_ba=lru_ba,
             lru_wx=lru_wx, lru_bx=lru_bx, lru_lam=lru_lam, attn_sink=attn_sink, w_o_rnn=w_o_rnn, w_o_attn=w_o_attn,
             w_out=w_out, w_ffn_in=w_ffn_in, w_ffn_out=w_ffn_out)
    Mo = dict(c_ctx=m_c_ctx, w_mod=m_w_mod, b_mod=m_b_mod, g_mix_pre=m_g_mix_pre, g_mix_post=m_g_mix_post,
              g_ffn_pre=m_g_ffn_pre, g_ffn_post=m_g_ffn_post, w_in=m_w_in, conv_w=m_conv_w, conv_b=m_conv_b,
              lru_wa=m_lru_wa, lru_ba=m_lru_ba, lru_wx=m_lru_wx, lru_bx=m_lru_bx, lru_lam=m_lru_lam,
              attn_sink=m_attn_sink, w_o_rnn=m_w_o_rnn, w_o_attn=m_w_o_attn, w_out=m_w_out, w_ffn_in=m_w_ffn_in,
              w_ffn_out=m_w_ffn_out)
    Vo = dict(c_ctx=v_c_ctx, w_mod=v_w_mod, b_mod=v_b_mod, g_mix_pre=v_g_mix_pre, g_mix_post=v_g_mix_post,
              g_ffn_pre=v_g_ffn_pre, g_ffn_post=v_g_ffn_post, w_in=v_w_in, conv_w=v_conv_w, conv_b=v_conv_b,
              lru_wa=v_lru_wa, lru_ba=v_lru_ba, lru_wx=v_lru_wx, lru_bx=v_lru_bx, lru_lam=v_lru_lam,
              attn_sink=v_attn_sink, w_o_rnn=v_w_o_rnn, w_o_attn=v_w_o_attn, w_out=v_w_out, w_ffn_in=v_w_ffn_in,
              w_ffn_out=v_w_ffn_out)
    L = w_in.shape[0]
    S = x.shape[1]
    me = _lin(*_place())

    small = jnp.concatenate([c.reshape(8, 128), conv_w.reshape(L * CONV_W, 128), lru_ba.reshape(2 * L, 128),
                             lru_bx.reshape(2 * L, 128), lru_lam.reshape(2 * L, 128), jnp.zeros((4, 128), F32)], axis=0)
    small_all = _allgather_small("ag_small", small)
    c_all = small_all[:, 0:8].reshape(N_DEV, D)
    conv_w_f = _chan_full(small_all[:, 8:16]).reshape(L, CONV_W, D)
    lru_ba_f = _chan_full(small_all[:, 16:20]).reshape(L, 2, D)
    lru_bx_f = _chan_full(small_all[:, 20:24]).reshape(L, 2, D)
    lru_lam_f = _chan_full(small_all[:, 24:28]).reshape(L, 2, D)

    c9 = jnp.concatenate([c_all, c_ctx[None], jnp.zeros((MOD_ROWS - N_DEV - 1, D), F32)], axis=0)
    b_shard = lax.dynamic_slice_in_dim(b_mod, me * MOD_SHARD, MOD_SHARD, axis=1)[:, None, :]
    mod_part = _mod_fwd("mod_fwd", c9, w_mod, b_shard)
    mod_all = _allgather_small("ag_mod", mod_part.reshape(L * MOD_ROWS, MOD_SHARD))
    mod_all = jnp.transpose(mod_all.reshape(N_DEV, L, MOD_ROWS, MOD_SHARD), (1, 2, 0, 3)).reshape(L, MOD_ROWS, 6 * D)
    own_row = lax.dynamic_index_in_dim(mod_all, me, axis=1, keepdims=False)
    modrows = jnp.stack([mod_all[:, N_DEV], own_row], axis=1)

    shards = [{k: (P[src][l].T if tr else P[src][l]).astype(BF16) for k, src, tr in zip(BIG, BIG_SRC, BIG_T)}
              for l in range(L)]
    win0, = _allgather_hbm("ag_w_in0", [shards[0]["win_t"]])
    Ws = []
    for l in range(L):
        W = {"win_t": win0.reshape(-1, D)} if l == 0 else {}
        W.update(
            cw=conv_w_f[l], cb=conv_b[l][None],
            w4=jnp.concatenate([lru_wa[l, 0], lru_wa[l, 1], lru_wx[l, 0], lru_wx[l, 1]], axis=-1).astype(BF16),
            b4=jnp.concatenate([lru_ba_f[l, 0].reshape(N_RNN_BLOCKS, 1, RB), lru_ba_f[l, 1].reshape(N_RNN_BLOCKS, 1, RB),
                                lru_bx_f[l, 0].reshape(N_RNN_BLOCKS, 1, RB), lru_bx_f[l, 1].reshape(N_RNN_BLOCKS, 1, RB)],
                               axis=-1),
            lam=lru_lam_f[l], sink4=jnp.broadcast_to(attn_sink[l].reshape(N_KV, Q_PER_KV, 1), (N_KV, Q_PER_KV, HEAD)),
            g_mix_pre=g_mix_pre[l][None], g_mix_post=g_mix_post[l][None], g_ffn_pre=g_ffn_pre[l][None],
            g_ffn_post=g_ffn_post[l][None], mod=modrows[l])
        Ws.append(W)

    xa = jnp.concatenate([ctx[0], x[0]], axis=0)
    plan = _Plan(shards, Ws)
    sq, dxa, Gs = _local_step(xa, loss_target[0], Ws, S, plan)
    loss = lax.psum((0.5 / D) * jnp.sum(sq), ("x", "y", "c"))
    grad_x = dxa[CTX:][None]

    dmod = jnp.concatenate([Gs[l]["mod"] for l in range(L)] + [jnp.zeros((8 - 2 * L, 6 * D), F32)], axis=0)
    dmod_all = _allgather_small("ag_dmod", dmod)
    dmod_cols = lax.dynamic_slice_in_dim(dmod_all, me * MOD_SHARD, MOD_SHARD, axis=2)
    g_w_mod, g_b_mod, dsc_part = _mod_bwd("mod_bwd", c9, w_mod, dmod_all, dmod_cols)
    g_b_mod = g_b_mod[:, 0]

    def rows(name, shape):
        return jnp.concatenate([Gs[l][name].reshape(shape) for l in range(L)], axis=0)

    b4g = [Gs[l]["b4"].reshape(N_RNN_BLOCKS, 4, RB) for l in range(L)]
    sink_row = jnp.concatenate([Gs[l]["sink4"][:, :, 0].reshape(1, N_Q) for l in range(L)]
                               + [jnp.zeros((1, D - L * N_Q), F32)], axis=1)
    small_g = jnp.concatenate(
        [rows("g_mix_pre", (1, D)), rows("g_mix_post", (1, D)), rows("g_ffn_pre", (1, D)), rows("g_ffn_post", (1, D)),
         rows("cb", (1, D)), rows("cw", (CONV_W, D))]
        + [b4g[l][:, d].reshape(1, D) for l in range(L) for d in range(2)]
        + [b4g[l][:, 2 + d].reshape(1, D) for l in range(L) for d in range(2)]
        + [rows("lam", (2, D)), sink_row, dsc_part], axis=0)
    n_small = small_g.shape[0]
    small_tot = _sum_blocks("sum_small", _allgather_small("ag_small_grads", small_g))
    o = 0
    G = {}
    for name in ("g_mix_pre", "g_mix_post", "g_ffn_pre", "g_ffn_post", "conv_b"):
        G[name] = small_tot[o:o + L]
        o += L
    G["conv_w"] = small_tot[o:o + L * CONV_W].reshape(L, CONV_W, D)
    o += L * CONV_W
    for name in ("lru_ba", "lru_bx", "lru_lam"):
        G[name] = small_tot[o:o + 2 * L].reshape(L, 2, D)
        o += 2 * L
    G["attn_sink"] = small_tot[o, :L * N_Q].reshape(L, N_Q)
    sg = jax.nn.sigmoid(c_ctx)
    G["c_ctx"] = small_tot[o + 1] * (sg * (1.0 + c_ctx * (1.0 - sg)))
    G["b_mod"] = g_b_mod
    G["w_mod"] = g_w_mod

    last_slots, = _exchange_shards("exchange_w_in0", [[Gs[0]["win_t_b"].reshape(N_DEV, -1, D // 2)]], 1)
    plan.slots[0]["win_t"] = [plan.slots[0]["win_t_a"], last_slots[0]]

    out_g, out_d, out_m, out_v = {}, {}, {}, {}

    def put(name, res, shape=None):
        g, d, m, v = res
        for dst, val in ((out_g, g), (out_d, d), (out_m, m), (out_v, v)):
            dst[name] = val if shape is None else val.reshape(shape)

    for k, src, tr, tile in zip(BIG, BIG_SRC, BIG_T, BIG_TILE):
        lay = (lambda a: jnp.swapaxes(a, 1, 2)) if tr else (lambda a: a)
        res = _adamw_slots("adamw_" + src, [plan.slots[l][k] for l in range(L)], lay(P[src]), lay(Mo[src]),
                           lay(Vo[src]), tile)
        put(src, [lay(r) for r in res])
    res = _adamw("adamw_w_mod", w_mod.reshape(L * D, MOD_SHARD), g_w_mod.reshape(L * D, MOD_SHARD),
                 m_w_mod.reshape(L * D, MOD_SHARD), v_w_mod.reshape(L * D, MOD_SHARD), 256)
    put("w_mod", (g_w_mod,) + tuple(res), w_mod.shape)
    def fuse4(wa, wx):
        return jnp.concatenate([wa[:, 0], wa[:, 1], wx[:, 0], wx[:, 1]], axis=-1).reshape(L, N_RNN_BLOCKS * RB, 4 * RB)

    res = _adamw_slots("adamw_gates", plan.gate_slots,
                       fuse4(lru_wa, lru_wx), fuse4(m_lru_wa, m_lru_wx), fuse4(v_lru_wa, v_lru_wx), 256)
    res = [r.reshape(L, N_RNN_BLOCKS, RB, 4, RB) for r in res]
    put("lru_wa", [jnp.stack([r[:, :, :, 0], r[:, :, :, 1]], axis=1) for r in res])
    put("lru_wx", [jnp.stack([r[:, :, :, 2], r[:, :, :, 3]], axis=1) for r in res])
    rep = ("g_mix_pre", "g_mix_post", "g_ffn_pre", "g_ffn_post", "conv_b", "b_mod")

    def pack_rep(T_):
        sink = jnp.concatenate([T_["attn_sink"].reshape(1, L * N_Q), jnp.zeros((1, D - L * N_Q), F32)], axis=1)
        return jnp.concatenate([T_[n].reshape(-1, D) for n in rep] + [sink, T_["c_ctx"][None]], axis=0)

    pk = [pack_rep(T_) for T_ in (P, G, Mo, Vo)]
    n_rep = pk[0].shape[0]
    res = _adamw("adamw_replicated", *[jnp.pad(a, ((0, 24 - n_rep), (0, 0))) for a in pk], 24)
    res = (pk[1],) + tuple(r[:n_rep] for r in res)
    o = 0
    for n in rep:
        k = P[n].size // D
        put(n, [r[o:o + k] for r in res], P[n].shape)
        o += k
    put("attn_sink", [r[o, :L * N_Q] for r in res], attn_sink.shape)
    put("c_ctx", [r[o + 1] for r in res], c_ctx.shape)
    chan = ("conv_w", "lru_ba", "lru_bx", "lru_lam")
    g_own = {n: lax.dynamic_slice_in_dim(G[n], me * RB, RB, axis=2) for n in chan}

    def pack_chan(T_):
        return jnp.concatenate([T_[n].reshape(-1, RB) for n in chan], axis=0)

    pk = [pack_chan(T_) for T_ in (P, g_own, Mo, Vo)]
    n_ch = pk[0].shape[0]
    res = _adamw("adamw_channels", *[jnp.pad(a, ((0, 24 - n_ch), (0, 0))) for a in pk], 24)
    res = (pk[1],) + tuple(r[:n_ch] for r in res)
    o = 0
    for n in chan:
        k = P[n].size // RB
        put(n, [r[o:o + k] for r in res], P[n].shape)
        o += k

    order = ("c_ctx", "w_mod", "b_mod", "g_mix_pre", "g_mix_post", "g_ffn_pre", "g_ffn_post", "w_in", "conv_w", "conv_b",
             "lru_wa", "lru_ba", "lru_wx", "lru_bx", "lru_lam", "attn_sink", "w_o_rnn", "w_o_attn", "w_out", "w_ffn_in",
             "w_ffn_out")
    return (loss, grad_x, *[out_g[n] for n in order], *[out_d[n] for n in order], *[out_m[n] for n in order],
            *[out_v[n] for n in order])
```

```python
import functools
import math

import numpy as np
import jax
import jax.numpy as jnp
from jax import lax
from jax.experimental import pallas as pl
from jax.experimental.pallas import tpu as pltpu

F32 = jnp.float32
BF16 = jnp.bfloat16

D = 1024
CTX = 256
TR = 256
HEAD = 128
N_Q = 8
N_KV = 2
Q_PER_KV = N_Q // N_KV
GRID_W = 64
N_FREQ = HEAD // 4
ROPE_BASE = 10000.0
N_RNN_BLOCKS = 8
CONV_W = 4
CONV_LEFT = 2
LRU_C = 8.0
D_FF = 2816
IN_W = 5632
P_W = IN_W
DP_W = 3584
COL_XR, COL_GR, COL_Q, COL_K, COL_V, COL_GL = 0, 1024, 2048, 3072, 3328, 3584
GLB = 512
EPS = 1e-6
NEG_INF = -1e30
ATT_SCALE = HEAD ** -0.5
N_DEV = 8
VMEM_LIMIT = 56 * 1024 * 1024

ADAM_LR, ADAM_B1, ADAM_B2, ADAM_EPS, ADAM_WD, ADAM_STEP = 0.001, 0.9, 0.999, 1e-08, 0.01, 10

NN = (((1,), (0,)), ((), ()))
NT = (((1,), (1,)), ((), ()))
TN = (((0,), (0,)), ((), ()))


def _dot(a, b, dims=NN):
    return lax.dot_general(a, b, dims, preferred_element_type=F32)


def _params(sem=("arbitrary",)):
    return pltpu.CompilerParams(dimension_semantics=sem, vmem_limit_bytes=VMEM_LIMIT)


def _full_spec(shape):
    nd = len(shape)
    return pl.BlockSpec(shape, lambda *_: (0,) * nd)


ANY = pl.BlockSpec(memory_space=pl.ANY)


def _ew(name, body, n, row_ins, pars, row_outs, accs=(), alias=None):
    n_ri, n_p, n_ro, n_acc = len(row_ins), len(pars), len(row_outs), len(accs)

    def kern(*refs):
        i = pl.program_id(0)
        ins = refs[:n_ri]
        ps = refs[n_ri:n_ri + n_p]
        outs = refs[n_ri + n_p:n_ri + n_p + n_ro]
        acc = refs[n_ri + n_p + n_ro:]
        if n_acc:
            @pl.when(i == 0)
            def _():
                for a in acc:
                    a[...] = jnp.zeros(a.shape, a.dtype)
        body(i, ins, ps, outs, acc)

    in_specs = [ANY if blk is None else pl.BlockSpec(blk, imap) for (_, blk, imap) in row_ins]
    in_specs += [_full_spec(p.shape) for p in pars]
    out_specs = [pl.BlockSpec(blk, imap) for (_, blk, imap) in row_outs] + [_full_spec(a.shape) for a in accs]
    out_shape = [s for (s, _, _) in row_outs] + list(accs)
    return pl.pallas_call(
        kern, name=name, grid=(n,), in_specs=in_specs, out_specs=out_specs, out_shape=out_shape,
        input_output_aliases=alias or {}, compiler_params=_params(),
    )(*[a for (a, _, _) in row_ins], *pars)


def _rowblk(width, colblk=0, roff=0, tile=TR):
    return (tile, width), (lambda i: (i + roff, colblk))


def _sds(shape, dtype):
    return jax.ShapeDtypeStruct(shape, dtype)


class _Carry:
    SAME_CORE = (1, 3, 5)

    def __init__(self, jobs):
        self.jobs = list(jobs)
        self.arrays = [a for _, a in self.jobs]
        self.out_shapes = [_sds(a.shape if kind == "scatter" else (N_DEV, *a.shape), a.dtype) for kind, a in self.jobs]
        n = len(self.jobs)
        self.scratch = [pltpu.SemaphoreType.DMA((n, 7)), pltpu.SemaphoreType.DMA((n, 7)), pltpu.SemaphoreType.DMA((n,))]

    def _setup(self, sems):
        send_sems, recv_sems, local_sems = sems
        x, y, c = _place()
        me = _lin(x, y, c)
        peers = [(x ^ ((k + 1) >> 2 & 1), y ^ ((k + 1) >> 1 & 1), c ^ ((k + 1) & 1)) for k in range(7)]

        def copy(a, k, sem_k, src, dst):
            return pltpu.make_async_remote_copy(src_ref=src, dst_ref=dst, send_sem=send_sems.at[a, sem_k],
                                                recv_sem=recv_sems.at[a, sem_k], device_id=peers[k], device_id_type=MESH)

        return me, [_lin(*p) for p in peers], copy, local_sems

    def _local(self, a, kind, ins, outs, me, local_sems):
        return pltpu.make_async_copy(ins[a].at[me] if kind == "scatter" else ins[a], outs[a].at[me], local_sems.at[a])

    def start(self, ins, outs, sems):
        me, theirs, copy, local_sems = self._setup(sems)
        for a, (kind, _) in enumerate(self.jobs):
            self._local(a, kind, ins, outs, me, local_sems).start()
            if kind == "scatter":
                for k in range(7):
                    copy(a, k, k, ins[a].at[theirs[k]], outs[a].at[me]).start()
            else:
                for k in (0,) + self.SAME_CORE:
                    copy(a, k, k, ins[a], outs[a].at[me]).start()

    def wait(self, ins, outs, sems):
        me, theirs, copy, local_sems = self._setup(sems)
        for a, (kind, _) in enumerate(self.jobs):
            if kind == "scatter":
                for k in range(7):
                    copy(a, k, k, ins[a].at[me], outs[a].at[theirs[k]]).wait_recv()
                for k in range(7):
                    copy(a, k, k, ins[a].at[theirs[k]], outs[a].at[me]).wait_send()
            else:
                for k in self.SAME_CORE:
                    blk = outs[a].at[theirs[k]]
                    copy(a, k, k, ins[a], blk).wait_recv()
                    copy(a, 0, k + 1, blk, blk).start()
                copy(a, 0, 0, ins[a], outs[a].at[theirs[0]]).wait_recv()
                for k in self.SAME_CORE:
                    copy(a, 0, k + 1, ins[a], outs[a].at[theirs[k + 1]]).wait_recv()
                for k in (0,) + self.SAME_CORE:
                    copy(a, k, k, ins[a], outs[a].at[me]).wait_send()
                for k in self.SAME_CORE:
                    blk = outs[a].at[theirs[k]]
                    copy(a, 0, k + 1, blk, blk).wait_send()
            self._local(a, kind, ins, outs, me, local_sems).wait()


def _carried(kern, carry, n_in, n_out, first, last):
    if carry is None:
        return kern
    nc = len(carry.jobs)

    def wrapped(*refs):
        ins, cin = refs[:n_in], refs[n_in:n_in + nc]
        outs, cout = refs[n_in + nc:n_in + nc + n_out], refs[n_in + nc + n_out:n_in + 2 * nc + n_out]
        scr, sems = refs[n_in + 2 * nc + n_out:-3], refs[-3:]

        @pl.when(first())
        def _():
            carry.start(cin, cout, sems)

        kern(*ins, *outs, *scr)

        @pl.when(last())
        def _():
            carry.wait(cin, cout, sems)

    return wrapped


def _carry_args(carry):
    if carry is None:
        return [], [], [], [], []
    n = len(carry.jobs)
    return [ANY] * n, carry.arrays, [ANY] * n, carry.out_shapes, carry.scratch


def _grid_ends(dims):
    first = lambda: functools.reduce(jnp.logical_and, [pl.program_id(d) == 0 for d in range(len(dims))])
    last = lambda: functools.reduce(jnp.logical_and, [pl.program_id(d) == n - 1 for d, n in enumerate(dims)])
    return first, last


def _mm_call(name, a, b, mode, out_dtype, tm, tn, rows_outer=True, single_b=False, carry=None):
    if mode == "TN":
        (K, M), N = a.shape, b.shape[1]
    else:
        (M, K), N = a.shape, (b.shape[1] if mode == "NN" else b.shape[0])
    assert M % tm == 0 and N % tn == 0, (name, M, N, K, tm, tn)
    ij = (lambda g0, g1: (g0, g1)) if rows_outer else (lambda g0, g1: (g1, g0))
    grid = (M // tm, N // tn) if rows_outer else (N // tn, M // tm)
    if mode == "TN":
        a_spec = pl.BlockSpec((K, tm), lambda g0, g1: (0, ij(g0, g1)[0]))
    else:
        a_spec = pl.BlockSpec((tm, K), lambda g0, g1: (ij(g0, g1)[0], 0))
    b_blk, b_map = ((tn, K), lambda g0, g1: (ij(g0, g1)[1], 0)) if mode == "NT" else \
                   ((K, tn), lambda g0, g1: (0, ij(g0, g1)[1]))
    b_spec = pl.BlockSpec(b_blk, b_map, pipeline_mode=pl.Buffered(1)) if single_b else pl.BlockSpec(b_blk, b_map)
    dims = {"NN": NN, "NT": NT, "TN": TN}[mode]

    def kern(a_ref, b_ref, o_ref):
        o_ref[...] = _dot(a_ref[...], b_ref[...], dims).astype(o_ref.dtype)

    ci, ca, co, cs, cscr = _carry_args(carry)
    res = pl.pallas_call(
        _carried(kern, carry, 2, 1, *_grid_ends(grid)), name=name, grid=grid, in_specs=[a_spec, b_spec] + ci,
        out_specs=[pl.BlockSpec((tm, tn), lambda g0, g1: ij(g0, g1))] + co,
        out_shape=[_sds((M, N), out_dtype)] + cs, scratch_shapes=cscr,
        compiler_params=_params(("arbitrary", "arbitrary")),
    )(a, b, *ca)
    return res[0] if carry is None else (res[0], res[1:])


def _mm_act(name, a, w, mode, out_dtype=BF16, carry=None):
    rows, K = a.shape
    N = w.shape[1] if mode == "NN" else w.shape[0]
    if K > D_FF:
        return _mm_call(name, a, w, mode, out_dtype, rows // 8, N, single_b=True, carry=carry)
    tn = N if N <= 1024 else 1408
    return _mm_call(name, a, w, mode, out_dtype, rows // 4, tn, carry=carry)


def _mm_wgrad(name, x, dy, out_dtype=BF16, carry=None):
    M = x.shape[1]
    tm = 1408 if M == D_FF else 512
    return _mm_call(name, x, dy, "TN", out_dtype, tm, dy.shape[1], single_b=True, carry=carry)


def _sigmoid(x):
    return 0.5 * jnp.tanh(0.5 * x) + 0.5


def _silu(x):
    return x * _sigmoid(x)


def _silu_grad(x):
    s = _sigmoid(x)
    return s * (1.0 + x * (1.0 - s))


_GELU_K = math.sqrt(2.0 / math.pi)


def _gelu(x):
    return 0.5 * x * (1.0 + jnp.tanh(_GELU_K * (x + 0.044715 * x * x * x)))


def _gelu_grad(x):
    t = jnp.tanh(_GELU_K * (x + 0.044715 * x * x * x))
    return 0.5 * (1.0 + t) + 0.5 * x * (1.0 - t * t) * _GELU_K * (1.0 + 3.0 * 0.044715 * x * x)


def _log_sigmoid(x):
    return jnp.minimum(x, 0.0) - jnp.log(1.0 + jnp.exp(-jnp.abs(x)))


def _rms(x):
    x = x.astype(F32)
    r = lax.rsqrt(jnp.mean(x * x, axis=-1, keepdims=True) + EPS)
    return x * r, r


def _rms_bwd(dy, y, r):
    return r * (dy - y * jnp.mean(dy * y, axis=-1, keepdims=True))


def _modrow(mod_ref, i, chunk):
    lo = mod_ref[0:1, chunk * D:(chunk + 1) * D]
    hi = mod_ref[1:2, chunk * D:(chunk + 1) * D]
    return jnp.where(i == 0, lo, hi)


def _acc_seg(acc_ref, i, val):
    zero = jnp.zeros_like(val)
    acc_ref[0:1, :] += jnp.where(i == 0, val, zero)
    acc_ref[1:2, :] += jnp.where(i == 0, zero, val)


def _colsum(x):
    return jnp.sum(x, axis=0, keepdims=True)


SH1, SC1, GA1, SH2, SC2, GA2 = range(6)


def _normmod_fwd(name, xa, g, mod, c_sh, c_sc):
    T = xa.shape[0]

    def body(i, ins, ps, outs, acc):
        y, _ = _rms(ins[0][...])
        h = (y * ps[0][...]) * (1.0 + _modrow(ps[1], i, c_sc)) + _modrow(ps[1], i, c_sh)
        outs[0][...] = h.astype(BF16)

    return _ew(name, body, T // TR, [(xa, *_rowblk(D))], [g, mod], [(_sds((T, D), BF16), *_rowblk(D))])[0]


def _modrows(mod_ref, row0, n, chunk):
    t = row0 + lax.broadcasted_iota(jnp.int32, (n, 1), 0)
    return jnp.where(t < CTX, mod_ref[0:1, chunk * D:(chunk + 1) * D], mod_ref[1:2, chunk * D:(chunk + 1) * D])


def _loss_resid_bwd(name, x_out, target, mat, gpost, mod, c_ga):
    T = x_out.shape[0]

    def body(i, ins, ps, outs, acc):
        err = ins[0][...] - ins[1][...]
        lat = i > 0
        dx = jnp.where(lat, err * (1.0 / D), 0.0)
        outs[0][...] = dx
        acc[2][...] += jnp.where(lat, _colsum(err * err), 0.0)
        outs[1][...] = _resid_bwd_vals(i, dx, ins[2][...], ps[0][...], ps[1], c_ga, acc[0], acc[1]).astype(BF16)

    tgt_blk = ((TR, D), lambda i: (jnp.maximum(i - 1, 0), 0))
    return _ew(name, body, T // TR, [(x_out, *_rowblk(D)), (target, *tgt_blk), (mat, *_rowblk(D))], [gpost, mod],
               [(_sds((T, D), F32), *_rowblk(D)), (_sds((T, D), BF16), *_rowblk(D))],
               [_sds((2, D), F32), _sds((1, D), F32), _sds((1, D), F32)])


def _mod_for(mod_ref, i, tm, chunk):
    return _modrow(mod_ref, i, chunk) if tm == TR else _modrows(mod_ref, i * tm, tm, chunk)


def _acc_for(acc_ref, i, tm, v):
    if tm == TR:
        _acc_seg(acc_ref, i, _colsum(v))
        return

    @pl.when(i * tm < CTX)
    def _():
        is_ctx = i * tm + lax.broadcasted_iota(jnp.int32, (tm, 1), 0) < CTX
        acc_ref[0:1, :] += _colsum(jnp.where(is_ctx, v, 0.0))
        acc_ref[1:2, :] += _colsum(jnp.where(is_ctx, 0.0, v))

    @pl.when(i * tm >= CTX)
    def _():
        acc_ref[1:2, :] += _colsum(v)


def _resid_bwd_vals(i, dout, mat, gpost, mod_ref, c_ga, acc_ga, acc_g, tm=TR):
    ym, rm = _rms(mat)
    ga = _mod_for(mod_ref, i, tm, c_ga)
    _acc_for(acc_ga, i, tm, dout * (ym * gpost))
    dn = dout * ga
    acc_g[...] += _colsum(dn * ym)
    return _rms_bwd(dn * gpost, ym, rm)


def _normmod_bwd_vals(i, dh, xin, g, mod_ref, c_sh, c_sc, acc_sh, acc_sc, acc_g, tm=TR):
    dh = dh.astype(F32)
    y, r = _rms(xin)
    _acc_for(acc_sc, i, tm, dh * (y * g))
    _acc_for(acc_sh, i, tm, dh)
    dyg = dh * (1.0 + _mod_for(mod_ref, i, tm, c_sc))
    acc_g[...] += _colsum(dyg * y)
    return _rms_bwd(dyg * g, y, r)


FT = 1408


def _ffn_in_fused(name, h2, w_t, carry=None):
    T = h2.shape[0]
    tm, nj = T // 4, D_FF // FT

    def kern(a_ref, bg_ref, bu_ref, fg_ref, fu_ref, s_ref):
        a = a_ref[...]
        g = _dot(a, bg_ref[...], NT)
        u = _dot(a, bu_ref[...], NT)
        fg_ref[...] = g.astype(BF16)
        fu_ref[...] = u.astype(BF16)
        s_ref[...] = (_silu(g) * u).astype(BF16)

    o_spec = pl.BlockSpec((tm, FT), lambda i, j: (i, j))
    ci, ca, co, cs, cscr = _carry_args(carry)
    res = pl.pallas_call(
        _carried(kern, carry, 3, 3, *_grid_ends((4, nj))), name=name, grid=(4, nj),
        in_specs=[pl.BlockSpec((tm, D), lambda i, j: (i, 0)), pl.BlockSpec((FT, D), lambda i, j: (j, 0)),
                  pl.BlockSpec((FT, D), lambda i, j: (j + nj, 0))] + ci,
        out_specs=[o_spec] * 3 + co, out_shape=[_sds((T, D_FF), BF16)] * 3 + cs, scratch_shapes=cscr,
        compiler_params=_params(("arbitrary", "arbitrary")),
    )(h2, w_t, w_t, *ca)
    return res if carry is None else (res[:3], res[3:])


def _norm_chain(i, tm, xin, mat, gpost, mod_ref, c_ga, gnext, modn_ref, c_sh, c_sc):
    ym, _ = _rms(mat.astype(BF16))
    xo = xin + _modrows(mod_ref, i * tm, tm, c_ga) * (ym * gpost)
    y, _ = _rms(xo)
    h = (y * gnext) * (1.0 + _modrows(modn_ref, i * tm, tm, c_sc)) + _modrows(modn_ref, i * tm, tm, c_sh)
    return xo, h.astype(BF16)


def _out_fused(name, p, ya, yb, xa, w_out, gpost, mod, gnext):
    T = ya.shape[0]
    tm = T // 8

    def kern(g0, g1, g2, g3, ya_ref, yb_ref, xa_ref, w_ref, gpost_ref, mod_ref, gnext_ref, z_ref, m_ref, x1_ref, h2_ref):
        i = pl.program_id(0)
        ga = _sigmoid(jnp.concatenate([g0[...], g1[...]], axis=1).astype(F32))
        gb = _sigmoid(jnp.concatenate([g2[...], g3[...]], axis=1).astype(F32))
        z = (ga * ya_ref[...].astype(F32) + gb * yb_ref[...].astype(F32)).astype(BF16)
        z_ref[...] = z
        m = _dot(z, w_ref[...])
        m_ref[...] = m.astype(BF16)
        x1_ref[...], h2_ref[...] = _norm_chain(i, tm, xa_ref[...], m, gpost_ref[...], mod_ref, GA1,
                                               gnext_ref[...], mod_ref, SH2, SC2)

    row = lambda w: pl.BlockSpec((tm, w), lambda i: (i, 0))
    return pl.pallas_call(
        kern, name=name, grid=(T // tm,),
        in_specs=[pl.BlockSpec((tm, GLB), lambda i, q=q: (i, COL_GL // GLB + q)) for q in range(4)]
                 + [row(D), row(D), row(D), _full_spec(w_out.shape), _full_spec(gpost.shape), _full_spec(mod.shape),
                    _full_spec(gnext.shape)],
        out_specs=[row(D)] * 4,
        out_shape=[_sds((T, D), BF16), _sds((T, D), BF16), _sds((T, D), F32), _sds((T, D), BF16)],
        compiler_params=_params(),
    )(p, p, p, p, ya, yb, xa, w_out, gpost, mod, gnext)


def _ffn_out_fused(name, s, w, x1, gpost, mod, nxt=None):
    T = s.shape[0]
    tm = T // 8

    def kern(s_ref, w_ref, x1_ref, gpost_ref, mod_ref, *rest):
        i = pl.program_id(0)
        e = _dot(s_ref[...], w_ref[...])
        if nxt is None:
            e_ref, xo_ref = rest
            ym, _ = _rms(e.astype(BF16))
            xo_ref[...] = x1_ref[...] + _modrows(mod_ref, i * tm, tm, GA2) * (ym * gpost_ref[...])
        else:
            gnext_ref, modn_ref, e_ref, xo_ref, h_ref = rest
            xo_ref[...], h_ref[...] = _norm_chain(i, tm, x1_ref[...], e, gpost_ref[...], mod_ref, GA2,
                                                  gnext_ref[...], modn_ref, SH1, SC1)
        e_ref[...] = e.astype(BF16)

    row = lambda w_: pl.BlockSpec((tm, w_), lambda i: (i, 0))
    extra = [] if nxt is None else list(nxt)
    return pl.pallas_call(
        kern, name=name, grid=(T // tm,),
        in_specs=[row(D_FF), _full_spec(w.shape), row(D), _full_spec(gpost.shape), _full_spec(mod.shape)]
                 + [_full_spec(a.shape) for a in extra],
        out_specs=[row(D)] * (2 if nxt is None else 3),
        out_shape=[_sds((T, D), BF16), _sds((T, D), F32)] + ([] if nxt is None else [_sds((T, D), BF16)]),
        compiler_params=_params(),
    )(s, w, x1, gpost, mod, *extra)


def _ffn_bwd_fused(name, fg, fu, w, de=None, head=None):
    T = fg.shape[0]
    tm = T // 8
    row = lambda w_: pl.BlockSpec((tm, w_), lambda i: (i, 0))
    w_spec = pl.BlockSpec(w.shape, lambda i: (0, 0), pipeline_mode=pl.Buffered(1))

    def tail(de_v, fg_ref, fu_ref, w_ref, df_ref):
        ds = _dot(de_v, w_ref[...], NT)
        g, u = fg_ref[...].astype(F32), fu_ref[...].astype(F32)
        df_ref[...] = jnp.concatenate([ds * u * _silu_grad(g), ds * _silu(g)], axis=1).astype(BF16)

    if head is None:
        def kern(de_ref, fg_ref, fu_ref, w_ref, df_ref):
            tail(de_ref[...], fg_ref, fu_ref, w_ref, df_ref)

        return pl.pallas_call(
            kern, name=name, grid=(T // tm,), in_specs=[row(D), row(D_FF), row(D_FF), w_spec],
            out_specs=[row(2 * D_FF)], out_shape=[_sds((T, 2 * D_FF), BF16)], compiler_params=_params(),
        )(de, fg, fu, w)

    dx2, e, gpost, mod = head

    def kern(dx_ref, e_ref, fg_ref, fu_ref, w_ref, gpost_ref, mod_ref, de_ref, df_ref, dga_ref, dg_ref):
        i = pl.program_id(0)

        @pl.when(i == 0)
        def _():
            dga_ref[...] = jnp.zeros(dga_ref.shape, F32)
            dg_ref[...] = jnp.zeros(dg_ref.shape, F32)

        de_v = _resid_bwd_vals(i, dx_ref[...], e_ref[...], gpost_ref[...], mod_ref, GA2, dga_ref, dg_ref,
                               tm=tm).astype(BF16)
        de_ref[...] = de_v
        tail(de_v, fg_ref, fu_ref, w_ref, df_ref)

    return pl.pallas_call(
        kern, name=name, grid=(T // tm,),
        in_specs=[row(D), row(D), row(D_FF), row(D_FF), w_spec, _full_spec(gpost.shape), _full_spec(mod.shape)],
        out_specs=[row(D), row(2 * D_FF), _full_spec((2, D)), _full_spec((1, D))],
        out_shape=[_sds((T, D), BF16), _sds((T, 2 * D_FF), BF16), _sds((2, D), F32), _sds((1, D), F32)],
        compiler_params=_params(),
    )(dx2, e, fg, fu, w, gpost, mod)


def _zero_at_start(i, refs):
    @pl.when(i == 0)
    def _():
        for r in refs:
            r[...] = jnp.zeros(r.shape, F32)


def _proj_bwd_fused(name, dp, dgl, w_in_t, xa, dx1, gpre, mod, carry=None):
    T = dp.shape[0]
    tm = T // 8
    row = lambda w_: pl.BlockSpec((tm, w_), lambda i: (i, 0))

    def kern(dp_ref, dgl_ref, w_ref, xa_ref, dx1_ref, g_ref, mod_ref, dxa_ref, dsh_ref, dsc_ref, dg_ref):
        i = pl.program_id(0)
        _zero_at_start(i, (dsh_ref, dsc_ref, dg_ref))
        dh = _dot(dp_ref[...], w_ref[0:DP_W, :]) + _dot(dgl_ref[...], w_ref[DP_W:, :])
        dxa_ref[...] = dx1_ref[...] + _normmod_bwd_vals(i, dh, xa_ref[...], g_ref[...], mod_ref, SH1, SC1,
                                                        dsh_ref, dsc_ref, dg_ref, tm=tm)

    ci, ca, co, cs, cscr = _carry_args(carry)
    res = pl.pallas_call(
        _carried(kern, carry, 7, 4, *_grid_ends((T // tm,))), name=name, grid=(T // tm,),
        in_specs=[row(DP_W), row(P_W - DP_W),
                  pl.BlockSpec(w_in_t.shape, lambda i: (0, 0), pipeline_mode=pl.Buffered(1)), row(D), row(D),
                  _full_spec(gpre.shape), _full_spec(mod.shape)] + ci,
        out_specs=[row(D), _full_spec((2, D)), _full_spec((2, D)), _full_spec((1, D))] + co,
        out_shape=[_sds((T, D), F32), _sds((2, D), F32), _sds((2, D), F32), _sds((1, D), F32)] + cs,
        scratch_shapes=cscr, compiler_params=_params(),
    )(dp, dgl, w_in_t, xa, dx1, gpre, mod, *ca)
    return res if carry is None else (res[:4], res[4:])


def _proj_wgrad(name, dp, dgl, h, carry=None):
    T, N = h.shape
    n1, n2 = DP_W // GLB, (P_W - DP_W) // GLB

    def kern(a1_ref, a2_ref, h_ref, o_ref):
        i = pl.program_id(0)

        @pl.when(i < n1)
        def _():
            o_ref[...] = _dot(a1_ref[...], h_ref[...], TN).astype(o_ref.dtype)

        @pl.when(i >= n1)
        def _():
            o_ref[...] = _dot(a2_ref[...], h_ref[...], TN).astype(o_ref.dtype)

    ci, ca, co, cs, cscr = _carry_args(carry)
    res = pl.pallas_call(
        _carried(kern, carry, 3, 1, *_grid_ends((n1 + n2,))), name=name, grid=(n1 + n2,),
        in_specs=[pl.BlockSpec((T, GLB), lambda i: (0, jnp.minimum(i, n1 - 1))),
                  pl.BlockSpec((T, GLB), lambda i: (0, jnp.maximum(i - n1, 0))),
                  pl.BlockSpec((T, N), lambda i: (0, 0), pipeline_mode=pl.Buffered(1))] + ci,
        out_specs=[pl.BlockSpec((GLB, N), lambda i: (i, 0))] + co,
        out_shape=[_sds((P_W, N), BF16)] + cs, scratch_shapes=cscr, compiler_params=_params(),
    )(dp, dgl, h, *ca)
    return res[0] if carry is None else (res[0], res[1:])


def _ffn_in_bwd_fused(name, df, w_t, x1, dres, mat, gpre, mod, gpost, carry=None):
    T = df.shape[0]
    tm = T // 8
    row = lambda w_: pl.BlockSpec((tm, w_), lambda i: (i, 0))

    def kern(df_ref, w_ref, x1_ref, dres_ref, mat_ref, gpre_ref, mod_ref, gpost_ref,
             dx1_ref, dm_ref, dsh_ref, dsc_ref, dgpre_ref, dga_ref, dgpost_ref):
        i = pl.program_id(0)
        _zero_at_start(i, (dsh_ref, dsc_ref, dgpre_ref, dga_ref, dgpost_ref))
        dh2 = _dot(df_ref[...], w_ref[...])
        dx1 = dres_ref[...] + _normmod_bwd_vals(i, dh2, x1_ref[...], gpre_ref[...], mod_ref, SH2, SC2,
                                                dsh_ref, dsc_ref, dgpre_ref, tm=tm)
        dx1_ref[...] = dx1
        dm_ref[...] = _resid_bwd_vals(i, dx1, mat_ref[...], gpost_ref[...], mod_ref, GA1, dga_ref, dgpost_ref,
                                      tm=tm).astype(BF16)

    ci, ca, co, cs, cscr = _carry_args(carry)
    res = pl.pallas_call(
        _carried(kern, carry, 8, 7, *_grid_ends((T // tm,))), name=name, grid=(T // tm,),
        in_specs=[row(2 * D_FF), pl.BlockSpec(w_t.shape, lambda i: (0, 0), pipeline_mode=pl.Buffered(1)), row(D),
                  row(D), row(D), _full_spec(gpre.shape), _full_spec(mod.shape), _full_spec(gpost.shape)] + ci,
        out_specs=[row(D), row(D), _full_spec((2, D)), _full_spec((2, D)), _full_spec((1, D)), _full_spec((2, D)),
                   _full_spec((1, D))] + co,
        out_shape=[_sds((T, D), F32), _sds((T, D), BF16), _sds((2, D), F32), _sds((2, D), F32), _sds((1, D), F32),
                   _sds((2, D), F32), _sds((1, D), F32)] + cs,
        scratch_shapes=cscr, compiler_params=_params(),
    )(df, w_t, x1, dres, mat, gpre, mod, gpost, *ca)
    return res if carry is None else (res[:7], res[7:])


def _out_bwd_fused(name, dm, w_out, p, ya, yb):
    T = dm.shape[0]
    tm = T // 8
    row = lambda w_: pl.BlockSpec((tm, w_), lambda i: (i, 0))

    def kern(dm_ref, w_ref, g0, g1, g2, g3, ya_ref, yb_ref, dya_ref, dyb_ref, dgl_ref):
        dz = _dot(dm_ref[...], w_ref[...], NT)
        ga = _sigmoid(jnp.concatenate([g0[...], g1[...]], axis=1).astype(F32))
        gb = _sigmoid(jnp.concatenate([g2[...], g3[...]], axis=1).astype(F32))
        dya_ref[...] = (dz * ga).astype(BF16)
        dyb_ref[...] = (dz * gb).astype(BF16)
        dgl_ref[...] = jnp.concatenate([dz * ya_ref[...].astype(F32) * ga * (1.0 - ga),
                                        dz * yb_ref[...].astype(F32) * gb * (1.0 - gb)], axis=1).astype(BF16)

    return pl.pallas_call(
        kern, name=name, grid=(T // tm,),
        in_specs=[row(D), _full_spec(w_out.shape)]
                 + [pl.BlockSpec((tm, GLB), lambda i, q=q: (i, COL_GL // GLB + q)) for q in range(4)] + [row(D), row(D)],
        out_specs=[row(D), row(D), row(2 * D)],
        out_shape=[_sds((T, D), BF16), _sds((T, D), BF16), _sds((T, 2 * D), BF16)],
        compiler_params=_params(),
    )(dm, w_out, p, p, p, p, ya, yb)


AB = 128
CTX_BLKS = CTX // AB


def _rope_tables(S):
    pos = jnp.arange(S, dtype=jnp.int32)
    inv = ROPE_BASE ** (-jnp.arange(N_FREQ, dtype=F32) / N_FREQ)
    ang_r = (pos // GRID_W).astype(F32)[:, None] * inv[None, :]
    ang_c = (pos % GRID_W).astype(F32)[:, None] * inv[None, :]
    cos = jnp.concatenate([jnp.cos(ang_r)] * 2 + [jnp.cos(ang_c)] * 2, axis=1)
    sin = jnp.concatenate([-jnp.sin(ang_r), jnp.sin(ang_r), -jnp.sin(ang_c), jnp.sin(ang_c)], axis=1)
    return cos, sin


def _rope(x, cos, sin):
    w = x.shape[1]
    reps = w // HEAD
    lane = lax.broadcasted_iota(jnp.int32, x.shape, 1)
    partner = jnp.where((lane & 63) < 32, pltpu.roll(x, w - 32, 1), pltpu.roll(x, 32, 1))
    return x * jnp.tile(cos, (1, reps)) + partner * jnp.tile(sin, (1, reps))


def _unrope(dx, cos, sin):
    w = dx.shape[1]
    reps = w // HEAD
    lane = lax.broadcasted_iota(jnp.int32, dx.shape, 1)
    t = dx * jnp.tile(sin, (1, reps))
    partner = jnp.where((lane & 63) < 32, pltpu.roll(t, w - 32, 1), pltpu.roll(t, 32, 1))
    return dx * jnp.tile(cos, (1, reps)) + partner


def _qkv_prep(name, p, cos, sin, S):
    T = CTX + S
    nt = T // AB
    KW = N_KV * HEAD

    def with_ones(v):
        ones = jnp.ones((AB, HEAD), BF16)
        return jnp.concatenate([v[:, kh * HEAD:(kh + 1) * HEAD] if part == 0 else ones
                                for kh in range(N_KV) for part in range(2)], axis=1)

    def kern(q_ref, k_ref, v_ref, cos_ref, sin_ref, qa_ref, kp_ref, vp_ref, kc_ref, vc_ref):
        i = pl.program_id(0)
        cos_v, sin_v = cos_ref[...], sin_ref[...]
        @pl.when(i < CTX_BLKS)
        def _():
            qa_ref[...] = (q_ref[...].astype(F32) * ATT_SCALE).astype(BF16)
            kc_ref[...] = k_ref[...]
            vc_ref[...] = with_ones(v_ref[...])

        @pl.when((i < CTX_BLKS) | (i >= nt))
        def _():
            kp_ref[...] = jnp.zeros(kp_ref.shape, BF16)
            vp_ref[...] = jnp.zeros(vp_ref.shape, BF16)

        @pl.when((i >= CTX_BLKS) & (i < nt))
        def _():
            qa_ref[...] = (_rope(q_ref[...].astype(F32), cos_v, sin_v) * ATT_SCALE).astype(BF16)
            kp_ref[...] = _rope(k_ref[...].astype(F32), cos_v, sin_v).astype(BF16)
            vp_ref[...] = with_ones(v_ref[...])

    tok = lambda i: jnp.minimum(i, nt - 1)
    lat_map = lambda i: (jnp.clip(i - CTX_BLKS, 0, nt - CTX_BLKS - 1), 0)
    ctx_map = lambda i: (jnp.minimum(i, CTX_BLKS - 1), 0)
    return pl.pallas_call(
        kern, name=name, grid=(nt + CTX_BLKS,),
        in_specs=[pl.BlockSpec((AB, N_Q * HEAD), lambda i: (tok(i), COL_Q // (N_Q * HEAD))),
                  pl.BlockSpec((AB, KW), lambda i: (tok(i), COL_K // KW)),
                  pl.BlockSpec((AB, KW), lambda i: (tok(i), COL_V // KW)),
                  pl.BlockSpec((AB, HEAD), lat_map), pl.BlockSpec((AB, HEAD), lat_map)],
        out_specs=[pl.BlockSpec((AB, N_Q * HEAD), lambda i: (tok(i), 0)),
                   pl.BlockSpec((AB, KW), lambda i: (i, 0)), pl.BlockSpec((AB, 2 * KW), lambda i: (i, 0)),
                   pl.BlockSpec((AB, KW), ctx_map), pl.BlockSpec((AB, 2 * KW), ctx_map)],
        out_shape=[_sds((T, N_Q * HEAD), BF16), _sds((S + 2 * CTX, KW), BF16), _sds((S + 2 * CTX, 2 * KW), BF16),
                   _sds((CTX, KW), BF16), _sds((CTX, 2 * KW), BF16)],
        compiler_params=_params(),
    )(p, p, p, cos, sin)


GW = Q_PER_KV * HEAD


def _band_bias(S):
    r = jnp.arange(AB, dtype=jnp.int32)[:, None]
    c = jnp.arange(3 * AB, dtype=jnp.int32)[None, :]
    near = jnp.abs(c - AB - r) <= AB
    valid = jnp.stack([near & (c >= AB), near, near & (c < 2 * AB)])
    return jnp.where(valid, 0.0, NEG_INF).astype(F32)


def _bias_spec(S):
    nb = S // AB
    return pl.BlockSpec((None, AB, 3 * AB), lambda kh, n: (jnp.where(n == 0, 0, jnp.where(n == nb - 1, 2, 1)), 0, 0))


def _head_probs(q, sink, kc, vce, kb, vbe, bias):
    s_c = _dot(q, kc, NT)
    m = jnp.maximum(jnp.max(s_c, axis=-1, keepdims=True), sink)
    if kb is not None:
        s_b = _dot(q, kb, NT) + bias
        m = jnp.maximum(m, jnp.max(s_b, axis=-1, keepdims=True))
    p_c = jnp.exp(s_c - m).astype(BF16)
    acc = _dot(p_c, vce)
    p_b = None
    if kb is not None:
        p_b = jnp.exp(s_b - m).astype(BF16)
        acc = acc + _dot(p_b, vbe)
    return p_c, p_b, m, acc


def _attn_fwd(name, qa, kc, vc, sink4, S, band=None, prev=None, carry=None):
    T = qa.shape[0]
    has_band = band is not None
    nq = S // AB if has_band else CTX_BLKS
    q_off = CTX_BLKS if has_band else 0

    def kern(*refs):
        q_ref, kc_ref, vc_ref, sink_ref = refs[:4]
        rest = refs[4:]
        o_ref = rest[-1]
        n = pl.program_id(1)
        kc_v, vce = kc_ref[...], vc_ref[...]
        kb = vbe = bias = None
        if has_band:
            kp_ref, vp_ref, bias_ref = rest[:3]
            start = pl.multiple_of(n * AB + (CTX - AB), AB)
            kb = kp_ref[pl.ds(start, 3 * AB), :]
            vbe = vp_ref[pl.ds(start, 3 * AB), :]
            bias = bias_ref[...]
        outs = []
        for g in range(Q_PER_KV):
            sink = sink_ref[g:g + 1, 0:1]
            _, _, m, acc = _head_probs(q_ref[:, g * HEAD:(g + 1) * HEAD], sink, kc_v, vce, kb, vbe, bias)
            l = acc[:, HEAD:] + jnp.exp(sink - m)
            outs.append(acc[:, :HEAD] / l)
        o_ref[...] = jnp.concatenate(outs, axis=1).astype(BF16)

    in_specs = [pl.BlockSpec((AB, GW), lambda kh, n: (n + q_off, kh)),
                pl.BlockSpec((CTX, HEAD), lambda kh, n: (0, kh)), pl.BlockSpec((CTX, 2 * HEAD), lambda kh, n: (0, kh)),
                pl.BlockSpec((None, Q_PER_KV, HEAD), lambda kh, n: (kh, 0, 0))]
    args = [qa, kc, vc, sink4]
    if has_band:
        in_specs += [pl.BlockSpec((S + 2 * CTX, HEAD), lambda kh, n: (0, kh)),
                     pl.BlockSpec((S + 2 * CTX, 2 * HEAD), lambda kh, n: (0, kh)), _bias_spec(S)]
        args += list(band)
    alias = {}
    if prev is not None:
        in_specs.append(ANY)
        alias = {len(args): 0}
        args.append(prev)
    ci, ca, co, cs, cscr = _carry_args(carry)
    res = pl.pallas_call(
        _carried(kern, carry, len(args), 1, *_grid_ends((N_KV, nq))), name=name, grid=(N_KV, nq),
        in_specs=in_specs + ci,
        out_specs=[pl.BlockSpec((AB, GW), lambda kh, n: (n + q_off, kh))] + co,
        out_shape=[_sds((T, N_Q * HEAD), BF16)] + cs, input_output_aliases=alias, scratch_shapes=cscr,
        compiler_params=_params(("arbitrary", "arbitrary")),
    )(*args, *ca)
    return res[0] if carry is None else (res[0], res[1:])


def _attn_bwd(name, qa, kc, vc, sink4, o_all, do_all, S, band=None, prev_dq=None, carry=None):
    T = qa.shape[0]
    has_band = band is not None
    nq = S // AB if has_band else CTX_BLKS
    q_off = CTX_BLKS if has_band else 0
    KW = N_KV * HEAD

    def kern(*refs):
        q_ref, kc_ref, vc_ref, sink_ref, o_ref, do_ref = refs[:6]
        rest = refs[6:]
        if has_band:
            kp_ref, vp_ref, bias_ref = rest[:3]
            rest = rest[3:]
        if prev_dq is not None:
            rest = rest[1:]
        dq_ref, dkc_ref, dvc_ref, dsink_ref = rest[:4]
        n = pl.program_id(1)

        @pl.when(n == 0)
        def _():
            dkc_ref[...] = jnp.zeros(dkc_ref.shape, F32)
            dvc_ref[...] = jnp.zeros(dvc_ref.shape, F32)
            dsink_ref[...] = jnp.zeros(dsink_ref.shape, F32)
            if has_band:
                rest[4][...] = jnp.zeros(rest[4].shape, F32)
                rest[5][...] = jnp.zeros(rest[5].shape, F32)

        kc_v, vce = kc_ref[...], vc_ref[...]
        vc_v = vce[:, :HEAD]
        kb = vbe = vb = bias = None
        if has_band:
            start = pl.multiple_of(n * AB + (CTX - AB), AB)
            kb = kp_ref[pl.ds(start, 3 * AB), :]
            vbe = vp_ref[pl.ds(start, 3 * AB), :]
            vb = vbe[:, :HEAD]
            bias = bias_ref[...]
        stack = lambda ref: jnp.concatenate([ref[:, g * HEAD:(g + 1) * HEAD] for g in range(Q_PER_KV)], axis=0)
        q4, do4 = stack(q_ref), stack(do_ref)
        sink = jnp.concatenate([jnp.broadcast_to(sink_ref[g:g + 1, 0:1], (AB, 1)) for g in range(Q_PER_KV)], axis=0)
        s_c = _dot(q4, kc_v, NT)
        m = jnp.maximum(jnp.max(s_c, axis=-1, keepdims=True), sink)
        if has_band:
            s_b = _dot(q4, kb, NT) + jnp.tile(bias, (Q_PER_KV, 1))
            m = jnp.maximum(m, jnp.max(s_b, axis=-1, keepdims=True))
        p_c = jnp.exp(s_c - m).astype(BF16).astype(F32)
        p_sink = jnp.exp(sink - m)
        l = jnp.sum(p_c, axis=-1, keepdims=True) + p_sink
        if has_band:
            p_b = jnp.exp(s_b - m).astype(BF16).astype(F32)
            l = l + jnp.sum(p_b, axis=-1, keepdims=True)
        inv = 1.0 / l
        delta = jnp.sum(do4.astype(F32) * stack(o_ref).astype(F32), axis=-1, keepdims=True)
        do4b = do4.astype(BF16)
        pn_c = (p_c * inv).astype(BF16)
        ds_c = (p_c * inv * (_dot(do4b, vc_v, NT) - delta)).astype(BF16)
        dq4 = _dot(ds_c, kc_v)
        dkc_ref[...] += _dot(ds_c, q4, TN)
        dvc_ref[...] += _dot(pn_c, do4b, TN)
        if has_band:
            pn_b = (p_b * inv).astype(BF16)
            ds_b = (p_b * inv * (_dot(do4b, vb, NT) - delta)).astype(BF16)
            dq4 = dq4 + _dot(ds_b, kb)
            rest[4][pl.ds(start, 3 * AB), :] += _dot(ds_b, q4, TN)
            rest[5][pl.ds(start, 3 * AB), :] += _dot(pn_b, do4b, TN)
        dq4 = dq4 * ATT_SCALE
        dq_ref[...] = jnp.concatenate([dq4[g * AB:(g + 1) * AB, :] for g in range(Q_PER_KV)], axis=1)
        ps = p_sink * inv * delta
        dsink_ref[...] += jnp.concatenate(
            [jnp.broadcast_to(-jnp.sum(ps[g * AB:(g + 1) * AB, :], axis=0, keepdims=True), (1, HEAD))
             for g in range(Q_PER_KV)], axis=0)

    q_spec = pl.BlockSpec((AB, GW), lambda kh, n: (n + q_off, kh))
    c_spec = pl.BlockSpec((CTX, HEAD), lambda kh, n: (0, kh))
    ce_spec = pl.BlockSpec((CTX, 2 * HEAD), lambda kh, n: (0, kh))
    s_spec = pl.BlockSpec((None, Q_PER_KV, HEAD), lambda kh, n: (kh, 0, 0))
    in_specs = [q_spec, c_spec, ce_spec, s_spec, q_spec, q_spec]
    args = [qa, kc, vc, sink4, o_all, do_all]
    out_specs = [q_spec, c_spec, c_spec, s_spec]
    out_shape = [_sds((T, N_Q * HEAD), F32), _sds((CTX, KW), F32), _sds((CTX, KW), F32), _sds((N_KV, Q_PER_KV, HEAD), F32)]
    if has_band:
        p_spec = pl.BlockSpec((S + 2 * CTX, HEAD), lambda kh, n: (0, kh))
        in_specs += [p_spec, pl.BlockSpec((S + 2 * CTX, 2 * HEAD), lambda kh, n: (0, kh)), _bias_spec(S)]
        args += list(band)
        out_specs += [p_spec, p_spec]
        out_shape += [_sds((S + 2 * CTX, KW), F32)] * 2
    alias = {}
    if prev_dq is not None:
        in_specs.append(ANY)
        alias = {len(args): 0}
        args.append(prev_dq)
    ci, ca, co, cs, cscr = _carry_args(carry)
    n_out = len(out_specs)
    res = pl.pallas_call(
        _carried(kern, carry, len(args), n_out, *_grid_ends((N_KV, nq))), name=name, grid=(N_KV, nq),
        in_specs=in_specs + ci, out_specs=out_specs + co, out_shape=out_shape + cs, scratch_shapes=cscr,
        input_output_aliases=alias, compiler_params=_params(("arbitrary", "arbitrary")),
    )(*args, *ca)
    return res if carry is None else (res[:n_out], res[n_out:])


def _dqkv_assemble(name, dq_all, dkp, dvp, dkc_l, dvc_l, dkc_c, dvc_c, cos, sin, S):
    T = CTX + S
    KW = N_KV * HEAD
    HALF = N_Q * HEAD // 2

    def kern(dq_ref, dkp_ref, dvp_ref, dkcl_ref, dvcl_ref, dkcc_ref, dvcc_ref, cos_ref, sin_ref, out_ref):
        i = pl.program_id(0)
        j = pl.program_id(1)
        cos_v, sin_v = cos_ref[...], sin_ref[...]

        @pl.when((j < 2) & (i == 0))
        def _():
            out_ref[...] = dq_ref[...].astype(BF16)

        @pl.when((j < 2) & (i > 0))
        def _():
            out_ref[...] = _unrope(dq_ref[...], cos_v, sin_v).astype(BF16)

        @pl.when((j == 2) & (i == 0))
        def _():
            out_ref[...] = jnp.concatenate([dkcl_ref[...] + dkcc_ref[...], dvcl_ref[...] + dvcc_ref[...]],
                                           axis=1).astype(BF16)

        @pl.when((j == 2) & (i > 0))
        def _():
            out_ref[...] = jnp.concatenate([_unrope(dkp_ref[...], cos_v, sin_v), dvp_ref[...]], axis=1).astype(BF16)

    same = lambda i, j: (i, 0)
    lat_map = lambda i, j: (jnp.maximum(i - 1, 0), 0)
    ctx_map = lambda i, j: (0, 0)
    return pl.pallas_call(
        kern, name=name, grid=(T // TR, 3),
        in_specs=[pl.BlockSpec((TR, HALF), lambda i, j: (i, jnp.minimum(j, 1))),
                  pl.BlockSpec((TR, KW), same), pl.BlockSpec((TR, KW), same),
                  pl.BlockSpec((CTX, KW), ctx_map), pl.BlockSpec((CTX, KW), ctx_map),
                  pl.BlockSpec((CTX, KW), ctx_map), pl.BlockSpec((CTX, KW), ctx_map),
                  pl.BlockSpec((TR, HEAD), lat_map), pl.BlockSpec((TR, HEAD), lat_map)],
        out_specs=pl.BlockSpec((TR, HALF), lambda i, j: (i, COL_Q // HALF + j)),
        out_shape=_sds((T, DP_W), BF16), compiler_params=_params(("arbitrary", "arbitrary")),
    )(dq_all, dkp, dvp, dkc_l, dvc_l, dkc_c, dvc_c, cos, sin)


RB = 128
CH = 256
HALO = 8
SUB = 8
GRP = 8


def _vscan(a, b, reverse):
    row = lax.broadcasted_iota(jnp.int32, a.shape, 0)
    A, H = a, b
    for s in (1, 2, 4):
        sh = SUB - s if reverse else s
        m = (row < SUB - s) if reverse else (row >= s)
        As = pltpu.roll(A, sh, 0)
        Hs = pltpu.roll(H, sh, 0)
        H = jnp.where(m, A * Hs + H, H)
        A = jnp.where(m, A * As, A)
    return A, H


def _scan_rows(a_ref, b_ref, r0, nrows, reverse, carry, emit):
    ngrp = nrows // (SUB * GRP)
    row = lax.broadcasted_iota(jnp.int32, (SUB, RB), 0)

    def grp(gi, carry):
        g = (ngrp - 1 - gi) if reverse else gi
        base = r0 + g * (SUB * GRP)
        for v in (range(GRP - 1, -1, -1) if reverse else range(GRP)):
            rs = pl.multiple_of(base + v * SUB, SUB)
            A, H = _vscan(a_ref[pl.ds(rs, SUB), :], b_ref[pl.ds(rs, SUB), :], reverse)
            hf = H + A * carry
            if reverse:
                before = jnp.where(row == SUB - 1, carry, pltpu.roll(hf, SUB - 1, 0))
                carry = hf[0:1, :]
            else:
                before = jnp.where(row == 0, carry, pltpu.roll(hf, 1, 0))
                carry = hf[SUB - 1:SUB, :]
            emit(rs, hf, before)
        return carry

    return lax.fori_loop(0, ngrp, grp, carry)


def _pad_start(ci):
    return pl.multiple_of(ci * CH + HALO * jnp.minimum(ci, 1), HALO)


def _conv_taps(ext, transpose=False):
    n = CH + 2 * HALO
    taps = []
    for k in range(CONV_W):
        off = CONV_LEFT - k if transpose else k - CONV_LEFT
        taps.append(ext[HALO:HALO + CH, :] if off == 0 else pltpu.roll(ext, (-off) % n, 0)[HALO:HALO + CH, :])
    return taps


def _lru_gates(xl, w4, b4, ls):
    pre = _dot(xl.astype(BF16), w4) + b4
    out = []
    for d in range(2):
        r = _sigmoid(pre[:, d * RB:(d + 1) * RB])
        i = _sigmoid(pre[:, (2 + d) * RB:(3 + d) * RB])
        la = LRU_C * r * ls[d:d + 1, :]
        a = jnp.exp(la)
        q = -jnp.tanh(la) * (1.0 + a * a)
        out.append((r, i, a, q))
    return out


def _rnn_specs(T):
    col = lambda n, *_: (0, n)
    return dict(
        xr=pl.BlockSpec((T, RB), lambda n, *_: (0, COL_XR // RB + n)),
        gr=pl.BlockSpec((T, RB), lambda n, *_: (0, COL_GR // RB + n)),
        act=pl.BlockSpec((T, RB), col),
        cw=pl.BlockSpec((CONV_W, RB), col), cb=pl.BlockSpec((1, RB), col),
        w4=pl.BlockSpec((None, RB, 4 * RB), lambda n, *_: (n, 0, 0)),
        b4=pl.BlockSpec((None, 1, 4 * RB), lambda n, *_: (n, 0, 0)),
        lam=pl.BlockSpec((2, RB), col))


PAD_ROWS = 3 * HALO


def _zero_pads(pad_ref, T):
    for r in (0, HALO + CTX, 2 * HALO + T):
        pad_ref[r:r + HALO, :] = jnp.zeros((HALO, RB), F32)


def _fill_padded(pad_ref, src_ref, T):
    _zero_pads(pad_ref, T)
    pad_ref[HALO:HALO + CTX, :] = src_ref[0:CTX, :].astype(F32)
    pad_ref[2 * HALO + CTX:2 * HALO + T, :] = src_ref[CTX:T, :].astype(F32)


def _pad_rows(ci):
    return pl.ds(pl.multiple_of(ci * CH + HALO + HALO * jnp.minimum(ci, 1), HALO), CH)


def _rnn_fwd(name, p, cw, cb, w4, b4, lam, T, carry=None):
    def kern(xr_ref, gr_ref, cw_ref, cb_ref, w4_ref, b4_ref, lam_ref,
             u_ref, a0, a1, yo_ref, hpf_ref, hpb_ref, r0_ref, r1_ref, i0_ref, i1_ref, xpad, b0, b1, y):
        _fill_padded(xpad, xr_ref, T)
        ls = _log_sigmoid(lam_ref[...])
        w4v, b4v, cwv, cbv = w4_ref[...], b4_ref[...], cw_ref[...], cb_ref[...]

        def chunk(ci, _):
            rows = pl.ds(pl.multiple_of(ci * CH, CH), CH)
            taps = _conv_taps(xpad[pl.ds(_pad_start(ci), CH + 2 * HALO), :])
            xl = cbv + sum(taps[k] * cwv[k:k + 1, :] for k in range(CONV_W))
            for d, (r, i, a, q) in enumerate(_lru_gates(xl, w4v, b4v, ls)):
                (a0, a1)[d][rows, :] = a
                (b0, b1)[d][rows, :] = jnp.sqrt(q) * (i * xl)
                (r0_ref, r1_ref)[d][rows, :] = r.astype(BF16)
                (i0_ref, i1_ref)[d][rows, :] = i.astype(BF16)
            return 0

        lax.fori_loop(0, T // CH, chunk, 0)
        zero = jnp.zeros((1, RB), F32)

        def emit_f(rs, hf, before):
            y[pl.ds(rs, SUB), :] = hf
            b0[pl.ds(rs, SUB), :] = before

        def emit_b(rs, hf, before):
            y[pl.ds(rs, SUB), :] += hf
            b1[pl.ds(rs, SUB), :] = before

        _scan_rows(a0, b0, 0, T, False, zero, emit_f)
        c = _scan_rows(a1, b1, 0, CTX, True, zero, emit_b)
        _scan_rows(a1, b1, CTX, T - CTX, True, c, emit_b)

        def finish(ci, _):
            rows = pl.ds(pl.multiple_of(ci * CH, CH), CH)
            yv = y[rows, :]
            u_ref[rows, :] = (yv * _gelu(gr_ref[rows, :].astype(F32))).astype(BF16)
            yo_ref[rows, :] = yv.astype(BF16)
            hpf_ref[rows, :] = b0[rows, :].astype(BF16)
            hpb_ref[rows, :] = b1[rows, :].astype(BF16)
            return 0

        lax.fori_loop(0, T // CH, finish, 0)

    sp = _rnn_specs(T)
    ci, ca, co, cs, cscr = _carry_args(carry)
    dts = [BF16, F32, F32] + [BF16] * 7
    res = pl.pallas_call(
        _carried(kern, carry, 7, 10, *_grid_ends((N_RNN_BLOCKS,))), name=name, grid=(N_RNN_BLOCKS,),
        in_specs=[sp["xr"], sp["gr"], sp["cw"], sp["cb"], sp["w4"], sp["b4"], sp["lam"]] + ci,
        out_specs=[sp["act"]] * 10 + co,
        out_shape=[_sds((T, D), dt) for dt in dts] + cs,
        scratch_shapes=[pltpu.VMEM((T + PAD_ROWS, RB), F32)] + [pltpu.VMEM((T, RB), F32)] * 3 + cscr,
        compiler_params=_params(),
    )(p, p, cw, cb, w4, b4, lam, *ca)
    return res if carry is None else (res[:10], res[10:])


def _rnn_bwd(name, p, du, saved, dp, cw, cb, w4, b4, lam, T, carry=None):
    def kern(xr_ref, gr_ref, du_ref, a0, a1, y_ref, hpf_ref, hpb_ref, r0_ref, r1_ref, i0_ref, i1_ref,
             cw_ref, cb_ref, w4_ref, b4_ref, lam_ref, dp_in,
             dp_ref, dcw_ref, dcb_ref, dw4_ref, db4_ref, dlam_ref,
             xpad, dxpad, c0, c1, dy, dgr_ref):
        j = pl.program_id(1)

        @pl.when(j == 0)
        def _():
            work(xr_ref, gr_ref, du_ref, a0, a1, y_ref, (hpf_ref, hpb_ref), (r0_ref, r1_ref), (i0_ref, i1_ref),
                 cw_ref, cb_ref, w4_ref, lam_ref, dp_ref, dgr_ref, dcw_ref, dcb_ref, dw4_ref, db4_ref, dlam_ref,
                 xpad, dxpad, c0, c1, dy)

        @pl.when(j == 1)
        def _():
            dp_ref[...] = dgr_ref[...]

    def work(xr_ref, gr_ref, du_ref, a0, a1, y_ref, hp_refs, r_refs, i_refs, cw_ref, cb_ref, w4_ref, lam_ref,
             dxr_ref, dgr_ref, dcw_ref, dcb_ref, dw4_ref, db4_ref, dlam_ref, xpad, dxpad, c0, c1, dy):
        _fill_padded(xpad, xr_ref, T)
        _zero_pads(dxpad, T)
        lam_v = lam_ref[...]
        ls = _log_sigmoid(lam_v)
        w4v, cwv, cbv = w4_ref[...], cw_ref[...], cb_ref[...]

        def conv_chunk(ci):
            taps = _conv_taps(xpad[pl.ds(_pad_start(ci), CH + 2 * HALO), :])
            return taps, cbv + sum(taps[k] * cwv[k:k + 1, :] for k in range(CONV_W))

        def phase_a(ci, _):
            rows = pl.ds(pl.multiple_of(ci * CH, CH), CH)
            gr = gr_ref[rows, :].astype(F32)
            duv = du_ref[rows, :].astype(F32)
            dyv = duv * _gelu(gr)
            dgr_ref[rows, :] = (duv * y_ref[rows, :].astype(F32) * _gelu_grad(gr)).astype(BF16)
            dy[rows, :] = dyv
            c0[rows, :] = a0[rows, :] * dyv
            c1[rows, :] = a1[rows, :] * dyv
            return 0

        lax.fori_loop(0, T // CH, phase_a, 0)
        zero = jnp.zeros((1, RB), F32)

        def emit0(rs, hf, before):
            c0[pl.ds(rs, SUB), :] = dy[pl.ds(rs, SUB), :] + before

        def emit1(rs, hf, before):
            c1[pl.ds(rs, SUB), :] = dy[pl.ds(rs, SUB), :] + before

        _scan_rows(a0, c0, 0, T, True, zero, emit0)
        c = _scan_rows(a1, c1, CTX, T - CTX, False, zero, emit1)
        _scan_rows(a1, c1, 0, CTX, False, c, emit1)

        dw4_ref[...] = jnp.zeros(dw4_ref.shape, F32)
        db4_ref[...] = jnp.zeros(db4_ref.shape, F32)
        dlam_ref[...] = jnp.zeros(dlam_ref.shape, F32)
        dcw_ref[...] = jnp.zeros(dcw_ref.shape, F32)
        dcb_ref[...] = jnp.zeros(dcb_ref.shape, F32)

        def phase_c(ci, _):
            base = pl.multiple_of(ci * CH, CH)
            rows = pl.ds(base, CH)
            _, xl = conv_chunk(ci)
            dxl = jnp.zeros((CH, RB), F32)
            dpre_a, dpre_x, dls = [], [], []
            for d in range(2):
                a = (a0, a1)[d][rows, :]
                r = r_refs[d][rows, :].astype(F32)
                i = i_refs[d][rows, :].astype(F32)
                q = -jnp.tanh(LRU_C * r * ls[d:d + 1, :]) * (1.0 + a * a)
                g = (c0, c1)[d][rows, :]
                hp = hp_refs[d][rows, :].astype(F32)
                gm = g * jnp.sqrt(q)
                di = gm * xl
                dxl = dxl + gm * i
                dla = a * (g * hp - a * (g * (i * xl)) * lax.rsqrt(q))
                dr = dla * (LRU_C * ls[d:d + 1, :])
                dls.append(_colsum(dla * (LRU_C * r)))
                dpre_a.append(dr * r * (1.0 - r))
                dpre_x.append(di * i * (1.0 - i))
            dpre = jnp.concatenate(dpre_a + dpre_x, axis=1)
            dpre_b = dpre.astype(BF16)
            dxl = dxl + _dot(dpre_b, w4v, NT)
            dw4_ref[...] += _dot(xl.astype(BF16), dpre_b, TN)
            db4_ref[...] += _colsum(dpre)
            dlam_ref[...] += jnp.concatenate(dls, axis=0)
            dcb_ref[...] += _colsum(dxl)
            dxpad[_pad_rows(ci), :] = dxl
            return 0

        lax.fori_loop(0, T // CH, phase_c, 0)
        dlam_ref[...] = dlam_ref[...] * _sigmoid(-lam_v)

        def phase_d(ci, _):
            base = pl.multiple_of(ci * CH, CH)
            rows = pl.ds(base, CH)
            xtaps, _ = conv_chunk(ci)
            dtaps = _conv_taps(dxpad[pl.ds(_pad_start(ci), CH + 2 * HALO), :], transpose=True)
            dxl = dxpad[_pad_rows(ci), :]
            dxr_ref[rows, :] = sum(dtaps[k] * cwv[k:k + 1, :] for k in range(CONV_W)).astype(BF16)
            dcw_ref[...] += jnp.concatenate([_colsum(dxl * xtaps[k]) for k in range(CONV_W)], axis=0)
            return 0

        lax.fori_loop(0, T // CH, phase_d, 0)

    sp = _rnn_specs(T)
    dp_spec = pl.BlockSpec((T, RB), lambda n, j: (0, COL_XR // RB + n + j * (COL_GR - COL_XR) // RB))
    ci, ca, co, cs, cscr = _carry_args(carry)
    n_in = 3 + len(saved) + 5 + 1
    res = pl.pallas_call(
        _carried(kern, carry, n_in, 6, *_grid_ends((N_RNN_BLOCKS, 2))), name=name, grid=(N_RNN_BLOCKS, 2),
        in_specs=[sp["xr"], sp["gr"]] + [sp["act"]] * (1 + len(saved)) + [sp["cw"], sp["cb"], sp["w4"], sp["b4"],
                                                                           sp["lam"], ANY] + ci,
        out_specs=[dp_spec, sp["cw"], sp["cb"], sp["w4"], sp["b4"], sp["lam"]] + co,
        out_shape=[_sds((T, DP_W), BF16), _sds((CONV_W, D), F32), _sds((1, D), F32),
                   _sds((N_RNN_BLOCKS, RB, 4 * RB), F32), _sds((N_RNN_BLOCKS, 1, 4 * RB), F32), _sds((2, D), F32)] + cs,
        scratch_shapes=([pltpu.VMEM((T + PAD_ROWS, RB), F32)] * 2 + [pltpu.VMEM((T, RB), F32)] * 3
                        + [pltpu.VMEM((T, RB), BF16)] + cscr),
        input_output_aliases={n_in - 1: 0},
        compiler_params=_params(("arbitrary", "arbitrary")),
    )(p, p, du, *saved, cw, cb, w4, b4, lam, dp, *ca)
    return res if carry is None else (res[:6], res[6:])


class _Plan:
    def __init__(self, shards, Ws):
        L = len(Ws)
        self.shards, self.Ws = shards, Ws
        self.Gs = [None] * L
        self.slots = [dict() for _ in range(L)]
        self.gate_slots = [None] * L
        self.table = {}
        for l in range(L):
            t = f"l{l}_"
            self.table[t + "proj"] = [("gather", l, k) for k in ("wo_rnn", "wo_attn", "wout")]
            self.table[t + "rnn_fwd"] = [("gather", l, "wffn_in_t")]
            self.table[t + "attn_lat_fwd"] = [("gather", l + 1, "win_t")] if l + 1 < L else []
            self.table[t + "ffn_in"] = [("gather", l, "wffn_out")]
            self.table[t + "ffn_in_dx"] = [("scatter", l, "wffn_out")]
            self.table[t + "attn_lat_bwd"] = [("scatter", l, "wffn_in_t")]
            self.table[t + "ffn_in_dw"] = [("gates", l + 1, "w4")] if l + 1 < L else []
            self.table[t + "rnn_bwd"] = ([("scatter", l, k) for k in ("wout", "wo_attn", "wo_rnn")]
                                         + ([("scatter", l + 1, "win_t")] if l + 1 < L else []))
        self.table["l0_proj_dx"] = [("scatter", 0, "win_t_a")]
        self.table["l0_proj_dw_b"] = [("gates", 0, "w4")]

    def carry(self, name):
        jobs = []
        for kind, l, k in self.table.get(name, []):
            if kind == "gather":
                jobs.append(("gather", self.shards[l][k]))
            elif kind == "scatter":
                jobs.append(("scatter", self.Gs[l][k].reshape(N_DEV, -1, self.Gs[l][k].shape[-1])))
            else:
                jobs.append(("gather", self.Gs[l]["w4"].reshape(N_RNN_BLOCKS * RB, 4 * RB).astype(BF16)))
        return _Carry(jobs) if jobs else None

    def done(self, name, got):
        for (kind, l, k), res in zip(self.table[name], got):
            if kind == "gather":
                self.Ws[l][k] = res.reshape(-1, D)
            elif kind == "scatter":
                self.slots[l][k] = res
            else:
                self.gate_slots[l] = res


def _run(X, fn, name, *args, **kw):
    carry = None if X is None else X.carry(name)
    if carry is None:
        return fn(name, *args, **kw)
    out, got = fn(name, *args, carry=carry, **kw)
    X.done(name, got)
    return out


def _layer_fwd(l, xa, h, W, rope, S, nxt, X=None):
    T = xa.shape[0]
    tag = f"l{l}_"
    cos, sin, bias = rope
    p = _run(X, _mm_act, tag + "proj", h, W["win_t"], "NT", BF16)
    u, *rnn_saved = _run(X, _rnn_fwd, tag + "rnn_fwd", p, W["cw"], W["cb"], W["w4"], W["b4"], W["lam"], T)
    qa, kp, vp, kc, vc = _qkv_prep(tag + "qkv_prep", p, cos, sin, S)
    o_all = _attn_fwd(tag + "attn_ctx_fwd", qa, kc, vc, W["sink4"], S)
    o_all = _run(X, _attn_fwd, tag + "attn_lat_fwd", qa, kc, vc, W["sink4"], S, band=(kp, vp, bias), prev=o_all)
    ya = _mm_act(tag + "o_rnn", u, W["wo_rnn"], "NN")
    yb = _mm_act(tag + "o_attn", o_all, W["wo_attn"], "NN")
    z, m, x1, h2 = _out_fused(tag + "out", p, ya, yb, xa, W["wout"], W["g_mix_post"], W["mod"], W["g_ffn_pre"])
    fg, fu, s = _run(X, _ffn_in_fused, tag + "ffn_in", h2, W["wffn_in_t"])
    e, *out = _ffn_out_fused(tag + "ffn_out", s, W["wffn_out"], x1, W["g_ffn_post"], W["mod"], nxt)
    saved = dict(xa=xa, h=h, p=p, u=u, rnn=rnn_saved, qa=qa, kp=kp, vp=vp, kc=kc, vc=vc, o_all=o_all,
                 ya=ya, yb=yb, z=z, m=m, x1=x1, h2=h2, fg=fg, fu=fu, s=s, e=e)
    return saved, out


def _layer_bwd(l, dx2, A, W, rope, S, X=None, loss_of=None):
    T = A["xa"].shape[0]
    tag = f"l{l}_"
    cos, sin, bias = rope
    G = {}
    if X is not None:
        X.Gs[l] = G
    if loss_of is None:
        de, df, dga2, G["g_ffn_post"] = _ffn_bwd_fused(tag + "ffn_bwd", A["fg"], A["fu"], W["wffn_out"],
                                                       head=(dx2, A["e"], W["g_ffn_post"], W["mod"]))
    else:
        dx2, de, dga2, G["g_ffn_post"], G["sq"] = _loss_resid_bwd(tag + "loss_ffn_resid_bwd", *loss_of, A["e"],
                                                                  W["g_ffn_post"], W["mod"], GA2)
        df, = _ffn_bwd_fused(tag + "ffn_bwd", A["fg"], A["fu"], W["wffn_out"], de=de)
    G["wffn_out"] = _mm_wgrad(tag + "ffn_out_dw", A["s"], de)
    dx1, dm, dsh2, dsc2, G["g_ffn_pre"], dga1, G["g_mix_post"] = _run(
        X, _ffn_in_bwd_fused, tag + "ffn_in_dx", df, W["wffn_in_t"], A["x1"], dx2, A["m"], W["g_ffn_pre"], W["mod"],
        W["g_mix_post"])
    G["wffn_in_t"] = _run(X, _mm_wgrad, tag + "ffn_in_dw", df, A["h2"])
    G["wout"] = _mm_wgrad(tag + "out_dw", A["z"], dm)
    dya, dyb, dgl = _out_bwd_fused(tag + "out_dx", dm, W["wout"], A["p"], A["ya"], A["yb"])
    do = _mm_act(tag + "o_attn_dx", dyb, W["wo_attn"], "NT")
    G["wo_attn"] = _mm_wgrad(tag + "o_attn_dw", A["o_all"], dyb)
    du = _mm_act(tag + "o_rnn_dx", dya, W["wo_rnn"], "NT")
    G["wo_rnn"] = _mm_wgrad(tag + "o_rnn_dw", A["u"], dya)
    dq_all, dkc_c, dvc_c, dsink_c = _attn_bwd(tag + "attn_ctx_bwd", A["qa"], A["kc"], A["vc"], W["sink4"],
                                               A["o_all"], do, S)
    dq_all, dkc_l, dvc_l, dsink_l, dkp, dvp = _run(
        X, _attn_bwd, tag + "attn_lat_bwd", A["qa"], A["kc"], A["vc"], W["sink4"], A["o_all"], do, S,
        band=(A["kp"], A["vp"], bias), prev_dq=dq_all)
    G["sink4"] = dsink_c + dsink_l
    dp = _dqkv_assemble(tag + "dqkv", dq_all, dkp, dvp, dkc_l, dvc_l, dkc_c, dvc_c, cos, sin, S)
    dp, G["cw"], G["cb"], G["w4"], G["b4"], G["lam"] = _run(
        X, _rnn_bwd, tag + "rnn_bwd", A["p"], du, A["rnn"], dp, W["cw"], W["cb"], W["w4"], W["b4"], W["lam"], T)
    proj_dx = (_proj_bwd_fused, tag + "proj_dx", dp, dgl, W["win_t"], A["xa"], dx1, W["g_mix_pre"], W["mod"])
    if X is not None and l == 0:
        G["win_t_a"] = _proj_wgrad(tag + "proj_dw_a", dp, dgl, A["h"][:, :D // 2])
        dxa, dsh1, dsc1, G["g_mix_pre"] = _run(X, *proj_dx)
        G["win_t_b"] = _run(X, _proj_wgrad, tag + "proj_dw_b", dp, dgl, A["h"][:, D // 2:])
    else:
        dxa, dsh1, dsc1, G["g_mix_pre"] = _run(X, *proj_dx)
        G["win_t"] = _proj_wgrad(tag + "proj_dw", dp, dgl, A["h"])
    G["mod"] = jnp.concatenate([dsh1, dsc1, dga1, dsh2, dsc2, dga2], axis=1)
    return dxa, G


def _local_step(xa, target, Ws, S, X=None):
    rope = (*_rope_tables(S), _band_bias(S))
    L = len(Ws)
    h = _normmod_fwd("l0_mix_norm", xa, Ws[0]["g_mix_pre"], Ws[0]["mod"], SH1, SC1)
    saved = []
    x = xa
    for l in range(L):
        nxt = (Ws[l + 1]["g_mix_pre"], Ws[l + 1]["mod"]) if l + 1 < L else None
        A, out = _layer_fwd(l, x, h, Ws[l], rope, S, nxt, X)
        saved.append(A)
        if l + 1 < L:
            x, h = out
    Gs = [None] * L
    dx = None
    for l in reversed(range(L)):
        dx, Gs[l] = _layer_bwd(l, dx, saved[l], Ws[l], rope, S, X, loss_of=(out[0], target) if l == L - 1 else None)
    return Gs[L - 1]["sq"], dx, Gs


MESH = pl.DeviceIdType.MESH


def _place():
    return lax.axis_index("x"), lax.axis_index("y"), lax.axis_index("c")


def _lin(px, py, pc):
    return 4 * px + 2 * py + pc


def _allgather_small(name, blk):
    m, n = blk.shape

    def body(x_ref, out_ref, send_sems, recv_sems, local_sem):
        x, y, c = _place()
        me, sibling = (x, y, c), (x, y, 1 - c)
        chips = [(1 - x, y), (x, 1 - y), (1 - x, 1 - y)]

        def copy(k, block, to, src=None):
            dst = out_ref.at[_lin(*block)]
            return pltpu.make_async_remote_copy(src_ref=dst if src is None else src, dst_ref=dst,
                                                send_sem=send_sems.at[k], recv_sem=recv_sems.at[k],
                                                device_id=to, device_id_type=MESH)

        mine = pltpu.make_async_copy(x_ref, out_ref.at[_lin(*me)], local_sem)
        mine.start()
        first = [copy(0, me, sibling, src=x_ref)]
        first += [copy(1 + j, me, (*chip, c), src=x_ref) for j, chip in enumerate(chips)]
        for cp in first:
            cp.start()
        passed = [copy(4 + j, (*chip, c), sibling) for j, chip in enumerate(chips)]
        for j, chip in enumerate(chips):
            copy(1 + j, (*chip, c), me).wait_recv()
            passed[j].start()
        copy(0, sibling, me).wait_recv()
        for j, chip in enumerate(chips):
            copy(4 + j, (*chip, 1 - c), me).wait_recv()
        for cp in first + passed:
            cp.wait_send()
        mine.wait()

    return pl.pallas_call(
        body, name=name, out_shape=_sds((N_DEV, m, n), blk.dtype),
        in_specs=[pl.BlockSpec(memory_space=pltpu.VMEM)], out_specs=pl.BlockSpec(memory_space=pltpu.VMEM),
        scratch_shapes=[pltpu.SemaphoreType.DMA((7,)), pltpu.SemaphoreType.DMA((7,)), pltpu.SemaphoreType.DMA],
        compiler_params=pltpu.CompilerParams(vmem_limit_bytes=VMEM_LIMIT),
    )(blk)


def _allgather_hbm(name, shards):
    na = len(shards)

    def body(*refs):
        ins, outs = refs[:na], refs[na:2 * na]
        send_sems, recv_sems, local_sems = refs[2 * na:]
        x, y, c = _place()
        me, sibling = (x, y, c), (x, y, 1 - c)
        chips = [(1 - x, y), (x, 1 - y), (1 - x, 1 - y)]

        def copy(a, k, block, to, from_input=False):
            dst = outs[a].at[_lin(*block)]
            return pltpu.make_async_remote_copy(src_ref=ins[a] if from_input else dst, dst_ref=dst,
                                                send_sem=send_sems.at[a, k], recv_sem=recv_sems.at[a, k],
                                                device_id=to, device_id_type=MESH)

        mine = [pltpu.make_async_copy(ins[a], outs[a].at[_lin(*me)], local_sems.at[a]) for a in range(na)]
        for cp in mine:
            cp.start()
        first = []
        for a in range(na):
            first.append(copy(a, 0, me, sibling, True))
            first += [copy(a, 1 + j, me, (*chip, c), True) for j, chip in enumerate(chips)]
        for cp in first:
            cp.start()
        passed = []
        for j, chip in enumerate(chips):
            for a in range(na):
                copy(a, 1 + j, (*chip, c), me).wait_recv()
                fwd = copy(a, 4 + j, (*chip, c), sibling)
                fwd.start()
                passed.append(fwd)
        for a in range(na):
            copy(a, 0, sibling, me).wait_recv()
            for j, chip in enumerate(chips):
                copy(a, 4 + j, (*chip, 1 - c), me).wait_recv()
        for cp in first + passed:
            cp.wait_send()
        for cp in mine:
            cp.wait()

    return pl.pallas_call(
        body, name=name, out_shape=[_sds((N_DEV, *s.shape), s.dtype) for s in shards],
        in_specs=[ANY] * na, out_specs=[ANY] * na,
        scratch_shapes=[pltpu.SemaphoreType.DMA((na, 7)), pltpu.SemaphoreType.DMA((na, 7)),
                        pltpu.SemaphoreType.DMA((na,))],
    )(*shards)


def _exchange_shards(name, grads, L):
    nw = len(grads)
    na = nw * L
    flat = [g for per_layer in grads for g in per_layer]

    def body(*refs):
        ins, outs = refs[:na], refs[na:na + nw]
        send_sems, recv_sems, local_sems = refs[na + nw:]
        x, y, c = _place()
        me = _lin(x, y, c)
        peers = [(x ^ ((k + 1) >> 2 & 1), y ^ ((k + 1) >> 1 & 1), c ^ ((k + 1) & 1)) for k in range(7)]

        def copy(a, k, src_blk, dst_blk):
            return pltpu.make_async_remote_copy(src_ref=ins[a].at[src_blk], dst_ref=outs[a // L].at[a % L, dst_blk],
                                                send_sem=send_sems.at[a, k], recv_sem=recv_sems.at[a, k],
                                                device_id=peers[k], device_id_type=MESH)

        mine = [pltpu.make_async_copy(ins[a].at[me], outs[a // L].at[a % L, me], local_sems.at[a]) for a in range(na)]
        for cp in mine:
            cp.start()
        sent = [copy(a, k, _lin(*peers[k]), me) for a in range(na) for k in range(7)]
        for cp in sent:
            cp.start()
        for a in range(na):
            for k in range(7):
                copy(a, k, me, _lin(*peers[k])).wait_recv()
        for cp in sent:
            cp.wait_send()
        for cp in mine:
            cp.wait()

    return pl.pallas_call(
        body, name=name, out_shape=[_sds((L, *per_layer[0].shape), per_layer[0].dtype) for per_layer in grads],
        in_specs=[ANY] * na, out_specs=[ANY] * nw,
        scratch_shapes=[pltpu.SemaphoreType.DMA((na, 7)), pltpu.SemaphoreType.DMA((na, 7)),
                        pltpu.SemaphoreType.DMA((na,))],
    )(*flat)


MOD_ROWS = 16
MOD_SHARD = 6 * D // N_DEV
HI = lax.Precision.HIGHEST


def _mod_fwd(name, c9, w_mod, b_shard):
    L = w_mod.shape[0]

    def kern(c_ref, w_ref, b_ref, o_ref):
        o_ref[...] = lax.dot_general(_silu(c_ref[...]), w_ref[...], NN, precision=HI,
                                     preferred_element_type=F32) + b_ref[...]

    return pl.pallas_call(
        kern, name=name, grid=(L,),
        in_specs=[_full_spec(c9.shape), pl.BlockSpec((None, D, MOD_SHARD), lambda l: (l, 0, 0)),
                  pl.BlockSpec((None, 1, MOD_SHARD), lambda l: (l, 0, 0))],
        out_specs=pl.BlockSpec((None, MOD_ROWS, MOD_SHARD), lambda l: (l, 0, 0)),
        out_shape=_sds((L, MOD_ROWS, MOD_SHARD), F32), compiler_params=_params(),
    )(c9, w_mod, b_shard)


def _mod_bwd(name, c9, w_mod, dmod_all, dmod_cols):
    L = w_mod.shape[0]

    def rows9(ref, l):
        own = jnp.concatenate([ref[j, 2 * l + 1:2 * l + 2, :] for j in range(N_DEV)], axis=0)
        ctx = ref[0, 2 * l:2 * l + 1, :]
        for j in range(1, N_DEV):
            ctx = ctx + ref[j, 2 * l:2 * l + 1, :]
        return own, ctx

    def kern(c_ref, w_ref, all_ref, cols_ref, gw_ref, gb_ref, gc_ref):
        l = pl.program_id(0)
        for ll in range(L):
            @pl.when(l == ll)
            def _():
                own, ctx = rows9(all_ref, ll)
                gb_ref[...] = _colsum(own) + ctx
                own_s, ctx_s = rows9(cols_ref, ll)
                r16 = jnp.concatenate([own_s, ctx_s, jnp.zeros((MOD_ROWS - N_DEV - 1, MOD_SHARD), F32)], axis=0)
                gw_ref[...] = lax.dot_general(_silu(c_ref[...]), r16, TN, precision=HI, preferred_element_type=F32)
                part = lax.dot_general(r16, w_ref[...], NT, precision=HI,
                                       preferred_element_type=F32)[N_DEV:N_DEV + 1, :]
                if ll == 0:
                    gc_ref[...] = part
                else:
                    gc_ref[...] += part

    return pl.pallas_call(
        kern, name=name, grid=(L,),
        in_specs=[_full_spec(c9.shape), pl.BlockSpec((None, D, MOD_SHARD), lambda l: (l, 0, 0)),
                  _full_spec(dmod_all.shape), _full_spec(dmod_cols.shape)],
        out_specs=[pl.BlockSpec((None, D, MOD_SHARD), lambda l: (l, 0, 0)),
                   pl.BlockSpec((None, 1, 6 * D), lambda l: (l, 0, 0)), _full_spec((1, D))],
        out_shape=[_sds((L, D, MOD_SHARD), F32), _sds((L, 1, 6 * D), F32), _sds((1, D), F32)],
        compiler_params=_params(),
    )(c9, w_mod, dmod_all, dmod_cols)


_BC1 = 1.0 - ADAM_B1 ** ADAM_STEP
_BC2 = 1.0 - ADAM_B2 ** ADAM_STEP


def _adamw_vals(w, g, m, v):
    m = ADAM_B1 * m + (1.0 - ADAM_B1) * g
    v = ADAM_B2 * v + (1.0 - ADAM_B2) * (g * g)
    delta = -ADAM_LR * ((m / _BC1) / (jnp.sqrt(v / _BC2) + ADAM_EPS) + ADAM_WD * w)
    return delta, m, v


def _adamw(name, w, g, m, v, tile):
    R, C = w.shape
    blk = ((tile, C), lambda i: (i, 0))

    def body(i, ins, ps, outs, acc):
        d, mm, vv = _adamw_vals(ins[0][...], ins[1][...], ins[2][...], ins[3][...])
        outs[0][...] = d
        outs[1][...] = mm
        outs[2][...] = vv

    return _ew(name, body, R // tile, [(a, *blk) for a in (w, g, m, v)], [], [(_sds((R, C), F32), *blk)] * 3)


def _sum_slots(ref):
    g = ref[0].astype(F32)
    for j in range(1, N_DEV):
        g = g + ref[j].astype(F32)
    return g


def _adamw_slots(name, slots, w, m, v, tile):
    L, R, C = w.shape
    n = R // tile
    spec = pl.BlockSpec((None, tile, C), lambda l, i: (l, i, 0))
    pieces = [s if isinstance(s, (list, tuple)) else [s] for s in slots]
    layer_of = [ll for ll, ps in enumerate(pieces) for _ in ps]
    flat = [p for ps in pieces for p in ps]

    def slot_spec(ll, cols):
        return pl.BlockSpec((N_DEV, tile, cols),
                            lambda l, i: (0, jnp.where(l == ll, i, jnp.where(l < ll, 0, n - 1)), 0))

    def kern(*refs):
        s_refs = refs[:len(flat)]
        w_ref, m_ref, v_ref, g_ref, d_ref, mo_ref, vo_ref = refs[len(flat):]
        l = pl.program_id(0)
        for ll in range(L):
            @pl.when(l == ll)
            def _():
                parts = [_sum_slots(r) for r, lr in zip(s_refs, layer_of) if lr == ll]
                g = parts[0] if len(parts) == 1 else jnp.concatenate(parts, axis=1)
                g_ref[...] = g
                d_ref[...], mo_ref[...], vo_ref[...] = _adamw_vals(w_ref[...], g, m_ref[...], v_ref[...])

    return pl.pallas_call(
        kern, name=name, grid=(L, n),
        in_specs=[slot_spec(ll, p.shape[-1]) for ll, p in zip(layer_of, flat)] + [spec, spec, spec],
        out_specs=[spec] * 4, out_shape=[_sds((L, R, C), F32)] * 4,
        compiler_params=_params(("arbitrary", "arbitrary")),
    )(*flat, w, m, v)


def _sum_blocks(name, blocks):
    _, R, C = blocks.shape

    def kern(b_ref, o_ref):
        o_ref[...] = _sum_slots(b_ref)

    return pl.pallas_call(kern, name=name, in_specs=[_full_spec(blocks.shape)], out_specs=_full_spec((R, C)),
                          grid=(1,), out_shape=_sds((R, C), F32), compiler_params=_params())(blocks)


BIG = ("win_t", "wo_rnn", "wo_attn", "wout", "wffn_in_t", "wffn_out")
BIG_SRC = ("w_in", "w_o_rnn", "w_o_attn", "w_out", "w_ffn_in", "w_ffn_out")
BIG_T = (True, False, False, False, True, False)
BIG_TILE = (176, 128, 128, 128, 176, 176)


def _chan_full(g8):
    return jnp.transpose(g8, (1, 0, 2)).reshape(g8.shape[1], D)


def kernel(x, c, ctx, c_ctx, w_mod, b_mod, g_mix_pre, g_mix_post, g_ffn_pre, g_ffn_post, w_in, conv_w, conv_b, lru_wa, lru_ba, lru_wx, lru_bx, lru_lam, attn_sink, w_o_rnn, w_o_attn, w_out, w_ffn_in, w_ffn_out, loss_target, m_c_ctx, m_w_mod, m_b_mod, m_g_mix_pre, m_g_mix_post, m_g_ffn_pre, m_g_ffn_post, m_w_in, m_conv_w, m_conv_b, m_lru_wa, m_lru_ba, m_lru_wx, m_lru_bx, m_lru_lam, m_attn_sink, m_w_o_rnn, m_w_o_attn, m_w_out, m_w_ffn_in, m_w_ffn_out, v_c_ctx, v_w_mod, v_b_mod, v_g_mix_pre, v_g_mix_post, v_g_ffn_pre, v_g_ffn_post, v_w_in, v_conv_w, v_conv_b, v_lru_wa, v_lru_ba, v_lru_wx, v_lru_bx, v_lru_lam, v_attn_sink, v_w_o_rnn, v_w_o_attn, v_w_out, v_w_ffn_in, v_w_ffn_out):
    P = dict(c_ctx=c_ctx, w_mod=w_mod, b_mod=b_mod, g_mix_pre=g_mix_pre, g_mix_post=g_mix_post, g_ffn_pre=g_ffn_pre,
             g_ffn_post=g_ffn_post, w_in=w_in, conv_w=conv_w, conv_b=conv_b, lru_wa=lru_wa, lru_ba=lru_ba,
             lru_wx=lru_wx, lru_bx=lru_bx, lru_lam=lru_lam, attn_sink=attn_sink, w_o_rnn=w_o_rnn, w_o_attn=w_o_attn,
             w_out=w_out, w_ffn_in=w_ffn_in, w_ffn_out=w_ffn_out)
    Mo = dict(c_ctx=m_c_ctx, w_mod=m_w_mod, b_mod=m_b_mod, g_mix_pre=m_g_mix_pre, g_mix_post=m_g_mix_post,
              g_ffn_pre=m_g_ffn_pre, g_ffn_post=m_g_ffn_post, w_in=m_w_in, conv_w=m_conv_w, conv_b=m_conv_b,
              lru_wa=m_lru_wa, lru_ba=m_lru_ba, lru_wx=m_lru_wx, lru_bx=m_lru_bx, lru_lam=m_lru_lam,
              attn_sink=m_attn_sink, w_o_rnn=m_w_o_rnn, w_o_attn=m_w_o_attn, w_out=m_w_out, w_ffn_in=m_w_ffn_in,
              w_ffn_out=m_w_ffn_out)
    Vo = dict(c_ctx=v_c_ctx, w_mod=v_w_mod, b_mod=v_b_mod, g_mix_pre=v_g_mix_pre, g_mix_post=v_g_mix_post,
              g_ffn_pre=v_g_ffn_pre, g_ffn_post=v_g_ffn_post, w_in=v_w_in, conv_w=v_conv_w, conv_b=v_conv_b,
              lru_wa=v_lru_wa, lru_ba=v_lru_ba, lru_wx=v_lru_wx, lru_bx=v_lru_bx, lru_lam=v_lru_lam,
              attn_sink=v_attn_sink, w_o_rnn=v_w_o_rnn, w_o_attn=v_w_o_attn, w_out=v_w_out, w_ffn_in=v_w_ffn_in,
              w_ffn_out=v_w_ffn_out)
    L = w_in.shape[0]
    S = x.shape[1]
    me = _lin(*_place())

    small = jnp.concatenate([c.reshape(8, 128), conv_w.reshape(L * CONV_W, 128), lru_ba.reshape(2 * L, 128),
                             lru_bx.reshape(2 * L, 128), lru_lam.reshape(2 * L, 128), jnp.zeros((4, 128), F32)], axis=0)
    small_all = _allgather_small("ag_small", small)
    c_all = small_all[:, 0:8].reshape(N_DEV, D)
    conv_w_f = _chan_full(small_all[:, 8:16]).reshape(L, CONV_W, D)
    lru_ba_f = _chan_full(small_all[:, 16:20]).reshape(L, 2, D)
    lru_bx_f = _chan_full(small_all[:, 20:24]).reshape(L, 2, D)
    lru_lam_f = _chan_full(small_all[:, 24:28]).reshape(L, 2, D)

    c9 = jnp.concatenate([c_all, c_ctx[None], jnp.zeros((MOD_ROWS - N_DEV - 1, D), F32)], axis=0)
    b_shard = lax.dynamic_slice_in_dim(b_mod, me * MOD_SHARD, MOD_SHARD, axis=1)[:, None, :]
    mod_part = _mod_fwd("mod_fwd", c9, w_mod, b_shard)
    mod_all = _allgather_small("ag_mod", mod_part.reshape(L * MOD_ROWS, MOD_SHARD))
    mod_all = jnp.transpose(mod_all.reshape(N_DEV, L, MOD_ROWS, MOD_SHARD), (1, 2, 0, 3)).reshape(L, MOD_ROWS, 6 * D)
    own_row = lax.dynamic_index_in_dim(mod_all, me, axis=1, keepdims=False)
    modrows = jnp.stack([mod_all[:, N_DEV], own_row], axis=1)

    shards = [{k: (P[src][l].T if tr else P[src][l]).astype(BF16) for k, src, tr in zip(BIG, BIG_SRC, BIG_T)}
              for l in range(L)]
    win0, = _allgather_hbm("ag_w_in0", [shards[0]["win_t"]])
    Ws = []
    for l in range(L):
        W = {"win_t": win0.reshape(-1, D)} if l == 0 else {}
        W.update(
            cw=conv_w_f[l], cb=conv_b[l][None],
            w4=jnp.concatenate([lru_wa[l, 0], lru_wa[l, 1], lru_wx[l, 0], lru_wx[l, 1]], axis=-1).astype(BF16),
            b4=jnp.concatenate([lru_ba_f[l, 0].reshape(N_RNN_BLOCKS, 1, RB), lru_ba_f[l, 1].reshape(N_RNN_BLOCKS, 1, RB),
                                lru_bx_f[l, 0].reshape(N_RNN_BLOCKS, 1, RB), lru_bx_f[l, 1].reshape(N_RNN_BLOCKS, 1, RB)],
                               axis=-1),
            lam=lru_lam_f[l], sink4=jnp.broadcast_to(attn_sink[l].reshape(N_KV, Q_PER_KV, 1), (N_KV, Q_PER_KV, HEAD)),
            g_mix_pre=g_mix_pre[l][None], g_mix_post=g_mix_post[l][None], g_ffn_pre=g_ffn_pre[l][None],
            g_ffn_post=g_ffn_post[l][None], mod=modrows[l])
        Ws.append(W)

    xa = jnp.concatenate([ctx[0], x[0]], axis=0)
    plan = _Plan(shards, Ws)
    sq, dxa, Gs = _local_step(xa, loss_target[0], Ws, S, plan)
    loss = lax.psum((0.5 / D) * jnp.sum(sq), ("x", "y", "c"))
    grad_x = dxa[CTX:][None]

    dmod = jnp.concatenate([Gs[l]["mod"] for l in range(L)] + [jnp.zeros((8 - 2 * L, 6 * D), F32)], axis=0)
    dmod_all = _allgather_small("ag_dmod", dmod)
    dmod_cols = lax.dynamic_slice_in_dim(dmod_all, me * MOD_SHARD, MOD_SHARD, axis=2)
    g_w_mod, g_b_mod, dsc_part = _mod_bwd("mod_bwd", c9, w_mod, dmod_all, dmod_cols)
    g_b_mod = g_b_mod[:, 0]

    def rows(name, shape):
        return jnp.concatenate([Gs[l][name].reshape(shape) for l in range(L)], axis=0)

    b4g = [Gs[l]["b4"].reshape(N_RNN_BLOCKS, 4, RB) for l in range(L)]
    sink_row = jnp.concatenate([Gs[l]["sink4"][:, :, 0].reshape(1, N_Q) for l in range(L)]
                               + [jnp.zeros((1, D - L * N_Q), F32)], axis=1)
    small_g = jnp.concatenate(
        [rows("g_mix_pre", (1, D)), rows("g_mix_post", (1, D)), rows("g_ffn_pre", (1, D)), rows("g_ffn_post", (1, D)),
         rows("cb", (1, D)), rows("cw", (CONV_W, D))]
        + [b4g[l][:, d].reshape(1, D) for l in range(L) for d in range(2)]
        + [b4g[l][:, 2 + d].reshape(1, D) for l in range(L) for d in range(2)]
        + [rows("lam", (2, D)), sink_row, dsc_part], axis=0)
    n_small = small_g.shape[0]
    small_tot = _sum_blocks("sum_small", _allgather_small("ag_small_grads", small_g))
    o = 0
    G = {}
    for name in ("g_mix_pre", "g_mix_post", "g_ffn_pre", "g_ffn_post", "conv_b"):
        G[name] = small_tot[o:o + L]
        o += L
    G["conv_w"] = small_tot[o:o + L * CONV_W].reshape(L, CONV_W, D)
    o += L * CONV_W
    for name in ("lru_ba", "lru_bx", "lru_lam"):
        G[name] = small_tot[o:o + 2 * L].reshape(L, 2, D)
        o += 2 * L
    G["attn_sink"] = small_tot[o, :L * N_Q].reshape(L, N_Q)
    sg = jax.nn.sigmoid(c_ctx)
    G["c_ctx"] = small_tot[o + 1] * (sg * (1.0 + c_ctx * (1.0 - sg)))
    G["b_mod"] = g_b_mod
    G["w_mod"] = g_w_mod

    last_slots, = _exchange_shards("exchange_w_in0", [[Gs[0]["win_t_b"].reshape(N_DEV, -1, D // 2)]], 1)
    plan.slots[0]["win_t"] = [plan.slots[0]["win_t_a"], last_slots[0]]

    out_g, out_d, out_m, out_v = {}, {}, {}, {}

    def put(name, res, shape=None):
        g, d, m, v = res
        for dst, val in ((out_g, g), (out_d, d), (out_m, m), (out_v, v)):
            dst[name] = val if shape is None else val.reshape(shape)

    for k, src, tr, tile in zip(BIG, BIG_SRC, BIG_T, BIG_TILE):
        lay = (lambda a: jnp.swapaxes(a, 1, 2)) if tr else (lambda a: a)
        res = _adamw_slots("adamw_" + src, [plan.slots[l][k] for l in range(L)], lay(P[src]), lay(Mo[src]),
                           lay(Vo[src]), tile)
        put(src, [lay(r) for r in res])
    res = _adamw("adamw_w_mod", w_mod.reshape(L * D, MOD_SHARD), g_w_mod.reshape(L * D, MOD_SHARD),
                 m_w_mod.reshape(L * D, MOD_SHARD), v_w_mod.reshape(L * D, MOD_SHARD), 256)
    put("w_mod", (g_w_mod,) + tuple(res), w_mod.shape)
    def fuse4(wa, wx):
        return jnp.concatenate([wa[:, 0], wa[:, 1], wx[:, 0], wx[:, 1]], axis=-1).reshape(L, N_RNN_BLOCKS * RB, 4 * RB)

    res = _adamw_slots("adamw_gates", plan.gate_slots,
                       fuse4(lru_wa, lru_wx), fuse4(m_lru_wa, m_lru_wx), fuse4(v_lru_wa, v_lru_wx), 256)
    res = [r.reshape(L, N_RNN_BLOCKS, RB, 4, RB) for r in res]
    put("lru_wa", [jnp.stack([r[:, :, :, 0], r[:, :, :, 1]], axis=1) for r in res])
    put("lru_wx", [jnp.stack([r[:, :, :, 2], r[:, :, :, 3]], axis=1) for r in res])
    rep = ("g_mix_pre", "g_mix_post", "g_ffn_pre", "g_ffn_post", "conv_b", "b_mod")

    def pack_rep(T_):
        sink = jnp.concatenate([T_["attn_sink"].reshape(1, L * N_Q), jnp.zeros((1, D - L * N_Q), F32)], axis=1)
        return jnp.concatenate([T_[n].reshape(-1, D) for n in rep] + [sink, T_["c_ctx"][None]], axis=0)

    pk = [pack_rep(T_) for T_ in (P, G, Mo, Vo)]
    n_rep = pk[0].shape[0]
    res = _adamw("adamw_replicated", *[jnp.pad(a, ((0, 24 - n_rep), (0, 0))) for a in pk], 24)
    res = (pk[1],) + tuple(r[:n_rep] for r in res)
    o = 0
    for n in rep:
        k = P[n].size // D
        put(n, [r[o:o + k] for r in res], P[n].shape)
        o += k
    put("attn_sink", [r[o, :L * N_Q] for r in res], attn_sink.shape)
    put("c_ctx", [r[o + 1] for r in res], c_ctx.shape)
    chan = ("conv_w", "lru_ba", "lru_bx", "lru_lam")
    g_own = {n: lax.dynamic_slice_in_dim(G[n], me * RB, RB, axis=2) for n in chan}

    def pack_chan(T_):
        return jnp.concatenate([T_[n].reshape(-1, RB) for n in chan], axis=0)

    pk = [pack_chan(T_) for T_ in (P, g_own, Mo, Vo)]
    n_ch = pk[0].shape[0]
    res = _adamw("adamw_channels", *[jnp.pad(a, ((0, 24 - n_ch), (0, 0))) for a in pk], 24)
    res = (pk[1],) + tuple(r[:n_ch] for r in res)
    o = 0
    for n in chan:
        k = P[n].size // RB
        put(n, [r[o:o + k] for r in res], P[n].shape)
        o += k

    order = ("c_ctx", "w_mod", "b_mod", "g_mix_pre", "g_mix_post", "g_ffn_pre", "g_ffn_post", "w_in", "conv_w", "conv_b",
             "lru_wa", "lru_ba", "lru_wx", "lru_bx", "lru_lam", "attn_sink", "w_o_rnn", "w_o_attn", "w_out", "w_ffn_in",
             "w_ffn_out")
    return (loss, grad_x, *[out_g[n] for n in order], *[out_d[n] for n in order], *[out_m[n] for n in order],
            *[out_v[n] for n in order])
```

```python
import functools
import math

import numpy as np
import jax
import jax.numpy as jnp
from jax import lax
from jax.experimental import pallas as pl
from jax.experimental.pallas import tpu as pltpu

F32 = jnp.float32
BF16 = jnp.bfloat16

D = 1024
CTX = 256
TR = 256
HEAD = 128
N_Q = 8
N_KV = 2
Q_PER_KV = N_Q // N_KV
GRID_W = 64
N_FREQ = HEAD // 4
ROPE_BASE = 10000.0
N_RNN_BLOCKS = 8
CONV_W = 4
CONV_LEFT = 2
LRU_C = 8.0
D_FF = 2816
IN_W = 5632
P_W = IN_W
DP_W = 3584
COL_XR, COL_GR, COL_Q, COL_K, COL_V, COL_GL = 0, 1024, 2048, 3072, 3328, 3584
GLB = 512
EPS = 1e-6
NEG_INF = -1e30
ATT_SCALE = HEAD ** -0.5
N_DEV = 8
VMEM_LIMIT = 56 * 1024 * 1024

ADAM_LR, ADAM_B1, ADAM_B2, ADAM_EPS, ADAM_WD, ADAM_STEP = 0.001, 0.9, 0.999, 1e-08, 0.01, 10

NN = (((1,), (0,)), ((), ()))
NT = (((1,), (1,)), ((), ()))
TN = (((0,), (0,)), ((), ()))


def _dot(a, b, dims=NN):
    return lax.dot_general(a, b, dims, preferred_element_type=F32)


def _params(sem=("arbitrary",)):
    return pltpu.CompilerParams(dimension_semantics=sem, vmem_limit_bytes=VMEM_LIMIT)


def _full_spec(shape):
    nd = len(shape)
    return pl.BlockSpec(shape, lambda *_: (0,) * nd)


ANY = pl.BlockSpec(memory_space=pl.ANY)


def _ew(name, body, n, row_ins, pars, row_outs, accs=(), alias=None):
    n_ri, n_p, n_ro, n_acc = len(row_ins), len(pars), len(row_outs), len(accs)

    def kern(*refs):
        i = pl.program_id(0)
        ins = refs[:n_ri]
        ps = refs[n_ri:n_ri + n_p]
        outs = refs[n_ri + n_p:n_ri + n_p + n_ro]
        acc = refs[n_ri + n_p + n_ro:]
        if n_acc:
            @pl.when(i == 0)
            def _():
                for a in acc:
                    a[...] = jnp.zeros(a.shape, a.dtype)
        body(i, ins, ps, outs, acc)

    in_specs = [ANY if blk is None else pl.BlockSpec(blk, imap) for (_, blk, imap) in row_ins]
    in_specs += [_full_spec(p.shape) for p in pars]
    out_specs = [pl.BlockSpec(blk, imap) for (_, blk, imap) in row_outs] + [_full_spec(a.shape) for a in accs]
    out_shape = [s for (s, _, _) in row_outs] + list(accs)
    return pl.pallas_call(
        kern, name=name, grid=(n,), in_specs=in_specs, out_specs=out_specs, out_shape=out_shape,
        input_output_aliases=alias or {}, compiler_params=_params(),
    )(*[a for (a, _, _) in row_ins], *pars)


def _rowblk(width, colblk=0, roff=0, tile=TR):
    return (tile, width), (lambda i: (i + roff, colblk))


def _sds(shape, dtype):
    return jax.ShapeDtypeStruct(shape, dtype)


class _Carry:
    SAME_CORE = (1, 3, 5)

    def __init__(self, jobs):
        self.jobs = list(jobs)
        self.arrays = [a for _, a in self.jobs]
        self.out_shapes = [_sds(a.shape if kind == "scatter" else (N_DEV, *a.shape), a.dtype) for kind, a in self.jobs]
        n = len(self.jobs)
        self.scratch = [pltpu.SemaphoreType.DMA((n, 7)), pltpu.SemaphoreType.DMA((n, 7)), pltpu.SemaphoreType.DMA((n,))]

    def _setup(self, sems):
        send_sems, recv_sems, local_sems = sems
        x, y, c = _place()
        me = _lin(x, y, c)
        peers = [(x ^ ((k + 1) >> 2 & 1), y ^ ((k + 1) >> 1 & 1), c ^ ((k + 1) & 1)) for k in range(7)]

        def copy(a, k, sem_k, src, dst):
            return pltpu.make_async_remote_copy(src_ref=src, dst_ref=dst, send_sem=send_sems.at[a, sem_k],
                                                recv_sem=recv_sems.at[a, sem_k], device_id=peers[k], device_id_type=MESH)

        return me, [_lin(*p) for p in peers], copy, local_sems

    def _local(self, a, kind, ins, outs, me, local_sems):
        return pltpu.make_async_copy(ins[a].at[me] if kind == "scatter" else ins[a], outs[a].at[me], local_sems.at[a])

    def start(self, ins, outs, sems):
        me, theirs, copy, local_sems = self._setup(sems)
        for a, (kind, _) in enumerate(self.jobs):
            self._local(a, kind, ins, outs, me, local_sems).start()
            if kind == "scatter":
                for k in range(7):
                    copy(a, k, k, ins[a].at[theirs[k]], outs[a].at[me]).start()
            else:
                for k in (0,) + self.SAME_CORE:
                    copy(a, k, k, ins[a], outs[a].at[me]).start()

    def wait(self, ins, outs, sems):
        me, theirs, copy, local_sems = self._setup(sems)
        for a, (kind, _) in enumerate(self.jobs):
            if kind == "scatter":
                for k in range(7):
                    copy(a, k, k, ins[a].at[me], outs[a].at[theirs[k]]).wait_recv()
                for k in range(7):
                    copy(a, k, k, ins[a].at[theirs[k]], outs[a].at[me]).wait_send()
            else:
                for k in self.SAME_CORE:
                    blk = outs[a].at[theirs[k]]
                    copy(a, k, k, ins[a], blk).wait_recv()
                    copy(a, 0, k + 1, blk, blk).start()
                copy(a, 0, 0, ins[a], outs[a].at[theirs[0]]).wait_recv()
                for k in self.SAME_CORE:
                    copy(a, 0, k + 1, ins[a], outs[a].at[theirs[k + 1]]).wait_recv()
                for k in (0,) + self.SAME_CORE:
                    copy(a, k, k, ins[a], outs[a].at[me]).wait_send()
                for k in self.SAME_CORE:
                    blk = outs[a].at[theirs[k]]
                    copy(a, 0, k + 1, blk, blk).wait_send()
            self._local(a, kind, ins, outs, me, local_sems).wait()


def _carried(kern, carry, n_in, n_out, first, last):
    if carry is None:
        return kern
    nc = len(carry.jobs)

    def wrapped(*refs):
        ins, cin = refs[:n_in], refs[n_in:n_in + nc]
        outs, cout = refs[n_in + nc:n_in + nc + n_out], refs[n_in + nc + n_out:n_in + 2 * nc + n_out]
        scr, sems = refs[n_in + 2 * nc + n_out:-3], refs[-3:]

        @pl.when(first())
        def _():
            carry.start(cin, cout, sems)

        kern(*ins, *outs, *scr)

        @pl.when(last())
        def _():
            carry.wait(cin, cout, sems)

    return wrapped


def _carry_args(carry):
    if carry is None:
        return [], [], [], [], []
    n = len(carry.jobs)
    return [ANY] * n, carry.arrays, [ANY] * n, carry.out_shapes, carry.scratch


def _grid_ends(dims):
    first = lambda: functools.reduce(jnp.logical_and, [pl.program_id(d) == 0 for d in range(len(dims))])
    last = lambda: functools.reduce(jnp.logical_and, [pl.program_id(d) == n - 1 for d, n in enumerate(dims)])
    return first, last


def _mm_call(name, a, b, mode, out_dtype, tm, tn, rows_outer=True, single_b=False, carry=None):
    if mode == "TN":
        (K, M), N = a.shape, b.shape[1]
    else:
        (M, K), N = a.shape, (b.shape[1] if mode == "NN" else b.shape[0])
    assert M % tm == 0 and N % tn == 0, (name, M, N, K, tm, tn)
    ij = (lambda g0, g1: (g0, g1)) if rows_outer else (lambda g0, g1: (g1, g0))
    grid = (M // tm, N // tn) if rows_outer else (N // tn, M // tm)
    if mode == "TN":
        a_spec = pl.BlockSpec((K, tm), lambda g0, g1: (0, ij(g0, g1)[0]))
    else:
        a_spec = pl.BlockSpec((tm, K), lambda g0, g1: (ij(g0, g1)[0], 0))
    b_blk, b_map = ((tn, K), lambda g0, g1: (ij(g0, g1)[1], 0)) if mode == "NT" else \
                   ((K, tn), lambda g0, g1: (0, ij(g0, g1)[1]))
    b_spec = pl.BlockSpec(b_blk, b_map, pipeline_mode=pl.Buffered(1)) if single_b else pl.BlockSpec(b_blk, b_map)
    dims = {"NN": NN, "NT": NT, "TN": TN}[mode]

    def kern(a_ref, b_ref, o_ref):
        o_ref[...] = _dot(a_ref[...], b_ref[...], dims).astype(o_ref.dtype)

    ci, ca, co, cs, cscr = _carry_args(carry)
    res = pl.pallas_call(
        _carried(kern, carry, 2, 1, *_grid_ends(grid)), name=name, grid=grid, in_specs=[a_spec, b_spec] + ci,
        out_specs=[pl.BlockSpec((tm, tn), lambda g0, g1: ij(g0, g1))] + co,
        out_shape=[_sds((M, N), out_dtype)] + cs, scratch_shapes=cscr,
        compiler_params=_params(("arbitrary", "arbitrary")),
    )(a, b, *ca)
    return res[0] if carry is None else (res[0], res[1:])


def _mm_act(name, a, w, mode, out_dtype=BF16, carry=None):
    rows, K = a.shape
    N = w.shape[1] if mode == "NN" else w.shape[0]
    if K > D_FF:
        return _mm_call(name, a, w, mode, out_dtype, rows // 8, N, single_b=True, carry=carry)
    tn = N if N <= 1024 else 1408
    return _mm_call(name, a, w, mode, out_dtype, rows // 4, tn, carry=carry)


def _mm_wgrad(name, x, dy, out_dtype=BF16, carry=None):
    M = x.shape[1]
    tm = 1408 if M == D_FF else 512
    return _mm_call(name, x, dy, "TN", out_dtype, tm, dy.shape[1], single_b=True, carry=carry)


def _sigmoid(x):
    return 0.5 * jnp.tanh(0.5 * x) + 0.5


def _silu(x):
    return x * _sigmoid(x)


def _silu_grad(x):
    s = _sigmoid(x)
    return s * (1.0 + x * (1.0 - s))


_GELU_K = math.sqrt(2.0 / math.pi)


def _gelu(x):
    return 0.5 * x * (1.0 + jnp.tanh(_GELU_K * (x + 0.044715 * x * x * x)))


def _gelu_grad(x):
    t = jnp.tanh(_GELU_K * (x + 0.044715 * x * x * x))
    return 0.5 * (1.0 + t) + 0.5 * x * (1.0 - t * t) * _GELU_K * (1.0 + 3.0 * 0.044715 * x * x)


def _log_sigmoid(x):
    return jnp.minimum(x, 0.0) - jnp.log(1.0 + jnp.exp(-jnp.abs(x)))


def _rms(x):
    x = x.astype(F32)
    r = lax.rsqrt(jnp.mean(x * x, axis=-1, keepdims=True) + EPS)
    return x * r, r


def _rms_bwd(dy, y, r):
    return r * (dy - y * jnp.mean(dy * y, axis=-1, keepdims=True))


def _modrow(mod_ref, i, chunk):
    lo = mod_ref[0:1, chunk * D:(chunk + 1) * D]
    hi = mod_ref[1:2, chunk * D:(chunk + 1) * D]
    return jnp.where(i == 0, lo, hi)


def _acc_seg(acc_ref, i, val):
    zero = jnp.zeros_like(val)
    acc_ref[0:1, :] += jnp.where(i == 0, val, zero)
    acc_ref[1:2, :] += jnp.where(i == 0, zero, val)


def _colsum(x):
    return jnp.sum(x, axis=0, keepdims=True)


SH1, SC1, GA1, SH2, SC2, GA2 = range(6)


def _normmod_fwd(name, xa, g, mod, c_sh, c_sc):
    T = xa.shape[0]

    def body(i, ins, ps, outs, acc):
        y, _ = _rms(ins[0][...])
        h = (y * ps[0][...]) * (1.0 + _modrow(ps[1], i, c_sc)) + _modrow(ps[1], i, c_sh)
        outs[0][...] = h.astype(BF16)

    return _ew(name, body, T // TR, [(xa, *_rowblk(D))], [g, mod], [(_sds((T, D), BF16), *_rowblk(D))])[0]


def _modrows(mod_ref, row0, n, chunk):
    t = row0 + lax.broadcasted_iota(jnp.int32, (n, 1), 0)
    return jnp.where(t < CTX, mod_ref[0:1, chunk * D:(chunk + 1) * D], mod_ref[1:2, chunk * D:(chunk + 1) * D])


def _loss_resid_bwd(name, x_out, target, mat, gpost, mod, c_ga):
    T = x_out.shape[0]

    def body(i, ins, ps, outs, acc):
        err = ins[0][...] - ins[1][...]
        lat = i > 0
        dx = jnp.where(lat, err * (1.0 / D), 0.0)
        outs[0][...] = dx
        acc[2][...] += jnp.where(lat, _colsum(err * err), 0.0)
        outs[1][...] = _resid_bwd_vals(i, dx, ins[2][...], ps[0][...], ps[1], c_ga, acc[0], acc[1]).astype(BF16)

    tgt_blk = ((TR, D), lambda i: (jnp.maximum(i - 1, 0), 0))
    return _ew(name, body, T // TR, [(x_out, *_rowblk(D)), (target, *tgt_blk), (mat, *_rowblk(D))], [gpost, mod],
               [(_sds((T, D), F32), *_rowblk(D)), (_sds((T, D), BF16), *_rowblk(D))],
               [_sds((2, D), F32), _sds((1, D), F32), _sds((1, D), F32)])


def _mod_for(mod_ref, i, chunk, row0, n):
    return _modrow(mod_ref, i, chunk) if row0 is None else _modrows(mod_ref, row0, n, chunk)


def _acc_for(acc_ref, i, v, row0):
    if row0 is None:
        _acc_seg(acc_ref, i, _colsum(v))
        return

    @pl.when(row0 < CTX)
    def _():
        is_ctx = row0 + lax.broadcasted_iota(jnp.int32, (v.shape[0], 1), 0) < CTX
        acc_ref[0:1, :] += _colsum(jnp.where(is_ctx, v, 0.0))
        acc_ref[1:2, :] += _colsum(jnp.where(is_ctx, 0.0, v))

    @pl.when(row0 >= CTX)
    def _():
        acc_ref[1:2, :] += _colsum(v)


def _resid_bwd_vals(i, dout, mat, gpost, mod_ref, c_ga, acc_ga, acc_g, row0=None):
    ym, rm = _rms(mat)
    ga = _mod_for(mod_ref, i, c_ga, row0, dout.shape[0])
    _acc_for(acc_ga, i, dout * (ym * gpost), row0)
    dn = dout * ga
    acc_g[...] += _colsum(dn * ym)
    return _rms_bwd(dn * gpost, ym, rm)


def _normmod_bwd_vals(i, dh, xin, g, mod_ref, c_sh, c_sc, acc_sh, acc_sc, acc_g, row0=None):
    dh = dh.astype(F32)
    y, r = _rms(xin)
    _acc_for(acc_sc, i, dh * (y * g), row0)
    _acc_for(acc_sh, i, dh, row0)
    dyg = dh * (1.0 + _mod_for(mod_ref, i, c_sc, row0, dh.shape[0]))
    acc_g[...] += _colsum(dyg * y)
    return _rms_bwd(dyg * g, y, r)


def _parts(i, tm):
    h = tm // 2
    return [(slice(r, r + h), i * tm + r) for r in (0, h)]


FT = 1408


def _ffn_in_fused(name, h2, w_t, carry=None):
    T = h2.shape[0]
    tm, nj = T // 4, D_FF // FT

    def kern(a_ref, bg_ref, bu_ref, fg_ref, fu_ref, s_ref):
        for rows, _ in _parts(0, tm):
            a = a_ref[rows, :]
            g = _dot(a, bg_ref[...], NT)
            u = _dot(a, bu_ref[...], NT)
            fg_ref[rows, :] = g.astype(BF16)
            fu_ref[rows, :] = u.astype(BF16)
            s_ref[rows, :] = (_silu(g) * u).astype(BF16)

    o_spec = pl.BlockSpec((tm, FT), lambda i, j: (i, j))
    ci, ca, co, cs, cscr = _carry_args(carry)
    res = pl.pallas_call(
        _carried(kern, carry, 3, 3, *_grid_ends((4, nj))), name=name, grid=(4, nj),
        in_specs=[pl.BlockSpec((tm, D), lambda i, j: (i, 0)), pl.BlockSpec((FT, D), lambda i, j: (j, 0)),
                  pl.BlockSpec((FT, D), lambda i, j: (j + nj, 0))] + ci,
        out_specs=[o_spec] * 3 + co, out_shape=[_sds((T, D_FF), BF16)] * 3 + cs, scratch_shapes=cscr,
        compiler_params=_params(("arbitrary", "arbitrary")),
    )(h2, w_t, w_t, *ca)
    return res if carry is None else (res[:3], res[3:])


def _norm_chain(row0, xin, mat, gpost, mod_ref, c_ga, gnext, modn_ref, c_sh, c_sc):
    n = xin.shape[0]
    ym, _ = _rms(mat.astype(BF16))
    xo = xin + _modrows(mod_ref, row0, n, c_ga) * (ym * gpost)
    y, _ = _rms(xo)
    h = (y * gnext) * (1.0 + _modrows(modn_ref, row0, n, c_sc)) + _modrows(modn_ref, row0, n, c_sh)
    return xo, h.astype(BF16)


def _out_fused(name, p, ya, yb, xa, w_out, gpost, mod, gnext):
    T = ya.shape[0]
    tm = T // 8

    def kern(g0, g1, g2, g3, ya_ref, yb_ref, xa_ref, w_ref, gpost_ref, mod_ref, gnext_ref, z_ref, m_ref, x1_ref, h2_ref):
        for rows, row0 in _parts(pl.program_id(0), tm):
            ga = _sigmoid(jnp.concatenate([g0[rows, :], g1[rows, :]], axis=1).astype(F32))
            gb = _sigmoid(jnp.concatenate([g2[rows, :], g3[rows, :]], axis=1).astype(F32))
            z = (ga * ya_ref[rows, :].astype(F32) + gb * yb_ref[rows, :].astype(F32)).astype(BF16)
            z_ref[rows, :] = z
            m = _dot(z, w_ref[...])
            m_ref[rows, :] = m.astype(BF16)
            x1_ref[rows, :], h2_ref[rows, :] = _norm_chain(row0, xa_ref[rows, :], m, gpost_ref[...], mod_ref, GA1,
                                                           gnext_ref[...], mod_ref, SH2, SC2)

    row = lambda w: pl.BlockSpec((tm, w), lambda i: (i, 0))
    return pl.pallas_call(
        kern, name=name, grid=(T // tm,),
        in_specs=[pl.BlockSpec((tm, GLB), lambda i, q=q: (i, COL_GL // GLB + q)) for q in range(4)]
                 + [row(D), row(D), row(D), _full_spec(w_out.shape), _full_spec(gpost.shape), _full_spec(mod.shape),
                    _full_spec(gnext.shape)],
        out_specs=[row(D)] * 4,
        out_shape=[_sds((T, D), BF16), _sds((T, D), BF16), _sds((T, D), F32), _sds((T, D), BF16)],
        compiler_params=_params(),
    )(p, p, p, p, ya, yb, xa, w_out, gpost, mod, gnext)


def _ffn_out_fused(name, s, w, x1, gpost, mod, nxt=None):
    T = s.shape[0]
    tm = T // 8

    def kern(s_ref, w_ref, x1_ref, gpost_ref, mod_ref, *rest):
        for rows, row0 in _parts(pl.program_id(0), tm):
            e = _dot(s_ref[rows, :], w_ref[...])
            if nxt is None:
                e_ref, xo_ref = rest
                ym, _ = _rms(e.astype(BF16))
                xo_ref[rows, :] = x1_ref[rows, :] + _modrows(mod_ref, row0, tm // 2, GA2) * (ym * gpost_ref[...])
            else:
                gnext_ref, modn_ref, e_ref, xo_ref, h_ref = rest
                xo_ref[rows, :], h_ref[rows, :] = _norm_chain(row0, x1_ref[rows, :], e, gpost_ref[...], mod_ref, GA2,
                                                              gnext_ref[...], modn_ref, SH1, SC1)
            e_ref[rows, :] = e.astype(BF16)

    row = lambda w_: pl.BlockSpec((tm, w_), lambda i: (i, 0))
    extra = [] if nxt is None else list(nxt)
    return pl.pallas_call(
        kern, name=name, grid=(T // tm,),
        in_specs=[row(D_FF), _full_spec(w.shape), row(D), _full_spec(gpost.shape), _full_spec(mod.shape)]
                 + [_full_spec(a.shape) for a in extra],
        out_specs=[row(D)] * (2 if nxt is None else 3),
        out_shape=[_sds((T, D), BF16), _sds((T, D), F32)] + ([] if nxt is None else [_sds((T, D), BF16)]),
        compiler_params=_params(),
    )(s, w, x1, gpost, mod, *extra)


def _ffn_bwd_fused(name, fg, fu, w, de=None, head=None):
    T = fg.shape[0]
    tm = T // 8
    row = lambda w_: pl.BlockSpec((tm, w_), lambda i: (i, 0))
    w_spec = pl.BlockSpec(w.shape, lambda i: (0, 0), pipeline_mode=pl.Buffered(1))

    def tail(rows, de_v, fg_ref, fu_ref, w_ref, df_ref):
        ds = _dot(de_v, w_ref[...], NT)
        g, u = fg_ref[rows, :].astype(F32), fu_ref[rows, :].astype(F32)
        df_ref[rows, :] = jnp.concatenate([ds * u * _silu_grad(g), ds * _silu(g)], axis=1).astype(BF16)

    if head is None:
        def kern(de_ref, fg_ref, fu_ref, w_ref, df_ref):
            for rows, _ in _parts(pl.program_id(0), tm):
                tail(rows, de_ref[rows, :], fg_ref, fu_ref, w_ref, df_ref)

        return pl.pallas_call(
            kern, name=name, grid=(T // tm,), in_specs=[row(D), row(D_FF), row(D_FF), w_spec],
            out_specs=[row(2 * D_FF)], out_shape=[_sds((T, 2 * D_FF), BF16)], compiler_params=_params(),
        )(de, fg, fu, w)

    dx2, e, gpost, mod = head

    def kern(dx_ref, e_ref, fg_ref, fu_ref, w_ref, gpost_ref, mod_ref, de_ref, df_ref, dga_ref, dg_ref):
        i = pl.program_id(0)

        @pl.when(i == 0)
        def _():
            dga_ref[...] = jnp.zeros(dga_ref.shape, F32)
            dg_ref[...] = jnp.zeros(dg_ref.shape, F32)

        for rows, row0 in _parts(i, tm):
            de_v = _resid_bwd_vals(i, dx_ref[rows, :], e_ref[rows, :], gpost_ref[...], mod_ref, GA2, dga_ref, dg_ref,
                                   row0=row0).astype(BF16)
            de_ref[rows, :] = de_v
            tail(rows, de_v, fg_ref, fu_ref, w_ref, df_ref)

    return pl.pallas_call(
        kern, name=name, grid=(T // tm,),
        in_specs=[row(D), row(D), row(D_FF), row(D_FF), w_spec, _full_spec(gpost.shape), _full_spec(mod.shape)],
        out_specs=[row(D), row(2 * D_FF), _full_spec((2, D)), _full_spec((1, D))],
        out_shape=[_sds((T, D), BF16), _sds((T, 2 * D_FF), BF16), _sds((2, D), F32), _sds((1, D), F32)],
        compiler_params=_params(),
    )(dx2, e, fg, fu, w, gpost, mod)


def _zero_at_start(i, refs):
    @pl.when(i == 0)
    def _():
        for r in refs:
            r[...] = jnp.zeros(r.shape, F32)


def _proj_bwd_fused(name, dp, dgl, w_in_t, xa, dx1, gpre, mod, carry=None):
    T = dp.shape[0]
    tm = T // 8
    row = lambda w_: pl.BlockSpec((tm, w_), lambda i: (i, 0))

    def kern(dp_ref, dgl_ref, w_ref, xa_ref, dx1_ref, g_ref, mod_ref, dxa_ref, dsh_ref, dsc_ref, dg_ref):
        i = pl.program_id(0)
        _zero_at_start(i, (dsh_ref, dsc_ref, dg_ref))
        for rows, row0 in _parts(i, tm):
            dh = _dot(dp_ref[rows, :], w_ref[0:DP_W, :]) + _dot(dgl_ref[rows, :], w_ref[DP_W:, :])
            dxa_ref[rows, :] = dx1_ref[rows, :] + _normmod_bwd_vals(i, dh, xa_ref[rows, :], g_ref[...], mod_ref, SH1,
                                                                    SC1, dsh_ref, dsc_ref, dg_ref, row0=row0)

    ci, ca, co, cs, cscr = _carry_args(carry)
    res = pl.pallas_call(
        _carried(kern, carry, 7, 4, *_grid_ends((T // tm,))), name=name, grid=(T // tm,),
        in_specs=[row(DP_W), row(P_W - DP_W),
                  pl.BlockSpec(w_in_t.shape, lambda i: (0, 0), pipeline_mode=pl.Buffered(1)), row(D), row(D),
                  _full_spec(gpre.shape), _full_spec(mod.shape)] + ci,
        out_specs=[row(D), _full_spec((2, D)), _full_spec((2, D)), _full_spec((1, D))] + co,
        out_shape=[_sds((T, D), F32), _sds((2, D), F32), _sds((2, D), F32), _sds((1, D), F32)] + cs,
        scratch_shapes=cscr, compiler_params=_params(),
    )(dp, dgl, w_in_t, xa, dx1, gpre, mod, *ca)
    return res if carry is None else (res[:4], res[4:])


def _proj_wgrad(name, dp, dgl, h, carry=None):
    T, N = h.shape
    n1, n2 = DP_W // GLB, (P_W - DP_W) // GLB

    def kern(a1_ref, a2_ref, h_ref, o_ref):
        i = pl.program_id(0)

        @pl.when(i < n1)
        def _():
            o_ref[...] = _dot(a1_ref[...], h_ref[...], TN).astype(o_ref.dtype)

        @pl.when(i >= n1)
        def _():
            o_ref[...] = _dot(a2_ref[...], h_ref[...], TN).astype(o_ref.dtype)

    ci, ca, co, cs, cscr = _carry_args(carry)
    res = pl.pallas_call(
        _carried(kern, carry, 3, 1, *_grid_ends((n1 + n2,))), name=name, grid=(n1 + n2,),
        in_specs=[pl.BlockSpec((T, GLB), lambda i: (0, jnp.minimum(i, n1 - 1))),
                  pl.BlockSpec((T, GLB), lambda i: (0, jnp.maximum(i - n1, 0))),
                  pl.BlockSpec((T, N), lambda i: (0, 0), pipeline_mode=pl.Buffered(1))] + ci,
        out_specs=[pl.BlockSpec((GLB, N), lambda i: (i, 0))] + co,
        out_shape=[_sds((P_W, N), BF16)] + cs, scratch_shapes=cscr, compiler_params=_params(),
    )(dp, dgl, h, *ca)
    return res[0] if carry is None else (res[0], res[1:])


def _ffn_in_bwd_fused(name, df, w_t, x1, dres, mat, gpre, mod, gpost, carry=None):
    T = df.shape[0]
    tm = T // 8
    row = lambda w_: pl.BlockSpec((tm, w_), lambda i: (i, 0))

    def kern(df_ref, w_ref, x1_ref, dres_ref, mat_ref, gpre_ref, mod_ref, gpost_ref,
             dx1_ref, dm_ref, dsh_ref, dsc_ref, dgpre_ref, dga_ref, dgpost_ref):
        i = pl.program_id(0)
        _zero_at_start(i, (dsh_ref, dsc_ref, dgpre_ref, dga_ref, dgpost_ref))
        for rows, row0 in _parts(i, tm):
            dh2 = _dot(df_ref[rows, :], w_ref[...])
            dx1 = dres_ref[rows, :] + _normmod_bwd_vals(i, dh2, x1_ref[rows, :], gpre_ref[...], mod_ref, SH2, SC2,
                                                        dsh_ref, dsc_ref, dgpre_ref, row0=row0)
            dx1_ref[rows, :] = dx1
            dm_ref[rows, :] = _resid_bwd_vals(i, dx1, mat_ref[rows, :], gpost_ref[...], mod_ref, GA1, dga_ref,
                                              dgpost_ref, row0=row0).astype(BF16)

    ci, ca, co, cs, cscr = _carry_args(carry)
    res = pl.pallas_call(
        _carried(kern, carry, 8, 7, *_grid_ends((T // tm,))), name=name, grid=(T // tm,),
        in_specs=[row(2 * D_FF), pl.BlockSpec(w_t.shape, lambda i: (0, 0), pipeline_mode=pl.Buffered(1)), row(D),
                  row(D), row(D), _full_spec(gpre.shape), _full_spec(mod.shape), _full_spec(gpost.shape)] + ci,
        out_specs=[row(D), row(D), _full_spec((2, D)), _full_spec((2, D)), _full_spec((1, D)), _full_spec((2, D)),
                   _full_spec((1, D))] + co,
        out_shape=[_sds((T, D), F32), _sds((T, D), BF16), _sds((2, D), F32), _sds((2, D), F32), _sds((1, D), F32),
                   _sds((2, D), F32), _sds((1, D), F32)] + cs,
        scratch_shapes=cscr, compiler_params=_params(),
    )(df, w_t, x1, dres, mat, gpre, mod, gpost, *ca)
    return res if carry is None else (res[:7], res[7:])


def _out_bwd_fused(name, dm, w_out, p, ya, yb):
    T = dm.shape[0]
    tm = T // 8
    row = lambda w_: pl.BlockSpec((tm, w_), lambda i: (i, 0))

    def kern(dm_ref, w_ref, g0, g1, g2, g3, ya_ref, yb_ref, dya_ref, dyb_ref, dgl_ref):
        for rows, _ in _parts(pl.program_id(0), tm):
            dz = _dot(dm_ref[rows, :], w_ref[...], NT)
            ga = _sigmoid(jnp.concatenate([g0[rows, :], g1[rows, :]], axis=1).astype(F32))
            gb = _sigmoid(jnp.concatenate([g2[rows, :], g3[rows, :]], axis=1).astype(F32))
            dya_ref[rows, :] = (dz * ga).astype(BF16)
            dyb_ref[rows, :] = (dz * gb).astype(BF16)
            dgl_ref[rows, :] = jnp.concatenate([dz * ya_ref[rows, :].astype(F32) * ga * (1.0 - ga),
                                                dz * yb_ref[rows, :].astype(F32) * gb * (1.0 - gb)],
                                               axis=1).astype(BF16)

    return pl.pallas_call(
        kern, name=name, grid=(T // tm,),
        in_specs=[row(D), _full_spec(w_out.shape)]
                 + [pl.BlockSpec((tm, GLB), lambda i, q=q: (i, COL_GL // GLB + q)) for q in range(4)] + [row(D), row(D)],
        out_specs=[row(D), row(D), row(2 * D)],
        out_shape=[_sds((T, D), BF16), _sds((T, D), BF16), _sds((T, 2 * D), BF16)],
        compiler_params=_params(),
    )(dm, w_out, p, p, p, p, ya, yb)


AB = 128
CTX_BLKS = CTX // AB


def _rope_tables(S):
    pos = jnp.arange(S, dtype=jnp.int32)
    inv = ROPE_BASE ** (-jnp.arange(N_FREQ, dtype=F32) / N_FREQ)
    ang_r = (pos // GRID_W).astype(F32)[:, None] * inv[None, :]
    ang_c = (pos % GRID_W).astype(F32)[:, None] * inv[None, :]
    cos = jnp.concatenate([jnp.cos(ang_r)] * 2 + [jnp.cos(ang_c)] * 2, axis=1)
    sin = jnp.concatenate([-jnp.sin(ang_r), jnp.sin(ang_r), -jnp.sin(ang_c), jnp.sin(ang_c)], axis=1)
    return cos, sin


def _rope(x, cos, sin):
    w = x.shape[1]
    reps = w // HEAD
    lane = lax.broadcasted_iota(jnp.int32, x.shape, 1)
    partner = jnp.where((lane & 63) < 32, pltpu.roll(x, w - 32, 1), pltpu.roll(x, 32, 1))
    return x * jnp.tile(cos, (1, reps)) + partner * jnp.tile(sin, (1, reps))


def _unrope(dx, cos, sin):
    w = dx.shape[1]
    reps = w // HEAD
    lane = lax.broadcasted_iota(jnp.int32, dx.shape, 1)
    t = dx * jnp.tile(sin, (1, reps))
    partner = jnp.where((lane & 63) < 32, pltpu.roll(t, w - 32, 1), pltpu.roll(t, 32, 1))
    return dx * jnp.tile(cos, (1, reps)) + partner


def _qkv_prep(name, p, cos, sin, S):
    T = CTX + S
    nt = T // AB
    KW = N_KV * HEAD

    def with_ones(v):
        ones = jnp.ones((AB, HEAD), BF16)
        return jnp.concatenate([v[:, kh * HEAD:(kh + 1) * HEAD] if part == 0 else ones
                                for kh in range(N_KV) for part in range(2)], axis=1)

    def kern(q_ref, k_ref, v_ref, cos_ref, sin_ref, qa_ref, kp_ref, vp_ref, kc_ref, vc_ref):
        i = pl.program_id(0)
        cos_v, sin_v = cos_ref[...], sin_ref[...]
        @pl.when(i < CTX_BLKS)
        def _():
            qa_ref[...] = (q_ref[...].astype(F32) * ATT_SCALE).astype(BF16)
            kc_ref[...] = k_ref[...]
            vc_ref[...] = with_ones(v_ref[...])

        @pl.when((i < CTX_BLKS) | (i >= nt))
        def _():
            kp_ref[...] = jnp.zeros(kp_ref.shape, BF16)
            vp_ref[...] = jnp.zeros(vp_ref.shape, BF16)

        @pl.when((i >= CTX_BLKS) & (i < nt))
        def _():
            qa_ref[...] = (_rope(q_ref[...].astype(F32), cos_v, sin_v) * ATT_SCALE).astype(BF16)
            kp_ref[...] = _rope(k_ref[...].astype(F32), cos_v, sin_v).astype(BF16)
            vp_ref[...] = with_ones(v_ref[...])

    tok = lambda i: jnp.minimum(i, nt - 1)
    lat_map = lambda i: (jnp.clip(i - CTX_BLKS, 0, nt - CTX_BLKS - 1), 0)
    ctx_map = lambda i: (jnp.minimum(i, CTX_BLKS - 1), 0)
    return pl.pallas_call(
        kern, name=name, grid=(nt + CTX_BLKS,),
        in_specs=[pl.BlockSpec((AB, N_Q * HEAD), lambda i: (tok(i), COL_Q // (N_Q * HEAD))),
                  pl.BlockSpec((AB, KW), lambda i: (tok(i), COL_K // KW)),
                  pl.BlockSpec((AB, KW), lambda i: (tok(i), COL_V // KW)),
                  pl.BlockSpec((AB, HEAD), lat_map), pl.BlockSpec((AB, HEAD), lat_map)],
        out_specs=[pl.BlockSpec((AB, N_Q * HEAD), lambda i: (tok(i), 0)),
                   pl.BlockSpec((AB, KW), lambda i: (i, 0)), pl.BlockSpec((AB, 2 * KW), lambda i: (i, 0)),
                   pl.BlockSpec((AB, KW), ctx_map), pl.BlockSpec((AB, 2 * KW), ctx_map)],
        out_shape=[_sds((T, N_Q * HEAD), BF16), _sds((S + 2 * CTX, KW), BF16), _sds((S + 2 * CTX, 2 * KW), BF16),
                   _sds((CTX, KW), BF16), _sds((CTX, 2 * KW), BF16)],
        compiler_params=_params(),
    )(p, p, p, cos, sin)


GW = Q_PER_KV * HEAD


def _band_bias(S):
    r = jnp.arange(AB, dtype=jnp.int32)[:, None]
    c = jnp.arange(3 * AB, dtype=jnp.int32)[None, :]
    near = jnp.abs(c - AB - r) <= AB
    valid = jnp.stack([near & (c >= AB), near, near & (c < 2 * AB)])
    return jnp.where(valid, 0.0, NEG_INF).astype(F32)


def _bias_spec(S):
    nb = S // AB
    return pl.BlockSpec((None, AB, 3 * AB), lambda kh, n: (jnp.where(n == 0, 0, jnp.where(n == nb - 1, 2, 1)), 0, 0))


def _head_probs(q, sink, kc, vce, kb, vbe, bias):
    s_c = _dot(q, kc, NT)
    m = jnp.maximum(jnp.max(s_c, axis=-1, keepdims=True), sink)
    if kb is not None:
        s_b = _dot(q, kb, NT) + bias
        m = jnp.maximum(m, jnp.max(s_b, axis=-1, keepdims=True))
    p_c = jnp.exp(s_c - m).astype(BF16)
    acc = _dot(p_c, vce)
    p_b = None
    if kb is not None:
        p_b = jnp.exp(s_b - m).astype(BF16)
        acc = acc + _dot(p_b, vbe)
    return p_c, p_b, m, acc


def _attn_fwd(name, qa, kc, vc, sink4, S, band=None, prev=None, carry=None):
    T = qa.shape[0]
    has_band = band is not None
    nq = S // AB if has_band else CTX_BLKS
    q_off = CTX_BLKS if has_band else 0

    def kern(*refs):
        q_ref, kc_ref, vc_ref, sink_ref = refs[:4]
        rest = refs[4:]
        o_ref = rest[-1]
        n = pl.program_id(1)
        kc_v, vce = kc_ref[...], vc_ref[...]
        kb = vbe = bias = None
        if has_band:
            kp_ref, vp_ref, bias_ref = rest[:3]
            start = pl.multiple_of(n * AB + (CTX - AB), AB)
            kb = kp_ref[pl.ds(start, 3 * AB), :]
            vbe = vp_ref[pl.ds(start, 3 * AB), :]
            bias = bias_ref[...]
        outs = []
        for g in range(Q_PER_KV):
            sink = sink_ref[g:g + 1, 0:1]
            _, _, m, acc = _head_probs(q_ref[:, g * HEAD:(g + 1) * HEAD], sink, kc_v, vce, kb, vbe, bias)
            l = acc[:, HEAD:] + jnp.exp(sink - m)
            outs.append(acc[:, :HEAD] / l)
        o_ref[...] = jnp.concatenate(outs, axis=1).astype(BF16)

    in_specs = [pl.BlockSpec((AB, GW), lambda kh, n: (n + q_off, kh)),
                pl.BlockSpec((CTX, HEAD), lambda kh, n: (0, kh)), pl.BlockSpec((CTX, 2 * HEAD), lambda kh, n: (0, kh)),
                pl.BlockSpec((None, Q_PER_KV, HEAD), lambda kh, n: (kh, 0, 0))]
    args = [qa, kc, vc, sink4]
    if has_band:
        in_specs += [pl.BlockSpec((S + 2 * CTX, HEAD), lambda kh, n: (0, kh)),
                     pl.BlockSpec((S + 2 * CTX, 2 * HEAD), lambda kh, n: (0, kh)), _bias_spec(S)]
        args += list(band)
    alias = {}
    if prev is not None:
        in_specs.append(ANY)
        alias = {len(args): 0}
        args.append(prev)
    ci, ca, co, cs, cscr = _carry_args(carry)
    res = pl.pallas_call(
        _carried(kern, carry, len(args), 1, *_grid_ends((N_KV, nq))), name=name, grid=(N_KV, nq),
        in_specs=in_specs + ci,
        out_specs=[pl.BlockSpec((AB, GW), lambda kh, n: (n + q_off, kh))] + co,
        out_shape=[_sds((T, N_Q * HEAD), BF16)] + cs, input_output_aliases=alias, scratch_shapes=cscr,
        compiler_params=_params(("arbitrary", "arbitrary")),
    )(*args, *ca)
    return res[0] if carry is None else (res[0], res[1:])


def _attn_bwd(name, qa, kc, vc, sink4, o_all, do_all, S, band=None, prev_dq=None, carry=None):
    T = qa.shape[0]
    has_band = band is not None
    nq = S // AB if has_band else CTX_BLKS
    q_off = CTX_BLKS if has_band else 0
    KW = N_KV * HEAD

    def kern(*refs):
        q_ref, kc_ref, vc_ref, sink_ref, o_ref, do_ref = refs[:6]
        rest = refs[6:]
        if has_band:
            kp_ref, vp_ref, bias_ref = rest[:3]
            rest = rest[3:]
        if prev_dq is not None:
            rest = rest[1:]
        dq_ref, dkc_ref, dvc_ref, dsink_ref = rest[:4]
        n = pl.program_id(1)

        @pl.when(n == 0)
        def _():
            dkc_ref[...] = jnp.zeros(dkc_ref.shape, F32)
            dvc_ref[...] = jnp.zeros(dvc_ref.shape, F32)
            dsink_ref[...] = jnp.zeros(dsink_ref.shape, F32)
            if has_band:
                rest[4][...] = jnp.zeros(rest[4].shape, F32)
                rest[5][...] = jnp.zeros(rest[5].shape, F32)

        kc_v, vce = kc_ref[...], vc_ref[...]
        vc_v = vce[:, :HEAD]
        kb = vbe = vb = bias = None
        if has_band:
            start = pl.multiple_of(n * AB + (CTX - AB), AB)
            kb = kp_ref[pl.ds(start, 3 * AB), :]
            vbe = vp_ref[pl.ds(start, 3 * AB), :]
            vb = vbe[:, :HEAD]
            bias = bias_ref[...]
        stack = lambda ref: jnp.concatenate([ref[:, g * HEAD:(g + 1) * HEAD] for g in range(Q_PER_KV)], axis=0)
        q4, do4 = stack(q_ref), stack(do_ref)
        sink = jnp.concatenate([jnp.broadcast_to(sink_ref[g:g + 1, 0:1], (AB, 1)) for g in range(Q_PER_KV)], axis=0)
        s_c = _dot(q4, kc_v, NT)
        m = jnp.maximum(jnp.max(s_c, axis=-1, keepdims=True), sink)
        if has_band:
            s_b = _dot(q4, kb, NT) + jnp.tile(bias, (Q_PER_KV, 1))
            m = jnp.maximum(m, jnp.max(s_b, axis=-1, keepdims=True))
        p_c = jnp.exp(s_c - m).astype(BF16).astype(F32)
        p_sink = jnp.exp(sink - m)
        l = jnp.sum(p_c, axis=-1, keepdims=True) + p_sink
        if has_band:
            p_b = jnp.exp(s_b - m).astype(BF16).astype(F32)
            l = l + jnp.sum(p_b, axis=-1, keepdims=True)
        inv = 1.0 / l
        delta = jnp.sum(do4.astype(F32) * stack(o_ref).astype(F32), axis=-1, keepdims=True)
        do4b = do4.astype(BF16)
        pn_c = (p_c * inv).astype(BF16)
        ds_c = (p_c * inv * (_dot(do4b, vc_v, NT) - delta)).astype(BF16)
        dq4 = _dot(ds_c, kc_v)
        dkc_ref[...] += _dot(ds_c, q4, TN)
        dvc_ref[...] += _dot(pn_c, do4b, TN)
        if has_band:
            pn_b = (p_b * inv).astype(BF16)
            ds_b = (p_b * inv * (_dot(do4b, vb, NT) - delta)).astype(BF16)
            dq4 = dq4 + _dot(ds_b, kb)
            rest[4][pl.ds(start, 3 * AB), :] += _dot(ds_b, q4, TN)
            rest[5][pl.ds(start, 3 * AB), :] += _dot(pn_b, do4b, TN)
        dq4 = dq4 * ATT_SCALE
        dq_ref[...] = jnp.concatenate([dq4[g * AB:(g + 1) * AB, :] for g in range(Q_PER_KV)], axis=1)
        ps = p_sink * inv * delta
        dsink_ref[...] += jnp.concatenate(
            [jnp.broadcast_to(-jnp.sum(ps[g * AB:(g + 1) * AB, :], axis=0, keepdims=True), (1, HEAD))
             for g in range(Q_PER_KV)], axis=0)

    q_spec = pl.BlockSpec((AB, GW), lambda kh, n: (n + q_off, kh))
    c_spec = pl.BlockSpec((CTX, HEAD), lambda kh, n: (0, kh))
    ce_spec = pl.BlockSpec((CTX, 2 * HEAD), lambda kh, n: (0, kh))
    s_spec = pl.BlockSpec((None, Q_PER_KV, HEAD), lambda kh, n: (kh, 0, 0))
    in_specs = [q_spec, c_spec, ce_spec, s_spec, q_spec, q_spec]
    args = [qa, kc, vc, sink4, o_all, do_all]
    out_specs = [q_spec, c_spec, c_spec, s_spec]
    out_shape = [_sds((T, N_Q * HEAD), F32), _sds((CTX, KW), F32), _sds((CTX, KW), F32), _sds((N_KV, Q_PER_KV, HEAD), F32)]
    if has_band:
        p_spec = pl.BlockSpec((S + 2 * CTX, HEAD), lambda kh, n: (0, kh))
        in_specs += [p_spec, pl.BlockSpec((S + 2 * CTX, 2 * HEAD), lambda kh, n: (0, kh)), _bias_spec(S)]
        args += list(band)
        out_specs += [p_spec, p_spec]
        out_shape += [_sds((S + 2 * CTX, KW), F32)] * 2
    alias = {}
    if prev_dq is not None:
        in_specs.append(ANY)
        alias = {len(args): 0}
        args.append(prev_dq)
    ci, ca, co, cs, cscr = _carry_args(carry)
    n_out = len(out_specs)
    res = pl.pallas_call(
        _carried(kern, carry, len(args), n_out, *_grid_ends((N_KV, nq))), name=name, grid=(N_KV, nq),
        in_specs=in_specs + ci, out_specs=out_specs + co, out_shape=out_shape + cs, scratch_shapes=cscr,
        input_output_aliases=alias, compiler_params=_params(("arbitrary", "arbitrary")),
    )(*args, *ca)
    return res if carry is None else (res[:n_out], res[n_out:])


def _dqkv_assemble(name, dq_all, dkp, dvp, dkc_l, dvc_l, dkc_c, dvc_c, cos, sin, S):
    T = CTX + S
    KW = N_KV * HEAD
    HALF = N_Q * HEAD // 2

    def kern(dq_ref, dkp_ref, dvp_ref, dkcl_ref, dvcl_ref, dkcc_ref, dvcc_ref, cos_ref, sin_ref, out_ref):
        i = pl.program_id(0)
        j = pl.program_id(1)
        cos_v, sin_v = cos_ref[...], sin_ref[...]

        @pl.when((j < 2) & (i == 0))
        def _():
            out_ref[...] = dq_ref[...].astype(BF16)

        @pl.when((j < 2) & (i > 0))
        def _():
            out_ref[...] = _unrope(dq_ref[...], cos_v, sin_v).astype(BF16)

        @pl.when((j == 2) & (i == 0))
        def _():
            out_ref[...] = jnp.concatenate([dkcl_ref[...] + dkcc_ref[...], dvcl_ref[...] + dvcc_ref[...]],
                                           axis=1).astype(BF16)

        @pl.when((j == 2) & (i > 0))
        def _():
            out_ref[...] = jnp.concatenate([_unrope(dkp_ref[...], cos_v, sin_v), dvp_ref[...]], axis=1).astype(BF16)

    same = lambda i, j: (i, 0)
    lat_map = lambda i, j: (jnp.maximum(i - 1, 0), 0)
    ctx_map = lambda i, j: (0, 0)
    return pl.pallas_call(
        kern, name=name, grid=(T // TR, 3),
        in_specs=[pl.BlockSpec((TR, HALF), lambda i, j: (i, jnp.minimum(j, 1))),
                  pl.BlockSpec((TR, KW), same), pl.BlockSpec((TR, KW), same),
                  pl.BlockSpec((CTX, KW), ctx_map), pl.BlockSpec((CTX, KW), ctx_map),
                  pl.BlockSpec((CTX, KW), ctx_map), pl.BlockSpec((CTX, KW), ctx_map),
                  pl.BlockSpec((TR, HEAD), lat_map), pl.BlockSpec((TR, HEAD), lat_map)],
        out_specs=pl.BlockSpec((TR, HALF), lambda i, j: (i, COL_Q // HALF + j)),
        out_shape=_sds((T, DP_W), BF16), compiler_params=_params(("arbitrary", "arbitrary")),
    )(dq_all, dkp, dvp, dkc_l, dvc_l, dkc_c, dvc_c, cos, sin)


RB = 128
CH = 256
HALO = 8
SUB = 8
GRP = 8


def _vscan(a, b, reverse):
    row = lax.broadcasted_iota(jnp.int32, a.shape, 0)
    A, H = a, b
    for s in (1, 2, 4):
        sh = SUB - s if reverse else s
        m = (row < SUB - s) if reverse else (row >= s)
        As = pltpu.roll(A, sh, 0)
        Hs = pltpu.roll(H, sh, 0)
        H = jnp.where(m, A * Hs + H, H)
        A = jnp.where(m, A * As, A)
    return A, H


def _scan_rows(a_ref, b_ref, r0, nrows, reverse, carry, emit):
    ngrp = nrows // (SUB * GRP)
    row = lax.broadcasted_iota(jnp.int32, (SUB, RB), 0)

    def grp(gi, carry):
        g = (ngrp - 1 - gi) if reverse else gi
        base = r0 + g * (SUB * GRP)
        for v in (range(GRP - 1, -1, -1) if reverse else range(GRP)):
            rs = pl.multiple_of(base + v * SUB, SUB)
            A, H = _vscan(a_ref[pl.ds(rs, SUB), :], b_ref[pl.ds(rs, SUB), :], reverse)
            hf = H + A * carry
            if reverse:
                before = jnp.where(row == SUB - 1, carry, pltpu.roll(hf, SUB - 1, 0))
                carry = hf[0:1, :]
            else:
                before = jnp.where(row == 0, carry, pltpu.roll(hf, 1, 0))
                carry = hf[SUB - 1:SUB, :]
            emit(rs, hf, before)
        return carry

    return lax.fori_loop(0, ngrp, grp, carry)


def _pad_start(ci):
    return pl.multiple_of(ci * CH + HALO * jnp.minimum(ci, 1), HALO)


def _conv_taps(ext, transpose=False):
    n = CH + 2 * HALO
    taps = []
    for k in range(CONV_W):
        off = CONV_LEFT - k if transpose else k - CONV_LEFT
        taps.append(ext[HALO:HALO + CH, :] if off == 0 else pltpu.roll(ext, (-off) % n, 0)[HALO:HALO + CH, :])
    return taps


def _lru_gates(xl, w4, b4, ls):
    pre = _dot(xl.astype(BF16), w4) + b4
    out = []
    for d in range(2):
        r = _sigmoid(pre[:, d * RB:(d + 1) * RB])
        i = _sigmoid(pre[:, (2 + d) * RB:(3 + d) * RB])
        la = LRU_C * r * ls[d:d + 1, :]
        a = jnp.exp(la)
        q = -jnp.tanh(la) * (1.0 + a * a)
        out.append((r, i, a, q))
    return out


def _rnn_specs(T):
    col = lambda n, *_: (0, n)
    return dict(
        xr=pl.BlockSpec((T, RB), lambda n, *_: (0, COL_XR // RB + n)),
        gr=pl.BlockSpec((T, RB), lambda n, *_: (0, COL_GR // RB + n)),
        act=pl.BlockSpec((T, RB), col),
        cw=pl.BlockSpec((CONV_W, RB), col), cb=pl.BlockSpec((1, RB), col),
        w4=pl.BlockSpec((None, RB, 4 * RB), lambda n, *_: (n, 0, 0)),
        b4=pl.BlockSpec((None, 1, 4 * RB), lambda n, *_: (n, 0, 0)),
        lam=pl.BlockSpec((2, RB), col))


PAD_ROWS = 3 * HALO


def _zero_pads(pad_ref, T):
    for r in (0, HALO + CTX, 2 * HALO + T):
        pad_ref[r:r + HALO, :] = jnp.zeros((HALO, RB), F32)


def _fill_padded(pad_ref, src_ref, T):
    _zero_pads(pad_ref, T)
    pad_ref[HALO:HALO + CTX, :] = src_ref[0:CTX, :].astype(F32)
    pad_ref[2 * HALO + CTX:2 * HALO + T, :] = src_ref[CTX:T, :].astype(F32)


def _pad_rows(ci):
    return pl.ds(pl.multiple_of(ci * CH + HALO + HALO * jnp.minimum(ci, 1), HALO), CH)


def _rnn_fwd(name, p, cw, cb, w4, b4, lam, T, carry=None):
    def kern(xr_ref, gr_ref, cw_ref, cb_ref, w4_ref, b4_ref, lam_ref,
             u_ref, a0, a1, yo_ref, hpf_ref, hpb_ref, r0_ref, r1_ref, i0_ref, i1_ref, xpad, b0, b1, y):
        _fill_padded(xpad, xr_ref, T)
        ls = _log_sigmoid(lam_ref[...])
        w4v, b4v, cwv, cbv = w4_ref[...], b4_ref[...], cw_ref[...], cb_ref[...]

        def chunk(ci, _):
            rows = pl.ds(pl.multiple_of(ci * CH, CH), CH)
            taps = _conv_taps(xpad[pl.ds(_pad_start(ci), CH + 2 * HALO), :])
            xl = cbv + sum(taps[k] * cwv[k:k + 1, :] for k in range(CONV_W))
            for d, (r, i, a, q) in enumerate(_lru_gates(xl, w4v, b4v, ls)):
                (a0, a1)[d][rows, :] = a
                (b0, b1)[d][rows, :] = jnp.sqrt(q) * (i * xl)
                (r0_ref, r1_ref)[d][rows, :] = r.astype(BF16)
                (i0_ref, i1_ref)[d][rows, :] = i.astype(BF16)
            return 0

        lax.fori_loop(0, T // CH, chunk, 0)
        zero = jnp.zeros((1, RB), F32)

        def emit_f(rs, hf, before):
            y[pl.ds(rs, SUB), :] = hf
            b0[pl.ds(rs, SUB), :] = before

        def emit_b(rs, hf, before):
            y[pl.ds(rs, SUB), :] += hf
            b1[pl.ds(rs, SUB), :] = before

        _scan_rows(a0, b0, 0, T, False, zero, emit_f)
        c = _scan_rows(a1, b1, 0, CTX, True, zero, emit_b)
        _scan_rows(a1, b1, CTX, T - CTX, True, c, emit_b)

        def finish(ci, _):
            rows = pl.ds(pl.multiple_of(ci * CH, CH), CH)
            yv = y[rows, :]
            u_ref[rows, :] = (yv * _gelu(gr_ref[rows, :].astype(F32))).astype(BF16)
            yo_ref[rows, :] = yv.astype(BF16)
            hpf_ref[rows, :] = b0[rows, :].astype(BF16)
            hpb_ref[rows, :] = b1[rows, :].astype(BF16)
            return 0

        lax.fori_loop(0, T // CH, finish, 0)

    sp = _rnn_specs(T)
    ci, ca, co, cs, cscr = _carry_args(carry)
    dts = [BF16, F32, F32] + [BF16] * 7
    res = pl.pallas_call(
        _carried(kern, carry, 7, 10, *_grid_ends((N_RNN_BLOCKS,))), name=name, grid=(N_RNN_BLOCKS,),
        in_specs=[sp["xr"], sp["gr"], sp["cw"], sp["cb"], sp["w4"], sp["b4"], sp["lam"]] + ci,
        out_specs=[sp["act"]] * 10 + co,
        out_shape=[_sds((T, D), dt) for dt in dts] + cs,
        scratch_shapes=[pltpu.VMEM((T + PAD_ROWS, RB), F32)] + [pltpu.VMEM((T, RB), F32)] * 3 + cscr,
        compiler_params=_params(),
    )(p, p, cw, cb, w4, b4, lam, *ca)
    return res if carry is None else (res[:10], res[10:])


def _rnn_bwd(name, p, du, saved, dp, cw, cb, w4, b4, lam, T, carry=None):
    def kern(xr_ref, gr_ref, du_ref, a0, a1, y_ref, hpf_ref, hpb_ref, r0_ref, r1_ref, i0_ref, i1_ref,
             cw_ref, cb_ref, w4_ref, b4_ref, lam_ref, dp_in,
             dp_ref, dcw_ref, dcb_ref, dw4_ref, db4_ref, dlam_ref,
             xpad, dxpad, c0, c1, dy, dgr_ref):
        j = pl.program_id(1)

        @pl.when(j == 0)
        def _():
            work(xr_ref, gr_ref, du_ref, a0, a1, y_ref, (hpf_ref, hpb_ref), (r0_ref, r1_ref), (i0_ref, i1_ref),
                 cw_ref, cb_ref, w4_ref, lam_ref, dp_ref, dgr_ref, dcw_ref, dcb_ref, dw4_ref, db4_ref, dlam_ref,
                 xpad, dxpad, c0, c1, dy)

        @pl.when(j == 1)
        def _():
            dp_ref[...] = dgr_ref[...]

    def work(xr_ref, gr_ref, du_ref, a0, a1, y_ref, hp_refs, r_refs, i_refs, cw_ref, cb_ref, w4_ref, lam_ref,
             dxr_ref, dgr_ref, dcw_ref, dcb_ref, dw4_ref, db4_ref, dlam_ref, xpad, dxpad, c0, c1, dy):
        _fill_padded(xpad, xr_ref, T)
        _zero_pads(dxpad, T)
        lam_v = lam_ref[...]
        ls = _log_sigmoid(lam_v)
        w4v, cwv, cbv = w4_ref[...], cw_ref[...], cb_ref[...]

        def conv_chunk(ci):
            taps = _conv_taps(xpad[pl.ds(_pad_start(ci), CH + 2 * HALO), :])
            return taps, cbv + sum(taps[k] * cwv[k:k + 1, :] for k in range(CONV_W))

        def phase_a(ci, _):
            rows = pl.ds(pl.multiple_of(ci * CH, CH), CH)
            gr = gr_ref[rows, :].astype(F32)
            duv = du_ref[rows, :].astype(F32)
            dyv = duv * _gelu(gr)
            dgr_ref[rows, :] = (duv * y_ref[rows, :].astype(F32) * _gelu_grad(gr)).astype(BF16)
            dy[rows, :] = dyv
            c0[rows, :] = a0[rows, :] * dyv
            c1[rows, :] = a1[rows, :] * dyv
            return 0

        lax.fori_loop(0, T // CH, phase_a, 0)
        zero = jnp.zeros((1, RB), F32)

        def emit0(rs, hf, before):
            c0[pl.ds(rs, SUB), :] = dy[pl.ds(rs, SUB), :] + before

        def emit1(rs, hf, before):
            c1[pl.ds(rs, SUB), :] = dy[pl.ds(rs, SUB), :] + before

        _scan_rows(a0, c0, 0, T, True, zero, emit0)
        c = _scan_rows(a1, c1, CTX, T - CTX, False, zero, emit1)
        _scan_rows(a1, c1, 0, CTX, False, c, emit1)

        dw4_ref[...] = jnp.zeros(dw4_ref.shape, F32)
        db4_ref[...] = jnp.zeros(db4_ref.shape, F32)
        dlam_ref[...] = jnp.zeros(dlam_ref.shape, F32)
        dcw_ref[...] = jnp.zeros(dcw_ref.shape, F32)
        dcb_ref[...] = jnp.zeros(dcb_ref.shape, F32)

        def phase_c(ci, _):
            base = pl.multiple_of(ci * CH, CH)
            rows = pl.ds(base, CH)
            _, xl = conv_chunk(ci)
            dxl = jnp.zeros((CH, RB), F32)
            dpre_a, dpre_x, dls = [], [], []
            for d in range(2):
                a = (a0, a1)[d][rows, :]
                r = r_refs[d][rows, :].astype(F32)
                i = i_refs[d][rows, :].astype(F32)
                q = -jnp.tanh(LRU_C * r * ls[d:d + 1, :]) * (1.0 + a * a)
                g = (c0, c1)[d][rows, :]
                hp = hp_refs[d][rows, :].astype(F32)
                gm = g * jnp.sqrt(q)
                di = gm * xl
                dxl = dxl + gm * i
                dla = a * (g * hp - a * (g * (i * xl)) * lax.rsqrt(q))
                dr = dla * (LRU_C * ls[d:d + 1, :])
                dls.append(_colsum(dla * (LRU_C * r)))
                dpre_a.append(dr * r * (1.0 - r))
                dpre_x.append(di * i * (1.0 - i))
            dpre = jnp.concatenate(dpre_a + dpre_x, axis=1)
            dpre_b = dpre.astype(BF16)
            dxl = dxl + _dot(dpre_b, w4v, NT)
            dw4_ref[...] += _dot(xl.astype(BF16), dpre_b, TN)
            db4_ref[...] += _colsum(dpre)
            dlam_ref[...] += jnp.concatenate(dls, axis=0)
            dcb_ref[...] += _colsum(dxl)
            dxpad[_pad_rows(ci), :] = dxl
            return 0

        lax.fori_loop(0, T // CH, phase_c, 0)
        dlam_ref[...] = dlam_ref[...] * _sigmoid(-lam_v)

        def phase_d(ci, _):
            base = pl.multiple_of(ci * CH, CH)
            rows = pl.ds(base, CH)
            xtaps, _ = conv_chunk(ci)
            dtaps = _conv_taps(dxpad[pl.ds(_pad_start(ci), CH + 2 * HALO), :], transpose=True)
            dxl = dxpad[_pad_rows(ci), :]
            dxr_ref[rows, :] = sum(dtaps[k] * cwv[k:k + 1, :] for k in range(CONV_W)).astype(BF16)
            dcw_ref[...] += jnp.concatenate([_colsum(dxl * xtaps[k]) for k in range(CONV_W)], axis=0)
            return 0

        lax.fori_loop(0, T // CH, phase_d, 0)

    sp = _rnn_specs(T)
    dp_spec = pl.BlockSpec((T, RB), lambda n, j: (0, COL_XR // RB + n + j * (COL_GR - COL_XR) // RB))
    ci, ca, co, cs, cscr = _carry_args(carry)
    n_in = 3 + len(saved) + 5 + 1
    res = pl.pallas_call(
        _carried(kern, carry, n_in, 6, *_grid_ends((N_RNN_BLOCKS, 2))), name=name, grid=(N_RNN_BLOCKS, 2),
        in_specs=[sp["xr"], sp["gr"]] + [sp["act"]] * (1 + len(saved)) + [sp["cw"], sp["cb"], sp["w4"], sp["b4"],
                                                                           sp["lam"], ANY] + ci,
        out_specs=[dp_spec, sp["cw"], sp["cb"], sp["w4"], sp["b4"], sp["lam"]] + co,
        out_shape=[_sds((T, DP_W), BF16), _sds((CONV_W, D), F32), _sds((1, D), F32),
                   _sds((N_RNN_BLOCKS, RB, 4 * RB), F32), _sds((N_RNN_BLOCKS, 1, 4 * RB), F32), _sds((2, D), F32)] + cs,
        scratch_shapes=([pltpu.VMEM((T + PAD_ROWS, RB), F32)] * 2 + [pltpu.VMEM((T, RB), F32)] * 3
                        + [pltpu.VMEM((T, RB), BF16)] + cscr),
        input_output_aliases={n_in - 1: 0},
        compiler_params=_params(("arbitrary", "arbitrary")),
    )(p, p, du, *saved, cw, cb, w4, b4, lam, dp, *ca)
    return res if carry is None else (res[:6], res[6:])


class _Plan:
    def __init__(self, shards, Ws):
        L = len(Ws)
        self.shards, self.Ws = shards, Ws
        self.Gs = [None] * L
        self.slots = [dict() for _ in range(L)]
        self.gate_slots = [None] * L
        self.table = {}
        for l in range(L):
            t = f"l{l}_"
            self.table[t + "proj"] = [("gather", l, k) for k in ("wo_rnn", "wo_attn", "wout")]
            self.table[t + "rnn_fwd"] = [("gather", l, "wffn_in_t")]
            self.table[t + "attn_lat_fwd"] = [("gather", l + 1, "win_t")] if l + 1 < L else []
            self.table[t + "ffn_in"] = [("gather", l, "wffn_out")]
            self.table[t + "ffn_in_dx"] = [("scatter", l, "wffn_out")]
            self.table[t + "attn_lat_bwd"] = [("scatter", l, "wffn_in_t")]
            self.table[t + "ffn_in_dw"] = [("gates", l + 1, "w4")] if l + 1 < L else []
            self.table[t + "rnn_bwd"] = ([("scatter", l, k) for k in ("wout", "wo_attn", "wo_rnn")]
                                         + ([("scatter", l + 1, "win_t")] if l + 1 < L else []))
        self.table["l0_proj_dx"] = [("scatter", 0, "win_t_a")]
        self.table["l0_proj_dw_b"] = [("gates", 0, "w4")]

    def carry(self, name):
        jobs = []
        for kind, l, k in self.table.get(name, []):
            if kind == "gather":
                jobs.append(("gather", self.shards[l][k]))
            elif kind == "scatter":
                jobs.append(("scatter", self.Gs[l][k].reshape(N_DEV, -1, self.Gs[l][k].shape[-1])))
            else:
                jobs.append(("gather", self.Gs[l]["w4"].reshape(N_RNN_BLOCKS * RB, 4 * RB).astype(BF16)))
        return _Carry(jobs) if jobs else None

    def done(self, name, got):
        for (kind, l, k), res in zip(self.table[name], got):
            if kind == "gather":
                self.Ws[l][k] = res.reshape(-1, D)
            elif kind == "scatter":
                self.slots[l][k] = res
            else:
                self.gate_slots[l] = res


def _run(X, fn, name, *args, **kw):
    carry = None if X is None else X.carry(name)
    if carry is None:
        return fn(name, *args, **kw)
    out, got = fn(name, *args, carry=carry, **kw)
    X.done(name, got)
    return out


def _layer_fwd(l, xa, h, W, rope, S, nxt, X=None):
    T = xa.shape[0]
    tag = f"l{l}_"
    cos, sin, bias = rope
    p = _run(X, _mm_act, tag + "proj", h, W["win_t"], "NT", BF16)
    u, *rnn_saved = _run(X, _rnn_fwd, tag + "rnn_fwd", p, W["cw"], W["cb"], W["w4"], W["b4"], W["lam"], T)
    qa, kp, vp, kc, vc = _qkv_prep(tag + "qkv_prep", p, cos, sin, S)
    o_all = _attn_fwd(tag + "attn_ctx_fwd", qa, kc, vc, W["sink4"], S)
    o_all = _run(X, _attn_fwd, tag + "attn_lat_fwd", qa, kc, vc, W["sink4"], S, band=(kp, vp, bias), prev=o_all)
    ya = _mm_act(tag + "o_rnn", u, W["wo_rnn"], "NN")
    yb = _mm_act(tag + "o_attn", o_all, W["wo_attn"], "NN")
    z, m, x1, h2 = _out_fused(tag + "out", p, ya, yb, xa, W["wout"], W["g_mix_post"], W["mod"], W["g_ffn_pre"])
    fg, fu, s = _run(X, _ffn_in_fused, tag + "ffn_in", h2, W["wffn_in_t"])
    e, *out = _ffn_out_fused(tag + "ffn_out", s, W["wffn_out"], x1, W["g_ffn_post"], W["mod"], nxt)
    saved = dict(xa=xa, h=h, p=p, u=u, rnn=rnn_saved, qa=qa, kp=kp, vp=vp, kc=kc, vc=vc, o_all=o_all,
                 ya=ya, yb=yb, z=z, m=m, x1=x1, h2=h2, fg=fg, fu=fu, s=s, e=e)
    return saved, out


def _layer_bwd(l, dx2, A, W, rope, S, X=None, loss_of=None):
    T = A["xa"].shape[0]
    tag = f"l{l}_"
    cos, sin, bias = rope
    G = {}
    if X is not None:
        X.Gs[l] = G
    if loss_of is None:
        de, df, dga2, G["g_ffn_post"] = _ffn_bwd_fused(tag + "ffn_bwd", A["fg"], A["fu"], W["wffn_out"],
                                                       head=(dx2, A["e"], W["g_ffn_post"], W["mod"]))
    else:
        dx2, de, dga2, G["g_ffn_post"], G["sq"] = _loss_resid_bwd(tag + "loss_ffn_resid_bwd", *loss_of, A["e"],
                                                                  W["g_ffn_post"], W["mod"], GA2)
        df, = _ffn_bwd_fused(tag + "ffn_bwd", A["fg"], A["fu"], W["wffn_out"], de=de)
    G["wffn_out"] = _mm_wgrad(tag + "ffn_out_dw", A["s"], de)
    dx1, dm, dsh2, dsc2, G["g_ffn_pre"], dga1, G["g_mix_post"] = _run(
        X, _ffn_in_bwd_fused, tag + "ffn_in_dx", df, W["wffn_in_t"], A["x1"], dx2, A["m"], W["g_ffn_pre"], W["mod"],
        W["g_mix_post"])
    G["wffn_in_t"] = _run(X, _mm_wgrad, tag + "ffn_in_dw", df, A["h2"])
    G["wout"] = _mm_wgrad(tag + "out_dw", A["z"], dm)
    dya, dyb, dgl = _out_bwd_fused(tag + "out_dx", dm, W["wout"], A["p"], A["ya"], A["yb"])
    do = _mm_act(tag + "o_attn_dx", dyb, W["wo_attn"], "NT")
    G["wo_attn"] = _mm_wgrad(tag + "o_attn_dw", A["o_all"], dyb)
    du = _mm_act(tag + "o_rnn_dx", dya, W["wo_rnn"], "NT")
    G["wo_rnn"] = _mm_wgrad(tag + "o_rnn_dw", A["u"], dya)
    dq_all, dkc_c, dvc_c, dsink_c = _attn_bwd(tag + "attn_ctx_bwd", A["qa"], A["kc"], A["vc"], W["sink4"],
                                               A["o_all"], do, S)
    dq_all, dkc_l, dvc_l, dsink_l, dkp, dvp = _run(
        X, _attn_bwd, tag + "attn_lat_bwd", A["qa"], A["kc"], A["vc"], W["sink4"], A["o_all"], do, S,
        band=(A["kp"], A["vp"], bias), prev_dq=dq_all)
    G["sink4"] = dsink_c + dsink_l
    dp = _dqkv_assemble(tag + "dqkv", dq_all, dkp, dvp, dkc_l, dvc_l, dkc_c, dvc_c, cos, sin, S)
    dp, G["cw"], G["cb"], G["w4"], G["b4"], G["lam"] = _run(
        X, _rnn_bwd, tag + "rnn_bwd", A["p"], du, A["rnn"], dp, W["cw"], W["cb"], W["w4"], W["b4"], W["lam"], T)
    proj_dx = (_proj_bwd_fused, tag + "proj_dx", dp, dgl, W["win_t"], A["xa"], dx1, W["g_mix_pre"], W["mod"])
    if X is not None and l == 0:
        G["win_t_a"] = _proj_wgrad(tag + "proj_dw_a", dp, dgl, A["h"][:, :D // 2])
        dxa, dsh1, dsc1, G["g_mix_pre"] = _run(X, *proj_dx)
        G["win_t_b"] = _run(X, _proj_wgrad, tag + "proj_dw_b", dp, dgl, A["h"][:, D // 2:])
    else:
        dxa, dsh1, dsc1, G["g_mix_pre"] = _run(X, *proj_dx)
        G["win_t"] = _proj_wgrad(tag + "proj_dw", dp, dgl, A["h"])
    G["mod"] = jnp.concatenate([dsh1, dsc1, dga1, dsh2, dsc2, dga2], axis=1)
    return dxa, G


def _local_step(xa, target, Ws, S, X=None):
    rope = (*_rope_tables(S), _band_bias(S))
    L = len(Ws)
    h = _normmod_fwd("l0_mix_norm", xa, Ws[0]["g_mix_pre"], Ws[0]["mod"], SH1, SC1)
    saved = []
    x = xa
    for l in range(L):
        nxt = (Ws[l + 1]["g_mix_pre"], Ws[l + 1]["mod"]) if l + 1 < L else None
        A, out = _layer_fwd(l, x, h, Ws[l], rope, S, nxt, X)
        saved.append(A)
        if l + 1 < L:
            x, h = out
    Gs = [None] * L
    dx = None
    for l in reversed(range(L)):
        dx, Gs[l] = _layer_bwd(l, dx, saved[l], Ws[l], rope, S, X, loss_of=(out[0], target) if l == L - 1 else None)
    return Gs[L - 1]["sq"], dx, Gs


MESH = pl.DeviceIdType.MESH


def _place():
    return lax.axis_index("x"), lax.axis_index("y"), lax.axis_index("c")


def _lin(px, py, pc):
    return 4 * px + 2 * py + pc


def _allgather_small(name, blk):
    m, n = blk.shape

    def body(x_ref, out_ref, send_sems, recv_sems, local_sem):
        x, y, c = _place()
        me, sibling = (x, y, c), (x, y, 1 - c)
        chips = [(1 - x, y), (x, 1 - y), (1 - x, 1 - y)]

        def copy(k, block, to, src=None):
            dst = out_ref.at[_lin(*block)]
            return pltpu.make_async_remote_copy(src_ref=dst if src is None else src, dst_ref=dst,
                                                send_sem=send_sems.at[k], recv_sem=recv_sems.at[k],
                                                device_id=to, device_id_type=MESH)

        mine = pltpu.make_async_copy(x_ref, out_ref.at[_lin(*me)], local_sem)
        mine.start()
        first = [copy(0, me, sibling, src=x_ref)]
        first += [copy(1 + j, me, (*chip, c), src=x_ref) for j, chip in enumerate(chips)]
        for cp in first:
            cp.start()
        passed = [copy(4 + j, (*chip, c), sibling) for j, chip in enumerate(chips)]
        for j, chip in enumerate(chips):
            copy(1 + j, (*chip, c), me).wait_recv()
            passed[j].start()
        copy(0, sibling, me).wait_recv()
        for j, chip in enumerate(chips):
            copy(4 + j, (*chip, 1 - c), me).wait_recv()
        for cp in first + passed:
            cp.wait_send()
        mine.wait()

    return pl.pallas_call(
        body, name=name, out_shape=_sds((N_DEV, m, n), blk.dtype),
        in_specs=[pl.BlockSpec(memory_space=pltpu.VMEM)], out_specs=pl.BlockSpec(memory_space=pltpu.VMEM),
        scratch_shapes=[pltpu.SemaphoreType.DMA((7,)), pltpu.SemaphoreType.DMA((7,)), pltpu.SemaphoreType.DMA],
        compiler_params=pltpu.CompilerParams(vmem_limit_bytes=VMEM_LIMIT),
    )(blk)


def _allgather_hbm(name, shards):
    na = len(shards)

    def body(*refs):
        ins, outs = refs[:na], refs[na:2 * na]
        send_sems, recv_sems, local_sems = refs[2 * na:]
        x, y, c = _place()
        me, sibling = (x, y, c), (x, y, 1 - c)
        chips = [(1 - x, y), (x, 1 - y), (1 - x, 1 - y)]

        def copy(a, k, block, to, from_input=False):
            dst = outs[a].at[_lin(*block)]
            return pltpu.make_async_remote_copy(src_ref=ins[a] if from_input else dst, dst_ref=dst,
                                                send_sem=send_sems.at[a, k], recv_sem=recv_sems.at[a, k],
                                                device_id=to, device_id_type=MESH)

        mine = [pltpu.make_async_copy(ins[a], outs[a].at[_lin(*me)], local_sems.at[a]) for a in range(na)]
        for cp in mine:
            cp.start()
        first = []
        for a in range(na):
            first.append(copy(a, 0, me, sibling, True))
            first += [copy(a, 1 + j, me, (*chip, c), True) for j, chip in enumerate(chips)]
        for cp in first:
            cp.start()
        passed = []
        for j, chip in enumerate(chips):
            for a in range(na):
                copy(a, 1 + j, (*chip, c), me).wait_recv()
                fwd = copy(a, 4 + j, (*chip, c), sibling)
                fwd.start()
                passed.append(fwd)
        for a in range(na):
            copy(a, 0, sibling, me).wait_recv()
            for j, chip in enumerate(chips):
                copy(a, 4 + j, (*chip, 1 - c), me).wait_recv()
        for cp in first + passed:
            cp.wait_send()
        for cp in mine:
            cp.wait()

    return pl.pallas_call(
        body, name=name, out_shape=[_sds((N_DEV, *s.shape), s.dtype) for s in shards],
        in_specs=[ANY] * na, out_specs=[ANY] * na,
        scratch_shapes=[pltpu.SemaphoreType.DMA((na, 7)), pltpu.SemaphoreType.DMA((na, 7)),
                        pltpu.SemaphoreType.DMA((na,))],
    )(*shards)


def _exchange_shards(name, grads, L):
    nw = len(grads)
    na = nw * L
    flat = [g for per_layer in grads for g in per_layer]

    def body(*refs):
        ins, outs = refs[:na], refs[na:na + nw]
        send_sems, recv_sems, local_sems = refs[na + nw:]
        x, y, c = _place()
        me = _lin(x, y, c)
        peers = [(x ^ ((k + 1) >> 2 & 1), y ^ ((k + 1) >> 1 & 1), c ^ ((k + 1) & 1)) for k in range(7)]

        def copy(a, k, src_blk, dst_blk):
            return pltpu.make_async_remote_copy(src_ref=ins[a].at[src_blk], dst_ref=outs[a // L].at[a % L, dst_blk],
                                                send_sem=send_sems.at[a, k], recv_sem=recv_sems.at[a, k],
                                                device_id=peers[k], device_id_type=MESH)

        mine = [pltpu.make_async_copy(ins[a].at[me], outs[a // L].at[a % L, me], local_sems.at[a]) for a in range(na)]
        for cp in mine:
            cp.start()
        sent = [copy(a, k, _lin(*peers[k]), me) for a in range(na) for k in range(7)]
        for cp in sent:
            cp.start()
        for a in range(na):
            for k in range(7):
                copy(a, k, me, _lin(*peers[k])).wait_recv()
        for cp in sent:
            cp.wait_send()
        for cp in mine:
            cp.wait()

    return pl.pallas_call(
        body, name=name, out_shape=[_sds((L, *per_layer[0].shape), per_layer[0].dtype) for per_layer in grads],
        in_specs=[ANY] * na, out_specs=[ANY] * nw,
        scratch_shapes=[pltpu.SemaphoreType.DMA((na, 7)), pltpu.SemaphoreType.DMA((na, 7)),
                        pltpu.SemaphoreType.DMA((na,))],
    )(*flat)


MOD_ROWS = 16
MOD_SHARD = 6 * D // N_DEV
HI = lax.Precision.HIGHEST


def _mod_fwd(name, c9, w_mod, b_shard):
    L = w_mod.shape[0]

    def kern(c_ref, w_ref, b_ref, o_ref):
        o_ref[...] = lax.dot_general(_silu(c_ref[...]), w_ref[...], NN, precision=HI,
                                     preferred_element_type=F32) + b_ref[...]

    return pl.pallas_call(
        kern, name=name, grid=(L,),
        in_specs=[_full_spec(c9.shape), pl.BlockSpec((None, D, MOD_SHARD), lambda l: (l, 0, 0)),
                  pl.BlockSpec((None, 1, MOD_SHARD), lambda l: (l, 0, 0))],
        out_specs=pl.BlockSpec((None, MOD_ROWS, MOD_SHARD), lambda l: (l, 0, 0)),
        out_shape=_sds((L, MOD_ROWS, MOD_SHARD), F32), compiler_params=_params(),
    )(c9, w_mod, b_shard)


def _mod_bwd(name, c9, w_mod, dmod_all, dmod_cols):
    L = w_mod.shape[0]

    def rows9(ref, l):
        own = jnp.concatenate([ref[j, 2 * l + 1:2 * l + 2, :] for j in range(N_DEV)], axis=0)
        ctx = ref[0, 2 * l:2 * l + 1, :]
        for j in range(1, N_DEV):
            ctx = ctx + ref[j, 2 * l:2 * l + 1, :]
        return own, ctx

    def kern(c_ref, w_ref, all_ref, cols_ref, gw_ref, gb_ref, gc_ref):
        l = pl.program_id(0)
        for ll in range(L):
            @pl.when(l == ll)
            def _():
                own, ctx = rows9(all_ref, ll)
                gb_ref[...] = _colsum(own) + ctx
                own_s, ctx_s = rows9(cols_ref, ll)
                r16 = jnp.concatenate([own_s, ctx_s, jnp.zeros((MOD_ROWS - N_DEV - 1, MOD_SHARD), F32)], axis=0)
                gw_ref[...] = lax.dot_general(_silu(c_ref[...]), r16, TN, precision=HI, preferred_element_type=F32)
                part = lax.dot_general(r16, w_ref[...], NT, precision=HI,
                                       preferred_element_type=F32)[N_DEV:N_DEV + 1, :]
                if ll == 0:
                    gc_ref[...] = part
                else:
                    gc_ref[...] += part

    return pl.pallas_call(
        kern, name=name, grid=(L,),
        in_specs=[_full_spec(c9.shape), pl.BlockSpec((None, D, MOD_SHARD), lambda l: (l, 0, 0)),
                  _full_spec(dmod_all.shape), _full_spec(dmod_cols.shape)],
        out_specs=[pl.BlockSpec((None, D, MOD_SHARD), lambda l: (l, 0, 0)),
                   pl.BlockSpec((None, 1, 6 * D), lambda l: (l, 0, 0)), _full_spec((1, D))],
        out_shape=[_sds((L, D, MOD_SHARD), F32), _sds((L, 1, 6 * D), F32), _sds((1, D), F32)],
        compiler_params=_params(),
    )(c9, w_mod, dmod_all, dmod_cols)


_BC1 = 1.0 - ADAM_B1 ** ADAM_STEP
_BC2 = 1.0 - ADAM_B2 ** ADAM_STEP


def _adamw_vals(w, g, m, v):
    m = ADAM_B1 * m + (1.0 - ADAM_B1) * g
    v = ADAM_B2 * v + (1.0 - ADAM_B2) * (g * g)
    delta = -ADAM_LR * ((m / _BC1) / (jnp.sqrt(v / _BC2) + ADAM_EPS) + ADAM_WD * w)
    return delta, m, v


def _adamw(name, w, g, m, v, tile):
    R, C = w.shape
    blk = ((tile, C), lambda i: (i, 0))

    def body(i, ins, ps, outs, acc):
        d, mm, vv = _adamw_vals(ins[0][...], ins[1][...], ins[2][...], ins[3][...])
        outs[0][...] = d
        outs[1][...] = mm
        outs[2][...] = vv

    return _ew(name, body, R // tile, [(a, *blk) for a in (w, g, m, v)], [], [(_sds((R, C), F32), *blk)] * 3)


def _sum_slots(ref):
    g = ref[0].astype(F32)
    for j in range(1, N_DEV):
        g = g + ref[j].astype(F32)
    return g


def _adamw_slots(name, slots, w, m, v, tile):
    L, R, C = w.shape
    n = R // tile
    spec = pl.BlockSpec((None, tile, C), lambda l, i: (l, i, 0))
    pieces = [s if isinstance(s, (list, tuple)) else [s] for s in slots]
    layer_of = [ll for ll, ps in enumerate(pieces) for _ in ps]
    flat = [p for ps in pieces for p in ps]

    def slot_spec(ll, cols):
        return pl.BlockSpec((N_DEV, tile, cols),
                            lambda l, i: (0, jnp.where(l == ll, i, jnp.where(l < ll, 0, n - 1)), 0))

    def kern(*refs):
        s_refs = refs[:len(flat)]
        w_ref, m_ref, v_ref, g_ref, d_ref, mo_ref, vo_ref = refs[len(flat):]
        l = pl.program_id(0)
        for ll in range(L):
            @pl.when(l == ll)
            def _():
                parts = [_sum_slots(r) for r, lr in zip(s_refs, layer_of) if lr == ll]
                g = parts[0] if len(parts) == 1 else jnp.concatenate(parts, axis=1)
                g_ref[...] = g
                d_ref[...], mo_ref[...], vo_ref[...] = _adamw_vals(w_ref[...], g, m_ref[...], v_ref[...])

    return pl.pallas_call(
        kern, name=name, grid=(L, n),
        in_specs=[slot_spec(ll, p.shape[-1]) for ll, p in zip(layer_of, flat)] + [spec, spec, spec],
        out_specs=[spec] * 4, out_shape=[_sds((L, R, C), F32)] * 4,
        compiler_params=_params(("arbitrary", "arbitrary")),
    )(*flat, w, m, v)


def _sum_blocks(name, blocks):
    _, R, C = blocks.shape

    def kern(b_ref, o_ref):
        o_ref[...] = _sum_slots(b_ref)

    return pl.pallas_call(kern, name=name, in_specs=[_full_spec(blocks.shape)], out_specs=_full_spec((R, C)),
                          grid=(1,), out_shape=_sds((R, C), F32), compiler_params=_params())(blocks)


BIG = ("win_t", "wo_rnn", "wo_attn", "wout", "wffn_in_t", "wffn_out")
BIG_SRC = ("w_in", "w_o_rnn", "w_o_attn", "w_out", "w_ffn_in", "w_ffn_out")
BIG_T = (True, False, False, False, True, False)
BIG_TILE = (176, 128, 128, 128, 176, 176)


def _chan_full(g8):
    return jnp.transpose(g8, (1, 0, 2)).reshape(g8.shape[1], D)


def kernel(x, c, ctx, c_ctx, w_mod, b_mod, g_mix_pre, g_mix_post, g_ffn_pre, g_ffn_post, w_in, conv_w, conv_b, lru_wa, lru_ba, lru_wx, lru_bx, lru_lam, attn_sink, w_o_rnn, w_o_attn, w_out, w_ffn_in, w_ffn_out, loss_target, m_c_ctx, m_w_mod, m_b_mod, m_g_mix_pre, m_g_mix_post, m_g_ffn_pre, m_g_ffn_post, m_w_in, m_conv_w, m_conv_b, m_lru_wa, m_lru_ba, m_lru_wx, m_lru_bx, m_lru_lam, m_attn_sink, m_w_o_rnn, m_w_o_attn, m_w_out, m_w_ffn_in, m_w_ffn_out, v_c_ctx, v_w_mod, v_b_mod, v_g_mix_pre, v_g_mix_post, v_g_ffn_pre, v_g_ffn_post, v_w_in, v_conv_w, v_conv_b, v_lru_wa, v_lru_ba, v_lru_wx, v_lru_bx, v_lru_lam, v_attn_sink, v_w_o_rnn, v_w_o_attn, v_w_out, v_w_ffn_in, v_w_ffn_out):
    P = dict(c_ctx=c_ctx, w_mod=w_mod, b_mod=b_mod, g_mix_pre=g_mix_pre, g_mix_post=g_mix_post, g_ffn_pre=g_ffn_pre,
             g_ffn_post=g_ffn_post, w_in=w_in, conv_w=conv_w, conv_b=conv_b, lru_wa=lru_wa, lru_ba=lru_ba,
             lru_wx=lru_wx, lru_bx=lru_bx, lru_lam=lru_lam, attn_sink=attn_sink, w_o_rnn=w_o_rnn, w_o_attn=w_o_attn,
             w_out=w_out, w_ffn_in=w_ffn_in, w_ffn_out=w_ffn_out)
    Mo = dict(c_ctx=m_c_ctx, w_mod=m_w_mod, b_mod=m_b_mod, g_mix_pre=m_g_mix_pre, g_mix_post=m_g_mix_post,
              g_ffn_pre=m_g_ffn_pre, g_ffn_post=m_g_ffn_post, w_in=m_w_in, conv_w=m_conv_w, conv_b=m_conv_b,
              lru_wa=m_lru_wa, lru_ba=m_lru_ba, lru_wx=m_lru_wx, lru_bx=m_lru_bx, lru_lam=m_lru_lam,
              attn_sink=m_attn_sink, w_o_rnn=m_w_o_rnn, w_o_attn=m_w_o_attn, w_out=m_w_out, w_ffn_in=m_w_ffn_in,
              w_ffn_out=m_w_ffn_out)
    Vo = dict(c_ctx=v_c_ctx, w_mod=v_w_mod, b_mod=v_b_mod, g_mix_pre=v_g_mix_pre, g_mix_post=v_g_mix_post,
              g_ffn_pre=v_g_ffn_pre, g_ffn_post=v_g_ffn_post, w_in=v_w_in, conv_w=v_conv_w, conv_b=v_conv_b,
              lru_wa=v_lru_wa, lru_ba=v_lru_ba, lru_wx=v_lru_wx, lru_bx=v_lru_bx, lru_lam=v_lru_lam,
              attn_sink=v_attn_sink, w_o_rnn=v_w_o_rnn, w_o_attn=v_w_o_attn, w_out=v_w_out, w_ffn_in=v_w_ffn_in,
              w_ffn_out=v_w_ffn_out)
    L = w_in.shape[0]
    S = x.shape[1]
    me = _lin(*_place())

    small = jnp.concatenate([c.reshape(8, 128), conv_w.reshape(L * CONV_W, 128), lru_ba.reshape(2 * L, 128),
                             lru_bx.reshape(2 * L, 128), lru_lam.reshape(2 * L, 128), jnp.zeros((4, 128), F32)], axis=0)
    small_all = _allgather_small("ag_small", small)
    c_all = small_all[:, 0:8].reshape(N_DEV, D)
    conv_w_f = _chan_full(small_all[:, 8:16]).reshape(L, CONV_W, D)
    lru_ba_f = _chan_full(small_all[:, 16:20]).reshape(L, 2, D)
    lru_bx_f = _chan_full(small_all[:, 20:24]).reshape(L, 2, D)
    lru_lam_f = _chan_full(small_all[:, 24:28]).reshape(L, 2, D)

    c9 = jnp.concatenate([c_all, c_ctx[None], jnp.zeros((MOD_ROWS - N_DEV - 1, D), F32)], axis=0)
    b_shard = lax.dynamic_slice_in_dim(b_mod, me * MOD_SHARD, MOD_SHARD, axis=1)[:, None, :]
    mod_part = _mod_fwd("mod_fwd", c9, w_mod, b_shard)
    mod_all = _allgather_small("ag_mod", mod_part.reshape(L * MOD_ROWS, MOD_SHARD))
    mod_all = jnp.transpose(mod_all.reshape(N_DEV, L, MOD_ROWS, MOD_SHARD), (1, 2, 0, 3)).reshape(L, MOD_ROWS, 6 * D)
    own_row = lax.dynamic_index_in_dim(mod_all, me, axis=1, keepdims=False)
    modrows = jnp.stack([mod_all[:, N_DEV], own_row], axis=1)

    shards = [{k: (P[src][l].T if tr else P[src][l]).astype(BF16) for k, src, tr in zip(BIG, BIG_SRC, BIG_T)}
              for l in range(L)]
    win0, = _allgather_hbm("ag_w_in0", [shards[0]["win_t"]])
    Ws = []
    for l in range(L):
        W = {"win_t": win0.reshape(-1, D)} if l == 0 else {}
        W.update(
            cw=conv_w_f[l], cb=conv_b[l][None],
            w4=jnp.concatenate([lru_wa[l, 0], lru_wa[l, 1], lru_wx[l, 0], lru_wx[l, 1]], axis=-1).astype(BF16),
            b4=jnp.concatenate([lru_ba_f[l, 0].reshape(N_RNN_BLOCKS, 1, RB), lru_ba_f[l, 1].reshape(N_RNN_BLOCKS, 1, RB),
                                lru_bx_f[l, 0].reshape(N_RNN_BLOCKS, 1, RB), lru_bx_f[l, 1].reshape(N_RNN_BLOCKS, 1, RB)],
                               axis=-1),
            lam=lru_lam_f[l], sink4=jnp.broadcast_to(attn_sink[l].reshape(N_KV, Q_PER_KV, 1), (N_KV, Q_PER_KV, HEAD)),
            g_mix_pre=g_mix_pre[l][None], g_mix_post=g_mix_post[l][None], g_ffn_pre=g_ffn_pre[l][None],
            g_ffn_post=g_ffn_post[l][None], mod=modrows[l])
        Ws.append(W)

    xa = jnp.concatenate([ctx[0], x[0]], axis=0)
    plan = _Plan(shards, Ws)
    sq, dxa, Gs = _local_step(xa, loss_target[0], Ws, S, plan)
    loss = lax.psum((0.5 / D) * jnp.sum(sq), ("x", "y", "c"))
    grad_x = dxa[CTX:][None]

    dmod = jnp.concatenate([Gs[l]["mod"] for l in range(L)] + [jnp.zeros((8 - 2 * L, 6 * D), F32)], axis=0)
    dmod_all = _allgather_small("ag_dmod", dmod)
    dmod_cols = lax.dynamic_slice_in_dim(dmod_all, me * MOD_SHARD, MOD_SHARD, axis=2)
    g_w_mod, g_b_mod, dsc_part = _mod_bwd("mod_bwd", c9, w_mod, dmod_all, dmod_cols)
    g_b_mod = g_b_mod[:, 0]

    def rows(name, shape):
        return jnp.concatenate([Gs[l][name].reshape(shape) for l in range(L)], axis=0)

    b4g = [Gs[l]["b4"].reshape(N_RNN_BLOCKS, 4, RB) for l in range(L)]
    sink_row = jnp.concatenate([Gs[l]["sink4"][:, :, 0].reshape(1, N_Q) for l in range(L)]
                               + [jnp.zeros((1, D - L * N_Q), F32)], axis=1)
    small_g = jnp.concatenate(
        [rows("g_mix_pre", (1, D)), rows("g_mix_post", (1, D)), rows("g_ffn_pre", (1, D)), rows("g_ffn_post", (1, D)),
         rows("cb", (1, D)), rows("cw", (CONV_W, D))]
        + [b4g[l][:, d].reshape(1, D) for l in range(L) for d in range(2)]
        + [b4g[l][:, 2 + d].reshape(1, D) for l in range(L) for d in range(2)]
        + [rows("lam", (2, D)), sink_row, dsc_part], axis=0)
    n_small = small_g.shape[0]
    small_tot = _sum_blocks("sum_small", _allgather_small("ag_small_grads", small_g))
    o = 0
    G = {}
    for name in ("g_mix_pre", "g_mix_post", "g_ffn_pre", "g_ffn_post", "conv_b"):
        G[name] = small_tot[o:o + L]
        o += L
    G["conv_w"] = small_tot[o:o + L * CONV_W].reshape(L, CONV_W, D)
    o += L * CONV_W
    for name in ("lru_ba", "lru_bx", "lru_lam"):
        G[name] = small_tot[o:o + 2 * L].reshape(L, 2, D)
        o += 2 * L
    G["attn_sink"] = small_tot[o, :L * N_Q].reshape(L, N_Q)
    sg = jax.nn.sigmoid(c_ctx)
    G["c_ctx"] = small_tot[o + 1] * (sg * (1.0 + c_ctx * (1.0 - sg)))
    G["b_mod"] = g_b_mod
    G["w_mod"] = g_w_mod

    last_slots, = _exchange_shards("exchange_w_in0", [[Gs[0]["win_t_b"].reshape(N_DEV, -1, D // 2)]], 1)
    plan.slots[0]["win_t"] = [plan.slots[0]["win_t_a"], last_slots[0]]

    out_g, out_d, out_m, out_v = {}, {}, {}, {}

    def put(name, res, shape=None):
        g, d, m, v = res
        for dst, val in ((out_g, g), (out_d, d), (out_m, m), (out_v, v)):
            dst[name] = val if shape is None else val.reshape(shape)

    for k, src, tr, tile in zip(BIG, BIG_SRC, BIG_T, BIG_TILE):
        lay = (lambda a: jnp.swapaxes(a, 1, 2)) if tr else (lambda a: a)
        res = _adamw_slots("adamw_" + src, [plan.slots[l][k] for l in range(L)], lay(P[src]), lay(Mo[src]),
                           lay(Vo[src]), tile)
        put(src, [lay(r) for r in res])
    res = _adamw("adamw_w_mod", w_mod.reshape(L * D, MOD_SHARD), g_w_mod.reshape(L * D, MOD_SHARD),
                 m_w_mod.reshape(L * D, MOD_SHARD), v_w_mod.reshape(L * D, MOD_SHARD), 256)
    put("w_mod", (g_w_mod,) + tuple(res), w_mod.shape)
    def fuse4(wa, wx):
        return jnp.concatenate([wa[:, 0], wa[:, 1], wx[:, 0], wx[:, 1]], axis=-1).reshape(L, N_RNN_BLOCKS * RB, 4 * RB)

    res = _adamw_slots("adamw_gates", plan.gate_slots,
                       fuse4(lru_wa, lru_wx), fuse4(m_lru_wa, m_lru_wx), fuse4(v_lru_wa, v_lru_wx), 256)
    res = [r.reshape(L, N_RNN_BLOCKS, RB, 4, RB) for r in res]
    put("lru_wa", [jnp.stack([r[:, :, :, 0], r[:, :, :, 1]], axis=1) for r in res])
    put("lru_wx", [jnp.stack([r[:, :, :, 2], r[:, :, :, 3]], axis=1) for r in res])
    rep = ("g_mix_pre", "g_mix_post", "g_ffn_pre", "g_ffn_post", "conv_b", "b_mod")

    def pack_rep(T_):
        sink = jnp.concatenate([T_["attn_sink"].reshape(1, L * N_Q), jnp.zeros((1, D - L * N_Q), F32)], axis=1)
        return jnp.concatenate([T_[n].reshape(-1, D) for n in rep] + [sink, T_["c_ctx"][None]], axis=0)

    pk = [pack_rep(T_) for T_ in (P, G, Mo, Vo)]
    n_rep = pk[0].shape[0]
    res = _adamw("adamw_replicated", *[jnp.pad(a, ((0, 24 - n_rep), (0, 0))) for a in pk], 24)
    res = (pk[1],) + tuple(r[:n_rep] for r in res)
    o = 0
    for n in rep:
        k = P[n].size // D
        put(n, [r[o:o + k] for r in res], P[n].shape)
        o += k
    put("attn_sink", [r[o, :L * N_Q] for r in res], attn_sink.shape)
    put("c_ctx", [r[o + 1] for r in res], c_ctx.shape)
    chan = ("conv_w", "lru_ba", "lru_bx", "lru_lam")
    g_own = {n: lax.dynamic_slice_in_dim(G[n], me * RB, RB, axis=2) for n in chan}

    def pack_chan(T_):
        return jnp.concatenate([T_[n].reshape(-1, RB) for n in chan], axis=0)

    pk = [pack_chan(T_) for T_ in (P, g_own, Mo, Vo)]
    n_ch = pk[0].shape[0]
    res = _adamw("adamw_channels", *[jnp.pad(a, ((0, 24 - n_ch), (0, 0))) for a in pk], 24)
    res = (pk[1],) + tuple(r[:n_ch] for r in res)
    o = 0
    for n in chan:
        k = P[n].size // RB
        put(n, [r[o:o + k] for r in res], P[n].shape)
        o += k

    order = ("c_ctx", "w_mod", "b_mod", "g_mix_pre", "g_mix_post", "g_ffn_pre", "g_ffn_post", "w_in", "conv_w", "conv_b",
             "lru_wa", "lru_ba", "lru_wx", "lru_bx", "lru_lam", "attn_sink", "w_o_rnn", "w_o_attn", "w_out", "w_ffn_in",
             "w_ffn_out")
    return (loss, grad_x, *[out_g[n] for n in order], *[out_d[n] for n in order], *[out_m[n] for n in order],
            *[out_v[n] for n in order])
```

```python
import functools
import math

import numpy as np
import jax
import jax.numpy as jnp
from jax import lax
from jax.experimental import pallas as pl
from jax.experimental.pallas import tpu as pltpu

F32 = jnp.float32
BF16 = jnp.bfloat16

D = 1024
CTX = 256
TR = 256
HEAD = 128
N_Q = 8
N_KV = 2
Q_PER_KV = N_Q // N_KV
GRID_W = 64
N_FREQ = HEAD // 4
ROPE_BASE = 10000.0
N_RNN_BLOCKS = 8
CONV_W = 4
CONV_LEFT = 2
LRU_C = 8.0
D_FF = 2816
IN_W = 5632
P_W = IN_W
DP_W = 3584
COL_XR, COL_GR, COL_Q, COL_K, COL_V, COL_GL = 0, 1024, 2048, 3072, 3328, 3584
GLB = 512
EPS = 1e-6
NEG_INF = -1e30
ATT_SCALE = HEAD ** -0.5
N_DEV = 8
VMEM_LIMIT = 56 * 1024 * 1024

ADAM_LR, ADAM_B1, ADAM_B2, ADAM_EPS, ADAM_WD, ADAM_STEP = 0.001, 0.9, 0.999, 1e-08, 0.01, 10

NN = (((1,), (0,)), ((), ()))
NT = (((1,), (1,)), ((), ()))
TN = (((0,), (0,)), ((), ()))


def _dot(a, b, dims=NN):
    return lax.dot_general(a, b, dims, preferred_element_type=F32)


def _params(sem=("arbitrary",)):
    return pltpu.CompilerParams(dimension_semantics=sem, vmem_limit_bytes=VMEM_LIMIT)


def _full_spec(shape):
    nd = len(shape)
    return pl.BlockSpec(shape, lambda *_: (0,) * nd)


ANY = pl.BlockSpec(memory_space=pl.ANY)


def _ew(name, body, n, row_ins, pars, row_outs, accs=(), alias=None):
    n_ri, n_p, n_ro, n_acc = len(row_ins), len(pars), len(row_outs), len(accs)

    def kern(*refs):
        i = pl.program_id(0)
        ins = refs[:n_ri]
        ps = refs[n_ri:n_ri + n_p]
        outs = refs[n_ri + n_p:n_ri + n_p + n_ro]
        acc = refs[n_ri + n_p + n_ro:]
        if n_acc:
            @pl.when(i == 0)
            def _():
                for a in acc:
                    a[...] = jnp.zeros(a.shape, a.dtype)
        body(i, ins, ps, outs, acc)

    in_specs = [ANY if blk is None else pl.BlockSpec(blk, imap) for (_, blk, imap) in row_ins]
    in_specs += [_full_spec(p.shape) for p in pars]
    out_specs = [pl.BlockSpec(blk, imap) for (_, blk, imap) in row_outs] + [_full_spec(a.shape) for a in accs]
    out_shape = [s for (s, _, _) in row_outs] + list(accs)
    return pl.pallas_call(
        kern, name=name, grid=(n,), in_specs=in_specs, out_specs=out_specs, out_shape=out_shape,
        input_output_aliases=alias or {}, compiler_params=_params(),
    )(*[a for (a, _, _) in row_ins], *pars)


def _rowblk(width, colblk=0, roff=0, tile=TR):
    return (tile, width), (lambda i: (i + roff, colblk))


def _sds(shape, dtype):
    return jax.ShapeDtypeStruct(shape, dtype)


class _Carry:
    SAME_CORE = (1, 3, 5)

    def __init__(self, jobs):
        self.jobs = list(jobs)
        self.arrays = [a for _, a in self.jobs]
        self.out_shapes = [_sds(a.shape if kind == "scatter" else (N_DEV, *a.shape), a.dtype) for kind, a in self.jobs]
        n = len(self.jobs)
        self.scratch = [pltpu.SemaphoreType.DMA((n, 7)), pltpu.SemaphoreType.DMA((n, 7)), pltpu.SemaphoreType.DMA((n,))]

    def _setup(self, sems):
        send_sems, recv_sems, local_sems = sems
        x, y, c = _place()
        me = _lin(x, y, c)
        peers = [(x ^ ((k + 1) >> 2 & 1), y ^ ((k + 1) >> 1 & 1), c ^ ((k + 1) & 1)) for k in range(7)]

        def copy(a, k, sem_k, src, dst):
            return pltpu.make_async_remote_copy(src_ref=src, dst_ref=dst, send_sem=send_sems.at[a, sem_k],
                                                recv_sem=recv_sems.at[a, sem_k], device_id=peers[k], device_id_type=MESH)

        return me, [_lin(*p) for p in peers], copy, local_sems

    def _local(self, a, kind, ins, outs, me, local_sems):
        return pltpu.make_async_copy(ins[a].at[me] if kind == "scatter" else ins[a], outs[a].at[me], local_sems.at[a])

    def start(self, ins, outs, sems):
        me, theirs, copy, local_sems = self._setup(sems)
        for a, (kind, _) in enumerate(self.jobs):
            self._local(a, kind, ins, outs, me, local_sems).start()
            if kind == "scatter":
                for k in range(7):
                    copy(a, k, k, ins[a].at[theirs[k]], outs[a].at[me]).start()
            else:
                for k in (0,) + self.SAME_CORE:
                    copy(a, k, k, ins[a], outs[a].at[me]).start()

    def wait(self, ins, outs, sems):
        me, theirs, copy, local_sems = self._setup(sems)
        for a, (kind, _) in enumerate(self.jobs):
            if kind == "scatter":
                for k in range(7):
                    copy(a, k, k, ins[a].at[me], outs[a].at[theirs[k]]).wait_recv()
                for k in range(7):
                    copy(a, k, k, ins[a].at[theirs[k]], outs[a].at[me]).wait_send()
            else:
                for k in self.SAME_CORE:
                    blk = outs[a].at[theirs[k]]
                    copy(a, k, k, ins[a], blk).wait_recv()
                    copy(a, 0, k + 1, blk, blk).start()
                copy(a, 0, 0, ins[a], outs[a].at[theirs[0]]).wait_recv()
                for k in self.SAME_CORE:
                    copy(a, 0, k + 1, ins[a], outs[a].at[theirs[k + 1]]).wait_recv()
                for k in (0,) + self.SAME_CORE:
                    copy(a, k, k, ins[a], outs[a].at[me]).wait_send()
                for k in self.SAME_CORE:
                    blk = outs[a].at[theirs[k]]
                    copy(a, 0, k + 1, blk, blk).wait_send()
            self._local(a, kind, ins, outs, me, local_sems).wait()


def _carried(kern, carry, n_in, n_out, first, last):
    if carry is None:
        return kern
    nc = len(carry.jobs)

    def wrapped(*refs):
        ins, cin = refs[:n_in], refs[n_in:n_in + nc]
        outs, cout = refs[n_in + nc:n_in + nc + n_out], refs[n_in + nc + n_out:n_in + 2 * nc + n_out]
        scr, sems = refs[n_in + 2 * nc + n_out:-3], refs[-3:]

        @pl.when(first())
        def _():
            carry.start(cin, cout, sems)

        kern(*ins, *outs, *scr)

        @pl.when(last())
        def _():
            carry.wait(cin, cout, sems)

    return wrapped


def _carry_args(carry):
    if carry is None:
        return [], [], [], [], []
    n = len(carry.jobs)
    return [ANY] * n, carry.arrays, [ANY] * n, carry.out_shapes, carry.scratch


def _grid_ends(dims):
    first = lambda: functools.reduce(jnp.logical_and, [pl.program_id(d) == 0 for d in range(len(dims))])
    last = lambda: functools.reduce(jnp.logical_and, [pl.program_id(d) == n - 1 for d, n in enumerate(dims)])
    return first, last


def _mm_call(name, a, b, mode, out_dtype, tm, tn, rows_outer=True, single_b=False, carry=None):
    if mode == "TN":
        (K, M), N = a.shape, b.shape[1]
    else:
        (M, K), N = a.shape, (b.shape[1] if mode == "NN" else b.shape[0])
    assert M % tm == 0 and N % tn == 0, (name, M, N, K, tm, tn)
    ij = (lambda g0, g1: (g0, g1)) if rows_outer else (lambda g0, g1: (g1, g0))
    grid = (M // tm, N // tn) if rows_outer else (N // tn, M // tm)
    if mode == "TN":
        a_spec = pl.BlockSpec((K, tm), lambda g0, g1: (0, ij(g0, g1)[0]))
    else:
        a_spec = pl.BlockSpec((tm, K), lambda g0, g1: (ij(g0, g1)[0], 0))
    b_blk, b_map = ((tn, K), lambda g0, g1: (ij(g0, g1)[1], 0)) if mode == "NT" else \
                   ((K, tn), lambda g0, g1: (0, ij(g0, g1)[1]))
    b_spec = pl.BlockSpec(b_blk, b_map, pipeline_mode=pl.Buffered(1)) if single_b else pl.BlockSpec(b_blk, b_map)
    dims = {"NN": NN, "NT": NT, "TN": TN}[mode]

    def kern(a_ref, b_ref, o_ref):
        o_ref[...] = _dot(a_ref[...], b_ref[...], dims).astype(o_ref.dtype)

    ci, ca, co, cs, cscr = _carry_args(carry)
    res = pl.pallas_call(
        _carried(kern, carry, 2, 1, *_grid_ends(grid)), name=name, grid=grid, in_specs=[a_spec, b_spec] + ci,
        out_specs=[pl.BlockSpec((tm, tn), lambda g0, g1: ij(g0, g1))] + co,
        out_shape=[_sds((M, N), out_dtype)] + cs, scratch_shapes=cscr,
        compiler_params=_params(("arbitrary", "arbitrary")),
    )(a, b, *ca)
    return res[0] if carry is None else (res[0], res[1:])


def _mm_act(name, a, w, mode, out_dtype=BF16, carry=None):
    rows, K = a.shape
    N = w.shape[1] if mode == "NN" else w.shape[0]
    if K > D_FF:
        return _mm_call(name, a, w, mode, out_dtype, rows // 8, N, single_b=True, carry=carry)
    tn = N if N <= 1024 else 1408
    return _mm_call(name, a, w, mode, out_dtype, rows // 4, tn, carry=carry)


def _mm_wgrad(name, x, dy, out_dtype=BF16, carry=None):
    M = x.shape[1]
    tm = 1408 if M == D_FF else 512
    return _mm_call(name, x, dy, "TN", out_dtype, tm, dy.shape[1], single_b=True, carry=carry)


def _sigmoid(x):
    return 0.5 * jnp.tanh(0.5 * x) + 0.5


def _silu(x):
    return x * _sigmoid(x)


def _silu_grad(x):
    s = _sigmoid(x)
    return s * (1.0 + x * (1.0 - s))


_GELU_K = math.sqrt(2.0 / math.pi)


def _gelu(x):
    return 0.5 * x * (1.0 + jnp.tanh(_GELU_K * (x + 0.044715 * x * x * x)))


def _gelu_grad(x):
    t = jnp.tanh(_GELU_K * (x + 0.044715 * x * x * x))
    return 0.5 * (1.0 + t) + 0.5 * x * (1.0 - t * t) * _GELU_K * (1.0 + 3.0 * 0.044715 * x * x)


def _log_sigmoid(x):
    return jnp.minimum(x, 0.0) - jnp.log(1.0 + jnp.exp(-jnp.abs(x)))


def _rms(x):
    x = x.astype(F32)
    r = lax.rsqrt(jnp.mean(x * x, axis=-1, keepdims=True) + EPS)
    return x * r, r


def _rms_bwd(dy, y, r):
    return r * (dy - y * jnp.mean(dy * y, axis=-1, keepdims=True))


def _modrow(mod_ref, i, chunk):
    lo = mod_ref[0:1, chunk * D:(chunk + 1) * D]
    hi = mod_ref[1:2, chunk * D:(chunk + 1) * D]
    return jnp.where(i == 0, lo, hi)


def _acc_seg(acc_ref, i, val):
    zero = jnp.zeros_like(val)
    acc_ref[0:1, :] += jnp.where(i == 0, val, zero)
    acc_ref[1:2, :] += jnp.where(i == 0, zero, val)


def _colsum(x):
    return jnp.sum(x, axis=0, keepdims=True)


SH1, SC1, GA1, SH2, SC2, GA2 = range(6)


def _normmod_fwd(name, xa, g, mod, c_sh, c_sc):
    T = xa.shape[0]

    def body(i, ins, ps, outs, acc):
        y, _ = _rms(ins[0][...])
        h = (y * ps[0][...]) * (1.0 + _modrow(ps[1], i, c_sc)) + _modrow(ps[1], i, c_sh)
        outs[0][...] = h.astype(BF16)

    return _ew(name, body, T // TR, [(xa, *_rowblk(D))], [g, mod], [(_sds((T, D), BF16), *_rowblk(D))])[0]


def _modrows(mod_ref, row0, n, chunk):
    t = row0 + lax.broadcasted_iota(jnp.int32, (n, 1), 0)
    return jnp.where(t < CTX, mod_ref[0:1, chunk * D:(chunk + 1) * D], mod_ref[1:2, chunk * D:(chunk + 1) * D])


def _loss_resid_bwd(name, x_out, target, mat, gpost, mod, c_ga):
    T = x_out.shape[0]

    def body(i, ins, ps, outs, acc):
        err = ins[0][...] - ins[1][...]
        lat = i > 0
        dx = jnp.where(lat, err * (1.0 / D), 0.0)
        outs[0][...] = dx
        acc[2][...] += jnp.where(lat, _colsum(err * err), 0.0)
        outs[1][...] = _resid_bwd_vals(i, dx, ins[2][...], ps[0][...], ps[1], c_ga, acc[0], acc[1]).astype(BF16)

    tgt_blk = ((TR, D), lambda i: (jnp.maximum(i - 1, 0), 0))
    return _ew(name, body, T // TR, [(x_out, *_rowblk(D)), (target, *tgt_blk), (mat, *_rowblk(D))], [gpost, mod],
               [(_sds((T, D), F32), *_rowblk(D)), (_sds((T, D), BF16), *_rowblk(D))],
               [_sds((2, D), F32), _sds((1, D), F32), _sds((1, D), F32)])


def _mod_for(mod_ref, i, chunk, row0, n):
    return _modrow(mod_ref, i, chunk) if row0 is None else _modrows(mod_ref, row0, n, chunk)


def _acc_for(acc_ref, i, v, row0):
    if row0 is None:
        _acc_seg(acc_ref, i, _colsum(v))
        return

    @pl.when(row0 < CTX)
    def _():
        is_ctx = row0 + lax.broadcasted_iota(jnp.int32, (v.shape[0], 1), 0) < CTX
        acc_ref[0:1, :] += _colsum(jnp.where(is_ctx, v, 0.0))
        acc_ref[1:2, :] += _colsum(jnp.where(is_ctx, 0.0, v))

    @pl.when(row0 >= CTX)
    def _():
        acc_ref[1:2, :] += _colsum(v)


def _resid_bwd_vals(i, dout, mat, gpost, mod_ref, c_ga, acc_ga, acc_g, row0=None):
    ym, rm = _rms(mat)
    ga = _mod_for(mod_ref, i, c_ga, row0, dout.shape[0])
    _acc_for(acc_ga, i, dout * (ym * gpost), row0)
    dn = dout * ga
    acc_g[...] += _colsum(dn * ym)
    return _rms_bwd(dn * gpost, ym, rm)


def _normmod_bwd_vals(i, dh, xin, g, mod_ref, c_sh, c_sc, acc_sh, acc_sc, acc_g, row0=None):
    dh = dh.astype(F32)
    y, r = _rms(xin)
    _acc_for(acc_sc, i, dh * (y * g), row0)
    _acc_for(acc_sh, i, dh, row0)
    dyg = dh * (1.0 + _mod_for(mod_ref, i, c_sc, row0, dh.shape[0]))
    acc_g[...] += _colsum(dyg * y)
    return _rms_bwd(dyg * g, y, r)


def _parts(i, tm):
    return [(slice(0, tm), i * tm)]


FT = 1408


def _ffn_in_fused(name, h2, w_t, carry=None):
    T = h2.shape[0]
    tm, nj = T // 4, D_FF // FT

    def kern(a_ref, bg_ref, bu_ref, fg_ref, fu_ref, s_ref):
        for rows, _ in _parts(0, tm):
            a = a_ref[rows, :]
            g = _dot(a, bg_ref[...], NT)
            u = _dot(a, bu_ref[...], NT)
            fg_ref[rows, :] = g.astype(BF16)
            fu_ref[rows, :] = u.astype(BF16)
            s_ref[rows, :] = (_silu(g) * u).astype(BF16)

    o_spec = pl.BlockSpec((tm, FT), lambda i, j: (i, j))
    ci, ca, co, cs, cscr = _carry_args(carry)
    res = pl.pallas_call(
        _carried(kern, carry, 3, 3, *_grid_ends((4, nj))), name=name, grid=(4, nj),
        in_specs=[pl.BlockSpec((tm, D), lambda i, j: (i, 0)), pl.BlockSpec((FT, D), lambda i, j: (j, 0)),
                  pl.BlockSpec((FT, D), lambda i, j: (j + nj, 0))] + ci,
        out_specs=[o_spec] * 3 + co, out_shape=[_sds((T, D_FF), BF16)] * 3 + cs, scratch_shapes=cscr,
        compiler_params=_params(("arbitrary", "arbitrary")),
    )(h2, w_t, w_t, *ca)
    return res if carry is None else (res[:3], res[3:])


def _norm_chain(row0, xin, mat, gpost, mod_ref, c_ga, gnext, modn_ref, c_sh, c_sc):
    n = xin.shape[0]
    ym, _ = _rms(mat.astype(BF16))
    xo = xin + _modrows(mod_ref, row0, n, c_ga) * (ym * gpost)
    y, _ = _rms(xo)
    h = (y * gnext) * (1.0 + _modrows(modn_ref, row0, n, c_sc)) + _modrows(modn_ref, row0, n, c_sh)
    return xo, h.astype(BF16)


def _out_fused(name, p, u, o_all, xa, w_o_rnn, w_o_attn, w_out, gpost, mod, gnext):
    T = u.shape[0]
    tm = T // 8

    def kern(g0, g1, g2, g3, u_ref, o_ref, xa_ref, wr_ref, wa_ref, w_ref, gpost_ref, mod_ref, gnext_ref,
             ya_ref, yb_ref, z_ref, m_ref, x1_ref, h2_ref):
        for rows, row0 in _parts(pl.program_id(0), tm):
            ya = _dot(u_ref[rows, :], wr_ref[...]).astype(BF16)
            yb = _dot(o_ref[rows, :], wa_ref[...]).astype(BF16)
            ya_ref[rows, :] = ya
            yb_ref[rows, :] = yb
            ga = _sigmoid(jnp.concatenate([g0[rows, :], g1[rows, :]], axis=1).astype(F32))
            gb = _sigmoid(jnp.concatenate([g2[rows, :], g3[rows, :]], axis=1).astype(F32))
            z = (ga * ya.astype(F32) + gb * yb.astype(F32)).astype(BF16)
            z_ref[rows, :] = z
            m = _dot(z, w_ref[...])
            m_ref[rows, :] = m.astype(BF16)
            x1_ref[rows, :], h2_ref[rows, :] = _norm_chain(row0, xa_ref[rows, :], m, gpost_ref[...], mod_ref, GA1,
                                                           gnext_ref[...], mod_ref, SH2, SC2)

    row = lambda w: pl.BlockSpec((tm, w), lambda i: (i, 0))
    return pl.pallas_call(
        kern, name=name, grid=(T // tm,),
        in_specs=[pl.BlockSpec((tm, GLB), lambda i, q=q: (i, COL_GL // GLB + q)) for q in range(4)]
                 + [row(D), row(D), row(D)] + [_full_spec(a.shape) for a in (w_o_rnn, w_o_attn, w_out, gpost, mod, gnext)],
        out_specs=[row(D)] * 6,
        out_shape=[_sds((T, D), BF16)] * 4 + [_sds((T, D), F32), _sds((T, D), BF16)],
        compiler_params=_params(),
    )(p, p, p, p, u, o_all, xa, w_o_rnn, w_o_attn, w_out, gpost, mod, gnext)


def _ffn_out_fused(name, s, w, x1, gpost, mod, nxt=None):
    T = s.shape[0]
    tm = T // 8

    def kern(s_ref, w_ref, x1_ref, gpost_ref, mod_ref, *rest):
        for rows, row0 in _parts(pl.program_id(0), tm):
            e = _dot(s_ref[rows, :], w_ref[...])
            if nxt is None:
                e_ref, xo_ref = rest
                ym, _ = _rms(e.astype(BF16))
                xo_ref[rows, :] = x1_ref[rows, :] + _modrows(mod_ref, row0, e.shape[0], GA2) * (ym * gpost_ref[...])
            else:
                gnext_ref, modn_ref, e_ref, xo_ref, h_ref = rest
                xo_ref[rows, :], h_ref[rows, :] = _norm_chain(row0, x1_ref[rows, :], e, gpost_ref[...], mod_ref, GA2,
                                                              gnext_ref[...], modn_ref, SH1, SC1)
            e_ref[rows, :] = e.astype(BF16)

    row = lambda w_: pl.BlockSpec((tm, w_), lambda i: (i, 0))
    extra = [] if nxt is None else list(nxt)
    return pl.pallas_call(
        kern, name=name, grid=(T // tm,),
        in_specs=[row(D_FF), _full_spec(w.shape), row(D), _full_spec(gpost.shape), _full_spec(mod.shape)]
                 + [_full_spec(a.shape) for a in extra],
        out_specs=[row(D)] * (2 if nxt is None else 3),
        out_shape=[_sds((T, D), BF16), _sds((T, D), F32)] + ([] if nxt is None else [_sds((T, D), BF16)]),
        compiler_params=_params(),
    )(s, w, x1, gpost, mod, *extra)


def _ffn_bwd_fused(name, fg, fu, w, de=None, head=None):
    T = fg.shape[0]
    tm = T // 8
    row = lambda w_: pl.BlockSpec((tm, w_), lambda i: (i, 0))
    w_spec = pl.BlockSpec(w.shape, lambda i: (0, 0), pipeline_mode=pl.Buffered(1))

    def tail(rows, de_v, fg_ref, fu_ref, w_ref, df_ref):
        ds = _dot(de_v, w_ref[...], NT)
        g, u = fg_ref[rows, :].astype(F32), fu_ref[rows, :].astype(F32)
        df_ref[rows, :] = jnp.concatenate([ds * u * _silu_grad(g), ds * _silu(g)], axis=1).astype(BF16)

    if head is None:
        def kern(de_ref, fg_ref, fu_ref, w_ref, df_ref):
            for rows, _ in _parts(pl.program_id(0), tm):
                tail(rows, de_ref[rows, :], fg_ref, fu_ref, w_ref, df_ref)

        return pl.pallas_call(
            kern, name=name, grid=(T // tm,), in_specs=[row(D), row(D_FF), row(D_FF), w_spec],
            out_specs=[row(2 * D_FF)], out_shape=[_sds((T, 2 * D_FF), BF16)], compiler_params=_params(),
        )(de, fg, fu, w)

    dx2, e, gpost, mod = head

    def kern(dx_ref, e_ref, fg_ref, fu_ref, w_ref, gpost_ref, mod_ref, de_ref, df_ref, dga_ref, dg_ref):
        i = pl.program_id(0)

        @pl.when(i == 0)
        def _():
            dga_ref[...] = jnp.zeros(dga_ref.shape, F32)
            dg_ref[...] = jnp.zeros(dg_ref.shape, F32)

        for rows, row0 in _parts(i, tm):
            de_v = _resid_bwd_vals(i, dx_ref[rows, :], e_ref[rows, :], gpost_ref[...], mod_ref, GA2, dga_ref, dg_ref,
                                   row0=row0).astype(BF16)
            de_ref[rows, :] = de_v
            tail(rows, de_v, fg_ref, fu_ref, w_ref, df_ref)

    return pl.pallas_call(
        kern, name=name, grid=(T // tm,),
        in_specs=[row(D), row(D), row(D_FF), row(D_FF), w_spec, _full_spec(gpost.shape), _full_spec(mod.shape)],
        out_specs=[row(D), row(2 * D_FF), _full_spec((2, D)), _full_spec((1, D))],
        out_shape=[_sds((T, D), BF16), _sds((T, 2 * D_FF), BF16), _sds((2, D), F32), _sds((1, D), F32)],
        compiler_params=_params(),
    )(dx2, e, fg, fu, w, gpost, mod)


def _zero_at_start(i, refs):
    @pl.when(i == 0)
    def _():
        for r in refs:
            r[...] = jnp.zeros(r.shape, F32)


def _proj_bwd_fused(name, dp, dgl, w_in_t, xa, dx1, gpre, mod, carry=None):
    T = dp.shape[0]
    tm = T // 8
    row = lambda w_: pl.BlockSpec((tm, w_), lambda i: (i, 0))

    def kern(dp_ref, dgl_ref, w_ref, xa_ref, dx1_ref, g_ref, mod_ref, dxa_ref, dsh_ref, dsc_ref, dg_ref):
        i = pl.program_id(0)
        _zero_at_start(i, (dsh_ref, dsc_ref, dg_ref))
        for rows, row0 in _parts(i, tm):
            dh = _dot(dp_ref[rows, :], w_ref[0:DP_W, :]) + _dot(dgl_ref[rows, :], w_ref[DP_W:, :])
            dxa_ref[rows, :] = dx1_ref[rows, :] + _normmod_bwd_vals(i, dh, xa_ref[rows, :], g_ref[...], mod_ref, SH1,
                                                                    SC1, dsh_ref, dsc_ref, dg_ref, row0=row0)

    ci, ca, co, cs, cscr = _carry_args(carry)
    res = pl.pallas_call(
        _carried(kern, carry, 7, 4, *_grid_ends((T // tm,))), name=name, grid=(T // tm,),
        in_specs=[row(DP_W), row(P_W - DP_W),
                  pl.BlockSpec(w_in_t.shape, lambda i: (0, 0), pipeline_mode=pl.Buffered(1)), row(D), row(D),
                  _full_spec(gpre.shape), _full_spec(mod.shape)] + ci,
        out_specs=[row(D), _full_spec((2, D)), _full_spec((2, D)), _full_spec((1, D))] + co,
        out_shape=[_sds((T, D), F32), _sds((2, D), F32), _sds((2, D), F32), _sds((1, D), F32)] + cs,
        scratch_shapes=cscr, compiler_params=_params(),
    )(dp, dgl, w_in_t, xa, dx1, gpre, mod, *ca)
    return res if carry is None else (res[:4], res[4:])


def _proj_wgrad(name, dp, dgl, h, carry=None):
    T, N = h.shape
    n1, n2 = DP_W // GLB, (P_W - DP_W) // GLB

    def kern(a1_ref, a2_ref, h_ref, o_ref):
        i = pl.program_id(0)

        @pl.when(i < n1)
        def _():
            o_ref[...] = _dot(a1_ref[...], h_ref[...], TN).astype(o_ref.dtype)

        @pl.when(i >= n1)
        def _():
            o_ref[...] = _dot(a2_ref[...], h_ref[...], TN).astype(o_ref.dtype)

    ci, ca, co, cs, cscr = _carry_args(carry)
    res = pl.pallas_call(
        _carried(kern, carry, 3, 1, *_grid_ends((n1 + n2,))), name=name, grid=(n1 + n2,),
        in_specs=[pl.BlockSpec((T, GLB), lambda i: (0, jnp.minimum(i, n1 - 1))),
                  pl.BlockSpec((T, GLB), lambda i: (0, jnp.maximum(i - n1, 0))),
                  pl.BlockSpec((T, N), lambda i: (0, 0), pipeline_mode=pl.Buffered(1))] + ci,
        out_specs=[pl.BlockSpec((GLB, N), lambda i: (i, 0))] + co,
        out_shape=[_sds((P_W, N), BF16)] + cs, scratch_shapes=cscr, compiler_params=_params(),
    )(dp, dgl, h, *ca)
    return res[0] if carry is None else (res[0], res[1:])


def _ffn_in_bwd_fused(name, df, w_t, x1, dres, mat, gpre, mod, gpost, carry=None):
    T = df.shape[0]
    tm = T // 8
    row = lambda w_: pl.BlockSpec((tm, w_), lambda i: (i, 0))

    def kern(df_ref, w_ref, x1_ref, dres_ref, mat_ref, gpre_ref, mod_ref, gpost_ref,
             dx1_ref, dm_ref, dsh_ref, dsc_ref, dgpre_ref, dga_ref, dgpost_ref):
        i = pl.program_id(0)
        _zero_at_start(i, (dsh_ref, dsc_ref, dgpre_ref, dga_ref, dgpost_ref))
        for rows, row0 in _parts(i, tm):
            dh2 = _dot(df_ref[rows, :], w_ref[...])
            dx1 = dres_ref[rows, :] + _normmod_bwd_vals(i, dh2, x1_ref[rows, :], gpre_ref[...], mod_ref, SH2, SC2,
                                                        dsh_ref, dsc_ref, dgpre_ref, row0=row0)
            dx1_ref[rows, :] = dx1
            dm_ref[rows, :] = _resid_bwd_vals(i, dx1, mat_ref[rows, :], gpost_ref[...], mod_ref, GA1, dga_ref,
                                              dgpost_ref, row0=row0).astype(BF16)

    ci, ca, co, cs, cscr = _carry_args(carry)
    res = pl.pallas_call(
        _carried(kern, carry, 8, 7, *_grid_ends((T // tm,))), name=name, grid=(T // tm,),
        in_specs=[row(2 * D_FF), pl.BlockSpec(w_t.shape, lambda i: (0, 0), pipeline_mode=pl.Buffered(1)), row(D),
                  row(D), row(D), _full_spec(gpre.shape), _full_spec(mod.shape), _full_spec(gpost.shape)] + ci,
        out_specs=[row(D), row(D), _full_spec((2, D)), _full_spec((2, D)), _full_spec((1, D)), _full_spec((2, D)),
                   _full_spec((1, D))] + co,
        out_shape=[_sds((T, D), F32), _sds((T, D), BF16), _sds((2, D), F32), _sds((2, D), F32), _sds((1, D), F32),
                   _sds((2, D), F32), _sds((1, D), F32)] + cs,
        scratch_shapes=cscr, compiler_params=_params(),
    )(df, w_t, x1, dres, mat, gpre, mod, gpost, *ca)
    return res if carry is None else (res[:7], res[7:])


def _out_bwd_fused(name, dm, w_out, w_o_rnn, w_o_attn, p, ya, yb):
    T = dm.shape[0]
    tm = T // 8
    row = lambda w_: pl.BlockSpec((tm, w_), lambda i: (i, 0))

    def kern(dm_ref, w_ref, wr_ref, wa_ref, g0, g1, g2, g3, ya_ref, yb_ref, dya_ref, dyb_ref, dgl_ref, du_ref, do_ref):
        for rows, _ in _parts(pl.program_id(0), tm):
            dz = _dot(dm_ref[rows, :], w_ref[...], NT)
            ga = _sigmoid(jnp.concatenate([g0[rows, :], g1[rows, :]], axis=1).astype(F32))
            gb = _sigmoid(jnp.concatenate([g2[rows, :], g3[rows, :]], axis=1).astype(F32))
            dya = (dz * ga).astype(BF16)
            dyb = (dz * gb).astype(BF16)
            dya_ref[rows, :] = dya
            dyb_ref[rows, :] = dyb
            dgl_ref[rows, :] = jnp.concatenate([dz * ya_ref[rows, :].astype(F32) * ga * (1.0 - ga),
                                                dz * yb_ref[rows, :].astype(F32) * gb * (1.0 - gb)],
                                               axis=1).astype(BF16)
            du_ref[rows, :] = _dot(dya, wr_ref[...], NT).astype(BF16)
            do_ref[rows, :] = _dot(dyb, wa_ref[...], NT).astype(BF16)

    return pl.pallas_call(
        kern, name=name, grid=(T // tm,),
        in_specs=[row(D)] + [_full_spec(w.shape) for w in (w_out, w_o_rnn, w_o_attn)]
                 + [pl.BlockSpec((tm, GLB), lambda i, q=q: (i, COL_GL // GLB + q)) for q in range(4)] + [row(D), row(D)],
        out_specs=[row(D), row(D), row(2 * D), row(D), row(D)],
        out_shape=[_sds((T, D), BF16), _sds((T, D), BF16), _sds((T, 2 * D), BF16), _sds((T, D), BF16),
                   _sds((T, D), BF16)],
        compiler_params=_params(),
    )(dm, w_out, w_o_rnn, w_o_attn, p, p, p, p, ya, yb)


AB = 128
CTX_BLKS = CTX // AB


def _rope_tables(S):
    pos = jnp.arange(S, dtype=jnp.int32)
    inv = ROPE_BASE ** (-jnp.arange(N_FREQ, dtype=F32) / N_FREQ)
    ang_r = (pos // GRID_W).astype(F32)[:, None] * inv[None, :]
    ang_c = (pos % GRID_W).astype(F32)[:, None] * inv[None, :]
    cos = jnp.concatenate([jnp.cos(ang_r)] * 2 + [jnp.cos(ang_c)] * 2, axis=1)
    sin = jnp.concatenate([-jnp.sin(ang_r), jnp.sin(ang_r), -jnp.sin(ang_c), jnp.sin(ang_c)], axis=1)
    return cos, sin


def _rope(x, cos, sin):
    w = x.shape[1]
    reps = w // HEAD
    lane = lax.broadcasted_iota(jnp.int32, x.shape, 1)
    partner = jnp.where((lane & 63) < 32, pltpu.roll(x, w - 32, 1), pltpu.roll(x, 32, 1))
    return x * jnp.tile(cos, (1, reps)) + partner * jnp.tile(sin, (1, reps))


def _unrope(dx, cos, sin):
    w = dx.shape[1]
    reps = w // HEAD
    lane = lax.broadcasted_iota(jnp.int32, dx.shape, 1)
    t = dx * jnp.tile(sin, (1, reps))
    partner = jnp.where((lane & 63) < 32, pltpu.roll(t, w - 32, 1), pltpu.roll(t, 32, 1))
    return dx * jnp.tile(cos, (1, reps)) + partner


def _qkv_prep(name, p, cos, sin, S):
    T = CTX + S
    nt = T // AB
    KW = N_KV * HEAD

    def with_ones(v):
        ones = jnp.ones((AB, HEAD), BF16)
        return jnp.concatenate([v[:, kh * HEAD:(kh + 1) * HEAD] if part == 0 else ones
                                for kh in range(N_KV) for part in range(2)], axis=1)

    def kern(q_ref, k_ref, v_ref, cos_ref, sin_ref, qa_ref, kp_ref, vp_ref, kc_ref, vc_ref):
        i = pl.program_id(0)
        cos_v, sin_v = cos_ref[...], sin_ref[...]
        @pl.when(i < CTX_BLKS)
        def _():
            qa_ref[...] = (q_ref[...].astype(F32) * ATT_SCALE).astype(BF16)
            kc_ref[...] = k_ref[...]
            vc_ref[...] = with_ones(v_ref[...])

        @pl.when((i < CTX_BLKS) | (i >= nt))
        def _():
            kp_ref[...] = jnp.zeros(kp_ref.shape, BF16)
            vp_ref[...] = jnp.zeros(vp_ref.shape, BF16)

        @pl.when((i >= CTX_BLKS) & (i < nt))
        def _():
            qa_ref[...] = (_rope(q_ref[...].astype(F32), cos_v, sin_v) * ATT_SCALE).astype(BF16)
            kp_ref[...] = _rope(k_ref[...].astype(F32), cos_v, sin_v).astype(BF16)
            vp_ref[...] = with_ones(v_ref[...])

    tok = lambda i: jnp.minimum(i, nt - 1)
    lat_map = lambda i: (jnp.clip(i - CTX_BLKS, 0, nt - CTX_BLKS - 1), 0)
    ctx_map = lambda i: (jnp.minimum(i, CTX_BLKS - 1), 0)
    return pl.pallas_call(
        kern, name=name, grid=(nt + CTX_BLKS,),
        in_specs=[pl.BlockSpec((AB, N_Q * HEAD), lambda i: (tok(i), COL_Q // (N_Q * HEAD))),
                  pl.BlockSpec((AB, KW), lambda i: (tok(i), COL_K // KW)),
                  pl.BlockSpec((AB, KW), lambda i: (tok(i), COL_V // KW)),
                  pl.BlockSpec((AB, HEAD), lat_map), pl.BlockSpec((AB, HEAD), lat_map)],
        out_specs=[pl.BlockSpec((AB, N_Q * HEAD), lambda i: (tok(i), 0)),
                   pl.BlockSpec((AB, KW), lambda i: (i, 0)), pl.BlockSpec((AB, 2 * KW), lambda i: (i, 0)),
                   pl.BlockSpec((AB, KW), ctx_map), pl.BlockSpec((AB, 2 * KW), ctx_map)],
        out_shape=[_sds((T, N_Q * HEAD), BF16), _sds((S + 2 * CTX, KW), BF16), _sds((S + 2 * CTX, 2 * KW), BF16),
                   _sds((CTX, KW), BF16), _sds((CTX, 2 * KW), BF16)],
        compiler_params=_params(),
    )(p, p, p, cos, sin)


GW = Q_PER_KV * HEAD


def _band_bias(S):
    r = jnp.arange(AB, dtype=jnp.int32)[:, None]
    c = jnp.arange(3 * AB, dtype=jnp.int32)[None, :]
    near = jnp.abs(c - AB - r) <= AB
    valid = jnp.stack([near & (c >= AB), near, near & (c < 2 * AB)])
    return jnp.where(valid, 0.0, NEG_INF).astype(F32)


def _bias_spec(S):
    nb = S // AB
    return pl.BlockSpec((None, AB, 3 * AB), lambda kh, n: (jnp.where(n == 0, 0, jnp.where(n == nb - 1, 2, 1)), 0, 0))


def _head_probs(q, sink, kc, vce, kb, vbe, bias):
    s_c = _dot(q, kc, NT)
    m = jnp.maximum(jnp.max(s_c, axis=-1, keepdims=True), sink)
    if kb is not None:
        s_b = _dot(q, kb, NT) + bias
        m = jnp.maximum(m, jnp.max(s_b, axis=-1, keepdims=True))
    p_c = jnp.exp(s_c - m).astype(BF16)
    acc = _dot(p_c, vce)
    p_b = None
    if kb is not None:
        p_b = jnp.exp(s_b - m).astype(BF16)
        acc = acc + _dot(p_b, vbe)
    return p_c, p_b, m, acc


def _attn_fwd(name, qa, kc, vc, sink4, S, band=None, prev=None, carry=None):
    T = qa.shape[0]
    has_band = band is not None
    nq = S // AB if has_band else CTX_BLKS
    q_off = CTX_BLKS if has_band else 0

    def kern(*refs):
        q_ref, kc_ref, vc_ref, sink_ref = refs[:4]
        rest = refs[4:]
        o_ref = rest[-1]
        n = pl.program_id(1)
        kc_v, vce = kc_ref[...], vc_ref[...]
        kb = vbe = bias = None
        if has_band:
            kp_ref, vp_ref, bias_ref = rest[:3]
            start = pl.multiple_of(n * AB + (CTX - AB), AB)
            kb = kp_ref[pl.ds(start, 3 * AB), :]
            vbe = vp_ref[pl.ds(start, 3 * AB), :]
            bias = bias_ref[...]
        outs = []
        for g in range(Q_PER_KV):
            sink = sink_ref[g:g + 1, 0:1]
            _, _, m, acc = _head_probs(q_ref[:, g * HEAD:(g + 1) * HEAD], sink, kc_v, vce, kb, vbe, bias)
            l = acc[:, HEAD:] + jnp.exp(sink - m)
            outs.append(acc[:, :HEAD] / l)
        o_ref[...] = jnp.concatenate(outs, axis=1).astype(BF16)

    in_specs = [pl.BlockSpec((AB, GW), lambda kh, n: (n + q_off, kh)),
                pl.BlockSpec((CTX, HEAD), lambda kh, n: (0, kh)), pl.BlockSpec((CTX, 2 * HEAD), lambda kh, n: (0, kh)),
                pl.BlockSpec((None, Q_PER_KV, HEAD), lambda kh, n: (kh, 0, 0))]
    args = [qa, kc, vc, sink4]
    if has_band:
        in_specs += [pl.BlockSpec((S + 2 * CTX, HEAD), lambda kh, n: (0, kh)),
                     pl.BlockSpec((S + 2 * CTX, 2 * HEAD), lambda kh, n: (0, kh)), _bias_spec(S)]
        args += list(band)
    alias = {}
    if prev is not None:
        in_specs.append(ANY)
        alias = {len(args): 0}
        args.append(prev)
    ci, ca, co, cs, cscr = _carry_args(carry)
    res = pl.pallas_call(
        _carried(kern, carry, len(args), 1, *_grid_ends((N_KV, nq))), name=name, grid=(N_KV, nq),
        in_specs=in_specs + ci,
        out_specs=[pl.BlockSpec((AB, GW), lambda kh, n: (n + q_off, kh))] + co,
        out_shape=[_sds((T, N_Q * HEAD), BF16)] + cs, input_output_aliases=alias, scratch_shapes=cscr,
        compiler_params=_params(("arbitrary", "arbitrary")),
    )(*args, *ca)
    return res[0] if carry is None else (res[0], res[1:])


def _attn_bwd(name, qa, kc, vc, sink4, o_all, do_all, S, band=None, prev_dq=None, carry=None):
    T = qa.shape[0]
    has_band = band is not None
    nq = S // AB if has_band else CTX_BLKS
    q_off = CTX_BLKS if has_band else 0
    KW = N_KV * HEAD

    def kern(*refs):
        q_ref, kc_ref, vc_ref, sink_ref, o_ref, do_ref = refs[:6]
        rest = refs[6:]
        if has_band:
            kp_ref, vp_ref, bias_ref = rest[:3]
            rest = rest[3:]
        if prev_dq is not None:
            rest = rest[1:]
        dq_ref, dkc_ref, dvc_ref, dsink_ref = rest[:4]
        n = pl.program_id(1)

        @pl.when(n == 0)
        def _():
            dkc_ref[...] = jnp.zeros(dkc_ref.shape, F32)
            dvc_ref[...] = jnp.zeros(dvc_ref.shape, F32)
            dsink_ref[...] = jnp.zeros(dsink_ref.shape, F32)
            if has_band:
                rest[4][...] = jnp.zeros(rest[4].shape, F32)
                rest[5][...] = jnp.zeros(rest[5].shape, F32)

        kc_v, vce = kc_ref[...], vc_ref[...]
        vc_v = vce[:, :HEAD]
        kb = vbe = vb = bias = None
        if has_band:
            start = pl.multiple_of(n * AB + (CTX - AB), AB)
            kb = kp_ref[pl.ds(start, 3 * AB), :]
            vbe = vp_ref[pl.ds(start, 3 * AB), :]
            vb = vbe[:, :HEAD]
            bias = bias_ref[...]
        stack = lambda ref: jnp.concatenate([ref[:, g * HEAD:(g + 1) * HEAD] for g in range(Q_PER_KV)], axis=0)
        q4, do4 = stack(q_ref), stack(do_ref)
        sink = jnp.concatenate([jnp.broadcast_to(sink_ref[g:g + 1, 0:1], (AB, 1)) for g in range(Q_PER_KV)], axis=0)
        s_c = _dot(q4, kc_v, NT)
        m = jnp.maximum(jnp.max(s_c, axis=-1, keepdims=True), sink)
        if has_band:
            s_b = _dot(q4, kb, NT) + jnp.tile(bias, (Q_PER_KV, 1))
            m = jnp.maximum(m, jnp.max(s_b, axis=-1, keepdims=True))
        p_c = jnp.exp(s_c - m).astype(BF16).astype(F32)
        p_sink = jnp.exp(sink - m)
        l = jnp.sum(p_c, axis=-1, keepdims=True) + p_sink
        if has_band:
            p_b = jnp.exp(s_b - m).astype(BF16).astype(F32)
            l = l + jnp.sum(p_b, axis=-1, keepdims=True)
        inv = 1.0 / l
        delta = jnp.sum(do4.astype(F32) * stack(o_ref).astype(F32), axis=-1, keepdims=True)
        do4b = do4.astype(BF16)
        pn_c = (p_c * inv).astype(BF16)
        ds_c = (p_c * inv * (_dot(do4b, vc_v, NT) - delta)).astype(BF16)
        dq4 = _dot(ds_c, kc_v)
        dkc_ref[...] += _dot(ds_c, q4, TN)
        dvc_ref[...] += _dot(pn_c, do4b, TN)
        if has_band:
            pn_b = (p_b * inv).astype(BF16)
            ds_b = (p_b * inv * (_dot(do4b, vb, NT) - delta)).astype(BF16)
            dq4 = dq4 + _dot(ds_b, kb)
            rest[4][pl.ds(start, 3 * AB), :] += _dot(ds_b, q4, TN)
            rest[5][pl.ds(start, 3 * AB), :] += _dot(pn_b, do4b, TN)
        dq4 = dq4 * ATT_SCALE
        dq_ref[...] = jnp.concatenate([dq4[g * AB:(g + 1) * AB, :] for g in range(Q_PER_KV)], axis=1)
        ps = p_sink * inv * delta
        dsink_ref[...] += jnp.concatenate(
            [jnp.broadcast_to(-jnp.sum(ps[g * AB:(g + 1) * AB, :], axis=0, keepdims=True), (1, HEAD))
             for g in range(Q_PER_KV)], axis=0)

    q_spec = pl.BlockSpec((AB, GW), lambda kh, n: (n + q_off, kh))
    c_spec = pl.BlockSpec((CTX, HEAD), lambda kh, n: (0, kh))
    ce_spec = pl.BlockSpec((CTX, 2 * HEAD), lambda kh, n: (0, kh))
    s_spec = pl.BlockSpec((None, Q_PER_KV, HEAD), lambda kh, n: (kh, 0, 0))
    in_specs = [q_spec, c_spec, ce_spec, s_spec, q_spec, q_spec]
    args = [qa, kc, vc, sink4, o_all, do_all]
    out_specs = [q_spec, c_spec, c_spec, s_spec]
    out_shape = [_sds((T, N_Q * HEAD), F32), _sds((CTX, KW), F32), _sds((CTX, KW), F32), _sds((N_KV, Q_PER_KV, HEAD), F32)]
    if has_band:
        p_spec = pl.BlockSpec((S + 2 * CTX, HEAD), lambda kh, n: (0, kh))
        in_specs += [p_spec, pl.BlockSpec((S + 2 * CTX, 2 * HEAD), lambda kh, n: (0, kh)), _bias_spec(S)]
        args += list(band)
        out_specs += [p_spec, p_spec]
        out_shape += [_sds((S + 2 * CTX, KW), F32)] * 2
    alias = {}
    if prev_dq is not None:
        in_specs.append(ANY)
        alias = {len(args): 0}
        args.append(prev_dq)
    ci, ca, co, cs, cscr = _carry_args(carry)
    n_out = len(out_specs)
    res = pl.pallas_call(
        _carried(kern, carry, len(args), n_out, *_grid_ends((N_KV, nq))), name=name, grid=(N_KV, nq),
        in_specs=in_specs + ci, out_specs=out_specs + co, out_shape=out_shape + cs, scratch_shapes=cscr,
        input_output_aliases=alias, compiler_params=_params(("arbitrary", "arbitrary")),
    )(*args, *ca)
    return res if carry is None else (res[:n_out], res[n_out:])


def _dqkv_assemble(name, dq_all, dkp, dvp, dkc_l, dvc_l, dkc_c, dvc_c, cos, sin, S):
    T = CTX + S
    KW = N_KV * HEAD
    HALF = N_Q * HEAD // 2

    def kern(dq_ref, dkp_ref, dvp_ref, dkcl_ref, dvcl_ref, dkcc_ref, dvcc_ref, cos_ref, sin_ref, out_ref):
        i = pl.program_id(0)
        j = pl.program_id(1)
        cos_v, sin_v = cos_ref[...], sin_ref[...]

        @pl.when((j < 2) & (i == 0))
        def _():
            out_ref[...] = dq_ref[...].astype(BF16)

        @pl.when((j < 2) & (i > 0))
        def _():
            out_ref[...] = _unrope(dq_ref[...], cos_v, sin_v).astype(BF16)

        @pl.when((j == 2) & (i == 0))
        def _():
            out_ref[...] = jnp.concatenate([dkcl_ref[...] + dkcc_ref[...], dvcl_ref[...] + dvcc_ref[...]],
                                           axis=1).astype(BF16)

        @pl.when((j == 2) & (i > 0))
        def _():
            out_ref[...] = jnp.concatenate([_unrope(dkp_ref[...], cos_v, sin_v), dvp_ref[...]], axis=1).astype(BF16)

    same = lambda i, j: (i, 0)
    lat_map = lambda i, j: (jnp.maximum(i - 1, 0), 0)
    ctx_map = lambda i, j: (0, 0)
    return pl.pallas_call(
        kern, name=name, grid=(T // TR, 3),
        in_specs=[pl.BlockSpec((TR, HALF), lambda i, j: (i, jnp.minimum(j, 1))),
                  pl.BlockSpec((TR, KW), same), pl.BlockSpec((TR, KW), same),
                  pl.BlockSpec((CTX, KW), ctx_map), pl.BlockSpec((CTX, KW), ctx_map),
                  pl.BlockSpec((CTX, KW), ctx_map), pl.BlockSpec((CTX, KW), ctx_map),
                  pl.BlockSpec((TR, HEAD), lat_map), pl.BlockSpec((TR, HEAD), lat_map)],
        out_specs=pl.BlockSpec((TR, HALF), lambda i, j: (i, COL_Q // HALF + j)),
        out_shape=_sds((T, DP_W), BF16), compiler_params=_params(("arbitrary", "arbitrary")),
    )(dq_all, dkp, dvp, dkc_l, dvc_l, dkc_c, dvc_c, cos, sin)


RB = 128
CH = 256
HALO = 8
SUB = 8
GRP = 8


def _vscan(a, b, reverse):
    row = lax.broadcasted_iota(jnp.int32, a.shape, 0)
    A, H = a, b
    for s in (1, 2, 4):
        sh = SUB - s if reverse else s
        m = (row < SUB - s) if reverse else (row >= s)
        As = pltpu.roll(A, sh, 0)
        Hs = pltpu.roll(H, sh, 0)
        H = jnp.where(m, A * Hs + H, H)
        A = jnp.where(m, A * As, A)
    return A, H


def _scan_rows(a_ref, b_ref, r0, nrows, reverse, carry, emit):
    ngrp = nrows // (SUB * GRP)
    row = lax.broadcasted_iota(jnp.int32, (SUB, RB), 0)

    def grp(gi, carry):
        g = (ngrp - 1 - gi) if reverse else gi
        base = r0 + g * (SUB * GRP)
        for v in (range(GRP - 1, -1, -1) if reverse else range(GRP)):
            rs = pl.multiple_of(base + v * SUB, SUB)
            A, H = _vscan(a_ref[pl.ds(rs, SUB), :], b_ref[pl.ds(rs, SUB), :], reverse)
            hf = H + A * carry
            if reverse:
                before = jnp.where(row == SUB - 1, carry, pltpu.roll(hf, SUB - 1, 0))
                carry = hf[0:1, :]
            else:
                before = jnp.where(row == 0, carry, pltpu.roll(hf, 1, 0))
                carry = hf[SUB - 1:SUB, :]
            emit(rs, hf, before)
        return carry

    return lax.fori_loop(0, ngrp, grp, carry)


def _pad_start(ci):
    return pl.multiple_of(ci * CH + HALO * jnp.minimum(ci, 1), HALO)


def _conv_taps(ext, transpose=False):
    n = CH + 2 * HALO
    taps = []
    for k in range(CONV_W):
        off = CONV_LEFT - k if transpose else k - CONV_LEFT
        taps.append(ext[HALO:HALO + CH, :] if off == 0 else pltpu.roll(ext, (-off) % n, 0)[HALO:HALO + CH, :])
    return taps


def _lru_gates(xl, w4, b4, ls):
    pre = _dot(xl.astype(BF16), w4) + b4
    out = []
    for d in range(2):
        r = _sigmoid(pre[:, d * RB:(d + 1) * RB])
        i = _sigmoid(pre[:, (2 + d) * RB:(3 + d) * RB])
        la = LRU_C * r * ls[d:d + 1, :]
        a = jnp.exp(la)
        q = -jnp.tanh(la) * (1.0 + a * a)
        out.append((r, i, a, q))
    return out


def _rnn_specs(T):
    col = lambda n, *_: (0, n)
    return dict(
        xr=pl.BlockSpec((T, RB), lambda n, *_: (0, COL_XR // RB + n)),
        gr=pl.BlockSpec((T, RB), lambda n, *_: (0, COL_GR // RB + n)),
        act=pl.BlockSpec((T, RB), col),
        cw=pl.BlockSpec((CONV_W, RB), col), cb=pl.BlockSpec((1, RB), col),
        w4=pl.BlockSpec((None, RB, 4 * RB), lambda n, *_: (n, 0, 0)),
        b4=pl.BlockSpec((None, 1, 4 * RB), lambda n, *_: (n, 0, 0)),
        lam=pl.BlockSpec((2, RB), col))


PAD_ROWS = 3 * HALO


def _zero_pads(pad_ref, T):
    for r in (0, HALO + CTX, 2 * HALO + T):
        pad_ref[r:r + HALO, :] = jnp.zeros((HALO, RB), F32)


def _fill_padded(pad_ref, src_ref, T):
    _zero_pads(pad_ref, T)
    pad_ref[HALO:HALO + CTX, :] = src_ref[0:CTX, :].astype(F32)
    pad_ref[2 * HALO + CTX:2 * HALO + T, :] = src_ref[CTX:T, :].astype(F32)


def _pad_rows(ci):
    return pl.ds(pl.multiple_of(ci * CH + HALO + HALO * jnp.minimum(ci, 1), HALO), CH)


def _rnn_fwd(name, p, cw, cb, w4, b4, lam, T, carry=None):
    def kern(xr_ref, gr_ref, cw_ref, cb_ref, w4_ref, b4_ref, lam_ref,
             u_ref, a0, a1, yo_ref, hpf_ref, hpb_ref, r0_ref, r1_ref, i0_ref, i1_ref, xpad, b0, b1, y):
        _fill_padded(xpad, xr_ref, T)
        ls = _log_sigmoid(lam_ref[...])
        w4v, b4v, cwv, cbv = w4_ref[...], b4_ref[...], cw_ref[...], cb_ref[...]

        def chunk(ci, _):
            rows = pl.ds(pl.multiple_of(ci * CH, CH), CH)
            taps = _conv_taps(xpad[pl.ds(_pad_start(ci), CH + 2 * HALO), :])
            xl = cbv + sum(taps[k] * cwv[k:k + 1, :] for k in range(CONV_W))
            for d, (r, i, a, q) in enumerate(_lru_gates(xl, w4v, b4v, ls)):
                (a0, a1)[d][rows, :] = a
                (b0, b1)[d][rows, :] = jnp.sqrt(q) * (i * xl)
                (r0_ref, r1_ref)[d][rows, :] = r.astype(BF16)
                (i0_ref, i1_ref)[d][rows, :] = i.astype(BF16)
            return 0

        lax.fori_loop(0, T // CH, chunk, 0)
        zero = jnp.zeros((1, RB), F32)

        def emit_f(rs, hf, before):
            y[pl.ds(rs, SUB), :] = hf
            b0[pl.ds(rs, SUB), :] = before

        def emit_b(rs, hf, before):
            y[pl.ds(rs, SUB), :] += hf
            b1[pl.ds(rs, SUB), :] = before

        _scan_rows(a0, b0, 0, T, False, zero, emit_f)
        c = _scan_rows(a1, b1, 0, CTX, True, zero, emit_b)
        _scan_rows(a1, b1, CTX, T - CTX, True, c, emit_b)

        def finish(ci, _):
            rows = pl.ds(pl.multiple_of(ci * CH, CH), CH)
            yv = y[rows, :]
            u_ref[rows, :] = (yv * _gelu(gr_ref[rows, :].astype(F32))).astype(BF16)
            yo_ref[rows, :] = yv.astype(BF16)
            hpf_ref[rows, :] = b0[rows, :].astype(BF16)
            hpb_ref[rows, :] = b1[rows, :].astype(BF16)
            return 0

        lax.fori_loop(0, T // CH, finish, 0)

    sp = _rnn_specs(T)
    ci, ca, co, cs, cscr = _carry_args(carry)
    dts = [BF16, F32, F32] + [BF16] * 7
    res = pl.pallas_call(
        _carried(kern, carry, 7, 10, *_grid_ends((N_RNN_BLOCKS,))), name=name, grid=(N_RNN_BLOCKS,),
        in_specs=[sp["xr"], sp["gr"], sp["cw"], sp["cb"], sp["w4"], sp["b4"], sp["lam"]] + ci,
        out_specs=[sp["act"]] * 10 + co,
        out_shape=[_sds((T, D), dt) for dt in dts] + cs,
        scratch_shapes=[pltpu.VMEM((T + PAD_ROWS, RB), F32)] + [pltpu.VMEM((T, RB), F32)] * 3 + cscr,
        compiler_params=_params(),
    )(p, p, cw, cb, w4, b4, lam, *ca)
    return res if carry is None else (res[:10], res[10:])


def _rnn_bwd(name, p, du, saved, dp, cw, cb, w4, b4, lam, T, carry=None):
    def kern(xr_ref, gr_ref, du_ref, a0, a1, y_ref, hpf_ref, hpb_ref, r0_ref, r1_ref, i0_ref, i1_ref,
             cw_ref, cb_ref, w4_ref, b4_ref, lam_ref, dp_in,
             dp_ref, dcw_ref, dcb_ref, dw4_ref, db4_ref, dlam_ref,
             xpad, dxpad, c0, c1, dy):
        j = pl.program_id(1)

        @pl.when(j == 0)
        def _():
            scans(gr_ref, du_ref, a0, a1, y_ref, dp_ref, c0, c1, dy)

        @pl.when(j == 1)
        def _():
            gates(xr_ref, a0, a1, (hpf_ref, hpb_ref), (r0_ref, r1_ref), (i0_ref, i1_ref), cw_ref, cb_ref, w4_ref,
                  lam_ref, dp_ref, dcw_ref, dcb_ref, dw4_ref, db4_ref, dlam_ref, xpad, dxpad, c0, c1)

    def scans(gr_ref, du_ref, a0, a1, y_ref, dgr_ref, c0, c1, dy):
        def phase_a(ci, _):
            rows = pl.ds(pl.multiple_of(ci * CH, CH), CH)
            gr = gr_ref[rows, :].astype(F32)
            duv = du_ref[rows, :].astype(F32)
            dyv = duv * _gelu(gr)
            dgr_ref[rows, :] = (duv * y_ref[rows, :].astype(F32) * _gelu_grad(gr)).astype(BF16)
            dy[rows, :] = dyv
            c0[rows, :] = a0[rows, :] * dyv
            c1[rows, :] = a1[rows, :] * dyv
            return 0

        lax.fori_loop(0, T // CH, phase_a, 0)
        zero = jnp.zeros((1, RB), F32)

        def emit0(rs, hf, before):
            c0[pl.ds(rs, SUB), :] = dy[pl.ds(rs, SUB), :] + before

        def emit1(rs, hf, before):
            c1[pl.ds(rs, SUB), :] = dy[pl.ds(rs, SUB), :] + before

        _scan_rows(a0, c0, 0, T, True, zero, emit0)
        c = _scan_rows(a1, c1, CTX, T - CTX, False, zero, emit1)
        _scan_rows(a1, c1, 0, CTX, False, c, emit1)

    def gates(xr_ref, a0, a1, hp_refs, r_refs, i_refs, cw_ref, cb_ref, w4_ref, lam_ref,
              dxr_ref, dcw_ref, dcb_ref, dw4_ref, db4_ref, dlam_ref, xpad, dxpad, c0, c1):
        _fill_padded(xpad, xr_ref, T)
        _zero_pads(dxpad, T)
        lam_v = lam_ref[...]
        ls = _log_sigmoid(lam_v)
        w4v, cwv, cbv = w4_ref[...], cw_ref[...], cb_ref[...]

        def conv_chunk(ci):
            taps = _conv_taps(xpad[pl.ds(_pad_start(ci), CH + 2 * HALO), :])
            return taps, cbv + sum(taps[k] * cwv[k:k + 1, :] for k in range(CONV_W))

        dw4_ref[...] = jnp.zeros(dw4_ref.shape, F32)
        db4_ref[...] = jnp.zeros(db4_ref.shape, F32)
        dlam_ref[...] = jnp.zeros(dlam_ref.shape, F32)
        dcw_ref[...] = jnp.zeros(dcw_ref.shape, F32)
        dcb_ref[...] = jnp.zeros(dcb_ref.shape, F32)

        def phase_c(ci, _):
            base = pl.multiple_of(ci * CH, CH)
            rows = pl.ds(base, CH)
            _, xl = conv_chunk(ci)
            dxl = jnp.zeros((CH, RB), F32)
            dpre_a, dpre_x, dls = [], [], []
            for d in range(2):
                a = (a0, a1)[d][rows, :]
                r = r_refs[d][rows, :].astype(F32)
                i = i_refs[d][rows, :].astype(F32)
                q = -jnp.tanh(LRU_C * r * ls[d:d + 1, :]) * (1.0 + a * a)
                g = (c0, c1)[d][rows, :]
                hp = hp_refs[d][rows, :].astype(F32)
                gm = g * jnp.sqrt(q)
                di = gm * xl
                dxl = dxl + gm * i
                dla = a * (g * hp - a * (g * (i * xl)) * lax.rsqrt(q))
                dr = dla * (LRU_C * ls[d:d + 1, :])
                dls.append(_colsum(dla * (LRU_C * r)))
                dpre_a.append(dr * r * (1.0 - r))
                dpre_x.append(di * i * (1.0 - i))
            dpre = jnp.concatenate(dpre_a + dpre_x, axis=1)
            dpre_b = dpre.astype(BF16)
            dxl = dxl + _dot(dpre_b, w4v, NT)
            dw4_ref[...] += _dot(xl.astype(BF16), dpre_b, TN)
            db4_ref[...] += _colsum(dpre)
            dlam_ref[...] += jnp.concatenate(dls, axis=0)
            dcb_ref[...] += _colsum(dxl)
            dxpad[_pad_rows(ci), :] = dxl
            return 0

        lax.fori_loop(0, T // CH, phase_c, 0)
        dlam_ref[...] = dlam_ref[...] * _sigmoid(-lam_v)

        def phase_d(ci, _):
            base = pl.multiple_of(ci * CH, CH)
            rows = pl.ds(base, CH)
            xtaps, _ = conv_chunk(ci)
            dtaps = _conv_taps(dxpad[pl.ds(_pad_start(ci), CH + 2 * HALO), :], transpose=True)
            dxl = dxpad[_pad_rows(ci), :]
            dxr_ref[rows, :] = sum(dtaps[k] * cwv[k:k + 1, :] for k in range(CONV_W)).astype(BF16)
            dcw_ref[...] += jnp.concatenate([_colsum(dxl * xtaps[k]) for k in range(CONV_W)], axis=0)
            return 0

        lax.fori_loop(0, T // CH, phase_d, 0)

    sp = _rnn_specs(T)
    dp_spec = pl.BlockSpec((T, RB), lambda n, j: (0, COL_GR // RB + n - j * (COL_GR - COL_XR) // RB))
    ci, ca, co, cs, cscr = _carry_args(carry)
    n_in = 3 + len(saved) + 5 + 1
    res = pl.pallas_call(
        _carried(kern, carry, n_in, 6, *_grid_ends((N_RNN_BLOCKS, 2))), name=name, grid=(N_RNN_BLOCKS, 2),
        in_specs=[sp["xr"], sp["gr"]] + [sp["act"]] * (1 + len(saved)) + [sp["cw"], sp["cb"], sp["w4"], sp["b4"],
                                                                           sp["lam"], ANY] + ci,
        out_specs=[dp_spec, sp["cw"], sp["cb"], sp["w4"], sp["b4"], sp["lam"]] + co,
        out_shape=[_sds((T, DP_W), BF16), _sds((CONV_W, D), F32), _sds((1, D), F32),
                   _sds((N_RNN_BLOCKS, RB, 4 * RB), F32), _sds((N_RNN_BLOCKS, 1, 4 * RB), F32), _sds((2, D), F32)] + cs,
        scratch_shapes=[pltpu.VMEM((T + PAD_ROWS, RB), F32)] * 2 + [pltpu.VMEM((T, RB), F32)] * 3 + cscr,
        input_output_aliases={n_in - 1: 0},
        compiler_params=_params(("arbitrary", "arbitrary")),
    )(p, p, du, *saved, cw, cb, w4, b4, lam, dp, *ca)
    return res if carry is None else (res[:6], res[6:])


class _Plan:
    def __init__(self, shards, Ws):
        L = len(Ws)
        self.shards, self.Ws = shards, Ws
        self.Gs = [None] * L
        self.slots = [dict() for _ in range(L)]
        self.gate_slots = [None] * L
        self.table = {}
        for l in range(L):
            t = f"l{l}_"
            self.table[t + "proj"] = [("gather", l, k) for k in ("wo_rnn", "wo_attn", "wout")]
            self.table[t + "rnn_fwd"] = [("gather", l, "wffn_in_t")]
            self.table[t + "attn_lat_fwd"] = [("gather", l + 1, "win_t")] if l + 1 < L else []
            self.table[t + "ffn_in"] = [("gather", l, "wffn_out")]
            self.table[t + "ffn_in_dx"] = [("scatter", l, "wffn_out")]
            self.table[t + "attn_lat_bwd"] = [("scatter", l, "wffn_in_t")]
            self.table[t + "ffn_in_dw"] = [("gates", l + 1, "w4")] if l + 1 < L else []
            self.table[t + "rnn_bwd"] = ([("scatter", l, k) for k in ("wout", "wo_attn", "wo_rnn")]
                                         + ([("scatter", l + 1, "win_t")] if l + 1 < L else []))
        self.table["l0_proj_dx"] = [("scatter", 0, "win_t_a")]
        self.table["l0_proj_dw_b"] = [("gates", 0, "w4")]

    def carry(self, name):
        jobs = []
        for kind, l, k in self.table.get(name, []):
            if kind == "gather":
                jobs.append(("gather", self.shards[l][k]))
            elif kind == "scatter":
                jobs.append(("scatter", self.Gs[l][k].reshape(N_DEV, -1, self.Gs[l][k].shape[-1])))
            else:
                jobs.append(("gather", self.Gs[l]["w4"].reshape(N_RNN_BLOCKS * RB, 4 * RB).astype(BF16)))
        return _Carry(jobs) if jobs else None

    def done(self, name, got):
        for (kind, l, k), res in zip(self.table[name], got):
            if kind == "gather":
                self.Ws[l][k] = res.reshape(-1, D)
            elif kind == "scatter":
                self.slots[l][k] = res
            else:
                self.gate_slots[l] = res


def _run(X, fn, name, *args, **kw):
    carry = None if X is None else X.carry(name)
    if carry is None:
        return fn(name, *args, **kw)
    out, got = fn(name, *args, carry=carry, **kw)
    X.done(name, got)
    return out


def _layer_fwd(l, xa, h, W, rope, S, nxt, X=None):
    T = xa.shape[0]
    tag = f"l{l}_"
    cos, sin, bias = rope
    p = _run(X, _mm_act, tag + "proj", h, W["win_t"], "NT", BF16)
    u, *rnn_saved = _run(X, _rnn_fwd, tag + "rnn_fwd", p, W["cw"], W["cb"], W["w4"], W["b4"], W["lam"], T)
    qa, kp, vp, kc, vc = _qkv_prep(tag + "qkv_prep", p, cos, sin, S)
    o_all = _attn_fwd(tag + "attn_ctx_fwd", qa, kc, vc, W["sink4"], S)
    o_all = _run(X, _attn_fwd, tag + "attn_lat_fwd", qa, kc, vc, W["sink4"], S, band=(kp, vp, bias), prev=o_all)
    ya, yb, z, m, x1, h2 = _out_fused(tag + "out", p, u, o_all, xa, W["wo_rnn"], W["wo_attn"], W["wout"],
                                      W["g_mix_post"], W["mod"], W["g_ffn_pre"])
    fg, fu, s = _run(X, _ffn_in_fused, tag + "ffn_in", h2, W["wffn_in_t"])
    e, *out = _ffn_out_fused(tag + "ffn_out", s, W["wffn_out"], x1, W["g_ffn_post"], W["mod"], nxt)
    saved = dict(xa=xa, h=h, p=p, u=u, rnn=rnn_saved, qa=qa, kp=kp, vp=vp, kc=kc, vc=vc, o_all=o_all,
                 ya=ya, yb=yb, z=z, m=m, x1=x1, h2=h2, fg=fg, fu=fu, s=s, e=e)
    return saved, out


def _layer_bwd(l, dx2, A, W, rope, S, X=None, loss_of=None):
    T = A["xa"].shape[0]
    tag = f"l{l}_"
    cos, sin, bias = rope
    G = {}
    if X is not None:
        X.Gs[l] = G
    if loss_of is None:
        de, df, dga2, G["g_ffn_post"] = _ffn_bwd_fused(tag + "ffn_bwd", A["fg"], A["fu"], W["wffn_out"],
                                                       head=(dx2, A["e"], W["g_ffn_post"], W["mod"]))
    else:
        dx2, de, dga2, G["g_ffn_post"], G["sq"] = _loss_resid_bwd(tag + "loss_ffn_resid_bwd", *loss_of, A["e"],
                                                                  W["g_ffn_post"], W["mod"], GA2)
        df, = _ffn_bwd_fused(tag + "ffn_bwd", A["fg"], A["fu"], W["wffn_out"], de=de)
    G["wffn_out"] = _mm_wgrad(tag + "ffn_out_dw", A["s"], de)
    dx1, dm, dsh2, dsc2, G["g_ffn_pre"], dga1, G["g_mix_post"] = _run(
        X, _ffn_in_bwd_fused, tag + "ffn_in_dx", df, W["wffn_in_t"], A["x1"], dx2, A["m"], W["g_ffn_pre"], W["mod"],
        W["g_mix_post"])
    G["wffn_in_t"] = _run(X, _mm_wgrad, tag + "ffn_in_dw", df, A["h2"])
    G["wout"] = _mm_wgrad(tag + "out_dw", A["z"], dm)
    dya, dyb, dgl, du, do = _out_bwd_fused(tag + "out_dx", dm, W["wout"], W["wo_rnn"], W["wo_attn"], A["p"], A["ya"],
                                           A["yb"])
    G["wo_attn"] = _mm_wgrad(tag + "o_attn_dw", A["o_all"], dyb)
    G["wo_rnn"] = _mm_wgrad(tag + "o_rnn_dw", A["u"], dya)
    dq_all, dkc_c, dvc_c, dsink_c = _attn_bwd(tag + "attn_ctx_bwd", A["qa"], A["kc"], A["vc"], W["sink4"],
                                               A["o_all"], do, S)
    dq_all, dkc_l, dvc_l, dsink_l, dkp, dvp = _run(
        X, _attn_bwd, tag + "attn_lat_bwd", A["qa"], A["kc"], A["vc"], W["sink4"], A["o_all"], do, S,
        band=(A["kp"], A["vp"], bias), prev_dq=dq_all)
    G["sink4"] = dsink_c + dsink_l
    dp = _dqkv_assemble(tag + "dqkv", dq_all, dkp, dvp, dkc_l, dvc_l, dkc_c, dvc_c, cos, sin, S)
    dp, G["cw"], G["cb"], G["w4"], G["b4"], G["lam"] = _run(
        X, _rnn_bwd, tag + "rnn_bwd", A["p"], du, A["rnn"], dp, W["cw"], W["cb"], W["w4"], W["b4"], W["lam"], T)
    proj_dx = (_proj_bwd_fused, tag + "proj_dx", dp, dgl, W["win_t"], A["xa"], dx1, W["g_mix_pre"], W["mod"])
    if X is not None and l == 0:
        G["win_t_a"] = _proj_wgrad(tag + "proj_dw_a", dp, dgl, A["h"][:, :D // 2])
        dxa, dsh1, dsc1, G["g_mix_pre"] = _run(X, *proj_dx)
        G["win_t_b"] = _run(X, _proj_wgrad, tag + "proj_dw_b", dp, dgl, A["h"][:, D // 2:])
    else:
        dxa, dsh1, dsc1, G["g_mix_pre"] = _run(X, *proj_dx)
        G["win_t"] = _proj_wgrad(tag + "proj_dw", dp, dgl, A["h"])
    G["mod"] = jnp.concatenate([dsh1, dsc1, dga1, dsh2, dsc2, dga2], axis=1)
    return dxa, G


def _local_step(xa, target, Ws, S, X=None):
    rope = (*_rope_tables(S), _band_bias(S))
    L = len(Ws)
    h = _normmod_fwd("l0_mix_norm", xa, Ws[0]["g_mix_pre"], Ws[0]["mod"], SH1, SC1)
    saved = []
    x = xa
    for l in range(L):
        nxt = (Ws[l + 1]["g_mix_pre"], Ws[l + 1]["mod"]) if l + 1 < L else None
        A, out = _layer_fwd(l, x, h, Ws[l], rope, S, nxt, X)
        saved.append(A)
        if l + 1 < L:
            x, h = out
    Gs = [None] * L
    dx = None
    for l in reversed(range(L)):
        dx, Gs[l] = _layer_bwd(l, dx, saved[l], Ws[l], rope, S, X, loss_of=(out[0], target) if l == L - 1 else None)
    return Gs[L - 1]["sq"], dx, Gs


MESH = pl.DeviceIdType.MESH


def _place():
    return lax.axis_index("x"), lax.axis_index("y"), lax.axis_index("c")


def _lin(px, py, pc):
    return 4 * px + 2 * py + pc


def _allgather_small(name, blk):
    m, n = blk.shape

    def body(x_ref, out_ref, send_sems, recv_sems, local_sem):
        x, y, c = _place()
        me, sibling = (x, y, c), (x, y, 1 - c)
        chips = [(1 - x, y), (x, 1 - y), (1 - x, 1 - y)]

        def copy(k, block, to, src=None):
            dst = out_ref.at[_lin(*block)]
            return pltpu.make_async_remote_copy(src_ref=dst if src is None else src, dst_ref=dst,
                                                send_sem=send_sems.at[k], recv_sem=recv_sems.at[k],
                                                device_id=to, device_id_type=MESH)

        mine = pltpu.make_async_copy(x_ref, out_ref.at[_lin(*me)], local_sem)
        mine.start()
        first = [copy(0, me, sibling, src=x_ref)]
        first += [copy(1 + j, me, (*chip, c), src=x_ref) for j, chip in enumerate(chips)]
        for cp in first:
            cp.start()
        passed = [copy(4 + j, (*chip, c), sibling) for j, chip in enumerate(chips)]
        for j, chip in enumerate(chips):
            copy(1 + j, (*chip, c), me).wait_recv()
            passed[j].start()
        copy(0, sibling, me).wait_recv()
        for j, chip in enumerate(chips):
            copy(4 + j, (*chip, 1 - c), me).wait_recv()
        for cp in first + passed:
            cp.wait_send()
        mine.wait()

    return pl.pallas_call(
        body, name=name, out_shape=_sds((N_DEV, m, n), blk.dtype),
        in_specs=[pl.BlockSpec(memory_space=pltpu.VMEM)], out_specs=pl.BlockSpec(memory_space=pltpu.VMEM),
        scratch_shapes=[pltpu.SemaphoreType.DMA((7,)), pltpu.SemaphoreType.DMA((7,)), pltpu.SemaphoreType.DMA],
        compiler_params=pltpu.CompilerParams(vmem_limit_bytes=VMEM_LIMIT),
    )(blk)


def _allgather_hbm(name, shards):
    na = len(shards)

    def body(*refs):
        ins, outs = refs[:na], refs[na:2 * na]
        send_sems, recv_sems, local_sems = refs[2 * na:]
        x, y, c = _place()
        me, sibling = (x, y, c), (x, y, 1 - c)
        chips = [(1 - x, y), (x, 1 - y), (1 - x, 1 - y)]

        def copy(a, k, block, to, from_input=False):
            dst = outs[a].at[_lin(*block)]
            return pltpu.make_async_remote_copy(src_ref=ins[a] if from_input else dst, dst_ref=dst,
                                                send_sem=send_sems.at[a, k], recv_sem=recv_sems.at[a, k],
                                                device_id=to, device_id_type=MESH)

        mine = [pltpu.make_async_copy(ins[a], outs[a].at[_lin(*me)], local_sems.at[a]) for a in range(na)]
        for cp in mine:
            cp.start()
        first = []
        for a in range(na):
            first.append(copy(a, 0, me, sibling, True))
            first += [copy(a, 1 + j, me, (*chip, c), True) for j, chip in enumerate(chips)]
        for cp in first:
            cp.start()
        passed = []
        for j, chip in enumerate(chips):
            for a in range(na):
                copy(a, 1 + j, (*chip, c), me).wait_recv()
                fwd = copy(a, 4 + j, (*chip, c), sibling)
                fwd.start()
                passed.append(fwd)
        for a in range(na):
            copy(a, 0, sibling, me).wait_recv()
            for j, chip in enumerate(chips):
                copy(a, 4 + j, (*chip, 1 - c), me).wait_recv()
        for cp in first + passed:
            cp.wait_send()
        for cp in mine:
            cp.wait()

    return pl.pallas_call(
        body, name=name, out_shape=[_sds((N_DEV, *s.shape), s.dtype) for s in shards],
        in_specs=[ANY] * na, out_specs=[ANY] * na,
        scratch_shapes=[pltpu.SemaphoreType.DMA((na, 7)), pltpu.SemaphoreType.DMA((na, 7)),
                        pltpu.SemaphoreType.DMA((na,))],
    )(*shards)


def _exchange_shards(name, grads, L):
    nw = len(grads)
    na = nw * L
    flat = [g for per_layer in grads for g in per_layer]

    def body(*refs):
        ins, outs = refs[:na], refs[na:na + nw]
        send_sems, recv_sems, local_sems = refs[na + nw:]
        x, y, c = _place()
        me = _lin(x, y, c)
        peers = [(x ^ ((k + 1) >> 2 & 1), y ^ ((k + 1) >> 1 & 1), c ^ ((k + 1) & 1)) for k in range(7)]

        def copy(a, k, src_blk, dst_blk):
            return pltpu.make_async_remote_copy(src_ref=ins[a].at[src_blk], dst_ref=outs[a // L].at[a % L, dst_blk],
                                                send_sem=send_sems.at[a, k], recv_sem=recv_sems.at[a, k],
                                                device_id=peers[k], device_id_type=MESH)

        mine = [pltpu.make_async_copy(ins[a].at[me], outs[a // L].at[a % L, me], local_sems.at[a]) for a in range(na)]
        for cp in mine:
            cp.start()
        sent = [copy(a, k, _lin(*peers[k]), me) for a in range(na) for k in range(7)]
        for cp in sent:
            cp.start()
        for a in range(na):
            for k in range(7):
                copy(a, k, me, _lin(*peers[k])).wait_recv()
        for cp in sent:
            cp.wait_send()
        for cp in mine:
            cp.wait()

    return pl.pallas_call(
        body, name=name, out_shape=[_sds((L, *per_layer[0].shape), per_layer[0].dtype) for per_layer in grads],
        in_specs=[ANY] * na, out_specs=[ANY] * nw,
        scratch_shapes=[pltpu.SemaphoreType.DMA((na, 7)), pltpu.SemaphoreType.DMA((na, 7)),
                        pltpu.SemaphoreType.DMA((na,))],
    )(*flat)


MOD_ROWS = 16
MOD_SHARD = 6 * D // N_DEV
HI = lax.Precision.HIGHEST


def _mod_fwd(name, c9, w_mod, b_shard):
    L = w_mod.shape[0]

    def kern(c_ref, w_ref, b_ref, o_ref):
        o_ref[...] = lax.dot_general(_silu(c_ref[...]), w_ref[...], NN, precision=HI,
                                     preferred_element_type=F32) + b_ref[...]

    return pl.pallas_call(
        kern, name=name, grid=(L,),
        in_specs=[_full_spec(c9.shape), pl.BlockSpec((None, D, MOD_SHARD), lambda l: (l, 0, 0)),
                  pl.BlockSpec((None, 1, MOD_SHARD), lambda l: (l, 0, 0))],
        out_specs=pl.BlockSpec((None, MOD_ROWS, MOD_SHARD), lambda l: (l, 0, 0)),
        out_shape=_sds((L, MOD_ROWS, MOD_SHARD), F32), compiler_params=_params(),
    )(c9, w_mod, b_shard)


def _mod_bwd(name, c9, w_mod, dmod_all, dmod_cols):
    L = w_mod.shape[0]

    def rows9(ref, l):
        own = jnp.concatenate([ref[j, 2 * l + 1:2 * l + 2, :] for j in range(N_DEV)], axis=0)
        ctx = ref[0, 2 * l:2 * l + 1, :]
        for j in range(1, N_DEV):
            ctx = ctx + ref[j, 2 * l:2 * l + 1, :]
        return own, ctx

    def kern(c_ref, w_ref, all_ref, cols_ref, gw_ref, gb_ref, gc_ref):
        l = pl.program_id(0)
        for ll in range(L):
            @pl.when(l == ll)
            def _():
                own, ctx = rows9(all_ref, ll)
                gb_ref[...] = _colsum(own) + ctx
                own_s, ctx_s = rows9(cols_ref, ll)
                r16 = jnp.concatenate([own_s, ctx_s, jnp.zeros((MOD_ROWS - N_DEV - 1, MOD_SHARD), F32)], axis=0)
                gw_ref[...] = lax.dot_general(_silu(c_ref[...]), r16, TN, precision=HI, preferred_element_type=F32)
                part = lax.dot_general(r16, w_ref[...], NT, precision=HI,
                                       preferred_element_type=F32)[N_DEV:N_DEV + 1, :]
                if ll == 0:
                    gc_ref[...] = part
                else:
                    gc_ref[...] += part

    return pl.pallas_call(
        kern, name=name, grid=(L,),
        in_specs=[_full_spec(c9.shape), pl.BlockSpec((None, D, MOD_SHARD), lambda l: (l, 0, 0)),
                  _full_spec(dmod_all.shape), _full_spec(dmod_cols.shape)],
        out_specs=[pl.BlockSpec((None, D, MOD_SHARD), lambda l: (l, 0, 0)),
                   pl.BlockSpec((None, 1, 6 * D), lambda l: (l, 0, 0)), _full_spec((1, D))],
        out_shape=[_sds((L, D, MOD_SHARD), F32), _sds((L, 1, 6 * D), F32), _sds((1, D), F32)],
        compiler_params=_params(),
    )(c9, w_mod, dmod_all, dmod_cols)


_BC1 = 1.0 - ADAM_B1 ** ADAM_STEP
_BC2 = 1.0 - ADAM_B2 ** ADAM_STEP


def _adamw_vals(w, g, m, v):
    m = ADAM_B1 * m + (1.0 - ADAM_B1) * g
    v = ADAM_B2 * v + (1.0 - ADAM_B2) * (g * g)
    delta = -ADAM_LR * ((m / _BC1) / (jnp.sqrt(v / _BC2) + ADAM_EPS) + ADAM_WD * w)
    return delta, m, v


def _adamw(name, w, g, m, v, tile):
    R, C = w.shape
    blk = ((tile, C), lambda i: (i, 0))

    def body(i, ins, ps, outs, acc):
        d, mm, vv = _adamw_vals(ins[0][...], ins[1][...], ins[2][...], ins[3][...])
        outs[0][...] = d
        outs[1][...] = mm
        outs[2][...] = vv

    return _ew(name, body, R // tile, [(a, *blk) for a in (w, g, m, v)], [], [(_sds((R, C), F32), *blk)] * 3)


def _sum_slots(ref):
    g = ref[0].astype(F32)
    for j in range(1, N_DEV):
        g = g + ref[j].astype(F32)
    return g


def _adamw_slots(name, slots, w, m, v, tile):
    L, R, C = w.shape
    n = R // tile
    spec = pl.BlockSpec((None, tile, C), lambda l, i: (l, i, 0))
    pieces = [s if isinstance(s, (list, tuple)) else [s] for s in slots]
    layer_of = [ll for ll, ps in enumerate(pieces) for _ in ps]
    flat = [p for ps in pieces for p in ps]

    def slot_spec(ll, cols):
        return pl.BlockSpec((N_DEV, tile, cols),
                            lambda l, i: (0, jnp.where(l == ll, i, jnp.where(l < ll, 0, n - 1)), 0))

    def kern(*refs):
        s_refs = refs[:len(flat)]
        w_ref, m_ref, v_ref, g_ref, d_ref, mo_ref, vo_ref = refs[len(flat):]
        l = pl.program_id(0)
        for ll in range(L):
            @pl.when(l == ll)
            def _():
                parts = [_sum_slots(r) for r, lr in zip(s_refs, layer_of) if lr == ll]
                g = parts[0] if len(parts) == 1 else jnp.concatenate(parts, axis=1)
                g_ref[...] = g
                d_ref[...], mo_ref[...], vo_ref[...] = _adamw_vals(w_ref[...], g, m_ref[...], v_ref[...])

    return pl.pallas_call(
        kern, name=name, grid=(L, n),
        in_specs=[slot_spec(ll, p.shape[-1]) for ll, p in zip(layer_of, flat)] + [spec, spec, spec],
        out_specs=[spec] * 4, out_shape=[_sds((L, R, C), F32)] * 4,
        compiler_params=_params(("arbitrary", "arbitrary")),
    )(*flat, w, m, v)


def _sum_blocks(name, blocks):
    _, R, C = blocks.shape

    def kern(b_ref, o_ref):
        o_ref[...] = _sum_slots(b_ref)

    return pl.pallas_call(kern, name=name, in_specs=[_full_spec(blocks.shape)], out_specs=_full_spec((R, C)),
                          grid=(1,), out_shape=_sds((R, C), F32), compiler_params=_params())(blocks)


BIG = ("win_t", "wo_rnn", "wo_attn", "wout", "wffn_in_t", "wffn_out")
BIG_SRC = ("w_in", "w_o_rnn", "w_o_attn", "w_out", "w_ffn_in", "w_ffn_out")
BIG_T = (True, False, False, False, True, False)
BIG_TILE = (176, 128, 128, 128, 176, 176)


def _chan_full(g8):
    return jnp.transpose(g8, (1, 0, 2)).reshape(g8.shape[1], D)


def kernel(x, c, ctx, c_ctx, w_mod, b_mod, g_mix_pre, g_mix_post, g_ffn_pre, g_ffn_post, w_in, conv_w, conv_b, lru_wa, lru_ba, lru_wx, lru_bx, lru_lam, attn_sink, w_o_rnn, w_o_attn, w_out, w_ffn_in, w_ffn_out, loss_target, m_c_ctx, m_w_mod, m_b_mod, m_g_mix_pre, m_g_mix_post, m_g_ffn_pre, m_g_ffn_post, m_w_in, m_conv_w, m_conv_b, m_lru_wa, m_lru_ba, m_lru_wx, m_lru_bx, m_lru_lam, m_attn_sink, m_w_o_rnn, m_w_o_attn, m_w_out, m_w_ffn_in, m_w_ffn_out, v_c_ctx, v_w_mod, v_b_mod, v_g_mix_pre, v_g_mix_post, v_g_ffn_pre, v_g_ffn_post, v_w_in, v_conv_w, v_conv_b, v_lru_wa, v_lru_ba, v_lru_wx, v_lru_bx, v_lru_lam, v_attn_sink, v_w_o_rnn, v_w_o_attn, v_w_out, v_w_ffn_in, v_w_ffn_out):
    P = dict(c_ctx=c_ctx, w_mod=w_mod, b_mod=b_mod, g_mix_pre=g_mix_pre, g_mix_post=g_mix_post, g_ffn_pre=g_ffn_pre,
             g_ffn_post=g_ffn_post, w_in=w_in, conv_w=conv_w, conv_b=conv_b, lru_wa=lru_wa, lru_ba=lru_ba,
             lru_wx=lru_wx, lru_bx=lru_bx, lru_lam=lru_lam, attn_sink=attn_sink, w_o_rnn=w_o_rnn, w_o_attn=w_o_attn,
             w_out=w_out, w_ffn_in=w_ffn_in, w_ffn_out=w_ffn_out)
    Mo = dict(c_ctx=m_c_ctx, w_mod=m_w_mod, b_mod=m_b_mod, g_mix_pre=m_g_mix_pre, g_mix_post=m_g_mix_post,
              g_ffn_pre=m_g_ffn_pre, g_ffn_post=m_g_ffn_post, w_in=m_w_in, conv_w=m_conv_w, conv_b=m_conv_b,
              lru_wa=m_lru_wa, lru_ba=m_lru_ba, lru_wx=m_lru_wx, lru_bx=m_lru_bx, lru_lam=m_lru_lam,
              attn_sink=m_attn_sink, w_o_rnn=m_w_o_rnn, w_o_attn=m_w_o_attn, w_out=m_w_out, w_ffn_in=m_w_ffn_in,
              w_ffn_out=m_w_ffn_out)
    Vo = dict(c_ctx=v_c_ctx, w_mod=v_w_mod, b_mod=v_b_mod, g_mix_pre=v_g_mix_pre, g_mix_post=v_g_mix_post,
              g_ffn_pre=v_g_ffn_pre, g_ffn_post=v_g_ffn_post, w_in=v_w_in, conv_w=v_conv_w, conv_b=v_conv_b,
              lru_wa=v_lru_wa, lru_ba=v_lru_ba, lru_wx=v_lru_wx, lru_bx=v_lru_bx, lru_lam=v_lru_lam,
              attn_sink=v_attn_sink, w_o_rnn=v_w_o_rnn, w_o_attn=v_w_o_attn, w_out=v_w_out, w_ffn_in=v_w_ffn_in,
              w_ffn_out=v_w_ffn_out)
    L = w_in.shape[0]
    S = x.shape[1]
    me = _lin(*_place())

    small = jnp.concatenate([c.reshape(8, 128), conv_w.reshape(L * CONV_W, 128), lru_ba.reshape(2 * L, 128),
                             lru_bx.reshape(2 * L, 128), lru_lam.reshape(2 * L, 128), jnp.zeros((4, 128), F32)], axis=0)
    small_all = _allgather_small("ag_small", small)
    c_all = small_all[:, 0:8].reshape(N_DEV, D)
    conv_w_f = _chan_full(small_all[:, 8:16]).reshape(L, CONV_W, D)
    lru_ba_f = _chan_full(small_all[:, 16:20]).reshape(L, 2, D)
    lru_bx_f = _chan_full(small_all[:, 20:24]).reshape(L, 2, D)
    lru_lam_f = _chan_full(small_all[:, 24:28]).reshape(L, 2, D)

    c9 = jnp.concatenate([c_all, c_ctx[None], jnp.zeros((MOD_ROWS - N_DEV - 1, D), F32)], axis=0)
    b_shard = lax.dynamic_slice_in_dim(b_mod, me * MOD_SHARD, MOD_SHARD, axis=1)[:, None, :]
    mod_part = _mod_fwd("mod_fwd", c9, w_mod, b_shard)
    mod_all = _allgather_small("ag_mod", mod_part.reshape(L * MOD_ROWS, MOD_SHARD))
    mod_all = jnp.transpose(mod_all.reshape(N_DEV, L, MOD_ROWS, MOD_SHARD), (1, 2, 0, 3)).reshape(L, MOD_ROWS, 6 * D)
    own_row = lax.dynamic_index_in_dim(mod_all, me, axis=1, keepdims=False)
    modrows = jnp.stack([mod_all[:, N_DEV], own_row], axis=1)

    shards = [{k: (P[src][l].T if tr else P[src][l]).astype(BF16) for k, src, tr in zip(BIG, BIG_SRC, BIG_T)}
              for l in range(L)]
    win0, = _allgather_hbm("ag_w_in0", [shards[0]["win_t"]])
    Ws = []
    for l in range(L):
        W = {"win_t": win0.reshape(-1, D)} if l == 0 else {}
        W.update(
            cw=conv_w_f[l], cb=conv_b[l][None],
            w4=jnp.concatenate([lru_wa[l, 0], lru_wa[l, 1], lru_wx[l, 0], lru_wx[l, 1]], axis=-1).astype(BF16),
            b4=jnp.concatenate([lru_ba_f[l, 0].reshape(N_RNN_BLOCKS, 1, RB), lru_ba_f[l, 1].reshape(N_RNN_BLOCKS, 1, RB),
                                lru_bx_f[l, 0].reshape(N_RNN_BLOCKS, 1, RB), lru_bx_f[l, 1].reshape(N_RNN_BLOCKS, 1, RB)],
                               axis=-1),
            lam=lru_lam_f[l], sink4=jnp.broadcast_to(attn_sink[l].reshape(N_KV, Q_PER_KV, 1), (N_KV, Q_PER_KV, HEAD)),
            g_mix_pre=g_mix_pre[l][None], g_mix_post=g_mix_post[l][None], g_ffn_pre=g_ffn_pre[l][None],
            g_ffn_post=g_ffn_post[l][None], mod=modrows[l])
        Ws.append(W)

    xa = jnp.concatenate([ctx[0], x[0]], axis=0)
    plan = _Plan(shards, Ws)
    sq, dxa, Gs = _local_step(xa, loss_target[0], Ws, S, plan)
    loss = lax.psum((0.5 / D) * jnp.sum(sq), ("x", "y", "c"))
    grad_x = dxa[CTX:][None]

    dmod = jnp.concatenate([Gs[l]["mod"] for l in range(L)] + [jnp.zeros((8 - 2 * L, 6 * D), F32)], axis=0)
    dmod_all = _allgather_small("ag_dmod", dmod)
    dmod_cols = lax.dynamic_slice_in_dim(dmod_all, me * MOD_SHARD, MOD_SHARD, axis=2)
    g_w_mod, g_b_mod, dsc_part = _mod_bwd("mod_bwd", c9, w_mod, dmod_all, dmod_cols)
    g_b_mod = g_b_mod[:, 0]

    def rows(name, shape):
        return jnp.concatenate([Gs[l][name].reshape(shape) for l in range(L)], axis=0)

    b4g = [Gs[l]["b4"].reshape(N_RNN_BLOCKS, 4, RB) for l in range(L)]
    sink_row = jnp.concatenate([Gs[l]["sink4"][:, :, 0].reshape(1, N_Q) for l in range(L)]
                               + [jnp.zeros((1, D - L * N_Q), F32)], axis=1)
    small_g = jnp.concatenate(
        [rows("g_mix_pre", (1, D)), rows("g_mix_post", (1, D)), rows("g_ffn_pre", (1, D)), rows("g_ffn_post", (1, D)),
         rows("cb", (1, D)), rows("cw", (CONV_W, D))]
        + [b4g[l][:, d].reshape(1, D) for l in range(L) for d in range(2)]
        + [b4g[l][:, 2 + d].reshape(1, D) for l in range(L) for d in range(2)]
        + [rows("lam", (2, D)), sink_row, dsc_part], axis=0)
    n_small = small_g.shape[0]
    small_tot = _sum_blocks("sum_small", _allgather_small("ag_small_grads", small_g))
    o = 0
    G = {}
    for name in ("g_mix_pre", "g_mix_post", "g_ffn_pre", "g_ffn_post", "conv_b"):
        G[name] = small_tot[o:o + L]
        o += L
    G["conv_w"] = small_tot[o:o + L * CONV_W].reshape(L, CONV_W, D)
    o += L * CONV_W
    for name in ("lru_ba", "lru_bx", "lru_lam"):
        G[name] = small_tot[o:o + 2 * L].reshape(L, 2, D)
        o += 2 * L
    G["attn_sink"] = small_tot[o, :L * N_Q].reshape(L, N_Q)
    sg = jax.nn.sigmoid(c_ctx)
    G["c_ctx"] = small_tot[o + 1] * (sg * (1.0 + c_ctx * (1.0 - sg)))
    G["b_mod"] = g_b_mod
    G["w_mod"] = g_w_mod

    last_slots, = _exchange_shards("exchange_w_in0", [[Gs[0]["win_t_b"].reshape(N_DEV, -1, D // 2)]], 1)
    plan.slots[0]["win_t"] = [plan.slots[0]["win_t_a"], last_slots[0]]

    out_g, out_d, out_m, out_v = {}, {}, {}, {}

    def put(name, res, shape=None):
        g, d, m, v = res
        for dst, val in ((out_g, g), (out_d, d), (out_m, m), (out_v, v)):
            dst[name] = val if shape is None else val.reshape(shape)

    for k, src, tr, tile in zip(BIG, BIG_SRC, BIG_T, BIG_TILE):
        lay = (lambda a: jnp.swapaxes(a, 1, 2)) if tr else (lambda a: a)
        res = _adamw_slots("adamw_" + src, [plan.slots[l][k] for l in range(L)], lay(P[src]), lay(Mo[src]),
                           lay(Vo[src]), tile)
        put(src, [lay(r) for r in res])
    res = _adamw("adamw_w_mod", w_mod.reshape(L * D, MOD_SHARD), g_w_mod.reshape(L * D, MOD_SHARD),
                 m_w_mod.reshape(L * D, MOD_SHARD), v_w_mod.reshape(L * D, MOD_SHARD), 256)
    put("w_mod", (g_w_mod,) + tuple(res), w_mod.shape)
    def fuse4(wa, wx):
        return jnp.concatenate([wa[:, 0], wa[:, 1], wx[:, 0], wx[:, 1]], axis=-1).reshape(L, N_RNN_BLOCKS * RB, 4 * RB)

    res = _adamw_slots("adamw_gates", plan.gate_slots,
                       fuse4(lru_wa, lru_wx), fuse4(m_lru_wa, m_lru_wx), fuse4(v_lru_wa, v_lru_wx), 256)
    res = [r.reshape(L, N_RNN_BLOCKS, RB, 4, RB) for r in res]
    put("lru_wa", [jnp.stack([r[:, :, :, 0], r[:, :, :, 1]], axis=1) for r in res])
    put("lru_wx", [jnp.stack([r[:, :, :, 2], r[:, :, :, 3]], axis=1) for r in res])
    rep = ("g_mix_pre", "g_mix_post", "g_ffn_pre", "g_ffn_post", "conv_b", "b_mod")

    def pack_rep(T_):
        sink = jnp.concatenate([T_["attn_sink"].reshape(1, L * N_Q), jnp.zeros((1, D - L * N_Q), F32)], axis=1)
        return jnp.concatenate([T_[n].reshape(-1, D) for n in rep] + [sink, T_["c_ctx"][None]], axis=0)

    pk = [pack_rep(T_) for T_ in (P, G, Mo, Vo)]
    n_rep = pk[0].shape[0]
    res = _adamw("adamw_replicated", *[jnp.pad(a, ((0, 24 - n_rep), (0, 0))) for a in pk], 24)
    res = (pk[1],) + tuple(r[:n_rep] for r in res)
    o = 0
    for n in rep:
        k = P[n].size // D
        put(n, [r[o:o + k] for r in res], P[n].shape)
        o += k
    put("attn_sink", [r[o, :L * N_Q] for r in res], attn_sink.shape)
    put("c_ctx", [r[o + 1] for r in res], c_ctx.shape)
    chan = ("conv_w", "lru_ba", "lru_bx", "lru_lam")
    g_own = {n: lax.dynamic_slice_in_dim(G[n], me * RB, RB, axis=2) for n in chan}

    def pack_chan(T_):
        return jnp.concatenate([T_[n].reshape(-1, RB) for n in chan], axis=0)

    pk = [pack_chan(T_) for T_ in (P, g_own, Mo, Vo)]
    n_ch = pk[0].shape[0]
    res = _adamw("adamw_channels", *[jnp.pad(a, ((0, 24 - n_ch), (0, 0))) for a in pk], 24)
    res = (pk[1],) + tuple(r[:n_ch] for r in res)
    o = 0
    for n in chan:
        k = P[n].size // RB
        put(n, [r[o:o + k] for r in res], P[n].shape)
        o += k

    order = ("c_ctx", "w_mod", "b_mod", "g_mix_pre", "g_mix_post", "g_ffn_pre", "g_ffn_post", "w_in", "conv_w", "conv_b",
             "lru_wa", "lru_ba", "lru_wx", "lru_bx", "lru_lam", "attn_sink", "w_o_rnn", "w_o_attn", "w_out", "w_ffn_in",
             "w_ffn_out")
    return (loss, grad_x, *[out_g[n] for n in order], *[out_d[n] for n in order], *[out_m[n] for n in order],
            *[out_v[n] for n in order])
```

```python
import functools
import math

import numpy as np
import jax
import jax.numpy as jnp
from jax import lax
from jax.experimental import pallas as pl
from jax.experimental.pallas import tpu as pltpu

F32 = jnp.float32
BF16 = jnp.bfloat16

D = 1024
CTX = 256
TR = 256
HEAD = 128
N_Q = 8
N_KV = 2
Q_PER_KV = N_Q // N_KV
GRID_W = 64
N_FREQ = HEAD // 4
ROPE_BASE = 10000.0
N_RNN_BLOCKS = 8
CONV_W = 4
CONV_LEFT = 2
LRU_C = 8.0
D_FF = 2816
IN_W = 5632
P_W = IN_W
DP_W = 3584
COL_XR, COL_GR, COL_Q, COL_K, COL_V, COL_GL = 0, 1024, 2048, 3072, 3328, 3584
GLB = 512
EPS = 1e-6
NEG_INF = -1e30
ATT_SCALE = HEAD ** -0.5
N_DEV = 8
VMEM_LIMIT = 56 * 1024 * 1024

ADAM_LR, ADAM_B1, ADAM_B2, ADAM_EPS, ADAM_WD, ADAM_STEP = 0.001, 0.9, 0.999, 1e-08, 0.01, 10

NN = (((1,), (0,)), ((), ()))
NT = (((1,), (1,)), ((), ()))
TN = (((0,), (0,)), ((), ()))


def _dot(a, b, dims=NN):
    return lax.dot_general(a, b, dims, preferred_element_type=F32)


def _params(sem=("arbitrary",)):
    return pltpu.CompilerParams(dimension_semantics=sem, vmem_limit_bytes=VMEM_LIMIT)


def _full_spec(shape):
    nd = len(shape)
    return pl.BlockSpec(shape, lambda *_: (0,) * nd)


ANY = pl.BlockSpec(memory_space=pl.ANY)


def _ew(name, body, n, row_ins, pars, row_outs, accs=(), alias=None):
    n_ri, n_p, n_ro, n_acc = len(row_ins), len(pars), len(row_outs), len(accs)

    def kern(*refs):
        i = pl.program_id(0)
        ins = refs[:n_ri]
        ps = refs[n_ri:n_ri + n_p]
        outs = refs[n_ri + n_p:n_ri + n_p + n_ro]
        acc = refs[n_ri + n_p + n_ro:]
        if n_acc:
            @pl.when(i == 0)
            def _():
                for a in acc:
                    a[...] = jnp.zeros(a.shape, a.dtype)
        body(i, ins, ps, outs, acc)

    in_specs = [ANY if blk is None else pl.BlockSpec(blk, imap) for (_, blk, imap) in row_ins]
    in_specs += [_full_spec(p.shape) for p in pars]
    out_specs = [pl.BlockSpec(blk, imap) for (_, blk, imap) in row_outs] + [_full_spec(a.shape) for a in accs]
    out_shape = [s for (s, _, _) in row_outs] + list(accs)
    return pl.pallas_call(
        kern, name=name, grid=(n,), in_specs=in_specs, out_specs=out_specs, out_shape=out_shape,
        input_output_aliases=alias or {}, compiler_params=_params(),
    )(*[a for (a, _, _) in row_ins], *pars)


def _rowblk(width, colblk=0, roff=0, tile=TR):
    return (tile, width), (lambda i: (i + roff, colblk))


def _sds(shape, dtype):
    return jax.ShapeDtypeStruct(shape, dtype)


class _Carry:
    SAME_CORE = (1, 3, 5)

    def __init__(self, jobs):
        self.jobs = list(jobs)
        self.arrays = [a for _, a in self.jobs]
        self.out_shapes = [_sds(a.shape if kind == "scatter" else (N_DEV, *a.shape), a.dtype) for kind, a in self.jobs]
        n = len(self.jobs)
        self.scratch = [pltpu.SemaphoreType.DMA((n, 7)), pltpu.SemaphoreType.DMA((n, 7)), pltpu.SemaphoreType.DMA((n,))]

    def _setup(self, sems):
        send_sems, recv_sems, local_sems = sems
        x, y, c = _place()
        me = _lin(x, y, c)
        peers = [(x ^ ((k + 1) >> 2 & 1), y ^ ((k + 1) >> 1 & 1), c ^ ((k + 1) & 1)) for k in range(7)]

        def copy(a, k, sem_k, src, dst):
            return pltpu.make_async_remote_copy(src_ref=src, dst_ref=dst, send_sem=send_sems.at[a, sem_k],
                                                recv_sem=recv_sems.at[a, sem_k], device_id=peers[k], device_id_type=MESH)

        return me, [_lin(*p) for p in peers], copy, local_sems

    def _local(self, a, kind, ins, outs, me, local_sems):
        return pltpu.make_async_copy(ins[a].at[me] if kind == "scatter" else ins[a], outs[a].at[me], local_sems.at[a])

    def start(self, ins, outs, sems):
        me, theirs, copy, local_sems = self._setup(sems)
        for a, (kind, _) in enumerate(self.jobs):
            self._local(a, kind, ins, outs, me, local_sems).start()
            if kind == "scatter":
                for k in range(7):
                    copy(a, k, k, ins[a].at[theirs[k]], outs[a].at[me]).start()
            else:
                for k in (0,) + self.SAME_CORE:
                    copy(a, k, k, ins[a], outs[a].at[me]).start()

    def wait(self, ins, outs, sems):
        me, theirs, copy, local_sems = self._setup(sems)
        for a, (kind, _) in enumerate(self.jobs):
            if kind == "scatter":
                for k in range(7):
                    copy(a, k, k, ins[a].at[me], outs[a].at[theirs[k]]).wait_recv()
                for k in range(7):
                    copy(a, k, k, ins[a].at[theirs[k]], outs[a].at[me]).wait_send()
            else:
                for k in self.SAME_CORE:
                    blk = outs[a].at[theirs[k]]
                    copy(a, k, k, ins[a], blk).wait_recv()
                    copy(a, 0, k + 1, blk, blk).start()
                copy(a, 0, 0, ins[a], outs[a].at[theirs[0]]).wait_recv()
                for k in self.SAME_CORE:
                    copy(a, 0, k + 1, ins[a], outs[a].at[theirs[k + 1]]).wait_recv()
                for k in (0,) + self.SAME_CORE:
                    copy(a, k, k, ins[a], outs[a].at[me]).wait_send()
                for k in self.SAME_CORE:
                    blk = outs[a].at[theirs[k]]
                    copy(a, 0, k + 1, blk, blk).wait_send()
            self._local(a, kind, ins, outs, me, local_sems).wait()


def _carried(kern, carry, n_in, n_out, first, last):
    if carry is None:
        return kern
    nc = len(carry.jobs)

    def wrapped(*refs):
        ins, cin = refs[:n_in], refs[n_in:n_in + nc]
        outs, cout = refs[n_in + nc:n_in + nc + n_out], refs[n_in + nc + n_out:n_in + 2 * nc + n_out]
        scr, sems = refs[n_in + 2 * nc + n_out:-3], refs[-3:]

        @pl.when(first())
        def _():
            carry.start(cin, cout, sems)

        kern(*ins, *outs, *scr)

        @pl.when(last())
        def _():
            carry.wait(cin, cout, sems)

    return wrapped


def _carry_args(carry):
    if carry is None:
        return [], [], [], [], []
    n = len(carry.jobs)
    return [ANY] * n, carry.arrays, [ANY] * n, carry.out_shapes, carry.scratch


def _grid_ends(dims):
    first = lambda: functools.reduce(jnp.logical_and, [pl.program_id(d) == 0 for d in range(len(dims))])
    last = lambda: functools.reduce(jnp.logical_and, [pl.program_id(d) == n - 1 for d, n in enumerate(dims)])
    return first, last


def _mm_call(name, a, b, mode, out_dtype, tm, tn, rows_outer=True, single_b=False, carry=None):
    if mode == "TN":
        (K, M), N = a.shape, b.shape[1]
    else:
        (M, K), N = a.shape, (b.shape[1] if mode == "NN" else b.shape[0])
    assert M % tm == 0 and N % tn == 0, (name, M, N, K, tm, tn)
    ij = (lambda g0, g1: (g0, g1)) if rows_outer else (lambda g0, g1: (g1, g0))
    grid = (M // tm, N // tn) if rows_outer else (N // tn, M // tm)
    if mode == "TN":
        a_spec = pl.BlockSpec((K, tm), lambda g0, g1: (0, ij(g0, g1)[0]))
    else:
        a_spec = pl.BlockSpec((tm, K), lambda g0, g1: (ij(g0, g1)[0], 0))
    b_blk, b_map = ((tn, K), lambda g0, g1: (ij(g0, g1)[1], 0)) if mode == "NT" else \
                   ((K, tn), lambda g0, g1: (0, ij(g0, g1)[1]))
    b_spec = pl.BlockSpec(b_blk, b_map, pipeline_mode=pl.Buffered(1)) if single_b else pl.BlockSpec(b_blk, b_map)
    dims = {"NN": NN, "NT": NT, "TN": TN}[mode]

    def kern(a_ref, b_ref, o_ref):
        o_ref[...] = _dot(a_ref[...], b_ref[...], dims).astype(o_ref.dtype)

    ci, ca, co, cs, cscr = _carry_args(carry)
    res = pl.pallas_call(
        _carried(kern, carry, 2, 1, *_grid_ends(grid)), name=name, grid=grid, in_specs=[a_spec, b_spec] + ci,
        out_specs=[pl.BlockSpec((tm, tn), lambda g0, g1: ij(g0, g1))] + co,
        out_shape=[_sds((M, N), out_dtype)] + cs, scratch_shapes=cscr,
        compiler_params=_params(("arbitrary", "arbitrary")),
    )(a, b, *ca)
    return res[0] if carry is None else (res[0], res[1:])


def _mm_act(name, a, w, mode, out_dtype=BF16, carry=None):
    rows, K = a.shape
    N = w.shape[1] if mode == "NN" else w.shape[0]
    if K > D_FF:
        return _mm_call(name, a, w, mode, out_dtype, rows // 8, N, single_b=True, carry=carry)
    tn = N if N <= 1024 else 1408
    return _mm_call(name, a, w, mode, out_dtype, rows // 4, tn, carry=carry)


def _mm_wgrad(name, x, dy, out_dtype=BF16, carry=None):
    M = x.shape[1]
    tm = 1408 if M == D_FF else 512
    return _mm_call(name, x, dy, "TN", out_dtype, tm, dy.shape[1], single_b=True, carry=carry)


def _sigmoid(x):
    return 0.5 * jnp.tanh(0.5 * x) + 0.5


def _silu(x):
    return x * _sigmoid(x)


def _silu_grad(x):
    s = _sigmoid(x)
    return s * (1.0 + x * (1.0 - s))


_GELU_K = math.sqrt(2.0 / math.pi)


def _gelu(x):
    return 0.5 * x * (1.0 + jnp.tanh(_GELU_K * (x + 0.044715 * x * x * x)))


def _gelu_grad(x):
    t = jnp.tanh(_GELU_K * (x + 0.044715 * x * x * x))
    return 0.5 * (1.0 + t) + 0.5 * x * (1.0 - t * t) * _GELU_K * (1.0 + 3.0 * 0.044715 * x * x)


def _log_sigmoid(x):
    return jnp.minimum(x, 0.0) - jnp.log(1.0 + jnp.exp(-jnp.abs(x)))


def _rms(x):
    x = x.astype(F32)
    r = lax.rsqrt(jnp.mean(x * x, axis=-1, keepdims=True) + EPS)
    return x * r, r


def _rms_bwd(dy, y, r):
    return r * (dy - y * jnp.mean(dy * y, axis=-1, keepdims=True))


def _modrow(mod_ref, i, chunk):
    lo = mod_ref[0:1, chunk * D:(chunk + 1) * D]
    hi = mod_ref[1:2, chunk * D:(chunk + 1) * D]
    return jnp.where(i == 0, lo, hi)


def _acc_seg(acc_ref, i, val):
    zero = jnp.zeros_like(val)
    acc_ref[0:1, :] += jnp.where(i == 0, val, zero)
    acc_ref[1:2, :] += jnp.where(i == 0, zero, val)


def _colsum(x):
    return jnp.sum(x, axis=0, keepdims=True)


SH1, SC1, GA1, SH2, SC2, GA2 = range(6)


def _normmod_fwd(name, xa, g, mod, c_sh, c_sc):
    T = xa.shape[0]

    def body(i, ins, ps, outs, acc):
        y, _ = _rms(ins[0][...])
        h = (y * ps[0][...]) * (1.0 + _modrow(ps[1], i, c_sc)) + _modrow(ps[1], i, c_sh)
        outs[0][...] = h.astype(BF16)

    return _ew(name, body, T // TR, [(xa, *_rowblk(D))], [g, mod], [(_sds((T, D), BF16), *_rowblk(D))])[0]


def _modrows(mod_ref, row0, n, chunk):
    t = row0 + lax.broadcasted_iota(jnp.int32, (n, 1), 0)
    return jnp.where(t < CTX, mod_ref[0:1, chunk * D:(chunk + 1) * D], mod_ref[1:2, chunk * D:(chunk + 1) * D])


def _loss_resid_bwd(name, x_out, target, mat, gpost, mod, c_ga):
    T = x_out.shape[0]

    def body(i, ins, ps, outs, acc):
        err = ins[0][...] - ins[1][...]
        lat = i > 0
        dx = jnp.where(lat, err * (1.0 / D), 0.0)
        outs[0][...] = dx
        acc[2][...] += jnp.where(lat, _colsum(err * err), 0.0)
        outs[1][...] = _resid_bwd_vals(i, dx, ins[2][...], ps[0][...], ps[1], c_ga, acc[0], acc[1]).astype(BF16)

    tgt_blk = ((TR, D), lambda i: (jnp.maximum(i - 1, 0), 0))
    return _ew(name, body, T // TR, [(x_out, *_rowblk(D)), (target, *tgt_blk), (mat, *_rowblk(D))], [gpost, mod],
               [(_sds((T, D), F32), *_rowblk(D)), (_sds((T, D), BF16), *_rowblk(D))],
               [_sds((2, D), F32), _sds((1, D), F32), _sds((1, D), F32)])


def _mod_for(mod_ref, i, chunk, row0, n):
    return _modrow(mod_ref, i, chunk) if row0 is None else _modrows(mod_ref, row0, n, chunk)


def _acc_for(acc_ref, i, v, row0):
    if row0 is None:
        _acc_seg(acc_ref, i, _colsum(v))
        return

    @pl.when(row0 < CTX)
    def _():
        is_ctx = row0 + lax.broadcasted_iota(jnp.int32, (v.shape[0], 1), 0) < CTX
        acc_ref[0:1, :] += _colsum(jnp.where(is_ctx, v, 0.0))
        acc_ref[1:2, :] += _colsum(jnp.where(is_ctx, 0.0, v))

    @pl.when(row0 >= CTX)
    def _():
        acc_ref[1:2, :] += _colsum(v)


def _resid_bwd_vals(i, dout, mat, gpost, mod_ref, c_ga, acc_ga, acc_g, row0=None):
    ym, rm = _rms(mat)
    ga = _mod_for(mod_ref, i, c_ga, row0, dout.shape[0])
    _acc_for(acc_ga, i, dout * (ym * gpost), row0)
    dn = dout * ga
    acc_g[...] += _colsum(dn * ym)
    return _rms_bwd(dn * gpost, ym, rm)


def _normmod_bwd_vals(i, dh, xin, g, mod_ref, c_sh, c_sc, acc_sh, acc_sc, acc_g, row0=None):
    dh = dh.astype(F32)
    y, r = _rms(xin)
    _acc_for(acc_sc, i, dh * (y * g), row0)
    _acc_for(acc_sh, i, dh, row0)
    dyg = dh * (1.0 + _mod_for(mod_ref, i, c_sc, row0, dh.shape[0]))
    acc_g[...] += _colsum(dyg * y)
    return _rms_bwd(dyg * g, y, r)


def _parts(i, tm):
    return [(slice(0, tm), i * tm)]


FT = 1408


def _ffn_in_fused(name, h2, w_t, carry=None):
    T = h2.shape[0]
    tm, nj = T // 4, D_FF // FT

    def kern(a_ref, bg_ref, bu_ref, fg_ref, fu_ref, s_ref):
        for rows, _ in _parts(0, tm):
            a = a_ref[rows, :]
            g = _dot(a, bg_ref[...], NT)
            u = _dot(a, bu_ref[...], NT)
            fg_ref[rows, :] = g.astype(BF16)
            fu_ref[rows, :] = u.astype(BF16)
            s_ref[rows, :] = (_silu(g) * u).astype(BF16)

    o_spec = pl.BlockSpec((tm, FT), lambda i, j: (i, j))
    ci, ca, co, cs, cscr = _carry_args(carry)
    res = pl.pallas_call(
        _carried(kern, carry, 3, 3, *_grid_ends((4, nj))), name=name, grid=(4, nj),
        in_specs=[pl.BlockSpec((tm, D), lambda i, j: (i, 0)), pl.BlockSpec((FT, D), lambda i, j: (j, 0)),
                  pl.BlockSpec((FT, D), lambda i, j: (j + nj, 0))] + ci,
        out_specs=[o_spec] * 3 + co, out_shape=[_sds((T, D_FF), BF16)] * 3 + cs, scratch_shapes=cscr,
        compiler_params=_params(("arbitrary", "arbitrary")),
    )(h2, w_t, w_t, *ca)
    return res if carry is None else (res[:3], res[3:])


def _norm_chain(row0, xin, mat, gpost, mod_ref, c_ga, gnext, modn_ref, c_sh, c_sc):
    n = xin.shape[0]
    ym, _ = _rms(mat.astype(BF16))
    xo = xin + _modrows(mod_ref, row0, n, c_ga) * (ym * gpost)
    y, _ = _rms(xo)
    h = (y * gnext) * (1.0 + _modrows(modn_ref, row0, n, c_sc)) + _modrows(modn_ref, row0, n, c_sh)
    return xo, h.astype(BF16)


def _out_fused(name, p, u, o_all, xa, w_o_rnn, w_o_attn, w_out, gpost, mod, gnext):
    T = u.shape[0]
    tm = T // 8

    def kern(g0, g1, g2, g3, u_ref, o_ref, xa_ref, wr_ref, wa_ref, w_ref, gpost_ref, mod_ref, gnext_ref,
             ya_ref, yb_ref, z_ref, m_ref, x1_ref, h2_ref):
        for rows, row0 in _parts(pl.program_id(0), tm):
            ya = _dot(u_ref[rows, :], wr_ref[...]).astype(BF16)
            yb = _dot(o_ref[rows, :], wa_ref[...]).astype(BF16)
            ya_ref[rows, :] = ya
            yb_ref[rows, :] = yb
            ga = _sigmoid(jnp.concatenate([g0[rows, :], g1[rows, :]], axis=1).astype(F32))
            gb = _sigmoid(jnp.concatenate([g2[rows, :], g3[rows, :]], axis=1).astype(F32))
            z = (ga * ya.astype(F32) + gb * yb.astype(F32)).astype(BF16)
            z_ref[rows, :] = z
            m = _dot(z, w_ref[...])
            m_ref[rows, :] = m.astype(BF16)
            x1_ref[rows, :], h2_ref[rows, :] = _norm_chain(row0, xa_ref[rows, :], m, gpost_ref[...], mod_ref, GA1,
                                                           gnext_ref[...], mod_ref, SH2, SC2)

    row = lambda w: pl.BlockSpec((tm, w), lambda i: (i, 0))
    return pl.pallas_call(
        kern, name=name, grid=(T // tm,),
        in_specs=[pl.BlockSpec((tm, GLB), lambda i, q=q: (i, COL_GL // GLB + q)) for q in range(4)]
                 + [row(D), row(D), row(D)] + [_full_spec(a.shape) for a in (w_o_rnn, w_o_attn, w_out, gpost, mod, gnext)],
        out_specs=[row(D)] * 6,
        out_shape=[_sds((T, D), BF16)] * 4 + [_sds((T, D), F32), _sds((T, D), BF16)],
        compiler_params=_params(),
    )(p, p, p, p, u, o_all, xa, w_o_rnn, w_o_attn, w_out, gpost, mod, gnext)


def _ffn_out_fused(name, s, w, x1, gpost, mod, nxt=None):
    T = s.shape[0]
    tm = T // 8

    def kern(s_ref, w_ref, x1_ref, gpost_ref, mod_ref, *rest):
        for rows, row0 in _parts(pl.program_id(0), tm):
            e = _dot(s_ref[rows, :], w_ref[...])
            if nxt is None:
                e_ref, xo_ref = rest
                ym, _ = _rms(e.astype(BF16))
                xo_ref[rows, :] = x1_ref[rows, :] + _modrows(mod_ref, row0, e.shape[0], GA2) * (ym * gpost_ref[...])
            else:
                gnext_ref, modn_ref, e_ref, xo_ref, h_ref = rest
                xo_ref[rows, :], h_ref[rows, :] = _norm_chain(row0, x1_ref[rows, :], e, gpost_ref[...], mod_ref, GA2,
                                                              gnext_ref[...], modn_ref, SH1, SC1)
            e_ref[rows, :] = e.astype(BF16)

    row = lambda w_: pl.BlockSpec((tm, w_), lambda i: (i, 0))
    extra = [] if nxt is None else list(nxt)
    return pl.pallas_call(
        kern, name=name, grid=(T // tm,),
        in_specs=[row(D_FF), _full_spec(w.shape), row(D), _full_spec(gpost.shape), _full_spec(mod.shape)]
                 + [_full_spec(a.shape) for a in extra],
        out_specs=[row(D)] * (2 if nxt is None else 3),
        out_shape=[_sds((T, D), BF16), _sds((T, D), F32)] + ([] if nxt is None else [_sds((T, D), BF16)]),
        compiler_params=_params(),
    )(s, w, x1, gpost, mod, *extra)


def _ffn_bwd_fused(name, fg, fu, w, de=None, head=None):
    T = fg.shape[0]
    tm = T // 8
    row = lambda w_: pl.BlockSpec((tm, w_), lambda i: (i, 0))
    w_spec = pl.BlockSpec(w.shape, lambda i: (0, 0), pipeline_mode=pl.Buffered(1))

    def tail(rows, de_v, fg_ref, fu_ref, w_ref, df_ref):
        ds = _dot(de_v, w_ref[...], NT)
        g, u = fg_ref[rows, :].astype(F32), fu_ref[rows, :].astype(F32)
        df_ref[rows, :] = jnp.concatenate([ds * u * _silu_grad(g), ds * _silu(g)], axis=1).astype(BF16)

    if head is None:
        def kern(de_ref, fg_ref, fu_ref, w_ref, df_ref):
            for rows, _ in _parts(pl.program_id(0), tm):
                tail(rows, de_ref[rows, :], fg_ref, fu_ref, w_ref, df_ref)

        return pl.pallas_call(
            kern, name=name, grid=(T // tm,), in_specs=[row(D), row(D_FF), row(D_FF), w_spec],
            out_specs=[row(2 * D_FF)], out_shape=[_sds((T, 2 * D_FF), BF16)], compiler_params=_params(),
        )(de, fg, fu, w)

    dx2, e, gpost, mod = head

    def kern(dx_ref, e_ref, fg_ref, fu_ref, w_ref, gpost_ref, mod_ref, de_ref, df_ref, dga_ref, dg_ref):
        i = pl.program_id(0)

        @pl.when(i == 0)
        def _():
            dga_ref[...] = jnp.zeros(dga_ref.shape, F32)
            dg_ref[...] = jnp.zeros(dg_ref.shape, F32)

        for rows, row0 in _parts(i, tm):
            de_v = _resid_bwd_vals(i, dx_ref[rows, :], e_ref[rows, :], gpost_ref[...], mod_ref, GA2, dga_ref, dg_ref,
                                   row0=row0).astype(BF16)
            de_ref[rows, :] = de_v
            tail(rows, de_v, fg_ref, fu_ref, w_ref, df_ref)

    return pl.pallas_call(
        kern, name=name, grid=(T // tm,),
        in_specs=[row(D), row(D), row(D_FF), row(D_FF), w_spec, _full_spec(gpost.shape), _full_spec(mod.shape)],
        out_specs=[row(D), row(2 * D_FF), _full_spec((2, D)), _full_spec((1, D))],
        out_shape=[_sds((T, D), BF16), _sds((T, 2 * D_FF), BF16), _sds((2, D), F32), _sds((1, D), F32)],
        compiler_params=_params(),
    )(dx2, e, fg, fu, w, gpost, mod)


def _zero_at_start(i, refs):
    @pl.when(i == 0)
    def _():
        for r in refs:
            r[...] = jnp.zeros(r.shape, F32)


def _proj_bwd_fused(name, dp, dgl, w_in_t, xa, dx1, gpre, mod, carry=None):
    T = dp.shape[0]
    tm = T // 8
    row = lambda w_: pl.BlockSpec((tm, w_), lambda i: (i, 0))

    def kern(dp_ref, dgl_ref, w_ref, xa_ref, dx1_ref, g_ref, mod_ref, dxa_ref, dsh_ref, dsc_ref, dg_ref):
        i = pl.program_id(0)
        _zero_at_start(i, (dsh_ref, dsc_ref, dg_ref))
        for rows, row0 in _parts(i, tm):
            dh = _dot(dp_ref[rows, :], w_ref[0:DP_W, :]) + _dot(dgl_ref[rows, :], w_ref[DP_W:, :])
            dxa_ref[rows, :] = dx1_ref[rows, :] + _normmod_bwd_vals(i, dh, xa_ref[rows, :], g_ref[...], mod_ref, SH1,
                                                                    SC1, dsh_ref, dsc_ref, dg_ref, row0=row0)

    ci, ca, co, cs, cscr = _carry_args(carry)
    res = pl.pallas_call(
        _carried(kern, carry, 7, 4, *_grid_ends((T // tm,))), name=name, grid=(T // tm,),
        in_specs=[row(DP_W), row(P_W - DP_W),
                  pl.BlockSpec(w_in_t.shape, lambda i: (0, 0), pipeline_mode=pl.Buffered(1)), row(D), row(D),
                  _full_spec(gpre.shape), _full_spec(mod.shape)] + ci,
        out_specs=[row(D), _full_spec((2, D)), _full_spec((2, D)), _full_spec((1, D))] + co,
        out_shape=[_sds((T, D), F32), _sds((2, D), F32), _sds((2, D), F32), _sds((1, D), F32)] + cs,
        scratch_shapes=cscr, compiler_params=_params(),
    )(dp, dgl, w_in_t, xa, dx1, gpre, mod, *ca)
    return res if carry is None else (res[:4], res[4:])


def _proj_wgrad(name, dp, dgl, h, carry=None):
    T, N = h.shape
    n1, n2 = DP_W // GLB, (P_W - DP_W) // GLB

    def kern(a1_ref, a2_ref, h_ref, o_ref):
        i = pl.program_id(0)

        @pl.when(i < n1)
        def _():
            o_ref[...] = _dot(a1_ref[...], h_ref[...], TN).astype(o_ref.dtype)

        @pl.when(i >= n1)
        def _():
            o_ref[...] = _dot(a2_ref[...], h_ref[...], TN).astype(o_ref.dtype)

    ci, ca, co, cs, cscr = _carry_args(carry)
    res = pl.pallas_call(
        _carried(kern, carry, 3, 1, *_grid_ends((n1 + n2,))), name=name, grid=(n1 + n2,),
        in_specs=[pl.BlockSpec((T, GLB), lambda i: (0, jnp.minimum(i, n1 - 1))),
                  pl.BlockSpec((T, GLB), lambda i: (0, jnp.maximum(i - n1, 0))),
                  pl.BlockSpec((T, N), lambda i: (0, 0), pipeline_mode=pl.Buffered(1))] + ci,
        out_specs=[pl.BlockSpec((GLB, N), lambda i: (i, 0))] + co,
        out_shape=[_sds((P_W, N), BF16)] + cs, scratch_shapes=cscr, compiler_params=_params(),
    )(dp, dgl, h, *ca)
    return res[0] if carry is None else (res[0], res[1:])


def _ffn_in_bwd_fused(name, df, w_t, x1, dres, mat, gpre, mod, gpost, carry=None):
    T = df.shape[0]
    tm = T // 8
    row = lambda w_: pl.BlockSpec((tm, w_), lambda i: (i, 0))

    def kern(df_ref, w_ref, x1_ref, dres_ref, mat_ref, gpre_ref, mod_ref, gpost_ref,
             dx1_ref, dm_ref, dsh_ref, dsc_ref, dgpre_ref, dga_ref, dgpost_ref):
        i = pl.program_id(0)
        _zero_at_start(i, (dsh_ref, dsc_ref, dgpre_ref, dga_ref, dgpost_ref))
        for rows, row0 in _parts(i, tm):
            dh2 = _dot(df_ref[rows, :], w_ref[...])
            dx1 = dres_ref[rows, :] + _normmod_bwd_vals(i, dh2, x1_ref[rows, :], gpre_ref[...], mod_ref, SH2, SC2,
                                                        dsh_ref, dsc_ref, dgpre_ref, row0=row0)
            dx1_ref[rows, :] = dx1
            dm_ref[rows, :] = _resid_bwd_vals(i, dx1, mat_ref[rows, :], gpost_ref[...], mod_ref, GA1, dga_ref,
                                              dgpost_ref, row0=row0).astype(BF16)

    ci, ca, co, cs, cscr = _carry_args(carry)
    res = pl.pallas_call(
        _carried(kern, carry, 8, 7, *_grid_ends((T // tm,))), name=name, grid=(T // tm,),
        in_specs=[row(2 * D_FF), pl.BlockSpec(w_t.shape, lambda i: (0, 0), pipeline_mode=pl.Buffered(1)), row(D),
                  row(D), row(D), _full_spec(gpre.shape), _full_spec(mod.shape), _full_spec(gpost.shape)] + ci,
        out_specs=[row(D), row(D), _full_spec((2, D)), _full_spec((2, D)), _full_spec((1, D)), _full_spec((2, D)),
                   _full_spec((1, D))] + co,
        out_shape=[_sds((T, D), F32), _sds((T, D), BF16), _sds((2, D), F32), _sds((2, D), F32), _sds((1, D), F32),
                   _sds((2, D), F32), _sds((1, D), F32)] + cs,
        scratch_shapes=cscr, compiler_params=_params(),
    )(df, w_t, x1, dres, mat, gpre, mod, gpost, *ca)
    return res if carry is None else (res[:7], res[7:])


def _out_bwd_fused(name, dm, w_out, w_o_rnn, w_o_attn, p, ya, yb):
    T = dm.shape[0]
    tm = T // 8
    row = lambda w_: pl.BlockSpec((tm, w_), lambda i: (i, 0))

    def kern(dm_ref, w_ref, wr_ref, wa_ref, g0, g1, g2, g3, ya_ref, yb_ref, dya_ref, dyb_ref, dgl_ref, du_ref, do_ref):
        for rows, _ in _parts(pl.program_id(0), tm):
            dz = _dot(dm_ref[rows, :], w_ref[...], NT)
            ga = _sigmoid(jnp.concatenate([g0[rows, :], g1[rows, :]], axis=1).astype(F32))
            gb = _sigmoid(jnp.concatenate([g2[rows, :], g3[rows, :]], axis=1).astype(F32))
            dya = (dz * ga).astype(BF16)
            dyb = (dz * gb).astype(BF16)
            dya_ref[rows, :] = dya
            dyb_ref[rows, :] = dyb
            dgl_ref[rows, :] = jnp.concatenate([dz * ya_ref[rows, :].astype(F32) * ga * (1.0 - ga),
                                                dz * yb_ref[rows, :].astype(F32) * gb * (1.0 - gb)],
                                               axis=1).astype(BF16)
            du_ref[rows, :] = _dot(dya, wr_ref[...], NT).astype(BF16)
            do_ref[rows, :] = _dot(dyb, wa_ref[...], NT).astype(BF16)

    return pl.pallas_call(
        kern, name=name, grid=(T // tm,),
        in_specs=[row(D)] + [_full_spec(w.shape) for w in (w_out, w_o_rnn, w_o_attn)]
                 + [pl.BlockSpec((tm, GLB), lambda i, q=q: (i, COL_GL // GLB + q)) for q in range(4)] + [row(D), row(D)],
        out_specs=[row(D), row(D), row(2 * D), row(D), row(D)],
        out_shape=[_sds((T, D), BF16), _sds((T, D), BF16), _sds((T, 2 * D), BF16), _sds((T, D), BF16),
                   _sds((T, D), BF16)],
        compiler_params=_params(),
    )(dm, w_out, w_o_rnn, w_o_attn, p, p, p, p, ya, yb)


AB = 128
CTX_BLKS = CTX // AB


def _rope_tables(S):
    pos = jnp.arange(S, dtype=jnp.int32)
    inv = ROPE_BASE ** (-jnp.arange(N_FREQ, dtype=F32) / N_FREQ)
    ang_r = (pos // GRID_W).astype(F32)[:, None] * inv[None, :]
    ang_c = (pos % GRID_W).astype(F32)[:, None] * inv[None, :]
    cos = jnp.concatenate([jnp.cos(ang_r)] * 2 + [jnp.cos(ang_c)] * 2, axis=1)
    sin = jnp.concatenate([-jnp.sin(ang_r), jnp.sin(ang_r), -jnp.sin(ang_c), jnp.sin(ang_c)], axis=1)
    return cos, sin


def _rope(x, cos, sin):
    w = x.shape[1]
    reps = w // HEAD
    lane = lax.broadcasted_iota(jnp.int32, x.shape, 1)
    partner = jnp.where((lane & 63) < 32, pltpu.roll(x, w - 32, 1), pltpu.roll(x, 32, 1))
    return x * jnp.tile(cos, (1, reps)) + partner * jnp.tile(sin, (1, reps))


def _unrope(dx, cos, sin):
    w = dx.shape[1]
    reps = w // HEAD
    lane = lax.broadcasted_iota(jnp.int32, dx.shape, 1)
    t = dx * jnp.tile(sin, (1, reps))
    partner = jnp.where((lane & 63) < 32, pltpu.roll(t, w - 32, 1), pltpu.roll(t, 32, 1))
    return dx * jnp.tile(cos, (1, reps)) + partner


def _qkv_prep(name, p, cos, sin, S):
    T = CTX + S
    nt = T // AB
    KW = N_KV * HEAD

    def with_ones(v):
        ones = jnp.ones((AB, HEAD), BF16)
        return jnp.concatenate([v[:, kh * HEAD:(kh + 1) * HEAD] if part == 0 else ones
                                for kh in range(N_KV) for part in range(2)], axis=1)

    def kern(q_ref, k_ref, v_ref, cos_ref, sin_ref, qa_ref, kp_ref, vp_ref, kc_ref, vc_ref):
        i = pl.program_id(0)
        cos_v, sin_v = cos_ref[...], sin_ref[...]
        @pl.when(i < CTX_BLKS)
        def _():
            qa_ref[...] = (q_ref[...].astype(F32) * ATT_SCALE).astype(BF16)
            kc_ref[...] = k_ref[...]
            vc_ref[...] = with_ones(v_ref[...])

        @pl.when((i < CTX_BLKS) | (i >= nt))
        def _():
            kp_ref[...] = jnp.zeros(kp_ref.shape, BF16)
            vp_ref[...] = jnp.zeros(vp_ref.shape, BF16)

        @pl.when((i >= CTX_BLKS) & (i < nt))
        def _():
            qa_ref[...] = (_rope(q_ref[...].astype(F32), cos_v, sin_v) * ATT_SCALE).astype(BF16)
            kp_ref[...] = _rope(k_ref[...].astype(F32), cos_v, sin_v).astype(BF16)
            vp_ref[...] = with_ones(v_ref[...])

    tok = lambda i: jnp.minimum(i, nt - 1)
    lat_map = lambda i: (jnp.clip(i - CTX_BLKS, 0, nt - CTX_BLKS - 1), 0)
    ctx_map = lambda i: (jnp.minimum(i, CTX_BLKS - 1), 0)
    return pl.pallas_call(
        kern, name=name, grid=(nt + CTX_BLKS,),
        in_specs=[pl.BlockSpec((AB, N_Q * HEAD), lambda i: (tok(i), COL_Q // (N_Q * HEAD))),
                  pl.BlockSpec((AB, KW), lambda i: (tok(i), COL_K // KW)),
                  pl.BlockSpec((AB, KW), lambda i: (tok(i), COL_V // KW)),
                  pl.BlockSpec((AB, HEAD), lat_map), pl.BlockSpec((AB, HEAD), lat_map)],
        out_specs=[pl.BlockSpec((AB, N_Q * HEAD), lambda i: (tok(i), 0)),
                   pl.BlockSpec((AB, KW), lambda i: (i, 0)), pl.BlockSpec((AB, 2 * KW), lambda i: (i, 0)),
                   pl.BlockSpec((AB, KW), ctx_map), pl.BlockSpec((AB, 2 * KW), ctx_map)],
        out_shape=[_sds((T, N_Q * HEAD), BF16), _sds((S + 2 * CTX, KW), BF16), _sds((S + 2 * CTX, 2 * KW), BF16),
                   _sds((CTX, KW), BF16), _sds((CTX, 2 * KW), BF16)],
        compiler_params=_params(),
    )(p, p, p, cos, sin)


GW = Q_PER_KV * HEAD


def _band_bias(S):
    r = jnp.arange(AB, dtype=jnp.int32)[:, None]
    c = jnp.arange(3 * AB, dtype=jnp.int32)[None, :]
    near = jnp.abs(c - AB - r) <= AB
    valid = jnp.stack([near & (c >= AB), near, near & (c < 2 * AB)])
    return jnp.where(valid, 0.0, NEG_INF).astype(F32)


def _bias_spec(S):
    nb = S // AB
    return pl.BlockSpec((None, AB, 3 * AB), lambda kh, n: (jnp.where(n == 0, 0, jnp.where(n == nb - 1, 2, 1)), 0, 0))


def _head_probs(q, sink, kc, vce, kb, vbe, bias):
    s_c = _dot(q, kc, NT)
    m = jnp.maximum(jnp.max(s_c, axis=-1, keepdims=True), sink)
    if kb is not None:
        s_b = _dot(q, kb, NT) + bias
        m = jnp.maximum(m, jnp.max(s_b, axis=-1, keepdims=True))
    p_c = jnp.exp(s_c - m).astype(BF16)
    acc = _dot(p_c, vce)
    p_b = None
    if kb is not None:
        p_b = jnp.exp(s_b - m).astype(BF16)
        acc = acc + _dot(p_b, vbe)
    return p_c, p_b, m, acc


def _attn_fwd(name, qa, kc, vc, sink4, S, band=None, prev=None, carry=None):
    T = qa.shape[0]
    has_band = band is not None
    nq = S // AB if has_band else CTX_BLKS
    q_off = CTX_BLKS if has_band else 0

    def kern(*refs):
        q_ref, kc_ref, vc_ref, sink_ref = refs[:4]
        rest = refs[4:]
        o_ref = rest[-1]
        n = pl.program_id(1)
        kc_v, vce = kc_ref[...], vc_ref[...]
        kb = vbe = bias = None
        if has_band:
            kp_ref, vp_ref, bias_ref = rest[:3]
            start = pl.multiple_of(n * AB + (CTX - AB), AB)
            kb = kp_ref[pl.ds(start, 3 * AB), :]
            vbe = vp_ref[pl.ds(start, 3 * AB), :]
            bias = bias_ref[...]
        outs = []
        for g in range(Q_PER_KV):
            sink = sink_ref[g:g + 1, 0:1]
            _, _, m, acc = _head_probs(q_ref[:, g * HEAD:(g + 1) * HEAD], sink, kc_v, vce, kb, vbe, bias)
            l = acc[:, HEAD:] + jnp.exp(sink - m)
            outs.append(acc[:, :HEAD] / l)
        o_ref[...] = jnp.concatenate(outs, axis=1).astype(BF16)

    in_specs = [pl.BlockSpec((AB, GW), lambda kh, n: (n + q_off, kh)),
                pl.BlockSpec((CTX, HEAD), lambda kh, n: (0, kh)), pl.BlockSpec((CTX, 2 * HEAD), lambda kh, n: (0, kh)),
                pl.BlockSpec((None, Q_PER_KV, HEAD), lambda kh, n: (kh, 0, 0))]
    args = [qa, kc, vc, sink4]
    if has_band:
        in_specs += [pl.BlockSpec((S + 2 * CTX, HEAD), lambda kh, n: (0, kh)),
                     pl.BlockSpec((S + 2 * CTX, 2 * HEAD), lambda kh, n: (0, kh)), _bias_spec(S)]
        args += list(band)
    alias = {}
    if prev is not None:
        in_specs.append(ANY)
        alias = {len(args): 0}
        args.append(prev)
    ci, ca, co, cs, cscr = _carry_args(carry)
    res = pl.pallas_call(
        _carried(kern, carry, len(args), 1, *_grid_ends((N_KV, nq))), name=name, grid=(N_KV, nq),
        in_specs=in_specs + ci,
        out_specs=[pl.BlockSpec((AB, GW), lambda kh, n: (n + q_off, kh))] + co,
        out_shape=[_sds((T, N_Q * HEAD), BF16)] + cs, input_output_aliases=alias, scratch_shapes=cscr,
        compiler_params=_params(("arbitrary", "arbitrary")),
    )(*args, *ca)
    return res[0] if carry is None else (res[0], res[1:])


def _attn_bwd(name, qa, kc, vc, sink4, o_all, do_all, S, band=None, prev_dq=None, carry=None):
    T = qa.shape[0]
    has_band = band is not None
    nq = S // AB if has_band else CTX_BLKS
    q_off = CTX_BLKS if has_band else 0
    KW = N_KV * HEAD

    def kern(*refs):
        q_ref, kc_ref, vc_ref, sink_ref, o_ref, do_ref = refs[:6]
        rest = refs[6:]
        if has_band:
            kp_ref, vp_ref, bias_ref = rest[:3]
            rest = rest[3:]
        if prev_dq is not None:
            rest = rest[1:]
        dq_ref, dkc_ref, dvc_ref, dsink_ref = rest[:4]
        n = pl.program_id(1)

        @pl.when(n == 0)
        def _():
            dkc_ref[...] = jnp.zeros(dkc_ref.shape, F32)
            dvc_ref[...] = jnp.zeros(dvc_ref.shape, F32)
            dsink_ref[...] = jnp.zeros(dsink_ref.shape, F32)
            if has_band:
                rest[4][...] = jnp.zeros(rest[4].shape, F32)
                rest[5][...] = jnp.zeros(rest[5].shape, F32)

        kc_v, vce = kc_ref[...], vc_ref[...]
        vc_v = vce[:, :HEAD]
        kb = vbe = vb = bias = None
        if has_band:
            start = pl.multiple_of(n * AB + (CTX - AB), AB)
            kb = kp_ref[pl.ds(start, 3 * AB), :]
            vbe = vp_ref[pl.ds(start, 3 * AB), :]
            vb = vbe[:, :HEAD]
            bias = bias_ref[...]
        stack = lambda ref: jnp.concatenate([ref[:, g * HEAD:(g + 1) * HEAD] for g in range(Q_PER_KV)], axis=0)
        q4, do4 = stack(q_ref), stack(do_ref)
        sink = jnp.concatenate([jnp.broadcast_to(sink_ref[g:g + 1, 0:1], (AB, 1)) for g in range(Q_PER_KV)], axis=0)
        s_c = _dot(q4, kc_v, NT)
        m = jnp.maximum(jnp.max(s_c, axis=-1, keepdims=True), sink)
        if has_band:
            s_b = _dot(q4, kb, NT) + jnp.tile(bias, (Q_PER_KV, 1))
            m = jnp.maximum(m, jnp.max(s_b, axis=-1, keepdims=True))
        p_c = jnp.exp(s_c - m).astype(BF16).astype(F32)
        p_sink = jnp.exp(sink - m)
        l = jnp.sum(p_c, axis=-1, keepdims=True) + p_sink
        if has_band:
            p_b = jnp.exp(s_b - m).astype(BF16).astype(F32)
            l = l + jnp.sum(p_b, axis=-1, keepdims=True)
        inv = 1.0 / l
        delta = jnp.sum(do4.astype(F32) * stack(o_ref).astype(F32), axis=-1, keepdims=True)
        do4b = do4.astype(BF16)
        pn_c = (p_c * inv).astype(BF16)
        ds_c = (p_c * inv * (_dot(do4b, vc_v, NT) - delta)).astype(BF16)
        dq4 = _dot(ds_c, kc_v)
        dkc_ref[...] += _dot(ds_c, q4, TN)
        dvc_ref[...] += _dot(pn_c, do4b, TN)
        if has_band:
            pn_b = (p_b * inv).astype(BF16)
            ds_b = (p_b * inv * (_dot(do4b, vb, NT) - delta)).astype(BF16)
            dq4 = dq4 + _dot(ds_b, kb)
            rest[4][pl.ds(start, 3 * AB), :] += _dot(ds_b, q4, TN)
            rest[5][pl.ds(start, 3 * AB), :] += _dot(pn_b, do4b, TN)
        dq4 = dq4 * ATT_SCALE
        dq_ref[...] = jnp.concatenate([dq4[g * AB:(g + 1) * AB, :] for g in range(Q_PER_KV)], axis=1)
        ps = p_sink * inv * delta
        dsink_ref[...] += jnp.concatenate(
            [jnp.broadcast_to(-jnp.sum(ps[g * AB:(g + 1) * AB, :], axis=0, keepdims=True), (1, HEAD))
             for g in range(Q_PER_KV)], axis=0)

    q_spec = pl.BlockSpec((AB, GW), lambda kh, n: (n + q_off, kh))
    c_spec = pl.BlockSpec((CTX, HEAD), lambda kh, n: (0, kh))
    ce_spec = pl.BlockSpec((CTX, 2 * HEAD), lambda kh, n: (0, kh))
    s_spec = pl.BlockSpec((None, Q_PER_KV, HEAD), lambda kh, n: (kh, 0, 0))
    in_specs = [q_spec, c_spec, ce_spec, s_spec, q_spec, q_spec]
    args = [qa, kc, vc, sink4, o_all, do_all]
    out_specs = [q_spec, c_spec, c_spec, s_spec]
    out_shape = [_sds((T, N_Q * HEAD), F32), _sds((CTX, KW), F32), _sds((CTX, KW), F32), _sds((N_KV, Q_PER_KV, HEAD), F32)]
    if has_band:
        p_spec = pl.BlockSpec((S + 2 * CTX, HEAD), lambda kh, n: (0, kh))
        in_specs += [p_spec, pl.BlockSpec((S + 2 * CTX, 2 * HEAD), lambda kh, n: (0, kh)), _bias_spec(S)]
        args += list(band)
        out_specs += [p_spec, p_spec]
        out_shape += [_sds((S + 2 * CTX, KW), F32)] * 2
    alias = {}
    if prev_dq is not None:
        in_specs.append(ANY)
        alias = {len(args): 0}
        args.append(prev_dq)
    ci, ca, co, cs, cscr = _carry_args(carry)
    n_out = len(out_specs)
    res = pl.pallas_call(
        _carried(kern, carry, len(args), n_out, *_grid_ends((N_KV, nq))), name=name, grid=(N_KV, nq),
        in_specs=in_specs + ci, out_specs=out_specs + co, out_shape=out_shape + cs, scratch_shapes=cscr,
        input_output_aliases=alias, compiler_params=_params(("arbitrary", "arbitrary")),
    )(*args, *ca)
    return res if carry is None else (res[:n_out], res[n_out:])


def _dqkv_assemble(name, dq_all, dkp, dvp, dkc_l, dvc_l, dkc_c, dvc_c, cos, sin, S):
    T = CTX + S
    KW = N_KV * HEAD
    HALF = N_Q * HEAD // 2

    def kern(dq_ref, dkp_ref, dvp_ref, dkcl_ref, dvcl_ref, dkcc_ref, dvcc_ref, cos_ref, sin_ref, out_ref):
        i = pl.program_id(0)
        j = pl.program_id(1)
        cos_v, sin_v = cos_ref[...], sin_ref[...]

        @pl.when((j < 2) & (i == 0))
        def _():
            out_ref[...] = dq_ref[...].astype(BF16)

        @pl.when((j < 2) & (i > 0))
        def _():
            out_ref[...] = _unrope(dq_ref[...], cos_v, sin_v).astype(BF16)

        @pl.when((j == 2) & (i == 0))
        def _():
            out_ref[...] = jnp.concatenate([dkcl_ref[...] + dkcc_ref[...], dvcl_ref[...] + dvcc_ref[...]],
                                           axis=1).astype(BF16)

        @pl.when((j == 2) & (i > 0))
        def _():
            out_ref[...] = jnp.concatenate([_unrope(dkp_ref[...], cos_v, sin_v), dvp_ref[...]], axis=1).astype(BF16)

    same = lambda i, j: (i, 0)
    lat_map = lambda i, j: (jnp.maximum(i - 1, 0), 0)
    ctx_map = lambda i, j: (0, 0)
    return pl.pallas_call(
        kern, name=name, grid=(T // TR, 3),
        in_specs=[pl.BlockSpec((TR, HALF), lambda i, j: (i, jnp.minimum(j, 1))),
                  pl.BlockSpec((TR, KW), same), pl.BlockSpec((TR, KW), same),
                  pl.BlockSpec((CTX, KW), ctx_map), pl.BlockSpec((CTX, KW), ctx_map),
                  pl.BlockSpec((CTX, KW), ctx_map), pl.BlockSpec((CTX, KW), ctx_map),
                  pl.BlockSpec((TR, HEAD), lat_map), pl.BlockSpec((TR, HEAD), lat_map)],
        out_specs=pl.BlockSpec((TR, HALF), lambda i, j: (i, COL_Q // HALF + j)),
        out_shape=_sds((T, DP_W), BF16), compiler_params=_params(("arbitrary", "arbitrary")),
    )(dq_all, dkp, dvp, dkc_l, dvc_l, dkc_c, dvc_c, cos, sin)


RB = 128
CH = 256
HALO = 8
SUB = 8
GRP = 8


def _vscan(a, b, reverse):
    row = lax.broadcasted_iota(jnp.int32, a.shape, 0)
    A, H = a, b
    for s in (1, 2, 4):
        sh = SUB - s if reverse else s
        m = (row < SUB - s) if reverse else (row >= s)
        As = pltpu.roll(A, sh, 0)
        Hs = pltpu.roll(H, sh, 0)
        H = jnp.where(m, A * Hs + H, H)
        A = jnp.where(m, A * As, A)
    return A, H


def _scan_rows(a_ref, b_ref, r0, nrows, reverse, carry, emit):
    ngrp = nrows // (SUB * GRP)
    row = lax.broadcasted_iota(jnp.int32, (SUB, RB), 0)

    def grp(gi, carry):
        g = (ngrp - 1 - gi) if reverse else gi
        base = r0 + g * (SUB * GRP)
        for v in (range(GRP - 1, -1, -1) if reverse else range(GRP)):
            rs = pl.multiple_of(base + v * SUB, SUB)
            A, H = _vscan(a_ref[pl.ds(rs, SUB), :], b_ref[pl.ds(rs, SUB), :], reverse)
            hf = H + A * carry
            if reverse:
                before = jnp.where(row == SUB - 1, carry, pltpu.roll(hf, SUB - 1, 0))
                carry = hf[0:1, :]
            else:
                before = jnp.where(row == 0, carry, pltpu.roll(hf, 1, 0))
                carry = hf[SUB - 1:SUB, :]
            emit(rs, hf, before)
        return carry

    return lax.fori_loop(0, ngrp, grp, carry)


def _pad_start(ci):
    return pl.multiple_of(ci * CH + HALO * jnp.minimum(ci, 1), HALO)


def _conv_taps(ext, transpose=False):
    n = CH + 2 * HALO
    taps = []
    for k in range(CONV_W):
        off = CONV_LEFT - k if transpose else k - CONV_LEFT
        taps.append(ext[HALO:HALO + CH, :] if off == 0 else pltpu.roll(ext, (-off) % n, 0)[HALO:HALO + CH, :])
    return taps


def _lru_gates(xl, w4, b4, ls):
    pre = _dot(xl.astype(BF16), w4) + b4
    out = []
    for d in range(2):
        r = _sigmoid(pre[:, d * RB:(d + 1) * RB])
        i = _sigmoid(pre[:, (2 + d) * RB:(3 + d) * RB])
        la = LRU_C * r * ls[d:d + 1, :]
        a = jnp.exp(la)
        q = -jnp.tanh(la) * (1.0 + a * a)
        out.append((r, i, a, q))
    return out


def _rnn_specs(T):
    col = lambda n, *_: (0, n)
    return dict(
        xr=pl.BlockSpec((T, RB), lambda n, *_: (0, COL_XR // RB + n)),
        gr=pl.BlockSpec((T, RB), lambda n, *_: (0, COL_GR // RB + n)),
        act=pl.BlockSpec((T, RB), col),
        cw=pl.BlockSpec((CONV_W, RB), col), cb=pl.BlockSpec((1, RB), col),
        w4=pl.BlockSpec((None, RB, 4 * RB), lambda n, *_: (n, 0, 0)),
        b4=pl.BlockSpec((None, 1, 4 * RB), lambda n, *_: (n, 0, 0)),
        lam=pl.BlockSpec((2, RB), col))


PAD_ROWS = 3 * HALO


def _zero_pads(pad_ref, T):
    for r in (0, HALO + CTX, 2 * HALO + T):
        pad_ref[r:r + HALO, :] = jnp.zeros((HALO, RB), F32)


def _fill_padded(pad_ref, src_ref, T):
    _zero_pads(pad_ref, T)
    pad_ref[HALO:HALO + CTX, :] = src_ref[0:CTX, :].astype(F32)
    pad_ref[2 * HALO + CTX:2 * HALO + T, :] = src_ref[CTX:T, :].astype(F32)


def _pad_rows(ci):
    return pl.ds(pl.multiple_of(ci * CH + HALO + HALO * jnp.minimum(ci, 1), HALO), CH)


def _rnn_fwd(name, p, cw, cb, w4, b4, lam, T, carry=None):
    def kern(xr_ref, gr_ref, cw_ref, cb_ref, w4_ref, b4_ref, lam_ref,
             u_ref, a0, a1, yo_ref, hpf_ref, hpb_ref, r0_ref, r1_ref, i0_ref, i1_ref, xpad, b0, b1, y):
        _fill_padded(xpad, xr_ref, T)
        ls = _log_sigmoid(lam_ref[...])
        w4v, b4v, cwv, cbv = w4_ref[...], b4_ref[...], cw_ref[...], cb_ref[...]

        def chunk(ci, _):
            rows = pl.ds(pl.multiple_of(ci * CH, CH), CH)
            taps = _conv_taps(xpad[pl.ds(_pad_start(ci), CH + 2 * HALO), :])
            xl = cbv + sum(taps[k] * cwv[k:k + 1, :] for k in range(CONV_W))
            for d, (r, i, a, q) in enumerate(_lru_gates(xl, w4v, b4v, ls)):
                (a0, a1)[d][rows, :] = a
                (b0, b1)[d][rows, :] = jnp.sqrt(q) * (i * xl)
                (r0_ref, r1_ref)[d][rows, :] = r.astype(BF16)
                (i0_ref, i1_ref)[d][rows, :] = i.astype(BF16)
            return 0

        lax.fori_loop(0, T // CH, chunk, 0)
        zero = jnp.zeros((1, RB), F32)

        def emit_f(rs, hf, before):
            y[pl.ds(rs, SUB), :] = hf
            b0[pl.ds(rs, SUB), :] = before

        def emit_b(rs, hf, before):
            y[pl.ds(rs, SUB), :] += hf
            b1[pl.ds(rs, SUB), :] = before

        _scan_rows(a0, b0, 0, T, False, zero, emit_f)
        c = _scan_rows(a1, b1, 0, CTX, True, zero, emit_b)
        _scan_rows(a1, b1, CTX, T - CTX, True, c, emit_b)

        def finish(ci, _):
            rows = pl.ds(pl.multiple_of(ci * CH, CH), CH)
            yv = y[rows, :]
            u_ref[rows, :] = (yv * _gelu(gr_ref[rows, :].astype(F32))).astype(BF16)
            yo_ref[rows, :] = yv.astype(BF16)
            hpf_ref[rows, :] = b0[rows, :].astype(BF16)
            hpb_ref[rows, :] = b1[rows, :].astype(BF16)
            return 0

        lax.fori_loop(0, T // CH, finish, 0)

    sp = _rnn_specs(T)
    ci, ca, co, cs, cscr = _carry_args(carry)
    dts = [BF16, F32, F32] + [BF16] * 7
    res = pl.pallas_call(
        _carried(kern, carry, 7, 10, *_grid_ends((N_RNN_BLOCKS,))), name=name, grid=(N_RNN_BLOCKS,),
        in_specs=[sp["xr"], sp["gr"], sp["cw"], sp["cb"], sp["w4"], sp["b4"], sp["lam"]] + ci,
        out_specs=[sp["act"]] * 10 + co,
        out_shape=[_sds((T, D), dt) for dt in dts] + cs,
        scratch_shapes=[pltpu.VMEM((T + PAD_ROWS, RB), F32)] + [pltpu.VMEM((T, RB), F32)] * 3 + cscr,
        compiler_params=_params(),
    )(p, p, cw, cb, w4, b4, lam, *ca)
    return res if carry is None else (res[:10], res[10:])


def _rnn_bwd(name, p, du, saved, dp, cw, cb, w4, b4, lam, T, carry=None):
    def kern(xr_ref, gr_ref, du_ref, a0, a1, y_ref, hpf_ref, hpb_ref, r0_ref, r1_ref, i0_ref, i1_ref,
             cw_ref, cb_ref, w4_ref, b4_ref, lam_ref, dp_in,
             dp_ref, dcw_ref, dcb_ref, dw4_ref, db4_ref, dlam_ref,
             xpad, dxpad, c0, c1, dy):
        j = pl.program_id(1)

        @pl.when(j == 0)
        def _():
            scans(gr_ref, du_ref, a0, a1, y_ref, dp_ref, c0, c1, dy)

        @pl.when(j == 1)
        def _():
            gates(xr_ref, a0, a1, (hpf_ref, hpb_ref), (r0_ref, r1_ref), (i0_ref, i1_ref), cw_ref, cb_ref, w4_ref,
                  lam_ref, dp_ref, dcw_ref, dcb_ref, dw4_ref, db4_ref, dlam_ref, xpad, dxpad, c0, c1)

    def scans(gr_ref, du_ref, a0, a1, y_ref, dgr_ref, c0, c1, dy):
        def phase_a(ci, _):
            rows = pl.ds(pl.multiple_of(ci * CH, CH), CH)
            gr = gr_ref[rows, :].astype(F32)
            duv = du_ref[rows, :].astype(F32)
            dyv = duv * _gelu(gr)
            dgr_ref[rows, :] = (duv * y_ref[rows, :].astype(F32) * _gelu_grad(gr)).astype(BF16)
            dy[rows, :] = dyv
            c0[rows, :] = a0[rows, :] * dyv
            c1[rows, :] = a1[rows, :] * dyv
            return 0

        lax.fori_loop(0, T // CH, phase_a, 0)
        zero = jnp.zeros((1, RB), F32)

        def emit0(rs, hf, before):
            c0[pl.ds(rs, SUB), :] = dy[pl.ds(rs, SUB), :] + before

        def emit1(rs, hf, before):
            c1[pl.ds(rs, SUB), :] = dy[pl.ds(rs, SUB), :] + before

        _scan_rows(a0, c0, 0, T, True, zero, emit0)
        c = _scan_rows(a1, c1, CTX, T - CTX, False, zero, emit1)
        _scan_rows(a1, c1, 0, CTX, False, c, emit1)

    def gates(xr_ref, a0, a1, hp_refs, r_refs, i_refs, cw_ref, cb_ref, w4_ref, lam_ref,
              dxr_ref, dcw_ref, dcb_ref, dw4_ref, db4_ref, dlam_ref, xpad, dxpad, c0, c1):
        _fill_padded(xpad, xr_ref, T)
        _zero_pads(dxpad, T)
        lam_v = lam_ref[...]
        ls = _log_sigmoid(lam_v)
        w4v, cwv, cbv = w4_ref[...], cw_ref[...], cb_ref[...]

        def conv_chunk(ci):
            taps = _conv_taps(xpad[pl.ds(_pad_start(ci), CH + 2 * HALO), :])
            return taps, cbv + sum(taps[k] * cwv[k:k + 1, :] for k in range(CONV_W))

        dw4_ref[...] = jnp.zeros(dw4_ref.shape, F32)
        db4_ref[...] = jnp.zeros(db4_ref.shape, F32)
        dlam_ref[...] = jnp.zeros(dlam_ref.shape, F32)
        dcw_ref[...] = jnp.zeros(dcw_ref.shape, F32)
        dcb_ref[...] = jnp.zeros(dcb_ref.shape, F32)

        def phase_c(ci, _):
            base = pl.multiple_of(ci * CH, CH)
            rows = pl.ds(base, CH)
            _, xl = conv_chunk(ci)
            dxl = jnp.zeros((CH, RB), F32)
            dpre_a, dpre_x, dls = [], [], []
            for d in range(2):
                a = (a0, a1)[d][rows, :]
                r = r_refs[d][rows, :].astype(F32)
                i = i_refs[d][rows, :].astype(F32)
                q = -jnp.tanh(LRU_C * r * ls[d:d + 1, :]) * (1.0 + a * a)
                g = (c0, c1)[d][rows, :]
                hp = hp_refs[d][rows, :].astype(F32)
                gm = g * jnp.sqrt(q)
                di = gm * xl
                dxl = dxl + gm * i
                dla = a * (g * hp - a * (g * (i * xl)) * lax.rsqrt(q))
                dr = dla * (LRU_C * ls[d:d + 1, :])
                dls.append(_colsum(dla * (LRU_C * r)))
                dpre_a.append(dr * r * (1.0 - r))
                dpre_x.append(di * i * (1.0 - i))
            dpre = jnp.concatenate(dpre_a + dpre_x, axis=1)
            dpre_b = dpre.astype(BF16)
            dxl = dxl + _dot(dpre_b, w4v, NT)
            dw4_ref[...] += _dot(xl.astype(BF16), dpre_b, TN)
            db4_ref[...] += _colsum(dpre)
            dlam_ref[...] += jnp.concatenate(dls, axis=0)
            dcb_ref[...] += _colsum(dxl)
            dxpad[_pad_rows(ci), :] = dxl
            return 0

        lax.fori_loop(0, T // CH, phase_c, 0)
        dlam_ref[...] = dlam_ref[...] * _sigmoid(-lam_v)

        def phase_d(ci, _):
            base = pl.multiple_of(ci * CH, CH)
            rows = pl.ds(base, CH)
            xtaps, _ = conv_chunk(ci)
            dtaps = _conv_taps(dxpad[pl.ds(_pad_start(ci), CH + 2 * HALO), :], transpose=True)
            dxl = dxpad[_pad_rows(ci), :]
            dxr_ref[rows, :] = sum(dtaps[k] * cwv[k:k + 1, :] for k in range(CONV_W)).astype(BF16)
            dcw_ref[...] += jnp.concatenate([_colsum(dxl * xtaps[k]) for k in range(CONV_W)], axis=0)
            return 0

        lax.fori_loop(0, T // CH, phase_d, 0)

    sp = _rnn_specs(T)
    dp_spec = pl.BlockSpec((T, RB), lambda n, j: (0, COL_GR // RB + n - j * (COL_GR - COL_XR) // RB))
    ci, ca, co, cs, cscr = _carry_args(carry)
    n_in = 3 + len(saved) + 5 + 1
    res = pl.pallas_call(
        _carried(kern, carry, n_in, 6, *_grid_ends((N_RNN_BLOCKS, 2))), name=name, grid=(N_RNN_BLOCKS, 2),
        in_specs=[sp["xr"], sp["gr"]] + [sp["act"]] * (1 + len(saved)) + [sp["cw"], sp["cb"], sp["w4"], sp["b4"],
                                                                           sp["lam"], ANY] + ci,
        out_specs=[dp_spec, sp["cw"], sp["cb"], sp["w4"], sp["b4"], sp["lam"]] + co,
        out_shape=[_sds((T, DP_W), BF16), _sds((CONV_W, D), F32), _sds((1, D), F32),
                   _sds((N_RNN_BLOCKS, RB, 4 * RB), F32), _sds((N_RNN_BLOCKS, 1, 4 * RB), F32), _sds((2, D), F32)] + cs,
        scratch_shapes=[pltpu.VMEM((T + PAD_ROWS, RB), F32)] * 2 + [pltpu.VMEM((T, RB), F32)] * 3 + cscr,
        input_output_aliases={n_in - 1: 0},
        compiler_params=_params(("arbitrary", "arbitrary")),
    )(p, p, du, *saved, cw, cb, w4, b4, lam, dp, *ca)
    return res if carry is None else (res[:6], res[6:])


class _Plan:
    def __init__(self, shards, Ws):
        L = len(Ws)
        self.shards, self.Ws = shards, Ws
        self.Gs = [None] * L
        self.slots = [dict() for _ in range(L)]
        self.gate_slots = [None] * L
        self.table = {}
        for l in range(L):
            t = f"l{l}_"
            self.table[t + "proj"] = [("gather", l, k) for k in ("wo_rnn", "wo_attn", "wout")]
            self.table[t + "rnn_fwd"] = [("gather", l, "wffn_in_t")]
            self.table[t + "attn_lat_fwd"] = [("gather", l + 1, "win_t")] if l + 1 < L else []
            self.table[t + "ffn_in"] = [("gather", l, "wffn_out")]
            self.table[t + "ffn_in_dx"] = [("scatter", l, "wffn_out")]
            self.table[t + "attn_lat_bwd"] = [("scatter", l, "wffn_in_t")]
            self.table[t + "ffn_in_dw"] = [("gates", l + 1, "w4")] if l + 1 < L else []
            self.table[t + "proj_dx"] = [("scatter", l, "win_t_a")]
            self.table[t + "rnn_bwd"] = ([("scatter", l, k) for k in ("wout", "wo_attn", "wo_rnn")]
                                         + ([("scatter", l + 1, "win_t_b")] if l + 1 < L else []))
        self.table["l0_proj_dw_b"] = [("gates", 0, "w4")]

    def carry(self, name):
        jobs = []
        for kind, l, k in self.table.get(name, []):
            if kind == "gather":
                jobs.append(("gather", self.shards[l][k]))
            elif kind == "scatter":
                jobs.append(("scatter", self.Gs[l][k].reshape(N_DEV, -1, self.Gs[l][k].shape[-1])))
            else:
                jobs.append(("gather", self.Gs[l]["w4"].reshape(N_RNN_BLOCKS * RB, 4 * RB).astype(BF16)))
        return _Carry(jobs) if jobs else None

    def done(self, name, got):
        for (kind, l, k), res in zip(self.table[name], got):
            if kind == "gather":
                self.Ws[l][k] = res.reshape(-1, D)
            elif kind == "scatter":
                self.slots[l][k] = res
            else:
                self.gate_slots[l] = res


def _run(X, fn, name, *args, **kw):
    carry = None if X is None else X.carry(name)
    if carry is None:
        return fn(name, *args, **kw)
    out, got = fn(name, *args, carry=carry, **kw)
    X.done(name, got)
    return out


def _layer_fwd(l, xa, h, W, rope, S, nxt, X=None):
    T = xa.shape[0]
    tag = f"l{l}_"
    cos, sin, bias = rope
    p = _run(X, _mm_act, tag + "proj", h, W["win_t"], "NT", BF16)
    u, *rnn_saved = _run(X, _rnn_fwd, tag + "rnn_fwd", p, W["cw"], W["cb"], W["w4"], W["b4"], W["lam"], T)
    qa, kp, vp, kc, vc = _qkv_prep(tag + "qkv_prep", p, cos, sin, S)
    o_all = _attn_fwd(tag + "attn_ctx_fwd", qa, kc, vc, W["sink4"], S)
    o_all = _run(X, _attn_fwd, tag + "attn_lat_fwd", qa, kc, vc, W["sink4"], S, band=(kp, vp, bias), prev=o_all)
    ya, yb, z, m, x1, h2 = _out_fused(tag + "out", p, u, o_all, xa, W["wo_rnn"], W["wo_attn"], W["wout"],
                                      W["g_mix_post"], W["mod"], W["g_ffn_pre"])
    fg, fu, s = _run(X, _ffn_in_fused, tag + "ffn_in", h2, W["wffn_in_t"])
    e, *out = _ffn_out_fused(tag + "ffn_out", s, W["wffn_out"], x1, W["g_ffn_post"], W["mod"], nxt)
    saved = dict(xa=xa, h=h, p=p, u=u, rnn=rnn_saved, qa=qa, kp=kp, vp=vp, kc=kc, vc=vc, o_all=o_all,
                 ya=ya, yb=yb, z=z, m=m, x1=x1, h2=h2, fg=fg, fu=fu, s=s, e=e)
    return saved, out


def _layer_bwd(l, dx2, A, W, rope, S, X=None, loss_of=None):
    T = A["xa"].shape[0]
    tag = f"l{l}_"
    cos, sin, bias = rope
    G = {}
    if X is not None:
        X.Gs[l] = G
    if loss_of is None:
        de, df, dga2, G["g_ffn_post"] = _ffn_bwd_fused(tag + "ffn_bwd", A["fg"], A["fu"], W["wffn_out"],
                                                       head=(dx2, A["e"], W["g_ffn_post"], W["mod"]))
    else:
        dx2, de, dga2, G["g_ffn_post"], G["sq"] = _loss_resid_bwd(tag + "loss_ffn_resid_bwd", *loss_of, A["e"],
                                                                  W["g_ffn_post"], W["mod"], GA2)
        df, = _ffn_bwd_fused(tag + "ffn_bwd", A["fg"], A["fu"], W["wffn_out"], de=de)
    G["wffn_out"] = _mm_wgrad(tag + "ffn_out_dw", A["s"], de)
    dx1, dm, dsh2, dsc2, G["g_ffn_pre"], dga1, G["g_mix_post"] = _run(
        X, _ffn_in_bwd_fused, tag + "ffn_in_dx", df, W["wffn_in_t"], A["x1"], dx2, A["m"], W["g_ffn_pre"], W["mod"],
        W["g_mix_post"])
    G["wffn_in_t"] = _run(X, _mm_wgrad, tag + "ffn_in_dw", df, A["h2"])
    G["wout"] = _mm_wgrad(tag + "out_dw", A["z"], dm)
    dya, dyb, dgl, du, do = _out_bwd_fused(tag + "out_dx", dm, W["wout"], W["wo_rnn"], W["wo_attn"], A["p"], A["ya"],
                                           A["yb"])
    G["wo_attn"] = _mm_wgrad(tag + "o_attn_dw", A["o_all"], dyb)
    G["wo_rnn"] = _mm_wgrad(tag + "o_rnn_dw", A["u"], dya)
    dq_all, dkc_c, dvc_c, dsink_c = _attn_bwd(tag + "attn_ctx_bwd", A["qa"], A["kc"], A["vc"], W["sink4"],
                                               A["o_all"], do, S)
    dq_all, dkc_l, dvc_l, dsink_l, dkp, dvp = _run(
        X, _attn_bwd, tag + "attn_lat_bwd", A["qa"], A["kc"], A["vc"], W["sink4"], A["o_all"], do, S,
        band=(A["kp"], A["vp"], bias), prev_dq=dq_all)
    G["sink4"] = dsink_c + dsink_l
    dp = _dqkv_assemble(tag + "dqkv", dq_all, dkp, dvp, dkc_l, dvc_l, dkc_c, dvc_c, cos, sin, S)
    dp, G["cw"], G["cb"], G["w4"], G["b4"], G["lam"] = _run(
        X, _rnn_bwd, tag + "rnn_bwd", A["p"], du, A["rnn"], dp, W["cw"], W["cb"], W["w4"], W["b4"], W["lam"], T)
    proj_dx = (_proj_bwd_fused, tag + "proj_dx", dp, dgl, W["win_t"], A["xa"], dx1, W["g_mix_pre"], W["mod"])
    if X is not None:
        G["win_t_a"] = _proj_wgrad(tag + "proj_dw_a", dp, dgl, A["h"][:, :D // 2])
        dxa, dsh1, dsc1, G["g_mix_pre"] = _run(X, *proj_dx)
        G["win_t_b"] = _run(X, _proj_wgrad, tag + "proj_dw_b", dp, dgl, A["h"][:, D // 2:])
    else:
        dxa, dsh1, dsc1, G["g_mix_pre"] = _run(X, *proj_dx)
        G["win_t"] = _proj_wgrad(tag + "proj_dw", dp, dgl, A["h"])
    G["mod"] = jnp.concatenate([dsh1, dsc1, dga1, dsh2, dsc2, dga2], axis=1)
    return dxa, G


def _local_step(xa, target, Ws, S, X=None):
    rope = (*_rope_tables(S), _band_bias(S))
    L = len(Ws)
    h = _normmod_fwd("l0_mix_norm", xa, Ws[0]["g_mix_pre"], Ws[0]["mod"], SH1, SC1)
    saved = []
    x = xa
    for l in range(L):
        nxt = (Ws[l + 1]["g_mix_pre"], Ws[l + 1]["mod"]) if l + 1 < L else None
        A, out = _layer_fwd(l, x, h, Ws[l], rope, S, nxt, X)
        saved.append(A)
        if l + 1 < L:
            x, h = out
    Gs = [None] * L
    dx = None
    for l in reversed(range(L)):
        dx, Gs[l] = _layer_bwd(l, dx, saved[l], Ws[l], rope, S, X, loss_of=(out[0], target) if l == L - 1 else None)
    return Gs[L - 1]["sq"], dx, Gs


MESH = pl.DeviceIdType.MESH


def _place():
    return lax.axis_index("x"), lax.axis_index("y"), lax.axis_index("c")


def _lin(px, py, pc):
    return 4 * px + 2 * py + pc


def _allgather_small(name, blk):
    m, n = blk.shape

    def body(x_ref, out_ref, send_sems, recv_sems, local_sem):
        x, y, c = _place()
        me, sibling = (x, y, c), (x, y, 1 - c)
        chips = [(1 - x, y), (x, 1 - y), (1 - x, 1 - y)]

        def copy(k, block, to, src=None):
            dst = out_ref.at[_lin(*block)]
            return pltpu.make_async_remote_copy(src_ref=dst if src is None else src, dst_ref=dst,
                                                send_sem=send_sems.at[k], recv_sem=recv_sems.at[k],
                                                device_id=to, device_id_type=MESH)

        mine = pltpu.make_async_copy(x_ref, out_ref.at[_lin(*me)], local_sem)
        mine.start()
        first = [copy(0, me, sibling, src=x_ref)]
        first += [copy(1 + j, me, (*chip, c), src=x_ref) for j, chip in enumerate(chips)]
        for cp in first:
            cp.start()
        passed = [copy(4 + j, (*chip, c), sibling) for j, chip in enumerate(chips)]
        for j, chip in enumerate(chips):
            copy(1 + j, (*chip, c), me).wait_recv()
            passed[j].start()
        copy(0, sibling, me).wait_recv()
        for j, chip in enumerate(chips):
            copy(4 + j, (*chip, 1 - c), me).wait_recv()
        for cp in first + passed:
            cp.wait_send()
        mine.wait()

    return pl.pallas_call(
        body, name=name, out_shape=_sds((N_DEV, m, n), blk.dtype),
        in_specs=[pl.BlockSpec(memory_space=pltpu.VMEM)], out_specs=pl.BlockSpec(memory_space=pltpu.VMEM),
        scratch_shapes=[pltpu.SemaphoreType.DMA((7,)), pltpu.SemaphoreType.DMA((7,)), pltpu.SemaphoreType.DMA],
        compiler_params=pltpu.CompilerParams(vmem_limit_bytes=VMEM_LIMIT),
    )(blk)


def _allgather_hbm(name, shards):
    na = len(shards)

    def body(*refs):
        ins, outs = refs[:na], refs[na:2 * na]
        send_sems, recv_sems, local_sems = refs[2 * na:]
        x, y, c = _place()
        me, sibling = (x, y, c), (x, y, 1 - c)
        chips = [(1 - x, y), (x, 1 - y), (1 - x, 1 - y)]

        def copy(a, k, block, to, from_input=False):
            dst = outs[a].at[_lin(*block)]
            return pltpu.make_async_remote_copy(src_ref=ins[a] if from_input else dst, dst_ref=dst,
                                                send_sem=send_sems.at[a, k], recv_sem=recv_sems.at[a, k],
                                                device_id=to, device_id_type=MESH)

        mine = [pltpu.make_async_copy(ins[a], outs[a].at[_lin(*me)], local_sems.at[a]) for a in range(na)]
        for cp in mine:
            cp.start()
        first = []
        for a in range(na):
            first.append(copy(a, 0, me, sibling, True))
            first += [copy(a, 1 + j, me, (*chip, c), True) for j, chip in enumerate(chips)]
        for cp in first:
            cp.start()
        passed = []
        for j, chip in enumerate(chips):
            for a in range(na):
                copy(a, 1 + j, (*chip, c), me).wait_recv()
                fwd = copy(a, 4 + j, (*chip, c), sibling)
                fwd.start()
                passed.append(fwd)
        for a in range(na):
            copy(a, 0, sibling, me).wait_recv()
            for j, chip in enumerate(chips):
                copy(a, 4 + j, (*chip, 1 - c), me).wait_recv()
        for cp in first + passed:
            cp.wait_send()
        for cp in mine:
            cp.wait()

    return pl.pallas_call(
        body, name=name, out_shape=[_sds((N_DEV, *s.shape), s.dtype) for s in shards],
        in_specs=[ANY] * na, out_specs=[ANY] * na,
        scratch_shapes=[pltpu.SemaphoreType.DMA((na, 7)), pltpu.SemaphoreType.DMA((na, 7)),
                        pltpu.SemaphoreType.DMA((na,))],
    )(*shards)


def _exchange_shards(name, grads, L):
    nw = len(grads)
    na = nw * L
    flat = [g for per_layer in grads for g in per_layer]

    def body(*refs):
        ins, outs = refs[:na], refs[na:na + nw]
        send_sems, recv_sems, local_sems = refs[na + nw:]
        x, y, c = _place()
        me = _lin(x, y, c)
        peers = [(x ^ ((k + 1) >> 2 & 1), y ^ ((k + 1) >> 1 & 1), c ^ ((k + 1) & 1)) for k in range(7)]

        def copy(a, k, src_blk, dst_blk):
            return pltpu.make_async_remote_copy(src_ref=ins[a].at[src_blk], dst_ref=outs[a // L].at[a % L, dst_blk],
                                                send_sem=send_sems.at[a, k], recv_sem=recv_sems.at[a, k],
                                                device_id=peers[k], device_id_type=MESH)

        mine = [pltpu.make_async_copy(ins[a].at[me], outs[a // L].at[a % L, me], local_sems.at[a]) for a in range(na)]
        for cp in mine:
            cp.start()
        sent = [copy(a, k, _lin(*peers[k]), me) for a in range(na) for k in range(7)]
        for cp in sent:
            cp.start()
        for a in range(na):
            for k in range(7):
                copy(a, k, me, _lin(*peers[k])).wait_recv()
        for cp in sent:
            cp.wait_send()
        for cp in mine:
            cp.wait()

    return pl.pallas_call(
        body, name=name, out_shape=[_sds((L, *per_layer[0].shape), per_layer[0].dtype) for per_layer in grads],
        in_specs=[ANY] * na, out_specs=[ANY] * nw,
        scratch_shapes=[pltpu.SemaphoreType.DMA((na, 7)), pltpu.SemaphoreType.DMA((na, 7)),
                        pltpu.SemaphoreType.DMA((na,))],
    )(*flat)


MOD_ROWS = 16
MOD_SHARD = 6 * D // N_DEV
HI = lax.Precision.HIGHEST


def _mod_fwd(name, c9, w_mod, b_shard):
    L = w_mod.shape[0]

    def kern(c_ref, w_ref, b_ref, o_ref):
        o_ref[...] = lax.dot_general(_silu(c_ref[...]), w_ref[...], NN, precision=HI,
                                     preferred_element_type=F32) + b_ref[...]

    return pl.pallas_call(
        kern, name=name, grid=(L,),
        in_specs=[_full_spec(c9.shape), pl.BlockSpec((None, D, MOD_SHARD), lambda l: (l, 0, 0)),
                  pl.BlockSpec((None, 1, MOD_SHARD), lambda l: (l, 0, 0))],
        out_specs=pl.BlockSpec((None, MOD_ROWS, MOD_SHARD), lambda l: (l, 0, 0)),
        out_shape=_sds((L, MOD_ROWS, MOD_SHARD), F32), compiler_params=_params(),
    )(c9, w_mod, b_shard)


def _mod_bwd(name, c9, w_mod, dmod_all, dmod_cols):
    L = w_mod.shape[0]

    def rows9(ref, l):
        own = jnp.concatenate([ref[j, 2 * l + 1:2 * l + 2, :] for j in range(N_DEV)], axis=0)
        ctx = ref[0, 2 * l:2 * l + 1, :]
        for j in range(1, N_DEV):
            ctx = ctx + ref[j, 2 * l:2 * l + 1, :]
        return own, ctx

    def kern(c_ref, w_ref, all_ref, cols_ref, gw_ref, gb_ref, gc_ref):
        l = pl.program_id(0)
        for ll in range(L):
            @pl.when(l == ll)
            def _():
                own, ctx = rows9(all_ref, ll)
                gb_ref[...] = _colsum(own) + ctx
                own_s, ctx_s = rows9(cols_ref, ll)
                r16 = jnp.concatenate([own_s, ctx_s, jnp.zeros((MOD_ROWS - N_DEV - 1, MOD_SHARD), F32)], axis=0)
                gw_ref[...] = lax.dot_general(_silu(c_ref[...]), r16, TN, precision=HI, preferred_element_type=F32)
                part = lax.dot_general(r16, w_ref[...], NT, precision=HI,
                                       preferred_element_type=F32)[N_DEV:N_DEV + 1, :]
                if ll == 0:
                    gc_ref[...] = part
                else:
                    gc_ref[...] += part

    return pl.pallas_call(
        kern, name=name, grid=(L,),
        in_specs=[_full_spec(c9.shape), pl.BlockSpec((None, D, MOD_SHARD), lambda l: (l, 0, 0)),
                  _full_spec(dmod_all.shape), _full_spec(dmod_cols.shape)],
        out_specs=[pl.BlockSpec((None, D, MOD_SHARD), lambda l: (l, 0, 0)),
                   pl.BlockSpec((None, 1, 6 * D), lambda l: (l, 0, 0)), _full_spec((1, D))],
        out_shape=[_sds((L, D, MOD_SHARD), F32), _sds((L, 1, 6 * D), F32), _sds((1, D), F32)],
        compiler_params=_params(),
    )(c9, w_mod, dmod_all, dmod_cols)


_BC1 = 1.0 - ADAM_B1 ** ADAM_STEP
_BC2 = 1.0 - ADAM_B2 ** ADAM_STEP


def _adamw_vals(w, g, m, v):
    m = ADAM_B1 * m + (1.0 - ADAM_B1) * g
    v = ADAM_B2 * v + (1.0 - ADAM_B2) * (g * g)
    delta = -ADAM_LR * ((m / _BC1) / (jnp.sqrt(v / _BC2) + ADAM_EPS) + ADAM_WD * w)
    return delta, m, v


def _adamw(name, w, g, m, v, tile):
    R, C = w.shape
    blk = ((tile, C), lambda i: (i, 0))

    def body(i, ins, ps, outs, acc):
        d, mm, vv = _adamw_vals(ins[0][...], ins[1][...], ins[2][...], ins[3][...])
        outs[0][...] = d
        outs[1][...] = mm
        outs[2][...] = vv

    return _ew(name, body, R // tile, [(a, *blk) for a in (w, g, m, v)], [], [(_sds((R, C), F32), *blk)] * 3)


def _sum_slots(ref):
    g = ref[0].astype(F32)
    for j in range(1, N_DEV):
        g = g + ref[j].astype(F32)
    return g


def _adamw_slots(name, slots, w, m, v, tile):
    L, R, C = w.shape
    n = R // tile
    spec = pl.BlockSpec((None, tile, C), lambda l, i: (l, i, 0))
    pieces = [s if isinstance(s, (list, tuple)) else [s] for s in slots]
    layer_of = [ll for ll, ps in enumerate(pieces) for _ in ps]
    flat = [p for ps in pieces for p in ps]

    def slot_spec(ll, cols):
        return pl.BlockSpec((N_DEV, tile, cols),
                            lambda l, i: (0, jnp.where(l == ll, i, jnp.where(l < ll, 0, n - 1)), 0))

    def kern(*refs):
        s_refs = refs[:len(flat)]
        w_ref, m_ref, v_ref, g_ref, d_ref, mo_ref, vo_ref = refs[len(flat):]
        l = pl.program_id(0)
        for ll in range(L):
            @pl.when(l == ll)
            def _():
                parts = [_sum_slots(r) for r, lr in zip(s_refs, layer_of) if lr == ll]
                g = parts[0] if len(parts) == 1 else jnp.concatenate(parts, axis=1)
                g_ref[...] = g
                d_ref[...], mo_ref[...], vo_ref[...] = _adamw_vals(w_ref[...], g, m_ref[...], v_ref[...])

    return pl.pallas_call(
        kern, name=name, grid=(L, n),
        in_specs=[slot_spec(ll, p.shape[-1]) for ll, p in zip(layer_of, flat)] + [spec, spec, spec],
        out_specs=[spec] * 4, out_shape=[_sds((L, R, C), F32)] * 4,
        compiler_params=_params(("arbitrary", "arbitrary")),
    )(*flat, w, m, v)


def _sum_blocks(name, blocks):
    _, R, C = blocks.shape

    def kern(b_ref, o_ref):
        o_ref[...] = _sum_slots(b_ref)

    return pl.pallas_call(kern, name=name, in_specs=[_full_spec(blocks.shape)], out_specs=_full_spec((R, C)),
                          grid=(1,), out_shape=_sds((R, C), F32), compiler_params=_params())(blocks)


BIG = ("win_t", "wo_rnn", "wo_attn", "wout", "wffn_in_t", "wffn_out")
BIG_SRC = ("w_in", "w_o_rnn", "w_o_attn", "w_out", "w_ffn_in", "w_ffn_out")
BIG_T = (True, False, False, False, True, False)
BIG_TILE = (176, 128, 128, 128, 176, 176)


def _chan_full(g8):
    return jnp.transpose(g8, (1, 0, 2)).reshape(g8.shape[1], D)


def kernel(x, c, ctx, c_ctx, w_mod, b_mod, g_mix_pre, g_mix_post, g_ffn_pre, g_ffn_post, w_in, conv_w, conv_b, lru_wa, lru_ba, lru_wx, lru_bx, lru_lam, attn_sink, w_o_rnn, w_o_attn, w_out, w_ffn_in, w_ffn_out, loss_target, m_c_ctx, m_w_mod, m_b_mod, m_g_mix_pre, m_g_mix_post, m_g_ffn_pre, m_g_ffn_post, m_w_in, m_conv_w, m_conv_b, m_lru_wa, m_lru_ba, m_lru_wx, m_lru_bx, m_lru_lam, m_attn_sink, m_w_o_rnn, m_w_o_attn, m_w_out, m_w_ffn_in, m_w_ffn_out, v_c_ctx, v_w_mod, v_b_mod, v_g_mix_pre, v_g_mix_post, v_g_ffn_pre, v_g_ffn_post, v_w_in, v_conv_w, v_conv_b, v_lru_wa, v_lru_ba, v_lru_wx, v_lru_bx, v_lru_lam, v_attn_sink, v_w_o_rnn, v_w_o_attn, v_w_out, v_w_ffn_in, v_w_ffn_out):
    P = dict(c_ctx=c_ctx, w_mod=w_mod, b_mod=b_mod, g_mix_pre=g_mix_pre, g_mix_post=g_mix_post, g_ffn_pre=g_ffn_pre,
             g_ffn_post=g_ffn_post, w_in=w_in, conv_w=conv_w, conv_b=conv_b, lru_wa=lru_wa, lru_ba=lru_ba,
             lru_wx=lru_wx, lru_bx=lru_bx, lru_lam=lru_lam, attn_sink=attn_sink, w_o_rnn=w_o_rnn, w_o_attn=w_o_attn,
             w_out=w_out, w_ffn_in=w_ffn_in, w_ffn_out=w_ffn_out)
    Mo = dict(c_ctx=m_c_ctx, w_mod=m_w_mod, b_mod=m_b_mod, g_mix_pre=m_g_mix_pre, g_mix_post=m_g_mix_post,
              g_ffn_pre=m_g_ffn_pre, g_ffn_post=m_g_ffn_post, w_in=m_w_in, conv_w=m_conv_w, conv_b=m_conv_b,
              lru_wa=m_lru_wa, lru_ba=m_lru_ba, lru_wx=m_lru_wx, lru_bx=m_lru_bx, lru_lam=m_lru_lam,
              attn_sink=m_attn_sink, w_o_rnn=m_w_o_rnn, w_o_attn=m_w_o_attn, w_out=m_w_out, w_ffn_in=m_w_ffn_in,
              w_ffn_out=m_w_ffn_out)
    Vo = dict(c_ctx=v_c_ctx, w_mod=v_w_mod, b_mod=v_b_mod, g_mix_pre=v_g_mix_pre, g_mix_post=v_g_mix_post,
              g_ffn_pre=v_g_ffn_pre, g_ffn_post=v_g_ffn_post, w_in=v_w_in, conv_w=v_conv_w, conv_b=v_conv_b,
              lru_wa=v_lru_wa, lru_ba=v_lru_ba, lru_wx=v_lru_wx, lru_bx=v_lru_bx, lru_lam=v_lru_lam,
              attn_sink=v_attn_sink, w_o_rnn=v_w_o_rnn, w_o_attn=v_w_o_attn, w_out=v_w_out, w_ffn_in=v_w_ffn_in,
              w_ffn_out=v_w_ffn_out)
    L = w_in.shape[0]
    S = x.shape[1]
    me = _lin(*_place())

    small = jnp.concatenate([c.reshape(8, 128), conv_w.reshape(L * CONV_W, 128), lru_ba.reshape(2 * L, 128),
                             lru_bx.reshape(2 * L, 128), lru_lam.reshape(2 * L, 128), jnp.zeros((4, 128), F32)], axis=0)
    small_all = _allgather_small("ag_small", small)
    c_all = small_all[:, 0:8].reshape(N_DEV, D)
    conv_w_f = _chan_full(small_all[:, 8:16]).reshape(L, CONV_W, D)
    lru_ba_f = _chan_full(small_all[:, 16:20]).reshape(L, 2, D)
    lru_bx_f = _chan_full(small_all[:, 20:24]).reshape(L, 2, D)
    lru_lam_f = _chan_full(small_all[:, 24:28]).reshape(L, 2, D)

    c9 = jnp.concatenate([c_all, c_ctx[None], jnp.zeros((MOD_ROWS - N_DEV - 1, D), F32)], axis=0)
    b_shard = lax.dynamic_slice_in_dim(b_mod, me * MOD_SHARD, MOD_SHARD, axis=1)[:, None, :]
    mod_part = _mod_fwd("mod_fwd", c9, w_mod, b_shard)
    mod_all = _allgather_small("ag_mod", mod_part.reshape(L * MOD_ROWS, MOD_SHARD))
    mod_all = jnp.transpose(mod_all.reshape(N_DEV, L, MOD_ROWS, MOD_SHARD), (1, 2, 0, 3)).reshape(L, MOD_ROWS, 6 * D)
    own_row = lax.dynamic_index_in_dim(mod_all, me, axis=1, keepdims=False)
    modrows = jnp.stack([mod_all[:, N_DEV], own_row], axis=1)

    shards = [{k: (P[src][l].T if tr else P[src][l]).astype(BF16) for k, src, tr in zip(BIG, BIG_SRC, BIG_T)}
              for l in range(L)]
    win0, = _allgather_hbm("ag_w_in0", [shards[0]["win_t"]])
    Ws = []
    for l in range(L):
        W = {"win_t": win0.reshape(-1, D)} if l == 0 else {}
        W.update(
            cw=conv_w_f[l], cb=conv_b[l][None],
            w4=jnp.concatenate([lru_wa[l, 0], lru_wa[l, 1], lru_wx[l, 0], lru_wx[l, 1]], axis=-1).astype(BF16),
            b4=jnp.concatenate([lru_ba_f[l, 0].reshape(N_RNN_BLOCKS, 1, RB), lru_ba_f[l, 1].reshape(N_RNN_BLOCKS, 1, RB),
                                lru_bx_f[l, 0].reshape(N_RNN_BLOCKS, 1, RB), lru_bx_f[l, 1].reshape(N_RNN_BLOCKS, 1, RB)],
                               axis=-1),
            lam=lru_lam_f[l], sink4=jnp.broadcast_to(attn_sink[l].reshape(N_KV, Q_PER_KV, 1), (N_KV, Q_PER_KV, HEAD)),
            g_mix_pre=g_mix_pre[l][None], g_mix_post=g_mix_post[l][None], g_ffn_pre=g_ffn_pre[l][None],
            g_ffn_post=g_ffn_post[l][None], mod=modrows[l])
        Ws.append(W)

    xa = jnp.concatenate([ctx[0], x[0]], axis=0)
    plan = _Plan(shards, Ws)
    sq, dxa, Gs = _local_step(xa, loss_target[0], Ws, S, plan)
    loss = lax.psum((0.5 / D) * jnp.sum(sq), ("x", "y", "c"))
    grad_x = dxa[CTX:][None]

    dmod = jnp.concatenate([Gs[l]["mod"] for l in range(L)] + [jnp.zeros((8 - 2 * L, 6 * D), F32)], axis=0)
    dmod_all = _allgather_small("ag_dmod", dmod)
    dmod_cols = lax.dynamic_slice_in_dim(dmod_all, me * MOD_SHARD, MOD_SHARD, axis=2)
    g_w_mod, g_b_mod, dsc_part = _mod_bwd("mod_bwd", c9, w_mod, dmod_all, dmod_cols)
    g_b_mod = g_b_mod[:, 0]

    def rows(name, shape):
        return jnp.concatenate([Gs[l][name].reshape(shape) for l in range(L)], axis=0)

    b4g = [Gs[l]["b4"].reshape(N_RNN_BLOCKS, 4, RB) for l in range(L)]
    sink_row = jnp.concatenate([Gs[l]["sink4"][:, :, 0].reshape(1, N_Q) for l in range(L)]
                               + [jnp.zeros((1, D - L * N_Q), F32)], axis=1)
    small_g = jnp.concatenate(
        [rows("g_mix_pre", (1, D)), rows("g_mix_post", (1, D)), rows("g_ffn_pre", (1, D)), rows("g_ffn_post", (1, D)),
         rows("cb", (1, D)), rows("cw", (CONV_W, D))]
        + [b4g[l][:, d].reshape(1, D) for l in range(L) for d in range(2)]
        + [b4g[l][:, 2 + d].reshape(1, D) for l in range(L) for d in range(2)]
        + [rows("lam", (2, D)), sink_row, dsc_part], axis=0)
    n_small = small_g.shape[0]
    small_tot = _sum_blocks("sum_small", _allgather_small("ag_small_grads", small_g))
    o = 0
    G = {}
    for name in ("g_mix_pre", "g_mix_post", "g_ffn_pre", "g_ffn_post", "conv_b"):
        G[name] = small_tot[o:o + L]
        o += L
    G["conv_w"] = small_tot[o:o + L * CONV_W].reshape(L, CONV_W, D)
    o += L * CONV_W
    for name in ("lru_ba", "lru_bx", "lru_lam"):
        G[name] = small_tot[o:o + 2 * L].reshape(L, 2, D)
        o += 2 * L
    G["attn_sink"] = small_tot[o, :L * N_Q].reshape(L, N_Q)
    sg = jax.nn.sigmoid(c_ctx)
    G["c_ctx"] = small_tot[o + 1] * (sg * (1.0 + c_ctx * (1.0 - sg)))
    G["b_mod"] = g_b_mod
    G["w_mod"] = g_w_mod

    last_slots, = _exchange_shards("exchange_w_in0", [[Gs[0]["win_t_b"].reshape(N_DEV, -1, D // 2)]], 1)
    plan.slots[0]["win_t_b"] = last_slots[0]
    for l in range(L):
        plan.slots[l]["win_t"] = [plan.slots[l]["win_t_a"], plan.slots[l]["win_t_b"]]

    out_g, out_d, out_m, out_v = {}, {}, {}, {}

    def put(name, res, shape=None):
        g, d, m, v = res
        for dst, val in ((out_g, g), (out_d, d), (out_m, m), (out_v, v)):
            dst[name] = val if shape is None else val.reshape(shape)

    for k, src, tr, tile in zip(BIG, BIG_SRC, BIG_T, BIG_TILE):
        lay = (lambda a: jnp.swapaxes(a, 1, 2)) if tr else (lambda a: a)
        res = _adamw_slots("adamw_" + src, [plan.slots[l][k] for l in range(L)], lay(P[src]), lay(Mo[src]),
                           lay(Vo[src]), tile)
        put(src, [lay(r) for r in res])
    res = _adamw("adamw_w_mod", w_mod.reshape(L * D, MOD_SHARD), g_w_mod.reshape(L * D, MOD_SHARD),
                 m_w_mod.reshape(L * D, MOD_SHARD), v_w_mod.reshape(L * D, MOD_SHARD), 256)
    put("w_mod", (g_w_mod,) + tuple(res), w_mod.shape)
    def fuse4(wa, wx):
        return jnp.concatenate([wa[:, 0], wa[:, 1], wx[:, 0], wx[:, 1]], axis=-1).reshape(L, N_RNN_BLOCKS * RB, 4 * RB)

    res = _adamw_slots("adamw_gates", plan.gate_slots,
                       fuse4(lru_wa, lru_wx), fuse4(m_lru_wa, m_lru_wx), fuse4(v_lru_wa, v_lru_wx), 256)
    res = [r.reshape(L, N_RNN_BLOCKS, RB, 4, RB) for r in res]
    put("lru_wa", [jnp.stack([r[:, :, :, 0], r[:, :, :, 1]], axis=1) for r in res])
    put("lru_wx", [jnp.stack([r[:, :, :, 2], r[:, :, :, 3]], axis=1) for r in res])
    rep = ("g_mix_pre", "g_mix_post", "g_ffn_pre", "g_ffn_post", "conv_b", "b_mod")

    def pack_rep(T_):
        sink = jnp.concatenate([T_["attn_sink"].reshape(1, L * N_Q), jnp.zeros((1, D - L * N_Q), F32)], axis=1)
        return jnp.concatenate([T_[n].reshape(-1, D) for n in rep] + [sink, T_["c_ctx"][None]], axis=0)

    pk = [pack_rep(T_) for T_ in (P, G, Mo, Vo)]
    n_rep = pk[0].shape[0]
    res = _adamw("adamw_replicated", *[jnp.pad(a, ((0, 24 - n_rep), (0, 0))) for a in pk], 24)
    res = (pk[1],) + tuple(r[:n_rep] for r in res)
    o = 0
    for n in rep:
        k = P[n].size // D
        put(n, [r[o:o + k] for r in res], P[n].shape)
        o += k
    put("attn_sink", [r[o, :L * N_Q] for r in res], attn_sink.shape)
    put("c_ctx", [r[o + 1] for r in res], c_ctx.shape)
    chan = ("conv_w", "lru_ba", "lru_bx", "lru_lam")
    g_own = {n: lax.dynamic_slice_in_dim(G[n], me * RB, RB, axis=2) for n in chan}

    def pack_chan(T_):
        return jnp.concatenate([T_[n].reshape(-1, RB) for n in chan], axis=0)

    pk = [pack_chan(T_) for T_ in (P, g_own, Mo, Vo)]
    n_ch = pk[0].shape[0]
    res = _adamw("adamw_channels", *[jnp.pad(a, ((0, 24 - n_ch), (0, 0))) for a in pk], 24)
    res = (pk[1],) + tuple(r[:n_ch] for r in res)
    o = 0
    for n in chan:
        k = P[n].size // RB
        put(n, [r[o:o + k] for r in res], P[n].shape)
        o += k

    order = ("c_ctx", "w_mod", "b_mod", "g_mix_pre", "g_mix_post", "g_ffn_pre", "g_ffn_post", "w_in", "conv_w", "conv_b",
             "lru_wa", "lru_ba", "lru_wx", "lru_bx", "lru_lam", "attn_sink", "w_o_rnn", "w_o_attn", "w_out", "w_ffn_in",
             "w_ffn_out")
    return (loss, grad_x, *[out_g[n] for n in order], *[out_d[n] for n in order], *[out_m[n] for n in order],
            *[out_v[n] for n in order])
```

```python
import functools
import math

import numpy as np
import jax
import jax.numpy as jnp
from jax import lax
from jax.experimental import pallas as pl
from jax.experimental.pallas import tpu as pltpu

F32 = jnp.float32
BF16 = jnp.bfloat16

D = 1024
CTX = 256
TR = 256
HEAD = 128
N_Q = 8
N_KV = 2
Q_PER_KV = N_Q // N_KV
GRID_W = 64
N_FREQ = HEAD // 4
ROPE_BASE = 10000.0
N_RNN_BLOCKS = 8
CONV_W = 4
CONV_LEFT = 2
LRU_C = 8.0
D_FF = 2816
IN_W = 5632
P_W = IN_W
DP_W = 3584
COL_XR, COL_GR, COL_Q, COL_K, COL_V, COL_GL = 0, 1024, 2048, 3072, 3328, 3584
GLB = 512
EPS = 1e-6
NEG_INF = -1e30
ATT_SCALE = HEAD ** -0.5
N_DEV = 8
VMEM_LIMIT = 56 * 1024 * 1024

ADAM_LR, ADAM_B1, ADAM_B2, ADAM_EPS, ADAM_WD, ADAM_STEP = 0.001, 0.9, 0.999, 1e-08, 0.01, 10

NN = (((1,), (0,)), ((), ()))
NT = (((1,), (1,)), ((), ()))
TN = (((0,), (0,)), ((), ()))


def _dot(a, b, dims=NN):
    return lax.dot_general(a, b, dims, preferred_element_type=F32)


def _params(sem=("arbitrary",)):
    return pltpu.CompilerParams(dimension_semantics=sem, vmem_limit_bytes=VMEM_LIMIT)


def _full_spec(shape):
    nd = len(shape)
    return pl.BlockSpec(shape, lambda *_: (0,) * nd)


ANY = pl.BlockSpec(memory_space=pl.ANY)


def _ew(name, body, n, row_ins, pars, row_outs, accs=(), alias=None):
    n_ri, n_p, n_ro, n_acc = len(row_ins), len(pars), len(row_outs), len(accs)

    def kern(*refs):
        i = pl.program_id(0)
        ins = refs[:n_ri]
        ps = refs[n_ri:n_ri + n_p]
        outs = refs[n_ri + n_p:n_ri + n_p + n_ro]
        acc = refs[n_ri + n_p + n_ro:]
        if n_acc:
            @pl.when(i == 0)
            def _():
                for a in acc:
                    a[...] = jnp.zeros(a.shape, a.dtype)
        body(i, ins, ps, outs, acc)

    in_specs = [ANY if blk is None else pl.BlockSpec(blk, imap) for (_, blk, imap) in row_ins]
    in_specs += [_full_spec(p.shape) for p in pars]
    out_specs = [pl.BlockSpec(blk, imap) for (_, blk, imap) in row_outs] + [_full_spec(a.shape) for a in accs]
    out_shape = [s for (s, _, _) in row_outs] + list(accs)
    return pl.pallas_call(
        kern, name=name, grid=(n,), in_specs=in_specs, out_specs=out_specs, out_shape=out_shape,
        input_output_aliases=alias or {}, compiler_params=_params(),
    )(*[a for (a, _, _) in row_ins], *pars)


def _rowblk(width, colblk=0, roff=0, tile=TR):
    return (tile, width), (lambda i: (i + roff, colblk))


def _sds(shape, dtype):
    return jax.ShapeDtypeStruct(shape, dtype)


class _Carry:
    SAME_CORE = (1, 3, 5)

    def __init__(self, jobs):
        self.jobs = list(jobs)
        self.arrays = [a for _, a in self.jobs]
        self.out_shapes = [_sds(a.shape if kind == "scatter" else (N_DEV, *a.shape), a.dtype) for kind, a in self.jobs]
        n = len(self.jobs)
        self.scratch = [pltpu.SemaphoreType.DMA((n, 7)), pltpu.SemaphoreType.DMA((n, 7)), pltpu.SemaphoreType.DMA((n,))]

    def _setup(self, sems):
        send_sems, recv_sems, local_sems = sems
        x, y, c = _place()
        me = _lin(x, y, c)
        peers = [(x ^ ((k + 1) >> 2 & 1), y ^ ((k + 1) >> 1 & 1), c ^ ((k + 1) & 1)) for k in range(7)]

        def copy(a, k, sem_k, src, dst):
            return pltpu.make_async_remote_copy(src_ref=src, dst_ref=dst, send_sem=send_sems.at[a, sem_k],
                                                recv_sem=recv_sems.at[a, sem_k], device_id=peers[k], device_id_type=MESH)

        return me, [_lin(*p) for p in peers], copy, local_sems

    def _local(self, a, kind, ins, outs, me, local_sems):
        return pltpu.make_async_copy(ins[a].at[me] if kind == "scatter" else ins[a], outs[a].at[me], local_sems.at[a])

    def start(self, ins, outs, sems):
        me, theirs, copy, local_sems = self._setup(sems)
        for a, (kind, _) in enumerate(self.jobs):
            self._local(a, kind, ins, outs, me, local_sems).start()
            if kind == "scatter":
                for k in range(7):
                    copy(a, k, k, ins[a].at[theirs[k]], outs[a].at[me]).start()
            else:
                for k in (0,) + self.SAME_CORE:
                    copy(a, k, k, ins[a], outs[a].at[me]).start()

    def wait(self, ins, outs, sems):
        me, theirs, copy, local_sems = self._setup(sems)
        for a, (kind, _) in enumerate(self.jobs):
            if kind == "scatter":
                for k in range(7):
                    copy(a, k, k, ins[a].at[me], outs[a].at[theirs[k]]).wait_recv()
                for k in range(7):
                    copy(a, k, k, ins[a].at[theirs[k]], outs[a].at[me]).wait_send()
            else:
                for k in self.SAME_CORE:
                    blk = outs[a].at[theirs[k]]
                    copy(a, k, k, ins[a], blk).wait_recv()
                    copy(a, 0, k + 1, blk, blk).start()
                copy(a, 0, 0, ins[a], outs[a].at[theirs[0]]).wait_recv()
                for k in self.SAME_CORE:
                    copy(a, 0, k + 1, ins[a], outs[a].at[theirs[k + 1]]).wait_recv()
                for k in (0,) + self.SAME_CORE:
                    copy(a, k, k, ins[a], outs[a].at[me]).wait_send()
                for k in self.SAME_CORE:
                    blk = outs[a].at[theirs[k]]
                    copy(a, 0, k + 1, blk, blk).wait_send()
            self._local(a, kind, ins, outs, me, local_sems).wait()


def _carried(kern, carry, n_in, n_out, first, last):
    if carry is None:
        return kern
    nc = len(carry.jobs)

    def wrapped(*refs):
        ins, cin = refs[:n_in], refs[n_in:n_in + nc]
        outs, cout = refs[n_in + nc:n_in + nc + n_out], refs[n_in + nc + n_out:n_in + 2 * nc + n_out]
        scr, sems = refs[n_in + 2 * nc + n_out:-3], refs[-3:]

        @pl.when(first())
        def _():
            carry.start(cin, cout, sems)

        kern(*ins, *outs, *scr)

        @pl.when(last())
        def _():
            carry.wait(cin, cout, sems)

    return wrapped


def _carry_args(carry):
    if carry is None:
        return [], [], [], [], []
    n = len(carry.jobs)
    return [ANY] * n, carry.arrays, [ANY] * n, carry.out_shapes, carry.scratch


def _grid_ends(dims):
    first = lambda: functools.reduce(jnp.logical_and, [pl.program_id(d) == 0 for d in range(len(dims))])
    last = lambda: functools.reduce(jnp.logical_and, [pl.program_id(d) == n - 1 for d, n in enumerate(dims)])
    return first, last


def _mm_call(name, a, b, mode, out_dtype, tm, tn, rows_outer=True, single_b=False, carry=None):
    if mode == "TN":
        (K, M), N = a.shape, b.shape[1]
    else:
        (M, K), N = a.shape, (b.shape[1] if mode == "NN" else b.shape[0])
    assert M % tm == 0 and N % tn == 0, (name, M, N, K, tm, tn)
    ij = (lambda g0, g1: (g0, g1)) if rows_outer else (lambda g0, g1: (g1, g0))
    grid = (M // tm, N // tn) if rows_outer else (N // tn, M // tm)
    if mode == "TN":
        a_spec = pl.BlockSpec((K, tm), lambda g0, g1: (0, ij(g0, g1)[0]))
    else:
        a_spec = pl.BlockSpec((tm, K), lambda g0, g1: (ij(g0, g1)[0], 0))
    b_blk, b_map = ((tn, K), lambda g0, g1: (ij(g0, g1)[1], 0)) if mode == "NT" else \
                   ((K, tn), lambda g0, g1: (0, ij(g0, g1)[1]))
    b_spec = pl.BlockSpec(b_blk, b_map, pipeline_mode=pl.Buffered(1)) if single_b else pl.BlockSpec(b_blk, b_map)
    dims = {"NN": NN, "NT": NT, "TN": TN}[mode]

    def kern(a_ref, b_ref, o_ref):
        o_ref[...] = _dot(a_ref[...], b_ref[...], dims).astype(o_ref.dtype)

    ci, ca, co, cs, cscr = _carry_args(carry)
    res = pl.pallas_call(
        _carried(kern, carry, 2, 1, *_grid_ends(grid)), name=name, grid=grid, in_specs=[a_spec, b_spec] + ci,
        out_specs=[pl.BlockSpec((tm, tn), lambda g0, g1: ij(g0, g1))] + co,
        out_shape=[_sds((M, N), out_dtype)] + cs, scratch_shapes=cscr,
        compiler_params=_params(("arbitrary", "arbitrary")),
    )(a, b, *ca)
    return res[0] if carry is None else (res[0], res[1:])


def _mm_act(name, a, w, mode, out_dtype=BF16, carry=None):
    rows, K = a.shape
    N = w.shape[1] if mode == "NN" else w.shape[0]
    if K > D_FF:
        return _mm_call(name, a, w, mode, out_dtype, rows // 8, N, single_b=True, carry=carry)
    tn = N if N <= 1024 else 1408
    return _mm_call(name, a, w, mode, out_dtype, rows // 4, tn, carry=carry)


def _mm_wgrad(name, x, dy, out_dtype=BF16, carry=None):
    M = x.shape[1]
    tm = 1408 if M == D_FF else 512
    return _mm_call(name, x, dy, "TN", out_dtype, tm, dy.shape[1], single_b=True, carry=carry)


def _sigmoid(x):
    return 0.5 * jnp.tanh(0.5 * x) + 0.5


def _silu(x):
    return x * _sigmoid(x)


def _silu_grad(x):
    s = _sigmoid(x)
    return s * (1.0 + x * (1.0 - s))


_GELU_K = math.sqrt(2.0 / math.pi)


def _gelu(x):
    return 0.5 * x * (1.0 + jnp.tanh(_GELU_K * (x + 0.044715 * x * x * x)))


def _gelu_grad(x):
    t = jnp.tanh(_GELU_K * (x + 0.044715 * x * x * x))
    return 0.5 * (1.0 + t) + 0.5 * x * (1.0 - t * t) * _GELU_K * (1.0 + 3.0 * 0.044715 * x * x)


def _log_sigmoid(x):
    return jnp.minimum(x, 0.0) - jnp.log(1.0 + jnp.exp(-jnp.abs(x)))


def _rms(x):
    x = x.astype(F32)
    r = lax.rsqrt(jnp.mean(x * x, axis=-1, keepdims=True) + EPS)
    return x * r, r


def _rms_bwd(dy, y, r):
    return r * (dy - y * jnp.mean(dy * y, axis=-1, keepdims=True))


def _modrow(mod_ref, i, chunk):
    lo = mod_ref[0:1, chunk * D:(chunk + 1) * D]
    hi = mod_ref[1:2, chunk * D:(chunk + 1) * D]
    return jnp.where(i == 0, lo, hi)


def _acc_seg(acc_ref, i, val):
    zero = jnp.zeros_like(val)
    acc_ref[0:1, :] += jnp.where(i == 0, val, zero)
    acc_ref[1:2, :] += jnp.where(i == 0, zero, val)


def _colsum(x):
    return jnp.sum(x, axis=0, keepdims=True)


SH1, SC1, GA1, SH2, SC2, GA2 = range(6)


def _normmod_fwd(name, xa, g, mod, c_sh, c_sc):
    T = xa.shape[0]

    def body(i, ins, ps, outs, acc):
        y, _ = _rms(ins[0][...])
        h = (y * ps[0][...]) * (1.0 + _modrow(ps[1], i, c_sc)) + _modrow(ps[1], i, c_sh)
        outs[0][...] = h.astype(BF16)

    return _ew(name, body, T // TR, [(xa, *_rowblk(D))], [g, mod], [(_sds((T, D), BF16), *_rowblk(D))])[0]


def _modrows(mod_ref, row0, n, chunk):
    t = row0 + lax.broadcasted_iota(jnp.int32, (n, 1), 0)
    return jnp.where(t < CTX, mod_ref[0:1, chunk * D:(chunk + 1) * D], mod_ref[1:2, chunk * D:(chunk + 1) * D])


def _loss_resid_bwd(name, x_out, target, mat, gpost, mod, c_ga):
    T = x_out.shape[0]

    def body(i, ins, ps, outs, acc):
        err = ins[0][...] - ins[1][...]
        lat = i > 0
        dx = jnp.where(lat, err * (1.0 / D), 0.0)
        outs[0][...] = dx
        acc[2][...] += jnp.where(lat, _colsum(err * err), 0.0)
        outs[1][...] = _resid_bwd_vals(i, dx, ins[2][...], ps[0][...], ps[1], c_ga, acc[0], acc[1]).astype(BF16)

    tgt_blk = ((TR, D), lambda i: (jnp.maximum(i - 1, 0), 0))
    return _ew(name, body, T // TR, [(x_out, *_rowblk(D)), (target, *tgt_blk), (mat, *_rowblk(D))], [gpost, mod],
               [(_sds((T, D), F32), *_rowblk(D)), (_sds((T, D), BF16), *_rowblk(D))],
               [_sds((2, D), F32), _sds((1, D), F32), _sds((1, D), F32)])


def _mod_for(mod_ref, i, chunk, row0, n):
    return _modrow(mod_ref, i, chunk) if row0 is None else _modrows(mod_ref, row0, n, chunk)


def _acc_for(acc_ref, i, v, row0):
    if row0 is None:
        _acc_seg(acc_ref, i, _colsum(v))
        return

    @pl.when(row0 < CTX)
    def _():
        is_ctx = row0 + lax.broadcasted_iota(jnp.int32, (v.shape[0], 1), 0) < CTX
        acc_ref[0:1, :] += _colsum(jnp.where(is_ctx, v, 0.0))
        acc_ref[1:2, :] += _colsum(jnp.where(is_ctx, 0.0, v))

    @pl.when(row0 >= CTX)
    def _():
        acc_ref[1:2, :] += _colsum(v)


def _resid_bwd_vals(i, dout, mat, gpost, mod_ref, c_ga, acc_ga, acc_g, row0=None):
    ym, rm = _rms(mat)
    ga = _mod_for(mod_ref, i, c_ga, row0, dout.shape[0])
    _acc_for(acc_ga, i, dout * (ym * gpost), row0)
    dn = dout * ga
    acc_g[...] += _colsum(dn * ym)
    return _rms_bwd(dn * gpost, ym, rm)


def _normmod_bwd_vals(i, dh, xin, g, mod_ref, c_sh, c_sc, acc_sh, acc_sc, acc_g, row0=None):
    dh = dh.astype(F32)
    y, r = _rms(xin)
    _acc_for(acc_sc, i, dh * (y * g), row0)
    _acc_for(acc_sh, i, dh, row0)
    dyg = dh * (1.0 + _mod_for(mod_ref, i, c_sc, row0, dh.shape[0]))
    acc_g[...] += _colsum(dyg * y)
    return _rms_bwd(dyg * g, y, r)


def _parts(i, tm):
    return [(slice(0, tm), i * tm)]


FT = 1408


def _ffn_in_fused(name, h2, w_t, carry=None):
    T = h2.shape[0]
    tm, nj = T // 4, D_FF // FT

    def kern(a_ref, bg_ref, bu_ref, fg_ref, fu_ref, s_ref):
        for rows, _ in _parts(0, tm):
            a = a_ref[rows, :]
            g = _dot(a, bg_ref[...], NT)
            u = _dot(a, bu_ref[...], NT)
            fg_ref[rows, :] = g.astype(BF16)
            fu_ref[rows, :] = u.astype(BF16)
            s_ref[rows, :] = (_silu(g) * u).astype(BF16)

    o_spec = pl.BlockSpec((tm, FT), lambda i, j: (i, j))
    ci, ca, co, cs, cscr = _carry_args(carry)
    res = pl.pallas_call(
        _carried(kern, carry, 3, 3, *_grid_ends((4, nj))), name=name, grid=(4, nj),
        in_specs=[pl.BlockSpec((tm, D), lambda i, j: (i, 0)), pl.BlockSpec((FT, D), lambda i, j: (j, 0)),
                  pl.BlockSpec((FT, D), lambda i, j: (j + nj, 0))] + ci,
        out_specs=[o_spec] * 3 + co, out_shape=[_sds((T, D_FF), BF16)] * 3 + cs, scratch_shapes=cscr,
        compiler_params=_params(("arbitrary", "arbitrary")),
    )(h2, w_t, w_t, *ca)
    return res if carry is None else (res[:3], res[3:])


def _norm_chain(row0, xin, mat, gpost, mod_ref, c_ga, gnext, modn_ref, c_sh, c_sc):
    n = xin.shape[0]
    ym, _ = _rms(mat.astype(BF16))
    xo = xin + _modrows(mod_ref, row0, n, c_ga) * (ym * gpost)
    y, _ = _rms(xo)
    h = (y * gnext) * (1.0 + _modrows(modn_ref, row0, n, c_sc)) + _modrows(modn_ref, row0, n, c_sh)
    return xo, h.astype(BF16)


def _out_fused(name, p, u, o_all, xa, w_o_rnn, w_o_attn, w_out, gpost, mod, gnext):
    T = u.shape[0]
    tm = T // 8

    def kern(g0, g1, g2, g3, u_ref, o_ref, xa_ref, wr_ref, wa_ref, w_ref, gpost_ref, mod_ref, gnext_ref,
             ya_ref, yb_ref, z_ref, m_ref, x1_ref, h2_ref):
        for rows, row0 in _parts(pl.program_id(0), tm):
            ya = _dot(u_ref[rows, :], wr_ref[...]).astype(BF16)
            yb = _dot(o_ref[rows, :], wa_ref[...]).astype(BF16)
            ya_ref[rows, :] = ya
            yb_ref[rows, :] = yb
            ga = _sigmoid(jnp.concatenate([g0[rows, :], g1[rows, :]], axis=1).astype(F32))
            gb = _sigmoid(jnp.concatenate([g2[rows, :], g3[rows, :]], axis=1).astype(F32))
            z = (ga * ya.astype(F32) + gb * yb.astype(F32)).astype(BF16)
            z_ref[rows, :] = z
            m = _dot(z, w_ref[...])
            m_ref[rows, :] = m.astype(BF16)
            x1_ref[rows, :], h2_ref[rows, :] = _norm_chain(row0, xa_ref[rows, :], m, gpost_ref[...], mod_ref, GA1,
                                                           gnext_ref[...], mod_ref, SH2, SC2)

    row = lambda w: pl.BlockSpec((tm, w), lambda i: (i, 0))
    return pl.pallas_call(
        kern, name=name, grid=(T // tm,),
        in_specs=[pl.BlockSpec((tm, GLB), lambda i, q=q: (i, COL_GL // GLB + q)) for q in range(4)]
                 + [row(D), row(D), row(D)] + [_full_spec(a.shape) for a in (w_o_rnn, w_o_attn, w_out, gpost, mod, gnext)],
        out_specs=[row(D)] * 6,
        out_shape=[_sds((T, D), BF16)] * 4 + [_sds((T, D), F32), _sds((T, D), BF16)],
        compiler_params=_params(),
    )(p, p, p, p, u, o_all, xa, w_o_rnn, w_o_attn, w_out, gpost, mod, gnext)


def _ffn_out_fused(name, s, w, x1, gpost, mod, nxt=None):
    T = s.shape[0]
    tm = T // 8

    def kern(s_ref, w_ref, x1_ref, gpost_ref, mod_ref, *rest):
        for rows, row0 in _parts(pl.program_id(0), tm):
            e = _dot(s_ref[rows, :], w_ref[...])
            if nxt is None:
                e_ref, xo_ref = rest
                ym, _ = _rms(e.astype(BF16))
                xo_ref[rows, :] = x1_ref[rows, :] + _modrows(mod_ref, row0, e.shape[0], GA2) * (ym * gpost_ref[...])
            else:
                gnext_ref, modn_ref, e_ref, xo_ref, h_ref = rest
                xo_ref[rows, :], h_ref[rows, :] = _norm_chain(row0, x1_ref[rows, :], e, gpost_ref[...], mod_ref, GA2,
                                                              gnext_ref[...], modn_ref, SH1, SC1)
            e_ref[rows, :] = e.astype(BF16)

    row = lambda w_: pl.BlockSpec((tm, w_), lambda i: (i, 0))
    extra = [] if nxt is None else list(nxt)
    return pl.pallas_call(
        kern, name=name, grid=(T // tm,),
        in_specs=[row(D_FF), _full_spec(w.shape), row(D), _full_spec(gpost.shape), _full_spec(mod.shape)]
                 + [_full_spec(a.shape) for a in extra],
        out_specs=[row(D)] * (2 if nxt is None else 3),
        out_shape=[_sds((T, D), BF16), _sds((T, D), F32)] + ([] if nxt is None else [_sds((T, D), BF16)]),
        compiler_params=_params(),
    )(s, w, x1, gpost, mod, *extra)


def _ffn_bwd_fused(name, fg, fu, w, de=None, head=None):
    T = fg.shape[0]
    tm = T // 8
    row = lambda w_: pl.BlockSpec((tm, w_), lambda i: (i, 0))
    w_spec = pl.BlockSpec(w.shape, lambda i: (0, 0), pipeline_mode=pl.Buffered(1))

    def tail(rows, de_v, fg_ref, fu_ref, w_ref, df_ref):
        ds = _dot(de_v, w_ref[...], NT)
        g, u = fg_ref[rows, :].astype(F32), fu_ref[rows, :].astype(F32)
        df_ref[rows, :] = jnp.concatenate([ds * u * _silu_grad(g), ds * _silu(g)], axis=1).astype(BF16)

    if head is None:
        def kern(de_ref, fg_ref, fu_ref, w_ref, df_ref):
            for rows, _ in _parts(pl.program_id(0), tm):
                tail(rows, de_ref[rows, :], fg_ref, fu_ref, w_ref, df_ref)

        return pl.pallas_call(
            kern, name=name, grid=(T // tm,), in_specs=[row(D), row(D_FF), row(D_FF), w_spec],
            out_specs=[row(2 * D_FF)], out_shape=[_sds((T, 2 * D_FF), BF16)], compiler_params=_params(),
        )(de, fg, fu, w)

    dx2, e, gpost, mod = head

    def kern(dx_ref, e_ref, fg_ref, fu_ref, w_ref, gpost_ref, mod_ref, de_ref, df_ref, dga_ref, dg_ref):
        i = pl.program_id(0)

        @pl.when(i == 0)
        def _():
            dga_ref[...] = jnp.zeros(dga_ref.shape, F32)
            dg_ref[...] = jnp.zeros(dg_ref.shape, F32)

        for rows, row0 in _parts(i, tm):
            de_v = _resid_bwd_vals(i, dx_ref[rows, :], e_ref[rows, :], gpost_ref[...], mod_ref, GA2, dga_ref, dg_ref,
                                   row0=row0).astype(BF16)
            de_ref[rows, :] = de_v
            tail(rows, de_v, fg_ref, fu_ref, w_ref, df_ref)

    return pl.pallas_call(
        kern, name=name, grid=(T // tm,),
        in_specs=[row(D), row(D), row(D_FF), row(D_FF), w_spec, _full_spec(gpost.shape), _full_spec(mod.shape)],
        out_specs=[row(D), row(2 * D_FF), _full_spec((2, D)), _full_spec((1, D))],
        out_shape=[_sds((T, D), BF16), _sds((T, 2 * D_FF), BF16), _sds((2, D), F32), _sds((1, D), F32)],
        compiler_params=_params(),
    )(dx2, e, fg, fu, w, gpost, mod)


def _zero_at_start(i, refs):
    @pl.when(i == 0)
    def _():
        for r in refs:
            r[...] = jnp.zeros(r.shape, F32)


def _proj_bwd_fused(name, dp, dgl, w_in_t, xa, dx1, gpre, mod, carry=None):
    T = dp.shape[0]
    tm = T // 8
    row = lambda w_: pl.BlockSpec((tm, w_), lambda i: (i, 0))

    def kern(dp_ref, dgl_ref, w_ref, xa_ref, dx1_ref, g_ref, mod_ref, dxa_ref, dsh_ref, dsc_ref, dg_ref):
        i = pl.program_id(0)
        _zero_at_start(i, (dsh_ref, dsc_ref, dg_ref))
        for rows, row0 in _parts(i, tm):
            dh = _dot(dp_ref[rows, :], w_ref[0:DP_W, :]) + _dot(dgl_ref[rows, :], w_ref[DP_W:, :])
            dxa_ref[rows, :] = dx1_ref[rows, :] + _normmod_bwd_vals(i, dh, xa_ref[rows, :], g_ref[...], mod_ref, SH1,
                                                                    SC1, dsh_ref, dsc_ref, dg_ref, row0=row0)

    ci, ca, co, cs, cscr = _carry_args(carry)
    res = pl.pallas_call(
        _carried(kern, carry, 7, 4, *_grid_ends((T // tm,))), name=name, grid=(T // tm,),
        in_specs=[row(DP_W), row(P_W - DP_W),
                  pl.BlockSpec(w_in_t.shape, lambda i: (0, 0), pipeline_mode=pl.Buffered(1)), row(D), row(D),
                  _full_spec(gpre.shape), _full_spec(mod.shape)] + ci,
        out_specs=[row(D), _full_spec((2, D)), _full_spec((2, D)), _full_spec((1, D))] + co,
        out_shape=[_sds((T, D), F32), _sds((2, D), F32), _sds((2, D), F32), _sds((1, D), F32)] + cs,
        scratch_shapes=cscr, compiler_params=_params(),
    )(dp, dgl, w_in_t, xa, dx1, gpre, mod, *ca)
    return res if carry is None else (res[:4], res[4:])


def _proj_wgrad(name, dp, dgl, h, carry=None):
    T, N = h.shape
    n1, n2 = DP_W // GLB, (P_W - DP_W) // GLB

    def kern(a1_ref, a2_ref, h_ref, o_ref):
        i = pl.program_id(0)

        @pl.when(i < n1)
        def _():
            o_ref[...] = _dot(a1_ref[...], h_ref[...], TN).astype(o_ref.dtype)

        @pl.when(i >= n1)
        def _():
            o_ref[...] = _dot(a2_ref[...], h_ref[...], TN).astype(o_ref.dtype)

    ci, ca, co, cs, cscr = _carry_args(carry)
    res = pl.pallas_call(
        _carried(kern, carry, 3, 1, *_grid_ends((n1 + n2,))), name=name, grid=(n1 + n2,),
        in_specs=[pl.BlockSpec((T, GLB), lambda i: (0, jnp.minimum(i, n1 - 1))),
                  pl.BlockSpec((T, GLB), lambda i: (0, jnp.maximum(i - n1, 0))),
                  pl.BlockSpec((T, N), lambda i: (0, 0), pipeline_mode=pl.Buffered(1))] + ci,
        out_specs=[pl.BlockSpec((GLB, N), lambda i: (i, 0))] + co,
        out_shape=[_sds((P_W, N), BF16)] + cs, scratch_shapes=cscr, compiler_params=_params(),
    )(dp, dgl, h, *ca)
    return res[0] if carry is None else (res[0], res[1:])


def _ffn_in_bwd_fused(name, df, w_t, x1, dres, mat, gpre, mod, gpost, carry=None):
    T = df.shape[0]
    tm = T // 8
    row = lambda w_: pl.BlockSpec((tm, w_), lambda i: (i, 0))

    def kern(df_ref, w_ref, x1_ref, dres_ref, mat_ref, gpre_ref, mod_ref, gpost_ref,
             dx1_ref, dm_ref, dsh_ref, dsc_ref, dgpre_ref, dga_ref, dgpost_ref):
        i = pl.program_id(0)
        _zero_at_start(i, (dsh_ref, dsc_ref, dgpre_ref, dga_ref, dgpost_ref))
        for rows, row0 in _parts(i, tm):
            dh2 = _dot(df_ref[rows, :], w_ref[...])
            dx1 = dres_ref[rows, :] + _normmod_bwd_vals(i, dh2, x1_ref[rows, :], gpre_ref[...], mod_ref, SH2, SC2,
                                                        dsh_ref, dsc_ref, dgpre_ref, row0=row0)
            dx1_ref[rows, :] = dx1
            dm_ref[rows, :] = _resid_bwd_vals(i, dx1, mat_ref[rows, :], gpost_ref[...], mod_ref, GA1, dga_ref,
                                              dgpost_ref, row0=row0).astype(BF16)

    ci, ca, co, cs, cscr = _carry_args(carry)
    res = pl.pallas_call(
        _carried(kern, carry, 8, 7, *_grid_ends((T // tm,))), name=name, grid=(T // tm,),
        in_specs=[row(2 * D_FF), pl.BlockSpec(w_t.shape, lambda i: (0, 0), pipeline_mode=pl.Buffered(1)), row(D),
                  row(D), row(D), _full_spec(gpre.shape), _full_spec(mod.shape), _full_spec(gpost.shape)] + ci,
        out_specs=[row(D), row(D), _full_spec((2, D)), _full_spec((2, D)), _full_spec((1, D)), _full_spec((2, D)),
                   _full_spec((1, D))] + co,
        out_shape=[_sds((T, D), F32), _sds((T, D), BF16), _sds((2, D), F32), _sds((2, D), F32), _sds((1, D), F32),
                   _sds((2, D), F32), _sds((1, D), F32)] + cs,
        scratch_shapes=cscr, compiler_params=_params(),
    )(df, w_t, x1, dres, mat, gpre, mod, gpost, *ca)
    return res if carry is None else (res[:7], res[7:])


def _out_bwd_fused(name, dm, w_out, w_o_rnn, w_o_attn, p, ya, yb):
    T = dm.shape[0]
    tm = T // 8
    row = lambda w_: pl.BlockSpec((tm, w_), lambda i: (i, 0))

    def kern(dm_ref, w_ref, wr_ref, wa_ref, g0, g1, g2, g3, ya_ref, yb_ref, dya_ref, dyb_ref, dgl_ref, du_ref, do_ref):
        for rows, _ in _parts(pl.program_id(0), tm):
            dz = _dot(dm_ref[rows, :], w_ref[...], NT)
            ga = _sigmoid(jnp.concatenate([g0[rows, :], g1[rows, :]], axis=1).astype(F32))
            gb = _sigmoid(jnp.concatenate([g2[rows, :], g3[rows, :]], axis=1).astype(F32))
            dya = (dz * ga).astype(BF16)
            dyb = (dz * gb).astype(BF16)
            dya_ref[rows, :] = dya
            dyb_ref[rows, :] = dyb
            dgl_ref[rows, :] = jnp.concatenate([dz * ya_ref[rows, :].astype(F32) * ga * (1.0 - ga),
                                                dz * yb_ref[rows, :].astype(F32) * gb * (1.0 - gb)],
                                               axis=1).astype(BF16)
            du_ref[rows, :] = _dot(dya, wr_ref[...], NT).astype(BF16)
            do_ref[rows, :] = _dot(dyb, wa_ref[...], NT).astype(BF16)

    return pl.pallas_call(
        kern, name=name, grid=(T // tm,),
        in_specs=[row(D)] + [_full_spec(w.shape) for w in (w_out, w_o_rnn, w_o_attn)]
                 + [pl.BlockSpec((tm, GLB), lambda i, q=q: (i, COL_GL // GLB + q)) for q in range(4)] + [row(D), row(D)],
        out_specs=[row(D), row(D), row(2 * D), row(D), row(D)],
        out_shape=[_sds((T, D), BF16), _sds((T, D), BF16), _sds((T, 2 * D), BF16), _sds((T, D), BF16),
                   _sds((T, D), BF16)],
        compiler_params=_params(),
    )(dm, w_out, w_o_rnn, w_o_attn, p, p, p, p, ya, yb)


AB = 128
CTX_BLKS = CTX // AB


def _rope_tables(S):
    pos = jnp.arange(S, dtype=jnp.int32)
    inv = ROPE_BASE ** (-jnp.arange(N_FREQ, dtype=F32) / N_FREQ)
    ang_r = (pos // GRID_W).astype(F32)[:, None] * inv[None, :]
    ang_c = (pos % GRID_W).astype(F32)[:, None] * inv[None, :]
    cos = jnp.concatenate([jnp.cos(ang_r)] * 2 + [jnp.cos(ang_c)] * 2, axis=1)
    sin = jnp.concatenate([-jnp.sin(ang_r), jnp.sin(ang_r), -jnp.sin(ang_c), jnp.sin(ang_c)], axis=1)
    return cos, sin


def _rope(x, cos, sin):
    w = x.shape[1]
    reps = w // HEAD
    lane = lax.broadcasted_iota(jnp.int32, x.shape, 1)
    partner = jnp.where((lane & 63) < 32, pltpu.roll(x, w - 32, 1), pltpu.roll(x, 32, 1))
    return x * jnp.tile(cos, (1, reps)) + partner * jnp.tile(sin, (1, reps))


def _unrope(dx, cos, sin):
    w = dx.shape[1]
    reps = w // HEAD
    lane = lax.broadcasted_iota(jnp.int32, dx.shape, 1)
    t = dx * jnp.tile(sin, (1, reps))
    partner = jnp.where((lane & 63) < 32, pltpu.roll(t, w - 32, 1), pltpu.roll(t, 32, 1))
    return dx * jnp.tile(cos, (1, reps)) + partner


def _qkv_prep(name, p, cos, sin, S):
    T = CTX + S
    nt = T // AB
    KW = N_KV * HEAD

    def with_ones(v):
        ones = jnp.ones((AB, HEAD), BF16)
        return jnp.concatenate([v[:, kh * HEAD:(kh + 1) * HEAD] if part == 0 else ones
                                for kh in range(N_KV) for part in range(2)], axis=1)

    def kern(q_ref, k_ref, v_ref, cos_ref, sin_ref, qa_ref, kp_ref, vp_ref, kc_ref, vc_ref):
        i = pl.program_id(0)
        cos_v, sin_v = cos_ref[...], sin_ref[...]
        @pl.when(i < CTX_BLKS)
        def _():
            qa_ref[...] = (q_ref[...].astype(F32) * ATT_SCALE).astype(BF16)
            kc_ref[...] = k_ref[...]
            vc_ref[...] = with_ones(v_ref[...])

        @pl.when((i < CTX_BLKS) | (i >= nt))
        def _():
            kp_ref[...] = jnp.zeros(kp_ref.shape, BF16)
            vp_ref[...] = jnp.zeros(vp_ref.shape, BF16)

        @pl.when((i >= CTX_BLKS) & (i < nt))
        def _():
            qa_ref[...] = (_rope(q_ref[...].astype(F32), cos_v, sin_v) * ATT_SCALE).astype(BF16)
            kp_ref[...] = _rope(k_ref[...].astype(F32), cos_v, sin_v).astype(BF16)
            vp_ref[...] = with_ones(v_ref[...])

    tok = lambda i: jnp.minimum(i, nt - 1)
    lat_map = lambda i: (jnp.clip(i - CTX_BLKS, 0, nt - CTX_BLKS - 1), 0)
    ctx_map = lambda i: (jnp.minimum(i, CTX_BLKS - 1), 0)
    return pl.pallas_call(
        kern, name=name, grid=(nt + CTX_BLKS,),
        in_specs=[pl.BlockSpec((AB, N_Q * HEAD), lambda i: (tok(i), COL_Q // (N_Q * HEAD))),
                  pl.BlockSpec((AB, KW), lambda i: (tok(i), COL_K // KW)),
                  pl.BlockSpec((AB, KW), lambda i: (tok(i), COL_V // KW)),
                  pl.BlockSpec((AB, HEAD), lat_map), pl.BlockSpec((AB, HEAD), lat_map)],
        out_specs=[pl.BlockSpec((AB, N_Q * HEAD), lambda i: (tok(i), 0)),
                   pl.BlockSpec((AB, KW), lambda i: (i, 0)), pl.BlockSpec((AB, 2 * KW), lambda i: (i, 0)),
                   pl.BlockSpec((AB, KW), ctx_map), pl.BlockSpec((AB, 2 * KW), ctx_map)],
        out_shape=[_sds((T, N_Q * HEAD), BF16), _sds((S + 2 * CTX, KW), BF16), _sds((S + 2 * CTX, 2 * KW), BF16),
                   _sds((CTX, KW), BF16), _sds((CTX, 2 * KW), BF16)],
        compiler_params=_params(),
    )(p, p, p, cos, sin)


GW = Q_PER_KV * HEAD


def _band_bias(S):
    r = jnp.arange(AB, dtype=jnp.int32)[:, None]
    c = jnp.arange(3 * AB, dtype=jnp.int32)[None, :]
    near = jnp.abs(c - AB - r) <= AB
    valid = jnp.stack([near & (c >= AB), near, near & (c < 2 * AB)])
    return jnp.where(valid, 0.0, NEG_INF).astype(F32)


def _bias_spec(S):
    nb = S // AB
    return pl.BlockSpec((None, AB, 3 * AB), lambda kh, n: (jnp.where(n == 0, 0, jnp.where(n == nb - 1, 2, 1)), 0, 0))


def _head_probs(q, sink, kc, vce, kb, vbe, bias):
    s_c = _dot(q, kc, NT)
    m = jnp.maximum(jnp.max(s_c, axis=-1, keepdims=True), sink)
    if kb is not None:
        s_b = _dot(q, kb, NT) + bias
        m = jnp.maximum(m, jnp.max(s_b, axis=-1, keepdims=True))
    p_c = jnp.exp(s_c - m).astype(BF16)
    acc = _dot(p_c, vce)
    p_b = None
    if kb is not None:
        p_b = jnp.exp(s_b - m).astype(BF16)
        acc = acc + _dot(p_b, vbe)
    return p_c, p_b, m, acc


def _attn_fwd(name, qa, kc, vc, sink4, S, band=None, prev=None, carry=None):
    T = qa.shape[0]
    has_band = band is not None
    nq = S // AB if has_band else CTX_BLKS
    q_off = CTX_BLKS if has_band else 0

    def kern(*refs):
        q_ref, kc_ref, vc_ref, sink_ref = refs[:4]
        rest = refs[4:]
        o_ref = rest[-1]
        n = pl.program_id(1)
        kc_v, vce = kc_ref[...], vc_ref[...]
        kb = vbe = bias = None
        if has_band:
            kp_ref, vp_ref, bias_ref = rest[:3]
            start = pl.multiple_of(n * AB + (CTX - AB), AB)
            kb = kp_ref[pl.ds(start, 3 * AB), :]
            vbe = vp_ref[pl.ds(start, 3 * AB), :]
            bias = bias_ref[...]
        outs = []
        for g in range(Q_PER_KV):
            sink = sink_ref[g:g + 1, 0:1]
            _, _, m, acc = _head_probs(q_ref[:, g * HEAD:(g + 1) * HEAD], sink, kc_v, vce, kb, vbe, bias)
            l = acc[:, HEAD:] + jnp.exp(sink - m)
            outs.append(acc[:, :HEAD] / l)
        o_ref[...] = jnp.concatenate(outs, axis=1).astype(BF16)

    in_specs = [pl.BlockSpec((AB, GW), lambda kh, n: (n + q_off, kh)),
                pl.BlockSpec((CTX, HEAD), lambda kh, n: (0, kh)), pl.BlockSpec((CTX, 2 * HEAD), lambda kh, n: (0, kh)),
                pl.BlockSpec((None, Q_PER_KV, HEAD), lambda kh, n: (kh, 0, 0))]
    args = [qa, kc, vc, sink4]
    if has_band:
        in_specs += [pl.BlockSpec((S + 2 * CTX, HEAD), lambda kh, n: (0, kh)),
                     pl.BlockSpec((S + 2 * CTX, 2 * HEAD), lambda kh, n: (0, kh)), _bias_spec(S)]
        args += list(band)
    alias = {}
    if prev is not None:
        in_specs.append(ANY)
        alias = {len(args): 0}
        args.append(prev)
    ci, ca, co, cs, cscr = _carry_args(carry)
    res = pl.pallas_call(
        _carried(kern, carry, len(args), 1, *_grid_ends((N_KV, nq))), name=name, grid=(N_KV, nq),
        in_specs=in_specs + ci,
        out_specs=[pl.BlockSpec((AB, GW), lambda kh, n: (n + q_off, kh))] + co,
        out_shape=[_sds((T, N_Q * HEAD), BF16)] + cs, input_output_aliases=alias, scratch_shapes=cscr,
        compiler_params=_params(("arbitrary", "arbitrary")),
    )(*args, *ca)
    return res[0] if carry is None else (res[0], res[1:])


def _attn_bwd(name, qa, kc, vc, sink4, o_all, do_all, S, band=None, prev_dq=None, carry=None):
    T = qa.shape[0]
    has_band = band is not None
    nq = S // AB if has_band else CTX_BLKS
    q_off = CTX_BLKS if has_band else 0
    KW = N_KV * HEAD

    def kern(*refs):
        q_ref, kc_ref, vc_ref, sink_ref, o_ref, do_ref = refs[:6]
        rest = refs[6:]
        if has_band:
            kp_ref, vp_ref, bias_ref = rest[:3]
            rest = rest[3:]
        if prev_dq is not None:
            rest = rest[1:]
        dq_ref, dkc_ref, dvc_ref, dsink_ref = rest[:4]
        n = pl.program_id(1)

        @pl.when(n == 0)
        def _():
            dkc_ref[...] = jnp.zeros(dkc_ref.shape, F32)
            dvc_ref[...] = jnp.zeros(dvc_ref.shape, F32)
            dsink_ref[...] = jnp.zeros(dsink_ref.shape, F32)
            if has_band:
                rest[4][...] = jnp.zeros(rest[4].shape, F32)
                rest[5][...] = jnp.zeros(rest[5].shape, F32)

        kc_v, vce = kc_ref[...], vc_ref[...]
        vc_v = vce[:, :HEAD]
        kb = vbe = vb = bias = None
        if has_band:
            start = pl.multiple_of(n * AB + (CTX - AB), AB)
            kb = kp_ref[pl.ds(start, 3 * AB), :]
            vbe = vp_ref[pl.ds(start, 3 * AB), :]
            vb = vbe[:, :HEAD]
            bias = bias_ref[...]
        stack = lambda ref: jnp.concatenate([ref[:, g * HEAD:(g + 1) * HEAD] for g in range(Q_PER_KV)], axis=0)
        q4, do4 = stack(q_ref), stack(do_ref)
        sink = jnp.concatenate([jnp.broadcast_to(sink_ref[g:g + 1, 0:1], (AB, 1)) for g in range(Q_PER_KV)], axis=0)
        s_c = _dot(q4, kc_v, NT)
        m = jnp.maximum(jnp.max(s_c, axis=-1, keepdims=True), sink)
        if has_band:
            s_b = _dot(q4, kb, NT) + jnp.tile(bias, (Q_PER_KV, 1))
            m = jnp.maximum(m, jnp.max(s_b, axis=-1, keepdims=True))
        p_c = jnp.exp(s_c - m).astype(BF16).astype(F32)
        p_sink = jnp.exp(sink - m)
        l = jnp.sum(p_c, axis=-1, keepdims=True) + p_sink
        if has_band:
            p_b = jnp.exp(s_b - m).astype(BF16).astype(F32)
            l = l + jnp.sum(p_b, axis=-1, keepdims=True)
        inv = 1.0 / l
        delta = jnp.sum(do4.astype(F32) * stack(o_ref).astype(F32), axis=-1, keepdims=True)
        do4b = do4.astype(BF16)
        pn_c = (p_c * inv).astype(BF16)
        ds_c = (p_c * inv * (_dot(do4b, vc_v, NT) - delta)).astype(BF16)
        dq4 = _dot(ds_c, kc_v)
        dkc_ref[...] += _dot(ds_c, q4, TN)
        dvc_ref[...] += _dot(pn_c, do4b, TN)
        if has_band:
            pn_b = (p_b * inv).astype(BF16)
            ds_b = (p_b * inv * (_dot(do4b, vb, NT) - delta)).astype(BF16)
            dq4 = dq4 + _dot(ds_b, kb)
            rest[4][pl.ds(start, 3 * AB), :] += _dot(ds_b, q4, TN)
            rest[5][pl.ds(start, 3 * AB), :] += _dot(pn_b, do4b, TN)
        dq4 = dq4 * ATT_SCALE
        dq_ref[...] = jnp.concatenate([dq4[g * AB:(g + 1) * AB, :] for g in range(Q_PER_KV)], axis=1)
        ps = p_sink * inv * delta
        dsink_ref[...] += jnp.concatenate(
            [jnp.broadcast_to(-jnp.sum(ps[g * AB:(g + 1) * AB, :], axis=0, keepdims=True), (1, HEAD))
             for g in range(Q_PER_KV)], axis=0)

    q_spec = pl.BlockSpec((AB, GW), lambda kh, n: (n + q_off, kh))
    c_spec = pl.BlockSpec((CTX, HEAD), lambda kh, n: (0, kh))
    ce_spec = pl.BlockSpec((CTX, 2 * HEAD), lambda kh, n: (0, kh))
    s_spec = pl.BlockSpec((None, Q_PER_KV, HEAD), lambda kh, n: (kh, 0, 0))
    in_specs = [q_spec, c_spec, ce_spec, s_spec, q_spec, q_spec]
    args = [qa, kc, vc, sink4, o_all, do_all]
    out_specs = [q_spec, c_spec, c_spec, s_spec]
    out_shape = [_sds((T, N_Q * HEAD), F32), _sds((CTX, KW), F32), _sds((CTX, KW), F32), _sds((N_KV, Q_PER_KV, HEAD), F32)]
    if has_band:
        p_spec = pl.BlockSpec((S + 2 * CTX, HEAD), lambda kh, n: (0, kh))
        in_specs += [p_spec, pl.BlockSpec((S + 2 * CTX, 2 * HEAD), lambda kh, n: (0, kh)), _bias_spec(S)]
        args += list(band)
        out_specs += [p_spec, p_spec]
        out_shape += [_sds((S + 2 * CTX, KW), F32)] * 2
    alias = {}
    if prev_dq is not None:
        in_specs.append(ANY)
        alias = {len(args): 0}
        args.append(prev_dq)
    ci, ca, co, cs, cscr = _carry_args(carry)
    n_out = len(out_specs)
    res = pl.pallas_call(
        _carried(kern, carry, len(args), n_out, *_grid_ends((N_KV, nq))), name=name, grid=(N_KV, nq),
        in_specs=in_specs + ci, out_specs=out_specs + co, out_shape=out_shape + cs, scratch_shapes=cscr,
        input_output_aliases=alias, compiler_params=_params(("arbitrary", "arbitrary")),
    )(*args, *ca)
    return res if carry is None else (res[:n_out], res[n_out:])


def _dqkv_assemble(name, dq_all, dkp, dvp, dkc_l, dvc_l, dkc_c, dvc_c, cos, sin, S):
    T = CTX + S
    KW = N_KV * HEAD
    HALF = N_Q * HEAD // 2

    def kern(dq_ref, dkp_ref, dvp_ref, dkcl_ref, dvcl_ref, dkcc_ref, dvcc_ref, cos_ref, sin_ref, out_ref):
        i = pl.program_id(0)
        j = pl.program_id(1)
        cos_v, sin_v = cos_ref[...], sin_ref[...]

        @pl.when((j < 2) & (i == 0))
        def _():
            out_ref[...] = dq_ref[...].astype(BF16)

        @pl.when((j < 2) & (i > 0))
        def _():
            out_ref[...] = _unrope(dq_ref[...], cos_v, sin_v).astype(BF16)

        @pl.when((j == 2) & (i == 0))
        def _():
            out_ref[...] = jnp.concatenate([dkcl_ref[...] + dkcc_ref[...], dvcl_ref[...] + dvcc_ref[...]],
                                           axis=1).astype(BF16)

        @pl.when((j == 2) & (i > 0))
        def _():
            out_ref[...] = jnp.concatenate([_unrope(dkp_ref[...], cos_v, sin_v), dvp_ref[...]], axis=1).astype(BF16)

    same = lambda i, j: (i, 0)
    lat_map = lambda i, j: (jnp.maximum(i - 1, 0), 0)
    ctx_map = lambda i, j: (0, 0)
    return pl.pallas_call(
        kern, name=name, grid=(T // TR, 3),
        in_specs=[pl.BlockSpec((TR, HALF), lambda i, j: (i, jnp.minimum(j, 1))),
                  pl.BlockSpec((TR, KW), same), pl.BlockSpec((TR, KW), same),
                  pl.BlockSpec((CTX, KW), ctx_map), pl.BlockSpec((CTX, KW), ctx_map),
                  pl.BlockSpec((CTX, KW), ctx_map), pl.BlockSpec((CTX, KW), ctx_map),
                  pl.BlockSpec((TR, HEAD), lat_map), pl.BlockSpec((TR, HEAD), lat_map)],
        out_specs=pl.BlockSpec((TR, HALF), lambda i, j: (i, COL_Q // HALF + j)),
        out_shape=_sds((T, DP_W), BF16), compiler_params=_params(("arbitrary", "arbitrary")),
    )(dq_all, dkp, dvp, dkc_l, dvc_l, dkc_c, dvc_c, cos, sin)


RB = 128
CH = 256
HALO = 8
SUB = 8
GRP = 8


def _vscan(a, b, reverse):
    row = lax.broadcasted_iota(jnp.int32, a.shape, 0)
    A, H = a, b
    for s in (1, 2, 4):
        sh = SUB - s if reverse else s
        m = (row < SUB - s) if reverse else (row >= s)
        As = pltpu.roll(A, sh, 0)
        Hs = pltpu.roll(H, sh, 0)
        H = jnp.where(m, A * Hs + H, H)
        A = jnp.where(m, A * As, A)
    return A, H


def _scan_rows(a_ref, b_ref, r0, nrows, reverse, carry, emit):
    ngrp = nrows // (SUB * GRP)
    row = lax.broadcasted_iota(jnp.int32, (SUB, RB), 0)

    def grp(gi, carry):
        g = (ngrp - 1 - gi) if reverse else gi
        base = r0 + g * (SUB * GRP)
        for v in (range(GRP - 1, -1, -1) if reverse else range(GRP)):
            rs = pl.multiple_of(base + v * SUB, SUB)
            A, H = _vscan(a_ref[pl.ds(rs, SUB), :], b_ref[pl.ds(rs, SUB), :], reverse)
            hf = H + A * carry
            if reverse:
                before = jnp.where(row == SUB - 1, carry, pltpu.roll(hf, SUB - 1, 0))
                carry = hf[0:1, :]
            else:
                before = jnp.where(row == 0, carry, pltpu.roll(hf, 1, 0))
                carry = hf[SUB - 1:SUB, :]
            emit(rs, hf, before)
        return carry

    return lax.fori_loop(0, ngrp, grp, carry)


def _pad_start(ci):
    return pl.multiple_of(ci * CH + HALO * jnp.minimum(ci, 1), HALO)


def _conv_taps(ext, transpose=False):
    n = CH + 2 * HALO
    taps = []
    for k in range(CONV_W):
        off = CONV_LEFT - k if transpose else k - CONV_LEFT
        taps.append(ext[HALO:HALO + CH, :] if off == 0 else pltpu.roll(ext, (-off) % n, 0)[HALO:HALO + CH, :])
    return taps


def _lru_gates(xl, w4, b4, ls):
    pre = _dot(xl.astype(BF16), w4) + b4
    out = []
    for d in range(2):
        r = _sigmoid(pre[:, d * RB:(d + 1) * RB])
        i = _sigmoid(pre[:, (2 + d) * RB:(3 + d) * RB])
        la = LRU_C * r * ls[d:d + 1, :]
        a = jnp.exp(la)
        q = -jnp.tanh(la) * (1.0 + a * a)
        out.append((r, i, a, q))
    return out


def _rnn_specs(T):
    col = lambda n, *_: (0, n)
    return dict(
        xr=pl.BlockSpec((T, RB), lambda n, *_: (0, COL_XR // RB + n)),
        gr=pl.BlockSpec((T, RB), lambda n, *_: (0, COL_GR // RB + n)),
        act=pl.BlockSpec((T, RB), col),
        cw=pl.BlockSpec((CONV_W, RB), col), cb=pl.BlockSpec((1, RB), col),
        w4=pl.BlockSpec((None, RB, 4 * RB), lambda n, *_: (n, 0, 0)),
        b4=pl.BlockSpec((None, 1, 4 * RB), lambda n, *_: (n, 0, 0)),
        lam=pl.BlockSpec((2, RB), col))


PAD_ROWS = 3 * HALO


def _zero_pads(pad_ref, T):
    for r in (0, HALO + CTX, 2 * HALO + T):
        pad_ref[r:r + HALO, :] = jnp.zeros((HALO, RB), F32)


def _fill_padded(pad_ref, src_ref, T):
    _zero_pads(pad_ref, T)
    pad_ref[HALO:HALO + CTX, :] = src_ref[0:CTX, :].astype(F32)
    pad_ref[2 * HALO + CTX:2 * HALO + T, :] = src_ref[CTX:T, :].astype(F32)


def _pad_rows(ci):
    return pl.ds(pl.multiple_of(ci * CH + HALO + HALO * jnp.minimum(ci, 1), HALO), CH)


def _rnn_fwd(name, p, cw, cb, w4, b4, lam, T, carry=None):
    def kern(xr_ref, gr_ref, cw_ref, cb_ref, w4_ref, b4_ref, lam_ref,
             u_ref, a0, a1, yo_ref, hpf_ref, hpb_ref, r0_ref, r1_ref, i0_ref, i1_ref, xpad, b0, b1, y):
        _fill_padded(xpad, xr_ref, T)
        ls = _log_sigmoid(lam_ref[...])
        w4v, b4v, cwv, cbv = w4_ref[...], b4_ref[...], cw_ref[...], cb_ref[...]

        def chunk(ci, _):
            rows = pl.ds(pl.multiple_of(ci * CH, CH), CH)
            taps = _conv_taps(xpad[pl.ds(_pad_start(ci), CH + 2 * HALO), :])
            xl = cbv + sum(taps[k] * cwv[k:k + 1, :] for k in range(CONV_W))
            for d, (r, i, a, q) in enumerate(_lru_gates(xl, w4v, b4v, ls)):
                (a0, a1)[d][rows, :] = a
                (b0, b1)[d][rows, :] = jnp.sqrt(q) * (i * xl)
                (r0_ref, r1_ref)[d][rows, :] = r.astype(BF16)
                (i0_ref, i1_ref)[d][rows, :] = i.astype(BF16)
            return 0

        lax.fori_loop(0, T // CH, chunk, 0)
        zero = jnp.zeros((1, RB), F32)

        def emit_f(rs, hf, before):
            y[pl.ds(rs, SUB), :] = hf
            b0[pl.ds(rs, SUB), :] = before

        def emit_b(rs, hf, before):
            y[pl.ds(rs, SUB), :] += hf
            b1[pl.ds(rs, SUB), :] = before

        _scan_rows(a0, b0, 0, T, False, zero, emit_f)
        c = _scan_rows(a1, b1, 0, CTX, True, zero, emit_b)
        _scan_rows(a1, b1, CTX, T - CTX, True, c, emit_b)

        def finish(ci, _):
            rows = pl.ds(pl.multiple_of(ci * CH, CH), CH)
            yv = y[rows, :]
            u_ref[rows, :] = (yv * _gelu(gr_ref[rows, :].astype(F32))).astype(BF16)
            yo_ref[rows, :] = yv.astype(BF16)
            hpf_ref[rows, :] = b0[rows, :].astype(BF16)
            hpb_ref[rows, :] = b1[rows, :].astype(BF16)
            return 0

        lax.fori_loop(0, T // CH, finish, 0)

    sp = _rnn_specs(T)
    ci, ca, co, cs, cscr = _carry_args(carry)
    dts = [BF16, F32, F32] + [BF16] * 7
    res = pl.pallas_call(
        _carried(kern, carry, 7, 10, *_grid_ends((N_RNN_BLOCKS,))), name=name, grid=(N_RNN_BLOCKS,),
        in_specs=[sp["xr"], sp["gr"], sp["cw"], sp["cb"], sp["w4"], sp["b4"], sp["lam"]] + ci,
        out_specs=[sp["act"]] * 10 + co,
        out_shape=[_sds((T, D), dt) for dt in dts] + cs,
        scratch_shapes=[pltpu.VMEM((T + PAD_ROWS, RB), F32)] + [pltpu.VMEM((T, RB), F32)] * 3 + cscr,
        compiler_params=_params(),
    )(p, p, cw, cb, w4, b4, lam, *ca)
    return res if carry is None else (res[:10], res[10:])


def _rnn_bwd(name, p, du, saved, dp, cw, cb, w4, b4, lam, T, carry=None):
    def kern(xr_ref, gr_ref, du_ref, a0, a1, y_ref, hpf_ref, hpb_ref, r0_ref, r1_ref, i0_ref, i1_ref,
             cw_ref, cb_ref, w4_ref, b4_ref, lam_ref, dp_in,
             dp_ref, dcw_ref, dcb_ref, dw4_ref, db4_ref, dlam_ref,
             xpad, dxpad, c0, c1, dy):
        j = pl.program_id(1)

        @pl.when(j == 0)
        def _():
            scans(gr_ref, du_ref, a0, a1, y_ref, dp_ref, c0, c1, dy)

        @pl.when(j == 1)
        def _():
            gates(xr_ref, a0, a1, (hpf_ref, hpb_ref), (r0_ref, r1_ref), (i0_ref, i1_ref), cw_ref, cb_ref, w4_ref,
                  lam_ref, dp_ref, dcw_ref, dcb_ref, dw4_ref, db4_ref, dlam_ref, xpad, dxpad, c0, c1)

    def scans(gr_ref, du_ref, a0, a1, y_ref, dgr_ref, c0, c1, dy):
        def phase_a(ci, _):
            rows = pl.ds(pl.multiple_of(ci * CH, CH), CH)
            gr = gr_ref[rows, :].astype(F32)
            duv = du_ref[rows, :].astype(F32)
            dyv = duv * _gelu(gr)
            dgr_ref[rows, :] = (duv * y_ref[rows, :].astype(F32) * _gelu_grad(gr)).astype(BF16)
            dy[rows, :] = dyv
            c0[rows, :] = a0[rows, :] * dyv
            c1[rows, :] = a1[rows, :] * dyv
            return 0

        lax.fori_loop(0, T // CH, phase_a, 0)
        zero = jnp.zeros((1, RB), F32)

        def emit0(rs, hf, before):
            c0[pl.ds(rs, SUB), :] = dy[pl.ds(rs, SUB), :] + before

        def emit1(rs, hf, before):
            c1[pl.ds(rs, SUB), :] = dy[pl.ds(rs, SUB), :] + before

        _scan_rows(a0, c0, 0, T, True, zero, emit0)
        c = _scan_rows(a1, c1, CTX, T - CTX, False, zero, emit1)
        _scan_rows(a1, c1, 0, CTX, False, c, emit1)

    def gates(xr_ref, a0, a1, hp_refs, r_refs, i_refs, cw_ref, cb_ref, w4_ref, lam_ref,
              dxr_ref, dcw_ref, dcb_ref, dw4_ref, db4_ref, dlam_ref, xpad, dxpad, c0, c1):
        _fill_padded(xpad, xr_ref, T)
        _zero_pads(dxpad, T)
        lam_v = lam_ref[...]
        ls = _log_sigmoid(lam_v)
        w4v, cwv, cbv = w4_ref[...], cw_ref[...], cb_ref[...]

        def conv_chunk(ci):
            taps = _conv_taps(xpad[pl.ds(_pad_start(ci), CH + 2 * HALO), :])
            return taps, cbv + sum(taps[k] * cwv[k:k + 1, :] for k in range(CONV_W))

        dw4_ref[...] = jnp.zeros(dw4_ref.shape, F32)
        db4_ref[...] = jnp.zeros(db4_ref.shape, F32)
        dlam_ref[...] = jnp.zeros(dlam_ref.shape, F32)
        dcw_ref[...] = jnp.zeros(dcw_ref.shape, F32)
        dcb_ref[...] = jnp.zeros(dcb_ref.shape, F32)

        def phase_c(ci, _):
            base = pl.multiple_of(ci * CH, CH)
            rows = pl.ds(base, CH)
            _, xl = conv_chunk(ci)
            dxl = jnp.zeros((CH, RB), F32)
            dpre_a, dpre_x, dls = [], [], []
            for d in range(2):
                a = (a0, a1)[d][rows, :]
                r = r_refs[d][rows, :].astype(F32)
                i = i_refs[d][rows, :].astype(F32)
                q = -jnp.tanh(LRU_C * r * ls[d:d + 1, :]) * (1.0 + a * a)
                g = (c0, c1)[d][rows, :]
                hp = hp_refs[d][rows, :].astype(F32)
                gm = g * jnp.sqrt(q)
                di = gm * xl
                dxl = dxl + gm * i
                dla = a * (g * hp - a * (g * (i * xl)) * lax.rsqrt(q))
                dr = dla * (LRU_C * ls[d:d + 1, :])
                dls.append(_colsum(dla * (LRU_C * r)))
                dpre_a.append(dr * r * (1.0 - r))
                dpre_x.append(di * i * (1.0 - i))
            dpre = jnp.concatenate(dpre_a + dpre_x, axis=1)
            dpre_b = dpre.astype(BF16)
            dxl = dxl + _dot(dpre_b, w4v, NT)
            dw4_ref[...] += _dot(xl.astype(BF16), dpre_b, TN)
            db4_ref[...] += _colsum(dpre)
            dlam_ref[...] += jnp.concatenate(dls, axis=0)
            dcb_ref[...] += _colsum(dxl)
            dxpad[_pad_rows(ci), :] = dxl
            return 0

        lax.fori_loop(0, T // CH, phase_c, 0)
        dlam_ref[...] = dlam_ref[...] * _sigmoid(-lam_v)

        def phase_d(ci, _):
            base = pl.multiple_of(ci * CH, CH)
            rows = pl.ds(base, CH)
            xtaps, _ = conv_chunk(ci)
            dtaps = _conv_taps(dxpad[pl.ds(_pad_start(ci), CH + 2 * HALO), :], transpose=True)
            dxl = dxpad[_pad_rows(ci), :]
            dxr_ref[rows, :] = sum(dtaps[k] * cwv[k:k + 1, :] for k in range(CONV_W)).astype(BF16)
            dcw_ref[...] += jnp.concatenate([_colsum(dxl * xtaps[k]) for k in range(CONV_W)], axis=0)
            return 0

        lax.fori_loop(0, T // CH, phase_d, 0)

    sp = _rnn_specs(T)
    dp_spec = pl.BlockSpec((T, RB), lambda n, j: (0, COL_GR // RB + n - j * (COL_GR - COL_XR) // RB))
    ci, ca, co, cs, cscr = _carry_args(carry)
    n_in = 3 + len(saved) + 5 + 1
    res = pl.pallas_call(
        _carried(kern, carry, n_in, 6, *_grid_ends((N_RNN_BLOCKS, 2))), name=name, grid=(N_RNN_BLOCKS, 2),
        in_specs=[sp["xr"], sp["gr"]] + [sp["act"]] * (1 + len(saved)) + [sp["cw"], sp["cb"], sp["w4"], sp["b4"],
                                                                           sp["lam"], ANY] + ci,
        out_specs=[dp_spec, sp["cw"], sp["cb"], sp["w4"], sp["b4"], sp["lam"]] + co,
        out_shape=[_sds((T, DP_W), BF16), _sds((CONV_W, D), F32), _sds((1, D), F32),
                   _sds((N_RNN_BLOCKS, RB, 4 * RB), F32), _sds((N_RNN_BLOCKS, 1, 4 * RB), F32), _sds((2, D), F32)] + cs,
        scratch_shapes=[pltpu.VMEM((T + PAD_ROWS, RB), F32)] * 2 + [pltpu.VMEM((T, RB), F32)] * 3 + cscr,
        input_output_aliases={n_in - 1: 0},
        compiler_params=_params(("arbitrary", "arbitrary")),
    )(p, p, du, *saved, cw, cb, w4, b4, lam, dp, *ca)
    return res if carry is None else (res[:6], res[6:])


class _Plan:
    def __init__(self, shards, Ws):
        L = len(Ws)
        self.shards, self.Ws = shards, Ws
        self.Gs = [None] * L
        self.slots = [dict() for _ in range(L)]
        self.gate_slots = [None] * L
        self.table = {}
        for l in range(L):
            t = f"l{l}_"
            self.table[t + "proj"] = [("gather", l, k) for k in ("wo_rnn", "wo_attn", "wout")]
            self.table[t + "rnn_fwd"] = [("gather", l, "wffn_in_t")]
            self.table[t + "attn_lat_fwd"] = [("gather", l + 1, "win_t")] if l + 1 < L else []
            self.table[t + "ffn_in"] = [("gather", l, "wffn_out")]
            self.table[t + "ffn_in_dx"] = [("scatter", l, "wffn_out")]
            self.table[t + "attn_lat_bwd"] = [("scatter", l, "wffn_in_t")]
            self.table[t + "ffn_in_dw"] = [("gates", l + 1, "w4")] if l + 1 < L else []
            self.table[t + "proj_dx"] = [("scatter", l, "win_t_a")]
            self.table[t + "rnn_bwd"] = ([("scatter", l, k) for k in ("wout", "wo_attn", "wo_rnn")]
                                         + ([("scatter", l + 1, "win_t_b")] if l + 1 < L else []))
        self.table["l0_proj_dw_b"] = [("gates", 0, "w4")]

    def carry(self, name):
        jobs = []
        for kind, l, k in self.table.get(name, []):
            if kind == "gather":
                jobs.append(("gather", self.shards[l][k]))
            elif kind == "scatter":
                jobs.append(("scatter", self.Gs[l][k].reshape(N_DEV, -1, self.Gs[l][k].shape[-1])))
            else:
                jobs.append(("gather", self.Gs[l]["w4"].reshape(N_RNN_BLOCKS * RB, 4 * RB).astype(BF16)))
        return _Carry(jobs) if jobs else None

    def done(self, name, got):
        for (kind, l, k), res in zip(self.table[name], got):
            if kind == "gather":
                self.Ws[l][k] = res.reshape(-1, D)
            elif kind == "scatter":
                self.slots[l][k] = res
            else:
                self.gate_slots[l] = res


def _run(X, fn, name, *args, **kw):
    carry = None if X is None else X.carry(name)
    if carry is None:
        return fn(name, *args, **kw)
    out, got = fn(name, *args, carry=carry, **kw)
    X.done(name, got)
    return out


def _layer_fwd(l, xa, h, W, rope, S, nxt, X=None):
    T = xa.shape[0]
    tag = f"l{l}_"
    cos, sin, bias = rope
    p = _run(X, _mm_act, tag + "proj", h, W["win_t"], "NT", BF16)
    u, *rnn_saved = _run(X, _rnn_fwd, tag + "rnn_fwd", p, W["cw"], W["cb"], W["w4"], W["b4"], W["lam"], T)
    qa, kp, vp, kc, vc = _qkv_prep(tag + "qkv_prep", p, cos, sin, S)
    o_all = _attn_fwd(tag + "attn_ctx_fwd", qa, kc, vc, W["sink4"], S)
    o_all = _run(X, _attn_fwd, tag + "attn_lat_fwd", qa, kc, vc, W["sink4"], S, band=(kp, vp, bias), prev=o_all)
    ya, yb, z, m, x1, h2 = _out_fused(tag + "out", p, u, o_all, xa, W["wo_rnn"], W["wo_attn"], W["wout"],
                                      W["g_mix_post"], W["mod"], W["g_ffn_pre"])
    fg, fu, s = _run(X, _ffn_in_fused, tag + "ffn_in", h2, W["wffn_in_t"])
    e, *out = _ffn_out_fused(tag + "ffn_out", s, W["wffn_out"], x1, W["g_ffn_post"], W["mod"], nxt)
    saved = dict(xa=xa, h=h, p=p, u=u, rnn=rnn_saved, qa=qa, kp=kp, vp=vp, kc=kc, vc=vc, o_all=o_all,
                 ya=ya, yb=yb, z=z, m=m, x1=x1, h2=h2, fg=fg, fu=fu, s=s, e=e)
    return saved, out


def _layer_bwd(l, dx2, A, W, rope, S, X=None, loss_of=None):
    T = A["xa"].shape[0]
    tag = f"l{l}_"
    cos, sin, bias = rope
    G = {}
    if X is not None:
        X.Gs[l] = G
    if loss_of is None:
        de, df, dga2, G["g_ffn_post"] = _ffn_bwd_fused(tag + "ffn_bwd", A["fg"], A["fu"], W["wffn_out"],
                                                       head=(dx2, A["e"], W["g_ffn_post"], W["mod"]))
    else:
        dx2, de, dga2, G["g_ffn_post"], G["sq"] = _loss_resid_bwd(tag + "loss_ffn_resid_bwd", *loss_of, A["e"],
                                                                  W["g_ffn_post"], W["mod"], GA2)
        df, = _ffn_bwd_fused(tag + "ffn_bwd", A["fg"], A["fu"], W["wffn_out"], de=de)
    G["wffn_out"] = _mm_wgrad(tag + "ffn_out_dw", A["s"], de)
    dx1, dm, dsh2, dsc2, G["g_ffn_pre"], dga1, G["g_mix_post"] = _run(
        X, _ffn_in_bwd_fused, tag + "ffn_in_dx", df, W["wffn_in_t"], A["x1"], dx2, A["m"], W["g_ffn_pre"], W["mod"],
        W["g_mix_post"])
    G["wffn_in_t"] = _run(X, _mm_wgrad, tag + "ffn_in_dw", df, A["h2"])
    G["wout"] = _mm_wgrad(tag + "out_dw", A["z"], dm)
    dya, dyb, dgl, du, do = _out_bwd_fused(tag + "out_dx", dm, W["wout"], W["wo_rnn"], W["wo_attn"], A["p"], A["ya"],
                                           A["yb"])
    G["wo_attn"] = _mm_wgrad(tag + "o_attn_dw", A["o_all"], dyb)
    G["wo_rnn"] = _mm_wgrad(tag + "o_rnn_dw", A["u"], dya)
    dq_all, dkc_c, dvc_c, dsink_c = _attn_bwd(tag + "attn_ctx_bwd", A["qa"], A["kc"], A["vc"], W["sink4"],
                                               A["o_all"], do, S)
    dq_all, dkc_l, dvc_l, dsink_l, dkp, dvp = _run(
        X, _attn_bwd, tag + "attn_lat_bwd", A["qa"], A["kc"], A["vc"], W["sink4"], A["o_all"], do, S,
        band=(A["kp"], A["vp"], bias), prev_dq=dq_all)
    G["sink4"] = dsink_c + dsink_l
    dp = _dqkv_assemble(tag + "dqkv", dq_all, dkp, dvp, dkc_l, dvc_l, dkc_c, dvc_c, cos, sin, S)
    dp, G["cw"], G["cb"], G["w4"], G["b4"], G["lam"] = _run(
        X, _rnn_bwd, tag + "rnn_bwd", A["p"], du, A["rnn"], dp, W["cw"], W["cb"], W["w4"], W["b4"], W["lam"], T)
    proj_dx = (_proj_bwd_fused, tag + "proj_dx", dp, dgl, W["win_t"], A["xa"], dx1, W["g_mix_pre"], W["mod"])
    if X is not None:
        G["win_t_a"] = _proj_wgrad(tag + "proj_dw_a", dp, dgl, A["h"][:, :D // 2])
        dxa, dsh1, dsc1, G["g_mix_pre"] = _run(X, *proj_dx)
        G["win_t_b"] = _run(X, _proj_wgrad, tag + "proj_dw_b", dp, dgl, A["h"][:, D // 2:])
    else:
        dxa, dsh1, dsc1, G["g_mix_pre"] = _run(X, *proj_dx)
        G["win_t"] = _proj_wgrad(tag + "proj_dw", dp, dgl, A["h"])
    G["mod"] = jnp.concatenate([dsh1, dsc1, dga1, dsh2, dsc2, dga2], axis=1)
    return dxa, G


def _local_step(xa, target, Ws, S, X=None):
    rope = (*_rope_tables(S), _band_bias(S))
    L = len(Ws)
    h = _normmod_fwd("l0_mix_norm", xa, Ws[0]["g_mix_pre"], Ws[0]["mod"], SH1, SC1)
    saved = []
    x = xa
    for l in range(L):
        nxt = (Ws[l + 1]["g_mix_pre"], Ws[l + 1]["mod"]) if l + 1 < L else None
        A, out = _layer_fwd(l, x, h, Ws[l], rope, S, nxt, X)
        saved.append(A)
        if l + 1 < L:
            x, h = out
    Gs = [None] * L
    dx = None
    for l in reversed(range(L)):
        dx, Gs[l] = _layer_bwd(l, dx, saved[l], Ws[l], rope, S, X, loss_of=(out[0], target) if l == L - 1 else None)
    return Gs[L - 1]["sq"], dx, Gs


MESH = pl.DeviceIdType.MESH


def _place():
    return lax.axis_index("x"), lax.axis_index("y"), lax.axis_index("c")


def _lin(px, py, pc):
    return 4 * px + 2 * py + pc


def _allgather_small(name, blk):
    m, n = blk.shape

    def body(x_ref, out_ref, send_sems, recv_sems, local_sem):
        x, y, c = _place()
        me, sibling = (x, y, c), (x, y, 1 - c)
        chips = [(1 - x, y), (x, 1 - y), (1 - x, 1 - y)]

        def copy(k, block, to, src=None):
            dst = out_ref.at[_lin(*block)]
            return pltpu.make_async_remote_copy(src_ref=dst if src is None else src, dst_ref=dst,
                                                send_sem=send_sems.at[k], recv_sem=recv_sems.at[k],
                                                device_id=to, device_id_type=MESH)

        mine = pltpu.make_async_copy(x_ref, out_ref.at[_lin(*me)], local_sem)
        mine.start()
        first = [copy(0, me, sibling, src=x_ref)]
        first += [copy(1 + j, me, (*chip, c), src=x_ref) for j, chip in enumerate(chips)]
        for cp in first:
            cp.start()
        passed = [copy(4 + j, (*chip, c), sibling) for j, chip in enumerate(chips)]
        for j, chip in enumerate(chips):
            copy(1 + j, (*chip, c), me).wait_recv()
            passed[j].start()
        copy(0, sibling, me).wait_recv()
        for j, chip in enumerate(chips):
            copy(4 + j, (*chip, 1 - c), me).wait_recv()
        for cp in first + passed:
            cp.wait_send()
        mine.wait()

    return pl.pallas_call(
        body, name=name, out_shape=_sds((N_DEV, m, n), blk.dtype),
        in_specs=[pl.BlockSpec(memory_space=pltpu.VMEM)], out_specs=pl.BlockSpec(memory_space=pltpu.VMEM),
        scratch_shapes=[pltpu.SemaphoreType.DMA((7,)), pltpu.SemaphoreType.DMA((7,)), pltpu.SemaphoreType.DMA],
        compiler_params=pltpu.CompilerParams(vmem_limit_bytes=VMEM_LIMIT),
    )(blk)


def _allgather_hbm(name, shards):
    na = len(shards)

    def body(*refs):
        ins, outs = refs[:na], refs[na:2 * na]
        send_sems, recv_sems, local_sems = refs[2 * na:]
        x, y, c = _place()
        me, sibling = (x, y, c), (x, y, 1 - c)
        chips = [(1 - x, y), (x, 1 - y), (1 - x, 1 - y)]

        def copy(a, k, block, to, from_input=False):
            dst = outs[a].at[_lin(*block)]
            return pltpu.make_async_remote_copy(src_ref=ins[a] if from_input else dst, dst_ref=dst,
                                                send_sem=send_sems.at[a, k], recv_sem=recv_sems.at[a, k],
                                                device_id=to, device_id_type=MESH)

        mine = [pltpu.make_async_copy(ins[a], outs[a].at[_lin(*me)], local_sems.at[a]) for a in range(na)]
        for cp in mine:
            cp.start()
        first = []
        for a in range(na):
            first.append(copy(a, 0, me, sibling, True))
            first += [copy(a, 1 + j, me, (*chip, c), True) for j, chip in enumerate(chips)]
        for cp in first:
            cp.start()
        passed = []
        for j, chip in enumerate(chips):
            for a in range(na):
                copy(a, 1 + j, (*chip, c), me).wait_recv()
                fwd = copy(a, 4 + j, (*chip, c), sibling)
                fwd.start()
                passed.append(fwd)
        for a in range(na):
            copy(a, 0, sibling, me).wait_recv()
            for j, chip in enumerate(chips):
                copy(a, 4 + j, (*chip, 1 - c), me).wait_recv()
        for cp in first + passed:
            cp.wait_send()
        for cp in mine:
            cp.wait()

    return pl.pallas_call(
        body, name=name, out_shape=[_sds((N_DEV, *s.shape), s.dtype) for s in shards],
        in_specs=[ANY] * na, out_specs=[ANY] * na,
        scratch_shapes=[pltpu.SemaphoreType.DMA((na, 7)), pltpu.SemaphoreType.DMA((na, 7)),
                        pltpu.SemaphoreType.DMA((na,))],
    )(*shards)


def _exchange_shards(name, grads, L):
    nw = len(grads)
    na = nw * L
    flat = [g for per_layer in grads for g in per_layer]

    def body(*refs):
        ins, outs = refs[:na], refs[na:na + nw]
        send_sems, recv_sems, local_sems = refs[na + nw:]
        x, y, c = _place()
        me = _lin(x, y, c)
        peers = [(x ^ ((k + 1) >> 2 & 1), y ^ ((k + 1) >> 1 & 1), c ^ ((k + 1) & 1)) for k in range(7)]

        def copy(a, k, src_blk, dst_blk):
            return pltpu.make_async_remote_copy(src_ref=ins[a].at[src_blk], dst_ref=outs[a // L].at[a % L, dst_blk],
                                                send_sem=send_sems.at[a, k], recv_sem=recv_sems.at[a, k],
                                                device_id=peers[k], device_id_type=MESH)

        mine = [pltpu.make_async_copy(ins[a].at[me], outs[a // L].at[a % L, me], local_sems.at[a]) for a in range(na)]
        for cp in mine:
            cp.start()
        sent = [copy(a, k, _lin(*peers[k]), me) for a in range(na) for k in range(7)]
        for cp in sent:
            cp.start()
        for a in range(na):
            for k in range(7):
                copy(a, k, me, _lin(*peers[k])).wait_recv()
        for cp in sent:
            cp.wait_send()
        for cp in mine:
            cp.wait()

    return pl.pallas_call(
        body, name=name, out_shape=[_sds((L, *per_layer[0].shape), per_layer[0].dtype) for per_layer in grads],
        in_specs=[ANY] * na, out_specs=[ANY] * nw,
        scratch_shapes=[pltpu.SemaphoreType.DMA((na, 7)), pltpu.SemaphoreType.DMA((na, 7)),
                        pltpu.SemaphoreType.DMA((na,))],
    )(*flat)


MOD_ROWS = 16
MOD_SHARD = 6 * D // N_DEV
HI = lax.Precision.HIGHEST


def _mod_fwd(name, c9, w_mod, b_shard):
    L = w_mod.shape[0]

    def kern(c_ref, w_ref, b_ref, o_ref):
        o_ref[...] = lax.dot_general(_silu(c_ref[...]), w_ref[...], NN, precision=HI,
                                     preferred_element_type=F32) + b_ref[...]

    return pl.pallas_call(
        kern, name=name, grid=(L,),
        in_specs=[_full_spec(c9.shape), pl.BlockSpec((None, D, MOD_SHARD), lambda l: (l, 0, 0)),
                  pl.BlockSpec((None, 1, MOD_SHARD), lambda l: (l, 0, 0))],
        out_specs=pl.BlockSpec((None, MOD_ROWS, MOD_SHARD), lambda l: (l, 0, 0)),
        out_shape=_sds((L, MOD_ROWS, MOD_SHARD), F32), compiler_params=_params(),
    )(c9, w_mod, b_shard)


def _mod_bwd(name, c9, w_mod, dmod_all, dmod_cols):
    L = w_mod.shape[0]

    def rows9(ref, l):
        own = jnp.concatenate([ref[j, 2 * l + 1:2 * l + 2, :] for j in range(N_DEV)], axis=0)
        ctx = ref[0, 2 * l:2 * l + 1, :]
        for j in range(1, N_DEV):
            ctx = ctx + ref[j, 2 * l:2 * l + 1, :]
        return own, ctx

    def kern(c_ref, w_ref, all_ref, cols_ref, gw_ref, gb_ref, gc_ref):
        l = pl.program_id(0)
        for ll in range(L):
            @pl.when(l == ll)
            def _():
                own, ctx = rows9(all_ref, ll)
                gb_ref[...] = _colsum(own) + ctx
                own_s, ctx_s = rows9(cols_ref, ll)
                r16 = jnp.concatenate([own_s, ctx_s, jnp.zeros((MOD_ROWS - N_DEV - 1, MOD_SHARD), F32)], axis=0)
                gw_ref[...] = lax.dot_general(_silu(c_ref[...]), r16, TN, precision=HI, preferred_element_type=F32)
                part = lax.dot_general(r16, w_ref[...], NT, precision=HI,
                                       preferred_element_type=F32)[N_DEV:N_DEV + 1, :]
                if ll == 0:
                    gc_ref[...] = part
                else:
                    gc_ref[...] += part

    return pl.pallas_call(
        kern, name=name, grid=(L,),
        in_specs=[_full_spec(c9.shape), pl.BlockSpec((None, D, MOD_SHARD), lambda l: (l, 0, 0)),
                  _full_spec(dmod_all.shape), _full_spec(dmod_cols.shape)],
        out_specs=[pl.BlockSpec((None, D, MOD_SHARD), lambda l: (l, 0, 0)),
                   pl.BlockSpec((None, 1, 6 * D), lambda l: (l, 0, 0)), _full_spec((1, D))],
        out_shape=[_sds((L, D, MOD_SHARD), F32), _sds((L, 1, 6 * D), F32), _sds((1, D), F32)],
        compiler_params=_params(),
    )(c9, w_mod, dmod_all, dmod_cols)


_BC1 = 1.0 - ADAM_B1 ** ADAM_STEP
_BC2 = 1.0 - ADAM_B2 ** ADAM_STEP


def _adamw_vals(w, g, m, v):
    m = ADAM_B1 * m + (1.0 - ADAM_B1) * g
    v = ADAM_B2 * v + (1.0 - ADAM_B2) * (g * g)
    delta = -ADAM_LR * ((m / _BC1) / (jnp.sqrt(v / _BC2) + ADAM_EPS) + ADAM_WD * w)
    return delta, m, v


def _adamw(name, w, g, m, v, tile):
    R, C = w.shape
    blk = ((tile, C), lambda i: (i, 0))

    def body(i, ins, ps, outs, acc):
        d, mm, vv = _adamw_vals(ins[0][...], ins[1][...], ins[2][...], ins[3][...])
        outs[0][...] = d
        outs[1][...] = mm
        outs[2][...] = vv

    return _ew(name, body, R // tile, [(a, *blk) for a in (w, g, m, v)], [], [(_sds((R, C), F32), *blk)] * 3)


def _sum_slots(ref):
    g = ref[0].astype(F32)
    for j in range(1, N_DEV):
        g = g + ref[j].astype(F32)
    return g


def _adamw_slots(name, slots, shape, tile, wmv=None):
    L, R, C = shape
    n = R // tile
    spec = pl.BlockSpec((None, tile, C), lambda l, i: (l, i, 0))
    pieces = [s if isinstance(s, (list, tuple)) else [s] for s in slots]
    layer_of = [ll for ll, ps in enumerate(pieces) for _ in ps]
    flat = [p for ps in pieces for p in ps]
    wmv = list(wmv or [])

    def slot_spec(ll, cols):
        return pl.BlockSpec((N_DEV, tile, cols),
                            lambda l, i: (0, jnp.where(l == ll, i, jnp.where(l < ll, 0, n - 1)), 0))

    def kern(*refs):
        s_refs = refs[:len(flat)]
        rest = refs[len(flat):]
        l = pl.program_id(0)
        for ll in range(L):
            @pl.when(l == ll)
            def _():
                parts = [_sum_slots(r) for r, lr in zip(s_refs, layer_of) if lr == ll]
                g = parts[0] if len(parts) == 1 else jnp.concatenate(parts, axis=1)
                if wmv:
                    w_ref, m_ref, v_ref, g_ref, d_ref, mo_ref, vo_ref = rest
                    d_ref[...], mo_ref[...], vo_ref[...] = _adamw_vals(w_ref[...], g, m_ref[...], v_ref[...])
                else:
                    g_ref, = rest
                g_ref[...] = g

    n_out = 4 if wmv else 1
    return pl.pallas_call(
        kern, name=name, grid=(L, n),
        in_specs=[slot_spec(ll, p.shape[-1]) for ll, p in zip(layer_of, flat)] + [spec] * len(wmv),
        out_specs=[spec] * n_out, out_shape=[_sds((L, R, C), F32)] * n_out,
        compiler_params=_params(("arbitrary", "arbitrary")),
    )(*flat, *wmv)


def _sum_blocks(name, blocks):
    _, R, C = blocks.shape

    def kern(b_ref, o_ref):
        o_ref[...] = _sum_slots(b_ref)

    return pl.pallas_call(kern, name=name, in_specs=[_full_spec(blocks.shape)], out_specs=_full_spec((R, C)),
                          grid=(1,), out_shape=_sds((R, C), F32), compiler_params=_params())(blocks)


BIG = ("win_t", "wo_rnn", "wo_attn", "wout", "wffn_in_t", "wffn_out")
BIG_SRC = ("w_in", "w_o_rnn", "w_o_attn", "w_out", "w_ffn_in", "w_ffn_out")
BIG_T = (True, False, False, False, True, False)
BIG_TILE = (176, 128, 128, 128, 176, 176)


def _chan_full(g8):
    return jnp.transpose(g8, (1, 0, 2)).reshape(g8.shape[1], D)


def kernel(x, c, ctx, c_ctx, w_mod, b_mod, g_mix_pre, g_mix_post, g_ffn_pre, g_ffn_post, w_in, conv_w, conv_b, lru_wa, lru_ba, lru_wx, lru_bx, lru_lam, attn_sink, w_o_rnn, w_o_attn, w_out, w_ffn_in, w_ffn_out, loss_target, m_c_ctx, m_w_mod, m_b_mod, m_g_mix_pre, m_g_mix_post, m_g_ffn_pre, m_g_ffn_post, m_w_in, m_conv_w, m_conv_b, m_lru_wa, m_lru_ba, m_lru_wx, m_lru_bx, m_lru_lam, m_attn_sink, m_w_o_rnn, m_w_o_attn, m_w_out, m_w_ffn_in, m_w_ffn_out, v_c_ctx, v_w_mod, v_b_mod, v_g_mix_pre, v_g_mix_post, v_g_ffn_pre, v_g_ffn_post, v_w_in, v_conv_w, v_conv_b, v_lru_wa, v_lru_ba, v_lru_wx, v_lru_bx, v_lru_lam, v_attn_sink, v_w_o_rnn, v_w_o_attn, v_w_out, v_w_ffn_in, v_w_ffn_out):
    P = dict(c_ctx=c_ctx, w_mod=w_mod, b_mod=b_mod, g_mix_pre=g_mix_pre, g_mix_post=g_mix_post, g_ffn_pre=g_ffn_pre,
             g_ffn_post=g_ffn_post, w_in=w_in, conv_w=conv_w, conv_b=conv_b, lru_wa=lru_wa, lru_ba=lru_ba,
             lru_wx=lru_wx, lru_bx=lru_bx, lru_lam=lru_lam, attn_sink=attn_sink, w_o_rnn=w_o_rnn, w_o_attn=w_o_attn,
             w_out=w_out, w_ffn_in=w_ffn_in, w_ffn_out=w_ffn_out)
    Mo = dict(c_ctx=m_c_ctx, w_mod=m_w_mod, b_mod=m_b_mod, g_mix_pre=m_g_mix_pre, g_mix_post=m_g_mix_post,
              g_ffn_pre=m_g_ffn_pre, g_ffn_post=m_g_ffn_post, w_in=m_w_in, conv_w=m_conv_w, conv_b=m_conv_b,
              lru_wa=m_lru_wa, lru_ba=m_lru_ba, lru_wx=m_lru_wx, lru_bx=m_lru_bx, lru_lam=m_lru_lam,
              attn_sink=m_attn_sink, w_o_rnn=m_w_o_rnn, w_o_attn=m_w_o_attn, w_out=m_w_out, w_ffn_in=m_w_ffn_in,
              w_ffn_out=m_w_ffn_out)
    Vo = dict(c_ctx=v_c_ctx, w_mod=v_w_mod, b_mod=v_b_mod, g_mix_pre=v_g_mix_pre, g_mix_post=v_g_mix_post,
              g_ffn_pre=v_g_ffn_pre, g_ffn_post=v_g_ffn_post, w_in=v_w_in, conv_w=v_conv_w, conv_b=v_conv_b,
              lru_wa=v_lru_wa, lru_ba=v_lru_ba, lru_wx=v_lru_wx, lru_bx=v_lru_bx, lru_lam=v_lru_lam,
              attn_sink=v_attn_sink, w_o_rnn=v_w_o_rnn, w_o_attn=v_w_o_attn, w_out=v_w_out, w_ffn_in=v_w_ffn_in,
              w_ffn_out=v_w_ffn_out)
    L = w_in.shape[0]
    S = x.shape[1]
    me = _lin(*_place())

    small = jnp.concatenate([c.reshape(8, 128), conv_w.reshape(L * CONV_W, 128), lru_ba.reshape(2 * L, 128),
                             lru_bx.reshape(2 * L, 128), lru_lam.reshape(2 * L, 128), jnp.zeros((4, 128), F32)], axis=0)
    small_all = _allgather_small("ag_small", small)
    c_all = small_all[:, 0:8].reshape(N_DEV, D)
    conv_w_f = _chan_full(small_all[:, 8:16]).reshape(L, CONV_W, D)
    lru_ba_f = _chan_full(small_all[:, 16:20]).reshape(L, 2, D)
    lru_bx_f = _chan_full(small_all[:, 20:24]).reshape(L, 2, D)
    lru_lam_f = _chan_full(small_all[:, 24:28]).reshape(L, 2, D)

    c9 = jnp.concatenate([c_all, c_ctx[None], jnp.zeros((MOD_ROWS - N_DEV - 1, D), F32)], axis=0)
    b_shard = lax.dynamic_slice_in_dim(b_mod, me * MOD_SHARD, MOD_SHARD, axis=1)[:, None, :]
    mod_part = _mod_fwd("mod_fwd", c9, w_mod, b_shard)
    mod_all = _allgather_small("ag_mod", mod_part.reshape(L * MOD_ROWS, MOD_SHARD))
    mod_all = jnp.transpose(mod_all.reshape(N_DEV, L, MOD_ROWS, MOD_SHARD), (1, 2, 0, 3)).reshape(L, MOD_ROWS, 6 * D)
    own_row = lax.dynamic_index_in_dim(mod_all, me, axis=1, keepdims=False)
    modrows = jnp.stack([mod_all[:, N_DEV], own_row], axis=1)

    shards = [{k: (P[src][l].T if tr else P[src][l]).astype(BF16) for k, src, tr in zip(BIG, BIG_SRC, BIG_T)}
              for l in range(L)]
    win0, = _allgather_hbm("ag_w_in0", [shards[0]["win_t"]])
    Ws = []
    for l in range(L):
        W = {"win_t": win0.reshape(-1, D)} if l == 0 else {}
        W.update(
            cw=conv_w_f[l], cb=conv_b[l][None],
            w4=jnp.concatenate([lru_wa[l, 0], lru_wa[l, 1], lru_wx[l, 0], lru_wx[l, 1]], axis=-1).astype(BF16),
            b4=jnp.concatenate([lru_ba_f[l, 0].reshape(N_RNN_BLOCKS, 1, RB), lru_ba_f[l, 1].reshape(N_RNN_BLOCKS, 1, RB),
                                lru_bx_f[l, 0].reshape(N_RNN_BLOCKS, 1, RB), lru_bx_f[l, 1].reshape(N_RNN_BLOCKS, 1, RB)],
                               axis=-1),
            lam=lru_lam_f[l], sink4=jnp.broadcast_to(attn_sink[l].reshape(N_KV, Q_PER_KV, 1), (N_KV, Q_PER_KV, HEAD)),
            g_mix_pre=g_mix_pre[l][None], g_mix_post=g_mix_post[l][None], g_ffn_pre=g_ffn_pre[l][None],
            g_ffn_post=g_ffn_post[l][None], mod=modrows[l])
        Ws.append(W)

    xa = jnp.concatenate([ctx[0], x[0]], axis=0)
    plan = _Plan(shards, Ws)
    sq, dxa, Gs = _local_step(xa, loss_target[0], Ws, S, plan)
    loss = lax.psum((0.5 / D) * jnp.sum(sq), ("x", "y", "c"))
    grad_x = dxa[CTX:][None]

    dmod = jnp.concatenate([Gs[l]["mod"] for l in range(L)] + [jnp.zeros((8 - 2 * L, 6 * D), F32)], axis=0)
    dmod_all = _allgather_small("ag_dmod", dmod)
    dmod_cols = lax.dynamic_slice_in_dim(dmod_all, me * MOD_SHARD, MOD_SHARD, axis=2)
    g_w_mod, g_b_mod, dsc_part = _mod_bwd("mod_bwd", c9, w_mod, dmod_all, dmod_cols)
    g_b_mod = g_b_mod[:, 0]

    def rows(name, shape):
        return jnp.concatenate([Gs[l][name].reshape(shape) for l in range(L)], axis=0)

    b4g = [Gs[l]["b4"].reshape(N_RNN_BLOCKS, 4, RB) for l in range(L)]
    sink_row = jnp.concatenate([Gs[l]["sink4"][:, :, 0].reshape(1, N_Q) for l in range(L)]
                               + [jnp.zeros((1, D - L * N_Q), F32)], axis=1)
    small_g = jnp.concatenate(
        [rows("g_mix_pre", (1, D)), rows("g_mix_post", (1, D)), rows("g_ffn_pre", (1, D)), rows("g_ffn_post", (1, D)),
         rows("cb", (1, D)), rows("cw", (CONV_W, D))]
        + [b4g[l][:, d].reshape(1, D) for l in range(L) for d in range(2)]
        + [b4g[l][:, 2 + d].reshape(1, D) for l in range(L) for d in range(2)]
        + [rows("lam", (2, D)), sink_row, dsc_part], axis=0)
    n_small = small_g.shape[0]
    small_tot = _sum_blocks("sum_small", _allgather_small("ag_small_grads", small_g))
    o = 0
    G = {}
    for name in ("g_mix_pre", "g_mix_post", "g_ffn_pre", "g_ffn_post", "conv_b"):
        G[name] = small_tot[o:o + L]
        o += L
    G["conv_w"] = small_tot[o:o + L * CONV_W].reshape(L, CONV_W, D)
    o += L * CONV_W
    for name in ("lru_ba", "lru_bx", "lru_lam"):
        G[name] = small_tot[o:o + 2 * L].reshape(L, 2, D)
        o += 2 * L
    G["attn_sink"] = small_tot[o, :L * N_Q].reshape(L, N_Q)
    sg = jax.nn.sigmoid(c_ctx)
    G["c_ctx"] = small_tot[o + 1] * (sg * (1.0 + c_ctx * (1.0 - sg)))
    G["b_mod"] = g_b_mod
    G["w_mod"] = g_w_mod

    last_slots, = _exchange_shards("exchange_w_in0", [[Gs[0]["win_t_b"].reshape(N_DEV, -1, D // 2)]], 1)
    plan.slots[0]["win_t_b"] = last_slots[0]
    for l in range(L):
        plan.slots[l]["win_t"] = [plan.slots[l]["win_t_a"], plan.slots[l]["win_t_b"]]

    out_g, out_d, out_m, out_v = {}, {}, {}, {}

    def put(name, res, shape=None):
        g, d, m, v = res
        for dst, val in ((out_g, g), (out_d, d), (out_m, m), (out_v, v)):
            dst[name] = val if shape is None else val.reshape(shape)

    for k, src, tr, tile in zip(BIG, BIG_SRC, BIG_T, BIG_TILE):
        slots = [plan.slots[l][k] for l in range(L)]
        if tr:
            rows, cols = P[src].shape[2], P[src].shape[1]
            g_t, = _adamw_slots("sum_" + src, slots, (L, rows, cols), tile)
            g = jnp.swapaxes(g_t, 1, 2)
            flat = lambda a: a.reshape(L * cols, rows)
            res = _adamw("adamw_" + src, flat(P[src]), flat(g), flat(Mo[src]), flat(Vo[src]), 256)
            put(src, (g,) + tuple(res), P[src].shape)
        else:
            put(src, _adamw_slots("adamw_" + src, slots, P[src].shape, tile, (P[src], Mo[src], Vo[src])))
    res = _adamw("adamw_w_mod", w_mod.reshape(L * D, MOD_SHARD), g_w_mod.reshape(L * D, MOD_SHARD),
                 m_w_mod.reshape(L * D, MOD_SHARD), v_w_mod.reshape(L * D, MOD_SHARD), 256)
    put("w_mod", (g_w_mod,) + tuple(res), w_mod.shape)
    def fuse4(wa, wx):
        return jnp.concatenate([wa[:, 0], wa[:, 1], wx[:, 0], wx[:, 1]], axis=-1).reshape(L, N_RNN_BLOCKS * RB, 4 * RB)

    res = _adamw_slots("adamw_gates", plan.gate_slots, (L, N_RNN_BLOCKS * RB, 4 * RB), 256,
                       (fuse4(lru_wa, lru_wx), fuse4(m_lru_wa, m_lru_wx), fuse4(v_lru_wa, v_lru_wx)))
    res = [r.reshape(L, N_RNN_BLOCKS, RB, 4, RB) for r in res]
    put("lru_wa", [jnp.stack([r[:, :, :, 0], r[:, :, :, 1]], axis=1) for r in res])
    put("lru_wx", [jnp.stack([r[:, :, :, 2], r[:, :, :, 3]], axis=1) for r in res])
    rep = ("g_mix_pre", "g_mix_post", "g_ffn_pre", "g_ffn_post", "conv_b", "b_mod")

    def pack_rep(T_):
        sink = jnp.concatenate([T_["attn_sink"].reshape(1, L * N_Q), jnp.zeros((1, D - L * N_Q), F32)], axis=1)
        return jnp.concatenate([T_[n].reshape(-1, D) for n in rep] + [sink, T_["c_ctx"][None]], axis=0)

    pk = [pack_rep(T_) for T_ in (P, G, Mo, Vo)]
    n_rep = pk[0].shape[0]
    res = _adamw("adamw_replicated", *[jnp.pad(a, ((0, 24 - n_rep), (0, 0))) for a in pk], 24)
    res = (pk[1],) + tuple(r[:n_rep] for r in res)
    o = 0
    for n in rep:
        k = P[n].size // D
        put(n, [r[o:o + k] for r in res], P[n].shape)
        o += k
    put("attn_sink", [r[o, :L * N_Q] for r in res], attn_sink.shape)
    put("c_ctx", [r[o + 1] for r in res], c_ctx.shape)
    chan = ("conv_w", "lru_ba", "lru_bx", "lru_lam")
    g_own = {n: lax.dynamic_slice_in_dim(G[n], me * RB, RB, axis=2) for n in chan}

    def pack_chan(T_):
        return jnp.concatenate([T_[n].reshape(-1, RB) for n in chan], axis=0)

    pk = [pack_chan(T_) for T_ in (P, g_own, Mo, Vo)]
    n_ch = pk[0].shape[0]
    res = _adamw("adamw_channels", *[jnp.pad(a, ((0, 24 - n_ch), (0, 0))) for a in pk], 24)
    res = (pk[1],) + tuple(r[:n_ch] for r in res)
    o = 0
    for n in chan:
        k = P[n].size // RB
        put(n, [r[o:o + k] for r in res], P[n].shape)
        o += k

    order = ("c_ctx", "w_mod", "b_mod", "g_mix_pre", "g_mix_post", "g_ffn_pre", "g_ffn_post", "w_in", "conv_w", "conv_b",
             "lru_wa", "lru_ba", "lru_wx", "lru_bx", "lru_lam", "attn_sink", "w_o_rnn", "w_o_attn", "w_out", "w_ffn_in",
             "w_ffn_out")
    return (loss, grad_x, *[out_g[n] for n in order], *[out_d[n] for n in order], *[out_m[n] for n in order],
            *[out_v[n] for n in order])
```

```python
import functools
import math

import numpy as np
import jax
import jax.numpy as jnp
from jax import lax
from jax.experimental import pallas as pl
from jax.experimental.pallas import tpu as pltpu

F32 = jnp.float32
BF16 = jnp.bfloat16

D = 1024
CTX = 256
TR = 256
HEAD = 128
N_Q = 8
N_KV = 2
Q_PER_KV = N_Q // N_KV
GRID_W = 64
N_FREQ = HEAD // 4
ROPE_BASE = 10000.0
N_RNN_BLOCKS = 8
CONV_W = 4
CONV_LEFT = 2
LRU_C = 8.0
D_FF = 2816
IN_W = 5632
P_W = IN_W
DP_W = 3584
COL_XR, COL_GR, COL_Q, COL_K, COL_V, COL_GL = 0, 1024, 2048, 3072, 3328, 3584
GLB = 512
EPS = 1e-6
NEG_INF = -1e30
ATT_SCALE = HEAD ** -0.5
N_DEV = 8
VMEM_LIMIT = 56 * 1024 * 1024

ADAM_LR, ADAM_B1, ADAM_B2, ADAM_EPS, ADAM_WD, ADAM_STEP = 0.001, 0.9, 0.999, 1e-08, 0.01, 10

NN = (((1,), (0,)), ((), ()))
NT = (((1,), (1,)), ((), ()))
TN = (((0,), (0,)), ((), ()))


def _dot(a, b, dims=NN):
    return lax.dot_general(a, b, dims, preferred_element_type=F32)


def _params(sem=("arbitrary",)):
    return pltpu.CompilerParams(dimension_semantics=sem, vmem_limit_bytes=VMEM_LIMIT)


def _full_spec(shape):
    nd = len(shape)
    return pl.BlockSpec(shape, lambda *_: (0,) * nd)


ANY = pl.BlockSpec(memory_space=pl.ANY)


def _ew(name, body, n, row_ins, pars, row_outs, accs=(), alias=None):
    n_ri, n_p, n_ro, n_acc = len(row_ins), len(pars), len(row_outs), len(accs)

    def kern(*refs):
        i = pl.program_id(0)
        ins = refs[:n_ri]
        ps = refs[n_ri:n_ri + n_p]
        outs = refs[n_ri + n_p:n_ri + n_p + n_ro]
        acc = refs[n_ri + n_p + n_ro:]
        if n_acc:
            @pl.when(i == 0)
            def _():
                for a in acc:
                    a[...] = jnp.zeros(a.shape, a.dtype)
        body(i, ins, ps, outs, acc)

    in_specs = [ANY if blk is None else pl.BlockSpec(blk, imap) for (_, blk, imap) in row_ins]
    in_specs += [_full_spec(p.shape) for p in pars]
    out_specs = [pl.BlockSpec(blk, imap) for (_, blk, imap) in row_outs] + [_full_spec(a.shape) for a in accs]
    out_shape = [s for (s, _, _) in row_outs] + list(accs)
    return pl.pallas_call(
        kern, name=name, grid=(n,), in_specs=in_specs, out_specs=out_specs, out_shape=out_shape,
        input_output_aliases=alias or {}, compiler_params=_params(),
    )(*[a for (a, _, _) in row_ins], *pars)


def _rowblk(width, colblk=0, roff=0, tile=TR):
    return (tile, width), (lambda i: (i + roff, colblk))


def _sds(shape, dtype):
    return jax.ShapeDtypeStruct(shape, dtype)


class _Carry:
    SAME_CORE = (1, 3, 5)

    def __init__(self, jobs):
        self.jobs = list(jobs)
        self.arrays = [a for _, a in self.jobs]
        self.out_shapes = [_sds(a.shape if kind == "scatter" else (N_DEV, *a.shape), a.dtype) for kind, a in self.jobs]
        n = len(self.jobs)
        self.scratch = [pltpu.SemaphoreType.DMA((n, 7)), pltpu.SemaphoreType.DMA((n, 7)), pltpu.SemaphoreType.DMA((n,))]

    def _setup(self, sems):
        send_sems, recv_sems, local_sems = sems
        x, y, c = _place()
        me = _lin(x, y, c)
        peers = [(x ^ ((k + 1) >> 2 & 1), y ^ ((k + 1) >> 1 & 1), c ^ ((k + 1) & 1)) for k in range(7)]

        def copy(a, k, sem_k, src, dst):
            return pltpu.make_async_remote_copy(src_ref=src, dst_ref=dst, send_sem=send_sems.at[a, sem_k],
                                                recv_sem=recv_sems.at[a, sem_k], device_id=peers[k], device_id_type=MESH)

        return me, [_lin(*p) for p in peers], copy, local_sems

    def _local(self, a, kind, ins, outs, me, local_sems):
        return pltpu.make_async_copy(ins[a].at[me] if kind == "scatter" else ins[a], outs[a].at[me], local_sems.at[a])

    def start(self, ins, outs, sems):
        me, theirs, copy, local_sems = self._setup(sems)
        for a, (kind, _) in enumerate(self.jobs):
            self._local(a, kind, ins, outs, me, local_sems).start()
            if kind == "scatter":
                for k in range(7):
                    copy(a, k, k, ins[a].at[theirs[k]], outs[a].at[me]).start()
            else:
                for k in (0,) + self.SAME_CORE:
                    copy(a, k, k, ins[a], outs[a].at[me]).start()

    def wait(self, ins, outs, sems):
        me, theirs, copy, local_sems = self._setup(sems)
        for a, (kind, _) in enumerate(self.jobs):
            if kind == "scatter":
                for k in range(7):
                    copy(a, k, k, ins[a].at[me], outs[a].at[theirs[k]]).wait_recv()
                for k in range(7):
                    copy(a, k, k, ins[a].at[theirs[k]], outs[a].at[me]).wait_send()
            else:
                for k in self.SAME_CORE:
                    blk = outs[a].at[theirs[k]]
                    copy(a, k, k, ins[a], blk).wait_recv()
                    copy(a, 0, k + 1, blk, blk).start()
                copy(a, 0, 0, ins[a], outs[a].at[theirs[0]]).wait_recv()
                for k in self.SAME_CORE:
                    copy(a, 0, k + 1, ins[a], outs[a].at[theirs[k + 1]]).wait_recv()
                for k in (0,) + self.SAME_CORE:
                    copy(a, k, k, ins[a], outs[a].at[me]).wait_send()
                for k in self.SAME_CORE:
                    blk = outs[a].at[theirs[k]]
                    copy(a, 0, k + 1, blk, blk).wait_send()
            self._local(a, kind, ins, outs, me, local_sems).wait()


def _carried(kern, carry, n_in, n_out, first, last):
    if carry is None:
        return kern
    nc = len(carry.jobs)

    def wrapped(*refs):
        ins, cin = refs[:n_in], refs[n_in:n_in + nc]
        outs, cout = refs[n_in + nc:n_in + nc + n_out], refs[n_in + nc + n_out:n_in + 2 * nc + n_out]
        scr, sems = refs[n_in + 2 * nc + n_out:-3], refs[-3:]

        @pl.when(first())
        def _():
            carry.start(cin, cout, sems)

        kern(*ins, *outs, *scr)

        @pl.when(last())
        def _():
            carry.wait(cin, cout, sems)

    return wrapped


def _carry_args(carry):
    if carry is None:
        return [], [], [], [], []
    n = len(carry.jobs)
    return [ANY] * n, carry.arrays, [ANY] * n, carry.out_shapes, carry.scratch


def _grid_ends(dims):
    first = lambda: functools.reduce(jnp.logical_and, [pl.program_id(d) == 0 for d in range(len(dims))])
    last = lambda: functools.reduce(jnp.logical_and, [pl.program_id(d) == n - 1 for d, n in enumerate(dims)])
    return first, last


def _mm_call(name, a, b, mode, out_dtype, tm, tn, rows_outer=True, single_b=False, carry=None):
    if mode == "TN":
        (K, M), N = a.shape, b.shape[1]
    else:
        (M, K), N = a.shape, (b.shape[1] if mode == "NN" else b.shape[0])
    assert M % tm == 0 and N % tn == 0, (name, M, N, K, tm, tn)
    ij = (lambda g0, g1: (g0, g1)) if rows_outer else (lambda g0, g1: (g1, g0))
    grid = (M // tm, N // tn) if rows_outer else (N // tn, M // tm)
    if mode == "TN":
        a_spec = pl.BlockSpec((K, tm), lambda g0, g1: (0, ij(g0, g1)[0]))
    else:
        a_spec = pl.BlockSpec((tm, K), lambda g0, g1: (ij(g0, g1)[0], 0))
    b_blk, b_map = ((tn, K), lambda g0, g1: (ij(g0, g1)[1], 0)) if mode == "NT" else \
                   ((K, tn), lambda g0, g1: (0, ij(g0, g1)[1]))
    b_spec = pl.BlockSpec(b_blk, b_map, pipeline_mode=pl.Buffered(1)) if single_b else pl.BlockSpec(b_blk, b_map)
    dims = {"NN": NN, "NT": NT, "TN": TN}[mode]

    def kern(a_ref, b_ref, o_ref):
        o_ref[...] = _dot(a_ref[...], b_ref[...], dims).astype(o_ref.dtype)

    ci, ca, co, cs, cscr = _carry_args(carry)
    res = pl.pallas_call(
        _carried(kern, carry, 2, 1, *_grid_ends(grid)), name=name, grid=grid, in_specs=[a_spec, b_spec] + ci,
        out_specs=[pl.BlockSpec((tm, tn), lambda g0, g1: ij(g0, g1))] + co,
        out_shape=[_sds((M, N), out_dtype)] + cs, scratch_shapes=cscr,
        compiler_params=_params(("arbitrary", "arbitrary")),
    )(a, b, *ca)
    return res[0] if carry is None else (res[0], res[1:])


def _mm_act(name, a, w, mode, out_dtype=BF16, carry=None):
    rows, K = a.shape
    N = w.shape[1] if mode == "NN" else w.shape[0]
    if K > D_FF:
        return _mm_call(name, a, w, mode, out_dtype, rows // 8, N, single_b=True, carry=carry)
    tn = N if N <= 1024 else 1408
    return _mm_call(name, a, w, mode, out_dtype, rows // 4, tn, carry=carry)


def _mm_wgrad(name, x, dy, out_dtype=BF16, carry=None):
    M = x.shape[1]
    tm = 1408 if M == D_FF else 512
    return _mm_call(name, x, dy, "TN", out_dtype, tm, dy.shape[1], single_b=True, carry=carry)


def _sigmoid(x):
    return 0.5 * jnp.tanh(0.5 * x) + 0.5


def _silu(x):
    return x * _sigmoid(x)


def _silu_grad(x):
    s = _sigmoid(x)
    return s * (1.0 + x * (1.0 - s))


_GELU_K = math.sqrt(2.0 / math.pi)


def _gelu(x):
    return 0.5 * x * (1.0 + jnp.tanh(_GELU_K * (x + 0.044715 * x * x * x)))


def _gelu_grad(x):
    t = jnp.tanh(_GELU_K * (x + 0.044715 * x * x * x))
    return 0.5 * (1.0 + t) + 0.5 * x * (1.0 - t * t) * _GELU_K * (1.0 + 3.0 * 0.044715 * x * x)


def _log_sigmoid(x):
    return jnp.minimum(x, 0.0) - jnp.log(1.0 + jnp.exp(-jnp.abs(x)))


def _rms(x):
    x = x.astype(F32)
    r = lax.rsqrt(jnp.mean(x * x, axis=-1, keepdims=True) + EPS)
    return x * r, r


def _rms_bwd(dy, y, r):
    return r * (dy - y * jnp.mean(dy * y, axis=-1, keepdims=True))


def _modrow(mod_ref, i, chunk):
    lo = mod_ref[0:1, chunk * D:(chunk + 1) * D]
    hi = mod_ref[1:2, chunk * D:(chunk + 1) * D]
    return jnp.where(i == 0, lo, hi)


def _acc_seg(acc_ref, i, val):
    zero = jnp.zeros_like(val)
    acc_ref[0:1, :] += jnp.where(i == 0, val, zero)
    acc_ref[1:2, :] += jnp.where(i == 0, zero, val)


def _colsum(x):
    return jnp.sum(x, axis=0, keepdims=True)


SH1, SC1, GA1, SH2, SC2, GA2 = range(6)


def _normmod_fwd(name, xa, g, mod, c_sh, c_sc):
    T = xa.shape[0]

    def body(i, ins, ps, outs, acc):
        y, _ = _rms(ins[0][...])
        h = (y * ps[0][...]) * (1.0 + _modrow(ps[1], i, c_sc)) + _modrow(ps[1], i, c_sh)
        outs[0][...] = h.astype(BF16)

    return _ew(name, body, T // TR, [(xa, *_rowblk(D))], [g, mod], [(_sds((T, D), BF16), *_rowblk(D))])[0]


def _modrows(mod_ref, row0, n, chunk):
    t = row0 + lax.broadcasted_iota(jnp.int32, (n, 1), 0)
    return jnp.where(t < CTX, mod_ref[0:1, chunk * D:(chunk + 1) * D], mod_ref[1:2, chunk * D:(chunk + 1) * D])


def _loss_resid_bwd(name, x_out, target, mat, gpost, mod, c_ga):
    T = x_out.shape[0]

    def body(i, ins, ps, outs, acc):
        err = ins[0][...] - ins[1][...]
        lat = i > 0
        dx = jnp.where(lat, err * (1.0 / D), 0.0)
        outs[0][...] = dx
        acc[2][...] += jnp.where(lat, _colsum(err * err), 0.0)
        outs[1][...] = _resid_bwd_vals(i, dx, ins[2][...], ps[0][...], ps[1], c_ga, acc[0], acc[1]).astype(BF16)

    tgt_blk = ((TR, D), lambda i: (jnp.maximum(i - 1, 0), 0))
    return _ew(name, body, T // TR, [(x_out, *_rowblk(D)), (target, *tgt_blk), (mat, *_rowblk(D))], [gpost, mod],
               [(_sds((T, D), F32), *_rowblk(D)), (_sds((T, D), BF16), *_rowblk(D))],
               [_sds((2, D), F32), _sds((1, D), F32), _sds((1, D), F32)])


def _mod_for(mod_ref, i, chunk, row0, n):
    return _modrow(mod_ref, i, chunk) if row0 is None else _modrows(mod_ref, row0, n, chunk)


def _acc_for(acc_ref, i, v, row0):
    if row0 is None:
        _acc_seg(acc_ref, i, _colsum(v))
        return

    @pl.when(row0 < CTX)
    def _():
        is_ctx = row0 + lax.broadcasted_iota(jnp.int32, (v.shape[0], 1), 0) < CTX
        acc_ref[0:1, :] += _colsum(jnp.where(is_ctx, v, 0.0))
        acc_ref[1:2, :] += _colsum(jnp.where(is_ctx, 0.0, v))

    @pl.when(row0 >= CTX)
    def _():
        acc_ref[1:2, :] += _colsum(v)


def _resid_bwd_vals(i, dout, mat, gpost, mod_ref, c_ga, acc_ga, acc_g, row0=None):
    ym, rm = _rms(mat)
    ga = _mod_for(mod_ref, i, c_ga, row0, dout.shape[0])
    _acc_for(acc_ga, i, dout * (ym * gpost), row0)
    dn = dout * ga
    acc_g[...] += _colsum(dn * ym)
    return _rms_bwd(dn * gpost, ym, rm)


def _normmod_bwd_vals(i, dh, xin, g, mod_ref, c_sh, c_sc, acc_sh, acc_sc, acc_g, row0=None):
    dh = dh.astype(F32)
    y, r = _rms(xin)
    _acc_for(acc_sc, i, dh * (y * g), row0)
    _acc_for(acc_sh, i, dh, row0)
    dyg = dh * (1.0 + _mod_for(mod_ref, i, c_sc, row0, dh.shape[0]))
    acc_g[...] += _colsum(dyg * y)
    return _rms_bwd(dyg * g, y, r)


def _parts(i, tm):
    return [(slice(0, tm), i * tm)]


FT = 1408


def _ffn_in_fused(name, h2, w_t, carry=None):
    T = h2.shape[0]
    tm, nj = T // 4, D_FF // FT

    def kern(a_ref, bg_ref, bu_ref, fg_ref, fu_ref, s_ref):
        for rows, _ in _parts(0, tm):
            a = a_ref[rows, :]
            g = _dot(a, bg_ref[...], NT)
            u = _dot(a, bu_ref[...], NT)
            fg_ref[rows, :] = g.astype(BF16)
            fu_ref[rows, :] = u.astype(BF16)
            s_ref[rows, :] = (_silu(g) * u).astype(BF16)

    o_spec = pl.BlockSpec((tm, FT), lambda i, j: (i, j))
    ci, ca, co, cs, cscr = _carry_args(carry)
    res = pl.pallas_call(
        _carried(kern, carry, 3, 3, *_grid_ends((4, nj))), name=name, grid=(4, nj),
        in_specs=[pl.BlockSpec((tm, D), lambda i, j: (i, 0)), pl.BlockSpec((FT, D), lambda i, j: (j, 0)),
                  pl.BlockSpec((FT, D), lambda i, j: (j + nj, 0))] + ci,
        out_specs=[o_spec] * 3 + co, out_shape=[_sds((T, D_FF), BF16)] * 3 + cs, scratch_shapes=cscr,
        compiler_params=_params(("arbitrary", "arbitrary")),
    )(h2, w_t, w_t, *ca)
    return res if carry is None else (res[:3], res[3:])


def _norm_chain(row0, xin, mat, gpost, mod_ref, c_ga, gnext, modn_ref, c_sh, c_sc):
    n = xin.shape[0]
    ym, _ = _rms(mat.astype(BF16))
    xo = xin + _modrows(mod_ref, row0, n, c_ga) * (ym * gpost)
    y, _ = _rms(xo)
    h = (y * gnext) * (1.0 + _modrows(modn_ref, row0, n, c_sc)) + _modrows(modn_ref, row0, n, c_sh)
    return xo, h.astype(BF16)


def _out_fused(name, p, u, o_all, xa, w_o_rnn, w_o_attn, w_out, gpost, mod, gnext):
    T = u.shape[0]
    tm = T // 8

    def kern(g0, g1, g2, g3, u_ref, o_ref, xa_ref, wr_ref, wa_ref, w_ref, gpost_ref, mod_ref, gnext_ref,
             ya_ref, yb_ref, z_ref, m_ref, x1_ref, h2_ref):
        for rows, row0 in _parts(pl.program_id(0), tm):
            ya = _dot(u_ref[rows, :], wr_ref[...]).astype(BF16)
            yb = _dot(o_ref[rows, :], wa_ref[...]).astype(BF16)
            ya_ref[rows, :] = ya
            yb_ref[rows, :] = yb
            ga = _sigmoid(jnp.concatenate([g0[rows, :], g1[rows, :]], axis=1).astype(F32))
            gb = _sigmoid(jnp.concatenate([g2[rows, :], g3[rows, :]], axis=1).astype(F32))
            z = (ga * ya.astype(F32) + gb * yb.astype(F32)).astype(BF16)
            z_ref[rows, :] = z
            m = _dot(z, w_ref[...])
            m_ref[rows, :] = m.astype(BF16)
            x1_ref[rows, :], h2_ref[rows, :] = _norm_chain(row0, xa_ref[rows, :], m, gpost_ref[...], mod_ref, GA1,
                                                           gnext_ref[...], mod_ref, SH2, SC2)

    row = lambda w: pl.BlockSpec((tm, w), lambda i: (i, 0))
    return pl.pallas_call(
        kern, name=name, grid=(T // tm,),
        in_specs=[pl.BlockSpec((tm, GLB), lambda i, q=q: (i, COL_GL // GLB + q)) for q in range(4)]
                 + [row(D), row(D), row(D)] + [_full_spec(a.shape) for a in (w_o_rnn, w_o_attn, w_out, gpost, mod, gnext)],
        out_specs=[row(D)] * 6,
        out_shape=[_sds((T, D), BF16)] * 4 + [_sds((T, D), F32), _sds((T, D), BF16)],
        compiler_params=_params(),
    )(p, p, p, p, u, o_all, xa, w_o_rnn, w_o_attn, w_out, gpost, mod, gnext)


def _ffn_out_fused(name, s, w, x1, gpost, mod, nxt=None):
    T = s.shape[0]
    tm = T // 8

    def kern(s_ref, w_ref, x1_ref, gpost_ref, mod_ref, *rest):
        for rows, row0 in _parts(pl.program_id(0), tm):
            e = _dot(s_ref[rows, :], w_ref[...])
            if nxt is None:
                e_ref, xo_ref = rest
                ym, _ = _rms(e.astype(BF16))
                xo_ref[rows, :] = x1_ref[rows, :] + _modrows(mod_ref, row0, e.shape[0], GA2) * (ym * gpost_ref[...])
            else:
                gnext_ref, modn_ref, e_ref, xo_ref, h_ref = rest
                xo_ref[rows, :], h_ref[rows, :] = _norm_chain(row0, x1_ref[rows, :], e, gpost_ref[...], mod_ref, GA2,
                                                              gnext_ref[...], modn_ref, SH1, SC1)
            e_ref[rows, :] = e.astype(BF16)

    row = lambda w_: pl.BlockSpec((tm, w_), lambda i: (i, 0))
    extra = [] if nxt is None else list(nxt)
    return pl.pallas_call(
        kern, name=name, grid=(T // tm,),
        in_specs=[row(D_FF), _full_spec(w.shape), row(D), _full_spec(gpost.shape), _full_spec(mod.shape)]
                 + [_full_spec(a.shape) for a in extra],
        out_specs=[row(D)] * (2 if nxt is None else 3),
        out_shape=[_sds((T, D), BF16), _sds((T, D), F32)] + ([] if nxt is None else [_sds((T, D), BF16)]),
        compiler_params=_params(),
    )(s, w, x1, gpost, mod, *extra)


def _ffn_bwd_fused(name, fg, fu, w, de=None, head=None):
    T = fg.shape[0]
    tm = T // 8
    row = lambda w_: pl.BlockSpec((tm, w_), lambda i: (i, 0))
    w_spec = pl.BlockSpec(w.shape, lambda i: (0, 0), pipeline_mode=pl.Buffered(1))

    def tail(rows, de_v, fg_ref, fu_ref, w_ref, df_ref):
        ds = _dot(de_v, w_ref[...], NT)
        g, u = fg_ref[rows, :].astype(F32), fu_ref[rows, :].astype(F32)
        df_ref[rows, :] = jnp.concatenate([ds * u * _silu_grad(g), ds * _silu(g)], axis=1).astype(BF16)

    if head is None:
        def kern(de_ref, fg_ref, fu_ref, w_ref, df_ref):
            for rows, _ in _parts(pl.program_id(0), tm):
                tail(rows, de_ref[rows, :], fg_ref, fu_ref, w_ref, df_ref)

        return pl.pallas_call(
            kern, name=name, grid=(T // tm,), in_specs=[row(D), row(D_FF), row(D_FF), w_spec],
            out_specs=[row(2 * D_FF)], out_shape=[_sds((T, 2 * D_FF), BF16)], compiler_params=_params(),
        )(de, fg, fu, w)

    dx2, e, gpost, mod = head

    def kern(dx_ref, e_ref, fg_ref, fu_ref, w_ref, gpost_ref, mod_ref, de_ref, df_ref, dga_ref, dg_ref):
        i = pl.program_id(0)

        @pl.when(i == 0)
        def _():
            dga_ref[...] = jnp.zeros(dga_ref.shape, F32)
            dg_ref[...] = jnp.zeros(dg_ref.shape, F32)

        for rows, row0 in _parts(i, tm):
            de_v = _resid_bwd_vals(i, dx_ref[rows, :], e_ref[rows, :], gpost_ref[...], mod_ref, GA2, dga_ref, dg_ref,
                                   row0=row0).astype(BF16)
            de_ref[rows, :] = de_v
            tail(rows, de_v, fg_ref, fu_ref, w_ref, df_ref)

    return pl.pallas_call(
        kern, name=name, grid=(T // tm,),
        in_specs=[row(D), row(D), row(D_FF), row(D_FF), w_spec, _full_spec(gpost.shape), _full_spec(mod.shape)],
        out_specs=[row(D), row(2 * D_FF), _full_spec((2, D)), _full_spec((1, D))],
        out_shape=[_sds((T, D), BF16), _sds((T, 2 * D_FF), BF16), _sds((2, D), F32), _sds((1, D), F32)],
        compiler_params=_params(),
    )(dx2, e, fg, fu, w, gpost, mod)


def _zero_at_start(i, refs):
    @pl.when(i == 0)
    def _():
        for r in refs:
            r[...] = jnp.zeros(r.shape, F32)


def _proj_bwd_fused(name, dp, dgl, w_in_t, xa, dx1, gpre, mod, carry=None):
    T = dp.shape[0]
    tm = T // 8
    row = lambda w_: pl.BlockSpec((tm, w_), lambda i: (i, 0))

    def kern(dp_ref, dgl_ref, w_ref, xa_ref, dx1_ref, g_ref, mod_ref, dxa_ref, dsh_ref, dsc_ref, dg_ref):
        i = pl.program_id(0)
        _zero_at_start(i, (dsh_ref, dsc_ref, dg_ref))
        for rows, row0 in _parts(i, tm):
            dh = _dot(dp_ref[rows, :], w_ref[0:DP_W, :]) + _dot(dgl_ref[rows, :], w_ref[DP_W:, :])
            dxa_ref[rows, :] = dx1_ref[rows, :] + _normmod_bwd_vals(i, dh, xa_ref[rows, :], g_ref[...], mod_ref, SH1,
                                                                    SC1, dsh_ref, dsc_ref, dg_ref, row0=row0)

    ci, ca, co, cs, cscr = _carry_args(carry)
    res = pl.pallas_call(
        _carried(kern, carry, 7, 4, *_grid_ends((T // tm,))), name=name, grid=(T // tm,),
        in_specs=[row(DP_W), row(P_W - DP_W),
                  pl.BlockSpec(w_in_t.shape, lambda i: (0, 0), pipeline_mode=pl.Buffered(1)), row(D), row(D),
                  _full_spec(gpre.shape), _full_spec(mod.shape)] + ci,
        out_specs=[row(D), _full_spec((2, D)), _full_spec((2, D)), _full_spec((1, D))] + co,
        out_shape=[_sds((T, D), F32), _sds((2, D), F32), _sds((2, D), F32), _sds((1, D), F32)] + cs,
        scratch_shapes=cscr, compiler_params=_params(),
    )(dp, dgl, w_in_t, xa, dx1, gpre, mod, *ca)
    return res if carry is None else (res[:4], res[4:])


def _proj_wgrad(name, dp, dgl, h, carry=None):
    T, N = h.shape
    n1, n2 = DP_W // GLB, (P_W - DP_W) // GLB

    def kern(a1_ref, a2_ref, h_ref, o_ref):
        i = pl.program_id(0)

        @pl.when(i < n1)
        def _():
            o_ref[...] = _dot(a1_ref[...], h_ref[...], TN).astype(o_ref.dtype)

        @pl.when(i >= n1)
        def _():
            o_ref[...] = _dot(a2_ref[...], h_ref[...], TN).astype(o_ref.dtype)

    ci, ca, co, cs, cscr = _carry_args(carry)
    res = pl.pallas_call(
        _carried(kern, carry, 3, 1, *_grid_ends((n1 + n2,))), name=name, grid=(n1 + n2,),
        in_specs=[pl.BlockSpec((T, GLB), lambda i: (0, jnp.minimum(i, n1 - 1))),
                  pl.BlockSpec((T, GLB), lambda i: (0, jnp.maximum(i - n1, 0))),
                  pl.BlockSpec((T, N), lambda i: (0, 0), pipeline_mode=pl.Buffered(1))] + ci,
        out_specs=[pl.BlockSpec((GLB, N), lambda i: (i, 0))] + co,
        out_shape=[_sds((P_W, N), BF16)] + cs, scratch_shapes=cscr, compiler_params=_params(),
    )(dp, dgl, h, *ca)
    return res[0] if carry is None else (res[0], res[1:])


def _ffn_in_bwd_fused(name, df, w_t, x1, dres, mat, gpre, mod, gpost, carry=None):
    T = df.shape[0]
    tm = T // 8
    row = lambda w_: pl.BlockSpec((tm, w_), lambda i: (i, 0))

    def kern(df_ref, w_ref, x1_ref, dres_ref, mat_ref, gpre_ref, mod_ref, gpost_ref,
             dx1_ref, dm_ref, dsh_ref, dsc_ref, dgpre_ref, dga_ref, dgpost_ref):
        i = pl.program_id(0)
        _zero_at_start(i, (dsh_ref, dsc_ref, dgpre_ref, dga_ref, dgpost_ref))
        for rows, row0 in _parts(i, tm):
            dh2 = _dot(df_ref[rows, :], w_ref[...])
            dx1 = dres_ref[rows, :] + _normmod_bwd_vals(i, dh2, x1_ref[rows, :], gpre_ref[...], mod_ref, SH2, SC2,
                                                        dsh_ref, dsc_ref, dgpre_ref, row0=row0)
            dx1_ref[rows, :] = dx1
            dm_ref[rows, :] = _resid_bwd_vals(i, dx1, mat_ref[rows, :], gpost_ref[...], mod_ref, GA1, dga_ref,
                                              dgpost_ref, row0=row0).astype(BF16)

    ci, ca, co, cs, cscr = _carry_args(carry)
    res = pl.pallas_call(
        _carried(kern, carry, 8, 7, *_grid_ends((T // tm,))), name=name, grid=(T // tm,),
        in_specs=[row(2 * D_FF), pl.BlockSpec(w_t.shape, lambda i: (0, 0), pipeline_mode=pl.Buffered(1)), row(D),
                  row(D), row(D), _full_spec(gpre.shape), _full_spec(mod.shape), _full_spec(gpost.shape)] + ci,
        out_specs=[row(D), row(D), _full_spec((2, D)), _full_spec((2, D)), _full_spec((1, D)), _full_spec((2, D)),
                   _full_spec((1, D))] + co,
        out_shape=[_sds((T, D), F32), _sds((T, D), BF16), _sds((2, D), F32), _sds((2, D), F32), _sds((1, D), F32),
                   _sds((2, D), F32), _sds((1, D), F32)] + cs,
        scratch_shapes=cscr, compiler_params=_params(),
    )(df, w_t, x1, dres, mat, gpre, mod, gpost, *ca)
    return res if carry is None else (res[:7], res[7:])


def _out_bwd_fused(name, dm, w_out, w_o_rnn, w_o_attn, p, ya, yb):
    T = dm.shape[0]
    tm = T // 8
    row = lambda w_: pl.BlockSpec((tm, w_), lambda i: (i, 0))

    def kern(dm_ref, w_ref, wr_ref, wa_ref, g0, g1, g2, g3, ya_ref, yb_ref, dya_ref, dyb_ref, dgl_ref, du_ref, do_ref):
        for rows, _ in _parts(pl.program_id(0), tm):
            dz = _dot(dm_ref[rows, :], w_ref[...], NT)
            ga = _sigmoid(jnp.concatenate([g0[rows, :], g1[rows, :]], axis=1).astype(F32))
            gb = _sigmoid(jnp.concatenate([g2[rows, :], g3[rows, :]], axis=1).astype(F32))
            dya = (dz * ga).astype(BF16)
            dyb = (dz * gb).astype(BF16)
            dya_ref[rows, :] = dya
            dyb_ref[rows, :] = dyb
            dgl_ref[rows, :] = jnp.concatenate([dz * ya_ref[rows, :].astype(F32) * ga * (1.0 - ga),
                                                dz * yb_ref[rows, :].astype(F32) * gb * (1.0 - gb)],
                                               axis=1).astype(BF16)
            du_ref[rows, :] = _dot(dya, wr_ref[...], NT).astype(BF16)
            do_ref[rows, :] = _dot(dyb, wa_ref[...], NT).astype(BF16)

    return pl.pallas_call(
        kern, name=name, grid=(T // tm,),
        in_specs=[row(D)] + [_full_spec(w.shape) for w in (w_out, w_o_rnn, w_o_attn)]
                 + [pl.BlockSpec((tm, GLB), lambda i, q=q: (i, COL_GL // GLB + q)) for q in range(4)] + [row(D), row(D)],
        out_specs=[row(D), row(D), row(2 * D), row(D), row(D)],
        out_shape=[_sds((T, D), BF16), _sds((T, D), BF16), _sds((T, 2 * D), BF16), _sds((T, D), BF16),
                   _sds((T, D), BF16)],
        compiler_params=_params(),
    )(dm, w_out, w_o_rnn, w_o_attn, p, p, p, p, ya, yb)


AB = 128
CTX_BLKS = CTX // AB


def _rope_tables(S):
    pos = jnp.arange(S, dtype=jnp.int32)
    inv = ROPE_BASE ** (-jnp.arange(N_FREQ, dtype=F32) / N_FREQ)
    ang_r = (pos // GRID_W).astype(F32)[:, None] * inv[None, :]
    ang_c = (pos % GRID_W).astype(F32)[:, None] * inv[None, :]
    cos = jnp.concatenate([jnp.cos(ang_r)] * 2 + [jnp.cos(ang_c)] * 2, axis=1)
    sin = jnp.concatenate([-jnp.sin(ang_r), jnp.sin(ang_r), -jnp.sin(ang_c), jnp.sin(ang_c)], axis=1)
    return cos, sin


def _rope(x, cos, sin):
    w = x.shape[1]
    reps = w // HEAD
    lane = lax.broadcasted_iota(jnp.int32, x.shape, 1)
    partner = jnp.where((lane & 63) < 32, pltpu.roll(x, w - 32, 1), pltpu.roll(x, 32, 1))
    return x * jnp.tile(cos, (1, reps)) + partner * jnp.tile(sin, (1, reps))


def _unrope(dx, cos, sin):
    w = dx.shape[1]
    reps = w // HEAD
    lane = lax.broadcasted_iota(jnp.int32, dx.shape, 1)
    t = dx * jnp.tile(sin, (1, reps))
    partner = jnp.where((lane & 63) < 32, pltpu.roll(t, w - 32, 1), pltpu.roll(t, 32, 1))
    return dx * jnp.tile(cos, (1, reps)) + partner


def _qkv_prep(name, p, cos, sin, S):
    T = CTX + S
    nt = T // AB
    KW = N_KV * HEAD

    def with_ones(v):
        ones = jnp.ones((AB, HEAD), BF16)
        return jnp.concatenate([v[:, kh * HEAD:(kh + 1) * HEAD] if part == 0 else ones
                                for kh in range(N_KV) for part in range(2)], axis=1)

    def kern(q_ref, k_ref, v_ref, cos_ref, sin_ref, qa_ref, kp_ref, vp_ref, kc_ref, vc_ref):
        i = pl.program_id(0)
        cos_v, sin_v = cos_ref[...], sin_ref[...]
        @pl.when(i < CTX_BLKS)
        def _():
            qa_ref[...] = (q_ref[...].astype(F32) * ATT_SCALE).astype(BF16)
            kc_ref[...] = k_ref[...]
            vc_ref[...] = with_ones(v_ref[...])

        @pl.when((i < CTX_BLKS) | (i >= nt))
        def _():
            kp_ref[...] = jnp.zeros(kp_ref.shape, BF16)
            vp_ref[...] = jnp.zeros(vp_ref.shape, BF16)

        @pl.when((i >= CTX_BLKS) & (i < nt))
        def _():
            qa_ref[...] = (_rope(q_ref[...].astype(F32), cos_v, sin_v) * ATT_SCALE).astype(BF16)
            kp_ref[...] = _rope(k_ref[...].astype(F32), cos_v, sin_v).astype(BF16)
            vp_ref[...] = with_ones(v_ref[...])

    tok = lambda i: jnp.minimum(i, nt - 1)
    lat_map = lambda i: (jnp.clip(i - CTX_BLKS, 0, nt - CTX_BLKS - 1), 0)
    ctx_map = lambda i: (jnp.minimum(i, CTX_BLKS - 1), 0)
    return pl.pallas_call(
        kern, name=name, grid=(nt + CTX_BLKS,),
        in_specs=[pl.BlockSpec((AB, N_Q * HEAD), lambda i: (tok(i), COL_Q // (N_Q * HEAD))),
                  pl.BlockSpec((AB, KW), lambda i: (tok(i), COL_K // KW)),
                  pl.BlockSpec((AB, KW), lambda i: (tok(i), COL_V // KW)),
                  pl.BlockSpec((AB, HEAD), lat_map), pl.BlockSpec((AB, HEAD), lat_map)],
        out_specs=[pl.BlockSpec((AB, N_Q * HEAD), lambda i: (tok(i), 0)),
                   pl.BlockSpec((AB, KW), lambda i: (i, 0)), pl.BlockSpec((AB, 2 * KW), lambda i: (i, 0)),
                   pl.BlockSpec((AB, KW), ctx_map), pl.BlockSpec((AB, 2 * KW), ctx_map)],
        out_shape=[_sds((T, N_Q * HEAD), BF16), _sds((S + 2 * CTX, KW), BF16), _sds((S + 2 * CTX, 2 * KW), BF16),
                   _sds((CTX, KW), BF16), _sds((CTX, 2 * KW), BF16)],
        compiler_params=_params(),
    )(p, p, p, cos, sin)


GW = Q_PER_KV * HEAD


def _band_bias(S):
    r = jnp.arange(AB, dtype=jnp.int32)[:, None]
    c = jnp.arange(3 * AB, dtype=jnp.int32)[None, :]
    near = jnp.abs(c - AB - r) <= AB
    valid = jnp.stack([near & (c >= AB), near, near & (c < 2 * AB)])
    return jnp.where(valid, 0.0, NEG_INF).astype(F32)


def _bias_spec(S):
    nb = S // AB
    return pl.BlockSpec((None, AB, 3 * AB), lambda kh, n: (jnp.where(n == 0, 0, jnp.where(n == nb - 1, 2, 1)), 0, 0))


def _head_probs(q, sink, kc, vce, kb, vbe, bias):
    s_c = _dot(q, kc, NT)
    m = jnp.maximum(jnp.max(s_c, axis=-1, keepdims=True), sink)
    if kb is not None:
        s_b = _dot(q, kb, NT) + bias
        m = jnp.maximum(m, jnp.max(s_b, axis=-1, keepdims=True))
    p_c = jnp.exp(s_c - m).astype(BF16)
    acc = _dot(p_c, vce)
    p_b = None
    if kb is not None:
        p_b = jnp.exp(s_b - m).astype(BF16)
        acc = acc + _dot(p_b, vbe)
    return p_c, p_b, m, acc


def _attn_fwd(name, qa, kc, vc, sink4, S, band=None, prev=None, carry=None):
    T = qa.shape[0]
    has_band = band is not None
    nq = S // AB if has_band else CTX_BLKS
    q_off = CTX_BLKS if has_band else 0

    def kern(*refs):
        q_ref, kc_ref, vc_ref, sink_ref = refs[:4]
        rest = refs[4:]
        o_ref = rest[-1]
        n = pl.program_id(1)
        kc_v, vce = kc_ref[...], vc_ref[...]
        kb = vbe = bias = None
        if has_band:
            kp_ref, vp_ref, bias_ref = rest[:3]
            start = pl.multiple_of(n * AB + (CTX - AB), AB)
            kb = kp_ref[pl.ds(start, 3 * AB), :]
            vbe = vp_ref[pl.ds(start, 3 * AB), :]
            bias = bias_ref[...]
        outs = []
        for g in range(Q_PER_KV):
            sink = sink_ref[g:g + 1, 0:1]
            _, _, m, acc = _head_probs(q_ref[:, g * HEAD:(g + 1) * HEAD], sink, kc_v, vce, kb, vbe, bias)
            l = acc[:, HEAD:] + jnp.exp(sink - m)
            outs.append(acc[:, :HEAD] / l)
        o_ref[...] = jnp.concatenate(outs, axis=1).astype(BF16)

    in_specs = [pl.BlockSpec((AB, GW), lambda kh, n: (n + q_off, kh)),
                pl.BlockSpec((CTX, HEAD), lambda kh, n: (0, kh)), pl.BlockSpec((CTX, 2 * HEAD), lambda kh, n: (0, kh)),
                pl.BlockSpec((None, Q_PER_KV, HEAD), lambda kh, n: (kh, 0, 0))]
    args = [qa, kc, vc, sink4]
    if has_band:
        in_specs += [pl.BlockSpec((S + 2 * CTX, HEAD), lambda kh, n: (0, kh)),
                     pl.BlockSpec((S + 2 * CTX, 2 * HEAD), lambda kh, n: (0, kh)), _bias_spec(S)]
        args += list(band)
    alias = {}
    if prev is not None:
        in_specs.append(ANY)
        alias = {len(args): 0}
        args.append(prev)
    ci, ca, co, cs, cscr = _carry_args(carry)
    res = pl.pallas_call(
        _carried(kern, carry, len(args), 1, *_grid_ends((N_KV, nq))), name=name, grid=(N_KV, nq),
        in_specs=in_specs + ci,
        out_specs=[pl.BlockSpec((AB, GW), lambda kh, n: (n + q_off, kh))] + co,
        out_shape=[_sds((T, N_Q * HEAD), BF16)] + cs, input_output_aliases=alias, scratch_shapes=cscr,
        compiler_params=_params(("arbitrary", "arbitrary")),
    )(*args, *ca)
    return res[0] if carry is None else (res[0], res[1:])


def _attn_bwd(name, qa, kc, vc, sink4, o_all, do_all, S, band=None, prev_dq=None, carry=None):
    T = qa.shape[0]
    has_band = band is not None
    nq = S // AB if has_band else CTX_BLKS
    q_off = CTX_BLKS if has_band else 0
    KW = N_KV * HEAD

    def kern(*refs):
        q_ref, kc_ref, vc_ref, sink_ref, o_ref, do_ref = refs[:6]
        rest = refs[6:]
        if has_band:
            kp_ref, vp_ref, bias_ref = rest[:3]
            rest = rest[3:]
        if prev_dq is not None:
            rest = rest[1:]
        dq_ref, dkc_ref, dvc_ref, dsink_ref = rest[:4]
        n = pl.program_id(1)

        @pl.when(n == 0)
        def _():
            dkc_ref[...] = jnp.zeros(dkc_ref.shape, F32)
            dvc_ref[...] = jnp.zeros(dvc_ref.shape, F32)
            dsink_ref[...] = jnp.zeros(dsink_ref.shape, F32)
            if has_band:
                rest[4][...] = jnp.zeros(rest[4].shape, F32)
                rest[5][...] = jnp.zeros(rest[5].shape, F32)

        kc_v, vce = kc_ref[...], vc_ref[...]
        vc_v = vce[:, :HEAD]
        kb = vbe = vb = bias = None
        if has_band:
            start = pl.multiple_of(n * AB + (CTX - AB), AB)
            kb = kp_ref[pl.ds(start, 3 * AB), :]
            vbe = vp_ref[pl.ds(start, 3 * AB), :]
            vb = vbe[:, :HEAD]
            bias = bias_ref[...]
        stack = lambda ref: jnp.concatenate([ref[:, g * HEAD:(g + 1) * HEAD] for g in range(Q_PER_KV)], axis=0)
        q4, do4 = stack(q_ref), stack(do_ref)
        sink = jnp.concatenate([jnp.broadcast_to(sink_ref[g:g + 1, 0:1], (AB, 1)) for g in range(Q_PER_KV)], axis=0)
        s_c = _dot(q4, kc_v, NT)
        m = jnp.maximum(jnp.max(s_c, axis=-1, keepdims=True), sink)
        if has_band:
            s_b = _dot(q4, kb, NT) + jnp.tile(bias, (Q_PER_KV, 1))
            m = jnp.maximum(m, jnp.max(s_b, axis=-1, keepdims=True))
        p_c = jnp.exp(s_c - m).astype(BF16).astype(F32)
        p_sink = jnp.exp(sink - m)
        l = jnp.sum(p_c, axis=-1, keepdims=True) + p_sink
        if has_band:
            p_b = jnp.exp(s_b - m).astype(BF16).astype(F32)
            l = l + jnp.sum(p_b, axis=-1, keepdims=True)
        inv = 1.0 / l
        delta = jnp.sum(do4.astype(F32) * stack(o_ref).astype(F32), axis=-1, keepdims=True)
        do4b = do4.astype(BF16)
        pn_c = (p_c * inv).astype(BF16)
        ds_c = (p_c * inv * (_dot(do4b, vc_v, NT) - delta)).astype(BF16)
        dq4 = _dot(ds_c, kc_v)
        dkc_ref[...] += _dot(ds_c, q4, TN)
        dvc_ref[...] += _dot(pn_c, do4b, TN)
        if has_band:
            pn_b = (p_b * inv).astype(BF16)
            ds_b = (p_b * inv * (_dot(do4b, vb, NT) - delta)).astype(BF16)
            dq4 = dq4 + _dot(ds_b, kb)
            rest[4][pl.ds(start, 3 * AB), :] += _dot(ds_b, q4, TN)
            rest[5][pl.ds(start, 3 * AB), :] += _dot(pn_b, do4b, TN)
        dq4 = dq4 * ATT_SCALE
        dq_ref[...] = jnp.concatenate([dq4[g * AB:(g + 1) * AB, :] for g in range(Q_PER_KV)], axis=1)
        ps = p_sink * inv * delta
        dsink_ref[...] += jnp.concatenate(
            [jnp.broadcast_to(-jnp.sum(ps[g * AB:(g + 1) * AB, :], axis=0, keepdims=True), (1, HEAD))
             for g in range(Q_PER_KV)], axis=0)

    q_spec = pl.BlockSpec((AB, GW), lambda kh, n: (n + q_off, kh))
    c_spec = pl.BlockSpec((CTX, HEAD), lambda kh, n: (0, kh))
    ce_spec = pl.BlockSpec((CTX, 2 * HEAD), lambda kh, n: (0, kh))
    s_spec = pl.BlockSpec((None, Q_PER_KV, HEAD), lambda kh, n: (kh, 0, 0))
    in_specs = [q_spec, c_spec, ce_spec, s_spec, q_spec, q_spec]
    args = [qa, kc, vc, sink4, o_all, do_all]
    out_specs = [q_spec, c_spec, c_spec, s_spec]
    out_shape = [_sds((T, N_Q * HEAD), F32), _sds((CTX, KW), F32), _sds((CTX, KW), F32), _sds((N_KV, Q_PER_KV, HEAD), F32)]
    if has_band:
        p_spec = pl.BlockSpec((S + 2 * CTX, HEAD), lambda kh, n: (0, kh))
        in_specs += [p_spec, pl.BlockSpec((S + 2 * CTX, 2 * HEAD), lambda kh, n: (0, kh)), _bias_spec(S)]
        args += list(band)
        out_specs += [p_spec, p_spec]
        out_shape += [_sds((S + 2 * CTX, KW), F32)] * 2
    alias = {}
    if prev_dq is not None:
        in_specs.append(ANY)
        alias = {len(args): 0}
        args.append(prev_dq)
    ci, ca, co, cs, cscr = _carry_args(carry)
    n_out = len(out_specs)
    res = pl.pallas_call(
        _carried(kern, carry, len(args), n_out, *_grid_ends((N_KV, nq))), name=name, grid=(N_KV, nq),
        in_specs=in_specs + ci, out_specs=out_specs + co, out_shape=out_shape + cs, scratch_shapes=cscr,
        input_output_aliases=alias, compiler_params=_params(("arbitrary", "arbitrary")),
    )(*args, *ca)
    return res if carry is None else (res[:n_out], res[n_out:])


def _dqkv_assemble(name, dq_all, dkp, dvp, dkc_l, dvc_l, dkc_c, dvc_c, cos, sin, S):
    T = CTX + S
    KW = N_KV * HEAD
    HALF = N_Q * HEAD // 2

    def kern(dq_ref, dkp_ref, dvp_ref, dkcl_ref, dvcl_ref, dkcc_ref, dvcc_ref, cos_ref, sin_ref, out_ref):
        i = pl.program_id(0)
        j = pl.program_id(1)
        cos_v, sin_v = cos_ref[...], sin_ref[...]

        @pl.when((j < 2) & (i == 0))
        def _():
            out_ref[...] = dq_ref[...].astype(BF16)

        @pl.when((j < 2) & (i > 0))
        def _():
            out_ref[...] = _unrope(dq_ref[...], cos_v, sin_v).astype(BF16)

        @pl.when((j == 2) & (i == 0))
        def _():
            out_ref[...] = jnp.concatenate([dkcl_ref[...] + dkcc_ref[...], dvcl_ref[...] + dvcc_ref[...]],
                                           axis=1).astype(BF16)

        @pl.when((j == 2) & (i > 0))
        def _():
            out_ref[...] = jnp.concatenate([_unrope(dkp_ref[...], cos_v, sin_v), dvp_ref[...]], axis=1).astype(BF16)

    same = lambda i, j: (i, 0)
    lat_map = lambda i, j: (jnp.maximum(i - 1, 0), 0)
    ctx_map = lambda i, j: (0, 0)
    return pl.pallas_call(
        kern, name=name, grid=(T // TR, 3),
        in_specs=[pl.BlockSpec((TR, HALF), lambda i, j: (i, jnp.minimum(j, 1))),
                  pl.BlockSpec((TR, KW), same), pl.BlockSpec((TR, KW), same),
                  pl.BlockSpec((CTX, KW), ctx_map), pl.BlockSpec((CTX, KW), ctx_map),
                  pl.BlockSpec((CTX, KW), ctx_map), pl.BlockSpec((CTX, KW), ctx_map),
                  pl.BlockSpec((TR, HEAD), lat_map), pl.BlockSpec((TR, HEAD), lat_map)],
        out_specs=pl.BlockSpec((TR, HALF), lambda i, j: (i, COL_Q // HALF + j)),
        out_shape=_sds((T, DP_W), BF16), compiler_params=_params(("arbitrary", "arbitrary")),
    )(dq_all, dkp, dvp, dkc_l, dvc_l, dkc_c, dvc_c, cos, sin)


RB = 128
CH = 256
HALO = 8
SUB = 8
GRP = 8


def _vscan(a, b, reverse):
    row = lax.broadcasted_iota(jnp.int32, a.shape, 0)
    A, H = a, b
    for s in (1, 2, 4):
        sh = SUB - s if reverse else s
        m = (row < SUB - s) if reverse else (row >= s)
        As = pltpu.roll(A, sh, 0)
        Hs = pltpu.roll(H, sh, 0)
        H = jnp.where(m, A * Hs + H, H)
        A = jnp.where(m, A * As, A)
    return A, H


def _scan_rows(a_ref, b_ref, r0, nrows, reverse, carry, emit):
    ngrp = nrows // (SUB * GRP)
    row = lax.broadcasted_iota(jnp.int32, (SUB, RB), 0)

    def grp(gi, carry):
        g = (ngrp - 1 - gi) if reverse else gi
        base = r0 + g * (SUB * GRP)
        for v in (range(GRP - 1, -1, -1) if reverse else range(GRP)):
            rs = pl.multiple_of(base + v * SUB, SUB)
            A, H = _vscan(a_ref[pl.ds(rs, SUB), :], b_ref[pl.ds(rs, SUB), :], reverse)
            hf = H + A * carry
            if reverse:
                before = jnp.where(row == SUB - 1, carry, pltpu.roll(hf, SUB - 1, 0))
                carry = hf[0:1, :]
            else:
                before = jnp.where(row == 0, carry, pltpu.roll(hf, 1, 0))
                carry = hf[SUB - 1:SUB, :]
            emit(rs, hf, before)
        return carry

    return lax.fori_loop(0, ngrp, grp, carry)


def _pad_start(ci):
    return pl.multiple_of(ci * CH + HALO * jnp.minimum(ci, 1), HALO)


def _conv_taps(ext, transpose=False):
    n = CH + 2 * HALO
    taps = []
    for k in range(CONV_W):
        off = CONV_LEFT - k if transpose else k - CONV_LEFT
        taps.append(ext[HALO:HALO + CH, :] if off == 0 else pltpu.roll(ext, (-off) % n, 0)[HALO:HALO + CH, :])
    return taps


def _lru_gates(xl, w4, b4, ls):
    pre = _dot(xl.astype(BF16), w4) + b4
    out = []
    for d in range(2):
        r = _sigmoid(pre[:, d * RB:(d + 1) * RB])
        i = _sigmoid(pre[:, (2 + d) * RB:(3 + d) * RB])
        la = LRU_C * r * ls[d:d + 1, :]
        a = jnp.exp(la)
        q = -jnp.tanh(la) * (1.0 + a * a)
        out.append((r, i, a, q))
    return out


def _rnn_specs(T):
    col = lambda n, *_: (0, n)
    return dict(
        xr=pl.BlockSpec((T, RB), lambda n, *_: (0, COL_XR // RB + n)),
        gr=pl.BlockSpec((T, RB), lambda n, *_: (0, COL_GR // RB + n)),
        act=pl.BlockSpec((T, RB), col),
        cw=pl.BlockSpec((CONV_W, RB), col), cb=pl.BlockSpec((1, RB), col),
        w4=pl.BlockSpec((None, RB, 4 * RB), lambda n, *_: (n, 0, 0)),
        b4=pl.BlockSpec((None, 1, 4 * RB), lambda n, *_: (n, 0, 0)),
        lam=pl.BlockSpec((2, RB), col))


PAD_ROWS = 3 * HALO


def _zero_pads(pad_ref, T):
    for r in (0, HALO + CTX, 2 * HALO + T):
        pad_ref[r:r + HALO, :] = jnp.zeros((HALO, RB), F32)


def _fill_padded(pad_ref, src_ref, T):
    _zero_pads(pad_ref, T)
    pad_ref[HALO:HALO + CTX, :] = src_ref[0:CTX, :].astype(F32)
    pad_ref[2 * HALO + CTX:2 * HALO + T, :] = src_ref[CTX:T, :].astype(F32)


def _pad_rows(ci):
    return pl.ds(pl.multiple_of(ci * CH + HALO + HALO * jnp.minimum(ci, 1), HALO), CH)


def _rnn_fwd(name, p, cw, cb, w4, b4, lam, T, carry=None):
    def kern(xr_ref, gr_ref, cw_ref, cb_ref, w4_ref, b4_ref, lam_ref,
             u_ref, a0, a1, yo_ref, hpf_ref, hpb_ref, r0_ref, r1_ref, i0_ref, i1_ref, xpad, b0, b1, y):
        _fill_padded(xpad, xr_ref, T)
        ls = _log_sigmoid(lam_ref[...])
        w4v, b4v, cwv, cbv = w4_ref[...], b4_ref[...], cw_ref[...], cb_ref[...]

        def chunk(ci, _):
            rows = pl.ds(pl.multiple_of(ci * CH, CH), CH)
            taps = _conv_taps(xpad[pl.ds(_pad_start(ci), CH + 2 * HALO), :])
            xl = cbv + sum(taps[k] * cwv[k:k + 1, :] for k in range(CONV_W))
            for d, (r, i, a, q) in enumerate(_lru_gates(xl, w4v, b4v, ls)):
                (a0, a1)[d][rows, :] = a
                (b0, b1)[d][rows, :] = jnp.sqrt(q) * (i * xl)
                (r0_ref, r1_ref)[d][rows, :] = r.astype(BF16)
                (i0_ref, i1_ref)[d][rows, :] = i.astype(BF16)
            return 0

        lax.fori_loop(0, T // CH, chunk, 0)
        zero = jnp.zeros((1, RB), F32)

        def emit_f(rs, hf, before):
            y[pl.ds(rs, SUB), :] = hf
            b0[pl.ds(rs, SUB), :] = before

        def emit_b(rs, hf, before):
            y[pl.ds(rs, SUB), :] += hf
            b1[pl.ds(rs, SUB), :] = before

        _scan_rows(a0, b0, 0, T, False, zero, emit_f)
        c = _scan_rows(a1, b1, 0, CTX, True, zero, emit_b)
        _scan_rows(a1, b1, CTX, T - CTX, True, c, emit_b)

        def finish(ci, _):
            rows = pl.ds(pl.multiple_of(ci * CH, CH), CH)
            yv = y[rows, :]
            u_ref[rows, :] = (yv * _gelu(gr_ref[rows, :].astype(F32))).astype(BF16)
            yo_ref[rows, :] = yv.astype(BF16)
            hpf_ref[rows, :] = b0[rows, :].astype(BF16)
            hpb_ref[rows, :] = b1[rows, :].astype(BF16)
            return 0

        lax.fori_loop(0, T // CH, finish, 0)

    sp = _rnn_specs(T)
    ci, ca, co, cs, cscr = _carry_args(carry)
    dts = [BF16, F32, F32] + [BF16] * 7
    res = pl.pallas_call(
        _carried(kern, carry, 7, 10, *_grid_ends((N_RNN_BLOCKS,))), name=name, grid=(N_RNN_BLOCKS,),
        in_specs=[sp["xr"], sp["gr"], sp["cw"], sp["cb"], sp["w4"], sp["b4"], sp["lam"]] + ci,
        out_specs=[sp["act"]] * 10 + co,
        out_shape=[_sds((T, D), dt) for dt in dts] + cs,
        scratch_shapes=[pltpu.VMEM((T + PAD_ROWS, RB), F32)] + [pltpu.VMEM((T, RB), F32)] * 3 + cscr,
        compiler_params=_params(),
    )(p, p, cw, cb, w4, b4, lam, *ca)
    return res if carry is None else (res[:10], res[10:])


def _rnn_bwd(name, p, du, saved, dp, cw, cb, w4, b4, lam, T, carry=None):
    def kern(xr_ref, gr_ref, du_ref, a0, a1, y_ref, hpf_ref, hpb_ref, r0_ref, r1_ref, i0_ref, i1_ref,
             cw_ref, cb_ref, w4_ref, b4_ref, lam_ref, dp_in,
             dp_ref, dcw_ref, dcb_ref, dw4_ref, db4_ref, dlam_ref,
             xpad, dxpad, c0, c1, dy):
        j = pl.program_id(1)

        @pl.when(j == 0)
        def _():
            scans(gr_ref, du_ref, a0, a1, y_ref, dp_ref, c0, c1, dy)

        @pl.when(j == 1)
        def _():
            gates(xr_ref, a0, a1, (hpf_ref, hpb_ref), (r0_ref, r1_ref), (i0_ref, i1_ref), cw_ref, cb_ref, w4_ref,
                  lam_ref, dp_ref, dcw_ref, dcb_ref, dw4_ref, db4_ref, dlam_ref, xpad, dxpad, c0, c1)

    def scans(gr_ref, du_ref, a0, a1, y_ref, dgr_ref, c0, c1, dy):
        def phase_a(ci, _):
            rows = pl.ds(pl.multiple_of(ci * CH, CH), CH)
            gr = gr_ref[rows, :].astype(F32)
            duv = du_ref[rows, :].astype(F32)
            dyv = duv * _gelu(gr)
            dgr_ref[rows, :] = (duv * y_ref[rows, :].astype(F32) * _gelu_grad(gr)).astype(BF16)
            dy[rows, :] = dyv
            c0[rows, :] = a0[rows, :] * dyv
            c1[rows, :] = a1[rows, :] * dyv
            return 0

        lax.fori_loop(0, T // CH, phase_a, 0)
        zero = jnp.zeros((1, RB), F32)

        def emit0(rs, hf, before):
            c0[pl.ds(rs, SUB), :] = dy[pl.ds(rs, SUB), :] + before

        def emit1(rs, hf, before):
            c1[pl.ds(rs, SUB), :] = dy[pl.ds(rs, SUB), :] + before

        _scan_rows(a0, c0, 0, T, True, zero, emit0)
        c = _scan_rows(a1, c1, CTX, T - CTX, False, zero, emit1)
        _scan_rows(a1, c1, 0, CTX, False, c, emit1)

    def gates(xr_ref, a0, a1, hp_refs, r_refs, i_refs, cw_ref, cb_ref, w4_ref, lam_ref,
              dxr_ref, dcw_ref, dcb_ref, dw4_ref, db4_ref, dlam_ref, xpad, dxpad, c0, c1):
        _fill_padded(xpad, xr_ref, T)
        _zero_pads(dxpad, T)
        lam_v = lam_ref[...]
        ls = _log_sigmoid(lam_v)
        w4v, cwv, cbv = w4_ref[...], cw_ref[...], cb_ref[...]

        def conv_chunk(ci):
            taps = _conv_taps(xpad[pl.ds(_pad_start(ci), CH + 2 * HALO), :])
            return taps, cbv + sum(taps[k] * cwv[k:k + 1, :] for k in range(CONV_W))

        dw4_ref[...] = jnp.zeros(dw4_ref.shape, F32)
        db4_ref[...] = jnp.zeros(db4_ref.shape, F32)
        dlam_ref[...] = jnp.zeros(dlam_ref.shape, F32)
        dcw_ref[...] = jnp.zeros(dcw_ref.shape, F32)
        dcb_ref[...] = jnp.zeros(dcb_ref.shape, F32)

        def phase_c(ci, _):
            base = pl.multiple_of(ci * CH, CH)
            rows = pl.ds(base, CH)
            _, xl = conv_chunk(ci)
            dxl = jnp.zeros((CH, RB), F32)
            dpre_a, dpre_x, dls = [], [], []
            for d in range(2):
                a = (a0, a1)[d][rows, :]
                r = r_refs[d][rows, :].astype(F32)
                i = i_refs[d][rows, :].astype(F32)
                q = -jnp.tanh(LRU_C * r * ls[d:d + 1, :]) * (1.0 + a * a)
                g = (c0, c1)[d][rows, :]
                hp = hp_refs[d][rows, :].astype(F32)
                gm = g * jnp.sqrt(q)
                di = gm * xl
                dxl = dxl + gm * i
                dla = a * (g * hp - a * (g * (i * xl)) * lax.rsqrt(q))
                dr = dla * (LRU_C * ls[d:d + 1, :])
                dls.append(_colsum(dla * (LRU_C * r)))
                dpre_a.append(dr * r * (1.0 - r))
                dpre_x.append(di * i * (1.0 - i))
            dpre = jnp.concatenate(dpre_a + dpre_x, axis=1)
            dpre_b = dpre.astype(BF16)
            dxl = dxl + _dot(dpre_b, w4v, NT)
            dw4_ref[...] += _dot(xl.astype(BF16), dpre_b, TN)
            db4_ref[...] += _colsum(dpre)
            dlam_ref[...] += jnp.concatenate(dls, axis=0)
            dcb_ref[...] += _colsum(dxl)
            dxpad[_pad_rows(ci), :] = dxl
            return 0

        lax.fori_loop(0, T // CH, phase_c, 0)
        dlam_ref[...] = dlam_ref[...] * _sigmoid(-lam_v)

        def phase_d(ci, _):
            base = pl.multiple_of(ci * CH, CH)
            rows = pl.ds(base, CH)
            xtaps, _ = conv_chunk(ci)
            dtaps = _conv_taps(dxpad[pl.ds(_pad_start(ci), CH + 2 * HALO), :], transpose=True)
            dxl = dxpad[_pad_rows(ci), :]
            dxr_ref[rows, :] = sum(dtaps[k] * cwv[k:k + 1, :] for k in range(CONV_W)).astype(BF16)
            dcw_ref[...] += jnp.concatenate([_colsum(dxl * xtaps[k]) for k in range(CONV_W)], axis=0)
            return 0

        lax.fori_loop(0, T // CH, phase_d, 0)

    sp = _rnn_specs(T)
    dp_spec = pl.BlockSpec((T, RB), lambda n, j: (0, COL_GR // RB + n - j * (COL_GR - COL_XR) // RB))
    ci, ca, co, cs, cscr = _carry_args(carry)
    n_in = 3 + len(saved) + 5 + 1
    res = pl.pallas_call(
        _carried(kern, carry, n_in, 6, *_grid_ends((N_RNN_BLOCKS, 2))), name=name, grid=(N_RNN_BLOCKS, 2),
        in_specs=[sp["xr"], sp["gr"]] + [sp["act"]] * (1 + len(saved)) + [sp["cw"], sp["cb"], sp["w4"], sp["b4"],
                                                                           sp["lam"], ANY] + ci,
        out_specs=[dp_spec, sp["cw"], sp["cb"], sp["w4"], sp["b4"], sp["lam"]] + co,
        out_shape=[_sds((T, DP_W), BF16), _sds((CONV_W, D), F32), _sds((1, D), F32),
                   _sds((N_RNN_BLOCKS, RB, 4 * RB), F32), _sds((N_RNN_BLOCKS, 1, 4 * RB), F32), _sds((2, D), F32)] + cs,
        scratch_shapes=[pltpu.VMEM((T + PAD_ROWS, RB), F32)] * 2 + [pltpu.VMEM((T, RB), F32)] * 3 + cscr,
        input_output_aliases={n_in - 1: 0},
        compiler_params=_params(("arbitrary", "arbitrary")),
    )(p, p, du, *saved, cw, cb, w4, b4, lam, dp, *ca)
    return res if carry is None else (res[:6], res[6:])


class _Plan:
    def __init__(self, shards, Ws):
        L = len(Ws)
        self.shards, self.Ws = shards, Ws
        self.Gs = [None] * L
        self.slots = [dict() for _ in range(L)]
        self.gate_slots = [None] * L
        self.table = {}
        for l in range(L):
            t = f"l{l}_"
            self.table[t + "proj"] = [("gather", l, k) for k in ("wo_rnn", "wo_attn", "wout")]
            self.table[t + "rnn_fwd"] = ([("gather", l, "wffn_in_t")]
                                         + ([("gather", l + 1, "win_t")] if l + 1 < L else []))
            self.table[t + "attn_lat_fwd"] = [("gather", l, "wffn_out")]
            self.table[t + "ffn_in_dx"] = [("scatter", l, "wffn_out")]
            self.table[t + "attn_lat_bwd"] = [("scatter", l, "wffn_in_t")]
            self.table[t + "proj_dx"] = [("scatter", l, "win_t_a")]
            self.table[t + "rnn_bwd"] = ([("scatter", l, k) for k in ("wout", "wo_attn", "wo_rnn")]
                                         + ([("scatter", l + 1, "win_t_b"), ("gates", l + 1, "w4")] if l + 1 < L else []))
        self.table["l0_proj_dw_b"] = [("gates", 0, "w4")]

    def carry(self, name):
        jobs = []
        for kind, l, k in self.table.get(name, []):
            if kind == "gather":
                jobs.append(("gather", self.shards[l][k]))
            elif kind == "scatter":
                jobs.append(("scatter", self.Gs[l][k].reshape(N_DEV, -1, self.Gs[l][k].shape[-1])))
            else:
                jobs.append(("gather", self.Gs[l]["w4"].reshape(N_RNN_BLOCKS * RB, 4 * RB).astype(BF16)))
        return _Carry(jobs) if jobs else None

    def done(self, name, got):
        for (kind, l, k), res in zip(self.table[name], got):
            if kind == "gather":
                self.Ws[l][k] = res.reshape(-1, D)
            elif kind == "scatter":
                self.slots[l][k] = res
            else:
                self.gate_slots[l] = res


def _run(X, fn, name, *args, **kw):
    carry = None if X is None else X.carry(name)
    if carry is None:
        return fn(name, *args, **kw)
    out, got = fn(name, *args, carry=carry, **kw)
    X.done(name, got)
    return out


def _layer_fwd(l, xa, h, W, rope, S, nxt, X=None):
    T = xa.shape[0]
    tag = f"l{l}_"
    cos, sin, bias = rope
    p = _run(X, _mm_act, tag + "proj", h, W["win_t"], "NT", BF16)
    u, *rnn_saved = _run(X, _rnn_fwd, tag + "rnn_fwd", p, W["cw"], W["cb"], W["w4"], W["b4"], W["lam"], T)
    qa, kp, vp, kc, vc = _qkv_prep(tag + "qkv_prep", p, cos, sin, S)
    o_all = _attn_fwd(tag + "attn_ctx_fwd", qa, kc, vc, W["sink4"], S)
    o_all = _run(X, _attn_fwd, tag + "attn_lat_fwd", qa, kc, vc, W["sink4"], S, band=(kp, vp, bias), prev=o_all)
    ya, yb, z, m, x1, h2 = _out_fused(tag + "out", p, u, o_all, xa, W["wo_rnn"], W["wo_attn"], W["wout"],
                                      W["g_mix_post"], W["mod"], W["g_ffn_pre"])
    fg, fu, s = _run(X, _ffn_in_fused, tag + "ffn_in", h2, W["wffn_in_t"])
    e, *out = _ffn_out_fused(tag + "ffn_out", s, W["wffn_out"], x1, W["g_ffn_post"], W["mod"], nxt)
    saved = dict(xa=xa, h=h, p=p, u=u, rnn=rnn_saved, qa=qa, kp=kp, vp=vp, kc=kc, vc=vc, o_all=o_all,
                 ya=ya, yb=yb, z=z, m=m, x1=x1, h2=h2, fg=fg, fu=fu, s=s, e=e)
    return saved, out


def _layer_bwd(l, dx2, A, W, rope, S, X=None, loss_of=None):
    T = A["xa"].shape[0]
    tag = f"l{l}_"
    cos, sin, bias = rope
    G = {}
    if X is not None:
        X.Gs[l] = G
    if loss_of is None:
        de, df, dga2, G["g_ffn_post"] = _ffn_bwd_fused(tag + "ffn_bwd", A["fg"], A["fu"], W["wffn_out"],
                                                       head=(dx2, A["e"], W["g_ffn_post"], W["mod"]))
    else:
        dx2, de, dga2, G["g_ffn_post"], G["sq"] = _loss_resid_bwd(tag + "loss_ffn_resid_bwd", *loss_of, A["e"],
                                                                  W["g_ffn_post"], W["mod"], GA2)
        df, = _ffn_bwd_fused(tag + "ffn_bwd", A["fg"], A["fu"], W["wffn_out"], de=de)
    G["wffn_out"] = _mm_wgrad(tag + "ffn_out_dw", A["s"], de)
    dx1, dm, dsh2, dsc2, G["g_ffn_pre"], dga1, G["g_mix_post"] = _run(
        X, _ffn_in_bwd_fused, tag + "ffn_in_dx", df, W["wffn_in_t"], A["x1"], dx2, A["m"], W["g_ffn_pre"], W["mod"],
        W["g_mix_post"])
    G["wffn_in_t"] = _run(X, _mm_wgrad, tag + "ffn_in_dw", df, A["h2"])
    G["wout"] = _mm_wgrad(tag + "out_dw", A["z"], dm)
    dya, dyb, dgl, du, do = _out_bwd_fused(tag + "out_dx", dm, W["wout"], W["wo_rnn"], W["wo_attn"], A["p"], A["ya"],
                                           A["yb"])
    G["wo_attn"] = _mm_wgrad(tag + "o_attn_dw", A["o_all"], dyb)
    G["wo_rnn"] = _mm_wgrad(tag + "o_rnn_dw", A["u"], dya)
    dq_all, dkc_c, dvc_c, dsink_c = _attn_bwd(tag + "attn_ctx_bwd", A["qa"], A["kc"], A["vc"], W["sink4"],
                                               A["o_all"], do, S)
    dq_all, dkc_l, dvc_l, dsink_l, dkp, dvp = _run(
        X, _attn_bwd, tag + "attn_lat_bwd", A["qa"], A["kc"], A["vc"], W["sink4"], A["o_all"], do, S,
        band=(A["kp"], A["vp"], bias), prev_dq=dq_all)
    G["sink4"] = dsink_c + dsink_l
    dp = _dqkv_assemble(tag + "dqkv", dq_all, dkp, dvp, dkc_l, dvc_l, dkc_c, dvc_c, cos, sin, S)
    dp, G["cw"], G["cb"], G["w4"], G["b4"], G["lam"] = _run(
        X, _rnn_bwd, tag + "rnn_bwd", A["p"], du, A["rnn"], dp, W["cw"], W["cb"], W["w4"], W["b4"], W["lam"], T)
    proj_dx = (_proj_bwd_fused, tag + "proj_dx", dp, dgl, W["win_t"], A["xa"], dx1, W["g_mix_pre"], W["mod"])
    if X is not None:
        G["win_t_a"] = _proj_wgrad(tag + "proj_dw_a", dp, dgl, A["h"][:, :D // 2])
        dxa, dsh1, dsc1, G["g_mix_pre"] = _run(X, *proj_dx)
        G["win_t_b"] = _run(X, _proj_wgrad, tag + "proj_dw_b", dp, dgl, A["h"][:, D // 2:])
    else:
        dxa, dsh1, dsc1, G["g_mix_pre"] = _run(X, *proj_dx)
        G["win_t"] = _proj_wgrad(tag + "proj_dw", dp, dgl, A["h"])
    G["mod"] = jnp.concatenate([dsh1, dsc1, dga1, dsh2, dsc2, dga2], axis=1)
    return dxa, G


def _local_step(xa, target, Ws, S, X=None):
    rope = (*_rope_tables(S), _band_bias(S))
    L = len(Ws)
    h = _normmod_fwd("l0_mix_norm", xa, Ws[0]["g_mix_pre"], Ws[0]["mod"], SH1, SC1)
    saved = []
    x = xa
    for l in range(L):
        nxt = (Ws[l + 1]["g_mix_pre"], Ws[l + 1]["mod"]) if l + 1 < L else None
        A, out = _layer_fwd(l, x, h, Ws[l], rope, S, nxt, X)
        saved.append(A)
        if l + 1 < L:
            x, h = out
    Gs = [None] * L
    dx = None
    for l in reversed(range(L)):
        dx, Gs[l] = _layer_bwd(l, dx, saved[l], Ws[l], rope, S, X, loss_of=(out[0], target) if l == L - 1 else None)
    return Gs[L - 1]["sq"], dx, Gs


MESH = pl.DeviceIdType.MESH


def _place():
    return lax.axis_index("x"), lax.axis_index("y"), lax.axis_index("c")


def _lin(px, py, pc):
    return 4 * px + 2 * py + pc


def _allgather_small(name, blk):
    m, n = blk.shape

    def body(x_ref, out_ref, send_sems, recv_sems, local_sem):
        x, y, c = _place()
        me, sibling = (x, y, c), (x, y, 1 - c)
        chips = [(1 - x, y), (x, 1 - y), (1 - x, 1 - y)]

        def copy(k, block, to, src=None):
            dst = out_ref.at[_lin(*block)]
            return pltpu.make_async_remote_copy(src_ref=dst if src is None else src, dst_ref=dst,
                                                send_sem=send_sems.at[k], recv_sem=recv_sems.at[k],
                                                device_id=to, device_id_type=MESH)

        mine = pltpu.make_async_copy(x_ref, out_ref.at[_lin(*me)], local_sem)
        mine.start()
        first = [copy(0, me, sibling, src=x_ref)]
        first += [copy(1 + j, me, (*chip, c), src=x_ref) for j, chip in enumerate(chips)]
        for cp in first:
            cp.start()
        passed = [copy(4 + j, (*chip, c), sibling) for j, chip in enumerate(chips)]
        for j, chip in enumerate(chips):
            copy(1 + j, (*chip, c), me).wait_recv()
            passed[j].start()
        copy(0, sibling, me).wait_recv()
        for j, chip in enumerate(chips):
            copy(4 + j, (*chip, 1 - c), me).wait_recv()
        for cp in first + passed:
            cp.wait_send()
        mine.wait()

    return pl.pallas_call(
        body, name=name, out_shape=_sds((N_DEV, m, n), blk.dtype),
        in_specs=[pl.BlockSpec(memory_space=pltpu.VMEM)], out_specs=pl.BlockSpec(memory_space=pltpu.VMEM),
        scratch_shapes=[pltpu.SemaphoreType.DMA((7,)), pltpu.SemaphoreType.DMA((7,)), pltpu.SemaphoreType.DMA],
        compiler_params=pltpu.CompilerParams(vmem_limit_bytes=VMEM_LIMIT),
    )(blk)


def _allgather_hbm(name, shards):
    na = len(shards)

    def body(*refs):
        ins, outs = refs[:na], refs[na:2 * na]
        send_sems, recv_sems, local_sems = refs[2 * na:]
        x, y, c = _place()
        me, sibling = (x, y, c), (x, y, 1 - c)
        chips = [(1 - x, y), (x, 1 - y), (1 - x, 1 - y)]

        def copy(a, k, block, to, from_input=False):
            dst = outs[a].at[_lin(*block)]
            return pltpu.make_async_remote_copy(src_ref=ins[a] if from_input else dst, dst_ref=dst,
                                                send_sem=send_sems.at[a, k], recv_sem=recv_sems.at[a, k],
                                                device_id=to, device_id_type=MESH)

        mine = [pltpu.make_async_copy(ins[a], outs[a].at[_lin(*me)], local_sems.at[a]) for a in range(na)]
        for cp in mine:
            cp.start()
        first = []
        for a in range(na):
            first.append(copy(a, 0, me, sibling, True))
            first += [copy(a, 1 + j, me, (*chip, c), True) for j, chip in enumerate(chips)]
        for cp in first:
            cp.start()
        passed = []
        for j, chip in enumerate(chips):
            for a in range(na):
                copy(a, 1 + j, (*chip, c), me).wait_recv()
                fwd = copy(a, 4 + j, (*chip, c), sibling)
                fwd.start()
                passed.append(fwd)
        for a in range(na):
            copy(a, 0, sibling, me).wait_recv()
            for j, chip in enumerate(chips):
                copy(a, 4 + j, (*chip, 1 - c), me).wait_recv()
        for cp in first + passed:
            cp.wait_send()
        for cp in mine:
            cp.wait()

    return pl.pallas_call(
        body, name=name, out_shape=[_sds((N_DEV, *s.shape), s.dtype) for s in shards],
        in_specs=[ANY] * na, out_specs=[ANY] * na,
        scratch_shapes=[pltpu.SemaphoreType.DMA((na, 7)), pltpu.SemaphoreType.DMA((na, 7)),
                        pltpu.SemaphoreType.DMA((na,))],
    )(*shards)


def _exchange_shards(name, grads, L):
    nw = len(grads)
    na = nw * L
    flat = [g for per_layer in grads for g in per_layer]

    def body(*refs):
        ins, outs = refs[:na], refs[na:na + nw]
        send_sems, recv_sems, local_sems = refs[na + nw:]
        x, y, c = _place()
        me = _lin(x, y, c)
        peers = [(x ^ ((k + 1) >> 2 & 1), y ^ ((k + 1) >> 1 & 1), c ^ ((k + 1) & 1)) for k in range(7)]

        def copy(a, k, src_blk, dst_blk):
            return pltpu.make_async_remote_copy(src_ref=ins[a].at[src_blk], dst_ref=outs[a // L].at[a % L, dst_blk],
                                                send_sem=send_sems.at[a, k], recv_sem=recv_sems.at[a, k],
                                                device_id=peers[k], device_id_type=MESH)

        mine = [pltpu.make_async_copy(ins[a].at[me], outs[a // L].at[a % L, me], local_sems.at[a]) for a in range(na)]
        for cp in mine:
            cp.start()
        sent = [copy(a, k, _lin(*peers[k]), me) for a in range(na) for k in range(7)]
        for cp in sent:
            cp.start()
        for a in range(na):
            for k in range(7):
                copy(a, k, me, _lin(*peers[k])).wait_recv()
        for cp in sent:
            cp.wait_send()
        for cp in mine:
            cp.wait()

    return pl.pallas_call(
        body, name=name, out_shape=[_sds((L, *per_layer[0].shape), per_layer[0].dtype) for per_layer in grads],
        in_specs=[ANY] * na, out_specs=[ANY] * nw,
        scratch_shapes=[pltpu.SemaphoreType.DMA((na, 7)), pltpu.SemaphoreType.DMA((na, 7)),
                        pltpu.SemaphoreType.DMA((na,))],
    )(*flat)


MOD_ROWS = 16
MOD_SHARD = 6 * D // N_DEV
HI = lax.Precision.HIGHEST


def _mod_fwd(name, c9, w_mod, b_shard):
    L = w_mod.shape[0]

    def kern(c_ref, w_ref, b_ref, o_ref):
        o_ref[...] = lax.dot_general(_silu(c_ref[...]), w_ref[...], NN, precision=HI,
                                     preferred_element_type=F32) + b_ref[...]

    return pl.pallas_call(
        kern, name=name, grid=(L,),
        in_specs=[_full_spec(c9.shape), pl.BlockSpec((None, D, MOD_SHARD), lambda l: (l, 0, 0)),
                  pl.BlockSpec((None, 1, MOD_SHARD), lambda l: (l, 0, 0))],
        out_specs=pl.BlockSpec((None, MOD_ROWS, MOD_SHARD), lambda l: (l, 0, 0)),
        out_shape=_sds((L, MOD_ROWS, MOD_SHARD), F32), compiler_params=_params(),
    )(c9, w_mod, b_shard)


def _mod_bwd(name, c9, w_mod, dmod_all, dmod_cols):
    L = w_mod.shape[0]

    def rows9(ref, l):
        own = jnp.concatenate([ref[j, 2 * l + 1:2 * l + 2, :] for j in range(N_DEV)], axis=0)
        ctx = ref[0, 2 * l:2 * l + 1, :]
        for j in range(1, N_DEV):
            ctx = ctx + ref[j, 2 * l:2 * l + 1, :]
        return own, ctx

    def kern(c_ref, w_ref, all_ref, cols_ref, gw_ref, gb_ref, gc_ref):
        l = pl.program_id(0)
        for ll in range(L):
            @pl.when(l == ll)
            def _():
                own, ctx = rows9(all_ref, ll)
                gb_ref[...] = _colsum(own) + ctx
                own_s, ctx_s = rows9(cols_ref, ll)
                r16 = jnp.concatenate([own_s, ctx_s, jnp.zeros((MOD_ROWS - N_DEV - 1, MOD_SHARD), F32)], axis=0)
                gw_ref[...] = lax.dot_general(_silu(c_ref[...]), r16, TN, precision=HI, preferred_element_type=F32)
                part = lax.dot_general(r16, w_ref[...], NT, precision=HI,
                                       preferred_element_type=F32)[N_DEV:N_DEV + 1, :]
                if ll == 0:
                    gc_ref[...] = part
                else:
                    gc_ref[...] += part

    return pl.pallas_call(
        kern, name=name, grid=(L,),
        in_specs=[_full_spec(c9.shape), pl.BlockSpec((None, D, MOD_SHARD), lambda l: (l, 0, 0)),
                  _full_spec(dmod_all.shape), _full_spec(dmod_cols.shape)],
        out_specs=[pl.BlockSpec((None, D, MOD_SHARD), lambda l: (l, 0, 0)),
                   pl.BlockSpec((None, 1, 6 * D), lambda l: (l, 0, 0)), _full_spec((1, D))],
        out_shape=[_sds((L, D, MOD_SHARD), F32), _sds((L, 1, 6 * D), F32), _sds((1, D), F32)],
        compiler_params=_params(),
    )(c9, w_mod, dmod_all, dmod_cols)


_BC1 = 1.0 - ADAM_B1 ** ADAM_STEP
_BC2 = 1.0 - ADAM_B2 ** ADAM_STEP


def _adamw_vals(w, g, m, v):
    m = ADAM_B1 * m + (1.0 - ADAM_B1) * g
    v = ADAM_B2 * v + (1.0 - ADAM_B2) * (g * g)
    delta = -ADAM_LR * ((m / _BC1) / (jnp.sqrt(v / _BC2) + ADAM_EPS) + ADAM_WD * w)
    return delta, m, v


def _adamw(name, w, g, m, v, tile):
    R, C = w.shape
    blk = ((tile, C), lambda i: (i, 0))

    def body(i, ins, ps, outs, acc):
        d, mm, vv = _adamw_vals(ins[0][...], ins[1][...], ins[2][...], ins[3][...])
        outs[0][...] = d
        outs[1][...] = mm
        outs[2][...] = vv

    return _ew(name, body, R // tile, [(a, *blk) for a in (w, g, m, v)], [], [(_sds((R, C), F32), *blk)] * 3)


def _sum_slots(ref):
    g = ref[0].astype(F32)
    for j in range(1, N_DEV):
        g = g + ref[j].astype(F32)
    return g


def _adamw_slots(name, slots, shape, tile, wmv=None):
    L, R, C = shape
    n = R // tile
    spec = pl.BlockSpec((None, tile, C), lambda l, i: (l, i, 0))
    pieces = [s if isinstance(s, (list, tuple)) else [s] for s in slots]
    layer_of = [ll for ll, ps in enumerate(pieces) for _ in ps]
    flat = [p for ps in pieces for p in ps]
    wmv = list(wmv or [])

    def slot_spec(ll, cols):
        return pl.BlockSpec((N_DEV, tile, cols),
                            lambda l, i: (0, jnp.where(l == ll, i, jnp.where(l < ll, 0, n - 1)), 0))

    def kern(*refs):
        s_refs = refs[:len(flat)]
        rest = refs[len(flat):]
        l = pl.program_id(0)
        for ll in range(L):
            @pl.when(l == ll)
            def _():
                parts = [_sum_slots(r) for r, lr in zip(s_refs, layer_of) if lr == ll]
                g = parts[0] if len(parts) == 1 else jnp.concatenate(parts, axis=1)
                if wmv:
                    w_ref, m_ref, v_ref, g_ref, d_ref, mo_ref, vo_ref = rest
                    d_ref[...], mo_ref[...], vo_ref[...] = _adamw_vals(w_ref[...], g, m_ref[...], v_ref[...])
                else:
                    g_ref, = rest
                g_ref[...] = g

    n_out = 4 if wmv else 1
    return pl.pallas_call(
        kern, name=name, grid=(L, n),
        in_specs=[slot_spec(ll, p.shape[-1]) for ll, p in zip(layer_of, flat)] + [spec] * len(wmv),
        out_specs=[spec] * n_out, out_shape=[_sds((L, R, C), F32)] * n_out,
        compiler_params=_params(("arbitrary", "arbitrary")),
    )(*flat, *wmv)


def _sum_blocks(name, blocks):
    _, R, C = blocks.shape

    def kern(b_ref, o_ref):
        o_ref[...] = _sum_slots(b_ref)

    return pl.pallas_call(kern, name=name, in_specs=[_full_spec(blocks.shape)], out_specs=_full_spec((R, C)),
                          grid=(1,), out_shape=_sds((R, C), F32), compiler_params=_params())(blocks)


BIG = ("win_t", "wo_rnn", "wo_attn", "wout", "wffn_in_t", "wffn_out")
BIG_SRC = ("w_in", "w_o_rnn", "w_o_attn", "w_out", "w_ffn_in", "w_ffn_out")
BIG_T = (True, False, False, False, True, False)
BIG_TILE = (176, 128, 128, 128, 176, 176)


def _chan_full(g8):
    return jnp.transpose(g8, (1, 0, 2)).reshape(g8.shape[1], D)


def kernel(x, c, ctx, c_ctx, w_mod, b_mod, g_mix_pre, g_mix_post, g_ffn_pre, g_ffn_post, w_in, conv_w, conv_b, lru_wa, lru_ba, lru_wx, lru_bx, lru_lam, attn_sink, w_o_rnn, w_o_attn, w_out, w_ffn_in, w_ffn_out, loss_target, m_c_ctx, m_w_mod, m_b_mod, m_g_mix_pre, m_g_mix_post, m_g_ffn_pre, m_g_ffn_post, m_w_in, m_conv_w, m_conv_b, m_lru_wa, m_lru_ba, m_lru_wx, m_lru_bx, m_lru_lam, m_attn_sink, m_w_o_rnn, m_w_o_attn, m_w_out, m_w_ffn_in, m_w_ffn_out, v_c_ctx, v_w_mod, v_b_mod, v_g_mix_pre, v_g_mix_post, v_g_ffn_pre, v_g_ffn_post, v_w_in, v_conv_w, v_conv_b, v_lru_wa, v_lru_ba, v_lru_wx, v_lru_bx, v_lru_lam, v_attn_sink, v_w_o_rnn, v_w_o_attn, v_w_out, v_w_ffn_in, v_w_ffn_out):
    P = dict(c_ctx=c_ctx, w_mod=w_mod, b_mod=b_mod, g_mix_pre=g_mix_pre, g_mix_post=g_mix_post, g_ffn_pre=g_ffn_pre,
             g_ffn_post=g_ffn_post, w_in=w_in, conv_w=conv_w, conv_b=conv_b, lru_wa=lru_wa, lru_ba=lru_ba,
             lru_wx=lru_wx, lru_bx=lru_bx, lru_lam=lru_lam, attn_sink=attn_sink, w_o_rnn=w_o_rnn, w_o_attn=w_o_attn,
             w_out=w_out, w_ffn_in=w_ffn_in, w_ffn_out=w_ffn_out)
    Mo = dict(c_ctx=m_c_ctx, w_mod=m_w_mod, b_mod=m_b_mod, g_mix_pre=m_g_mix_pre, g_mix_post=m_g_mix_post,
              g_ffn_pre=m_g_ffn_pre, g_ffn_post=m_g_ffn_post, w_in=m_w_in, conv_w=m_conv_w, conv_b=m_conv_b,
              lru_wa=m_lru_wa, lru_ba=m_lru_ba, lru_wx=m_lru_wx, lru_bx=m_lru_bx, lru_lam=m_lru_lam,
              attn_sink=m_attn_sink, w_o_rnn=m_w_o_rnn, w_o_attn=m_w_o_attn, w_out=m_w_out, w_ffn_in=m_w_ffn_in,
              w_ffn_out=m_w_ffn_out)
    Vo = dict(c_ctx=v_c_ctx, w_mod=v_w_mod, b_mod=v_b_mod, g_mix_pre=v_g_mix_pre, g_mix_post=v_g_mix_post,
              g_ffn_pre=v_g_ffn_pre, g_ffn_post=v_g_ffn_post, w_in=v_w_in, conv_w=v_conv_w, conv_b=v_conv_b,
              lru_wa=v_lru_wa, lru_ba=v_lru_ba, lru_wx=v_lru_wx, lru_bx=v_lru_bx, lru_lam=v_lru_lam,
              attn_sink=v_attn_sink, w_o_rnn=v_w_o_rnn, w_o_attn=v_w_o_attn, w_out=v_w_out, w_ffn_in=v_w_ffn_in,
              w_ffn_out=v_w_ffn_out)
    L = w_in.shape[0]
    S = x.shape[1]
    me = _lin(*_place())

    small = jnp.concatenate([c.reshape(8, 128), conv_w.reshape(L * CONV_W, 128), lru_ba.reshape(2 * L, 128),
                             lru_bx.reshape(2 * L, 128), lru_lam.reshape(2 * L, 128), jnp.zeros((4, 128), F32)], axis=0)
    small_all = _allgather_small("ag_small", small)
    c_all = small_all[:, 0:8].reshape(N_DEV, D)
    conv_w_f = _chan_full(small_all[:, 8:16]).reshape(L, CONV_W, D)
    lru_ba_f = _chan_full(small_all[:, 16:20]).reshape(L, 2, D)
    lru_bx_f = _chan_full(small_all[:, 20:24]).reshape(L, 2, D)
    lru_lam_f = _chan_full(small_all[:, 24:28]).reshape(L, 2, D)

    c9 = jnp.concatenate([c_all, c_ctx[None], jnp.zeros((MOD_ROWS - N_DEV - 1, D), F32)], axis=0)
    b_shard = lax.dynamic_slice_in_dim(b_mod, me * MOD_SHARD, MOD_SHARD, axis=1)[:, None, :]
    mod_part = _mod_fwd("mod_fwd", c9, w_mod, b_shard)
    mod_all = _allgather_small("ag_mod", mod_part.reshape(L * MOD_ROWS, MOD_SHARD))
    mod_all = jnp.transpose(mod_all.reshape(N_DEV, L, MOD_ROWS, MOD_SHARD), (1, 2, 0, 3)).reshape(L, MOD_ROWS, 6 * D)
    own_row = lax.dynamic_index_in_dim(mod_all, me, axis=1, keepdims=False)
    modrows = jnp.stack([mod_all[:, N_DEV], own_row], axis=1)

    shards = [{k: (P[src][l].T if tr else P[src][l]).astype(BF16) for k, src, tr in zip(BIG, BIG_SRC, BIG_T)}
              for l in range(L)]
    win0, = _allgather_hbm("ag_w_in0", [shards[0]["win_t"]])
    Ws = []
    for l in range(L):
        W = {"win_t": win0.reshape(-1, D)} if l == 0 else {}
        W.update(
            cw=conv_w_f[l], cb=conv_b[l][None],
            w4=jnp.concatenate([lru_wa[l, 0], lru_wa[l, 1], lru_wx[l, 0], lru_wx[l, 1]], axis=-1).astype(BF16),
            b4=jnp.concatenate([lru_ba_f[l, 0].reshape(N_RNN_BLOCKS, 1, RB), lru_ba_f[l, 1].reshape(N_RNN_BLOCKS, 1, RB),
                                lru_bx_f[l, 0].reshape(N_RNN_BLOCKS, 1, RB), lru_bx_f[l, 1].reshape(N_RNN_BLOCKS, 1, RB)],
                               axis=-1),
            lam=lru_lam_f[l], sink4=jnp.broadcast_to(attn_sink[l].reshape(N_KV, Q_PER_KV, 1), (N_KV, Q_PER_KV, HEAD)),
            g_mix_pre=g_mix_pre[l][None], g_mix_post=g_mix_post[l][None], g_ffn_pre=g_ffn_pre[l][None],
            g_ffn_post=g_ffn_post[l][None], mod=modrows[l])
        Ws.append(W)

    xa = jnp.concatenate([ctx[0], x[0]], axis=0)
    plan = _Plan(shards, Ws)
    sq, dxa, Gs = _local_step(xa, loss_target[0], Ws, S, plan)
    loss_part = ((0.5 / D) * jnp.sum(sq)).reshape(1, 1)
    grad_x = dxa[CTX:][None]

    dmod = jnp.concatenate([Gs[l]["mod"] for l in range(L)] + [jnp.zeros((8 - 2 * L, 6 * D), F32)], axis=0)
    dmod_all = _allgather_small("ag_dmod", dmod)
    dmod_cols = lax.dynamic_slice_in_dim(dmod_all, me * MOD_SHARD, MOD_SHARD, axis=2)
    g_w_mod, g_b_mod, dsc_part = _mod_bwd("mod_bwd", c9, w_mod, dmod_all, dmod_cols)
    g_b_mod = g_b_mod[:, 0]

    def rows(name, shape):
        return jnp.concatenate([Gs[l][name].reshape(shape) for l in range(L)], axis=0)

    b4g = [Gs[l]["b4"].reshape(N_RNN_BLOCKS, 4, RB) for l in range(L)]
    sink_row = jnp.concatenate([Gs[l]["sink4"][:, :, 0].reshape(1, N_Q) for l in range(L)]
                               + [loss_part, jnp.zeros((1, D - L * N_Q - 1), F32)], axis=1)
    small_g = jnp.concatenate(
        [rows("g_mix_pre", (1, D)), rows("g_mix_post", (1, D)), rows("g_ffn_pre", (1, D)), rows("g_ffn_post", (1, D)),
         rows("cb", (1, D)), rows("cw", (CONV_W, D))]
        + [b4g[l][:, d].reshape(1, D) for l in range(L) for d in range(2)]
        + [b4g[l][:, 2 + d].reshape(1, D) for l in range(L) for d in range(2)]
        + [rows("lam", (2, D)), sink_row, dsc_part], axis=0)
    n_small = small_g.shape[0]
    small_tot = _sum_blocks("sum_small", _allgather_small("ag_small_grads", small_g))
    o = 0
    G = {}
    for name in ("g_mix_pre", "g_mix_post", "g_ffn_pre", "g_ffn_post", "conv_b"):
        G[name] = small_tot[o:o + L]
        o += L
    G["conv_w"] = small_tot[o:o + L * CONV_W].reshape(L, CONV_W, D)
    o += L * CONV_W
    for name in ("lru_ba", "lru_bx", "lru_lam"):
        G[name] = small_tot[o:o + 2 * L].reshape(L, 2, D)
        o += 2 * L
    G["attn_sink"] = small_tot[o, :L * N_Q].reshape(L, N_Q)
    loss = small_tot[o, L * N_Q]
    sg = jax.nn.sigmoid(c_ctx)
    G["c_ctx"] = small_tot[o + 1] * (sg * (1.0 + c_ctx * (1.0 - sg)))
    G["b_mod"] = g_b_mod
    G["w_mod"] = g_w_mod

    last_slots, = _exchange_shards("exchange_w_in0", [[Gs[0]["win_t_b"].reshape(N_DEV, -1, D // 2)]], 1)
    plan.slots[0]["win_t_b"] = last_slots[0]
    for l in range(L):
        plan.slots[l]["win_t"] = [plan.slots[l]["win_t_a"], plan.slots[l]["win_t_b"]]

    out_g, out_d, out_m, out_v = {}, {}, {}, {}

    def put(name, res, shape=None):
        g, d, m, v = res
        for dst, val in ((out_g, g), (out_d, d), (out_m, m), (out_v, v)):
            dst[name] = val if shape is None else val.reshape(shape)

    for k, src, tr, tile in zip(BIG, BIG_SRC, BIG_T, BIG_TILE):
        lay = (lambda a: jnp.swapaxes(a, 1, 2)) if tr else (lambda a: a)
        wmv = (lay(P[src]), lay(Mo[src]), lay(Vo[src]))
        res = _adamw_slots("adamw_" + src, [plan.slots[l][k] for l in range(L)], wmv[0].shape, tile, wmv)
        put(src, [lay(r) for r in res])
    res = _adamw("adamw_w_mod", w_mod.reshape(L * D, MOD_SHARD), g_w_mod.reshape(L * D, MOD_SHARD),
                 m_w_mod.reshape(L * D, MOD_SHARD), v_w_mod.reshape(L * D, MOD_SHARD), 256)
    put("w_mod", (g_w_mod,) + tuple(res), w_mod.shape)
    def fuse4(wa, wx):
        return jnp.concatenate([wa[:, 0], wa[:, 1], wx[:, 0], wx[:, 1]], axis=-1).reshape(L, N_RNN_BLOCKS * RB, 4 * RB)

    res = _adamw_slots("adamw_gates", plan.gate_slots, (L, N_RNN_BLOCKS * RB, 4 * RB), 256,
                       (fuse4(lru_wa, lru_wx), fuse4(m_lru_wa, m_lru_wx), fuse4(v_lru_wa, v_lru_wx)))
    res = [r.reshape(L, N_RNN_BLOCKS, RB, 4, RB) for r in res]
    put("lru_wa", [jnp.stack([r[:, :, :, 0], r[:, :, :, 1]], axis=1) for r in res])
    put("lru_wx", [jnp.stack([r[:, :, :, 2], r[:, :, :, 3]], axis=1) for r in res])
    rep = ("g_mix_pre", "g_mix_post", "g_ffn_pre", "g_ffn_post", "conv_b", "b_mod")

    def pack_rep(T_):
        sink = jnp.concatenate([T_["attn_sink"].reshape(1, L * N_Q), jnp.zeros((1, D - L * N_Q), F32)], axis=1)
        return jnp.concatenate([T_[n].reshape(-1, D) for n in rep] + [sink, T_["c_ctx"][None]], axis=0)

    pk = [pack_rep(T_) for T_ in (P, G, Mo, Vo)]
    n_rep = pk[0].shape[0]
    res = _adamw("adamw_replicated", *[jnp.pad(a, ((0, 24 - n_rep), (0, 0))) for a in pk], 24)
    res = (pk[1],) + tuple(r[:n_rep] for r in res)
    o = 0
    for n in rep:
        k = P[n].size // D
        put(n, [r[o:o + k] for r in res], P[n].shape)
        o += k
    put("attn_sink", [r[o, :L * N_Q] for r in res], attn_sink.shape)
    put("c_ctx", [r[o + 1] for r in res], c_ctx.shape)
    chan = ("conv_w", "lru_ba", "lru_bx", "lru_lam")
    g_own = {n: lax.dynamic_slice_in_dim(G[n], me * RB, RB, axis=2) for n in chan}

    def pack_chan(T_):
        return jnp.concatenate([T_[n].reshape(-1, RB) for n in chan], axis=0)

    pk = [pack_chan(T_) for T_ in (P, g_own, Mo, Vo)]
    n_ch = pk[0].shape[0]
    res = _adamw("adamw_channels", *[jnp.pad(a, ((0, 24 - n_ch), (0, 0))) for a in pk], 24)
    res = (pk[1],) + tuple(r[:n_ch] for r in res)
    o = 0
    for n in chan:
        k = P[n].size // RB
        put(n, [r[o:o + k] for r in res], P[n].shape)
        o += k

    order = ("c_ctx", "w_mod", "b_mod", "g_mix_pre", "g_mix_post", "g_ffn_pre", "g_ffn_post", "w_in", "conv_w", "conv_b",
             "lru_wa", "lru_ba", "lru_wx", "lru_bx", "lru_lam", "attn_sink", "w_o_rnn", "w_o_attn", "w_out", "w_ffn_in",
             "w_ffn_out")
    return (loss, grad_x, *[out_g[n] for n in order], *[out_d[n] for n in order], *[out_m[n] for n in order],
            *[out_v[n] for n in order])
```

```python
import functools
import math

import numpy as np
import jax
import jax.numpy as jnp
from jax import lax
from jax.experimental import pallas as pl
from jax.experimental.pallas import tpu as pltpu

F32 = jnp.float32
BF16 = jnp.bfloat16

D = 1024
CTX = 256
TR = 256
HEAD = 128
N_Q = 8
N_KV = 2
Q_PER_KV = N_Q // N_KV
GRID_W = 64
N_FREQ = HEAD // 4
ROPE_BASE = 10000.0
N_RNN_BLOCKS = 8
CONV_W = 4
CONV_LEFT = 2
LRU_C = 8.0
D_FF = 2816
IN_W = 5632
P_W = IN_W
DP_W = 3584
COL_XR, COL_GR, COL_Q, COL_K, COL_V, COL_GL = 0, 1024, 2048, 3072, 3328, 3584
GLB = 512
EPS = 1e-6
NEG_INF = -1e30
ATT_SCALE = HEAD ** -0.5
N_DEV = 8
VMEM_LIMIT = 56 * 1024 * 1024

ADAM_LR, ADAM_B1, ADAM_B2, ADAM_EPS, ADAM_WD, ADAM_STEP = 0.001, 0.9, 0.999, 1e-08, 0.01, 10

NN = (((1,), (0,)), ((), ()))
NT = (((1,), (1,)), ((), ()))
TN = (((0,), (0,)), ((), ()))


def _dot(a, b, dims=NN):
    return lax.dot_general(a, b, dims, preferred_element_type=F32)


def _params(sem=("arbitrary",)):
    return pltpu.CompilerParams(dimension_semantics=sem, vmem_limit_bytes=VMEM_LIMIT)


def _full_spec(shape):
    nd = len(shape)
    return pl.BlockSpec(shape, lambda *_: (0,) * nd)


ANY = pl.BlockSpec(memory_space=pl.ANY)


def _ew(name, body, n, row_ins, pars, row_outs, accs=(), alias=None):
    n_ri, n_p, n_ro, n_acc = len(row_ins), len(pars), len(row_outs), len(accs)

    def kern(*refs):
        i = pl.program_id(0)
        ins = refs[:n_ri]
        ps = refs[n_ri:n_ri + n_p]
        outs = refs[n_ri + n_p:n_ri + n_p + n_ro]
        acc = refs[n_ri + n_p + n_ro:]
        if n_acc:
            @pl.when(i == 0)
            def _():
                for a in acc:
                    a[...] = jnp.zeros(a.shape, a.dtype)
        body(i, ins, ps, outs, acc)

    in_specs = [ANY if blk is None else pl.BlockSpec(blk, imap) for (_, blk, imap) in row_ins]
    in_specs += [_full_spec(p.shape) for p in pars]
    out_specs = [pl.BlockSpec(blk, imap) for (_, blk, imap) in row_outs] + [_full_spec(a.shape) for a in accs]
    out_shape = [s for (s, _, _) in row_outs] + list(accs)
    return pl.pallas_call(
        kern, name=name, grid=(n,), in_specs=in_specs, out_specs=out_specs, out_shape=out_shape,
        input_output_aliases=alias or {}, compiler_params=_params(),
    )(*[a for (a, _, _) in row_ins], *pars)


def _rowblk(width, colblk=0, roff=0, tile=TR):
    return (tile, width), (lambda i: (i + roff, colblk))


def _sds(shape, dtype):
    return jax.ShapeDtypeStruct(shape, dtype)


class _Carry:
    SAME_CORE = (1, 3, 5)

    def __init__(self, jobs):
        self.jobs = list(jobs)
        self.arrays = [a for _, a in self.jobs]
        self.out_shapes = [_sds(a.shape if kind == "scatter" else (N_DEV, *a.shape), a.dtype) for kind, a in self.jobs]
        n = len(self.jobs)
        self.scratch = [pltpu.SemaphoreType.DMA((n, 7)), pltpu.SemaphoreType.DMA((n, 7)), pltpu.SemaphoreType.DMA((n,))]

    def _setup(self, sems):
        send_sems, recv_sems, local_sems = sems
        x, y, c = _place()
        me = _lin(x, y, c)
        peers = [(x ^ ((k + 1) >> 2 & 1), y ^ ((k + 1) >> 1 & 1), c ^ ((k + 1) & 1)) for k in range(7)]

        def copy(a, k, sem_k, src, dst):
            return pltpu.make_async_remote_copy(src_ref=src, dst_ref=dst, send_sem=send_sems.at[a, sem_k],
                                                recv_sem=recv_sems.at[a, sem_k], device_id=peers[k], device_id_type=MESH)

        return me, [_lin(*p) for p in peers], copy, local_sems

    def _local(self, a, kind, ins, outs, me, local_sems):
        return pltpu.make_async_copy(ins[a].at[me] if kind == "scatter" else ins[a], outs[a].at[me], local_sems.at[a])

    def start(self, ins, outs, sems):
        me, theirs, copy, local_sems = self._setup(sems)
        for a, (kind, _) in enumerate(self.jobs):
            self._local(a, kind, ins, outs, me, local_sems).start()
            if kind == "scatter":
                for k in range(7):
                    copy(a, k, k, ins[a].at[theirs[k]], outs[a].at[me]).start()
            else:
                for k in (0,) + self.SAME_CORE:
                    copy(a, k, k, ins[a], outs[a].at[me]).start()

    def wait(self, ins, outs, sems):
        me, theirs, copy, local_sems = self._setup(sems)
        for a, (kind, _) in enumerate(self.jobs):
            if kind == "scatter":
                for k in range(7):
                    copy(a, k, k, ins[a].at[me], outs[a].at[theirs[k]]).wait_recv()
                for k in range(7):
                    copy(a, k, k, ins[a].at[theirs[k]], outs[a].at[me]).wait_send()
            else:
                for k in self.SAME_CORE:
                    blk = outs[a].at[theirs[k]]
                    copy(a, k, k, ins[a], blk).wait_recv()
                    copy(a, 0, k + 1, blk, blk).start()
                copy(a, 0, 0, ins[a], outs[a].at[theirs[0]]).wait_recv()
                for k in self.SAME_CORE:
                    copy(a, 0, k + 1, ins[a], outs[a].at[theirs[k + 1]]).wait_recv()
                for k in (0,) + self.SAME_CORE:
                    copy(a, k, k, ins[a], outs[a].at[me]).wait_send()
                for k in self.SAME_CORE:
                    blk = outs[a].at[theirs[k]]
                    copy(a, 0, k + 1, blk, blk).wait_send()
            self._local(a, kind, ins, outs, me, local_sems).wait()


def _carried(kern, carry, n_in, n_out, first, last):
    if carry is None:
        return kern
    nc = len(carry.jobs)

    def wrapped(*refs):
        ins, cin = refs[:n_in], refs[n_in:n_in + nc]
        outs, cout = refs[n_in + nc:n_in + nc + n_out], refs[n_in + nc + n_out:n_in + 2 * nc + n_out]
        scr, sems = refs[n_in + 2 * nc + n_out:-3], refs[-3:]

        @pl.when(first())
        def _():
            carry.start(cin, cout, sems)

        kern(*ins, *outs, *scr)

        @pl.when(last())
        def _():
            carry.wait(cin, cout, sems)

    return wrapped


def _carry_args(carry):
    if carry is None:
        return [], [], [], [], []
    n = len(carry.jobs)
    return [ANY] * n, carry.arrays, [ANY] * n, carry.out_shapes, carry.scratch


def _grid_ends(dims):
    first = lambda: functools.reduce(jnp.logical_and, [pl.program_id(d) == 0 for d in range(len(dims))])
    last = lambda: functools.reduce(jnp.logical_and, [pl.program_id(d) == n - 1 for d, n in enumerate(dims)])
    return first, last


def _mm_call(name, a, b, mode, out_dtype, tm, tn, rows_outer=True, single_b=False, carry=None):
    if mode == "TN":
        (K, M), N = a.shape, b.shape[1]
    else:
        (M, K), N = a.shape, (b.shape[1] if mode == "NN" else b.shape[0])
    assert M % tm == 0 and N % tn == 0, (name, M, N, K, tm, tn)
    ij = (lambda g0, g1: (g0, g1)) if rows_outer else (lambda g0, g1: (g1, g0))
    grid = (M // tm, N // tn) if rows_outer else (N // tn, M // tm)
    if mode == "TN":
        a_spec = pl.BlockSpec((K, tm), lambda g0, g1: (0, ij(g0, g1)[0]))
    else:
        a_spec = pl.BlockSpec((tm, K), lambda g0, g1: (ij(g0, g1)[0], 0))
    b_blk, b_map = ((tn, K), lambda g0, g1: (ij(g0, g1)[1], 0)) if mode == "NT" else \
                   ((K, tn), lambda g0, g1: (0, ij(g0, g1)[1]))
    b_spec = pl.BlockSpec(b_blk, b_map, pipeline_mode=pl.Buffered(1)) if single_b else pl.BlockSpec(b_blk, b_map)
    dims = {"NN": NN, "NT": NT, "TN": TN}[mode]

    def kern(a_ref, b_ref, o_ref):
        o_ref[...] = _dot(a_ref[...], b_ref[...], dims).astype(o_ref.dtype)

    ci, ca, co, cs, cscr = _carry_args(carry)
    res = pl.pallas_call(
        _carried(kern, carry, 2, 1, *_grid_ends(grid)), name=name, grid=grid, in_specs=[a_spec, b_spec] + ci,
        out_specs=[pl.BlockSpec((tm, tn), lambda g0, g1: ij(g0, g1))] + co,
        out_shape=[_sds((M, N), out_dtype)] + cs, scratch_shapes=cscr,
        compiler_params=_params(("arbitrary", "arbitrary")),
    )(a, b, *ca)
    return res[0] if carry is None else (res[0], res[1:])


def _mm_act(name, a, w, mode, out_dtype=BF16, carry=None):
    rows, K = a.shape
    N = w.shape[1] if mode == "NN" else w.shape[0]
    if K > D_FF:
        return _mm_call(name, a, w, mode, out_dtype, rows // 8, N, single_b=True, carry=carry)
    tn = N if N <= 1024 else 1408
    return _mm_call(name, a, w, mode, out_dtype, rows // 4, tn, carry=carry)


def _mm_wgrad(name, x, dy, out_dtype=BF16, carry=None):
    M = x.shape[1]
    tm = 1408 if M == D_FF else 512
    return _mm_call(name, x, dy, "TN", out_dtype, tm, dy.shape[1], single_b=True, carry=carry)


def _sigmoid(x):
    return 0.5 * jnp.tanh(0.5 * x) + 0.5


def _silu(x):
    return x * _sigmoid(x)


def _silu_grad(x):
    s = _sigmoid(x)
    return s * (1.0 + x * (1.0 - s))


_GELU_K = math.sqrt(2.0 / math.pi)


def _gelu(x):
    return 0.5 * x * (1.0 + jnp.tanh(_GELU_K * (x + 0.044715 * x * x * x)))


def _gelu_grad(x):
    t = jnp.tanh(_GELU_K * (x + 0.044715 * x * x * x))
    return 0.5 * (1.0 + t) + 0.5 * x * (1.0 - t * t) * _GELU_K * (1.0 + 3.0 * 0.044715 * x * x)


def _log_sigmoid(x):
    return jnp.minimum(x, 0.0) - jnp.log(1.0 + jnp.exp(-jnp.abs(x)))


def _rms(x):
    x = x.astype(F32)
    r = lax.rsqrt(jnp.mean(x * x, axis=-1, keepdims=True) + EPS)
    return x * r, r


def _rms_bwd(dy, y, r):
    return r * (dy - y * jnp.mean(dy * y, axis=-1, keepdims=True))


def _modrow(mod_ref, i, chunk):
    lo = mod_ref[0:1, chunk * D:(chunk + 1) * D]
    hi = mod_ref[1:2, chunk * D:(chunk + 1) * D]
    return jnp.where(i == 0, lo, hi)


def _acc_seg(acc_ref, i, val):
    zero = jnp.zeros_like(val)
    acc_ref[0:1, :] += jnp.where(i == 0, val, zero)
    acc_ref[1:2, :] += jnp.where(i == 0, zero, val)


def _colsum(x):
    return jnp.sum(x, axis=0, keepdims=True)


SH1, SC1, GA1, SH2, SC2, GA2 = range(6)


def _normmod_fwd(name, xa, g, mod, c_sh, c_sc):
    T = xa.shape[0]

    def body(i, ins, ps, outs, acc):
        y, _ = _rms(ins[0][...])
        h = (y * ps[0][...]) * (1.0 + _modrow(ps[1], i, c_sc)) + _modrow(ps[1], i, c_sh)
        outs[0][...] = h.astype(BF16)

    return _ew(name, body, T // TR, [(xa, *_rowblk(D))], [g, mod], [(_sds((T, D), BF16), *_rowblk(D))])[0]


def _modrows(mod_ref, row0, n, chunk):
    t = row0 + lax.broadcasted_iota(jnp.int32, (n, 1), 0)
    return jnp.where(t < CTX, mod_ref[0:1, chunk * D:(chunk + 1) * D], mod_ref[1:2, chunk * D:(chunk + 1) * D])


def _loss_resid_bwd(name, x_out, target, mat, gpost, mod, c_ga):
    T = x_out.shape[0]

    def body(i, ins, ps, outs, acc):
        err = ins[0][...] - ins[1][...]
        lat = i > 0
        dx = jnp.where(lat, err * (1.0 / D), 0.0)
        outs[0][...] = dx
        acc[2][...] += jnp.where(lat, _colsum(err * err), 0.0)
        outs[1][...] = _resid_bwd_vals(i, dx, ins[2][...], ps[0][...], ps[1], c_ga, acc[0], acc[1]).astype(BF16)

    tgt_blk = ((TR, D), lambda i: (jnp.maximum(i - 1, 0), 0))
    return _ew(name, body, T // TR, [(x_out, *_rowblk(D)), (target, *tgt_blk), (mat, *_rowblk(D))], [gpost, mod],
               [(_sds((T, D), F32), *_rowblk(D)), (_sds((T, D), BF16), *_rowblk(D))],
               [_sds((2, D), F32), _sds((1, D), F32), _sds((1, D), F32)])


def _mod_for(mod_ref, i, chunk, row0, n):
    return _modrow(mod_ref, i, chunk) if row0 is None else _modrows(mod_ref, row0, n, chunk)


def _acc_for(acc_ref, i, v, row0):
    if row0 is None:
        _acc_seg(acc_ref, i, _colsum(v))
        return

    @pl.when(row0 < CTX)
    def _():
        is_ctx = row0 + lax.broadcasted_iota(jnp.int32, (v.shape[0], 1), 0) < CTX
        acc_ref[0:1, :] += _colsum(jnp.where(is_ctx, v, 0.0))
        acc_ref[1:2, :] += _colsum(jnp.where(is_ctx, 0.0, v))

    @pl.when(row0 >= CTX)
    def _():
        acc_ref[1:2, :] += _colsum(v)


def _resid_bwd_vals(i, dout, mat, gpost, mod_ref, c_ga, acc_ga, acc_g, row0=None):
    ym, rm = _rms(mat)
    ga = _mod_for(mod_ref, i, c_ga, row0, dout.shape[0])
    _acc_for(acc_ga, i, dout * (ym * gpost), row0)
    dn = dout * ga
    acc_g[...] += _colsum(dn * ym)
    return _rms_bwd(dn * gpost, ym, rm)


def _normmod_bwd_vals(i, dh, xin, g, mod_ref, c_sh, c_sc, acc_sh, acc_sc, acc_g, row0=None):
    dh = dh.astype(F32)
    y, r = _rms(xin)
    _acc_for(acc_sc, i, dh * (y * g), row0)
    _acc_for(acc_sh, i, dh, row0)
    dyg = dh * (1.0 + _mod_for(mod_ref, i, c_sc, row0, dh.shape[0]))
    acc_g[...] += _colsum(dyg * y)
    return _rms_bwd(dyg * g, y, r)


def _parts(i, tm):
    return [(slice(0, tm), i * tm)]


FT = 1408


def _ffn_in_fused(name, h2, w_t, carry=None):
    T = h2.shape[0]
    tm, nj = T // 4, D_FF // FT

    def kern(a_ref, bg_ref, bu_ref, fg_ref, fu_ref, s_ref):
        for rows, _ in _parts(0, tm):
            a = a_ref[rows, :]
            g = _dot(a, bg_ref[...], NT)
            u = _dot(a, bu_ref[...], NT)
            fg_ref[rows, :] = g.astype(BF16)
            fu_ref[rows, :] = u.astype(BF16)
            s_ref[rows, :] = (_silu(g) * u).astype(BF16)

    o_spec = pl.BlockSpec((tm, FT), lambda i, j: (i, j))
    ci, ca, co, cs, cscr = _carry_args(carry)
    res = pl.pallas_call(
        _carried(kern, carry, 3, 3, *_grid_ends((4, nj))), name=name, grid=(4, nj),
        in_specs=[pl.BlockSpec((tm, D), lambda i, j: (i, 0)), pl.BlockSpec((FT, D), lambda i, j: (j, 0)),
                  pl.BlockSpec((FT, D), lambda i, j: (j + nj, 0))] + ci,
        out_specs=[o_spec] * 3 + co, out_shape=[_sds((T, D_FF), BF16)] * 3 + cs, scratch_shapes=cscr,
        compiler_params=_params(("arbitrary", "arbitrary")),
    )(h2, w_t, w_t, *ca)
    return res if carry is None else (res[:3], res[3:])


def _norm_chain(row0, xin, mat, gpost, mod_ref, c_ga, gnext, modn_ref, c_sh, c_sc):
    n = xin.shape[0]
    ym, _ = _rms(mat.astype(BF16))
    xo = xin + _modrows(mod_ref, row0, n, c_ga) * (ym * gpost)
    y, _ = _rms(xo)
    h = (y * gnext) * (1.0 + _modrows(modn_ref, row0, n, c_sc)) + _modrows(modn_ref, row0, n, c_sh)
    return xo, h.astype(BF16)


def _out_fused(name, p, u, o_all, xa, w_o_rnn, w_o_attn, w_out, gpost, mod, gnext):
    T = u.shape[0]
    tm = T // 8

    def kern(g0, g1, g2, g3, u_ref, o_ref, xa_ref, wr_ref, wa_ref, w_ref, gpost_ref, mod_ref, gnext_ref,
             ya_ref, yb_ref, z_ref, m_ref, x1_ref, h2_ref):
        for rows, row0 in _parts(pl.program_id(0), tm):
            ya = _dot(u_ref[rows, :], wr_ref[...]).astype(BF16)
            yb = _dot(o_ref[rows, :], wa_ref[...]).astype(BF16)
            ya_ref[rows, :] = ya
            yb_ref[rows, :] = yb
            ga = _sigmoid(jnp.concatenate([g0[rows, :], g1[rows, :]], axis=1).astype(F32))
            gb = _sigmoid(jnp.concatenate([g2[rows, :], g3[rows, :]], axis=1).astype(F32))
            z = (ga * ya.astype(F32) + gb * yb.astype(F32)).astype(BF16)
            z_ref[rows, :] = z
            m = _dot(z, w_ref[...])
            m_ref[rows, :] = m.astype(BF16)
            x1_ref[rows, :], h2_ref[rows, :] = _norm_chain(row0, xa_ref[rows, :], m, gpost_ref[...], mod_ref, GA1,
                                                           gnext_ref[...], mod_ref, SH2, SC2)

    row = lambda w: pl.BlockSpec((tm, w), lambda i: (i, 0))
    return pl.pallas_call(
        kern, name=name, grid=(T // tm,),
        in_specs=[pl.BlockSpec((tm, GLB), lambda i, q=q: (i, COL_GL // GLB + q)) for q in range(4)]
                 + [row(D), row(D), row(D)] + [_full_spec(a.shape) for a in (w_o_rnn, w_o_attn, w_out, gpost, mod, gnext)],
        out_specs=[row(D)] * 6,
        out_shape=[_sds((T, D), BF16)] * 4 + [_sds((T, D), F32), _sds((T, D), BF16)],
        compiler_params=_params(),
    )(p, p, p, p, u, o_all, xa, w_o_rnn, w_o_attn, w_out, gpost, mod, gnext)


def _ffn_out_fused(name, s, w, x1, gpost, mod, nxt=None):
    T = s.shape[0]
    tm = T // 8

    def kern(s_ref, w_ref, x1_ref, gpost_ref, mod_ref, *rest):
        for rows, row0 in _parts(pl.program_id(0), tm):
            e = _dot(s_ref[rows, :], w_ref[...])
            if nxt is None:
                e_ref, xo_ref = rest
                ym, _ = _rms(e.astype(BF16))
                xo_ref[rows, :] = x1_ref[rows, :] + _modrows(mod_ref, row0, e.shape[0], GA2) * (ym * gpost_ref[...])
            else:
                gnext_ref, modn_ref, e_ref, xo_ref, h_ref = rest
                xo_ref[rows, :], h_ref[rows, :] = _norm_chain(row0, x1_ref[rows, :], e, gpost_ref[...], mod_ref, GA2,
                                                              gnext_ref[...], modn_ref, SH1, SC1)
            e_ref[rows, :] = e.astype(BF16)

    row = lambda w_: pl.BlockSpec((tm, w_), lambda i: (i, 0))
    extra = [] if nxt is None else list(nxt)
    return pl.pallas_call(
        kern, name=name, grid=(T // tm,),
        in_specs=[row(D_FF), _full_spec(w.shape), row(D), _full_spec(gpost.shape), _full_spec(mod.shape)]
                 + [_full_spec(a.shape) for a in extra],
        out_specs=[row(D)] * (2 if nxt is None else 3),
        out_shape=[_sds((T, D), BF16), _sds((T, D), F32)] + ([] if nxt is None else [_sds((T, D), BF16)]),
        compiler_params=_params(),
    )(s, w, x1, gpost, mod, *extra)


def _ffn_bwd_fused(name, fg, fu, w, de=None, head=None):
    T = fg.shape[0]
    tm = T // 8
    row = lambda w_: pl.BlockSpec((tm, w_), lambda i: (i, 0))
    w_spec = pl.BlockSpec(w.shape, lambda i: (0, 0), pipeline_mode=pl.Buffered(1))

    def tail(rows, de_v, fg_ref, fu_ref, w_ref, df_ref):
        ds = _dot(de_v, w_ref[...], NT)
        g, u = fg_ref[rows, :].astype(F32), fu_ref[rows, :].astype(F32)
        df_ref[rows, :] = jnp.concatenate([ds * u * _silu_grad(g), ds * _silu(g)], axis=1).astype(BF16)

    if head is None:
        def kern(de_ref, fg_ref, fu_ref, w_ref, df_ref):
            for rows, _ in _parts(pl.program_id(0), tm):
                tail(rows, de_ref[rows, :], fg_ref, fu_ref, w_ref, df_ref)

        return pl.pallas_call(
            kern, name=name, grid=(T // tm,), in_specs=[row(D), row(D_FF), row(D_FF), w_spec],
            out_specs=[row(2 * D_FF)], out_shape=[_sds((T, 2 * D_FF), BF16)], compiler_params=_params(),
        )(de, fg, fu, w)

    dx2, e, gpost, mod = head

    def kern(dx_ref, e_ref, fg_ref, fu_ref, w_ref, gpost_ref, mod_ref, de_ref, df_ref, dga_ref, dg_ref):
        i = pl.program_id(0)

        @pl.when(i == 0)
        def _():
            dga_ref[...] = jnp.zeros(dga_ref.shape, F32)
            dg_ref[...] = jnp.zeros(dg_ref.shape, F32)

        for rows, row0 in _parts(i, tm):
            de_v = _resid_bwd_vals(i, dx_ref[rows, :], e_ref[rows, :], gpost_ref[...], mod_ref, GA2, dga_ref, dg_ref,
                                   row0=row0).astype(BF16)
            de_ref[rows, :] = de_v
            tail(rows, de_v, fg_ref, fu_ref, w_ref, df_ref)

    return pl.pallas_call(
        kern, name=name, grid=(T // tm,),
        in_specs=[row(D), row(D), row(D_FF), row(D_FF), w_spec, _full_spec(gpost.shape), _full_spec(mod.shape)],
        out_specs=[row(D), row(2 * D_FF), _full_spec((2, D)), _full_spec((1, D))],
        out_shape=[_sds((T, D), BF16), _sds((T, 2 * D_FF), BF16), _sds((2, D), F32), _sds((1, D), F32)],
        compiler_params=_params(),
    )(dx2, e, fg, fu, w, gpost, mod)


def _zero_at_start(i, refs):
    @pl.when(i == 0)
    def _():
        for r in refs:
            r[...] = jnp.zeros(r.shape, F32)


def _proj_bwd_fused(name, dp, dgl, w_in_t, xa, dx1, gpre, mod, carry=None):
    T = dp.shape[0]
    tm = T // 8
    row = lambda w_: pl.BlockSpec((tm, w_), lambda i: (i, 0))

    def kern(dp_ref, dgl_ref, w_ref, xa_ref, dx1_ref, g_ref, mod_ref, dxa_ref, dsh_ref, dsc_ref, dg_ref):
        i = pl.program_id(0)
        _zero_at_start(i, (dsh_ref, dsc_ref, dg_ref))
        for rows, row0 in _parts(i, tm):
            dh = _dot(dp_ref[rows, :], w_ref[0:DP_W, :]) + _dot(dgl_ref[rows, :], w_ref[DP_W:, :])
            dxa_ref[rows, :] = dx1_ref[rows, :] + _normmod_bwd_vals(i, dh, xa_ref[rows, :], g_ref[...], mod_ref, SH1,
                                                                    SC1, dsh_ref, dsc_ref, dg_ref, row0=row0)

    ci, ca, co, cs, cscr = _carry_args(carry)
    res = pl.pallas_call(
        _carried(kern, carry, 7, 4, *_grid_ends((T // tm,))), name=name, grid=(T // tm,),
        in_specs=[row(DP_W), row(P_W - DP_W),
                  pl.BlockSpec(w_in_t.shape, lambda i: (0, 0), pipeline_mode=pl.Buffered(1)), row(D), row(D),
                  _full_spec(gpre.shape), _full_spec(mod.shape)] + ci,
        out_specs=[row(D), _full_spec((2, D)), _full_spec((2, D)), _full_spec((1, D))] + co,
        out_shape=[_sds((T, D), F32), _sds((2, D), F32), _sds((2, D), F32), _sds((1, D), F32)] + cs,
        scratch_shapes=cscr, compiler_params=_params(),
    )(dp, dgl, w_in_t, xa, dx1, gpre, mod, *ca)
    return res if carry is None else (res[:4], res[4:])


def _proj_wgrad(name, dp, dgl, h, carry=None):
    T, N = h.shape
    n1, n2 = DP_W // GLB, (P_W - DP_W) // GLB

    def kern(a1_ref, a2_ref, h_ref, o_ref):
        i = pl.program_id(0)

        @pl.when(i < n1)
        def _():
            o_ref[...] = _dot(a1_ref[...], h_ref[...], TN).astype(o_ref.dtype)

        @pl.when(i >= n1)
        def _():
            o_ref[...] = _dot(a2_ref[...], h_ref[...], TN).astype(o_ref.dtype)

    ci, ca, co, cs, cscr = _carry_args(carry)
    res = pl.pallas_call(
        _carried(kern, carry, 3, 1, *_grid_ends((n1 + n2,))), name=name, grid=(n1 + n2,),
        in_specs=[pl.BlockSpec((T, GLB), lambda i: (0, jnp.minimum(i, n1 - 1))),
                  pl.BlockSpec((T, GLB), lambda i: (0, jnp.maximum(i - n1, 0))),
                  pl.BlockSpec((T, N), lambda i: (0, 0), pipeline_mode=pl.Buffered(1))] + ci,
        out_specs=[pl.BlockSpec((GLB, N), lambda i: (i, 0))] + co,
        out_shape=[_sds((P_W, N), BF16)] + cs, scratch_shapes=cscr, compiler_params=_params(),
    )(dp, dgl, h, *ca)
    return res[0] if carry is None else (res[0], res[1:])


def _ffn_in_bwd_fused(name, df, w_t, x1, dres, mat, gpre, mod, gpost, carry=None):
    T = df.shape[0]
    tm = T // 8
    row = lambda w_: pl.BlockSpec((tm, w_), lambda i: (i, 0))

    def kern(df_ref, w_ref, x1_ref, dres_ref, mat_ref, gpre_ref, mod_ref, gpost_ref,
             dx1_ref, dm_ref, dsh_ref, dsc_ref, dgpre_ref, dga_ref, dgpost_ref):
        i = pl.program_id(0)
        _zero_at_start(i, (dsh_ref, dsc_ref, dgpre_ref, dga_ref, dgpost_ref))
        for rows, row0 in _parts(i, tm):
            dh2 = _dot(df_ref[rows, :], w_ref[...])
            dx1 = dres_ref[rows, :] + _normmod_bwd_vals(i, dh2, x1_ref[rows, :], gpre_ref[...], mod_ref, SH2, SC2,
                                                        dsh_ref, dsc_ref, dgpre_ref, row0=row0)
            dx1_ref[rows, :] = dx1
            dm_ref[rows, :] = _resid_bwd_vals(i, dx1, mat_ref[rows, :], gpost_ref[...], mod_ref, GA1, dga_ref,
                                              dgpost_ref, row0=row0).astype(BF16)

    ci, ca, co, cs, cscr = _carry_args(carry)
    res = pl.pallas_call(
        _carried(kern, carry, 8, 7, *_grid_ends((T // tm,))), name=name, grid=(T // tm,),
        in_specs=[row(2 * D_FF), pl.BlockSpec(w_t.shape, lambda i: (0, 0), pipeline_mode=pl.Buffered(1)), row(D),
                  row(D), row(D), _full_spec(gpre.shape), _full_spec(mod.shape), _full_spec(gpost.shape)] + ci,
        out_specs=[row(D), row(D), _full_spec((2, D)), _full_spec((2, D)), _full_spec((1, D)), _full_spec((2, D)),
                   _full_spec((1, D))] + co,
        out_shape=[_sds((T, D), F32), _sds((T, D), BF16), _sds((2, D), F32), _sds((2, D), F32), _sds((1, D), F32),
                   _sds((2, D), F32), _sds((1, D), F32)] + cs,
        scratch_shapes=cscr, compiler_params=_params(),
    )(df, w_t, x1, dres, mat, gpre, mod, gpost, *ca)
    return res if carry is None else (res[:7], res[7:])


def _out_bwd_fused(name, dm, w_out, w_o_rnn, w_o_attn, p, ya, yb):
    T = dm.shape[0]
    tm = T // 8
    row = lambda w_: pl.BlockSpec((tm, w_), lambda i: (i, 0))

    def kern(dm_ref, w_ref, wr_ref, wa_ref, g0, g1, g2, g3, ya_ref, yb_ref, dya_ref, dyb_ref, dgl_ref, du_ref, do_ref):
        for rows, _ in _parts(pl.program_id(0), tm):
            dz = _dot(dm_ref[rows, :], w_ref[...], NT)
            ga = _sigmoid(jnp.concatenate([g0[rows, :], g1[rows, :]], axis=1).astype(F32))
            gb = _sigmoid(jnp.concatenate([g2[rows, :], g3[rows, :]], axis=1).astype(F32))
            dya = (dz * ga).astype(BF16)
            dyb = (dz * gb).astype(BF16)
            dya_ref[rows, :] = dya
            dyb_ref[rows, :] = dyb
            dgl_ref[rows, :] = jnp.concatenate([dz * ya_ref[rows, :].astype(F32) * ga * (1.0 - ga),
                                                dz * yb_ref[rows, :].astype(F32) * gb * (1.0 - gb)],
                                               axis=1).astype(BF16)
            du_ref[rows, :] = _dot(dya, wr_ref[...], NT).astype(BF16)
            do_ref[rows, :] = _dot(dyb, wa_ref[...], NT).astype(BF16)

    return pl.pallas_call(
        kern, name=name, grid=(T // tm,),
        in_specs=[row(D)] + [_full_spec(w.shape) for w in (w_out, w_o_rnn, w_o_attn)]
                 + [pl.BlockSpec((tm, GLB), lambda i, q=q: (i, COL_GL // GLB + q)) for q in range(4)] + [row(D), row(D)],
        out_specs=[row(D), row(D), row(2 * D), row(D), row(D)],
        out_shape=[_sds((T, D), BF16), _sds((T, D), BF16), _sds((T, 2 * D), BF16), _sds((T, D), BF16),
                   _sds((T, D), BF16)],
        compiler_params=_params(),
    )(dm, w_out, w_o_rnn, w_o_attn, p, p, p, p, ya, yb)


AB = 128
CTX_BLKS = CTX // AB


def _rope_tables(S):
    pos = jnp.arange(S, dtype=jnp.int32)
    inv = ROPE_BASE ** (-jnp.arange(N_FREQ, dtype=F32) / N_FREQ)
    ang_r = (pos // GRID_W).astype(F32)[:, None] * inv[None, :]
    ang_c = (pos % GRID_W).astype(F32)[:, None] * inv[None, :]
    cos = jnp.concatenate([jnp.cos(ang_r)] * 2 + [jnp.cos(ang_c)] * 2, axis=1)
    sin = jnp.concatenate([-jnp.sin(ang_r), jnp.sin(ang_r), -jnp.sin(ang_c), jnp.sin(ang_c)], axis=1)
    return cos, sin


def _rope(x, cos, sin):
    w = x.shape[1]
    reps = w // HEAD
    lane = lax.broadcasted_iota(jnp.int32, x.shape, 1)
    partner = jnp.where((lane & 63) < 32, pltpu.roll(x, w - 32, 1), pltpu.roll(x, 32, 1))
    return x * jnp.tile(cos, (1, reps)) + partner * jnp.tile(sin, (1, reps))


def _unrope(dx, cos, sin):
    w = dx.shape[1]
    reps = w // HEAD
    lane = lax.broadcasted_iota(jnp.int32, dx.shape, 1)
    t = dx * jnp.tile(sin, (1, reps))
    partner = jnp.where((lane & 63) < 32, pltpu.roll(t, w - 32, 1), pltpu.roll(t, 32, 1))
    return dx * jnp.tile(cos, (1, reps)) + partner


def _qkv_prep(name, p, cos, sin, S):
    T = CTX + S
    nt = T // AB
    KW = N_KV * HEAD

    def with_ones(v):
        ones = jnp.ones((AB, HEAD), BF16)
        return jnp.concatenate([v[:, kh * HEAD:(kh + 1) * HEAD] if part == 0 else ones
                                for kh in range(N_KV) for part in range(2)], axis=1)

    def kern(q_ref, k_ref, v_ref, cos_ref, sin_ref, qa_ref, kp_ref, vp_ref, kc_ref, vc_ref):
        i = pl.program_id(0)
        cos_v, sin_v = cos_ref[...], sin_ref[...]
        @pl.when(i < CTX_BLKS)
        def _():
            qa_ref[...] = (q_ref[...].astype(F32) * ATT_SCALE).astype(BF16)
            kc_ref[...] = k_ref[...]
            vc_ref[...] = with_ones(v_ref[...])

        @pl.when((i < CTX_BLKS) | (i >= nt))
        def _():
            kp_ref[...] = jnp.zeros(kp_ref.shape, BF16)
            vp_ref[...] = jnp.zeros(vp_ref.shape, BF16)

        @pl.when((i >= CTX_BLKS) & (i < nt))
        def _():
            qa_ref[...] = (_rope(q_ref[...].astype(F32), cos_v, sin_v) * ATT_SCALE).astype(BF16)
            kp_ref[...] = _rope(k_ref[...].astype(F32), cos_v, sin_v).astype(BF16)
            vp_ref[...] = with_ones(v_ref[...])

    tok = lambda i: jnp.minimum(i, nt - 1)
    lat_map = lambda i: (jnp.clip(i - CTX_BLKS, 0, nt - CTX_BLKS - 1), 0)
    ctx_map = lambda i: (jnp.minimum(i, CTX_BLKS - 1), 0)
    return pl.pallas_call(
        kern, name=name, grid=(nt + CTX_BLKS,),
        in_specs=[pl.BlockSpec((AB, N_Q * HEAD), lambda i: (tok(i), COL_Q // (N_Q * HEAD))),
                  pl.BlockSpec((AB, KW), lambda i: (tok(i), COL_K // KW)),
                  pl.BlockSpec((AB, KW), lambda i: (tok(i), COL_V // KW)),
                  pl.BlockSpec((AB, HEAD), lat_map), pl.BlockSpec((AB, HEAD), lat_map)],
        out_specs=[pl.BlockSpec((AB, N_Q * HEAD), lambda i: (tok(i), 0)),
                   pl.BlockSpec((AB, KW), lambda i: (i, 0)), pl.BlockSpec((AB, 2 * KW), lambda i: (i, 0)),
                   pl.BlockSpec((AB, KW), ctx_map), pl.BlockSpec((AB, 2 * KW), ctx_map)],
        out_shape=[_sds((T, N_Q * HEAD), BF16), _sds((S + 2 * CTX, KW), BF16), _sds((S + 2 * CTX, 2 * KW), BF16),
                   _sds((CTX, KW), BF16), _sds((CTX, 2 * KW), BF16)],
        compiler_params=_params(),
    )(p, p, p, cos, sin)


GW = Q_PER_KV * HEAD
HG = Q_PER_KV


def _band_bias(S):
    r = jnp.arange(AB, dtype=jnp.int32)[:, None]
    c = jnp.arange(3 * AB, dtype=jnp.int32)[None, :]
    near = jnp.abs(c - AB - r) <= AB
    valid = jnp.stack([near & (c >= AB), near, near & (c < 2 * AB)])
    return jnp.where(valid, 0.0, NEG_INF).astype(F32)


def _bias_spec(S):
    nb = S // AB
    return pl.BlockSpec((None, AB, 3 * AB), lambda kh, n: (jnp.where(n == 0, 0, jnp.where(n == nb - 1, 2, 1)), 0, 0))


def _head_probs(q, sink, kc, vce, kb, vbe, bias):
    s_c = _dot(q, kc, NT)
    m = jnp.maximum(jnp.max(s_c, axis=-1, keepdims=True), sink)
    if kb is not None:
        s_b = _dot(q, kb, NT) + bias
        m = jnp.maximum(m, jnp.max(s_b, axis=-1, keepdims=True))
    p_c = jnp.exp(s_c - m).astype(BF16)
    acc = _dot(p_c, vce)
    p_b = None
    if kb is not None:
        p_b = jnp.exp(s_b - m).astype(BF16)
        acc = acc + _dot(p_b, vbe)
    return p_c, p_b, m, acc


def _attn_fwd(name, qa, kc, vc, sink4, S, band=None, prev=None, carry=None):
    T = qa.shape[0]
    has_band = band is not None
    nq = S // AB if has_band else CTX_BLKS
    q_off = CTX_BLKS if has_band else 0

    def kern(*refs):
        q_ref, kc_ref, vc_ref, sink_ref = refs[:4]
        rest = refs[4:]
        o_ref = rest[-1]
        n = pl.program_id(1)
        kc_v, vce = kc_ref[...], vc_ref[...]
        kb = vbe = bias = None
        if has_band:
            kp_ref, vp_ref, bias_ref = rest[:3]
            start = pl.multiple_of(n * AB + (CTX - AB), AB)
            kb = kp_ref[pl.ds(start, 3 * AB), :]
            vbe = vp_ref[pl.ds(start, 3 * AB), :]
            bias = bias_ref[...]
        outs = []
        for g in range(Q_PER_KV):
            sink = sink_ref[g:g + 1, 0:1]
            _, _, m, acc = _head_probs(q_ref[:, g * HEAD:(g + 1) * HEAD], sink, kc_v, vce, kb, vbe, bias)
            l = acc[:, HEAD:] + jnp.exp(sink - m)
            outs.append(acc[:, :HEAD] / l)
        o_ref[...] = jnp.concatenate(outs, axis=1).astype(BF16)

    in_specs = [pl.BlockSpec((AB, GW), lambda kh, n: (n + q_off, kh)),
                pl.BlockSpec((CTX, HEAD), lambda kh, n: (0, kh)), pl.BlockSpec((CTX, 2 * HEAD), lambda kh, n: (0, kh)),
                pl.BlockSpec((None, Q_PER_KV, HEAD), lambda kh, n: (kh, 0, 0))]
    args = [qa, kc, vc, sink4]
    if has_band:
        in_specs += [pl.BlockSpec((S + 2 * CTX, HEAD), lambda kh, n: (0, kh)),
                     pl.BlockSpec((S + 2 * CTX, 2 * HEAD), lambda kh, n: (0, kh)), _bias_spec(S)]
        args += list(band)
    alias = {}
    if prev is not None:
        in_specs.append(ANY)
        alias = {len(args): 0}
        args.append(prev)
    ci, ca, co, cs, cscr = _carry_args(carry)
    res = pl.pallas_call(
        _carried(kern, carry, len(args), 1, *_grid_ends((N_KV, nq))), name=name, grid=(N_KV, nq),
        in_specs=in_specs + ci,
        out_specs=[pl.BlockSpec((AB, GW), lambda kh, n: (n + q_off, kh))] + co,
        out_shape=[_sds((T, N_Q * HEAD), BF16)] + cs, input_output_aliases=alias, scratch_shapes=cscr,
        compiler_params=_params(("arbitrary", "arbitrary")),
    )(*args, *ca)
    return res[0] if carry is None else (res[0], res[1:])


def _attn_bwd(name, qa, kc, vc, sink4, o_all, do_all, S, band=None, prev_dq=None, carry=None):
    T = qa.shape[0]
    has_band = band is not None
    nq = S // AB if has_band else CTX_BLKS
    q_off = CTX_BLKS if has_band else 0
    KW = N_KV * HEAD

    def kern(*refs):
        q_ref, kc_ref, vc_ref, sink_ref, o_ref, do_ref = refs[:6]
        rest = refs[6:]
        if has_band:
            kp_ref, vp_ref, bias_ref = rest[:3]
            rest = rest[3:]
        if prev_dq is not None:
            rest = rest[1:]
        dq_ref, dkc_ref, dvc_ref, dsink_ref = rest[:4]
        n = pl.program_id(1)

        @pl.when(n == 0)
        def _():
            dkc_ref[...] = jnp.zeros(dkc_ref.shape, F32)
            dvc_ref[...] = jnp.zeros(dvc_ref.shape, F32)
            dsink_ref[...] = jnp.zeros(dsink_ref.shape, F32)
            if has_band:
                rest[4][...] = jnp.zeros(rest[4].shape, F32)
                rest[5][...] = jnp.zeros(rest[5].shape, F32)

        kc_v, vce = kc_ref[...], vc_ref[...]
        vc_v = vce[:, :HEAD]
        kb = vbe = vb = bias = None
        if has_band:
            start = pl.multiple_of(n * AB + (CTX - AB), AB)
            kb = kp_ref[pl.ds(start, 3 * AB), :]
            vbe = vp_ref[pl.ds(start, 3 * AB), :]
            vb = vbe[:, :HEAD]
            bias = bias_ref[...]
        dq_parts, dsink_parts = [], []
        for g0 in range(0, Q_PER_KV, HG):
            heads = range(g0, g0 + HG)
            stack = lambda ref: jnp.concatenate([ref[:, g * HEAD:(g + 1) * HEAD] for g in heads], axis=0)
            q4, do4 = stack(q_ref), stack(do_ref)
            sink = jnp.concatenate([jnp.broadcast_to(sink_ref[g:g + 1, 0:1], (AB, 1)) for g in heads], axis=0)
            s_c = _dot(q4, kc_v, NT)
            m = jnp.maximum(jnp.max(s_c, axis=-1, keepdims=True), sink)
            if has_band:
                s_b = _dot(q4, kb, NT) + jnp.tile(bias, (HG, 1))
                m = jnp.maximum(m, jnp.max(s_b, axis=-1, keepdims=True))
            p_c = jnp.exp(s_c - m).astype(BF16).astype(F32)
            p_sink = jnp.exp(sink - m)
            l = jnp.sum(p_c, axis=-1, keepdims=True) + p_sink
            if has_band:
                p_b = jnp.exp(s_b - m).astype(BF16).astype(F32)
                l = l + jnp.sum(p_b, axis=-1, keepdims=True)
            inv = 1.0 / l
            delta = jnp.sum(do4.astype(F32) * stack(o_ref).astype(F32), axis=-1, keepdims=True)
            do4b = do4.astype(BF16)
            pn_c = (p_c * inv).astype(BF16)
            ds_c = (p_c * inv * (_dot(do4b, vc_v, NT) - delta)).astype(BF16)
            dq4 = _dot(ds_c, kc_v)
            dkc_ref[...] += _dot(q4, ds_c, TN)
            dvc_ref[...] += _dot(do4b, pn_c, TN)
            if has_band:
                pn_b = (p_b * inv).astype(BF16)
                ds_b = (p_b * inv * (_dot(do4b, vb, NT) - delta)).astype(BF16)
                dq4 = dq4 + _dot(ds_b, kb)
                rest[4][:, pl.ds(start, 3 * AB)] += _dot(q4, ds_b, TN)
                rest[5][:, pl.ds(start, 3 * AB)] += _dot(do4b, pn_b, TN)
            dq4 = dq4 * ATT_SCALE
            dq_parts += [dq4[k * AB:(k + 1) * AB, :] for k in range(HG)]
            ps = p_sink * inv * delta
            dsink_parts += [jnp.broadcast_to(-jnp.sum(ps[k * AB:(k + 1) * AB, :], axis=0, keepdims=True), (1, HEAD))
                            for k in range(HG)]
        dq_ref[...] = jnp.concatenate(dq_parts, axis=1)
        dsink_ref[...] += jnp.concatenate(dsink_parts, axis=0)

    q_spec = pl.BlockSpec((AB, GW), lambda kh, n: (n + q_off, kh))
    c_spec = pl.BlockSpec((CTX, HEAD), lambda kh, n: (0, kh))
    ce_spec = pl.BlockSpec((CTX, 2 * HEAD), lambda kh, n: (0, kh))
    s_spec = pl.BlockSpec((None, Q_PER_KV, HEAD), lambda kh, n: (kh, 0, 0))
    in_specs = [q_spec, c_spec, ce_spec, s_spec, q_spec, q_spec]
    args = [qa, kc, vc, sink4, o_all, do_all]
    ct_spec = pl.BlockSpec((HEAD, CTX), lambda kh, n: (kh, 0))
    out_specs = [q_spec, ct_spec, ct_spec, s_spec]
    out_shape = [_sds((T, N_Q * HEAD), F32), _sds((KW, CTX), F32), _sds((KW, CTX), F32), _sds((N_KV, Q_PER_KV, HEAD), F32)]
    if has_band:
        p_spec = pl.BlockSpec((S + 2 * CTX, HEAD), lambda kh, n: (0, kh))
        pt_spec = pl.BlockSpec((HEAD, S + 2 * CTX), lambda kh, n: (kh, 0))
        in_specs += [p_spec, pl.BlockSpec((S + 2 * CTX, 2 * HEAD), lambda kh, n: (0, kh)), _bias_spec(S)]
        args += list(band)
        out_specs += [pt_spec, pt_spec]
        out_shape += [_sds((KW, S + 2 * CTX), F32)] * 2
    alias = {}
    if prev_dq is not None:
        in_specs.append(ANY)
        alias = {len(args): 0}
        args.append(prev_dq)
    ci, ca, co, cs, cscr = _carry_args(carry)
    n_out = len(out_specs)
    res = pl.pallas_call(
        _carried(kern, carry, len(args), n_out, *_grid_ends((N_KV, nq))), name=name, grid=(N_KV, nq),
        in_specs=in_specs + ci, out_specs=out_specs + co, out_shape=out_shape + cs, scratch_shapes=cscr,
        input_output_aliases=alias, compiler_params=_params(("arbitrary", "arbitrary")),
    )(*args, *ca)
    return res if carry is None else (res[:n_out], res[n_out:])


def _dqkv_assemble(name, dq_all, dkp, dvp, dkc_l, dvc_l, dkc_c, dvc_c, cos, sin, S):
    T = CTX + S
    KW = N_KV * HEAD
    HALF = N_Q * HEAD // 2

    def kern(dq_ref, dkp_ref, dvp_ref, dkcl_ref, dvcl_ref, dkcc_ref, dvcc_ref, cos_ref, sin_ref, out_ref):
        i = pl.program_id(0)
        j = pl.program_id(1)
        cos_v, sin_v = cos_ref[...], sin_ref[...]

        @pl.when((j < 2) & (i == 0))
        def _():
            out_ref[...] = dq_ref[...].astype(BF16)

        @pl.when((j < 2) & (i > 0))
        def _():
            out_ref[...] = _unrope(dq_ref[...], cos_v, sin_v).astype(BF16)

        @pl.when((j == 2) & (i == 0))
        def _():
            out_ref[...] = jnp.concatenate([(dkcl_ref[...] + dkcc_ref[...]).T, (dvcl_ref[...] + dvcc_ref[...]).T],
                                           axis=1).astype(BF16)

        @pl.when((j == 2) & (i > 0))
        def _():
            out_ref[...] = jnp.concatenate([_unrope(dkp_ref[...].T, cos_v, sin_v), dvp_ref[...].T],
                                           axis=1).astype(BF16)

    same = lambda i, j: (0, i)
    lat_map = lambda i, j: (jnp.maximum(i - 1, 0), 0)
    ctx_map = lambda i, j: (0, 0)
    return pl.pallas_call(
        kern, name=name, grid=(T // TR, 3),
        in_specs=[pl.BlockSpec((TR, HALF), lambda i, j: (i, jnp.minimum(j, 1))),
                  pl.BlockSpec((KW, TR), same), pl.BlockSpec((KW, TR), same),
                  pl.BlockSpec((KW, CTX), ctx_map), pl.BlockSpec((KW, CTX), ctx_map),
                  pl.BlockSpec((KW, CTX), ctx_map), pl.BlockSpec((KW, CTX), ctx_map),
                  pl.BlockSpec((TR, HEAD), lat_map), pl.BlockSpec((TR, HEAD), lat_map)],
        out_specs=pl.BlockSpec((TR, HALF), lambda i, j: (i, COL_Q // HALF + j)),
        out_shape=_sds((T, DP_W), BF16), compiler_params=_params(("arbitrary", "arbitrary")),
    )(dq_all, dkp, dvp, dkc_l, dvc_l, dkc_c, dvc_c, cos, sin)


RB = 128
CH = 256
HALO = 8
SUB = 8
GRP = 8


def _vscan(a, b, reverse):
    row = lax.broadcasted_iota(jnp.int32, a.shape, 0)
    A, H = a, b
    for s in (1, 2, 4):
        sh = SUB - s if reverse else s
        m = (row < SUB - s) if reverse else (row >= s)
        As = pltpu.roll(A, sh, 0)
        Hs = pltpu.roll(H, sh, 0)
        H = jnp.where(m, A * Hs + H, H)
        A = jnp.where(m, A * As, A)
    return A, H


def _scan_rows(a_ref, b_ref, r0, nrows, reverse, carry, emit):
    ngrp = nrows // (SUB * GRP)
    row = lax.broadcasted_iota(jnp.int32, (SUB, RB), 0)

    def grp(gi, carry):
        g = (ngrp - 1 - gi) if reverse else gi
        base = r0 + g * (SUB * GRP)
        for v in (range(GRP - 1, -1, -1) if reverse else range(GRP)):
            rs = pl.multiple_of(base + v * SUB, SUB)
            A, H = _vscan(a_ref[pl.ds(rs, SUB), :], b_ref[pl.ds(rs, SUB), :], reverse)
            hf = H + A * carry
            if reverse:
                before = jnp.where(row == SUB - 1, carry, pltpu.roll(hf, SUB - 1, 0))
                carry = hf[0:1, :]
            else:
                before = jnp.where(row == 0, carry, pltpu.roll(hf, 1, 0))
                carry = hf[SUB - 1:SUB, :]
            emit(rs, hf, before)
        return carry

    return lax.fori_loop(0, ngrp, grp, carry)


def _pad_start(ci):
    return pl.multiple_of(ci * CH + HALO * jnp.minimum(ci, 1), HALO)


def _conv_taps(ext, transpose=False):
    n = CH + 2 * HALO
    taps = []
    for k in range(CONV_W):
        off = CONV_LEFT - k if transpose else k - CONV_LEFT
        taps.append(ext[HALO:HALO + CH, :] if off == 0 else pltpu.roll(ext, (-off) % n, 0)[HALO:HALO + CH, :])
    return taps


def _lru_gates(xl, w4, b4, ls):
    pre = _dot(xl.astype(BF16), w4) + b4
    out = []
    for d in range(2):
        r = _sigmoid(pre[:, d * RB:(d + 1) * RB])
        i = _sigmoid(pre[:, (2 + d) * RB:(3 + d) * RB])
        la = LRU_C * r * ls[d:d + 1, :]
        a = jnp.exp(la)
        q = -jnp.tanh(la) * (1.0 + a * a)
        out.append((r, i, a, q))
    return out


def _rnn_specs(T):
    col = lambda n, *_: (0, n)
    return dict(
        xr=pl.BlockSpec((T, RB), lambda n, *_: (0, COL_XR // RB + n)),
        gr=pl.BlockSpec((T, RB), lambda n, *_: (0, COL_GR // RB + n)),
        act=pl.BlockSpec((T, RB), col),
        cw=pl.BlockSpec((CONV_W, RB), col), cb=pl.BlockSpec((1, RB), col),
        w4=pl.BlockSpec((None, RB, 4 * RB), lambda n, *_: (n, 0, 0)),
        b4=pl.BlockSpec((None, 1, 4 * RB), lambda n, *_: (n, 0, 0)),
        lam=pl.BlockSpec((2, RB), col))


PAD_ROWS = 3 * HALO


def _zero_pads(pad_ref, T):
    for r in (0, HALO + CTX, 2 * HALO + T):
        pad_ref[r:r + HALO, :] = jnp.zeros((HALO, RB), F32)


def _fill_padded(pad_ref, src_ref, T):
    _zero_pads(pad_ref, T)
    pad_ref[HALO:HALO + CTX, :] = src_ref[0:CTX, :].astype(F32)
    pad_ref[2 * HALO + CTX:2 * HALO + T, :] = src_ref[CTX:T, :].astype(F32)


def _pad_rows(ci):
    return pl.ds(pl.multiple_of(ci * CH + HALO + HALO * jnp.minimum(ci, 1), HALO), CH)


def _rnn_fwd(name, p, cw, cb, w4, b4, lam, T, carry=None):
    def kern(xr_ref, gr_ref, cw_ref, cb_ref, w4_ref, b4_ref, lam_ref,
             u_ref, a0, a1, yo_ref, hpf_ref, hpb_ref, r0_ref, r1_ref, i0_ref, i1_ref, xpad, b0, b1, y):
        _fill_padded(xpad, xr_ref, T)
        ls = _log_sigmoid(lam_ref[...])
        w4v, b4v, cwv, cbv = w4_ref[...], b4_ref[...], cw_ref[...], cb_ref[...]

        def chunk(ci, _):
            rows = pl.ds(pl.multiple_of(ci * CH, CH), CH)
            taps = _conv_taps(xpad[pl.ds(_pad_start(ci), CH + 2 * HALO), :])
            xl = cbv + sum(taps[k] * cwv[k:k + 1, :] for k in range(CONV_W))
            for d, (r, i, a, q) in enumerate(_lru_gates(xl, w4v, b4v, ls)):
                (a0, a1)[d][rows, :] = a
                (b0, b1)[d][rows, :] = jnp.sqrt(q) * (i * xl)
                (r0_ref, r1_ref)[d][rows, :] = r.astype(BF16)
                (i0_ref, i1_ref)[d][rows, :] = i.astype(BF16)
            return 0

        lax.fori_loop(0, T // CH, chunk, 0)
        zero = jnp.zeros((1, RB), F32)

        def emit_f(rs, hf, before):
            y[pl.ds(rs, SUB), :] = hf
            b0[pl.ds(rs, SUB), :] = before

        def emit_b(rs, hf, before):
            y[pl.ds(rs, SUB), :] += hf
            b1[pl.ds(rs, SUB), :] = before

        _scan_rows(a0, b0, 0, T, False, zero, emit_f)
        c = _scan_rows(a1, b1, 0, CTX, True, zero, emit_b)
        _scan_rows(a1, b1, CTX, T - CTX, True, c, emit_b)

        def finish(ci, _):
            rows = pl.ds(pl.multiple_of(ci * CH, CH), CH)
            yv = y[rows, :]
            u_ref[rows, :] = (yv * _gelu(gr_ref[rows, :].astype(F32))).astype(BF16)
            yo_ref[rows, :] = yv.astype(BF16)
            hpf_ref[rows, :] = b0[rows, :].astype(BF16)
            hpb_ref[rows, :] = b1[rows, :].astype(BF16)
            return 0

        lax.fori_loop(0, T // CH, finish, 0)

    sp = _rnn_specs(T)
    ci, ca, co, cs, cscr = _carry_args(carry)
    dts = [BF16, F32, F32] + [BF16] * 7
    res = pl.pallas_call(
        _carried(kern, carry, 7, 10, *_grid_ends((N_RNN_BLOCKS,))), name=name, grid=(N_RNN_BLOCKS,),
        in_specs=[sp["xr"], sp["gr"], sp["cw"], sp["cb"], sp["w4"], sp["b4"], sp["lam"]] + ci,
        out_specs=[sp["act"]] * 10 + co,
        out_shape=[_sds((T, D), dt) for dt in dts] + cs,
        scratch_shapes=[pltpu.VMEM((T + PAD_ROWS, RB), F32)] + [pltpu.VMEM((T, RB), F32)] * 3 + cscr,
        compiler_params=_params(),
    )(p, p, cw, cb, w4, b4, lam, *ca)
    return res if carry is None else (res[:10], res[10:])


def _rnn_bwd(name, p, du, saved, dp, cw, cb, w4, b4, lam, T, carry=None):
    def kern(xr_ref, gr_ref, du_ref, a0, a1, y_ref, hpf_ref, hpb_ref, r0_ref, r1_ref, i0_ref, i1_ref,
             cw_ref, cb_ref, w4_ref, b4_ref, lam_ref, dp_in,
             dp_ref, dcw_ref, dcb_ref, dw4_ref, db4_ref, dlam_ref,
             xpad, dxpad, c0, c1, dy):
        j = pl.program_id(1)

        @pl.when(j == 0)
        def _():
            scans(gr_ref, du_ref, a0, a1, y_ref, dp_ref, c0, c1, dy)

        @pl.when(j == 1)
        def _():
            gates(xr_ref, a0, a1, (hpf_ref, hpb_ref), (r0_ref, r1_ref), (i0_ref, i1_ref), cw_ref, cb_ref, w4_ref,
                  lam_ref, dp_ref, dcw_ref, dcb_ref, dw4_ref, db4_ref, dlam_ref, xpad, dxpad, c0, c1)

    def scans(gr_ref, du_ref, a0, a1, y_ref, dgr_ref, c0, c1, dy):
        def phase_a(ci, _):
            rows = pl.ds(pl.multiple_of(ci * CH, CH), CH)
            gr = gr_ref[rows, :].astype(F32)
            duv = du_ref[rows, :].astype(F32)
            dyv = duv * _gelu(gr)
            dgr_ref[rows, :] = (duv * y_ref[rows, :].astype(F32) * _gelu_grad(gr)).astype(BF16)
            dy[rows, :] = dyv
            c0[rows, :] = a0[rows, :] * dyv
            c1[rows, :] = a1[rows, :] * dyv
            return 0

        lax.fori_loop(0, T // CH, phase_a, 0)
        zero = jnp.zeros((1, RB), F32)

        def emit0(rs, hf, before):
            c0[pl.ds(rs, SUB), :] = dy[pl.ds(rs, SUB), :] + before

        def emit1(rs, hf, before):
            c1[pl.ds(rs, SUB), :] = dy[pl.ds(rs, SUB), :] + before

        _scan_rows(a0, c0, 0, T, True, zero, emit0)
        c = _scan_rows(a1, c1, CTX, T - CTX, False, zero, emit1)
        _scan_rows(a1, c1, 0, CTX, False, c, emit1)

    def gates(xr_ref, a0, a1, hp_refs, r_refs, i_refs, cw_ref, cb_ref, w4_ref, lam_ref,
              dxr_ref, dcw_ref, dcb_ref, dw4_ref, db4_ref, dlam_ref, xpad, dxpad, c0, c1):
        _fill_padded(xpad, xr_ref, T)
        _zero_pads(dxpad, T)
        lam_v = lam_ref[...]
        ls = _log_sigmoid(lam_v)
        w4v, cwv, cbv = w4_ref[...], cw_ref[...], cb_ref[...]

        def conv_chunk(ci):
            taps = _conv_taps(xpad[pl.ds(_pad_start(ci), CH + 2 * HALO), :])
            return taps, cbv + sum(taps[k] * cwv[k:k + 1, :] for k in range(CONV_W))

        dw4_ref[...] = jnp.zeros(dw4_ref.shape, F32)
        db4_ref[...] = jnp.zeros(db4_ref.shape, F32)
        dlam_ref[...] = jnp.zeros(dlam_ref.shape, F32)
        dcw_ref[...] = jnp.zeros(dcw_ref.shape, F32)
        dcb_ref[...] = jnp.zeros(dcb_ref.shape, F32)

        def phase_c(ci, _):
            base = pl.multiple_of(ci * CH, CH)
            rows = pl.ds(base, CH)
            _, xl = conv_chunk(ci)
            dxl = jnp.zeros((CH, RB), F32)
            dpre_a, dpre_x, dls = [], [], []
            for d in range(2):
                a = (a0, a1)[d][rows, :]
                r = r_refs[d][rows, :].astype(F32)
                i = i_refs[d][rows, :].astype(F32)
                q = -jnp.tanh(LRU_C * r * ls[d:d + 1, :]) * (1.0 + a * a)
                g = (c0, c1)[d][rows, :]
                hp = hp_refs[d][rows, :].astype(F32)
                gm = g * jnp.sqrt(q)
                di = gm * xl
                dxl = dxl + gm * i
                dla = a * (g * hp - a * (g * (i * xl)) * lax.rsqrt(q))
                dr = dla * (LRU_C * ls[d:d + 1, :])
                dls.append(_colsum(dla * (LRU_C * r)))
                dpre_a.append(dr * r * (1.0 - r))
                dpre_x.append(di * i * (1.0 - i))
            dpre = jnp.concatenate(dpre_a + dpre_x, axis=1)
            dpre_b = dpre.astype(BF16)
            dxl = dxl + _dot(dpre_b, w4v, NT)
            dw4_ref[...] += _dot(xl.astype(BF16), dpre_b, TN)
            db4_ref[...] += _colsum(dpre)
            dlam_ref[...] += jnp.concatenate(dls, axis=0)
            dcb_ref[...] += _colsum(dxl)
            dxpad[_pad_rows(ci), :] = dxl
            return 0

        lax.fori_loop(0, T // CH, phase_c, 0)
        dlam_ref[...] = dlam_ref[...] * _sigmoid(-lam_v)

        def phase_d(ci, _):
            base = pl.multiple_of(ci * CH, CH)
            rows = pl.ds(base, CH)
            xtaps, _ = conv_chunk(ci)
            dtaps = _conv_taps(dxpad[pl.ds(_pad_start(ci), CH + 2 * HALO), :], transpose=True)
            dxl = dxpad[_pad_rows(ci), :]
            dxr_ref[rows, :] = sum(dtaps[k] * cwv[k:k + 1, :] for k in range(CONV_W)).astype(BF16)
            dcw_ref[...] += jnp.concatenate([_colsum(dxl * xtaps[k]) for k in range(CONV_W)], axis=0)
            return 0

        lax.fori_loop(0, T // CH, phase_d, 0)

    sp = _rnn_specs(T)
    dp_spec = pl.BlockSpec((T, RB), lambda n, j: (0, COL_GR // RB + n - j * (COL_GR - COL_XR) // RB))
    ci, ca, co, cs, cscr = _carry_args(carry)
    n_in = 3 + len(saved) + 5 + 1
    res = pl.pallas_call(
        _carried(kern, carry, n_in, 6, *_grid_ends((N_RNN_BLOCKS, 2))), name=name, grid=(N_RNN_BLOCKS, 2),
        in_specs=[sp["xr"], sp["gr"]] + [sp["act"]] * (1 + len(saved)) + [sp["cw"], sp["cb"], sp["w4"], sp["b4"],
                                                                           sp["lam"], ANY] + ci,
        out_specs=[dp_spec, sp["cw"], sp["cb"], sp["w4"], sp["b4"], sp["lam"]] + co,
        out_shape=[_sds((T, DP_W), BF16), _sds((CONV_W, D), F32), _sds((1, D), F32),
                   _sds((N_RNN_BLOCKS, RB, 4 * RB), F32), _sds((N_RNN_BLOCKS, 1, 4 * RB), F32), _sds((2, D), F32)] + cs,
        scratch_shapes=[pltpu.VMEM((T + PAD_ROWS, RB), F32)] * 2 + [pltpu.VMEM((T, RB), F32)] * 3 + cscr,
        input_output_aliases={n_in - 1: 0},
        compiler_params=_params(("arbitrary", "arbitrary")),
    )(p, p, du, *saved, cw, cb, w4, b4, lam, dp, *ca)
    return res if carry is None else (res[:6], res[6:])


class _Plan:
    def __init__(self, shards, Ws):
        L = len(Ws)
        self.shards, self.Ws = shards, Ws
        self.Gs = [None] * L
        self.slots = [dict() for _ in range(L)]
        self.gate_slots = [None] * L
        self.table = {}
        for l in range(L):
            t = f"l{l}_"
            self.table[t + "proj"] = [("gather", l, k) for k in ("wo_rnn", "wo_attn", "wout")]
            self.table[t + "rnn_fwd"] = ([("gather", l, "wffn_in_t")]
                                         + ([("gather", l + 1, "win_t")] if l + 1 < L else []))
            self.table[t + "attn_lat_fwd"] = [("gather", l, "wffn_out")]
            self.table[t + "ffn_in_dx"] = [("scatter", l, "wffn_out")]
            self.table[t + "attn_lat_bwd"] = [("scatter", l, "wffn_in_t")]
            self.table[t + "proj_dx"] = [("scatter", l, "win_t_a")]
            self.table[t + "rnn_bwd"] = ([("scatter", l, k) for k in ("wout", "wo_attn", "wo_rnn")]
                                         + ([("scatter", l + 1, "win_t_b"), ("gates", l + 1, "w4")] if l + 1 < L else []))
        self.table["l0_proj_dw_b"] = [("gates", 0, "w4")]

    def carry(self, name):
        jobs = []
        for kind, l, k in self.table.get(name, []):
            if kind == "gather":
                jobs.append(("gather", self.shards[l][k]))
            elif kind == "scatter":
                jobs.append(("scatter", self.Gs[l][k].reshape(N_DEV, -1, self.Gs[l][k].shape[-1])))
            else:
                jobs.append(("gather", self.Gs[l]["w4"].reshape(N_RNN_BLOCKS * RB, 4 * RB).astype(BF16)))
        return _Carry(jobs) if jobs else None

    def done(self, name, got):
        for (kind, l, k), res in zip(self.table[name], got):
            if kind == "gather":
                self.Ws[l][k] = res.reshape(-1, D)
            elif kind == "scatter":
                self.slots[l][k] = res
            else:
                self.gate_slots[l] = res


def _run(X, fn, name, *args, **kw):
    carry = None if X is None else X.carry(name)
    if carry is None:
        return fn(name, *args, **kw)
    out, got = fn(name, *args, carry=carry, **kw)
    X.done(name, got)
    return out


def _layer_fwd(l, xa, h, W, rope, S, nxt, X=None):
    T = xa.shape[0]
    tag = f"l{l}_"
    cos, sin, bias = rope
    p = _run(X, _mm_act, tag + "proj", h, W["win_t"], "NT", BF16)
    u, *rnn_saved = _run(X, _rnn_fwd, tag + "rnn_fwd", p, W["cw"], W["cb"], W["w4"], W["b4"], W["lam"], T)
    qa, kp, vp, kc, vc = _qkv_prep(tag + "qkv_prep", p, cos, sin, S)
    o_all = _attn_fwd(tag + "attn_ctx_fwd", qa, kc, vc, W["sink4"], S)
    o_all = _run(X, _attn_fwd, tag + "attn_lat_fwd", qa, kc, vc, W["sink4"], S, band=(kp, vp, bias), prev=o_all)
    ya, yb, z, m, x1, h2 = _out_fused(tag + "out", p, u, o_all, xa, W["wo_rnn"], W["wo_attn"], W["wout"],
                                      W["g_mix_post"], W["mod"], W["g_ffn_pre"])
    fg, fu, s = _run(X, _ffn_in_fused, tag + "ffn_in", h2, W["wffn_in_t"])
    e, *out = _ffn_out_fused(tag + "ffn_out", s, W["wffn_out"], x1, W["g_ffn_post"], W["mod"], nxt)
    saved = dict(xa=xa, h=h, p=p, u=u, rnn=rnn_saved, qa=qa, kp=kp, vp=vp, kc=kc, vc=vc, o_all=o_all,
                 ya=ya, yb=yb, z=z, m=m, x1=x1, h2=h2, fg=fg, fu=fu, s=s, e=e)
    return saved, out


def _layer_bwd(l, dx2, A, W, rope, S, X=None, loss_of=None):
    T = A["xa"].shape[0]
    tag = f"l{l}_"
    cos, sin, bias = rope
    G = {}
    if X is not None:
        X.Gs[l] = G
    if loss_of is None:
        de, df, dga2, G["g_ffn_post"] = _ffn_bwd_fused(tag + "ffn_bwd", A["fg"], A["fu"], W["wffn_out"],
                                                       head=(dx2, A["e"], W["g_ffn_post"], W["mod"]))
    else:
        dx2, de, dga2, G["g_ffn_post"], G["sq"] = _loss_resid_bwd(tag + "loss_ffn_resid_bwd", *loss_of, A["e"],
                                                                  W["g_ffn_post"], W["mod"], GA2)
        df, = _ffn_bwd_fused(tag + "ffn_bwd", A["fg"], A["fu"], W["wffn_out"], de=de)
    G["wffn_out"] = _mm_wgrad(tag + "ffn_out_dw", A["s"], de)
    dx1, dm, dsh2, dsc2, G["g_ffn_pre"], dga1, G["g_mix_post"] = _run(
        X, _ffn_in_bwd_fused, tag + "ffn_in_dx", df, W["wffn_in_t"], A["x1"], dx2, A["m"], W["g_ffn_pre"], W["mod"],
        W["g_mix_post"])
    G["wffn_in_t"] = _run(X, _mm_wgrad, tag + "ffn_in_dw", df, A["h2"])
    G["wout"] = _mm_wgrad(tag + "out_dw", A["z"], dm)
    dya, dyb, dgl, du, do = _out_bwd_fused(tag + "out_dx", dm, W["wout"], W["wo_rnn"], W["wo_attn"], A["p"], A["ya"],
                                           A["yb"])
    G["wo_attn"] = _mm_wgrad(tag + "o_attn_dw", A["o_all"], dyb)
    G["wo_rnn"] = _mm_wgrad(tag + "o_rnn_dw", A["u"], dya)
    dq_all, dkc_c, dvc_c, dsink_c = _attn_bwd(tag + "attn_ctx_bwd", A["qa"], A["kc"], A["vc"], W["sink4"],
                                               A["o_all"], do, S)
    dq_all, dkc_l, dvc_l, dsink_l, dkp, dvp = _run(
        X, _attn_bwd, tag + "attn_lat_bwd", A["qa"], A["kc"], A["vc"], W["sink4"], A["o_all"], do, S,
        band=(A["kp"], A["vp"], bias), prev_dq=dq_all)
    G["sink4"] = dsink_c + dsink_l
    dp = _dqkv_assemble(tag + "dqkv", dq_all, dkp, dvp, dkc_l, dvc_l, dkc_c, dvc_c, cos, sin, S)
    dp, G["cw"], G["cb"], G["w4"], G["b4"], G["lam"] = _run(
        X, _rnn_bwd, tag + "rnn_bwd", A["p"], du, A["rnn"], dp, W["cw"], W["cb"], W["w4"], W["b4"], W["lam"], T)
    proj_dx = (_proj_bwd_fused, tag + "proj_dx", dp, dgl, W["win_t"], A["xa"], dx1, W["g_mix_pre"], W["mod"])
    if X is not None:
        G["win_t_a"] = _proj_wgrad(tag + "proj_dw_a", dp, dgl, A["h"][:, :D // 2])
        dxa, dsh1, dsc1, G["g_mix_pre"] = _run(X, *proj_dx)
        G["win_t_b"] = _run(X, _proj_wgrad, tag + "proj_dw_b", dp, dgl, A["h"][:, D // 2:])
    else:
        dxa, dsh1, dsc1, G["g_mix_pre"] = _run(X, *proj_dx)
        G["win_t"] = _proj_wgrad(tag + "proj_dw", dp, dgl, A["h"])
    G["mod"] = jnp.concatenate([dsh1, dsc1, dga1, dsh2, dsc2, dga2], axis=1)
    return dxa, G


def _local_step(xa, target, Ws, S, X=None):
    rope = (*_rope_tables(S), _band_bias(S))
    L = len(Ws)
    h = _normmod_fwd("l0_mix_norm", xa, Ws[0]["g_mix_pre"], Ws[0]["mod"], SH1, SC1)
    saved = []
    x = xa
    for l in range(L):
        nxt = (Ws[l + 1]["g_mix_pre"], Ws[l + 1]["mod"]) if l + 1 < L else None
        A, out = _layer_fwd(l, x, h, Ws[l], rope, S, nxt, X)
        saved.append(A)
        if l + 1 < L:
            x, h = out
    Gs = [None] * L
    dx = None
    for l in reversed(range(L)):
        dx, Gs[l] = _layer_bwd(l, dx, saved[l], Ws[l], rope, S, X, loss_of=(out[0], target) if l == L - 1 else None)
    return Gs[L - 1]["sq"], dx, Gs


MESH = pl.DeviceIdType.MESH


def _place():
    return lax.axis_index("x"), lax.axis_index("y"), lax.axis_index("c")


def _lin(px, py, pc):
    return 4 * px + 2 * py + pc


def _allgather_small(name, blk):
    m, n = blk.shape

    def body(x_ref, out_ref, send_sems, recv_sems, local_sem):
        x, y, c = _place()
        me, sibling = (x, y, c), (x, y, 1 - c)
        chips = [(1 - x, y), (x, 1 - y), (1 - x, 1 - y)]

        def copy(k, block, to, src=None):
            dst = out_ref.at[_lin(*block)]
            return pltpu.make_async_remote_copy(src_ref=dst if src is None else src, dst_ref=dst,
                                                send_sem=send_sems.at[k], recv_sem=recv_sems.at[k],
                                                device_id=to, device_id_type=MESH)

        mine = pltpu.make_async_copy(x_ref, out_ref.at[_lin(*me)], local_sem)
        mine.start()
        first = [copy(0, me, sibling, src=x_ref)]
        first += [copy(1 + j, me, (*chip, c), src=x_ref) for j, chip in enumerate(chips)]
        for cp in first:
            cp.start()
        passed = [copy(4 + j, (*chip, c), sibling) for j, chip in enumerate(chips)]
        for j, chip in enumerate(chips):
            copy(1 + j, (*chip, c), me).wait_recv()
            passed[j].start()
        copy(0, sibling, me).wait_recv()
        for j, chip in enumerate(chips):
            copy(4 + j, (*chip, 1 - c), me).wait_recv()
        for cp in first + passed:
            cp.wait_send()
        mine.wait()

    return pl.pallas_call(
        body, name=name, out_shape=_sds((N_DEV, m, n), blk.dtype),
        in_specs=[pl.BlockSpec(memory_space=pltpu.VMEM)], out_specs=pl.BlockSpec(memory_space=pltpu.VMEM),
        scratch_shapes=[pltpu.SemaphoreType.DMA((7,)), pltpu.SemaphoreType.DMA((7,)), pltpu.SemaphoreType.DMA],
        compiler_params=pltpu.CompilerParams(vmem_limit_bytes=VMEM_LIMIT),
    )(blk)


def _allgather_hbm(name, shards):
    na = len(shards)

    def body(*refs):
        ins, outs = refs[:na], refs[na:2 * na]
        send_sems, recv_sems, local_sems = refs[2 * na:]
        x, y, c = _place()
        me, sibling = (x, y, c), (x, y, 1 - c)
        chips = [(1 - x, y), (x, 1 - y), (1 - x, 1 - y)]

        def copy(a, k, block, to, from_input=False):
            dst = outs[a].at[_lin(*block)]
            return pltpu.make_async_remote_copy(src_ref=ins[a] if from_input else dst, dst_ref=dst,
                                                send_sem=send_sems.at[a, k], recv_sem=recv_sems.at[a, k],
                                                device_id=to, device_id_type=MESH)

        mine = [pltpu.make_async_copy(ins[a], outs[a].at[_lin(*me)], local_sems.at[a]) for a in range(na)]
        for cp in mine:
            cp.start()
        first = []
        for a in range(na):
            first.append(copy(a, 0, me, sibling, True))
            first += [copy(a, 1 + j, me, (*chip, c), True) for j, chip in enumerate(chips)]
        for cp in first:
            cp.start()
        passed = []
        for j, chip in enumerate(chips):
            for a in range(na):
                copy(a, 1 + j, (*chip, c), me).wait_recv()
                fwd = copy(a, 4 + j, (*chip, c), sibling)
                fwd.start()
                passed.append(fwd)
        for a in range(na):
            copy(a, 0, sibling, me).wait_recv()
            for j, chip in enumerate(chips):
                copy(a, 4 + j, (*chip, 1 - c), me).wait_recv()
        for cp in first + passed:
            cp.wait_send()
        for cp in mine:
            cp.wait()

    return pl.pallas_call(
        body, name=name, out_shape=[_sds((N_DEV, *s.shape), s.dtype) for s in shards],
        in_specs=[ANY] * na, out_specs=[ANY] * na,
        scratch_shapes=[pltpu.SemaphoreType.DMA((na, 7)), pltpu.SemaphoreType.DMA((na, 7)),
                        pltpu.SemaphoreType.DMA((na,))],
    )(*shards)


def _exchange_shards(name, grads, L):
    nw = len(grads)
    na = nw * L
    flat = [g for per_layer in grads for g in per_layer]

    def body(*refs):
        ins, outs = refs[:na], refs[na:na + nw]
        send_sems, recv_sems, local_sems = refs[na + nw:]
        x, y, c = _place()
        me = _lin(x, y, c)
        peers = [(x ^ ((k + 1) >> 2 & 1), y ^ ((k + 1) >> 1 & 1), c ^ ((k + 1) & 1)) for k in range(7)]

        def copy(a, k, src_blk, dst_blk):
            return pltpu.make_async_remote_copy(src_ref=ins[a].at[src_blk], dst_ref=outs[a // L].at[a % L, dst_blk],
                                                send_sem=send_sems.at[a, k], recv_sem=recv_sems.at[a, k],
                                                device_id=peers[k], device_id_type=MESH)

        mine = [pltpu.make_async_copy(ins[a].at[me], outs[a // L].at[a % L, me], local_sems.at[a]) for a in range(na)]
        for cp in mine:
            cp.start()
        sent = [copy(a, k, _lin(*peers[k]), me) for a in range(na) for k in range(7)]
        for cp in sent:
            cp.start()
        for a in range(na):
            for k in range(7):
                copy(a, k, me, _lin(*peers[k])).wait_recv()
        for cp in sent:
            cp.wait_send()
        for cp in mine:
            cp.wait()

    return pl.pallas_call(
        body, name=name, out_shape=[_sds((L, *per_layer[0].shape), per_layer[0].dtype) for per_layer in grads],
        in_specs=[ANY] * na, out_specs=[ANY] * nw,
        scratch_shapes=[pltpu.SemaphoreType.DMA((na, 7)), pltpu.SemaphoreType.DMA((na, 7)),
                        pltpu.SemaphoreType.DMA((na,))],
    )(*flat)


MOD_ROWS = 16
MOD_SHARD = 6 * D // N_DEV
HI = lax.Precision.HIGHEST


def _mod_fwd(name, c9, w_mod, b_shard):
    L = w_mod.shape[0]

    def kern(c_ref, w_ref, b_ref, o_ref):
        o_ref[...] = lax.dot_general(_silu(c_ref[...]), w_ref[...], NN, precision=HI,
                                     preferred_element_type=F32) + b_ref[...]

    return pl.pallas_call(
        kern, name=name, grid=(L,),
        in_specs=[_full_spec(c9.shape), pl.BlockSpec((None, D, MOD_SHARD), lambda l: (l, 0, 0)),
                  pl.BlockSpec((None, 1, MOD_SHARD), lambda l: (l, 0, 0))],
        out_specs=pl.BlockSpec((None, MOD_ROWS, MOD_SHARD), lambda l: (l, 0, 0)),
        out_shape=_sds((L, MOD_ROWS, MOD_SHARD), F32), compiler_params=_params(),
    )(c9, w_mod, b_shard)


def _mod_bwd(name, c9, w_mod, dmod_all, dmod_cols):
    L = w_mod.shape[0]

    def rows9(ref, l):
        own = jnp.concatenate([ref[j, 2 * l + 1:2 * l + 2, :] for j in range(N_DEV)], axis=0)
        ctx = ref[0, 2 * l:2 * l + 1, :]
        for j in range(1, N_DEV):
            ctx = ctx + ref[j, 2 * l:2 * l + 1, :]
        return own, ctx

    def kern(c_ref, w_ref, all_ref, cols_ref, gw_ref, gb_ref, gc_ref):
        l = pl.program_id(0)
        for ll in range(L):
            @pl.when(l == ll)
            def _():
                own, ctx = rows9(all_ref, ll)
                gb_ref[...] = _colsum(own) + ctx
                own_s, ctx_s = rows9(cols_ref, ll)
                r16 = jnp.concatenate([own_s, ctx_s, jnp.zeros((MOD_ROWS - N_DEV - 1, MOD_SHARD), F32)], axis=0)
                gw_ref[...] = lax.dot_general(_silu(c_ref[...]), r16, TN, precision=HI, preferred_element_type=F32)
                part = lax.dot_general(r16, w_ref[...], NT, precision=HI,
                                       preferred_element_type=F32)[N_DEV:N_DEV + 1, :]
                if ll == 0:
                    gc_ref[...] = part
                else:
                    gc_ref[...] += part

    return pl.pallas_call(
        kern, name=name, grid=(L,),
        in_specs=[_full_spec(c9.shape), pl.BlockSpec((None, D, MOD_SHARD), lambda l: (l, 0, 0)),
                  _full_spec(dmod_all.shape), _full_spec(dmod_cols.shape)],
        out_specs=[pl.BlockSpec((None, D, MOD_SHARD), lambda l: (l, 0, 0)),
                   pl.BlockSpec((None, 1, 6 * D), lambda l: (l, 0, 0)), _full_spec((1, D))],
        out_shape=[_sds((L, D, MOD_SHARD), F32), _sds((L, 1, 6 * D), F32), _sds((1, D), F32)],
        compiler_params=_params(),
    )(c9, w_mod, dmod_all, dmod_cols)


_BC1 = 1.0 - ADAM_B1 ** ADAM_STEP
_BC2 = 1.0 - ADAM_B2 ** ADAM_STEP


def _adamw_vals(w, g, m, v):
    m = ADAM_B1 * m + (1.0 - ADAM_B1) * g
    v = ADAM_B2 * v + (1.0 - ADAM_B2) * (g * g)
    delta = -ADAM_LR * ((m / _BC1) / (jnp.sqrt(v / _BC2) + ADAM_EPS) + ADAM_WD * w)
    return delta, m, v


def _adamw(name, w, g, m, v, tile):
    R, C = w.shape
    blk = ((tile, C), lambda i: (i, 0))

    def body(i, ins, ps, outs, acc):
        d, mm, vv = _adamw_vals(ins[0][...], ins[1][...], ins[2][...], ins[3][...])
        outs[0][...] = d
        outs[1][...] = mm
        outs[2][...] = vv

    return _ew(name, body, R // tile, [(a, *blk) for a in (w, g, m, v)], [], [(_sds((R, C), F32), *blk)] * 3)


def _sum_slots(ref):
    g = ref[0].astype(F32)
    for j in range(1, N_DEV):
        g = g + ref[j].astype(F32)
    return g


def _adamw_slots(name, slots, shape, tile, wmv=None):
    L, R, C = shape
    n = R // tile
    spec = pl.BlockSpec((None, tile, C), lambda l, i: (l, i, 0))
    pieces = [s if isinstance(s, (list, tuple)) else [s] for s in slots]
    layer_of = [ll for ll, ps in enumerate(pieces) for _ in ps]
    flat = [p for ps in pieces for p in ps]
    wmv = list(wmv or [])

    def slot_spec(ll, cols):
        return pl.BlockSpec((N_DEV, tile, cols),
                            lambda l, i: (0, jnp.where(l == ll, i, jnp.where(l < ll, 0, n - 1)), 0))

    def kern(*refs):
        s_refs = refs[:len(flat)]
        rest = refs[len(flat):]
        l = pl.program_id(0)
        for ll in range(L):
            @pl.when(l == ll)
            def _():
                parts = [_sum_slots(r) for r, lr in zip(s_refs, layer_of) if lr == ll]
                g = parts[0] if len(parts) == 1 else jnp.concatenate(parts, axis=1)
                if wmv:
                    w_ref, m_ref, v_ref, g_ref, d_ref, mo_ref, vo_ref = rest
                    d_ref[...], mo_ref[...], vo_ref[...] = _adamw_vals(w_ref[...], g, m_ref[...], v_ref[...])
                else:
                    g_ref, = rest
                g_ref[...] = g

    n_out = 4 if wmv else 1
    return pl.pallas_call(
        kern, name=name, grid=(L, n),
        in_specs=[slot_spec(ll, p.shape[-1]) for ll, p in zip(layer_of, flat)] + [spec] * len(wmv),
        out_specs=[spec] * n_out, out_shape=[_sds((L, R, C), F32)] * n_out,
        compiler_params=_params(("arbitrary", "arbitrary")),
    )(*flat, *wmv)


def _sum_blocks(name, blocks):
    _, R, C = blocks.shape

    def kern(b_ref, o_ref):
        o_ref[...] = _sum_slots(b_ref)

    return pl.pallas_call(kern, name=name, in_specs=[_full_spec(blocks.shape)], out_specs=_full_spec((R, C)),
                          grid=(1,), out_shape=_sds((R, C), F32), compiler_params=_params())(blocks)


BIG = ("win_t", "wo_rnn", "wo_attn", "wout", "wffn_in_t", "wffn_out")
BIG_SRC = ("w_in", "w_o_rnn", "w_o_attn", "w_out", "w_ffn_in", "w_ffn_out")
BIG_T = (True, False, False, False, True, False)
BIG_TILE = (176, 128, 128, 128, 176, 176)


def _chan_full(g8):
    return jnp.transpose(g8, (1, 0, 2)).reshape(g8.shape[1], D)


def kernel(x, c, ctx, c_ctx, w_mod, b_mod, g_mix_pre, g_mix_post, g_ffn_pre, g_ffn_post, w_in, conv_w, conv_b, lru_wa, lru_ba, lru_wx, lru_bx, lru_lam, attn_sink, w_o_rnn, w_o_attn, w_out, w_ffn_in, w_ffn_out, loss_target, m_c_ctx, m_w_mod, m_b_mod, m_g_mix_pre, m_g_mix_post, m_g_ffn_pre, m_g_ffn_post, m_w_in, m_conv_w, m_conv_b, m_lru_wa, m_lru_ba, m_lru_wx, m_lru_bx, m_lru_lam, m_attn_sink, m_w_o_rnn, m_w_o_attn, m_w_out, m_w_ffn_in, m_w_ffn_out, v_c_ctx, v_w_mod, v_b_mod, v_g_mix_pre, v_g_mix_post, v_g_ffn_pre, v_g_ffn_post, v_w_in, v_conv_w, v_conv_b, v_lru_wa, v_lru_ba, v_lru_wx, v_lru_bx, v_lru_lam, v_attn_sink, v_w_o_rnn, v_w_o_attn, v_w_out, v_w_ffn_in, v_w_ffn_out):
    P = dict(c_ctx=c_ctx, w_mod=w_mod, b_mod=b_mod, g_mix_pre=g_mix_pre, g_mix_post=g_mix_post, g_ffn_pre=g_ffn_pre,
             g_ffn_post=g_ffn_post, w_in=w_in, conv_w=conv_w, conv_b=conv_b, lru_wa=lru_wa, lru_ba=lru_ba,
             lru_wx=lru_wx, lru_bx=lru_bx, lru_lam=lru_lam, attn_sink=attn_sink, w_o_rnn=w_o_rnn, w_o_attn=w_o_attn,
             w_out=w_out, w_ffn_in=w_ffn_in, w_ffn_out=w_ffn_out)
    Mo = dict(c_ctx=m_c_ctx, w_mod=m_w_mod, b_mod=m_b_mod, g_mix_pre=m_g_mix_pre, g_mix_post=m_g_mix_post,
              g_ffn_pre=m_g_ffn_pre, g_ffn_post=m_g_ffn_post, w_in=m_w_in, conv_w=m_conv_w, conv_b=m_conv_b,
              lru_wa=m_lru_wa, lru_ba=m_lru_ba, lru_wx=m_lru_wx, lru_bx=m_lru_bx, lru_lam=m_lru_lam,
              attn_sink=m_attn_sink, w_o_rnn=m_w_o_rnn, w_o_attn=m_w_o_attn, w_out=m_w_out, w_ffn_in=m_w_ffn_in,
              w_ffn_out=m_w_ffn_out)
    Vo = dict(c_ctx=v_c_ctx, w_mod=v_w_mod, b_mod=v_b_mod, g_mix_pre=v_g_mix_pre, g_mix_post=v_g_mix_post,
              g_ffn_pre=v_g_ffn_pre, g_ffn_post=v_g_ffn_post, w_in=v_w_in, conv_w=v_conv_w, conv_b=v_conv_b,
              lru_wa=v_lru_wa, lru_ba=v_lru_ba, lru_wx=v_lru_wx, lru_bx=v_lru_bx, lru_lam=v_lru_lam,
              attn_sink=v_attn_sink, w_o_rnn=v_w_o_rnn, w_o_attn=v_w_o_attn, w_out=v_w_out, w_ffn_in=v_w_ffn_in,
              w_ffn_out=v_w_ffn_out)
    L = w_in.shape[0]
    S = x.shape[1]
    me = _lin(*_place())

    small = jnp.concatenate([c.reshape(8, 128), conv_w.reshape(L * CONV_W, 128), lru_ba.reshape(2 * L, 128),
                             lru_bx.reshape(2 * L, 128), lru_lam.reshape(2 * L, 128), jnp.zeros((4, 128), F32)], axis=0)
    small_all = _allgather_small("ag_small", small)
    c_all = small_all[:, 0:8].reshape(N_DEV, D)
    conv_w_f = _chan_full(small_all[:, 8:16]).reshape(L, CONV_W, D)
    lru_ba_f = _chan_full(small_all[:, 16:20]).reshape(L, 2, D)
    lru_bx_f = _chan_full(small_all[:, 20:24]).reshape(L, 2, D)
    lru_lam_f = _chan_full(small_all[:, 24:28]).reshape(L, 2, D)

    c9 = jnp.concatenate([c_all, c_ctx[None], jnp.zeros((MOD_ROWS - N_DEV - 1, D), F32)], axis=0)
    b_shard = lax.dynamic_slice_in_dim(b_mod, me * MOD_SHARD, MOD_SHARD, axis=1)[:, None, :]
    mod_part = _mod_fwd("mod_fwd", c9, w_mod, b_shard)
    mod_all = _allgather_small("ag_mod", mod_part.reshape(L * MOD_ROWS, MOD_SHARD))
    mod_all = jnp.transpose(mod_all.reshape(N_DEV, L, MOD_ROWS, MOD_SHARD), (1, 2, 0, 3)).reshape(L, MOD_ROWS, 6 * D)
    own_row = lax.dynamic_index_in_dim(mod_all, me, axis=1, keepdims=False)
    modrows = jnp.stack([mod_all[:, N_DEV], own_row], axis=1)

    shards = [{k: (P[src][l].T if tr else P[src][l]).astype(BF16) for k, src, tr in zip(BIG, BIG_SRC, BIG_T)}
              for l in range(L)]
    win0, = _allgather_hbm("ag_w_in0", [shards[0]["win_t"]])
    Ws = []
    for l in range(L):
        W = {"win_t": win0.reshape(-1, D)} if l == 0 else {}
        W.update(
            cw=conv_w_f[l], cb=conv_b[l][None],
            w4=jnp.concatenate([lru_wa[l, 0], lru_wa[l, 1], lru_wx[l, 0], lru_wx[l, 1]], axis=-1).astype(BF16),
            b4=jnp.concatenate([lru_ba_f[l, 0].reshape(N_RNN_BLOCKS, 1, RB), lru_ba_f[l, 1].reshape(N_RNN_BLOCKS, 1, RB),
                                lru_bx_f[l, 0].reshape(N_RNN_BLOCKS, 1, RB), lru_bx_f[l, 1].reshape(N_RNN_BLOCKS, 1, RB)],
                               axis=-1),
            lam=lru_lam_f[l], sink4=jnp.broadcast_to(attn_sink[l].reshape(N_KV, Q_PER_KV, 1), (N_KV, Q_PER_KV, HEAD)),
            g_mix_pre=g_mix_pre[l][None], g_mix_post=g_mix_post[l][None], g_ffn_pre=g_ffn_pre[l][None],
            g_ffn_post=g_ffn_post[l][None], mod=modrows[l])
        Ws.append(W)

    xa = jnp.concatenate([ctx[0], x[0]], axis=0)
    plan = _Plan(shards, Ws)
    sq, dxa, Gs = _local_step(xa, loss_target[0], Ws, S, plan)
    loss_part = ((0.5 / D) * jnp.sum(sq)).reshape(1, 1)
    grad_x = dxa[CTX:][None]

    dmod = jnp.concatenate([Gs[l]["mod"] for l in range(L)] + [jnp.zeros((8 - 2 * L, 6 * D), F32)], axis=0)
    dmod_all = _allgather_small("ag_dmod", dmod)
    dmod_cols = lax.dynamic_slice_in_dim(dmod_all, me * MOD_SHARD, MOD_SHARD, axis=2)
    g_w_mod, g_b_mod, dsc_part = _mod_bwd("mod_bwd", c9, w_mod, dmod_all, dmod_cols)
    g_b_mod = g_b_mod[:, 0]

    def rows(name, shape):
        return jnp.concatenate([Gs[l][name].reshape(shape) for l in range(L)], axis=0)

    b4g = [Gs[l]["b4"].reshape(N_RNN_BLOCKS, 4, RB) for l in range(L)]
    sink_row = jnp.concatenate([Gs[l]["sink4"][:, :, 0].reshape(1, N_Q) for l in range(L)]
                               + [loss_part, jnp.zeros((1, D - L * N_Q - 1), F32)], axis=1)
    small_g = jnp.concatenate(
        [rows("g_mix_pre", (1, D)), rows("g_mix_post", (1, D)), rows("g_ffn_pre", (1, D)), rows("g_ffn_post", (1, D)),
         rows("cb", (1, D)), rows("cw", (CONV_W, D))]
        + [b4g[l][:, d].reshape(1, D) for l in range(L) for d in range(2)]
        + [b4g[l][:, 2 + d].reshape(1, D) for l in range(L) for d in range(2)]
        + [rows("lam", (2, D)), sink_row, dsc_part], axis=0)
    n_small = small_g.shape[0]
    small_tot = _sum_blocks("sum_small", _allgather_small("ag_small_grads", small_g))
    o = 0
    G = {}
    for name in ("g_mix_pre", "g_mix_post", "g_ffn_pre", "g_ffn_post", "conv_b"):
        G[name] = small_tot[o:o + L]
        o += L
    G["conv_w"] = small_tot[o:o + L * CONV_W].reshape(L, CONV_W, D)
    o += L * CONV_W
    for name in ("lru_ba", "lru_bx", "lru_lam"):
        G[name] = small_tot[o:o + 2 * L].reshape(L, 2, D)
        o += 2 * L
    G["attn_sink"] = small_tot[o, :L * N_Q].reshape(L, N_Q)
    loss = small_tot[o, L * N_Q]
    sg = jax.nn.sigmoid(c_ctx)
    G["c_ctx"] = small_tot[o + 1] * (sg * (1.0 + c_ctx * (1.0 - sg)))
    G["b_mod"] = g_b_mod
    G["w_mod"] = g_w_mod

    last_slots, = _exchange_shards("exchange_w_in0", [[Gs[0]["win_t_b"].reshape(N_DEV, -1, D // 2)]], 1)
    plan.slots[0]["win_t_b"] = last_slots[0]
    for l in range(L):
        plan.slots[l]["win_t"] = [plan.slots[l]["win_t_a"], plan.slots[l]["win_t_b"]]

    out_g, out_d, out_m, out_v = {}, {}, {}, {}

    def put(name, res, shape=None):
        g, d, m, v = res
        for dst, val in ((out_g, g), (out_d, d), (out_m, m), (out_v, v)):
            dst[name] = val if shape is None else val.reshape(shape)

    for k, src, tr, tile in zip(BIG, BIG_SRC, BIG_T, BIG_TILE):
        lay = (lambda a: jnp.swapaxes(a, 1, 2)) if tr else (lambda a: a)
        wmv = (lay(P[src]), lay(Mo[src]), lay(Vo[src]))
        res = _adamw_slots("adamw_" + src, [plan.slots[l][k] for l in range(L)], wmv[0].shape, tile, wmv)
        put(src, [lay(r) for r in res])
    res = _adamw("adamw_w_mod", w_mod.reshape(L * D, MOD_SHARD), g_w_mod.reshape(L * D, MOD_SHARD),
                 m_w_mod.reshape(L * D, MOD_SHARD), v_w_mod.reshape(L * D, MOD_SHARD), 256)
    put("w_mod", (g_w_mod,) + tuple(res), w_mod.shape)
    def fuse4(wa, wx):
        return jnp.concatenate([wa[:, 0], wa[:, 1], wx[:, 0], wx[:, 1]], axis=-1).reshape(L, N_RNN_BLOCKS * RB, 4 * RB)

    res = _adamw_slots("adamw_gates", plan.gate_slots, (L, N_RNN_BLOCKS * RB, 4 * RB), 256,
                       (fuse4(lru_wa, lru_wx), fuse4(m_lru_wa, m_lru_wx), fuse4(v_lru_wa, v_lru_wx)))
    res = [r.reshape(L, N_RNN_BLOCKS, RB, 4, RB) for r in res]
    put("lru_wa", [jnp.stack([r[:, :, :, 0], r[:, :, :, 1]], axis=1) for r in res])
    put("lru_wx", [jnp.stack([r[:, :, :, 2], r[:, :, :, 3]], axis=1) for r in res])
    rep = ("g_mix_pre", "g_mix_post", "g_ffn_pre", "g_ffn_post", "conv_b", "b_mod")

    def pack_rep(T_):
        sink = jnp.concatenate([T_["attn_sink"].reshape(1, L * N_Q), jnp.zeros((1, D - L * N_Q), F32)], axis=1)
        return jnp.concatenate([T_[n].reshape(-1, D) for n in rep] + [sink, T_["c_ctx"][None]], axis=0)

    pk = [pack_rep(T_) for T_ in (P, G, Mo, Vo)]
    n_rep = pk[0].shape[0]
    res = _adamw("adamw_replicated", *[jnp.pad(a, ((0, 24 - n_rep), (0, 0))) for a in pk], 24)
    res = (pk[1],) + tuple(r[:n_rep] for r in res)
    o = 0
    for n in rep:
        k = P[n].size // D
        put(n, [r[o:o + k] for r in res], P[n].shape)
        o += k
    put("attn_sink", [r[o, :L * N_Q] for r in res], attn_sink.shape)
    put("c_ctx", [r[o + 1] for r in res], c_ctx.shape)
    chan = ("conv_w", "lru_ba", "lru_bx", "lru_lam")
    g_own = {n: lax.dynamic_slice_in_dim(G[n], me * RB, RB, axis=2) for n in chan}

    def pack_chan(T_):
        return jnp.concatenate([T_[n].reshape(-1, RB) for n in chan], axis=0)

    pk = [pack_chan(T_) for T_ in (P, g_own, Mo, Vo)]
    n_ch = pk[0].shape[0]
    res = _adamw("adamw_channels", *[jnp.pad(a, ((0, 24 - n_ch), (0, 0))) for a in pk], 24)
    res = (pk[1],) + tuple(r[:n_ch] for r in res)
    o = 0
    for n in chan:
        k = P[n].size // RB
        put(n, [r[o:o + k] for r in res], P[n].shape)
        o += k

    order = ("c_ctx", "w_mod", "b_mod", "g_mix_pre", "g_mix_post", "g_ffn_pre", "g_ffn_post", "w_in", "conv_w", "conv_b",
             "lru_wa", "lru_ba", "lru_wx", "lru_bx", "lru_lam", "attn_sink", "w_o_rnn", "w_o_attn", "w_out", "w_ffn_in",
             "w_ffn_out")
    return (loss, grad_x, *[out_g[n] for n in order], *[out_d[n] for n in order], *[out_m[n] for n in order],
            *[out_v[n] for n in order])
```

```python
import functools
import math

import numpy as np
import jax
import jax.numpy as jnp
from jax import lax
from jax.experimental import pallas as pl
from jax.experimental.pallas import tpu as pltpu

F32 = jnp.float32
BF16 = jnp.bfloat16

D = 1024
CTX = 256
TR = 256
HEAD = 128
N_Q = 8
N_KV = 2
Q_PER_KV = N_Q // N_KV
GRID_W = 64
N_FREQ = HEAD // 4
ROPE_BASE = 10000.0
N_RNN_BLOCKS = 8
CONV_W = 4
CONV_LEFT = 2
LRU_C = 8.0
D_FF = 2816
IN_W = 5632
P_W = IN_W
DP_W = 3584
COL_XR, COL_GR, COL_Q, COL_K, COL_V, COL_GL = 0, 1024, 2048, 3072, 3328, 3584
GLB = 512
EPS = 1e-6
NEG_INF = -1e30
ATT_SCALE = HEAD ** -0.5
N_DEV = 8
VMEM_LIMIT = 56 * 1024 * 1024

ADAM_LR, ADAM_B1, ADAM_B2, ADAM_EPS, ADAM_WD, ADAM_STEP = 0.001, 0.9, 0.999, 1e-08, 0.01, 10

NN = (((1,), (0,)), ((), ()))
NT = (((1,), (1,)), ((), ()))
TN = (((0,), (0,)), ((), ()))


def _dot(a, b, dims=NN):
    return lax.dot_general(a, b, dims, preferred_element_type=F32)


def _params(sem=("arbitrary",)):
    return pltpu.CompilerParams(dimension_semantics=sem, vmem_limit_bytes=VMEM_LIMIT)


def _full_spec(shape):
    nd = len(shape)
    return pl.BlockSpec(shape, lambda *_: (0,) * nd)


ANY = pl.BlockSpec(memory_space=pl.ANY)


def _ew(name, body, n, row_ins, pars, row_outs, accs=(), alias=None):
    n_ri, n_p, n_ro, n_acc = len(row_ins), len(pars), len(row_outs), len(accs)

    def kern(*refs):
        i = pl.program_id(0)
        ins = refs[:n_ri]
        ps = refs[n_ri:n_ri + n_p]
        outs = refs[n_ri + n_p:n_ri + n_p + n_ro]
        acc = refs[n_ri + n_p + n_ro:]
        if n_acc:
            @pl.when(i == 0)
            def _():
                for a in acc:
                    a[...] = jnp.zeros(a.shape, a.dtype)
        body(i, ins, ps, outs, acc)

    in_specs = [ANY if blk is None else pl.BlockSpec(blk, imap) for (_, blk, imap) in row_ins]
    in_specs += [_full_spec(p.shape) for p in pars]
    out_specs = [pl.BlockSpec(blk, imap) for (_, blk, imap) in row_outs] + [_full_spec(a.shape) for a in accs]
    out_shape = [s for (s, _, _) in row_outs] + list(accs)
    return pl.pallas_call(
        kern, name=name, grid=(n,), in_specs=in_specs, out_specs=out_specs, out_shape=out_shape,
        input_output_aliases=alias or {}, compiler_params=_params(),
    )(*[a for (a, _, _) in row_ins], *pars)


def _rowblk(width, colblk=0, roff=0, tile=TR):
    return (tile, width), (lambda i: (i + roff, colblk))


def _sds(shape, dtype):
    return jax.ShapeDtypeStruct(shape, dtype)


class _Carry:
    SAME_CORE = (1, 3, 5)

    def __init__(self, jobs):
        self.jobs = list(jobs)
        self.arrays = [a for _, a in self.jobs]
        self.out_shapes = [_sds(a.shape if kind == "scatter" else (N_DEV, *a.shape), a.dtype) for kind, a in self.jobs]
        n = len(self.jobs)
        self.scratch = [pltpu.SemaphoreType.DMA((n, 7)), pltpu.SemaphoreType.DMA((n, 7)), pltpu.SemaphoreType.DMA((n,))]

    def _setup(self, sems):
        send_sems, recv_sems, local_sems = sems
        x, y, c = _place()
        me = _lin(x, y, c)
        peers = [(x ^ ((k + 1) >> 2 & 1), y ^ ((k + 1) >> 1 & 1), c ^ ((k + 1) & 1)) for k in range(7)]

        def copy(a, k, sem_k, src, dst):
            return pltpu.make_async_remote_copy(src_ref=src, dst_ref=dst, send_sem=send_sems.at[a, sem_k],
                                                recv_sem=recv_sems.at[a, sem_k], device_id=peers[k], device_id_type=MESH)

        return me, [_lin(*p) for p in peers], copy, local_sems

    def _local(self, a, kind, ins, outs, me, local_sems):
        return pltpu.make_async_copy(ins[a].at[me] if kind == "scatter" else ins[a], outs[a].at[me], local_sems.at[a])

    def start(self, ins, outs, sems):
        me, theirs, copy, local_sems = self._setup(sems)
        for a, (kind, _) in enumerate(self.jobs):
            self._local(a, kind, ins, outs, me, local_sems).start()
            if kind == "scatter":
                for k in range(7):
                    copy(a, k, k, ins[a].at[theirs[k]], outs[a].at[me]).start()
            else:
                for k in (0,) + self.SAME_CORE:
                    copy(a, k, k, ins[a], outs[a].at[me]).start()

    def wait(self, ins, outs, sems):
        me, theirs, copy, local_sems = self._setup(sems)
        for a, (kind, _) in enumerate(self.jobs):
            if kind == "scatter":
                for k in range(7):
                    copy(a, k, k, ins[a].at[me], outs[a].at[theirs[k]]).wait_recv()
                for k in range(7):
                    copy(a, k, k, ins[a].at[theirs[k]], outs[a].at[me]).wait_send()
            else:
                for k in self.SAME_CORE:
                    blk = outs[a].at[theirs[k]]
                    copy(a, k, k, ins[a], blk).wait_recv()
                    copy(a, 0, k + 1, blk, blk).start()
                copy(a, 0, 0, ins[a], outs[a].at[theirs[0]]).wait_recv()
                for k in self.SAME_CORE:
                    copy(a, 0, k + 1, ins[a], outs[a].at[theirs[k + 1]]).wait_recv()
                for k in (0,) + self.SAME_CORE:
                    copy(a, k, k, ins[a], outs[a].at[me]).wait_send()
                for k in self.SAME_CORE:
                    blk = outs[a].at[theirs[k]]
                    copy(a, 0, k + 1, blk, blk).wait_send()
            self._local(a, kind, ins, outs, me, local_sems).wait()


def _carried(kern, carry, n_in, n_out, first, last):
    if carry is None:
        return kern
    nc = len(carry.jobs)

    def wrapped(*refs):
        ins, cin = refs[:n_in], refs[n_in:n_in + nc]
        outs, cout = refs[n_in + nc:n_in + nc + n_out], refs[n_in + nc + n_out:n_in + 2 * nc + n_out]
        scr, sems = refs[n_in + 2 * nc + n_out:-3], refs[-3:]

        @pl.when(first())
        def _():
            carry.start(cin, cout, sems)

        kern(*ins, *outs, *scr)

        @pl.when(last())
        def _():
            carry.wait(cin, cout, sems)

    return wrapped


def _carry_args(carry):
    if carry is None:
        return [], [], [], [], []
    n = len(carry.jobs)
    return [ANY] * n, carry.arrays, [ANY] * n, carry.out_shapes, carry.scratch


def _grid_ends(dims):
    first = lambda: functools.reduce(jnp.logical_and, [pl.program_id(d) == 0 for d in range(len(dims))])
    last = lambda: functools.reduce(jnp.logical_and, [pl.program_id(d) == n - 1 for d, n in enumerate(dims)])
    return first, last


def _mm_call(name, a, b, mode, out_dtype, tm, tn, rows_outer=True, single_b=False, carry=None):
    if mode == "TN":
        (K, M), N = a.shape, b.shape[1]
    else:
        (M, K), N = a.shape, (b.shape[1] if mode == "NN" else b.shape[0])
    assert M % tm == 0 and N % tn == 0, (name, M, N, K, tm, tn)
    ij = (lambda g0, g1: (g0, g1)) if rows_outer else (lambda g0, g1: (g1, g0))
    grid = (M // tm, N // tn) if rows_outer else (N // tn, M // tm)
    if mode == "TN":
        a_spec = pl.BlockSpec((K, tm), lambda g0, g1: (0, ij(g0, g1)[0]))
    else:
        a_spec = pl.BlockSpec((tm, K), lambda g0, g1: (ij(g0, g1)[0], 0))
    b_blk, b_map = ((tn, K), lambda g0, g1: (ij(g0, g1)[1], 0)) if mode == "NT" else \
                   ((K, tn), lambda g0, g1: (0, ij(g0, g1)[1]))
    b_spec = pl.BlockSpec(b_blk, b_map, pipeline_mode=pl.Buffered(1)) if single_b else pl.BlockSpec(b_blk, b_map)
    dims = {"NN": NN, "NT": NT, "TN": TN}[mode]

    def kern(a_ref, b_ref, o_ref):
        o_ref[...] = _dot(a_ref[...], b_ref[...], dims).astype(o_ref.dtype)

    ci, ca, co, cs, cscr = _carry_args(carry)
    res = pl.pallas_call(
        _carried(kern, carry, 2, 1, *_grid_ends(grid)), name=name, grid=grid, in_specs=[a_spec, b_spec] + ci,
        out_specs=[pl.BlockSpec((tm, tn), lambda g0, g1: ij(g0, g1))] + co,
        out_shape=[_sds((M, N), out_dtype)] + cs, scratch_shapes=cscr,
        compiler_params=_params(("arbitrary", "arbitrary")),
    )(a, b, *ca)
    return res[0] if carry is None else (res[0], res[1:])


def _mm_act(name, a, w, mode, out_dtype=BF16, carry=None):
    rows, K = a.shape
    N = w.shape[1] if mode == "NN" else w.shape[0]
    if K > D_FF:
        return _mm_call(name, a, w, mode, out_dtype, rows // 8, N, single_b=True, carry=carry)
    tn = N if N <= 1024 else 1408
    return _mm_call(name, a, w, mode, out_dtype, rows // 4, tn, carry=carry)


def _mm_wgrad(name, x, dy, out_dtype=BF16, carry=None):
    M = x.shape[1]
    tm = 1408 if M == D_FF else 512
    return _mm_call(name, x, dy, "TN", out_dtype, tm, dy.shape[1], single_b=True, carry=carry)


def _sigmoid(x):
    return 0.5 * jnp.tanh(0.5 * x) + 0.5


def _silu(x):
    return x * _sigmoid(x)


def _silu_grad(x):
    s = _sigmoid(x)
    return s * (1.0 + x * (1.0 - s))


_GELU_K = math.sqrt(2.0 / math.pi)


def _gelu(x):
    return 0.5 * x * (1.0 + jnp.tanh(_GELU_K * (x + 0.044715 * x * x * x)))


def _gelu_grad(x):
    t = jnp.tanh(_GELU_K * (x + 0.044715 * x * x * x))
    return 0.5 * (1.0 + t) + 0.5 * x * (1.0 - t * t) * _GELU_K * (1.0 + 3.0 * 0.044715 * x * x)


def _log_sigmoid(x):
    return jnp.minimum(x, 0.0) - jnp.log(1.0 + jnp.exp(-jnp.abs(x)))


def _rms(x):
    x = x.astype(F32)
    r = lax.rsqrt(jnp.mean(x * x, axis=-1, keepdims=True) + EPS)
    return x * r, r


def _rms_bwd(dy, y, r):
    return r * (dy - y * jnp.mean(dy * y, axis=-1, keepdims=True))


def _modrow(mod_ref, i, chunk):
    lo = mod_ref[0:1, chunk * D:(chunk + 1) * D]
    hi = mod_ref[1:2, chunk * D:(chunk + 1) * D]
    return jnp.where(i == 0, lo, hi)


def _acc_seg(acc_ref, i, val):
    zero = jnp.zeros_like(val)
    acc_ref[0:1, :] += jnp.where(i == 0, val, zero)
    acc_ref[1:2, :] += jnp.where(i == 0, zero, val)


def _colsum(x):
    return jnp.sum(x, axis=0, keepdims=True)


SH1, SC1, GA1, SH2, SC2, GA2 = range(6)


def _normmod_fwd(name, xa, g, mod, c_sh, c_sc):
    T = xa.shape[0]

    def body(i, ins, ps, outs, acc):
        y, _ = _rms(ins[0][...])
        h = (y * ps[0][...]) * (1.0 + _modrow(ps[1], i, c_sc)) + _modrow(ps[1], i, c_sh)
        outs[0][...] = h.astype(BF16)

    return _ew(name, body, T // TR, [(xa, *_rowblk(D))], [g, mod], [(_sds((T, D), BF16), *_rowblk(D))])[0]


def _modrows(mod_ref, row0, n, chunk):
    t = row0 + lax.broadcasted_iota(jnp.int32, (n, 1), 0)
    return jnp.where(t < CTX, mod_ref[0:1, chunk * D:(chunk + 1) * D], mod_ref[1:2, chunk * D:(chunk + 1) * D])


def _loss_resid_bwd(name, x_out, target, mat, gpost, mod, c_ga):
    T = x_out.shape[0]

    def body(i, ins, ps, outs, acc):
        err = ins[0][...] - ins[1][...]
        lat = i > 0
        dx = jnp.where(lat, err * (1.0 / D), 0.0)
        outs[0][...] = dx
        acc[2][...] += jnp.where(lat, _colsum(err * err), 0.0)
        outs[1][...] = _resid_bwd_vals(i, dx, ins[2][...], ps[0][...], ps[1], c_ga, acc[0], acc[1]).astype(BF16)

    tgt_blk = ((TR, D), lambda i: (jnp.maximum(i - 1, 0), 0))
    return _ew(name, body, T // TR, [(x_out, *_rowblk(D)), (target, *tgt_blk), (mat, *_rowblk(D))], [gpost, mod],
               [(_sds((T, D), F32), *_rowblk(D)), (_sds((T, D), BF16), *_rowblk(D))],
               [_sds((2, D), F32), _sds((1, D), F32), _sds((1, D), F32)])


def _mod_for(mod_ref, i, chunk, row0, n):
    return _modrow(mod_ref, i, chunk) if row0 is None else _modrows(mod_ref, row0, n, chunk)


def _acc_for(acc_ref, i, v, row0):
    if row0 is None:
        _acc_seg(acc_ref, i, _colsum(v))
        return

    @pl.when(row0 < CTX)
    def _():
        is_ctx = row0 + lax.broadcasted_iota(jnp.int32, (v.shape[0], 1), 0) < CTX
        acc_ref[0:1, :] += _colsum(jnp.where(is_ctx, v, 0.0))
        acc_ref[1:2, :] += _colsum(jnp.where(is_ctx, 0.0, v))

    @pl.when(row0 >= CTX)
    def _():
        acc_ref[1:2, :] += _colsum(v)


def _resid_bwd_vals(i, dout, mat, gpost, mod_ref, c_ga, acc_ga, acc_g, row0=None):
    ym, rm = _rms(mat)
    ga = _mod_for(mod_ref, i, c_ga, row0, dout.shape[0])
    _acc_for(acc_ga, i, dout * (ym * gpost), row0)
    dn = dout * ga
    acc_g[...] += _colsum(dn * ym)
    return _rms_bwd(dn * gpost, ym, rm)


def _normmod_bwd_vals(i, dh, xin, g, mod_ref, c_sh, c_sc, acc_sh, acc_sc, acc_g, row0=None):
    dh = dh.astype(F32)
    y, r = _rms(xin)
    _acc_for(acc_sc, i, dh * (y * g), row0)
    _acc_for(acc_sh, i, dh, row0)
    dyg = dh * (1.0 + _mod_for(mod_ref, i, c_sc, row0, dh.shape[0]))
    acc_g[...] += _colsum(dyg * y)
    return _rms_bwd(dyg * g, y, r)


def _parts(i, tm):
    return [(slice(0, tm), i * tm)]


FT = 1408


def _ffn_in_fused(name, h2, w_t, carry=None):
    T = h2.shape[0]
    tm, nj = T // 4, D_FF // FT

    def kern(a_ref, bg_ref, bu_ref, fg_ref, fu_ref, s_ref):
        for rows, _ in _parts(0, tm):
            a = a_ref[rows, :]
            g = _dot(a, bg_ref[...], NT)
            u = _dot(a, bu_ref[...], NT)
            fg_ref[rows, :] = g.astype(BF16)
            fu_ref[rows, :] = u.astype(BF16)
            s_ref[rows, :] = (_silu(g) * u).astype(BF16)

    o_spec = pl.BlockSpec((tm, FT), lambda i, j: (i, j))
    ci, ca, co, cs, cscr = _carry_args(carry)
    res = pl.pallas_call(
        _carried(kern, carry, 3, 3, *_grid_ends((4, nj))), name=name, grid=(4, nj),
        in_specs=[pl.BlockSpec((tm, D), lambda i, j: (i, 0)), pl.BlockSpec((FT, D), lambda i, j: (j, 0)),
                  pl.BlockSpec((FT, D), lambda i, j: (j + nj, 0))] + ci,
        out_specs=[o_spec] * 3 + co, out_shape=[_sds((T, D_FF), BF16)] * 3 + cs, scratch_shapes=cscr,
        compiler_params=_params(("arbitrary", "arbitrary")),
    )(h2, w_t, w_t, *ca)
    return res if carry is None else (res[:3], res[3:])


def _norm_chain(row0, xin, mat, gpost, mod_ref, c_ga, gnext, modn_ref, c_sh, c_sc):
    n = xin.shape[0]
    ym, _ = _rms(mat.astype(BF16))
    xo = xin + _modrows(mod_ref, row0, n, c_ga) * (ym * gpost)
    y, _ = _rms(xo)
    h = (y * gnext) * (1.0 + _modrows(modn_ref, row0, n, c_sc)) + _modrows(modn_ref, row0, n, c_sh)
    return xo, h.astype(BF16)


def _out_fused(name, p, u, o_all, xa, w_o_rnn, w_o_attn, w_out, gpost, mod, gnext):
    T = u.shape[0]
    tm = T // 8

    def kern(g0, g1, g2, g3, u_ref, o_ref, xa_ref, wr_ref, wa_ref, w_ref, gpost_ref, mod_ref, gnext_ref,
             ya_ref, yb_ref, z_ref, m_ref, x1_ref, h2_ref):
        for rows, row0 in _parts(pl.program_id(0), tm):
            ya = _dot(u_ref[rows, :], wr_ref[...]).astype(BF16)
            yb = _dot(o_ref[rows, :], wa_ref[...]).astype(BF16)
            ya_ref[rows, :] = ya
            yb_ref[rows, :] = yb
            ga = _sigmoid(jnp.concatenate([g0[rows, :], g1[rows, :]], axis=1).astype(F32))
            gb = _sigmoid(jnp.concatenate([g2[rows, :], g3[rows, :]], axis=1).astype(F32))
            z = (ga * ya.astype(F32) + gb * yb.astype(F32)).astype(BF16)
            z_ref[rows, :] = z
            m = _dot(z, w_ref[...])
            m_ref[rows, :] = m.astype(BF16)
            x1_ref[rows, :], h2_ref[rows, :] = _norm_chain(row0, xa_ref[rows, :], m, gpost_ref[...], mod_ref, GA1,
                                                           gnext_ref[...], mod_ref, SH2, SC2)

    row = lambda w: pl.BlockSpec((tm, w), lambda i: (i, 0))
    return pl.pallas_call(
        kern, name=name, grid=(T // tm,),
        in_specs=[pl.BlockSpec((tm, GLB), lambda i, q=q: (i, COL_GL // GLB + q)) for q in range(4)]
                 + [row(D), row(D), row(D)] + [_full_spec(a.shape) for a in (w_o_rnn, w_o_attn, w_out, gpost, mod, gnext)],
        out_specs=[row(D)] * 6,
        out_shape=[_sds((T, D), BF16)] * 4 + [_sds((T, D), F32), _sds((T, D), BF16)],
        compiler_params=_params(),
    )(p, p, p, p, u, o_all, xa, w_o_rnn, w_o_attn, w_out, gpost, mod, gnext)


def _ffn_out_fused(name, s, w, x1, gpost, mod, nxt=None):
    T = s.shape[0]
    tm = T // 8

    def kern(s_ref, w_ref, x1_ref, gpost_ref, mod_ref, *rest):
        for rows, row0 in _parts(pl.program_id(0), tm):
            e = _dot(s_ref[rows, :], w_ref[...])
            if nxt is None:
                e_ref, xo_ref = rest
                ym, _ = _rms(e.astype(BF16))
                xo_ref[rows, :] = x1_ref[rows, :] + _modrows(mod_ref, row0, e.shape[0], GA2) * (ym * gpost_ref[...])
            else:
                gnext_ref, modn_ref, e_ref, xo_ref, h_ref = rest
                xo_ref[rows, :], h_ref[rows, :] = _norm_chain(row0, x1_ref[rows, :], e, gpost_ref[...], mod_ref, GA2,
                                                              gnext_ref[...], modn_ref, SH1, SC1)
            e_ref[rows, :] = e.astype(BF16)

    row = lambda w_: pl.BlockSpec((tm, w_), lambda i: (i, 0))
    extra = [] if nxt is None else list(nxt)
    return pl.pallas_call(
        kern, name=name, grid=(T // tm,),
        in_specs=[row(D_FF), _full_spec(w.shape), row(D), _full_spec(gpost.shape), _full_spec(mod.shape)]
                 + [_full_spec(a.shape) for a in extra],
        out_specs=[row(D)] * (2 if nxt is None else 3),
        out_shape=[_sds((T, D), BF16), _sds((T, D), F32)] + ([] if nxt is None else [_sds((T, D), BF16)]),
        compiler_params=_params(),
    )(s, w, x1, gpost, mod, *extra)


def _ffn_bwd_fused(name, fg, fu, w, de=None, head=None):
    T = fg.shape[0]
    tm = T // 8
    row = lambda w_: pl.BlockSpec((tm, w_), lambda i: (i, 0))
    w_spec = pl.BlockSpec(w.shape, lambda i: (0, 0), pipeline_mode=pl.Buffered(1))

    def tail(rows, de_v, fg_ref, fu_ref, w_ref, df_ref):
        ds = _dot(de_v, w_ref[...], NT)
        g, u = fg_ref[rows, :].astype(F32), fu_ref[rows, :].astype(F32)
        df_ref[rows, :] = jnp.concatenate([ds * u * _silu_grad(g), ds * _silu(g)], axis=1).astype(BF16)

    if head is None:
        def kern(de_ref, fg_ref, fu_ref, w_ref, df_ref):
            for rows, _ in _parts(pl.program_id(0), tm):
                tail(rows, de_ref[rows, :], fg_ref, fu_ref, w_ref, df_ref)

        return pl.pallas_call(
            kern, name=name, grid=(T // tm,), in_specs=[row(D), row(D_FF), row(D_FF), w_spec],
            out_specs=[row(2 * D_FF)], out_shape=[_sds((T, 2 * D_FF), BF16)], compiler_params=_params(),
        )(de, fg, fu, w)

    dx2, e, gpost, mod = head

    def kern(dx_ref, e_ref, fg_ref, fu_ref, w_ref, gpost_ref, mod_ref, de_ref, df_ref, dga_ref, dg_ref):
        i = pl.program_id(0)

        @pl.when(i == 0)
        def _():
            dga_ref[...] = jnp.zeros(dga_ref.shape, F32)
            dg_ref[...] = jnp.zeros(dg_ref.shape, F32)

        for rows, row0 in _parts(i, tm):
            de_v = _resid_bwd_vals(i, dx_ref[rows, :], e_ref[rows, :], gpost_ref[...], mod_ref, GA2, dga_ref, dg_ref,
                                   row0=row0).astype(BF16)
            de_ref[rows, :] = de_v
            tail(rows, de_v, fg_ref, fu_ref, w_ref, df_ref)

    return pl.pallas_call(
        kern, name=name, grid=(T // tm,),
        in_specs=[row(D), row(D), row(D_FF), row(D_FF), w_spec, _full_spec(gpost.shape), _full_spec(mod.shape)],
        out_specs=[row(D), row(2 * D_FF), _full_spec((2, D)), _full_spec((1, D))],
        out_shape=[_sds((T, D), BF16), _sds((T, 2 * D_FF), BF16), _sds((2, D), F32), _sds((1, D), F32)],
        compiler_params=_params(),
    )(dx2, e, fg, fu, w, gpost, mod)


def _zero_at_start(i, refs):
    @pl.when(i == 0)
    def _():
        for r in refs:
            r[...] = jnp.zeros(r.shape, F32)


def _proj_bwd_fused(name, dp, dgl, w_in_t, xa, dx1, gpre, mod, carry=None):
    T = dp.shape[0]
    tm = T // 8
    row = lambda w_: pl.BlockSpec((tm, w_), lambda i: (i, 0))

    def kern(dp_ref, dgl_ref, w_ref, xa_ref, dx1_ref, g_ref, mod_ref, dxa_ref, dsh_ref, dsc_ref, dg_ref):
        i = pl.program_id(0)
        _zero_at_start(i, (dsh_ref, dsc_ref, dg_ref))
        for rows, row0 in _parts(i, tm):
            dh = _dot(dp_ref[rows, :], w_ref[0:DP_W, :]) + _dot(dgl_ref[rows, :], w_ref[DP_W:, :])
            dxa_ref[rows, :] = dx1_ref[rows, :] + _normmod_bwd_vals(i, dh, xa_ref[rows, :], g_ref[...], mod_ref, SH1,
                                                                    SC1, dsh_ref, dsc_ref, dg_ref, row0=row0)

    ci, ca, co, cs, cscr = _carry_args(carry)
    res = pl.pallas_call(
        _carried(kern, carry, 7, 4, *_grid_ends((T // tm,))), name=name, grid=(T // tm,),
        in_specs=[row(DP_W), row(P_W - DP_W),
                  pl.BlockSpec(w_in_t.shape, lambda i: (0, 0), pipeline_mode=pl.Buffered(1)), row(D), row(D),
                  _full_spec(gpre.shape), _full_spec(mod.shape)] + ci,
        out_specs=[row(D), _full_spec((2, D)), _full_spec((2, D)), _full_spec((1, D))] + co,
        out_shape=[_sds((T, D), F32), _sds((2, D), F32), _sds((2, D), F32), _sds((1, D), F32)] + cs,
        scratch_shapes=cscr, compiler_params=_params(),
    )(dp, dgl, w_in_t, xa, dx1, gpre, mod, *ca)
    return res if carry is None else (res[:4], res[4:])


def _proj_wgrad(name, dp, dgl, h, carry=None):
    T, N = h.shape
    n1, n2 = DP_W // GLB, (P_W - DP_W) // GLB

    def kern(a1_ref, a2_ref, h_ref, o_ref):
        i = pl.program_id(0)

        @pl.when(i < n1)
        def _():
            o_ref[...] = _dot(a1_ref[...], h_ref[...], TN).astype(o_ref.dtype)

        @pl.when(i >= n1)
        def _():
            o_ref[...] = _dot(a2_ref[...], h_ref[...], TN).astype(o_ref.dtype)

    ci, ca, co, cs, cscr = _carry_args(carry)
    res = pl.pallas_call(
        _carried(kern, carry, 3, 1, *_grid_ends((n1 + n2,))), name=name, grid=(n1 + n2,),
        in_specs=[pl.BlockSpec((T, GLB), lambda i: (0, jnp.minimum(i, n1 - 1))),
                  pl.BlockSpec((T, GLB), lambda i: (0, jnp.maximum(i - n1, 0))),
                  pl.BlockSpec((T, N), lambda i: (0, 0), pipeline_mode=pl.Buffered(1))] + ci,
        out_specs=[pl.BlockSpec((GLB, N), lambda i: (i, 0))] + co,
        out_shape=[_sds((P_W, N), BF16)] + cs, scratch_shapes=cscr, compiler_params=_params(),
    )(dp, dgl, h, *ca)
    return res[0] if carry is None else (res[0], res[1:])


def _ffn_in_bwd_fused(name, df, w_t, x1, dres, mat, gpre, mod, gpost, carry=None):
    T = df.shape[0]
    tm = T // 8
    row = lambda w_: pl.BlockSpec((tm, w_), lambda i: (i, 0))

    def kern(df_ref, w_ref, x1_ref, dres_ref, mat_ref, gpre_ref, mod_ref, gpost_ref,
             dx1_ref, dm_ref, dsh_ref, dsc_ref, dgpre_ref, dga_ref, dgpost_ref):
        i = pl.program_id(0)
        _zero_at_start(i, (dsh_ref, dsc_ref, dgpre_ref, dga_ref, dgpost_ref))
        for rows, row0 in _parts(i, tm):
            dh2 = _dot(df_ref[rows, :], w_ref[...])
            dx1 = dres_ref[rows, :] + _normmod_bwd_vals(i, dh2, x1_ref[rows, :], gpre_ref[...], mod_ref, SH2, SC2,
                                                        dsh_ref, dsc_ref, dgpre_ref, row0=row0)
            dx1_ref[rows, :] = dx1
            dm_ref[rows, :] = _resid_bwd_vals(i, dx1, mat_ref[rows, :], gpost_ref[...], mod_ref, GA1, dga_ref,
                                              dgpost_ref, row0=row0).astype(BF16)

    ci, ca, co, cs, cscr = _carry_args(carry)
    res = pl.pallas_call(
        _carried(kern, carry, 8, 7, *_grid_ends((T // tm,))), name=name, grid=(T // tm,),
        in_specs=[row(2 * D_FF), pl.BlockSpec(w_t.shape, lambda i: (0, 0), pipeline_mode=pl.Buffered(1)), row(D),
                  row(D), row(D), _full_spec(gpre.shape), _full_spec(mod.shape), _full_spec(gpost.shape)] + ci,
        out_specs=[row(D), row(D), _full_spec((2, D)), _full_spec((2, D)), _full_spec((1, D)), _full_spec((2, D)),
                   _full_spec((1, D))] + co,
        out_shape=[_sds((T, D), F32), _sds((T, D), BF16), _sds((2, D), F32), _sds((2, D), F32), _sds((1, D), F32),
                   _sds((2, D), F32), _sds((1, D), F32)] + cs,
        scratch_shapes=cscr, compiler_params=_params(),
    )(df, w_t, x1, dres, mat, gpre, mod, gpost, *ca)
    return res if carry is None else (res[:7], res[7:])


def _out_bwd_fused(name, dm, w_out, w_o_rnn, w_o_attn, p, ya, yb):
    T = dm.shape[0]
    tm = T // 8
    row = lambda w_: pl.BlockSpec((tm, w_), lambda i: (i, 0))

    def kern(dm_ref, w_ref, wr_ref, wa_ref, g0, g1, g2, g3, ya_ref, yb_ref, dya_ref, dyb_ref, dgl_ref, du_ref, do_ref):
        for rows, _ in _parts(pl.program_id(0), tm):
            dz = _dot(dm_ref[rows, :], w_ref[...], NT)
            ga = _sigmoid(jnp.concatenate([g0[rows, :], g1[rows, :]], axis=1).astype(F32))
            gb = _sigmoid(jnp.concatenate([g2[rows, :], g3[rows, :]], axis=1).astype(F32))
            dya = (dz * ga).astype(BF16)
            dyb = (dz * gb).astype(BF16)
            dya_ref[rows, :] = dya
            dyb_ref[rows, :] = dyb
            dgl_ref[rows, :] = jnp.concatenate([dz * ya_ref[rows, :].astype(F32) * ga * (1.0 - ga),
                                                dz * yb_ref[rows, :].astype(F32) * gb * (1.0 - gb)],
                                               axis=1).astype(BF16)
            du_ref[rows, :] = _dot(dya, wr_ref[...], NT).astype(BF16)
            do_ref[rows, :] = _dot(dyb, wa_ref[...], NT).astype(BF16)

    return pl.pallas_call(
        kern, name=name, grid=(T // tm,),
        in_specs=[row(D)] + [_full_spec(w.shape) for w in (w_out, w_o_rnn, w_o_attn)]
                 + [pl.BlockSpec((tm, GLB), lambda i, q=q: (i, COL_GL // GLB + q)) for q in range(4)] + [row(D), row(D)],
        out_specs=[row(D), row(D), row(2 * D), row(D), row(D)],
        out_shape=[_sds((T, D), BF16), _sds((T, D), BF16), _sds((T, 2 * D), BF16), _sds((T, D), BF16),
                   _sds((T, D), BF16)],
        compiler_params=_params(),
    )(dm, w_out, w_o_rnn, w_o_attn, p, p, p, p, ya, yb)


AB = 128
CTX_BLKS = CTX // AB


def _rope_tables(S):
    pos = jnp.arange(S, dtype=jnp.int32)
    inv = ROPE_BASE ** (-jnp.arange(N_FREQ, dtype=F32) / N_FREQ)
    ang_r = (pos // GRID_W).astype(F32)[:, None] * inv[None, :]
    ang_c = (pos % GRID_W).astype(F32)[:, None] * inv[None, :]
    cos = jnp.concatenate([jnp.cos(ang_r)] * 2 + [jnp.cos(ang_c)] * 2, axis=1)
    sin = jnp.concatenate([-jnp.sin(ang_r), jnp.sin(ang_r), -jnp.sin(ang_c), jnp.sin(ang_c)], axis=1)
    return cos, sin


def _rope(x, cos, sin):
    w = x.shape[1]
    reps = w // HEAD
    lane = lax.broadcasted_iota(jnp.int32, x.shape, 1)
    partner = jnp.where((lane & 63) < 32, pltpu.roll(x, w - 32, 1), pltpu.roll(x, 32, 1))
    return x * jnp.tile(cos, (1, reps)) + partner * jnp.tile(sin, (1, reps))


def _unrope(dx, cos, sin):
    w = dx.shape[1]
    reps = w // HEAD
    lane = lax.broadcasted_iota(jnp.int32, dx.shape, 1)
    t = dx * jnp.tile(sin, (1, reps))
    partner = jnp.where((lane & 63) < 32, pltpu.roll(t, w - 32, 1), pltpu.roll(t, 32, 1))
    return dx * jnp.tile(cos, (1, reps)) + partner


def _qkv_prep(name, p, cos, sin, S):
    T = CTX + S
    nt = T // AB
    KW = N_KV * HEAD

    def with_ones(v):
        ones = jnp.ones((AB, HEAD), BF16)
        return jnp.concatenate([v[:, kh * HEAD:(kh + 1) * HEAD] if part == 0 else ones
                                for kh in range(N_KV) for part in range(2)], axis=1)

    def kern(q_ref, k_ref, v_ref, cos_ref, sin_ref, qa_ref, kp_ref, vp_ref, kc_ref, vc_ref):
        i = pl.program_id(0)
        cos_v, sin_v = cos_ref[...], sin_ref[...]
        @pl.when(i < CTX_BLKS)
        def _():
            qa_ref[...] = (q_ref[...].astype(F32) * ATT_SCALE).astype(BF16)
            kc_ref[...] = k_ref[...]
            vc_ref[...] = with_ones(v_ref[...])

        @pl.when((i < CTX_BLKS) | (i >= nt))
        def _():
            kp_ref[...] = jnp.zeros(kp_ref.shape, BF16)
            vp_ref[...] = jnp.zeros(vp_ref.shape, BF16)

        @pl.when((i >= CTX_BLKS) & (i < nt))
        def _():
            qa_ref[...] = (_rope(q_ref[...].astype(F32), cos_v, sin_v) * ATT_SCALE).astype(BF16)
            kp_ref[...] = _rope(k_ref[...].astype(F32), cos_v, sin_v).astype(BF16)
            vp_ref[...] = with_ones(v_ref[...])

    tok = lambda i: jnp.minimum(i, nt - 1)
    lat_map = lambda i: (jnp.clip(i - CTX_BLKS, 0, nt - CTX_BLKS - 1), 0)
    ctx_map = lambda i: (jnp.minimum(i, CTX_BLKS - 1), 0)
    return pl.pallas_call(
        kern, name=name, grid=(nt + CTX_BLKS,),
        in_specs=[pl.BlockSpec((AB, N_Q * HEAD), lambda i: (tok(i), COL_Q // (N_Q * HEAD))),
                  pl.BlockSpec((AB, KW), lambda i: (tok(i), COL_K // KW)),
                  pl.BlockSpec((AB, KW), lambda i: (tok(i), COL_V // KW)),
                  pl.BlockSpec((AB, HEAD), lat_map), pl.BlockSpec((AB, HEAD), lat_map)],
        out_specs=[pl.BlockSpec((AB, N_Q * HEAD), lambda i: (tok(i), 0)),
                   pl.BlockSpec((AB, KW), lambda i: (i, 0)), pl.BlockSpec((AB, 2 * KW), lambda i: (i, 0)),
                   pl.BlockSpec((AB, KW), ctx_map), pl.BlockSpec((AB, 2 * KW), ctx_map)],
        out_shape=[_sds((T, N_Q * HEAD), BF16), _sds((S + 2 * CTX, KW), BF16), _sds((S + 2 * CTX, 2 * KW), BF16),
                   _sds((CTX, KW), BF16), _sds((CTX, 2 * KW), BF16)],
        compiler_params=_params(),
    )(p, p, p, cos, sin)


GW = Q_PER_KV * HEAD
HG = Q_PER_KV


def _band_bias(S):
    r = jnp.arange(AB, dtype=jnp.int32)[:, None]
    c = jnp.arange(3 * AB, dtype=jnp.int32)[None, :]
    near = jnp.abs(c - AB - r) <= AB
    valid = jnp.stack([near & (c >= AB), near, near & (c < 2 * AB)])
    return jnp.where(valid, 0.0, NEG_INF).astype(F32)


def _bias_spec(S):
    nb = S // AB
    return pl.BlockSpec((None, AB, 3 * AB), lambda kh, n: (jnp.where(n == 0, 0, jnp.where(n == nb - 1, 2, 1)), 0, 0))


def _head_probs(q, sink, kc, vce, kb, vbe, bias):
    s_c = _dot(q, kc, NT)
    m = jnp.maximum(jnp.max(s_c, axis=-1, keepdims=True), sink)
    if kb is not None:
        s_b = _dot(q, kb, NT) + bias
        m = jnp.maximum(m, jnp.max(s_b, axis=-1, keepdims=True))
    p_c = jnp.exp(s_c - m).astype(BF16)
    acc = _dot(p_c, vce)
    p_b = None
    if kb is not None:
        p_b = jnp.exp(s_b - m).astype(BF16)
        acc = acc + _dot(p_b, vbe)
    return p_c, p_b, m, acc


def _attn_fwd(name, qa, kc, vc, sink4, S, band=None, prev=None, carry=None):
    T = qa.shape[0]
    has_band = band is not None
    nq = S // AB if has_band else CTX_BLKS
    q_off = CTX_BLKS if has_band else 0

    def kern(*refs):
        q_ref, kc_ref, vc_ref, sink_ref = refs[:4]
        rest = refs[4:]
        o_ref = rest[-1]
        n = pl.program_id(1)
        kc_v, vce = kc_ref[...], vc_ref[...]
        kb = vbe = bias = None
        if has_band:
            kp_ref, vp_ref, bias_ref = rest[:3]
            start = pl.multiple_of(n * AB + (CTX - AB), AB)
            kb = kp_ref[pl.ds(start, 3 * AB), :]
            vbe = vp_ref[pl.ds(start, 3 * AB), :]
            bias = bias_ref[...]
        outs = []
        for g in range(Q_PER_KV):
            sink = sink_ref[g:g + 1, 0:1]
            _, _, m, acc = _head_probs(q_ref[:, g * HEAD:(g + 1) * HEAD], sink, kc_v, vce, kb, vbe, bias)
            l = acc[:, HEAD:] + jnp.exp(sink - m)
            outs.append(acc[:, :HEAD] / l)
        o_ref[...] = jnp.concatenate(outs, axis=1).astype(BF16)

    in_specs = [pl.BlockSpec((AB, GW), lambda kh, n: (n + q_off, kh)),
                pl.BlockSpec((CTX, HEAD), lambda kh, n: (0, kh)), pl.BlockSpec((CTX, 2 * HEAD), lambda kh, n: (0, kh)),
                pl.BlockSpec((None, Q_PER_KV, HEAD), lambda kh, n: (kh, 0, 0))]
    args = [qa, kc, vc, sink4]
    if has_band:
        in_specs += [pl.BlockSpec((S + 2 * CTX, HEAD), lambda kh, n: (0, kh)),
                     pl.BlockSpec((S + 2 * CTX, 2 * HEAD), lambda kh, n: (0, kh)), _bias_spec(S)]
        args += list(band)
    alias = {}
    if prev is not None:
        in_specs.append(ANY)
        alias = {len(args): 0}
        args.append(prev)
    ci, ca, co, cs, cscr = _carry_args(carry)
    res = pl.pallas_call(
        _carried(kern, carry, len(args), 1, *_grid_ends((N_KV, nq))), name=name, grid=(N_KV, nq),
        in_specs=in_specs + ci,
        out_specs=[pl.BlockSpec((AB, GW), lambda kh, n: (n + q_off, kh))] + co,
        out_shape=[_sds((T, N_Q * HEAD), BF16)] + cs, input_output_aliases=alias, scratch_shapes=cscr,
        compiler_params=_params(("arbitrary", "arbitrary")),
    )(*args, *ca)
    return res[0] if carry is None else (res[0], res[1:])


def _attn_bwd(name, qa, kc, vc, sink4, o_all, do_all, S, band=None, prev_dq=None, carry=None):
    T = qa.shape[0]
    has_band = band is not None
    nq = S // AB if has_band else CTX_BLKS
    q_off = CTX_BLKS if has_band else 0
    KW = N_KV * HEAD

    def kern(*refs):
        q_ref, kc_ref, vc_ref, sink_ref, o_ref, do_ref = refs[:6]
        rest = refs[6:]
        if has_band:
            kp_ref, vp_ref, bias_ref = rest[:3]
            rest = rest[3:]
        if prev_dq is not None:
            rest = rest[1:]
        dq_ref, dkc_ref, dvc_ref, dsink_ref = rest[:4]
        n = pl.program_id(1)

        @pl.when(n == 0)
        def _():
            dkc_ref[...] = jnp.zeros(dkc_ref.shape, F32)
            dvc_ref[...] = jnp.zeros(dvc_ref.shape, F32)
            dsink_ref[...] = jnp.zeros(dsink_ref.shape, F32)
            if has_band:
                rest[4][...] = jnp.zeros(rest[4].shape, F32)
                rest[5][...] = jnp.zeros(rest[5].shape, F32)

        kc_v, vce = kc_ref[...], vc_ref[...]
        vc_v = vce[:, :HEAD]
        kb = vbe = vb = bias = None
        if has_band:
            start = pl.multiple_of(n * AB + (CTX - AB), AB)
            kb = kp_ref[pl.ds(start, 3 * AB), :]
            vbe = vp_ref[pl.ds(start, 3 * AB), :]
            vb = vbe[:, :HEAD]
            bias = bias_ref[...]
        dq_parts, dsink_parts = [], []
        for g0 in range(0, Q_PER_KV, HG):
            heads = range(g0, g0 + HG)
            stack = lambda ref: jnp.concatenate([ref[:, g * HEAD:(g + 1) * HEAD] for g in heads], axis=0)
            q4, do4 = stack(q_ref), stack(do_ref)
            sink = jnp.concatenate([jnp.broadcast_to(sink_ref[g:g + 1, 0:1], (AB, 1)) for g in heads], axis=0)
            s_c = _dot(q4, kc_v, NT)
            m = jnp.maximum(jnp.max(s_c, axis=-1, keepdims=True), sink)
            if has_band:
                s_b = _dot(q4, kb, NT) + jnp.tile(bias, (HG, 1))
                m = jnp.maximum(m, jnp.max(s_b, axis=-1, keepdims=True))
            p_c = jnp.exp(s_c - m).astype(BF16).astype(F32)
            p_sink = jnp.exp(sink - m)
            l = jnp.sum(p_c, axis=-1, keepdims=True) + p_sink
            if has_band:
                p_b = jnp.exp(s_b - m).astype(BF16).astype(F32)
                l = l + jnp.sum(p_b, axis=-1, keepdims=True)
            inv = 1.0 / l
            delta = jnp.sum(do4.astype(F32) * stack(o_ref).astype(F32), axis=-1, keepdims=True)
            do4b = do4.astype(BF16)
            pn_c = (p_c * inv).astype(BF16)
            ds_c = (p_c * inv * (_dot(do4b, vc_v, NT) - delta)).astype(BF16)
            dq4 = _dot(ds_c, kc_v)
            dkc_ref[...] += _dot(q4, ds_c, TN)
            dvc_ref[...] += _dot(do4b, pn_c, TN)
            if has_band:
                pn_b = (p_b * inv).astype(BF16)
                ds_b = (p_b * inv * (_dot(do4b, vb, NT) - delta)).astype(BF16)
                dq4 = dq4 + _dot(ds_b, kb)
                rest[4][:, pl.ds(start, 3 * AB)] += _dot(q4, ds_b, TN)
                rest[5][:, pl.ds(start, 3 * AB)] += _dot(do4b, pn_b, TN)
            dq4 = dq4 * ATT_SCALE
            dq_parts += [dq4[k * AB:(k + 1) * AB, :] for k in range(HG)]
            ps = p_sink * inv * delta
            dsink_parts += [jnp.broadcast_to(-jnp.sum(ps[k * AB:(k + 1) * AB, :], axis=0, keepdims=True), (1, HEAD))
                            for k in range(HG)]
        dq_ref[...] = jnp.concatenate(dq_parts, axis=1)
        dsink_ref[...] += jnp.concatenate(dsink_parts, axis=0)

    q_spec = pl.BlockSpec((AB, GW), lambda kh, n: (n + q_off, kh))
    c_spec = pl.BlockSpec((CTX, HEAD), lambda kh, n: (0, kh))
    ce_spec = pl.BlockSpec((CTX, 2 * HEAD), lambda kh, n: (0, kh))
    s_spec = pl.BlockSpec((None, Q_PER_KV, HEAD), lambda kh, n: (kh, 0, 0))
    in_specs = [q_spec, c_spec, ce_spec, s_spec, q_spec, q_spec]
    args = [qa, kc, vc, sink4, o_all, do_all]
    ct_spec = pl.BlockSpec((HEAD, CTX), lambda kh, n: (kh, 0))
    out_specs = [q_spec, ct_spec, ct_spec, s_spec]
    out_shape = [_sds((T, N_Q * HEAD), F32), _sds((KW, CTX), F32), _sds((KW, CTX), F32), _sds((N_KV, Q_PER_KV, HEAD), F32)]
    if has_band:
        p_spec = pl.BlockSpec((S + 2 * CTX, HEAD), lambda kh, n: (0, kh))
        pt_spec = pl.BlockSpec((HEAD, S + 2 * CTX), lambda kh, n: (kh, 0))
        in_specs += [p_spec, pl.BlockSpec((S + 2 * CTX, 2 * HEAD), lambda kh, n: (0, kh)), _bias_spec(S)]
        args += list(band)
        out_specs += [pt_spec, pt_spec]
        out_shape += [_sds((KW, S + 2 * CTX), F32)] * 2
    alias = {}
    if prev_dq is not None:
        in_specs.append(ANY)
        alias = {len(args): 0}
        args.append(prev_dq)
    ci, ca, co, cs, cscr = _carry_args(carry)
    n_out = len(out_specs)
    res = pl.pallas_call(
        _carried(kern, carry, len(args), n_out, *_grid_ends((N_KV, nq))), name=name, grid=(N_KV, nq),
        in_specs=in_specs + ci, out_specs=out_specs + co, out_shape=out_shape + cs, scratch_shapes=cscr,
        input_output_aliases=alias, compiler_params=_params(("arbitrary", "arbitrary")),
    )(*args, *ca)
    return res if carry is None else (res[:n_out], res[n_out:])


def _dqkv_assemble(name, dq_all, dkp, dvp, dkc_l, dvc_l, dkc_c, dvc_c, cos, sin, S):
    T = CTX + S
    KW = N_KV * HEAD
    HALF = N_Q * HEAD // 2

    def kern(dq_ref, dkp_ref, dvp_ref, dkcl_ref, dvcl_ref, dkcc_ref, dvcc_ref, cos_ref, sin_ref, out_ref):
        i = pl.program_id(0)
        j = pl.program_id(1)
        cos_v, sin_v = cos_ref[...], sin_ref[...]

        @pl.when((j < 2) & (i == 0))
        def _():
            out_ref[...] = dq_ref[...].astype(BF16)

        @pl.when((j < 2) & (i > 0))
        def _():
            out_ref[...] = _unrope(dq_ref[...], cos_v, sin_v).astype(BF16)

        @pl.when((j == 2) & (i == 0))
        def _():
            out_ref[...] = jnp.concatenate([(dkcl_ref[...] + dkcc_ref[...]).T, (dvcl_ref[...] + dvcc_ref[...]).T],
                                           axis=1).astype(BF16)

        @pl.when((j == 2) & (i > 0))
        def _():
            out_ref[...] = jnp.concatenate([_unrope(dkp_ref[...].T, cos_v, sin_v), dvp_ref[...].T],
                                           axis=1).astype(BF16)

    same = lambda i, j: (0, i)
    lat_map = lambda i, j: (jnp.maximum(i - 1, 0), 0)
    ctx_map = lambda i, j: (0, 0)
    return pl.pallas_call(
        kern, name=name, grid=(T // TR, 3),
        in_specs=[pl.BlockSpec((TR, HALF), lambda i, j: (i, jnp.minimum(j, 1))),
                  pl.BlockSpec((KW, TR), same), pl.BlockSpec((KW, TR), same),
                  pl.BlockSpec((KW, CTX), ctx_map), pl.BlockSpec((KW, CTX), ctx_map),
                  pl.BlockSpec((KW, CTX), ctx_map), pl.BlockSpec((KW, CTX), ctx_map),
                  pl.BlockSpec((TR, HEAD), lat_map), pl.BlockSpec((TR, HEAD), lat_map)],
        out_specs=pl.BlockSpec((TR, HALF), lambda i, j: (i, COL_Q // HALF + j)),
        out_shape=_sds((T, DP_W), BF16), compiler_params=_params(("arbitrary", "arbitrary")),
    )(dq_all, dkp, dvp, dkc_l, dvc_l, dkc_c, dvc_c, cos, sin)


RB = 128
CH = 256
HALO = 8
SUB = 8
GRP = 8


def _vscan(a, b, reverse):
    row = lax.broadcasted_iota(jnp.int32, a.shape, 0)
    A, H = a, b
    for s in (1, 2, 4):
        sh = SUB - s if reverse else s
        m = (row < SUB - s) if reverse else (row >= s)
        As = pltpu.roll(A, sh, 0)
        Hs = pltpu.roll(H, sh, 0)
        H = jnp.where(m, A * Hs + H, H)
        A = jnp.where(m, A * As, A)
    return A, H


def _scan_rows(a_ref, b_ref, r0, nrows, reverse, carry, emit):
    ngrp = nrows // (SUB * GRP)
    row = lax.broadcasted_iota(jnp.int32, (SUB, RB), 0)

    def grp(gi, carry):
        g = (ngrp - 1 - gi) if reverse else gi
        base = r0 + g * (SUB * GRP)
        for v in (range(GRP - 1, -1, -1) if reverse else range(GRP)):
            rs = pl.multiple_of(base + v * SUB, SUB)
            A, H = _vscan(a_ref[pl.ds(rs, SUB), :], b_ref[pl.ds(rs, SUB), :], reverse)
            hf = H + A * carry
            if reverse:
                before = jnp.where(row == SUB - 1, carry, pltpu.roll(hf, SUB - 1, 0))
                carry = hf[0:1, :]
            else:
                before = jnp.where(row == 0, carry, pltpu.roll(hf, 1, 0))
                carry = hf[SUB - 1:SUB, :]
            emit(rs, hf, before)
        return carry

    return lax.fori_loop(0, ngrp, grp, carry)


def _pad_start(ci):
    return pl.multiple_of(ci * CH + HALO * jnp.minimum(ci, 1), HALO)


def _conv_taps(ext, transpose=False):
    n = CH + 2 * HALO
    taps = []
    for k in range(CONV_W):
        off = CONV_LEFT - k if transpose else k - CONV_LEFT
        taps.append(ext[HALO:HALO + CH, :] if off == 0 else pltpu.roll(ext, (-off) % n, 0)[HALO:HALO + CH, :])
    return taps


def _lru_gates(xl, w4, b4, ls):
    pre = _dot(xl.astype(BF16), w4) + b4
    out = []
    for d in range(2):
        r = _sigmoid(pre[:, d * RB:(d + 1) * RB])
        i = _sigmoid(pre[:, (2 + d) * RB:(3 + d) * RB])
        la = LRU_C * r * ls[d:d + 1, :]
        a = jnp.exp(la)
        q = -jnp.tanh(la) * (1.0 + a * a)
        out.append((r, i, a, q))
    return out


def _rnn_specs(T):
    col = lambda n, *_: (0, n)
    return dict(
        xr=pl.BlockSpec((T, RB), lambda n, *_: (0, COL_XR // RB + n)),
        gr=pl.BlockSpec((T, RB), lambda n, *_: (0, COL_GR // RB + n)),
        act=pl.BlockSpec((T, RB), col),
        cw=pl.BlockSpec((CONV_W, RB), col), cb=pl.BlockSpec((1, RB), col),
        w4=pl.BlockSpec((None, RB, 4 * RB), lambda n, *_: (n, 0, 0)),
        b4=pl.BlockSpec((None, 1, 4 * RB), lambda n, *_: (n, 0, 0)),
        lam=pl.BlockSpec((2, RB), col))


PAD_ROWS = 3 * HALO


def _zero_pads(pad_ref, T):
    for r in (0, HALO + CTX, 2 * HALO + T):
        pad_ref[r:r + HALO, :] = jnp.zeros((HALO, RB), F32)


def _fill_padded(pad_ref, src_ref, T):
    _zero_pads(pad_ref, T)
    pad_ref[HALO:HALO + CTX, :] = src_ref[0:CTX, :].astype(F32)
    pad_ref[2 * HALO + CTX:2 * HALO + T, :] = src_ref[CTX:T, :].astype(F32)


def _pad_rows(ci):
    return pl.ds(pl.multiple_of(ci * CH + HALO + HALO * jnp.minimum(ci, 1), HALO), CH)


def _rnn_fwd(name, p, cw, cb, w4, b4, lam, T, carry=None):
    def kern(xr_ref, gr_ref, cw_ref, cb_ref, w4_ref, b4_ref, lam_ref,
             u_ref, a0, a1, yo_ref, hpf_ref, hpb_ref, r0_ref, r1_ref, i0_ref, i1_ref, xpad, b0, b1, y):
        _fill_padded(xpad, xr_ref, T)
        ls = _log_sigmoid(lam_ref[...])
        w4v, b4v, cwv, cbv = w4_ref[...], b4_ref[...], cw_ref[...], cb_ref[...]

        def chunk(ci, _):
            rows = pl.ds(pl.multiple_of(ci * CH, CH), CH)
            taps = _conv_taps(xpad[pl.ds(_pad_start(ci), CH + 2 * HALO), :])
            xl = cbv + sum(taps[k] * cwv[k:k + 1, :] for k in range(CONV_W))
            for d, (r, i, a, q) in enumerate(_lru_gates(xl, w4v, b4v, ls)):
                (a0, a1)[d][rows, :] = a
                (b0, b1)[d][rows, :] = jnp.sqrt(q) * (i * xl)
                (r0_ref, r1_ref)[d][rows, :] = r.astype(BF16)
                (i0_ref, i1_ref)[d][rows, :] = i.astype(BF16)
            return 0

        lax.fori_loop(0, T // CH, chunk, 0)
        zero = jnp.zeros((1, RB), F32)

        def emit_f(rs, hf, before):
            y[pl.ds(rs, SUB), :] = hf
            b0[pl.ds(rs, SUB), :] = before

        def emit_b(rs, hf, before):
            y[pl.ds(rs, SUB), :] += hf
            b1[pl.ds(rs, SUB), :] = before

        _scan_rows(a0, b0, 0, T, False, zero, emit_f)
        c = _scan_rows(a1, b1, 0, CTX, True, zero, emit_b)
        _scan_rows(a1, b1, CTX, T - CTX, True, c, emit_b)

        def finish(ci, _):
            rows = pl.ds(pl.multiple_of(ci * CH, CH), CH)
            yv = y[rows, :]
            u_ref[rows, :] = (yv * _gelu(gr_ref[rows, :].astype(F32))).astype(BF16)
            yo_ref[rows, :] = yv.astype(BF16)
            hpf_ref[rows, :] = b0[rows, :].astype(BF16)
            hpb_ref[rows, :] = b1[rows, :].astype(BF16)
            return 0

        lax.fori_loop(0, T // CH, finish, 0)

    sp = _rnn_specs(T)
    ci, ca, co, cs, cscr = _carry_args(carry)
    dts = [BF16, F32, F32] + [BF16] * 7
    res = pl.pallas_call(
        _carried(kern, carry, 7, 10, *_grid_ends((N_RNN_BLOCKS,))), name=name, grid=(N_RNN_BLOCKS,),
        in_specs=[sp["xr"], sp["gr"], sp["cw"], sp["cb"], sp["w4"], sp["b4"], sp["lam"]] + ci,
        out_specs=[sp["act"]] * 10 + co,
        out_shape=[_sds((T, D), dt) for dt in dts] + cs,
        scratch_shapes=[pltpu.VMEM((T + PAD_ROWS, RB), F32)] + [pltpu.VMEM((T, RB), F32)] * 3 + cscr,
        compiler_params=_params(),
    )(p, p, cw, cb, w4, b4, lam, *ca)
    return res if carry is None else (res[:10], res[10:])


def _rnn_bwd(name, p, du, saved, dp, cw, cb, w4, b4, lam, T, carry=None):
    def kern(xr_ref, gr_ref, du_ref, a0, a1, y_ref, hpf_ref, hpb_ref, r0_ref, r1_ref, i0_ref, i1_ref,
             cw_ref, cb_ref, w4_ref, b4_ref, lam_ref, dp_in,
             dp_ref, dcw_ref, dcb_ref, dw4_ref, db4_ref, dlam_ref,
             xpad, dxpad, c0, c1, dy):
        j = pl.program_id(1)

        @pl.when(j == 0)
        def _():
            scans(gr_ref, du_ref, a0, a1, y_ref, dp_ref, c0, c1, dy)

        @pl.when(j == 1)
        def _():
            gates(xr_ref, a0, a1, (hpf_ref, hpb_ref), (r0_ref, r1_ref), (i0_ref, i1_ref), cw_ref, cb_ref, w4_ref,
                  lam_ref, dp_ref, dcw_ref, dcb_ref, dw4_ref, db4_ref, dlam_ref, xpad, dxpad, c0, c1)

    def scans(gr_ref, du_ref, a0, a1, y_ref, dgr_ref, c0, c1, dy):
        def phase_a(ci, _):
            rows = pl.ds(pl.multiple_of(ci * CH, CH), CH)
            gr = gr_ref[rows, :].astype(F32)
            duv = du_ref[rows, :].astype(F32)
            dyv = duv * _gelu(gr)
            dgr_ref[rows, :] = (duv * y_ref[rows, :].astype(F32) * _gelu_grad(gr)).astype(BF16)
            dy[rows, :] = dyv
            c0[rows, :] = a0[rows, :] * dyv
            c1[rows, :] = a1[rows, :] * dyv
            return 0

        lax.fori_loop(0, T // CH, phase_a, 0)
        zero = jnp.zeros((1, RB), F32)

        def emit0(rs, hf, before):
            c0[pl.ds(rs, SUB), :] = dy[pl.ds(rs, SUB), :] + before

        def emit1(rs, hf, before):
            c1[pl.ds(rs, SUB), :] = dy[pl.ds(rs, SUB), :] + before

        _scan_rows(a0, c0, 0, T, True, zero, emit0)
        c = _scan_rows(a1, c1, CTX, T - CTX, False, zero, emit1)
        _scan_rows(a1, c1, 0, CTX, False, c, emit1)

    def gates(xr_ref, a0, a1, hp_refs, r_refs, i_refs, cw_ref, cb_ref, w4_ref, lam_ref,
              dxr_ref, dcw_ref, dcb_ref, dw4_ref, db4_ref, dlam_ref, xpad, dxpad, c0, c1):
        _fill_padded(xpad, xr_ref, T)
        _zero_pads(dxpad, T)
        lam_v = lam_ref[...]
        ls = _log_sigmoid(lam_v)
        w4v, cwv, cbv = w4_ref[...], cw_ref[...], cb_ref[...]

        def conv_chunk(ci):
            taps = _conv_taps(xpad[pl.ds(_pad_start(ci), CH + 2 * HALO), :])
            return taps, cbv + sum(taps[k] * cwv[k:k + 1, :] for k in range(CONV_W))

        dw4_ref[...] = jnp.zeros(dw4_ref.shape, F32)
        db4_ref[...] = jnp.zeros(db4_ref.shape, F32)
        dlam_ref[...] = jnp.zeros(dlam_ref.shape, F32)
        dcw_ref[...] = jnp.zeros(dcw_ref.shape, F32)
        dcb_ref[...] = jnp.zeros(dcb_ref.shape, F32)

        def phase_c(ci, _):
            base = pl.multiple_of(ci * CH, CH)
            rows = pl.ds(base, CH)
            _, xl = conv_chunk(ci)
            dxl = jnp.zeros((CH, RB), F32)
            dpre_a, dpre_x, dls = [], [], []
            for d in range(2):
                a = (a0, a1)[d][rows, :]
                r = r_refs[d][rows, :].astype(F32)
                i = i_refs[d][rows, :].astype(F32)
                q = -jnp.tanh(LRU_C * r * ls[d:d + 1, :]) * (1.0 + a * a)
                g = (c0, c1)[d][rows, :]
                hp = hp_refs[d][rows, :].astype(F32)
                gm = g * jnp.sqrt(q)
                di = gm * xl
                dxl = dxl + gm * i
                dla = a * (g * hp - a * (g * (i * xl)) * lax.rsqrt(q))
                dr = dla * (LRU_C * ls[d:d + 1, :])
                dls.append(_colsum(dla * (LRU_C * r)))
                dpre_a.append(dr * r * (1.0 - r))
                dpre_x.append(di * i * (1.0 - i))
            dpre = jnp.concatenate(dpre_a + dpre_x, axis=1)
            dpre_b = dpre.astype(BF16)
            dxl = dxl + _dot(dpre_b, w4v, NT)
            dw4_ref[...] += _dot(xl.astype(BF16), dpre_b, TN)
            db4_ref[...] += _colsum(dpre)
            dlam_ref[...] += jnp.concatenate(dls, axis=0)
            dcb_ref[...] += _colsum(dxl)
            dxpad[_pad_rows(ci), :] = dxl
            return 0

        lax.fori_loop(0, T // CH, phase_c, 0)
        dlam_ref[...] = dlam_ref[...] * _sigmoid(-lam_v)

        def phase_d(ci, _):
            base = pl.multiple_of(ci * CH, CH)
            rows = pl.ds(base, CH)
            xtaps, _ = conv_chunk(ci)
            dtaps = _conv_taps(dxpad[pl.ds(_pad_start(ci), CH + 2 * HALO), :], transpose=True)
            dxl = dxpad[_pad_rows(ci), :]
            dxr_ref[rows, :] = sum(dtaps[k] * cwv[k:k + 1, :] for k in range(CONV_W)).astype(BF16)
            dcw_ref[...] += jnp.concatenate([_colsum(dxl * xtaps[k]) for k in range(CONV_W)], axis=0)
            return 0

        lax.fori_loop(0, T // CH, phase_d, 0)

    sp = _rnn_specs(T)
    dp_spec = pl.BlockSpec((T, RB), lambda n, j: (0, COL_GR // RB + n - j * (COL_GR - COL_XR) // RB))
    ci, ca, co, cs, cscr = _carry_args(carry)
    n_in = 3 + len(saved) + 5 + 1
    res = pl.pallas_call(
        _carried(kern, carry, n_in, 6, *_grid_ends((N_RNN_BLOCKS, 2))), name=name, grid=(N_RNN_BLOCKS, 2),
        in_specs=[sp["xr"], sp["gr"]] + [sp["act"]] * (1 + len(saved)) + [sp["cw"], sp["cb"], sp["w4"], sp["b4"],
                                                                           sp["lam"], ANY] + ci,
        out_specs=[dp_spec, sp["cw"], sp["cb"], sp["w4"], sp["b4"], sp["lam"]] + co,
        out_shape=[_sds((T, DP_W), BF16), _sds((CONV_W, D), F32), _sds((1, D), F32),
                   _sds((N_RNN_BLOCKS, RB, 4 * RB), F32), _sds((N_RNN_BLOCKS, 1, 4 * RB), F32), _sds((2, D), F32)] + cs,
        scratch_shapes=[pltpu.VMEM((T + PAD_ROWS, RB), F32)] * 2 + [pltpu.VMEM((T, RB), F32)] * 3 + cscr,
        input_output_aliases={n_in - 1: 0},
        compiler_params=_params(("arbitrary", "arbitrary")),
    )(p, p, du, *saved, cw, cb, w4, b4, lam, dp, *ca)
    return res if carry is None else (res[:6], res[6:])


class _Plan:
    def __init__(self, shards, Ws):
        L = len(Ws)
        self.shards, self.Ws = shards, Ws
        self.Gs = [None] * L
        self.slots = [dict() for _ in range(L)]
        self.gate_slots = [None] * L
        self.table = {}
        for l in range(L):
            t = f"l{l}_"
            self.table[t + "rnn_fwd"] = [("gather", l, k) for k in ("wffn_in_t", "wo_rnn", "wo_attn", "wout")]
            if l + 1 < L:
                self.table[t + "attn_lat_fwd"] = [("gather", l + 1, "win_t")]
                self.table[t + "ffn_in"] = [("gather", l, "wffn_out")]
            else:
                self.table[t + "attn_lat_fwd"] = [("gather", l, "wffn_out")]
            self.table[t + "ffn_in_dx"] = [("scatter", l, "wffn_out")]
            self.table[t + "attn_lat_bwd"] = [("scatter", l, "wffn_in_t")]
            self.table[t + "proj_dx"] = [("scatter", l, "win_t_a")]
            self.table[t + "rnn_bwd"] = ([("scatter", l, k) for k in ("wout", "wo_attn", "wo_rnn")]
                                         + ([("scatter", l + 1, "win_t_b"), ("gates", l + 1, "w4")] if l + 1 < L else []))
        self.table["l0_proj_dw_b"] = [("gates", 0, "w4")]

    def carry(self, name):
        jobs = []
        for kind, l, k in self.table.get(name, []):
            if kind == "gather":
                jobs.append(("gather", self.shards[l][k]))
            elif kind == "scatter":
                jobs.append(("scatter", self.Gs[l][k].reshape(N_DEV, -1, self.Gs[l][k].shape[-1])))
            else:
                jobs.append(("gather", self.Gs[l]["w4"].reshape(N_RNN_BLOCKS * RB, 4 * RB).astype(BF16)))
        return _Carry(jobs) if jobs else None

    def done(self, name, got):
        for (kind, l, k), res in zip(self.table[name], got):
            if kind == "gather":
                self.Ws[l][k] = res.reshape(-1, D)
            elif kind == "scatter":
                self.slots[l][k] = res
            else:
                self.gate_slots[l] = res


def _run(X, fn, name, *args, **kw):
    carry = None if X is None else X.carry(name)
    if carry is None:
        return fn(name, *args, **kw)
    out, got = fn(name, *args, carry=carry, **kw)
    X.done(name, got)
    return out


def _layer_fwd(l, xa, h, W, rope, S, nxt, X=None):
    T = xa.shape[0]
    tag = f"l{l}_"
    cos, sin, bias = rope
    p = _run(X, _mm_act, tag + "proj", h, W["win_t"], "NT", BF16)
    u, *rnn_saved = _run(X, _rnn_fwd, tag + "rnn_fwd", p, W["cw"], W["cb"], W["w4"], W["b4"], W["lam"], T)
    qa, kp, vp, kc, vc = _qkv_prep(tag + "qkv_prep", p, cos, sin, S)
    o_all = _attn_fwd(tag + "attn_ctx_fwd", qa, kc, vc, W["sink4"], S)
    o_all = _run(X, _attn_fwd, tag + "attn_lat_fwd", qa, kc, vc, W["sink4"], S, band=(kp, vp, bias), prev=o_all)
    ya, yb, z, m, x1, h2 = _out_fused(tag + "out", p, u, o_all, xa, W["wo_rnn"], W["wo_attn"], W["wout"],
                                      W["g_mix_post"], W["mod"], W["g_ffn_pre"])
    fg, fu, s = _run(X, _ffn_in_fused, tag + "ffn_in", h2, W["wffn_in_t"])
    e, *out = _ffn_out_fused(tag + "ffn_out", s, W["wffn_out"], x1, W["g_ffn_post"], W["mod"], nxt)
    saved = dict(xa=xa, h=h, p=p, u=u, rnn=rnn_saved, qa=qa, kp=kp, vp=vp, kc=kc, vc=vc, o_all=o_all,
                 ya=ya, yb=yb, z=z, m=m, x1=x1, h2=h2, fg=fg, fu=fu, s=s, e=e)
    return saved, out


def _layer_bwd(l, dx2, A, W, rope, S, X=None, loss_of=None):
    T = A["xa"].shape[0]
    tag = f"l{l}_"
    cos, sin, bias = rope
    G = {}
    if X is not None:
        X.Gs[l] = G
    if loss_of is None:
        de, df, dga2, G["g_ffn_post"] = _ffn_bwd_fused(tag + "ffn_bwd", A["fg"], A["fu"], W["wffn_out"],
                                                       head=(dx2, A["e"], W["g_ffn_post"], W["mod"]))
    else:
        dx2, de, dga2, G["g_ffn_post"], G["sq"] = _loss_resid_bwd(tag + "loss_ffn_resid_bwd", *loss_of, A["e"],
                                                                  W["g_ffn_post"], W["mod"], GA2)
        df, = _ffn_bwd_fused(tag + "ffn_bwd", A["fg"], A["fu"], W["wffn_out"], de=de)
    G["wffn_out"] = _mm_wgrad(tag + "ffn_out_dw", A["s"], de)
    dx1, dm, dsh2, dsc2, G["g_ffn_pre"], dga1, G["g_mix_post"] = _run(
        X, _ffn_in_bwd_fused, tag + "ffn_in_dx", df, W["wffn_in_t"], A["x1"], dx2, A["m"], W["g_ffn_pre"], W["mod"],
        W["g_mix_post"])
    G["wffn_in_t"] = _run(X, _mm_wgrad, tag + "ffn_in_dw", df, A["h2"])
    G["wout"] = _mm_wgrad(tag + "out_dw", A["z"], dm)
    dya, dyb, dgl, du, do = _out_bwd_fused(tag + "out_dx", dm, W["wout"], W["wo_rnn"], W["wo_attn"], A["p"], A["ya"],
                                           A["yb"])
    G["wo_attn"] = _mm_wgrad(tag + "o_attn_dw", A["o_all"], dyb)
    G["wo_rnn"] = _mm_wgrad(tag + "o_rnn_dw", A["u"], dya)
    dq_all, dkc_c, dvc_c, dsink_c = _attn_bwd(tag + "attn_ctx_bwd", A["qa"], A["kc"], A["vc"], W["sink4"],
                                               A["o_all"], do, S)
    dq_all, dkc_l, dvc_l, dsink_l, dkp, dvp = _run(
        X, _attn_bwd, tag + "attn_lat_bwd", A["qa"], A["kc"], A["vc"], W["sink4"], A["o_all"], do, S,
        band=(A["kp"], A["vp"], bias), prev_dq=dq_all)
    G["sink4"] = dsink_c + dsink_l
    dp = _dqkv_assemble(tag + "dqkv", dq_all, dkp, dvp, dkc_l, dvc_l, dkc_c, dvc_c, cos, sin, S)
    dp, G["cw"], G["cb"], G["w4"], G["b4"], G["lam"] = _run(
        X, _rnn_bwd, tag + "rnn_bwd", A["p"], du, A["rnn"], dp, W["cw"], W["cb"], W["w4"], W["b4"], W["lam"], T)
    proj_dx = (_proj_bwd_fused, tag + "proj_dx", dp, dgl, W["win_t"], A["xa"], dx1, W["g_mix_pre"], W["mod"])
    if X is not None:
        G["win_t_a"] = _proj_wgrad(tag + "proj_dw_a", dp, dgl, A["h"][:, :D // 2])
        dxa, dsh1, dsc1, G["g_mix_pre"] = _run(X, *proj_dx)
        G["win_t_b"] = _run(X, _proj_wgrad, tag + "proj_dw_b", dp, dgl, A["h"][:, D // 2:])
    else:
        dxa, dsh1, dsc1, G["g_mix_pre"] = _run(X, *proj_dx)
        G["win_t"] = _proj_wgrad(tag + "proj_dw", dp, dgl, A["h"])
    G["mod"] = jnp.concatenate([dsh1, dsc1, dga1, dsh2, dsc2, dga2], axis=1)
    return dxa, G


def _local_step(xa, target, Ws, S, X=None):
    rope = (*_rope_tables(S), _band_bias(S))
    L = len(Ws)
    h = _normmod_fwd("l0_mix_norm", xa, Ws[0]["g_mix_pre"], Ws[0]["mod"], SH1, SC1)
    saved = []
    x = xa
    for l in range(L):
        nxt = (Ws[l + 1]["g_mix_pre"], Ws[l + 1]["mod"]) if l + 1 < L else None
        A, out = _layer_fwd(l, x, h, Ws[l], rope, S, nxt, X)
        saved.append(A)
        if l + 1 < L:
            x, h = out
    Gs = [None] * L
    dx = None
    for l in reversed(range(L)):
        dx, Gs[l] = _layer_bwd(l, dx, saved[l], Ws[l], rope, S, X, loss_of=(out[0], target) if l == L - 1 else None)
    return Gs[L - 1]["sq"], dx, Gs


MESH = pl.DeviceIdType.MESH


def _place():
    return lax.axis_index("x"), lax.axis_index("y"), lax.axis_index("c")


def _lin(px, py, pc):
    return 4 * px + 2 * py + pc


def _allgather_small(name, blk):
    m, n = blk.shape

    def body(x_ref, out_ref, send_sems, recv_sems, local_sem):
        x, y, c = _place()
        me, sibling = (x, y, c), (x, y, 1 - c)
        chips = [(1 - x, y), (x, 1 - y), (1 - x, 1 - y)]

        def copy(k, block, to, src=None):
            dst = out_ref.at[_lin(*block)]
            return pltpu.make_async_remote_copy(src_ref=dst if src is None else src, dst_ref=dst,
                                                send_sem=send_sems.at[k], recv_sem=recv_sems.at[k],
                                                device_id=to, device_id_type=MESH)

        mine = pltpu.make_async_copy(x_ref, out_ref.at[_lin(*me)], local_sem)
        mine.start()
        first = [copy(0, me, sibling, src=x_ref)]
        first += [copy(1 + j, me, (*chip, c), src=x_ref) for j, chip in enumerate(chips)]
        for cp in first:
            cp.start()
        passed = [copy(4 + j, (*chip, c), sibling) for j, chip in enumerate(chips)]
        for j, chip in enumerate(chips):
            copy(1 + j, (*chip, c), me).wait_recv()
            passed[j].start()
        copy(0, sibling, me).wait_recv()
        for j, chip in enumerate(chips):
            copy(4 + j, (*chip, 1 - c), me).wait_recv()
        for cp in first + passed:
            cp.wait_send()
        mine.wait()

    return pl.pallas_call(
        body, name=name, out_shape=_sds((N_DEV, m, n), blk.dtype),
        in_specs=[pl.BlockSpec(memory_space=pltpu.VMEM)], out_specs=pl.BlockSpec(memory_space=pltpu.VMEM),
        scratch_shapes=[pltpu.SemaphoreType.DMA((7,)), pltpu.SemaphoreType.DMA((7,)), pltpu.SemaphoreType.DMA],
        compiler_params=pltpu.CompilerParams(vmem_limit_bytes=VMEM_LIMIT),
    )(blk)


def _allgather_hbm(name, shards):
    na = len(shards)

    def body(*refs):
        ins, outs = refs[:na], refs[na:2 * na]
        send_sems, recv_sems, local_sems = refs[2 * na:]
        x, y, c = _place()
        me, sibling = (x, y, c), (x, y, 1 - c)
        chips = [(1 - x, y), (x, 1 - y), (1 - x, 1 - y)]

        def copy(a, k, block, to, from_input=False):
            dst = outs[a].at[_lin(*block)]
            return pltpu.make_async_remote_copy(src_ref=ins[a] if from_input else dst, dst_ref=dst,
                                                send_sem=send_sems.at[a, k], recv_sem=recv_sems.at[a, k],
                                                device_id=to, device_id_type=MESH)

        mine = [pltpu.make_async_copy(ins[a], outs[a].at[_lin(*me)], local_sems.at[a]) for a in range(na)]
        for cp in mine:
            cp.start()
        first = []
        for a in range(na):
            first.append(copy(a, 0, me, sibling, True))
            first += [copy(a, 1 + j, me, (*chip, c), True) for j, chip in enumerate(chips)]
        for cp in first:
            cp.start()
        passed = []
        for j, chip in enumerate(chips):
            for a in range(na):
                copy(a, 1 + j, (*chip, c), me).wait_recv()
                fwd = copy(a, 4 + j, (*chip, c), sibling)
                fwd.start()
                passed.append(fwd)
        for a in range(na):
            copy(a, 0, sibling, me).wait_recv()
            for j, chip in enumerate(chips):
                copy(a, 4 + j, (*chip, 1 - c), me).wait_recv()
        for cp in first + passed:
            cp.wait_send()
        for cp in mine:
            cp.wait()

    return pl.pallas_call(
        body, name=name, out_shape=[_sds((N_DEV, *s.shape), s.dtype) for s in shards],
        in_specs=[ANY] * na, out_specs=[ANY] * na,
        scratch_shapes=[pltpu.SemaphoreType.DMA((na, 7)), pltpu.SemaphoreType.DMA((na, 7)),
                        pltpu.SemaphoreType.DMA((na,))],
    )(*shards)


def _exchange_shards(name, grads, L):
    nw = len(grads)
    na = nw * L
    flat = [g for per_layer in grads for g in per_layer]

    def body(*refs):
        ins, outs = refs[:na], refs[na:na + nw]
        send_sems, recv_sems, local_sems = refs[na + nw:]
        x, y, c = _place()
        me = _lin(x, y, c)
        peers = [(x ^ ((k + 1) >> 2 & 1), y ^ ((k + 1) >> 1 & 1), c ^ ((k + 1) & 1)) for k in range(7)]

        def copy(a, k, src_blk, dst_blk):
            return pltpu.make_async_remote_copy(src_ref=ins[a].at[src_blk], dst_ref=outs[a // L].at[a % L, dst_blk],
                                                send_sem=send_sems.at[a, k], recv_sem=recv_sems.at[a, k],
                                                device_id=peers[k], device_id_type=MESH)

        mine = [pltpu.make_async_copy(ins[a].at[me], outs[a // L].at[a % L, me], local_sems.at[a]) for a in range(na)]
        for cp in mine:
            cp.start()
        sent = [copy(a, k, _lin(*peers[k]), me) for a in range(na) for k in range(7)]
        for cp in sent:
            cp.start()
        for a in range(na):
            for k in range(7):
                copy(a, k, me, _lin(*peers[k])).wait_recv()
        for cp in sent:
            cp.wait_send()
        for cp in mine:
            cp.wait()

    return pl.pallas_call(
        body, name=name, out_shape=[_sds((L, *per_layer[0].shape), per_layer[0].dtype) for per_layer in grads],
        in_specs=[ANY] * na, out_specs=[ANY] * nw,
        scratch_shapes=[pltpu.SemaphoreType.DMA((na, 7)), pltpu.SemaphoreType.DMA((na, 7)),
                        pltpu.SemaphoreType.DMA((na,))],
    )(*flat)


MOD_ROWS = 16
MOD_SHARD = 6 * D // N_DEV
HI = lax.Precision.HIGHEST


def _mod_fwd(name, c9, w_mod, b_shard):
    L = w_mod.shape[0]

    def kern(c_ref, w_ref, b_ref, o_ref):
        o_ref[...] = lax.dot_general(_silu(c_ref[...]), w_ref[...], NN, precision=HI,
                                     preferred_element_type=F32) + b_ref[...]

    return pl.pallas_call(
        kern, name=name, grid=(L,),
        in_specs=[_full_spec(c9.shape), pl.BlockSpec((None, D, MOD_SHARD), lambda l: (l, 0, 0)),
                  pl.BlockSpec((None, 1, MOD_SHARD), lambda l: (l, 0, 0))],
        out_specs=pl.BlockSpec((None, MOD_ROWS, MOD_SHARD), lambda l: (l, 0, 0)),
        out_shape=_sds((L, MOD_ROWS, MOD_SHARD), F32), compiler_params=_params(),
    )(c9, w_mod, b_shard)


def _mod_bwd(name, c9, w_mod, dmod_all, dmod_cols):
    L = w_mod.shape[0]

    def rows9(ref, l):
        own = jnp.concatenate([ref[j, 2 * l + 1:2 * l + 2, :] for j in range(N_DEV)], axis=0)
        ctx = ref[0, 2 * l:2 * l + 1, :]
        for j in range(1, N_DEV):
            ctx = ctx + ref[j, 2 * l:2 * l + 1, :]
        return own, ctx

    def kern(c_ref, w_ref, all_ref, cols_ref, gw_ref, gb_ref, gc_ref):
        l = pl.program_id(0)
        for ll in range(L):
            @pl.when(l == ll)
            def _():
                own, ctx = rows9(all_ref, ll)
                gb_ref[...] = _colsum(own) + ctx
                own_s, ctx_s = rows9(cols_ref, ll)
                r16 = jnp.concatenate([own_s, ctx_s, jnp.zeros((MOD_ROWS - N_DEV - 1, MOD_SHARD), F32)], axis=0)
                gw_ref[...] = lax.dot_general(_silu(c_ref[...]), r16, TN, precision=HI, preferred_element_type=F32)
                part = lax.dot_general(r16, w_ref[...], NT, precision=HI,
                                       preferred_element_type=F32)[N_DEV:N_DEV + 1, :]
                if ll == 0:
                    gc_ref[...] = part
                else:
                    gc_ref[...] += part

    return pl.pallas_call(
        kern, name=name, grid=(L,),
        in_specs=[_full_spec(c9.shape), pl.BlockSpec((None, D, MOD_SHARD), lambda l: (l, 0, 0)),
                  _full_spec(dmod_all.shape), _full_spec(dmod_cols.shape)],
        out_specs=[pl.BlockSpec((None, D, MOD_SHARD), lambda l: (l, 0, 0)),
                   pl.BlockSpec((None, 1, 6 * D), lambda l: (l, 0, 0)), _full_spec((1, D))],
        out_shape=[_sds((L, D, MOD_SHARD), F32), _sds((L, 1, 6 * D), F32), _sds((1, D), F32)],
        compiler_params=_params(),
    )(c9, w_mod, dmod_all, dmod_cols)


_BC1 = 1.0 - ADAM_B1 ** ADAM_STEP
_BC2 = 1.0 - ADAM_B2 ** ADAM_STEP


def _adamw_vals(w, g, m, v):
    m = ADAM_B1 * m + (1.0 - ADAM_B1) * g
    v = ADAM_B2 * v + (1.0 - ADAM_B2) * (g * g)
    delta = -ADAM_LR * ((m / _BC1) / (jnp.sqrt(v / _BC2) + ADAM_EPS) + ADAM_WD * w)
    return delta, m, v


def _adamw(name, w, g, m, v, tile):
    R, C = w.shape
    blk = ((tile, C), lambda i: (i, 0))

    def body(i, ins, ps, outs, acc):
        d, mm, vv = _adamw_vals(ins[0][...], ins[1][...], ins[2][...], ins[3][...])
        outs[0][...] = d
        outs[1][...] = mm
        outs[2][...] = vv

    return _ew(name, body, R // tile, [(a, *blk) for a in (w, g, m, v)], [], [(_sds((R, C), F32), *blk)] * 3)


def _sum_slots(ref):
    g = ref[0].astype(F32)
    for j in range(1, N_DEV):
        g = g + ref[j].astype(F32)
    return g


def _adamw_slots(name, slots, shape, tile, wmv=None):
    L, R, C = shape
    n = R // tile
    spec = pl.BlockSpec((None, tile, C), lambda l, i: (l, i, 0))
    pieces = [s if isinstance(s, (list, tuple)) else [s] for s in slots]
    layer_of = [ll for ll, ps in enumerate(pieces) for _ in ps]
    flat = [p for ps in pieces for p in ps]
    wmv = list(wmv or [])

    def slot_spec(ll, cols):
        return pl.BlockSpec((N_DEV, tile, cols),
                            lambda l, i: (0, jnp.where(l == ll, i, jnp.where(l < ll, 0, n - 1)), 0))

    def kern(*refs):
        s_refs = refs[:len(flat)]
        rest = refs[len(flat):]
        l = pl.program_id(0)
        for ll in range(L):
            @pl.when(l == ll)
            def _():
                parts = [_sum_slots(r) for r, lr in zip(s_refs, layer_of) if lr == ll]
                g = parts[0] if len(parts) == 1 else jnp.concatenate(parts, axis=1)
                if wmv:
                    w_ref, m_ref, v_ref, g_ref, d_ref, mo_ref, vo_ref = rest
                    d_ref[...], mo_ref[...], vo_ref[...] = _adamw_vals(w_ref[...], g, m_ref[...], v_ref[...])
                else:
                    g_ref, = rest
                g_ref[...] = g

    n_out = 4 if wmv else 1
    return pl.pallas_call(
        kern, name=name, grid=(L, n),
        in_specs=[slot_spec(ll, p.shape[-1]) for ll, p in zip(layer_of, flat)] + [spec] * len(wmv),
        out_specs=[spec] * n_out, out_shape=[_sds((L, R, C), F32)] * n_out,
        compiler_params=_params(("arbitrary", "arbitrary")),
    )(*flat, *wmv)


def _sum_blocks(name, blocks):
    _, R, C = blocks.shape

    def kern(b_ref, o_ref):
        o_ref[...] = _sum_slots(b_ref)

    return pl.pallas_call(kern, name=name, in_specs=[_full_spec(blocks.shape)], out_specs=_full_spec((R, C)),
                          grid=(1,), out_shape=_sds((R, C), F32), compiler_params=_params())(blocks)


BIG = ("win_t", "wo_rnn", "wo_attn", "wout", "wffn_in_t", "wffn_out")
BIG_SRC = ("w_in", "w_o_rnn", "w_o_attn", "w_out", "w_ffn_in", "w_ffn_out")
BIG_T = (True, False, False, False, True, False)
BIG_TILE = (176, 128, 128, 128, 176, 176)


def _chan_full(g8):
    return jnp.transpose(g8, (1, 0, 2)).reshape(g8.shape[1], D)


def kernel(x, c, ctx, c_ctx, w_mod, b_mod, g_mix_pre, g_mix_post, g_ffn_pre, g_ffn_post, w_in, conv_w, conv_b, lru_wa, lru_ba, lru_wx, lru_bx, lru_lam, attn_sink, w_o_rnn, w_o_attn, w_out, w_ffn_in, w_ffn_out, loss_target, m_c_ctx, m_w_mod, m_b_mod, m_g_mix_pre, m_g_mix_post, m_g_ffn_pre, m_g_ffn_post, m_w_in, m_conv_w, m_conv_b, m_lru_wa, m_lru_ba, m_lru_wx, m_lru_bx, m_lru_lam, m_attn_sink, m_w_o_rnn, m_w_o_attn, m_w_out, m_w_ffn_in, m_w_ffn_out, v_c_ctx, v_w_mod, v_b_mod, v_g_mix_pre, v_g_mix_post, v_g_ffn_pre, v_g_ffn_post, v_w_in, v_conv_w, v_conv_b, v_lru_wa, v_lru_ba, v_lru_wx, v_lru_bx, v_lru_lam, v_attn_sink, v_w_o_rnn, v_w_o_attn, v_w_out, v_w_ffn_in, v_w_ffn_out):
    P = dict(c_ctx=c_ctx, w_mod=w_mod, b_mod=b_mod, g_mix_pre=g_mix_pre, g_mix_post=g_mix_post, g_ffn_pre=g_ffn_pre,
             g_ffn_post=g_ffn_post, w_in=w_in, conv_w=conv_w, conv_b=conv_b, lru_wa=lru_wa, lru_ba=lru_ba,
             lru_wx=lru_wx, lru_bx=lru_bx, lru_lam=lru_lam, attn_sink=attn_sink, w_o_rnn=w_o_rnn, w_o_attn=w_o_attn,
             w_out=w_out, w_ffn_in=w_ffn_in, w_ffn_out=w_ffn_out)
    Mo = dict(c_ctx=m_c_ctx, w_mod=m_w_mod, b_mod=m_b_mod, g_mix_pre=m_g_mix_pre, g_mix_post=m_g_mix_post,
              g_ffn_pre=m_g_ffn_pre, g_ffn_post=m_g_ffn_post, w_in=m_w_in, conv_w=m_conv_w, conv_b=m_conv_b,
              lru_wa=m_lru_wa, lru_ba=m_lru_ba, lru_wx=m_lru_wx, lru_bx=m_lru_bx, lru_lam=m_lru_lam,
              attn_sink=m_attn_sink, w_o_rnn=m_w_o_rnn, w_o_attn=m_w_o_attn, w_out=m_w_out, w_ffn_in=m_w_ffn_in,
              w_ffn_out=m_w_ffn_out)
    Vo = dict(c_ctx=v_c_ctx, w_mod=v_w_mod, b_mod=v_b_mod, g_mix_pre=v_g_mix_pre, g_mix_post=v_g_mix_post,
              g_ffn_pre=v_g_ffn_pre, g_ffn_post=v_g_ffn_post, w_in=v_w_in, conv_w=v_conv_w, conv_b=v_conv_b,
              lru_wa=v_lru_wa, lru_ba=v_lru_ba, lru_wx=v_lru_wx, lru_bx=v_lru_bx, lru_lam=v_lru_lam,
              attn_sink=v_attn_sink, w_o_rnn=v_w_o_rnn, w_o_attn=v_w_o_attn, w_out=v_w_out, w_ffn_in=v_w_ffn_in,
              w_ffn_out=v_w_ffn_out)
    L = w_in.shape[0]
    S = x.shape[1]
    me = _lin(*_place())

    small = jnp.concatenate([c.reshape(8, 128), conv_w.reshape(L * CONV_W, 128), lru_ba.reshape(2 * L, 128),
                             lru_bx.reshape(2 * L, 128), lru_lam.reshape(2 * L, 128), jnp.zeros((4, 128), F32)], axis=0)
    small_all = _allgather_small("ag_small", small)
    c_all = small_all[:, 0:8].reshape(N_DEV, D)
    conv_w_f = _chan_full(small_all[:, 8:16]).reshape(L, CONV_W, D)
    lru_ba_f = _chan_full(small_all[:, 16:20]).reshape(L, 2, D)
    lru_bx_f = _chan_full(small_all[:, 20:24]).reshape(L, 2, D)
    lru_lam_f = _chan_full(small_all[:, 24:28]).reshape(L, 2, D)

    c9 = jnp.concatenate([c_all, c_ctx[None], jnp.zeros((MOD_ROWS - N_DEV - 1, D), F32)], axis=0)
    b_shard = lax.dynamic_slice_in_dim(b_mod, me * MOD_SHARD, MOD_SHARD, axis=1)[:, None, :]
    mod_part = _mod_fwd("mod_fwd", c9, w_mod, b_shard)
    mod_all = _allgather_small("ag_mod", mod_part.reshape(L * MOD_ROWS, MOD_SHARD))
    mod_all = jnp.transpose(mod_all.reshape(N_DEV, L, MOD_ROWS, MOD_SHARD), (1, 2, 0, 3)).reshape(L, MOD_ROWS, 6 * D)
    own_row = lax.dynamic_index_in_dim(mod_all, me, axis=1, keepdims=False)
    modrows = jnp.stack([mod_all[:, N_DEV], own_row], axis=1)

    shards = [{k: (P[src][l].T if tr else P[src][l]).astype(BF16) for k, src, tr in zip(BIG, BIG_SRC, BIG_T)}
              for l in range(L)]
    win0, = _allgather_hbm("ag_w_in0", [shards[0]["win_t"]])
    Ws = []
    for l in range(L):
        W = {"win_t": win0.reshape(-1, D)} if l == 0 else {}
        W.update(
            cw=conv_w_f[l], cb=conv_b[l][None],
            w4=jnp.concatenate([lru_wa[l, 0], lru_wa[l, 1], lru_wx[l, 0], lru_wx[l, 1]], axis=-1).astype(BF16),
            b4=jnp.concatenate([lru_ba_f[l, 0].reshape(N_RNN_BLOCKS, 1, RB), lru_ba_f[l, 1].reshape(N_RNN_BLOCKS, 1, RB),
                                lru_bx_f[l, 0].reshape(N_RNN_BLOCKS, 1, RB), lru_bx_f[l, 1].reshape(N_RNN_BLOCKS, 1, RB)],
                               axis=-1),
            lam=lru_lam_f[l], sink4=jnp.broadcast_to(attn_sink[l].reshape(N_KV, Q_PER_KV, 1), (N_KV, Q_PER_KV, HEAD)),
            g_mix_pre=g_mix_pre[l][None], g_mix_post=g_mix_post[l][None], g_ffn_pre=g_ffn_pre[l][None],
            g_ffn_post=g_ffn_post[l][None], mod=modrows[l])
        Ws.append(W)

    xa = jnp.concatenate([ctx[0], x[0]], axis=0)
    plan = _Plan(shards, Ws)
    sq, dxa, Gs = _local_step(xa, loss_target[0], Ws, S, plan)
    loss_part = ((0.5 / D) * jnp.sum(sq)).reshape(1, 1)
    grad_x = dxa[CTX:][None]

    dmod = jnp.concatenate([Gs[l]["mod"] for l in range(L)] + [jnp.zeros((8 - 2 * L, 6 * D), F32)], axis=0)
    dmod_all = _allgather_small("ag_dmod", dmod)
    dmod_cols = lax.dynamic_slice_in_dim(dmod_all, me * MOD_SHARD, MOD_SHARD, axis=2)
    g_w_mod, g_b_mod, dsc_part = _mod_bwd("mod_bwd", c9, w_mod, dmod_all, dmod_cols)
    g_b_mod = g_b_mod[:, 0]

    def rows(name, shape):
        return jnp.concatenate([Gs[l][name].reshape(shape) for l in range(L)], axis=0)

    b4g = [Gs[l]["b4"].reshape(N_RNN_BLOCKS, 4, RB) for l in range(L)]
    sink_row = jnp.concatenate([Gs[l]["sink4"][:, :, 0].reshape(1, N_Q) for l in range(L)]
                               + [loss_part, jnp.zeros((1, D - L * N_Q - 1), F32)], axis=1)
    small_g = jnp.concatenate(
        [rows("g_mix_pre", (1, D)), rows("g_mix_post", (1, D)), rows("g_ffn_pre", (1, D)), rows("g_ffn_post", (1, D)),
         rows("cb", (1, D)), rows("cw", (CONV_W, D))]
        + [b4g[l][:, d].reshape(1, D) for l in range(L) for d in range(2)]
        + [b4g[l][:, 2 + d].reshape(1, D) for l in range(L) for d in range(2)]
        + [rows("lam", (2, D)), sink_row, dsc_part], axis=0)
    n_small = small_g.shape[0]
    small_tot = _sum_blocks("sum_small", _allgather_small("ag_small_grads", small_g))
    o = 0
    G = {}
    for name in ("g_mix_pre", "g_mix_post", "g_ffn_pre", "g_ffn_post", "conv_b"):
        G[name] = small_tot[o:o + L]
        o += L
    G["conv_w"] = small_tot[o:o + L * CONV_W].reshape(L, CONV_W, D)
    o += L * CONV_W
    for name in ("lru_ba", "lru_bx", "lru_lam"):
        G[name] = small_tot[o:o + 2 * L].reshape(L, 2, D)
        o += 2 * L
    G["attn_sink"] = small_tot[o, :L * N_Q].reshape(L, N_Q)
    loss = small_tot[o, L * N_Q]
    sg = jax.nn.sigmoid(c_ctx)
    G["c_ctx"] = small_tot[o + 1] * (sg * (1.0 + c_ctx * (1.0 - sg)))
    G["b_mod"] = g_b_mod
    G["w_mod"] = g_w_mod

    last_slots, = _exchange_shards("exchange_w_in0", [[Gs[0]["win_t_b"].reshape(N_DEV, -1, D // 2)]], 1)
    plan.slots[0]["win_t_b"] = last_slots[0]
    for l in range(L):
        plan.slots[l]["win_t"] = [plan.slots[l]["win_t_a"], plan.slots[l]["win_t_b"]]

    out_g, out_d, out_m, out_v = {}, {}, {}, {}

    def put(name, res, shape=None):
        g, d, m, v = res
        for dst, val in ((out_g, g), (out_d, d), (out_m, m), (out_v, v)):
            dst[name] = val if shape is None else val.reshape(shape)

    for k, src, tr, tile in zip(BIG, BIG_SRC, BIG_T, BIG_TILE):
        lay = (lambda a: jnp.swapaxes(a, 1, 2)) if tr else (lambda a: a)
        wmv = (lay(P[src]), lay(Mo[src]), lay(Vo[src]))
        res = _adamw_slots("adamw_" + src, [plan.slots[l][k] for l in range(L)], wmv[0].shape, tile, wmv)
        put(src, [lay(r) for r in res])
    res = _adamw("adamw_w_mod", w_mod.reshape(L * D, MOD_SHARD), g_w_mod.reshape(L * D, MOD_SHARD),
                 m_w_mod.reshape(L * D, MOD_SHARD), v_w_mod.reshape(L * D, MOD_SHARD), 256)
    put("w_mod", (g_w_mod,) + tuple(res), w_mod.shape)
    def fuse4(wa, wx):
        return jnp.concatenate([wa[:, 0], wa[:, 1], wx[:, 0], wx[:, 1]], axis=-1).reshape(L, N_RNN_BLOCKS * RB, 4 * RB)

    res = _adamw_slots("adamw_gates", plan.gate_slots, (L, N_RNN_BLOCKS * RB, 4 * RB), 256,
                       (fuse4(lru_wa, lru_wx), fuse4(m_lru_wa, m_lru_wx), fuse4(v_lru_wa, v_lru_wx)))
    res = [r.reshape(L, N_RNN_BLOCKS, RB, 4, RB) for r in res]
    put("lru_wa", [jnp.stack([r[:, :, :, 0], r[:, :, :, 1]], axis=1) for r in res])
    put("lru_wx", [jnp.stack([r[:, :, :, 2], r[:, :, :, 3]], axis=1) for r in res])
    rep = ("g_mix_pre", "g_mix_post", "g_ffn_pre", "g_ffn_post", "conv_b", "b_mod")

    def pack_rep(T_):
        sink = jnp.concatenate([T_["attn_sink"].reshape(1, L * N_Q), jnp.zeros((1, D - L * N_Q), F32)], axis=1)
        return jnp.concatenate([T_[n].reshape(-1, D) for n in rep] + [sink, T_["c_ctx"][None]], axis=0)

    pk = [pack_rep(T_) for T_ in (P, G, Mo, Vo)]
    n_rep = pk[0].shape[0]
    res = _adamw("adamw_replicated", *[jnp.pad(a, ((0, 24 - n_rep), (0, 0))) for a in pk], 24)
    res = (pk[1],) + tuple(r[:n_rep] for r in res)
    o = 0
    for n in rep:
        k = P[n].size // D
        put(n, [r[o:o + k] for r in res], P[n].shape)
        o += k
    put("attn_sink", [r[o, :L * N_Q] for r in res], attn_sink.shape)
    put("c_ctx", [r[o + 1] for r in res], c_ctx.shape)
    chan = ("conv_w", "lru_ba", "lru_bx", "lru_lam")
    g_own = {n: lax.dynamic_slice_in_dim(G[n], me * RB, RB, axis=2) for n in chan}

    def pack_chan(T_):
        return jnp.concatenate([T_[n].reshape(-1, RB) for n in chan], axis=0)

    pk = [pack_chan(T_) for T_ in (P, g_own, Mo, Vo)]
    n_ch = pk[0].shape[0]
    res = _adamw("adamw_channels", *[jnp.pad(a, ((0, 24 - n_ch), (0, 0))) for a in pk], 24)
    res = (pk[1],) + tuple(r[:n_ch] for r in res)
    o = 0
    for n in chan:
        k = P[n].size // RB
        put(n, [r[o:o + k] for r in res], P[n].shape)
        o += k

    order = ("c_ctx", "w_mod", "b_mod", "g_mix_pre", "g_mix_post", "g_ffn_pre", "g_ffn_post", "w_in", "conv_w", "conv_b",
             "lru_wa", "lru_ba", "lru_wx", "lru_bx", "lru_lam", "attn_sink", "w_o_rnn", "w_o_attn", "w_out", "w_ffn_in",
             "w_ffn_out")
    return (loss, grad_x, *[out_g[n] for n in order], *[out_d[n] for n in order], *[out_m[n] for n in order],
            *[out_v[n] for n in order])
```

```python
import functools
import math

import numpy as np
import jax
import jax.numpy as jnp
from jax import lax
from jax.experimental import pallas as pl
from jax.experimental.pallas import tpu as pltpu

F32 = jnp.float32
BF16 = jnp.bfloat16

D = 1024
CTX = 256
TR = 256
HEAD = 128
N_Q = 8
N_KV = 2
Q_PER_KV = N_Q // N_KV
GRID_W = 64
N_FREQ = HEAD // 4
ROPE_BASE = 10000.0
N_RNN_BLOCKS = 8
CONV_W = 4
CONV_LEFT = 2
LRU_C = 8.0
D_FF = 2816
IN_W = 5632
P_W = IN_W
DP_W = 3584
COL_XR, COL_GR, COL_Q, COL_K, COL_V, COL_GL = 0, 1024, 2048, 3072, 3328, 3584
GLB = 512
EPS = 1e-6
NEG_INF = -1e30
ATT_SCALE = HEAD ** -0.5
N_DEV = 8
VMEM_LIMIT = 56 * 1024 * 1024

ADAM_LR, ADAM_B1, ADAM_B2, ADAM_EPS, ADAM_WD, ADAM_STEP = 0.001, 0.9, 0.999, 1e-08, 0.01, 10

NN = (((1,), (0,)), ((), ()))
NT = (((1,), (1,)), ((), ()))
TN = (((0,), (0,)), ((), ()))


def _dot(a, b, dims=NN):
    return lax.dot_general(a, b, dims, preferred_element_type=F32)


def _params(sem=("arbitrary",)):
    return pltpu.CompilerParams(dimension_semantics=sem, vmem_limit_bytes=VMEM_LIMIT)


def _full_spec(shape):
    nd = len(shape)
    return pl.BlockSpec(shape, lambda *_: (0,) * nd)


ANY = pl.BlockSpec(memory_space=pl.ANY)


def _ew(name, body, n, row_ins, pars, row_outs, accs=(), alias=None):
    n_ri, n_p, n_ro, n_acc = len(row_ins), len(pars), len(row_outs), len(accs)

    def kern(*refs):
        i = pl.program_id(0)
        ins = refs[:n_ri]
        ps = refs[n_ri:n_ri + n_p]
        outs = refs[n_ri + n_p:n_ri + n_p + n_ro]
        acc = refs[n_ri + n_p + n_ro:]
        if n_acc:
            @pl.when(i == 0)
            def _():
                for a in acc:
                    a[...] = jnp.zeros(a.shape, a.dtype)
        body(i, ins, ps, outs, acc)

    in_specs = [ANY if blk is None else pl.BlockSpec(blk, imap) for (_, blk, imap) in row_ins]
    in_specs += [_full_spec(p.shape) for p in pars]
    out_specs = [pl.BlockSpec(blk, imap) for (_, blk, imap) in row_outs] + [_full_spec(a.shape) for a in accs]
    out_shape = [s for (s, _, _) in row_outs] + list(accs)
    return pl.pallas_call(
        kern, name=name, grid=(n,), in_specs=in_specs, out_specs=out_specs, out_shape=out_shape,
        input_output_aliases=alias or {}, compiler_params=_params(),
    )(*[a for (a, _, _) in row_ins], *pars)


def _rowblk(width, colblk=0, roff=0, tile=TR):
    return (tile, width), (lambda i: (i + roff, colblk))


def _sds(shape, dtype):
    return jax.ShapeDtypeStruct(shape, dtype)


class _Carry:
    SAME_CORE = (1, 3, 5)

    def __init__(self, jobs):
        self.jobs = list(jobs)
        self.arrays = [a for _, a in self.jobs]
        self.out_shapes = [_sds(a.shape if kind == "scatter" else (N_DEV, *a.shape), a.dtype) for kind, a in self.jobs]
        n = len(self.jobs)
        self.scratch = [pltpu.SemaphoreType.DMA((n, 7)), pltpu.SemaphoreType.DMA((n, 7)), pltpu.SemaphoreType.DMA((n,))]

    def _setup(self, sems):
        send_sems, recv_sems, local_sems = sems
        x, y, c = _place()
        me = _lin(x, y, c)
        peers = [(x ^ ((k + 1) >> 2 & 1), y ^ ((k + 1) >> 1 & 1), c ^ ((k + 1) & 1)) for k in range(7)]

        def copy(a, k, sem_k, src, dst):
            return pltpu.make_async_remote_copy(src_ref=src, dst_ref=dst, send_sem=send_sems.at[a, sem_k],
                                                recv_sem=recv_sems.at[a, sem_k], device_id=peers[k], device_id_type=MESH)

        return me, [_lin(*p) for p in peers], copy, local_sems

    def _local(self, a, kind, ins, outs, me, local_sems):
        return pltpu.make_async_copy(ins[a].at[me] if kind == "scatter" else ins[a], outs[a].at[me], local_sems.at[a])

    def start(self, ins, outs, sems):
        me, theirs, copy, local_sems = self._setup(sems)
        for a, (kind, _) in enumerate(self.jobs):
            self._local(a, kind, ins, outs, me, local_sems).start()
            if kind == "scatter":
                for k in range(7):
                    copy(a, k, k, ins[a].at[theirs[k]], outs[a].at[me]).start()
            else:
                for k in (0,) + self.SAME_CORE:
                    copy(a, k, k, ins[a], outs[a].at[me]).start()

    def wait(self, ins, outs, sems):
        me, theirs, copy, local_sems = self._setup(sems)
        for a, (kind, _) in enumerate(self.jobs):
            if kind == "scatter":
                for k in range(7):
                    copy(a, k, k, ins[a].at[me], outs[a].at[theirs[k]]).wait_recv()
                for k in range(7):
                    copy(a, k, k, ins[a].at[theirs[k]], outs[a].at[me]).wait_send()
            else:
                for k in self.SAME_CORE:
                    blk = outs[a].at[theirs[k]]
                    copy(a, k, k, ins[a], blk).wait_recv()
                    copy(a, 0, k + 1, blk, blk).start()
                copy(a, 0, 0, ins[a], outs[a].at[theirs[0]]).wait_recv()
                for k in self.SAME_CORE:
                    copy(a, 0, k + 1, ins[a], outs[a].at[theirs[k + 1]]).wait_recv()
                for k in (0,) + self.SAME_CORE:
                    copy(a, k, k, ins[a], outs[a].at[me]).wait_send()
                for k in self.SAME_CORE:
                    blk = outs[a].at[theirs[k]]
                    copy(a, 0, k + 1, blk, blk).wait_send()
            self._local(a, kind, ins, outs, me, local_sems).wait()


def _carried(kern, carry, n_in, n_out, first, last):
    if carry is None:
        return kern
    nc = len(carry.jobs)

    def wrapped(*refs):
        ins, cin = refs[:n_in], refs[n_in:n_in + nc]
        outs, cout = refs[n_in + nc:n_in + nc + n_out], refs[n_in + nc + n_out:n_in + 2 * nc + n_out]
        scr, sems = refs[n_in + 2 * nc + n_out:-3], refs[-3:]

        @pl.when(first())
        def _():
            carry.start(cin, cout, sems)

        kern(*ins, *outs, *scr)

        @pl.when(last())
        def _():
            carry.wait(cin, cout, sems)

    return wrapped


def _carry_args(carry):
    if carry is None:
        return [], [], [], [], []
    n = len(carry.jobs)
    return [ANY] * n, carry.arrays, [ANY] * n, carry.out_shapes, carry.scratch


def _grid_ends(dims):
    first = lambda: functools.reduce(jnp.logical_and, [pl.program_id(d) == 0 for d in range(len(dims))])
    last = lambda: functools.reduce(jnp.logical_and, [pl.program_id(d) == n - 1 for d, n in enumerate(dims)])
    return first, last


def _mm_call(name, a, b, mode, out_dtype, tm, tn, rows_outer=True, single_b=False, carry=None):
    if mode == "TN":
        (K, M), N = a.shape, b.shape[1]
    else:
        (M, K), N = a.shape, (b.shape[1] if mode == "NN" else b.shape[0])
    assert M % tm == 0 and N % tn == 0, (name, M, N, K, tm, tn)
    ij = (lambda g0, g1: (g0, g1)) if rows_outer else (lambda g0, g1: (g1, g0))
    grid = (M // tm, N // tn) if rows_outer else (N // tn, M // tm)
    if mode == "TN":
        a_spec = pl.BlockSpec((K, tm), lambda g0, g1: (0, ij(g0, g1)[0]))
    else:
        a_spec = pl.BlockSpec((tm, K), lambda g0, g1: (ij(g0, g1)[0], 0))
    b_blk, b_map = ((tn, K), lambda g0, g1: (ij(g0, g1)[1], 0)) if mode == "NT" else \
                   ((K, tn), lambda g0, g1: (0, ij(g0, g1)[1]))
    b_spec = pl.BlockSpec(b_blk, b_map, pipeline_mode=pl.Buffered(1)) if single_b else pl.BlockSpec(b_blk, b_map)
    dims = {"NN": NN, "NT": NT, "TN": TN}[mode]

    def kern(a_ref, b_ref, o_ref):
        o_ref[...] = _dot(a_ref[...], b_ref[...], dims).astype(o_ref.dtype)

    ci, ca, co, cs, cscr = _carry_args(carry)
    res = pl.pallas_call(
        _carried(kern, carry, 2, 1, *_grid_ends(grid)), name=name, grid=grid, in_specs=[a_spec, b_spec] + ci,
        out_specs=[pl.BlockSpec((tm, tn), lambda g0, g1: ij(g0, g1))] + co,
        out_shape=[_sds((M, N), out_dtype)] + cs, scratch_shapes=cscr,
        compiler_params=_params(("arbitrary", "arbitrary")),
    )(a, b, *ca)
    return res[0] if carry is None else (res[0], res[1:])


def _mm_act(name, a, w, mode, out_dtype=BF16, carry=None):
    rows, K = a.shape
    N = w.shape[1] if mode == "NN" else w.shape[0]
    if K > D_FF:
        return _mm_call(name, a, w, mode, out_dtype, rows // 8, N, single_b=True, carry=carry)
    tn = N if N <= 1024 else 1408
    return _mm_call(name, a, w, mode, out_dtype, rows // 4, tn, carry=carry)


def _mm_wgrad(name, x, dy, out_dtype=BF16, carry=None):
    M = x.shape[1]
    tm = 1408 if M == D_FF else 512
    return _mm_call(name, x, dy, "TN", out_dtype, tm, dy.shape[1], single_b=True, carry=carry)


def _sigmoid(x):
    return 0.5 * jnp.tanh(0.5 * x) + 0.5


def _silu(x):
    return x * _sigmoid(x)


def _silu_grad(x):
    s = _sigmoid(x)
    return s * (1.0 + x * (1.0 - s))


_GELU_K = math.sqrt(2.0 / math.pi)


def _gelu(x):
    return 0.5 * x * (1.0 + jnp.tanh(_GELU_K * (x + 0.044715 * x * x * x)))


def _gelu_grad(x):
    t = jnp.tanh(_GELU_K * (x + 0.044715 * x * x * x))
    return 0.5 * (1.0 + t) + 0.5 * x * (1.0 - t * t) * _GELU_K * (1.0 + 3.0 * 0.044715 * x * x)


def _log_sigmoid(x):
    return jnp.minimum(x, 0.0) - jnp.log(1.0 + jnp.exp(-jnp.abs(x)))


def _rms(x):
    x = x.astype(F32)
    r = lax.rsqrt(jnp.mean(x * x, axis=-1, keepdims=True) + EPS)
    return x * r, r


def _rms_bwd(dy, y, r):
    return r * (dy - y * jnp.mean(dy * y, axis=-1, keepdims=True))


def _modrow(mod_ref, i, chunk):
    lo = mod_ref[0:1, chunk * D:(chunk + 1) * D]
    hi = mod_ref[1:2, chunk * D:(chunk + 1) * D]
    return jnp.where(i == 0, lo, hi)


def _acc_seg(acc_ref, i, val):
    zero = jnp.zeros_like(val)
    acc_ref[0:1, :] += jnp.where(i == 0, val, zero)
    acc_ref[1:2, :] += jnp.where(i == 0, zero, val)


def _colsum(x):
    return jnp.sum(x, axis=0, keepdims=True)


SH1, SC1, GA1, SH2, SC2, GA2 = range(6)


def _normmod_fwd(name, xa, g, mod, c_sh, c_sc):
    T = xa.shape[0]

    def body(i, ins, ps, outs, acc):
        y, _ = _rms(ins[0][...])
        h = (y * ps[0][...]) * (1.0 + _modrow(ps[1], i, c_sc)) + _modrow(ps[1], i, c_sh)
        outs[0][...] = h.astype(BF16)

    return _ew(name, body, T // TR, [(xa, *_rowblk(D))], [g, mod], [(_sds((T, D), BF16), *_rowblk(D))])[0]


def _modrows(mod_ref, row0, n, chunk):
    t = row0 + lax.broadcasted_iota(jnp.int32, (n, 1), 0)
    return jnp.where(t < CTX, mod_ref[0:1, chunk * D:(chunk + 1) * D], mod_ref[1:2, chunk * D:(chunk + 1) * D])


def _loss_resid_bwd(name, x_out, target, mat, gpost, mod, c_ga):
    T = x_out.shape[0]

    def body(i, ins, ps, outs, acc):
        err = ins[0][...] - ins[1][...]
        lat = i > 0
        dx = jnp.where(lat, err * (1.0 / D), 0.0)
        outs[0][...] = dx
        acc[2][...] += jnp.where(lat, _colsum(err * err), 0.0)
        outs[1][...] = _resid_bwd_vals(i, dx, ins[2][...], ps[0][...], ps[1], c_ga, acc[0], acc[1]).astype(BF16)

    tgt_blk = ((TR, D), lambda i: (jnp.maximum(i - 1, 0), 0))
    return _ew(name, body, T // TR, [(x_out, *_rowblk(D)), (target, *tgt_blk), (mat, *_rowblk(D))], [gpost, mod],
               [(_sds((T, D), F32), *_rowblk(D)), (_sds((T, D), BF16), *_rowblk(D))],
               [_sds((2, D), F32), _sds((1, D), F32), _sds((1, D), F32)])


def _mod_for(mod_ref, i, chunk, row0, n):
    return _modrow(mod_ref, i, chunk) if row0 is None else _modrows(mod_ref, row0, n, chunk)


def _acc_for(acc_ref, i, v, row0):
    if row0 is None:
        _acc_seg(acc_ref, i, _colsum(v))
        return

    @pl.when(row0 < CTX)
    def _():
        is_ctx = row0 + lax.broadcasted_iota(jnp.int32, (v.shape[0], 1), 0) < CTX
        acc_ref[0:1, :] += _colsum(jnp.where(is_ctx, v, 0.0))
        acc_ref[1:2, :] += _colsum(jnp.where(is_ctx, 0.0, v))

    @pl.when(row0 >= CTX)
    def _():
        acc_ref[1:2, :] += _colsum(v)


def _resid_bwd_vals(i, dout, mat, gpost, mod_ref, c_ga, acc_ga, acc_g, row0=None):
    ym, rm = _rms(mat)
    ga = _mod_for(mod_ref, i, c_ga, row0, dout.shape[0])
    _acc_for(acc_ga, i, dout * (ym * gpost), row0)
    dn = dout * ga
    acc_g[...] += _colsum(dn * ym)
    return _rms_bwd(dn * gpost, ym, rm)


def _normmod_bwd_vals(i, dh, xin, g, mod_ref, c_sh, c_sc, acc_sh, acc_sc, acc_g, row0=None):
    dh = dh.astype(F32)
    y, r = _rms(xin)
    _acc_for(acc_sc, i, dh * (y * g), row0)
    _acc_for(acc_sh, i, dh, row0)
    dyg = dh * (1.0 + _mod_for(mod_ref, i, c_sc, row0, dh.shape[0]))
    acc_g[...] += _colsum(dyg * y)
    return _rms_bwd(dyg * g, y, r)


def _parts(i, tm):
    return [(slice(0, tm), i * tm)]


FT = 1408


def _ffn_in_fused(name, h2, w_t, carry=None):
    T = h2.shape[0]
    tm, nj = T // 4, D_FF // FT

    def kern(a_ref, bg_ref, bu_ref, fg_ref, fu_ref, s_ref):
        for rows, _ in _parts(0, tm):
            a = a_ref[rows, :]
            g = _dot(a, bg_ref[...], NT)
            u = _dot(a, bu_ref[...], NT)
            fg_ref[rows, :] = g.astype(BF16)
            fu_ref[rows, :] = u.astype(BF16)
            s_ref[rows, :] = (_silu(g) * u).astype(BF16)

    o_spec = pl.BlockSpec((tm, FT), lambda i, j: (i, j))
    ci, ca, co, cs, cscr = _carry_args(carry)
    res = pl.pallas_call(
        _carried(kern, carry, 3, 3, *_grid_ends((4, nj))), name=name, grid=(4, nj),
        in_specs=[pl.BlockSpec((tm, D), lambda i, j: (i, 0)), pl.BlockSpec((FT, D), lambda i, j: (j, 0)),
                  pl.BlockSpec((FT, D), lambda i, j: (j + nj, 0))] + ci,
        out_specs=[o_spec] * 3 + co, out_shape=[_sds((T, D_FF), BF16)] * 3 + cs, scratch_shapes=cscr,
        compiler_params=_params(("arbitrary", "arbitrary")),
    )(h2, w_t, w_t, *ca)
    return res if carry is None else (res[:3], res[3:])


def _norm_chain(row0, xin, mat, gpost, mod_ref, c_ga, gnext, modn_ref, c_sh, c_sc):
    n = xin.shape[0]
    ym, _ = _rms(mat.astype(BF16))
    xo = xin + _modrows(mod_ref, row0, n, c_ga) * (ym * gpost)
    y, _ = _rms(xo)
    h = (y * gnext) * (1.0 + _modrows(modn_ref, row0, n, c_sc)) + _modrows(modn_ref, row0, n, c_sh)
    return xo, h.astype(BF16)


def _out_fused(name, p, u, o_all, xa, w_o_rnn, w_o_attn, w_out, gpost, mod, gnext):
    T = u.shape[0]
    tm = T // 8

    def kern(g0, g1, g2, g3, u_ref, o_ref, xa_ref, wr_ref, wa_ref, w_ref, gpost_ref, mod_ref, gnext_ref,
             ya_ref, yb_ref, z_ref, m_ref, x1_ref, h2_ref):
        for rows, row0 in _parts(pl.program_id(0), tm):
            ya = _dot(u_ref[rows, :], wr_ref[...]).astype(BF16)
            yb = _dot(o_ref[rows, :], wa_ref[...]).astype(BF16)
            ya_ref[rows, :] = ya
            yb_ref[rows, :] = yb
            ga = _sigmoid(jnp.concatenate([g0[rows, :], g1[rows, :]], axis=1).astype(F32))
            gb = _sigmoid(jnp.concatenate([g2[rows, :], g3[rows, :]], axis=1).astype(F32))
            z = (ga * ya.astype(F32) + gb * yb.astype(F32)).astype(BF16)
            z_ref[rows, :] = z
            m = _dot(z, w_ref[...])
            m_ref[rows, :] = m.astype(BF16)
            x1_ref[rows, :], h2_ref[rows, :] = _norm_chain(row0, xa_ref[rows, :], m, gpost_ref[...], mod_ref, GA1,
                                                           gnext_ref[...], mod_ref, SH2, SC2)

    row = lambda w: pl.BlockSpec((tm, w), lambda i: (i, 0))
    return pl.pallas_call(
        kern, name=name, grid=(T // tm,),
        in_specs=[pl.BlockSpec((tm, GLB), lambda i, q=q: (i, COL_GL // GLB + q)) for q in range(4)]
                 + [row(D), row(D), row(D)] + [_full_spec(a.shape) for a in (w_o_rnn, w_o_attn, w_out, gpost, mod, gnext)],
        out_specs=[row(D)] * 6,
        out_shape=[_sds((T, D), BF16)] * 4 + [_sds((T, D), F32), _sds((T, D), BF16)],
        compiler_params=_params(),
    )(p, p, p, p, u, o_all, xa, w_o_rnn, w_o_attn, w_out, gpost, mod, gnext)


def _ffn_out_fused(name, s, w, x1, gpost, mod, nxt=None):
    T = s.shape[0]
    tm = T // 8

    def kern(s_ref, w_ref, x1_ref, gpost_ref, mod_ref, *rest):
        for rows, row0 in _parts(pl.program_id(0), tm):
            e = _dot(s_ref[rows, :], w_ref[...])
            if nxt is None:
                e_ref, xo_ref = rest
                ym, _ = _rms(e.astype(BF16))
                xo_ref[rows, :] = x1_ref[rows, :] + _modrows(mod_ref, row0, e.shape[0], GA2) * (ym * gpost_ref[...])
            else:
                gnext_ref, modn_ref, e_ref, xo_ref, h_ref = rest
                xo_ref[rows, :], h_ref[rows, :] = _norm_chain(row0, x1_ref[rows, :], e, gpost_ref[...], mod_ref, GA2,
                                                              gnext_ref[...], modn_ref, SH1, SC1)
            e_ref[rows, :] = e.astype(BF16)

    row = lambda w_: pl.BlockSpec((tm, w_), lambda i: (i, 0))
    extra = [] if nxt is None else list(nxt)
    return pl.pallas_call(
        kern, name=name, grid=(T // tm,),
        in_specs=[row(D_FF), _full_spec(w.shape), row(D), _full_spec(gpost.shape), _full_spec(mod.shape)]
                 + [_full_spec(a.shape) for a in extra],
        out_specs=[row(D)] * (2 if nxt is None else 3),
        out_shape=[_sds((T, D), BF16), _sds((T, D), F32)] + ([] if nxt is None else [_sds((T, D), BF16)]),
        compiler_params=_params(),
    )(s, w, x1, gpost, mod, *extra)


def _ffn_bwd_fused(name, fg, fu, w, de=None, head=None):
    T = fg.shape[0]
    tm = T // 8
    row = lambda w_: pl.BlockSpec((tm, w_), lambda i: (i, 0))
    w_spec = pl.BlockSpec(w.shape, lambda i: (0, 0), pipeline_mode=pl.Buffered(1))

    def tail(rows, de_v, fg_ref, fu_ref, w_ref, df_ref):
        ds = _dot(de_v, w_ref[...], NT)
        g, u = fg_ref[rows, :].astype(F32), fu_ref[rows, :].astype(F32)
        df_ref[rows, :] = jnp.concatenate([ds * u * _silu_grad(g), ds * _silu(g)], axis=1).astype(BF16)

    if head is None:
        def kern(de_ref, fg_ref, fu_ref, w_ref, df_ref):
            for rows, _ in _parts(pl.program_id(0), tm):
                tail(rows, de_ref[rows, :], fg_ref, fu_ref, w_ref, df_ref)

        return pl.pallas_call(
            kern, name=name, grid=(T // tm,), in_specs=[row(D), row(D_FF), row(D_FF), w_spec],
            out_specs=[row(2 * D_FF)], out_shape=[_sds((T, 2 * D_FF), BF16)], compiler_params=_params(),
        )(de, fg, fu, w)

    dx2, e, gpost, mod = head

    def kern(dx_ref, e_ref, fg_ref, fu_ref, w_ref, gpost_ref, mod_ref, de_ref, df_ref, dga_ref, dg_ref):
        i = pl.program_id(0)

        @pl.when(i == 0)
        def _():
            dga_ref[...] = jnp.zeros(dga_ref.shape, F32)
            dg_ref[...] = jnp.zeros(dg_ref.shape, F32)

        for rows, row0 in _parts(i, tm):
            de_v = _resid_bwd_vals(i, dx_ref[rows, :], e_ref[rows, :], gpost_ref[...], mod_ref, GA2, dga_ref, dg_ref,
                                   row0=row0).astype(BF16)
            de_ref[rows, :] = de_v
            tail(rows, de_v, fg_ref, fu_ref, w_ref, df_ref)

    return pl.pallas_call(
        kern, name=name, grid=(T // tm,),
        in_specs=[row(D), row(D), row(D_FF), row(D_FF), w_spec, _full_spec(gpost.shape), _full_spec(mod.shape)],
        out_specs=[row(D), row(2 * D_FF), _full_spec((2, D)), _full_spec((1, D))],
        out_shape=[_sds((T, D), BF16), _sds((T, 2 * D_FF), BF16), _sds((2, D), F32), _sds((1, D), F32)],
        compiler_params=_params(),
    )(dx2, e, fg, fu, w, gpost, mod)


def _zero_at_start(i, refs):
    @pl.when(i == 0)
    def _():
        for r in refs:
            r[...] = jnp.zeros(r.shape, F32)


def _proj_bwd_fused(name, dp, dgl, w_in_t, xa, dx1, gpre, mod, carry=None):
    T = dp.shape[0]
    tm = T // 8
    row = lambda w_: pl.BlockSpec((tm, w_), lambda i: (i, 0))

    def kern(dp_ref, dgl_ref, w_ref, xa_ref, dx1_ref, g_ref, mod_ref, dxa_ref, dsh_ref, dsc_ref, dg_ref):
        i = pl.program_id(0)
        _zero_at_start(i, (dsh_ref, dsc_ref, dg_ref))
        for rows, row0 in _parts(i, tm):
            dh = _dot(dp_ref[rows, :], w_ref[0:DP_W, :]) + _dot(dgl_ref[rows, :], w_ref[DP_W:, :])
            dxa_ref[rows, :] = dx1_ref[rows, :] + _normmod_bwd_vals(i, dh, xa_ref[rows, :], g_ref[...], mod_ref, SH1,
                                                                    SC1, dsh_ref, dsc_ref, dg_ref, row0=row0)

    ci, ca, co, cs, cscr = _carry_args(carry)
    res = pl.pallas_call(
        _carried(kern, carry, 7, 4, *_grid_ends((T // tm,))), name=name, grid=(T // tm,),
        in_specs=[row(DP_W), row(P_W - DP_W),
                  pl.BlockSpec(w_in_t.shape, lambda i: (0, 0), pipeline_mode=pl.Buffered(1)), row(D), row(D),
                  _full_spec(gpre.shape), _full_spec(mod.shape)] + ci,
        out_specs=[row(D), _full_spec((2, D)), _full_spec((2, D)), _full_spec((1, D))] + co,
        out_shape=[_sds((T, D), F32), _sds((2, D), F32), _sds((2, D), F32), _sds((1, D), F32)] + cs,
        scratch_shapes=cscr, compiler_params=_params(),
    )(dp, dgl, w_in_t, xa, dx1, gpre, mod, *ca)
    return res if carry is None else (res[:4], res[4:])


def _proj_wgrad(name, dp, dgl, h, carry=None):
    T, N = h.shape
    n1, n2 = DP_W // GLB, (P_W - DP_W) // GLB

    def kern(a1_ref, a2_ref, h_ref, o_ref):
        i = pl.program_id(0)

        @pl.when(i < n1)
        def _():
            o_ref[...] = _dot(a1_ref[...], h_ref[...], TN).astype(o_ref.dtype)

        @pl.when(i >= n1)
        def _():
            o_ref[...] = _dot(a2_ref[...], h_ref[...], TN).astype(o_ref.dtype)

    ci, ca, co, cs, cscr = _carry_args(carry)
    res = pl.pallas_call(
        _carried(kern, carry, 3, 1, *_grid_ends((n1 + n2,))), name=name, grid=(n1 + n2,),
        in_specs=[pl.BlockSpec((T, GLB), lambda i: (0, jnp.minimum(i, n1 - 1))),
                  pl.BlockSpec((T, GLB), lambda i: (0, jnp.maximum(i - n1, 0))),
                  pl.BlockSpec((T, N), lambda i: (0, 0), pipeline_mode=pl.Buffered(1))] + ci,
        out_specs=[pl.BlockSpec((GLB, N), lambda i: (i, 0))] + co,
        out_shape=[_sds((P_W, N), BF16)] + cs, scratch_shapes=cscr, compiler_params=_params(),
    )(dp, dgl, h, *ca)
    return res[0] if carry is None else (res[0], res[1:])


def _ffn_in_bwd_fused(name, df, w_t, x1, dres, mat, gpre, mod, gpost, carry=None):
    T = df.shape[0]
    tm = T // 8
    row = lambda w_: pl.BlockSpec((tm, w_), lambda i: (i, 0))

    def kern(df_ref, w_ref, x1_ref, dres_ref, mat_ref, gpre_ref, mod_ref, gpost_ref,
             dx1_ref, dm_ref, dsh_ref, dsc_ref, dgpre_ref, dga_ref, dgpost_ref):
        i = pl.program_id(0)
        _zero_at_start(i, (dsh_ref, dsc_ref, dgpre_ref, dga_ref, dgpost_ref))
        for rows, row0 in _parts(i, tm):
            dh2 = _dot(df_ref[rows, :], w_ref[...])
            dx1 = dres_ref[rows, :] + _normmod_bwd_vals(i, dh2, x1_ref[rows, :], gpre_ref[...], mod_ref, SH2, SC2,
                                                        dsh_ref, dsc_ref, dgpre_ref, row0=row0)
            dx1_ref[rows, :] = dx1
            dm_ref[rows, :] = _resid_bwd_vals(i, dx1, mat_ref[rows, :], gpost_ref[...], mod_ref, GA1, dga_ref,
                                              dgpost_ref, row0=row0).astype(BF16)

    ci, ca, co, cs, cscr = _carry_args(carry)
    res = pl.pallas_call(
        _carried(kern, carry, 8, 7, *_grid_ends((T // tm,))), name=name, grid=(T // tm,),
        in_specs=[row(2 * D_FF), pl.BlockSpec(w_t.shape, lambda i: (0, 0), pipeline_mode=pl.Buffered(1)), row(D),
                  row(D), row(D), _full_spec(gpre.shape), _full_spec(mod.shape), _full_spec(gpost.shape)] + ci,
        out_specs=[row(D), row(D), _full_spec((2, D)), _full_spec((2, D)), _full_spec((1, D)), _full_spec((2, D)),
                   _full_spec((1, D))] + co,
        out_shape=[_sds((T, D), F32), _sds((T, D), BF16), _sds((2, D), F32), _sds((2, D), F32), _sds((1, D), F32),
                   _sds((2, D), F32), _sds((1, D), F32)] + cs,
        scratch_shapes=cscr, compiler_params=_params(),
    )(df, w_t, x1, dres, mat, gpre, mod, gpost, *ca)
    return res if carry is None else (res[:7], res[7:])


def _out_bwd_fused(name, dm, w_out, w_o_rnn, w_o_attn, p, ya, yb):
    T = dm.shape[0]
    tm = T // 8
    row = lambda w_: pl.BlockSpec((tm, w_), lambda i: (i, 0))

    def kern(dm_ref, w_ref, wr_ref, wa_ref, g0, g1, g2, g3, ya_ref, yb_ref, dya_ref, dyb_ref, dgl_ref, du_ref, do_ref):
        for rows, _ in _parts(pl.program_id(0), tm):
            dz = _dot(dm_ref[rows, :], w_ref[...], NT)
            ga = _sigmoid(jnp.concatenate([g0[rows, :], g1[rows, :]], axis=1).astype(F32))
            gb = _sigmoid(jnp.concatenate([g2[rows, :], g3[rows, :]], axis=1).astype(F32))
            dya = (dz * ga).astype(BF16)
            dyb = (dz * gb).astype(BF16)
            dya_ref[rows, :] = dya
            dyb_ref[rows, :] = dyb
            dgl_ref[rows, :] = jnp.concatenate([dz * ya_ref[rows, :].astype(F32) * ga * (1.0 - ga),
                                                dz * yb_ref[rows, :].astype(F32) * gb * (1.0 - gb)],
                                               axis=1).astype(BF16)
            du_ref[rows, :] = _dot(dya, wr_ref[...], NT).astype(BF16)
            do_ref[rows, :] = _dot(dyb, wa_ref[...], NT).astype(BF16)

    return pl.pallas_call(
        kern, name=name, grid=(T // tm,),
        in_specs=[row(D)] + [_full_spec(w.shape) for w in (w_out, w_o_rnn, w_o_attn)]
                 + [pl.BlockSpec((tm, GLB), lambda i, q=q: (i, COL_GL // GLB + q)) for q in range(4)] + [row(D), row(D)],
        out_specs=[row(D), row(D), row(2 * D), row(D), row(D)],
        out_shape=[_sds((T, D), BF16), _sds((T, D), BF16), _sds((T, 2 * D), BF16), _sds((T, D), BF16),
                   _sds((T, D), BF16)],
        compiler_params=_params(),
    )(dm, w_out, w_o_rnn, w_o_attn, p, p, p, p, ya, yb)


AB = 128
CTX_BLKS = CTX // AB


def _rope_tables(S):
    pos = jnp.arange(S, dtype=jnp.int32)
    inv = ROPE_BASE ** (-jnp.arange(N_FREQ, dtype=F32) / N_FREQ)
    ang_r = (pos // GRID_W).astype(F32)[:, None] * inv[None, :]
    ang_c = (pos % GRID_W).astype(F32)[:, None] * inv[None, :]
    cos = jnp.concatenate([jnp.cos(ang_r)] * 2 + [jnp.cos(ang_c)] * 2, axis=1)
    sin = jnp.concatenate([-jnp.sin(ang_r), jnp.sin(ang_r), -jnp.sin(ang_c), jnp.sin(ang_c)], axis=1)
    return cos, sin


def _rope(x, cos, sin):
    w = x.shape[1]
    reps = w // HEAD
    lane = lax.broadcasted_iota(jnp.int32, x.shape, 1)
    partner = jnp.where((lane & 63) < 32, pltpu.roll(x, w - 32, 1), pltpu.roll(x, 32, 1))
    return x * jnp.tile(cos, (1, reps)) + partner * jnp.tile(sin, (1, reps))


def _unrope(dx, cos, sin):
    w = dx.shape[1]
    reps = w // HEAD
    lane = lax.broadcasted_iota(jnp.int32, dx.shape, 1)
    t = dx * jnp.tile(sin, (1, reps))
    partner = jnp.where((lane & 63) < 32, pltpu.roll(t, w - 32, 1), pltpu.roll(t, 32, 1))
    return dx * jnp.tile(cos, (1, reps)) + partner


def _qkv_prep(name, p, cos, sin, S):
    T = CTX + S
    nt = T // AB
    KW = N_KV * HEAD

    def with_ones(v):
        ones = jnp.ones((AB, HEAD), BF16)
        return jnp.concatenate([v[:, kh * HEAD:(kh + 1) * HEAD] if part == 0 else ones
                                for kh in range(N_KV) for part in range(2)], axis=1)

    def kern(q_ref, k_ref, v_ref, cos_ref, sin_ref, qa_ref, kp_ref, vp_ref, kc_ref, vc_ref):
        i = pl.program_id(0)
        cos_v, sin_v = cos_ref[...], sin_ref[...]
        @pl.when(i < CTX_BLKS)
        def _():
            qa_ref[...] = (q_ref[...].astype(F32) * ATT_SCALE).astype(BF16)
            kc_ref[...] = k_ref[...]
            vc_ref[...] = with_ones(v_ref[...])

        @pl.when((i < CTX_BLKS) | (i >= nt))
        def _():
            kp_ref[...] = jnp.zeros(kp_ref.shape, BF16)
            vp_ref[...] = jnp.zeros(vp_ref.shape, BF16)

        @pl.when((i >= CTX_BLKS) & (i < nt))
        def _():
            qa_ref[...] = (_rope(q_ref[...].astype(F32), cos_v, sin_v) * ATT_SCALE).astype(BF16)
            kp_ref[...] = _rope(k_ref[...].astype(F32), cos_v, sin_v).astype(BF16)
            vp_ref[...] = with_ones(v_ref[...])

    tok = lambda i: jnp.minimum(i, nt - 1)
    lat_map = lambda i: (jnp.clip(i - CTX_BLKS, 0, nt - CTX_BLKS - 1), 0)
    ctx_map = lambda i: (jnp.minimum(i, CTX_BLKS - 1), 0)
    return pl.pallas_call(
        kern, name=name, grid=(nt + CTX_BLKS,),
        in_specs=[pl.BlockSpec((AB, N_Q * HEAD), lambda i: (tok(i), COL_Q // (N_Q * HEAD))),
                  pl.BlockSpec((AB, KW), lambda i: (tok(i), COL_K // KW)),
                  pl.BlockSpec((AB, KW), lambda i: (tok(i), COL_V // KW)),
                  pl.BlockSpec((AB, HEAD), lat_map), pl.BlockSpec((AB, HEAD), lat_map)],
        out_specs=[pl.BlockSpec((AB, N_Q * HEAD), lambda i: (tok(i), 0)),
                   pl.BlockSpec((AB, KW), lambda i: (i, 0)), pl.BlockSpec((AB, 2 * KW), lambda i: (i, 0)),
                   pl.BlockSpec((AB, KW), ctx_map), pl.BlockSpec((AB, 2 * KW), ctx_map)],
        out_shape=[_sds((T, N_Q * HEAD), BF16), _sds((S + 2 * CTX, KW), BF16), _sds((S + 2 * CTX, 2 * KW), BF16),
                   _sds((CTX, KW), BF16), _sds((CTX, 2 * KW), BF16)],
        compiler_params=_params(),
    )(p, p, p, cos, sin)


GW = Q_PER_KV * HEAD
HG = Q_PER_KV


def _band_bias(S):
    r = jnp.arange(AB, dtype=jnp.int32)[:, None]
    c = jnp.arange(3 * AB, dtype=jnp.int32)[None, :]
    near = jnp.abs(c - AB - r) <= AB
    valid = jnp.stack([near & (c >= AB), near, near & (c < 2 * AB)])
    return jnp.where(valid, 0.0, NEG_INF).astype(F32)


def _bias_spec(S):
    nb = S // AB
    return pl.BlockSpec((None, AB, 3 * AB), lambda kh, n: (jnp.where(n == 0, 0, jnp.where(n == nb - 1, 2, 1)), 0, 0))


def _head_probs(q, sink, kc, vce, kb, vbe, bias):
    s_c = _dot(q, kc, NT)
    m = jnp.maximum(jnp.max(s_c, axis=-1, keepdims=True), sink)
    if kb is not None:
        s_b = _dot(q, kb, NT) + bias
        m = jnp.maximum(m, jnp.max(s_b, axis=-1, keepdims=True))
    p_c = jnp.exp(s_c - m).astype(BF16)
    acc = _dot(p_c, vce)
    p_b = None
    if kb is not None:
        p_b = jnp.exp(s_b - m).astype(BF16)
        acc = acc + _dot(p_b, vbe)
    return p_c, p_b, m, acc


def _attn_fwd(name, qa, kc, vc, sink4, S, band=None, prev=None, carry=None):
    T = qa.shape[0]
    has_band = band is not None
    nq = S // AB if has_band else CTX_BLKS
    q_off = CTX_BLKS if has_band else 0

    def kern(*refs):
        q_ref, kc_ref, vc_ref, sink_ref = refs[:4]
        rest = refs[4:]
        o_ref = rest[-1]
        n = pl.program_id(1)
        kc_v, vce = kc_ref[...], vc_ref[...]
        kb = vbe = bias = None
        if has_band:
            kp_ref, vp_ref, bias_ref = rest[:3]
            start = pl.multiple_of(n * AB + (CTX - AB), AB)
            kb = kp_ref[pl.ds(start, 3 * AB), :]
            vbe = vp_ref[pl.ds(start, 3 * AB), :]
            bias = bias_ref[...]
        outs = []
        for g in range(Q_PER_KV):
            sink = sink_ref[g:g + 1, 0:1]
            _, _, m, acc = _head_probs(q_ref[:, g * HEAD:(g + 1) * HEAD], sink, kc_v, vce, kb, vbe, bias)
            l = acc[:, HEAD:] + jnp.exp(sink - m)
            outs.append(acc[:, :HEAD] / l)
        o_ref[...] = jnp.concatenate(outs, axis=1).astype(BF16)

    in_specs = [pl.BlockSpec((AB, GW), lambda kh, n: (n + q_off, kh)),
                pl.BlockSpec((CTX, HEAD), lambda kh, n: (0, kh)), pl.BlockSpec((CTX, 2 * HEAD), lambda kh, n: (0, kh)),
                pl.BlockSpec((None, Q_PER_KV, HEAD), lambda kh, n: (kh, 0, 0))]
    args = [qa, kc, vc, sink4]
    if has_band:
        in_specs += [pl.BlockSpec((S + 2 * CTX, HEAD), lambda kh, n: (0, kh)),
                     pl.BlockSpec((S + 2 * CTX, 2 * HEAD), lambda kh, n: (0, kh)), _bias_spec(S)]
        args += list(band)
    alias = {}
    if prev is not None:
        in_specs.append(ANY)
        alias = {len(args): 0}
        args.append(prev)
    ci, ca, co, cs, cscr = _carry_args(carry)
    res = pl.pallas_call(
        _carried(kern, carry, len(args), 1, *_grid_ends((N_KV, nq))), name=name, grid=(N_KV, nq),
        in_specs=in_specs + ci,
        out_specs=[pl.BlockSpec((AB, GW), lambda kh, n: (n + q_off, kh))] + co,
        out_shape=[_sds((T, N_Q * HEAD), BF16)] + cs, input_output_aliases=alias, scratch_shapes=cscr,
        compiler_params=_params(("arbitrary", "arbitrary")),
    )(*args, *ca)
    return res[0] if carry is None else (res[0], res[1:])


def _attn_bwd(name, qa, kc, vc, sink4, o_all, do_all, S, band=None, prev_dq=None, carry=None):
    T = qa.shape[0]
    has_band = band is not None
    nq = S // AB if has_band else CTX_BLKS
    q_off = CTX_BLKS if has_band else 0
    KW = N_KV * HEAD

    def kern(*refs):
        q_ref, kc_ref, vc_ref, sink_ref, o_ref, do_ref = refs[:6]
        rest = refs[6:]
        if has_band:
            kp_ref, vp_ref, bias_ref, cos_ref, sin_ref = rest[:5]
            rest = rest[5:]
        if prev_dq is not None:
            rest = rest[1:]
        dq_ref, dkc_ref, dvc_ref, dsink_ref = rest[:4]
        n = pl.program_id(1)

        @pl.when(n == 0)
        def _():
            dkc_ref[...] = jnp.zeros(dkc_ref.shape, F32)
            dvc_ref[...] = jnp.zeros(dvc_ref.shape, F32)
            dsink_ref[...] = jnp.zeros(dsink_ref.shape, F32)
            if has_band:
                rest[4][...] = jnp.zeros(rest[4].shape, F32)
                rest[5][...] = jnp.zeros(rest[5].shape, F32)

        kc_v, vce = kc_ref[...], vc_ref[...]
        vc_v = vce[:, :HEAD]
        kb = vbe = vb = bias = None
        if has_band:
            start = pl.multiple_of(n * AB + (CTX - AB), AB)
            kb = kp_ref[pl.ds(start, 3 * AB), :]
            vbe = vp_ref[pl.ds(start, 3 * AB), :]
            vb = vbe[:, :HEAD]
            bias = bias_ref[...]
        dq_parts, dsink_parts = [], []
        for g0 in range(0, Q_PER_KV, HG):
            heads = range(g0, g0 + HG)
            stack = lambda ref: jnp.concatenate([ref[:, g * HEAD:(g + 1) * HEAD] for g in heads], axis=0)
            q4, do4 = stack(q_ref), stack(do_ref)
            sink = jnp.concatenate([jnp.broadcast_to(sink_ref[g:g + 1, 0:1], (AB, 1)) for g in heads], axis=0)
            s_c = _dot(q4, kc_v, NT)
            m = jnp.maximum(jnp.max(s_c, axis=-1, keepdims=True), sink)
            if has_band:
                s_b = _dot(q4, kb, NT) + jnp.tile(bias, (HG, 1))
                m = jnp.maximum(m, jnp.max(s_b, axis=-1, keepdims=True))
            p_c = jnp.exp(s_c - m).astype(BF16).astype(F32)
            p_sink = jnp.exp(sink - m)
            l = jnp.sum(p_c, axis=-1, keepdims=True) + p_sink
            if has_band:
                p_b = jnp.exp(s_b - m).astype(BF16).astype(F32)
                l = l + jnp.sum(p_b, axis=-1, keepdims=True)
            inv = 1.0 / l
            delta = jnp.sum(do4.astype(F32) * stack(o_ref).astype(F32), axis=-1, keepdims=True)
            do4b = do4.astype(BF16)
            pn_c = (p_c * inv).astype(BF16)
            ds_c = (p_c * inv * (_dot(do4b, vc_v, NT) - delta)).astype(BF16)
            dq4 = _dot(ds_c, kc_v)
            dkc_ref[...] += _dot(q4, ds_c, TN)
            dvc_ref[...] += _dot(do4b, pn_c, TN)
            if has_band:
                pn_b = (p_b * inv).astype(BF16)
                ds_b = (p_b * inv * (_dot(do4b, vb, NT) - delta)).astype(BF16)
                dq4 = dq4 + _dot(ds_b, kb)
                rest[4][:, pl.ds(start, 3 * AB)] += _dot(q4, ds_b, TN)
                rest[5][:, pl.ds(start, 3 * AB)] += _dot(do4b, pn_b, TN)
            dq4 = dq4 * ATT_SCALE
            dq_parts += [dq4[k * AB:(k + 1) * AB, :] for k in range(HG)]
            ps = p_sink * inv * delta
            dsink_parts += [jnp.broadcast_to(-jnp.sum(ps[k * AB:(k + 1) * AB, :], axis=0, keepdims=True), (1, HEAD))
                            for k in range(HG)]
        dq = jnp.concatenate(dq_parts, axis=1)
        dq_ref[...] = (_unrope(dq, cos_ref[...], sin_ref[...]) if has_band else dq).astype(BF16)
        dsink_ref[...] += jnp.concatenate(dsink_parts, axis=0)

    q_spec = pl.BlockSpec((AB, GW), lambda kh, n: (n + q_off, kh))
    c_spec = pl.BlockSpec((CTX, HEAD), lambda kh, n: (0, kh))
    ce_spec = pl.BlockSpec((CTX, 2 * HEAD), lambda kh, n: (0, kh))
    s_spec = pl.BlockSpec((None, Q_PER_KV, HEAD), lambda kh, n: (kh, 0, 0))
    in_specs = [q_spec, c_spec, ce_spec, s_spec, q_spec, q_spec]
    args = [qa, kc, vc, sink4, o_all, do_all]
    ct_spec = pl.BlockSpec((HEAD, CTX), lambda kh, n: (kh, 0))
    dq_spec = pl.BlockSpec((AB, GW), lambda kh, n: (n + q_off, COL_Q // GW + kh))
    out_specs = [dq_spec, ct_spec, ct_spec, s_spec]
    out_shape = [_sds((T, DP_W), BF16), _sds((KW, CTX), F32), _sds((KW, CTX), F32), _sds((N_KV, Q_PER_KV, HEAD), F32)]
    if has_band:
        p_spec = pl.BlockSpec((S + 2 * CTX, HEAD), lambda kh, n: (0, kh))
        pt_spec = pl.BlockSpec((HEAD, S + 2 * CTX), lambda kh, n: (kh, 0))
        rope_spec = pl.BlockSpec((AB, HEAD), lambda kh, n: (n, 0))
        in_specs += [p_spec, pl.BlockSpec((S + 2 * CTX, 2 * HEAD), lambda kh, n: (0, kh)), _bias_spec(S), rope_spec,
                     rope_spec]
        args += list(band)
        out_specs += [pt_spec, pt_spec]
        out_shape += [_sds((KW, S + 2 * CTX), F32)] * 2
    alias = {}
    if prev_dq is not None:
        in_specs.append(ANY)
        alias = {len(args): 0}
        args.append(prev_dq)
    ci, ca, co, cs, cscr = _carry_args(carry)
    n_out = len(out_specs)
    res = pl.pallas_call(
        _carried(kern, carry, len(args), n_out, *_grid_ends((N_KV, nq))), name=name, grid=(N_KV, nq),
        in_specs=in_specs + ci, out_specs=out_specs + co, out_shape=out_shape + cs, scratch_shapes=cscr,
        input_output_aliases=alias, compiler_params=_params(("arbitrary", "arbitrary")),
    )(*args, *ca)
    return res if carry is None else (res[:n_out], res[n_out:])


def _dkv_assemble(name, dp, dkp, dvp, dkc_l, dvc_l, dkc_c, dvc_c, cos, sin, S):
    T = CTX + S
    KW = N_KV * HEAD

    def kern(dkp_ref, dvp_ref, dkcl_ref, dvcl_ref, dkcc_ref, dvcc_ref, cos_ref, sin_ref, dp_in, out_ref):
        i = pl.program_id(0)

        @pl.when(i == 0)
        def _():
            out_ref[...] = jnp.concatenate([(dkcl_ref[...] + dkcc_ref[...]).T, (dvcl_ref[...] + dvcc_ref[...]).T],
                                           axis=1).astype(BF16)

        @pl.when(i > 0)
        def _():
            out_ref[...] = jnp.concatenate([_unrope(dkp_ref[...].T, cos_ref[...], sin_ref[...]), dvp_ref[...].T],
                                           axis=1).astype(BF16)

    same = lambda i: (0, i)
    lat_map = lambda i: (jnp.maximum(i - 1, 0), 0)
    ctx_map = lambda i: (0, 0)
    return pl.pallas_call(
        kern, name=name, grid=(T // TR,),
        in_specs=[pl.BlockSpec((KW, TR), same), pl.BlockSpec((KW, TR), same),
                  pl.BlockSpec((KW, CTX), ctx_map), pl.BlockSpec((KW, CTX), ctx_map),
                  pl.BlockSpec((KW, CTX), ctx_map), pl.BlockSpec((KW, CTX), ctx_map),
                  pl.BlockSpec((TR, HEAD), lat_map), pl.BlockSpec((TR, HEAD), lat_map), ANY],
        out_specs=pl.BlockSpec((TR, 2 * KW), lambda i: (i, COL_K // (2 * KW))),
        out_shape=_sds((T, DP_W), BF16), input_output_aliases={8: 0}, compiler_params=_params(),
    )(dkp, dvp, dkc_l, dvc_l, dkc_c, dvc_c, cos, sin, dp)


RB = 128
CH = 256
HALO = 8
SUB = 8
GRP = 8


def _vscan(a, b, reverse):
    row = lax.broadcasted_iota(jnp.int32, a.shape, 0)
    A, H = a, b
    for s in (1, 2, 4):
        sh = SUB - s if reverse else s
        m = (row < SUB - s) if reverse else (row >= s)
        As = pltpu.roll(A, sh, 0)
        Hs = pltpu.roll(H, sh, 0)
        H = jnp.where(m, A * Hs + H, H)
        A = jnp.where(m, A * As, A)
    return A, H


def _scan_rows(a_ref, b_ref, r0, nrows, reverse, carry, emit):
    ngrp = nrows // (SUB * GRP)
    row = lax.broadcasted_iota(jnp.int32, (SUB, RB), 0)

    def grp(gi, carry):
        g = (ngrp - 1 - gi) if reverse else gi
        base = r0 + g * (SUB * GRP)
        for v in (range(GRP - 1, -1, -1) if reverse else range(GRP)):
            rs = pl.multiple_of(base + v * SUB, SUB)
            A, H = _vscan(a_ref[pl.ds(rs, SUB), :], b_ref[pl.ds(rs, SUB), :], reverse)
            hf = H + A * carry
            if reverse:
                before = jnp.where(row == SUB - 1, carry, pltpu.roll(hf, SUB - 1, 0))
                carry = hf[0:1, :]
            else:
                before = jnp.where(row == 0, carry, pltpu.roll(hf, 1, 0))
                carry = hf[SUB - 1:SUB, :]
            emit(rs, hf, before)
        return carry

    return lax.fori_loop(0, ngrp, grp, carry)


def _pad_start(ci):
    return pl.multiple_of(ci * CH + HALO * jnp.minimum(ci, 1), HALO)


def _conv_taps(ext, transpose=False):
    n = CH + 2 * HALO
    taps = []
    for k in range(CONV_W):
        off = CONV_LEFT - k if transpose else k - CONV_LEFT
        taps.append(ext[HALO:HALO + CH, :] if off == 0 else pltpu.roll(ext, (-off) % n, 0)[HALO:HALO + CH, :])
    return taps


def _lru_gates(xl, w4, b4, ls):
    pre = _dot(xl.astype(BF16), w4) + b4
    out = []
    for d in range(2):
        r = _sigmoid(pre[:, d * RB:(d + 1) * RB])
        i = _sigmoid(pre[:, (2 + d) * RB:(3 + d) * RB])
        la = LRU_C * r * ls[d:d + 1, :]
        a = jnp.exp(la)
        q = -jnp.tanh(la) * (1.0 + a * a)
        out.append((r, i, a, q))
    return out


def _rnn_specs(T):
    col = lambda n, *_: (0, n)
    return dict(
        xr=pl.BlockSpec((T, RB), lambda n, *_: (0, COL_XR // RB + n)),
        gr=pl.BlockSpec((T, RB), lambda n, *_: (0, COL_GR // RB + n)),
        act=pl.BlockSpec((T, RB), col),
        cw=pl.BlockSpec((CONV_W, RB), col), cb=pl.BlockSpec((1, RB), col),
        w4=pl.BlockSpec((None, RB, 4 * RB), lambda n, *_: (n, 0, 0)),
        b4=pl.BlockSpec((None, 1, 4 * RB), lambda n, *_: (n, 0, 0)),
        lam=pl.BlockSpec((2, RB), col))


PAD_ROWS = 3 * HALO


def _zero_pads(pad_ref, T):
    for r in (0, HALO + CTX, 2 * HALO + T):
        pad_ref[r:r + HALO, :] = jnp.zeros((HALO, RB), F32)


def _fill_padded(pad_ref, src_ref, T):
    _zero_pads(pad_ref, T)
    pad_ref[HALO:HALO + CTX, :] = src_ref[0:CTX, :].astype(F32)
    pad_ref[2 * HALO + CTX:2 * HALO + T, :] = src_ref[CTX:T, :].astype(F32)


def _pad_rows(ci):
    return pl.ds(pl.multiple_of(ci * CH + HALO + HALO * jnp.minimum(ci, 1), HALO), CH)


def _rnn_fwd(name, p, cw, cb, w4, b4, lam, T, carry=None):
    def kern(xr_ref, gr_ref, cw_ref, cb_ref, w4_ref, b4_ref, lam_ref,
             u_ref, a0, a1, yo_ref, hpf_ref, hpb_ref, r0_ref, r1_ref, i0_ref, i1_ref, xpad, b0, b1, y):
        _fill_padded(xpad, xr_ref, T)
        ls = _log_sigmoid(lam_ref[...])
        w4v, b4v, cwv, cbv = w4_ref[...], b4_ref[...], cw_ref[...], cb_ref[...]

        def chunk(ci, _):
            rows = pl.ds(pl.multiple_of(ci * CH, CH), CH)
            taps = _conv_taps(xpad[pl.ds(_pad_start(ci), CH + 2 * HALO), :])
            xl = cbv + sum(taps[k] * cwv[k:k + 1, :] for k in range(CONV_W))
            for d, (r, i, a, q) in enumerate(_lru_gates(xl, w4v, b4v, ls)):
                (a0, a1)[d][rows, :] = a
                (b0, b1)[d][rows, :] = jnp.sqrt(q) * (i * xl)
                (r0_ref, r1_ref)[d][rows, :] = r.astype(BF16)
                (i0_ref, i1_ref)[d][rows, :] = i.astype(BF16)
            return 0

        lax.fori_loop(0, T // CH, chunk, 0)
        zero = jnp.zeros((1, RB), F32)

        def emit_f(rs, hf, before):
            y[pl.ds(rs, SUB), :] = hf
            b0[pl.ds(rs, SUB), :] = before

        def emit_b(rs, hf, before):
            y[pl.ds(rs, SUB), :] += hf
            b1[pl.ds(rs, SUB), :] = before

        _scan_rows(a0, b0, 0, T, False, zero, emit_f)
        c = _scan_rows(a1, b1, 0, CTX, True, zero, emit_b)
        _scan_rows(a1, b1, CTX, T - CTX, True, c, emit_b)

        def finish(ci, _):
            rows = pl.ds(pl.multiple_of(ci * CH, CH), CH)
            yv = y[rows, :]
            u_ref[rows, :] = (yv * _gelu(gr_ref[rows, :].astype(F32))).astype(BF16)
            yo_ref[rows, :] = yv.astype(BF16)
            hpf_ref[rows, :] = b0[rows, :].astype(BF16)
            hpb_ref[rows, :] = b1[rows, :].astype(BF16)
            return 0

        lax.fori_loop(0, T // CH, finish, 0)

    sp = _rnn_specs(T)
    ci, ca, co, cs, cscr = _carry_args(carry)
    dts = [BF16, F32, F32] + [BF16] * 7
    res = pl.pallas_call(
        _carried(kern, carry, 7, 10, *_grid_ends((N_RNN_BLOCKS,))), name=name, grid=(N_RNN_BLOCKS,),
        in_specs=[sp["xr"], sp["gr"], sp["cw"], sp["cb"], sp["w4"], sp["b4"], sp["lam"]] + ci,
        out_specs=[sp["act"]] * 10 + co,
        out_shape=[_sds((T, D), dt) for dt in dts] + cs,
        scratch_shapes=[pltpu.VMEM((T + PAD_ROWS, RB), F32)] + [pltpu.VMEM((T, RB), F32)] * 3 + cscr,
        compiler_params=_params(),
    )(p, p, cw, cb, w4, b4, lam, *ca)
    return res if carry is None else (res[:10], res[10:])


def _rnn_bwd(name, p, du, saved, dp, cw, cb, w4, b4, lam, T, carry=None):
    def kern(xr_ref, gr_ref, du_ref, a0, a1, y_ref, hpf_ref, hpb_ref, r0_ref, r1_ref, i0_ref, i1_ref,
             cw_ref, cb_ref, w4_ref, b4_ref, lam_ref, dp_in,
             dp_ref, dcw_ref, dcb_ref, dw4_ref, db4_ref, dlam_ref,
             xpad, dxpad, c0, c1, dy):
        j = pl.program_id(1)

        @pl.when(j == 0)
        def _():
            scans(gr_ref, du_ref, a0, a1, y_ref, dp_ref, c0, c1, dy)

        @pl.when(j == 1)
        def _():
            gates(xr_ref, a0, a1, (hpf_ref, hpb_ref), (r0_ref, r1_ref), (i0_ref, i1_ref), cw_ref, cb_ref, w4_ref,
                  lam_ref, dp_ref, dcw_ref, dcb_ref, dw4_ref, db4_ref, dlam_ref, xpad, dxpad, c0, c1)

    def scans(gr_ref, du_ref, a0, a1, y_ref, dgr_ref, c0, c1, dy):
        def phase_a(ci, _):
            rows = pl.ds(pl.multiple_of(ci * CH, CH), CH)
            gr = gr_ref[rows, :].astype(F32)
            duv = du_ref[rows, :].astype(F32)
            dyv = duv * _gelu(gr)
            dgr_ref[rows, :] = (duv * y_ref[rows, :].astype(F32) * _gelu_grad(gr)).astype(BF16)
            dy[rows, :] = dyv
            c0[rows, :] = a0[rows, :] * dyv
            c1[rows, :] = a1[rows, :] * dyv
            return 0

        lax.fori_loop(0, T // CH, phase_a, 0)
        zero = jnp.zeros((1, RB), F32)

        def emit0(rs, hf, before):
            c0[pl.ds(rs, SUB), :] = dy[pl.ds(rs, SUB), :] + before

        def emit1(rs, hf, before):
            c1[pl.ds(rs, SUB), :] = dy[pl.ds(rs, SUB), :] + before

        _scan_rows(a0, c0, 0, T, True, zero, emit0)
        c = _scan_rows(a1, c1, CTX, T - CTX, False, zero, emit1)
        _scan_rows(a1, c1, 0, CTX, False, c, emit1)

    def gates(xr_ref, a0, a1, hp_refs, r_refs, i_refs, cw_ref, cb_ref, w4_ref, lam_ref,
              dxr_ref, dcw_ref, dcb_ref, dw4_ref, db4_ref, dlam_ref, xpad, dxpad, c0, c1):
        _fill_padded(xpad, xr_ref, T)
        _zero_pads(dxpad, T)
        lam_v = lam_ref[...]
        ls = _log_sigmoid(lam_v)
        w4v, cwv, cbv = w4_ref[...], cw_ref[...], cb_ref[...]

        def conv_chunk(ci):
            taps = _conv_taps(xpad[pl.ds(_pad_start(ci), CH + 2 * HALO), :])
            return taps, cbv + sum(taps[k] * cwv[k:k + 1, :] for k in range(CONV_W))

        dw4_ref[...] = jnp.zeros(dw4_ref.shape, F32)
        db4_ref[...] = jnp.zeros(db4_ref.shape, F32)
        dlam_ref[...] = jnp.zeros(dlam_ref.shape, F32)
        dcw_ref[...] = jnp.zeros(dcw_ref.shape, F32)
        dcb_ref[...] = jnp.zeros(dcb_ref.shape, F32)

        def phase_c(ci, _):
            base = pl.multiple_of(ci * CH, CH)
            rows = pl.ds(base, CH)
            _, xl = conv_chunk(ci)
            dxl = jnp.zeros((CH, RB), F32)
            dpre_a, dpre_x, dls = [], [], []
            for d in range(2):
                a = (a0, a1)[d][rows, :]
                r = r_refs[d][rows, :].astype(F32)
                i = i_refs[d][rows, :].astype(F32)
                q = -jnp.tanh(LRU_C * r * ls[d:d + 1, :]) * (1.0 + a * a)
                g = (c0, c1)[d][rows, :]
                hp = hp_refs[d][rows, :].astype(F32)
                gm = g * jnp.sqrt(q)
                di = gm * xl
                dxl = dxl + gm * i
                dla = a * (g * hp - a * (g * (i * xl)) * lax.rsqrt(q))
                dr = dla * (LRU_C * ls[d:d + 1, :])
                dls.append(_colsum(dla * (LRU_C * r)))
                dpre_a.append(dr * r * (1.0 - r))
                dpre_x.append(di * i * (1.0 - i))
            dpre = jnp.concatenate(dpre_a + dpre_x, axis=1)
            dpre_b = dpre.astype(BF16)
            dxl = dxl + _dot(dpre_b, w4v, NT)
            dw4_ref[...] += _dot(xl.astype(BF16), dpre_b, TN)
            db4_ref[...] += _colsum(dpre)
            dlam_ref[...] += jnp.concatenate(dls, axis=0)
            dcb_ref[...] += _colsum(dxl)
            dxpad[_pad_rows(ci), :] = dxl
            return 0

        lax.fori_loop(0, T // CH, phase_c, 0)
        dlam_ref[...] = dlam_ref[...] * _sigmoid(-lam_v)

        def phase_d(ci, _):
            base = pl.multiple_of(ci * CH, CH)
            rows = pl.ds(base, CH)
            xtaps, _ = conv_chunk(ci)
            dtaps = _conv_taps(dxpad[pl.ds(_pad_start(ci), CH + 2 * HALO), :], transpose=True)
            dxl = dxpad[_pad_rows(ci), :]
            dxr_ref[rows, :] = sum(dtaps[k] * cwv[k:k + 1, :] for k in range(CONV_W)).astype(BF16)
            dcw_ref[...] += jnp.concatenate([_colsum(dxl * xtaps[k]) for k in range(CONV_W)], axis=0)
            return 0

        lax.fori_loop(0, T // CH, phase_d, 0)

    sp = _rnn_specs(T)
    dp_spec = pl.BlockSpec((T, RB), lambda n, j: (0, COL_GR // RB + n - j * (COL_GR - COL_XR) // RB))
    ci, ca, co, cs, cscr = _carry_args(carry)
    n_in = 3 + len(saved) + 5 + 1
    res = pl.pallas_call(
        _carried(kern, carry, n_in, 6, *_grid_ends((N_RNN_BLOCKS, 2))), name=name, grid=(N_RNN_BLOCKS, 2),
        in_specs=[sp["xr"], sp["gr"]] + [sp["act"]] * (1 + len(saved)) + [sp["cw"], sp["cb"], sp["w4"], sp["b4"],
                                                                           sp["lam"], ANY] + ci,
        out_specs=[dp_spec, sp["cw"], sp["cb"], sp["w4"], sp["b4"], sp["lam"]] + co,
        out_shape=[_sds((T, DP_W), BF16), _sds((CONV_W, D), F32), _sds((1, D), F32),
                   _sds((N_RNN_BLOCKS, RB, 4 * RB), F32), _sds((N_RNN_BLOCKS, 1, 4 * RB), F32), _sds((2, D), F32)] + cs,
        scratch_shapes=[pltpu.VMEM((T + PAD_ROWS, RB), F32)] * 2 + [pltpu.VMEM((T, RB), F32)] * 3 + cscr,
        input_output_aliases={n_in - 1: 0},
        compiler_params=_params(("arbitrary", "arbitrary")),
    )(p, p, du, *saved, cw, cb, w4, b4, lam, dp, *ca)
    return res if carry is None else (res[:6], res[6:])


class _Plan:
    def __init__(self, shards, Ws):
        L = len(Ws)
        self.shards, self.Ws = shards, Ws
        self.Gs = [None] * L
        self.slots = [dict() for _ in range(L)]
        self.gate_slots = [None] * L
        self.table = {}
        for l in range(L):
            t = f"l{l}_"
            self.table[t + "rnn_fwd"] = [("gather", l, k) for k in ("wffn_in_t", "wo_rnn", "wo_attn", "wout")]
            if l + 1 < L:
                self.table[t + "attn_lat_fwd"] = [("gather", l + 1, "win_t")]
                self.table[t + "ffn_in"] = [("gather", l, "wffn_out")]
            else:
                self.table[t + "attn_lat_fwd"] = [("gather", l, "wffn_out")]
            self.table[t + "ffn_in_dx"] = [("scatter", l, "wffn_out")]
            self.table[t + "attn_lat_bwd"] = [("scatter", l, "wffn_in_t")]
            self.table[t + "proj_dx"] = [("scatter", l, "win_t_a")]
            self.table[t + "rnn_bwd"] = ([("scatter", l, k) for k in ("wout", "wo_attn", "wo_rnn")]
                                         + ([("scatter", l + 1, "win_t_b"), ("gates", l + 1, "w4")] if l + 1 < L else []))
        self.table["l0_proj_dw_b"] = [("gates", 0, "w4")]

    def carry(self, name):
        jobs = []
        for kind, l, k in self.table.get(name, []):
            if kind == "gather":
                jobs.append(("gather", self.shards[l][k]))
            elif kind == "scatter":
                jobs.append(("scatter", self.Gs[l][k].reshape(N_DEV, -1, self.Gs[l][k].shape[-1])))
            else:
                jobs.append(("gather", self.Gs[l]["w4"].reshape(N_RNN_BLOCKS * RB, 4 * RB).astype(BF16)))
        return _Carry(jobs) if jobs else None

    def done(self, name, got):
        for (kind, l, k), res in zip(self.table[name], got):
            if kind == "gather":
                self.Ws[l][k] = res.reshape(-1, D)
            elif kind == "scatter":
                self.slots[l][k] = res
            else:
                self.gate_slots[l] = res


def _run(X, fn, name, *args, **kw):
    carry = None if X is None else X.carry(name)
    if carry is None:
        return fn(name, *args, **kw)
    out, got = fn(name, *args, carry=carry, **kw)
    X.done(name, got)
    return out


def _layer_fwd(l, xa, h, W, rope, S, nxt, X=None):
    T = xa.shape[0]
    tag = f"l{l}_"
    cos, sin, bias = rope
    p = _run(X, _mm_act, tag + "proj", h, W["win_t"], "NT", BF16)
    u, *rnn_saved = _run(X, _rnn_fwd, tag + "rnn_fwd", p, W["cw"], W["cb"], W["w4"], W["b4"], W["lam"], T)
    qa, kp, vp, kc, vc = _qkv_prep(tag + "qkv_prep", p, cos, sin, S)
    o_all = _attn_fwd(tag + "attn_ctx_fwd", qa, kc, vc, W["sink4"], S)
    o_all = _run(X, _attn_fwd, tag + "attn_lat_fwd", qa, kc, vc, W["sink4"], S, band=(kp, vp, bias), prev=o_all)
    ya, yb, z, m, x1, h2 = _out_fused(tag + "out", p, u, o_all, xa, W["wo_rnn"], W["wo_attn"], W["wout"],
                                      W["g_mix_post"], W["mod"], W["g_ffn_pre"])
    fg, fu, s = _run(X, _ffn_in_fused, tag + "ffn_in", h2, W["wffn_in_t"])
    e, *out = _ffn_out_fused(tag + "ffn_out", s, W["wffn_out"], x1, W["g_ffn_post"], W["mod"], nxt)
    saved = dict(xa=xa, h=h, p=p, u=u, rnn=rnn_saved, qa=qa, kp=kp, vp=vp, kc=kc, vc=vc, o_all=o_all,
                 ya=ya, yb=yb, z=z, m=m, x1=x1, h2=h2, fg=fg, fu=fu, s=s, e=e)
    return saved, out


def _layer_bwd(l, dx2, A, W, rope, S, X=None, loss_of=None):
    T = A["xa"].shape[0]
    tag = f"l{l}_"
    cos, sin, bias = rope
    G = {}
    if X is not None:
        X.Gs[l] = G
    if loss_of is None:
        de, df, dga2, G["g_ffn_post"] = _ffn_bwd_fused(tag + "ffn_bwd", A["fg"], A["fu"], W["wffn_out"],
                                                       head=(dx2, A["e"], W["g_ffn_post"], W["mod"]))
    else:
        dx2, de, dga2, G["g_ffn_post"], G["sq"] = _loss_resid_bwd(tag + "loss_ffn_resid_bwd", *loss_of, A["e"],
                                                                  W["g_ffn_post"], W["mod"], GA2)
        df, = _ffn_bwd_fused(tag + "ffn_bwd", A["fg"], A["fu"], W["wffn_out"], de=de)
    G["wffn_out"] = _mm_wgrad(tag + "ffn_out_dw", A["s"], de)
    dx1, dm, dsh2, dsc2, G["g_ffn_pre"], dga1, G["g_mix_post"] = _run(
        X, _ffn_in_bwd_fused, tag + "ffn_in_dx", df, W["wffn_in_t"], A["x1"], dx2, A["m"], W["g_ffn_pre"], W["mod"],
        W["g_mix_post"])
    G["wffn_in_t"] = _run(X, _mm_wgrad, tag + "ffn_in_dw", df, A["h2"])
    G["wout"] = _mm_wgrad(tag + "out_dw", A["z"], dm)
    dya, dyb, dgl, du, do = _out_bwd_fused(tag + "out_dx", dm, W["wout"], W["wo_rnn"], W["wo_attn"], A["p"], A["ya"],
                                           A["yb"])
    G["wo_attn"] = _mm_wgrad(tag + "o_attn_dw", A["o_all"], dyb)
    G["wo_rnn"] = _mm_wgrad(tag + "o_rnn_dw", A["u"], dya)
    dp, dkc_c, dvc_c, dsink_c = _attn_bwd(tag + "attn_ctx_bwd", A["qa"], A["kc"], A["vc"], W["sink4"], A["o_all"], do, S)
    dp, dkc_l, dvc_l, dsink_l, dkp, dvp = _run(
        X, _attn_bwd, tag + "attn_lat_bwd", A["qa"], A["kc"], A["vc"], W["sink4"], A["o_all"], do, S,
        band=(A["kp"], A["vp"], bias, cos, sin), prev_dq=dp)
    G["sink4"] = dsink_c + dsink_l
    dp = _dkv_assemble(tag + "dkv", dp, dkp, dvp, dkc_l, dvc_l, dkc_c, dvc_c, cos, sin, S)
    dp, G["cw"], G["cb"], G["w4"], G["b4"], G["lam"] = _run(
        X, _rnn_bwd, tag + "rnn_bwd", A["p"], du, A["rnn"], dp, W["cw"], W["cb"], W["w4"], W["b4"], W["lam"], T)
    proj_dx = (_proj_bwd_fused, tag + "proj_dx", dp, dgl, W["win_t"], A["xa"], dx1, W["g_mix_pre"], W["mod"])
    if X is not None:
        G["win_t_a"] = _proj_wgrad(tag + "proj_dw_a", dp, dgl, A["h"][:, :D // 2])
        dxa, dsh1, dsc1, G["g_mix_pre"] = _run(X, *proj_dx)
        G["win_t_b"] = _run(X, _proj_wgrad, tag + "proj_dw_b", dp, dgl, A["h"][:, D // 2:])
    else:
        dxa, dsh1, dsc1, G["g_mix_pre"] = _run(X, *proj_dx)
        G["win_t"] = _proj_wgrad(tag + "proj_dw", dp, dgl, A["h"])
    G["mod"] = jnp.concatenate([dsh1, dsc1, dga1, dsh2, dsc2, dga2], axis=1)
    return dxa, G


def _local_step(xa, target, Ws, S, X=None):
    rope = (*_rope_tables(S), _band_bias(S))
    L = len(Ws)
    h = _normmod_fwd("l0_mix_norm", xa, Ws[0]["g_mix_pre"], Ws[0]["mod"], SH1, SC1)
    saved = []
    x = xa
    for l in range(L):
        nxt = (Ws[l + 1]["g_mix_pre"], Ws[l + 1]["mod"]) if l + 1 < L else None
        A, out = _layer_fwd(l, x, h, Ws[l], rope, S, nxt, X)
        saved.append(A)
        if l + 1 < L:
            x, h = out
    Gs = [None] * L
    dx = None
    for l in reversed(range(L)):
        dx, Gs[l] = _layer_bwd(l, dx, saved[l], Ws[l], rope, S, X, loss_of=(out[0], target) if l == L - 1 else None)
    return Gs[L - 1]["sq"], dx, Gs


MESH = pl.DeviceIdType.MESH


def _place():
    return lax.axis_index("x"), lax.axis_index("y"), lax.axis_index("c")


def _lin(px, py, pc):
    return 4 * px + 2 * py + pc


def _allgather_small(name, blk):
    m, n = blk.shape

    def body(x_ref, out_ref, send_sems, recv_sems, local_sem):
        x, y, c = _place()
        me, sibling = (x, y, c), (x, y, 1 - c)
        chips = [(1 - x, y), (x, 1 - y), (1 - x, 1 - y)]

        def copy(k, block, to, src=None):
            dst = out_ref.at[_lin(*block)]
            return pltpu.make_async_remote_copy(src_ref=dst if src is None else src, dst_ref=dst,
                                                send_sem=send_sems.at[k], recv_sem=recv_sems.at[k],
                                                device_id=to, device_id_type=MESH)

        mine = pltpu.make_async_copy(x_ref, out_ref.at[_lin(*me)], local_sem)
        mine.start()
        first = [copy(0, me, sibling, src=x_ref)]
        first += [copy(1 + j, me, (*chip, c), src=x_ref) for j, chip in enumerate(chips)]
        for cp in first:
            cp.start()
        passed = [copy(4 + j, (*chip, c), sibling) for j, chip in enumerate(chips)]
        for j, chip in enumerate(chips):
            copy(1 + j, (*chip, c), me).wait_recv()
            passed[j].start()
        copy(0, sibling, me).wait_recv()
        for j, chip in enumerate(chips):
            copy(4 + j, (*chip, 1 - c), me).wait_recv()
        for cp in first + passed:
            cp.wait_send()
        mine.wait()

    return pl.pallas_call(
        body, name=name, out_shape=_sds((N_DEV, m, n), blk.dtype),
        in_specs=[pl.BlockSpec(memory_space=pltpu.VMEM)], out_specs=pl.BlockSpec(memory_space=pltpu.VMEM),
        scratch_shapes=[pltpu.SemaphoreType.DMA((7,)), pltpu.SemaphoreType.DMA((7,)), pltpu.SemaphoreType.DMA],
        compiler_params=pltpu.CompilerParams(vmem_limit_bytes=VMEM_LIMIT),
    )(blk)


def _allgather_hbm(name, shards):
    na = len(shards)

    def body(*refs):
        ins, outs = refs[:na], refs[na:2 * na]
        send_sems, recv_sems, local_sems = refs[2 * na:]
        x, y, c = _place()
        me, sibling = (x, y, c), (x, y, 1 - c)
        chips = [(1 - x, y), (x, 1 - y), (1 - x, 1 - y)]

        def copy(a, k, block, to, from_input=False):
            dst = outs[a].at[_lin(*block)]
            return pltpu.make_async_remote_copy(src_ref=ins[a] if from_input else dst, dst_ref=dst,
                                                send_sem=send_sems.at[a, k], recv_sem=recv_sems.at[a, k],
                                                device_id=to, device_id_type=MESH)

        mine = [pltpu.make_async_copy(ins[a], outs[a].at[_lin(*me)], local_sems.at[a]) for a in range(na)]
        for cp in mine:
            cp.start()
        first = []
        for a in range(na):
            first.append(copy(a, 0, me, sibling, True))
            first += [copy(a, 1 + j, me, (*chip, c), True) for j, chip in enumerate(chips)]
        for cp in first:
            cp.start()
        passed = []
        for j, chip in enumerate(chips):
            for a in range(na):
                copy(a, 1 + j, (*chip, c), me).wait_recv()
                fwd = copy(a, 4 + j, (*chip, c), sibling)
                fwd.start()
                passed.append(fwd)
        for a in range(na):
            copy(a, 0, sibling, me).wait_recv()
            for j, chip in enumerate(chips):
                copy(a, 4 + j, (*chip, 1 - c), me).wait_recv()
        for cp in first + passed:
            cp.wait_send()
        for cp in mine:
            cp.wait()

    return pl.pallas_call(
        body, name=name, out_shape=[_sds((N_DEV, *s.shape), s.dtype) for s in shards],
        in_specs=[ANY] * na, out_specs=[ANY] * na,
        scratch_shapes=[pltpu.SemaphoreType.DMA((na, 7)), pltpu.SemaphoreType.DMA((na, 7)),
                        pltpu.SemaphoreType.DMA((na,))],
    )(*shards)


def _exchange_shards(name, grads, L):
    nw = len(grads)
    na = nw * L
    flat = [g for per_layer in grads for g in per_layer]

    def body(*refs):
        ins, outs = refs[:na], refs[na:na + nw]
        send_sems, recv_sems, local_sems = refs[na + nw:]
        x, y, c = _place()
        me = _lin(x, y, c)
        peers = [(x ^ ((k + 1) >> 2 & 1), y ^ ((k + 1) >> 1 & 1), c ^ ((k + 1) & 1)) for k in range(7)]

        def copy(a, k, src_blk, dst_blk):
            return pltpu.make_async_remote_copy(src_ref=ins[a].at[src_blk], dst_ref=outs[a // L].at[a % L, dst_blk],
                                                send_sem=send_sems.at[a, k], recv_sem=recv_sems.at[a, k],
                                                device_id=peers[k], device_id_type=MESH)

        mine = [pltpu.make_async_copy(ins[a].at[me], outs[a // L].at[a % L, me], local_sems.at[a]) for a in range(na)]
        for cp in mine:
            cp.start()
        sent = [copy(a, k, _lin(*peers[k]), me) for a in range(na) for k in range(7)]
        for cp in sent:
            cp.start()
        for a in range(na):
            for k in range(7):
                copy(a, k, me, _lin(*peers[k])).wait_recv()
        for cp in sent:
            cp.wait_send()
        for cp in mine:
            cp.wait()

    return pl.pallas_call(
        body, name=name, out_shape=[_sds((L, *per_layer[0].shape), per_layer[0].dtype) for per_layer in grads],
        in_specs=[ANY] * na, out_specs=[ANY] * nw,
        scratch_shapes=[pltpu.SemaphoreType.DMA((na, 7)), pltpu.SemaphoreType.DMA((na, 7)),
                        pltpu.SemaphoreType.DMA((na,))],
    )(*flat)


MOD_ROWS = 16
MOD_SHARD = 6 * D // N_DEV
HI = lax.Precision.HIGHEST


def _mod_fwd(name, c9, w_mod, b_shard):
    L = w_mod.shape[0]

    def kern(c_ref, w_ref, b_ref, o_ref):
        o_ref[...] = lax.dot_general(_silu(c_ref[...]), w_ref[...], NN, precision=HI,
                                     preferred_element_type=F32) + b_ref[...]

    return pl.pallas_call(
        kern, name=name, grid=(L,),
        in_specs=[_full_spec(c9.shape), pl.BlockSpec((None, D, MOD_SHARD), lambda l: (l, 0, 0)),
                  pl.BlockSpec((None, 1, MOD_SHARD), lambda l: (l, 0, 0))],
        out_specs=pl.BlockSpec((None, MOD_ROWS, MOD_SHARD), lambda l: (l, 0, 0)),
        out_shape=_sds((L, MOD_ROWS, MOD_SHARD), F32), compiler_params=_params(),
    )(c9, w_mod, b_shard)


def _mod_bwd(name, c9, w_mod, dmod_all, dmod_cols):
    L = w_mod.shape[0]

    def rows9(ref, l):
        own = jnp.concatenate([ref[j, 2 * l + 1:2 * l + 2, :] for j in range(N_DEV)], axis=0)
        ctx = ref[0, 2 * l:2 * l + 1, :]
        for j in range(1, N_DEV):
            ctx = ctx + ref[j, 2 * l:2 * l + 1, :]
        return own, ctx

    def kern(c_ref, w_ref, all_ref, cols_ref, gw_ref, gb_ref, gc_ref):
        l = pl.program_id(0)
        for ll in range(L):
            @pl.when(l == ll)
            def _():
                own, ctx = rows9(all_ref, ll)
                gb_ref[...] = _colsum(own) + ctx
                own_s, ctx_s = rows9(cols_ref, ll)
                r16 = jnp.concatenate([own_s, ctx_s, jnp.zeros((MOD_ROWS - N_DEV - 1, MOD_SHARD), F32)], axis=0)
                gw_ref[...] = lax.dot_general(_silu(c_ref[...]), r16, TN, precision=HI, preferred_element_type=F32)
                part = lax.dot_general(r16, w_ref[...], NT, precision=HI,
                                       preferred_element_type=F32)[N_DEV:N_DEV + 1, :]
                if ll == 0:
                    gc_ref[...] = part
                else:
                    gc_ref[...] += part

    return pl.pallas_call(
        kern, name=name, grid=(L,),
        in_specs=[_full_spec(c9.shape), pl.BlockSpec((None, D, MOD_SHARD), lambda l: (l, 0, 0)),
                  _full_spec(dmod_all.shape), _full_spec(dmod_cols.shape)],
        out_specs=[pl.BlockSpec((None, D, MOD_SHARD), lambda l: (l, 0, 0)),
                   pl.BlockSpec((None, 1, 6 * D), lambda l: (l, 0, 0)), _full_spec((1, D))],
        out_shape=[_sds((L, D, MOD_SHARD), F32), _sds((L, 1, 6 * D), F32), _sds((1, D), F32)],
        compiler_params=_params(),
    )(c9, w_mod, dmod_all, dmod_cols)


_BC1 = 1.0 - ADAM_B1 ** ADAM_STEP
_BC2 = 1.0 - ADAM_B2 ** ADAM_STEP


def _adamw_vals(w, g, m, v):
    m = ADAM_B1 * m + (1.0 - ADAM_B1) * g
    v = ADAM_B2 * v + (1.0 - ADAM_B2) * (g * g)
    delta = -ADAM_LR * ((m / _BC1) / (jnp.sqrt(v / _BC2) + ADAM_EPS) + ADAM_WD * w)
    return delta, m, v


def _adamw(name, w, g, m, v, tile):
    R, C = w.shape
    blk = ((tile, C), lambda i: (i, 0))

    def body(i, ins, ps, outs, acc):
        d, mm, vv = _adamw_vals(ins[0][...], ins[1][...], ins[2][...], ins[3][...])
        outs[0][...] = d
        outs[1][...] = mm
        outs[2][...] = vv

    return _ew(name, body, R // tile, [(a, *blk) for a in (w, g, m, v)], [], [(_sds((R, C), F32), *blk)] * 3)


def _sum_slots(ref):
    g = ref[0].astype(F32)
    for j in range(1, N_DEV):
        g = g + ref[j].astype(F32)
    return g


def _adamw_slots(name, slots, shape, tile, wmv=None):
    L, R, C = shape
    n = R // tile
    spec = pl.BlockSpec((None, tile, C), lambda l, i: (l, i, 0))
    pieces = [s if isinstance(s, (list, tuple)) else [s] for s in slots]
    layer_of = [ll for ll, ps in enumerate(pieces) for _ in ps]
    flat = [p for ps in pieces for p in ps]
    wmv = list(wmv or [])

    def slot_spec(ll, cols):
        return pl.BlockSpec((N_DEV, tile, cols),
                            lambda l, i: (0, jnp.where(l == ll, i, jnp.where(l < ll, 0, n - 1)), 0))

    def kern(*refs):
        s_refs = refs[:len(flat)]
        rest = refs[len(flat):]
        l = pl.program_id(0)
        for ll in range(L):
            @pl.when(l == ll)
            def _():
                parts = [_sum_slots(r) for r, lr in zip(s_refs, layer_of) if lr == ll]
                g = parts[0] if len(parts) == 1 else jnp.concatenate(parts, axis=1)
                if wmv:
                    w_ref, m_ref, v_ref, g_ref, d_ref, mo_ref, vo_ref = rest
                    d_ref[...], mo_ref[...], vo_ref[...] = _adamw_vals(w_ref[...], g, m_ref[...], v_ref[...])
                else:
                    g_ref, = rest
                g_ref[...] = g

    n_out = 4 if wmv else 1
    return pl.pallas_call(
        kern, name=name, grid=(L, n),
        in_specs=[slot_spec(ll, p.shape[-1]) for ll, p in zip(layer_of, flat)] + [spec] * len(wmv),
        out_specs=[spec] * n_out, out_shape=[_sds((L, R, C), F32)] * n_out,
        compiler_params=_params(("arbitrary", "arbitrary")),
    )(*flat, *wmv)


def _sum_blocks(name, blocks):
    _, R, C = blocks.shape

    def kern(b_ref, o_ref):
        o_ref[...] = _sum_slots(b_ref)

    return pl.pallas_call(kern, name=name, in_specs=[_full_spec(blocks.shape)], out_specs=_full_spec((R, C)),
                          grid=(1,), out_shape=_sds((R, C), F32), compiler_params=_params())(blocks)


BIG = ("win_t", "wo_rnn", "wo_attn", "wout", "wffn_in_t", "wffn_out")
BIG_SRC = ("w_in", "w_o_rnn", "w_o_attn", "w_out", "w_ffn_in", "w_ffn_out")
BIG_T = (True, False, False, False, True, False)
BIG_TILE = (176, 128, 128, 128, 176, 176)


def _chan_full(g8):
    return jnp.transpose(g8, (1, 0, 2)).reshape(g8.shape[1], D)


def kernel(x, c, ctx, c_ctx, w_mod, b_mod, g_mix_pre, g_mix_post, g_ffn_pre, g_ffn_post, w_in, conv_w, conv_b, lru_wa, lru_ba, lru_wx, lru_bx, lru_lam, attn_sink, w_o_rnn, w_o_attn, w_out, w_ffn_in, w_ffn_out, loss_target, m_c_ctx, m_w_mod, m_b_mod, m_g_mix_pre, m_g_mix_post, m_g_ffn_pre, m_g_ffn_post, m_w_in, m_conv_w, m_conv_b, m_lru_wa, m_lru_ba, m_lru_wx, m_lru_bx, m_lru_lam, m_attn_sink, m_w_o_rnn, m_w_o_attn, m_w_out, m_w_ffn_in, m_w_ffn_out, v_c_ctx, v_w_mod, v_b_mod, v_g_mix_pre, v_g_mix_post, v_g_ffn_pre, v_g_ffn_post, v_w_in, v_conv_w, v_conv_b, v_lru_wa, v_lru_ba, v_lru_wx, v_lru_bx, v_lru_lam, v_attn_sink, v_w_o_rnn, v_w_o_attn, v_w_out, v_w_ffn_in, v_w_ffn_out):
    P = dict(c_ctx=c_ctx, w_mod=w_mod, b_mod=b_mod, g_mix_pre=g_mix_pre, g_mix_post=g_mix_post, g_ffn_pre=g_ffn_pre,
             g_ffn_post=g_ffn_post, w_in=w_in, conv_w=conv_w, conv_b=conv_b, lru_wa=lru_wa, lru_ba=lru_ba,
             lru_wx=lru_wx, lru_bx=lru_bx, lru_lam=lru_lam, attn_sink=attn_sink, w_o_rnn=w_o_rnn, w_o_attn=w_o_attn,
             w_out=w_out, w_ffn_in=w_ffn_in, w_ffn_out=w_ffn_out)
    Mo = dict(c_ctx=m_c_ctx, w_mod=m_w_mod, b_mod=m_b_mod, g_mix_pre=m_g_mix_pre, g_mix_post=m_g_mix_post,
              g_ffn_pre=m_g_ffn_pre, g_ffn_post=m_g_ffn_post, w_in=m_w_in, conv_w=m_conv_w, conv_b=m_conv_b,
              lru_wa=m_lru_wa, lru_ba=m_lru_ba, lru_wx=m_lru_wx, lru_bx=m_lru_bx, lru_lam=m_lru_lam,
              attn_sink=m_attn_sink, w_o_rnn=m_w_o_rnn, w_o_attn=m_w_o_attn, w_out=m_w_out, w_ffn_in=m_w_ffn_in,
              w_ffn_out=m_w_ffn_out)
    Vo = dict(c_ctx=v_c_ctx, w_mod=v_w_mod, b_mod=v_b_mod, g_mix_pre=v_g_mix_pre, g_mix_post=v_g_mix_post,
              g_ffn_pre=v_g_ffn_pre, g_ffn_post=v_g_ffn_post, w_in=v_w_in, conv_w=v_conv_w, conv_b=v_conv_b,
              lru_wa=v_lru_wa, lru_ba=v_lru_ba, lru_wx=v_lru_wx, lru_bx=v_lru_bx, lru_lam=v_lru_lam,
              attn_sink=v_attn_sink, w_o_rnn=v_w_o_rnn, w_o_attn=v_w_o_attn, w_out=v_w_out, w_ffn_in=v_w_ffn_in,
              w_ffn_out=v_w_ffn_out)
    L = w_in.shape[0]
    S = x.shape[1]
    me = _lin(*_place())

    small = jnp.concatenate([c.reshape(8, 128), conv_w.reshape(L * CONV_W, 128), lru_ba.reshape(2 * L, 128),
                             lru_bx.reshape(2 * L, 128), lru_lam.reshape(2 * L, 128), jnp.zeros((4, 128), F32)], axis=0)
    small_all = _allgather_small("ag_small", small)
    c_all = small_all[:, 0:8].reshape(N_DEV, D)
    conv_w_f = _chan_full(small_all[:, 8:16]).reshape(L, CONV_W, D)
    lru_ba_f = _chan_full(small_all[:, 16:20]).reshape(L, 2, D)
    lru_bx_f = _chan_full(small_all[:, 20:24]).reshape(L, 2, D)
    lru_lam_f = _chan_full(small_all[:, 24:28]).reshape(L, 2, D)

    c9 = jnp.concatenate([c_all, c_ctx[None], jnp.zeros((MOD_ROWS - N_DEV - 1, D), F32)], axis=0)
    b_shard = lax.dynamic_slice_in_dim(b_mod, me * MOD_SHARD, MOD_SHARD, axis=1)[:, None, :]
    mod_part = _mod_fwd("mod_fwd", c9, w_mod, b_shard)
    mod_all = _allgather_small("ag_mod", mod_part.reshape(L * MOD_ROWS, MOD_SHARD))
    mod_all = jnp.transpose(mod_all.reshape(N_DEV, L, MOD_ROWS, MOD_SHARD), (1, 2, 0, 3)).reshape(L, MOD_ROWS, 6 * D)
    own_row = lax.dynamic_index_in_dim(mod_all, me, axis=1, keepdims=False)
    modrows = jnp.stack([mod_all[:, N_DEV], own_row], axis=1)

    shards = [{k: (P[src][l].T if tr else P[src][l]).astype(BF16) for k, src, tr in zip(BIG, BIG_SRC, BIG_T)}
              for l in range(L)]
    win0, = _allgather_hbm("ag_w_in0", [shards[0]["win_t"]])
    Ws = []
    for l in range(L):
        W = {"win_t": win0.reshape(-1, D)} if l == 0 else {}
        W.update(
            cw=conv_w_f[l], cb=conv_b[l][None],
            w4=jnp.concatenate([lru_wa[l, 0], lru_wa[l, 1], lru_wx[l, 0], lru_wx[l, 1]], axis=-1).astype(BF16),
            b4=jnp.concatenate([lru_ba_f[l, 0].reshape(N_RNN_BLOCKS, 1, RB), lru_ba_f[l, 1].reshape(N_RNN_BLOCKS, 1, RB),
                                lru_bx_f[l, 0].reshape(N_RNN_BLOCKS, 1, RB), lru_bx_f[l, 1].reshape(N_RNN_BLOCKS, 1, RB)],
                               axis=-1),
            lam=lru_lam_f[l], sink4=jnp.broadcast_to(attn_sink[l].reshape(N_KV, Q_PER_KV, 1), (N_KV, Q_PER_KV, HEAD)),
            g_mix_pre=g_mix_pre[l][None], g_mix_post=g_mix_post[l][None], g_ffn_pre=g_ffn_pre[l][None],
            g_ffn_post=g_ffn_post[l][None], mod=modrows[l])
        Ws.append(W)

    xa = jnp.concatenate([ctx[0], x[0]], axis=0)
    plan = _Plan(shards, Ws)
    sq, dxa, Gs = _local_step(xa, loss_target[0], Ws, S, plan)
    loss_part = ((0.5 / D) * jnp.sum(sq)).reshape(1, 1)
    grad_x = dxa[CTX:][None]

    dmod = jnp.concatenate([Gs[l]["mod"] for l in range(L)] + [jnp.zeros((8 - 2 * L, 6 * D), F32)], axis=0)
    dmod_all = _allgather_small("ag_dmod", dmod)
    dmod_cols = lax.dynamic_slice_in_dim(dmod_all, me * MOD_SHARD, MOD_SHARD, axis=2)
    g_w_mod, g_b_mod, dsc_part = _mod_bwd("mod_bwd", c9, w_mod, dmod_all, dmod_cols)
    g_b_mod = g_b_mod[:, 0]

    def rows(name, shape):
        return jnp.concatenate([Gs[l][name].reshape(shape) for l in range(L)], axis=0)

    b4g = [Gs[l]["b4"].reshape(N_RNN_BLOCKS, 4, RB) for l in range(L)]
    sink_row = jnp.concatenate([Gs[l]["sink4"][:, :, 0].reshape(1, N_Q) for l in range(L)]
                               + [loss_part, jnp.zeros((1, D - L * N_Q - 1), F32)], axis=1)
    small_g = jnp.concatenate(
        [rows("g_mix_pre", (1, D)), rows("g_mix_post", (1, D)), rows("g_ffn_pre", (1, D)), rows("g_ffn_post", (1, D)),
         rows("cb", (1, D)), rows("cw", (CONV_W, D))]
        + [b4g[l][:, d].reshape(1, D) for l in range(L) for d in range(2)]
        + [b4g[l][:, 2 + d].reshape(1, D) for l in range(L) for d in range(2)]
        + [rows("lam", (2, D)), sink_row, dsc_part], axis=0)
    n_small = small_g.shape[0]
    small_tot = _sum_blocks("sum_small", _allgather_small("ag_small_grads", small_g))
    o = 0
    G = {}
    for name in ("g_mix_pre", "g_mix_post", "g_ffn_pre", "g_ffn_post", "conv_b"):
        G[name] = small_tot[o:o + L]
        o += L
    G["conv_w"] = small_tot[o:o + L * CONV_W].reshape(L, CONV_W, D)
    o += L * CONV_W
    for name in ("lru_ba", "lru_bx", "lru_lam"):
        G[name] = small_tot[o:o + 2 * L].reshape(L, 2, D)
        o += 2 * L
    G["attn_sink"] = small_tot[o, :L * N_Q].reshape(L, N_Q)
    loss = small_tot[o, L * N_Q]
    sg = jax.nn.sigmoid(c_ctx)
    G["c_ctx"] = small_tot[o + 1] * (sg * (1.0 + c_ctx * (1.0 - sg)))
    G["b_mod"] = g_b_mod
    G["w_mod"] = g_w_mod

    last_slots, = _exchange_shards("exchange_w_in0", [[Gs[0]["win_t_b"].reshape(N_DEV, -1, D // 2)]], 1)
    plan.slots[0]["win_t_b"] = last_slots[0]
    for l in range(L):
        plan.slots[l]["win_t"] = [plan.slots[l]["win_t_a"], plan.slots[l]["win_t_b"]]

    out_g, out_d, out_m, out_v = {}, {}, {}, {}

    def put(name, res, shape=None):
        g, d, m, v = res
        for dst, val in ((out_g, g), (out_d, d), (out_m, m), (out_v, v)):
            dst[name] = val if shape is None else val.reshape(shape)

    for k, src, tr, tile in zip(BIG, BIG_SRC, BIG_T, BIG_TILE):
        lay = (lambda a: jnp.swapaxes(a, 1, 2)) if tr else (lambda a: a)
        wmv = (lay(P[src]), lay(Mo[src]), lay(Vo[src]))
        res = _adamw_slots("adamw_" + src, [plan.slots[l][k] for l in range(L)], wmv[0].shape, tile, wmv)
        put(src, [lay(r) for r in res])
    res = _adamw("adamw_w_mod", w_mod.reshape(L * D, MOD_SHARD), g_w_mod.reshape(L * D, MOD_SHARD),
                 m_w_mod.reshape(L * D, MOD_SHARD), v_w_mod.reshape(L * D, MOD_SHARD), 256)
    put("w_mod", (g_w_mod,) + tuple(res), w_mod.shape)
    def fuse4(wa, wx):
        return jnp.concatenate([wa[:, 0], wa[:, 1], wx[:, 0], wx[:, 1]], axis=-1).reshape(L, N_RNN_BLOCKS * RB, 4 * RB)

    res = _adamw_slots("adamw_gates", plan.gate_slots, (L, N_RNN_BLOCKS * RB, 4 * RB), 256,
                       (fuse4(lru_wa, lru_wx), fuse4(m_lru_wa, m_lru_wx), fuse4(v_lru_wa, v_lru_wx)))
    res = [r.reshape(L, N_RNN_BLOCKS, RB, 4, RB) for r in res]
    put("lru_wa", [jnp.stack([r[:, :, :, 0], r[:, :, :, 1]], axis=1) for r in res])
    put("lru_wx", [jnp.stack([r[:, :, :, 2], r[:, :, :, 3]], axis=1) for r in res])
    rep = ("g_mix_pre", "g_mix_post", "g_ffn_pre", "g_ffn_post", "conv_b", "b_mod")

    def pack_rep(T_):
        sink = jnp.concatenate([T_["attn_sink"].reshape(1, L * N_Q), jnp.zeros((1, D - L * N_Q), F32)], axis=1)
        return jnp.concatenate([T_[n].reshape(-1, D) for n in rep] + [sink, T_["c_ctx"][None]], axis=0)

    pk = [pack_rep(T_) for T_ in (P, G, Mo, Vo)]
    n_rep = pk[0].shape[0]
    res = _adamw("adamw_replicated", *[jnp.pad(a, ((0, 24 - n_rep), (0, 0))) for a in pk], 24)
    res = (pk[1],) + tuple(r[:n_rep] for r in res)
    o = 0
    for n in rep:
        k = P[n].size // D
        put(n, [r[o:o + k] for r in res], P[n].shape)
        o += k
    put("attn_sink", [r[o, :L * N_Q] for r in res], attn_sink.shape)
    put("c_ctx", [r[o + 1] for r in res], c_ctx.shape)
    chan = ("conv_w", "lru_ba", "lru_bx", "lru_lam")
    g_own = {n: lax.dynamic_slice_in_dim(G[n], me * RB, RB, axis=2) for n in chan}

    def pack_chan(T_):
        return jnp.concatenate([T_[n].reshape(-1, RB) for n in chan], axis=0)

    pk = [pack_chan(T_) for T_ in (P, g_own, Mo, Vo)]
    n_ch = pk[0].shape[0]
    res = _adamw("adamw_channels", *[jnp.pad(a, ((0, 24 - n_ch), (0, 0))) for a in pk], 24)
    res = (pk[1],) + tuple(r[:n_ch] for r in res)
    o = 0
    for n in chan:
        k = P[n].size // RB
        put(n, [r[o:o + k] for r in res], P[n].shape)
        o += k

    order = ("c_ctx", "w_mod", "b_mod", "g_mix_pre", "g_mix_post", "g_ffn_pre", "g_ffn_post", "w_in", "conv_w", "conv_b",
             "lru_wa", "lru_ba", "lru_wx", "lru_bx", "lru_lam", "attn_sink", "w_o_rnn", "w_o_attn", "w_out", "w_ffn_in",
             "w_ffn_out")
    return (loss, grad_x, *[out_g[n] for n in order], *[out_d[n] for n in order], *[out_m[n] for n in order],
            *[out_v[n] for n in order])
```

```python
import functools
import math

import numpy as np
import jax
import jax.numpy as jnp
from jax import lax
from jax.experimental import pallas as pl
from jax.experimental.pallas import tpu as pltpu

F32 = jnp.float32
BF16 = jnp.bfloat16

D = 1024
CTX = 256
TR = 256
HEAD = 128
N_Q = 8
N_KV = 2
Q_PER_KV = N_Q // N_KV
GRID_W = 64
N_FREQ = HEAD // 4
ROPE_BASE = 10000.0
N_RNN_BLOCKS = 8
CONV_W = 4
CONV_LEFT = 2
LRU_C = 8.0
D_FF = 2816
IN_W = 5632
P_W = IN_W
DP_W = 3584
COL_XR, COL_GR, COL_Q, COL_K, COL_V, COL_GL = 0, 1024, 2048, 3072, 3328, 3584
GLB = 512
EPS = 1e-6
NEG_INF = -1e30
ATT_SCALE = HEAD ** -0.5
N_DEV = 8
VMEM_LIMIT = 56 * 1024 * 1024

ADAM_LR, ADAM_B1, ADAM_B2, ADAM_EPS, ADAM_WD, ADAM_STEP = 0.001, 0.9, 0.999, 1e-08, 0.01, 10

NN = (((1,), (0,)), ((), ()))
NT = (((1,), (1,)), ((), ()))
TN = (((0,), (0,)), ((), ()))


def _dot(a, b, dims=NN):
    return lax.dot_general(a, b, dims, preferred_element_type=F32)


def _params(sem=("arbitrary",)):
    return pltpu.CompilerParams(dimension_semantics=sem, vmem_limit_bytes=VMEM_LIMIT)


def _full_spec(shape):
    nd = len(shape)
    return pl.BlockSpec(shape, lambda *_: (0,) * nd)


ANY = pl.BlockSpec(memory_space=pl.ANY)


def _ew(name, body, n, row_ins, pars, row_outs, accs=(), alias=None):
    n_ri, n_p, n_ro, n_acc = len(row_ins), len(pars), len(row_outs), len(accs)

    def kern(*refs):
        i = pl.program_id(0)
        ins = refs[:n_ri]
        ps = refs[n_ri:n_ri + n_p]
        outs = refs[n_ri + n_p:n_ri + n_p + n_ro]
        acc = refs[n_ri + n_p + n_ro:]
        if n_acc:
            @pl.when(i == 0)
            def _():
                for a in acc:
                    a[...] = jnp.zeros(a.shape, a.dtype)
        body(i, ins, ps, outs, acc)

    in_specs = [ANY if blk is None else pl.BlockSpec(blk, imap) for (_, blk, imap) in row_ins]
    in_specs += [_full_spec(p.shape) for p in pars]
    out_specs = [pl.BlockSpec(blk, imap) for (_, blk, imap) in row_outs] + [_full_spec(a.shape) for a in accs]
    out_shape = [s for (s, _, _) in row_outs] + list(accs)
    return pl.pallas_call(
        kern, name=name, grid=(n,), in_specs=in_specs, out_specs=out_specs, out_shape=out_shape,
        input_output_aliases=alias or {}, compiler_params=_params(),
    )(*[a for (a, _, _) in row_ins], *pars)


def _rowblk(width, colblk=0, roff=0, tile=TR):
    return (tile, width), (lambda i: (i + roff, colblk))


def _sds(shape, dtype):
    return jax.ShapeDtypeStruct(shape, dtype)


class _Carry:
    SAME_CORE = (1, 3, 5)

    def __init__(self, jobs):
        self.jobs = list(jobs)
        self.arrays = [a for _, a in self.jobs]
        self.out_shapes = [_sds(a.shape if kind == "scatter" else (N_DEV, *a.shape), a.dtype) for kind, a in self.jobs]
        n = len(self.jobs)
        self.scratch = [pltpu.SemaphoreType.DMA((n, 7)), pltpu.SemaphoreType.DMA((n, 7)), pltpu.SemaphoreType.DMA((n,))]

    def _setup(self, sems):
        send_sems, recv_sems, local_sems = sems
        x, y, c = _place()
        me = _lin(x, y, c)
        peers = [(x ^ ((k + 1) >> 2 & 1), y ^ ((k + 1) >> 1 & 1), c ^ ((k + 1) & 1)) for k in range(7)]

        def copy(a, k, sem_k, src, dst):
            return pltpu.make_async_remote_copy(src_ref=src, dst_ref=dst, send_sem=send_sems.at[a, sem_k],
                                                recv_sem=recv_sems.at[a, sem_k], device_id=peers[k], device_id_type=MESH)

        return me, [_lin(*p) for p in peers], copy, local_sems

    def _local(self, a, kind, ins, outs, me, local_sems):
        return pltpu.make_async_copy(ins[a].at[me] if kind == "scatter" else ins[a], outs[a].at[me], local_sems.at[a])

    def start(self, ins, outs, sems):
        me, theirs, copy, local_sems = self._setup(sems)
        for a, (kind, _) in enumerate(self.jobs):
            self._local(a, kind, ins, outs, me, local_sems).start()
            if kind == "scatter":
                for k in range(7):
                    copy(a, k, k, ins[a].at[theirs[k]], outs[a].at[me]).start()
            else:
                for k in (0,) + self.SAME_CORE:
                    copy(a, k, k, ins[a], outs[a].at[me]).start()

    def wait(self, ins, outs, sems):
        me, theirs, copy, local_sems = self._setup(sems)
        for a, (kind, _) in enumerate(self.jobs):
            if kind == "scatter":
                for k in range(7):
                    copy(a, k, k, ins[a].at[me], outs[a].at[theirs[k]]).wait_recv()
                for k in range(7):
                    copy(a, k, k, ins[a].at[theirs[k]], outs[a].at[me]).wait_send()
            else:
                for k in self.SAME_CORE:
                    blk = outs[a].at[theirs[k]]
                    copy(a, k, k, ins[a], blk).wait_recv()
                    copy(a, 0, k + 1, blk, blk).start()
                copy(a, 0, 0, ins[a], outs[a].at[theirs[0]]).wait_recv()
                for k in self.SAME_CORE:
                    copy(a, 0, k + 1, ins[a], outs[a].at[theirs[k + 1]]).wait_recv()
                for k in (0,) + self.SAME_CORE:
                    copy(a, k, k, ins[a], outs[a].at[me]).wait_send()
                for k in self.SAME_CORE:
                    blk = outs[a].at[theirs[k]]
                    copy(a, 0, k + 1, blk, blk).wait_send()
            self._local(a, kind, ins, outs, me, local_sems).wait()


def _carried(kern, carry, n_in, n_out, first, last):
    if carry is None:
        return kern
    nc = len(carry.jobs)

    def wrapped(*refs):
        ins, cin = refs[:n_in], refs[n_in:n_in + nc]
        outs, cout = refs[n_in + nc:n_in + nc + n_out], refs[n_in + nc + n_out:n_in + 2 * nc + n_out]
        scr, sems = refs[n_in + 2 * nc + n_out:-3], refs[-3:]

        @pl.when(first())
        def _():
            carry.start(cin, cout, sems)

        kern(*ins, *outs, *scr)

        @pl.when(last())
        def _():
            carry.wait(cin, cout, sems)

    return wrapped


def _carry_args(carry):
    if carry is None:
        return [], [], [], [], []
    n = len(carry.jobs)
    return [ANY] * n, carry.arrays, [ANY] * n, carry.out_shapes, carry.scratch


def _grid_ends(dims):
    first = lambda: functools.reduce(jnp.logical_and, [pl.program_id(d) == 0 for d in range(len(dims))])
    last = lambda: functools.reduce(jnp.logical_and, [pl.program_id(d) == n - 1 for d, n in enumerate(dims)])
    return first, last


def _mm_call(name, a, b, mode, out_dtype, tm, tn, rows_outer=True, single_b=False, carry=None):
    if mode == "TN":
        (K, M), N = a.shape, b.shape[1]
    else:
        (M, K), N = a.shape, (b.shape[1] if mode == "NN" else b.shape[0])
    assert M % tm == 0 and N % tn == 0, (name, M, N, K, tm, tn)
    ij = (lambda g0, g1: (g0, g1)) if rows_outer else (lambda g0, g1: (g1, g0))
    grid = (M // tm, N // tn) if rows_outer else (N // tn, M // tm)
    if mode == "TN":
        a_spec = pl.BlockSpec((K, tm), lambda g0, g1: (0, ij(g0, g1)[0]))
    else:
        a_spec = pl.BlockSpec((tm, K), lambda g0, g1: (ij(g0, g1)[0], 0))
    b_blk, b_map = ((tn, K), lambda g0, g1: (ij(g0, g1)[1], 0)) if mode == "NT" else \
                   ((K, tn), lambda g0, g1: (0, ij(g0, g1)[1]))
    b_spec = pl.BlockSpec(b_blk, b_map, pipeline_mode=pl.Buffered(1)) if single_b else pl.BlockSpec(b_blk, b_map)
    dims = {"NN": NN, "NT": NT, "TN": TN}[mode]

    def kern(a_ref, b_ref, o_ref):
        o_ref[...] = _dot(a_ref[...], b_ref[...], dims).astype(o_ref.dtype)

    ci, ca, co, cs, cscr = _carry_args(carry)
    res = pl.pallas_call(
        _carried(kern, carry, 2, 1, *_grid_ends(grid)), name=name, grid=grid, in_specs=[a_spec, b_spec] + ci,
        out_specs=[pl.BlockSpec((tm, tn), lambda g0, g1: ij(g0, g1))] + co,
        out_shape=[_sds((M, N), out_dtype)] + cs, scratch_shapes=cscr,
        compiler_params=_params(("arbitrary", "arbitrary")),
    )(a, b, *ca)
    return res[0] if carry is None else (res[0], res[1:])


def _mm_act(name, a, w, mode, out_dtype=BF16, carry=None):
    rows, K = a.shape
    N = w.shape[1] if mode == "NN" else w.shape[0]
    if K > D_FF:
        return _mm_call(name, a, w, mode, out_dtype, rows // 8, N, single_b=True, carry=carry)
    tn = N if N <= 1024 else 1408
    return _mm_call(name, a, w, mode, out_dtype, rows // 4, tn, carry=carry)


def _mm_wgrad(name, x, dy, out_dtype=BF16, carry=None):
    M = x.shape[1]
    tm = 1408 if M == D_FF else 512
    return _mm_call(name, x, dy, "TN", out_dtype, tm, dy.shape[1], single_b=True, carry=carry)


def _sigmoid(x):
    return 0.5 * jnp.tanh(0.5 * x) + 0.5


def _silu(x):
    return x * _sigmoid(x)


def _silu_grad(x):
    s = _sigmoid(x)
    return s * (1.0 + x * (1.0 - s))


_GELU_K = math.sqrt(2.0 / math.pi)


def _gelu(x):
    return 0.5 * x * (1.0 + jnp.tanh(_GELU_K * (x + 0.044715 * x * x * x)))


def _gelu_grad(x):
    t = jnp.tanh(_GELU_K * (x + 0.044715 * x * x * x))
    return 0.5 * (1.0 + t) + 0.5 * x * (1.0 - t * t) * _GELU_K * (1.0 + 3.0 * 0.044715 * x * x)


def _log_sigmoid(x):
    return jnp.minimum(x, 0.0) - jnp.log(1.0 + jnp.exp(-jnp.abs(x)))


def _rms(x):
    x = x.astype(F32)
    r = lax.rsqrt(jnp.mean(x * x, axis=-1, keepdims=True) + EPS)
    return x * r, r


def _rms_bwd(dy, y, r):
    return r * (dy - y * jnp.mean(dy * y, axis=-1, keepdims=True))


def _modrow(mod_ref, i, chunk):
    lo = mod_ref[0:1, chunk * D:(chunk + 1) * D]
    hi = mod_ref[1:2, chunk * D:(chunk + 1) * D]
    return jnp.where(i == 0, lo, hi)


def _acc_seg(acc_ref, i, val):
    zero = jnp.zeros_like(val)
    acc_ref[0:1, :] += jnp.where(i == 0, val, zero)
    acc_ref[1:2, :] += jnp.where(i == 0, zero, val)


def _colsum(x):
    return jnp.sum(x, axis=0, keepdims=True)


SH1, SC1, GA1, SH2, SC2, GA2 = range(6)


def _normmod_fwd(name, xa, g, mod, c_sh, c_sc):
    T = xa.shape[0]

    def body(i, ins, ps, outs, acc):
        y, _ = _rms(ins[0][...])
        h = (y * ps[0][...]) * (1.0 + _modrow(ps[1], i, c_sc)) + _modrow(ps[1], i, c_sh)
        outs[0][...] = h.astype(BF16)

    return _ew(name, body, T // TR, [(xa, *_rowblk(D))], [g, mod], [(_sds((T, D), BF16), *_rowblk(D))])[0]


def _modrows(mod_ref, row0, n, chunk):
    t = row0 + lax.broadcasted_iota(jnp.int32, (n, 1), 0)
    return jnp.where(t < CTX, mod_ref[0:1, chunk * D:(chunk + 1) * D], mod_ref[1:2, chunk * D:(chunk + 1) * D])


def _loss_resid_bwd(name, x_out, target, mat, gpost, mod, c_ga):
    T = x_out.shape[0]

    def body(i, ins, ps, outs, acc):
        err = ins[0][...] - ins[1][...]
        lat = i > 0
        dx = jnp.where(lat, err * (1.0 / D), 0.0)
        outs[0][...] = dx
        acc[2][...] += jnp.where(lat, _colsum(err * err), 0.0)
        outs[1][...] = _resid_bwd_vals(i, dx, ins[2][...], ps[0][...], ps[1], c_ga, acc[0], acc[1]).astype(BF16)

    tgt_blk = ((TR, D), lambda i: (jnp.maximum(i - 1, 0), 0))
    return _ew(name, body, T // TR, [(x_out, *_rowblk(D)), (target, *tgt_blk), (mat, *_rowblk(D))], [gpost, mod],
               [(_sds((T, D), F32), *_rowblk(D)), (_sds((T, D), BF16), *_rowblk(D))],
               [_sds((2, D), F32), _sds((1, D), F32), _sds((1, D), F32)])


def _mod_for(mod_ref, i, chunk, row0, n):
    return _modrow(mod_ref, i, chunk) if row0 is None else _modrows(mod_ref, row0, n, chunk)


def _acc_for(acc_ref, i, v, row0):
    if row0 is None:
        _acc_seg(acc_ref, i, _colsum(v))
        return

    @pl.when(row0 < CTX)
    def _():
        is_ctx = row0 + lax.broadcasted_iota(jnp.int32, (v.shape[0], 1), 0) < CTX
        acc_ref[0:1, :] += _colsum(jnp.where(is_ctx, v, 0.0))
        acc_ref[1:2, :] += _colsum(jnp.where(is_ctx, 0.0, v))

    @pl.when(row0 >= CTX)
    def _():
        acc_ref[1:2, :] += _colsum(v)


def _resid_bwd_vals(i, dout, mat, gpost, mod_ref, c_ga, acc_ga, acc_g, row0=None):
    ym, rm = _rms(mat)
    ga = _mod_for(mod_ref, i, c_ga, row0, dout.shape[0])
    _acc_for(acc_ga, i, dout * (ym * gpost), row0)
    dn = dout * ga
    acc_g[...] += _colsum(dn * ym)
    return _rms_bwd(dn * gpost, ym, rm)


def _normmod_bwd_vals(i, dh, xin, g, mod_ref, c_sh, c_sc, acc_sh, acc_sc, acc_g, row0=None):
    dh = dh.astype(F32)
    y, r = _rms(xin)
    _acc_for(acc_sc, i, dh * (y * g), row0)
    _acc_for(acc_sh, i, dh, row0)
    dyg = dh * (1.0 + _mod_for(mod_ref, i, c_sc, row0, dh.shape[0]))
    acc_g[...] += _colsum(dyg * y)
    return _rms_bwd(dyg * g, y, r)


def _parts(i, tm):
    return [(slice(0, tm), i * tm)]


FT = 1408


def _ffn_in_fused(name, h2, w_t, carry=None):
    T = h2.shape[0]
    tm, nj = T // 4, D_FF // FT

    def kern(a_ref, bg_ref, bu_ref, fg_ref, fu_ref, s_ref):
        for rows, _ in _parts(0, tm):
            a = a_ref[rows, :]
            g = _dot(a, bg_ref[...], NT)
            u = _dot(a, bu_ref[...], NT)
            fg_ref[rows, :] = g.astype(BF16)
            fu_ref[rows, :] = u.astype(BF16)
            s_ref[rows, :] = (_silu(g) * u).astype(BF16)

    o_spec = pl.BlockSpec((tm, FT), lambda i, j: (i, j))
    ci, ca, co, cs, cscr = _carry_args(carry)
    res = pl.pallas_call(
        _carried(kern, carry, 3, 3, *_grid_ends((4, nj))), name=name, grid=(4, nj),
        in_specs=[pl.BlockSpec((tm, D), lambda i, j: (i, 0)), pl.BlockSpec((FT, D), lambda i, j: (j, 0)),
                  pl.BlockSpec((FT, D), lambda i, j: (j + nj, 0))] + ci,
        out_specs=[o_spec] * 3 + co, out_shape=[_sds((T, D_FF), BF16)] * 3 + cs, scratch_shapes=cscr,
        compiler_params=_params(("arbitrary", "arbitrary")),
    )(h2, w_t, w_t, *ca)
    return res if carry is None else (res[:3], res[3:])


def _norm_chain(row0, xin, mat, gpost, mod_ref, c_ga, gnext, modn_ref, c_sh, c_sc):
    n = xin.shape[0]
    ym, _ = _rms(mat.astype(BF16))
    xo = xin + _modrows(mod_ref, row0, n, c_ga) * (ym * gpost)
    y, _ = _rms(xo)
    h = (y * gnext) * (1.0 + _modrows(modn_ref, row0, n, c_sc)) + _modrows(modn_ref, row0, n, c_sh)
    return xo, h.astype(BF16)


def _out_fused(name, p, u, o_all, xa, w_o_rnn, w_o_attn, w_out, gpost, mod, gnext):
    T = u.shape[0]
    tm = T // 8

    def kern(g0, g1, g2, g3, u_ref, o_ref, xa_ref, wr_ref, wa_ref, w_ref, gpost_ref, mod_ref, gnext_ref,
             ya_ref, yb_ref, z_ref, m_ref, x1_ref, h2_ref):
        for rows, row0 in _parts(pl.program_id(0), tm):
            ya = _dot(u_ref[rows, :], wr_ref[...]).astype(BF16)
            yb = _dot(o_ref[rows, :], wa_ref[...]).astype(BF16)
            ya_ref[rows, :] = ya
            yb_ref[rows, :] = yb
            ga = _sigmoid(jnp.concatenate([g0[rows, :], g1[rows, :]], axis=1).astype(F32))
            gb = _sigmoid(jnp.concatenate([g2[rows, :], g3[rows, :]], axis=1).astype(F32))
            z = (ga * ya.astype(F32) + gb * yb.astype(F32)).astype(BF16)
            z_ref[rows, :] = z
            m = _dot(z, w_ref[...])
            m_ref[rows, :] = m.astype(BF16)
            x1_ref[rows, :], h2_ref[rows, :] = _norm_chain(row0, xa_ref[rows, :], m, gpost_ref[...], mod_ref, GA1,
                                                           gnext_ref[...], mod_ref, SH2, SC2)

    row = lambda w: pl.BlockSpec((tm, w), lambda i: (i, 0))
    return pl.pallas_call(
        kern, name=name, grid=(T // tm,),
        in_specs=[pl.BlockSpec((tm, GLB), lambda i, q=q: (i, COL_GL // GLB + q)) for q in range(4)]
                 + [row(D), row(D), row(D)] + [_full_spec(a.shape) for a in (w_o_rnn, w_o_attn, w_out, gpost, mod, gnext)],
        out_specs=[row(D)] * 6,
        out_shape=[_sds((T, D), BF16)] * 4 + [_sds((T, D), F32), _sds((T, D), BF16)],
        compiler_params=_params(),
    )(p, p, p, p, u, o_all, xa, w_o_rnn, w_o_attn, w_out, gpost, mod, gnext)


def _ffn_out_fused(name, s, w, x1, gpost, mod, nxt=None):
    T = s.shape[0]
    tm = T // 8

    def kern(s_ref, w_ref, x1_ref, gpost_ref, mod_ref, *rest):
        for rows, row0 in _parts(pl.program_id(0), tm):
            e = _dot(s_ref[rows, :], w_ref[...])
            if nxt is None:
                e_ref, xo_ref = rest
                ym, _ = _rms(e.astype(BF16))
                xo_ref[rows, :] = x1_ref[rows, :] + _modrows(mod_ref, row0, e.shape[0], GA2) * (ym * gpost_ref[...])
            else:
                gnext_ref, modn_ref, e_ref, xo_ref, h_ref = rest
                xo_ref[rows, :], h_ref[rows, :] = _norm_chain(row0, x1_ref[rows, :], e, gpost_ref[...], mod_ref, GA2,
                                                              gnext_ref[...], modn_ref, SH1, SC1)
            e_ref[rows, :] = e.astype(BF16)

    row = lambda w_: pl.BlockSpec((tm, w_), lambda i: (i, 0))
    extra = [] if nxt is None else list(nxt)
    return pl.pallas_call(
        kern, name=name, grid=(T // tm,),
        in_specs=[row(D_FF), _full_spec(w.shape), row(D), _full_spec(gpost.shape), _full_spec(mod.shape)]
                 + [_full_spec(a.shape) for a in extra],
        out_specs=[row(D)] * (2 if nxt is None else 3),
        out_shape=[_sds((T, D), BF16), _sds((T, D), F32)] + ([] if nxt is None else [_sds((T, D), BF16)]),
        compiler_params=_params(),
    )(s, w, x1, gpost, mod, *extra)


def _ffn_bwd_fused(name, fg, fu, w, de=None, head=None):
    T = fg.shape[0]
    tm = T // 8
    row = lambda w_: pl.BlockSpec((tm, w_), lambda i: (i, 0))
    w_spec = pl.BlockSpec(w.shape, lambda i: (0, 0), pipeline_mode=pl.Buffered(1))

    def tail(rows, de_v, fg_ref, fu_ref, w_ref, df_ref):
        ds = _dot(de_v, w_ref[...], NT)
        g, u = fg_ref[rows, :].astype(F32), fu_ref[rows, :].astype(F32)
        df_ref[rows, :] = jnp.concatenate([ds * u * _silu_grad(g), ds * _silu(g)], axis=1).astype(BF16)

    if head is None:
        def kern(de_ref, fg_ref, fu_ref, w_ref, df_ref):
            for rows, _ in _parts(pl.program_id(0), tm):
                tail(rows, de_ref[rows, :], fg_ref, fu_ref, w_ref, df_ref)

        return pl.pallas_call(
            kern, name=name, grid=(T // tm,), in_specs=[row(D), row(D_FF), row(D_FF), w_spec],
            out_specs=[row(2 * D_FF)], out_shape=[_sds((T, 2 * D_FF), BF16)], compiler_params=_params(),
        )(de, fg, fu, w)

    dx2, e, gpost, mod = head

    def kern(dx_ref, e_ref, fg_ref, fu_ref, w_ref, gpost_ref, mod_ref, de_ref, df_ref, dga_ref, dg_ref):
        i = pl.program_id(0)

        @pl.when(i == 0)
        def _():
            dga_ref[...] = jnp.zeros(dga_ref.shape, F32)
            dg_ref[...] = jnp.zeros(dg_ref.shape, F32)

        for rows, row0 in _parts(i, tm):
            de_v = _resid_bwd_vals(i, dx_ref[rows, :], e_ref[rows, :], gpost_ref[...], mod_ref, GA2, dga_ref, dg_ref,
                                   row0=row0).astype(BF16)
            de_ref[rows, :] = de_v
            tail(rows, de_v, fg_ref, fu_ref, w_ref, df_ref)

    return pl.pallas_call(
        kern, name=name, grid=(T // tm,),
        in_specs=[row(D), row(D), row(D_FF), row(D_FF), w_spec, _full_spec(gpost.shape), _full_spec(mod.shape)],
        out_specs=[row(D), row(2 * D_FF), _full_spec((2, D)), _full_spec((1, D))],
        out_shape=[_sds((T, D), BF16), _sds((T, 2 * D_FF), BF16), _sds((2, D), F32), _sds((1, D), F32)],
        compiler_params=_params(),
    )(dx2, e, fg, fu, w, gpost, mod)


def _zero_at_start(i, refs):
    @pl.when(i == 0)
    def _():
        for r in refs:
            r[...] = jnp.zeros(r.shape, F32)


def _proj_bwd_fused(name, dp, dgl, w_in_t, xa, dx1, gpre, mod, carry=None):
    T = dp.shape[0]
    tm = T // 8
    row = lambda w_: pl.BlockSpec((tm, w_), lambda i: (i, 0))

    def kern(dp_ref, dgl_ref, w_ref, xa_ref, dx1_ref, g_ref, mod_ref, dxa_ref, dsh_ref, dsc_ref, dg_ref):
        i = pl.program_id(0)
        _zero_at_start(i, (dsh_ref, dsc_ref, dg_ref))
        for rows, row0 in _parts(i, tm):
            dh = _dot(dp_ref[rows, :], w_ref[0:DP_W, :]) + _dot(dgl_ref[rows, :], w_ref[DP_W:, :])
            dxa_ref[rows, :] = dx1_ref[rows, :] + _normmod_bwd_vals(i, dh, xa_ref[rows, :], g_ref[...], mod_ref, SH1,
                                                                    SC1, dsh_ref, dsc_ref, dg_ref, row0=row0)

    ci, ca, co, cs, cscr = _carry_args(carry)
    res = pl.pallas_call(
        _carried(kern, carry, 7, 4, *_grid_ends((T // tm,))), name=name, grid=(T // tm,),
        in_specs=[row(DP_W), row(P_W - DP_W),
                  pl.BlockSpec(w_in_t.shape, lambda i: (0, 0), pipeline_mode=pl.Buffered(1)), row(D), row(D),
                  _full_spec(gpre.shape), _full_spec(mod.shape)] + ci,
        out_specs=[row(D), _full_spec((2, D)), _full_spec((2, D)), _full_spec((1, D))] + co,
        out_shape=[_sds((T, D), F32), _sds((2, D), F32), _sds((2, D), F32), _sds((1, D), F32)] + cs,
        scratch_shapes=cscr, compiler_params=_params(),
    )(dp, dgl, w_in_t, xa, dx1, gpre, mod, *ca)
    return res if carry is None else (res[:4], res[4:])


def _proj_wgrad(name, dp, dgl, h, carry=None):
    T, N = h.shape
    n1, n2 = DP_W // GLB, (P_W - DP_W) // GLB

    def kern(a1_ref, a2_ref, h_ref, o_ref):
        i = pl.program_id(0)

        @pl.when(i < n1)
        def _():
            o_ref[...] = _dot(a1_ref[...], h_ref[...], TN).astype(o_ref.dtype)

        @pl.when(i >= n1)
        def _():
            o_ref[...] = _dot(a2_ref[...], h_ref[...], TN).astype(o_ref.dtype)

    ci, ca, co, cs, cscr = _carry_args(carry)
    res = pl.pallas_call(
        _carried(kern, carry, 3, 1, *_grid_ends((n1 + n2,))), name=name, grid=(n1 + n2,),
        in_specs=[pl.BlockSpec((T, GLB), lambda i: (0, jnp.minimum(i, n1 - 1))),
                  pl.BlockSpec((T, GLB), lambda i: (0, jnp.maximum(i - n1, 0))),
                  pl.BlockSpec((T, N), lambda i: (0, 0), pipeline_mode=pl.Buffered(1))] + ci,
        out_specs=[pl.BlockSpec((GLB, N), lambda i: (i, 0))] + co,
        out_shape=[_sds((P_W, N), BF16)] + cs, scratch_shapes=cscr, compiler_params=_params(),
    )(dp, dgl, h, *ca)
    return res[0] if carry is None else (res[0], res[1:])


def _ffn_in_bwd_fused(name, df, w_t, x1, dres, mat, gpre, mod, gpost, carry=None):
    T = df.shape[0]
    tm = T // 8
    row = lambda w_: pl.BlockSpec((tm, w_), lambda i: (i, 0))

    def kern(df_ref, w_ref, x1_ref, dres_ref, mat_ref, gpre_ref, mod_ref, gpost_ref,
             dx1_ref, dm_ref, dsh_ref, dsc_ref, dgpre_ref, dga_ref, dgpost_ref):
        i = pl.program_id(0)
        _zero_at_start(i, (dsh_ref, dsc_ref, dgpre_ref, dga_ref, dgpost_ref))
        for rows, row0 in _parts(i, tm):
            dh2 = _dot(df_ref[rows, :], w_ref[...])
            dx1 = dres_ref[rows, :] + _normmod_bwd_vals(i, dh2, x1_ref[rows, :], gpre_ref[...], mod_ref, SH2, SC2,
                                                        dsh_ref, dsc_ref, dgpre_ref, row0=row0)
            dx1_ref[rows, :] = dx1
            dm_ref[rows, :] = _resid_bwd_vals(i, dx1, mat_ref[rows, :], gpost_ref[...], mod_ref, GA1, dga_ref,
                                              dgpost_ref, row0=row0).astype(BF16)

    ci, ca, co, cs, cscr = _carry_args(carry)
    res = pl.pallas_call(
        _carried(kern, carry, 8, 7, *_grid_ends((T // tm,))), name=name, grid=(T // tm,),
        in_specs=[row(2 * D_FF), pl.BlockSpec(w_t.shape, lambda i: (0, 0), pipeline_mode=pl.Buffered(1)), row(D),
                  row(D), row(D), _full_spec(gpre.shape), _full_spec(mod.shape), _full_spec(gpost.shape)] + ci,
        out_specs=[row(D), row(D), _full_spec((2, D)), _full_spec((2, D)), _full_spec((1, D)), _full_spec((2, D)),
                   _full_spec((1, D))] + co,
        out_shape=[_sds((T, D), F32), _sds((T, D), BF16), _sds((2, D), F32), _sds((2, D), F32), _sds((1, D), F32),
                   _sds((2, D), F32), _sds((1, D), F32)] + cs,
        scratch_shapes=cscr, compiler_params=_params(),
    )(df, w_t, x1, dres, mat, gpre, mod, gpost, *ca)
    return res if carry is None else (res[:7], res[7:])


def _out_bwd_fused(name, dm, w_out, w_o_rnn, w_o_attn, p, ya, yb):
    T = dm.shape[0]
    tm = T // 8
    row = lambda w_: pl.BlockSpec((tm, w_), lambda i: (i, 0))

    def kern(dm_ref, w_ref, wr_ref, wa_ref, g0, g1, g2, g3, ya_ref, yb_ref, dya_ref, dyb_ref, dgl_ref, du_ref, do_ref):
        for rows, _ in _parts(pl.program_id(0), tm):
            dz = _dot(dm_ref[rows, :], w_ref[...], NT)
            ga = _sigmoid(jnp.concatenate([g0[rows, :], g1[rows, :]], axis=1).astype(F32))
            gb = _sigmoid(jnp.concatenate([g2[rows, :], g3[rows, :]], axis=1).astype(F32))
            dya = (dz * ga).astype(BF16)
            dyb = (dz * gb).astype(BF16)
            dya_ref[rows, :] = dya
            dyb_ref[rows, :] = dyb
            dgl_ref[rows, :] = jnp.concatenate([dz * ya_ref[rows, :].astype(F32) * ga * (1.0 - ga),
                                                dz * yb_ref[rows, :].astype(F32) * gb * (1.0 - gb)],
                                               axis=1).astype(BF16)
            du_ref[rows, :] = _dot(dya, wr_ref[...], NT).astype(BF16)
            do_ref[rows, :] = _dot(dyb, wa_ref[...], NT).astype(BF16)

    return pl.pallas_call(
        kern, name=name, grid=(T // tm,),
        in_specs=[row(D)] + [_full_spec(w.shape) for w in (w_out, w_o_rnn, w_o_attn)]
                 + [pl.BlockSpec((tm, GLB), lambda i, q=q: (i, COL_GL // GLB + q)) for q in range(4)] + [row(D), row(D)],
        out_specs=[row(D), row(D), row(2 * D), row(D), row(D)],
        out_shape=[_sds((T, D), BF16), _sds((T, D), BF16), _sds((T, 2 * D), BF16), _sds((T, D), BF16),
                   _sds((T, D), BF16)],
        compiler_params=_params(),
    )(dm, w_out, w_o_rnn, w_o_attn, p, p, p, p, ya, yb)


AB = 128
CTX_BLKS = CTX // AB


def _rope_tables(S):
    pos = jnp.arange(S, dtype=jnp.int32)
    inv = ROPE_BASE ** (-jnp.arange(N_FREQ, dtype=F32) / N_FREQ)
    ang_r = (pos // GRID_W).astype(F32)[:, None] * inv[None, :]
    ang_c = (pos % GRID_W).astype(F32)[:, None] * inv[None, :]
    cos = jnp.concatenate([jnp.cos(ang_r)] * 2 + [jnp.cos(ang_c)] * 2, axis=1)
    sin = jnp.concatenate([-jnp.sin(ang_r), jnp.sin(ang_r), -jnp.sin(ang_c), jnp.sin(ang_c)], axis=1)
    return cos, sin


def _rope(x, cos, sin):
    w = x.shape[1]
    reps = w // HEAD
    lane = lax.broadcasted_iota(jnp.int32, x.shape, 1)
    partner = jnp.where((lane & 63) < 32, pltpu.roll(x, w - 32, 1), pltpu.roll(x, 32, 1))
    return x * jnp.tile(cos, (1, reps)) + partner * jnp.tile(sin, (1, reps))


def _unrope(dx, cos, sin):
    w = dx.shape[1]
    reps = w // HEAD
    lane = lax.broadcasted_iota(jnp.int32, dx.shape, 1)
    t = dx * jnp.tile(sin, (1, reps))
    partner = jnp.where((lane & 63) < 32, pltpu.roll(t, w - 32, 1), pltpu.roll(t, 32, 1))
    return dx * jnp.tile(cos, (1, reps)) + partner


def _kv_prep(name, p, cos, sin, S):
    T = CTX + S
    nt = T // AB
    KW = N_KV * HEAD

    def with_ones(v):
        ones = jnp.ones((AB, HEAD), BF16)
        return jnp.concatenate([v[:, kh * HEAD:(kh + 1) * HEAD] if part == 0 else ones
                                for kh in range(N_KV) for part in range(2)], axis=1)

    def kern(k_ref, v_ref, cos_ref, sin_ref, kp_ref, vp_ref, kc_ref, vc_ref):
        i = pl.program_id(0)

        @pl.when(i < CTX_BLKS)
        def _():
            kc_ref[...] = k_ref[...]
            vc_ref[...] = with_ones(v_ref[...])

        @pl.when((i < CTX_BLKS) | (i >= nt))
        def _():
            kp_ref[...] = jnp.zeros(kp_ref.shape, BF16)
            vp_ref[...] = jnp.zeros(vp_ref.shape, BF16)

        @pl.when((i >= CTX_BLKS) & (i < nt))
        def _():
            kp_ref[...] = _rope(k_ref[...].astype(F32), cos_ref[...], sin_ref[...]).astype(BF16)
            vp_ref[...] = with_ones(v_ref[...])

    tok = lambda i: jnp.minimum(i, nt - 1)
    lat_map = lambda i: (jnp.clip(i - CTX_BLKS, 0, nt - CTX_BLKS - 1), 0)
    ctx_map = lambda i: (jnp.minimum(i, CTX_BLKS - 1), 0)
    return pl.pallas_call(
        kern, name=name, grid=(nt + CTX_BLKS,),
        in_specs=[pl.BlockSpec((AB, KW), lambda i: (tok(i), COL_K // KW)),
                  pl.BlockSpec((AB, KW), lambda i: (tok(i), COL_V // KW)),
                  pl.BlockSpec((AB, HEAD), lat_map), pl.BlockSpec((AB, HEAD), lat_map)],
        out_specs=[pl.BlockSpec((AB, KW), lambda i: (i, 0)), pl.BlockSpec((AB, 2 * KW), lambda i: (i, 0)),
                   pl.BlockSpec((AB, KW), ctx_map), pl.BlockSpec((AB, 2 * KW), ctx_map)],
        out_shape=[_sds((S + 2 * CTX, KW), BF16), _sds((S + 2 * CTX, 2 * KW), BF16),
                   _sds((CTX, KW), BF16), _sds((CTX, 2 * KW), BF16)],
        compiler_params=_params(),
    )(p, p, cos, sin)


GW = Q_PER_KV * HEAD
HG = Q_PER_KV


def _band_bias(S):
    r = jnp.arange(AB, dtype=jnp.int32)[:, None]
    c = jnp.arange(3 * AB, dtype=jnp.int32)[None, :]
    near = jnp.abs(c - AB - r) <= AB
    valid = jnp.stack([near & (c >= AB), near, near & (c < 2 * AB)])
    return jnp.where(valid, 0.0, NEG_INF).astype(F32)


def _bias_spec(S):
    nb = S // AB
    return pl.BlockSpec((None, AB, 3 * AB), lambda kh, n: (jnp.where(n == 0, 0, jnp.where(n == nb - 1, 2, 1)), 0, 0))


def _head_probs(q, sink, kc, vce, kb, vbe, bias):
    s_c = _dot(q, kc, NT)
    m = jnp.maximum(jnp.max(s_c, axis=-1, keepdims=True), sink)
    if kb is not None:
        s_b = _dot(q, kb, NT) + bias
        m = jnp.maximum(m, jnp.max(s_b, axis=-1, keepdims=True))
    p_c = jnp.exp(s_c - m).astype(BF16)
    acc = _dot(p_c, vce)
    p_b = None
    if kb is not None:
        p_b = jnp.exp(s_b - m).astype(BF16)
        acc = acc + _dot(p_b, vbe)
    return p_c, p_b, m, acc


def _query_block(q_ref, rope_refs):
    q = q_ref[...].astype(F32)
    if rope_refs is not None:
        q = _rope(q, rope_refs[0][...], rope_refs[1][...])
    return (q * ATT_SCALE).astype(BF16)


def _attn_fwd(name, p, kc, vc, sink4, S, band=None, prev=None, carry=None):
    T = p.shape[0]
    has_band = band is not None
    nq = S // AB if has_band else CTX_BLKS
    q_off = CTX_BLKS if has_band else 0

    def kern(*refs):
        q_ref, kc_ref, vc_ref, sink_ref = refs[:4]
        rest = refs[4:]
        o_ref = rest[-1]
        n = pl.program_id(1)
        kc_v, vce = kc_ref[...], vc_ref[...]
        kb = vbe = bias = None
        if has_band:
            kp_ref, vp_ref, bias_ref = rest[:3]
            start = pl.multiple_of(n * AB + (CTX - AB), AB)
            kb = kp_ref[pl.ds(start, 3 * AB), :]
            vbe = vp_ref[pl.ds(start, 3 * AB), :]
            bias = bias_ref[...]
        q_blk = _query_block(q_ref, rest[3:5] if has_band else None)
        outs = []
        for g in range(Q_PER_KV):
            sink = sink_ref[g:g + 1, 0:1]
            _, _, m, acc = _head_probs(q_blk[:, g * HEAD:(g + 1) * HEAD], sink, kc_v, vce, kb, vbe, bias)
            l = acc[:, HEAD:] + jnp.exp(sink - m)
            outs.append(acc[:, :HEAD] / l)
        o_ref[...] = jnp.concatenate(outs, axis=1).astype(BF16)

    rope_spec = pl.BlockSpec((AB, HEAD), lambda kh, n: (n, 0))
    in_specs = [pl.BlockSpec((AB, GW), lambda kh, n: (n + q_off, COL_Q // GW + kh)),
                pl.BlockSpec((CTX, HEAD), lambda kh, n: (0, kh)), pl.BlockSpec((CTX, 2 * HEAD), lambda kh, n: (0, kh)),
                pl.BlockSpec((None, Q_PER_KV, HEAD), lambda kh, n: (kh, 0, 0))]
    args = [p, kc, vc, sink4]
    if has_band:
        in_specs += [pl.BlockSpec((S + 2 * CTX, HEAD), lambda kh, n: (0, kh)),
                     pl.BlockSpec((S + 2 * CTX, 2 * HEAD), lambda kh, n: (0, kh)), _bias_spec(S), rope_spec, rope_spec]
        args += list(band)
    alias = {}
    if prev is not None:
        in_specs.append(ANY)
        alias = {len(args): 0}
        args.append(prev)
    ci, ca, co, cs, cscr = _carry_args(carry)
    res = pl.pallas_call(
        _carried(kern, carry, len(args), 1, *_grid_ends((N_KV, nq))), name=name, grid=(N_KV, nq),
        in_specs=in_specs + ci,
        out_specs=[pl.BlockSpec((AB, GW), lambda kh, n: (n + q_off, kh))] + co,
        out_shape=[_sds((T, N_Q * HEAD), BF16)] + cs, input_output_aliases=alias, scratch_shapes=cscr,
        compiler_params=_params(("arbitrary", "arbitrary")),
    )(*args, *ca)
    return res[0] if carry is None else (res[0], res[1:])


def _attn_bwd(name, p, kc, vc, sink4, o_all, do_all, S, band=None, prev_dq=None, carry=None):
    T = p.shape[0]
    has_band = band is not None
    nq = S // AB if has_band else CTX_BLKS
    q_off = CTX_BLKS if has_band else 0
    KW = N_KV * HEAD

    def kern(*refs):
        q_ref, kc_ref, vc_ref, sink_ref, o_ref, do_ref = refs[:6]
        rest = refs[6:]
        if has_band:
            kp_ref, vp_ref, bias_ref, cos_ref, sin_ref = rest[:5]
            rest = rest[5:]
        if prev_dq is not None:
            rest = rest[1:]
        dq_ref, dkc_ref, dvc_ref, dsink_ref = rest[:4]
        n = pl.program_id(1)

        @pl.when(n == 0)
        def _():
            dkc_ref[...] = jnp.zeros(dkc_ref.shape, F32)
            dvc_ref[...] = jnp.zeros(dvc_ref.shape, F32)
            dsink_ref[...] = jnp.zeros(dsink_ref.shape, F32)
            if has_band:
                rest[4][...] = jnp.zeros(rest[4].shape, F32)
                rest[5][...] = jnp.zeros(rest[5].shape, F32)

        kc_v, vce = kc_ref[...], vc_ref[...]
        vc_v = vce[:, :HEAD]
        kb = vbe = vb = bias = None
        if has_band:
            start = pl.multiple_of(n * AB + (CTX - AB), AB)
            kb = kp_ref[pl.ds(start, 3 * AB), :]
            vbe = vp_ref[pl.ds(start, 3 * AB), :]
            vb = vbe[:, :HEAD]
            bias = bias_ref[...]
        dq_parts, dsink_parts = [], []
        q_blk = _query_block(q_ref, (cos_ref, sin_ref) if has_band else None)
        for g0 in range(0, Q_PER_KV, HG):
            heads = range(g0, g0 + HG)
            stack = lambda ref: jnp.concatenate([ref[:, g * HEAD:(g + 1) * HEAD] for g in heads], axis=0)
            q4, do4 = stack(q_blk), stack(do_ref)
            sink = jnp.concatenate([jnp.broadcast_to(sink_ref[g:g + 1, 0:1], (AB, 1)) for g in heads], axis=0)
            s_c = _dot(q4, kc_v, NT)
            m = jnp.maximum(jnp.max(s_c, axis=-1, keepdims=True), sink)
            if has_band:
                s_b = _dot(q4, kb, NT) + jnp.tile(bias, (HG, 1))
                m = jnp.maximum(m, jnp.max(s_b, axis=-1, keepdims=True))
            p_c = jnp.exp(s_c - m).astype(BF16).astype(F32)
            p_sink = jnp.exp(sink - m)
            l = jnp.sum(p_c, axis=-1, keepdims=True) + p_sink
            if has_band:
                p_b = jnp.exp(s_b - m).astype(BF16).astype(F32)
                l = l + jnp.sum(p_b, axis=-1, keepdims=True)
            inv = 1.0 / l
            delta = jnp.sum(do4.astype(F32) * stack(o_ref).astype(F32), axis=-1, keepdims=True)
            do4b = do4.astype(BF16)
            pn_c = (p_c * inv).astype(BF16)
            ds_c = (p_c * inv * (_dot(do4b, vc_v, NT) - delta)).astype(BF16)
            dq4 = _dot(ds_c, kc_v)
            dkc_ref[...] += _dot(q4, ds_c, TN)
            dvc_ref[...] += _dot(do4b, pn_c, TN)
            if has_band:
                pn_b = (p_b * inv).astype(BF16)
                ds_b = (p_b * inv * (_dot(do4b, vb, NT) - delta)).astype(BF16)
                dq4 = dq4 + _dot(ds_b, kb)
                rest[4][:, pl.ds(start, 3 * AB)] += _dot(q4, ds_b, TN)
                rest[5][:, pl.ds(start, 3 * AB)] += _dot(do4b, pn_b, TN)
            dq4 = dq4 * ATT_SCALE
            dq_parts += [dq4[k * AB:(k + 1) * AB, :] for k in range(HG)]
            ps = p_sink * inv * delta
            dsink_parts += [jnp.broadcast_to(-jnp.sum(ps[k * AB:(k + 1) * AB, :], axis=0, keepdims=True), (1, HEAD))
                            for k in range(HG)]
        dq = jnp.concatenate(dq_parts, axis=1)
        dq_ref[...] = (_unrope(dq, cos_ref[...], sin_ref[...]) if has_band else dq).astype(BF16)
        dsink_ref[...] += jnp.concatenate(dsink_parts, axis=0)

    q_spec = pl.BlockSpec((AB, GW), lambda kh, n: (n + q_off, kh))
    c_spec = pl.BlockSpec((CTX, HEAD), lambda kh, n: (0, kh))
    ce_spec = pl.BlockSpec((CTX, 2 * HEAD), lambda kh, n: (0, kh))
    s_spec = pl.BlockSpec((None, Q_PER_KV, HEAD), lambda kh, n: (kh, 0, 0))
    ct_spec = pl.BlockSpec((HEAD, CTX), lambda kh, n: (kh, 0))
    dq_spec = pl.BlockSpec((AB, GW), lambda kh, n: (n + q_off, COL_Q // GW + kh))
    in_specs = [dq_spec, c_spec, ce_spec, s_spec, q_spec, q_spec]
    args = [p, kc, vc, sink4, o_all, do_all]
    out_specs = [dq_spec, ct_spec, ct_spec, s_spec]
    out_shape = [_sds((T, DP_W), BF16), _sds((KW, CTX), F32), _sds((KW, CTX), F32), _sds((N_KV, Q_PER_KV, HEAD), F32)]
    if has_band:
        p_spec = pl.BlockSpec((S + 2 * CTX, HEAD), lambda kh, n: (0, kh))
        pt_spec = pl.BlockSpec((HEAD, S + 2 * CTX), lambda kh, n: (kh, 0))
        rope_spec = pl.BlockSpec((AB, HEAD), lambda kh, n: (n, 0))
        in_specs += [p_spec, pl.BlockSpec((S + 2 * CTX, 2 * HEAD), lambda kh, n: (0, kh)), _bias_spec(S), rope_spec,
                     rope_spec]
        args += list(band)
        out_specs += [pt_spec, pt_spec]
        out_shape += [_sds((KW, S + 2 * CTX), F32)] * 2
    alias = {}
    if prev_dq is not None:
        in_specs.append(ANY)
        alias = {len(args): 0}
        args.append(prev_dq)
    ci, ca, co, cs, cscr = _carry_args(carry)
    n_out = len(out_specs)
    res = pl.pallas_call(
        _carried(kern, carry, len(args), n_out, *_grid_ends((N_KV, nq))), name=name, grid=(N_KV, nq),
        in_specs=in_specs + ci, out_specs=out_specs + co, out_shape=out_shape + cs, scratch_shapes=cscr,
        input_output_aliases=alias, compiler_params=_params(("arbitrary", "arbitrary")),
    )(*args, *ca)
    return res if carry is None else (res[:n_out], res[n_out:])


def _dkv_assemble(name, dp, dkp, dvp, dkc_l, dvc_l, dkc_c, dvc_c, cos, sin, S):
    T = CTX + S
    KW = N_KV * HEAD

    def kern(dkp_ref, dvp_ref, dkcl_ref, dvcl_ref, dkcc_ref, dvcc_ref, cos_ref, sin_ref, dp_in, out_ref):
        i = pl.program_id(0)

        @pl.when(i == 0)
        def _():
            out_ref[...] = jnp.concatenate([(dkcl_ref[...] + dkcc_ref[...]).T, (dvcl_ref[...] + dvcc_ref[...]).T],
                                           axis=1).astype(BF16)

        @pl.when(i > 0)
        def _():
            out_ref[...] = jnp.concatenate([_unrope(dkp_ref[...].T, cos_ref[...], sin_ref[...]), dvp_ref[...].T],
                                           axis=1).astype(BF16)

    same = lambda i: (0, i)
    lat_map = lambda i: (jnp.maximum(i - 1, 0), 0)
    ctx_map = lambda i: (0, 0)
    return pl.pallas_call(
        kern, name=name, grid=(T // TR,),
        in_specs=[pl.BlockSpec((KW, TR), same), pl.BlockSpec((KW, TR), same),
                  pl.BlockSpec((KW, CTX), ctx_map), pl.BlockSpec((KW, CTX), ctx_map),
                  pl.BlockSpec((KW, CTX), ctx_map), pl.BlockSpec((KW, CTX), ctx_map),
                  pl.BlockSpec((TR, HEAD), lat_map), pl.BlockSpec((TR, HEAD), lat_map), ANY],
        out_specs=pl.BlockSpec((TR, 2 * KW), lambda i: (i, COL_K // (2 * KW))),
        out_shape=_sds((T, DP_W), BF16), input_output_aliases={8: 0}, compiler_params=_params(),
    )(dkp, dvp, dkc_l, dvc_l, dkc_c, dvc_c, cos, sin, dp)


RB = 128
CH = 256
HALO = 8
SUB = 8
GRP = 8


def _vscan(a, b, reverse):
    row = lax.broadcasted_iota(jnp.int32, a.shape, 0)
    A, H = a, b
    for s in (1, 2, 4):
        sh = SUB - s if reverse else s
        m = (row < SUB - s) if reverse else (row >= s)
        As = pltpu.roll(A, sh, 0)
        Hs = pltpu.roll(H, sh, 0)
        H = jnp.where(m, A * Hs + H, H)
        A = jnp.where(m, A * As, A)
    return A, H


def _scan_rows(a_ref, b_ref, r0, nrows, reverse, carry, emit):
    ngrp = nrows // (SUB * GRP)
    row = lax.broadcasted_iota(jnp.int32, (SUB, RB), 0)

    def grp(gi, carry):
        g = (ngrp - 1 - gi) if reverse else gi
        base = r0 + g * (SUB * GRP)
        for v in (range(GRP - 1, -1, -1) if reverse else range(GRP)):
            rs = pl.multiple_of(base + v * SUB, SUB)
            A, H = _vscan(a_ref[pl.ds(rs, SUB), :], b_ref[pl.ds(rs, SUB), :], reverse)
            hf = H + A * carry
            if reverse:
                before = jnp.where(row == SUB - 1, carry, pltpu.roll(hf, SUB - 1, 0))
                carry = hf[0:1, :]
            else:
                before = jnp.where(row == 0, carry, pltpu.roll(hf, 1, 0))
                carry = hf[SUB - 1:SUB, :]
            emit(rs, hf, before)
        return carry

    return lax.fori_loop(0, ngrp, grp, carry)


def _pad_start(ci):
    return pl.multiple_of(ci * CH + HALO * jnp.minimum(ci, 1), HALO)


def _conv_taps(ext, transpose=False):
    n = CH + 2 * HALO
    taps = []
    for k in range(CONV_W):
        off = CONV_LEFT - k if transpose else k - CONV_LEFT
        taps.append(ext[HALO:HALO + CH, :] if off == 0 else pltpu.roll(ext, (-off) % n, 0)[HALO:HALO + CH, :])
    return taps


def _lru_gates(xl, w4, b4, ls):
    pre = _dot(xl.astype(BF16), w4) + b4
    out = []
    for d in range(2):
        r = _sigmoid(pre[:, d * RB:(d + 1) * RB])
        i = _sigmoid(pre[:, (2 + d) * RB:(3 + d) * RB])
        la = LRU_C * r * ls[d:d + 1, :]
        a = jnp.exp(la)
        q = -jnp.tanh(la) * (1.0 + a * a)
        out.append((r, i, a, q))
    return out


def _rnn_specs(T):
    col = lambda n, *_: (0, n)
    return dict(
        xr=pl.BlockSpec((T, RB), lambda n, *_: (0, COL_XR // RB + n)),
        gr=pl.BlockSpec((T, RB), lambda n, *_: (0, COL_GR // RB + n)),
        act=pl.BlockSpec((T, RB), col),
        cw=pl.BlockSpec((CONV_W, RB), col), cb=pl.BlockSpec((1, RB), col),
        w4=pl.BlockSpec((None, RB, 4 * RB), lambda n, *_: (n, 0, 0)),
        b4=pl.BlockSpec((None, 1, 4 * RB), lambda n, *_: (n, 0, 0)),
        lam=pl.BlockSpec((2, RB), col))


PAD_ROWS = 3 * HALO


def _zero_pads(pad_ref, T):
    for r in (0, HALO + CTX, 2 * HALO + T):
        pad_ref[r:r + HALO, :] = jnp.zeros((HALO, RB), F32)


def _fill_padded(pad_ref, src_ref, T):
    _zero_pads(pad_ref, T)
    pad_ref[HALO:HALO + CTX, :] = src_ref[0:CTX, :].astype(F32)
    pad_ref[2 * HALO + CTX:2 * HALO + T, :] = src_ref[CTX:T, :].astype(F32)


def _pad_rows(ci):
    return pl.ds(pl.multiple_of(ci * CH + HALO + HALO * jnp.minimum(ci, 1), HALO), CH)


def _rnn_fwd(name, p, cw, cb, w4, b4, lam, T, carry=None):
    def kern(xr_ref, gr_ref, cw_ref, cb_ref, w4_ref, b4_ref, lam_ref,
             u_ref, a0, a1, yo_ref, hpf_ref, hpb_ref, r0_ref, r1_ref, i0_ref, i1_ref, xpad, b0, b1, y):
        _fill_padded(xpad, xr_ref, T)
        ls = _log_sigmoid(lam_ref[...])
        w4v, b4v, cwv, cbv = w4_ref[...], b4_ref[...], cw_ref[...], cb_ref[...]

        def chunk(ci, _):
            rows = pl.ds(pl.multiple_of(ci * CH, CH), CH)
            taps = _conv_taps(xpad[pl.ds(_pad_start(ci), CH + 2 * HALO), :])
            xl = cbv + sum(taps[k] * cwv[k:k + 1, :] for k in range(CONV_W))
            for d, (r, i, a, q) in enumerate(_lru_gates(xl, w4v, b4v, ls)):
                (a0, a1)[d][rows, :] = a
                (b0, b1)[d][rows, :] = jnp.sqrt(q) * (i * xl)
                (r0_ref, r1_ref)[d][rows, :] = r.astype(BF16)
                (i0_ref, i1_ref)[d][rows, :] = i.astype(BF16)
            return 0

        lax.fori_loop(0, T // CH, chunk, 0)
        zero = jnp.zeros((1, RB), F32)

        def emit_f(rs, hf, before):
            y[pl.ds(rs, SUB), :] = hf
            b0[pl.ds(rs, SUB), :] = before

        def emit_b(rs, hf, before):
            y[pl.ds(rs, SUB), :] += hf
            b1[pl.ds(rs, SUB), :] = before

        _scan_rows(a0, b0, 0, T, False, zero, emit_f)
        c = _scan_rows(a1, b1, 0, CTX, True, zero, emit_b)
        _scan_rows(a1, b1, CTX, T - CTX, True, c, emit_b)

        def finish(ci, _):
            rows = pl.ds(pl.multiple_of(ci * CH, CH), CH)
            yv = y[rows, :]
            u_ref[rows, :] = (yv * _gelu(gr_ref[rows, :].astype(F32))).astype(BF16)
            yo_ref[rows, :] = yv.astype(BF16)
            hpf_ref[rows, :] = b0[rows, :].astype(BF16)
            hpb_ref[rows, :] = b1[rows, :].astype(BF16)
            return 0

        lax.fori_loop(0, T // CH, finish, 0)

    sp = _rnn_specs(T)
    ci, ca, co, cs, cscr = _carry_args(carry)
    dts = [BF16, F32, F32] + [BF16] * 7
    res = pl.pallas_call(
        _carried(kern, carry, 7, 10, *_grid_ends((N_RNN_BLOCKS,))), name=name, grid=(N_RNN_BLOCKS,),
        in_specs=[sp["xr"], sp["gr"], sp["cw"], sp["cb"], sp["w4"], sp["b4"], sp["lam"]] + ci,
        out_specs=[sp["act"]] * 10 + co,
        out_shape=[_sds((T, D), dt) for dt in dts] + cs,
        scratch_shapes=[pltpu.VMEM((T + PAD_ROWS, RB), F32)] + [pltpu.VMEM((T, RB), F32)] * 3 + cscr,
        compiler_params=_params(),
    )(p, p, cw, cb, w4, b4, lam, *ca)
    return res if carry is None else (res[:10], res[10:])


def _rnn_bwd(name, p, du, saved, dp, cw, cb, w4, b4, lam, T, carry=None):
    def kern(xr_ref, gr_ref, du_ref, a0, a1, y_ref, hpf_ref, hpb_ref, r0_ref, r1_ref, i0_ref, i1_ref,
             cw_ref, cb_ref, w4_ref, b4_ref, lam_ref, dp_in,
             dp_ref, dcw_ref, dcb_ref, dw4_ref, db4_ref, dlam_ref,
             xpad, dxpad, c0, c1, dy):
        j = pl.program_id(1)

        @pl.when(j == 0)
        def _():
            scans(gr_ref, du_ref, a0, a1, y_ref, dp_ref, c0, c1, dy)

        @pl.when(j == 1)
        def _():
            gates(xr_ref, a0, a1, (hpf_ref, hpb_ref), (r0_ref, r1_ref), (i0_ref, i1_ref), cw_ref, cb_ref, w4_ref,
                  lam_ref, dp_ref, dcw_ref, dcb_ref, dw4_ref, db4_ref, dlam_ref, xpad, dxpad, c0, c1)

    def scans(gr_ref, du_ref, a0, a1, y_ref, dgr_ref, c0, c1, dy):
        def phase_a(ci, _):
            rows = pl.ds(pl.multiple_of(ci * CH, CH), CH)
            gr = gr_ref[rows, :].astype(F32)
            duv = du_ref[rows, :].astype(F32)
            dyv = duv * _gelu(gr)
            dgr_ref[rows, :] = (duv * y_ref[rows, :].astype(F32) * _gelu_grad(gr)).astype(BF16)
            dy[rows, :] = dyv
            c0[rows, :] = a0[rows, :] * dyv
            c1[rows, :] = a1[rows, :] * dyv
            return 0

        lax.fori_loop(0, T // CH, phase_a, 0)
        zero = jnp.zeros((1, RB), F32)

        def emit0(rs, hf, before):
            c0[pl.ds(rs, SUB), :] = dy[pl.ds(rs, SUB), :] + before

        def emit1(rs, hf, before):
            c1[pl.ds(rs, SUB), :] = dy[pl.ds(rs, SUB), :] + before

        _scan_rows(a0, c0, 0, T, True, zero, emit0)
        c = _scan_rows(a1, c1, CTX, T - CTX, False, zero, emit1)
        _scan_rows(a1, c1, 0, CTX, False, c, emit1)

    def gates(xr_ref, a0, a1, hp_refs, r_refs, i_refs, cw_ref, cb_ref, w4_ref, lam_ref,
              dxr_ref, dcw_ref, dcb_ref, dw4_ref, db4_ref, dlam_ref, xpad, dxpad, c0, c1):
        _fill_padded(xpad, xr_ref, T)
        _zero_pads(dxpad, T)
        lam_v = lam_ref[...]
        ls = _log_sigmoid(lam_v)
        w4v, cwv, cbv = w4_ref[...], cw_ref[...], cb_ref[...]

        def conv_chunk(ci):
            taps = _conv_taps(xpad[pl.ds(_pad_start(ci), CH + 2 * HALO), :])
            return taps, cbv + sum(taps[k] * cwv[k:k + 1, :] for k in range(CONV_W))

        dw4_ref[...] = jnp.zeros(dw4_ref.shape, F32)
        db4_ref[...] = jnp.zeros(db4_ref.shape, F32)
        dlam_ref[...] = jnp.zeros(dlam_ref.shape, F32)
        dcw_ref[...] = jnp.zeros(dcw_ref.shape, F32)
        dcb_ref[...] = jnp.zeros(dcb_ref.shape, F32)

        def phase_c(ci, _):
            base = pl.multiple_of(ci * CH, CH)
            rows = pl.ds(base, CH)
            _, xl = conv_chunk(ci)
            dxl = jnp.zeros((CH, RB), F32)
            dpre_a, dpre_x, dls = [], [], []
            for d in range(2):
                a = (a0, a1)[d][rows, :]
                r = r_refs[d][rows, :].astype(F32)
                i = i_refs[d][rows, :].astype(F32)
                q = -jnp.tanh(LRU_C * r * ls[d:d + 1, :]) * (1.0 + a * a)
                g = (c0, c1)[d][rows, :]
                hp = hp_refs[d][rows, :].astype(F32)
                gm = g * jnp.sqrt(q)
                di = gm * xl
                dxl = dxl + gm * i
                dla = a * (g * hp - a * (g * (i * xl)) * lax.rsqrt(q))
                dr = dla * (LRU_C * ls[d:d + 1, :])
                dls.append(_colsum(dla * (LRU_C * r)))
                dpre_a.append(dr * r * (1.0 - r))
                dpre_x.append(di * i * (1.0 - i))
            dpre = jnp.concatenate(dpre_a + dpre_x, axis=1)
            dpre_b = dpre.astype(BF16)
            dxl = dxl + _dot(dpre_b, w4v, NT)
            dw4_ref[...] += _dot(xl.astype(BF16), dpre_b, TN)
            db4_ref[...] += _colsum(dpre)
            dlam_ref[...] += jnp.concatenate(dls, axis=0)
            dcb_ref[...] += _colsum(dxl)
            dxpad[_pad_rows(ci), :] = dxl
            return 0

        lax.fori_loop(0, T // CH, phase_c, 0)
        dlam_ref[...] = dlam_ref[...] * _sigmoid(-lam_v)

        def phase_d(ci, _):
            base = pl.multiple_of(ci * CH, CH)
            rows = pl.ds(base, CH)
            xtaps, _ = conv_chunk(ci)
            dtaps = _conv_taps(dxpad[pl.ds(_pad_start(ci), CH + 2 * HALO), :], transpose=True)
            dxl = dxpad[_pad_rows(ci), :]
            dxr_ref[rows, :] = sum(dtaps[k] * cwv[k:k + 1, :] for k in range(CONV_W)).astype(BF16)
            dcw_ref[...] += jnp.concatenate([_colsum(dxl * xtaps[k]) for k in range(CONV_W)], axis=0)
            return 0

        lax.fori_loop(0, T // CH, phase_d, 0)

    sp = _rnn_specs(T)
    dp_spec = pl.BlockSpec((T, RB), lambda n, j: (0, COL_GR // RB + n - j * (COL_GR - COL_XR) // RB))
    ci, ca, co, cs, cscr = _carry_args(carry)
    n_in = 3 + len(saved) + 5 + 1
    res = pl.pallas_call(
        _carried(kern, carry, n_in, 6, *_grid_ends((N_RNN_BLOCKS, 2))), name=name, grid=(N_RNN_BLOCKS, 2),
        in_specs=[sp["xr"], sp["gr"]] + [sp["act"]] * (1 + len(saved)) + [sp["cw"], sp["cb"], sp["w4"], sp["b4"],
                                                                           sp["lam"], ANY] + ci,
        out_specs=[dp_spec, sp["cw"], sp["cb"], sp["w4"], sp["b4"], sp["lam"]] + co,
        out_shape=[_sds((T, DP_W), BF16), _sds((CONV_W, D), F32), _sds((1, D), F32),
                   _sds((N_RNN_BLOCKS, RB, 4 * RB), F32), _sds((N_RNN_BLOCKS, 1, 4 * RB), F32), _sds((2, D), F32)] + cs,
        scratch_shapes=[pltpu.VMEM((T + PAD_ROWS, RB), F32)] * 2 + [pltpu.VMEM((T, RB), F32)] * 3 + cscr,
        input_output_aliases={n_in - 1: 0},
        compiler_params=_params(("arbitrary", "arbitrary")),
    )(p, p, du, *saved, cw, cb, w4, b4, lam, dp, *ca)
    return res if carry is None else (res[:6], res[6:])


class _Plan:
    def __init__(self, shards, Ws):
        L = len(Ws)
        self.shards, self.Ws = shards, Ws
        self.Gs = [None] * L
        self.slots = [dict() for _ in range(L)]
        self.gate_slots = [None] * L
        self.table = {}
        for l in range(L):
            t = f"l{l}_"
            self.table[t + "rnn_fwd"] = [("gather", l, k) for k in ("wffn_in_t", "wo_rnn", "wo_attn", "wout")]
            if l + 1 < L:
                self.table[t + "attn_lat_fwd"] = [("gather", l + 1, "win_t")]
                self.table[t + "ffn_in"] = [("gather", l, "wffn_out")]
            else:
                self.table[t + "attn_lat_fwd"] = [("gather", l, "wffn_out")]
            self.table[t + "ffn_in_dx"] = [("scatter", l, "wffn_out")]
            self.table[t + "attn_lat_bwd"] = [("scatter", l, "wffn_in_t")]
            self.table[t + "proj_dx"] = [("scatter", l, "win_t_a")]
            self.table[t + "rnn_bwd"] = ([("scatter", l, k) for k in ("wout", "wo_attn", "wo_rnn")]
                                         + ([("scatter", l + 1, "win_t_b"), ("gates", l + 1, "w4")] if l + 1 < L else []))
        self.table["l0_proj_dw_b"] = [("gates", 0, "w4")]

    def carry(self, name):
        jobs = []
        for kind, l, k in self.table.get(name, []):
            if kind == "gather":
                jobs.append(("gather", self.shards[l][k]))
            elif kind == "scatter":
                jobs.append(("scatter", self.Gs[l][k].reshape(N_DEV, -1, self.Gs[l][k].shape[-1])))
            else:
                jobs.append(("gather", self.Gs[l]["w4"].reshape(N_RNN_BLOCKS * RB, 4 * RB).astype(BF16)))
        return _Carry(jobs) if jobs else None

    def done(self, name, got):
        for (kind, l, k), res in zip(self.table[name], got):
            if kind == "gather":
                self.Ws[l][k] = res.reshape(-1, D)
            elif kind == "scatter":
                self.slots[l][k] = res
            else:
                self.gate_slots[l] = res


def _run(X, fn, name, *args, **kw):
    carry = None if X is None else X.carry(name)
    if carry is None:
        return fn(name, *args, **kw)
    out, got = fn(name, *args, carry=carry, **kw)
    X.done(name, got)
    return out


def _layer_fwd(l, xa, h, W, rope, S, nxt, X=None):
    T = xa.shape[0]
    tag = f"l{l}_"
    cos, sin, bias = rope
    p = _run(X, _mm_act, tag + "proj", h, W["win_t"], "NT", BF16)
    u, *rnn_saved = _run(X, _rnn_fwd, tag + "rnn_fwd", p, W["cw"], W["cb"], W["w4"], W["b4"], W["lam"], T)
    kp, vp, kc, vc = _kv_prep(tag + "kv_prep", p, cos, sin, S)
    o_all = _attn_fwd(tag + "attn_ctx_fwd", p, kc, vc, W["sink4"], S)
    o_all = _run(X, _attn_fwd, tag + "attn_lat_fwd", p, kc, vc, W["sink4"], S, band=(kp, vp, bias, cos, sin),
                 prev=o_all)
    ya, yb, z, m, x1, h2 = _out_fused(tag + "out", p, u, o_all, xa, W["wo_rnn"], W["wo_attn"], W["wout"],
                                      W["g_mix_post"], W["mod"], W["g_ffn_pre"])
    fg, fu, s = _run(X, _ffn_in_fused, tag + "ffn_in", h2, W["wffn_in_t"])
    e, *out = _ffn_out_fused(tag + "ffn_out", s, W["wffn_out"], x1, W["g_ffn_post"], W["mod"], nxt)
    saved = dict(xa=xa, h=h, p=p, u=u, rnn=rnn_saved, kp=kp, vp=vp, kc=kc, vc=vc, o_all=o_all,
                 ya=ya, yb=yb, z=z, m=m, x1=x1, h2=h2, fg=fg, fu=fu, s=s, e=e)
    return saved, out


def _layer_bwd(l, dx2, A, W, rope, S, X=None, loss_of=None):
    T = A["xa"].shape[0]
    tag = f"l{l}_"
    cos, sin, bias = rope
    G = {}
    if X is not None:
        X.Gs[l] = G
    if loss_of is None:
        de, df, dga2, G["g_ffn_post"] = _ffn_bwd_fused(tag + "ffn_bwd", A["fg"], A["fu"], W["wffn_out"],
                                                       head=(dx2, A["e"], W["g_ffn_post"], W["mod"]))
    else:
        dx2, de, dga2, G["g_ffn_post"], G["sq"] = _loss_resid_bwd(tag + "loss_ffn_resid_bwd", *loss_of, A["e"],
                                                                  W["g_ffn_post"], W["mod"], GA2)
        df, = _ffn_bwd_fused(tag + "ffn_bwd", A["fg"], A["fu"], W["wffn_out"], de=de)
    G["wffn_out"] = _mm_wgrad(tag + "ffn_out_dw", A["s"], de)
    dx1, dm, dsh2, dsc2, G["g_ffn_pre"], dga1, G["g_mix_post"] = _run(
        X, _ffn_in_bwd_fused, tag + "ffn_in_dx", df, W["wffn_in_t"], A["x1"], dx2, A["m"], W["g_ffn_pre"], W["mod"],
        W["g_mix_post"])
    G["wffn_in_t"] = _run(X, _mm_wgrad, tag + "ffn_in_dw", df, A["h2"])
    G["wout"] = _mm_wgrad(tag + "out_dw", A["z"], dm)
    dya, dyb, dgl, du, do = _out_bwd_fused(tag + "out_dx", dm, W["wout"], W["wo_rnn"], W["wo_attn"], A["p"], A["ya"],
                                           A["yb"])
    G["wo_attn"] = _mm_wgrad(tag + "o_attn_dw", A["o_all"], dyb)
    G["wo_rnn"] = _mm_wgrad(tag + "o_rnn_dw", A["u"], dya)
    dp, dkc_c, dvc_c, dsink_c = _attn_bwd(tag + "attn_ctx_bwd", A["p"], A["kc"], A["vc"], W["sink4"], A["o_all"], do, S)
    dp, dkc_l, dvc_l, dsink_l, dkp, dvp = _run(
        X, _attn_bwd, tag + "attn_lat_bwd", A["p"], A["kc"], A["vc"], W["sink4"], A["o_all"], do, S,
        band=(A["kp"], A["vp"], bias, cos, sin), prev_dq=dp)
    G["sink4"] = dsink_c + dsink_l
    dp = _dkv_assemble(tag + "dkv", dp, dkp, dvp, dkc_l, dvc_l, dkc_c, dvc_c, cos, sin, S)
    dp, G["cw"], G["cb"], G["w4"], G["b4"], G["lam"] = _run(
        X, _rnn_bwd, tag + "rnn_bwd", A["p"], du, A["rnn"], dp, W["cw"], W["cb"], W["w4"], W["b4"], W["lam"], T)
    proj_dx = (_proj_bwd_fused, tag + "proj_dx", dp, dgl, W["win_t"], A["xa"], dx1, W["g_mix_pre"], W["mod"])
    if X is not None:
        G["win_t_a"] = _proj_wgrad(tag + "proj_dw_a", dp, dgl, A["h"][:, :D // 2])
        dxa, dsh1, dsc1, G["g_mix_pre"] = _run(X, *proj_dx)
        G["win_t_b"] = _run(X, _proj_wgrad, tag + "proj_dw_b", dp, dgl, A["h"][:, D // 2:])
    else:
        dxa, dsh1, dsc1, G["g_mix_pre"] = _run(X, *proj_dx)
        G["win_t"] = _proj_wgrad(tag + "proj_dw", dp, dgl, A["h"])
    G["mod"] = jnp.concatenate([dsh1, dsc1, dga1, dsh2, dsc2, dga2], axis=1)
    return dxa, G


def _local_step(xa, target, Ws, S, X=None):
    rope = (*_rope_tables(S), _band_bias(S))
    L = len(Ws)
    h = _normmod_fwd("l0_mix_norm", xa, Ws[0]["g_mix_pre"], Ws[0]["mod"], SH1, SC1)
    saved = []
    x = xa
    for l in range(L):
        nxt = (Ws[l + 1]["g_mix_pre"], Ws[l + 1]["mod"]) if l + 1 < L else None
        A, out = _layer_fwd(l, x, h, Ws[l], rope, S, nxt, X)
        saved.append(A)
        if l + 1 < L:
            x, h = out
    Gs = [None] * L
    dx = None
    for l in reversed(range(L)):
        dx, Gs[l] = _layer_bwd(l, dx, saved[l], Ws[l], rope, S, X, loss_of=(out[0], target) if l == L - 1 else None)
    return Gs[L - 1]["sq"], dx, Gs


MESH = pl.DeviceIdType.MESH


def _place():
    return lax.axis_index("x"), lax.axis_index("y"), lax.axis_index("c")


def _lin(px, py, pc):
    return 4 * px + 2 * py + pc


def _allgather_small(name, blk):
    m, n = blk.shape

    def body(x_ref, out_ref, send_sems, recv_sems, local_sem):
        x, y, c = _place()
        me, sibling = (x, y, c), (x, y, 1 - c)
        chips = [(1 - x, y), (x, 1 - y), (1 - x, 1 - y)]

        def copy(k, block, to, src=None):
            dst = out_ref.at[_lin(*block)]
            return pltpu.make_async_remote_copy(src_ref=dst if src is None else src, dst_ref=dst,
                                                send_sem=send_sems.at[k], recv_sem=recv_sems.at[k],
                                                device_id=to, device_id_type=MESH)

        mine = pltpu.make_async_copy(x_ref, out_ref.at[_lin(*me)], local_sem)
        mine.start()
        first = [copy(0, me, sibling, src=x_ref)]
        first += [copy(1 + j, me, (*chip, c), src=x_ref) for j, chip in enumerate(chips)]
        for cp in first:
            cp.start()
        passed = [copy(4 + j, (*chip, c), sibling) for j, chip in enumerate(chips)]
        for j, chip in enumerate(chips):
            copy(1 + j, (*chip, c), me).wait_recv()
            passed[j].start()
        copy(0, sibling, me).wait_recv()
        for j, chip in enumerate(chips):
            copy(4 + j, (*chip, 1 - c), me).wait_recv()
        for cp in first + passed:
            cp.wait_send()
        mine.wait()

    return pl.pallas_call(
        body, name=name, out_shape=_sds((N_DEV, m, n), blk.dtype),
        in_specs=[pl.BlockSpec(memory_space=pltpu.VMEM)], out_specs=pl.BlockSpec(memory_space=pltpu.VMEM),
        scratch_shapes=[pltpu.SemaphoreType.DMA((7,)), pltpu.SemaphoreType.DMA((7,)), pltpu.SemaphoreType.DMA],
        compiler_params=pltpu.CompilerParams(vmem_limit_bytes=VMEM_LIMIT),
    )(blk)


def _allgather_hbm(name, shards):
    na = len(shards)

    def body(*refs):
        ins, outs = refs[:na], refs[na:2 * na]
        send_sems, recv_sems, local_sems = refs[2 * na:]
        x, y, c = _place()
        me, sibling = (x, y, c), (x, y, 1 - c)
        chips = [(1 - x, y), (x, 1 - y), (1 - x, 1 - y)]

        def copy(a, k, block, to, from_input=False):
            dst = outs[a].at[_lin(*block)]
            return pltpu.make_async_remote_copy(src_ref=ins[a] if from_input else dst, dst_ref=dst,
                                                send_sem=send_sems.at[a, k], recv_sem=recv_sems.at[a, k],
                                                device_id=to, device_id_type=MESH)

        mine = [pltpu.make_async_copy(ins[a], outs[a].at[_lin(*me)], local_sems.at[a]) for a in range(na)]
        for cp in mine:
            cp.start()
        first = []
        for a in range(na):
            first.append(copy(a, 0, me, sibling, True))
            first += [copy(a, 1 + j, me, (*chip, c), True) for j, chip in enumerate(chips)]
        for cp in first:
            cp.start()
        passed = []
        for j, chip in enumerate(chips):
            for a in range(na):
                copy(a, 1 + j, (*chip, c), me).wait_recv()
                fwd = copy(a, 4 + j, (*chip, c), sibling)
                fwd.start()
                passed.append(fwd)
        for a in range(na):
            copy(a, 0, sibling, me).wait_recv()
            for j, chip in enumerate(chips):
                copy(a, 4 + j, (*chip, 1 - c), me).wait_recv()
        for cp in first + passed:
            cp.wait_send()
        for cp in mine:
            cp.wait()

    return pl.pallas_call(
        body, name=name, out_shape=[_sds((N_DEV, *s.shape), s.dtype) for s in shards],
        in_specs=[ANY] * na, out_specs=[ANY] * na,
        scratch_shapes=[pltpu.SemaphoreType.DMA((na, 7)), pltpu.SemaphoreType.DMA((na, 7)),
                        pltpu.SemaphoreType.DMA((na,))],
    )(*shards)


def _exchange_shards(name, grads, L):
    nw = len(grads)
    na = nw * L
    flat = [g for per_layer in grads for g in per_layer]

    def body(*refs):
        ins, outs = refs[:na], refs[na:na + nw]
        send_sems, recv_sems, local_sems = refs[na + nw:]
        x, y, c = _place()
        me = _lin(x, y, c)
        peers = [(x ^ ((k + 1) >> 2 & 1), y ^ ((k + 1) >> 1 & 1), c ^ ((k + 1) & 1)) for k in range(7)]

        def copy(a, k, src_blk, dst_blk):
            return pltpu.make_async_remote_copy(src_ref=ins[a].at[src_blk], dst_ref=outs[a // L].at[a % L, dst_blk],
                                                send_sem=send_sems.at[a, k], recv_sem=recv_sems.at[a, k],
                                                device_id=peers[k], device_id_type=MESH)

        mine = [pltpu.make_async_copy(ins[a].at[me], outs[a // L].at[a % L, me], local_sems.at[a]) for a in range(na)]
        for cp in mine:
            cp.start()
        sent = [copy(a, k, _lin(*peers[k]), me) for a in range(na) for k in range(7)]
        for cp in sent:
            cp.start()
        for a in range(na):
            for k in range(7):
                copy(a, k, me, _lin(*peers[k])).wait_recv()
        for cp in sent:
            cp.wait_send()
        for cp in mine:
            cp.wait()

    return pl.pallas_call(
        body, name=name, out_shape=[_sds((L, *per_layer[0].shape), per_layer[0].dtype) for per_layer in grads],
        in_specs=[ANY] * na, out_specs=[ANY] * nw,
        scratch_shapes=[pltpu.SemaphoreType.DMA((na, 7)), pltpu.SemaphoreType.DMA((na, 7)),
                        pltpu.SemaphoreType.DMA((na,))],
    )(*flat)


MOD_ROWS = 16
MOD_SHARD = 6 * D // N_DEV
HI = lax.Precision.HIGHEST


def _mod_fwd(name, c9, w_mod, b_shard):
    L = w_mod.shape[0]

    def kern(c_ref, w_ref, b_ref, o_ref):
        o_ref[...] = lax.dot_general(_silu(c_ref[...]), w_ref[...], NN, precision=HI,
                                     preferred_element_type=F32) + b_ref[...]

    return pl.pallas_call(
        kern, name=name, grid=(L,),
        in_specs=[_full_spec(c9.shape), pl.BlockSpec((None, D, MOD_SHARD), lambda l: (l, 0, 0)),
                  pl.BlockSpec((None, 1, MOD_SHARD), lambda l: (l, 0, 0))],
        out_specs=pl.BlockSpec((None, MOD_ROWS, MOD_SHARD), lambda l: (l, 0, 0)),
        out_shape=_sds((L, MOD_ROWS, MOD_SHARD), F32), compiler_params=_params(),
    )(c9, w_mod, b_shard)


def _mod_bwd(name, c9, w_mod, dmod_all, dmod_cols):
    L = w_mod.shape[0]

    def rows9(ref, l):
        own = jnp.concatenate([ref[j, 2 * l + 1:2 * l + 2, :] for j in range(N_DEV)], axis=0)
        ctx = ref[0, 2 * l:2 * l + 1, :]
        for j in range(1, N_DEV):
            ctx = ctx + ref[j, 2 * l:2 * l + 1, :]
        return own, ctx

    def kern(c_ref, w_ref, all_ref, cols_ref, gw_ref, gb_ref, gc_ref):
        l = pl.program_id(0)
        for ll in range(L):
            @pl.when(l == ll)
            def _():
                own, ctx = rows9(all_ref, ll)
                gb_ref[...] = _colsum(own) + ctx
                own_s, ctx_s = rows9(cols_ref, ll)
                r16 = jnp.concatenate([own_s, ctx_s, jnp.zeros((MOD_ROWS - N_DEV - 1, MOD_SHARD), F32)], axis=0)
                gw_ref[...] = lax.dot_general(_silu(c_ref[...]), r16, TN, precision=HI, preferred_element_type=F32)
                part = lax.dot_general(r16, w_ref[...], NT, precision=HI,
                                       preferred_element_type=F32)[N_DEV:N_DEV + 1, :]
                if ll == 0:
                    gc_ref[...] = part
                else:
                    gc_ref[...] += part

    return pl.pallas_call(
        kern, name=name, grid=(L,),
        in_specs=[_full_spec(c9.shape), pl.BlockSpec((None, D, MOD_SHARD), lambda l: (l, 0, 0)),
                  _full_spec(dmod_all.shape), _full_spec(dmod_cols.shape)],
        out_specs=[pl.BlockSpec((None, D, MOD_SHARD), lambda l: (l, 0, 0)),
                   pl.BlockSpec((None, 1, 6 * D), lambda l: (l, 0, 0)), _full_spec((1, D))],
        out_shape=[_sds((L, D, MOD_SHARD), F32), _sds((L, 1, 6 * D), F32), _sds((1, D), F32)],
        compiler_params=_params(),
    )(c9, w_mod, dmod_all, dmod_cols)


_BC1 = 1.0 - ADAM_B1 ** ADAM_STEP
_BC2 = 1.0 - ADAM_B2 ** ADAM_STEP


def _adamw_vals(w, g, m, v):
    m = ADAM_B1 * m + (1.0 - ADAM_B1) * g
    v = ADAM_B2 * v + (1.0 - ADAM_B2) * (g * g)
    delta = -ADAM_LR * ((m / _BC1) / (jnp.sqrt(v / _BC2) + ADAM_EPS) + ADAM_WD * w)
    return delta, m, v


def _adamw(name, w, g, m, v, tile):
    R, C = w.shape
    blk = ((tile, C), lambda i: (i, 0))

    def body(i, ins, ps, outs, acc):
        d, mm, vv = _adamw_vals(ins[0][...], ins[1][...], ins[2][...], ins[3][...])
        outs[0][...] = d
        outs[1][...] = mm
        outs[2][...] = vv

    return _ew(name, body, R // tile, [(a, *blk) for a in (w, g, m, v)], [], [(_sds((R, C), F32), *blk)] * 3)


def _sum_slots(ref):
    g = ref[0].astype(F32)
    for j in range(1, N_DEV):
        g = g + ref[j].astype(F32)
    return g


def _adamw_slots(name, slots, shape, tile, wmv=None):
    L, R, C = shape
    n = R // tile
    spec = pl.BlockSpec((None, tile, C), lambda l, i: (l, i, 0))
    pieces = [s if isinstance(s, (list, tuple)) else [s] for s in slots]
    layer_of = [ll for ll, ps in enumerate(pieces) for _ in ps]
    flat = [p for ps in pieces for p in ps]
    wmv = list(wmv or [])

    def slot_spec(ll, cols):
        return pl.BlockSpec((N_DEV, tile, cols),
                            lambda l, i: (0, jnp.where(l == ll, i, jnp.where(l < ll, 0, n - 1)), 0))

    def kern(*refs):
        s_refs = refs[:len(flat)]
        rest = refs[len(flat):]
        l = pl.program_id(0)
        for ll in range(L):
            @pl.when(l == ll)
            def _():
                parts = [_sum_slots(r) for r, lr in zip(s_refs, layer_of) if lr == ll]
                g = parts[0] if len(parts) == 1 else jnp.concatenate(parts, axis=1)
                if wmv:
                    w_ref, m_ref, v_ref, g_ref, d_ref, mo_ref, vo_ref = rest
                    d_ref[...], mo_ref[...], vo_ref[...] = _adamw_vals(w_ref[...], g, m_ref[...], v_ref[...])
                else:
                    g_ref, = rest
                g_ref[...] = g

    n_out = 4 if wmv else 1
    return pl.pallas_call(
        kern, name=name, grid=(L, n),
        in_specs=[slot_spec(ll, p.shape[-1]) for ll, p in zip(layer_of, flat)] + [spec] * len(wmv),
        out_specs=[spec] * n_out, out_shape=[_sds((L, R, C), F32)] * n_out,
        compiler_params=_params(("arbitrary", "arbitrary")),
    )(*flat, *wmv)


def _sum_blocks(name, blocks):
    _, R, C = blocks.shape

    def kern(b_ref, o_ref):
        o_ref[...] = _sum_slots(b_ref)

    return pl.pallas_call(kern, name=name, in_specs=[_full_spec(blocks.shape)], out_specs=_full_spec((R, C)),
                          grid=(1,), out_shape=_sds((R, C), F32), compiler_params=_params())(blocks)


BIG = ("win_t", "wo_rnn", "wo_attn", "wout", "wffn_in_t", "wffn_out")
BIG_SRC = ("w_in", "w_o_rnn", "w_o_attn", "w_out", "w_ffn_in", "w_ffn_out")
BIG_T = (True, False, False, False, True, False)
BIG_TILE = (176, 128, 128, 128, 176, 176)


def _chan_full(g8):
    return jnp.transpose(g8, (1, 0, 2)).reshape(g8.shape[1], D)


def kernel(x, c, ctx, c_ctx, w_mod, b_mod, g_mix_pre, g_mix_post, g_ffn_pre, g_ffn_post, w_in, conv_w, conv_b, lru_wa, lru_ba, lru_wx, lru_bx, lru_lam, attn_sink, w_o_rnn, w_o_attn, w_out, w_ffn_in, w_ffn_out, loss_target, m_c_ctx, m_w_mod, m_b_mod, m_g_mix_pre, m_g_mix_post, m_g_ffn_pre, m_g_ffn_post, m_w_in, m_conv_w, m_conv_b, m_lru_wa, m_lru_ba, m_lru_wx, m_lru_bx, m_lru_lam, m_attn_sink, m_w_o_rnn, m_w_o_attn, m_w_out, m_w_ffn_in, m_w_ffn_out, v_c_ctx, v_w_mod, v_b_mod, v_g_mix_pre, v_g_mix_post, v_g_ffn_pre, v_g_ffn_post, v_w_in, v_conv_w, v_conv_b, v_lru_wa, v_lru_ba, v_lru_wx, v_lru_bx, v_lru_lam, v_attn_sink, v_w_o_rnn, v_w_o_attn, v_w_out, v_w_ffn_in, v_w_ffn_out):
    P = dict(c_ctx=c_ctx, w_mod=w_mod, b_mod=b_mod, g_mix_pre=g_mix_pre, g_mix_post=g_mix_post, g_ffn_pre=g_ffn_pre,
             g_ffn_post=g_ffn_post, w_in=w_in, conv_w=conv_w, conv_b=conv_b, lru_wa=lru_wa, lru_ba=lru_ba,
             lru_wx=lru_wx, lru_bx=lru_bx, lru_lam=lru_lam, attn_sink=attn_sink, w_o_rnn=w_o_rnn, w_o_attn=w_o_attn,
             w_out=w_out, w_ffn_in=w_ffn_in, w_ffn_out=w_ffn_out)
    Mo = dict(c_ctx=m_c_ctx, w_mod=m_w_mod, b_mod=m_b_mod, g_mix_pre=m_g_mix_pre, g_mix_post=m_g_mix_post,
              g_ffn_pre=m_g_ffn_pre, g_ffn_post=m_g_ffn_post, w_in=m_w_in, conv_w=m_conv_w, conv_b=m_conv_b,
              lru_wa=m_lru_wa, lru_ba=m_lru_ba, lru_wx=m_lru_wx, lru_bx=m_lru_bx, lru_lam=m_lru_lam,
              attn_sink=m_attn_sink, w_o_rnn=m_w_o_rnn, w_o_attn=m_w_o_attn, w_out=m_w_out, w_ffn_in=m_w_ffn_in,
              w_ffn_out=m_w_ffn_out)
    Vo = dict(c_ctx=v_c_ctx, w_mod=v_w_mod, b_mod=v_b_mod, g_mix_pre=v_g_mix_pre, g_mix_post=v_g_mix_post,
              g_ffn_pre=v_g_ffn_pre, g_ffn_post=v_g_ffn_post, w_in=v_w_in, conv_w=v_conv_w, conv_b=v_conv_b,
              lru_wa=v_lru_wa, lru_ba=v_lru_ba, lru_wx=v_lru_wx, lru_bx=v_lru_bx, lru_lam=v_lru_lam,
              attn_sink=v_attn_sink, w_o_rnn=v_w_o_rnn, w_o_attn=v_w_o_attn, w_out=v_w_out, w_ffn_in=v_w_ffn_in,
              w_ffn_out=v_w_ffn_out)
    L = w_in.shape[0]
    S = x.shape[1]
    me = _lin(*_place())

    small = jnp.concatenate([c.reshape(8, 128), conv_w.reshape(L * CONV_W, 128), lru_ba.reshape(2 * L, 128),
                             lru_bx.reshape(2 * L, 128), lru_lam.reshape(2 * L, 128), jnp.zeros((4, 128), F32)], axis=0)
    small_all = _allgather_small("ag_small", small)
    c_all = small_all[:, 0:8].reshape(N_DEV, D)
    conv_w_f = _chan_full(small_all[:, 8:16]).reshape(L, CONV_W, D)
    lru_ba_f = _chan_full(small_all[:, 16:20]).reshape(L, 2, D)
    lru_bx_f = _chan_full(small_all[:, 20:24]).reshape(L, 2, D)
    lru_lam_f = _chan_full(small_all[:, 24:28]).reshape(L, 2, D)

    c9 = jnp.concatenate([c_all, c_ctx[None], jnp.zeros((MOD_ROWS - N_DEV - 1, D), F32)], axis=0)
    b_shard = lax.dynamic_slice_in_dim(b_mod, me * MOD_SHARD, MOD_SHARD, axis=1)[:, None, :]
    mod_part = _mod_fwd("mod_fwd", c9, w_mod, b_shard)
    mod_all = _allgather_small("ag_mod", mod_part.reshape(L * MOD_ROWS, MOD_SHARD))
    mod_all = jnp.transpose(mod_all.reshape(N_DEV, L, MOD_ROWS, MOD_SHARD), (1, 2, 0, 3)).reshape(L, MOD_ROWS, 6 * D)
    own_row = lax.dynamic_index_in_dim(mod_all, me, axis=1, keepdims=False)
    modrows = jnp.stack([mod_all[:, N_DEV], own_row], axis=1)

    shards = [{k: (P[src][l].T if tr else P[src][l]).astype(BF16) for k, src, tr in zip(BIG, BIG_SRC, BIG_T)}
              for l in range(L)]
    win0, = _allgather_hbm("ag_w_in0", [shards[0]["win_t"]])
    Ws = []
    for l in range(L):
        W = {"win_t": win0.reshape(-1, D)} if l == 0 else {}
        W.update(
            cw=conv_w_f[l], cb=conv_b[l][None],
            w4=jnp.concatenate([lru_wa[l, 0], lru_wa[l, 1], lru_wx[l, 0], lru_wx[l, 1]], axis=-1).astype(BF16),
            b4=jnp.concatenate([lru_ba_f[l, 0].reshape(N_RNN_BLOCKS, 1, RB), lru_ba_f[l, 1].reshape(N_RNN_BLOCKS, 1, RB),
                                lru_bx_f[l, 0].reshape(N_RNN_BLOCKS, 1, RB), lru_bx_f[l, 1].reshape(N_RNN_BLOCKS, 1, RB)],
                               axis=-1),
            lam=lru_lam_f[l], sink4=jnp.broadcast_to(attn_sink[l].reshape(N_KV, Q_PER_KV, 1), (N_KV, Q_PER_KV, HEAD)),
            g_mix_pre=g_mix_pre[l][None], g_mix_post=g_mix_post[l][None], g_ffn_pre=g_ffn_pre[l][None],
            g_ffn_post=g_ffn_post[l][None], mod=modrows[l])
        Ws.append(W)

    xa = jnp.concatenate([ctx[0], x[0]], axis=0)
    plan = _Plan(shards, Ws)
    sq, dxa, Gs = _local_step(xa, loss_target[0], Ws, S, plan)
    loss_part = ((0.5 / D) * jnp.sum(sq)).reshape(1, 1)
    grad_x = dxa[CTX:][None]

    dmod = jnp.concatenate([Gs[l]["mod"] for l in range(L)] + [jnp.zeros((8 - 2 * L, 6 * D), F32)], axis=0)
    dmod_all = _allgather_small("ag_dmod", dmod)
    dmod_cols = lax.dynamic_slice_in_dim(dmod_all, me * MOD_SHARD, MOD_SHARD, axis=2)
    g_w_mod, g_b_mod, dsc_part = _mod_bwd("mod_bwd", c9, w_mod, dmod_all, dmod_cols)
    g_b_mod = g_b_mod[:, 0]

    def rows(name, shape):
        return jnp.concatenate([Gs[l][name].reshape(shape) for l in range(L)], axis=0)

    b4g = [Gs[l]["b4"].reshape(N_RNN_BLOCKS, 4, RB) for l in range(L)]
    sink_row = jnp.concatenate([Gs[l]["sink4"][:, :, 0].reshape(1, N_Q) for l in range(L)]
                               + [loss_part, jnp.zeros((1, D - L * N_Q - 1), F32)], axis=1)
    small_g = jnp.concatenate(
        [rows("g_mix_pre", (1, D)), rows("g_mix_post", (1, D)), rows("g_ffn_pre", (1, D)), rows("g_ffn_post", (1, D)),
         rows("cb", (1, D)), rows("cw", (CONV_W, D))]
        + [b4g[l][:, d].reshape(1, D) for l in range(L) for d in range(2)]
        + [b4g[l][:, 2 + d].reshape(1, D) for l in range(L) for d in range(2)]
        + [rows("lam", (2, D)), sink_row, dsc_part], axis=0)
    n_small = small_g.shape[0]
    small_tot = _sum_blocks("sum_small", _allgather_small("ag_small_grads", small_g))
    o = 0
    G = {}
    for name in ("g_mix_pre", "g_mix_post", "g_ffn_pre", "g_ffn_post", "conv_b"):
        G[name] = small_tot[o:o + L]
        o += L
    G["conv_w"] = small_tot[o:o + L * CONV_W].reshape(L, CONV_W, D)
    o += L * CONV_W
    for name in ("lru_ba", "lru_bx", "lru_lam"):
        G[name] = small_tot[o:o + 2 * L].reshape(L, 2, D)
        o += 2 * L
    G["attn_sink"] = small_tot[o, :L * N_Q].reshape(L, N_Q)
    loss = small_tot[o, L * N_Q]
    sg = jax.nn.sigmoid(c_ctx)
    G["c_ctx"] = small_tot[o + 1] * (sg * (1.0 + c_ctx * (1.0 - sg)))
    G["b_mod"] = g_b_mod
    G["w_mod"] = g_w_mod

    last_slots, = _exchange_shards("exchange_w_in0", [[Gs[0]["win_t_b"].reshape(N_DEV, -1, D // 2)]], 1)
    plan.slots[0]["win_t_b"] = last_slots[0]
    for l in range(L):
        plan.slots[l]["win_t"] = [plan.slots[l]["win_t_a"], plan.slots[l]["win_t_b"]]

    out_g, out_d, out_m, out_v = {}, {}, {}, {}

    def put(name, res, shape=None):
        g, d, m, v = res
        for dst, val in ((out_g, g), (out_d, d), (out_m, m), (out_v, v)):
            dst[name] = val if shape is None else val.reshape(shape)

    for k, src, tr, tile in zip(BIG, BIG_SRC, BIG_T, BIG_TILE):
        lay = (lambda a: jnp.swapaxes(a, 1, 2)) if tr else (lambda a: a)
        wmv = (lay(P[src]), lay(Mo[src]), lay(Vo[src]))
        res = _adamw_slots("adamw_" + src, [plan.slots[l][k] for l in range(L)], wmv[0].shape, tile, wmv)
        put(src, [lay(r) for r in res])
    res = _adamw("adamw_w_mod", w_mod.reshape(L * D, MOD_SHARD), g_w_mod.reshape(L * D, MOD_SHARD),
                 m_w_mod.reshape(L * D, MOD_SHARD), v_w_mod.reshape(L * D, MOD_SHARD), 256)
    put("w_mod", (g_w_mod,) + tuple(res), w_mod.shape)
    def fuse4(wa, wx):
        return jnp.concatenate([wa[:, 0], wa[:, 1], wx[:, 0], wx[:, 1]], axis=-1).reshape(L, N_RNN_BLOCKS * RB, 4 * RB)

    res = _adamw_slots("adamw_gates", plan.gate_slots, (L, N_RNN_BLOCKS * RB, 4 * RB), 256,
                       (fuse4(lru_wa, lru_wx), fuse4(m_lru_wa, m_lru_wx), fuse4(v_lru_wa, v_lru_wx)))
    res = [r.reshape(L, N_RNN_BLOCKS, RB, 4, RB) for r in res]
    put("lru_wa", [jnp.stack([r[:, :, :, 0], r[:, :, :, 1]], axis=1) for r in res])
    put("lru_wx", [jnp.stack([r[:, :, :, 2], r[:, :, :, 3]], axis=1) for r in res])
    rep = ("g_mix_pre", "g_mix_post", "g_ffn_pre", "g_ffn_post", "conv_b", "b_mod")

    def pack_rep(T_):
        sink = jnp.concatenate([T_["attn_sink"].reshape(1, L * N_Q), jnp.zeros((1, D - L * N_Q), F32)], axis=1)
        return jnp.concatenate([T_[n].reshape(-1, D) for n in rep] + [sink, T_["c_ctx"][None]], axis=0)

    pk = [pack_rep(T_) for T_ in (P, G, Mo, Vo)]
    n_rep = pk[0].shape[0]
    res = _adamw("adamw_replicated", *[jnp.pad(a, ((0, 24 - n_rep), (0, 0))) for a in pk], 24)
    res = (pk[1],) + tuple(r[:n_rep] for r in res)
    o = 0
    for n in rep:
        k = P[n].size // D
        put(n, [r[o:o + k] for r in res], P[n].shape)
        o += k
    put("attn_sink", [r[o, :L * N_Q] for r in res], attn_sink.shape)
    put("c_ctx", [r[o + 1] for r in res], c_ctx.shape)
    chan = ("conv_w", "lru_ba", "lru_bx", "lru_lam")
    g_own = {n: lax.dynamic_slice_in_dim(G[n], me * RB, RB, axis=2) for n in chan}

    def pack_chan(T_):
        return jnp.concatenate([T_[n].reshape(-1, RB) for n in chan], axis=0)

    pk = [pack_chan(T_) for T_ in (P, g_own, Mo, Vo)]
    n_ch = pk[0].shape[0]
    res = _adamw("adamw_channels", *[jnp.pad(a, ((0, 24 - n_ch), (0, 0))) for a in pk], 24)
    res = (pk[1],) + tuple(r[:n_ch] for r in res)
    o = 0
    for n in chan:
        k = P[n].size // RB
        put(n, [r[o:o + k] for r in res], P[n].shape)
        o += k

    order = ("c_ctx", "w_mod", "b_mod", "g_mix_pre", "g_mix_post", "g_ffn_pre", "g_ffn_post", "w_in", "conv_w", "conv_b",
             "lru_wa", "lru_ba", "lru_wx", "lru_bx", "lru_lam", "attn_sink", "w_o_rnn", "w_o_attn", "w_out", "w_ffn_in",
             "w_ffn_out")
    return (loss, grad_x, *[out_g[n] for n in order], *[out_d[n] for n in order], *[out_m[n] for n in order],
            *[out_v[n] for n in order])
```

```python
import functools
import math

import numpy as np
import jax
import jax.numpy as jnp
from jax import lax
from jax.experimental import pallas as pl
from jax.experimental.pallas import tpu as pltpu

F32 = jnp.float32
BF16 = jnp.bfloat16

D = 1024
CTX = 256
TR = 256
HEAD = 128
N_Q = 8
N_KV = 2
Q_PER_KV = N_Q // N_KV
GRID_W = 64
N_FREQ = HEAD // 4
ROPE_BASE = 10000.0
N_RNN_BLOCKS = 8
CONV_W = 4
CONV_LEFT = 2
LRU_C = 8.0
D_FF = 2816
IN_W = 5632
P_W = IN_W
DP_W = 3584
COL_XR, COL_GR, COL_Q, COL_K, COL_V, COL_GL = 0, 1024, 2048, 3072, 3328, 3584
GLB = 512
EPS = 1e-6
NEG_INF = -1e30
ATT_SCALE = HEAD ** -0.5
N_DEV = 8
VMEM_LIMIT = 56 * 1024 * 1024

ADAM_LR, ADAM_B1, ADAM_B2, ADAM_EPS, ADAM_WD, ADAM_STEP = 0.001, 0.9, 0.999, 1e-08, 0.01, 10

NN = (((1,), (0,)), ((), ()))
NT = (((1,), (1,)), ((), ()))
TN = (((0,), (0,)), ((), ()))


def _dot(a, b, dims=NN):
    return lax.dot_general(a, b, dims, preferred_element_type=F32)


def _params(sem=("arbitrary",)):
    return pltpu.CompilerParams(dimension_semantics=sem, vmem_limit_bytes=VMEM_LIMIT)


def _full_spec(shape):
    nd = len(shape)
    return pl.BlockSpec(shape, lambda *_: (0,) * nd)


ANY = pl.BlockSpec(memory_space=pl.ANY)


def _ew(name, body, n, row_ins, pars, row_outs, accs=(), alias=None):
    n_ri, n_p, n_ro, n_acc = len(row_ins), len(pars), len(row_outs), len(accs)

    def kern(*refs):
        i = pl.program_id(0)
        ins = refs[:n_ri]
        ps = refs[n_ri:n_ri + n_p]
        outs = refs[n_ri + n_p:n_ri + n_p + n_ro]
        acc = refs[n_ri + n_p + n_ro:]
        if n_acc:
            @pl.when(i == 0)
            def _():
                for a in acc:
                    a[...] = jnp.zeros(a.shape, a.dtype)
        body(i, ins, ps, outs, acc)

    in_specs = [ANY if blk is None else pl.BlockSpec(blk, imap) for (_, blk, imap) in row_ins]
    in_specs += [_full_spec(p.shape) for p in pars]
    out_specs = [pl.BlockSpec(blk, imap) for (_, blk, imap) in row_outs] + [_full_spec(a.shape) for a in accs]
    out_shape = [s for (s, _, _) in row_outs] + list(accs)
    return pl.pallas_call(
        kern, name=name, grid=(n,), in_specs=in_specs, out_specs=out_specs, out_shape=out_shape,
        input_output_aliases=alias or {}, compiler_params=_params(),
    )(*[a for (a, _, _) in row_ins], *pars)


def _rowblk(width, colblk=0, roff=0, tile=TR):
    return (tile, width), (lambda i: (i + roff, colblk))


def _sds(shape, dtype):
    return jax.ShapeDtypeStruct(shape, dtype)


class _Carry:
    SAME_CORE = (1, 3, 5)

    def __init__(self, jobs):
        self.jobs = list(jobs)
        self.arrays = [a for _, a in self.jobs]
        self.out_shapes = [_sds(a.shape if kind == "scatter" else (N_DEV, *a.shape), a.dtype) for kind, a in self.jobs]
        n = len(self.jobs)
        self.scratch = [pltpu.SemaphoreType.DMA((n, 7)), pltpu.SemaphoreType.DMA((n, 7)), pltpu.SemaphoreType.DMA((n,))]

    def _setup(self, sems):
        send_sems, recv_sems, local_sems = sems
        x, y, c = _place()
        me = _lin(x, y, c)
        peers = [(x ^ ((k + 1) >> 2 & 1), y ^ ((k + 1) >> 1 & 1), c ^ ((k + 1) & 1)) for k in range(7)]

        def copy(a, k, sem_k, src, dst):
            return pltpu.make_async_remote_copy(src_ref=src, dst_ref=dst, send_sem=send_sems.at[a, sem_k],
                                                recv_sem=recv_sems.at[a, sem_k], device_id=peers[k], device_id_type=MESH)

        return me, [_lin(*p) for p in peers], copy, local_sems

    def _local(self, a, kind, ins, outs, me, local_sems):
        return pltpu.make_async_copy(ins[a].at[me] if kind == "scatter" else ins[a], outs[a].at[me], local_sems.at[a])

    def start(self, ins, outs, sems):
        me, theirs, copy, local_sems = self._setup(sems)
        for a, (kind, _) in enumerate(self.jobs):
            self._local(a, kind, ins, outs, me, local_sems).start()
            if kind == "scatter":
                for k in range(7):
                    copy(a, k, k, ins[a].at[theirs[k]], outs[a].at[me]).start()
            else:
                for k in (0,) + self.SAME_CORE:
                    copy(a, k, k, ins[a], outs[a].at[me]).start()

    def wait(self, ins, outs, sems):
        me, theirs, copy, local_sems = self._setup(sems)
        for a, (kind, _) in enumerate(self.jobs):
            if kind == "scatter":
                for k in range(7):
                    copy(a, k, k, ins[a].at[me], outs[a].at[theirs[k]]).wait_recv()
                for k in range(7):
                    copy(a, k, k, ins[a].at[theirs[k]], outs[a].at[me]).wait_send()
            else:
                for k in self.SAME_CORE:
                    blk = outs[a].at[theirs[k]]
                    copy(a, k, k, ins[a], blk).wait_recv()
                    copy(a, 0, k + 1, blk, blk).start()
                copy(a, 0, 0, ins[a], outs[a].at[theirs[0]]).wait_recv()
                for k in self.SAME_CORE:
                    copy(a, 0, k + 1, ins[a], outs[a].at[theirs[k + 1]]).wait_recv()
                for k in (0,) + self.SAME_CORE:
                    copy(a, k, k, ins[a], outs[a].at[me]).wait_send()
                for k in self.SAME_CORE:
                    blk = outs[a].at[theirs[k]]
                    copy(a, 0, k + 1, blk, blk).wait_send()
            self._local(a, kind, ins, outs, me, local_sems).wait()


def _carried(kern, carry, n_in, n_out, first, last):
    if carry is None:
        return kern
    nc = len(carry.jobs)

    def wrapped(*refs):
        ins, cin = refs[:n_in], refs[n_in:n_in + nc]
        outs, cout = refs[n_in + nc:n_in + nc + n_out], refs[n_in + nc + n_out:n_in + 2 * nc + n_out]
        scr, sems = refs[n_in + 2 * nc + n_out:-3], refs[-3:]

        @pl.when(first())
        def _():
            carry.start(cin, cout, sems)

        kern(*ins, *outs, *scr)

        @pl.when(last())
        def _():
            carry.wait(cin, cout, sems)

    return wrapped


def _carry_args(carry):
    if carry is None:
        return [], [], [], [], []
    n = len(carry.jobs)
    return [ANY] * n, carry.arrays, [ANY] * n, carry.out_shapes, carry.scratch


def _grid_ends(dims):
    first = lambda: functools.reduce(jnp.logical_and, [pl.program_id(d) == 0 for d in range(len(dims))])
    last = lambda: functools.reduce(jnp.logical_and, [pl.program_id(d) == n - 1 for d, n in enumerate(dims)])
    return first, last


def _mm_call(name, a, b, mode, out_dtype, tm, tn, rows_outer=True, single_b=False, carry=None):
    if mode == "TN":
        (K, M), N = a.shape, b.shape[1]
    else:
        (M, K), N = a.shape, (b.shape[1] if mode == "NN" else b.shape[0])
    assert M % tm == 0 and N % tn == 0, (name, M, N, K, tm, tn)
    ij = (lambda g0, g1: (g0, g1)) if rows_outer else (lambda g0, g1: (g1, g0))
    grid = (M // tm, N // tn) if rows_outer else (N // tn, M // tm)
    if mode == "TN":
        a_spec = pl.BlockSpec((K, tm), lambda g0, g1: (0, ij(g0, g1)[0]))
    else:
        a_spec = pl.BlockSpec((tm, K), lambda g0, g1: (ij(g0, g1)[0], 0))
    b_blk, b_map = ((tn, K), lambda g0, g1: (ij(g0, g1)[1], 0)) if mode == "NT" else \
                   ((K, tn), lambda g0, g1: (0, ij(g0, g1)[1]))
    b_spec = pl.BlockSpec(b_blk, b_map, pipeline_mode=pl.Buffered(1)) if single_b else pl.BlockSpec(b_blk, b_map)
    dims = {"NN": NN, "NT": NT, "TN": TN}[mode]

    def kern(a_ref, b_ref, o_ref):
        o_ref[...] = _dot(a_ref[...], b_ref[...], dims).astype(o_ref.dtype)

    ci, ca, co, cs, cscr = _carry_args(carry)
    res = pl.pallas_call(
        _carried(kern, carry, 2, 1, *_grid_ends(grid)), name=name, grid=grid, in_specs=[a_spec, b_spec] + ci,
        out_specs=[pl.BlockSpec((tm, tn), lambda g0, g1: ij(g0, g1))] + co,
        out_shape=[_sds((M, N), out_dtype)] + cs, scratch_shapes=cscr,
        compiler_params=_params(("arbitrary", "arbitrary")),
    )(a, b, *ca)
    return res[0] if carry is None else (res[0], res[1:])


def _mm_act(name, a, w, mode, out_dtype=BF16, carry=None):
    rows, K = a.shape
    N = w.shape[1] if mode == "NN" else w.shape[0]
    if K > D_FF:
        return _mm_call(name, a, w, mode, out_dtype, rows // 8, N, single_b=True, carry=carry)
    tn = N if N <= 1024 else 1408
    return _mm_call(name, a, w, mode, out_dtype, rows // 4, tn, carry=carry)


def _mm_wgrad(name, x, dy, out_dtype=BF16, carry=None):
    M = x.shape[1]
    tm = 1408 if M == D_FF else 512
    return _mm_call(name, x, dy, "TN", out_dtype, tm, dy.shape[1], single_b=True, carry=carry)


def _sigmoid(x):
    return 0.5 * jnp.tanh(0.5 * x) + 0.5


def _silu(x):
    return x * _sigmoid(x)


def _silu_grad(x):
    s = _sigmoid(x)
    return s * (1.0 + x * (1.0 - s))


_GELU_K = math.sqrt(2.0 / math.pi)


def _gelu(x):
    return 0.5 * x * (1.0 + jnp.tanh(_GELU_K * (x + 0.044715 * x * x * x)))


def _gelu_grad(x):
    t = jnp.tanh(_GELU_K * (x + 0.044715 * x * x * x))
    return 0.5 * (1.0 + t) + 0.5 * x * (1.0 - t * t) * _GELU_K * (1.0 + 3.0 * 0.044715 * x * x)


def _log_sigmoid(x):
    return jnp.minimum(x, 0.0) - jnp.log(1.0 + jnp.exp(-jnp.abs(x)))


def _rms(x):
    x = x.astype(F32)
    r = lax.rsqrt(jnp.mean(x * x, axis=-1, keepdims=True) + EPS)
    return x * r, r


def _rms_bwd(dy, y, r):
    return r * (dy - y * jnp.mean(dy * y, axis=-1, keepdims=True))


def _modrow(mod_ref, i, chunk):
    lo = mod_ref[0:1, chunk * D:(chunk + 1) * D]
    hi = mod_ref[1:2, chunk * D:(chunk + 1) * D]
    return jnp.where(i == 0, lo, hi)


def _acc_seg(acc_ref, i, val):
    zero = jnp.zeros_like(val)
    acc_ref[0:1, :] += jnp.where(i == 0, val, zero)
    acc_ref[1:2, :] += jnp.where(i == 0, zero, val)


def _colsum(x):
    return jnp.sum(x, axis=0, keepdims=True)


SH1, SC1, GA1, SH2, SC2, GA2 = range(6)


def _normmod_fwd(name, xa, g, mod, c_sh, c_sc):
    T = xa.shape[0]

    def body(i, ins, ps, outs, acc):
        y, _ = _rms(ins[0][...])
        h = (y * ps[0][...]) * (1.0 + _modrow(ps[1], i, c_sc)) + _modrow(ps[1], i, c_sh)
        outs[0][...] = h.astype(BF16)

    return _ew(name, body, T // TR, [(xa, *_rowblk(D))], [g, mod], [(_sds((T, D), BF16), *_rowblk(D))])[0]


def _modrows(mod_ref, row0, n, chunk):
    t = row0 + lax.broadcasted_iota(jnp.int32, (n, 1), 0)
    return jnp.where(t < CTX, mod_ref[0:1, chunk * D:(chunk + 1) * D], mod_ref[1:2, chunk * D:(chunk + 1) * D])


def _loss_resid_bwd(name, x_out, target, mat, gpost, mod, c_ga):
    T = x_out.shape[0]

    def body(i, ins, ps, outs, acc):
        err = ins[0][...] - ins[1][...]
        lat = i > 0
        dx = jnp.where(lat, err * (1.0 / D), 0.0)
        outs[0][...] = dx
        acc[2][...] += jnp.where(lat, _colsum(err * err), 0.0)
        outs[1][...] = _resid_bwd_vals(i, dx, ins[2][...], ps[0][...], ps[1], c_ga, acc[0], acc[1]).astype(BF16)

    tgt_blk = ((TR, D), lambda i: (jnp.maximum(i - 1, 0), 0))
    return _ew(name, body, T // TR, [(x_out, *_rowblk(D)), (target, *tgt_blk), (mat, *_rowblk(D))], [gpost, mod],
               [(_sds((T, D), F32), *_rowblk(D)), (_sds((T, D), BF16), *_rowblk(D))],
               [_sds((2, D), F32), _sds((1, D), F32), _sds((1, D), F32)])


def _mod_for(mod_ref, i, chunk, row0, n):
    return _modrow(mod_ref, i, chunk) if row0 is None else _modrows(mod_ref, row0, n, chunk)


def _acc_for(acc_ref, i, v, row0):
    if row0 is None:
        _acc_seg(acc_ref, i, _colsum(v))
        return

    @pl.when(row0 < CTX)
    def _():
        is_ctx = row0 + lax.broadcasted_iota(jnp.int32, (v.shape[0], 1), 0) < CTX
        acc_ref[0:1, :] += _colsum(jnp.where(is_ctx, v, 0.0))
        acc_ref[1:2, :] += _colsum(jnp.where(is_ctx, 0.0, v))

    @pl.when(row0 >= CTX)
    def _():
        acc_ref[1:2, :] += _colsum(v)


def _resid_bwd_vals(i, dout, mat, gpost, mod_ref, c_ga, acc_ga, acc_g, row0=None):
    ym, rm = _rms(mat)
    ga = _mod_for(mod_ref, i, c_ga, row0, dout.shape[0])
    _acc_for(acc_ga, i, dout * (ym * gpost), row0)
    dn = dout * ga
    acc_g[...] += _colsum(dn * ym)
    return _rms_bwd(dn * gpost, ym, rm)


def _normmod_bwd_vals(i, dh, xin, g, mod_ref, c_sh, c_sc, acc_sh, acc_sc, acc_g, row0=None):
    dh = dh.astype(F32)
    y, r = _rms(xin)
    _acc_for(acc_sc, i, dh * (y * g), row0)
    _acc_for(acc_sh, i, dh, row0)
    dyg = dh * (1.0 + _mod_for(mod_ref, i, c_sc, row0, dh.shape[0]))
    acc_g[...] += _colsum(dyg * y)
    return _rms_bwd(dyg * g, y, r)


def _parts(i, tm):
    return [(slice(0, tm), i * tm)]


FT = 1408


def _ffn_in_fused(name, h2, w_t, carry=None):
    T = h2.shape[0]
    tm, nj = T // 4, D_FF // FT

    def kern(a_ref, bg_ref, bu_ref, fg_ref, fu_ref, s_ref):
        for rows, _ in _parts(0, tm):
            a = a_ref[rows, :]
            g = _dot(a, bg_ref[...], NT)
            u = _dot(a, bu_ref[...], NT)
            fg_ref[rows, :] = g.astype(BF16)
            fu_ref[rows, :] = u.astype(BF16)
            s_ref[rows, :] = (_silu(g) * u).astype(BF16)

    o_spec = pl.BlockSpec((tm, FT), lambda i, j: (i, j))
    ci, ca, co, cs, cscr = _carry_args(carry)
    res = pl.pallas_call(
        _carried(kern, carry, 3, 3, *_grid_ends((4, nj))), name=name, grid=(4, nj),
        in_specs=[pl.BlockSpec((tm, D), lambda i, j: (i, 0)), pl.BlockSpec((FT, D), lambda i, j: (j, 0)),
                  pl.BlockSpec((FT, D), lambda i, j: (j + nj, 0))] + ci,
        out_specs=[o_spec] * 3 + co, out_shape=[_sds((T, D_FF), BF16)] * 3 + cs, scratch_shapes=cscr,
        compiler_params=_params(("arbitrary", "arbitrary")),
    )(h2, w_t, w_t, *ca)
    return res if carry is None else (res[:3], res[3:])


def _norm_chain(row0, xin, mat, gpost, mod_ref, c_ga, gnext, modn_ref, c_sh, c_sc):
    n = xin.shape[0]
    ym, _ = _rms(mat.astype(BF16))
    xo = xin + _modrows(mod_ref, row0, n, c_ga) * (ym * gpost)
    y, _ = _rms(xo)
    h = (y * gnext) * (1.0 + _modrows(modn_ref, row0, n, c_sc)) + _modrows(modn_ref, row0, n, c_sh)
    return xo, h.astype(BF16)


def _out_fused(name, p, u, o_all, xa, w_o_rnn, w_o_attn, w_out, gpost, mod, gnext):
    T = u.shape[0]
    tm = T // 8

    def kern(g0, g1, g2, g3, u_ref, o_ref, xa_ref, wr_ref, wa_ref, w_ref, gpost_ref, mod_ref, gnext_ref,
             ya_ref, yb_ref, z_ref, m_ref, x1_ref, h2_ref):
        for rows, row0 in _parts(pl.program_id(0), tm):
            ya = _dot(u_ref[rows, :], wr_ref[...]).astype(BF16)
            yb = _dot(o_ref[rows, :], wa_ref[...]).astype(BF16)
            ya_ref[rows, :] = ya
            yb_ref[rows, :] = yb
            ga = _sigmoid(jnp.concatenate([g0[rows, :], g1[rows, :]], axis=1).astype(F32))
            gb = _sigmoid(jnp.concatenate([g2[rows, :], g3[rows, :]], axis=1).astype(F32))
            z = (ga * ya.astype(F32) + gb * yb.astype(F32)).astype(BF16)
            z_ref[rows, :] = z
            m = _dot(z, w_ref[...])
            m_ref[rows, :] = m.astype(BF16)
            x1_ref[rows, :], h2_ref[rows, :] = _norm_chain(row0, xa_ref[rows, :], m, gpost_ref[...], mod_ref, GA1,
                                                           gnext_ref[...], mod_ref, SH2, SC2)

    row = lambda w: pl.BlockSpec((tm, w), lambda i: (i, 0))
    return pl.pallas_call(
        kern, name=name, grid=(T // tm,),
        in_specs=[pl.BlockSpec((tm, GLB), lambda i, q=q: (i, COL_GL // GLB + q)) for q in range(4)]
                 + [row(D), row(D), row(D)] + [_full_spec(a.shape) for a in (w_o_rnn, w_o_attn, w_out, gpost, mod, gnext)],
        out_specs=[row(D)] * 6,
        out_shape=[_sds((T, D), BF16)] * 4 + [_sds((T, D), F32), _sds((T, D), BF16)],
        compiler_params=_params(),
    )(p, p, p, p, u, o_all, xa, w_o_rnn, w_o_attn, w_out, gpost, mod, gnext)


def _ffn_out_fused(name, s, w, x1, gpost, mod, nxt=None):
    T = s.shape[0]
    tm = T // 8

    def kern(s_ref, w_ref, x1_ref, gpost_ref, mod_ref, *rest):
        for rows, row0 in _parts(pl.program_id(0), tm):
            e = _dot(s_ref[rows, :], w_ref[...])
            if nxt is None:
                e_ref, xo_ref = rest
                ym, _ = _rms(e.astype(BF16))
                xo_ref[rows, :] = x1_ref[rows, :] + _modrows(mod_ref, row0, e.shape[0], GA2) * (ym * gpost_ref[...])
            else:
                gnext_ref, modn_ref, e_ref, xo_ref, h_ref = rest
                xo_ref[rows, :], h_ref[rows, :] = _norm_chain(row0, x1_ref[rows, :], e, gpost_ref[...], mod_ref, GA2,
                                                              gnext_ref[...], modn_ref, SH1, SC1)
            e_ref[rows, :] = e.astype(BF16)

    row = lambda w_: pl.BlockSpec((tm, w_), lambda i: (i, 0))
    extra = [] if nxt is None else list(nxt)
    return pl.pallas_call(
        kern, name=name, grid=(T // tm,),
        in_specs=[row(D_FF), _full_spec(w.shape), row(D), _full_spec(gpost.shape), _full_spec(mod.shape)]
                 + [_full_spec(a.shape) for a in extra],
        out_specs=[row(D)] * (2 if nxt is None else 3),
        out_shape=[_sds((T, D), BF16), _sds((T, D), F32)] + ([] if nxt is None else [_sds((T, D), BF16)]),
        compiler_params=_params(),
    )(s, w, x1, gpost, mod, *extra)


def _ffn_bwd_fused(name, fg, fu, w, de=None, head=None):
    T = fg.shape[0]
    tm = T // 8
    row = lambda w_: pl.BlockSpec((tm, w_), lambda i: (i, 0))
    w_spec = pl.BlockSpec(w.shape, lambda i: (0, 0), pipeline_mode=pl.Buffered(1))

    def tail(rows, de_v, fg_ref, fu_ref, w_ref, df_ref):
        ds = _dot(de_v, w_ref[...], NT)
        g, u = fg_ref[rows, :].astype(F32), fu_ref[rows, :].astype(F32)
        df_ref[rows, :] = jnp.concatenate([ds * u * _silu_grad(g), ds * _silu(g)], axis=1).astype(BF16)

    if head is None:
        def kern(de_ref, fg_ref, fu_ref, w_ref, df_ref):
            for rows, _ in _parts(pl.program_id(0), tm):
                tail(rows, de_ref[rows, :], fg_ref, fu_ref, w_ref, df_ref)

        return pl.pallas_call(
            kern, name=name, grid=(T // tm,), in_specs=[row(D), row(D_FF), row(D_FF), w_spec],
            out_specs=[row(2 * D_FF)], out_shape=[_sds((T, 2 * D_FF), BF16)], compiler_params=_params(),
        )(de, fg, fu, w)

    dx2, e, gpost, mod = head

    def kern(dx_ref, e_ref, fg_ref, fu_ref, w_ref, gpost_ref, mod_ref, de_ref, df_ref, dga_ref, dg_ref):
        i = pl.program_id(0)

        @pl.when(i == 0)
        def _():
            dga_ref[...] = jnp.zeros(dga_ref.shape, F32)
            dg_ref[...] = jnp.zeros(dg_ref.shape, F32)

        for rows, row0 in _parts(i, tm):
            de_v = _resid_bwd_vals(i, dx_ref[rows, :], e_ref[rows, :], gpost_ref[...], mod_ref, GA2, dga_ref, dg_ref,
                                   row0=row0).astype(BF16)
            de_ref[rows, :] = de_v
            tail(rows, de_v, fg_ref, fu_ref, w_ref, df_ref)

    return pl.pallas_call(
        kern, name=name, grid=(T // tm,),
        in_specs=[row(D), row(D), row(D_FF), row(D_FF), w_spec, _full_spec(gpost.shape), _full_spec(mod.shape)],
        out_specs=[row(D), row(2 * D_FF), _full_spec((2, D)), _full_spec((1, D))],
        out_shape=[_sds((T, D), BF16), _sds((T, 2 * D_FF), BF16), _sds((2, D), F32), _sds((1, D), F32)],
        compiler_params=_params(),
    )(dx2, e, fg, fu, w, gpost, mod)


def _zero_at_start(i, refs):
    @pl.when(i == 0)
    def _():
        for r in refs:
            r[...] = jnp.zeros(r.shape, F32)


def _proj_bwd_fused(name, dp, dgl, w_in_t, xa, dx1, gpre, mod, carry=None):
    T = dp.shape[0]
    tm = T // 8
    row = lambda w_: pl.BlockSpec((tm, w_), lambda i: (i, 0))

    def kern(dp_ref, dgl_ref, w_ref, xa_ref, dx1_ref, g_ref, mod_ref, dxa_ref, dsh_ref, dsc_ref, dg_ref):
        i = pl.program_id(0)
        _zero_at_start(i, (dsh_ref, dsc_ref, dg_ref))
        for rows, row0 in _parts(i, tm):
            dh = _dot(dp_ref[rows, :], w_ref[0:DP_W, :]) + _dot(dgl_ref[rows, :], w_ref[DP_W:, :])
            dxa_ref[rows, :] = dx1_ref[rows, :] + _normmod_bwd_vals(i, dh, xa_ref[rows, :], g_ref[...], mod_ref, SH1,
                                                                    SC1, dsh_ref, dsc_ref, dg_ref, row0=row0)

    ci, ca, co, cs, cscr = _carry_args(carry)
    res = pl.pallas_call(
        _carried(kern, carry, 7, 4, *_grid_ends((T // tm,))), name=name, grid=(T // tm,),
        in_specs=[row(DP_W), row(P_W - DP_W),
                  pl.BlockSpec(w_in_t.shape, lambda i: (0, 0), pipeline_mode=pl.Buffered(1)), row(D), row(D),
                  _full_spec(gpre.shape), _full_spec(mod.shape)] + ci,
        out_specs=[row(D), _full_spec((2, D)), _full_spec((2, D)), _full_spec((1, D))] + co,
        out_shape=[_sds((T, D), F32), _sds((2, D), F32), _sds((2, D), F32), _sds((1, D), F32)] + cs,
        scratch_shapes=cscr, compiler_params=_params(),
    )(dp, dgl, w_in_t, xa, dx1, gpre, mod, *ca)
    return res if carry is None else (res[:4], res[4:])


def _proj_wgrad(name, dp, dgl, h, carry=None):
    T, N = h.shape
    n1, n2 = DP_W // GLB, (P_W - DP_W) // GLB

    def kern(a1_ref, a2_ref, h_ref, o_ref):
        i = pl.program_id(0)

        @pl.when(i < n1)
        def _():
            o_ref[...] = _dot(a1_ref[...], h_ref[...], TN).astype(o_ref.dtype)

        @pl.when(i >= n1)
        def _():
            o_ref[...] = _dot(a2_ref[...], h_ref[...], TN).astype(o_ref.dtype)

    ci, ca, co, cs, cscr = _carry_args(carry)
    res = pl.pallas_call(
        _carried(kern, carry, 3, 1, *_grid_ends((n1 + n2,))), name=name, grid=(n1 + n2,),
        in_specs=[pl.BlockSpec((T, GLB), lambda i: (0, jnp.minimum(i, n1 - 1))),
                  pl.BlockSpec((T, GLB), lambda i: (0, jnp.maximum(i - n1, 0))),
                  pl.BlockSpec((T, N), lambda i: (0, 0), pipeline_mode=pl.Buffered(1))] + ci,
        out_specs=[pl.BlockSpec((GLB, N), lambda i: (i, 0))] + co,
        out_shape=[_sds((P_W, N), BF16)] + cs, scratch_shapes=cscr, compiler_params=_params(),
    )(dp, dgl, h, *ca)
    return res[0] if carry is None else (res[0], res[1:])


def _ffn_in_bwd_fused(name, df, w_t, x1, dres, mat, gpre, mod, gpost, carry=None):
    T = df.shape[0]
    tm = T // 8
    row = lambda w_: pl.BlockSpec((tm, w_), lambda i: (i, 0))

    def kern(df_ref, w_ref, x1_ref, dres_ref, mat_ref, gpre_ref, mod_ref, gpost_ref,
             dx1_ref, dm_ref, dsh_ref, dsc_ref, dgpre_ref, dga_ref, dgpost_ref):
        i = pl.program_id(0)
        _zero_at_start(i, (dsh_ref, dsc_ref, dgpre_ref, dga_ref, dgpost_ref))
        for rows, row0 in _parts(i, tm):
            dh2 = _dot(df_ref[rows, :], w_ref[...])
            dx1 = dres_ref[rows, :] + _normmod_bwd_vals(i, dh2, x1_ref[rows, :], gpre_ref[...], mod_ref, SH2, SC2,
                                                        dsh_ref, dsc_ref, dgpre_ref, row0=row0)
            dx1_ref[rows, :] = dx1
            dm_ref[rows, :] = _resid_bwd_vals(i, dx1, mat_ref[rows, :], gpost_ref[...], mod_ref, GA1, dga_ref,
                                              dgpost_ref, row0=row0).astype(BF16)

    ci, ca, co, cs, cscr = _carry_args(carry)
    res = pl.pallas_call(
        _carried(kern, carry, 8, 7, *_grid_ends((T // tm,))), name=name, grid=(T // tm,),
        in_specs=[row(2 * D_FF), pl.BlockSpec(w_t.shape, lambda i: (0, 0), pipeline_mode=pl.Buffered(1)), row(D),
                  row(D), row(D), _full_spec(gpre.shape), _full_spec(mod.shape), _full_spec(gpost.shape)] + ci,
        out_specs=[row(D), row(D), _full_spec((2, D)), _full_spec((2, D)), _full_spec((1, D)), _full_spec((2, D)),
                   _full_spec((1, D))] + co,
        out_shape=[_sds((T, D), F32), _sds((T, D), BF16), _sds((2, D), F32), _sds((2, D), F32), _sds((1, D), F32),
                   _sds((2, D), F32), _sds((1, D), F32)] + cs,
        scratch_shapes=cscr, compiler_params=_params(),
    )(df, w_t, x1, dres, mat, gpre, mod, gpost, *ca)
    return res if carry is None else (res[:7], res[7:])


def _out_bwd_fused(name, dm, w_out, w_o_rnn, w_o_attn, p, ya, yb):
    T = dm.shape[0]
    tm = T // 8
    row = lambda w_: pl.BlockSpec((tm, w_), lambda i: (i, 0))

    def kern(dm_ref, w_ref, wr_ref, wa_ref, g0, g1, g2, g3, ya_ref, yb_ref, dya_ref, dyb_ref, dgl_ref, du_ref, do_ref):
        for rows, _ in _parts(pl.program_id(0), tm):
            dz = _dot(dm_ref[rows, :], w_ref[...], NT)
            ga = _sigmoid(jnp.concatenate([g0[rows, :], g1[rows, :]], axis=1).astype(F32))
            gb = _sigmoid(jnp.concatenate([g2[rows, :], g3[rows, :]], axis=1).astype(F32))
            dya = (dz * ga).astype(BF16)
            dyb = (dz * gb).astype(BF16)
            dya_ref[rows, :] = dya
            dyb_ref[rows, :] = dyb
            dgl_ref[rows, :] = jnp.concatenate([dz * ya_ref[rows, :].astype(F32) * ga * (1.0 - ga),
                                                dz * yb_ref[rows, :].astype(F32) * gb * (1.0 - gb)],
                                               axis=1).astype(BF16)
            du_ref[rows, :] = _dot(dya, wr_ref[...], NT).astype(BF16)
            do_ref[rows, :] = _dot(dyb, wa_ref[...], NT).astype(BF16)

    return pl.pallas_call(
        kern, name=name, grid=(T // tm,),
        in_specs=[row(D)] + [_full_spec(w.shape) for w in (w_out, w_o_rnn, w_o_attn)]
                 + [pl.BlockSpec((tm, GLB), lambda i, q=q: (i, COL_GL // GLB + q)) for q in range(4)] + [row(D), row(D)],
        out_specs=[row(D), row(D), row(2 * D), row(D), row(D)],
        out_shape=[_sds((T, D), BF16), _sds((T, D), BF16), _sds((T, 2 * D), BF16), _sds((T, D), BF16),
                   _sds((T, D), BF16)],
        compiler_params=_params(),
    )(dm, w_out, w_o_rnn, w_o_attn, p, p, p, p, ya, yb)


AB = 128
CTX_BLKS = CTX // AB


def _rope_tables(S):
    pos = jnp.arange(S, dtype=jnp.int32)
    inv = ROPE_BASE ** (-jnp.arange(N_FREQ, dtype=F32) / N_FREQ)
    ang_r = (pos // GRID_W).astype(F32)[:, None] * inv[None, :]
    ang_c = (pos % GRID_W).astype(F32)[:, None] * inv[None, :]
    cos = jnp.concatenate([jnp.cos(ang_r)] * 2 + [jnp.cos(ang_c)] * 2, axis=1)
    sin = jnp.concatenate([-jnp.sin(ang_r), jnp.sin(ang_r), -jnp.sin(ang_c), jnp.sin(ang_c)], axis=1)
    return cos, sin


def _rope(x, cos, sin):
    w = x.shape[1]
    reps = w // HEAD
    lane = lax.broadcasted_iota(jnp.int32, x.shape, 1)
    partner = jnp.where((lane & 63) < 32, pltpu.roll(x, w - 32, 1), pltpu.roll(x, 32, 1))
    return x * jnp.tile(cos, (1, reps)) + partner * jnp.tile(sin, (1, reps))


def _unrope(dx, cos, sin):
    w = dx.shape[1]
    reps = w // HEAD
    lane = lax.broadcasted_iota(jnp.int32, dx.shape, 1)
    t = dx * jnp.tile(sin, (1, reps))
    partner = jnp.where((lane & 63) < 32, pltpu.roll(t, w - 32, 1), pltpu.roll(t, 32, 1))
    return dx * jnp.tile(cos, (1, reps)) + partner


def _qkv_prep(name, p, cos, sin, S):
    T = CTX + S
    nt = T // TR
    cb = CTX // TR
    KW = N_KV * HEAD

    def with_ones(v):
        ones = jnp.ones((TR, HEAD), BF16)
        return jnp.concatenate([v[:, kh * HEAD:(kh + 1) * HEAD] if part == 0 else ones
                                for kh in range(N_KV) for part in range(2)], axis=1)

    def kern(q_ref, k_ref, v_ref, cos_ref, sin_ref, qa_ref, kp_ref, vp_ref, kc_ref, vc_ref):
        i = pl.program_id(0)
        cos_v, sin_v = cos_ref[...], sin_ref[...]
        @pl.when(i < cb)
        def _():
            qa_ref[...] = (q_ref[...].astype(F32) * ATT_SCALE).astype(BF16)
            kc_ref[...] = k_ref[...]
            vc_ref[...] = with_ones(v_ref[...])

        @pl.when((i < cb) | (i >= nt))
        def _():
            kp_ref[...] = jnp.zeros(kp_ref.shape, BF16)
            vp_ref[...] = jnp.zeros(vp_ref.shape, BF16)

        @pl.when((i >= cb) & (i < nt))
        def _():
            qa_ref[...] = (_rope(q_ref[...].astype(F32), cos_v, sin_v) * ATT_SCALE).astype(BF16)
            kp_ref[...] = _rope(k_ref[...].astype(F32), cos_v, sin_v).astype(BF16)
            vp_ref[...] = with_ones(v_ref[...])

    tok = lambda i: jnp.minimum(i, nt - 1)
    lat_map = lambda i: (jnp.clip(i - cb, 0, nt - cb - 1), 0)
    ctx_map = lambda i: (jnp.minimum(i, cb - 1), 0)
    return pl.pallas_call(
        kern, name=name, grid=(nt + cb,),
        in_specs=[pl.BlockSpec((TR, N_Q * HEAD), lambda i: (tok(i), COL_Q // (N_Q * HEAD))),
                  pl.BlockSpec((TR, KW), lambda i: (tok(i), COL_K // KW)),
                  pl.BlockSpec((TR, KW), lambda i: (tok(i), COL_V // KW)),
                  pl.BlockSpec((TR, HEAD), lat_map), pl.BlockSpec((TR, HEAD), lat_map)],
        out_specs=[pl.BlockSpec((TR, N_Q * HEAD), lambda i: (tok(i), 0)),
                   pl.BlockSpec((TR, KW), lambda i: (i, 0)), pl.BlockSpec((TR, 2 * KW), lambda i: (i, 0)),
                   pl.BlockSpec((TR, KW), ctx_map), pl.BlockSpec((TR, 2 * KW), ctx_map)],
        out_shape=[_sds((T, N_Q * HEAD), BF16), _sds((S + 2 * CTX, KW), BF16), _sds((S + 2 * CTX, 2 * KW), BF16),
                   _sds((CTX, KW), BF16), _sds((CTX, 2 * KW), BF16)],
        compiler_params=_params(),
    )(p, p, p, cos, sin)


GW = Q_PER_KV * HEAD
HG = Q_PER_KV


def _band_bias(S):
    r = jnp.arange(AB, dtype=jnp.int32)[:, None]
    c = jnp.arange(3 * AB, dtype=jnp.int32)[None, :]
    near = jnp.abs(c - AB - r) <= AB
    valid = jnp.stack([near & (c >= AB), near, near & (c < 2 * AB)])
    return jnp.where(valid, 0.0, NEG_INF).astype(F32)


def _bias_spec(S):
    nb = S // AB
    return pl.BlockSpec((None, AB, 3 * AB), lambda kh, n: (jnp.where(n == 0, 0, jnp.where(n == nb - 1, 2, 1)), 0, 0))


def _head_probs(q, sink, kc, vce, kb, vbe, bias):
    s_c = _dot(q, kc, NT)
    m = jnp.maximum(jnp.max(s_c, axis=-1, keepdims=True), sink)
    if kb is not None:
        s_b = _dot(q, kb, NT) + bias
        m = jnp.maximum(m, jnp.max(s_b, axis=-1, keepdims=True))
    p_c = jnp.exp(s_c - m).astype(BF16)
    acc = _dot(p_c, vce)
    p_b = None
    if kb is not None:
        p_b = jnp.exp(s_b - m).astype(BF16)
        acc = acc + _dot(p_b, vbe)
    return p_c, p_b, m, acc


def _attn_fwd(name, qa, kc, vc, sink4, S, band=None, prev=None, carry=None):
    T = qa.shape[0]
    has_band = band is not None
    nq = S // AB if has_band else CTX_BLKS
    q_off = CTX_BLKS if has_band else 0

    def kern(*refs):
        q_ref, kc_ref, vc_ref, sink_ref = refs[:4]
        rest = refs[4:]
        o_ref = rest[-1]
        n = pl.program_id(1)
        kc_v, vce = kc_ref[...], vc_ref[...]
        kb = vbe = bias = None
        if has_band:
            kp_ref, vp_ref, bias_ref = rest[:3]
            start = pl.multiple_of(n * AB + (CTX - AB), AB)
            kb = kp_ref[pl.ds(start, 3 * AB), :]
            vbe = vp_ref[pl.ds(start, 3 * AB), :]
            bias = bias_ref[...]
        outs = []
        for g in range(Q_PER_KV):
            sink = sink_ref[g:g + 1, 0:1]
            _, _, m, acc = _head_probs(q_ref[:, g * HEAD:(g + 1) * HEAD], sink, kc_v, vce, kb, vbe, bias)
            l = acc[:, HEAD:] + jnp.exp(sink - m)
            outs.append(acc[:, :HEAD] / l)
        o_ref[...] = jnp.concatenate(outs, axis=1).astype(BF16)

    in_specs = [pl.BlockSpec((AB, GW), lambda kh, n: (n + q_off, kh)),
                pl.BlockSpec((CTX, HEAD), lambda kh, n: (0, kh)), pl.BlockSpec((CTX, 2 * HEAD), lambda kh, n: (0, kh)),
                pl.BlockSpec((None, Q_PER_KV, HEAD), lambda kh, n: (kh, 0, 0))]
    args = [qa, kc, vc, sink4]
    if has_band:
        in_specs += [pl.BlockSpec((S + 2 * CTX, HEAD), lambda kh, n: (0, kh)),
                     pl.BlockSpec((S + 2 * CTX, 2 * HEAD), lambda kh, n: (0, kh)), _bias_spec(S)]
        args += list(band)
    alias = {}
    if prev is not None:
        in_specs.append(ANY)
        alias = {len(args): 0}
        args.append(prev)
    ci, ca, co, cs, cscr = _carry_args(carry)
    res = pl.pallas_call(
        _carried(kern, carry, len(args), 1, *_grid_ends((N_KV, nq))), name=name, grid=(N_KV, nq),
        in_specs=in_specs + ci,
        out_specs=[pl.BlockSpec((AB, GW), lambda kh, n: (n + q_off, kh))] + co,
        out_shape=[_sds((T, N_Q * HEAD), BF16)] + cs, input_output_aliases=alias, scratch_shapes=cscr,
        compiler_params=_params(("arbitrary", "arbitrary")),
    )(*args, *ca)
    return res[0] if carry is None else (res[0], res[1:])


def _attn_bwd(name, qa, kc, vc, sink4, o_all, do_all, S, band=None, prev_dq=None, carry=None):
    T = qa.shape[0]
    has_band = band is not None
    nq = S // AB if has_band else CTX_BLKS
    q_off = CTX_BLKS if has_band else 0
    KW = N_KV * HEAD

    def kern(*refs):
        q_ref, kc_ref, vc_ref, sink_ref, o_ref, do_ref = refs[:6]
        rest = refs[6:]
        if has_band:
            kp_ref, vp_ref, bias_ref, cos_ref, sin_ref = rest[:5]
            rest = rest[5:]
        if prev_dq is not None:
            rest = rest[1:]
        dq_ref, dkc_ref, dvc_ref, dsink_ref = rest[:4]
        n = pl.program_id(1)

        @pl.when(n == 0)
        def _():
            dkc_ref[...] = jnp.zeros(dkc_ref.shape, F32)
            dvc_ref[...] = jnp.zeros(dvc_ref.shape, F32)
            dsink_ref[...] = jnp.zeros(dsink_ref.shape, F32)
            if has_band:
                rest[4][...] = jnp.zeros(rest[4].shape, F32)
                rest[5][...] = jnp.zeros(rest[5].shape, F32)

        kc_v, vce = kc_ref[...], vc_ref[...]
        vc_v = vce[:, :HEAD]
        kb = vbe = vb = bias = None
        if has_band:
            start = pl.multiple_of(n * AB + (CTX - AB), AB)
            kb = kp_ref[pl.ds(start, 3 * AB), :]
            vbe = vp_ref[pl.ds(start, 3 * AB), :]
            vb = vbe[:, :HEAD]
            bias = bias_ref[...]
        dq_parts, dsink_parts = [], []
        for g0 in range(0, Q_PER_KV, HG):
            heads = range(g0, g0 + HG)
            stack = lambda ref: jnp.concatenate([ref[:, g * HEAD:(g + 1) * HEAD] for g in heads], axis=0)
            q4, do4 = stack(q_ref), stack(do_ref)
            sink = jnp.concatenate([jnp.broadcast_to(sink_ref[g:g + 1, 0:1], (AB, 1)) for g in heads], axis=0)
            s_c = _dot(q4, kc_v, NT)
            m = jnp.maximum(jnp.max(s_c, axis=-1, keepdims=True), sink)
            if has_band:
                s_b = _dot(q4, kb, NT) + jnp.tile(bias, (HG, 1))
                m = jnp.maximum(m, jnp.max(s_b, axis=-1, keepdims=True))
            p_c = jnp.exp(s_c - m).astype(BF16).astype(F32)
            p_sink = jnp.exp(sink - m)
            l = jnp.sum(p_c, axis=-1, keepdims=True) + p_sink
            if has_band:
                p_b = jnp.exp(s_b - m).astype(BF16).astype(F32)
                l = l + jnp.sum(p_b, axis=-1, keepdims=True)
            inv = 1.0 / l
            delta = jnp.sum(do4.astype(F32) * stack(o_ref).astype(F32), axis=-1, keepdims=True)
            do4b = do4.astype(BF16)
            pn_c = (p_c * inv).astype(BF16)
            ds_c = (p_c * inv * (_dot(do4b, vc_v, NT) - delta)).astype(BF16)
            dq4 = _dot(ds_c, kc_v)
            dkc_ref[...] += _dot(q4, ds_c, TN)
            dvc_ref[...] += _dot(do4b, pn_c, TN)
            if has_band:
                pn_b = (p_b * inv).astype(BF16)
                ds_b = (p_b * inv * (_dot(do4b, vb, NT) - delta)).astype(BF16)
                dq4 = dq4 + _dot(ds_b, kb)
                rest[4][:, pl.ds(start, 3 * AB)] += _dot(q4, ds_b, TN)
                rest[5][:, pl.ds(start, 3 * AB)] += _dot(do4b, pn_b, TN)
            dq4 = dq4 * ATT_SCALE
            dq_parts += [dq4[k * AB:(k + 1) * AB, :] for k in range(HG)]
            ps = p_sink * inv * delta
            dsink_parts += [jnp.broadcast_to(-jnp.sum(ps[k * AB:(k + 1) * AB, :], axis=0, keepdims=True), (1, HEAD))
                            for k in range(HG)]
        dq = jnp.concatenate(dq_parts, axis=1)
        dq_ref[...] = (_unrope(dq, cos_ref[...], sin_ref[...]) if has_band else dq).astype(BF16)
        dsink_ref[...] += jnp.concatenate(dsink_parts, axis=0)

    q_spec = pl.BlockSpec((AB, GW), lambda kh, n: (n + q_off, kh))
    c_spec = pl.BlockSpec((CTX, HEAD), lambda kh, n: (0, kh))
    ce_spec = pl.BlockSpec((CTX, 2 * HEAD), lambda kh, n: (0, kh))
    s_spec = pl.BlockSpec((None, Q_PER_KV, HEAD), lambda kh, n: (kh, 0, 0))
    in_specs = [q_spec, c_spec, ce_spec, s_spec, q_spec, q_spec]
    args = [qa, kc, vc, sink4, o_all, do_all]
    ct_spec = pl.BlockSpec((HEAD, CTX), lambda kh, n: (kh, 0))
    dq_spec = pl.BlockSpec((AB, GW), lambda kh, n: (n + q_off, COL_Q // GW + kh))
    out_specs = [dq_spec, ct_spec, ct_spec, s_spec]
    out_shape = [_sds((T, DP_W), BF16), _sds((KW, CTX), F32), _sds((KW, CTX), F32), _sds((N_KV, Q_PER_KV, HEAD), F32)]
    if has_band:
        p_spec = pl.BlockSpec((S + 2 * CTX, HEAD), lambda kh, n: (0, kh))
        pt_spec = pl.BlockSpec((HEAD, S + 2 * CTX), lambda kh, n: (kh, 0))
        rope_spec = pl.BlockSpec((AB, HEAD), lambda kh, n: (n, 0))
        in_specs += [p_spec, pl.BlockSpec((S + 2 * CTX, 2 * HEAD), lambda kh, n: (0, kh)), _bias_spec(S), rope_spec,
                     rope_spec]
        args += list(band)
        out_specs += [pt_spec, pt_spec]
        out_shape += [_sds((KW, S + 2 * CTX), F32)] * 2
    alias = {}
    if prev_dq is not None:
        in_specs.append(ANY)
        alias = {len(args): 0}
        args.append(prev_dq)
    ci, ca, co, cs, cscr = _carry_args(carry)
    n_out = len(out_specs)
    res = pl.pallas_call(
        _carried(kern, carry, len(args), n_out, *_grid_ends((N_KV, nq))), name=name, grid=(N_KV, nq),
        in_specs=in_specs + ci, out_specs=out_specs + co, out_shape=out_shape + cs, scratch_shapes=cscr,
        input_output_aliases=alias, compiler_params=_params(("arbitrary", "arbitrary")),
    )(*args, *ca)
    return res if carry is None else (res[:n_out], res[n_out:])


def _dkv_assemble(name, dp, dkp, dvp, dkc_l, dvc_l, dkc_c, dvc_c, cos, sin, S):
    T = CTX + S
    KW = N_KV * HEAD

    def kern(dkp_ref, dvp_ref, dkcl_ref, dvcl_ref, dkcc_ref, dvcc_ref, cos_ref, sin_ref, dp_in, out_ref):
        i = pl.program_id(0)

        @pl.when(i == 0)
        def _():
            out_ref[...] = jnp.concatenate([(dkcl_ref[...] + dkcc_ref[...]).T, (dvcl_ref[...] + dvcc_ref[...]).T],
                                           axis=1).astype(BF16)

        @pl.when(i > 0)
        def _():
            out_ref[...] = jnp.concatenate([_unrope(dkp_ref[...].T, cos_ref[...], sin_ref[...]), dvp_ref[...].T],
                                           axis=1).astype(BF16)

    same = lambda i: (0, i)
    lat_map = lambda i: (jnp.maximum(i - 1, 0), 0)
    ctx_map = lambda i: (0, 0)
    return pl.pallas_call(
        kern, name=name, grid=(T // TR,),
        in_specs=[pl.BlockSpec((KW, TR), same), pl.BlockSpec((KW, TR), same),
                  pl.BlockSpec((KW, CTX), ctx_map), pl.BlockSpec((KW, CTX), ctx_map),
                  pl.BlockSpec((KW, CTX), ctx_map), pl.BlockSpec((KW, CTX), ctx_map),
                  pl.BlockSpec((TR, HEAD), lat_map), pl.BlockSpec((TR, HEAD), lat_map), ANY],
        out_specs=pl.BlockSpec((TR, 2 * KW), lambda i: (i, COL_K // (2 * KW))),
        out_shape=_sds((T, DP_W), BF16), input_output_aliases={8: 0}, compiler_params=_params(),
    )(dkp, dvp, dkc_l, dvc_l, dkc_c, dvc_c, cos, sin, dp)


RB = 128
CH = 256
HALO = 8
SUB = 8
GRP = 8


def _vscan(a, b, reverse):
    row = lax.broadcasted_iota(jnp.int32, a.shape, 0)
    A, H = a, b
    for s in (1, 2, 4):
        sh = SUB - s if reverse else s
        m = (row < SUB - s) if reverse else (row >= s)
        As = pltpu.roll(A, sh, 0)
        Hs = pltpu.roll(H, sh, 0)
        H = jnp.where(m, A * Hs + H, H)
        A = jnp.where(m, A * As, A)
    return A, H


def _scan_rows(a_ref, b_ref, r0, nrows, reverse, carry, emit):
    ngrp = nrows // (SUB * GRP)
    row = lax.broadcasted_iota(jnp.int32, (SUB, RB), 0)

    def grp(gi, carry):
        g = (ngrp - 1 - gi) if reverse else gi
        base = r0 + g * (SUB * GRP)
        for v in (range(GRP - 1, -1, -1) if reverse else range(GRP)):
            rs = pl.multiple_of(base + v * SUB, SUB)
            A, H = _vscan(a_ref[pl.ds(rs, SUB), :], b_ref[pl.ds(rs, SUB), :], reverse)
            hf = H + A * carry
            if reverse:
                before = jnp.where(row == SUB - 1, carry, pltpu.roll(hf, SUB - 1, 0))
                carry = hf[0:1, :]
            else:
                before = jnp.where(row == 0, carry, pltpu.roll(hf, 1, 0))
                carry = hf[SUB - 1:SUB, :]
            emit(rs, hf, before)
        return carry

    return lax.fori_loop(0, ngrp, grp, carry)


def _pad_start(ci):
    return pl.multiple_of(ci * CH + HALO * jnp.minimum(ci, 1), HALO)


def _conv_taps(ext, transpose=False):
    n = CH + 2 * HALO
    taps = []
    for k in range(CONV_W):
        off = CONV_LEFT - k if transpose else k - CONV_LEFT
        taps.append(ext[HALO:HALO + CH, :] if off == 0 else pltpu.roll(ext, (-off) % n, 0)[HALO:HALO + CH, :])
    return taps


def _lru_gates(xl, w4, b4, ls):
    pre = _dot(xl.astype(BF16), w4) + b4
    out = []
    for d in range(2):
        r = _sigmoid(pre[:, d * RB:(d + 1) * RB])
        i = _sigmoid(pre[:, (2 + d) * RB:(3 + d) * RB])
        la = LRU_C * r * ls[d:d + 1, :]
        a = jnp.exp(la)
        q = -jnp.tanh(la) * (1.0 + a * a)
        out.append((r, i, a, q))
    return out


def _rnn_specs(T):
    col = lambda n, *_: (0, n)
    return dict(
        xr=pl.BlockSpec((T, RB), lambda n, *_: (0, COL_XR // RB + n)),
        gr=pl.BlockSpec((T, RB), lambda n, *_: (0, COL_GR // RB + n)),
        act=pl.BlockSpec((T, RB), col),
        cw=pl.BlockSpec((CONV_W, RB), col), cb=pl.BlockSpec((1, RB), col),
        w4=pl.BlockSpec((None, RB, 4 * RB), lambda n, *_: (n, 0, 0)),
        b4=pl.BlockSpec((None, 1, 4 * RB), lambda n, *_: (n, 0, 0)),
        lam=pl.BlockSpec((2, RB), col))


PAD_ROWS = 3 * HALO


def _zero_pads(pad_ref, T):
    for r in (0, HALO + CTX, 2 * HALO + T):
        pad_ref[r:r + HALO, :] = jnp.zeros((HALO, RB), F32)


def _fill_padded(pad_ref, src_ref, T):
    _zero_pads(pad_ref, T)
    pad_ref[HALO:HALO + CTX, :] = src_ref[0:CTX, :].astype(F32)
    pad_ref[2 * HALO + CTX:2 * HALO + T, :] = src_ref[CTX:T, :].astype(F32)


def _pad_rows(ci):
    return pl.ds(pl.multiple_of(ci * CH + HALO + HALO * jnp.minimum(ci, 1), HALO), CH)


def _rnn_fwd(name, p, cw, cb, w4, b4, lam, T, carry=None):
    def kern(xr_ref, gr_ref, cw_ref, cb_ref, w4_ref, b4_ref, lam_ref,
             u_ref, a0, a1, yo_ref, hpf_ref, hpb_ref, r0_ref, r1_ref, i0_ref, i1_ref, xpad, b0, b1, y):
        _fill_padded(xpad, xr_ref, T)
        ls = _log_sigmoid(lam_ref[...])
        w4v, b4v, cwv, cbv = w4_ref[...], b4_ref[...], cw_ref[...], cb_ref[...]

        def chunk(ci, _):
            rows = pl.ds(pl.multiple_of(ci * CH, CH), CH)
            taps = _conv_taps(xpad[pl.ds(_pad_start(ci), CH + 2 * HALO), :])
            xl = cbv + sum(taps[k] * cwv[k:k + 1, :] for k in range(CONV_W))
            for d, (r, i, a, q) in enumerate(_lru_gates(xl, w4v, b4v, ls)):
                (a0, a1)[d][rows, :] = a
                (b0, b1)[d][rows, :] = jnp.sqrt(q) * (i * xl)
                (r0_ref, r1_ref)[d][rows, :] = r.astype(BF16)
                (i0_ref, i1_ref)[d][rows, :] = i.astype(BF16)
            return 0

        lax.fori_loop(0, T // CH, chunk, 0)
        zero = jnp.zeros((1, RB), F32)

        def emit_f(rs, hf, before):
            y[pl.ds(rs, SUB), :] = hf
            b0[pl.ds(rs, SUB), :] = before

        def emit_b(rs, hf, before):
            y[pl.ds(rs, SUB), :] += hf
            b1[pl.ds(rs, SUB), :] = before

        _scan_rows(a0, b0, 0, T, False, zero, emit_f)
        c = _scan_rows(a1, b1, 0, CTX, True, zero, emit_b)
        _scan_rows(a1, b1, CTX, T - CTX, True, c, emit_b)

        def finish(ci, _):
            rows = pl.ds(pl.multiple_of(ci * CH, CH), CH)
            yv = y[rows, :]
            u_ref[rows, :] = (yv * _gelu(gr_ref[rows, :].astype(F32))).astype(BF16)
            yo_ref[rows, :] = yv.astype(BF16)
            hpf_ref[rows, :] = b0[rows, :].astype(BF16)
            hpb_ref[rows, :] = b1[rows, :].astype(BF16)
            return 0

        lax.fori_loop(0, T // CH, finish, 0)

    sp = _rnn_specs(T)
    ci, ca, co, cs, cscr = _carry_args(carry)
    dts = [BF16, F32, F32] + [BF16] * 7
    res = pl.pallas_call(
        _carried(kern, carry, 7, 10, *_grid_ends((N_RNN_BLOCKS,))), name=name, grid=(N_RNN_BLOCKS,),
        in_specs=[sp["xr"], sp["gr"], sp["cw"], sp["cb"], sp["w4"], sp["b4"], sp["lam"]] + ci,
        out_specs=[sp["act"]] * 10 + co,
        out_shape=[_sds((T, D), dt) for dt in dts] + cs,
        scratch_shapes=[pltpu.VMEM((T + PAD_ROWS, RB), F32)] + [pltpu.VMEM((T, RB), F32)] * 3 + cscr,
        compiler_params=_params(),
    )(p, p, cw, cb, w4, b4, lam, *ca)
    return res if carry is None else (res[:10], res[10:])


def _rnn_bwd(name, p, du, saved, dp, cw, cb, w4, b4, lam, T, carry=None):
    def kern(xr_ref, gr_ref, du_ref, a0, a1, y_ref, hpf_ref, hpb_ref, r0_ref, r1_ref, i0_ref, i1_ref,
             cw_ref, cb_ref, w4_ref, b4_ref, lam_ref, dp_in,
             dp_ref, dcw_ref, dcb_ref, dw4_ref, db4_ref, dlam_ref,
             xpad, dxpad, c0, c1, dy):
        j = pl.program_id(1)

        @pl.when(j == 0)
        def _():
            scans(gr_ref, du_ref, a0, a1, y_ref, dp_ref, c0, c1, dy)

        @pl.when(j == 1)
        def _():
            gates(xr_ref, a0, a1, (hpf_ref, hpb_ref), (r0_ref, r1_ref), (i0_ref, i1_ref), cw_ref, cb_ref, w4_ref,
                  lam_ref, dp_ref, dcw_ref, dcb_ref, dw4_ref, db4_ref, dlam_ref, xpad, dxpad, c0, c1)

    def scans(gr_ref, du_ref, a0, a1, y_ref, dgr_ref, c0, c1, dy):
        def phase_a(ci, _):
            rows = pl.ds(pl.multiple_of(ci * CH, CH), CH)
            gr = gr_ref[rows, :].astype(F32)
            duv = du_ref[rows, :].astype(F32)
            dyv = duv * _gelu(gr)
            dgr_ref[rows, :] = (duv * y_ref[rows, :].astype(F32) * _gelu_grad(gr)).astype(BF16)
            dy[rows, :] = dyv
            c0[rows, :] = a0[rows, :] * dyv
            c1[rows, :] = a1[rows, :] * dyv
            return 0

        lax.fori_loop(0, T // CH, phase_a, 0)
        zero = jnp.zeros((1, RB), F32)

        def emit0(rs, hf, before):
            c0[pl.ds(rs, SUB), :] = dy[pl.ds(rs, SUB), :] + before

        def emit1(rs, hf, before):
            c1[pl.ds(rs, SUB), :] = dy[pl.ds(rs, SUB), :] + before

        _scan_rows(a0, c0, 0, T, True, zero, emit0)
        c = _scan_rows(a1, c1, CTX, T - CTX, False, zero, emit1)
        _scan_rows(a1, c1, 0, CTX, False, c, emit1)

    def gates(xr_ref, a0, a1, hp_refs, r_refs, i_refs, cw_ref, cb_ref, w4_ref, lam_ref,
              dxr_ref, dcw_ref, dcb_ref, dw4_ref, db4_ref, dlam_ref, xpad, dxpad, c0, c1):
        _fill_padded(xpad, xr_ref, T)
        _zero_pads(dxpad, T)
        lam_v = lam_ref[...]
        ls = _log_sigmoid(lam_v)
        w4v, cwv, cbv = w4_ref[...], cw_ref[...], cb_ref[...]

        def conv_chunk(ci):
            taps = _conv_taps(xpad[pl.ds(_pad_start(ci), CH + 2 * HALO), :])
            return taps, cbv + sum(taps[k] * cwv[k:k + 1, :] for k in range(CONV_W))

        dw4_ref[...] = jnp.zeros(dw4_ref.shape, F32)
        db4_ref[...] = jnp.zeros(db4_ref.shape, F32)
        dlam_ref[...] = jnp.zeros(dlam_ref.shape, F32)
        dcw_ref[...] = jnp.zeros(dcw_ref.shape, F32)
        dcb_ref[...] = jnp.zeros(dcb_ref.shape, F32)

        def phase_c(ci, _):
            base = pl.multiple_of(ci * CH, CH)
            rows = pl.ds(base, CH)
            _, xl = conv_chunk(ci)
            dxl = jnp.zeros((CH, RB), F32)
            dpre_a, dpre_x, dls = [], [], []
            for d in range(2):
                a = (a0, a1)[d][rows, :]
                r = r_refs[d][rows, :].astype(F32)
                i = i_refs[d][rows, :].astype(F32)
                q = -jnp.tanh(LRU_C * r * ls[d:d + 1, :]) * (1.0 + a * a)
                g = (c0, c1)[d][rows, :]
                hp = hp_refs[d][rows, :].astype(F32)
                gm = g * jnp.sqrt(q)
                di = gm * xl
                dxl = dxl + gm * i
                dla = a * (g * hp - a * (g * (i * xl)) * lax.rsqrt(q))
                dr = dla * (LRU_C * ls[d:d + 1, :])
                dls.append(_colsum(dla * (LRU_C * r)))
                dpre_a.append(dr * r * (1.0 - r))
                dpre_x.append(di * i * (1.0 - i))
            dpre = jnp.concatenate(dpre_a + dpre_x, axis=1)
            dpre_b = dpre.astype(BF16)
            dxl = dxl + _dot(dpre_b, w4v, NT)
            dw4_ref[...] += _dot(xl.astype(BF16), dpre_b, TN)
            db4_ref[...] += _colsum(dpre)
            dlam_ref[...] += jnp.concatenate(dls, axis=0)
            dcb_ref[...] += _colsum(dxl)
            dxpad[_pad_rows(ci), :] = dxl
            return 0

        lax.fori_loop(0, T // CH, phase_c, 0)
        dlam_ref[...] = dlam_ref[...] * _sigmoid(-lam_v)

        def phase_d(ci, _):
            base = pl.multiple_of(ci * CH, CH)
            rows = pl.ds(base, CH)
            xtaps, _ = conv_chunk(ci)
            dtaps = _conv_taps(dxpad[pl.ds(_pad_start(ci), CH + 2 * HALO), :], transpose=True)
            dxl = dxpad[_pad_rows(ci), :]
            dxr_ref[rows, :] = sum(dtaps[k] * cwv[k:k + 1, :] for k in range(CONV_W)).astype(BF16)
            dcw_ref[...] += jnp.concatenate([_colsum(dxl * xtaps[k]) for k in range(CONV_W)], axis=0)
            return 0

        lax.fori_loop(0, T // CH, phase_d, 0)

    sp = _rnn_specs(T)
    dp_spec = pl.BlockSpec((T, RB), lambda n, j: (0, COL_GR // RB + n - j * (COL_GR - COL_XR) // RB))
    ci, ca, co, cs, cscr = _carry_args(carry)
    n_in = 3 + len(saved) + 5 + 1
    res = pl.pallas_call(
        _carried(kern, carry, n_in, 6, *_grid_ends((N_RNN_BLOCKS, 2))), name=name, grid=(N_RNN_BLOCKS, 2),
        in_specs=[sp["xr"], sp["gr"]] + [sp["act"]] * (1 + len(saved)) + [sp["cw"], sp["cb"], sp["w4"], sp["b4"],
                                                                           sp["lam"], ANY] + ci,
        out_specs=[dp_spec, sp["cw"], sp["cb"], sp["w4"], sp["b4"], sp["lam"]] + co,
        out_shape=[_sds((T, DP_W), BF16), _sds((CONV_W, D), F32), _sds((1, D), F32),
                   _sds((N_RNN_BLOCKS, RB, 4 * RB), F32), _sds((N_RNN_BLOCKS, 1, 4 * RB), F32), _sds((2, D), F32)] + cs,
        scratch_shapes=[pltpu.VMEM((T + PAD_ROWS, RB), F32)] * 2 + [pltpu.VMEM((T, RB), F32)] * 3 + cscr,
        input_output_aliases={n_in - 1: 0},
        compiler_params=_params(("arbitrary", "arbitrary")),
    )(p, p, du, *saved, cw, cb, w4, b4, lam, dp, *ca)
    return res if carry is None else (res[:6], res[6:])


class _Plan:
    def __init__(self, shards, Ws):
        L = len(Ws)
        self.shards, self.Ws = shards, Ws
        self.Gs = [None] * L
        self.slots = [dict() for _ in range(L)]
        self.gate_slots = [None] * L
        self.table = {}
        for l in range(L):
            t = f"l{l}_"
            self.table[t + "rnn_fwd"] = [("gather", l, k) for k in ("wffn_in_t", "wo_rnn", "wo_attn", "wout")]
            if l + 1 < L:
                self.table[t + "attn_lat_fwd"] = [("gather", l + 1, "win_t")]
                self.table[t + "ffn_in"] = [("gather", l, "wffn_out")]
            else:
                self.table[t + "attn_lat_fwd"] = [("gather", l, "wffn_out")]
            self.table[t + "ffn_in_dx"] = [("scatter", l, "wffn_out")]
            self.table[t + "attn_lat_bwd"] = [("scatter", l, "wffn_in_t")]
            self.table[t + "proj_dx"] = [("scatter", l, "win_t_a")]
            self.table[t + "rnn_bwd"] = ([("scatter", l, k) for k in ("wout", "wo_attn", "wo_rnn")]
                                         + ([("scatter", l + 1, "win_t_b"), ("gates", l + 1, "w4")] if l + 1 < L else []))
        self.table["l0_proj_dw_b"] = [("gates", 0, "w4")]

    def carry(self, name):
        jobs = []
        for kind, l, k in self.table.get(name, []):
            if kind == "gather":
                jobs.append(("gather", self.shards[l][k]))
            elif kind == "scatter":
                jobs.append(("scatter", self.Gs[l][k].reshape(N_DEV, -1, self.Gs[l][k].shape[-1])))
            else:
                jobs.append(("gather", self.Gs[l]["w4"].reshape(N_RNN_BLOCKS * RB, 4 * RB).astype(BF16)))
        return _Carry(jobs) if jobs else None

    def done(self, name, got):
        for (kind, l, k), res in zip(self.table[name], got):
            if kind == "gather":
                self.Ws[l][k] = res.reshape(-1, D)
            elif kind == "scatter":
                self.slots[l][k] = res
            else:
                self.gate_slots[l] = res


def _run(X, fn, name, *args, **kw):
    carry = None if X is None else X.carry(name)
    if carry is None:
        return fn(name, *args, **kw)
    out, got = fn(name, *args, carry=carry, **kw)
    X.done(name, got)
    return out


def _layer_fwd(l, xa, h, W, rope, S, nxt, X=None):
    T = xa.shape[0]
    tag = f"l{l}_"
    cos, sin, bias = rope
    p = _run(X, _mm_act, tag + "proj", h, W["win_t"], "NT", BF16)
    u, *rnn_saved = _run(X, _rnn_fwd, tag + "rnn_fwd", p, W["cw"], W["cb"], W["w4"], W["b4"], W["lam"], T)
    qa, kp, vp, kc, vc = _qkv_prep(tag + "qkv_prep", p, cos, sin, S)
    o_all = _attn_fwd(tag + "attn_ctx_fwd", qa, kc, vc, W["sink4"], S)
    o_all = _run(X, _attn_fwd, tag + "attn_lat_fwd", qa, kc, vc, W["sink4"], S, band=(kp, vp, bias), prev=o_all)
    ya, yb, z, m, x1, h2 = _out_fused(tag + "out", p, u, o_all, xa, W["wo_rnn"], W["wo_attn"], W["wout"],
                                      W["g_mix_post"], W["mod"], W["g_ffn_pre"])
    fg, fu, s = _run(X, _ffn_in_fused, tag + "ffn_in", h2, W["wffn_in_t"])
    e, *out = _ffn_out_fused(tag + "ffn_out", s, W["wffn_out"], x1, W["g_ffn_post"], W["mod"], nxt)
    saved = dict(xa=xa, h=h, p=p, u=u, rnn=rnn_saved, qa=qa, kp=kp, vp=vp, kc=kc, vc=vc, o_all=o_all,
                 ya=ya, yb=yb, z=z, m=m, x1=x1, h2=h2, fg=fg, fu=fu, s=s, e=e)
    return saved, out


def _layer_bwd(l, dx2, A, W, rope, S, X=None, loss_of=None):
    T = A["xa"].shape[0]
    tag = f"l{l}_"
    cos, sin, bias = rope
    G = {}
    if X is not None:
        X.Gs[l] = G
    if loss_of is None:
        de, df, dga2, G["g_ffn_post"] = _ffn_bwd_fused(tag + "ffn_bwd", A["fg"], A["fu"], W["wffn_out"],
                                                       head=(dx2, A["e"], W["g_ffn_post"], W["mod"]))
    else:
        dx2, de, dga2, G["g_ffn_post"], G["sq"] = _loss_resid_bwd(tag + "loss_ffn_resid_bwd", *loss_of, A["e"],
                                                                  W["g_ffn_post"], W["mod"], GA2)
        df, = _ffn_bwd_fused(tag + "ffn_bwd", A["fg"], A["fu"], W["wffn_out"], de=de)
    G["wffn_out"] = _mm_wgrad(tag + "ffn_out_dw", A["s"], de)
    dx1, dm, dsh2, dsc2, G["g_ffn_pre"], dga1, G["g_mix_post"] = _run(
        X, _ffn_in_bwd_fused, tag + "ffn_in_dx", df, W["wffn_in_t"], A["x1"], dx2, A["m"], W["g_ffn_pre"], W["mod"],
        W["g_mix_post"])
    G["wffn_in_t"] = _run(X, _mm_wgrad, tag + "ffn_in_dw", df, A["h2"])
    G["wout"] = _mm_wgrad(tag + "out_dw", A["z"], dm)
    dya, dyb, dgl, du, do = _out_bwd_fused(tag + "out_dx", dm, W["wout"], W["wo_rnn"], W["wo_attn"], A["p"], A["ya"],
                                           A["yb"])
    G["wo_attn"] = _mm_wgrad(tag + "o_attn_dw", A["o_all"], dyb)
    G["wo_rnn"] = _mm_wgrad(tag + "o_rnn_dw", A["u"], dya)
    dp, dkc_c, dvc_c, dsink_c = _attn_bwd(tag + "attn_ctx_bwd", A["qa"], A["kc"], A["vc"], W["sink4"], A["o_all"], do, S)
    dp, dkc_l, dvc_l, dsink_l, dkp, dvp = _run(
        X, _attn_bwd, tag + "attn_lat_bwd", A["qa"], A["kc"], A["vc"], W["sink4"], A["o_all"], do, S,
        band=(A["kp"], A["vp"], bias, cos, sin), prev_dq=dp)
    G["sink4"] = dsink_c + dsink_l
    dp = _dkv_assemble(tag + "dkv", dp, dkp, dvp, dkc_l, dvc_l, dkc_c, dvc_c, cos, sin, S)
    dp, G["cw"], G["cb"], G["w4"], G["b4"], G["lam"] = _run(
        X, _rnn_bwd, tag + "rnn_bwd", A["p"], du, A["rnn"], dp, W["cw"], W["cb"], W["w4"], W["b4"], W["lam"], T)
    proj_dx = (_proj_bwd_fused, tag + "proj_dx", dp, dgl, W["win_t"], A["xa"], dx1, W["g_mix_pre"], W["mod"])
    if X is not None:
        G["win_t_a"] = _proj_wgrad(tag + "proj_dw_a", dp, dgl, A["h"][:, :D // 2])
        dxa, dsh1, dsc1, G["g_mix_pre"] = _run(X, *proj_dx)
        G["win_t_b"] = _run(X, _proj_wgrad, tag + "proj_dw_b", dp, dgl, A["h"][:, D // 2:])
    else:
        dxa, dsh1, dsc1, G["g_mix_pre"] = _run(X, *proj_dx)
        G["win_t"] = _proj_wgrad(tag + "proj_dw", dp, dgl, A["h"])
    G["mod"] = jnp.concatenate([dsh1, dsc1, dga1, dsh2, dsc2, dga2], axis=1)
    return dxa, G


def _local_step(xa, target, Ws, S, X=None):
    rope = (*_rope_tables(S), _band_bias(S))
    L = len(Ws)
    h = _normmod_fwd("l0_mix_norm", xa, Ws[0]["g_mix_pre"], Ws[0]["mod"], SH1, SC1)
    saved = []
    x = xa
    for l in range(L):
        nxt = (Ws[l + 1]["g_mix_pre"], Ws[l + 1]["mod"]) if l + 1 < L else None
        A, out = _layer_fwd(l, x, h, Ws[l], rope, S, nxt, X)
        saved.append(A)
        if l + 1 < L:
            x, h = out
    Gs = [None] * L
    dx = None
    for l in reversed(range(L)):
        dx, Gs[l] = _layer_bwd(l, dx, saved[l], Ws[l], rope, S, X, loss_of=(out[0], target) if l == L - 1 else None)
    return Gs[L - 1]["sq"], dx, Gs


MESH = pl.DeviceIdType.MESH


def _place():
    return lax.axis_index("x"), lax.axis_index("y"), lax.axis_index("c")


def _lin(px, py, pc):
    return 4 * px + 2 * py + pc


def _allgather_small(name, blk):
    m, n = blk.shape

    def body(x_ref, out_ref, send_sems, recv_sems, local_sem):
        x, y, c = _place()
        me, sibling = (x, y, c), (x, y, 1 - c)
        chips = [(1 - x, y), (x, 1 - y), (1 - x, 1 - y)]

        def copy(k, block, to, src=None):
            dst = out_ref.at[_lin(*block)]
            return pltpu.make_async_remote_copy(src_ref=dst if src is None else src, dst_ref=dst,
                                                send_sem=send_sems.at[k], recv_sem=recv_sems.at[k],
                                                device_id=to, device_id_type=MESH)

        mine = pltpu.make_async_copy(x_ref, out_ref.at[_lin(*me)], local_sem)
        mine.start()
        first = [copy(0, me, sibling, src=x_ref)]
        first += [copy(1 + j, me, (*chip, c), src=x_ref) for j, chip in enumerate(chips)]
        for cp in first:
            cp.start()
        passed = [copy(4 + j, (*chip, c), sibling) for j, chip in enumerate(chips)]
        for j, chip in enumerate(chips):
            copy(1 + j, (*chip, c), me).wait_recv()
            passed[j].start()
        copy(0, sibling, me).wait_recv()
        for j, chip in enumerate(chips):
            copy(4 + j, (*chip, 1 - c), me).wait_recv()
        for cp in first + passed:
            cp.wait_send()
        mine.wait()

    return pl.pallas_call(
        body, name=name, out_shape=_sds((N_DEV, m, n), blk.dtype),
        in_specs=[pl.BlockSpec(memory_space=pltpu.VMEM)], out_specs=pl.BlockSpec(memory_space=pltpu.VMEM),
        scratch_shapes=[pltpu.SemaphoreType.DMA((7,)), pltpu.SemaphoreType.DMA((7,)), pltpu.SemaphoreType.DMA],
        compiler_params=pltpu.CompilerParams(vmem_limit_bytes=VMEM_LIMIT),
    )(blk)


def _allgather_hbm(name, shards):
    na = len(shards)

    def body(*refs):
        ins, outs = refs[:na], refs[na:2 * na]
        send_sems, recv_sems, local_sems = refs[2 * na:]
        x, y, c = _place()
        me, sibling = (x, y, c), (x, y, 1 - c)
        chips = [(1 - x, y), (x, 1 - y), (1 - x, 1 - y)]

        def copy(a, k, block, to, from_input=False):
            dst = outs[a].at[_lin(*block)]
            return pltpu.make_async_remote_copy(src_ref=ins[a] if from_input else dst, dst_ref=dst,
                                                send_sem=send_sems.at[a, k], recv_sem=recv_sems.at[a, k],
                                                device_id=to, device_id_type=MESH)

        mine = [pltpu.make_async_copy(ins[a], outs[a].at[_lin(*me)], local_sems.at[a]) for a in range(na)]
        for cp in mine:
            cp.start()
        first = []
        for a in range(na):
            first.append(copy(a, 0, me, sibling, True))
            first += [copy(a, 1 + j, me, (*chip, c), True) for j, chip in enumerate(chips)]
        for cp in first:
            cp.start()
        passed = []
        for j, chip in enumerate(chips):
            for a in range(na):
                copy(a, 1 + j, (*chip, c), me).wait_recv()
                fwd = copy(a, 4 + j, (*chip, c), sibling)
                fwd.start()
                passed.append(fwd)
        for a in range(na):
            copy(a, 0, sibling, me).wait_recv()
            for j, chip in enumerate(chips):
                copy(a, 4 + j, (*chip, 1 - c), me).wait_recv()
        for cp in first + passed:
            cp.wait_send()
        for cp in mine:
            cp.wait()

    return pl.pallas_call(
        body, name=name, out_shape=[_sds((N_DEV, *s.shape), s.dtype) for s in shards],
        in_specs=[ANY] * na, out_specs=[ANY] * na,
        scratch_shapes=[pltpu.SemaphoreType.DMA((na, 7)), pltpu.SemaphoreType.DMA((na, 7)),
                        pltpu.SemaphoreType.DMA((na,))],
    )(*shards)


def _exchange_shards(name, grads, L):
    nw = len(grads)
    na = nw * L
    flat = [g for per_layer in grads for g in per_layer]

    def body(*refs):
        ins, outs = refs[:na], refs[na:na + nw]
        send_sems, recv_sems, local_sems = refs[na + nw:]
        x, y, c = _place()
        me = _lin(x, y, c)
        peers = [(x ^ ((k + 1) >> 2 & 1), y ^ ((k + 1) >> 1 & 1), c ^ ((k + 1) & 1)) for k in range(7)]

        def copy(a, k, src_blk, dst_blk):
            return pltpu.make_async_remote_copy(src_ref=ins[a].at[src_blk], dst_ref=outs[a // L].at[a % L, dst_blk],
                                                send_sem=send_sems.at[a, k], recv_sem=recv_sems.at[a, k],
                                                device_id=peers[k], device_id_type=MESH)

        mine = [pltpu.make_async_copy(ins[a].at[me], outs[a // L].at[a % L, me], local_sems.at[a]) for a in range(na)]
        for cp in mine:
            cp.start()
        sent = [copy(a, k, _lin(*peers[k]), me) for a in range(na) for k in range(7)]
        for cp in sent:
            cp.start()
        for a in range(na):
            for k in range(7):
                copy(a, k, me, _lin(*peers[k])).wait_recv()
        for cp in sent:
            cp.wait_send()
        for cp in mine:
            cp.wait()

    return pl.pallas_call(
        body, name=name, out_shape=[_sds((L, *per_layer[0].shape), per_layer[0].dtype) for per_layer in grads],
        in_specs=[ANY] * na, out_specs=[ANY] * nw,
        scratch_shapes=[pltpu.SemaphoreType.DMA((na, 7)), pltpu.SemaphoreType.DMA((na, 7)),
                        pltpu.SemaphoreType.DMA((na,))],
    )(*flat)


MOD_ROWS = 16
MOD_SHARD = 6 * D // N_DEV
HI = lax.Precision.HIGHEST


def _mod_fwd(name, c9, w_mod, b_shard):
    L = w_mod.shape[0]

    def kern(c_ref, w_ref, b_ref, o_ref):
        o_ref[...] = lax.dot_general(_silu(c_ref[...]), w_ref[...], NN, precision=HI,
                                     preferred_element_type=F32) + b_ref[...]

    return pl.pallas_call(
        kern, name=name, grid=(L,),
        in_specs=[_full_spec(c9.shape), pl.BlockSpec((None, D, MOD_SHARD), lambda l: (l, 0, 0)),
                  pl.BlockSpec((None, 1, MOD_SHARD), lambda l: (l, 0, 0))],
        out_specs=pl.BlockSpec((None, MOD_ROWS, MOD_SHARD), lambda l: (l, 0, 0)),
        out_shape=_sds((L, MOD_ROWS, MOD_SHARD), F32), compiler_params=_params(),
    )(c9, w_mod, b_shard)


def _mod_bwd(name, c9, w_mod, dmod_all, dmod_cols):
    L = w_mod.shape[0]

    def rows9(ref, l):
        own = jnp.concatenate([ref[j, 2 * l + 1:2 * l + 2, :] for j in range(N_DEV)], axis=0)
        ctx = ref[0, 2 * l:2 * l + 1, :]
        for j in range(1, N_DEV):
            ctx = ctx + ref[j, 2 * l:2 * l + 1, :]
        return own, ctx

    def kern(c_ref, w_ref, all_ref, cols_ref, gw_ref, gb_ref, gc_ref):
        l = pl.program_id(0)
        for ll in range(L):
            @pl.when(l == ll)
            def _():
                own, ctx = rows9(all_ref, ll)
                gb_ref[...] = _colsum(own) + ctx
                own_s, ctx_s = rows9(cols_ref, ll)
                r16 = jnp.concatenate([own_s, ctx_s, jnp.zeros((MOD_ROWS - N_DEV - 1, MOD_SHARD), F32)], axis=0)
                gw_ref[...] = lax.dot_general(_silu(c_ref[...]), r16, TN, precision=HI, preferred_element_type=F32)
                part = lax.dot_general(r16, w_ref[...], NT, precision=HI,
                                       preferred_element_type=F32)[N_DEV:N_DEV + 1, :]
                if ll == 0:
                    gc_ref[...] = part
                else:
                    gc_ref[...] += part

    return pl.pallas_call(
        kern, name=name, grid=(L,),
        in_specs=[_full_spec(c9.shape), pl.BlockSpec((None, D, MOD_SHARD), lambda l: (l, 0, 0)),
                  _full_spec(dmod_all.shape), _full_spec(dmod_cols.shape)],
        out_specs=[pl.BlockSpec((None, D, MOD_SHARD), lambda l: (l, 0, 0)),
                   pl.BlockSpec((None, 1, 6 * D), lambda l: (l, 0, 0)), _full_spec((1, D))],
        out_shape=[_sds((L, D, MOD_SHARD), F32), _sds((L, 1, 6 * D), F32), _sds((1, D), F32)],
        compiler_params=_params(),
    )(c9, w_mod, dmod_all, dmod_cols)


_BC1 = 1.0 - ADAM_B1 ** ADAM_STEP
_BC2 = 1.0 - ADAM_B2 ** ADAM_STEP


def _adamw_vals(w, g, m, v):
    m = ADAM_B1 * m + (1.0 - ADAM_B1) * g
    v = ADAM_B2 * v + (1.0 - ADAM_B2) * (g * g)
    delta = -ADAM_LR * ((m / _BC1) / (jnp.sqrt(v / _BC2) + ADAM_EPS) + ADAM_WD * w)
    return delta, m, v


def _adamw(name, w, g, m, v, tile):
    R, C = w.shape
    blk = ((tile, C), lambda i: (i, 0))

    def body(i, ins, ps, outs, acc):
        d, mm, vv = _adamw_vals(ins[0][...], ins[1][...], ins[2][...], ins[3][...])
        outs[0][...] = d
        outs[1][...] = mm
        outs[2][...] = vv

    return _ew(name, body, R // tile, [(a, *blk) for a in (w, g, m, v)], [], [(_sds((R, C), F32), *blk)] * 3)


def _sum_slots(ref):
    g = ref[0].astype(F32)
    for j in range(1, N_DEV):
        g = g + ref[j].astype(F32)
    return g


def _adamw_slots(name, slots, shape, tile, wmv=None):
    L, R, C = shape
    n = R // tile
    spec = pl.BlockSpec((None, tile, C), lambda l, i: (l, i, 0))
    pieces = [s if isinstance(s, (list, tuple)) else [s] for s in slots]
    layer_of = [ll for ll, ps in enumerate(pieces) for _ in ps]
    flat = [p for ps in pieces for p in ps]
    wmv = list(wmv or [])

    def slot_spec(ll, cols):
        return pl.BlockSpec((N_DEV, tile, cols),
                            lambda l, i: (0, jnp.where(l == ll, i, jnp.where(l < ll, 0, n - 1)), 0))

    def kern(*refs):
        s_refs = refs[:len(flat)]
        rest = refs[len(flat):]
        l = pl.program_id(0)
        for ll in range(L):
            @pl.when(l == ll)
            def _():
                parts = [_sum_slots(r) for r, lr in zip(s_refs, layer_of) if lr == ll]
                g = parts[0] if len(parts) == 1 else jnp.concatenate(parts, axis=1)
                if wmv:
                    w_ref, m_ref, v_ref, g_ref, d_ref, mo_ref, vo_ref = rest
                    d_ref[...], mo_ref[...], vo_ref[...] = _adamw_vals(w_ref[...], g, m_ref[...], v_ref[...])
                else:
                    g_ref, = rest
                g_ref[...] = g

    n_out = 4 if wmv else 1
    return pl.pallas_call(
        kern, name=name, grid=(L, n),
        in_specs=[slot_spec(ll, p.shape[-1]) for ll, p in zip(layer_of, flat)] + [spec] * len(wmv),
        out_specs=[spec] * n_out, out_shape=[_sds((L, R, C), F32)] * n_out,
        compiler_params=_params(("arbitrary", "arbitrary")),
    )(*flat, *wmv)


def _sum_blocks(name, blocks):
    _, R, C = blocks.shape

    def kern(b_ref, o_ref):
        o_ref[...] = _sum_slots(b_ref)

    return pl.pallas_call(kern, name=name, in_specs=[_full_spec(blocks.shape)], out_specs=_full_spec((R, C)),
                          grid=(1,), out_shape=_sds((R, C), F32), compiler_params=_params())(blocks)


BIG = ("win_t", "wo_rnn", "wo_attn", "wout", "wffn_in_t", "wffn_out")
BIG_SRC = ("w_in", "w_o_rnn", "w_o_attn", "w_out", "w_ffn_in", "w_ffn_out")
BIG_T = (True, False, False, False, True, False)
BIG_TILE = (176, 128, 128, 128, 176, 176)


def _chan_full(g8):
    return jnp.transpose(g8, (1, 0, 2)).reshape(g8.shape[1], D)


def kernel(x, c, ctx, c_ctx, w_mod, b_mod, g_mix_pre, g_mix_post, g_ffn_pre, g_ffn_post, w_in, conv_w, conv_b, lru_wa, lru_ba, lru_wx, lru_bx, lru_lam, attn_sink, w_o_rnn, w_o_attn, w_out, w_ffn_in, w_ffn_out, loss_target, m_c_ctx, m_w_mod, m_b_mod, m_g_mix_pre, m_g_mix_post, m_g_ffn_pre, m_g_ffn_post, m_w_in, m_conv_w, m_conv_b, m_lru_wa, m_lru_ba, m_lru_wx, m_lru_bx, m_lru_lam, m_attn_sink, m_w_o_rnn, m_w_o_attn, m_w_out, m_w_ffn_in, m_w_ffn_out, v_c_ctx, v_w_mod, v_b_mod, v_g_mix_pre, v_g_mix_post, v_g_ffn_pre, v_g_ffn_post, v_w_in, v_conv_w, v_conv_b, v_lru_wa, v_lru_ba, v_lru_wx, v_lru_bx, v_lru_lam, v_attn_sink, v_w_o_rnn, v_w_o_attn, v_w_out, v_w_ffn_in, v_w_ffn_out):
    P = dict(c_ctx=c_ctx, w_mod=w_mod, b_mod=b_mod, g_mix_pre=g_mix_pre, g_mix_post=g_mix_post, g_ffn_pre=g_ffn_pre,
             g_ffn_post=g_ffn_post, w_in=w_in, conv_w=conv_w, conv_b=conv_b, lru_wa=lru_wa, lru_ba=lru_ba,
             lru_wx=lru_wx, lru_bx=lru_bx, lru_lam=lru_lam, attn_sink=attn_sink, w_o_rnn=w_o_rnn, w_o_attn=w_o_attn,
             w_out=w_out, w_ffn_in=w_ffn_in, w_ffn_out=w_ffn_out)
    Mo = dict(c_ctx=m_c_ctx, w_mod=m_w_mod, b_mod=m_b_mod, g_mix_pre=m_g_mix_pre, g_mix_post=m_g_mix_post,
              g_ffn_pre=m_g_ffn_pre, g_ffn_post=m_g_ffn_post, w_in=m_w_in, conv_w=m_conv_w, conv_b=m_conv_b,
              lru_wa=m_lru_wa, lru_ba=m_lru_ba, lru_wx=m_lru_wx, lru_bx=m_lru_bx, lru_lam=m_lru_lam,
              attn_sink=m_attn_sink, w_o_rnn=m_w_o_rnn, w_o_attn=m_w_o_attn, w_out=m_w_out, w_ffn_in=m_w_ffn_in,
              w_ffn_out=m_w_ffn_out)
    Vo = dict(c_ctx=v_c_ctx, w_mod=v_w_mod, b_mod=v_b_mod, g_mix_pre=v_g_mix_pre, g_mix_post=v_g_mix_post,
              g_ffn_pre=v_g_ffn_pre, g_ffn_post=v_g_ffn_post, w_in=v_w_in, conv_w=v_conv_w, conv_b=v_conv_b,
              lru_wa=v_lru_wa, lru_ba=v_lru_ba, lru_wx=v_lru_wx, lru_bx=v_lru_bx, lru_lam=v_lru_lam,
              attn_sink=v_attn_sink, w_o_rnn=v_w_o_rnn, w_o_attn=v_w_o_attn, w_out=v_w_out, w_ffn_in=v_w_ffn_in,
              w_ffn_out=v_w_ffn_out)
    L = w_in.shape[0]
    S = x.shape[1]
    me = _lin(*_place())

    small = jnp.concatenate([c.reshape(8, 128), conv_w.reshape(L * CONV_W, 128), lru_ba.reshape(2 * L, 128),
                             lru_bx.reshape(2 * L, 128), lru_lam.reshape(2 * L, 128), jnp.zeros((4, 128), F32)], axis=0)
    small_all = _allgather_small("ag_small", small)
    c_all = small_all[:, 0:8].reshape(N_DEV, D)
    conv_w_f = _chan_full(small_all[:, 8:16]).reshape(L, CONV_W, D)
    lru_ba_f = _chan_full(small_all[:, 16:20]).reshape(L, 2, D)
    lru_bx_f = _chan_full(small_all[:, 20:24]).reshape(L, 2, D)
    lru_lam_f = _chan_full(small_all[:, 24:28]).reshape(L, 2, D)

    c9 = jnp.concatenate([c_all, c_ctx[None], jnp.zeros((MOD_ROWS - N_DEV - 1, D), F32)], axis=0)
    b_shard = lax.dynamic_slice_in_dim(b_mod, me * MOD_SHARD, MOD_SHARD, axis=1)[:, None, :]
    mod_part = _mod_fwd("mod_fwd", c9, w_mod, b_shard)
    mod_all = _allgather_small("ag_mod", mod_part.reshape(L * MOD_ROWS, MOD_SHARD))
    mod_all = jnp.transpose(mod_all.reshape(N_DEV, L, MOD_ROWS, MOD_SHARD), (1, 2, 0, 3)).reshape(L, MOD_ROWS, 6 * D)
    own_row = lax.dynamic_index_in_dim(mod_all, me, axis=1, keepdims=False)
    modrows = jnp.stack([mod_all[:, N_DEV], own_row], axis=1)

    shards = [{k: (P[src][l].T if tr else P[src][l]).astype(BF16) for k, src, tr in zip(BIG, BIG_SRC, BIG_T)}
              for l in range(L)]
    win0, = _allgather_hbm("ag_w_in0", [shards[0]["win_t"]])
    Ws = []
    for l in range(L):
        W = {"win_t": win0.reshape(-1, D)} if l == 0 else {}
        W.update(
            cw=conv_w_f[l], cb=conv_b[l][None],
            w4=jnp.concatenate([lru_wa[l, 0], lru_wa[l, 1], lru_wx[l, 0], lru_wx[l, 1]], axis=-1).astype(BF16),
            b4=jnp.concatenate([lru_ba_f[l, 0].reshape(N_RNN_BLOCKS, 1, RB), lru_ba_f[l, 1].reshape(N_RNN_BLOCKS, 1, RB),
                                lru_bx_f[l, 0].reshape(N_RNN_BLOCKS, 1, RB), lru_bx_f[l, 1].reshape(N_RNN_BLOCKS, 1, RB)],
                               axis=-1),
            lam=lru_lam_f[l], sink4=jnp.broadcast_to(attn_sink[l].reshape(N_KV, Q_PER_KV, 1), (N_KV, Q_PER_KV, HEAD)),
            g_mix_pre=g_mix_pre[l][None], g_mix_post=g_mix_post[l][None], g_ffn_pre=g_ffn_pre[l][None],
            g_ffn_post=g_ffn_post[l][None], mod=modrows[l])
        Ws.append(W)

    xa = jnp.concatenate([ctx[0], x[0]], axis=0)
    plan = _Plan(shards, Ws)
    sq, dxa, Gs = _local_step(xa, loss_target[0], Ws, S, plan)
    loss_part = ((0.5 / D) * jnp.sum(sq)).reshape(1, 1)
    grad_x = dxa[CTX:][None]

    dmod = jnp.concatenate([Gs[l]["mod"] for l in range(L)] + [jnp.zeros((8 - 2 * L, 6 * D), F32)], axis=0)
    dmod_all = _allgather_small("ag_dmod", dmod)
    dmod_cols = lax.dynamic_slice_in_dim(dmod_all, me * MOD_SHARD, MOD_SHARD, axis=2)
    g_w_mod, g_b_mod, dsc_part = _mod_bwd("mod_bwd", c9, w_mod, dmod_all, dmod_cols)
    g_b_mod = g_b_mod[:, 0]

    def rows(name, shape):
        return jnp.concatenate([Gs[l][name].reshape(shape) for l in range(L)], axis=0)

    b4g = [Gs[l]["b4"].reshape(N_RNN_BLOCKS, 4, RB) for l in range(L)]
    sink_row = jnp.concatenate([Gs[l]["sink4"][:, :, 0].reshape(1, N_Q) for l in range(L)]
                               + [loss_part, jnp.zeros((1, D - L * N_Q - 1), F32)], axis=1)
    small_g = jnp.concatenate(
        [rows("g_mix_pre", (1, D)), rows("g_mix_post", (1, D)), rows("g_ffn_pre", (1, D)), rows("g_ffn_post", (1, D)),
         rows("cb", (1, D)), rows("cw", (CONV_W, D))]
        + [b4g[l][:, d].reshape(1, D) for l in range(L) for d in range(2)]
        + [b4g[l][:, 2 + d].reshape(1, D) for l in range(L) for d in range(2)]
        + [rows("lam", (2, D)), sink_row, dsc_part], axis=0)
    n_small = small_g.shape[0]
    small_tot = _sum_blocks("sum_small", _allgather_small("ag_small_grads", small_g))
    o = 0
    G = {}
    for name in ("g_mix_pre", "g_mix_post", "g_ffn_pre", "g_ffn_post", "conv_b"):
        G[name] = small_tot[o:o + L]
        o += L
    G["conv_w"] = small_tot[o:o + L * CONV_W].reshape(L, CONV_W, D)
    o += L * CONV_W
    for name in ("lru_ba", "lru_bx", "lru_lam"):
        G[name] = small_tot[o:o + 2 * L].reshape(L, 2, D)
        o += 2 * L
    G["attn_sink"] = small_tot[o, :L * N_Q].reshape(L, N_Q)
    loss = small_tot[o, L * N_Q]
    sg = jax.nn.sigmoid(c_ctx)
    G["c_ctx"] = small_tot[o + 1] * (sg * (1.0 + c_ctx * (1.0 - sg)))
    G["b_mod"] = g_b_mod
    G["w_mod"] = g_w_mod

    last_slots, = _exchange_shards("exchange_w_in0", [[Gs[0]["win_t_b"].reshape(N_DEV, -1, D // 2)]], 1)
    plan.slots[0]["win_t_b"] = last_slots[0]
    for l in range(L):
        plan.slots[l]["win_t"] = [plan.slots[l]["win_t_a"], plan.slots[l]["win_t_b"]]

    out_g, out_d, out_m, out_v = {}, {}, {}, {}

    def put(name, res, shape=None):
        g, d, m, v = res
        for dst, val in ((out_g, g), (out_d, d), (out_m, m), (out_v, v)):
            dst[name] = val if shape is None else val.reshape(shape)

    for k, src, tr, tile in zip(BIG, BIG_SRC, BIG_T, BIG_TILE):
        lay = (lambda a: jnp.swapaxes(a, 1, 2)) if tr else (lambda a: a)
        wmv = (lay(P[src]), lay(Mo[src]), lay(Vo[src]))
        res = _adamw_slots("adamw_" + src, [plan.slots[l][k] for l in range(L)], wmv[0].shape, tile, wmv)
        put(src, [lay(r) for r in res])
    res = _adamw("adamw_w_mod", w_mod.reshape(L * D, MOD_SHARD), g_w_mod.reshape(L * D, MOD_SHARD),
                 m_w_mod.reshape(L * D, MOD_SHARD), v_w_mod.reshape(L * D, MOD_SHARD), 256)
    put("w_mod", (g_w_mod,) + tuple(res), w_mod.shape)
    def fuse4(wa, wx):
        return jnp.concatenate([wa[:, 0], wa[:, 1], wx[:, 0], wx[:, 1]], axis=-1).reshape(L, N_RNN_BLOCKS * RB, 4 * RB)

    res = _adamw_slots("adamw_gates", plan.gate_slots, (L, N_RNN_BLOCKS * RB, 4 * RB), 256,
                       (fuse4(lru_wa, lru_wx), fuse4(m_lru_wa, m_lru_wx), fuse4(v_lru_wa, v_lru_wx)))
    res = [r.reshape(L, N_RNN_BLOCKS, RB, 4, RB) for r in res]
    put("lru_wa", [jnp.stack([r[:, :, :, 0], r[:, :, :, 1]], axis=1) for r in res])
    put("lru_wx", [jnp.stack([r[:, :, :, 2], r[:, :, :, 3]], axis=1) for r in res])
    rep = ("g_mix_pre", "g_mix_post", "g_ffn_pre", "g_ffn_post", "conv_b", "b_mod")

    def pack_rep(T_):
        sink = jnp.concatenate([T_["attn_sink"].reshape(1, L * N_Q), jnp.zeros((1, D - L * N_Q), F32)], axis=1)
        return jnp.concatenate([T_[n].reshape(-1, D) for n in rep] + [sink, T_["c_ctx"][None]], axis=0)

    pk = [pack_rep(T_) for T_ in (P, G, Mo, Vo)]
    n_rep = pk[0].shape[0]
    res = _adamw("adamw_replicated", *[jnp.pad(a, ((0, 24 - n_rep), (0, 0))) for a in pk], 24)
    res = (pk[1],) + tuple(r[:n_rep] for r in res)
    o = 0
    for n in rep:
        k = P[n].size // D
        put(n, [r[o:o + k] for r in res], P[n].shape)
        o += k
    put("attn_sink", [r[o, :L * N_Q] for r in res], attn_sink.shape)
    put("c_ctx", [r[o + 1] for r in res], c_ctx.shape)
    chan = ("conv_w", "lru_ba", "lru_bx", "lru_lam")
    g_own = {n: lax.dynamic_slice_in_dim(G[n], me * RB, RB, axis=2) for n in chan}

    def pack_chan(T_):
        return jnp.concatenate([T_[n].reshape(-1, RB) for n in chan], axis=0)

    pk = [pack_chan(T_) for T_ in (P, g_own, Mo, Vo)]
    n_ch = pk[0].shape[0]
    res = _adamw("adamw_channels", *[jnp.pad(a, ((0, 24 - n_ch), (0, 0))) for a in pk], 24)
    res = (pk[1],) + tuple(r[:n_ch] for r in res)
    o = 0
    for n in chan:
        k = P[n].size // RB
        put(n, [r[o:o + k] for r in res], P[n].shape)
        o += k

    order = ("c_ctx", "w_mod", "b_mod", "g_mix_pre", "g_mix_post", "g_ffn_pre", "g_ffn_post", "w_in", "conv_w", "conv_b",
             "lru_wa", "lru_ba", "lru_wx", "lru_bx", "lru_lam", "attn_sink", "w_o_rnn", "w_o_attn", "w_out", "w_ffn_in",
             "w_ffn_out")
    return (loss, grad_x, *[out_g[n] for n in order], *[out_d[n] for n in order], *[out_m[n] for n in order],
            *[out_v[n] for n in order])
```

```python
import functools
import math

import numpy as np
import jax
import jax.numpy as jnp
from jax import lax
from jax.experimental import pallas as pl
from jax.experimental.pallas import tpu as pltpu

F32 = jnp.float32
BF16 = jnp.bfloat16

D = 1024
CTX = 256
TR = 256
HEAD = 128
N_Q = 8
N_KV = 2
Q_PER_KV = N_Q // N_KV
GRID_W = 64
N_FREQ = HEAD // 4
ROPE_BASE = 10000.0
N_RNN_BLOCKS = 8
CONV_W = 4
CONV_LEFT = 2
LRU_C = 8.0
D_FF = 2816
IN_W = 5632
P_W = IN_W
DP_W = 3584
COL_XR, COL_GR, COL_Q, COL_K, COL_V, COL_GL = 0, 1024, 2048, 3072, 3328, 3584
GLB = 512
EPS = 1e-6
NEG_INF = -1e30
ATT_SCALE = HEAD ** -0.5
N_DEV = 8
VMEM_LIMIT = 56 * 1024 * 1024

ADAM_LR, ADAM_B1, ADAM_B2, ADAM_EPS, ADAM_WD, ADAM_STEP = 0.001, 0.9, 0.999, 1e-08, 0.01, 10

NN = (((1,), (0,)), ((), ()))
NT = (((1,), (1,)), ((), ()))
TN = (((0,), (0,)), ((), ()))


def _dot(a, b, dims=NN):
    return lax.dot_general(a, b, dims, preferred_element_type=F32)


def _params(sem=("arbitrary",)):
    return pltpu.CompilerParams(dimension_semantics=sem, vmem_limit_bytes=VMEM_LIMIT)


def _full_spec(shape):
    nd = len(shape)
    return pl.BlockSpec(shape, lambda *_: (0,) * nd)


ANY = pl.BlockSpec(memory_space=pl.ANY)


def _ew(name, body, n, row_ins, pars, row_outs, accs=(), alias=None):
    n_ri, n_p, n_ro, n_acc = len(row_ins), len(pars), len(row_outs), len(accs)

    def kern(*refs):
        i = pl.program_id(0)
        ins = refs[:n_ri]
        ps = refs[n_ri:n_ri + n_p]
        outs = refs[n_ri + n_p:n_ri + n_p + n_ro]
        acc = refs[n_ri + n_p + n_ro:]
        if n_acc:
            @pl.when(i == 0)
            def _():
                for a in acc:
                    a[...] = jnp.zeros(a.shape, a.dtype)
        body(i, ins, ps, outs, acc)

    in_specs = [ANY if blk is None else pl.BlockSpec(blk, imap) for (_, blk, imap) in row_ins]
    in_specs += [_full_spec(p.shape) for p in pars]
    out_specs = [pl.BlockSpec(blk, imap) for (_, blk, imap) in row_outs] + [_full_spec(a.shape) for a in accs]
    out_shape = [s for (s, _, _) in row_outs] + list(accs)
    return pl.pallas_call(
        kern, name=name, grid=(n,), in_specs=in_specs, out_specs=out_specs, out_shape=out_shape,
        input_output_aliases=alias or {}, compiler_params=_params(),
    )(*[a for (a, _, _) in row_ins], *pars)


def _rowblk(width, colblk=0, roff=0, tile=TR):
    return (tile, width), (lambda i: (i + roff, colblk))


def _sds(shape, dtype):
    return jax.ShapeDtypeStruct(shape, dtype)


class _Carry:
    SAME_CORE = (1, 3, 5)

    def __init__(self, jobs):
        self.jobs = list(jobs)
        self.arrays = [a for _, a in self.jobs]
        self.out_shapes = [_sds(a.shape if kind == "scatter" else (N_DEV, *a.shape), a.dtype) for kind, a in self.jobs]
        n = len(self.jobs)
        self.scratch = [pltpu.SemaphoreType.DMA((n, 7)), pltpu.SemaphoreType.DMA((n, 7)), pltpu.SemaphoreType.DMA((n,))]

    def _setup(self, sems):
        send_sems, recv_sems, local_sems = sems
        x, y, c = _place()
        me = _lin(x, y, c)
        peers = [(x ^ ((k + 1) >> 2 & 1), y ^ ((k + 1) >> 1 & 1), c ^ ((k + 1) & 1)) for k in range(7)]

        def copy(a, k, sem_k, src, dst):
            return pltpu.make_async_remote_copy(src_ref=src, dst_ref=dst, send_sem=send_sems.at[a, sem_k],
                                                recv_sem=recv_sems.at[a, sem_k], device_id=peers[k], device_id_type=MESH)

        return me, [_lin(*p) for p in peers], copy, local_sems

    def _local(self, a, kind, ins, outs, me, local_sems):
        return pltpu.make_async_copy(ins[a].at[me] if kind == "scatter" else ins[a], outs[a].at[me], local_sems.at[a])

    def start(self, ins, outs, sems):
        me, theirs, copy, local_sems = self._setup(sems)
        for a, (kind, _) in enumerate(self.jobs):
            self._local(a, kind, ins, outs, me, local_sems).start()
            if kind == "scatter":
                for k in range(7):
                    copy(a, k, k, ins[a].at[theirs[k]], outs[a].at[me]).start()
            else:
                for k in (0,) + self.SAME_CORE:
                    copy(a, k, k, ins[a], outs[a].at[me]).start()

    def wait(self, ins, outs, sems):
        me, theirs, copy, local_sems = self._setup(sems)
        for a, (kind, _) in enumerate(self.jobs):
            if kind == "scatter":
                for k in range(7):
                    copy(a, k, k, ins[a].at[me], outs[a].at[theirs[k]]).wait_recv()
                for k in range(7):
                    copy(a, k, k, ins[a].at[theirs[k]], outs[a].at[me]).wait_send()
            else:
                for k in self.SAME_CORE:
                    blk = outs[a].at[theirs[k]]
                    copy(a, k, k, ins[a], blk).wait_recv()
                    copy(a, 0, k + 1, blk, blk).start()
                copy(a, 0, 0, ins[a], outs[a].at[theirs[0]]).wait_recv()
                for k in self.SAME_CORE:
                    copy(a, 0, k + 1, ins[a], outs[a].at[theirs[k + 1]]).wait_recv()
                for k in (0,) + self.SAME_CORE:
                    copy(a, k, k, ins[a], outs[a].at[me]).wait_send()
                for k in self.SAME_CORE:
                    blk = outs[a].at[theirs[k]]
                    copy(a, 0, k + 1, blk, blk).wait_send()
            self._local(a, kind, ins, outs, me, local_sems).wait()


def _carried(kern, carry, n_in, n_out, first, last):
    if carry is None:
        return kern
    nc = len(carry.jobs)

    def wrapped(*refs):
        ins, cin = refs[:n_in], refs[n_in:n_in + nc]
        outs, cout = refs[n_in + nc:n_in + nc + n_out], refs[n_in + nc + n_out:n_in + 2 * nc + n_out]
        scr, sems = refs[n_in + 2 * nc + n_out:-3], refs[-3:]

        @pl.when(first())
        def _():
            carry.start(cin, cout, sems)

        kern(*ins, *outs, *scr)

        @pl.when(last())
        def _():
            carry.wait(cin, cout, sems)

    return wrapped


def _carry_args(carry):
    if carry is None:
        return [], [], [], [], []
    n = len(carry.jobs)
    return [ANY] * n, carry.arrays, [ANY] * n, carry.out_shapes, carry.scratch


def _grid_ends(dims):
    first = lambda: functools.reduce(jnp.logical_and, [pl.program_id(d) == 0 for d in range(len(dims))])
    last = lambda: functools.reduce(jnp.logical_and, [pl.program_id(d) == n - 1 for d, n in enumerate(dims)])
    return first, last


def _mm_call(name, a, b, mode, out_dtype, tm, tn, rows_outer=True, single_b=False, carry=None):
    if mode == "TN":
        (K, M), N = a.shape, b.shape[1]
    else:
        (M, K), N = a.shape, (b.shape[1] if mode == "NN" else b.shape[0])
    assert M % tm == 0 and N % tn == 0, (name, M, N, K, tm, tn)
    ij = (lambda g0, g1: (g0, g1)) if rows_outer else (lambda g0, g1: (g1, g0))
    grid = (M // tm, N // tn) if rows_outer else (N // tn, M // tm)
    if mode == "TN":
        a_spec = pl.BlockSpec((K, tm), lambda g0, g1: (0, ij(g0, g1)[0]))
    else:
        a_spec = pl.BlockSpec((tm, K), lambda g0, g1: (ij(g0, g1)[0], 0))
    b_blk, b_map = ((tn, K), lambda g0, g1: (ij(g0, g1)[1], 0)) if mode == "NT" else \
                   ((K, tn), lambda g0, g1: (0, ij(g0, g1)[1]))
    b_spec = pl.BlockSpec(b_blk, b_map, pipeline_mode=pl.Buffered(1)) if single_b else pl.BlockSpec(b_blk, b_map)
    dims = {"NN": NN, "NT": NT, "TN": TN}[mode]

    def kern(a_ref, b_ref, o_ref):
        o_ref[...] = _dot(a_ref[...], b_ref[...], dims).astype(o_ref.dtype)

    ci, ca, co, cs, cscr = _carry_args(carry)
    res = pl.pallas_call(
        _carried(kern, carry, 2, 1, *_grid_ends(grid)), name=name, grid=grid, in_specs=[a_spec, b_spec] + ci,
        out_specs=[pl.BlockSpec((tm, tn), lambda g0, g1: ij(g0, g1))] + co,
        out_shape=[_sds((M, N), out_dtype)] + cs, scratch_shapes=cscr,
        compiler_params=_params(("arbitrary", "arbitrary")),
    )(a, b, *ca)
    return res[0] if carry is None else (res[0], res[1:])


def _mm_act(name, a, w, mode, out_dtype=BF16, carry=None):
    rows, K = a.shape
    N = w.shape[1] if mode == "NN" else w.shape[0]
    if K > D_FF:
        return _mm_call(name, a, w, mode, out_dtype, rows // 8, N, single_b=True, carry=carry)
    tn = N if N <= 1024 else 1408
    return _mm_call(name, a, w, mode, out_dtype, rows // 4, tn, carry=carry)


def _mm_wgrad(name, x, dy, out_dtype=BF16, carry=None):
    M = x.shape[1]
    tm = 1408 if M == D_FF else 512
    return _mm_call(name, x, dy, "TN", out_dtype, tm, dy.shape[1], single_b=True, carry=carry)


def _sigmoid(x):
    return 0.5 * jnp.tanh(0.5 * x) + 0.5


def _silu(x):
    return x * _sigmoid(x)


def _silu_grad(x):
    s = _sigmoid(x)
    return s * (1.0 + x * (1.0 - s))


_GELU_K = math.sqrt(2.0 / math.pi)


def _gelu(x):
    return 0.5 * x * (1.0 + jnp.tanh(_GELU_K * (x + 0.044715 * x * x * x)))


def _gelu_grad(x):
    t = jnp.tanh(_GELU_K * (x + 0.044715 * x * x * x))
    return 0.5 * (1.0 + t) + 0.5 * x * (1.0 - t * t) * _GELU_K * (1.0 + 3.0 * 0.044715 * x * x)


def _log_sigmoid(x):
    return jnp.minimum(x, 0.0) - jnp.log(1.0 + jnp.exp(-jnp.abs(x)))


def _rms(x):
    x = x.astype(F32)
    r = lax.rsqrt(jnp.mean(x * x, axis=-1, keepdims=True) + EPS)
    return x * r, r


def _rms_bwd(dy, y, r):
    return r * (dy - y * jnp.mean(dy * y, axis=-1, keepdims=True))


def _modrow(mod_ref, i, chunk):
    lo = mod_ref[0:1, chunk * D:(chunk + 1) * D]
    hi = mod_ref[1:2, chunk * D:(chunk + 1) * D]
    return jnp.where(i == 0, lo, hi)


def _acc_seg(acc_ref, i, val):
    zero = jnp.zeros_like(val)
    acc_ref[0:1, :] += jnp.where(i == 0, val, zero)
    acc_ref[1:2, :] += jnp.where(i == 0, zero, val)


def _colsum(x):
    return jnp.sum(x, axis=0, keepdims=True)


SH1, SC1, GA1, SH2, SC2, GA2 = range(6)


def _normmod_fwd(name, xa, g, mod, c_sh, c_sc, carry=None):
    T = xa.shape[0]
    n = T // TR

    def kern(x_ref, g_ref, mod_ref, h_ref):
        i = pl.program_id(0)
        y, _ = _rms(x_ref[...])
        h = (y * g_ref[...]) * (1.0 + _modrow(mod_ref, i, c_sc)) + _modrow(mod_ref, i, c_sh)
        h_ref[...] = h.astype(BF16)

    row = pl.BlockSpec((TR, D), lambda i: (i, 0))
    ci, ca, co, cs, cscr = _carry_args(carry)
    res = pl.pallas_call(
        _carried(kern, carry, 3, 1, *_grid_ends((n,))), name=name, grid=(n,),
        in_specs=[row, _full_spec(g.shape), _full_spec(mod.shape)] + ci, out_specs=[row] + co,
        out_shape=[_sds((T, D), BF16)] + cs, scratch_shapes=cscr, compiler_params=_params(),
    )(xa, g, mod, *ca)
    return res[0] if carry is None else (res[0], res[1:])


def _modrows(mod_ref, row0, n, chunk):
    t = row0 + lax.broadcasted_iota(jnp.int32, (n, 1), 0)
    return jnp.where(t < CTX, mod_ref[0:1, chunk * D:(chunk + 1) * D], mod_ref[1:2, chunk * D:(chunk + 1) * D])


def _loss_resid_bwd(name, x_out, target, mat, gpost, mod, c_ga):
    T = x_out.shape[0]

    def body(i, ins, ps, outs, acc):
        err = ins[0][...] - ins[1][...]
        lat = i > 0
        dx = jnp.where(lat, err * (1.0 / D), 0.0)
        outs[0][...] = dx
        acc[2][...] += jnp.where(lat, _colsum(err * err), 0.0)
        outs[1][...] = _resid_bwd_vals(i, dx, ins[2][...], ps[0][...], ps[1], c_ga, acc[0], acc[1]).astype(BF16)

    tgt_blk = ((TR, D), lambda i: (jnp.maximum(i - 1, 0), 0))
    return _ew(name, body, T // TR, [(x_out, *_rowblk(D)), (target, *tgt_blk), (mat, *_rowblk(D))], [gpost, mod],
               [(_sds((T, D), F32), *_rowblk(D)), (_sds((T, D), BF16), *_rowblk(D))],
               [_sds((2, D), F32), _sds((1, D), F32), _sds((1, D), F32)])


def _mod_for(mod_ref, i, chunk, row0, n):
    return _modrow(mod_ref, i, chunk) if row0 is None else _modrows(mod_ref, row0, n, chunk)


def _acc_for(acc_ref, i, v, row0):
    if row0 is None:
        _acc_seg(acc_ref, i, _colsum(v))
        return

    @pl.when(row0 < CTX)
    def _():
        is_ctx = row0 + lax.broadcasted_iota(jnp.int32, (v.shape[0], 1), 0) < CTX
        acc_ref[0:1, :] += _colsum(jnp.where(is_ctx, v, 0.0))
        acc_ref[1:2, :] += _colsum(jnp.where(is_ctx, 0.0, v))

    @pl.when(row0 >= CTX)
    def _():
        acc_ref[1:2, :] += _colsum(v)


def _resid_bwd_vals(i, dout, mat, gpost, mod_ref, c_ga, acc_ga, acc_g, row0=None):
    ym, rm = _rms(mat)
    ga = _mod_for(mod_ref, i, c_ga, row0, dout.shape[0])
    _acc_for(acc_ga, i, dout * (ym * gpost), row0)
    dn = dout * ga
    acc_g[...] += _colsum(dn * ym)
    return _rms_bwd(dn * gpost, ym, rm)


def _normmod_bwd_vals(i, dh, xin, g, mod_ref, c_sh, c_sc, acc_sh, acc_sc, acc_g, row0=None):
    dh = dh.astype(F32)
    y, r = _rms(xin)
    _acc_for(acc_sc, i, dh * (y * g), row0)
    _acc_for(acc_sh, i, dh, row0)
    dyg = dh * (1.0 + _mod_for(mod_ref, i, c_sc, row0, dh.shape[0]))
    acc_g[...] += _colsum(dyg * y)
    return _rms_bwd(dyg * g, y, r)


def _parts(i, tm):
    return [(slice(0, tm), i * tm)]


FT = 1408


def _ffn_in_fused(name, h2, w_t, carry=None):
    T = h2.shape[0]
    tm, nj = T // 4, D_FF // FT

    def kern(a_ref, bg_ref, bu_ref, fg_ref, fu_ref, s_ref):
        for rows, _ in _parts(0, tm):
            a = a_ref[rows, :]
            g = _dot(a, bg_ref[...], NT)
            u = _dot(a, bu_ref[...], NT)
            fg_ref[rows, :] = g.astype(BF16)
            fu_ref[rows, :] = u.astype(BF16)
            s_ref[rows, :] = (_silu(g) * u).astype(BF16)

    o_spec = pl.BlockSpec((tm, FT), lambda i, j: (i, j))
    ci, ca, co, cs, cscr = _carry_args(carry)
    res = pl.pallas_call(
        _carried(kern, carry, 3, 3, *_grid_ends((4, nj))), name=name, grid=(4, nj),
        in_specs=[pl.BlockSpec((tm, D), lambda i, j: (i, 0)), pl.BlockSpec((FT, D), lambda i, j: (j, 0)),
                  pl.BlockSpec((FT, D), lambda i, j: (j + nj, 0))] + ci,
        out_specs=[o_spec] * 3 + co, out_shape=[_sds((T, D_FF), BF16)] * 3 + cs, scratch_shapes=cscr,
        compiler_params=_params(("arbitrary", "arbitrary")),
    )(h2, w_t, w_t, *ca)
    return res if carry is None else (res[:3], res[3:])


def _norm_chain(row0, xin, mat, gpost, mod_ref, c_ga, gnext, modn_ref, c_sh, c_sc):
    n = xin.shape[0]
    ym, _ = _rms(mat.astype(BF16))
    xo = xin + _modrows(mod_ref, row0, n, c_ga) * (ym * gpost)
    y, _ = _rms(xo)
    h = (y * gnext) * (1.0 + _modrows(modn_ref, row0, n, c_sc)) + _modrows(modn_ref, row0, n, c_sh)
    return xo, h.astype(BF16)


def _out_fused(name, p, u, o_all, xa, w_o_rnn, w_o_attn, w_out, gpost, mod, gnext):
    T = u.shape[0]
    tm = T // 8

    def kern(g0, g1, g2, g3, u_ref, o_ref, xa_ref, wr_ref, wa_ref, w_ref, gpost_ref, mod_ref, gnext_ref,
             ya_ref, yb_ref, z_ref, m_ref, x1_ref, h2_ref):
        for rows, row0 in _parts(pl.program_id(0), tm):
            ya = _dot(u_ref[rows, :], wr_ref[...]).astype(BF16)
            yb = _dot(o_ref[rows, :], wa_ref[...]).astype(BF16)
            ya_ref[rows, :] = ya
            yb_ref[rows, :] = yb
            ga = _sigmoid(jnp.concatenate([g0[rows, :], g1[rows, :]], axis=1).astype(F32))
            gb = _sigmoid(jnp.concatenate([g2[rows, :], g3[rows, :]], axis=1).astype(F32))
            z = (ga * ya.astype(F32) + gb * yb.astype(F32)).astype(BF16)
            z_ref[rows, :] = z
            m = _dot(z, w_ref[...])
            m_ref[rows, :] = m.astype(BF16)
            x1_ref[rows, :], h2_ref[rows, :] = _norm_chain(row0, xa_ref[rows, :], m, gpost_ref[...], mod_ref, GA1,
                                                           gnext_ref[...], mod_ref, SH2, SC2)

    row = lambda w: pl.BlockSpec((tm, w), lambda i: (i, 0))
    return pl.pallas_call(
        kern, name=name, grid=(T // tm,),
        in_specs=[pl.BlockSpec((tm, GLB), lambda i, q=q: (i, COL_GL // GLB + q)) for q in range(4)]
                 + [row(D), row(D), row(D)] + [_full_spec(a.shape) for a in (w_o_rnn, w_o_attn, w_out, gpost, mod, gnext)],
        out_specs=[row(D)] * 6,
        out_shape=[_sds((T, D), BF16)] * 4 + [_sds((T, D), F32), _sds((T, D), BF16)],
        compiler_params=_params(),
    )(p, p, p, p, u, o_all, xa, w_o_rnn, w_o_attn, w_out, gpost, mod, gnext)


def _ffn_out_fused(name, s, w, x1, gpost, mod, nxt=None):
    T = s.shape[0]
    tm = T // 8

    def kern(s_ref, w_ref, x1_ref, gpost_ref, mod_ref, *rest):
        for rows, row0 in _parts(pl.program_id(0), tm):
            e = _dot(s_ref[rows, :], w_ref[...])
            if nxt is None:
                e_ref, xo_ref = rest
                ym, _ = _rms(e.astype(BF16))
                xo_ref[rows, :] = x1_ref[rows, :] + _modrows(mod_ref, row0, e.shape[0], GA2) * (ym * gpost_ref[...])
            else:
                gnext_ref, modn_ref, e_ref, xo_ref, h_ref = rest
                xo_ref[rows, :], h_ref[rows, :] = _norm_chain(row0, x1_ref[rows, :], e, gpost_ref[...], mod_ref, GA2,
                                                              gnext_ref[...], modn_ref, SH1, SC1)
            e_ref[rows, :] = e.astype(BF16)

    row = lambda w_: pl.BlockSpec((tm, w_), lambda i: (i, 0))
    extra = [] if nxt is None else list(nxt)
    return pl.pallas_call(
        kern, name=name, grid=(T // tm,),
        in_specs=[row(D_FF), _full_spec(w.shape), row(D), _full_spec(gpost.shape), _full_spec(mod.shape)]
                 + [_full_spec(a.shape) for a in extra],
        out_specs=[row(D)] * (2 if nxt is None else 3),
        out_shape=[_sds((T, D), BF16), _sds((T, D), F32)] + ([] if nxt is None else [_sds((T, D), BF16)]),
        compiler_params=_params(),
    )(s, w, x1, gpost, mod, *extra)


def _ffn_bwd_fused(name, fg, fu, w, de=None, head=None):
    T = fg.shape[0]
    tm = T // 8
    row = lambda w_: pl.BlockSpec((tm, w_), lambda i: (i, 0))
    w_spec = pl.BlockSpec(w.shape, lambda i: (0, 0), pipeline_mode=pl.Buffered(1))

    def tail(rows, de_v, fg_ref, fu_ref, w_ref, df_ref):
        ds = _dot(de_v, w_ref[...], NT)
        g, u = fg_ref[rows, :].astype(F32), fu_ref[rows, :].astype(F32)
        df_ref[rows, :] = jnp.concatenate([ds * u * _silu_grad(g), ds * _silu(g)], axis=1).astype(BF16)

    if head is None:
        def kern(de_ref, fg_ref, fu_ref, w_ref, df_ref):
            for rows, _ in _parts(pl.program_id(0), tm):
                tail(rows, de_ref[rows, :], fg_ref, fu_ref, w_ref, df_ref)

        return pl.pallas_call(
            kern, name=name, grid=(T // tm,), in_specs=[row(D), row(D_FF), row(D_FF), w_spec],
            out_specs=[row(2 * D_FF)], out_shape=[_sds((T, 2 * D_FF), BF16)], compiler_params=_params(),
        )(de, fg, fu, w)

    dx2, e, gpost, mod = head

    def kern(dx_ref, e_ref, fg_ref, fu_ref, w_ref, gpost_ref, mod_ref, de_ref, df_ref, dga_ref, dg_ref):
        i = pl.program_id(0)

        @pl.when(i == 0)
        def _():
            dga_ref[...] = jnp.zeros(dga_ref.shape, F32)
            dg_ref[...] = jnp.zeros(dg_ref.shape, F32)

        for rows, row0 in _parts(i, tm):
            de_v = _resid_bwd_vals(i, dx_ref[rows, :], e_ref[rows, :], gpost_ref[...], mod_ref, GA2, dga_ref, dg_ref,
                                   row0=row0).astype(BF16)
            de_ref[rows, :] = de_v
            tail(rows, de_v, fg_ref, fu_ref, w_ref, df_ref)

    return pl.pallas_call(
        kern, name=name, grid=(T // tm,),
        in_specs=[row(D), row(D), row(D_FF), row(D_FF), w_spec, _full_spec(gpost.shape), _full_spec(mod.shape)],
        out_specs=[row(D), row(2 * D_FF), _full_spec((2, D)), _full_spec((1, D))],
        out_shape=[_sds((T, D), BF16), _sds((T, 2 * D_FF), BF16), _sds((2, D), F32), _sds((1, D), F32)],
        compiler_params=_params(),
    )(dx2, e, fg, fu, w, gpost, mod)


def _zero_at_start(i, refs):
    @pl.when(i == 0)
    def _():
        for r in refs:
            r[...] = jnp.zeros(r.shape, F32)


def _proj_bwd_fused(name, dp, dgl, w_in_t, xa, dx1, gpre, mod, carry=None):
    T = dp.shape[0]
    tm = T // 8
    row = lambda w_: pl.BlockSpec((tm, w_), lambda i: (i, 0))

    def kern(dp_ref, dgl_ref, w_ref, xa_ref, dx1_ref, g_ref, mod_ref, dxa_ref, dsh_ref, dsc_ref, dg_ref):
        i = pl.program_id(0)
        _zero_at_start(i, (dsh_ref, dsc_ref, dg_ref))
        for rows, row0 in _parts(i, tm):
            dh = _dot(dp_ref[rows, :], w_ref[0:DP_W, :]) + _dot(dgl_ref[rows, :], w_ref[DP_W:, :])
            dxa_ref[rows, :] = dx1_ref[rows, :] + _normmod_bwd_vals(i, dh, xa_ref[rows, :], g_ref[...], mod_ref, SH1,
                                                                    SC1, dsh_ref, dsc_ref, dg_ref, row0=row0)

    ci, ca, co, cs, cscr = _carry_args(carry)
    res = pl.pallas_call(
        _carried(kern, carry, 7, 4, *_grid_ends((T // tm,))), name=name, grid=(T // tm,),
        in_specs=[row(DP_W), row(P_W - DP_W),
                  pl.BlockSpec(w_in_t.shape, lambda i: (0, 0), pipeline_mode=pl.Buffered(1)), row(D), row(D),
                  _full_spec(gpre.shape), _full_spec(mod.shape)] + ci,
        out_specs=[row(D), _full_spec((2, D)), _full_spec((2, D)), _full_spec((1, D))] + co,
        out_shape=[_sds((T, D), F32), _sds((2, D), F32), _sds((2, D), F32), _sds((1, D), F32)] + cs,
        scratch_shapes=cscr, compiler_params=_params(),
    )(dp, dgl, w_in_t, xa, dx1, gpre, mod, *ca)
    return res if carry is None else (res[:4], res[4:])


def _proj_wgrad(name, dp, dgl, h, carry=None):
    T, N = h.shape
    n1, n2 = DP_W // GLB, (P_W - DP_W) // GLB

    def kern(a1_ref, a2_ref, h_ref, o_ref):
        i = pl.program_id(0)

        @pl.when(i < n1)
        def _():
            o_ref[...] = _dot(a1_ref[...], h_ref[...], TN).astype(o_ref.dtype)

        @pl.when(i >= n1)
        def _():
            o_ref[...] = _dot(a2_ref[...], h_ref[...], TN).astype(o_ref.dtype)

    ci, ca, co, cs, cscr = _carry_args(carry)
    res = pl.pallas_call(
        _carried(kern, carry, 3, 1, *_grid_ends((n1 + n2,))), name=name, grid=(n1 + n2,),
        in_specs=[pl.BlockSpec((T, GLB), lambda i: (0, jnp.minimum(i, n1 - 1))),
                  pl.BlockSpec((T, GLB), lambda i: (0, jnp.maximum(i - n1, 0))),
                  pl.BlockSpec((T, N), lambda i: (0, 0), pipeline_mode=pl.Buffered(1))] + ci,
        out_specs=[pl.BlockSpec((GLB, N), lambda i: (i, 0))] + co,
        out_shape=[_sds((P_W, N), BF16)] + cs, scratch_shapes=cscr, compiler_params=_params(),
    )(dp, dgl, h, *ca)
    return res[0] if carry is None else (res[0], res[1:])


def _ffn_in_bwd_fused(name, df, w_t, x1, dres, mat, gpre, mod, gpost, carry=None):
    T = df.shape[0]
    tm = T // 8
    row = lambda w_: pl.BlockSpec((tm, w_), lambda i: (i, 0))

    def kern(df_ref, w_ref, x1_ref, dres_ref, mat_ref, gpre_ref, mod_ref, gpost_ref,
             dx1_ref, dm_ref, dsh_ref, dsc_ref, dgpre_ref, dga_ref, dgpost_ref):
        i = pl.program_id(0)
        _zero_at_start(i, (dsh_ref, dsc_ref, dgpre_ref, dga_ref, dgpost_ref))
        for rows, row0 in _parts(i, tm):
            dh2 = _dot(df_ref[rows, :], w_ref[...])
            dx1 = dres_ref[rows, :] + _normmod_bwd_vals(i, dh2, x1_ref[rows, :], gpre_ref[...], mod_ref, SH2, SC2,
                                                        dsh_ref, dsc_ref, dgpre_ref, row0=row0)
            dx1_ref[rows, :] = dx1
            dm_ref[rows, :] = _resid_bwd_vals(i, dx1, mat_ref[rows, :], gpost_ref[...], mod_ref, GA1, dga_ref,
                                              dgpost_ref, row0=row0).astype(BF16)

    ci, ca, co, cs, cscr = _carry_args(carry)
    res = pl.pallas_call(
        _carried(kern, carry, 8, 7, *_grid_ends((T // tm,))), name=name, grid=(T // tm,),
        in_specs=[row(2 * D_FF), pl.BlockSpec(w_t.shape, lambda i: (0, 0), pipeline_mode=pl.Buffered(1)), row(D),
                  row(D), row(D), _full_spec(gpre.shape), _full_spec(mod.shape), _full_spec(gpost.shape)] + ci,
        out_specs=[row(D), row(D), _full_spec((2, D)), _full_spec((2, D)), _full_spec((1, D)), _full_spec((2, D)),
                   _full_spec((1, D))] + co,
        out_shape=[_sds((T, D), F32), _sds((T, D), BF16), _sds((2, D), F32), _sds((2, D), F32), _sds((1, D), F32),
                   _sds((2, D), F32), _sds((1, D), F32)] + cs,
        scratch_shapes=cscr, compiler_params=_params(),
    )(df, w_t, x1, dres, mat, gpre, mod, gpost, *ca)
    return res if carry is None else (res[:7], res[7:])


def _out_bwd_fused(name, dm, w_out, w_o_rnn, w_o_attn, p, ya, yb):
    T = dm.shape[0]
    tm = T // 8
    row = lambda w_: pl.BlockSpec((tm, w_), lambda i: (i, 0))

    def kern(dm_ref, w_ref, wr_ref, wa_ref, g0, g1, g2, g3, ya_ref, yb_ref, dya_ref, dyb_ref, dgl_ref, du_ref, do_ref):
        for rows, _ in _parts(pl.program_id(0), tm):
            dz = _dot(dm_ref[rows, :], w_ref[...], NT)
            ga = _sigmoid(jnp.concatenate([g0[rows, :], g1[rows, :]], axis=1).astype(F32))
            gb = _sigmoid(jnp.concatenate([g2[rows, :], g3[rows, :]], axis=1).astype(F32))
            dya = (dz * ga).astype(BF16)
            dyb = (dz * gb).astype(BF16)
            dya_ref[rows, :] = dya
            dyb_ref[rows, :] = dyb
            dgl_ref[rows, :] = jnp.concatenate([dz * ya_ref[rows, :].astype(F32) * ga * (1.0 - ga),
                                                dz * yb_ref[rows, :].astype(F32) * gb * (1.0 - gb)],
                                               axis=1).astype(BF16)
            du_ref[rows, :] = _dot(dya, wr_ref[...], NT).astype(BF16)
            do_ref[rows, :] = _dot(dyb, wa_ref[...], NT).astype(BF16)

    return pl.pallas_call(
        kern, name=name, grid=(T // tm,),
        in_specs=[row(D)] + [_full_spec(w.shape) for w in (w_out, w_o_rnn, w_o_attn)]
                 + [pl.BlockSpec((tm, GLB), lambda i, q=q: (i, COL_GL // GLB + q)) for q in range(4)] + [row(D), row(D)],
        out_specs=[row(D), row(D), row(2 * D), row(D), row(D)],
        out_shape=[_sds((T, D), BF16), _sds((T, D), BF16), _sds((T, 2 * D), BF16), _sds((T, D), BF16),
                   _sds((T, D), BF16)],
        compiler_params=_params(),
    )(dm, w_out, w_o_rnn, w_o_attn, p, p, p, p, ya, yb)


AB = 128
CTX_BLKS = CTX // AB


def _rope_tables(S):
    pos = jnp.arange(S, dtype=jnp.int32)
    inv = ROPE_BASE ** (-jnp.arange(N_FREQ, dtype=F32) / N_FREQ)
    ang_r = (pos // GRID_W).astype(F32)[:, None] * inv[None, :]
    ang_c = (pos % GRID_W).astype(F32)[:, None] * inv[None, :]
    cos = jnp.concatenate([jnp.cos(ang_r)] * 2 + [jnp.cos(ang_c)] * 2, axis=1)
    sin = jnp.concatenate([-jnp.sin(ang_r), jnp.sin(ang_r), -jnp.sin(ang_c), jnp.sin(ang_c)], axis=1)
    return cos, sin


def _rope(x, cos, sin):
    w = x.shape[1]
    reps = w // HEAD
    lane = lax.broadcasted_iota(jnp.int32, x.shape, 1)
    partner = jnp.where((lane & 63) < 32, pltpu.roll(x, w - 32, 1), pltpu.roll(x, 32, 1))
    return x * jnp.tile(cos, (1, reps)) + partner * jnp.tile(sin, (1, reps))


def _unrope(dx, cos, sin):
    w = dx.shape[1]
    reps = w // HEAD
    lane = lax.broadcasted_iota(jnp.int32, dx.shape, 1)
    t = dx * jnp.tile(sin, (1, reps))
    partner = jnp.where((lane & 63) < 32, pltpu.roll(t, w - 32, 1), pltpu.roll(t, 32, 1))
    return dx * jnp.tile(cos, (1, reps)) + partner


def _qkv_prep(name, p, cos, sin, S):
    T = CTX + S
    nt = T // TR
    cb = CTX // TR
    KW = N_KV * HEAD

    def with_ones(v):
        ones = jnp.ones((TR, HEAD), BF16)
        return jnp.concatenate([v[:, kh * HEAD:(kh + 1) * HEAD] if part == 0 else ones
                                for kh in range(N_KV) for part in range(2)], axis=1)

    def kern(q_ref, k_ref, v_ref, cos_ref, sin_ref, qa_ref, kp_ref, vp_ref, kc_ref, vc_ref):
        i = pl.program_id(0)
        cos_v, sin_v = cos_ref[...], sin_ref[...]
        @pl.when(i < cb)
        def _():
            qa_ref[...] = (q_ref[...].astype(F32) * ATT_SCALE).astype(BF16)
            kc_ref[...] = k_ref[...]
            vc_ref[...] = with_ones(v_ref[...])

        @pl.when((i < cb) | (i >= nt))
        def _():
            kp_ref[...] = jnp.zeros(kp_ref.shape, BF16)
            vp_ref[...] = jnp.zeros(vp_ref.shape, BF16)

        @pl.when((i >= cb) & (i < nt))
        def _():
            qa_ref[...] = (_rope(q_ref[...].astype(F32), cos_v, sin_v) * ATT_SCALE).astype(BF16)
            kp_ref[...] = _rope(k_ref[...].astype(F32), cos_v, sin_v).astype(BF16)
            vp_ref[...] = with_ones(v_ref[...])

    tok = lambda i: jnp.minimum(i, nt - 1)
    lat_map = lambda i: (jnp.clip(i - cb, 0, nt - cb - 1), 0)
    ctx_map = lambda i: (jnp.minimum(i, cb - 1), 0)
    return pl.pallas_call(
        kern, name=name, grid=(nt + cb,),
        in_specs=[pl.BlockSpec((TR, N_Q * HEAD), lambda i: (tok(i), COL_Q // (N_Q * HEAD))),
                  pl.BlockSpec((TR, KW), lambda i: (tok(i), COL_K // KW)),
                  pl.BlockSpec((TR, KW), lambda i: (tok(i), COL_V // KW)),
                  pl.BlockSpec((TR, HEAD), lat_map), pl.BlockSpec((TR, HEAD), lat_map)],
        out_specs=[pl.BlockSpec((TR, N_Q * HEAD), lambda i: (tok(i), 0)),
                   pl.BlockSpec((TR, KW), lambda i: (i, 0)), pl.BlockSpec((TR, 2 * KW), lambda i: (i, 0)),
                   pl.BlockSpec((TR, KW), ctx_map), pl.BlockSpec((TR, 2 * KW), ctx_map)],
        out_shape=[_sds((T, N_Q * HEAD), BF16), _sds((S + 2 * CTX, KW), BF16), _sds((S + 2 * CTX, 2 * KW), BF16),
                   _sds((CTX, KW), BF16), _sds((CTX, 2 * KW), BF16)],
        compiler_params=_params(),
    )(p, p, p, cos, sin)


GW = Q_PER_KV * HEAD
HG = Q_PER_KV


def _band_bias(S):
    r = jnp.arange(AB, dtype=jnp.int32)[:, None]
    c = jnp.arange(3 * AB, dtype=jnp.int32)[None, :]
    near = jnp.abs(c - AB - r) <= AB
    valid = jnp.stack([near & (c >= AB), near, near & (c < 2 * AB)])
    return jnp.where(valid, 0.0, NEG_INF).astype(F32)


def _bias_spec(S):
    nb = S // AB
    return pl.BlockSpec((None, AB, 3 * AB), lambda kh, n: (jnp.where(n == 0, 0, jnp.where(n == nb - 1, 2, 1)), 0, 0))


def _head_probs(q, sink, kc, vce, kb, vbe, bias):
    s_c = _dot(q, kc, NT)
    m = jnp.maximum(jnp.max(s_c, axis=-1, keepdims=True), sink)
    if kb is not None:
        s_b = _dot(q, kb, NT) + bias
        m = jnp.maximum(m, jnp.max(s_b, axis=-1, keepdims=True))
    p_c = jnp.exp(s_c - m).astype(BF16)
    acc = _dot(p_c, vce)
    p_b = None
    if kb is not None:
        p_b = jnp.exp(s_b - m).astype(BF16)
        acc = acc + _dot(p_b, vbe)
    return p_c, p_b, m, acc


def _attn_fwd(name, qa, kc, vc, sink4, S, band=None, prev=None, carry=None):
    T = qa.shape[0]
    has_band = band is not None
    nq = S // AB if has_band else CTX_BLKS
    q_off = CTX_BLKS if has_band else 0

    def kern(*refs):
        q_ref, kc_ref, vc_ref, sink_ref = refs[:4]
        rest = refs[4:]
        o_ref = rest[-1]
        n = pl.program_id(1)
        kc_v, vce = kc_ref[...], vc_ref[...]
        kb = vbe = bias = None
        if has_band:
            kp_ref, vp_ref, bias_ref = rest[:3]
            start = pl.multiple_of(n * AB + (CTX - AB), AB)
            kb = kp_ref[pl.ds(start, 3 * AB), :]
            vbe = vp_ref[pl.ds(start, 3 * AB), :]
            bias = bias_ref[...]
        outs = []
        for g in range(Q_PER_KV):
            sink = sink_ref[g:g + 1, 0:1]
            _, _, m, acc = _head_probs(q_ref[:, g * HEAD:(g + 1) * HEAD], sink, kc_v, vce, kb, vbe, bias)
            l = acc[:, HEAD:] + jnp.exp(sink - m)
            outs.append(acc[:, :HEAD] / l)
        o_ref[...] = jnp.concatenate(outs, axis=1).astype(BF16)

    in_specs = [pl.BlockSpec((AB, GW), lambda kh, n: (n + q_off, kh)),
                pl.BlockSpec((CTX, HEAD), lambda kh, n: (0, kh)), pl.BlockSpec((CTX, 2 * HEAD), lambda kh, n: (0, kh)),
                pl.BlockSpec((None, Q_PER_KV, HEAD), lambda kh, n: (kh, 0, 0))]
    args = [qa, kc, vc, sink4]
    if has_band:
        in_specs += [pl.BlockSpec((S + 2 * CTX, HEAD), lambda kh, n: (0, kh)),
                     pl.BlockSpec((S + 2 * CTX, 2 * HEAD), lambda kh, n: (0, kh)), _bias_spec(S)]
        args += list(band)
    alias = {}
    if prev is not None:
        in_specs.append(ANY)
        alias = {len(args): 0}
        args.append(prev)
    ci, ca, co, cs, cscr = _carry_args(carry)
    res = pl.pallas_call(
        _carried(kern, carry, len(args), 1, *_grid_ends((N_KV, nq))), name=name, grid=(N_KV, nq),
        in_specs=in_specs + ci,
        out_specs=[pl.BlockSpec((AB, GW), lambda kh, n: (n + q_off, kh))] + co,
        out_shape=[_sds((T, N_Q * HEAD), BF16)] + cs, input_output_aliases=alias, scratch_shapes=cscr,
        compiler_params=_params(("arbitrary", "arbitrary")),
    )(*args, *ca)
    return res[0] if carry is None else (res[0], res[1:])


def _attn_bwd(name, qa, kc, vc, sink4, o_all, do_all, S, band=None, prev_dq=None, carry=None):
    T = qa.shape[0]
    has_band = band is not None
    nq = S // AB if has_band else CTX_BLKS
    q_off = CTX_BLKS if has_band else 0
    KW = N_KV * HEAD

    def kern(*refs):
        q_ref, kc_ref, vc_ref, sink_ref, o_ref, do_ref = refs[:6]
        rest = refs[6:]
        if has_band:
            kp_ref, vp_ref, bias_ref, cos_ref, sin_ref = rest[:5]
            rest = rest[5:]
        if prev_dq is not None:
            rest = rest[1:]
        dq_ref, dkc_ref, dvc_ref, dsink_ref = rest[:4]
        n = pl.program_id(1)

        @pl.when(n == 0)
        def _():
            dkc_ref[...] = jnp.zeros(dkc_ref.shape, F32)
            dvc_ref[...] = jnp.zeros(dvc_ref.shape, F32)
            dsink_ref[...] = jnp.zeros(dsink_ref.shape, F32)
            if has_band:
                rest[4][...] = jnp.zeros(rest[4].shape, F32)
                rest[5][...] = jnp.zeros(rest[5].shape, F32)

        kc_v, vce = kc_ref[...], vc_ref[...]
        vc_v = vce[:, :HEAD]
        kb = vbe = vb = bias = None
        if has_band:
            start = pl.multiple_of(n * AB + (CTX - AB), AB)
            kb = kp_ref[pl.ds(start, 3 * AB), :]
            vbe = vp_ref[pl.ds(start, 3 * AB), :]
            vb = vbe[:, :HEAD]
            bias = bias_ref[...]
        dq_parts, dsink_parts = [], []
        for g0 in range(0, Q_PER_KV, HG):
            heads = range(g0, g0 + HG)
            stack = lambda ref: jnp.concatenate([ref[:, g * HEAD:(g + 1) * HEAD] for g in heads], axis=0)
            q4, do4 = stack(q_ref), stack(do_ref)
            sink = jnp.concatenate([jnp.broadcast_to(sink_ref[g:g + 1, 0:1], (AB, 1)) for g in heads], axis=0)
            s_c = _dot(q4, kc_v, NT)
            m = jnp.maximum(jnp.max(s_c, axis=-1, keepdims=True), sink)
            if has_band:
                s_b = _dot(q4, kb, NT) + jnp.tile(bias, (HG, 1))
                m = jnp.maximum(m, jnp.max(s_b, axis=-1, keepdims=True))
            p_c = jnp.exp(s_c - m).astype(BF16).astype(F32)
            p_sink = jnp.exp(sink - m)
            l = jnp.sum(p_c, axis=-1, keepdims=True) + p_sink
            if has_band:
                p_b = jnp.exp(s_b - m).astype(BF16).astype(F32)
                l = l + jnp.sum(p_b, axis=-1, keepdims=True)
            inv = 1.0 / l
            delta = jnp.sum(do4.astype(F32) * stack(o_ref).astype(F32), axis=-1, keepdims=True)
            do4b = do4.astype(BF16)
            pn_c = (p_c * inv).astype(BF16)
            ds_c = (p_c * inv * (_dot(do4b, vc_v, NT) - delta)).astype(BF16)
            dq4 = _dot(ds_c, kc_v)
            dkc_ref[...] += _dot(q4, ds_c, TN)
            dvc_ref[...] += _dot(do4b, pn_c, TN)
            if has_band:
                pn_b = (p_b * inv).astype(BF16)
                ds_b = (p_b * inv * (_dot(do4b, vb, NT) - delta)).astype(BF16)
                dq4 = dq4 + _dot(ds_b, kb)
                rest[4][:, pl.ds(start, 3 * AB)] += _dot(q4, ds_b, TN)
                rest[5][:, pl.ds(start, 3 * AB)] += _dot(do4b, pn_b, TN)
            dq4 = dq4 * ATT_SCALE
            dq_parts += [dq4[k * AB:(k + 1) * AB, :] for k in range(HG)]
            ps = p_sink * inv * delta
            dsink_parts += [jnp.broadcast_to(-jnp.sum(ps[k * AB:(k + 1) * AB, :], axis=0, keepdims=True), (1, HEAD))
                            for k in range(HG)]
        dq = jnp.concatenate(dq_parts, axis=1)
        dq_ref[...] = (_unrope(dq, cos_ref[...], sin_ref[...]) if has_band else dq).astype(BF16)
        dsink_ref[...] += jnp.concatenate(dsink_parts, axis=0)

    q_spec = pl.BlockSpec((AB, GW), lambda kh, n: (n + q_off, kh))
    c_spec = pl.BlockSpec((CTX, HEAD), lambda kh, n: (0, kh))
    ce_spec = pl.BlockSpec((CTX, 2 * HEAD), lambda kh, n: (0, kh))
    s_spec = pl.BlockSpec((None, Q_PER_KV, HEAD), lambda kh, n: (kh, 0, 0))
    in_specs = [q_spec, c_spec, ce_spec, s_spec, q_spec, q_spec]
    args = [qa, kc, vc, sink4, o_all, do_all]
    ct_spec = pl.BlockSpec((HEAD, CTX), lambda kh, n: (kh, 0))
    dq_spec = pl.BlockSpec((AB, GW), lambda kh, n: (n + q_off, COL_Q // GW + kh))
    out_specs = [dq_spec, ct_spec, ct_spec, s_spec]
    out_shape = [_sds((T, DP_W), BF16), _sds((KW, CTX), F32), _sds((KW, CTX), F32), _sds((N_KV, Q_PER_KV, HEAD), F32)]
    if has_band:
        p_spec = pl.BlockSpec((S + 2 * CTX, HEAD), lambda kh, n: (0, kh))
        pt_spec = pl.BlockSpec((HEAD, S + 2 * CTX), lambda kh, n: (kh, 0))
        rope_spec = pl.BlockSpec((AB, HEAD), lambda kh, n: (n, 0))
        in_specs += [p_spec, pl.BlockSpec((S + 2 * CTX, 2 * HEAD), lambda kh, n: (0, kh)), _bias_spec(S), rope_spec,
                     rope_spec]
        args += list(band)
        out_specs += [pt_spec, pt_spec]
        out_shape += [_sds((KW, S + 2 * CTX), F32)] * 2
    alias = {}
    if prev_dq is not None:
        in_specs.append(ANY)
        alias = {len(args): 0}
        args.append(prev_dq)
    ci, ca, co, cs, cscr = _carry_args(carry)
    n_out = len(out_specs)
    res = pl.pallas_call(
        _carried(kern, carry, len(args), n_out, *_grid_ends((N_KV, nq))), name=name, grid=(N_KV, nq),
        in_specs=in_specs + ci, out_specs=out_specs + co, out_shape=out_shape + cs, scratch_shapes=cscr,
        input_output_aliases=alias, compiler_params=_params(("arbitrary", "arbitrary")),
    )(*args, *ca)
    return res if carry is None else (res[:n_out], res[n_out:])


def _dkv_assemble(name, dp, dkp, dvp, dkc_l, dvc_l, dkc_c, dvc_c, cos, sin, S):
    T = CTX + S
    KW = N_KV * HEAD

    def kern(dkp_ref, dvp_ref, dkcl_ref, dvcl_ref, dkcc_ref, dvcc_ref, cos_ref, sin_ref, dp_in, out_ref):
        i = pl.program_id(0)

        @pl.when(i == 0)
        def _():
            out_ref[...] = jnp.concatenate([(dkcl_ref[...] + dkcc_ref[...]).T, (dvcl_ref[...] + dvcc_ref[...]).T],
                                           axis=1).astype(BF16)

        @pl.when(i > 0)
        def _():
            out_ref[...] = jnp.concatenate([_unrope(dkp_ref[...].T, cos_ref[...], sin_ref[...]), dvp_ref[...].T],
                                           axis=1).astype(BF16)

    same = lambda i: (0, i)
    lat_map = lambda i: (jnp.maximum(i - 1, 0), 0)
    ctx_map = lambda i: (0, 0)
    return pl.pallas_call(
        kern, name=name, grid=(T // TR,),
        in_specs=[pl.BlockSpec((KW, TR), same), pl.BlockSpec((KW, TR), same),
                  pl.BlockSpec((KW, CTX), ctx_map), pl.BlockSpec((KW, CTX), ctx_map),
                  pl.BlockSpec((KW, CTX), ctx_map), pl.BlockSpec((KW, CTX), ctx_map),
                  pl.BlockSpec((TR, HEAD), lat_map), pl.BlockSpec((TR, HEAD), lat_map), ANY],
        out_specs=pl.BlockSpec((TR, 2 * KW), lambda i: (i, COL_K // (2 * KW))),
        out_shape=_sds((T, DP_W), BF16), input_output_aliases={8: 0}, compiler_params=_params(),
    )(dkp, dvp, dkc_l, dvc_l, dkc_c, dvc_c, cos, sin, dp)


RB = 128
CH = 256
HALO = 8
SUB = 8
GRP = 8


def _vscan(a, b, reverse):
    row = lax.broadcasted_iota(jnp.int32, a.shape, 0)
    A, H = a, b
    for s in (1, 2, 4):
        sh = SUB - s if reverse else s
        m = (row < SUB - s) if reverse else (row >= s)
        As = pltpu.roll(A, sh, 0)
        Hs = pltpu.roll(H, sh, 0)
        H = jnp.where(m, A * Hs + H, H)
        A = jnp.where(m, A * As, A)
    return A, H


def _scan_rows(a_ref, b_ref, r0, nrows, reverse, carry, emit):
    ngrp = nrows // (SUB * GRP)
    row = lax.broadcasted_iota(jnp.int32, (SUB, RB), 0)

    def grp(gi, carry):
        g = (ngrp - 1 - gi) if reverse else gi
        base = r0 + g * (SUB * GRP)
        for v in (range(GRP - 1, -1, -1) if reverse else range(GRP)):
            rs = pl.multiple_of(base + v * SUB, SUB)
            A, H = _vscan(a_ref[pl.ds(rs, SUB), :], b_ref[pl.ds(rs, SUB), :], reverse)
            hf = H + A * carry
            if reverse:
                before = jnp.where(row == SUB - 1, carry, pltpu.roll(hf, SUB - 1, 0))
                carry = hf[0:1, :]
            else:
                before = jnp.where(row == 0, carry, pltpu.roll(hf, 1, 0))
                carry = hf[SUB - 1:SUB, :]
            emit(rs, hf, before)
        return carry

    return lax.fori_loop(0, ngrp, grp, carry)


def _pad_start(ci):
    return pl.multiple_of(ci * CH + HALO * jnp.minimum(ci, 1), HALO)


def _conv_taps(ext, transpose=False):
    n = CH + 2 * HALO
    taps = []
    for k in range(CONV_W):
        off = CONV_LEFT - k if transpose else k - CONV_LEFT
        taps.append(ext[HALO:HALO + CH, :] if off == 0 else pltpu.roll(ext, (-off) % n, 0)[HALO:HALO + CH, :])
    return taps


def _lru_gates(xl, w4, b4, ls):
    pre = _dot(xl.astype(BF16), w4) + b4
    out = []
    for d in range(2):
        r = _sigmoid(pre[:, d * RB:(d + 1) * RB])
        i = _sigmoid(pre[:, (2 + d) * RB:(3 + d) * RB])
        la = LRU_C * r * ls[d:d + 1, :]
        a = jnp.exp(la)
        q = -jnp.tanh(la) * (1.0 + a * a)
        out.append((r, i, a, q))
    return out


def _rnn_specs(T):
    col = lambda n, *_: (0, n)
    return dict(
        xr=pl.BlockSpec((T, RB), lambda n, *_: (0, COL_XR // RB + n)),
        gr=pl.BlockSpec((T, RB), lambda n, *_: (0, COL_GR // RB + n)),
        act=pl.BlockSpec((T, RB), col),
        cw=pl.BlockSpec((CONV_W, RB), col), cb=pl.BlockSpec((1, RB), col),
        w4=pl.BlockSpec((None, RB, 4 * RB), lambda n, *_: (n, 0, 0)),
        b4=pl.BlockSpec((None, 1, 4 * RB), lambda n, *_: (n, 0, 0)),
        lam=pl.BlockSpec((2, RB), col))


PAD_ROWS = 3 * HALO


def _zero_pads(pad_ref, T):
    for r in (0, HALO + CTX, 2 * HALO + T):
        pad_ref[r:r + HALO, :] = jnp.zeros((HALO, RB), F32)


def _fill_padded(pad_ref, src_ref, T):
    _zero_pads(pad_ref, T)
    pad_ref[HALO:HALO + CTX, :] = src_ref[0:CTX, :].astype(F32)
    pad_ref[2 * HALO + CTX:2 * HALO + T, :] = src_ref[CTX:T, :].astype(F32)


def _pad_rows(ci):
    return pl.ds(pl.multiple_of(ci * CH + HALO + HALO * jnp.minimum(ci, 1), HALO), CH)


def _rnn_fwd(name, p, cw, cb, w4, b4, lam, T, carry=None):
    def kern(xr_ref, gr_ref, cw_ref, cb_ref, w4_ref, b4_ref, lam_ref,
             u_ref, a0, a1, yo_ref, hpf_ref, hpb_ref, r0_ref, r1_ref, i0_ref, i1_ref, xpad, b0, b1, y):
        _fill_padded(xpad, xr_ref, T)
        ls = _log_sigmoid(lam_ref[...])
        w4v, b4v, cwv, cbv = w4_ref[...], b4_ref[...], cw_ref[...], cb_ref[...]

        def chunk(ci, _):
            rows = pl.ds(pl.multiple_of(ci * CH, CH), CH)
            taps = _conv_taps(xpad[pl.ds(_pad_start(ci), CH + 2 * HALO), :])
            xl = cbv + sum(taps[k] * cwv[k:k + 1, :] for k in range(CONV_W))
            for d, (r, i, a, q) in enumerate(_lru_gates(xl, w4v, b4v, ls)):
                (a0, a1)[d][rows, :] = a
                (b0, b1)[d][rows, :] = jnp.sqrt(q) * (i * xl)
                (r0_ref, r1_ref)[d][rows, :] = r.astype(BF16)
                (i0_ref, i1_ref)[d][rows, :] = i.astype(BF16)
            return 0

        lax.fori_loop(0, T // CH, chunk, 0)
        zero = jnp.zeros((1, RB), F32)

        def emit_f(rs, hf, before):
            y[pl.ds(rs, SUB), :] = hf
            b0[pl.ds(rs, SUB), :] = before

        def emit_b(rs, hf, before):
            y[pl.ds(rs, SUB), :] += hf
            b1[pl.ds(rs, SUB), :] = before

        _scan_rows(a0, b0, 0, T, False, zero, emit_f)
        c = _scan_rows(a1, b1, 0, CTX, True, zero, emit_b)
        _scan_rows(a1, b1, CTX, T - CTX, True, c, emit_b)

        def finish(ci, _):
            rows = pl.ds(pl.multiple_of(ci * CH, CH), CH)
            yv = y[rows, :]
            u_ref[rows, :] = (yv * _gelu(gr_ref[rows, :].astype(F32))).astype(BF16)
            yo_ref[rows, :] = yv.astype(BF16)
            hpf_ref[rows, :] = b0[rows, :].astype(BF16)
            hpb_ref[rows, :] = b1[rows, :].astype(BF16)
            return 0

        lax.fori_loop(0, T // CH, finish, 0)

    sp = _rnn_specs(T)
    ci, ca, co, cs, cscr = _carry_args(carry)
    dts = [BF16, F32, F32] + [BF16] * 7
    res = pl.pallas_call(
        _carried(kern, carry, 7, 10, *_grid_ends((N_RNN_BLOCKS,))), name=name, grid=(N_RNN_BLOCKS,),
        in_specs=[sp["xr"], sp["gr"], sp["cw"], sp["cb"], sp["w4"], sp["b4"], sp["lam"]] + ci,
        out_specs=[sp["act"]] * 10 + co,
        out_shape=[_sds((T, D), dt) for dt in dts] + cs,
        scratch_shapes=[pltpu.VMEM((T + PAD_ROWS, RB), F32)] + [pltpu.VMEM((T, RB), F32)] * 3 + cscr,
        compiler_params=_params(),
    )(p, p, cw, cb, w4, b4, lam, *ca)
    return res if carry is None else (res[:10], res[10:])


def _rnn_bwd(name, p, du, saved, dp, cw, cb, w4, b4, lam, T, carry=None):
    def kern(xr_ref, gr_ref, du_ref, a0, a1, y_ref, hpf_ref, hpb_ref, r0_ref, r1_ref, i0_ref, i1_ref,
             cw_ref, cb_ref, w4_ref, b4_ref, lam_ref, dp_in,
             dp_ref, dcw_ref, dcb_ref, dw4_ref, db4_ref, dlam_ref,
             xpad, dxpad, c0, c1, dy):
        j = pl.program_id(1)

        @pl.when(j == 0)
        def _():
            scans(gr_ref, du_ref, a0, a1, y_ref, dp_ref, c0, c1, dy)

        @pl.when(j == 1)
        def _():
            gates(xr_ref, a0, a1, (hpf_ref, hpb_ref), (r0_ref, r1_ref), (i0_ref, i1_ref), cw_ref, cb_ref, w4_ref,
                  lam_ref, dp_ref, dcw_ref, dcb_ref, dw4_ref, db4_ref, dlam_ref, xpad, dxpad, c0, c1)

    def scans(gr_ref, du_ref, a0, a1, y_ref, dgr_ref, c0, c1, dy):
        def phase_a(ci, _):
            rows = pl.ds(pl.multiple_of(ci * CH, CH), CH)
            gr = gr_ref[rows, :].astype(F32)
            duv = du_ref[rows, :].astype(F32)
            dyv = duv * _gelu(gr)
            dgr_ref[rows, :] = (duv * y_ref[rows, :].astype(F32) * _gelu_grad(gr)).astype(BF16)
            dy[rows, :] = dyv
            c0[rows, :] = a0[rows, :] * dyv
            c1[rows, :] = a1[rows, :] * dyv
            return 0

        lax.fori_loop(0, T // CH, phase_a, 0)
        zero = jnp.zeros((1, RB), F32)

        def emit0(rs, hf, before):
            c0[pl.ds(rs, SUB), :] = dy[pl.ds(rs, SUB), :] + before

        def emit1(rs, hf, before):
            c1[pl.ds(rs, SUB), :] = dy[pl.ds(rs, SUB), :] + before

        _scan_rows(a0, c0, 0, T, True, zero, emit0)
        c = _scan_rows(a1, c1, CTX, T - CTX, False, zero, emit1)
        _scan_rows(a1, c1, 0, CTX, False, c, emit1)

    def gates(xr_ref, a0, a1, hp_refs, r_refs, i_refs, cw_ref, cb_ref, w4_ref, lam_ref,
              dxr_ref, dcw_ref, dcb_ref, dw4_ref, db4_ref, dlam_ref, xpad, dxpad, c0, c1):
        _fill_padded(xpad, xr_ref, T)
        _zero_pads(dxpad, T)
        lam_v = lam_ref[...]
        ls = _log_sigmoid(lam_v)
        w4v, cwv, cbv = w4_ref[...], cw_ref[...], cb_ref[...]

        def conv_chunk(ci):
            taps = _conv_taps(xpad[pl.ds(_pad_start(ci), CH + 2 * HALO), :])
            return taps, cbv + sum(taps[k] * cwv[k:k + 1, :] for k in range(CONV_W))

        dw4_ref[...] = jnp.zeros(dw4_ref.shape, F32)
        db4_ref[...] = jnp.zeros(db4_ref.shape, F32)
        dlam_ref[...] = jnp.zeros(dlam_ref.shape, F32)
        dcw_ref[...] = jnp.zeros(dcw_ref.shape, F32)
        dcb_ref[...] = jnp.zeros(dcb_ref.shape, F32)

        def phase_c(ci, _):
            base = pl.multiple_of(ci * CH, CH)
            rows = pl.ds(base, CH)
            _, xl = conv_chunk(ci)
            dxl = jnp.zeros((CH, RB), F32)
            dpre_a, dpre_x, dls = [], [], []
            for d in range(2):
                a = (a0, a1)[d][rows, :]
                r = r_refs[d][rows, :].astype(F32)
                i = i_refs[d][rows, :].astype(F32)
                q = -jnp.tanh(LRU_C * r * ls[d:d + 1, :]) * (1.0 + a * a)
                g = (c0, c1)[d][rows, :]
                hp = hp_refs[d][rows, :].astype(F32)
                gm = g * jnp.sqrt(q)
                di = gm * xl
                dxl = dxl + gm * i
                dla = a * (g * hp - a * (g * (i * xl)) * lax.rsqrt(q))
                dr = dla * (LRU_C * ls[d:d + 1, :])
                dls.append(_colsum(dla * (LRU_C * r)))
                dpre_a.append(dr * r * (1.0 - r))
                dpre_x.append(di * i * (1.0 - i))
            dpre = jnp.concatenate(dpre_a + dpre_x, axis=1)
            dpre_b = dpre.astype(BF16)
            dxl = dxl + _dot(dpre_b, w4v, NT)
            dw4_ref[...] += _dot(xl.astype(BF16), dpre_b, TN)
            db4_ref[...] += _colsum(dpre)
            dlam_ref[...] += jnp.concatenate(dls, axis=0)
            dcb_ref[...] += _colsum(dxl)
            dxpad[_pad_rows(ci), :] = dxl
            return 0

        lax.fori_loop(0, T // CH, phase_c, 0)
        dlam_ref[...] = dlam_ref[...] * _sigmoid(-lam_v)

        def phase_d(ci, _):
            base = pl.multiple_of(ci * CH, CH)
            rows = pl.ds(base, CH)
            xtaps, _ = conv_chunk(ci)
            dtaps = _conv_taps(dxpad[pl.ds(_pad_start(ci), CH + 2 * HALO), :], transpose=True)
            dxl = dxpad[_pad_rows(ci), :]
            dxr_ref[rows, :] = sum(dtaps[k] * cwv[k:k + 1, :] for k in range(CONV_W)).astype(BF16)
            dcw_ref[...] += jnp.concatenate([_colsum(dxl * xtaps[k]) for k in range(CONV_W)], axis=0)
            return 0

        lax.fori_loop(0, T // CH, phase_d, 0)

    sp = _rnn_specs(T)
    dp_spec = pl.BlockSpec((T, RB), lambda n, j: (0, COL_GR // RB + n - j * (COL_GR - COL_XR) // RB))
    ci, ca, co, cs, cscr = _carry_args(carry)
    n_in = 3 + len(saved) + 5 + 1
    res = pl.pallas_call(
        _carried(kern, carry, n_in, 6, *_grid_ends((N_RNN_BLOCKS, 2))), name=name, grid=(N_RNN_BLOCKS, 2),
        in_specs=[sp["xr"], sp["gr"]] + [sp["act"]] * (1 + len(saved)) + [sp["cw"], sp["cb"], sp["w4"], sp["b4"],
                                                                           sp["lam"], ANY] + ci,
        out_specs=[dp_spec, sp["cw"], sp["cb"], sp["w4"], sp["b4"], sp["lam"]] + co,
        out_shape=[_sds((T, DP_W), BF16), _sds((CONV_W, D), F32), _sds((1, D), F32),
                   _sds((N_RNN_BLOCKS, RB, 4 * RB), F32), _sds((N_RNN_BLOCKS, 1, 4 * RB), F32), _sds((2, D), F32)] + cs,
        scratch_shapes=[pltpu.VMEM((T + PAD_ROWS, RB), F32)] * 2 + [pltpu.VMEM((T, RB), F32)] * 3 + cscr,
        input_output_aliases={n_in - 1: 0},
        compiler_params=_params(("arbitrary", "arbitrary")),
    )(p, p, du, *saved, cw, cb, w4, b4, lam, dp, *ca)
    return res if carry is None else (res[:6], res[6:])


class _Plan:
    def __init__(self, shards, Ws):
        L = len(Ws)
        self.shards, self.Ws = shards, Ws
        self.Gs = [None] * L
        self.slots = [dict() for _ in range(L)]
        self.gate_slots = [None] * L
        self.table = {}
        for l in range(L):
            t = f"l{l}_"
            self.table[t + "rnn_fwd"] = [("gather", l, k) for k in ("wffn_in_t", "wo_rnn", "wo_attn", "wout")]
            if l + 1 < L:
                self.table[t + "attn_lat_fwd"] = [("gather", l + 1, "win_t")]
                self.table[t + "ffn_in"] = [("gather", l, "wffn_out")]
            else:
                self.table[t + "attn_lat_fwd"] = [("gather", l, "wffn_out")]
            self.table[t + "ffn_in_dx"] = [("scatter", l, "wffn_out")]
            self.table[t + "attn_lat_bwd"] = [("scatter", l, "wffn_in_t")]
            self.table[t + "proj_dx"] = [("scatter", l, "win_t_a")]
            self.table[t + "rnn_bwd"] = ([("scatter", l, k) for k in ("wout", "wo_attn", "wo_rnn")]
                                         + ([("scatter", l + 1, "win_t_b"), ("gates", l + 1, "w4")] if l + 1 < L else []))
        self.table["l0_proj_dw_b"] = [("gates", 0, "w4")]
        self.table["l0_mix_norm"] = [("gather", 0, "win_t")]

    def carry(self, name):
        jobs = []
        for kind, l, k in self.table.get(name, []):
            if kind == "gather":
                jobs.append(("gather", self.shards[l][k]))
            elif kind == "scatter":
                jobs.append(("scatter", self.Gs[l][k].reshape(N_DEV, -1, self.Gs[l][k].shape[-1])))
            else:
                jobs.append(("gather", self.Gs[l]["w4"].reshape(N_RNN_BLOCKS * RB, 4 * RB).astype(BF16)))
        return _Carry(jobs) if jobs else None

    def done(self, name, got):
        for (kind, l, k), res in zip(self.table[name], got):
            if kind == "gather":
                self.Ws[l][k] = res.reshape(-1, D)
            elif kind == "scatter":
                self.slots[l][k] = res
            else:
                self.gate_slots[l] = res


def _run(X, fn, name, *args, **kw):
    carry = None if X is None else X.carry(name)
    if carry is None:
        return fn(name, *args, **kw)
    out, got = fn(name, *args, carry=carry, **kw)
    X.done(name, got)
    return out


def _layer_fwd(l, xa, h, W, rope, S, nxt, X=None):
    T = xa.shape[0]
    tag = f"l{l}_"
    cos, sin, bias = rope
    p = _run(X, _mm_act, tag + "proj", h, W["win_t"], "NT", BF16)
    u, *rnn_saved = _run(X, _rnn_fwd, tag + "rnn_fwd", p, W["cw"], W["cb"], W["w4"], W["b4"], W["lam"], T)
    qa, kp, vp, kc, vc = _qkv_prep(tag + "qkv_prep", p, cos, sin, S)
    o_all = _attn_fwd(tag + "attn_ctx_fwd", qa, kc, vc, W["sink4"], S)
    o_all = _run(X, _attn_fwd, tag + "attn_lat_fwd", qa, kc, vc, W["sink4"], S, band=(kp, vp, bias), prev=o_all)
    ya, yb, z, m, x1, h2 = _out_fused(tag + "out", p, u, o_all, xa, W["wo_rnn"], W["wo_attn"], W["wout"],
                                      W["g_mix_post"], W["mod"], W["g_ffn_pre"])
    fg, fu, s = _run(X, _ffn_in_fused, tag + "ffn_in", h2, W["wffn_in_t"])
    e, *out = _ffn_out_fused(tag + "ffn_out", s, W["wffn_out"], x1, W["g_ffn_post"], W["mod"], nxt)
    saved = dict(xa=xa, h=h, p=p, u=u, rnn=rnn_saved, qa=qa, kp=kp, vp=vp, kc=kc, vc=vc, o_all=o_all,
                 ya=ya, yb=yb, z=z, m=m, x1=x1, h2=h2, fg=fg, fu=fu, s=s, e=e)
    return saved, out


def _layer_bwd(l, dx2, A, W, rope, S, X=None, loss_of=None):
    T = A["xa"].shape[0]
    tag = f"l{l}_"
    cos, sin, bias = rope
    G = {}
    if X is not None:
        X.Gs[l] = G
    if loss_of is None:
        de, df, dga2, G["g_ffn_post"] = _ffn_bwd_fused(tag + "ffn_bwd", A["fg"], A["fu"], W["wffn_out"],
                                                       head=(dx2, A["e"], W["g_ffn_post"], W["mod"]))
    else:
        dx2, de, dga2, G["g_ffn_post"], G["sq"] = _loss_resid_bwd(tag + "loss_ffn_resid_bwd", *loss_of, A["e"],
                                                                  W["g_ffn_post"], W["mod"], GA2)
        df, = _ffn_bwd_fused(tag + "ffn_bwd", A["fg"], A["fu"], W["wffn_out"], de=de)
    G["wffn_out"] = _mm_wgrad(tag + "ffn_out_dw", A["s"], de)
    dx1, dm, dsh2, dsc2, G["g_ffn_pre"], dga1, G["g_mix_post"] = _run(
        X, _ffn_in_bwd_fused, tag + "ffn_in_dx", df, W["wffn_in_t"], A["x1"], dx2, A["m"], W["g_ffn_pre"], W["mod"],
        W["g_mix_post"])
    G["wffn_in_t"] = _run(X, _mm_wgrad, tag + "ffn_in_dw", df, A["h2"])
    G["wout"] = _mm_wgrad(tag + "out_dw", A["z"], dm)
    dya, dyb, dgl, du, do = _out_bwd_fused(tag + "out_dx", dm, W["wout"], W["wo_rnn"], W["wo_attn"], A["p"], A["ya"],
                                           A["yb"])
    G["wo_attn"] = _mm_wgrad(tag + "o_attn_dw", A["o_all"], dyb)
    G["wo_rnn"] = _mm_wgrad(tag + "o_rnn_dw", A["u"], dya)
    dp, dkc_c, dvc_c, dsink_c = _attn_bwd(tag + "attn_ctx_bwd", A["qa"], A["kc"], A["vc"], W["sink4"], A["o_all"], do, S)
    dp, dkc_l, dvc_l, dsink_l, dkp, dvp = _run(
        X, _attn_bwd, tag + "attn_lat_bwd", A["qa"], A["kc"], A["vc"], W["sink4"], A["o_all"], do, S,
        band=(A["kp"], A["vp"], bias, cos, sin), prev_dq=dp)
    G["sink4"] = dsink_c + dsink_l
    dp = _dkv_assemble(tag + "dkv", dp, dkp, dvp, dkc_l, dvc_l, dkc_c, dvc_c, cos, sin, S)
    dp, G["cw"], G["cb"], G["w4"], G["b4"], G["lam"] = _run(
        X, _rnn_bwd, tag + "rnn_bwd", A["p"], du, A["rnn"], dp, W["cw"], W["cb"], W["w4"], W["b4"], W["lam"], T)
    proj_dx = (_proj_bwd_fused, tag + "proj_dx", dp, dgl, W["win_t"], A["xa"], dx1, W["g_mix_pre"], W["mod"])
    if X is not None:
        G["win_t_a"] = _proj_wgrad(tag + "proj_dw_a", dp, dgl, A["h"][:, :D // 2])
        dxa, dsh1, dsc1, G["g_mix_pre"] = _run(X, *proj_dx)
        G["win_t_b"] = _run(X, _proj_wgrad, tag + "proj_dw_b", dp, dgl, A["h"][:, D // 2:])
    else:
        dxa, dsh1, dsc1, G["g_mix_pre"] = _run(X, *proj_dx)
        G["win_t"] = _proj_wgrad(tag + "proj_dw", dp, dgl, A["h"])
    G["mod"] = jnp.concatenate([dsh1, dsc1, dga1, dsh2, dsc2, dga2], axis=1)
    return dxa, G


def _local_step(xa, target, Ws, S, X=None):
    rope = (*_rope_tables(S), _band_bias(S))
    L = len(Ws)
    h = _run(X, _normmod_fwd, "l0_mix_norm", xa, Ws[0]["g_mix_pre"], Ws[0]["mod"], SH1, SC1)
    saved = []
    x = xa
    for l in range(L):
        nxt = (Ws[l + 1]["g_mix_pre"], Ws[l + 1]["mod"]) if l + 1 < L else None
        A, out = _layer_fwd(l, x, h, Ws[l], rope, S, nxt, X)
        saved.append(A)
        if l + 1 < L:
            x, h = out
    Gs = [None] * L
    dx = None
    for l in reversed(range(L)):
        dx, Gs[l] = _layer_bwd(l, dx, saved[l], Ws[l], rope, S, X, loss_of=(out[0], target) if l == L - 1 else None)
    return Gs[L - 1]["sq"], dx, Gs


MESH = pl.DeviceIdType.MESH


def _place():
    return lax.axis_index("x"), lax.axis_index("y"), lax.axis_index("c")


def _lin(px, py, pc):
    return 4 * px + 2 * py + pc


def _allgather_small(name, blk):
    m, n = blk.shape

    def body(x_ref, out_ref, send_sems, recv_sems, local_sem):
        x, y, c = _place()
        me, sibling = (x, y, c), (x, y, 1 - c)
        chips = [(1 - x, y), (x, 1 - y), (1 - x, 1 - y)]

        def copy(k, block, to, src=None):
            dst = out_ref.at[_lin(*block)]
            return pltpu.make_async_remote_copy(src_ref=dst if src is None else src, dst_ref=dst,
                                                send_sem=send_sems.at[k], recv_sem=recv_sems.at[k],
                                                device_id=to, device_id_type=MESH)

        mine = pltpu.make_async_copy(x_ref, out_ref.at[_lin(*me)], local_sem)
        mine.start()
        first = [copy(0, me, sibling, src=x_ref)]
        first += [copy(1 + j, me, (*chip, c), src=x_ref) for j, chip in enumerate(chips)]
        for cp in first:
            cp.start()
        passed = [copy(4 + j, (*chip, c), sibling) for j, chip in enumerate(chips)]
        for j, chip in enumerate(chips):
            copy(1 + j, (*chip, c), me).wait_recv()
            passed[j].start()
        copy(0, sibling, me).wait_recv()
        for j, chip in enumerate(chips):
            copy(4 + j, (*chip, 1 - c), me).wait_recv()
        for cp in first + passed:
            cp.wait_send()
        mine.wait()

    return pl.pallas_call(
        body, name=name, out_shape=_sds((N_DEV, m, n), blk.dtype),
        in_specs=[pl.BlockSpec(memory_space=pltpu.VMEM)], out_specs=pl.BlockSpec(memory_space=pltpu.VMEM),
        scratch_shapes=[pltpu.SemaphoreType.DMA((7,)), pltpu.SemaphoreType.DMA((7,)), pltpu.SemaphoreType.DMA],
        compiler_params=pltpu.CompilerParams(vmem_limit_bytes=VMEM_LIMIT),
    )(blk)


def _exchange_shards(name, grads, L):
    nw = len(grads)
    na = nw * L
    flat = [g for per_layer in grads for g in per_layer]

    def body(*refs):
        ins, outs = refs[:na], refs[na:na + nw]
        send_sems, recv_sems, local_sems = refs[na + nw:]
        x, y, c = _place()
        me = _lin(x, y, c)
        peers = [(x ^ ((k + 1) >> 2 & 1), y ^ ((k + 1) >> 1 & 1), c ^ ((k + 1) & 1)) for k in range(7)]

        def copy(a, k, src_blk, dst_blk):
            return pltpu.make_async_remote_copy(src_ref=ins[a].at[src_blk], dst_ref=outs[a // L].at[a % L, dst_blk],
                                                send_sem=send_sems.at[a, k], recv_sem=recv_sems.at[a, k],
                                                device_id=peers[k], device_id_type=MESH)

        mine = [pltpu.make_async_copy(ins[a].at[me], outs[a // L].at[a % L, me], local_sems.at[a]) for a in range(na)]
        for cp in mine:
            cp.start()
        sent = [copy(a, k, _lin(*peers[k]), me) for a in range(na) for k in range(7)]
        for cp in sent:
            cp.start()
        for a in range(na):
            for k in range(7):
                copy(a, k, me, _lin(*peers[k])).wait_recv()
        for cp in sent:
            cp.wait_send()
        for cp in mine:
            cp.wait()

    return pl.pallas_call(
        body, name=name, out_shape=[_sds((L, *per_layer[0].shape), per_layer[0].dtype) for per_layer in grads],
        in_specs=[ANY] * na, out_specs=[ANY] * nw,
        scratch_shapes=[pltpu.SemaphoreType.DMA((na, 7)), pltpu.SemaphoreType.DMA((na, 7)),
                        pltpu.SemaphoreType.DMA((na,))],
    )(*flat)


MOD_ROWS = 16
MOD_SHARD = 6 * D // N_DEV
HI = lax.Precision.HIGHEST


def _mod_fwd(name, c9, w_mod, b_shard):
    L = w_mod.shape[0]

    def kern(c_ref, w_ref, b_ref, o_ref):
        o_ref[...] = lax.dot_general(_silu(c_ref[...]), w_ref[...], NN, precision=HI,
                                     preferred_element_type=F32) + b_ref[...]

    return pl.pallas_call(
        kern, name=name, grid=(L,),
        in_specs=[_full_spec(c9.shape), pl.BlockSpec((None, D, MOD_SHARD), lambda l: (l, 0, 0)),
                  pl.BlockSpec((None, 1, MOD_SHARD), lambda l: (l, 0, 0))],
        out_specs=pl.BlockSpec((None, MOD_ROWS, MOD_SHARD), lambda l: (l, 0, 0)),
        out_shape=_sds((L, MOD_ROWS, MOD_SHARD), F32), compiler_params=_params(),
    )(c9, w_mod, b_shard)


def _mod_bwd(name, c9, w_mod, dmod_all, dmod_cols):
    L = w_mod.shape[0]

    def rows9(ref, l):
        own = jnp.concatenate([ref[j, 2 * l + 1:2 * l + 2, :] for j in range(N_DEV)], axis=0)
        ctx = ref[0, 2 * l:2 * l + 1, :]
        for j in range(1, N_DEV):
            ctx = ctx + ref[j, 2 * l:2 * l + 1, :]
        return own, ctx

    def kern(c_ref, w_ref, all_ref, cols_ref, gw_ref, gb_ref, gc_ref):
        l = pl.program_id(0)
        for ll in range(L):
            @pl.when(l == ll)
            def _():
                own, ctx = rows9(all_ref, ll)
                gb_ref[...] = _colsum(own) + ctx
                own_s, ctx_s = rows9(cols_ref, ll)
                r16 = jnp.concatenate([own_s, ctx_s, jnp.zeros((MOD_ROWS - N_DEV - 1, MOD_SHARD), F32)], axis=0)
                gw_ref[...] = lax.dot_general(_silu(c_ref[...]), r16, TN, precision=HI, preferred_element_type=F32)
                part = lax.dot_general(r16, w_ref[...], NT, precision=HI,
                                       preferred_element_type=F32)[N_DEV:N_DEV + 1, :]
                if ll == 0:
                    gc_ref[...] = part
                else:
                    gc_ref[...] += part

    return pl.pallas_call(
        kern, name=name, grid=(L,),
        in_specs=[_full_spec(c9.shape), pl.BlockSpec((None, D, MOD_SHARD), lambda l: (l, 0, 0)),
                  _full_spec(dmod_all.shape), _full_spec(dmod_cols.shape)],
        out_specs=[pl.BlockSpec((None, D, MOD_SHARD), lambda l: (l, 0, 0)),
                   pl.BlockSpec((None, 1, 6 * D), lambda l: (l, 0, 0)), _full_spec((1, D))],
        out_shape=[_sds((L, D, MOD_SHARD), F32), _sds((L, 1, 6 * D), F32), _sds((1, D), F32)],
        compiler_params=_params(),
    )(c9, w_mod, dmod_all, dmod_cols)


_BC1 = 1.0 - ADAM_B1 ** ADAM_STEP
_BC2 = 1.0 - ADAM_B2 ** ADAM_STEP


def _adamw_vals(w, g, m, v):
    m = ADAM_B1 * m + (1.0 - ADAM_B1) * g
    v = ADAM_B2 * v + (1.0 - ADAM_B2) * (g * g)
    delta = -ADAM_LR * ((m / _BC1) / (jnp.sqrt(v / _BC2) + ADAM_EPS) + ADAM_WD * w)
    return delta, m, v


def _adamw(name, w, g, m, v, tile):
    R, C = w.shape
    blk = ((tile, C), lambda i: (i, 0))

    def body(i, ins, ps, outs, acc):
        d, mm, vv = _adamw_vals(ins[0][...], ins[1][...], ins[2][...], ins[3][...])
        outs[0][...] = d
        outs[1][...] = mm
        outs[2][...] = vv

    return _ew(name, body, R // tile, [(a, *blk) for a in (w, g, m, v)], [], [(_sds((R, C), F32), *blk)] * 3)


def _sum_slots(ref):
    g = ref[0].astype(F32)
    for j in range(1, N_DEV):
        g = g + ref[j].astype(F32)
    return g


def _adamw_slots(name, slots, shape, tile, wmv=None):
    L, R, C = shape
    n = R // tile
    spec = pl.BlockSpec((None, tile, C), lambda l, i: (l, i, 0))
    pieces = [s if isinstance(s, (list, tuple)) else [s] for s in slots]
    layer_of = [ll for ll, ps in enumerate(pieces) for _ in ps]
    flat = [p for ps in pieces for p in ps]
    wmv = list(wmv or [])

    def slot_spec(ll, cols):
        return pl.BlockSpec((N_DEV, tile, cols),
                            lambda l, i: (0, jnp.where(l == ll, i, jnp.where(l < ll, 0, n - 1)), 0))

    def kern(*refs):
        s_refs = refs[:len(flat)]
        rest = refs[len(flat):]
        l = pl.program_id(0)
        for ll in range(L):
            @pl.when(l == ll)
            def _():
                parts = [_sum_slots(r) for r, lr in zip(s_refs, layer_of) if lr == ll]
                g = parts[0] if len(parts) == 1 else jnp.concatenate(parts, axis=1)
                if wmv:
                    w_ref, m_ref, v_ref, g_ref, d_ref, mo_ref, vo_ref = rest
                    d_ref[...], mo_ref[...], vo_ref[...] = _adamw_vals(w_ref[...], g, m_ref[...], v_ref[...])
                else:
                    g_ref, = rest
                g_ref[...] = g

    n_out = 4 if wmv else 1
    return pl.pallas_call(
        kern, name=name, grid=(L, n),
        in_specs=[slot_spec(ll, p.shape[-1]) for ll, p in zip(layer_of, flat)] + [spec] * len(wmv),
        out_specs=[spec] * n_out, out_shape=[_sds((L, R, C), F32)] * n_out,
        compiler_params=_params(("arbitrary", "arbitrary")),
    )(*flat, *wmv)


def _sum_blocks(name, blocks):
    _, R, C = blocks.shape

    def kern(b_ref, o_ref):
        o_ref[...] = _sum_slots(b_ref)

    return pl.pallas_call(kern, name=name, in_specs=[_full_spec(blocks.shape)], out_specs=_full_spec((R, C)),
                          grid=(1,), out_shape=_sds((R, C), F32), compiler_params=_params())(blocks)


BIG = ("win_t", "wo_rnn", "wo_attn", "wout", "wffn_in_t", "wffn_out")
BIG_SRC = ("w_in", "w_o_rnn", "w_o_attn", "w_out", "w_ffn_in", "w_ffn_out")
BIG_T = (True, False, False, False, True, False)
BIG_TILE = (176, 128, 128, 128, 176, 176)


def _chan_full(g8):
    return jnp.transpose(g8, (1, 0, 2)).reshape(g8.shape[1], D)


def kernel(x, c, ctx, c_ctx, w_mod, b_mod, g_mix_pre, g_mix_post, g_ffn_pre, g_ffn_post, w_in, conv_w, conv_b, lru_wa, lru_ba, lru_wx, lru_bx, lru_lam, attn_sink, w_o_rnn, w_o_attn, w_out, w_ffn_in, w_ffn_out, loss_target, m_c_ctx, m_w_mod, m_b_mod, m_g_mix_pre, m_g_mix_post, m_g_ffn_pre, m_g_ffn_post, m_w_in, m_conv_w, m_conv_b, m_lru_wa, m_lru_ba, m_lru_wx, m_lru_bx, m_lru_lam, m_attn_sink, m_w_o_rnn, m_w_o_attn, m_w_out, m_w_ffn_in, m_w_ffn_out, v_c_ctx, v_w_mod, v_b_mod, v_g_mix_pre, v_g_mix_post, v_g_ffn_pre, v_g_ffn_post, v_w_in, v_conv_w, v_conv_b, v_lru_wa, v_lru_ba, v_lru_wx, v_lru_bx, v_lru_lam, v_attn_sink, v_w_o_rnn, v_w_o_attn, v_w_out, v_w_ffn_in, v_w_ffn_out):
    P = dict(c_ctx=c_ctx, w_mod=w_mod, b_mod=b_mod, g_mix_pre=g_mix_pre, g_mix_post=g_mix_post, g_ffn_pre=g_ffn_pre,
             g_ffn_post=g_ffn_post, w_in=w_in, conv_w=conv_w, conv_b=conv_b, lru_wa=lru_wa, lru_ba=lru_ba,
             lru_wx=lru_wx, lru_bx=lru_bx, lru_lam=lru_lam, attn_sink=attn_sink, w_o_rnn=w_o_rnn, w_o_attn=w_o_attn,
             w_out=w_out, w_ffn_in=w_ffn_in, w_ffn_out=w_ffn_out)
    Mo = dict(c_ctx=m_c_ctx, w_mod=m_w_mod, b_mod=m_b_mod, g_mix_pre=m_g_mix_pre, g_mix_post=m_g_mix_post,
              g_ffn_pre=m_g_ffn_pre, g_ffn_post=m_g_ffn_post, w_in=m_w_in, conv_w=m_conv_w, conv_b=m_conv_b,
              lru_wa=m_lru_wa, lru_ba=m_lru_ba, lru_wx=m_lru_wx, lru_bx=m_lru_bx, lru_lam=m_lru_lam,
              attn_sink=m_attn_sink, w_o_rnn=m_w_o_rnn, w_o_attn=m_w_o_attn, w_out=m_w_out, w_ffn_in=m_w_ffn_in,
              w_ffn_out=m_w_ffn_out)
    Vo = dict(c_ctx=v_c_ctx, w_mod=v_w_mod, b_mod=v_b_mod, g_mix_pre=v_g_mix_pre, g_mix_post=v_g_mix_post,
              g_ffn_pre=v_g_ffn_pre, g_ffn_post=v_g_ffn_post, w_in=v_w_in, conv_w=v_conv_w, conv_b=v_conv_b,
              lru_wa=v_lru_wa, lru_ba=v_lru_ba, lru_wx=v_lru_wx, lru_bx=v_lru_bx, lru_lam=v_lru_lam,
              attn_sink=v_attn_sink, w_o_rnn=v_w_o_rnn, w_o_attn=v_w_o_attn, w_out=v_w_out, w_ffn_in=v_w_ffn_in,
              w_ffn_out=v_w_ffn_out)
    L = w_in.shape[0]
    S = x.shape[1]
    me = _lin(*_place())

    small = jnp.concatenate([c.reshape(8, 128), conv_w.reshape(L * CONV_W, 128), lru_ba.reshape(2 * L, 128),
                             lru_bx.reshape(2 * L, 128), lru_lam.reshape(2 * L, 128), jnp.zeros((4, 128), F32)], axis=0)
    small_all = _allgather_small("ag_small", small)
    c_all = small_all[:, 0:8].reshape(N_DEV, D)
    conv_w_f = _chan_full(small_all[:, 8:16]).reshape(L, CONV_W, D)
    lru_ba_f = _chan_full(small_all[:, 16:20]).reshape(L, 2, D)
    lru_bx_f = _chan_full(small_all[:, 20:24]).reshape(L, 2, D)
    lru_lam_f = _chan_full(small_all[:, 24:28]).reshape(L, 2, D)

    c9 = jnp.concatenate([c_all, c_ctx[None], jnp.zeros((MOD_ROWS - N_DEV - 1, D), F32)], axis=0)
    b_shard = lax.dynamic_slice_in_dim(b_mod, me * MOD_SHARD, MOD_SHARD, axis=1)[:, None, :]
    mod_part = _mod_fwd("mod_fwd", c9, w_mod, b_shard)
    mod_all = _allgather_small("ag_mod", mod_part.reshape(L * MOD_ROWS, MOD_SHARD))
    mod_all = jnp.transpose(mod_all.reshape(N_DEV, L, MOD_ROWS, MOD_SHARD), (1, 2, 0, 3)).reshape(L, MOD_ROWS, 6 * D)
    own_row = lax.dynamic_index_in_dim(mod_all, me, axis=1, keepdims=False)
    modrows = jnp.stack([mod_all[:, N_DEV], own_row], axis=1)

    shards = [{k: (P[src][l].T if tr else P[src][l]).astype(BF16) for k, src, tr in zip(BIG, BIG_SRC, BIG_T)}
              for l in range(L)]
    Ws = []
    for l in range(L):
        W = {}
        W.update(
            cw=conv_w_f[l], cb=conv_b[l][None],
            w4=jnp.concatenate([lru_wa[l, 0], lru_wa[l, 1], lru_wx[l, 0], lru_wx[l, 1]], axis=-1).astype(BF16),
            b4=jnp.concatenate([lru_ba_f[l, 0].reshape(N_RNN_BLOCKS, 1, RB), lru_ba_f[l, 1].reshape(N_RNN_BLOCKS, 1, RB),
                                lru_bx_f[l, 0].reshape(N_RNN_BLOCKS, 1, RB), lru_bx_f[l, 1].reshape(N_RNN_BLOCKS, 1, RB)],
                               axis=-1),
            lam=lru_lam_f[l], sink4=jnp.broadcast_to(attn_sink[l].reshape(N_KV, Q_PER_KV, 1), (N_KV, Q_PER_KV, HEAD)),
            g_mix_pre=g_mix_pre[l][None], g_mix_post=g_mix_post[l][None], g_ffn_pre=g_ffn_pre[l][None],
            g_ffn_post=g_ffn_post[l][None], mod=modrows[l])
        Ws.append(W)

    xa = jnp.concatenate([ctx[0], x[0]], axis=0)
    plan = _Plan(shards, Ws)
    sq, dxa, Gs = _local_step(xa, loss_target[0], Ws, S, plan)
    loss_part = ((0.5 / D) * jnp.sum(sq)).reshape(1, 1)
    grad_x = dxa[CTX:][None]

    dmod = jnp.concatenate([Gs[l]["mod"] for l in range(L)] + [jnp.zeros((8 - 2 * L, 6 * D), F32)], axis=0)
    dmod_all = _allgather_small("ag_dmod", dmod)
    dmod_cols = lax.dynamic_slice_in_dim(dmod_all, me * MOD_SHARD, MOD_SHARD, axis=2)
    g_w_mod, g_b_mod, dsc_part = _mod_bwd("mod_bwd", c9, w_mod, dmod_all, dmod_cols)
    g_b_mod = g_b_mod[:, 0]

    def rows(name, shape):
        return jnp.concatenate([Gs[l][name].reshape(shape) for l in range(L)], axis=0)

    b4g = [Gs[l]["b4"].reshape(N_RNN_BLOCKS, 4, RB) for l in range(L)]
    sink_row = jnp.concatenate([Gs[l]["sink4"][:, :, 0].reshape(1, N_Q) for l in range(L)]
                               + [loss_part, jnp.zeros((1, D - L * N_Q - 1), F32)], axis=1)
    small_g = jnp.concatenate(
        [rows("g_mix_pre", (1, D)), rows("g_mix_post", (1, D)), rows("g_ffn_pre", (1, D)), rows("g_ffn_post", (1, D)),
         rows("cb", (1, D)), rows("cw", (CONV_W, D))]
        + [b4g[l][:, d].reshape(1, D) for l in range(L) for d in range(2)]
        + [b4g[l][:, 2 + d].reshape(1, D) for l in range(L) for d in range(2)]
        + [rows("lam", (2, D)), sink_row, dsc_part], axis=0)
    n_small = small_g.shape[0]
    small_tot = _sum_blocks("sum_small", _allgather_small("ag_small_grads", small_g))
    o = 0
    G = {}
    for name in ("g_mix_pre", "g_mix_post", "g_ffn_pre", "g_ffn_post", "conv_b"):
        G[name] = small_tot[o:o + L]
        o += L
    G["conv_w"] = small_tot[o:o + L * CONV_W].reshape(L, CONV_W, D)
    o += L * CONV_W
    for name in ("lru_ba", "lru_bx", "lru_lam"):
        G[name] = small_tot[o:o + 2 * L].reshape(L, 2, D)
        o += 2 * L
    G["attn_sink"] = small_tot[o, :L * N_Q].reshape(L, N_Q)
    loss = small_tot[o, L * N_Q]
    sg = jax.nn.sigmoid(c_ctx)
    G["c_ctx"] = small_tot[o + 1] * (sg * (1.0 + c_ctx * (1.0 - sg)))
    G["b_mod"] = g_b_mod
    G["w_mod"] = g_w_mod

    last_slots, = _exchange_shards("exchange_w_in0", [[Gs[0]["win_t_b"].reshape(N_DEV, -1, D // 2)]], 1)
    plan.slots[0]["win_t_b"] = last_slots[0]
    for l in range(L):
        plan.slots[l]["win_t"] = [plan.slots[l]["win_t_a"], plan.slots[l]["win_t_b"]]

    out_g, out_d, out_m, out_v = {}, {}, {}, {}

    def put(name, res, shape=None):
        g, d, m, v = res
        for dst, val in ((out_g, g), (out_d, d), (out_m, m), (out_v, v)):
            dst[name] = val if shape is None else val.reshape(shape)

    for k, src, tr, tile in zip(BIG, BIG_SRC, BIG_T, BIG_TILE):
        lay = (lambda a: jnp.swapaxes(a, 1, 2)) if tr else (lambda a: a)
        wmv = (lay(P[src]), lay(Mo[src]), lay(Vo[src]))
        res = _adamw_slots("adamw_" + src, [plan.slots[l][k] for l in range(L)], wmv[0].shape, tile, wmv)
        put(src, [lay(r) for r in res])
    res = _adamw("adamw_w_mod", w_mod.reshape(L * D, MOD_SHARD), g_w_mod.reshape(L * D, MOD_SHARD),
                 m_w_mod.reshape(L * D, MOD_SHARD), v_w_mod.reshape(L * D, MOD_SHARD), 256)
    put("w_mod", (g_w_mod,) + tuple(res), w_mod.shape)
    def fuse4(wa, wx):
        return jnp.concatenate([wa[:, 0], wa[:, 1], wx[:, 0], wx[:, 1]], axis=-1).reshape(L, N_RNN_BLOCKS * RB, 4 * RB)

    res = _adamw_slots("adamw_gates", plan.gate_slots, (L, N_RNN_BLOCKS * RB, 4 * RB), 256,
                       (fuse4(lru_wa, lru_wx), fuse4(m_lru_wa, m_lru_wx), fuse4(v_lru_wa, v_lru_wx)))
    res = [r.reshape(L, N_RNN_BLOCKS, RB, 4, RB) for r in res]
    put("lru_wa", [jnp.stack([r[:, :, :, 0], r[:, :, :, 1]], axis=1) for r in res])
    put("lru_wx", [jnp.stack([r[:, :, :, 2], r[:, :, :, 3]], axis=1) for r in res])
    rep = ("g_mix_pre", "g_mix_post", "g_ffn_pre", "g_ffn_post", "conv_b", "b_mod")

    def pack_rep(T_):
        sink = jnp.concatenate([T_["attn_sink"].reshape(1, L * N_Q), jnp.zeros((1, D - L * N_Q), F32)], axis=1)
        return jnp.concatenate([T_[n].reshape(-1, D) for n in rep] + [sink, T_["c_ctx"][None]], axis=0)

    pk = [pack_rep(T_) for T_ in (P, G, Mo, Vo)]
    n_rep = pk[0].shape[0]
    res = _adamw("adamw_replicated", *[jnp.pad(a, ((0, 24 - n_rep), (0, 0))) for a in pk], 24)
    res = (pk[1],) + tuple(r[:n_rep] for r in res)
    o = 0
    for n in rep:
        k = P[n].size // D
        put(n, [r[o:o + k] for r in res], P[n].shape)
        o += k
    put("attn_sink", [r[o, :L * N_Q] for r in res], attn_sink.shape)
    put("c_ctx", [r[o + 1] for r in res], c_ctx.shape)
    chan = ("conv_w", "lru_ba", "lru_bx", "lru_lam")
    g_own = {n: lax.dynamic_slice_in_dim(G[n], me * RB, RB, axis=2) for n in chan}

    def pack_chan(T_):
        return jnp.concatenate([T_[n].reshape(-1, RB) for n in chan], axis=0)

    pk = [pack_chan(T_) for T_ in (P, g_own, Mo, Vo)]
    n_ch = pk[0].shape[0]
    res = _adamw("adamw_channels", *[jnp.pad(a, ((0, 24 - n_ch), (0, 0))) for a in pk], 24)
    res = (pk[1],) + tuple(r[:n_ch] for r in res)
    o = 0
    for n in chan:
        k = P[n].size // RB
        put(n, [r[o:o + k] for r in res], P[n].shape)
        o += k

    order = ("c_ctx", "w_mod", "b_mod", "g_mix_pre", "g_mix_post", "g_ffn_pre", "g_ffn_post", "w_in", "conv_w", "conv_b",
             "lru_wa", "lru_ba", "lru_wx", "lru_bx", "lru_lam", "attn_sink", "w_o_rnn", "w_o_attn", "w_out", "w_ffn_in",
             "w_ffn_out")
    return (loss, grad_x, *[out_g[n] for n in order], *[out_d[n] for n in order], *[out_m[n] for n in order],
            *[out_v[n] for n in order])
```

```python
import functools
import math

import numpy as np
import jax
import jax.numpy as jnp
from jax import lax
from jax.experimental import pallas as pl
from jax.experimental.pallas import tpu as pltpu

F32 = jnp.float32
BF16 = jnp.bfloat16

D = 1024
CTX = 256
TR = 256
HEAD = 128
N_Q = 8
N_KV = 2
Q_PER_KV = N_Q // N_KV
GRID_W = 64
N_FREQ = HEAD // 4
ROPE_BASE = 10000.0
N_RNN_BLOCKS = 8
CONV_W = 4
CONV_LEFT = 2
LRU_C = 8.0
D_FF = 2816
IN_W = 5632
P_W = IN_W
DP_W = 3584
COL_XR, COL_GR, COL_Q, COL_K, COL_V, COL_GL = 0, 1024, 2048, 3072, 3328, 3584
GLB = 512
EPS = 1e-6
NEG_INF = -1e30
ATT_SCALE = HEAD ** -0.5
N_DEV = 8
VMEM_LIMIT = 56 * 1024 * 1024

ADAM_LR, ADAM_B1, ADAM_B2, ADAM_EPS, ADAM_WD, ADAM_STEP = 0.001, 0.9, 0.999, 1e-08, 0.01, 10

NN = (((1,), (0,)), ((), ()))
NT = (((1,), (1,)), ((), ()))
TN = (((0,), (0,)), ((), ()))


def _dot(a, b, dims=NN):
    return lax.dot_general(a, b, dims, preferred_element_type=F32)


def _params(sem=("arbitrary",)):
    return pltpu.CompilerParams(dimension_semantics=sem, vmem_limit_bytes=VMEM_LIMIT)


def _full_spec(shape):
    nd = len(shape)
    return pl.BlockSpec(shape, lambda *_: (0,) * nd)


ANY = pl.BlockSpec(memory_space=pl.ANY)


def _ew(name, body, n, row_ins, pars, row_outs, accs=(), alias=None):
    n_ri, n_p, n_ro, n_acc = len(row_ins), len(pars), len(row_outs), len(accs)

    def kern(*refs):
        i = pl.program_id(0)
        ins = refs[:n_ri]
        ps = refs[n_ri:n_ri + n_p]
        outs = refs[n_ri + n_p:n_ri + n_p + n_ro]
        acc = refs[n_ri + n_p + n_ro:]
        if n_acc:
            @pl.when(i == 0)
            def _():
                for a in acc:
                    a[...] = jnp.zeros(a.shape, a.dtype)
        body(i, ins, ps, outs, acc)

    in_specs = [ANY if blk is None else pl.BlockSpec(blk, imap) for (_, blk, imap) in row_ins]
    in_specs += [_full_spec(p.shape) for p in pars]
    out_specs = [pl.BlockSpec(blk, imap) for (_, blk, imap) in row_outs] + [_full_spec(a.shape) for a in accs]
    out_shape = [s for (s, _, _) in row_outs] + list(accs)
    return pl.pallas_call(
        kern, name=name, grid=(n,), in_specs=in_specs, out_specs=out_specs, out_shape=out_shape,
        input_output_aliases=alias or {}, compiler_params=_params(),
    )(*[a for (a, _, _) in row_ins], *pars)


def _rowblk(width, colblk=0, roff=0, tile=TR):
    return (tile, width), (lambda i: (i + roff, colblk))


def _sds(shape, dtype):
    return jax.ShapeDtypeStruct(shape, dtype)


class _Carry:
    SAME_CORE = (1, 3, 5)

    def __init__(self, jobs):
        self.jobs = list(jobs)
        self.arrays = [a for _, a in self.jobs]
        self.out_shapes = [_sds(a.shape if kind == "scatter" else (N_DEV, *a.shape), a.dtype) for kind, a in self.jobs]
        n = len(self.jobs)
        self.scratch = [pltpu.SemaphoreType.DMA((n, 7)), pltpu.SemaphoreType.DMA((n, 7)), pltpu.SemaphoreType.DMA((n,))]

    def _setup(self, sems):
        send_sems, recv_sems, local_sems = sems
        x, y, c = _place()
        me = _lin(x, y, c)
        peers = [(x ^ ((k + 1) >> 2 & 1), y ^ ((k + 1) >> 1 & 1), c ^ ((k + 1) & 1)) for k in range(7)]

        def copy(a, k, sem_k, src, dst):
            return pltpu.make_async_remote_copy(src_ref=src, dst_ref=dst, send_sem=send_sems.at[a, sem_k],
                                                recv_sem=recv_sems.at[a, sem_k], device_id=peers[k], device_id_type=MESH)

        return me, [_lin(*p) for p in peers], copy, local_sems

    def _local(self, a, kind, ins, outs, me, local_sems):
        return pltpu.make_async_copy(ins[a].at[me] if kind == "scatter" else ins[a], outs[a].at[me], local_sems.at[a])

    def start(self, ins, outs, sems):
        me, theirs, copy, local_sems = self._setup(sems)
        for a, (kind, _) in enumerate(self.jobs):
            self._local(a, kind, ins, outs, me, local_sems).start()
            if kind == "scatter":
                for k in range(7):
                    copy(a, k, k, ins[a].at[theirs[k]], outs[a].at[me]).start()
            else:
                for k in (0,) + self.SAME_CORE:
                    copy(a, k, k, ins[a], outs[a].at[me]).start()

    def wait(self, ins, outs, sems):
        me, theirs, copy, local_sems = self._setup(sems)
        for a, (kind, _) in enumerate(self.jobs):
            if kind == "scatter":
                for k in range(7):
                    copy(a, k, k, ins[a].at[me], outs[a].at[theirs[k]]).wait_recv()
                for k in range(7):
                    copy(a, k, k, ins[a].at[theirs[k]], outs[a].at[me]).wait_send()
            else:
                for k in self.SAME_CORE:
                    blk = outs[a].at[theirs[k]]
                    copy(a, k, k, ins[a], blk).wait_recv()
                    copy(a, 0, k + 1, blk, blk).start()
                copy(a, 0, 0, ins[a], outs[a].at[theirs[0]]).wait_recv()
                for k in self.SAME_CORE:
                    copy(a, 0, k + 1, ins[a], outs[a].at[theirs[k + 1]]).wait_recv()
                for k in (0,) + self.SAME_CORE:
                    copy(a, k, k, ins[a], outs[a].at[me]).wait_send()
                for k in self.SAME_CORE:
                    blk = outs[a].at[theirs[k]]
                    copy(a, 0, k + 1, blk, blk).wait_send()
            self._local(a, kind, ins, outs, me, local_sems).wait()


def _carried(kern, carry, n_in, n_out, first, last):
    if carry is None:
        return kern
    nc = len(carry.jobs)

    def wrapped(*refs):
        ins, cin = refs[:n_in], refs[n_in:n_in + nc]
        outs, cout = refs[n_in + nc:n_in + nc + n_out], refs[n_in + nc + n_out:n_in + 2 * nc + n_out]
        scr, sems = refs[n_in + 2 * nc + n_out:-3], refs[-3:]

        @pl.when(first())
        def _():
            carry.start(cin, cout, sems)

        kern(*ins, *outs, *scr)

        @pl.when(last())
        def _():
            carry.wait(cin, cout, sems)

    return wrapped


def _carry_args(carry):
    if carry is None:
        return [], [], [], [], []
    n = len(carry.jobs)
    return [ANY] * n, carry.arrays, [ANY] * n, carry.out_shapes, carry.scratch


def _grid_ends(dims):
    first = lambda: functools.reduce(jnp.logical_and, [pl.program_id(d) == 0 for d in range(len(dims))])
    last = lambda: functools.reduce(jnp.logical_and, [pl.program_id(d) == n - 1 for d, n in enumerate(dims)])
    return first, last


def _mm_call(name, a, b, mode, out_dtype, tm, tn, rows_outer=True, single_b=False, carry=None):
    if mode == "TN":
        (K, M), N = a.shape, b.shape[1]
    else:
        (M, K), N = a.shape, (b.shape[1] if mode == "NN" else b.shape[0])
    assert M % tm == 0 and N % tn == 0, (name, M, N, K, tm, tn)
    ij = (lambda g0, g1: (g0, g1)) if rows_outer else (lambda g0, g1: (g1, g0))
    grid = (M // tm, N // tn) if rows_outer else (N // tn, M // tm)
    if mode == "TN":
        a_spec = pl.BlockSpec((K, tm), lambda g0, g1: (0, ij(g0, g1)[0]))
    else:
        a_spec = pl.BlockSpec((tm, K), lambda g0, g1: (ij(g0, g1)[0], 0))
    b_blk, b_map = ((tn, K), lambda g0, g1: (ij(g0, g1)[1], 0)) if mode == "NT" else \
                   ((K, tn), lambda g0, g1: (0, ij(g0, g1)[1]))
    b_spec = pl.BlockSpec(b_blk, b_map, pipeline_mode=pl.Buffered(1)) if single_b else pl.BlockSpec(b_blk, b_map)
    dims = {"NN": NN, "NT": NT, "TN": TN}[mode]

    def kern(a_ref, b_ref, o_ref):
        o_ref[...] = _dot(a_ref[...], b_ref[...], dims).astype(o_ref.dtype)

    ci, ca, co, cs, cscr = _carry_args(carry)
    res = pl.pallas_call(
        _carried(kern, carry, 2, 1, *_grid_ends(grid)), name=name, grid=grid, in_specs=[a_spec, b_spec] + ci,
        out_specs=[pl.BlockSpec((tm, tn), lambda g0, g1: ij(g0, g1))] + co,
        out_shape=[_sds((M, N), out_dtype)] + cs, scratch_shapes=cscr,
        compiler_params=_params(("arbitrary", "arbitrary")),
    )(a, b, *ca)
    return res[0] if carry is None else (res[0], res[1:])


def _mm_act(name, a, w, mode, out_dtype=BF16, carry=None):
    rows, K = a.shape
    N = w.shape[1] if mode == "NN" else w.shape[0]
    if K > D_FF:
        return _mm_call(name, a, w, mode, out_dtype, rows // 8, N, single_b=True, carry=carry)
    tn = N if N <= 1024 else 1408
    return _mm_call(name, a, w, mode, out_dtype, rows // 4, tn, carry=carry)


def _mm_wgrad(name, x, dy, out_dtype=BF16, carry=None):
    M = x.shape[1]
    tm = 1408 if M == D_FF else 512
    return _mm_call(name, x, dy, "TN", out_dtype, tm, dy.shape[1], single_b=True, carry=carry)


def _sigmoid(x):
    return 0.5 * jnp.tanh(0.5 * x) + 0.5


def _silu(x):
    return x * _sigmoid(x)


def _silu_grad(x):
    s = _sigmoid(x)
    return s * (1.0 + x * (1.0 - s))


_GELU_K = math.sqrt(2.0 / math.pi)


def _gelu(x):
    return 0.5 * x * (1.0 + jnp.tanh(_GELU_K * (x + 0.044715 * x * x * x)))


def _gelu_grad(x):
    t = jnp.tanh(_GELU_K * (x + 0.044715 * x * x * x))
    return 0.5 * (1.0 + t) + 0.5 * x * (1.0 - t * t) * _GELU_K * (1.0 + 3.0 * 0.044715 * x * x)


def _log_sigmoid(x):
    return jnp.minimum(x, 0.0) - jnp.log(1.0 + jnp.exp(-jnp.abs(x)))


def _rms(x):
    x = x.astype(F32)
    r = lax.rsqrt(jnp.mean(x * x, axis=-1, keepdims=True) + EPS)
    return x * r, r


def _rms_bwd(dy, y, r):
    return r * (dy - y * jnp.mean(dy * y, axis=-1, keepdims=True))


def _modrow(mod_ref, i, chunk):
    lo = mod_ref[0:1, chunk * D:(chunk + 1) * D]
    hi = mod_ref[1:2, chunk * D:(chunk + 1) * D]
    return jnp.where(i == 0, lo, hi)


def _acc_seg(acc_ref, i, val):
    zero = jnp.zeros_like(val)
    acc_ref[0:1, :] += jnp.where(i == 0, val, zero)
    acc_ref[1:2, :] += jnp.where(i == 0, zero, val)


def _colsum(x):
    return jnp.sum(x, axis=0, keepdims=True)


SH1, SC1, GA1, SH2, SC2, GA2 = range(6)


def _normmod_fwd(name, ctx, x, g, mod, c_sh, c_sc, carry=None):
    T = ctx.shape[0] + x.shape[0]
    assert ctx.shape[0] == TR
    n = T // TR

    def kern(ctx_ref, x_ref, g_ref, mod_ref, xa_ref, h_ref):
        i = pl.program_id(0)
        v = jnp.where(i == 0, ctx_ref[...], x_ref[...])
        xa_ref[...] = v
        y, _ = _rms(v)
        h = (y * g_ref[...]) * (1.0 + _modrow(mod_ref, i, c_sc)) + _modrow(mod_ref, i, c_sh)
        h_ref[...] = h.astype(BF16)

    row = pl.BlockSpec((TR, D), lambda i: (i, 0))
    ci, ca, co, cs, cscr = _carry_args(carry)
    res = pl.pallas_call(
        _carried(kern, carry, 4, 2, *_grid_ends((n,))), name=name, grid=(n,),
        in_specs=[pl.BlockSpec((TR, D), lambda i: (0, 0)), pl.BlockSpec((TR, D), lambda i: (jnp.maximum(i - 1, 0), 0)),
                  _full_spec(g.shape), _full_spec(mod.shape)] + ci,
        out_specs=[row, row] + co, out_shape=[_sds((T, D), F32), _sds((T, D), BF16)] + cs, scratch_shapes=cscr,
        compiler_params=_params(),
    )(ctx, x, g, mod, *ca)
    return res if carry is None else (res[:2], res[2:])


def _modrows(mod_ref, row0, n, chunk):
    t = row0 + lax.broadcasted_iota(jnp.int32, (n, 1), 0)
    return jnp.where(t < CTX, mod_ref[0:1, chunk * D:(chunk + 1) * D], mod_ref[1:2, chunk * D:(chunk + 1) * D])


def _loss_resid_bwd(name, x_out, target, mat, gpost, mod, c_ga):
    T = x_out.shape[0]

    def body(i, ins, ps, outs, acc):
        err = ins[0][...] - ins[1][...]
        lat = i > 0
        dx = jnp.where(lat, err * (1.0 / D), 0.0)
        outs[0][...] = dx
        acc[2][...] += jnp.where(lat, _colsum(err * err), 0.0)
        outs[1][...] = _resid_bwd_vals(i, dx, ins[2][...], ps[0][...], ps[1], c_ga, acc[0], acc[1]).astype(BF16)

    tgt_blk = ((TR, D), lambda i: (jnp.maximum(i - 1, 0), 0))
    return _ew(name, body, T // TR, [(x_out, *_rowblk(D)), (target, *tgt_blk), (mat, *_rowblk(D))], [gpost, mod],
               [(_sds((T, D), F32), *_rowblk(D)), (_sds((T, D), BF16), *_rowblk(D))],
               [_sds((2, D), F32), _sds((1, D), F32), _sds((1, D), F32)])


def _mod_for(mod_ref, i, chunk, row0, n):
    return _modrow(mod_ref, i, chunk) if row0 is None else _modrows(mod_ref, row0, n, chunk)


def _acc_for(acc_ref, i, v, row0):
    if row0 is None:
        _acc_seg(acc_ref, i, _colsum(v))
        return

    @pl.when(row0 < CTX)
    def _():
        is_ctx = row0 + lax.broadcasted_iota(jnp.int32, (v.shape[0], 1), 0) < CTX
        acc_ref[0:1, :] += _colsum(jnp.where(is_ctx, v, 0.0))
        acc_ref[1:2, :] += _colsum(jnp.where(is_ctx, 0.0, v))

    @pl.when(row0 >= CTX)
    def _():
        acc_ref[1:2, :] += _colsum(v)


def _resid_bwd_vals(i, dout, mat, gpost, mod_ref, c_ga, acc_ga, acc_g, row0=None):
    ym, rm = _rms(mat)
    ga = _mod_for(mod_ref, i, c_ga, row0, dout.shape[0])
    _acc_for(acc_ga, i, dout * (ym * gpost), row0)
    dn = dout * ga
    acc_g[...] += _colsum(dn * ym)
    return _rms_bwd(dn * gpost, ym, rm)


def _normmod_bwd_vals(i, dh, xin, g, mod_ref, c_sh, c_sc, acc_sh, acc_sc, acc_g, row0=None):
    dh = dh.astype(F32)
    y, r = _rms(xin)
    _acc_for(acc_sc, i, dh * (y * g), row0)
    _acc_for(acc_sh, i, dh, row0)
    dyg = dh * (1.0 + _mod_for(mod_ref, i, c_sc, row0, dh.shape[0]))
    acc_g[...] += _colsum(dyg * y)
    return _rms_bwd(dyg * g, y, r)


def _parts(i, tm):
    return [(slice(0, tm), i * tm)]


FT = 1408


def _ffn_in_fused(name, h2, w_t, carry=None):
    T = h2.shape[0]
    tm, nj = T // 4, D_FF // FT

    def kern(a_ref, bg_ref, bu_ref, fg_ref, fu_ref, s_ref):
        for rows, _ in _parts(0, tm):
            a = a_ref[rows, :]
            g = _dot(a, bg_ref[...], NT)
            u = _dot(a, bu_ref[...], NT)
            fg_ref[rows, :] = g.astype(BF16)
            fu_ref[rows, :] = u.astype(BF16)
            s_ref[rows, :] = (_silu(g) * u).astype(BF16)

    o_spec = pl.BlockSpec((tm, FT), lambda i, j: (i, j))
    ci, ca, co, cs, cscr = _carry_args(carry)
    res = pl.pallas_call(
        _carried(kern, carry, 3, 3, *_grid_ends((4, nj))), name=name, grid=(4, nj),
        in_specs=[pl.BlockSpec((tm, D), lambda i, j: (i, 0)), pl.BlockSpec((FT, D), lambda i, j: (j, 0)),
                  pl.BlockSpec((FT, D), lambda i, j: (j + nj, 0))] + ci,
        out_specs=[o_spec] * 3 + co, out_shape=[_sds((T, D_FF), BF16)] * 3 + cs, scratch_shapes=cscr,
        compiler_params=_params(("arbitrary", "arbitrary")),
    )(h2, w_t, w_t, *ca)
    return res if carry is None else (res[:3], res[3:])


def _norm_chain(row0, xin, mat, gpost, mod_ref, c_ga, gnext, modn_ref, c_sh, c_sc):
    n = xin.shape[0]
    ym, _ = _rms(mat.astype(BF16))
    xo = xin + _modrows(mod_ref, row0, n, c_ga) * (ym * gpost)
    y, _ = _rms(xo)
    h = (y * gnext) * (1.0 + _modrows(modn_ref, row0, n, c_sc)) + _modrows(modn_ref, row0, n, c_sh)
    return xo, h.astype(BF16)


def _out_fused(name, p, u, o_all, xa, w_o_rnn, w_o_attn, w_out, gpost, mod, gnext):
    T = u.shape[0]
    tm = T // 8

    def kern(g0, g1, g2, g3, u_ref, o_ref, xa_ref, wr_ref, wa_ref, w_ref, gpost_ref, mod_ref, gnext_ref,
             ya_ref, yb_ref, z_ref, m_ref, x1_ref, h2_ref):
        for rows, row0 in _parts(pl.program_id(0), tm):
            ya = _dot(u_ref[rows, :], wr_ref[...]).astype(BF16)
            yb = _dot(o_ref[rows, :], wa_ref[...]).astype(BF16)
            ya_ref[rows, :] = ya
            yb_ref[rows, :] = yb
            ga = _sigmoid(jnp.concatenate([g0[rows, :], g1[rows, :]], axis=1).astype(F32))
            gb = _sigmoid(jnp.concatenate([g2[rows, :], g3[rows, :]], axis=1).astype(F32))
            z = (ga * ya.astype(F32) + gb * yb.astype(F32)).astype(BF16)
            z_ref[rows, :] = z
            m = _dot(z, w_ref[...])
            m_ref[rows, :] = m.astype(BF16)
            x1_ref[rows, :], h2_ref[rows, :] = _norm_chain(row0, xa_ref[rows, :], m, gpost_ref[...], mod_ref, GA1,
                                                           gnext_ref[...], mod_ref, SH2, SC2)

    row = lambda w: pl.BlockSpec((tm, w), lambda i: (i, 0))
    return pl.pallas_call(
        kern, name=name, grid=(T // tm,),
        in_specs=[pl.BlockSpec((tm, GLB), lambda i, q=q: (i, COL_GL // GLB + q)) for q in range(4)]
                 + [row(D), row(D), row(D)] + [_full_spec(a.shape) for a in (w_o_rnn, w_o_attn, w_out, gpost, mod, gnext)],
        out_specs=[row(D)] * 6,
        out_shape=[_sds((T, D), BF16)] * 4 + [_sds((T, D), F32), _sds((T, D), BF16)],
        compiler_params=_params(),
    )(p, p, p, p, u, o_all, xa, w_o_rnn, w_o_attn, w_out, gpost, mod, gnext)


def _ffn_out_fused(name, s, w, x1, gpost, mod, nxt=None):
    T = s.shape[0]
    tm = T // 8

    def kern(s_ref, w_ref, x1_ref, gpost_ref, mod_ref, *rest):
        for rows, row0 in _parts(pl.program_id(0), tm):
            e = _dot(s_ref[rows, :], w_ref[...])
            if nxt is None:
                e_ref, xo_ref = rest
                ym, _ = _rms(e.astype(BF16))
                xo_ref[rows, :] = x1_ref[rows, :] + _modrows(mod_ref, row0, e.shape[0], GA2) * (ym * gpost_ref[...])
            else:
                gnext_ref, modn_ref, e_ref, xo_ref, h_ref = rest
                xo_ref[rows, :], h_ref[rows, :] = _norm_chain(row0, x1_ref[rows, :], e, gpost_ref[...], mod_ref, GA2,
                                                              gnext_ref[...], modn_ref, SH1, SC1)
            e_ref[rows, :] = e.astype(BF16)

    row = lambda w_: pl.BlockSpec((tm, w_), lambda i: (i, 0))
    extra = [] if nxt is None else list(nxt)
    return pl.pallas_call(
        kern, name=name, grid=(T // tm,),
        in_specs=[row(D_FF), _full_spec(w.shape), row(D), _full_spec(gpost.shape), _full_spec(mod.shape)]
                 + [_full_spec(a.shape) for a in extra],
        out_specs=[row(D)] * (2 if nxt is None else 3),
        out_shape=[_sds((T, D), BF16), _sds((T, D), F32)] + ([] if nxt is None else [_sds((T, D), BF16)]),
        compiler_params=_params(),
    )(s, w, x1, gpost, mod, *extra)


def _ffn_bwd_fused(name, fg, fu, w, de=None, head=None):
    T = fg.shape[0]
    tm = T // 8
    row = lambda w_: pl.BlockSpec((tm, w_), lambda i: (i, 0))
    w_spec = pl.BlockSpec(w.shape, lambda i: (0, 0), pipeline_mode=pl.Buffered(1))

    def tail(rows, de_v, fg_ref, fu_ref, w_ref, df_ref):
        ds = _dot(de_v, w_ref[...], NT)
        g, u = fg_ref[rows, :].astype(F32), fu_ref[rows, :].astype(F32)
        df_ref[rows, :] = jnp.concatenate([ds * u * _silu_grad(g), ds * _silu(g)], axis=1).astype(BF16)

    if head is None:
        def kern(de_ref, fg_ref, fu_ref, w_ref, df_ref):
            for rows, _ in _parts(pl.program_id(0), tm):
                tail(rows, de_ref[rows, :], fg_ref, fu_ref, w_ref, df_ref)

        return pl.pallas_call(
            kern, name=name, grid=(T // tm,), in_specs=[row(D), row(D_FF), row(D_FF), w_spec],
            out_specs=[row(2 * D_FF)], out_shape=[_sds((T, 2 * D_FF), BF16)], compiler_params=_params(),
        )(de, fg, fu, w)

    dx2, e, gpost, mod = head

    def kern(dx_ref, e_ref, fg_ref, fu_ref, w_ref, gpost_ref, mod_ref, de_ref, df_ref, dga_ref, dg_ref):
        i = pl.program_id(0)

        @pl.when(i == 0)
        def _():
            dga_ref[...] = jnp.zeros(dga_ref.shape, F32)
            dg_ref[...] = jnp.zeros(dg_ref.shape, F32)

        for rows, row0 in _parts(i, tm):
            de_v = _resid_bwd_vals(i, dx_ref[rows, :], e_ref[rows, :], gpost_ref[...], mod_ref, GA2, dga_ref, dg_ref,
                                   row0=row0).astype(BF16)
            de_ref[rows, :] = de_v
            tail(rows, de_v, fg_ref, fu_ref, w_ref, df_ref)

    return pl.pallas_call(
        kern, name=name, grid=(T // tm,),
        in_specs=[row(D), row(D), row(D_FF), row(D_FF), w_spec, _full_spec(gpost.shape), _full_spec(mod.shape)],
        out_specs=[row(D), row(2 * D_FF), _full_spec((2, D)), _full_spec((1, D))],
        out_shape=[_sds((T, D), BF16), _sds((T, 2 * D_FF), BF16), _sds((2, D), F32), _sds((1, D), F32)],
        compiler_params=_params(),
    )(dx2, e, fg, fu, w, gpost, mod)


def _zero_at_start(i, refs):
    @pl.when(i == 0)
    def _():
        for r in refs:
            r[...] = jnp.zeros(r.shape, F32)


def _proj_bwd_fused(name, dp, dgl, w_in_t, xa, dx1, gpre, mod, carry=None):
    T = dp.shape[0]
    tm = T // 8
    row = lambda w_: pl.BlockSpec((tm, w_), lambda i: (i, 0))

    def kern(dp_ref, dgl_ref, w_ref, xa_ref, dx1_ref, g_ref, mod_ref, dxa_ref, dsh_ref, dsc_ref, dg_ref):
        i = pl.program_id(0)
        _zero_at_start(i, (dsh_ref, dsc_ref, dg_ref))
        for rows, row0 in _parts(i, tm):
            dh = _dot(dp_ref[rows, :], w_ref[0:DP_W, :]) + _dot(dgl_ref[rows, :], w_ref[DP_W:, :])
            dxa_ref[rows, :] = dx1_ref[rows, :] + _normmod_bwd_vals(i, dh, xa_ref[rows, :], g_ref[...], mod_ref, SH1,
                                                                    SC1, dsh_ref, dsc_ref, dg_ref, row0=row0)

    ci, ca, co, cs, cscr = _carry_args(carry)
    res = pl.pallas_call(
        _carried(kern, carry, 7, 4, *_grid_ends((T // tm,))), name=name, grid=(T // tm,),
        in_specs=[row(DP_W), row(P_W - DP_W),
                  pl.BlockSpec(w_in_t.shape, lambda i: (0, 0), pipeline_mode=pl.Buffered(1)), row(D), row(D),
                  _full_spec(gpre.shape), _full_spec(mod.shape)] + ci,
        out_specs=[row(D), _full_spec((2, D)), _full_spec((2, D)), _full_spec((1, D))] + co,
        out_shape=[_sds((T, D), F32), _sds((2, D), F32), _sds((2, D), F32), _sds((1, D), F32)] + cs,
        scratch_shapes=cscr, compiler_params=_params(),
    )(dp, dgl, w_in_t, xa, dx1, gpre, mod, *ca)
    return res if carry is None else (res[:4], res[4:])


def _proj_wgrad(name, dp, dgl, h, carry=None):
    T, N = h.shape
    n1, n2 = DP_W // GLB, (P_W - DP_W) // GLB

    def kern(a1_ref, a2_ref, h_ref, o_ref):
        i = pl.program_id(0)

        @pl.when(i < n1)
        def _():
            o_ref[...] = _dot(a1_ref[...], h_ref[...], TN).astype(o_ref.dtype)

        @pl.when(i >= n1)
        def _():
            o_ref[...] = _dot(a2_ref[...], h_ref[...], TN).astype(o_ref.dtype)

    ci, ca, co, cs, cscr = _carry_args(carry)
    res = pl.pallas_call(
        _carried(kern, carry, 3, 1, *_grid_ends((n1 + n2,))), name=name, grid=(n1 + n2,),
        in_specs=[pl.BlockSpec((T, GLB), lambda i: (0, jnp.minimum(i, n1 - 1))),
                  pl.BlockSpec((T, GLB), lambda i: (0, jnp.maximum(i - n1, 0))),
                  pl.BlockSpec((T, N), lambda i: (0, 0), pipeline_mode=pl.Buffered(1))] + ci,
        out_specs=[pl.BlockSpec((GLB, N), lambda i: (i, 0))] + co,
        out_shape=[_sds((P_W, N), BF16)] + cs, scratch_shapes=cscr, compiler_params=_params(),
    )(dp, dgl, h, *ca)
    return res[0] if carry is None else (res[0], res[1:])


def _ffn_in_bwd_fused(name, df, w_t, x1, dres, mat, gpre, mod, gpost, carry=None):
    T = df.shape[0]
    tm = T // 8
    row = lambda w_: pl.BlockSpec((tm, w_), lambda i: (i, 0))

    def kern(df_ref, w_ref, x1_ref, dres_ref, mat_ref, gpre_ref, mod_ref, gpost_ref,
             dx1_ref, dm_ref, dsh_ref, dsc_ref, dgpre_ref, dga_ref, dgpost_ref):
        i = pl.program_id(0)
        _zero_at_start(i, (dsh_ref, dsc_ref, dgpre_ref, dga_ref, dgpost_ref))
        for rows, row0 in _parts(i, tm):
            dh2 = _dot(df_ref[rows, :], w_ref[...])
            dx1 = dres_ref[rows, :] + _normmod_bwd_vals(i, dh2, x1_ref[rows, :], gpre_ref[...], mod_ref, SH2, SC2,
                                                        dsh_ref, dsc_ref, dgpre_ref, row0=row0)
            dx1_ref[rows, :] = dx1
            dm_ref[rows, :] = _resid_bwd_vals(i, dx1, mat_ref[rows, :], gpost_ref[...], mod_ref, GA1, dga_ref,
                                              dgpost_ref, row0=row0).astype(BF16)

    ci, ca, co, cs, cscr = _carry_args(carry)
    res = pl.pallas_call(
        _carried(kern, carry, 8, 7, *_grid_ends((T // tm,))), name=name, grid=(T // tm,),
        in_specs=[row(2 * D_FF), pl.BlockSpec(w_t.shape, lambda i: (0, 0), pipeline_mode=pl.Buffered(1)), row(D),
                  row(D), row(D), _full_spec(gpre.shape), _full_spec(mod.shape), _full_spec(gpost.shape)] + ci,
        out_specs=[row(D), row(D), _full_spec((2, D)), _full_spec((2, D)), _full_spec((1, D)), _full_spec((2, D)),
                   _full_spec((1, D))] + co,
        out_shape=[_sds((T, D), F32), _sds((T, D), BF16), _sds((2, D), F32), _sds((2, D), F32), _sds((1, D), F32),
                   _sds((2, D), F32), _sds((1, D), F32)] + cs,
        scratch_shapes=cscr, compiler_params=_params(),
    )(df, w_t, x1, dres, mat, gpre, mod, gpost, *ca)
    return res if carry is None else (res[:7], res[7:])


def _out_bwd_fused(name, dm, w_out, w_o_rnn, w_o_attn, p, ya, yb):
    T = dm.shape[0]
    tm = T // 8
    row = lambda w_: pl.BlockSpec((tm, w_), lambda i: (i, 0))

    def kern(dm_ref, w_ref, wr_ref, wa_ref, g0, g1, g2, g3, ya_ref, yb_ref, dya_ref, dyb_ref, dgl_ref, du_ref, do_ref):
        for rows, _ in _parts(pl.program_id(0), tm):
            dz = _dot(dm_ref[rows, :], w_ref[...], NT)
            ga = _sigmoid(jnp.concatenate([g0[rows, :], g1[rows, :]], axis=1).astype(F32))
            gb = _sigmoid(jnp.concatenate([g2[rows, :], g3[rows, :]], axis=1).astype(F32))
            dya = (dz * ga).astype(BF16)
            dyb = (dz * gb).astype(BF16)
            dya_ref[rows, :] = dya
            dyb_ref[rows, :] = dyb
            dgl_ref[rows, :] = jnp.concatenate([dz * ya_ref[rows, :].astype(F32) * ga * (1.0 - ga),
                                                dz * yb_ref[rows, :].astype(F32) * gb * (1.0 - gb)],
                                               axis=1).astype(BF16)
            du_ref[rows, :] = _dot(dya, wr_ref[...], NT).astype(BF16)
            do_ref[rows, :] = _dot(dyb, wa_ref[...], NT).astype(BF16)

    return pl.pallas_call(
        kern, name=name, grid=(T // tm,),
        in_specs=[row(D)] + [_full_spec(w.shape) for w in (w_out, w_o_rnn, w_o_attn)]
                 + [pl.BlockSpec((tm, GLB), lambda i, q=q: (i, COL_GL // GLB + q)) for q in range(4)] + [row(D), row(D)],
        out_specs=[row(D), row(D), row(2 * D), row(D), row(D)],
        out_shape=[_sds((T, D), BF16), _sds((T, D), BF16), _sds((T, 2 * D), BF16), _sds((T, D), BF16),
                   _sds((T, D), BF16)],
        compiler_params=_params(),
    )(dm, w_out, w_o_rnn, w_o_attn, p, p, p, p, ya, yb)


AB = 128
CTX_BLKS = CTX // AB


def _rope_tables(S):
    pos = jnp.arange(S, dtype=jnp.int32)
    inv = ROPE_BASE ** (-jnp.arange(N_FREQ, dtype=F32) / N_FREQ)
    ang_r = (pos // GRID_W).astype(F32)[:, None] * inv[None, :]
    ang_c = (pos % GRID_W).astype(F32)[:, None] * inv[None, :]
    cos = jnp.concatenate([jnp.cos(ang_r)] * 2 + [jnp.cos(ang_c)] * 2, axis=1)
    sin = jnp.concatenate([-jnp.sin(ang_r), jnp.sin(ang_r), -jnp.sin(ang_c), jnp.sin(ang_c)], axis=1)
    return cos, sin


def _rope(x, cos, sin):
    w = x.shape[1]
    reps = w // HEAD
    lane = lax.broadcasted_iota(jnp.int32, x.shape, 1)
    partner = jnp.where((lane & 63) < 32, pltpu.roll(x, w - 32, 1), pltpu.roll(x, 32, 1))
    return x * jnp.tile(cos, (1, reps)) + partner * jnp.tile(sin, (1, reps))


def _unrope(dx, cos, sin):
    w = dx.shape[1]
    reps = w // HEAD
    lane = lax.broadcasted_iota(jnp.int32, dx.shape, 1)
    t = dx * jnp.tile(sin, (1, reps))
    partner = jnp.where((lane & 63) < 32, pltpu.roll(t, w - 32, 1), pltpu.roll(t, 32, 1))
    return dx * jnp.tile(cos, (1, reps)) + partner


def _qkv_prep(name, p, cos, sin, S):
    T = CTX + S
    nt = T // TR
    cb = CTX // TR
    KW = N_KV * HEAD

    def with_ones(v):
        ones = jnp.ones((TR, HEAD), BF16)
        return jnp.concatenate([v[:, kh * HEAD:(kh + 1) * HEAD] if part == 0 else ones
                                for kh in range(N_KV) for part in range(2)], axis=1)

    def kern(q_ref, k_ref, v_ref, cos_ref, sin_ref, qa_ref, kp_ref, vp_ref, kc_ref, vc_ref):
        i = pl.program_id(0)
        cos_v, sin_v = cos_ref[...], sin_ref[...]
        @pl.when(i < cb)
        def _():
            qa_ref[...] = (q_ref[...].astype(F32) * ATT_SCALE).astype(BF16)
            kc_ref[...] = k_ref[...]
            vc_ref[...] = with_ones(v_ref[...])

        @pl.when((i < cb) | (i >= nt))
        def _():
            kp_ref[...] = jnp.zeros(kp_ref.shape, BF16)
            vp_ref[...] = jnp.zeros(vp_ref.shape, BF16)

        @pl.when((i >= cb) & (i < nt))
        def _():
            qa_ref[...] = (_rope(q_ref[...].astype(F32), cos_v, sin_v) * ATT_SCALE).astype(BF16)
            kp_ref[...] = _rope(k_ref[...].astype(F32), cos_v, sin_v).astype(BF16)
            vp_ref[...] = with_ones(v_ref[...])

    tok = lambda i: jnp.minimum(i, nt - 1)
    lat_map = lambda i: (jnp.clip(i - cb, 0, nt - cb - 1), 0)
    ctx_map = lambda i: (jnp.minimum(i, cb - 1), 0)
    return pl.pallas_call(
        kern, name=name, grid=(nt + cb,),
        in_specs=[pl.BlockSpec((TR, N_Q * HEAD), lambda i: (tok(i), COL_Q // (N_Q * HEAD))),
                  pl.BlockSpec((TR, KW), lambda i: (tok(i), COL_K // KW)),
                  pl.BlockSpec((TR, KW), lambda i: (tok(i), COL_V // KW)),
                  pl.BlockSpec((TR, HEAD), lat_map), pl.BlockSpec((TR, HEAD), lat_map)],
        out_specs=[pl.BlockSpec((TR, N_Q * HEAD), lambda i: (tok(i), 0)),
                   pl.BlockSpec((TR, KW), lambda i: (i, 0)), pl.BlockSpec((TR, 2 * KW), lambda i: (i, 0)),
                   pl.BlockSpec((TR, KW), ctx_map), pl.BlockSpec((TR, 2 * KW), ctx_map)],
        out_shape=[_sds((T, N_Q * HEAD), BF16), _sds((S + 2 * CTX, KW), BF16), _sds((S + 2 * CTX, 2 * KW), BF16),
                   _sds((CTX, KW), BF16), _sds((CTX, 2 * KW), BF16)],
        compiler_params=_params(),
    )(p, p, p, cos, sin)


GW = Q_PER_KV * HEAD
HG = Q_PER_KV


def _band_bias(S):
    r = jnp.arange(AB, dtype=jnp.int32)[:, None]
    c = jnp.arange(3 * AB, dtype=jnp.int32)[None, :]
    near = jnp.abs(c - AB - r) <= AB
    valid = jnp.stack([near & (c >= AB), near, near & (c < 2 * AB)])
    return jnp.where(valid, 0.0, NEG_INF).astype(F32)


def _bias_spec(S):
    nb = S // AB
    return pl.BlockSpec((None, AB, 3 * AB), lambda kh, n: (jnp.where(n == 0, 0, jnp.where(n == nb - 1, 2, 1)), 0, 0))


def _head_probs(q, sink, kc, vce, kb, vbe, bias):
    s_c = _dot(q, kc, NT)
    m = jnp.maximum(jnp.max(s_c, axis=-1, keepdims=True), sink)
    if kb is not None:
        s_b = _dot(q, kb, NT) + bias
        m = jnp.maximum(m, jnp.max(s_b, axis=-1, keepdims=True))
    p_c = jnp.exp(s_c - m).astype(BF16)
    acc = _dot(p_c, vce)
    p_b = None
    if kb is not None:
        p_b = jnp.exp(s_b - m).astype(BF16)
        acc = acc + _dot(p_b, vbe)
    return p_c, p_b, m, acc


def _attn_fwd(name, qa, kc, vc, sink4, S, band=None, prev=None, carry=None):
    T = qa.shape[0]
    has_band = band is not None
    nq = S // AB if has_band else CTX_BLKS
    q_off = CTX_BLKS if has_band else 0

    def kern(*refs):
        q_ref, kc_ref, vc_ref, sink_ref = refs[:4]
        rest = refs[4:]
        o_ref = rest[-1]
        n = pl.program_id(1)
        kc_v, vce = kc_ref[...], vc_ref[...]
        kb = vbe = bias = None
        if has_band:
            kp_ref, vp_ref, bias_ref = rest[:3]
            start = pl.multiple_of(n * AB + (CTX - AB), AB)
            kb = kp_ref[pl.ds(start, 3 * AB), :]
            vbe = vp_ref[pl.ds(start, 3 * AB), :]
            bias = bias_ref[...]
        outs = []
        for g in range(Q_PER_KV):
            sink = sink_ref[g:g + 1, 0:1]
            _, _, m, acc = _head_probs(q_ref[:, g * HEAD:(g + 1) * HEAD], sink, kc_v, vce, kb, vbe, bias)
            l = acc[:, HEAD:] + jnp.exp(sink - m)
            outs.append(acc[:, :HEAD] / l)
        o_ref[...] = jnp.concatenate(outs, axis=1).astype(BF16)

    in_specs = [pl.BlockSpec((AB, GW), lambda kh, n: (n + q_off, kh)),
                pl.BlockSpec((CTX, HEAD), lambda kh, n: (0, kh)), pl.BlockSpec((CTX, 2 * HEAD), lambda kh, n: (0, kh)),
                pl.BlockSpec((None, Q_PER_KV, HEAD), lambda kh, n: (kh, 0, 0))]
    args = [qa, kc, vc, sink4]
    if has_band:
        in_specs += [pl.BlockSpec((S + 2 * CTX, HEAD), lambda kh, n: (0, kh)),
                     pl.BlockSpec((S + 2 * CTX, 2 * HEAD), lambda kh, n: (0, kh)), _bias_spec(S)]
        args += list(band)
    alias = {}
    if prev is not None:
        in_specs.append(ANY)
        alias = {len(args): 0}
        args.append(prev)
    ci, ca, co, cs, cscr = _carry_args(carry)
    res = pl.pallas_call(
        _carried(kern, carry, len(args), 1, *_grid_ends((N_KV, nq))), name=name, grid=(N_KV, nq),
        in_specs=in_specs + ci,
        out_specs=[pl.BlockSpec((AB, GW), lambda kh, n: (n + q_off, kh))] + co,
        out_shape=[_sds((T, N_Q * HEAD), BF16)] + cs, input_output_aliases=alias, scratch_shapes=cscr,
        compiler_params=_params(("arbitrary", "arbitrary")),
    )(*args, *ca)
    return res[0] if carry is None else (res[0], res[1:])


def _attn_bwd(name, qa, kc, vc, sink4, o_all, do_all, S, band=None, prev_dq=None, carry=None):
    T = qa.shape[0]
    has_band = band is not None
    nq = S // AB if has_band else CTX_BLKS
    q_off = CTX_BLKS if has_band else 0
    KW = N_KV * HEAD

    def kern(*refs):
        q_ref, kc_ref, vc_ref, sink_ref, o_ref, do_ref = refs[:6]
        rest = refs[6:]
        if has_band:
            kp_ref, vp_ref, bias_ref, cos_ref, sin_ref = rest[:5]
            rest = rest[5:]
        if prev_dq is not None:
            rest = rest[1:]
        dq_ref, dkc_ref, dvc_ref, dsink_ref = rest[:4]
        n = pl.program_id(1)

        @pl.when(n == 0)
        def _():
            dkc_ref[...] = jnp.zeros(dkc_ref.shape, F32)
            dvc_ref[...] = jnp.zeros(dvc_ref.shape, F32)
            dsink_ref[...] = jnp.zeros(dsink_ref.shape, F32)
            if has_band:
                rest[4][...] = jnp.zeros(rest[4].shape, F32)
                rest[5][...] = jnp.zeros(rest[5].shape, F32)

        kc_v, vce = kc_ref[...], vc_ref[...]
        vc_v = vce[:, :HEAD]
        kb = vbe = vb = bias = None
        if has_band:
            start = pl.multiple_of(n * AB + (CTX - AB), AB)
            kb = kp_ref[pl.ds(start, 3 * AB), :]
            vbe = vp_ref[pl.ds(start, 3 * AB), :]
            vb = vbe[:, :HEAD]
            bias = bias_ref[...]
        dq_parts, dsink_parts = [], []
        for g0 in range(0, Q_PER_KV, HG):
            heads = range(g0, g0 + HG)
            stack = lambda ref: jnp.concatenate([ref[:, g * HEAD:(g + 1) * HEAD] for g in heads], axis=0)
            q4, do4 = stack(q_ref), stack(do_ref)
            sink = jnp.concatenate([jnp.broadcast_to(sink_ref[g:g + 1, 0:1], (AB, 1)) for g in heads], axis=0)
            s_c = _dot(q4, kc_v, NT)
            m = jnp.maximum(jnp.max(s_c, axis=-1, keepdims=True), sink)
            if has_band:
                s_b = _dot(q4, kb, NT) + jnp.tile(bias, (HG, 1))
                m = jnp.maximum(m, jnp.max(s_b, axis=-1, keepdims=True))
            p_c = jnp.exp(s_c - m).astype(BF16).astype(F32)
            p_sink = jnp.exp(sink - m)
            l = jnp.sum(p_c, axis=-1, keepdims=True) + p_sink
            if has_band:
                p_b = jnp.exp(s_b - m).astype(BF16).astype(F32)
                l = l + jnp.sum(p_b, axis=-1, keepdims=True)
            inv = 1.0 / l
            delta = jnp.sum(do4.astype(F32) * stack(o_ref).astype(F32), axis=-1, keepdims=True)
            do4b = do4.astype(BF16)
            pn_c = (p_c * inv).astype(BF16)
            ds_c = (p_c * inv * (_dot(do4b, vc_v, NT) - delta)).astype(BF16)
            dq4 = _dot(ds_c, kc_v)
            dkc_ref[...] += _dot(q4, ds_c, TN)
            dvc_ref[...] += _dot(do4b, pn_c, TN)
            if has_band:
                pn_b = (p_b * inv).astype(BF16)
                ds_b = (p_b * inv * (_dot(do4b, vb, NT) - delta)).astype(BF16)
                dq4 = dq4 + _dot(ds_b, kb)
                rest[4][:, pl.ds(start, 3 * AB)] += _dot(q4, ds_b, TN)
                rest[5][:, pl.ds(start, 3 * AB)] += _dot(do4b, pn_b, TN)
            dq4 = dq4 * ATT_SCALE
            dq_parts += [dq4[k * AB:(k + 1) * AB, :] for k in range(HG)]
            ps = p_sink * inv * delta
            dsink_parts += [jnp.broadcast_to(-jnp.sum(ps[k * AB:(k + 1) * AB, :], axis=0, keepdims=True), (1, HEAD))
                            for k in range(HG)]
        dq = jnp.concatenate(dq_parts, axis=1)
        dq_ref[...] = (_unrope(dq, cos_ref[...], sin_ref[...]) if has_band else dq).astype(BF16)
        dsink_ref[...] += jnp.concatenate(dsink_parts, axis=0)

    q_spec = pl.BlockSpec((AB, GW), lambda kh, n: (n + q_off, kh))
    c_spec = pl.BlockSpec((CTX, HEAD), lambda kh, n: (0, kh))
    ce_spec = pl.BlockSpec((CTX, 2 * HEAD), lambda kh, n: (0, kh))
    s_spec = pl.BlockSpec((None, Q_PER_KV, HEAD), lambda kh, n: (kh, 0, 0))
    in_specs = [q_spec, c_spec, ce_spec, s_spec, q_spec, q_spec]
    args = [qa, kc, vc, sink4, o_all, do_all]
    ct_spec = pl.BlockSpec((HEAD, CTX), lambda kh, n: (kh, 0))
    dq_spec = pl.BlockSpec((AB, GW), lambda kh, n: (n + q_off, COL_Q // GW + kh))
    out_specs = [dq_spec, ct_spec, ct_spec, s_spec]
    out_shape = [_sds((T, DP_W), BF16), _sds((KW, CTX), F32), _sds((KW, CTX), F32), _sds((N_KV, Q_PER_KV, HEAD), F32)]
    if has_band:
        p_spec = pl.BlockSpec((S + 2 * CTX, HEAD), lambda kh, n: (0, kh))
        pt_spec = pl.BlockSpec((HEAD, S + 2 * CTX), lambda kh, n: (kh, 0))
        rope_spec = pl.BlockSpec((AB, HEAD), lambda kh, n: (n, 0))
        in_specs += [p_spec, pl.BlockSpec((S + 2 * CTX, 2 * HEAD), lambda kh, n: (0, kh)), _bias_spec(S), rope_spec,
                     rope_spec]
        args += list(band)
        out_specs += [pt_spec, pt_spec]
        out_shape += [_sds((KW, S + 2 * CTX), F32)] * 2
    alias = {}
    if prev_dq is not None:
        in_specs.append(ANY)
        alias = {len(args): 0}
        args.append(prev_dq)
    ci, ca, co, cs, cscr = _carry_args(carry)
    n_out = len(out_specs)
    res = pl.pallas_call(
        _carried(kern, carry, len(args), n_out, *_grid_ends((N_KV, nq))), name=name, grid=(N_KV, nq),
        in_specs=in_specs + ci, out_specs=out_specs + co, out_shape=out_shape + cs, scratch_shapes=cscr,
        input_output_aliases=alias, compiler_params=_params(("arbitrary", "arbitrary")),
    )(*args, *ca)
    return res if carry is None else (res[:n_out], res[n_out:])


def _dkv_assemble(name, dp, dkp, dvp, dkc_l, dvc_l, dkc_c, dvc_c, cos, sin, S):
    T = CTX + S
    KW = N_KV * HEAD

    def kern(dkp_ref, dvp_ref, dkcl_ref, dvcl_ref, dkcc_ref, dvcc_ref, cos_ref, sin_ref, dp_in, out_ref):
        i = pl.program_id(0)

        @pl.when(i == 0)
        def _():
            out_ref[...] = jnp.concatenate([(dkcl_ref[...] + dkcc_ref[...]).T, (dvcl_ref[...] + dvcc_ref[...]).T],
                                           axis=1).astype(BF16)

        @pl.when(i > 0)
        def _():
            out_ref[...] = jnp.concatenate([_unrope(dkp_ref[...].T, cos_ref[...], sin_ref[...]), dvp_ref[...].T],
                                           axis=1).astype(BF16)

    same = lambda i: (0, i)
    lat_map = lambda i: (jnp.maximum(i - 1, 0), 0)
    ctx_map = lambda i: (0, 0)
    return pl.pallas_call(
        kern, name=name, grid=(T // TR,),
        in_specs=[pl.BlockSpec((KW, TR), same), pl.BlockSpec((KW, TR), same),
                  pl.BlockSpec((KW, CTX), ctx_map), pl.BlockSpec((KW, CTX), ctx_map),
                  pl.BlockSpec((KW, CTX), ctx_map), pl.BlockSpec((KW, CTX), ctx_map),
                  pl.BlockSpec((TR, HEAD), lat_map), pl.BlockSpec((TR, HEAD), lat_map), ANY],
        out_specs=pl.BlockSpec((TR, 2 * KW), lambda i: (i, COL_K // (2 * KW))),
        out_shape=_sds((T, DP_W), BF16), input_output_aliases={8: 0}, compiler_params=_params(),
    )(dkp, dvp, dkc_l, dvc_l, dkc_c, dvc_c, cos, sin, dp)


RB = 128
CH = 256
HALO = 8
SUB = 8
GRP = 8


def _vscan(a, b, reverse):
    row = lax.broadcasted_iota(jnp.int32, a.shape, 0)
    A, H = a, b
    for s in (1, 2, 4):
        sh = SUB - s if reverse else s
        m = (row < SUB - s) if reverse else (row >= s)
        As = pltpu.roll(A, sh, 0)
        Hs = pltpu.roll(H, sh, 0)
        H = jnp.where(m, A * Hs + H, H)
        A = jnp.where(m, A * As, A)
    return A, H


def _scan_rows(a_ref, b_ref, r0, nrows, reverse, carry, emit):
    ngrp = nrows // (SUB * GRP)
    row = lax.broadcasted_iota(jnp.int32, (SUB, RB), 0)

    def grp(gi, carry):
        g = (ngrp - 1 - gi) if reverse else gi
        base = r0 + g * (SUB * GRP)
        for v in (range(GRP - 1, -1, -1) if reverse else range(GRP)):
            rs = pl.multiple_of(base + v * SUB, SUB)
            A, H = _vscan(a_ref[pl.ds(rs, SUB), :], b_ref[pl.ds(rs, SUB), :], reverse)
            hf = H + A * carry
            if reverse:
                before = jnp.where(row == SUB - 1, carry, pltpu.roll(hf, SUB - 1, 0))
                carry = hf[0:1, :]
            else:
                before = jnp.where(row == 0, carry, pltpu.roll(hf, 1, 0))
                carry = hf[SUB - 1:SUB, :]
            emit(rs, hf, before)
        return carry

    return lax.fori_loop(0, ngrp, grp, carry)


def _pad_start(ci):
    return pl.multiple_of(ci * CH + HALO * jnp.minimum(ci, 1), HALO)


def _conv_taps(ext, transpose=False):
    n = CH + 2 * HALO
    taps = []
    for k in range(CONV_W):
        off = CONV_LEFT - k if transpose else k - CONV_LEFT
        taps.append(ext[HALO:HALO + CH, :] if off == 0 else pltpu.roll(ext, (-off) % n, 0)[HALO:HALO + CH, :])
    return taps


def _lru_gates(xl, w4, b4, ls):
    pre = _dot(xl.astype(BF16), w4) + b4
    out = []
    for d in range(2):
        r = _sigmoid(pre[:, d * RB:(d + 1) * RB])
        i = _sigmoid(pre[:, (2 + d) * RB:(3 + d) * RB])
        la = LRU_C * r * ls[d:d + 1, :]
        a = jnp.exp(la)
        q = -jnp.tanh(la) * (1.0 + a * a)
        out.append((r, i, a, q))
    return out


def _rnn_specs(T):
    col = lambda n, *_: (0, n)
    return dict(
        xr=pl.BlockSpec((T, RB), lambda n, *_: (0, COL_XR // RB + n)),
        gr=pl.BlockSpec((T, RB), lambda n, *_: (0, COL_GR // RB + n)),
        act=pl.BlockSpec((T, RB), col),
        cw=pl.BlockSpec((CONV_W, RB), col), cb=pl.BlockSpec((1, RB), col),
        w4=pl.BlockSpec((None, RB, 4 * RB), lambda n, *_: (n, 0, 0)),
        b4=pl.BlockSpec((None, 1, 4 * RB), lambda n, *_: (n, 0, 0)),
        lam=pl.BlockSpec((2, RB), col))


PAD_ROWS = 3 * HALO


def _zero_pads(pad_ref, T):
    for r in (0, HALO + CTX, 2 * HALO + T):
        pad_ref[r:r + HALO, :] = jnp.zeros((HALO, RB), F32)


def _fill_padded(pad_ref, src_ref, T):
    _zero_pads(pad_ref, T)
    pad_ref[HALO:HALO + CTX, :] = src_ref[0:CTX, :].astype(F32)
    pad_ref[2 * HALO + CTX:2 * HALO + T, :] = src_ref[CTX:T, :].astype(F32)


def _pad_rows(ci):
    return pl.ds(pl.multiple_of(ci * CH + HALO + HALO * jnp.minimum(ci, 1), HALO), CH)


def _rnn_fwd(name, p, cw, cb, w4, b4, lam, T, carry=None):
    def kern(xr_ref, gr_ref, cw_ref, cb_ref, w4_ref, b4_ref, lam_ref,
             u_ref, a0, a1, yo_ref, hpf_ref, hpb_ref, r0_ref, r1_ref, i0_ref, i1_ref, xpad, b0, b1, y):
        _fill_padded(xpad, xr_ref, T)
        ls = _log_sigmoid(lam_ref[...])
        w4v, b4v, cwv, cbv = w4_ref[...], b4_ref[...], cw_ref[...], cb_ref[...]

        def chunk(ci, _):
            rows = pl.ds(pl.multiple_of(ci * CH, CH), CH)
            taps = _conv_taps(xpad[pl.ds(_pad_start(ci), CH + 2 * HALO), :])
            xl = cbv + sum(taps[k] * cwv[k:k + 1, :] for k in range(CONV_W))
            for d, (r, i, a, q) in enumerate(_lru_gates(xl, w4v, b4v, ls)):
                (a0, a1)[d][rows, :] = a
                (b0, b1)[d][rows, :] = jnp.sqrt(q) * (i * xl)
                (r0_ref, r1_ref)[d][rows, :] = r.astype(BF16)
                (i0_ref, i1_ref)[d][rows, :] = i.astype(BF16)
            return 0

        lax.fori_loop(0, T // CH, chunk, 0)
        zero = jnp.zeros((1, RB), F32)

        def emit_f(rs, hf, before):
            y[pl.ds(rs, SUB), :] = hf
            b0[pl.ds(rs, SUB), :] = before

        def emit_b(rs, hf, before):
            y[pl.ds(rs, SUB), :] += hf
            b1[pl.ds(rs, SUB), :] = before

        _scan_rows(a0, b0, 0, T, False, zero, emit_f)
        c = _scan_rows(a1, b1, 0, CTX, True, zero, emit_b)
        _scan_rows(a1, b1, CTX, T - CTX, True, c, emit_b)

        def finish(ci, _):
            rows = pl.ds(pl.multiple_of(ci * CH, CH), CH)
            yv = y[rows, :]
            u_ref[rows, :] = (yv * _gelu(gr_ref[rows, :].astype(F32))).astype(BF16)
            yo_ref[rows, :] = yv.astype(BF16)
            hpf_ref[rows, :] = b0[rows, :].astype(BF16)
            hpb_ref[rows, :] = b1[rows, :].astype(BF16)
            return 0

        lax.fori_loop(0, T // CH, finish, 0)

    sp = _rnn_specs(T)
    ci, ca, co, cs, cscr = _carry_args(carry)
    dts = [BF16, F32, F32] + [BF16] * 7
    res = pl.pallas_call(
        _carried(kern, carry, 7, 10, *_grid_ends((N_RNN_BLOCKS,))), name=name, grid=(N_RNN_BLOCKS,),
        in_specs=[sp["xr"], sp["gr"], sp["cw"], sp["cb"], sp["w4"], sp["b4"], sp["lam"]] + ci,
        out_specs=[sp["act"]] * 10 + co,
        out_shape=[_sds((T, D), dt) for dt in dts] + cs,
        scratch_shapes=[pltpu.VMEM((T + PAD_ROWS, RB), F32)] + [pltpu.VMEM((T, RB), F32)] * 3 + cscr,
        compiler_params=_params(),
    )(p, p, cw, cb, w4, b4, lam, *ca)
    return res if carry is None else (res[:10], res[10:])


def _rnn_bwd(name, p, du, saved, dp, cw, cb, w4, b4, lam, T, carry=None):
    def kern(xr_ref, gr_ref, du_ref, a0, a1, y_ref, hpf_ref, hpb_ref, r0_ref, r1_ref, i0_ref, i1_ref,
             cw_ref, cb_ref, w4_ref, b4_ref, lam_ref, dp_in,
             dp_ref, dcw_ref, dcb_ref, dw4_ref, db4_ref, dlam_ref,
             xpad, dxpad, c0, c1, dy):
        j = pl.program_id(1)

        @pl.when(j == 0)
        def _():
            scans(gr_ref, du_ref, a0, a1, y_ref, dp_ref, c0, c1, dy)

        @pl.when(j == 1)
        def _():
            gates(xr_ref, a0, a1, (hpf_ref, hpb_ref), (r0_ref, r1_ref), (i0_ref, i1_ref), cw_ref, cb_ref, w4_ref,
                  lam_ref, dp_ref, dcw_ref, dcb_ref, dw4_ref, db4_ref, dlam_ref, xpad, dxpad, c0, c1)

    def scans(gr_ref, du_ref, a0, a1, y_ref, dgr_ref, c0, c1, dy):
        def phase_a(ci, _):
            rows = pl.ds(pl.multiple_of(ci * CH, CH), CH)
            gr = gr_ref[rows, :].astype(F32)
            duv = du_ref[rows, :].astype(F32)
            dyv = duv * _gelu(gr)
            dgr_ref[rows, :] = (duv * y_ref[rows, :].astype(F32) * _gelu_grad(gr)).astype(BF16)
            dy[rows, :] = dyv
            c0[rows, :] = a0[rows, :] * dyv
            c1[rows, :] = a1[rows, :] * dyv
            return 0

        lax.fori_loop(0, T // CH, phase_a, 0)
        zero = jnp.zeros((1, RB), F32)

        def emit0(rs, hf, before):
            c0[pl.ds(rs, SUB), :] = dy[pl.ds(rs, SUB), :] + before

        def emit1(rs, hf, before):
            c1[pl.ds(rs, SUB), :] = dy[pl.ds(rs, SUB), :] + before

        _scan_rows(a0, c0, 0, T, True, zero, emit0)
        c = _scan_rows(a1, c1, CTX, T - CTX, False, zero, emit1)
        _scan_rows(a1, c1, 0, CTX, False, c, emit1)

    def gates(xr_ref, a0, a1, hp_refs, r_refs, i_refs, cw_ref, cb_ref, w4_ref, lam_ref,
              dxr_ref, dcw_ref, dcb_ref, dw4_ref, db4_ref, dlam_ref, xpad, dxpad, c0, c1):
        _fill_padded(xpad, xr_ref, T)
        _zero_pads(dxpad, T)
        lam_v = lam_ref[...]
        ls = _log_sigmoid(lam_v)
        w4v, cwv, cbv = w4_ref[...], cw_ref[...], cb_ref[...]

        def conv_chunk(ci):
            taps = _conv_taps(xpad[pl.ds(_pad_start(ci), CH + 2 * HALO), :])
            return taps, cbv + sum(taps[k] * cwv[k:k + 1, :] for k in range(CONV_W))

        dw4_ref[...] = jnp.zeros(dw4_ref.shape, F32)
        db4_ref[...] = jnp.zeros(db4_ref.shape, F32)
        dlam_ref[...] = jnp.zeros(dlam_ref.shape, F32)
        dcw_ref[...] = jnp.zeros(dcw_ref.shape, F32)
        dcb_ref[...] = jnp.zeros(dcb_ref.shape, F32)

        def phase_c(ci, _):
            base = pl.multiple_of(ci * CH, CH)
            rows = pl.ds(base, CH)
            _, xl = conv_chunk(ci)
            dxl = jnp.zeros((CH, RB), F32)
            dpre_a, dpre_x, dls = [], [], []
            for d in range(2):
                a = (a0, a1)[d][rows, :]
                r = r_refs[d][rows, :].astype(F32)
                i = i_refs[d][rows, :].astype(F32)
                q = -jnp.tanh(LRU_C * r * ls[d:d + 1, :]) * (1.0 + a * a)
                g = (c0, c1)[d][rows, :]
                hp = hp_refs[d][rows, :].astype(F32)
                gm = g * jnp.sqrt(q)
                di = gm * xl
                dxl = dxl + gm * i
                dla = a * (g * hp - a * (g * (i * xl)) * lax.rsqrt(q))
                dr = dla * (LRU_C * ls[d:d + 1, :])
                dls.append(_colsum(dla * (LRU_C * r)))
                dpre_a.append(dr * r * (1.0 - r))
                dpre_x.append(di * i * (1.0 - i))
            dpre = jnp.concatenate(dpre_a + dpre_x, axis=1)
            dpre_b = dpre.astype(BF16)
            dxl = dxl + _dot(dpre_b, w4v, NT)
            dw4_ref[...] += _dot(xl.astype(BF16), dpre_b, TN)
            db4_ref[...] += _colsum(dpre)
            dlam_ref[...] += jnp.concatenate(dls, axis=0)
            dcb_ref[...] += _colsum(dxl)
            dxpad[_pad_rows(ci), :] = dxl
            return 0

        lax.fori_loop(0, T // CH, phase_c, 0)
        dlam_ref[...] = dlam_ref[...] * _sigmoid(-lam_v)

        def phase_d(ci, _):
            base = pl.multiple_of(ci * CH, CH)
            rows = pl.ds(base, CH)
            xtaps, _ = conv_chunk(ci)
            dtaps = _conv_taps(dxpad[pl.ds(_pad_start(ci), CH + 2 * HALO), :], transpose=True)
            dxl = dxpad[_pad_rows(ci), :]
            dxr_ref[rows, :] = sum(dtaps[k] * cwv[k:k + 1, :] for k in range(CONV_W)).astype(BF16)
            dcw_ref[...] += jnp.concatenate([_colsum(dxl * xtaps[k]) for k in range(CONV_W)], axis=0)
            return 0

        lax.fori_loop(0, T // CH, phase_d, 0)

    sp = _rnn_specs(T)
    dp_spec = pl.BlockSpec((T, RB), lambda n, j: (0, COL_GR // RB + n - j * (COL_GR - COL_XR) // RB))
    ci, ca, co, cs, cscr = _carry_args(carry)
    n_in = 3 + len(saved) + 5 + 1
    res = pl.pallas_call(
        _carried(kern, carry, n_in, 6, *_grid_ends((N_RNN_BLOCKS, 2))), name=name, grid=(N_RNN_BLOCKS, 2),
        in_specs=[sp["xr"], sp["gr"]] + [sp["act"]] * (1 + len(saved)) + [sp["cw"], sp["cb"], sp["w4"], sp["b4"],
                                                                           sp["lam"], ANY] + ci,
        out_specs=[dp_spec, sp["cw"], sp["cb"], sp["w4"], sp["b4"], sp["lam"]] + co,
        out_shape=[_sds((T, DP_W), BF16), _sds((CONV_W, D), F32), _sds((1, D), F32),
                   _sds((N_RNN_BLOCKS, RB, 4 * RB), F32), _sds((N_RNN_BLOCKS, 1, 4 * RB), F32), _sds((2, D), F32)] + cs,
        scratch_shapes=[pltpu.VMEM((T + PAD_ROWS, RB), F32)] * 2 + [pltpu.VMEM((T, RB), F32)] * 3 + cscr,
        input_output_aliases={n_in - 1: 0},
        compiler_params=_params(("arbitrary", "arbitrary")),
    )(p, p, du, *saved, cw, cb, w4, b4, lam, dp, *ca)
    return res if carry is None else (res[:6], res[6:])


class _Plan:
    def __init__(self, shards, Ws):
        L = len(Ws)
        self.shards, self.Ws = shards, Ws
        self.Gs = [None] * L
        self.slots = [dict() for _ in range(L)]
        self.gate_slots = [None] * L
        self.table = {}
        for l in range(L):
            t = f"l{l}_"
            self.table[t + "rnn_fwd"] = [("gather", l, k) for k in ("wffn_in_t", "wo_rnn", "wo_attn", "wout")]
            if l + 1 < L:
                self.table[t + "attn_lat_fwd"] = [("gather", l + 1, "win_t")]
                self.table[t + "ffn_in"] = [("gather", l, "wffn_out")]
            else:
                self.table[t + "attn_lat_fwd"] = [("gather", l, "wffn_out")]
            self.table[t + "ffn_in_dx"] = [("scatter", l, "wffn_out")]
            self.table[t + "attn_lat_bwd"] = [("scatter", l, "wffn_in_t")]
            self.table[t + "proj_dx"] = [("scatter", l, "win_t_a")]
            self.table[t + "rnn_bwd"] = ([("scatter", l, k) for k in ("wout", "wo_attn", "wo_rnn")]
                                         + ([("scatter", l + 1, "win_t_b"), ("gates", l + 1, "w4")] if l + 1 < L else []))
        self.table["l0_proj_dw_b"] = [("gates", 0, "w4")]
        self.table["l0_mix_norm"] = [("gather", 0, "win_t")]

    def carry(self, name):
        jobs = []
        for kind, l, k in self.table.get(name, []):
            if kind == "gather":
                jobs.append(("gather", self.shards[l][k]))
            elif kind == "scatter":
                jobs.append(("scatter", self.Gs[l][k].reshape(N_DEV, -1, self.Gs[l][k].shape[-1])))
            else:
                jobs.append(("gather", self.Gs[l]["w4"].reshape(N_RNN_BLOCKS * RB, 4 * RB).astype(BF16)))
        return _Carry(jobs) if jobs else None

    def done(self, name, got):
        for (kind, l, k), res in zip(self.table[name], got):
            if kind == "gather":
                self.Ws[l][k] = res.reshape(-1, D)
            elif kind == "scatter":
                self.slots[l][k] = res
            else:
                self.gate_slots[l] = res


def _run(X, fn, name, *args, **kw):
    carry = None if X is None else X.carry(name)
    if carry is None:
        return fn(name, *args, **kw)
    out, got = fn(name, *args, carry=carry, **kw)
    X.done(name, got)
    return out


def _layer_fwd(l, xa, h, W, rope, S, nxt, X=None):
    T = xa.shape[0]
    tag = f"l{l}_"
    cos, sin, bias = rope
    p = _run(X, _mm_act, tag + "proj", h, W["win_t"], "NT", BF16)
    u, *rnn_saved = _run(X, _rnn_fwd, tag + "rnn_fwd", p, W["cw"], W["cb"], W["w4"], W["b4"], W["lam"], T)
    qa, kp, vp, kc, vc = _qkv_prep(tag + "qkv_prep", p, cos, sin, S)
    o_all = _attn_fwd(tag + "attn_ctx_fwd", qa, kc, vc, W["sink4"], S)
    o_all = _run(X, _attn_fwd, tag + "attn_lat_fwd", qa, kc, vc, W["sink4"], S, band=(kp, vp, bias), prev=o_all)
    ya, yb, z, m, x1, h2 = _out_fused(tag + "out", p, u, o_all, xa, W["wo_rnn"], W["wo_attn"], W["wout"],
                                      W["g_mix_post"], W["mod"], W["g_ffn_pre"])
    fg, fu, s = _run(X, _ffn_in_fused, tag + "ffn_in", h2, W["wffn_in_t"])
    e, *out = _ffn_out_fused(tag + "ffn_out", s, W["wffn_out"], x1, W["g_ffn_post"], W["mod"], nxt)
    saved = dict(xa=xa, h=h, p=p, u=u, rnn=rnn_saved, qa=qa, kp=kp, vp=vp, kc=kc, vc=vc, o_all=o_all,
                 ya=ya, yb=yb, z=z, m=m, x1=x1, h2=h2, fg=fg, fu=fu, s=s, e=e)
    return saved, out


def _layer_bwd(l, dx2, A, W, rope, S, X=None, loss_of=None):
    T = A["xa"].shape[0]
    tag = f"l{l}_"
    cos, sin, bias = rope
    G = {}
    if X is not None:
        X.Gs[l] = G
    if loss_of is None:
        de, df, dga2, G["g_ffn_post"] = _ffn_bwd_fused(tag + "ffn_bwd", A["fg"], A["fu"], W["wffn_out"],
                                                       head=(dx2, A["e"], W["g_ffn_post"], W["mod"]))
    else:
        dx2, de, dga2, G["g_ffn_post"], G["sq"] = _loss_resid_bwd(tag + "loss_ffn_resid_bwd", *loss_of, A["e"],
                                                                  W["g_ffn_post"], W["mod"], GA2)
        df, = _ffn_bwd_fused(tag + "ffn_bwd", A["fg"], A["fu"], W["wffn_out"], de=de)
    G["wffn_out"] = _mm_wgrad(tag + "ffn_out_dw", A["s"], de)
    dx1, dm, dsh2, dsc2, G["g_ffn_pre"], dga1, G["g_mix_post"] = _run(
        X, _ffn_in_bwd_fused, tag + "ffn_in_dx", df, W["wffn_in_t"], A["x1"], dx2, A["m"], W["g_ffn_pre"], W["mod"],
        W["g_mix_post"])
    G["wffn_in_t"] = _run(X, _mm_wgrad, tag + "ffn_in_dw", df, A["h2"])
    G["wout"] = _mm_wgrad(tag + "out_dw", A["z"], dm)
    dya, dyb, dgl, du, do = _out_bwd_fused(tag + "out_dx", dm, W["wout"], W["wo_rnn"], W["wo_attn"], A["p"], A["ya"],
                                           A["yb"])
    G["wo_attn"] = _mm_wgrad(tag + "o_attn_dw", A["o_all"], dyb)
    G["wo_rnn"] = _mm_wgrad(tag + "o_rnn_dw", A["u"], dya)
    dp, dkc_c, dvc_c, dsink_c = _attn_bwd(tag + "attn_ctx_bwd", A["qa"], A["kc"], A["vc"], W["sink4"], A["o_all"], do, S)
    dp, dkc_l, dvc_l, dsink_l, dkp, dvp = _run(
        X, _attn_bwd, tag + "attn_lat_bwd", A["qa"], A["kc"], A["vc"], W["sink4"], A["o_all"], do, S,
        band=(A["kp"], A["vp"], bias, cos, sin), prev_dq=dp)
    G["sink4"] = dsink_c + dsink_l
    dp = _dkv_assemble(tag + "dkv", dp, dkp, dvp, dkc_l, dvc_l, dkc_c, dvc_c, cos, sin, S)
    dp, G["cw"], G["cb"], G["w4"], G["b4"], G["lam"] = _run(
        X, _rnn_bwd, tag + "rnn_bwd", A["p"], du, A["rnn"], dp, W["cw"], W["cb"], W["w4"], W["b4"], W["lam"], T)
    proj_dx = (_proj_bwd_fused, tag + "proj_dx", dp, dgl, W["win_t"], A["xa"], dx1, W["g_mix_pre"], W["mod"])
    if X is not None:
        G["win_t_a"] = _proj_wgrad(tag + "proj_dw_a", dp, dgl, A["h"][:, :D // 2])
        dxa, dsh1, dsc1, G["g_mix_pre"] = _run(X, *proj_dx)
        G["win_t_b"] = _run(X, _proj_wgrad, tag + "proj_dw_b", dp, dgl, A["h"][:, D // 2:])
    else:
        dxa, dsh1, dsc1, G["g_mix_pre"] = _run(X, *proj_dx)
        G["win_t"] = _proj_wgrad(tag + "proj_dw", dp, dgl, A["h"])
    G["mod"] = jnp.concatenate([dsh1, dsc1, dga1, dsh2, dsc2, dga2], axis=1)
    return dxa, G


def _local_step(ctx, x, target, Ws, S, X=None):
    rope = (*_rope_tables(S), _band_bias(S))
    L = len(Ws)
    x, h = _run(X, _normmod_fwd, "l0_mix_norm", ctx, x, Ws[0]["g_mix_pre"], Ws[0]["mod"], SH1, SC1)
    saved = []
    for l in range(L):
        nxt = (Ws[l + 1]["g_mix_pre"], Ws[l + 1]["mod"]) if l + 1 < L else None
        A, out = _layer_fwd(l, x, h, Ws[l], rope, S, nxt, X)
        saved.append(A)
        if l + 1 < L:
            x, h = out
    Gs = [None] * L
    dx = None
    for l in reversed(range(L)):
        dx, Gs[l] = _layer_bwd(l, dx, saved[l], Ws[l], rope, S, X, loss_of=(out[0], target) if l == L - 1 else None)
    return Gs[L - 1]["sq"], dx, Gs


MESH = pl.DeviceIdType.MESH


def _place():
    return lax.axis_index("x"), lax.axis_index("y"), lax.axis_index("c")


def _lin(px, py, pc):
    return 4 * px + 2 * py + pc


def _allgather_small(name, blk):
    m, n = blk.shape

    def body(x_ref, out_ref, send_sems, recv_sems, local_sem):
        x, y, c = _place()
        me, sibling = (x, y, c), (x, y, 1 - c)
        chips = [(1 - x, y), (x, 1 - y), (1 - x, 1 - y)]

        def copy(k, block, to, src=None):
            dst = out_ref.at[_lin(*block)]
            return pltpu.make_async_remote_copy(src_ref=dst if src is None else src, dst_ref=dst,
                                                send_sem=send_sems.at[k], recv_sem=recv_sems.at[k],
                                                device_id=to, device_id_type=MESH)

        mine = pltpu.make_async_copy(x_ref, out_ref.at[_lin(*me)], local_sem)
        mine.start()
        first = [copy(0, me, sibling, src=x_ref)]
        first += [copy(1 + j, me, (*chip, c), src=x_ref) for j, chip in enumerate(chips)]
        for cp in first:
            cp.start()
        passed = [copy(4 + j, (*chip, c), sibling) for j, chip in enumerate(chips)]
        for j, chip in enumerate(chips):
            copy(1 + j, (*chip, c), me).wait_recv()
            passed[j].start()
        copy(0, sibling, me).wait_recv()
        for j, chip in enumerate(chips):
            copy(4 + j, (*chip, 1 - c), me).wait_recv()
        for cp in first + passed:
            cp.wait_send()
        mine.wait()

    return pl.pallas_call(
        body, name=name, out_shape=_sds((N_DEV, m, n), blk.dtype),
        in_specs=[pl.BlockSpec(memory_space=pltpu.VMEM)], out_specs=pl.BlockSpec(memory_space=pltpu.VMEM),
        scratch_shapes=[pltpu.SemaphoreType.DMA((7,)), pltpu.SemaphoreType.DMA((7,)), pltpu.SemaphoreType.DMA],
        compiler_params=pltpu.CompilerParams(vmem_limit_bytes=VMEM_LIMIT),
    )(blk)


def _exchange_shards(name, grads, L):
    nw = len(grads)
    na = nw * L
    flat = [g for per_layer in grads for g in per_layer]

    def body(*refs):
        ins, outs = refs[:na], refs[na:na + nw]
        send_sems, recv_sems, local_sems = refs[na + nw:]
        x, y, c = _place()
        me = _lin(x, y, c)
        peers = [(x ^ ((k + 1) >> 2 & 1), y ^ ((k + 1) >> 1 & 1), c ^ ((k + 1) & 1)) for k in range(7)]

        def copy(a, k, src_blk, dst_blk):
            return pltpu.make_async_remote_copy(src_ref=ins[a].at[src_blk], dst_ref=outs[a // L].at[a % L, dst_blk],
                                                send_sem=send_sems.at[a, k], recv_sem=recv_sems.at[a, k],
                                                device_id=peers[k], device_id_type=MESH)

        mine = [pltpu.make_async_copy(ins[a].at[me], outs[a // L].at[a % L, me], local_sems.at[a]) for a in range(na)]
        for cp in mine:
            cp.start()
        sent = [copy(a, k, _lin(*peers[k]), me) for a in range(na) for k in range(7)]
        for cp in sent:
            cp.start()
        for a in range(na):
            for k in range(7):
                copy(a, k, me, _lin(*peers[k])).wait_recv()
        for cp in sent:
            cp.wait_send()
        for cp in mine:
            cp.wait()

    return pl.pallas_call(
        body, name=name, out_shape=[_sds((L, *per_layer[0].shape), per_layer[0].dtype) for per_layer in grads],
        in_specs=[ANY] * na, out_specs=[ANY] * nw,
        scratch_shapes=[pltpu.SemaphoreType.DMA((na, 7)), pltpu.SemaphoreType.DMA((na, 7)),
                        pltpu.SemaphoreType.DMA((na,))],
    )(*flat)


MOD_ROWS = 16
MOD_SHARD = 6 * D // N_DEV
HI = lax.Precision.HIGHEST


def _mod_fwd(name, c9, w_mod, b_shard):
    L = w_mod.shape[0]

    def kern(c_ref, w_ref, b_ref, o_ref):
        o_ref[...] = lax.dot_general(_silu(c_ref[...]), w_ref[...], NN, precision=HI,
                                     preferred_element_type=F32) + b_ref[...]

    return pl.pallas_call(
        kern, name=name, grid=(L,),
        in_specs=[_full_spec(c9.shape), pl.BlockSpec((None, D, MOD_SHARD), lambda l: (l, 0, 0)),
                  pl.BlockSpec((None, 1, MOD_SHARD), lambda l: (l, 0, 0))],
        out_specs=pl.BlockSpec((None, MOD_ROWS, MOD_SHARD), lambda l: (l, 0, 0)),
        out_shape=_sds((L, MOD_ROWS, MOD_SHARD), F32), compiler_params=_params(),
    )(c9, w_mod, b_shard)


def _mod_bwd(name, c9, w_mod, dmod_all, dmod_cols):
    L = w_mod.shape[0]

    def rows9(ref, l):
        own = jnp.concatenate([ref[j, 2 * l + 1:2 * l + 2, :] for j in range(N_DEV)], axis=0)
        ctx = ref[0, 2 * l:2 * l + 1, :]
        for j in range(1, N_DEV):
            ctx = ctx + ref[j, 2 * l:2 * l + 1, :]
        return own, ctx

    def kern(c_ref, w_ref, all_ref, cols_ref, gw_ref, gb_ref, gc_ref):
        l = pl.program_id(0)
        for ll in range(L):
            @pl.when(l == ll)
            def _():
                own, ctx = rows9(all_ref, ll)
                gb_ref[...] = _colsum(own) + ctx
                own_s, ctx_s = rows9(cols_ref, ll)
                r16 = jnp.concatenate([own_s, ctx_s, jnp.zeros((MOD_ROWS - N_DEV - 1, MOD_SHARD), F32)], axis=0)
                gw_ref[...] = lax.dot_general(_silu(c_ref[...]), r16, TN, precision=HI, preferred_element_type=F32)
                part = lax.dot_general(r16, w_ref[...], NT, precision=HI,
                                       preferred_element_type=F32)[N_DEV:N_DEV + 1, :]
                if ll == 0:
                    gc_ref[...] = part
                else:
                    gc_ref[...] += part

    return pl.pallas_call(
        kern, name=name, grid=(L,),
        in_specs=[_full_spec(c9.shape), pl.BlockSpec((None, D, MOD_SHARD), lambda l: (l, 0, 0)),
                  _full_spec(dmod_all.shape), _full_spec(dmod_cols.shape)],
        out_specs=[pl.BlockSpec((None, D, MOD_SHARD), lambda l: (l, 0, 0)),
                   pl.BlockSpec((None, 1, 6 * D), lambda l: (l, 0, 0)), _full_spec((1, D))],
        out_shape=[_sds((L, D, MOD_SHARD), F32), _sds((L, 1, 6 * D), F32), _sds((1, D), F32)],
        compiler_params=_params(),
    )(c9, w_mod, dmod_all, dmod_cols)


_BC1 = 1.0 - ADAM_B1 ** ADAM_STEP
_BC2 = 1.0 - ADAM_B2 ** ADAM_STEP


def _adamw_vals(w, g, m, v):
    m = ADAM_B1 * m + (1.0 - ADAM_B1) * g
    v = ADAM_B2 * v + (1.0 - ADAM_B2) * (g * g)
    delta = -ADAM_LR * ((m / _BC1) / (jnp.sqrt(v / _BC2) + ADAM_EPS) + ADAM_WD * w)
    return delta, m, v


def _adamw(name, w, g, m, v, tile):
    R, C = w.shape
    blk = ((tile, C), lambda i: (i, 0))

    def body(i, ins, ps, outs, acc):
        d, mm, vv = _adamw_vals(ins[0][...], ins[1][...], ins[2][...], ins[3][...])
        outs[0][...] = d
        outs[1][...] = mm
        outs[2][...] = vv

    return _ew(name, body, R // tile, [(a, *blk) for a in (w, g, m, v)], [], [(_sds((R, C), F32), *blk)] * 3)


def _sum_slots(ref):
    g = ref[0].astype(F32)
    for j in range(1, N_DEV):
        g = g + ref[j].astype(F32)
    return g


def _adamw_slots(name, slots, shape, tile, wmv=None):
    L, R, C = shape
    n = R // tile
    spec = pl.BlockSpec((None, tile, C), lambda l, i: (l, i, 0))
    pieces = [s if isinstance(s, (list, tuple)) else [s] for s in slots]
    layer_of = [ll for ll, ps in enumerate(pieces) for _ in ps]
    flat = [p for ps in pieces for p in ps]
    wmv = list(wmv or [])

    def slot_spec(ll, cols):
        return pl.BlockSpec((N_DEV, tile, cols),
                            lambda l, i: (0, jnp.where(l == ll, i, jnp.where(l < ll, 0, n - 1)), 0))

    def kern(*refs):
        s_refs = refs[:len(flat)]
        rest = refs[len(flat):]
        l = pl.program_id(0)
        for ll in range(L):
            @pl.when(l == ll)
            def _():
                parts = [_sum_slots(r) for r, lr in zip(s_refs, layer_of) if lr == ll]
                g = parts[0] if len(parts) == 1 else jnp.concatenate(parts, axis=1)
                if wmv:
                    w_ref, m_ref, v_ref, g_ref, d_ref, mo_ref, vo_ref = rest
                    d_ref[...], mo_ref[...], vo_ref[...] = _adamw_vals(w_ref[...], g, m_ref[...], v_ref[...])
                else:
                    g_ref, = rest
                g_ref[...] = g

    n_out = 4 if wmv else 1
    return pl.pallas_call(
        kern, name=name, grid=(L, n),
        in_specs=[slot_spec(ll, p.shape[-1]) for ll, p in zip(layer_of, flat)] + [spec] * len(wmv),
        out_specs=[spec] * n_out, out_shape=[_sds((L, R, C), F32)] * n_out,
        compiler_params=_params(("arbitrary", "arbitrary")),
    )(*flat, *wmv)


def _sum_blocks(name, blocks):
    _, R, C = blocks.shape

    def kern(b_ref, o_ref):
        o_ref[...] = _sum_slots(b_ref)

    return pl.pallas_call(kern, name=name, in_specs=[_full_spec(blocks.shape)], out_specs=_full_spec((R, C)),
                          grid=(1,), out_shape=_sds((R, C), F32), compiler_params=_params())(blocks)


BIG = ("win_t", "wo_rnn", "wo_attn", "wout", "wffn_in_t", "wffn_out")
BIG_SRC = ("w_in", "w_o_rnn", "w_o_attn", "w_out", "w_ffn_in", "w_ffn_out")
BIG_T = (True, False, False, False, True, False)
BIG_TILE = (176, 128, 128, 128, 176, 176)


def _chan_full(g8):
    return jnp.transpose(g8, (1, 0, 2)).reshape(g8.shape[1], D)


def kernel(x, c, ctx, c_ctx, w_mod, b_mod, g_mix_pre, g_mix_post, g_ffn_pre, g_ffn_post, w_in, conv_w, conv_b, lru_wa, lru_ba, lru_wx, lru_bx, lru_lam, attn_sink, w_o_rnn, w_o_attn, w_out, w_ffn_in, w_ffn_out, loss_target, m_c_ctx, m_w_mod, m_b_mod, m_g_mix_pre, m_g_mix_post, m_g_ffn_pre, m_g_ffn_post, m_w_in, m_conv_w, m_conv_b, m_lru_wa, m_lru_ba, m_lru_wx, m_lru_bx, m_lru_lam, m_attn_sink, m_w_o_rnn, m_w_o_attn, m_w_out, m_w_ffn_in, m_w_ffn_out, v_c_ctx, v_w_mod, v_b_mod, v_g_mix_pre, v_g_mix_post, v_g_ffn_pre, v_g_ffn_post, v_w_in, v_conv_w, v_conv_b, v_lru_wa, v_lru_ba, v_lru_wx, v_lru_bx, v_lru_lam, v_attn_sink, v_w_o_rnn, v_w_o_attn, v_w_out, v_w_ffn_in, v_w_ffn_out):
    P = dict(c_ctx=c_ctx, w_mod=w_mod, b_mod=b_mod, g_mix_pre=g_mix_pre, g_mix_post=g_mix_post, g_ffn_pre=g_ffn_pre,
             g_ffn_post=g_ffn_post, w_in=w_in, conv_w=conv_w, conv_b=conv_b, lru_wa=lru_wa, lru_ba=lru_ba,
             lru_wx=lru_wx, lru_bx=lru_bx, lru_lam=lru_lam, attn_sink=attn_sink, w_o_rnn=w_o_rnn, w_o_attn=w_o_attn,
             w_out=w_out, w_ffn_in=w_ffn_in, w_ffn_out=w_ffn_out)
    Mo = dict(c_ctx=m_c_ctx, w_mod=m_w_mod, b_mod=m_b_mod, g_mix_pre=m_g_mix_pre, g_mix_post=m_g_mix_post,
              g_ffn_pre=m_g_ffn_pre, g_ffn_post=m_g_ffn_post, w_in=m_w_in, conv_w=m_conv_w, conv_b=m_conv_b,
              lru_wa=m_lru_wa, lru_ba=m_lru_ba, lru_wx=m_lru_wx, lru_bx=m_lru_bx, lru_lam=m_lru_lam,
              attn_sink=m_attn_sink, w_o_rnn=m_w_o_rnn, w_o_attn=m_w_o_attn, w_out=m_w_out, w_ffn_in=m_w_ffn_in,
              w_ffn_out=m_w_ffn_out)
    Vo = dict(c_ctx=v_c_ctx, w_mod=v_w_mod, b_mod=v_b_mod, g_mix_pre=v_g_mix_pre, g_mix_post=v_g_mix_post,
              g_ffn_pre=v_g_ffn_pre, g_ffn_post=v_g_ffn_post, w_in=v_w_in, conv_w=v_conv_w, conv_b=v_conv_b,
              lru_wa=v_lru_wa, lru_ba=v_lru_ba, lru_wx=v_lru_wx, lru_bx=v_lru_bx, lru_lam=v_lru_lam,
              attn_sink=v_attn_sink, w_o_rnn=v_w_o_rnn, w_o_attn=v_w_o_attn, w_out=v_w_out, w_ffn_in=v_w_ffn_in,
              w_ffn_out=v_w_ffn_out)
    L = w_in.shape[0]
    S = x.shape[1]
    me = _lin(*_place())

    small = jnp.concatenate([c.reshape(8, 128), conv_w.reshape(L * CONV_W, 128), lru_ba.reshape(2 * L, 128),
                             lru_bx.reshape(2 * L, 128), lru_lam.reshape(2 * L, 128), jnp.zeros((4, 128), F32)], axis=0)
    small_all = _allgather_small("ag_small", small)
    c_all = small_all[:, 0:8].reshape(N_DEV, D)
    conv_w_f = _chan_full(small_all[:, 8:16]).reshape(L, CONV_W, D)
    lru_ba_f = _chan_full(small_all[:, 16:20]).reshape(L, 2, D)
    lru_bx_f = _chan_full(small_all[:, 20:24]).reshape(L, 2, D)
    lru_lam_f = _chan_full(small_all[:, 24:28]).reshape(L, 2, D)

    c9 = jnp.concatenate([c_all, c_ctx[None], jnp.zeros((MOD_ROWS - N_DEV - 1, D), F32)], axis=0)
    b_shard = lax.dynamic_slice_in_dim(b_mod, me * MOD_SHARD, MOD_SHARD, axis=1)[:, None, :]
    mod_part = _mod_fwd("mod_fwd", c9, w_mod, b_shard)
    mod_all = _allgather_small("ag_mod", mod_part.reshape(L * MOD_ROWS, MOD_SHARD))
    mod_all = jnp.transpose(mod_all.reshape(N_DEV, L, MOD_ROWS, MOD_SHARD), (1, 2, 0, 3)).reshape(L, MOD_ROWS, 6 * D)
    own_row = lax.dynamic_index_in_dim(mod_all, me, axis=1, keepdims=False)
    modrows = jnp.stack([mod_all[:, N_DEV], own_row], axis=1)

    shards = [{k: (P[src][l].T if tr else P[src][l]).astype(BF16) for k, src, tr in zip(BIG, BIG_SRC, BIG_T)}
              for l in range(L)]
    Ws = []
    for l in range(L):
        W = {}
        W.update(
            cw=conv_w_f[l], cb=conv_b[l][None],
            w4=jnp.concatenate([lru_wa[l, 0], lru_wa[l, 1], lru_wx[l, 0], lru_wx[l, 1]], axis=-1).astype(BF16),
            b4=jnp.concatenate([lru_ba_f[l, 0].reshape(N_RNN_BLOCKS, 1, RB), lru_ba_f[l, 1].reshape(N_RNN_BLOCKS, 1, RB),
                                lru_bx_f[l, 0].reshape(N_RNN_BLOCKS, 1, RB), lru_bx_f[l, 1].reshape(N_RNN_BLOCKS, 1, RB)],
                               axis=-1),
            lam=lru_lam_f[l], sink4=jnp.broadcast_to(attn_sink[l].reshape(N_KV, Q_PER_KV, 1), (N_KV, Q_PER_KV, HEAD)),
            g_mix_pre=g_mix_pre[l][None], g_mix_post=g_mix_post[l][None], g_ffn_pre=g_ffn_pre[l][None],
            g_ffn_post=g_ffn_post[l][None], mod=modrows[l])
        Ws.append(W)

    plan = _Plan(shards, Ws)
    sq, dxa, Gs = _local_step(ctx[0], x[0], loss_target[0], Ws, S, plan)
    loss_part = ((0.5 / D) * jnp.sum(sq)).reshape(1, 1)
    grad_x = dxa[CTX:][None]

    dmod = jnp.concatenate([Gs[l]["mod"] for l in range(L)] + [jnp.zeros((8 - 2 * L, 6 * D), F32)], axis=0)
    dmod_all = _allgather_small("ag_dmod", dmod)
    dmod_cols = lax.dynamic_slice_in_dim(dmod_all, me * MOD_SHARD, MOD_SHARD, axis=2)
    g_w_mod, g_b_mod, dsc_part = _mod_bwd("mod_bwd", c9, w_mod, dmod_all, dmod_cols)
    g_b_mod = g_b_mod[:, 0]

    def rows(name, shape):
        return jnp.concatenate([Gs[l][name].reshape(shape) for l in range(L)], axis=0)

    b4g = [Gs[l]["b4"].reshape(N_RNN_BLOCKS, 4, RB) for l in range(L)]
    sink_row = jnp.concatenate([Gs[l]["sink4"][:, :, 0].reshape(1, N_Q) for l in range(L)]
                               + [loss_part, jnp.zeros((1, D - L * N_Q - 1), F32)], axis=1)
    small_g = jnp.concatenate(
        [rows("g_mix_pre", (1, D)), rows("g_mix_post", (1, D)), rows("g_ffn_pre", (1, D)), rows("g_ffn_post", (1, D)),
         rows("cb", (1, D)), rows("cw", (CONV_W, D))]
        + [b4g[l][:, d].reshape(1, D) for l in range(L) for d in range(2)]
        + [b4g[l][:, 2 + d].reshape(1, D) for l in range(L) for d in range(2)]
        + [rows("lam", (2, D)), sink_row, dsc_part], axis=0)
    n_small = small_g.shape[0]
    small_tot = _sum_blocks("sum_small", _allgather_small("ag_small_grads", small_g))
    o = 0
    G = {}
    for name in ("g_mix_pre", "g_mix_post", "g_ffn_pre", "g_ffn_post", "conv_b"):
        G[name] = small_tot[o:o + L]
        o += L
    G["conv_w"] = small_tot[o:o + L * CONV_W].reshape(L, CONV_W, D)
    o += L * CONV_W
    for name in ("lru_ba", "lru_bx", "lru_lam"):
        G[name] = small_tot[o:o + 2 * L].reshape(L, 2, D)
        o += 2 * L
    G["attn_sink"] = small_tot[o, :L * N_Q].reshape(L, N_Q)
    loss = small_tot[o, L * N_Q]
    sg = jax.nn.sigmoid(c_ctx)
    G["c_ctx"] = small_tot[o + 1] * (sg * (1.0 + c_ctx * (1.0 - sg)))
    G["b_mod"] = g_b_mod
    G["w_mod"] = g_w_mod

    last_slots, = _exchange_shards("exchange_w_in0", [[Gs[0]["win_t_b"].reshape(N_DEV, -1, D // 2)]], 1)
    plan.slots[0]["win_t_b"] = last_slots[0]
    for l in range(L):
        plan.slots[l]["win_t"] = [plan.slots[l]["win_t_a"], plan.slots[l]["win_t_b"]]

    out_g, out_d, out_m, out_v = {}, {}, {}, {}

    def put(name, res, shape=None):
        g, d, m, v = res
        for dst, val in ((out_g, g), (out_d, d), (out_m, m), (out_v, v)):
            dst[name] = val if shape is None else val.reshape(shape)

    for k, src, tr, tile in zip(BIG, BIG_SRC, BIG_T, BIG_TILE):
        lay = (lambda a: jnp.swapaxes(a, 1, 2)) if tr else (lambda a: a)
        wmv = (lay(P[src]), lay(Mo[src]), lay(Vo[src]))
        res = _adamw_slots("adamw_" + src, [plan.slots[l][k] for l in range(L)], wmv[0].shape, tile, wmv)
        put(src, [lay(r) for r in res])
    res = _adamw("adamw_w_mod", w_mod.reshape(L * D, MOD_SHARD), g_w_mod.reshape(L * D, MOD_SHARD),
                 m_w_mod.reshape(L * D, MOD_SHARD), v_w_mod.reshape(L * D, MOD_SHARD), 256)
    put("w_mod", (g_w_mod,) + tuple(res), w_mod.shape)
    def fuse4(wa, wx):
        return jnp.concatenate([wa[:, 0], wa[:, 1], wx[:, 0], wx[:, 1]], axis=-1).reshape(L, N_RNN_BLOCKS * RB, 4 * RB)

    res = _adamw_slots("adamw_gates", plan.gate_slots, (L, N_RNN_BLOCKS * RB, 4 * RB), 256,
                       (fuse4(lru_wa, lru_wx), fuse4(m_lru_wa, m_lru_wx), fuse4(v_lru_wa, v_lru_wx)))
    res = [r.reshape(L, N_RNN_BLOCKS, RB, 4, RB) for r in res]
    put("lru_wa", [jnp.stack([r[:, :, :, 0], r[:, :, :, 1]], axis=1) for r in res])
    put("lru_wx", [jnp.stack([r[:, :, :, 2], r[:, :, :, 3]], axis=1) for r in res])
    rep = ("g_mix_pre", "g_mix_post", "g_ffn_pre", "g_ffn_post", "conv_b", "b_mod")

    def pack_rep(T_):
        sink = jnp.concatenate([T_["attn_sink"].reshape(1, L * N_Q), jnp.zeros((1, D - L * N_Q), F32)], axis=1)
        return jnp.concatenate([T_[n].reshape(-1, D) for n in rep] + [sink, T_["c_ctx"][None]], axis=0)

    pk = [pack_rep(T_) for T_ in (P, G, Mo, Vo)]
    n_rep = pk[0].shape[0]
    res = _adamw("adamw_replicated", *[jnp.pad(a, ((0, 24 - n_rep), (0, 0))) for a in pk], 24)
    res = (pk[1],) + tuple(r[:n_rep] for r in res)
    o = 0
    for n in rep:
        k = P[n].size // D
        put(n, [r[o:o + k] for r in res], P[n].shape)
        o += k
    put("attn_sink", [r[o, :L * N_Q] for r in res], attn_sink.shape)
    put("c_ctx", [r[o + 1] for r in res], c_ctx.shape)
    chan = ("conv_w", "lru_ba", "lru_bx", "lru_lam")
    g_own = {n: lax.dynamic_slice_in_dim(G[n], me * RB, RB, axis=2) for n in chan}

    def pack_chan(T_):
        return jnp.concatenate([T_[n].reshape(-1, RB) for n in chan], axis=0)

    pk = [pack_chan(T_) for T_ in (P, g_own, Mo, Vo)]
    n_ch = pk[0].shape[0]
    res = _adamw("adamw_channels", *[jnp.pad(a, ((0, 24 - n_ch), (0, 0))) for a in pk], 24)
    res = (pk[1],) + tuple(r[:n_ch] for r in res)
    o = 0
    for n in chan:
        k = P[n].size // RB
        put(n, [r[o:o + k] for r in res], P[n].shape)
        o += k

    order = ("c_ctx", "w_mod", "b_mod", "g_mix_pre", "g_mix_post", "g_ffn_pre", "g_ffn_post", "w_in", "conv_w", "conv_b",
             "lru_wa", "lru_ba", "lru_wx", "lru_bx", "lru_lam", "attn_sink", "w_o_rnn", "w_o_attn", "w_out", "w_ffn_in",
             "w_ffn_out")
    return (loss, grad_x, *[out_g[n] for n in order], *[out_d[n] for n in order], *[out_m[n] for n in order],
            *[out_v[n] for n in order])
```

```python
import functools
import math

import numpy as np
import jax
import jax.numpy as jnp
from jax import lax
from jax.experimental import pallas as pl
from jax.experimental.pallas import tpu as pltpu

F32 = jnp.float32
BF16 = jnp.bfloat16

D = 1024
CTX = 256
TR = 256
HEAD = 128
N_Q = 8
N_KV = 2
Q_PER_KV = N_Q // N_KV
GRID_W = 64
N_FREQ = HEAD // 4
ROPE_BASE = 10000.0
N_RNN_BLOCKS = 8
CONV_W = 4
CONV_LEFT = 2
LRU_C = 8.0
D_FF = 2816
IN_W = 5632
P_W = IN_W
DP_W = 3584
COL_XR, COL_GR, COL_Q, COL_K, COL_V, COL_GL = 0, 1024, 2048, 3072, 3328, 3584
GLB = 512
EPS = 1e-6
NEG_INF = -1e30
ATT_SCALE = HEAD ** -0.5
N_DEV = 8
VMEM_LIMIT = 56 * 1024 * 1024

ADAM_LR, ADAM_B1, ADAM_B2, ADAM_EPS, ADAM_WD, ADAM_STEP = 0.001, 0.9, 0.999, 1e-08, 0.01, 10

NN = (((1,), (0,)), ((), ()))
NT = (((1,), (1,)), ((), ()))
TN = (((0,), (0,)), ((), ()))


def _dot(a, b, dims=NN):
    return lax.dot_general(a, b, dims, preferred_element_type=F32)


def _params(sem=("arbitrary",)):
    return pltpu.CompilerParams(dimension_semantics=sem, vmem_limit_bytes=VMEM_LIMIT)


def _full_spec(shape):
    nd = len(shape)
    return pl.BlockSpec(shape, lambda *_: (0,) * nd)


ANY = pl.BlockSpec(memory_space=pl.ANY)


def _ew(name, body, n, row_ins, pars, row_outs, accs=(), alias=None):
    n_ri, n_p, n_ro, n_acc = len(row_ins), len(pars), len(row_outs), len(accs)

    def kern(*refs):
        i = pl.program_id(0)
        ins = refs[:n_ri]
        ps = refs[n_ri:n_ri + n_p]
        outs = refs[n_ri + n_p:n_ri + n_p + n_ro]
        acc = refs[n_ri + n_p + n_ro:]
        if n_acc:
            @pl.when(i == 0)
            def _():
                for a in acc:
                    a[...] = jnp.zeros(a.shape, a.dtype)
        body(i, ins, ps, outs, acc)

    in_specs = [ANY if blk is None else pl.BlockSpec(blk, imap) for (_, blk, imap) in row_ins]
    in_specs += [_full_spec(p.shape) for p in pars]
    out_specs = [pl.BlockSpec(blk, imap) for (_, blk, imap) in row_outs] + [_full_spec(a.shape) for a in accs]
    out_shape = [s for (s, _, _) in row_outs] + list(accs)
    return pl.pallas_call(
        kern, name=name, grid=(n,), in_specs=in_specs, out_specs=out_specs, out_shape=out_shape,
        input_output_aliases=alias or {}, compiler_params=_params(),
    )(*[a for (a, _, _) in row_ins], *pars)


def _rowblk(width, colblk=0, roff=0, tile=TR):
    return (tile, width), (lambda i: (i + roff, colblk))


def _sds(shape, dtype):
    return jax.ShapeDtypeStruct(shape, dtype)


class _Carry:
    SAME_CORE = (1, 3, 5)

    def __init__(self, jobs):
        self.jobs = list(jobs)
        self.arrays = [a for _, a in self.jobs]
        self.out_shapes = [_sds(a.shape if kind == "scatter" else (N_DEV, *a.shape), a.dtype) for kind, a in self.jobs]
        n = len(self.jobs)
        self.scratch = [pltpu.SemaphoreType.DMA((n, 7)), pltpu.SemaphoreType.DMA((n, 7)), pltpu.SemaphoreType.DMA((n,))]

    def _setup(self, sems):
        send_sems, recv_sems, local_sems = sems
        x, y, c = _place()
        me = _lin(x, y, c)
        peers = [(x ^ ((k + 1) >> 2 & 1), y ^ ((k + 1) >> 1 & 1), c ^ ((k + 1) & 1)) for k in range(7)]

        def copy(a, k, sem_k, src, dst):
            return pltpu.make_async_remote_copy(src_ref=src, dst_ref=dst, send_sem=send_sems.at[a, sem_k],
                                                recv_sem=recv_sems.at[a, sem_k], device_id=peers[k], device_id_type=MESH)

        return me, [_lin(*p) for p in peers], copy, local_sems

    def _local(self, a, kind, ins, outs, me, local_sems):
        return pltpu.make_async_copy(ins[a].at[me] if kind == "scatter" else ins[a], outs[a].at[me], local_sems.at[a])

    def start(self, ins, outs, sems):
        me, theirs, copy, local_sems = self._setup(sems)
        for a, (kind, _) in enumerate(self.jobs):
            self._local(a, kind, ins, outs, me, local_sems).start()
            if kind == "scatter":
                for k in range(7):
                    copy(a, k, k, ins[a].at[theirs[k]], outs[a].at[me]).start()
            else:
                for k in (0,) + self.SAME_CORE:
                    copy(a, k, k, ins[a], outs[a].at[me]).start()

    def wait(self, ins, outs, sems):
        me, theirs, copy, local_sems = self._setup(sems)
        for a, (kind, _) in enumerate(self.jobs):
            if kind == "scatter":
                for k in range(7):
                    copy(a, k, k, ins[a].at[me], outs[a].at[theirs[k]]).wait_recv()
                for k in range(7):
                    copy(a, k, k, ins[a].at[theirs[k]], outs[a].at[me]).wait_send()
            else:
                for k in self.SAME_CORE:
                    blk = outs[a].at[theirs[k]]
                    copy(a, k, k, ins[a], blk).wait_recv()
                    copy(a, 0, k + 1, blk, blk).start()
                copy(a, 0, 0, ins[a], outs[a].at[theirs[0]]).wait_recv()
                for k in self.SAME_CORE:
                    copy(a, 0, k + 1, ins[a], outs[a].at[theirs[k + 1]]).wait_recv()
                for k in (0,) + self.SAME_CORE:
                    copy(a, k, k, ins[a], outs[a].at[me]).wait_send()
                for k in self.SAME_CORE:
                    blk = outs[a].at[theirs[k]]
                    copy(a, 0, k + 1, blk, blk).wait_send()
            self._local(a, kind, ins, outs, me, local_sems).wait()


def _carried(kern, carry, n_in, n_out, first, last):
    if carry is None:
        return kern
    nc = len(carry.jobs)

    def wrapped(*refs):
        ins, cin = refs[:n_in], refs[n_in:n_in + nc]
        outs, cout = refs[n_in + nc:n_in + nc + n_out], refs[n_in + nc + n_out:n_in + 2 * nc + n_out]
        scr, sems = refs[n_in + 2 * nc + n_out:-3], refs[-3:]

        @pl.when(first())
        def _():
            carry.start(cin, cout, sems)

        kern(*ins, *outs, *scr)

        @pl.when(last())
        def _():
            carry.wait(cin, cout, sems)

    return wrapped


def _carry_args(carry):
    if carry is None:
        return [], [], [], [], []
    n = len(carry.jobs)
    return [ANY] * n, carry.arrays, [ANY] * n, carry.out_shapes, carry.scratch


def _grid_ends(dims):
    first = lambda: functools.reduce(jnp.logical_and, [pl.program_id(d) == 0 for d in range(len(dims))])
    last = lambda: functools.reduce(jnp.logical_and, [pl.program_id(d) == n - 1 for d, n in enumerate(dims)])
    return first, last


def _mm_call(name, a, b, mode, out_dtype, tm, tn, rows_outer=True, single_b=False, carry=None):
    if mode == "TN":
        (K, M), N = a.shape, b.shape[1]
    else:
        (M, K), N = a.shape, (b.shape[1] if mode == "NN" else b.shape[0])
    assert M % tm == 0 and N % tn == 0, (name, M, N, K, tm, tn)
    ij = (lambda g0, g1: (g0, g1)) if rows_outer else (lambda g0, g1: (g1, g0))
    grid = (M // tm, N // tn) if rows_outer else (N // tn, M // tm)
    if mode == "TN":
        a_spec = pl.BlockSpec((K, tm), lambda g0, g1: (0, ij(g0, g1)[0]))
    else:
        a_spec = pl.BlockSpec((tm, K), lambda g0, g1: (ij(g0, g1)[0], 0))
    b_blk, b_map = ((tn, K), lambda g0, g1: (ij(g0, g1)[1], 0)) if mode == "NT" else \
                   ((K, tn), lambda g0, g1: (0, ij(g0, g1)[1]))
    b_spec = pl.BlockSpec(b_blk, b_map, pipeline_mode=pl.Buffered(1)) if single_b else pl.BlockSpec(b_blk, b_map)
    dims = {"NN": NN, "NT": NT, "TN": TN}[mode]

    def kern(a_ref, b_ref, o_ref):
        o_ref[...] = _dot(a_ref[...], b_ref[...], dims).astype(o_ref.dtype)

    ci, ca, co, cs, cscr = _carry_args(carry)
    res = pl.pallas_call(
        _carried(kern, carry, 2, 1, *_grid_ends(grid)), name=name, grid=grid, in_specs=[a_spec, b_spec] + ci,
        out_specs=[pl.BlockSpec((tm, tn), lambda g0, g1: ij(g0, g1))] + co,
        out_shape=[_sds((M, N), out_dtype)] + cs, scratch_shapes=cscr,
        compiler_params=_params(("arbitrary", "arbitrary")),
    )(a, b, *ca)
    return res[0] if carry is None else (res[0], res[1:])


def _mm_act(name, a, w, mode, out_dtype=BF16, carry=None):
    rows, K = a.shape
    N = w.shape[1] if mode == "NN" else w.shape[0]
    if K > D_FF:
        return _mm_call(name, a, w, mode, out_dtype, rows // 8, N, single_b=True, carry=carry)
    tn = N if N <= 1024 else 1408
    return _mm_call(name, a, w, mode, out_dtype, rows // 4, tn, carry=carry)


def _mm_wgrad(name, x, dy, out_dtype=BF16, carry=None):
    M = x.shape[1]
    tm = 1408 if M == D_FF else 512
    return _mm_call(name, x, dy, "TN", out_dtype, tm, dy.shape[1], single_b=True, carry=carry)


def _sigmoid(x):
    return 0.5 * jnp.tanh(0.5 * x) + 0.5


def _silu(x):
    return x * _sigmoid(x)


def _silu_grad(x):
    s = _sigmoid(x)
    return s * (1.0 + x * (1.0 - s))


_GELU_K = math.sqrt(2.0 / math.pi)


def _gelu(x):
    return 0.5 * x * (1.0 + jnp.tanh(_GELU_K * (x + 0.044715 * x * x * x)))


def _gelu_grad(x):
    t = jnp.tanh(_GELU_K * (x + 0.044715 * x * x * x))
    return 0.5 * (1.0 + t) + 0.5 * x * (1.0 - t * t) * _GELU_K * (1.0 + 3.0 * 0.044715 * x * x)


def _log_sigmoid(x):
    return jnp.minimum(x, 0.0) - jnp.log(1.0 + jnp.exp(-jnp.abs(x)))


def _rms(x):
    x = x.astype(F32)
    r = lax.rsqrt(jnp.mean(x * x, axis=-1, keepdims=True) + EPS)
    return x * r, r


def _rms_bwd(dy, y, r):
    return r * (dy - y * jnp.mean(dy * y, axis=-1, keepdims=True))


def _modrow(mod_ref, i, chunk):
    lo = mod_ref[0:1, chunk * D:(chunk + 1) * D]
    hi = mod_ref[1:2, chunk * D:(chunk + 1) * D]
    return jnp.where(i == 0, lo, hi)


def _acc_seg(acc_ref, i, val):
    zero = jnp.zeros_like(val)
    acc_ref[0:1, :] += jnp.where(i == 0, val, zero)
    acc_ref[1:2, :] += jnp.where(i == 0, zero, val)


def _colsum(x):
    return jnp.sum(x, axis=0, keepdims=True)


SH1, SC1, GA1, SH2, SC2, GA2 = range(6)


def _normmod_fwd(name, ctx, x, g, mod, c_sh, c_sc, carry=None):
    T = ctx.shape[0] + x.shape[0]
    assert ctx.shape[0] == TR
    n = T // TR

    def kern(ctx_ref, x_ref, g_ref, mod_ref, xa_ref, h_ref):
        i = pl.program_id(0)
        v = jnp.where(i == 0, ctx_ref[...], x_ref[...])
        xa_ref[...] = v
        y, _ = _rms(v)
        h = (y * g_ref[...]) * (1.0 + _modrow(mod_ref, i, c_sc)) + _modrow(mod_ref, i, c_sh)
        h_ref[...] = h.astype(BF16)

    row = pl.BlockSpec((TR, D), lambda i: (i, 0))
    ci, ca, co, cs, cscr = _carry_args(carry)
    res = pl.pallas_call(
        _carried(kern, carry, 4, 2, *_grid_ends((n,))), name=name, grid=(n,),
        in_specs=[pl.BlockSpec((TR, D), lambda i: (0, 0)), pl.BlockSpec((TR, D), lambda i: (jnp.maximum(i - 1, 0), 0)),
                  _full_spec(g.shape), _full_spec(mod.shape)] + ci,
        out_specs=[row, row] + co, out_shape=[_sds((T, D), F32), _sds((T, D), BF16)] + cs, scratch_shapes=cscr,
        compiler_params=_params(),
    )(ctx, x, g, mod, *ca)
    return res if carry is None else (res[:2], res[2:])


def _modrows(mod_ref, row0, n, chunk):
    t = row0 + lax.broadcasted_iota(jnp.int32, (n, 1), 0)
    return jnp.where(t < CTX, mod_ref[0:1, chunk * D:(chunk + 1) * D], mod_ref[1:2, chunk * D:(chunk + 1) * D])


def _loss_resid_bwd(name, x_out, target, mat, gpost, mod, c_ga):
    T = x_out.shape[0]

    def body(i, ins, ps, outs, acc):
        err = ins[0][...] - ins[1][...]
        lat = i > 0
        dx = jnp.where(lat, err * (1.0 / D), 0.0)
        outs[0][...] = dx
        acc[2][...] += jnp.where(lat, _colsum(err * err), 0.0)
        outs[1][...] = _resid_bwd_vals(i, dx, ins[2][...], ps[0][...], ps[1], c_ga, acc[0], acc[1]).astype(BF16)

    tgt_blk = ((TR, D), lambda i: (jnp.maximum(i - 1, 0), 0))
    return _ew(name, body, T // TR, [(x_out, *_rowblk(D)), (target, *tgt_blk), (mat, *_rowblk(D))], [gpost, mod],
               [(_sds((T, D), F32), *_rowblk(D)), (_sds((T, D), BF16), *_rowblk(D))],
               [_sds((2, D), F32), _sds((1, D), F32), _sds((1, D), F32)])


def _mod_for(mod_ref, i, chunk, row0, n):
    return _modrow(mod_ref, i, chunk) if row0 is None else _modrows(mod_ref, row0, n, chunk)


def _acc_for(acc_ref, i, v, row0):
    if row0 is None:
        _acc_seg(acc_ref, i, _colsum(v))
        return

    @pl.when(row0 < CTX)
    def _():
        is_ctx = row0 + lax.broadcasted_iota(jnp.int32, (v.shape[0], 1), 0) < CTX
        acc_ref[0:1, :] += _colsum(jnp.where(is_ctx, v, 0.0))
        acc_ref[1:2, :] += _colsum(jnp.where(is_ctx, 0.0, v))

    @pl.when(row0 >= CTX)
    def _():
        acc_ref[1:2, :] += _colsum(v)


def _resid_bwd_vals(i, dout, mat, gpost, mod_ref, c_ga, acc_ga, acc_g, row0=None):
    ym, rm = _rms(mat)
    ga = _mod_for(mod_ref, i, c_ga, row0, dout.shape[0])
    _acc_for(acc_ga, i, dout * (ym * gpost), row0)
    dn = dout * ga
    acc_g[...] += _colsum(dn * ym)
    return _rms_bwd(dn * gpost, ym, rm)


def _normmod_bwd_vals(i, dh, xin, g, mod_ref, c_sh, c_sc, acc_sh, acc_sc, acc_g, row0=None):
    dh = dh.astype(F32)
    y, r = _rms(xin)
    _acc_for(acc_sc, i, dh * (y * g), row0)
    _acc_for(acc_sh, i, dh, row0)
    dyg = dh * (1.0 + _mod_for(mod_ref, i, c_sc, row0, dh.shape[0]))
    acc_g[...] += _colsum(dyg * y)
    return _rms_bwd(dyg * g, y, r)


def _parts(i, tm):
    return [(slice(0, tm), i * tm)]


FT = 1408


def _ffn_in_fused(name, h2, w_t, carry=None):
    T = h2.shape[0]
    tm, nj = T // 4, D_FF // FT

    def kern(a_ref, bg_ref, bu_ref, fg_ref, fu_ref, s_ref):
        for rows, _ in _parts(0, tm):
            a = a_ref[rows, :]
            g = _dot(a, bg_ref[...], NT)
            u = _dot(a, bu_ref[...], NT)
            fg_ref[rows, :] = g.astype(BF16)
            fu_ref[rows, :] = u.astype(BF16)
            s_ref[rows, :] = (_silu(g) * u).astype(BF16)

    o_spec = pl.BlockSpec((tm, FT), lambda i, j: (i, j))
    ci, ca, co, cs, cscr = _carry_args(carry)
    res = pl.pallas_call(
        _carried(kern, carry, 3, 3, *_grid_ends((4, nj))), name=name, grid=(4, nj),
        in_specs=[pl.BlockSpec((tm, D), lambda i, j: (i, 0)), pl.BlockSpec((FT, D), lambda i, j: (j, 0)),
                  pl.BlockSpec((FT, D), lambda i, j: (j + nj, 0))] + ci,
        out_specs=[o_spec] * 3 + co, out_shape=[_sds((T, D_FF), BF16)] * 3 + cs, scratch_shapes=cscr,
        compiler_params=_params(("arbitrary", "arbitrary")),
    )(h2, w_t, w_t, *ca)
    return res if carry is None else (res[:3], res[3:])


def _norm_chain(row0, xin, mat, gpost, mod_ref, c_ga, gnext, modn_ref, c_sh, c_sc):
    n = xin.shape[0]
    ym, _ = _rms(mat.astype(BF16))
    xo = xin + _modrows(mod_ref, row0, n, c_ga) * (ym * gpost)
    y, _ = _rms(xo)
    h = (y * gnext) * (1.0 + _modrows(modn_ref, row0, n, c_sc)) + _modrows(modn_ref, row0, n, c_sh)
    return xo, h.astype(BF16)


def _out_fused(name, p, u, o_all, xa, w_o_rnn, w_o_attn, w_out, gpost, mod, gnext):
    T = u.shape[0]
    tm = T // 8

    def kern(g0, g1, g2, g3, u_ref, o_ref, xa_ref, wr_ref, wa_ref, w_ref, gpost_ref, mod_ref, gnext_ref,
             ya_ref, yb_ref, z_ref, m_ref, x1_ref, h2_ref):
        for rows, row0 in _parts(pl.program_id(0), tm):
            ya = _dot(u_ref[rows, :], wr_ref[...]).astype(BF16)
            yb = _dot(o_ref[rows, :], wa_ref[...]).astype(BF16)
            ya_ref[rows, :] = ya
            yb_ref[rows, :] = yb
            ga = _sigmoid(jnp.concatenate([g0[rows, :], g1[rows, :]], axis=1).astype(F32))
            gb = _sigmoid(jnp.concatenate([g2[rows, :], g3[rows, :]], axis=1).astype(F32))
            z = (ga * ya.astype(F32) + gb * yb.astype(F32)).astype(BF16)
            z_ref[rows, :] = z
            m = _dot(z, w_ref[...])
            m_ref[rows, :] = m.astype(BF16)
            x1_ref[rows, :], h2_ref[rows, :] = _norm_chain(row0, xa_ref[rows, :], m, gpost_ref[...], mod_ref, GA1,
                                                           gnext_ref[...], mod_ref, SH2, SC2)

    row = lambda w: pl.BlockSpec((tm, w), lambda i: (i, 0))
    return pl.pallas_call(
        kern, name=name, grid=(T // tm,),
        in_specs=[pl.BlockSpec((tm, GLB), lambda i, q=q: (i, COL_GL // GLB + q)) for q in range(4)]
                 + [row(D), row(D), row(D)] + [_full_spec(a.shape) for a in (w_o_rnn, w_o_attn, w_out, gpost, mod, gnext)],
        out_specs=[row(D)] * 6,
        out_shape=[_sds((T, D), BF16)] * 4 + [_sds((T, D), F32), _sds((T, D), BF16)],
        compiler_params=_params(),
    )(p, p, p, p, u, o_all, xa, w_o_rnn, w_o_attn, w_out, gpost, mod, gnext)


def _ffn_out_fused(name, s, w, x1, gpost, mod, nxt=None):
    T = s.shape[0]
    tm = T // 8

    def kern(s_ref, w_ref, x1_ref, gpost_ref, mod_ref, *rest):
        for rows, row0 in _parts(pl.program_id(0), tm):
            e = _dot(s_ref[rows, :], w_ref[...])
            if nxt is None:
                e_ref, xo_ref = rest
                ym, _ = _rms(e.astype(BF16))
                xo_ref[rows, :] = x1_ref[rows, :] + _modrows(mod_ref, row0, e.shape[0], GA2) * (ym * gpost_ref[...])
            else:
                gnext_ref, modn_ref, e_ref, xo_ref, h_ref = rest
                xo_ref[rows, :], h_ref[rows, :] = _norm_chain(row0, x1_ref[rows, :], e, gpost_ref[...], mod_ref, GA2,
                                                              gnext_ref[...], modn_ref, SH1, SC1)
            e_ref[rows, :] = e.astype(BF16)

    row = lambda w_: pl.BlockSpec((tm, w_), lambda i: (i, 0))
    extra = [] if nxt is None else list(nxt)
    return pl.pallas_call(
        kern, name=name, grid=(T // tm,),
        in_specs=[row(D_FF), _full_spec(w.shape), row(D), _full_spec(gpost.shape), _full_spec(mod.shape)]
                 + [_full_spec(a.shape) for a in extra],
        out_specs=[row(D)] * (2 if nxt is None else 3),
        out_shape=[_sds((T, D), BF16), _sds((T, D), F32)] + ([] if nxt is None else [_sds((T, D), BF16)]),
        compiler_params=_params(),
    )(s, w, x1, gpost, mod, *extra)


def _ffn_bwd_fused(name, fg, fu, w, de=None, head=None):
    T = fg.shape[0]
    tm = T // 8
    row = lambda w_: pl.BlockSpec((tm, w_), lambda i: (i, 0))
    w_spec = pl.BlockSpec(w.shape, lambda i: (0, 0), pipeline_mode=pl.Buffered(1))

    def tail(rows, de_v, fg_ref, fu_ref, w_ref, df_ref):
        ds = _dot(de_v, w_ref[...], NT)
        g, u = fg_ref[rows, :].astype(F32), fu_ref[rows, :].astype(F32)
        df_ref[rows, :] = jnp.concatenate([ds * u * _silu_grad(g), ds * _silu(g)], axis=1).astype(BF16)

    if head is None:
        def kern(de_ref, fg_ref, fu_ref, w_ref, df_ref):
            for rows, _ in _parts(pl.program_id(0), tm):
                tail(rows, de_ref[rows, :], fg_ref, fu_ref, w_ref, df_ref)

        return pl.pallas_call(
            kern, name=name, grid=(T // tm,), in_specs=[row(D), row(D_FF), row(D_FF), w_spec],
            out_specs=[row(2 * D_FF)], out_shape=[_sds((T, 2 * D_FF), BF16)], compiler_params=_params(),
        )(de, fg, fu, w)

    dx2, e, gpost, mod = head

    def kern(dx_ref, e_ref, fg_ref, fu_ref, w_ref, gpost_ref, mod_ref, de_ref, df_ref, dga_ref, dg_ref):
        i = pl.program_id(0)

        @pl.when(i == 0)
        def _():
            dga_ref[...] = jnp.zeros(dga_ref.shape, F32)
            dg_ref[...] = jnp.zeros(dg_ref.shape, F32)

        for rows, row0 in _parts(i, tm):
            de_v = _resid_bwd_vals(i, dx_ref[rows, :], e_ref[rows, :], gpost_ref[...], mod_ref, GA2, dga_ref, dg_ref,
                                   row0=row0).astype(BF16)
            de_ref[rows, :] = de_v
            tail(rows, de_v, fg_ref, fu_ref, w_ref, df_ref)

    return pl.pallas_call(
        kern, name=name, grid=(T // tm,),
        in_specs=[row(D), row(D), row(D_FF), row(D_FF), w_spec, _full_spec(gpost.shape), _full_spec(mod.shape)],
        out_specs=[row(D), row(2 * D_FF), _full_spec((2, D)), _full_spec((1, D))],
        out_shape=[_sds((T, D), BF16), _sds((T, 2 * D_FF), BF16), _sds((2, D), F32), _sds((1, D), F32)],
        compiler_params=_params(),
    )(dx2, e, fg, fu, w, gpost, mod)


def _zero_at_start(i, refs):
    @pl.when(i == 0)
    def _():
        for r in refs:
            r[...] = jnp.zeros(r.shape, F32)


def _proj_bwd_fused(name, dp, dgl, w_in_t, xa, dx1, gpre, mod, carry=None):
    T = dp.shape[0]
    tm = T // 8
    row = lambda w_: pl.BlockSpec((tm, w_), lambda i: (i, 0))

    def kern(dp_ref, dgl_ref, w_ref, xa_ref, dx1_ref, g_ref, mod_ref, dxa_ref, dsh_ref, dsc_ref, dg_ref):
        i = pl.program_id(0)
        _zero_at_start(i, (dsh_ref, dsc_ref, dg_ref))
        for rows, row0 in _parts(i, tm):
            dh = _dot(dp_ref[rows, :], w_ref[0:DP_W, :]) + _dot(dgl_ref[rows, :], w_ref[DP_W:, :])
            dxa_ref[rows, :] = dx1_ref[rows, :] + _normmod_bwd_vals(i, dh, xa_ref[rows, :], g_ref[...], mod_ref, SH1,
                                                                    SC1, dsh_ref, dsc_ref, dg_ref, row0=row0)

    ci, ca, co, cs, cscr = _carry_args(carry)
    res = pl.pallas_call(
        _carried(kern, carry, 7, 4, *_grid_ends((T // tm,))), name=name, grid=(T // tm,),
        in_specs=[row(DP_W), row(P_W - DP_W),
                  pl.BlockSpec(w_in_t.shape, lambda i: (0, 0), pipeline_mode=pl.Buffered(1)), row(D), row(D),
                  _full_spec(gpre.shape), _full_spec(mod.shape)] + ci,
        out_specs=[row(D), _full_spec((2, D)), _full_spec((2, D)), _full_spec((1, D))] + co,
        out_shape=[_sds((T, D), F32), _sds((2, D), F32), _sds((2, D), F32), _sds((1, D), F32)] + cs,
        scratch_shapes=cscr, compiler_params=_params(),
    )(dp, dgl, w_in_t, xa, dx1, gpre, mod, *ca)
    return res if carry is None else (res[:4], res[4:])


def _proj_wgrad(name, dp, dgl, h, carry=None):
    T, N = h.shape
    n1, n2 = DP_W // GLB, (P_W - DP_W) // GLB

    def kern(a1_ref, a2_ref, h_ref, o_ref):
        i = pl.program_id(0)

        @pl.when(i < n1)
        def _():
            o_ref[...] = _dot(a1_ref[...], h_ref[...], TN).astype(o_ref.dtype)

        @pl.when(i >= n1)
        def _():
            o_ref[...] = _dot(a2_ref[...], h_ref[...], TN).astype(o_ref.dtype)

    ci, ca, co, cs, cscr = _carry_args(carry)
    res = pl.pallas_call(
        _carried(kern, carry, 3, 1, *_grid_ends((n1 + n2,))), name=name, grid=(n1 + n2,),
        in_specs=[pl.BlockSpec((T, GLB), lambda i: (0, jnp.minimum(i, n1 - 1))),
                  pl.BlockSpec((T, GLB), lambda i: (0, jnp.maximum(i - n1, 0))),
                  pl.BlockSpec((T, N), lambda i: (0, 0), pipeline_mode=pl.Buffered(1))] + ci,
        out_specs=[pl.BlockSpec((GLB, N), lambda i: (i, 0))] + co,
        out_shape=[_sds((P_W, N), BF16)] + cs, scratch_shapes=cscr, compiler_params=_params(),
    )(dp, dgl, h, *ca)
    return res[0] if carry is None else (res[0], res[1:])


def _ffn_in_bwd_fused(name, df, w_t, x1, dres, mat, gpre, mod, gpost, carry=None):
    T = df.shape[0]
    tm = T // 8
    row = lambda w_: pl.BlockSpec((tm, w_), lambda i: (i, 0))

    def kern(df_ref, w_ref, x1_ref, dres_ref, mat_ref, gpre_ref, mod_ref, gpost_ref,
             dx1_ref, dm_ref, dsh_ref, dsc_ref, dgpre_ref, dga_ref, dgpost_ref):
        i = pl.program_id(0)
        _zero_at_start(i, (dsh_ref, dsc_ref, dgpre_ref, dga_ref, dgpost_ref))
        for rows, row0 in _parts(i, tm):
            dh2 = _dot(df_ref[rows, :], w_ref[...])
            dx1 = dres_ref[rows, :] + _normmod_bwd_vals(i, dh2, x1_ref[rows, :], gpre_ref[...], mod_ref, SH2, SC2,
                                                        dsh_ref, dsc_ref, dgpre_ref, row0=row0)
            dx1_ref[rows, :] = dx1
            dm_ref[rows, :] = _resid_bwd_vals(i, dx1, mat_ref[rows, :], gpost_ref[...], mod_ref, GA1, dga_ref,
                                              dgpost_ref, row0=row0).astype(BF16)

    ci, ca, co, cs, cscr = _carry_args(carry)
    res = pl.pallas_call(
        _carried(kern, carry, 8, 7, *_grid_ends((T // tm,))), name=name, grid=(T // tm,),
        in_specs=[row(2 * D_FF), pl.BlockSpec(w_t.shape, lambda i: (0, 0), pipeline_mode=pl.Buffered(1)), row(D),
                  row(D), row(D), _full_spec(gpre.shape), _full_spec(mod.shape), _full_spec(gpost.shape)] + ci,
        out_specs=[row(D), row(D), _full_spec((2, D)), _full_spec((2, D)), _full_spec((1, D)), _full_spec((2, D)),
                   _full_spec((1, D))] + co,
        out_shape=[_sds((T, D), F32), _sds((T, D), BF16), _sds((2, D), F32), _sds((2, D), F32), _sds((1, D), F32),
                   _sds((2, D), F32), _sds((1, D), F32)] + cs,
        scratch_shapes=cscr, compiler_params=_params(),
    )(df, w_t, x1, dres, mat, gpre, mod, gpost, *ca)
    return res if carry is None else (res[:7], res[7:])


def _out_bwd_fused(name, dm, w_out, w_o_rnn, w_o_attn, p, ya, yb):
    T = dm.shape[0]
    tm = T // 8
    row = lambda w_: pl.BlockSpec((tm, w_), lambda i: (i, 0))

    def kern(dm_ref, w_ref, wr_ref, wa_ref, g0, g1, g2, g3, ya_ref, yb_ref, dya_ref, dyb_ref, dgl_ref, du_ref, do_ref):
        for rows, _ in _parts(pl.program_id(0), tm):
            dz = _dot(dm_ref[rows, :], w_ref[...], NT)
            ga = _sigmoid(jnp.concatenate([g0[rows, :], g1[rows, :]], axis=1).astype(F32))
            gb = _sigmoid(jnp.concatenate([g2[rows, :], g3[rows, :]], axis=1).astype(F32))
            dya = (dz * ga).astype(BF16)
            dyb = (dz * gb).astype(BF16)
            dya_ref[rows, :] = dya
            dyb_ref[rows, :] = dyb
            dgl_ref[rows, :] = jnp.concatenate([dz * ya_ref[rows, :].astype(F32) * ga * (1.0 - ga),
                                                dz * yb_ref[rows, :].astype(F32) * gb * (1.0 - gb)],
                                               axis=1).astype(BF16)
            du_ref[rows, :] = _dot(dya, wr_ref[...], NT).astype(BF16)
            do_ref[rows, :] = _dot(dyb, wa_ref[...], NT).astype(BF16)

    return pl.pallas_call(
        kern, name=name, grid=(T // tm,),
        in_specs=[row(D)] + [_full_spec(w.shape) for w in (w_out, w_o_rnn, w_o_attn)]
                 + [pl.BlockSpec((tm, GLB), lambda i, q=q: (i, COL_GL // GLB + q)) for q in range(4)] + [row(D), row(D)],
        out_specs=[row(D), row(D), row(2 * D), row(D), row(D)],
        out_shape=[_sds((T, D), BF16), _sds((T, D), BF16), _sds((T, 2 * D), BF16), _sds((T, D), BF16),
                   _sds((T, D), BF16)],
        compiler_params=_params(),
    )(dm, w_out, w_o_rnn, w_o_attn, p, p, p, p, ya, yb)


AB = 128
CTX_BLKS = CTX // AB


def _rope_tables(S):
    pos = jnp.arange(S, dtype=jnp.int32)
    inv = ROPE_BASE ** (-jnp.arange(N_FREQ, dtype=F32) / N_FREQ)
    ang_r = (pos // GRID_W).astype(F32)[:, None] * inv[None, :]
    ang_c = (pos % GRID_W).astype(F32)[:, None] * inv[None, :]
    cos = jnp.concatenate([jnp.cos(ang_r)] * 2 + [jnp.cos(ang_c)] * 2, axis=1)
    sin = jnp.concatenate([-jnp.sin(ang_r), jnp.sin(ang_r), -jnp.sin(ang_c), jnp.sin(ang_c)], axis=1)
    return cos, sin


def _rope(x, cos, sin):
    w = x.shape[1]
    reps = w // HEAD
    lane = lax.broadcasted_iota(jnp.int32, x.shape, 1)
    partner = jnp.where((lane & 63) < 32, pltpu.roll(x, w - 32, 1), pltpu.roll(x, 32, 1))
    return x * jnp.tile(cos, (1, reps)) + partner * jnp.tile(sin, (1, reps))


def _unrope(dx, cos, sin):
    w = dx.shape[1]
    reps = w // HEAD
    lane = lax.broadcasted_iota(jnp.int32, dx.shape, 1)
    t = dx * jnp.tile(sin, (1, reps))
    partner = jnp.where((lane & 63) < 32, pltpu.roll(t, w - 32, 1), pltpu.roll(t, 32, 1))
    return dx * jnp.tile(cos, (1, reps)) + partner


def _qkv_prep(name, p, cos, sin, S):
    T = CTX + S
    nt = T // TR
    cb = CTX // TR
    KW = N_KV * HEAD

    def with_ones(v):
        ones = jnp.ones((TR, HEAD), BF16)
        return jnp.concatenate([v[:, kh * HEAD:(kh + 1) * HEAD] if part == 0 else ones
                                for kh in range(N_KV) for part in range(2)], axis=1)

    def kern(q_ref, k_ref, v_ref, cos_ref, sin_ref, qa_ref, kp_ref, vp_ref, kc_ref, vc_ref):
        i = pl.program_id(0)
        cos_v, sin_v = cos_ref[...], sin_ref[...]
        @pl.when(i < cb)
        def _():
            qa_ref[...] = (q_ref[...].astype(F32) * ATT_SCALE).astype(BF16)
            kc_ref[...] = k_ref[...]
            vc_ref[...] = with_ones(v_ref[...])

        @pl.when((i < cb) | (i >= nt))
        def _():
            kp_ref[...] = jnp.zeros(kp_ref.shape, BF16)
            vp_ref[...] = jnp.zeros(vp_ref.shape, BF16)

        @pl.when((i >= cb) & (i < nt))
        def _():
            qa_ref[...] = (_rope(q_ref[...].astype(F32), cos_v, sin_v) * ATT_SCALE).astype(BF16)
            kp_ref[...] = _rope(k_ref[...].astype(F32), cos_v, sin_v).astype(BF16)
            vp_ref[...] = with_ones(v_ref[...])

    tok = lambda i: jnp.minimum(i, nt - 1)
    lat_map = lambda i: (jnp.clip(i - cb, 0, nt - cb - 1), 0)
    ctx_map = lambda i: (jnp.minimum(i, cb - 1), 0)
    return pl.pallas_call(
        kern, name=name, grid=(nt + cb,),
        in_specs=[pl.BlockSpec((TR, N_Q * HEAD), lambda i: (tok(i), COL_Q // (N_Q * HEAD))),
                  pl.BlockSpec((TR, KW), lambda i: (tok(i), COL_K // KW)),
                  pl.BlockSpec((TR, KW), lambda i: (tok(i), COL_V // KW)),
                  pl.BlockSpec((TR, HEAD), lat_map), pl.BlockSpec((TR, HEAD), lat_map)],
        out_specs=[pl.BlockSpec((TR, N_Q * HEAD), lambda i: (tok(i), 0)),
                   pl.BlockSpec((TR, KW), lambda i: (i, 0)), pl.BlockSpec((TR, 2 * KW), lambda i: (i, 0)),
                   pl.BlockSpec((TR, KW), ctx_map), pl.BlockSpec((TR, 2 * KW), ctx_map)],
        out_shape=[_sds((T, N_Q * HEAD), BF16), _sds((S + 2 * CTX, KW), BF16), _sds((S + 2 * CTX, 2 * KW), BF16),
                   _sds((CTX, KW), BF16), _sds((CTX, 2 * KW), BF16)],
        compiler_params=_params(),
    )(p, p, p, cos, sin)


GW = Q_PER_KV * HEAD
HG = Q_PER_KV


def _band_bias(S):
    r = jnp.arange(AB, dtype=jnp.int32)[:, None]
    c = jnp.arange(3 * AB, dtype=jnp.int32)[None, :]
    near = jnp.abs(c - AB - r) <= AB
    valid = jnp.stack([near & (c >= AB), near, near & (c < 2 * AB)])
    return jnp.where(valid, 0.0, NEG_INF).astype(F32)


def _bias_spec(S):
    nb = S // AB
    return pl.BlockSpec((None, AB, 3 * AB), lambda kh, n: (jnp.where(n == 0, 0, jnp.where(n == nb - 1, 2, 1)), 0, 0))


def _head_probs(q, sink, kc, vce, kb, vbe, bias):
    s_c = _dot(q, kc, NT)
    m = jnp.maximum(jnp.max(s_c, axis=-1, keepdims=True), sink)
    if kb is not None:
        s_b = _dot(q, kb, NT) + bias
        m = jnp.maximum(m, jnp.max(s_b, axis=-1, keepdims=True))
    p_c = jnp.exp(s_c - m).astype(BF16)
    acc = _dot(p_c, vce)
    p_b = None
    if kb is not None:
        p_b = jnp.exp(s_b - m).astype(BF16)
        acc = acc + _dot(p_b, vbe)
    return p_c, p_b, m, acc


def _attn_fwd(name, qa, kc, vc, sink4, S, band=None, prev=None, carry=None):
    T = qa.shape[0]
    has_band = band is not None
    nq = S // AB if has_band else CTX_BLKS
    q_off = CTX_BLKS if has_band else 0

    def kern(*refs):
        q_ref, kc_ref, vc_ref, sink_ref = refs[:4]
        rest = refs[4:]
        o_ref = rest[-1]
        n = pl.program_id(1)
        kc_v, vce = kc_ref[...], vc_ref[...]
        kb = vbe = bias = None
        if has_band:
            kp_ref, vp_ref, bias_ref = rest[:3]
            start = pl.multiple_of(n * AB + (CTX - AB), AB)
            kb = kp_ref[pl.ds(start, 3 * AB), :]
            vbe = vp_ref[pl.ds(start, 3 * AB), :]
            bias = bias_ref[...]
        outs = []
        for g in range(Q_PER_KV):
            sink = sink_ref[g:g + 1, 0:1]
            _, _, m, acc = _head_probs(q_ref[:, g * HEAD:(g + 1) * HEAD], sink, kc_v, vce, kb, vbe, bias)
            l = acc[:, HEAD:] + jnp.exp(sink - m)
            outs.append(acc[:, :HEAD] / l)
        o_ref[...] = jnp.concatenate(outs, axis=1).astype(BF16)

    in_specs = [pl.BlockSpec((AB, GW), lambda kh, n: (n + q_off, kh)),
                pl.BlockSpec((CTX, HEAD), lambda kh, n: (0, kh)), pl.BlockSpec((CTX, 2 * HEAD), lambda kh, n: (0, kh)),
                pl.BlockSpec((None, Q_PER_KV, HEAD), lambda kh, n: (kh, 0, 0))]
    args = [qa, kc, vc, sink4]
    if has_band:
        in_specs += [pl.BlockSpec((S + 2 * CTX, HEAD), lambda kh, n: (0, kh)),
                     pl.BlockSpec((S + 2 * CTX, 2 * HEAD), lambda kh, n: (0, kh)), _bias_spec(S)]
        args += list(band)
    alias = {}
    if prev is not None:
        in_specs.append(ANY)
        alias = {len(args): 0}
        args.append(prev)
    ci, ca, co, cs, cscr = _carry_args(carry)
    res = pl.pallas_call(
        _carried(kern, carry, len(args), 1, *_grid_ends((N_KV, nq))), name=name, grid=(N_KV, nq),
        in_specs=in_specs + ci,
        out_specs=[pl.BlockSpec((AB, GW), lambda kh, n: (n + q_off, kh))] + co,
        out_shape=[_sds((T, N_Q * HEAD), BF16)] + cs, input_output_aliases=alias, scratch_shapes=cscr,
        compiler_params=_params(("arbitrary", "arbitrary")),
    )(*args, *ca)
    return res[0] if carry is None else (res[0], res[1:])


def _attn_bwd(name, qa, kc, vc, sink4, o_all, do_all, S, band=None, prev_dq=None, carry=None):
    T = qa.shape[0]
    has_band = band is not None
    nq = S // AB if has_band else CTX_BLKS
    q_off = CTX_BLKS if has_band else 0
    KW = N_KV * HEAD

    def kern(*refs):
        q_ref, kc_ref, vc_ref, sink_ref, o_ref, do_ref = refs[:6]
        rest = refs[6:]
        if has_band:
            kp_ref, vp_ref, bias_ref, cos_ref, sin_ref = rest[:5]
            rest = rest[5:]
        if prev_dq is not None:
            rest = rest[1:]
        dq_ref, dkc_ref, dvc_ref, dsink_ref = rest[:4]
        n = pl.program_id(1)

        @pl.when(n == 0)
        def _():
            dkc_ref[...] = jnp.zeros(dkc_ref.shape, F32)
            dvc_ref[...] = jnp.zeros(dvc_ref.shape, F32)
            dsink_ref[...] = jnp.zeros(dsink_ref.shape, F32)
            if has_band:
                rest[4][...] = jnp.zeros(rest[4].shape, F32)
                rest[5][...] = jnp.zeros(rest[5].shape, F32)

        kc_v, vce = kc_ref[...], vc_ref[...]
        vc_v = vce[:, :HEAD]
        kb = vbe = vb = bias = None
        if has_band:
            start = pl.multiple_of(n * AB + (CTX - AB), AB)
            kb = kp_ref[pl.ds(start, 3 * AB), :]
            vbe = vp_ref[pl.ds(start, 3 * AB), :]
            vb = vbe[:, :HEAD]
            bias = bias_ref[...]
        dq_parts, dsink_parts = [], []
        for g0 in range(0, Q_PER_KV, HG):
            heads = range(g0, g0 + HG)
            stack = lambda ref: jnp.concatenate([ref[:, g * HEAD:(g + 1) * HEAD] for g in heads], axis=0)
            q4, do4 = stack(q_ref), stack(do_ref)
            sink = jnp.concatenate([jnp.broadcast_to(sink_ref[g:g + 1, 0:1], (AB, 1)) for g in heads], axis=0)
            s_c = _dot(q4, kc_v, NT)
            m = jnp.maximum(jnp.max(s_c, axis=-1, keepdims=True), sink)
            if has_band:
                s_b = _dot(q4, kb, NT) + jnp.tile(bias, (HG, 1))
                m = jnp.maximum(m, jnp.max(s_b, axis=-1, keepdims=True))
            p_c = jnp.exp(s_c - m).astype(BF16).astype(F32)
            p_sink = jnp.exp(sink - m)
            l = jnp.sum(p_c, axis=-1, keepdims=True) + p_sink
            if has_band:
                p_b = jnp.exp(s_b - m).astype(BF16).astype(F32)
                l = l + jnp.sum(p_b, axis=-1, keepdims=True)
            inv = 1.0 / l
            delta = jnp.sum(do4.astype(F32) * stack(o_ref).astype(F32), axis=-1, keepdims=True)
            do4b = do4.astype(BF16)
            pn_c = (p_c * inv).astype(BF16)
            ds_c = (p_c * inv * (_dot(do4b, vc_v, NT) - delta)).astype(BF16)
            dq4 = _dot(ds_c, kc_v)
            dkc_ref[...] += _dot(q4, ds_c, TN)
            dvc_ref[...] += _dot(do4b, pn_c, TN)
            if has_band:
                pn_b = (p_b * inv).astype(BF16)
                ds_b = (p_b * inv * (_dot(do4b, vb, NT) - delta)).astype(BF16)
                dq4 = dq4 + _dot(ds_b, kb)
                rest[4][:, pl.ds(start, 3 * AB)] += _dot(q4, ds_b, TN)
                rest[5][:, pl.ds(start, 3 * AB)] += _dot(do4b, pn_b, TN)
            dq4 = dq4 * ATT_SCALE
            dq_parts += [dq4[k * AB:(k + 1) * AB, :] for k in range(HG)]
            ps = p_sink * inv * delta
            dsink_parts += [jnp.broadcast_to(-jnp.sum(ps[k * AB:(k + 1) * AB, :], axis=0, keepdims=True), (1, HEAD))
                            for k in range(HG)]
        dq = jnp.concatenate(dq_parts, axis=1)
        dq_ref[...] = (_unrope(dq, cos_ref[...], sin_ref[...]) if has_band else dq).astype(BF16)
        dsink_ref[...] += jnp.concatenate(dsink_parts, axis=0)

    q_spec = pl.BlockSpec((AB, GW), lambda kh, n: (n + q_off, kh))
    c_spec = pl.BlockSpec((CTX, HEAD), lambda kh, n: (0, kh))
    ce_spec = pl.BlockSpec((CTX, 2 * HEAD), lambda kh, n: (0, kh))
    s_spec = pl.BlockSpec((None, Q_PER_KV, HEAD), lambda kh, n: (kh, 0, 0))
    in_specs = [q_spec, c_spec, ce_spec, s_spec, q_spec, q_spec]
    args = [qa, kc, vc, sink4, o_all, do_all]
    ct_spec = pl.BlockSpec((HEAD, CTX), lambda kh, n: (kh, 0))
    dq_spec = pl.BlockSpec((AB, GW), lambda kh, n: (n + q_off, COL_Q // GW + kh))
    out_specs = [dq_spec, ct_spec, ct_spec, s_spec]
    out_shape = [_sds((T, DP_W), BF16), _sds((KW, CTX), F32), _sds((KW, CTX), F32), _sds((N_KV, Q_PER_KV, HEAD), F32)]
    if has_band:
        p_spec = pl.BlockSpec((S + 2 * CTX, HEAD), lambda kh, n: (0, kh))
        pt_spec = pl.BlockSpec((HEAD, S + 2 * CTX), lambda kh, n: (kh, 0))
        rope_spec = pl.BlockSpec((AB, HEAD), lambda kh, n: (n, 0))
        in_specs += [p_spec, pl.BlockSpec((S + 2 * CTX, 2 * HEAD), lambda kh, n: (0, kh)), _bias_spec(S), rope_spec,
                     rope_spec]
        args += list(band)
        out_specs += [pt_spec, pt_spec]
        out_shape += [_sds((KW, S + 2 * CTX), F32)] * 2
    alias = {}
    if prev_dq is not None:
        in_specs.append(ANY)
        alias = {len(args): 0}
        args.append(prev_dq)
    ci, ca, co, cs, cscr = _carry_args(carry)
    n_out = len(out_specs)
    res = pl.pallas_call(
        _carried(kern, carry, len(args), n_out, *_grid_ends((N_KV, nq))), name=name, grid=(N_KV, nq),
        in_specs=in_specs + ci, out_specs=out_specs + co, out_shape=out_shape + cs, scratch_shapes=cscr,
        input_output_aliases=alias, compiler_params=_params(("arbitrary", "arbitrary")),
    )(*args, *ca)
    return res if carry is None else (res[:n_out], res[n_out:])


def _dkv_assemble(name, dp, dkp, dvp, dkc_l, dvc_l, dkc_c, dvc_c, cos, sin, S):
    T = CTX + S
    KW = N_KV * HEAD

    def kern(dkp_ref, dvp_ref, dkcl_ref, dvcl_ref, dkcc_ref, dvcc_ref, cos_ref, sin_ref, dp_in, out_ref):
        i = pl.program_id(0)

        @pl.when(i == 0)
        def _():
            out_ref[...] = jnp.concatenate([(dkcl_ref[...] + dkcc_ref[...]).T, (dvcl_ref[...] + dvcc_ref[...]).T],
                                           axis=1).astype(BF16)

        @pl.when(i > 0)
        def _():
            out_ref[...] = jnp.concatenate([_unrope(dkp_ref[...].T, cos_ref[...], sin_ref[...]), dvp_ref[...].T],
                                           axis=1).astype(BF16)

    same = lambda i: (0, i)
    lat_map = lambda i: (jnp.maximum(i - 1, 0), 0)
    ctx_map = lambda i: (0, 0)
    return pl.pallas_call(
        kern, name=name, grid=(T // TR,),
        in_specs=[pl.BlockSpec((KW, TR), same), pl.BlockSpec((KW, TR), same),
                  pl.BlockSpec((KW, CTX), ctx_map), pl.BlockSpec((KW, CTX), ctx_map),
                  pl.BlockSpec((KW, CTX), ctx_map), pl.BlockSpec((KW, CTX), ctx_map),
                  pl.BlockSpec((TR, HEAD), lat_map), pl.BlockSpec((TR, HEAD), lat_map), ANY],
        out_specs=pl.BlockSpec((TR, 2 * KW), lambda i: (i, COL_K // (2 * KW))),
        out_shape=_sds((T, DP_W), BF16), input_output_aliases={8: 0}, compiler_params=_params(),
    )(dkp, dvp, dkc_l, dvc_l, dkc_c, dvc_c, cos, sin, dp)


RB = 128
CH = 256
HALO = 8
SUB = 8
GRP = 8


def _vscan(a, b, reverse):
    row = lax.broadcasted_iota(jnp.int32, a.shape, 0)
    A, H = a, b
    for s in (1, 2, 4):
        sh = SUB - s if reverse else s
        m = (row < SUB - s) if reverse else (row >= s)
        As = pltpu.roll(A, sh, 0)
        Hs = pltpu.roll(H, sh, 0)
        H = jnp.where(m, A * Hs + H, H)
        A = jnp.where(m, A * As, A)
    return A, H


def _scan_rows(a_ref, b_ref, r0, nrows, reverse, carry, emit):
    ngrp = nrows // (SUB * GRP)
    row = lax.broadcasted_iota(jnp.int32, (SUB, RB), 0)

    def grp(gi, carry):
        g = (ngrp - 1 - gi) if reverse else gi
        base = r0 + g * (SUB * GRP)
        for v in (range(GRP - 1, -1, -1) if reverse else range(GRP)):
            rs = pl.multiple_of(base + v * SUB, SUB)
            A, H = _vscan(a_ref[pl.ds(rs, SUB), :], b_ref[pl.ds(rs, SUB), :], reverse)
            hf = H + A * carry
            if reverse:
                before = jnp.where(row == SUB - 1, carry, pltpu.roll(hf, SUB - 1, 0))
                carry = hf[0:1, :]
            else:
                before = jnp.where(row == 0, carry, pltpu.roll(hf, 1, 0))
                carry = hf[SUB - 1:SUB, :]
            emit(rs, hf, before)
        return carry

    return lax.fori_loop(0, ngrp, grp, carry)


def _pad_start(ci):
    return pl.multiple_of(ci * CH + HALO * jnp.minimum(ci, 1), HALO)


def _conv_taps(ext, transpose=False):
    n = CH + 2 * HALO
    taps = []
    for k in range(CONV_W):
        off = CONV_LEFT - k if transpose else k - CONV_LEFT
        taps.append(ext[HALO:HALO + CH, :] if off == 0 else pltpu.roll(ext, (-off) % n, 0)[HALO:HALO + CH, :])
    return taps


def _lru_gates(xl, w4, b4, ls):
    pre = _dot(xl.astype(BF16), w4) + b4
    out = []
    for d in range(2):
        r = _sigmoid(pre[:, d * RB:(d + 1) * RB])
        i = _sigmoid(pre[:, (2 + d) * RB:(3 + d) * RB])
        la = LRU_C * r * ls[d:d + 1, :]
        a = jnp.exp(la)
        q = -jnp.tanh(la) * (1.0 + a * a)
        out.append((r, i, a, q))
    return out


def _rnn_specs(T):
    col = lambda n, *_: (0, n)
    return dict(
        xr=pl.BlockSpec((T, RB), lambda n, *_: (0, COL_XR // RB + n)),
        gr=pl.BlockSpec((T, RB), lambda n, *_: (0, COL_GR // RB + n)),
        act=pl.BlockSpec((T, RB), col),
        cw=pl.BlockSpec((CONV_W, RB), col), cb=pl.BlockSpec((1, RB), col),
        w4=pl.BlockSpec((None, RB, 4 * RB), lambda n, *_: (n, 0, 0)),
        b4=pl.BlockSpec((None, 1, 4 * RB), lambda n, *_: (n, 0, 0)),
        lam=pl.BlockSpec((2, RB), col))


PAD_ROWS = 3 * HALO


def _zero_pads(pad_ref, T):
    for r in (0, HALO + CTX, 2 * HALO + T):
        pad_ref[r:r + HALO, :] = jnp.zeros((HALO, RB), F32)


def _fill_padded(pad_ref, src_ref, T):
    _zero_pads(pad_ref, T)
    pad_ref[HALO:HALO + CTX, :] = src_ref[0:CTX, :].astype(F32)
    pad_ref[2 * HALO + CTX:2 * HALO + T, :] = src_ref[CTX:T, :].astype(F32)


def _pad_rows(ci):
    return pl.ds(pl.multiple_of(ci * CH + HALO + HALO * jnp.minimum(ci, 1), HALO), CH)


def _rnn_fwd(name, p, cw, cb, w4, b4, lam, T, carry=None):
    def kern(xr_ref, gr_ref, cw_ref, cb_ref, w4_ref, b4_ref, lam_ref,
             u_ref, a0, a1, yo_ref, hpf_ref, hpb_ref, r0_ref, r1_ref, i0_ref, i1_ref, xpad, b0, b1, y):
        _fill_padded(xpad, xr_ref, T)
        ls = _log_sigmoid(lam_ref[...])
        w4v, b4v, cwv, cbv = w4_ref[...], b4_ref[...], cw_ref[...], cb_ref[...]

        def chunk(ci, _):
            rows = pl.ds(pl.multiple_of(ci * CH, CH), CH)
            taps = _conv_taps(xpad[pl.ds(_pad_start(ci), CH + 2 * HALO), :])
            xl = cbv + sum(taps[k] * cwv[k:k + 1, :] for k in range(CONV_W))
            for d, (r, i, a, q) in enumerate(_lru_gates(xl, w4v, b4v, ls)):
                (a0, a1)[d][rows, :] = a
                (b0, b1)[d][rows, :] = jnp.sqrt(q) * (i * xl)
                (r0_ref, r1_ref)[d][rows, :] = r.astype(BF16)
                (i0_ref, i1_ref)[d][rows, :] = i.astype(BF16)
            return 0

        lax.fori_loop(0, T // CH, chunk, 0)
        zero = jnp.zeros((1, RB), F32)

        def emit_f(rs, hf, before):
            y[pl.ds(rs, SUB), :] = hf
            b0[pl.ds(rs, SUB), :] = before

        def emit_b(rs, hf, before):
            y[pl.ds(rs, SUB), :] += hf
            b1[pl.ds(rs, SUB), :] = before

        _scan_rows(a0, b0, 0, T, False, zero, emit_f)
        c = _scan_rows(a1, b1, 0, CTX, True, zero, emit_b)
        _scan_rows(a1, b1, CTX, T - CTX, True, c, emit_b)

        def finish(ci, _):
            rows = pl.ds(pl.multiple_of(ci * CH, CH), CH)
            yv = y[rows, :]
            u_ref[rows, :] = (yv * _gelu(gr_ref[rows, :].astype(F32))).astype(BF16)
            yo_ref[rows, :] = yv.astype(BF16)
            hpf_ref[rows, :] = b0[rows, :].astype(BF16)
            hpb_ref[rows, :] = b1[rows, :].astype(BF16)
            return 0

        lax.fori_loop(0, T // CH, finish, 0)

    sp = _rnn_specs(T)
    ci, ca, co, cs, cscr = _carry_args(carry)
    dts = [BF16, F32, F32] + [BF16] * 7
    res = pl.pallas_call(
        _carried(kern, carry, 7, 10, *_grid_ends((N_RNN_BLOCKS,))), name=name, grid=(N_RNN_BLOCKS,),
        in_specs=[sp["xr"], sp["gr"], sp["cw"], sp["cb"], sp["w4"], sp["b4"], sp["lam"]] + ci,
        out_specs=[sp["act"]] * 10 + co,
        out_shape=[_sds((T, D), dt) for dt in dts] + cs,
        scratch_shapes=[pltpu.VMEM((T + PAD_ROWS, RB), F32)] + [pltpu.VMEM((T, RB), F32)] * 3 + cscr,
        compiler_params=_params(),
    )(p, p, cw, cb, w4, b4, lam, *ca)
    return res if carry is None else (res[:10], res[10:])


def _rnn_bwd(name, p, du, saved, dp, cw, cb, w4, b4, lam, T, carry=None):
    def kern(xr_ref, gr_ref, du_ref, a0, a1, y_ref, hpf_ref, hpb_ref, r0_ref, r1_ref, i0_ref, i1_ref,
             cw_ref, cb_ref, w4_ref, b4_ref, lam_ref, dp_in,
             dp_ref, dcw_ref, dcb_ref, dw4_ref, db4_ref, dlam_ref,
             xpad, dxpad, c0, c1, dy):
        j = pl.program_id(1)

        @pl.when(j == 0)
        def _():
            scans(gr_ref, du_ref, a0, a1, y_ref, dp_ref, c0, c1, dy)

        @pl.when(j == 1)
        def _():
            gates(xr_ref, a0, a1, (hpf_ref, hpb_ref), (r0_ref, r1_ref), (i0_ref, i1_ref), cw_ref, cb_ref, w4_ref,
                  lam_ref, dp_ref, dcw_ref, dcb_ref, dw4_ref, db4_ref, dlam_ref, xpad, dxpad, c0, c1)

    def scans(gr_ref, du_ref, a0, a1, y_ref, dgr_ref, c0, c1, dy):
        def phase_a(ci, _):
            rows = pl.ds(pl.multiple_of(ci * CH, CH), CH)
            gr = gr_ref[rows, :].astype(F32)
            duv = du_ref[rows, :].astype(F32)
            dyv = duv * _gelu(gr)
            dgr_ref[rows, :] = (duv * y_ref[rows, :].astype(F32) * _gelu_grad(gr)).astype(BF16)
            dy[rows, :] = dyv
            c0[rows, :] = a0[rows, :] * dyv
            c1[rows, :] = a1[rows, :] * dyv
            return 0

        lax.fori_loop(0, T // CH, phase_a, 0)
        zero = jnp.zeros((1, RB), F32)

        def emit0(rs, hf, before):
            c0[pl.ds(rs, SUB), :] = dy[pl.ds(rs, SUB), :] + before

        def emit1(rs, hf, before):
            c1[pl.ds(rs, SUB), :] = dy[pl.ds(rs, SUB), :] + before

        _scan_rows(a0, c0, 0, T, True, zero, emit0)
        c = _scan_rows(a1, c1, CTX, T - CTX, False, zero, emit1)
        _scan_rows(a1, c1, 0, CTX, False, c, emit1)

    def gates(xr_ref, a0, a1, hp_refs, r_refs, i_refs, cw_ref, cb_ref, w4_ref, lam_ref,
              dxr_ref, dcw_ref, dcb_ref, dw4_ref, db4_ref, dlam_ref, xpad, dxpad, c0, c1):
        _fill_padded(xpad, xr_ref, T)
        _zero_pads(dxpad, T)
        lam_v = lam_ref[...]
        ls = _log_sigmoid(lam_v)
        w4v, cwv, cbv = w4_ref[...], cw_ref[...], cb_ref[...]

        def conv_chunk(ci):
            taps = _conv_taps(xpad[pl.ds(_pad_start(ci), CH + 2 * HALO), :])
            return taps, cbv + sum(taps[k] * cwv[k:k + 1, :] for k in range(CONV_W))

        dw4_ref[...] = jnp.zeros(dw4_ref.shape, F32)
        db4_ref[...] = jnp.zeros(db4_ref.shape, F32)
        dlam_ref[...] = jnp.zeros(dlam_ref.shape, F32)
        dcw_ref[...] = jnp.zeros(dcw_ref.shape, F32)
        dcb_ref[...] = jnp.zeros(dcb_ref.shape, F32)

        def phase_c(ci, _):
            base = pl.multiple_of(ci * CH, CH)
            rows = pl.ds(base, CH)
            _, xl = conv_chunk(ci)
            dxl = jnp.zeros((CH, RB), F32)
            dpre_a, dpre_x, dls = [], [], []
            for d in range(2):
                a = (a0, a1)[d][rows, :]
                r = r_refs[d][rows, :].astype(F32)
                i = i_refs[d][rows, :].astype(F32)
                q = -jnp.tanh(LRU_C * r * ls[d:d + 1, :]) * (1.0 + a * a)
                g = (c0, c1)[d][rows, :]
                hp = hp_refs[d][rows, :].astype(F32)
                gm = g * jnp.sqrt(q)
                di = gm * xl
                dxl = dxl + gm * i
                dla = a * (g * hp - a * (g * (i * xl)) * lax.rsqrt(q))
                dr = dla * (LRU_C * ls[d:d + 1, :])
                dls.append(_colsum(dla * (LRU_C * r)))
                dpre_a.append(dr * r * (1.0 - r))
                dpre_x.append(di * i * (1.0 - i))
            dpre = jnp.concatenate(dpre_a + dpre_x, axis=1)
            dpre_b = dpre.astype(BF16)
            dxl = dxl + _dot(dpre_b, w4v, NT)
            dw4_ref[...] += _dot(xl.astype(BF16), dpre_b, TN)
            db4_ref[...] += _colsum(dpre)
            dlam_ref[...] += jnp.concatenate(dls, axis=0)
            dcb_ref[...] += _colsum(dxl)
            dxpad[_pad_rows(ci), :] = dxl
            return 0

        lax.fori_loop(0, T // CH, phase_c, 0)
        dlam_ref[...] = dlam_ref[...] * _sigmoid(-lam_v)

        def phase_d(ci, _):
            base = pl.multiple_of(ci * CH, CH)
            rows = pl.ds(base, CH)
            xtaps, _ = conv_chunk(ci)
            dtaps = _conv_taps(dxpad[pl.ds(_pad_start(ci), CH + 2 * HALO), :], transpose=True)
            dxl = dxpad[_pad_rows(ci), :]
            dxr_ref[rows, :] = sum(dtaps[k] * cwv[k:k + 1, :] for k in range(CONV_W)).astype(BF16)
            dcw_ref[...] += jnp.concatenate([_colsum(dxl * xtaps[k]) for k in range(CONV_W)], axis=0)
            return 0

        lax.fori_loop(0, T // CH, phase_d, 0)

    sp = _rnn_specs(T)
    dp_spec = pl.BlockSpec((T, RB), lambda n, j: (0, COL_GR // RB + n - j * (COL_GR - COL_XR) // RB))
    ci, ca, co, cs, cscr = _carry_args(carry)
    n_in = 3 + len(saved) + 5 + 1
    res = pl.pallas_call(
        _carried(kern, carry, n_in, 6, *_grid_ends((N_RNN_BLOCKS, 2))), name=name, grid=(N_RNN_BLOCKS, 2),
        in_specs=[sp["xr"], sp["gr"]] + [sp["act"]] * (1 + len(saved)) + [sp["cw"], sp["cb"], sp["w4"], sp["b4"],
                                                                           sp["lam"], ANY] + ci,
        out_specs=[dp_spec, sp["cw"], sp["cb"], sp["w4"], sp["b4"], sp["lam"]] + co,
        out_shape=[_sds((T, DP_W), BF16), _sds((CONV_W, D), F32), _sds((1, D), F32),
                   _sds((N_RNN_BLOCKS, RB, 4 * RB), F32), _sds((N_RNN_BLOCKS, 1, 4 * RB), F32), _sds((2, D), F32)] + cs,
        scratch_shapes=[pltpu.VMEM((T + PAD_ROWS, RB), F32)] * 2 + [pltpu.VMEM((T, RB), F32)] * 3 + cscr,
        input_output_aliases={n_in - 1: 0},
        compiler_params=_params(("arbitrary", "arbitrary")),
    )(p, p, du, *saved, cw, cb, w4, b4, lam, dp, *ca)
    return res if carry is None else (res[:6], res[6:])


class _Plan:
    def __init__(self, shards, Ws):
        L = len(Ws)
        self.shards, self.Ws = shards, Ws
        self.Gs = [None] * L
        self.slots = [dict() for _ in range(L)]
        self.gate_slots = [None] * L
        self.table = {}
        for l in range(L):
            t = f"l{l}_"
            self.table[t + "rnn_fwd"] = [("gather", l, k) for k in ("wffn_in_t", "wo_rnn", "wo_attn", "wout")]
            if l + 1 < L:
                self.table[t + "attn_lat_fwd"] = [("gather", l + 1, "win_t")]
                self.table[t + "ffn_in"] = [("gather", l, "wffn_out")]
            else:
                self.table[t + "attn_lat_fwd"] = [("gather", l, "wffn_out")]
            self.table[t + "ffn_in_dx"] = [("scatter", l, "wffn_out")]
            self.table[t + "attn_lat_bwd"] = [("scatter", l, "wffn_in_t")]
            self.table[t + "proj_dx"] = [("scatter", l, "win_t_a" if l > 0 else "win_t_b")]
            self.table[t + "rnn_bwd"] = ([("scatter", l, k) for k in ("wout", "wo_attn", "wo_rnn")]
                                         + ([("scatter", l + 1, "win_t_b"), ("gates", l + 1, "w4")] if l + 1 < L else []))
        self.table["l0_proj_dw_a"] = [("gates", 0, "w4")]
        self.table["l0_proj_dw_b"] = [("scatter", 0, "win_t_a")]
        self.table["l0_mix_norm"] = [("gather", 0, "win_t")]

    def carry(self, name):
        jobs = []
        for kind, l, k in self.table.get(name, []):
            if kind == "gather":
                jobs.append(("gather", self.shards[l][k]))
            elif kind == "scatter":
                jobs.append(("scatter", self.Gs[l][k].reshape(N_DEV, -1, self.Gs[l][k].shape[-1])))
            else:
                jobs.append(("gather", self.Gs[l]["w4"].reshape(N_RNN_BLOCKS * RB, 4 * RB).astype(BF16)))
        return _Carry(jobs) if jobs else None

    def done(self, name, got):
        for (kind, l, k), res in zip(self.table[name], got):
            if kind == "gather":
                self.Ws[l][k] = res.reshape(-1, D)
            elif kind == "scatter":
                self.slots[l][k] = res
            else:
                self.gate_slots[l] = res


def _run(X, fn, name, *args, **kw):
    carry = None if X is None else X.carry(name)
    if carry is None:
        return fn(name, *args, **kw)
    out, got = fn(name, *args, carry=carry, **kw)
    X.done(name, got)
    return out


def _layer_fwd(l, xa, h, W, rope, S, nxt, X=None):
    T = xa.shape[0]
    tag = f"l{l}_"
    cos, sin, bias = rope
    p = _run(X, _mm_act, tag + "proj", h, W["win_t"], "NT", BF16)
    u, *rnn_saved = _run(X, _rnn_fwd, tag + "rnn_fwd", p, W["cw"], W["cb"], W["w4"], W["b4"], W["lam"], T)
    qa, kp, vp, kc, vc = _qkv_prep(tag + "qkv_prep", p, cos, sin, S)
    o_all = _attn_fwd(tag + "attn_ctx_fwd", qa, kc, vc, W["sink4"], S)
    o_all = _run(X, _attn_fwd, tag + "attn_lat_fwd", qa, kc, vc, W["sink4"], S, band=(kp, vp, bias), prev=o_all)
    ya, yb, z, m, x1, h2 = _out_fused(tag + "out", p, u, o_all, xa, W["wo_rnn"], W["wo_attn"], W["wout"],
                                      W["g_mix_post"], W["mod"], W["g_ffn_pre"])
    fg, fu, s = _run(X, _ffn_in_fused, tag + "ffn_in", h2, W["wffn_in_t"])
    e, *out = _ffn_out_fused(tag + "ffn_out", s, W["wffn_out"], x1, W["g_ffn_post"], W["mod"], nxt)
    saved = dict(xa=xa, h=h, p=p, u=u, rnn=rnn_saved, qa=qa, kp=kp, vp=vp, kc=kc, vc=vc, o_all=o_all,
                 ya=ya, yb=yb, z=z, m=m, x1=x1, h2=h2, fg=fg, fu=fu, s=s, e=e)
    return saved, out


def _layer_bwd(l, dx2, A, W, rope, S, X=None, loss_of=None):
    T = A["xa"].shape[0]
    tag = f"l{l}_"
    cos, sin, bias = rope
    G = {}
    if X is not None:
        X.Gs[l] = G
    if loss_of is None:
        de, df, dga2, G["g_ffn_post"] = _ffn_bwd_fused(tag + "ffn_bwd", A["fg"], A["fu"], W["wffn_out"],
                                                       head=(dx2, A["e"], W["g_ffn_post"], W["mod"]))
    else:
        dx2, de, dga2, G["g_ffn_post"], G["sq"] = _loss_resid_bwd(tag + "loss_ffn_resid_bwd", *loss_of, A["e"],
                                                                  W["g_ffn_post"], W["mod"], GA2)
        df, = _ffn_bwd_fused(tag + "ffn_bwd", A["fg"], A["fu"], W["wffn_out"], de=de)
    G["wffn_out"] = _mm_wgrad(tag + "ffn_out_dw", A["s"], de)
    dx1, dm, dsh2, dsc2, G["g_ffn_pre"], dga1, G["g_mix_post"] = _run(
        X, _ffn_in_bwd_fused, tag + "ffn_in_dx", df, W["wffn_in_t"], A["x1"], dx2, A["m"], W["g_ffn_pre"], W["mod"],
        W["g_mix_post"])
    G["wffn_in_t"] = _run(X, _mm_wgrad, tag + "ffn_in_dw", df, A["h2"])
    G["wout"] = _mm_wgrad(tag + "out_dw", A["z"], dm)
    dya, dyb, dgl, du, do = _out_bwd_fused(tag + "out_dx", dm, W["wout"], W["wo_rnn"], W["wo_attn"], A["p"], A["ya"],
                                           A["yb"])
    G["wo_attn"] = _mm_wgrad(tag + "o_attn_dw", A["o_all"], dyb)
    G["wo_rnn"] = _mm_wgrad(tag + "o_rnn_dw", A["u"], dya)
    dp, dkc_c, dvc_c, dsink_c = _attn_bwd(tag + "attn_ctx_bwd", A["qa"], A["kc"], A["vc"], W["sink4"], A["o_all"], do, S)
    dp, dkc_l, dvc_l, dsink_l, dkp, dvp = _run(
        X, _attn_bwd, tag + "attn_lat_bwd", A["qa"], A["kc"], A["vc"], W["sink4"], A["o_all"], do, S,
        band=(A["kp"], A["vp"], bias, cos, sin), prev_dq=dp)
    G["sink4"] = dsink_c + dsink_l
    dp = _dkv_assemble(tag + "dkv", dp, dkp, dvp, dkc_l, dvc_l, dkc_c, dvc_c, cos, sin, S)
    dp, G["cw"], G["cb"], G["w4"], G["b4"], G["lam"] = _run(
        X, _rnn_bwd, tag + "rnn_bwd", A["p"], du, A["rnn"], dp, W["cw"], W["cb"], W["w4"], W["b4"], W["lam"], T)
    proj_dx = (_proj_bwd_fused, tag + "proj_dx", dp, dgl, W["win_t"], A["xa"], dx1, W["g_mix_pre"], W["mod"])
    if X is not None:
        G["win_t_a"] = _run(X, _proj_wgrad, tag + "proj_dw_a", dp, dgl, A["h"][:, :D // 2])
        if l > 0:
            dxa, dsh1, dsc1, G["g_mix_pre"] = _run(X, *proj_dx)
        G["win_t_b"] = _run(X, _proj_wgrad, tag + "proj_dw_b", dp, dgl, A["h"][:, D // 2:])
        if l == 0:
            dxa, dsh1, dsc1, G["g_mix_pre"] = _run(X, *proj_dx)
    else:
        dxa, dsh1, dsc1, G["g_mix_pre"] = _run(X, *proj_dx)
        G["win_t"] = _proj_wgrad(tag + "proj_dw", dp, dgl, A["h"])
    G["mod"] = jnp.concatenate([dsh1, dsc1, dga1, dsh2, dsc2, dga2], axis=1)
    return dxa, G


def _local_step(ctx, x, target, Ws, S, X=None):
    rope = (*_rope_tables(S), _band_bias(S))
    L = len(Ws)
    x, h = _run(X, _normmod_fwd, "l0_mix_norm", ctx, x, Ws[0]["g_mix_pre"], Ws[0]["mod"], SH1, SC1)
    saved = []
    for l in range(L):
        nxt = (Ws[l + 1]["g_mix_pre"], Ws[l + 1]["mod"]) if l + 1 < L else None
        A, out = _layer_fwd(l, x, h, Ws[l], rope, S, nxt, X)
        saved.append(A)
        if l + 1 < L:
            x, h = out
    Gs = [None] * L
    dx = None
    for l in reversed(range(L)):
        dx, Gs[l] = _layer_bwd(l, dx, saved[l], Ws[l], rope, S, X, loss_of=(out[0], target) if l == L - 1 else None)
    return Gs[L - 1]["sq"], dx, Gs


MESH = pl.DeviceIdType.MESH


def _place():
    return lax.axis_index("x"), lax.axis_index("y"), lax.axis_index("c")


def _lin(px, py, pc):
    return 4 * px + 2 * py + pc


def _allgather_small(name, blk):
    m, n = blk.shape

    def body(x_ref, out_ref, send_sems, recv_sems, local_sem):
        x, y, c = _place()
        me, sibling = (x, y, c), (x, y, 1 - c)
        chips = [(1 - x, y), (x, 1 - y), (1 - x, 1 - y)]

        def copy(k, block, to, src=None):
            dst = out_ref.at[_lin(*block)]
            return pltpu.make_async_remote_copy(src_ref=dst if src is None else src, dst_ref=dst,
                                                send_sem=send_sems.at[k], recv_sem=recv_sems.at[k],
                                                device_id=to, device_id_type=MESH)

        mine = pltpu.make_async_copy(x_ref, out_ref.at[_lin(*me)], local_sem)
        mine.start()
        first = [copy(0, me, sibling, src=x_ref)]
        first += [copy(1 + j, me, (*chip, c), src=x_ref) for j, chip in enumerate(chips)]
        for cp in first:
            cp.start()
        passed = [copy(4 + j, (*chip, c), sibling) for j, chip in enumerate(chips)]
        for j, chip in enumerate(chips):
            copy(1 + j, (*chip, c), me).wait_recv()
            passed[j].start()
        copy(0, sibling, me).wait_recv()
        for j, chip in enumerate(chips):
            copy(4 + j, (*chip, 1 - c), me).wait_recv()
        for cp in first + passed:
            cp.wait_send()
        mine.wait()

    return pl.pallas_call(
        body, name=name, out_shape=_sds((N_DEV, m, n), blk.dtype),
        in_specs=[pl.BlockSpec(memory_space=pltpu.VMEM)], out_specs=pl.BlockSpec(memory_space=pltpu.VMEM),
        scratch_shapes=[pltpu.SemaphoreType.DMA((7,)), pltpu.SemaphoreType.DMA((7,)), pltpu.SemaphoreType.DMA],
        compiler_params=pltpu.CompilerParams(vmem_limit_bytes=VMEM_LIMIT),
    )(blk)


MOD_ROWS = 16
MOD_SHARD = 6 * D // N_DEV
HI = lax.Precision.HIGHEST


def _mod_fwd(name, c9, w_mod, b_shard):
    L = w_mod.shape[0]

    def kern(c_ref, w_ref, b_ref, o_ref):
        o_ref[...] = lax.dot_general(_silu(c_ref[...]), w_ref[...], NN, precision=HI,
                                     preferred_element_type=F32) + b_ref[...]

    return pl.pallas_call(
        kern, name=name, grid=(L,),
        in_specs=[_full_spec(c9.shape), pl.BlockSpec((None, D, MOD_SHARD), lambda l: (l, 0, 0)),
                  pl.BlockSpec((None, 1, MOD_SHARD), lambda l: (l, 0, 0))],
        out_specs=pl.BlockSpec((None, MOD_ROWS, MOD_SHARD), lambda l: (l, 0, 0)),
        out_shape=_sds((L, MOD_ROWS, MOD_SHARD), F32), compiler_params=_params(),
    )(c9, w_mod, b_shard)


def _mod_bwd(name, c9, w_mod, dmod_all, dmod_cols):
    L = w_mod.shape[0]

    def rows9(ref, l):
        own = jnp.concatenate([ref[j, 2 * l + 1:2 * l + 2, :] for j in range(N_DEV)], axis=0)
        ctx = ref[0, 2 * l:2 * l + 1, :]
        for j in range(1, N_DEV):
            ctx = ctx + ref[j, 2 * l:2 * l + 1, :]
        return own, ctx

    def kern(c_ref, w_ref, all_ref, cols_ref, gw_ref, gb_ref, gc_ref):
        l = pl.program_id(0)
        for ll in range(L):
            @pl.when(l == ll)
            def _():
                own, ctx = rows9(all_ref, ll)
                gb_ref[...] = _colsum(own) + ctx
                own_s, ctx_s = rows9(cols_ref, ll)
                r16 = jnp.concatenate([own_s, ctx_s, jnp.zeros((MOD_ROWS - N_DEV - 1, MOD_SHARD), F32)], axis=0)
                gw_ref[...] = lax.dot_general(_silu(c_ref[...]), r16, TN, precision=HI, preferred_element_type=F32)
                part = lax.dot_general(r16, w_ref[...], NT, precision=HI,
                                       preferred_element_type=F32)[N_DEV:N_DEV + 1, :]
                if ll == 0:
                    gc_ref[...] = part
                else:
                    gc_ref[...] += part

    return pl.pallas_call(
        kern, name=name, grid=(L,),
        in_specs=[_full_spec(c9.shape), pl.BlockSpec((None, D, MOD_SHARD), lambda l: (l, 0, 0)),
                  _full_spec(dmod_all.shape), _full_spec(dmod_cols.shape)],
        out_specs=[pl.BlockSpec((None, D, MOD_SHARD), lambda l: (l, 0, 0)),
                   pl.BlockSpec((None, 1, 6 * D), lambda l: (l, 0, 0)), _full_spec((1, D))],
        out_shape=[_sds((L, D, MOD_SHARD), F32), _sds((L, 1, 6 * D), F32), _sds((1, D), F32)],
        compiler_params=_params(),
    )(c9, w_mod, dmod_all, dmod_cols)


_BC1 = 1.0 - ADAM_B1 ** ADAM_STEP
_BC2 = 1.0 - ADAM_B2 ** ADAM_STEP


def _adamw_vals(w, g, m, v):
    m = ADAM_B1 * m + (1.0 - ADAM_B1) * g
    v = ADAM_B2 * v + (1.0 - ADAM_B2) * (g * g)
    delta = -ADAM_LR * ((m / _BC1) / (jnp.sqrt(v / _BC2) + ADAM_EPS) + ADAM_WD * w)
    return delta, m, v


def _adamw(name, w, g, m, v, tile):
    R, C = w.shape
    blk = ((tile, C), lambda i: (i, 0))

    def body(i, ins, ps, outs, acc):
        d, mm, vv = _adamw_vals(ins[0][...], ins[1][...], ins[2][...], ins[3][...])
        outs[0][...] = d
        outs[1][...] = mm
        outs[2][...] = vv

    return _ew(name, body, R // tile, [(a, *blk) for a in (w, g, m, v)], [], [(_sds((R, C), F32), *blk)] * 3)


def _sum_slots(ref):
    g = ref[0].astype(F32)
    for j in range(1, N_DEV):
        g = g + ref[j].astype(F32)
    return g


def _adamw_slots(name, slots, shape, tile, wmv=None):
    L, R, C = shape
    n = R // tile
    spec = pl.BlockSpec((None, tile, C), lambda l, i: (l, i, 0))
    pieces = [s if isinstance(s, (list, tuple)) else [s] for s in slots]
    layer_of = [ll for ll, ps in enumerate(pieces) for _ in ps]
    flat = [p for ps in pieces for p in ps]
    wmv = list(wmv or [])

    def slot_spec(ll, cols):
        return pl.BlockSpec((N_DEV, tile, cols),
                            lambda l, i: (0, jnp.where(l == ll, i, jnp.where(l < ll, 0, n - 1)), 0))

    def kern(*refs):
        s_refs = refs[:len(flat)]
        rest = refs[len(flat):]
        l = pl.program_id(0)
        for ll in range(L):
            @pl.when(l == ll)
            def _():
                parts = [_sum_slots(r) for r, lr in zip(s_refs, layer_of) if lr == ll]
                g = parts[0] if len(parts) == 1 else jnp.concatenate(parts, axis=1)
                if wmv:
                    w_ref, m_ref, v_ref, g_ref, d_ref, mo_ref, vo_ref = rest
                    d_ref[...], mo_ref[...], vo_ref[...] = _adamw_vals(w_ref[...], g, m_ref[...], v_ref[...])
                else:
                    g_ref, = rest
                g_ref[...] = g

    n_out = 4 if wmv else 1
    return pl.pallas_call(
        kern, name=name, grid=(L, n),
        in_specs=[slot_spec(ll, p.shape[-1]) for ll, p in zip(layer_of, flat)] + [spec] * len(wmv),
        out_specs=[spec] * n_out, out_shape=[_sds((L, R, C), F32)] * n_out,
        compiler_params=_params(("arbitrary", "arbitrary")),
    )(*flat, *wmv)


def _sum_blocks(name, blocks):
    _, R, C = blocks.shape

    def kern(b_ref, o_ref):
        o_ref[...] = _sum_slots(b_ref)

    return pl.pallas_call(kern, name=name, in_specs=[_full_spec(blocks.shape)], out_specs=_full_spec((R, C)),
                          grid=(1,), out_shape=_sds((R, C), F32), compiler_params=_params())(blocks)


BIG = ("win_t", "wo_rnn", "wo_attn", "wout", "wffn_in_t", "wffn_out")
BIG_SRC = ("w_in", "w_o_rnn", "w_o_attn", "w_out", "w_ffn_in", "w_ffn_out")
BIG_T = (True, False, False, False, True, False)
BIG_TILE = (176, 128, 128, 128, 176, 176)


def _chan_full(g8):
    return jnp.transpose(g8, (1, 0, 2)).reshape(g8.shape[1], D)


def kernel(x, c, ctx, c_ctx, w_mod, b_mod, g_mix_pre, g_mix_post, g_ffn_pre, g_ffn_post, w_in, conv_w, conv_b, lru_wa, lru_ba, lru_wx, lru_bx, lru_lam, attn_sink, w_o_rnn, w_o_attn, w_out, w_ffn_in, w_ffn_out, loss_target, m_c_ctx, m_w_mod, m_b_mod, m_g_mix_pre, m_g_mix_post, m_g_ffn_pre, m_g_ffn_post, m_w_in, m_conv_w, m_conv_b, m_lru_wa, m_lru_ba, m_lru_wx, m_lru_bx, m_lru_lam, m_attn_sink, m_w_o_rnn, m_w_o_attn, m_w_out, m_w_ffn_in, m_w_ffn_out, v_c_ctx, v_w_mod, v_b_mod, v_g_mix_pre, v_g_mix_post, v_g_ffn_pre, v_g_ffn_post, v_w_in, v_conv_w, v_conv_b, v_lru_wa, v_lru_ba, v_lru_wx, v_lru_bx, v_lru_lam, v_attn_sink, v_w_o_rnn, v_w_o_attn, v_w_out, v_w_ffn_in, v_w_ffn_out):
    P = dict(c_ctx=c_ctx, w_mod=w_mod, b_mod=b_mod, g_mix_pre=g_mix_pre, g_mix_post=g_mix_post, g_ffn_pre=g_ffn_pre,
             g_ffn_post=g_ffn_post, w_in=w_in, conv_w=conv_w, conv_b=conv_b, lru_wa=lru_wa, lru_ba=lru_ba,
             lru_wx=lru_wx, lru_bx=lru_bx, lru_lam=lru_lam, attn_sink=attn_sink, w_o_rnn=w_o_rnn, w_o_attn=w_o_attn,
             w_out=w_out, w_ffn_in=w_ffn_in, w_ffn_out=w_ffn_out)
    Mo = dict(c_ctx=m_c_ctx, w_mod=m_w_mod, b_mod=m_b_mod, g_mix_pre=m_g_mix_pre, g_mix_post=m_g_mix_post,
              g_ffn_pre=m_g_ffn_pre, g_ffn_post=m_g_ffn_post, w_in=m_w_in, conv_w=m_conv_w, conv_b=m_conv_b,
              lru_wa=m_lru_wa, lru_ba=m_lru_ba, lru_wx=m_lru_wx, lru_bx=m_lru_bx, lru_lam=m_lru_lam,
              attn_sink=m_attn_sink, w_o_rnn=m_w_o_rnn, w_o_attn=m_w_o_attn, w_out=m_w_out, w_ffn_in=m_w_ffn_in,
              w_ffn_out=m_w_ffn_out)
    Vo = dict(c_ctx=v_c_ctx, w_mod=v_w_mod, b_mod=v_b_mod, g_mix_pre=v_g_mix_pre, g_mix_post=v_g_mix_post,
              g_ffn_pre=v_g_ffn_pre, g_ffn_post=v_g_ffn_post, w_in=v_w_in, conv_w=v_conv_w, conv_b=v_conv_b,
              lru_wa=v_lru_wa, lru_ba=v_lru_ba, lru_wx=v_lru_wx, lru_bx=v_lru_bx, lru_lam=v_lru_lam,
              attn_sink=v_attn_sink, w_o_rnn=v_w_o_rnn, w_o_attn=v_w_o_attn, w_out=v_w_out, w_ffn_in=v_w_ffn_in,
              w_ffn_out=v_w_ffn_out)
    L = w_in.shape[0]
    S = x.shape[1]
    me = _lin(*_place())

    small = jnp.concatenate([c.reshape(8, 128), conv_w.reshape(L * CONV_W, 128), lru_ba.reshape(2 * L, 128),
                             lru_bx.reshape(2 * L, 128), lru_lam.reshape(2 * L, 128), jnp.zeros((4, 128), F32)], axis=0)
    small_all = _allgather_small("ag_small", small)
    c_all = small_all[:, 0:8].reshape(N_DEV, D)
    conv_w_f = _chan_full(small_all[:, 8:16]).reshape(L, CONV_W, D)
    lru_ba_f = _chan_full(small_all[:, 16:20]).reshape(L, 2, D)
    lru_bx_f = _chan_full(small_all[:, 20:24]).reshape(L, 2, D)
    lru_lam_f = _chan_full(small_all[:, 24:28]).reshape(L, 2, D)

    c9 = jnp.concatenate([c_all, c_ctx[None], jnp.zeros((MOD_ROWS - N_DEV - 1, D), F32)], axis=0)
    b_shard = lax.dynamic_slice_in_dim(b_mod, me * MOD_SHARD, MOD_SHARD, axis=1)[:, None, :]
    mod_part = _mod_fwd("mod_fwd", c9, w_mod, b_shard)
    mod_all = _allgather_small("ag_mod", mod_part.reshape(L * MOD_ROWS, MOD_SHARD))
    mod_all = jnp.transpose(mod_all.reshape(N_DEV, L, MOD_ROWS, MOD_SHARD), (1, 2, 0, 3)).reshape(L, MOD_ROWS, 6 * D)
    own_row = lax.dynamic_index_in_dim(mod_all, me, axis=1, keepdims=False)
    modrows = jnp.stack([mod_all[:, N_DEV], own_row], axis=1)

    shards = [{k: (P[src][l].T if tr else P[src][l]).astype(BF16) for k, src, tr in zip(BIG, BIG_SRC, BIG_T)}
              for l in range(L)]
    Ws = []
    for l in range(L):
        W = {}
        W.update(
            cw=conv_w_f[l], cb=conv_b[l][None],
            w4=jnp.concatenate([lru_wa[l, 0], lru_wa[l, 1], lru_wx[l, 0], lru_wx[l, 1]], axis=-1).astype(BF16),
            b4=jnp.concatenate([lru_ba_f[l, 0].reshape(N_RNN_BLOCKS, 1, RB), lru_ba_f[l, 1].reshape(N_RNN_BLOCKS, 1, RB),
                                lru_bx_f[l, 0].reshape(N_RNN_BLOCKS, 1, RB), lru_bx_f[l, 1].reshape(N_RNN_BLOCKS, 1, RB)],
                               axis=-1),
            lam=lru_lam_f[l], sink4=jnp.broadcast_to(attn_sink[l].reshape(N_KV, Q_PER_KV, 1), (N_KV, Q_PER_KV, HEAD)),
            g_mix_pre=g_mix_pre[l][None], g_mix_post=g_mix_post[l][None], g_ffn_pre=g_ffn_pre[l][None],
            g_ffn_post=g_ffn_post[l][None], mod=modrows[l])
        Ws.append(W)

    plan = _Plan(shards, Ws)
    sq, dxa, Gs = _local_step(ctx[0], x[0], loss_target[0], Ws, S, plan)
    loss_part = ((0.5 / D) * jnp.sum(sq)).reshape(1, 1)
    grad_x = dxa[CTX:][None]

    dmod = jnp.concatenate([Gs[l]["mod"] for l in range(L)] + [jnp.zeros((8 - 2 * L, 6 * D), F32)], axis=0)
    dmod_all = _allgather_small("ag_dmod", dmod)
    dmod_cols = lax.dynamic_slice_in_dim(dmod_all, me * MOD_SHARD, MOD_SHARD, axis=2)
    g_w_mod, g_b_mod, dsc_part = _mod_bwd("mod_bwd", c9, w_mod, dmod_all, dmod_cols)
    g_b_mod = g_b_mod[:, 0]

    def rows(name, shape):
        return jnp.concatenate([Gs[l][name].reshape(shape) for l in range(L)], axis=0)

    b4g = [Gs[l]["b4"].reshape(N_RNN_BLOCKS, 4, RB) for l in range(L)]
    sink_row = jnp.concatenate([Gs[l]["sink4"][:, :, 0].reshape(1, N_Q) for l in range(L)]
                               + [loss_part, jnp.zeros((1, D - L * N_Q - 1), F32)], axis=1)
    small_g = jnp.concatenate(
        [rows("g_mix_pre", (1, D)), rows("g_mix_post", (1, D)), rows("g_ffn_pre", (1, D)), rows("g_ffn_post", (1, D)),
         rows("cb", (1, D)), rows("cw", (CONV_W, D))]
        + [b4g[l][:, d].reshape(1, D) for l in range(L) for d in range(2)]
        + [b4g[l][:, 2 + d].reshape(1, D) for l in range(L) for d in range(2)]
        + [rows("lam", (2, D)), sink_row, dsc_part], axis=0)
    n_small = small_g.shape[0]
    small_tot = _sum_blocks("sum_small", _allgather_small("ag_small_grads", small_g))
    o = 0
    G = {}
    for name in ("g_mix_pre", "g_mix_post", "g_ffn_pre", "g_ffn_post", "conv_b"):
        G[name] = small_tot[o:o + L]
        o += L
    G["conv_w"] = small_tot[o:o + L * CONV_W].reshape(L, CONV_W, D)
    o += L * CONV_W
    for name in ("lru_ba", "lru_bx", "lru_lam"):
        G[name] = small_tot[o:o + 2 * L].reshape(L, 2, D)
        o += 2 * L
    G["attn_sink"] = small_tot[o, :L * N_Q].reshape(L, N_Q)
    loss = small_tot[o, L * N_Q]
    sg = jax.nn.sigmoid(c_ctx)
    G["c_ctx"] = small_tot[o + 1] * (sg * (1.0 + c_ctx * (1.0 - sg)))
    G["b_mod"] = g_b_mod
    G["w_mod"] = g_w_mod

    for l in range(L):
        plan.slots[l]["win_t"] = [plan.slots[l]["win_t_a"], plan.slots[l]["win_t_b"]]

    out_g, out_d, out_m, out_v = {}, {}, {}, {}

    def put(name, res, shape=None):
        g, d, m, v = res
        for dst, val in ((out_g, g), (out_d, d), (out_m, m), (out_v, v)):
            dst[name] = val if shape is None else val.reshape(shape)

    for k, src, tr, tile in zip(BIG, BIG_SRC, BIG_T, BIG_TILE):
        lay = (lambda a: jnp.swapaxes(a, 1, 2)) if tr else (lambda a: a)
        wmv = (lay(P[src]), lay(Mo[src]), lay(Vo[src]))
        res = _adamw_slots("adamw_" + src, [plan.slots[l][k] for l in range(L)], wmv[0].shape, tile, wmv)
        put(src, [lay(r) for r in res])
    res = _adamw("adamw_w_mod", w_mod.reshape(L * D, MOD_SHARD), g_w_mod.reshape(L * D, MOD_SHARD),
                 m_w_mod.reshape(L * D, MOD_SHARD), v_w_mod.reshape(L * D, MOD_SHARD), 256)
    put("w_mod", (g_w_mod,) + tuple(res), w_mod.shape)
    def fuse4(wa, wx):
        return jnp.concatenate([wa[:, 0], wa[:, 1], wx[:, 0], wx[:, 1]], axis=-1).reshape(L, N_RNN_BLOCKS * RB, 4 * RB)

    res = _adamw_slots("adamw_gates", plan.gate_slots, (L, N_RNN_BLOCKS * RB, 4 * RB), 256,
                       (fuse4(lru_wa, lru_wx), fuse4(m_lru_wa, m_lru_wx), fuse4(v_lru_wa, v_lru_wx)))
    res = [r.reshape(L, N_RNN_BLOCKS, RB, 4, RB) for r in res]
    put("lru_wa", [jnp.stack([r[:, :, :, 0], r[:, :, :, 1]], axis=1) for r in res])
    put("lru_wx", [jnp.stack([r[:, :, :, 2], r[:, :, :, 3]], axis=1) for r in res])
    rep = ("g_mix_pre", "g_mix_post", "g_ffn_pre", "g_ffn_post", "conv_b", "b_mod")

    def pack_rep(T_):
        sink = jnp.concatenate([T_["attn_sink"].reshape(1, L * N_Q), jnp.zeros((1, D - L * N_Q), F32)], axis=1)
        return jnp.concatenate([T_[n].reshape(-1, D) for n in rep] + [sink, T_["c_ctx"][None]], axis=0)

    pk = [pack_rep(T_) for T_ in (P, G, Mo, Vo)]
    n_rep = pk[0].shape[0]
    res = _adamw("adamw_replicated", *[jnp.pad(a, ((0, 24 - n_rep), (0, 0))) for a in pk], 24)
    res = (pk[1],) + tuple(r[:n_rep] for r in res)
    o = 0
    for n in rep:
        k = P[n].size // D
        put(n, [r[o:o + k] for r in res], P[n].shape)
        o += k
    put("attn_sink", [r[o, :L * N_Q] for r in res], attn_sink.shape)
    put("c_ctx", [r[o + 1] for r in res], c_ctx.shape)
    chan = ("conv_w", "lru_ba", "lru_bx", "lru_lam")
    g_own = {n: lax.dynamic_slice_in_dim(G[n], me * RB, RB, axis=2) for n in chan}

    def pack_chan(T_):
        return jnp.concatenate([T_[n].reshape(-1, RB) for n in chan], axis=0)

    pk = [pack_chan(T_) for T_ in (P, g_own, Mo, Vo)]
    n_ch = pk[0].shape[0]
    res = _adamw("adamw_channels", *[jnp.pad(a, ((0, 24 - n_ch), (0, 0))) for a in pk], 24)
    res = (pk[1],) + tuple(r[:n_ch] for r in res)
    o = 0
    for n in chan:
        k = P[n].size // RB
        put(n, [r[o:o + k] for r in res], P[n].shape)
        o += k

    order = ("c_ctx", "w_mod", "b_mod", "g_mix_pre", "g_mix_post", "g_ffn_pre", "g_ffn_post", "w_in", "conv_w", "conv_b",
             "lru_wa", "lru_ba", "lru_wx", "lru_bx", "lru_lam", "attn_sink", "w_o_rnn", "w_o_attn", "w_out", "w_ffn_in",
             "w_ffn_out")
    return (loss, grad_x, *[out_g[n] for n in order], *[out_d[n] for n in order], *[out_m[n] for n in order],
            *[out_v[n] for n in order])
```

```python
import functools
import math

import numpy as np
import jax
import jax.numpy as jnp
from jax import lax
from jax.experimental import pallas as pl
from jax.experimental.pallas import tpu as pltpu

F32 = jnp.float32
BF16 = jnp.bfloat16

D = 1024
CTX = 256
TR = 256
HEAD = 128
N_Q = 8
N_KV = 2
Q_PER_KV = N_Q // N_KV
GRID_W = 64
N_FREQ = HEAD // 4
ROPE_BASE = 10000.0
N_RNN_BLOCKS = 8
CONV_W = 4
CONV_LEFT = 2
LRU_C = 8.0
D_FF = 2816
IN_W = 5632
P_W = IN_W
DP_W = 3584
COL_XR, COL_GR, COL_Q, COL_K, COL_V, COL_GL = 0, 1024, 2048, 3072, 3328, 3584
GLB = 512
EPS = 1e-6
NEG_INF = -1e30
ATT_SCALE = HEAD ** -0.5
N_DEV = 8
VMEM_LIMIT = 56 * 1024 * 1024

ADAM_LR, ADAM_B1, ADAM_B2, ADAM_EPS, ADAM_WD, ADAM_STEP = 0.001, 0.9, 0.999, 1e-08, 0.01, 10

NN = (((1,), (0,)), ((), ()))
NT = (((1,), (1,)), ((), ()))
TN = (((0,), (0,)), ((), ()))


def _dot(a, b, dims=NN):
    return lax.dot_general(a, b, dims, preferred_element_type=F32)


def _params(sem=("arbitrary",)):
    return pltpu.CompilerParams(dimension_semantics=sem, vmem_limit_bytes=VMEM_LIMIT)


def _full_spec(shape):
    nd = len(shape)
    return pl.BlockSpec(shape, lambda *_: (0,) * nd)


ANY = pl.BlockSpec(memory_space=pl.ANY)


def _ew(name, body, n, row_ins, pars, row_outs, accs=(), alias=None):
    n_ri, n_p, n_ro, n_acc = len(row_ins), len(pars), len(row_outs), len(accs)

    def kern(*refs):
        i = pl.program_id(0)
        ins = refs[:n_ri]
        ps = refs[n_ri:n_ri + n_p]
        outs = refs[n_ri + n_p:n_ri + n_p + n_ro]
        acc = refs[n_ri + n_p + n_ro:]
        if n_acc:
            @pl.when(i == 0)
            def _():
                for a in acc:
                    a[...] = jnp.zeros(a.shape, a.dtype)
        body(i, ins, ps, outs, acc)

    in_specs = [ANY if blk is None else pl.BlockSpec(blk, imap) for (_, blk, imap) in row_ins]
    in_specs += [_full_spec(p.shape) for p in pars]
    out_specs = [pl.BlockSpec(blk, imap) for (_, blk, imap) in row_outs] + [_full_spec(a.shape) for a in accs]
    out_shape = [s for (s, _, _) in row_outs] + list(accs)
    return pl.pallas_call(
        kern, name=name, grid=(n,), in_specs=in_specs, out_specs=out_specs, out_shape=out_shape,
        input_output_aliases=alias or {}, compiler_params=_params(),
    )(*[a for (a, _, _) in row_ins], *pars)


def _rowblk(width, colblk=0, roff=0, tile=TR):
    return (tile, width), (lambda i: (i + roff, colblk))


def _sds(shape, dtype):
    return jax.ShapeDtypeStruct(shape, dtype)


class _Carry:
    SAME_CORE = (1, 3, 5)

    def __init__(self, jobs):
        self.jobs = list(jobs)
        self.arrays = [a for _, a in self.jobs]
        self.out_shapes = [_sds(a.shape if kind == "scatter" else (N_DEV, *a.shape), a.dtype) for kind, a in self.jobs]
        n = len(self.jobs)
        self.scratch = [pltpu.SemaphoreType.DMA((n, 7)), pltpu.SemaphoreType.DMA((n, 7)), pltpu.SemaphoreType.DMA((n,))]

    def _setup(self, sems):
        send_sems, recv_sems, local_sems = sems
        x, y, c = _place()
        me = _lin(x, y, c)
        peers = [(x ^ ((k + 1) >> 2 & 1), y ^ ((k + 1) >> 1 & 1), c ^ ((k + 1) & 1)) for k in range(7)]

        def copy(a, k, sem_k, src, dst):
            return pltpu.make_async_remote_copy(src_ref=src, dst_ref=dst, send_sem=send_sems.at[a, sem_k],
                                                recv_sem=recv_sems.at[a, sem_k], device_id=peers[k], device_id_type=MESH)

        return me, [_lin(*p) for p in peers], copy, local_sems

    def _local(self, a, kind, ins, outs, me, local_sems):
        return pltpu.make_async_copy(ins[a].at[me] if kind == "scatter" else ins[a], outs[a].at[me], local_sems.at[a])

    def start(self, ins, outs, sems):
        me, theirs, copy, local_sems = self._setup(sems)
        for a, (kind, _) in enumerate(self.jobs):
            self._local(a, kind, ins, outs, me, local_sems).start()
            if kind == "scatter":
                for k in range(7):
                    copy(a, k, k, ins[a].at[theirs[k]], outs[a].at[me]).start()
            else:
                for k in (0,) + self.SAME_CORE:
                    copy(a, k, k, ins[a], outs[a].at[me]).start()

    def wait(self, ins, outs, sems):
        me, theirs, copy, local_sems = self._setup(sems)
        for a, (kind, _) in enumerate(self.jobs):
            if kind == "scatter":
                for k in range(7):
                    copy(a, k, k, ins[a].at[me], outs[a].at[theirs[k]]).wait_recv()
                for k in range(7):
                    copy(a, k, k, ins[a].at[theirs[k]], outs[a].at[me]).wait_send()
            else:
                for k in self.SAME_CORE:
                    blk = outs[a].at[theirs[k]]
                    copy(a, k, k, ins[a], blk).wait_recv()
                    copy(a, 0, k + 1, blk, blk).start()
                copy(a, 0, 0, ins[a], outs[a].at[theirs[0]]).wait_recv()
                for k in self.SAME_CORE:
                    copy(a, 0, k + 1, ins[a], outs[a].at[theirs[k + 1]]).wait_recv()
                for k in (0,) + self.SAME_CORE:
                    copy(a, k, k, ins[a], outs[a].at[me]).wait_send()
                for k in self.SAME_CORE:
                    blk = outs[a].at[theirs[k]]
                    copy(a, 0, k + 1, blk, blk).wait_send()
            self._local(a, kind, ins, outs, me, local_sems).wait()


def _carried(kern, carry, n_in, n_out, first, last):
    if carry is None:
        return kern
    nc = len(carry.jobs)

    def wrapped(*refs):
        ins, cin = refs[:n_in], refs[n_in:n_in + nc]
        outs, cout = refs[n_in + nc:n_in + nc + n_out], refs[n_in + nc + n_out:n_in + 2 * nc + n_out]
        scr, sems = refs[n_in + 2 * nc + n_out:-3], refs[-3:]

        @pl.when(first())
        def _():
            carry.start(cin, cout, sems)

        kern(*ins, *outs, *scr)

        @pl.when(last())
        def _():
            carry.wait(cin, cout, sems)

    return wrapped


def _carry_args(carry):
    if carry is None:
        return [], [], [], [], []
    n = len(carry.jobs)
    return [ANY] * n, carry.arrays, [ANY] * n, carry.out_shapes, carry.scratch


def _grid_ends(dims):
    first = lambda: functools.reduce(jnp.logical_and, [pl.program_id(d) == 0 for d in range(len(dims))])
    last = lambda: functools.reduce(jnp.logical_and, [pl.program_id(d) == n - 1 for d, n in enumerate(dims)])
    return first, last


def _mm_call(name, a, b, mode, out_dtype, tm, tn, rows_outer=True, single_b=False, carry=None):
    if mode == "TN":
        (K, M), N = a.shape, b.shape[1]
    else:
        (M, K), N = a.shape, (b.shape[1] if mode == "NN" else b.shape[0])
    assert M % tm == 0 and N % tn == 0, (name, M, N, K, tm, tn)
    ij = (lambda g0, g1: (g0, g1)) if rows_outer else (lambda g0, g1: (g1, g0))
    grid = (M // tm, N // tn) if rows_outer else (N // tn, M // tm)
    if mode == "TN":
        a_spec = pl.BlockSpec((K, tm), lambda g0, g1: (0, ij(g0, g1)[0]))
    else:
        a_spec = pl.BlockSpec((tm, K), lambda g0, g1: (ij(g0, g1)[0], 0))
    b_blk, b_map = ((tn, K), lambda g0, g1: (ij(g0, g1)[1], 0)) if mode == "NT" else \
                   ((K, tn), lambda g0, g1: (0, ij(g0, g1)[1]))
    b_spec = pl.BlockSpec(b_blk, b_map, pipeline_mode=pl.Buffered(1)) if single_b else pl.BlockSpec(b_blk, b_map)
    dims = {"NN": NN, "NT": NT, "TN": TN}[mode]

    def kern(a_ref, b_ref, o_ref):
        o_ref[...] = _dot(a_ref[...], b_ref[...], dims).astype(o_ref.dtype)

    ci, ca, co, cs, cscr = _carry_args(carry)
    res = pl.pallas_call(
        _carried(kern, carry, 2, 1, *_grid_ends(grid)), name=name, grid=grid, in_specs=[a_spec, b_spec] + ci,
        out_specs=[pl.BlockSpec((tm, tn), lambda g0, g1: ij(g0, g1))] + co,
        out_shape=[_sds((M, N), out_dtype)] + cs, scratch_shapes=cscr,
        compiler_params=_params(("arbitrary", "arbitrary")),
    )(a, b, *ca)
    return res[0] if carry is None else (res[0], res[1:])


def _mm_act(name, a, w, mode, out_dtype=BF16, carry=None):
    rows, K = a.shape
    N = w.shape[1] if mode == "NN" else w.shape[0]
    if K > D_FF:
        return _mm_call(name, a, w, mode, out_dtype, rows // 8, N, single_b=True, carry=carry)
    tn = N if N <= 1024 else 1408
    return _mm_call(name, a, w, mode, out_dtype, rows // 4, tn, carry=carry)


def _mm_wgrad(name, x, dy, out_dtype=BF16, carry=None):
    M = x.shape[1]
    tm = 1408 if M == D_FF else 512
    return _mm_call(name, x, dy, "TN", out_dtype, tm, dy.shape[1], single_b=True, carry=carry)


def _sigmoid(x):
    return 0.5 * jnp.tanh(0.5 * x) + 0.5


def _silu(x):
    return x * _sigmoid(x)


def _silu_grad(x):
    s = _sigmoid(x)
    return s * (1.0 + x * (1.0 - s))


_GELU_K = math.sqrt(2.0 / math.pi)


def _gelu(x):
    return 0.5 * x * (1.0 + jnp.tanh(_GELU_K * (x + 0.044715 * x * x * x)))


def _gelu_grad(x):
    t = jnp.tanh(_GELU_K * (x + 0.044715 * x * x * x))
    return 0.5 * (1.0 + t) + 0.5 * x * (1.0 - t * t) * _GELU_K * (1.0 + 3.0 * 0.044715 * x * x)


def _log_sigmoid(x):
    return jnp.minimum(x, 0.0) - jnp.log(1.0 + jnp.exp(-jnp.abs(x)))


def _rms(x):
    x = x.astype(F32)
    r = lax.rsqrt(jnp.mean(x * x, axis=-1, keepdims=True) + EPS)
    return x * r, r


def _rms_bwd(dy, y, r):
    return r * (dy - y * jnp.mean(dy * y, axis=-1, keepdims=True))


def _modrow(mod_ref, i, chunk):
    lo = mod_ref[0:1, chunk * D:(chunk + 1) * D]
    hi = mod_ref[1:2, chunk * D:(chunk + 1) * D]
    return jnp.where(i == 0, lo, hi)


def _acc_seg(acc_ref, i, val):
    zero = jnp.zeros_like(val)
    acc_ref[0:1, :] += jnp.where(i == 0, val, zero)
    acc_ref[1:2, :] += jnp.where(i == 0, zero, val)


def _colsum(x):
    return jnp.sum(x, axis=0, keepdims=True)


SH1, SC1, GA1, SH2, SC2, GA2 = range(6)


def _normmod_fwd(name, ctx, x, g, mod, c_sh, c_sc, carry=None):
    T = ctx.shape[0] + x.shape[0]
    assert ctx.shape[0] == TR
    n = T // TR

    def kern(ctx_ref, x_ref, g_ref, mod_ref, xa_ref, h_ref):
        i = pl.program_id(0)
        v = jnp.where(i == 0, ctx_ref[...], x_ref[...])
        xa_ref[...] = v
        y, _ = _rms(v)
        h = (y * g_ref[...]) * (1.0 + _modrow(mod_ref, i, c_sc)) + _modrow(mod_ref, i, c_sh)
        h_ref[...] = h.astype(BF16)

    row = pl.BlockSpec((TR, D), lambda i: (i, 0))
    ci, ca, co, cs, cscr = _carry_args(carry)
    res = pl.pallas_call(
        _carried(kern, carry, 4, 2, *_grid_ends((n,))), name=name, grid=(n,),
        in_specs=[pl.BlockSpec((TR, D), lambda i: (0, 0)), pl.BlockSpec((TR, D), lambda i: (jnp.maximum(i - 1, 0), 0)),
                  _full_spec(g.shape), _full_spec(mod.shape)] + ci,
        out_specs=[row, row] + co, out_shape=[_sds((T, D), F32), _sds((T, D), BF16)] + cs, scratch_shapes=cscr,
        compiler_params=_params(),
    )(ctx, x, g, mod, *ca)
    return res if carry is None else (res[:2], res[2:])


def _modrows(mod_ref, row0, n, chunk):
    t = row0 + lax.broadcasted_iota(jnp.int32, (n, 1), 0)
    return jnp.where(t < CTX, mod_ref[0:1, chunk * D:(chunk + 1) * D], mod_ref[1:2, chunk * D:(chunk + 1) * D])


def _loss_resid_bwd(name, x_out, target, mat, gpost, mod, c_ga):
    T = x_out.shape[0]

    def body(i, ins, ps, outs, acc):
        err = ins[0][...] - ins[1][...]
        lat = i > 0
        dx = jnp.where(lat, err * (1.0 / D), 0.0)
        outs[0][...] = dx
        acc[2][...] += jnp.where(lat, _colsum(err * err), 0.0)
        outs[1][...] = _resid_bwd_vals(i, dx, ins[2][...], ps[0][...], ps[1], c_ga, acc[0], acc[1]).astype(BF16)

    tgt_blk = ((TR, D), lambda i: (jnp.maximum(i - 1, 0), 0))
    return _ew(name, body, T // TR, [(x_out, *_rowblk(D)), (target, *tgt_blk), (mat, *_rowblk(D))], [gpost, mod],
               [(_sds((T, D), F32), *_rowblk(D)), (_sds((T, D), BF16), *_rowblk(D))],
               [_sds((2, D), F32), _sds((1, D), F32), _sds((1, D), F32)])


def _mod_for(mod_ref, i, chunk, row0, n):
    return _modrow(mod_ref, i, chunk) if row0 is None else _modrows(mod_ref, row0, n, chunk)


def _acc_for(acc_ref, i, v, row0):
    if row0 is None:
        _acc_seg(acc_ref, i, _colsum(v))
        return

    @pl.when(row0 < CTX)
    def _():
        is_ctx = row0 + lax.broadcasted_iota(jnp.int32, (v.shape[0], 1), 0) < CTX
        acc_ref[0:1, :] += _colsum(jnp.where(is_ctx, v, 0.0))
        acc_ref[1:2, :] += _colsum(jnp.where(is_ctx, 0.0, v))

    @pl.when(row0 >= CTX)
    def _():
        acc_ref[1:2, :] += _colsum(v)


def _resid_bwd_vals(i, dout, mat, gpost, mod_ref, c_ga, acc_ga, acc_g, row0=None):
    ym, rm = _rms(mat)
    ga = _mod_for(mod_ref, i, c_ga, row0, dout.shape[0])
    _acc_for(acc_ga, i, dout * (ym * gpost), row0)
    dn = dout * ga
    acc_g[...] += _colsum(dn * ym)
    return _rms_bwd(dn * gpost, ym, rm)


def _normmod_bwd_vals(i, dh, xin, g, mod_ref, c_sh, c_sc, acc_sh, acc_sc, acc_g, row0=None):
    dh = dh.astype(F32)
    y, r = _rms(xin)
    _acc_for(acc_sc, i, dh * (y * g), row0)
    _acc_for(acc_sh, i, dh, row0)
    dyg = dh * (1.0 + _mod_for(mod_ref, i, c_sc, row0, dh.shape[0]))
    acc_g[...] += _colsum(dyg * y)
    return _rms_bwd(dyg * g, y, r)


def _parts(i, tm):
    return [(slice(0, tm), i * tm)]


FT = 1408


def _ffn_in_fused(name, h2, w_t, carry=None):
    T = h2.shape[0]
    tm, nj = T // 4, D_FF // FT

    def kern(a_ref, bg_ref, bu_ref, fg_ref, fu_ref, s_ref):
        for rows, _ in _parts(0, tm):
            a = a_ref[rows, :]
            g = _dot(a, bg_ref[...], NT)
            u = _dot(a, bu_ref[...], NT)
            fg_ref[rows, :] = g.astype(BF16)
            fu_ref[rows, :] = u.astype(BF16)
            s_ref[rows, :] = (_silu(g) * u).astype(BF16)

    o_spec = pl.BlockSpec((tm, FT), lambda i, j: (i, j))
    ci, ca, co, cs, cscr = _carry_args(carry)
    res = pl.pallas_call(
        _carried(kern, carry, 3, 3, *_grid_ends((4, nj))), name=name, grid=(4, nj),
        in_specs=[pl.BlockSpec((tm, D), lambda i, j: (i, 0)), pl.BlockSpec((FT, D), lambda i, j: (j, 0)),
                  pl.BlockSpec((FT, D), lambda i, j: (j + nj, 0))] + ci,
        out_specs=[o_spec] * 3 + co, out_shape=[_sds((T, D_FF), BF16)] * 3 + cs, scratch_shapes=cscr,
        compiler_params=_params(("arbitrary", "arbitrary")),
    )(h2, w_t, w_t, *ca)
    return res if carry is None else (res[:3], res[3:])


def _norm_chain(row0, xin, mat, gpost, mod_ref, c_ga, gnext, modn_ref, c_sh, c_sc):
    n = xin.shape[0]
    ym, _ = _rms(mat.astype(BF16))
    xo = xin + _modrows(mod_ref, row0, n, c_ga) * (ym * gpost)
    y, _ = _rms(xo)
    h = (y * gnext) * (1.0 + _modrows(modn_ref, row0, n, c_sc)) + _modrows(modn_ref, row0, n, c_sh)
    return xo, h.astype(BF16)


def _out_fused(name, p, u, o_all, xa, w_o_rnn, w_o_attn, w_out, gpost, mod, gnext):
    T = u.shape[0]
    tm = T // 8

    def kern(g0, g1, g2, g3, u_ref, o_ref, xa_ref, wr_ref, wa_ref, w_ref, gpost_ref, mod_ref, gnext_ref,
             ya_ref, yb_ref, z_ref, m_ref, x1_ref, h2_ref):
        for rows, row0 in _parts(pl.program_id(0), tm):
            ya = _dot(u_ref[rows, :], wr_ref[...]).astype(BF16)
            yb = _dot(o_ref[rows, :], wa_ref[...]).astype(BF16)
            ya_ref[rows, :] = ya
            yb_ref[rows, :] = yb
            ga = _sigmoid(jnp.concatenate([g0[rows, :], g1[rows, :]], axis=1).astype(F32))
            gb = _sigmoid(jnp.concatenate([g2[rows, :], g3[rows, :]], axis=1).astype(F32))
            z = (ga * ya.astype(F32) + gb * yb.astype(F32)).astype(BF16)
            z_ref[rows, :] = z
            m = _dot(z, w_ref[...])
            m_ref[rows, :] = m.astype(BF16)
            x1_ref[rows, :], h2_ref[rows, :] = _norm_chain(row0, xa_ref[rows, :], m, gpost_ref[...], mod_ref, GA1,
                                                           gnext_ref[...], mod_ref, SH2, SC2)

    row = lambda w: pl.BlockSpec((tm, w), lambda i: (i, 0))
    return pl.pallas_call(
        kern, name=name, grid=(T // tm,),
        in_specs=[pl.BlockSpec((tm, GLB), lambda i, q=q: (i, COL_GL // GLB + q)) for q in range(4)]
                 + [row(D), row(D), row(D)] + [_full_spec(a.shape) for a in (w_o_rnn, w_o_attn, w_out, gpost, mod, gnext)],
        out_specs=[row(D)] * 6,
        out_shape=[_sds((T, D), BF16)] * 4 + [_sds((T, D), F32), _sds((T, D), BF16)],
        compiler_params=_params(),
    )(p, p, p, p, u, o_all, xa, w_o_rnn, w_o_attn, w_out, gpost, mod, gnext)


def _ffn_out_fused(name, s, w, x1, gpost, mod, nxt=None):
    T = s.shape[0]
    tm = T // 8

    def kern(s_ref, w_ref, x1_ref, gpost_ref, mod_ref, *rest):
        for rows, row0 in _parts(pl.program_id(0), tm):
            e = _dot(s_ref[rows, :], w_ref[...])
            if nxt is None:
                e_ref, xo_ref = rest
                ym, _ = _rms(e.astype(BF16))
                xo_ref[rows, :] = x1_ref[rows, :] + _modrows(mod_ref, row0, e.shape[0], GA2) * (ym * gpost_ref[...])
            else:
                gnext_ref, modn_ref, e_ref, xo_ref, h_ref = rest
                xo_ref[rows, :], h_ref[rows, :] = _norm_chain(row0, x1_ref[rows, :], e, gpost_ref[...], mod_ref, GA2,
                                                              gnext_ref[...], modn_ref, SH1, SC1)
            e_ref[rows, :] = e.astype(BF16)

    row = lambda w_: pl.BlockSpec((tm, w_), lambda i: (i, 0))
    extra = [] if nxt is None else list(nxt)
    return pl.pallas_call(
        kern, name=name, grid=(T // tm,),
        in_specs=[row(D_FF), _full_spec(w.shape), row(D), _full_spec(gpost.shape), _full_spec(mod.shape)]
                 + [_full_spec(a.shape) for a in extra],
        out_specs=[row(D)] * (2 if nxt is None else 3),
        out_shape=[_sds((T, D), BF16), _sds((T, D), F32)] + ([] if nxt is None else [_sds((T, D), BF16)]),
        compiler_params=_params(),
    )(s, w, x1, gpost, mod, *extra)


def _ffn_bwd_fused(name, fg, fu, w, de=None, head=None):
    T = fg.shape[0]
    tm = T // 8
    row = lambda w_: pl.BlockSpec((tm, w_), lambda i: (i, 0))
    w_spec = pl.BlockSpec(w.shape, lambda i: (0, 0), pipeline_mode=pl.Buffered(1))

    def tail(rows, de_v, fg_ref, fu_ref, w_ref, df_ref):
        ds = _dot(de_v, w_ref[...], NT)
        g, u = fg_ref[rows, :].astype(F32), fu_ref[rows, :].astype(F32)
        df_ref[rows, :] = jnp.concatenate([ds * u * _silu_grad(g), ds * _silu(g)], axis=1).astype(BF16)

    if head is None:
        def kern(de_ref, fg_ref, fu_ref, w_ref, df_ref):
            for rows, _ in _parts(pl.program_id(0), tm):
                tail(rows, de_ref[rows, :], fg_ref, fu_ref, w_ref, df_ref)

        return pl.pallas_call(
            kern, name=name, grid=(T // tm,), in_specs=[row(D), row(D_FF), row(D_FF), w_spec],
            out_specs=[row(2 * D_FF)], out_shape=[_sds((T, 2 * D_FF), BF16)], compiler_params=_params(),
        )(de, fg, fu, w)

    dx2, e, gpost, mod = head

    def kern(dx_ref, e_ref, fg_ref, fu_ref, w_ref, gpost_ref, mod_ref, de_ref, df_ref, dga_ref, dg_ref):
        i = pl.program_id(0)

        @pl.when(i == 0)
        def _():
            dga_ref[...] = jnp.zeros(dga_ref.shape, F32)
            dg_ref[...] = jnp.zeros(dg_ref.shape, F32)

        for rows, row0 in _parts(i, tm):
            de_v = _resid_bwd_vals(i, dx_ref[rows, :], e_ref[rows, :], gpost_ref[...], mod_ref, GA2, dga_ref, dg_ref,
                                   row0=row0).astype(BF16)
            de_ref[rows, :] = de_v
            tail(rows, de_v, fg_ref, fu_ref, w_ref, df_ref)

    return pl.pallas_call(
        kern, name=name, grid=(T // tm,),
        in_specs=[row(D), row(D), row(D_FF), row(D_FF), w_spec, _full_spec(gpost.shape), _full_spec(mod.shape)],
        out_specs=[row(D), row(2 * D_FF), _full_spec((2, D)), _full_spec((1, D))],
        out_shape=[_sds((T, D), BF16), _sds((T, 2 * D_FF), BF16), _sds((2, D), F32), _sds((1, D), F32)],
        compiler_params=_params(),
    )(dx2, e, fg, fu, w, gpost, mod)


def _zero_at_start(i, refs):
    @pl.when(i == 0)
    def _():
        for r in refs:
            r[...] = jnp.zeros(r.shape, F32)


def _proj_bwd_fused(name, dp, dgl, w_in_t, xa, dx1, gpre, mod, carry=None):
    T = dp.shape[0]
    tm = T // 8
    row = lambda w_: pl.BlockSpec((tm, w_), lambda i: (i, 0))

    def kern(dp_ref, dgl_ref, w_ref, xa_ref, dx1_ref, g_ref, mod_ref, dxa_ref, dsh_ref, dsc_ref, dg_ref):
        i = pl.program_id(0)
        _zero_at_start(i, (dsh_ref, dsc_ref, dg_ref))
        for rows, row0 in _parts(i, tm):
            dh = _dot(dp_ref[rows, :], w_ref[0:DP_W, :]) + _dot(dgl_ref[rows, :], w_ref[DP_W:, :])
            dxa_ref[rows, :] = dx1_ref[rows, :] + _normmod_bwd_vals(i, dh, xa_ref[rows, :], g_ref[...], mod_ref, SH1,
                                                                    SC1, dsh_ref, dsc_ref, dg_ref, row0=row0)

    ci, ca, co, cs, cscr = _carry_args(carry)
    res = pl.pallas_call(
        _carried(kern, carry, 7, 4, *_grid_ends((T // tm,))), name=name, grid=(T // tm,),
        in_specs=[row(DP_W), row(P_W - DP_W),
                  pl.BlockSpec(w_in_t.shape, lambda i: (0, 0), pipeline_mode=pl.Buffered(1)), row(D), row(D),
                  _full_spec(gpre.shape), _full_spec(mod.shape)] + ci,
        out_specs=[row(D), _full_spec((2, D)), _full_spec((2, D)), _full_spec((1, D))] + co,
        out_shape=[_sds((T, D), F32), _sds((2, D), F32), _sds((2, D), F32), _sds((1, D), F32)] + cs,
        scratch_shapes=cscr, compiler_params=_params(),
    )(dp, dgl, w_in_t, xa, dx1, gpre, mod, *ca)
    return res if carry is None else (res[:4], res[4:])


def _proj_wgrad(name, dp, dgl, h, carry=None):
    T, N = h.shape
    n1, n2 = DP_W // GLB, (P_W - DP_W) // GLB

    def kern(a1_ref, a2_ref, h_ref, o_ref):
        i = pl.program_id(0)

        @pl.when(i < n1)
        def _():
            o_ref[...] = _dot(a1_ref[...], h_ref[...], TN).astype(o_ref.dtype)

        @pl.when(i >= n1)
        def _():
            o_ref[...] = _dot(a2_ref[...], h_ref[...], TN).astype(o_ref.dtype)

    ci, ca, co, cs, cscr = _carry_args(carry)
    res = pl.pallas_call(
        _carried(kern, carry, 3, 1, *_grid_ends((n1 + n2,))), name=name, grid=(n1 + n2,),
        in_specs=[pl.BlockSpec((T, GLB), lambda i: (0, jnp.minimum(i, n1 - 1))),
                  pl.BlockSpec((T, GLB), lambda i: (0, jnp.maximum(i - n1, 0))),
                  pl.BlockSpec((T, N), lambda i: (0, 0), pipeline_mode=pl.Buffered(1))] + ci,
        out_specs=[pl.BlockSpec((GLB, N), lambda i: (i, 0))] + co,
        out_shape=[_sds((P_W, N), BF16)] + cs, scratch_shapes=cscr, compiler_params=_params(),
    )(dp, dgl, h, *ca)
    return res[0] if carry is None else (res[0], res[1:])


def _ffn_in_bwd_fused(name, df, w_t, x1, dres, mat, gpre, mod, gpost, carry=None):
    T = df.shape[0]
    tm = T // 8
    row = lambda w_: pl.BlockSpec((tm, w_), lambda i: (i, 0))

    def kern(df_ref, w_ref, x1_ref, dres_ref, mat_ref, gpre_ref, mod_ref, gpost_ref,
             dx1_ref, dm_ref, dsh_ref, dsc_ref, dgpre_ref, dga_ref, dgpost_ref):
        i = pl.program_id(0)
        _zero_at_start(i, (dsh_ref, dsc_ref, dgpre_ref, dga_ref, dgpost_ref))
        for rows, row0 in _parts(i, tm):
            dh2 = _dot(df_ref[rows, :], w_ref[...])
            dx1 = dres_ref[rows, :] + _normmod_bwd_vals(i, dh2, x1_ref[rows, :], gpre_ref[...], mod_ref, SH2, SC2,
                                                        dsh_ref, dsc_ref, dgpre_ref, row0=row0)
            dx1_ref[rows, :] = dx1
            dm_ref[rows, :] = _resid_bwd_vals(i, dx1, mat_ref[rows, :], gpost_ref[...], mod_ref, GA1, dga_ref,
                                              dgpost_ref, row0=row0).astype(BF16)

    ci, ca, co, cs, cscr = _carry_args(carry)
    res = pl.pallas_call(
        _carried(kern, carry, 8, 7, *_grid_ends((T // tm,))), name=name, grid=(T // tm,),
        in_specs=[row(2 * D_FF), pl.BlockSpec(w_t.shape, lambda i: (0, 0), pipeline_mode=pl.Buffered(1)), row(D),
                  row(D), row(D), _full_spec(gpre.shape), _full_spec(mod.shape), _full_spec(gpost.shape)] + ci,
        out_specs=[row(D), row(D), _full_spec((2, D)), _full_spec((2, D)), _full_spec((1, D)), _full_spec((2, D)),
                   _full_spec((1, D))] + co,
        out_shape=[_sds((T, D), F32), _sds((T, D), BF16), _sds((2, D), F32), _sds((2, D), F32), _sds((1, D), F32),
                   _sds((2, D), F32), _sds((1, D), F32)] + cs,
        scratch_shapes=cscr, compiler_params=_params(),
    )(df, w_t, x1, dres, mat, gpre, mod, gpost, *ca)
    return res if carry is None else (res[:7], res[7:])


def _out_bwd_fused(name, dm, w_out, w_o_rnn, w_o_attn, p, ya, yb):
    T = dm.shape[0]
    tm = T // 8
    row = lambda w_: pl.BlockSpec((tm, w_), lambda i: (i, 0))

    def kern(dm_ref, w_ref, wr_ref, wa_ref, g0, g1, g2, g3, ya_ref, yb_ref, dya_ref, dyb_ref, dgl_ref, du_ref, do_ref):
        for rows, _ in _parts(pl.program_id(0), tm):
            dz = _dot(dm_ref[rows, :], w_ref[...], NT)
            ga = _sigmoid(jnp.concatenate([g0[rows, :], g1[rows, :]], axis=1).astype(F32))
            gb = _sigmoid(jnp.concatenate([g2[rows, :], g3[rows, :]], axis=1).astype(F32))
            dya = (dz * ga).astype(BF16)
            dyb = (dz * gb).astype(BF16)
            dya_ref[rows, :] = dya
            dyb_ref[rows, :] = dyb
            dgl_ref[rows, :] = jnp.concatenate([dz * ya_ref[rows, :].astype(F32) * ga * (1.0 - ga),
                                                dz * yb_ref[rows, :].astype(F32) * gb * (1.0 - gb)],
                                               axis=1).astype(BF16)
            du_ref[rows, :] = _dot(dya, wr_ref[...], NT).astype(BF16)
            do_ref[rows, :] = _dot(dyb, wa_ref[...], NT).astype(BF16)

    return pl.pallas_call(
        kern, name=name, grid=(T // tm,),
        in_specs=[row(D)] + [_full_spec(w.shape) for w in (w_out, w_o_rnn, w_o_attn)]
                 + [pl.BlockSpec((tm, GLB), lambda i, q=q: (i, COL_GL // GLB + q)) for q in range(4)] + [row(D), row(D)],
        out_specs=[row(D), row(D), row(2 * D), row(D), row(D)],
        out_shape=[_sds((T, D), BF16), _sds((T, D), BF16), _sds((T, 2 * D), BF16), _sds((T, D), BF16),
                   _sds((T, D), BF16)],
        compiler_params=_params(),
    )(dm, w_out, w_o_rnn, w_o_attn, p, p, p, p, ya, yb)


AB = 128
CTX_BLKS = CTX // AB


def _rope_tables(S):
    pos = jnp.arange(S, dtype=jnp.int32)
    inv = ROPE_BASE ** (-jnp.arange(N_FREQ, dtype=F32) / N_FREQ)
    ang_r = (pos // GRID_W).astype(F32)[:, None] * inv[None, :]
    ang_c = (pos % GRID_W).astype(F32)[:, None] * inv[None, :]
    cos = jnp.concatenate([jnp.cos(ang_r)] * 2 + [jnp.cos(ang_c)] * 2, axis=1)
    sin = jnp.concatenate([-jnp.sin(ang_r), jnp.sin(ang_r), -jnp.sin(ang_c), jnp.sin(ang_c)], axis=1)
    return cos, sin


def _rope(x, cos, sin):
    w = x.shape[1]
    reps = w // HEAD
    lane = lax.broadcasted_iota(jnp.int32, x.shape, 1)
    partner = jnp.where((lane & 63) < 32, pltpu.roll(x, w - 32, 1), pltpu.roll(x, 32, 1))
    return x * jnp.tile(cos, (1, reps)) + partner * jnp.tile(sin, (1, reps))


def _unrope(dx, cos, sin):
    w = dx.shape[1]
    reps = w // HEAD
    lane = lax.broadcasted_iota(jnp.int32, dx.shape, 1)
    t = dx * jnp.tile(sin, (1, reps))
    partner = jnp.where((lane & 63) < 32, pltpu.roll(t, w - 32, 1), pltpu.roll(t, 32, 1))
    return dx * jnp.tile(cos, (1, reps)) + partner


def _qkv_prep(name, p, cos, sin, S):
    T = CTX + S
    nt = T // TR
    cb = CTX // TR
    KW = N_KV * HEAD

    def with_ones(v):
        ones = jnp.ones((TR, HEAD), BF16)
        return jnp.concatenate([v[:, kh * HEAD:(kh + 1) * HEAD] if part == 0 else ones
                                for kh in range(N_KV) for part in range(2)], axis=1)

    def kern(q_ref, k_ref, v_ref, cos_ref, sin_ref, qa_ref, kp_ref, vp_ref, kc_ref, vc_ref):
        i = pl.program_id(0)
        cos_v, sin_v = cos_ref[...], sin_ref[...]
        @pl.when(i < cb)
        def _():
            qa_ref[...] = (q_ref[...].astype(F32) * ATT_SCALE).astype(BF16)
            kc_ref[...] = k_ref[...]
            vc_ref[...] = with_ones(v_ref[...])

        @pl.when((i < cb) | (i >= nt))
        def _():
            kp_ref[...] = jnp.zeros(kp_ref.shape, BF16)
            vp_ref[...] = jnp.zeros(vp_ref.shape, BF16)

        @pl.when((i >= cb) & (i < nt))
        def _():
            qa_ref[...] = (_rope(q_ref[...].astype(F32), cos_v, sin_v) * ATT_SCALE).astype(BF16)
            kp_ref[...] = _rope(k_ref[...].astype(F32), cos_v, sin_v).astype(BF16)
            vp_ref[...] = with_ones(v_ref[...])

    tok = lambda i: jnp.minimum(i, nt - 1)
    lat_map = lambda i: (jnp.clip(i - cb, 0, nt - cb - 1), 0)
    ctx_map = lambda i: (jnp.minimum(i, cb - 1), 0)
    return pl.pallas_call(
        kern, name=name, grid=(nt + cb,),
        in_specs=[pl.BlockSpec((TR, N_Q * HEAD), lambda i: (tok(i), COL_Q // (N_Q * HEAD))),
                  pl.BlockSpec((TR, KW), lambda i: (tok(i), COL_K // KW)),
                  pl.BlockSpec((TR, KW), lambda i: (tok(i), COL_V // KW)),
                  pl.BlockSpec((TR, HEAD), lat_map), pl.BlockSpec((TR, HEAD), lat_map)],
        out_specs=[pl.BlockSpec((TR, N_Q * HEAD), lambda i: (tok(i), 0)),
                   pl.BlockSpec((TR, KW), lambda i: (i, 0)), pl.BlockSpec((TR, 2 * KW), lambda i: (i, 0)),
                   pl.BlockSpec((TR, KW), ctx_map), pl.BlockSpec((TR, 2 * KW), ctx_map)],
        out_shape=[_sds((T, N_Q * HEAD), BF16), _sds((S + 2 * CTX, KW), BF16), _sds((S + 2 * CTX, 2 * KW), BF16),
                   _sds((CTX, KW), BF16), _sds((CTX, 2 * KW), BF16)],
        compiler_params=_params(),
    )(p, p, p, cos, sin)


GW = Q_PER_KV * HEAD
HG = Q_PER_KV


def _band_bias(S):
    r = jnp.arange(AB, dtype=jnp.int32)[:, None]
    c = jnp.arange(3 * AB, dtype=jnp.int32)[None, :]
    near = jnp.abs(c - AB - r) <= AB
    valid = jnp.stack([near & (c >= AB), near, near & (c < 2 * AB)])
    return jnp.where(valid, 0.0, NEG_INF).astype(F32)


def _bias_spec(S):
    nb = S // AB
    return pl.BlockSpec((None, AB, 3 * AB), lambda kh, n: (jnp.where(n == 0, 0, jnp.where(n == nb - 1, 2, 1)), 0, 0))


def _head_probs(q, sink, kc, vce, kb, vbe, bias):
    s_c = _dot(q, kc, NT)
    m = jnp.maximum(jnp.max(s_c, axis=-1, keepdims=True), sink)
    if kb is not None:
        s_b = _dot(q, kb, NT) + bias
        m = jnp.maximum(m, jnp.max(s_b, axis=-1, keepdims=True))
    p_c = jnp.exp(s_c - m).astype(BF16)
    acc = _dot(p_c, vce)
    p_b = None
    if kb is not None:
        p_b = jnp.exp(s_b - m).astype(BF16)
        acc = acc + _dot(p_b, vbe)
    return p_c, p_b, m, acc


def _attn_fwd(name, qa, kc, vc, sink4, S, band=None, prev=None, carry=None):
    T = qa.shape[0]
    has_band = band is not None
    nq = S // AB if has_band else CTX_BLKS
    q_off = CTX_BLKS if has_band else 0

    def kern(*refs):
        q_ref, kc_ref, vc_ref, sink_ref = refs[:4]
        rest = refs[4:]
        o_ref = rest[-1]
        n = pl.program_id(1)
        kc_v, vce = kc_ref[...], vc_ref[...]
        kb = vbe = bias = None
        if has_band:
            kp_ref, vp_ref, bias_ref = rest[:3]
            start = pl.multiple_of(n * AB + (CTX - AB), AB)
            kb = kp_ref[pl.ds(start, 3 * AB), :]
            vbe = vp_ref[pl.ds(start, 3 * AB), :]
            bias = bias_ref[...]
        outs = []
        for g in range(Q_PER_KV):
            sink = sink_ref[g:g + 1, 0:1]
            _, _, m, acc = _head_probs(q_ref[:, g * HEAD:(g + 1) * HEAD], sink, kc_v, vce, kb, vbe, bias)
            l = acc[:, HEAD:] + jnp.exp(sink - m)
            outs.append(acc[:, :HEAD] / l)
        o_ref[...] = jnp.concatenate(outs, axis=1).astype(BF16)

    in_specs = [pl.BlockSpec((AB, GW), lambda kh, n: (n + q_off, kh)),
                pl.BlockSpec((CTX, HEAD), lambda kh, n: (0, kh)), pl.BlockSpec((CTX, 2 * HEAD), lambda kh, n: (0, kh)),
                pl.BlockSpec((None, Q_PER_KV, HEAD), lambda kh, n: (kh, 0, 0))]
    args = [qa, kc, vc, sink4]
    if has_band:
        in_specs += [pl.BlockSpec((S + 2 * CTX, HEAD), lambda kh, n: (0, kh)),
                     pl.BlockSpec((S + 2 * CTX, 2 * HEAD), lambda kh, n: (0, kh)), _bias_spec(S)]
        args += list(band)
    alias = {}
    if prev is not None:
        in_specs.append(ANY)
        alias = {len(args): 0}
        args.append(prev)
    ci, ca, co, cs, cscr = _carry_args(carry)
    res = pl.pallas_call(
        _carried(kern, carry, len(args), 1, *_grid_ends((N_KV, nq))), name=name, grid=(N_KV, nq),
        in_specs=in_specs + ci,
        out_specs=[pl.BlockSpec((AB, GW), lambda kh, n: (n + q_off, kh))] + co,
        out_shape=[_sds((T, N_Q * HEAD), BF16)] + cs, input_output_aliases=alias, scratch_shapes=cscr,
        compiler_params=_params(("arbitrary", "arbitrary")),
    )(*args, *ca)
    return res[0] if carry is None else (res[0], res[1:])


def _attn_bwd(name, qa, kc, vc, sink4, o_all, do_all, S, band=None, prev_dq=None, carry=None):
    T = qa.shape[0]
    has_band = band is not None
    nq = S // AB if has_band else CTX_BLKS
    q_off = CTX_BLKS if has_band else 0
    KW = N_KV * HEAD

    def kern(*refs):
        q_ref, kc_ref, vc_ref, sink_ref, o_ref, do_ref = refs[:6]
        rest = refs[6:]
        if has_band:
            kp_ref, vp_ref, bias_ref, cos_ref, sin_ref = rest[:5]
            rest = rest[5:]
        if prev_dq is not None:
            rest = rest[1:]
        dq_ref, dkc_ref, dvc_ref, dsink_ref = rest[:4]
        n = pl.program_id(1)

        @pl.when(n == 0)
        def _():
            dkc_ref[...] = jnp.zeros(dkc_ref.shape, F32)
            dvc_ref[...] = jnp.zeros(dvc_ref.shape, F32)
            dsink_ref[...] = jnp.zeros(dsink_ref.shape, F32)
            if has_band:
                rest[4][...] = jnp.zeros(rest[4].shape, F32)
                rest[5][...] = jnp.zeros(rest[5].shape, F32)

        kc_v, vce = kc_ref[...], vc_ref[...]
        vc_v = vce[:, :HEAD]
        kb = vbe = vb = bias = None
        if has_band:
            start = pl.multiple_of(n * AB + (CTX - AB), AB)
            kb = kp_ref[pl.ds(start, 3 * AB), :]
            vbe = vp_ref[pl.ds(start, 3 * AB), :]
            vb = vbe[:, :HEAD]
            bias = bias_ref[...]
        dq_parts, dsink_parts = [], []
        for g0 in range(0, Q_PER_KV, HG):
            heads = range(g0, g0 + HG)
            stack = lambda ref: jnp.concatenate([ref[:, g * HEAD:(g + 1) * HEAD] for g in heads], axis=0)
            q4, do4 = stack(q_ref), stack(do_ref)
            sink = jnp.concatenate([jnp.broadcast_to(sink_ref[g:g + 1, 0:1], (AB, 1)) for g in heads], axis=0)
            s_c = _dot(q4, kc_v, NT)
            m = jnp.maximum(jnp.max(s_c, axis=-1, keepdims=True), sink)
            if has_band:
                s_b = _dot(q4, kb, NT) + jnp.tile(bias, (HG, 1))
                m = jnp.maximum(m, jnp.max(s_b, axis=-1, keepdims=True))
            p_c = jnp.exp(s_c - m).astype(BF16).astype(F32)
            p_sink = jnp.exp(sink - m)
            l = jnp.sum(p_c, axis=-1, keepdims=True) + p_sink
            if has_band:
                p_b = jnp.exp(s_b - m).astype(BF16).astype(F32)
                l = l + jnp.sum(p_b, axis=-1, keepdims=True)
            inv = 1.0 / l
            delta = jnp.sum(do4.astype(F32) * stack(o_ref).astype(F32), axis=-1, keepdims=True)
            do4b = do4.astype(BF16)
            pn_c = (p_c * inv).astype(BF16)
            ds_c = (p_c * inv * (_dot(do4b, vc_v, NT) - delta)).astype(BF16)
            dq4 = _dot(ds_c, kc_v)
            dkc_ref[...] += _dot(q4, ds_c, TN)
            dvc_ref[...] += _dot(do4b, pn_c, TN)
            if has_band:
                pn_b = (p_b * inv).astype(BF16)
                ds_b = (p_b * inv * (_dot(do4b, vb, NT) - delta)).astype(BF16)
                dq4 = dq4 + _dot(ds_b, kb)
                rest[4][:, pl.ds(start, 3 * AB)] += _dot(q4, ds_b, TN)
                rest[5][:, pl.ds(start, 3 * AB)] += _dot(do4b, pn_b, TN)
            dq4 = dq4 * ATT_SCALE
            dq_parts += [dq4[k * AB:(k + 1) * AB, :] for k in range(HG)]
            ps = p_sink * inv * delta
            dsink_parts += [jnp.broadcast_to(-jnp.sum(ps[k * AB:(k + 1) * AB, :], axis=0, keepdims=True), (1, HEAD))
                            for k in range(HG)]
        dq = jnp.concatenate(dq_parts, axis=1)
        dq_ref[...] = (_unrope(dq, cos_ref[...], sin_ref[...]) if has_band else dq).astype(BF16)
        dsink_ref[...] += jnp.concatenate(dsink_parts, axis=0)

    q_spec = pl.BlockSpec((AB, GW), lambda kh, n: (n + q_off, kh))
    c_spec = pl.BlockSpec((CTX, HEAD), lambda kh, n: (0, kh))
    ce_spec = pl.BlockSpec((CTX, 2 * HEAD), lambda kh, n: (0, kh))
    s_spec = pl.BlockSpec((None, Q_PER_KV, HEAD), lambda kh, n: (kh, 0, 0))
    in_specs = [q_spec, c_spec, ce_spec, s_spec, q_spec, q_spec]
    args = [qa, kc, vc, sink4, o_all, do_all]
    ct_spec = pl.BlockSpec((HEAD, CTX), lambda kh, n: (kh, 0))
    dq_spec = pl.BlockSpec((AB, GW), lambda kh, n: (n + q_off, COL_Q // GW + kh))
    out_specs = [dq_spec, ct_spec, ct_spec, s_spec]
    out_shape = [_sds((T, DP_W), BF16), _sds((KW, CTX), F32), _sds((KW, CTX), F32), _sds((N_KV, Q_PER_KV, HEAD), F32)]
    if has_band:
        p_spec = pl.BlockSpec((S + 2 * CTX, HEAD), lambda kh, n: (0, kh))
        pt_spec = pl.BlockSpec((HEAD, S + 2 * CTX), lambda kh, n: (kh, 0))
        rope_spec = pl.BlockSpec((AB, HEAD), lambda kh, n: (n, 0))
        in_specs += [p_spec, pl.BlockSpec((S + 2 * CTX, 2 * HEAD), lambda kh, n: (0, kh)), _bias_spec(S), rope_spec,
                     rope_spec]
        args += list(band)
        out_specs += [pt_spec, pt_spec]
        out_shape += [_sds((KW, S + 2 * CTX), F32)] * 2
    alias = {}
    if prev_dq is not None:
        in_specs.append(ANY)
        alias = {len(args): 0}
        args.append(prev_dq)
    ci, ca, co, cs, cscr = _carry_args(carry)
    n_out = len(out_specs)
    res = pl.pallas_call(
        _carried(kern, carry, len(args), n_out, *_grid_ends((N_KV, nq))), name=name, grid=(N_KV, nq),
        in_specs=in_specs + ci, out_specs=out_specs + co, out_shape=out_shape + cs, scratch_shapes=cscr,
        input_output_aliases=alias, compiler_params=_params(("arbitrary", "arbitrary")),
    )(*args, *ca)
    return res if carry is None else (res[:n_out], res[n_out:])


def _dkv_assemble(name, dp, dkp, dvp, dkc_l, dvc_l, dkc_c, dvc_c, cos, sin, S):
    T = CTX + S
    KW = N_KV * HEAD

    def kern(dkp_ref, dvp_ref, dkcl_ref, dvcl_ref, dkcc_ref, dvcc_ref, cos_ref, sin_ref, dp_in, out_ref):
        i = pl.program_id(0)

        @pl.when(i == 0)
        def _():
            out_ref[...] = jnp.concatenate([(dkcl_ref[...] + dkcc_ref[...]).T, (dvcl_ref[...] + dvcc_ref[...]).T],
                                           axis=1).astype(BF16)

        @pl.when(i > 0)
        def _():
            out_ref[...] = jnp.concatenate([_unrope(dkp_ref[...].T, cos_ref[...], sin_ref[...]), dvp_ref[...].T],
                                           axis=1).astype(BF16)

    same = lambda i: (0, i)
    lat_map = lambda i: (jnp.maximum(i - 1, 0), 0)
    ctx_map = lambda i: (0, 0)
    return pl.pallas_call(
        kern, name=name, grid=(T // TR,),
        in_specs=[pl.BlockSpec((KW, TR), same), pl.BlockSpec((KW, TR), same),
                  pl.BlockSpec((KW, CTX), ctx_map), pl.BlockSpec((KW, CTX), ctx_map),
                  pl.BlockSpec((KW, CTX), ctx_map), pl.BlockSpec((KW, CTX), ctx_map),
                  pl.BlockSpec((TR, HEAD), lat_map), pl.BlockSpec((TR, HEAD), lat_map), ANY],
        out_specs=pl.BlockSpec((TR, 2 * KW), lambda i: (i, COL_K // (2 * KW))),
        out_shape=_sds((T, DP_W), BF16), input_output_aliases={8: 0}, compiler_params=_params(),
    )(dkp, dvp, dkc_l, dvc_l, dkc_c, dvc_c, cos, sin, dp)


RB = 128
CH = 256
HALO = 8
SUB = 8
GRP = 8


def _vscan(a, b, reverse):
    row = lax.broadcasted_iota(jnp.int32, a.shape, 0)
    A, H = a, b
    for s in (1, 2, 4):
        sh = SUB - s if reverse else s
        m = (row < SUB - s) if reverse else (row >= s)
        As = pltpu.roll(A, sh, 0)
        Hs = pltpu.roll(H, sh, 0)
        H = jnp.where(m, A * Hs + H, H)
        A = jnp.where(m, A * As, A)
    return A, H


def _scan_rows(a_ref, b_ref, r0, nrows, reverse, carry, emit):
    ngrp = nrows // (SUB * GRP)
    row = lax.broadcasted_iota(jnp.int32, (SUB, RB), 0)

    def grp(gi, carry):
        g = (ngrp - 1 - gi) if reverse else gi
        base = r0 + g * (SUB * GRP)
        for v in (range(GRP - 1, -1, -1) if reverse else range(GRP)):
            rs = pl.multiple_of(base + v * SUB, SUB)
            A, H = _vscan(a_ref[pl.ds(rs, SUB), :], b_ref[pl.ds(rs, SUB), :], reverse)
            hf = H + A * carry
            if reverse:
                before = jnp.where(row == SUB - 1, carry, pltpu.roll(hf, SUB - 1, 0))
                carry = hf[0:1, :]
            else:
                before = jnp.where(row == 0, carry, pltpu.roll(hf, 1, 0))
                carry = hf[SUB - 1:SUB, :]
            emit(rs, hf, before)
        return carry

    return lax.fori_loop(0, ngrp, grp, carry)


def _pad_start(ci):
    return pl.multiple_of(ci * CH + HALO * jnp.minimum(ci, 1), HALO)


def _conv_taps(ext, transpose=False):
    n = CH + 2 * HALO
    taps = []
    for k in range(CONV_W):
        off = CONV_LEFT - k if transpose else k - CONV_LEFT
        taps.append(ext[HALO:HALO + CH, :] if off == 0 else pltpu.roll(ext, (-off) % n, 0)[HALO:HALO + CH, :])
    return taps


def _lru_gates(xl, w4, b4, ls):
    pre = _dot(xl.astype(BF16), w4) + b4
    out = []
    for d in range(2):
        r = _sigmoid(pre[:, d * RB:(d + 1) * RB])
        i = _sigmoid(pre[:, (2 + d) * RB:(3 + d) * RB])
        la = LRU_C * r * ls[d:d + 1, :]
        a = jnp.exp(la)
        q = -jnp.tanh(la) * (1.0 + a * a)
        out.append((r, i, a, q))
    return out


def _rnn_specs(T):
    col = lambda n, *_: (0, n)
    return dict(
        xr=pl.BlockSpec((T, RB), lambda n, *_: (0, COL_XR // RB + n)),
        gr=pl.BlockSpec((T, RB), lambda n, *_: (0, COL_GR // RB + n)),
        act=pl.BlockSpec((T, RB), col),
        cw=pl.BlockSpec((CONV_W, RB), col), cb=pl.BlockSpec((1, RB), col),
        w4=pl.BlockSpec((None, RB, 4 * RB), lambda n, *_: (n, 0, 0)),
        b4=pl.BlockSpec((None, 1, 4 * RB), lambda n, *_: (n, 0, 0)),
        lam=pl.BlockSpec((2, RB), col))


PAD_ROWS = 3 * HALO


def _zero_pads(pad_ref, T):
    for r in (0, HALO + CTX, 2 * HALO + T):
        pad_ref[r:r + HALO, :] = jnp.zeros((HALO, RB), F32)


def _fill_padded(pad_ref, src_ref, T):
    _zero_pads(pad_ref, T)
    pad_ref[HALO:HALO + CTX, :] = src_ref[0:CTX, :].astype(F32)
    pad_ref[2 * HALO + CTX:2 * HALO + T, :] = src_ref[CTX:T, :].astype(F32)


def _pad_rows(ci):
    return pl.ds(pl.multiple_of(ci * CH + HALO + HALO * jnp.minimum(ci, 1), HALO), CH)


def _rnn_fwd(name, p, cw, cb, w4, b4, lam, T, carry=None):
    def kern(xr_ref, gr_ref, cw_ref, cb_ref, w4_ref, b4_ref, lam_ref,
             u_ref, a0, a1, yo_ref, hpf_ref, hpb_ref, r0_ref, r1_ref, i0_ref, i1_ref, xpad, b0, b1, y):
        _fill_padded(xpad, xr_ref, T)
        ls = _log_sigmoid(lam_ref[...])
        w4v, b4v, cwv, cbv = w4_ref[...], b4_ref[...], cw_ref[...], cb_ref[...]

        def chunk(ci, _):
            rows = pl.ds(pl.multiple_of(ci * CH, CH), CH)
            taps = _conv_taps(xpad[pl.ds(_pad_start(ci), CH + 2 * HALO), :])
            xl = cbv + sum(taps[k] * cwv[k:k + 1, :] for k in range(CONV_W))
            for d, (r, i, a, q) in enumerate(_lru_gates(xl, w4v, b4v, ls)):
                (a0, a1)[d][rows, :] = a
                (b0, b1)[d][rows, :] = jnp.sqrt(q) * (i * xl)
                (r0_ref, r1_ref)[d][rows, :] = r.astype(BF16)
                (i0_ref, i1_ref)[d][rows, :] = i.astype(BF16)
            return 0

        lax.fori_loop(0, T // CH, chunk, 0)
        zero = jnp.zeros((1, RB), F32)

        def emit_f(rs, hf, before):
            y[pl.ds(rs, SUB), :] = hf
            b0[pl.ds(rs, SUB), :] = before

        def emit_b(rs, hf, before):
            y[pl.ds(rs, SUB), :] += hf
            b1[pl.ds(rs, SUB), :] = before

        _scan_rows(a0, b0, 0, T, False, zero, emit_f)
        c = _scan_rows(a1, b1, 0, CTX, True, zero, emit_b)
        _scan_rows(a1, b1, CTX, T - CTX, True, c, emit_b)

        def finish(ci, _):
            rows = pl.ds(pl.multiple_of(ci * CH, CH), CH)
            yv = y[rows, :]
            u_ref[rows, :] = (yv * _gelu(gr_ref[rows, :].astype(F32))).astype(BF16)
            yo_ref[rows, :] = yv.astype(BF16)
            hpf_ref[rows, :] = b0[rows, :].astype(BF16)
            hpb_ref[rows, :] = b1[rows, :].astype(BF16)
            return 0

        lax.fori_loop(0, T // CH, finish, 0)

    sp = _rnn_specs(T)
    ci, ca, co, cs, cscr = _carry_args(carry)
    dts = [BF16, F32, F32] + [BF16] * 7
    res = pl.pallas_call(
        _carried(kern, carry, 7, 10, *_grid_ends((N_RNN_BLOCKS,))), name=name, grid=(N_RNN_BLOCKS,),
        in_specs=[sp["xr"], sp["gr"], sp["cw"], sp["cb"], sp["w4"], sp["b4"], sp["lam"]] + ci,
        out_specs=[sp["act"]] * 10 + co,
        out_shape=[_sds((T, D), dt) for dt in dts] + cs,
        scratch_shapes=[pltpu.VMEM((T + PAD_ROWS, RB), F32)] + [pltpu.VMEM((T, RB), F32)] * 3 + cscr,
        compiler_params=_params(),
    )(p, p, cw, cb, w4, b4, lam, *ca)
    return res if carry is None else (res[:10], res[10:])


def _rnn_bwd(name, p, du, saved, dp, cw, cb, w4, b4, lam, T, carry=None):
    def kern(xr_ref, gr_ref, du_ref, a0, a1, y_ref, hpf_ref, hpb_ref, r0_ref, r1_ref, i0_ref, i1_ref,
             cw_ref, cb_ref, w4_ref, b4_ref, lam_ref, dp_in,
             dp_ref, dcw_ref, dcb_ref, dw4_ref, db4_ref, dlam_ref,
             xpad, dxpad, c0, c1, dy):
        j = pl.program_id(1)

        @pl.when(j == 0)
        def _():
            scans(gr_ref, du_ref, a0, a1, y_ref, dp_ref, c0, c1, dy)

        @pl.when(j == 1)
        def _():
            gates(xr_ref, a0, a1, (hpf_ref, hpb_ref), (r0_ref, r1_ref), (i0_ref, i1_ref), cw_ref, cb_ref, w4_ref,
                  lam_ref, dp_ref, dcw_ref, dcb_ref, dw4_ref, db4_ref, dlam_ref, xpad, dxpad, c0, c1)

    def scans(gr_ref, du_ref, a0, a1, y_ref, dgr_ref, c0, c1, dy):
        def phase_a(ci, _):
            rows = pl.ds(pl.multiple_of(ci * CH, CH), CH)
            gr = gr_ref[rows, :].astype(F32)
            duv = du_ref[rows, :].astype(F32)
            dyv = duv * _gelu(gr)
            dgr_ref[rows, :] = (duv * y_ref[rows, :].astype(F32) * _gelu_grad(gr)).astype(BF16)
            dy[rows, :] = dyv
            c0[rows, :] = a0[rows, :] * dyv
            c1[rows, :] = a1[rows, :] * dyv
            return 0

        lax.fori_loop(0, T // CH, phase_a, 0)
        zero = jnp.zeros((1, RB), F32)

        def emit0(rs, hf, before):
            c0[pl.ds(rs, SUB), :] = dy[pl.ds(rs, SUB), :] + before

        def emit1(rs, hf, before):
            c1[pl.ds(rs, SUB), :] = dy[pl.ds(rs, SUB), :] + before

        _scan_rows(a0, c0, 0, T, True, zero, emit0)
        c = _scan_rows(a1, c1, CTX, T - CTX, False, zero, emit1)
        _scan_rows(a1, c1, 0, CTX, False, c, emit1)

    def gates(xr_ref, a0, a1, hp_refs, r_refs, i_refs, cw_ref, cb_ref, w4_ref, lam_ref,
              dxr_ref, dcw_ref, dcb_ref, dw4_ref, db4_ref, dlam_ref, xpad, dxpad, c0, c1):
        _fill_padded(xpad, xr_ref, T)
        _zero_pads(dxpad, T)
        lam_v = lam_ref[...]
        ls = _log_sigmoid(lam_v)
        w4v, cwv, cbv = w4_ref[...], cw_ref[...], cb_ref[...]

        def conv_chunk(ci):
            taps = _conv_taps(xpad[pl.ds(_pad_start(ci), CH + 2 * HALO), :])
            return taps, cbv + sum(taps[k] * cwv[k:k + 1, :] for k in range(CONV_W))

        dw4_ref[...] = jnp.zeros(dw4_ref.shape, F32)
        db4_ref[...] = jnp.zeros(db4_ref.shape, F32)
        dlam_ref[...] = jnp.zeros(dlam_ref.shape, F32)
        dcw_ref[...] = jnp.zeros(dcw_ref.shape, F32)
        dcb_ref[...] = jnp.zeros(dcb_ref.shape, F32)

        def phase_c(ci, _):
            base = pl.multiple_of(ci * CH, CH)
            rows = pl.ds(base, CH)
            _, xl = conv_chunk(ci)
            dxl = jnp.zeros((CH, RB), F32)
            dpre_a, dpre_x, dls = [], [], []
            for d in range(2):
                a = (a0, a1)[d][rows, :]
                r = r_refs[d][rows, :].astype(F32)
                i = i_refs[d][rows, :].astype(F32)
                q = -jnp.tanh(LRU_C * r * ls[d:d + 1, :]) * (1.0 + a * a)
                g = (c0, c1)[d][rows, :]
                hp = hp_refs[d][rows, :].astype(F32)
                gm = g * jnp.sqrt(q)
                di = gm * xl
                dxl = dxl + gm * i
                dla = a * (g * hp - a * (g * (i * xl)) * lax.rsqrt(q))
                dr = dla * (LRU_C * ls[d:d + 1, :])
                dls.append(_colsum(dla * (LRU_C * r)))
                dpre_a.append(dr * r * (1.0 - r))
                dpre_x.append(di * i * (1.0 - i))
            dpre = jnp.concatenate(dpre_a + dpre_x, axis=1)
            dpre_b = dpre.astype(BF16)
            dxl = dxl + _dot(dpre_b, w4v, NT)
            dw4_ref[...] += _dot(xl.astype(BF16), dpre_b, TN)
            db4_ref[...] += _colsum(dpre)
            dlam_ref[...] += jnp.concatenate(dls, axis=0)
            dcb_ref[...] += _colsum(dxl)
            dxpad[_pad_rows(ci), :] = dxl
            return 0

        lax.fori_loop(0, T // CH, phase_c, 0)
        dlam_ref[...] = dlam_ref[...] * _sigmoid(-lam_v)

        def phase_d(ci, _):
            base = pl.multiple_of(ci * CH, CH)
            rows = pl.ds(base, CH)
            xtaps, _ = conv_chunk(ci)
            dtaps = _conv_taps(dxpad[pl.ds(_pad_start(ci), CH + 2 * HALO), :], transpose=True)
            dxl = dxpad[_pad_rows(ci), :]
            dxr_ref[rows, :] = sum(dtaps[k] * cwv[k:k + 1, :] for k in range(CONV_W)).astype(BF16)
            dcw_ref[...] += jnp.concatenate([_colsum(dxl * xtaps[k]) for k in range(CONV_W)], axis=0)
            return 0

        lax.fori_loop(0, T // CH, phase_d, 0)

    sp = _rnn_specs(T)
    dp_spec = pl.BlockSpec((T, RB), lambda n, j: (0, COL_GR // RB + n - j * (COL_GR - COL_XR) // RB))
    ci, ca, co, cs, cscr = _carry_args(carry)
    n_in = 3 + len(saved) + 5 + 1
    res = pl.pallas_call(
        _carried(kern, carry, n_in, 6, *_grid_ends((N_RNN_BLOCKS, 2))), name=name, grid=(N_RNN_BLOCKS, 2),
        in_specs=[sp["xr"], sp["gr"]] + [sp["act"]] * (1 + len(saved)) + [sp["cw"], sp["cb"], sp["w4"], sp["b4"],
                                                                           sp["lam"], ANY] + ci,
        out_specs=[dp_spec, sp["cw"], sp["cb"], sp["w4"], sp["b4"], sp["lam"]] + co,
        out_shape=[_sds((T, DP_W), BF16), _sds((CONV_W, D), F32), _sds((1, D), F32),
                   _sds((N_RNN_BLOCKS, RB, 4 * RB), F32), _sds((N_RNN_BLOCKS, 1, 4 * RB), F32), _sds((2, D), F32)] + cs,
        scratch_shapes=[pltpu.VMEM((T + PAD_ROWS, RB), F32)] * 2 + [pltpu.VMEM((T, RB), F32)] * 3 + cscr,
        input_output_aliases={n_in - 1: 0},
        compiler_params=_params(("arbitrary", "arbitrary")),
    )(p, p, du, *saved, cw, cb, w4, b4, lam, dp, *ca)
    return res if carry is None else (res[:6], res[6:])


class _Plan:
    def __init__(self, shards, Ws):
        L = len(Ws)
        self.shards, self.Ws = shards, Ws
        self.Gs = [None] * L
        self.slots = [dict() for _ in range(L)]
        self.gate_slots = [None] * L
        self.table = {}
        for l in range(L):
            t = f"l{l}_"
            self.table[t + "rnn_fwd"] = [("gather", l, k) for k in ("wffn_in_t", "wo_rnn", "wo_attn", "wout")]
            if l + 1 < L:
                self.table[t + "attn_lat_fwd"] = [("gather", l + 1, "win_t")]
                self.table[t + "ffn_in"] = [("gather", l, "wffn_out")]
            else:
                self.table[t + "attn_lat_fwd"] = [("gather", l, "wffn_out")]
            self.table[t + "ffn_in_dx"] = [("scatter", l, "wffn_out")]
            self.table[t + "attn_lat_bwd"] = [("scatter", l, "wffn_in_t")]
            self.table[t + "proj_dx"] = [("scatter", l, "win_t_a" if l > 0 else "win_t_b")]
            self.table[t + "rnn_bwd"] = ([("scatter", l, k) for k in ("wout", "wo_attn", "wo_rnn")]
                                         + ([("scatter", l + 1, "win_t_b"), ("gates", l + 1, "w4")] if l + 1 < L else []))
        self.table["l0_proj_dw_b"] = [("scatter", 0, "win_t_a"), ("gates", 0, "w4")]
        self.table["l0_mix_norm"] = [("gather", 0, "win_t")]

    def carry(self, name):
        jobs = []
        for kind, l, k in self.table.get(name, []):
            if kind == "gather":
                jobs.append(("gather", self.shards[l][k]))
            elif kind == "scatter":
                jobs.append(("scatter", self.Gs[l][k].reshape(N_DEV, -1, self.Gs[l][k].shape[-1])))
            else:
                jobs.append(("gather", self.Gs[l]["w4"].reshape(N_RNN_BLOCKS * RB, 4 * RB).astype(BF16)))
        return _Carry(jobs) if jobs else None

    def done(self, name, got):
        for (kind, l, k), res in zip(self.table[name], got):
            if kind == "gather":
                self.Ws[l][k] = res.reshape(-1, D)
            elif kind == "scatter":
                self.slots[l][k] = res
            else:
                self.gate_slots[l] = res


def _run(X, fn, name, *args, **kw):
    carry = None if X is None else X.carry(name)
    if carry is None:
        return fn(name, *args, **kw)
    out, got = fn(name, *args, carry=carry, **kw)
    X.done(name, got)
    return out


def _layer_fwd(l, xa, h, W, rope, S, nxt, X=None):
    T = xa.shape[0]
    tag = f"l{l}_"
    cos, sin, bias = rope
    p = _run(X, _mm_act, tag + "proj", h, W["win_t"], "NT", BF16)
    u, *rnn_saved = _run(X, _rnn_fwd, tag + "rnn_fwd", p, W["cw"], W["cb"], W["w4"], W["b4"], W["lam"], T)
    qa, kp, vp, kc, vc = _qkv_prep(tag + "qkv_prep", p, cos, sin, S)
    o_all = _attn_fwd(tag + "attn_ctx_fwd", qa, kc, vc, W["sink4"], S)
    o_all = _run(X, _attn_fwd, tag + "attn_lat_fwd", qa, kc, vc, W["sink4"], S, band=(kp, vp, bias), prev=o_all)
    ya, yb, z, m, x1, h2 = _out_fused(tag + "out", p, u, o_all, xa, W["wo_rnn"], W["wo_attn"], W["wout"],
                                      W["g_mix_post"], W["mod"], W["g_ffn_pre"])
    fg, fu, s = _run(X, _ffn_in_fused, tag + "ffn_in", h2, W["wffn_in_t"])
    e, *out = _ffn_out_fused(tag + "ffn_out", s, W["wffn_out"], x1, W["g_ffn_post"], W["mod"], nxt)
    saved = dict(xa=xa, h=h, p=p, u=u, rnn=rnn_saved, qa=qa, kp=kp, vp=vp, kc=kc, vc=vc, o_all=o_all,
                 ya=ya, yb=yb, z=z, m=m, x1=x1, h2=h2, fg=fg, fu=fu, s=s, e=e)
    return saved, out


def _layer_bwd(l, dx2, A, W, rope, S, X=None, loss_of=None):
    T = A["xa"].shape[0]
    tag = f"l{l}_"
    cos, sin, bias = rope
    G = {}
    if X is not None:
        X.Gs[l] = G
    if loss_of is None:
        de, df, dga2, G["g_ffn_post"] = _ffn_bwd_fused(tag + "ffn_bwd", A["fg"], A["fu"], W["wffn_out"],
                                                       head=(dx2, A["e"], W["g_ffn_post"], W["mod"]))
    else:
        dx2, de, dga2, G["g_ffn_post"], G["sq"] = _loss_resid_bwd(tag + "loss_ffn_resid_bwd", *loss_of, A["e"],
                                                                  W["g_ffn_post"], W["mod"], GA2)
        df, = _ffn_bwd_fused(tag + "ffn_bwd", A["fg"], A["fu"], W["wffn_out"], de=de)
    G["wffn_out"] = _mm_wgrad(tag + "ffn_out_dw", A["s"], de)
    dx1, dm, dsh2, dsc2, G["g_ffn_pre"], dga1, G["g_mix_post"] = _run(
        X, _ffn_in_bwd_fused, tag + "ffn_in_dx", df, W["wffn_in_t"], A["x1"], dx2, A["m"], W["g_ffn_pre"], W["mod"],
        W["g_mix_post"])
    G["wffn_in_t"] = _run(X, _mm_wgrad, tag + "ffn_in_dw", df, A["h2"])
    G["wout"] = _mm_wgrad(tag + "out_dw", A["z"], dm)
    dya, dyb, dgl, du, do = _out_bwd_fused(tag + "out_dx", dm, W["wout"], W["wo_rnn"], W["wo_attn"], A["p"], A["ya"],
                                           A["yb"])
    G["wo_attn"] = _mm_wgrad(tag + "o_attn_dw", A["o_all"], dyb)
    G["wo_rnn"] = _mm_wgrad(tag + "o_rnn_dw", A["u"], dya)
    dp, dkc_c, dvc_c, dsink_c = _attn_bwd(tag + "attn_ctx_bwd", A["qa"], A["kc"], A["vc"], W["sink4"], A["o_all"], do, S)
    dp, dkc_l, dvc_l, dsink_l, dkp, dvp = _run(
        X, _attn_bwd, tag + "attn_lat_bwd", A["qa"], A["kc"], A["vc"], W["sink4"], A["o_all"], do, S,
        band=(A["kp"], A["vp"], bias, cos, sin), prev_dq=dp)
    G["sink4"] = dsink_c + dsink_l
    dp = _dkv_assemble(tag + "dkv", dp, dkp, dvp, dkc_l, dvc_l, dkc_c, dvc_c, cos, sin, S)
    dp, G["cw"], G["cb"], G["w4"], G["b4"], G["lam"] = _run(
        X, _rnn_bwd, tag + "rnn_bwd", A["p"], du, A["rnn"], dp, W["cw"], W["cb"], W["w4"], W["b4"], W["lam"], T)
    proj_dx = (_proj_bwd_fused, tag + "proj_dx", dp, dgl, W["win_t"], A["xa"], dx1, W["g_mix_pre"], W["mod"])
    if X is not None:
        G["win_t_a"] = _run(X, _proj_wgrad, tag + "proj_dw_a", dp, dgl, A["h"][:, :D // 2])
        if l > 0:
            dxa, dsh1, dsc1, G["g_mix_pre"] = _run(X, *proj_dx)
        G["win_t_b"] = _run(X, _proj_wgrad, tag + "proj_dw_b", dp, dgl, A["h"][:, D // 2:])
        if l == 0:
            dxa, dsh1, dsc1, G["g_mix_pre"] = _run(X, *proj_dx)
    else:
        dxa, dsh1, dsc1, G["g_mix_pre"] = _run(X, *proj_dx)
        G["win_t"] = _proj_wgrad(tag + "proj_dw", dp, dgl, A["h"])
    G["mod"] = jnp.concatenate([dsh1, dsc1, dga1, dsh2, dsc2, dga2], axis=1)
    return dxa, G


def _local_step(ctx, x, target, Ws, S, X=None):
    rope = (*_rope_tables(S), _band_bias(S))
    L = len(Ws)
    x, h = _run(X, _normmod_fwd, "l0_mix_norm", ctx, x, Ws[0]["g_mix_pre"], Ws[0]["mod"], SH1, SC1)
    saved = []
    for l in range(L):
        nxt = (Ws[l + 1]["g_mix_pre"], Ws[l + 1]["mod"]) if l + 1 < L else None
        A, out = _layer_fwd(l, x, h, Ws[l], rope, S, nxt, X)
        saved.append(A)
        if l + 1 < L:
            x, h = out
    Gs = [None] * L
    dx = None
    for l in reversed(range(L)):
        dx, Gs[l] = _layer_bwd(l, dx, saved[l], Ws[l], rope, S, X, loss_of=(out[0], target) if l == L - 1 else None)
    return Gs[L - 1]["sq"], dx, Gs


MESH = pl.DeviceIdType.MESH


def _place():
    return lax.axis_index("x"), lax.axis_index("y"), lax.axis_index("c")


def _lin(px, py, pc):
    return 4 * px + 2 * py + pc


def _allgather_small(name, blk):
    m, n = blk.shape

    def body(x_ref, out_ref, send_sems, recv_sems, local_sem):
        x, y, c = _place()
        me, sibling = (x, y, c), (x, y, 1 - c)
        chips = [(1 - x, y), (x, 1 - y), (1 - x, 1 - y)]

        def copy(k, block, to, src=None):
            dst = out_ref.at[_lin(*block)]
            return pltpu.make_async_remote_copy(src_ref=dst if src is None else src, dst_ref=dst,
                                                send_sem=send_sems.at[k], recv_sem=recv_sems.at[k],
                                                device_id=to, device_id_type=MESH)

        mine = pltpu.make_async_copy(x_ref, out_ref.at[_lin(*me)], local_sem)
        mine.start()
        first = [copy(0, me, sibling, src=x_ref)]
        first += [copy(1 + j, me, (*chip, c), src=x_ref) for j, chip in enumerate(chips)]
        for cp in first:
            cp.start()
        passed = [copy(4 + j, (*chip, c), sibling) for j, chip in enumerate(chips)]
        for j, chip in enumerate(chips):
            copy(1 + j, (*chip, c), me).wait_recv()
            passed[j].start()
        copy(0, sibling, me).wait_recv()
        for j, chip in enumerate(chips):
            copy(4 + j, (*chip, 1 - c), me).wait_recv()
        for cp in first + passed:
            cp.wait_send()
        mine.wait()

    return pl.pallas_call(
        body, name=name, out_shape=_sds((N_DEV, m, n), blk.dtype),
        in_specs=[pl.BlockSpec(memory_space=pltpu.VMEM)], out_specs=pl.BlockSpec(memory_space=pltpu.VMEM),
        scratch_shapes=[pltpu.SemaphoreType.DMA((7,)), pltpu.SemaphoreType.DMA((7,)), pltpu.SemaphoreType.DMA],
        compiler_params=pltpu.CompilerParams(vmem_limit_bytes=VMEM_LIMIT),
    )(blk)


MOD_ROWS = 16
MOD_SHARD = 6 * D // N_DEV
HI = lax.Precision.HIGHEST


def _mod_fwd(name, c9, w_mod, b_shard):
    L = w_mod.shape[0]

    def kern(c_ref, w_ref, b_ref, o_ref):
        o_ref[...] = lax.dot_general(_silu(c_ref[...]), w_ref[...], NN, precision=HI,
                                     preferred_element_type=F32) + b_ref[...]

    return pl.pallas_call(
        kern, name=name, grid=(L,),
        in_specs=[_full_spec(c9.shape), pl.BlockSpec((None, D, MOD_SHARD), lambda l: (l, 0, 0)),
                  pl.BlockSpec((None, 1, MOD_SHARD), lambda l: (l, 0, 0))],
        out_specs=pl.BlockSpec((None, MOD_ROWS, MOD_SHARD), lambda l: (l, 0, 0)),
        out_shape=_sds((L, MOD_ROWS, MOD_SHARD), F32), compiler_params=_params(),
    )(c9, w_mod, b_shard)


def _mod_bwd(name, c9, w_mod, dmod_all, dmod_cols):
    L = w_mod.shape[0]

    def rows9(ref, l):
        own = jnp.concatenate([ref[j, 2 * l + 1:2 * l + 2, :] for j in range(N_DEV)], axis=0)
        ctx = ref[0, 2 * l:2 * l + 1, :]
        for j in range(1, N_DEV):
            ctx = ctx + ref[j, 2 * l:2 * l + 1, :]
        return own, ctx

    def kern(c_ref, w_ref, all_ref, cols_ref, gw_ref, gb_ref, gc_ref):
        l = pl.program_id(0)
        for ll in range(L):
            @pl.when(l == ll)
            def _():
                own, ctx = rows9(all_ref, ll)
                gb_ref[...] = _colsum(own) + ctx
                own_s, ctx_s = rows9(cols_ref, ll)
                r16 = jnp.concatenate([own_s, ctx_s, jnp.zeros((MOD_ROWS - N_DEV - 1, MOD_SHARD), F32)], axis=0)
                gw_ref[...] = lax.dot_general(_silu(c_ref[...]), r16, TN, precision=HI, preferred_element_type=F32)
                part = lax.dot_general(r16, w_ref[...], NT, precision=HI,
                                       preferred_element_type=F32)[N_DEV:N_DEV + 1, :]
                if ll == 0:
                    gc_ref[...] = part
                else:
                    gc_ref[...] += part

    return pl.pallas_call(
        kern, name=name, grid=(L,),
        in_specs=[_full_spec(c9.shape), pl.BlockSpec((None, D, MOD_SHARD), lambda l: (l, 0, 0)),
                  _full_spec(dmod_all.shape), _full_spec(dmod_cols.shape)],
        out_specs=[pl.BlockSpec((None, D, MOD_SHARD), lambda l: (l, 0, 0)),
                   pl.BlockSpec((None, 1, 6 * D), lambda l: (l, 0, 0)), _full_spec((1, D))],
        out_shape=[_sds((L, D, MOD_SHARD), F32), _sds((L, 1, 6 * D), F32), _sds((1, D), F32)],
        compiler_params=_params(),
    )(c9, w_mod, dmod_all, dmod_cols)


_BC1 = 1.0 - ADAM_B1 ** ADAM_STEP
_BC2 = 1.0 - ADAM_B2 ** ADAM_STEP


def _adamw_vals(w, g, m, v):
    m = ADAM_B1 * m + (1.0 - ADAM_B1) * g
    v = ADAM_B2 * v + (1.0 - ADAM_B2) * (g * g)
    delta = -ADAM_LR * ((m / _BC1) / (jnp.sqrt(v / _BC2) + ADAM_EPS) + ADAM_WD * w)
    return delta, m, v


def _adamw(name, w, g, m, v, tile):
    R, C = w.shape
    blk = ((tile, C), lambda i: (i, 0))

    def body(i, ins, ps, outs, acc):
        d, mm, vv = _adamw_vals(ins[0][...], ins[1][...], ins[2][...], ins[3][...])
        outs[0][...] = d
        outs[1][...] = mm
        outs[2][...] = vv

    return _ew(name, body, R // tile, [(a, *blk) for a in (w, g, m, v)], [], [(_sds((R, C), F32), *blk)] * 3)


def _sum_slots(ref):
    g = ref[0].astype(F32)
    for j in range(1, N_DEV):
        g = g + ref[j].astype(F32)
    return g


def _adamw_slots(name, slots, shape, tile, wmv=None):
    L, R, C = shape
    n = R // tile
    spec = pl.BlockSpec((None, tile, C), lambda l, i: (l, i, 0))
    pieces = [s if isinstance(s, (list, tuple)) else [s] for s in slots]
    layer_of = [ll for ll, ps in enumerate(pieces) for _ in ps]
    flat = [p for ps in pieces for p in ps]
    wmv = list(wmv or [])

    def slot_spec(ll, cols):
        return pl.BlockSpec((N_DEV, tile, cols),
                            lambda l, i: (0, jnp.where(l == ll, i, jnp.where(l < ll, 0, n - 1)), 0))

    def kern(*refs):
        s_refs = refs[:len(flat)]
        rest = refs[len(flat):]
        l = pl.program_id(0)
        for ll in range(L):
            @pl.when(l == ll)
            def _():
                parts = [_sum_slots(r) for r, lr in zip(s_refs, layer_of) if lr == ll]
                g = parts[0] if len(parts) == 1 else jnp.concatenate(parts, axis=1)
                if wmv:
                    w_ref, m_ref, v_ref, g_ref, d_ref, mo_ref, vo_ref = rest
                    d_ref[...], mo_ref[...], vo_ref[...] = _adamw_vals(w_ref[...], g, m_ref[...], v_ref[...])
                else:
                    g_ref, = rest
                g_ref[...] = g

    n_out = 4 if wmv else 1
    return pl.pallas_call(
        kern, name=name, grid=(L, n),
        in_specs=[slot_spec(ll, p.shape[-1]) for ll, p in zip(layer_of, flat)] + [spec] * len(wmv),
        out_specs=[spec] * n_out, out_shape=[_sds((L, R, C), F32)] * n_out,
        compiler_params=_params(("arbitrary", "arbitrary")),
    )(*flat, *wmv)


def _sum_blocks(name, blocks):
    _, R, C = blocks.shape

    def kern(b_ref, o_ref):
        o_ref[...] = _sum_slots(b_ref)

    return pl.pallas_call(kern, name=name, in_specs=[_full_spec(blocks.shape)], out_specs=_full_spec((R, C)),
                          grid=(1,), out_shape=_sds((R, C), F32), compiler_params=_params())(blocks)


BIG = ("win_t", "wo_rnn", "wo_attn", "wout", "wffn_in_t", "wffn_out")
BIG_SRC = ("w_in", "w_o_rnn", "w_o_attn", "w_out", "w_ffn_in", "w_ffn_out")
BIG_T = (True, False, False, False, True, False)
BIG_TILE = (176, 128, 128, 128, 176, 176)


def _chan_full(g8):
    return jnp.transpose(g8, (1, 0, 2)).reshape(g8.shape[1], D)


def kernel(x, c, ctx, c_ctx, w_mod, b_mod, g_mix_pre, g_mix_post, g_ffn_pre, g_ffn_post, w_in, conv_w, conv_b, lru_wa, lru_ba, lru_wx, lru_bx, lru_lam, attn_sink, w_o_rnn, w_o_attn, w_out, w_ffn_in, w_ffn_out, loss_target, m_c_ctx, m_w_mod, m_b_mod, m_g_mix_pre, m_g_mix_post, m_g_ffn_pre, m_g_ffn_post, m_w_in, m_conv_w, m_conv_b, m_lru_wa, m_lru_ba, m_lru_wx, m_lru_bx, m_lru_lam, m_attn_sink, m_w_o_rnn, m_w_o_attn, m_w_out, m_w_ffn_in, m_w_ffn_out, v_c_ctx, v_w_mod, v_b_mod, v_g_mix_pre, v_g_mix_post, v_g_ffn_pre, v_g_ffn_post, v_w_in, v_conv_w, v_conv_b, v_lru_wa, v_lru_ba, v_lru_wx, v_lru_bx, v_lru_lam, v_attn_sink, v_w_o_rnn, v_w_o_attn, v_w_out, v_w_ffn_in, v_w_ffn_out):
    P = dict(c_ctx=c_ctx, w_mod=w_mod, b_mod=b_mod, g_mix_pre=g_mix_pre, g_mix_post=g_mix_post, g_ffn_pre=g_ffn_pre,
             g_ffn_post=g_ffn_post, w_in=w_in, conv_w=conv_w, conv_b=conv_b, lru_wa=lru_wa, lru_ba=lru_ba,
             lru_wx=lru_wx, lru_bx=lru_bx, lru_lam=lru_lam, attn_sink=attn_sink, w_o_rnn=w_o_rnn, w_o_attn=w_o_attn,
             w_out=w_out, w_ffn_in=w_ffn_in, w_ffn_out=w_ffn_out)
    Mo = dict(c_ctx=m_c_ctx, w_mod=m_w_mod, b_mod=m_b_mod, g_mix_pre=m_g_mix_pre, g_mix_post=m_g_mix_post,
              g_ffn_pre=m_g_ffn_pre, g_ffn_post=m_g_ffn_post, w_in=m_w_in, conv_w=m_conv_w, conv_b=m_conv_b,
              lru_wa=m_lru_wa, lru_ba=m_lru_ba, lru_wx=m_lru_wx, lru_bx=m_lru_bx, lru_lam=m_lru_lam,
              attn_sink=m_attn_sink, w_o_rnn=m_w_o_rnn, w_o_attn=m_w_o_attn, w_out=m_w_out, w_ffn_in=m_w_ffn_in,
              w_ffn_out=m_w_ffn_out)
    Vo = dict(c_ctx=v_c_ctx, w_mod=v_w_mod, b_mod=v_b_mod, g_mix_pre=v_g_mix_pre, g_mix_post=v_g_mix_post,
              g_ffn_pre=v_g_ffn_pre, g_ffn_post=v_g_ffn_post, w_in=v_w_in, conv_w=v_conv_w, conv_b=v_conv_b,
              lru_wa=v_lru_wa, lru_ba=v_lru_ba, lru_wx=v_lru_wx, lru_bx=v_lru_bx, lru_lam=v_lru_lam,
              attn_sink=v_attn_sink, w_o_rnn=v_w_o_rnn, w_o_attn=v_w_o_attn, w_out=v_w_out, w_ffn_in=v_w_ffn_in,
              w_ffn_out=v_w_ffn_out)
    L = w_in.shape[0]
    S = x.shape[1]
    me = _lin(*_place())

    small = jnp.concatenate([c.reshape(8, 128), conv_w.reshape(L * CONV_W, 128), lru_ba.reshape(2 * L, 128),
                             lru_bx.reshape(2 * L, 128), lru_lam.reshape(2 * L, 128), jnp.zeros((4, 128), F32)], axis=0)
    small_all = _allgather_small("ag_small", small)
    c_all = small_all[:, 0:8].reshape(N_DEV, D)
    conv_w_f = _chan_full(small_all[:, 8:16]).reshape(L, CONV_W, D)
    lru_ba_f = _chan_full(small_all[:, 16:20]).reshape(L, 2, D)
    lru_bx_f = _chan_full(small_all[:, 20:24]).reshape(L, 2, D)
    lru_lam_f = _chan_full(small_all[:, 24:28]).reshape(L, 2, D)

    c9 = jnp.concatenate([c_all, c_ctx[None], jnp.zeros((MOD_ROWS - N_DEV - 1, D), F32)], axis=0)
    b_shard = lax.dynamic_slice_in_dim(b_mod, me * MOD_SHARD, MOD_SHARD, axis=1)[:, None, :]
    mod_part = _mod_fwd("mod_fwd", c9, w_mod, b_shard)
    mod_all = _allgather_small("ag_mod", mod_part.reshape(L * MOD_ROWS, MOD_SHARD))
    mod_all = jnp.transpose(mod_all.reshape(N_DEV, L, MOD_ROWS, MOD_SHARD), (1, 2, 0, 3)).reshape(L, MOD_ROWS, 6 * D)
    own_row = lax.dynamic_index_in_dim(mod_all, me, axis=1, keepdims=False)
    modrows = jnp.stack([mod_all[:, N_DEV], own_row], axis=1)

    shards = [{k: (P[src][l].T if tr else P[src][l]).astype(BF16) for k, src, tr in zip(BIG, BIG_SRC, BIG_T)}
              for l in range(L)]
    Ws = []
    for l in range(L):
        W = {}
        W.update(
            cw=conv_w_f[l], cb=conv_b[l][None],
            w4=jnp.concatenate([lru_wa[l, 0], lru_wa[l, 1], lru_wx[l, 0], lru_wx[l, 1]], axis=-1).astype(BF16),
            b4=jnp.concatenate([lru_ba_f[l, 0].reshape(N_RNN_BLOCKS, 1, RB), lru_ba_f[l, 1].reshape(N_RNN_BLOCKS, 1, RB),
                                lru_bx_f[l, 0].reshape(N_RNN_BLOCKS, 1, RB), lru_bx_f[l, 1].reshape(N_RNN_BLOCKS, 1, RB)],
                               axis=-1),
            lam=lru_lam_f[l], sink4=jnp.broadcast_to(attn_sink[l].reshape(N_KV, Q_PER_KV, 1), (N_KV, Q_PER_KV, HEAD)),
            g_mix_pre=g_mix_pre[l][None], g_mix_post=g_mix_post[l][None], g_ffn_pre=g_ffn_pre[l][None],
            g_ffn_post=g_ffn_post[l][None], mod=modrows[l])
        Ws.append(W)

    plan = _Plan(shards, Ws)
    sq, dxa, Gs = _local_step(ctx[0], x[0], loss_target[0], Ws, S, plan)
    loss_part = ((0.5 / D) * jnp.sum(sq)).reshape(1, 1)
    grad_x = dxa[CTX:][None]

    dmod = jnp.concatenate([Gs[l]["mod"] for l in range(L)] + [jnp.zeros((8 - 2 * L, 6 * D), F32)], axis=0)
    dmod_all = _allgather_small("ag_dmod", dmod)
    dmod_cols = lax.dynamic_slice_in_dim(dmod_all, me * MOD_SHARD, MOD_SHARD, axis=2)
    g_w_mod, g_b_mod, dsc_part = _mod_bwd("mod_bwd", c9, w_mod, dmod_all, dmod_cols)
    g_b_mod = g_b_mod[:, 0]

    def rows(name, shape):
        return jnp.concatenate([Gs[l][name].reshape(shape) for l in range(L)], axis=0)

    b4g = [Gs[l]["b4"].reshape(N_RNN_BLOCKS, 4, RB) for l in range(L)]
    sink_row = jnp.concatenate([Gs[l]["sink4"][:, :, 0].reshape(1, N_Q) for l in range(L)]
                               + [loss_part, jnp.zeros((1, D - L * N_Q - 1), F32)], axis=1)
    small_g = jnp.concatenate(
        [rows("g_mix_pre", (1, D)), rows("g_mix_post", (1, D)), rows("g_ffn_pre", (1, D)), rows("g_ffn_post", (1, D)),
         rows("cb", (1, D)), rows("cw", (CONV_W, D))]
        + [b4g[l][:, d].reshape(1, D) for l in range(L) for d in range(2)]
        + [b4g[l][:, 2 + d].reshape(1, D) for l in range(L) for d in range(2)]
        + [rows("lam", (2, D)), sink_row, dsc_part], axis=0)
    n_small = small_g.shape[0]
    small_tot = _sum_blocks("sum_small", _allgather_small("ag_small_grads", small_g))
    o = 0
    G = {}
    for name in ("g_mix_pre", "g_mix_post", "g_ffn_pre", "g_ffn_post", "conv_b"):
        G[name] = small_tot[o:o + L]
        o += L
    G["conv_w"] = small_tot[o:o + L * CONV_W].reshape(L, CONV_W, D)
    o += L * CONV_W
    for name in ("lru_ba", "lru_bx", "lru_lam"):
        G[name] = small_tot[o:o + 2 * L].reshape(L, 2, D)
        o += 2 * L
    G["attn_sink"] = small_tot[o, :L * N_Q].reshape(L, N_Q)
    loss = small_tot[o, L * N_Q]
    sg = jax.nn.sigmoid(c_ctx)
    G["c_ctx"] = small_tot[o + 1] * (sg * (1.0 + c_ctx * (1.0 - sg)))
    G["b_mod"] = g_b_mod
    G["w_mod"] = g_w_mod

    for l in range(L):
        plan.slots[l]["win_t"] = [plan.slots[l]["win_t_a"], plan.slots[l]["win_t_b"]]

    out_g, out_d, out_m, out_v = {}, {}, {}, {}

    def put(name, res, shape=None):
        g, d, m, v = res
        for dst, val in ((out_g, g), (out_d, d), (out_m, m), (out_v, v)):
            dst[name] = val if shape is None else val.reshape(shape)

    for k, src, tr, tile in zip(BIG, BIG_SRC, BIG_T, BIG_TILE):
        lay = (lambda a: jnp.swapaxes(a, 1, 2)) if tr else (lambda a: a)
        wmv = (lay(P[src]), lay(Mo[src]), lay(Vo[src]))
        res = _adamw_slots("adamw_" + src, [plan.slots[l][k] for l in range(L)], wmv[0].shape, tile, wmv)
        put(src, [lay(r) for r in res])
    res = _adamw("adamw_w_mod", w_mod.reshape(L * D, MOD_SHARD), g_w_mod.reshape(L * D, MOD_SHARD),
                 m_w_mod.reshape(L * D, MOD_SHARD), v_w_mod.reshape(L * D, MOD_SHARD), 256)
    put("w_mod", (g_w_mod,) + tuple(res), w_mod.shape)
    def fuse4(wa, wx):
        return jnp.concatenate([wa[:, 0], wa[:, 1], wx[:, 0], wx[:, 1]], axis=-1).reshape(L, N_RNN_BLOCKS * RB, 4 * RB)

    res = _adamw_slots("adamw_gates", plan.gate_slots, (L, N_RNN_BLOCKS * RB, 4 * RB), 256,
                       (fuse4(lru_wa, lru_wx), fuse4(m_lru_wa, m_lru_wx), fuse4(v_lru_wa, v_lru_wx)))
    res = [r.reshape(L, N_RNN_BLOCKS, RB, 4, RB) for r in res]
    put("lru_wa", [jnp.stack([r[:, :, :, 0], r[:, :, :, 1]], axis=1) for r in res])
    put("lru_wx", [jnp.stack([r[:, :, :, 2], r[:, :, :, 3]], axis=1) for r in res])
    rep = ("g_mix_pre", "g_mix_post", "g_ffn_pre", "g_ffn_post", "conv_b", "b_mod")

    def pack_rep(T_):
        sink = jnp.concatenate([T_["attn_sink"].reshape(1, L * N_Q), jnp.zeros((1, D - L * N_Q), F32)], axis=1)
        return jnp.concatenate([T_[n].reshape(-1, D) for n in rep] + [sink, T_["c_ctx"][None]], axis=0)

    pk = [pack_rep(T_) for T_ in (P, G, Mo, Vo)]
    n_rep = pk[0].shape[0]
    res = _adamw("adamw_replicated", *[jnp.pad(a, ((0, 24 - n_rep), (0, 0))) for a in pk], 24)
    res = (pk[1],) + tuple(r[:n_rep] for r in res)
    o = 0
    for n in rep:
        k = P[n].size // D
        put(n, [r[o:o + k] for r in res], P[n].shape)
        o += k
    put("attn_sink", [r[o, :L * N_Q] for r in res], attn_sink.shape)
    put("c_ctx", [r[o + 1] for r in res], c_ctx.shape)
    chan = ("conv_w", "lru_ba", "lru_bx", "lru_lam")
    g_own = {n: lax.dynamic_slice_in_dim(G[n], me * RB, RB, axis=2) for n in chan}

    def pack_chan(T_):
        return jnp.concatenate([T_[n].reshape(-1, RB) for n in chan], axis=0)

    pk = [pack_chan(T_) for T_ in (P, g_own, Mo, Vo)]
    n_ch = pk[0].shape[0]
    res = _adamw("adamw_channels", *[jnp.pad(a, ((0, 24 - n_ch), (0, 0))) for a in pk], 24)
    res = (pk[1],) + tuple(r[:n_ch] for r in res)
    o = 0
    for n in chan:
        k = P[n].size // RB
        put(n, [r[o:o + k] for r in res], P[n].shape)
        o += k

    order = ("c_ctx", "w_mod", "b_mod", "g_mix_pre", "g_mix_post", "g_ffn_pre", "g_ffn_post", "w_in", "conv_w", "conv_b",
             "lru_wa", "lru_ba", "lru_wx", "lru_bx", "lru_lam", "attn_sink", "w_o_rnn", "w_o_attn", "w_out", "w_ffn_in",
             "w_ffn_out")
    return (loss, grad_x, *[out_g[n] for n in order], *[out_d[n] for n in order], *[out_m[n] for n in order],
            *[out_v[n] for n in order])
```

```python
import functools
import math

import numpy as np
import jax
import jax.numpy as jnp
from jax import lax
from jax.experimental import pallas as pl
from jax.experimental.pallas import tpu as pltpu

F32 = jnp.float32
BF16 = jnp.bfloat16

D = 1024
CTX = 256
TR = 256
HEAD = 128
N_Q = 8
N_KV = 2
Q_PER_KV = N_Q // N_KV
GRID_W = 64
N_FREQ = HEAD // 4
ROPE_BASE = 10000.0
N_RNN_BLOCKS = 8
CONV_W = 4
CONV_LEFT = 2
LRU_C = 8.0
D_FF = 2816
IN_W = 5632
P_W = IN_W
DP_W = 3584
COL_XR, COL_GR, COL_Q, COL_K, COL_V, COL_GL = 0, 1024, 2048, 3072, 3328, 3584
GLB = 512
EPS = 1e-6
NEG_INF = -1e30
ATT_SCALE = HEAD ** -0.5
N_DEV = 8
VMEM_LIMIT = 56 * 1024 * 1024

ADAM_LR, ADAM_B1, ADAM_B2, ADAM_EPS, ADAM_WD, ADAM_STEP = 0.001, 0.9, 0.999, 1e-08, 0.01, 10

NN = (((1,), (0,)), ((), ()))
NT = (((1,), (1,)), ((), ()))
TN = (((0,), (0,)), ((), ()))


def _dot(a, b, dims=NN):
    return lax.dot_general(a, b, dims, preferred_element_type=F32)


def _params(sem=("arbitrary",)):
    return pltpu.CompilerParams(dimension_semantics=sem, vmem_limit_bytes=VMEM_LIMIT)


def _full_spec(shape):
    nd = len(shape)
    return pl.BlockSpec(shape, lambda *_: (0,) * nd)


ANY = pl.BlockSpec(memory_space=pl.ANY)


def _ew(name, body, n, row_ins, pars, row_outs, accs=(), alias=None):
    n_ri, n_p, n_ro, n_acc = len(row_ins), len(pars), len(row_outs), len(accs)

    def kern(*refs):
        i = pl.program_id(0)
        ins = refs[:n_ri]
        ps = refs[n_ri:n_ri + n_p]
        outs = refs[n_ri + n_p:n_ri + n_p + n_ro]
        acc = refs[n_ri + n_p + n_ro:]
        if n_acc:
            @pl.when(i == 0)
            def _():
                for a in acc:
                    a[...] = jnp.zeros(a.shape, a.dtype)
        body(i, ins, ps, outs, acc)

    in_specs = [ANY if blk is None else pl.BlockSpec(blk, imap) for (_, blk, imap) in row_ins]
    in_specs += [_full_spec(p.shape) for p in pars]
    out_specs = [pl.BlockSpec(blk, imap) for (_, blk, imap) in row_outs] + [_full_spec(a.shape) for a in accs]
    out_shape = [s for (s, _, _) in row_outs] + list(accs)
    return pl.pallas_call(
        kern, name=name, grid=(n,), in_specs=in_specs, out_specs=out_specs, out_shape=out_shape,
        input_output_aliases=alias or {}, compiler_params=_params(),
    )(*[a for (a, _, _) in row_ins], *pars)


def _rowblk(width, colblk=0, roff=0, tile=TR):
    return (tile, width), (lambda i: (i + roff, colblk))


def _sds(shape, dtype):
    return jax.ShapeDtypeStruct(shape, dtype)


class _Carry:
    SAME_CORE = (1, 3, 5)

    def __init__(self, jobs):
        self.jobs = list(jobs)
        self.arrays = [a for _, a in self.jobs]
        self.out_shapes = [_sds(a.shape if kind == "scatter" else (N_DEV, *a.shape), a.dtype) for kind, a in self.jobs]
        n = len(self.jobs)
        self.scratch = [pltpu.SemaphoreType.DMA((n, 7)), pltpu.SemaphoreType.DMA((n, 7)), pltpu.SemaphoreType.DMA((n,))]

    def _setup(self, sems):
        send_sems, recv_sems, local_sems = sems
        x, y, c = _place()
        me = _lin(x, y, c)
        peers = [(x ^ ((k + 1) >> 2 & 1), y ^ ((k + 1) >> 1 & 1), c ^ ((k + 1) & 1)) for k in range(7)]

        def copy(a, k, sem_k, src, dst):
            return pltpu.make_async_remote_copy(src_ref=src, dst_ref=dst, send_sem=send_sems.at[a, sem_k],
                                                recv_sem=recv_sems.at[a, sem_k], device_id=peers[k], device_id_type=MESH)

        return me, [_lin(*p) for p in peers], copy, local_sems

    def _local(self, a, kind, ins, outs, me, local_sems):
        return pltpu.make_async_copy(ins[a].at[me] if kind == "scatter" else ins[a], outs[a].at[me], local_sems.at[a])

    def start(self, ins, outs, sems):
        me, theirs, copy, local_sems = self._setup(sems)
        for a, (kind, _) in enumerate(self.jobs):
            self._local(a, kind, ins, outs, me, local_sems).start()
            if kind == "scatter":
                for k in range(7):
                    copy(a, k, k, ins[a].at[theirs[k]], outs[a].at[me]).start()
            else:
                for k in (0,) + self.SAME_CORE:
                    copy(a, k, k, ins[a], outs[a].at[me]).start()

    def wait(self, ins, outs, sems):
        me, theirs, copy, local_sems = self._setup(sems)
        for a, (kind, _) in enumerate(self.jobs):
            if kind == "scatter":
                for k in range(7):
                    copy(a, k, k, ins[a].at[me], outs[a].at[theirs[k]]).wait_recv()
                for k in range(7):
                    copy(a, k, k, ins[a].at[theirs[k]], outs[a].at[me]).wait_send()
            else:
                for k in self.SAME_CORE:
                    blk = outs[a].at[theirs[k]]
                    copy(a, k, k, ins[a], blk).wait_recv()
                    copy(a, 0, k + 1, blk, blk).start()
                copy(a, 0, 0, ins[a], outs[a].at[theirs[0]]).wait_recv()
                for k in self.SAME_CORE:
                    copy(a, 0, k + 1, ins[a], outs[a].at[theirs[k + 1]]).wait_recv()
                for k in (0,) + self.SAME_CORE:
                    copy(a, k, k, ins[a], outs[a].at[me]).wait_send()
                for k in self.SAME_CORE:
                    blk = outs[a].at[theirs[k]]
                    copy(a, 0, k + 1, blk, blk).wait_send()
            self._local(a, kind, ins, outs, me, local_sems).wait()


def _carried(kern, carry, n_in, n_out, first, last):
    if carry is None:
        return kern
    nc = len(carry.jobs)

    def wrapped(*refs):
        ins, cin = refs[:n_in], refs[n_in:n_in + nc]
        outs, cout = refs[n_in + nc:n_in + nc + n_out], refs[n_in + nc + n_out:n_in + 2 * nc + n_out]
        scr, sems = refs[n_in + 2 * nc + n_out:-3], refs[-3:]

        @pl.when(first())
        def _():
            carry.start(cin, cout, sems)

        kern(*ins, *outs, *scr)

        @pl.when(last())
        def _():
            carry.wait(cin, cout, sems)

    return wrapped


def _carry_args(carry):
    if carry is None:
        return [], [], [], [], []
    n = len(carry.jobs)
    return [ANY] * n, carry.arrays, [ANY] * n, carry.out_shapes, carry.scratch


def _grid_ends(dims):
    first = lambda: functools.reduce(jnp.logical_and, [pl.program_id(d) == 0 for d in range(len(dims))])
    last = lambda: functools.reduce(jnp.logical_and, [pl.program_id(d) == n - 1 for d, n in enumerate(dims)])
    return first, last


def _mm_call(name, a, b, mode, out_dtype, tm, tn, rows_outer=True, single_b=False, carry=None):
    if mode == "TN":
        (K, M), N = a.shape, b.shape[1]
    else:
        (M, K), N = a.shape, (b.shape[1] if mode == "NN" else b.shape[0])
    assert M % tm == 0 and N % tn == 0, (name, M, N, K, tm, tn)
    ij = (lambda g0, g1: (g0, g1)) if rows_outer else (lambda g0, g1: (g1, g0))
    grid = (M // tm, N // tn) if rows_outer else (N // tn, M // tm)
    if mode == "TN":
        a_spec = pl.BlockSpec((K, tm), lambda g0, g1: (0, ij(g0, g1)[0]))
    else:
        a_spec = pl.BlockSpec((tm, K), lambda g0, g1: (ij(g0, g1)[0], 0))
    b_blk, b_map = ((tn, K), lambda g0, g1: (ij(g0, g1)[1], 0)) if mode == "NT" else \
                   ((K, tn), lambda g0, g1: (0, ij(g0, g1)[1]))
    b_spec = pl.BlockSpec(b_blk, b_map, pipeline_mode=pl.Buffered(1)) if single_b else pl.BlockSpec(b_blk, b_map)
    dims = {"NN": NN, "NT": NT, "TN": TN}[mode]

    def kern(a_ref, b_ref, o_ref):
        o_ref[...] = _dot(a_ref[...], b_ref[...], dims).astype(o_ref.dtype)

    ci, ca, co, cs, cscr = _carry_args(carry)
    res = pl.pallas_call(
        _carried(kern, carry, 2, 1, *_grid_ends(grid)), name=name, grid=grid, in_specs=[a_spec, b_spec] + ci,
        out_specs=[pl.BlockSpec((tm, tn), lambda g0, g1: ij(g0, g1))] + co,
        out_shape=[_sds((M, N), out_dtype)] + cs, scratch_shapes=cscr,
        compiler_params=_params(("arbitrary", "arbitrary")),
    )(a, b, *ca)
    return res[0] if carry is None else (res[0], res[1:])


def _mm_act(name, a, w, mode, out_dtype=BF16, carry=None):
    rows, K = a.shape
    N = w.shape[1] if mode == "NN" else w.shape[0]
    if K > D_FF:
        return _mm_call(name, a, w, mode, out_dtype, rows // 8, N, single_b=True, carry=carry)
    tn = N if N <= 1024 else 1408
    return _mm_call(name, a, w, mode, out_dtype, rows // 4, tn, carry=carry)


def _mm_wgrad(name, x, dy, out_dtype=BF16, carry=None):
    M = x.shape[1]
    tm = 1408 if M == D_FF else 512
    return _mm_call(name, x, dy, "TN", out_dtype, tm, dy.shape[1], single_b=True, carry=carry)


def _sigmoid(x):
    return 0.5 * jnp.tanh(0.5 * x) + 0.5


def _silu(x):
    return x * _sigmoid(x)


def _silu_grad(x):
    s = _sigmoid(x)
    return s * (1.0 + x * (1.0 - s))


_GELU_K = math.sqrt(2.0 / math.pi)


def _gelu(x):
    return 0.5 * x * (1.0 + jnp.tanh(_GELU_K * (x + 0.044715 * x * x * x)))


def _gelu_grad(x):
    t = jnp.tanh(_GELU_K * (x + 0.044715 * x * x * x))
    return 0.5 * (1.0 + t) + 0.5 * x * (1.0 - t * t) * _GELU_K * (1.0 + 3.0 * 0.044715 * x * x)


def _log_sigmoid(x):
    return jnp.minimum(x, 0.0) - jnp.log(1.0 + jnp.exp(-jnp.abs(x)))


def _rms(x):
    x = x.astype(F32)
    r = lax.rsqrt(jnp.mean(x * x, axis=-1, keepdims=True) + EPS)
    return x * r, r


def _rms_bwd(dy, y, r):
    return r * (dy - y * jnp.mean(dy * y, axis=-1, keepdims=True))


def _modrow(mod_ref, i, chunk):
    lo = mod_ref[0:1, chunk * D:(chunk + 1) * D]
    hi = mod_ref[1:2, chunk * D:(chunk + 1) * D]
    return jnp.where(i == 0, lo, hi)


def _acc_seg(acc_ref, i, val):
    zero = jnp.zeros_like(val)
    acc_ref[0:1, :] += jnp.where(i == 0, val, zero)
    acc_ref[1:2, :] += jnp.where(i == 0, zero, val)


def _colsum(x):
    return jnp.sum(x, axis=0, keepdims=True)


SH1, SC1, GA1, SH2, SC2, GA2 = range(6)


def _normmod_fwd(name, ctx, x, g, mod, c_sh, c_sc, carry=None):
    T = ctx.shape[0] + x.shape[0]
    assert ctx.shape[0] == TR
    n = T // TR

    def kern(ctx_ref, x_ref, g_ref, mod_ref, xa_ref, h_ref):
        i = pl.program_id(0)
        v = jnp.where(i == 0, ctx_ref[...], x_ref[...])
        xa_ref[...] = v
        y, _ = _rms(v)
        h = (y * g_ref[...]) * (1.0 + _modrow(mod_ref, i, c_sc)) + _modrow(mod_ref, i, c_sh)
        h_ref[...] = h.astype(BF16)

    row = pl.BlockSpec((TR, D), lambda i: (i, 0))
    ci, ca, co, cs, cscr = _carry_args(carry)
    res = pl.pallas_call(
        _carried(kern, carry, 4, 2, *_grid_ends((n,))), name=name, grid=(n,),
        in_specs=[pl.BlockSpec((TR, D), lambda i: (0, 0)), pl.BlockSpec((TR, D), lambda i: (jnp.maximum(i - 1, 0), 0)),
                  _full_spec(g.shape), _full_spec(mod.shape)] + ci,
        out_specs=[row, row] + co, out_shape=[_sds((T, D), F32), _sds((T, D), BF16)] + cs, scratch_shapes=cscr,
        compiler_params=_params(),
    )(ctx, x, g, mod, *ca)
    return res if carry is None else (res[:2], res[2:])


def _modrows(mod_ref, row0, n, chunk):
    t = row0 + lax.broadcasted_iota(jnp.int32, (n, 1), 0)
    return jnp.where(t < CTX, mod_ref[0:1, chunk * D:(chunk + 1) * D], mod_ref[1:2, chunk * D:(chunk + 1) * D])


def _loss_resid_bwd(name, x_out, target, mat, gpost, mod, c_ga):
    T = x_out.shape[0]

    def body(i, ins, ps, outs, acc):
        err = ins[0][...] - ins[1][...]
        lat = i > 0
        dx = jnp.where(lat, err * (1.0 / D), 0.0)
        outs[0][...] = dx
        acc[2][...] += jnp.where(lat, _colsum(err * err), 0.0)
        outs[1][...] = _resid_bwd_vals(i, dx, ins[2][...], ps[0][...], ps[1], c_ga, acc[0], acc[1]).astype(BF16)

    tgt_blk = ((TR, D), lambda i: (jnp.maximum(i - 1, 0), 0))
    return _ew(name, body, T // TR, [(x_out, *_rowblk(D)), (target, *tgt_blk), (mat, *_rowblk(D))], [gpost, mod],
               [(_sds((T, D), F32), *_rowblk(D)), (_sds((T, D), BF16), *_rowblk(D))],
               [_sds((2, D), F32), _sds((1, D), F32), _sds((1, D), F32)])


def _mod_for(mod_ref, i, chunk, row0, n):
    return _modrow(mod_ref, i, chunk) if row0 is None else _modrows(mod_ref, row0, n, chunk)


def _acc_for(acc_ref, i, v, row0):
    if row0 is None:
        _acc_seg(acc_ref, i, _colsum(v))
        return

    @pl.when(row0 < CTX)
    def _():
        is_ctx = row0 + lax.broadcasted_iota(jnp.int32, (v.shape[0], 1), 0) < CTX
        acc_ref[0:1, :] += _colsum(jnp.where(is_ctx, v, 0.0))
        acc_ref[1:2, :] += _colsum(jnp.where(is_ctx, 0.0, v))

    @pl.when(row0 >= CTX)
    def _():
        acc_ref[1:2, :] += _colsum(v)


def _resid_bwd_vals(i, dout, mat, gpost, mod_ref, c_ga, acc_ga, acc_g, row0=None):
    ym, rm = _rms(mat)
    ga = _mod_for(mod_ref, i, c_ga, row0, dout.shape[0])
    _acc_for(acc_ga, i, dout * (ym * gpost), row0)
    dn = dout * ga
    acc_g[...] += _colsum(dn * ym)
    return _rms_bwd(dn * gpost, ym, rm)


def _normmod_bwd_vals(i, dh, xin, g, mod_ref, c_sh, c_sc, acc_sh, acc_sc, acc_g, row0=None):
    dh = dh.astype(F32)
    y, r = _rms(xin)
    _acc_for(acc_sc, i, dh * (y * g), row0)
    _acc_for(acc_sh, i, dh, row0)
    dyg = dh * (1.0 + _mod_for(mod_ref, i, c_sc, row0, dh.shape[0]))
    acc_g[...] += _colsum(dyg * y)
    return _rms_bwd(dyg * g, y, r)


def _parts(i, tm):
    return [(slice(0, tm), i * tm)]


FT = 1408


def _ffn_in_fused(name, h2, w_t, carry=None):
    T = h2.shape[0]
    tm, nj = T // 4, D_FF // FT

    def kern(a_ref, bg_ref, bu_ref, fg_ref, fu_ref, s_ref):
        for rows, _ in _parts(0, tm):
            a = a_ref[rows, :]
            g = _dot(a, bg_ref[...], NT)
            u = _dot(a, bu_ref[...], NT)
            fg_ref[rows, :] = g.astype(BF16)
            fu_ref[rows, :] = u.astype(BF16)
            s_ref[rows, :] = (_silu(g) * u).astype(BF16)

    o_spec = pl.BlockSpec((tm, FT), lambda i, j: (i, j))
    ci, ca, co, cs, cscr = _carry_args(carry)
    res = pl.pallas_call(
        _carried(kern, carry, 3, 3, *_grid_ends((4, nj))), name=name, grid=(4, nj),
        in_specs=[pl.BlockSpec((tm, D), lambda i, j: (i, 0)), pl.BlockSpec((FT, D), lambda i, j: (j, 0)),
                  pl.BlockSpec((FT, D), lambda i, j: (j + nj, 0))] + ci,
        out_specs=[o_spec] * 3 + co, out_shape=[_sds((T, D_FF), BF16)] * 3 + cs, scratch_shapes=cscr,
        compiler_params=_params(("arbitrary", "arbitrary")),
    )(h2, w_t, w_t, *ca)
    return res if carry is None else (res[:3], res[3:])


def _norm_chain(row0, xin, mat, gpost, mod_ref, c_ga, gnext, modn_ref, c_sh, c_sc):
    n = xin.shape[0]
    ym, _ = _rms(mat.astype(BF16))
    xo = xin + _modrows(mod_ref, row0, n, c_ga) * (ym * gpost)
    y, _ = _rms(xo)
    h = (y * gnext) * (1.0 + _modrows(modn_ref, row0, n, c_sc)) + _modrows(modn_ref, row0, n, c_sh)
    return xo, h.astype(BF16)


def _out_fused(name, p, u, o_all, xa, w_o_rnn, w_o_attn, w_out, gpost, mod, gnext):
    T = u.shape[0]
    tm = T // 8

    def kern(g0, g1, g2, g3, u_ref, o_ref, xa_ref, wr_ref, wa_ref, w_ref, gpost_ref, mod_ref, gnext_ref,
             ya_ref, yb_ref, z_ref, m_ref, x1_ref, h2_ref):
        for rows, row0 in _parts(pl.program_id(0), tm):
            ya = _dot(u_ref[rows, :], wr_ref[...]).astype(BF16)
            yb = _dot(o_ref[rows, :], wa_ref[...]).astype(BF16)
            ya_ref[rows, :] = ya
            yb_ref[rows, :] = yb
            ga = _sigmoid(jnp.concatenate([g0[rows, :], g1[rows, :]], axis=1).astype(F32))
            gb = _sigmoid(jnp.concatenate([g2[rows, :], g3[rows, :]], axis=1).astype(F32))
            z = (ga * ya.astype(F32) + gb * yb.astype(F32)).astype(BF16)
            z_ref[rows, :] = z
            m = _dot(z, w_ref[...])
            m_ref[rows, :] = m.astype(BF16)
            x1_ref[rows, :], h2_ref[rows, :] = _norm_chain(row0, xa_ref[rows, :], m, gpost_ref[...], mod_ref, GA1,
                                                           gnext_ref[...], mod_ref, SH2, SC2)

    row = lambda w: pl.BlockSpec((tm, w), lambda i: (i, 0))
    return pl.pallas_call(
        kern, name=name, grid=(T // tm,),
        in_specs=[pl.BlockSpec((tm, GLB), lambda i, q=q: (i, COL_GL // GLB + q)) for q in range(4)]
                 + [row(D), row(D), row(D)] + [_full_spec(a.shape) for a in (w_o_rnn, w_o_attn, w_out, gpost, mod, gnext)],
        out_specs=[row(D)] * 6,
        out_shape=[_sds((T, D), BF16)] * 4 + [_sds((T, D), F32), _sds((T, D), BF16)],
        compiler_params=_params(),
    )(p, p, p, p, u, o_all, xa, w_o_rnn, w_o_attn, w_out, gpost, mod, gnext)


def _ffn_out_fused(name, s, w, x1, gpost, mod, nxt=None):
    T = s.shape[0]
    tm = T // 8

    def kern(s_ref, w_ref, x1_ref, gpost_ref, mod_ref, *rest):
        for rows, row0 in _parts(pl.program_id(0), tm):
            e = _dot(s_ref[rows, :], w_ref[...])
            if nxt is None:
                e_ref, xo_ref = rest
                ym, _ = _rms(e.astype(BF16))
                xo_ref[rows, :] = x1_ref[rows, :] + _modrows(mod_ref, row0, e.shape[0], GA2) * (ym * gpost_ref[...])
            else:
                gnext_ref, modn_ref, e_ref, xo_ref, h_ref = rest
                xo_ref[rows, :], h_ref[rows, :] = _norm_chain(row0, x1_ref[rows, :], e, gpost_ref[...], mod_ref, GA2,
                                                              gnext_ref[...], modn_ref, SH1, SC1)
            e_ref[rows, :] = e.astype(BF16)

    row = lambda w_: pl.BlockSpec((tm, w_), lambda i: (i, 0))
    extra = [] if nxt is None else list(nxt)
    return pl.pallas_call(
        kern, name=name, grid=(T // tm,),
        in_specs=[row(D_FF), _full_spec(w.shape), row(D), _full_spec(gpost.shape), _full_spec(mod.shape)]
                 + [_full_spec(a.shape) for a in extra],
        out_specs=[row(D)] * (2 if nxt is None else 3),
        out_shape=[_sds((T, D), BF16), _sds((T, D), F32)] + ([] if nxt is None else [_sds((T, D), BF16)]),
        compiler_params=_params(),
    )(s, w, x1, gpost, mod, *extra)


def _ffn_bwd_fused(name, fg, fu, w, de=None, head=None):
    T = fg.shape[0]
    tm = T // 8
    row = lambda w_: pl.BlockSpec((tm, w_), lambda i: (i, 0))
    w_spec = pl.BlockSpec(w.shape, lambda i: (0, 0), pipeline_mode=pl.Buffered(1))

    def tail(rows, de_v, fg_ref, fu_ref, w_ref, df_ref):
        ds = _dot(de_v, w_ref[...], NT)
        g, u = fg_ref[rows, :].astype(F32), fu_ref[rows, :].astype(F32)
        df_ref[rows, :] = jnp.concatenate([ds * u * _silu_grad(g), ds * _silu(g)], axis=1).astype(BF16)

    if head is None:
        def kern(de_ref, fg_ref, fu_ref, w_ref, df_ref):
            for rows, _ in _parts(pl.program_id(0), tm):
                tail(rows, de_ref[rows, :], fg_ref, fu_ref, w_ref, df_ref)

        return pl.pallas_call(
            kern, name=name, grid=(T // tm,), in_specs=[row(D), row(D_FF), row(D_FF), w_spec],
            out_specs=[row(2 * D_FF)], out_shape=[_sds((T, 2 * D_FF), BF16)], compiler_params=_params(),
        )(de, fg, fu, w)

    dx2, e, gpost, mod = head

    def kern(dx_ref, e_ref, fg_ref, fu_ref, w_ref, gpost_ref, mod_ref, de_ref, df_ref, dga_ref, dg_ref):
        i = pl.program_id(0)

        @pl.when(i == 0)
        def _():
            dga_ref[...] = jnp.zeros(dga_ref.shape, F32)
            dg_ref[...] = jnp.zeros(dg_ref.shape, F32)

        for rows, row0 in _parts(i, tm):
            de_v = _resid_bwd_vals(i, dx_ref[rows, :], e_ref[rows, :], gpost_ref[...], mod_ref, GA2, dga_ref, dg_ref,
                                   row0=row0).astype(BF16)
            de_ref[rows, :] = de_v
            tail(rows, de_v, fg_ref, fu_ref, w_ref, df_ref)

    return pl.pallas_call(
        kern, name=name, grid=(T // tm,),
        in_specs=[row(D), row(D), row(D_FF), row(D_FF), w_spec, _full_spec(gpost.shape), _full_spec(mod.shape)],
        out_specs=[row(D), row(2 * D_FF), _full_spec((2, D)), _full_spec((1, D))],
        out_shape=[_sds((T, D), BF16), _sds((T, 2 * D_FF), BF16), _sds((2, D), F32), _sds((1, D), F32)],
        compiler_params=_params(),
    )(dx2, e, fg, fu, w, gpost, mod)


def _zero_at_start(i, refs):
    @pl.when(i == 0)
    def _():
        for r in refs:
            r[...] = jnp.zeros(r.shape, F32)


def _proj_bwd_fused(name, dp, dgl, w_in_t, xa, dx1, gpre, mod, latent_only=False, carry=None):
    T = dp.shape[0]
    tm = T // 8
    n = T // tm
    row = lambda w_: pl.BlockSpec((tm, w_), lambda i: (i, 0))

    def tile_vals(i, dp_ref, dgl_ref, w_ref, xa_ref, dx1_ref, g_ref, mod_ref, dsh_ref, dsc_ref, dg_ref):
        _zero_at_start(i, (dsh_ref, dsc_ref, dg_ref))
        dh = _dot(dp_ref[...], w_ref[0:DP_W, :]) + _dot(dgl_ref[...], w_ref[DP_W:, :])
        return dx1_ref[...] + _normmod_bwd_vals(i, dh, xa_ref[...], g_ref[...], mod_ref, SH1, SC1, dsh_ref, dsc_ref,
                                                dg_ref, row0=i * tm)

    def kern(dp_ref, dgl_ref, w_ref, xa_ref, dx1_ref, g_ref, mod_ref, dxa_ref, dsh_ref, dsc_ref, dg_ref):
        dxa_ref[...] = tile_vals(pl.program_id(0), dp_ref, dgl_ref, w_ref, xa_ref, dx1_ref, g_ref, mod_ref, dsh_ref,
                                 dsc_ref, dg_ref)

    def kern_latent(dp_ref, dgl_ref, w_ref, xa_ref, dx1_ref, g_ref, mod_ref, dx_hbm, dsh_ref, dsc_ref, dg_ref, buf, sems):
        i = pl.program_id(0)

        def send(t, slot):
            head = pltpu.make_async_copy(buf.at[slot, pl.ds(CTX, tm - CTX)], dx_hbm.at[pl.ds(0, tm - CTX)], sems.at[slot])
            start = pl.multiple_of(jnp.maximum(t * tm - CTX, 0), 8)
            body = pltpu.make_async_copy(buf.at[slot], dx_hbm.at[pl.ds(start, tm)], sems.at[slot])
            return head, body

        def each(t, slot, fn):
            head, body = send(t, slot)

            @pl.when(t == 0)
            def _():
                fn(head)

            @pl.when(t > 0)
            def _():
                fn(body)

        slot = i % 2
        buf[slot] = tile_vals(i, dp_ref, dgl_ref, w_ref, xa_ref, dx1_ref, g_ref, mod_ref, dsh_ref, dsc_ref, dg_ref)
        each(i, slot, lambda cp: cp.start())

        @pl.when(i > 0)
        def _():
            each(i - 1, 1 - slot, lambda cp: cp.wait())

        @pl.when(i == n - 1)
        def _():
            each(i, slot, lambda cp: cp.wait())

    assert tm > CTX and (tm - CTX) % 8 == 0
    ci, ca, co, cs, cscr = _carry_args(carry)
    own = [pltpu.VMEM((2, tm, D), F32), pltpu.SemaphoreType.DMA((2,))] if latent_only else []
    res = pl.pallas_call(
        _carried(kern_latent if latent_only else kern, carry, 7, 4, *_grid_ends((n,))), name=name, grid=(n,),
        in_specs=[row(DP_W), row(P_W - DP_W),
                  pl.BlockSpec(w_in_t.shape, lambda i: (0, 0), pipeline_mode=pl.Buffered(1)), row(D), row(D),
                  _full_spec(gpre.shape), _full_spec(mod.shape)] + ci,
        out_specs=[ANY if latent_only else row(D), _full_spec((2, D)), _full_spec((2, D)), _full_spec((1, D))] + co,
        out_shape=[_sds((T - CTX if latent_only else T, D), F32), _sds((2, D), F32), _sds((2, D), F32),
                   _sds((1, D), F32)] + cs,
        scratch_shapes=own + cscr, compiler_params=_params(),
    )(dp, dgl, w_in_t, xa, dx1, gpre, mod, *ca)
    return res if carry is None else (res[:4], res[4:])


def _proj_wgrad(name, dp, dgl, h, carry=None):
    T, N = h.shape
    n1, n2 = DP_W // GLB, (P_W - DP_W) // GLB

    def kern(a1_ref, a2_ref, h_ref, o_ref):
        i = pl.program_id(0)

        @pl.when(i < n1)
        def _():
            o_ref[...] = _dot(a1_ref[...], h_ref[...], TN).astype(o_ref.dtype)

        @pl.when(i >= n1)
        def _():
            o_ref[...] = _dot(a2_ref[...], h_ref[...], TN).astype(o_ref.dtype)

    ci, ca, co, cs, cscr = _carry_args(carry)
    res = pl.pallas_call(
        _carried(kern, carry, 3, 1, *_grid_ends((n1 + n2,))), name=name, grid=(n1 + n2,),
        in_specs=[pl.BlockSpec((T, GLB), lambda i: (0, jnp.minimum(i, n1 - 1))),
                  pl.BlockSpec((T, GLB), lambda i: (0, jnp.maximum(i - n1, 0))),
                  pl.BlockSpec((T, N), lambda i: (0, 0), pipeline_mode=pl.Buffered(1))] + ci,
        out_specs=[pl.BlockSpec((GLB, N), lambda i: (i, 0))] + co,
        out_shape=[_sds((P_W, N), BF16)] + cs, scratch_shapes=cscr, compiler_params=_params(),
    )(dp, dgl, h, *ca)
    return res[0] if carry is None else (res[0], res[1:])


def _ffn_in_bwd_fused(name, df, w_t, x1, dres, mat, gpre, mod, gpost, carry=None):
    T = df.shape[0]
    tm = T // 8
    row = lambda w_: pl.BlockSpec((tm, w_), lambda i: (i, 0))

    def kern(df_ref, w_ref, x1_ref, dres_ref, mat_ref, gpre_ref, mod_ref, gpost_ref,
             dx1_ref, dm_ref, dsh_ref, dsc_ref, dgpre_ref, dga_ref, dgpost_ref):
        i = pl.program_id(0)
        _zero_at_start(i, (dsh_ref, dsc_ref, dgpre_ref, dga_ref, dgpost_ref))
        for rows, row0 in _parts(i, tm):
            dh2 = _dot(df_ref[rows, :], w_ref[...])
            dx1 = dres_ref[rows, :] + _normmod_bwd_vals(i, dh2, x1_ref[rows, :], gpre_ref[...], mod_ref, SH2, SC2,
                                                        dsh_ref, dsc_ref, dgpre_ref, row0=row0)
            dx1_ref[rows, :] = dx1
            dm_ref[rows, :] = _resid_bwd_vals(i, dx1, mat_ref[rows, :], gpost_ref[...], mod_ref, GA1, dga_ref,
                                              dgpost_ref, row0=row0).astype(BF16)

    ci, ca, co, cs, cscr = _carry_args(carry)
    res = pl.pallas_call(
        _carried(kern, carry, 8, 7, *_grid_ends((T // tm,))), name=name, grid=(T // tm,),
        in_specs=[row(2 * D_FF), pl.BlockSpec(w_t.shape, lambda i: (0, 0), pipeline_mode=pl.Buffered(1)), row(D),
                  row(D), row(D), _full_spec(gpre.shape), _full_spec(mod.shape), _full_spec(gpost.shape)] + ci,
        out_specs=[row(D), row(D), _full_spec((2, D)), _full_spec((2, D)), _full_spec((1, D)), _full_spec((2, D)),
                   _full_spec((1, D))] + co,
        out_shape=[_sds((T, D), F32), _sds((T, D), BF16), _sds((2, D), F32), _sds((2, D), F32), _sds((1, D), F32),
                   _sds((2, D), F32), _sds((1, D), F32)] + cs,
        scratch_shapes=cscr, compiler_params=_params(),
    )(df, w_t, x1, dres, mat, gpre, mod, gpost, *ca)
    return res if carry is None else (res[:7], res[7:])


def _out_bwd_fused(name, dm, w_out, w_o_rnn, w_o_attn, p, ya, yb):
    T = dm.shape[0]
    tm = T // 8
    row = lambda w_: pl.BlockSpec((tm, w_), lambda i: (i, 0))

    def kern(dm_ref, w_ref, wr_ref, wa_ref, g0, g1, g2, g3, ya_ref, yb_ref, dya_ref, dyb_ref, dgl_ref, du_ref, do_ref):
        for rows, _ in _parts(pl.program_id(0), tm):
            dz = _dot(dm_ref[rows, :], w_ref[...], NT)
            ga = _sigmoid(jnp.concatenate([g0[rows, :], g1[rows, :]], axis=1).astype(F32))
            gb = _sigmoid(jnp.concatenate([g2[rows, :], g3[rows, :]], axis=1).astype(F32))
            dya = (dz * ga).astype(BF16)
            dyb = (dz * gb).astype(BF16)
            dya_ref[rows, :] = dya
            dyb_ref[rows, :] = dyb
            dgl_ref[rows, :] = jnp.concatenate([dz * ya_ref[rows, :].astype(F32) * ga * (1.0 - ga),
                                                dz * yb_ref[rows, :].astype(F32) * gb * (1.0 - gb)],
                                               axis=1).astype(BF16)
            du_ref[rows, :] = _dot(dya, wr_ref[...], NT).astype(BF16)
            do_ref[rows, :] = _dot(dyb, wa_ref[...], NT).astype(BF16)

    return pl.pallas_call(
        kern, name=name, grid=(T // tm,),
        in_specs=[row(D)] + [_full_spec(w.shape) for w in (w_out, w_o_rnn, w_o_attn)]
                 + [pl.BlockSpec((tm, GLB), lambda i, q=q: (i, COL_GL // GLB + q)) for q in range(4)] + [row(D), row(D)],
        out_specs=[row(D), row(D), row(2 * D), row(D), row(D)],
        out_shape=[_sds((T, D), BF16), _sds((T, D), BF16), _sds((T, 2 * D), BF16), _sds((T, D), BF16),
                   _sds((T, D), BF16)],
        compiler_params=_params(),
    )(dm, w_out, w_o_rnn, w_o_attn, p, p, p, p, ya, yb)


AB = 128
CTX_BLKS = CTX // AB


def _rope_tables(S):
    pos = jnp.arange(S, dtype=jnp.int32)
    inv = ROPE_BASE ** (-jnp.arange(N_FREQ, dtype=F32) / N_FREQ)
    ang_r = (pos // GRID_W).astype(F32)[:, None] * inv[None, :]
    ang_c = (pos % GRID_W).astype(F32)[:, None] * inv[None, :]
    cos = jnp.concatenate([jnp.cos(ang_r)] * 2 + [jnp.cos(ang_c)] * 2, axis=1)
    sin = jnp.concatenate([-jnp.sin(ang_r), jnp.sin(ang_r), -jnp.sin(ang_c), jnp.sin(ang_c)], axis=1)
    return cos, sin


def _rope(x, cos, sin):
    w = x.shape[1]
    reps = w // HEAD
    lane = lax.broadcasted_iota(jnp.int32, x.shape, 1)
    partner = jnp.where((lane & 63) < 32, pltpu.roll(x, w - 32, 1), pltpu.roll(x, 32, 1))
    return x * jnp.tile(cos, (1, reps)) + partner * jnp.tile(sin, (1, reps))


def _unrope(dx, cos, sin):
    w = dx.shape[1]
    reps = w // HEAD
    lane = lax.broadcasted_iota(jnp.int32, dx.shape, 1)
    t = dx * jnp.tile(sin, (1, reps))
    partner = jnp.where((lane & 63) < 32, pltpu.roll(t, w - 32, 1), pltpu.roll(t, 32, 1))
    return dx * jnp.tile(cos, (1, reps)) + partner


def _qkv_prep(name, p, cos, sin, S):
    T = CTX + S
    nt = T // TR
    cb = CTX // TR
    KW = N_KV * HEAD

    def with_ones(v):
        ones = jnp.ones((TR, HEAD), BF16)
        return jnp.concatenate([v[:, kh * HEAD:(kh + 1) * HEAD] if part == 0 else ones
                                for kh in range(N_KV) for part in range(2)], axis=1)

    def kern(q_ref, k_ref, v_ref, cos_ref, sin_ref, qa_ref, kp_ref, vp_ref, kc_ref, vc_ref):
        i = pl.program_id(0)
        cos_v, sin_v = cos_ref[...], sin_ref[...]
        @pl.when(i < cb)
        def _():
            qa_ref[...] = (q_ref[...].astype(F32) * ATT_SCALE).astype(BF16)
            kc_ref[...] = k_ref[...]
            vc_ref[...] = with_ones(v_ref[...])

        @pl.when((i < cb) | (i >= nt))
        def _():
            kp_ref[...] = jnp.zeros(kp_ref.shape, BF16)
            vp_ref[...] = jnp.zeros(vp_ref.shape, BF16)

        @pl.when((i >= cb) & (i < nt))
        def _():
            qa_ref[...] = (_rope(q_ref[...].astype(F32), cos_v, sin_v) * ATT_SCALE).astype(BF16)
            kp_ref[...] = _rope(k_ref[...].astype(F32), cos_v, sin_v).astype(BF16)
            vp_ref[...] = with_ones(v_ref[...])

    tok = lambda i: jnp.minimum(i, nt - 1)
    lat_map = lambda i: (jnp.clip(i - cb, 0, nt - cb - 1), 0)
    ctx_map = lambda i: (jnp.minimum(i, cb - 1), 0)
    return pl.pallas_call(
        kern, name=name, grid=(nt + cb,),
        in_specs=[pl.BlockSpec((TR, N_Q * HEAD), lambda i: (tok(i), COL_Q // (N_Q * HEAD))),
                  pl.BlockSpec((TR, KW), lambda i: (tok(i), COL_K // KW)),
                  pl.BlockSpec((TR, KW), lambda i: (tok(i), COL_V // KW)),
                  pl.BlockSpec((TR, HEAD), lat_map), pl.BlockSpec((TR, HEAD), lat_map)],
        out_specs=[pl.BlockSpec((TR, N_Q * HEAD), lambda i: (tok(i), 0)),
                   pl.BlockSpec((TR, KW), lambda i: (i, 0)), pl.BlockSpec((TR, 2 * KW), lambda i: (i, 0)),
                   pl.BlockSpec((TR, KW), ctx_map), pl.BlockSpec((TR, 2 * KW), ctx_map)],
        out_shape=[_sds((T, N_Q * HEAD), BF16), _sds((S + 2 * CTX, KW), BF16), _sds((S + 2 * CTX, 2 * KW), BF16),
                   _sds((CTX, KW), BF16), _sds((CTX, 2 * KW), BF16)],
        compiler_params=_params(),
    )(p, p, p, cos, sin)


GW = Q_PER_KV * HEAD
HG = Q_PER_KV


def _band_bias(S):
    r = jnp.arange(AB, dtype=jnp.int32)[:, None]
    c = jnp.arange(3 * AB, dtype=jnp.int32)[None, :]
    near = jnp.abs(c - AB - r) <= AB
    valid = jnp.stack([near & (c >= AB), near, near & (c < 2 * AB)])
    return jnp.where(valid, 0.0, NEG_INF).astype(F32)


def _bias_spec(S):
    nb = S // AB
    return pl.BlockSpec((None, AB, 3 * AB), lambda kh, n: (jnp.where(n == 0, 0, jnp.where(n == nb - 1, 2, 1)), 0, 0))


def _head_probs(q, sink, kc, vce, kb, vbe, bias):
    s_c = _dot(q, kc, NT)
    m = jnp.maximum(jnp.max(s_c, axis=-1, keepdims=True), sink)
    if kb is not None:
        s_b = _dot(q, kb, NT) + bias
        m = jnp.maximum(m, jnp.max(s_b, axis=-1, keepdims=True))
    p_c = jnp.exp(s_c - m).astype(BF16)
    acc = _dot(p_c, vce)
    p_b = None
    if kb is not None:
        p_b = jnp.exp(s_b - m).astype(BF16)
        acc = acc + _dot(p_b, vbe)
    return p_c, p_b, m, acc


def _attn_fwd(name, qa, kc, vc, sink4, S, band=None, prev=None, carry=None):
    T = qa.shape[0]
    has_band = band is not None
    nq = S // AB if has_band else CTX_BLKS
    q_off = CTX_BLKS if has_band else 0

    def kern(*refs):
        q_ref, kc_ref, vc_ref, sink_ref = refs[:4]
        rest = refs[4:]
        o_ref = rest[-1]
        n = pl.program_id(1)
        kc_v, vce = kc_ref[...], vc_ref[...]
        kb = vbe = bias = None
        if has_band:
            kp_ref, vp_ref, bias_ref = rest[:3]
            start = pl.multiple_of(n * AB + (CTX - AB), AB)
            kb = kp_ref[pl.ds(start, 3 * AB), :]
            vbe = vp_ref[pl.ds(start, 3 * AB), :]
            bias = bias_ref[...]
        outs = []
        for g in range(Q_PER_KV):
            sink = sink_ref[g:g + 1, 0:1]
            _, _, m, acc = _head_probs(q_ref[:, g * HEAD:(g + 1) * HEAD], sink, kc_v, vce, kb, vbe, bias)
            l = acc[:, HEAD:] + jnp.exp(sink - m)
            outs.append(acc[:, :HEAD] / l)
        o_ref[...] = jnp.concatenate(outs, axis=1).astype(BF16)

    in_specs = [pl.BlockSpec((AB, GW), lambda kh, n: (n + q_off, kh)),
                pl.BlockSpec((CTX, HEAD), lambda kh, n: (0, kh)), pl.BlockSpec((CTX, 2 * HEAD), lambda kh, n: (0, kh)),
                pl.BlockSpec((None, Q_PER_KV, HEAD), lambda kh, n: (kh, 0, 0))]
    args = [qa, kc, vc, sink4]
    if has_band:
        in_specs += [pl.BlockSpec((S + 2 * CTX, HEAD), lambda kh, n: (0, kh)),
                     pl.BlockSpec((S + 2 * CTX, 2 * HEAD), lambda kh, n: (0, kh)), _bias_spec(S)]
        args += list(band)
    alias = {}
    if prev is not None:
        in_specs.append(ANY)
        alias = {len(args): 0}
        args.append(prev)
    ci, ca, co, cs, cscr = _carry_args(carry)
    res = pl.pallas_call(
        _carried(kern, carry, len(args), 1, *_grid_ends((N_KV, nq))), name=name, grid=(N_KV, nq),
        in_specs=in_specs + ci,
        out_specs=[pl.BlockSpec((AB, GW), lambda kh, n: (n + q_off, kh))] + co,
        out_shape=[_sds((T, N_Q * HEAD), BF16)] + cs, input_output_aliases=alias, scratch_shapes=cscr,
        compiler_params=_params(("arbitrary", "arbitrary")),
    )(*args, *ca)
    return res[0] if carry is None else (res[0], res[1:])


def _attn_bwd(name, qa, kc, vc, sink4, o_all, do_all, S, band=None, prev_dq=None, carry=None):
    T = qa.shape[0]
    has_band = band is not None
    nq = S // AB if has_band else CTX_BLKS
    q_off = CTX_BLKS if has_band else 0
    KW = N_KV * HEAD

    def kern(*refs):
        q_ref, kc_ref, vc_ref, sink_ref, o_ref, do_ref = refs[:6]
        rest = refs[6:]
        if has_band:
            kp_ref, vp_ref, bias_ref, cos_ref, sin_ref = rest[:5]
            rest = rest[5:]
        if prev_dq is not None:
            rest = rest[1:]
        dq_ref, dkc_ref, dvc_ref, dsink_ref = rest[:4]
        n = pl.program_id(1)

        @pl.when(n == 0)
        def _():
            dkc_ref[...] = jnp.zeros(dkc_ref.shape, F32)
            dvc_ref[...] = jnp.zeros(dvc_ref.shape, F32)
            dsink_ref[...] = jnp.zeros(dsink_ref.shape, F32)
            if has_band:
                rest[4][...] = jnp.zeros(rest[4].shape, F32)
                rest[5][...] = jnp.zeros(rest[5].shape, F32)

        kc_v, vce = kc_ref[...], vc_ref[...]
        vc_v = vce[:, :HEAD]
        kb = vbe = vb = bias = None
        if has_band:
            start = pl.multiple_of(n * AB + (CTX - AB), AB)
            kb = kp_ref[pl.ds(start, 3 * AB), :]
            vbe = vp_ref[pl.ds(start, 3 * AB), :]
            vb = vbe[:, :HEAD]
            bias = bias_ref[...]
        dq_parts, dsink_parts = [], []
        for g0 in range(0, Q_PER_KV, HG):
            heads = range(g0, g0 + HG)
            stack = lambda ref: jnp.concatenate([ref[:, g * HEAD:(g + 1) * HEAD] for g in heads], axis=0)
            q4, do4 = stack(q_ref), stack(do_ref)
            sink = jnp.concatenate([jnp.broadcast_to(sink_ref[g:g + 1, 0:1], (AB, 1)) for g in heads], axis=0)
            s_c = _dot(q4, kc_v, NT)
            m = jnp.maximum(jnp.max(s_c, axis=-1, keepdims=True), sink)
            if has_band:
                s_b = _dot(q4, kb, NT) + jnp.tile(bias, (HG, 1))
                m = jnp.maximum(m, jnp.max(s_b, axis=-1, keepdims=True))
            p_c = jnp.exp(s_c - m).astype(BF16).astype(F32)
            p_sink = jnp.exp(sink - m)
            l = jnp.sum(p_c, axis=-1, keepdims=True) + p_sink
            if has_band:
                p_b = jnp.exp(s_b - m).astype(BF16).astype(F32)
                l = l + jnp.sum(p_b, axis=-1, keepdims=True)
            inv = 1.0 / l
            delta = jnp.sum(do4.astype(F32) * stack(o_ref).astype(F32), axis=-1, keepdims=True)
            do4b = do4.astype(BF16)
            pn_c = (p_c * inv).astype(BF16)
            ds_c = (p_c * inv * (_dot(do4b, vc_v, NT) - delta)).astype(BF16)
            dq4 = _dot(ds_c, kc_v)
            dkc_ref[...] += _dot(q4, ds_c, TN)
            dvc_ref[...] += _dot(do4b, pn_c, TN)
            if has_band:
                pn_b = (p_b * inv).astype(BF16)
                ds_b = (p_b * inv * (_dot(do4b, vb, NT) - delta)).astype(BF16)
                dq4 = dq4 + _dot(ds_b, kb)
                rest[4][:, pl.ds(start, 3 * AB)] += _dot(q4, ds_b, TN)
                rest[5][:, pl.ds(start, 3 * AB)] += _dot(do4b, pn_b, TN)
            dq4 = dq4 * ATT_SCALE
            dq_parts += [dq4[k * AB:(k + 1) * AB, :] for k in range(HG)]
            ps = p_sink * inv * delta
            dsink_parts += [jnp.broadcast_to(-jnp.sum(ps[k * AB:(k + 1) * AB, :], axis=0, keepdims=True), (1, HEAD))
                            for k in range(HG)]
        dq = jnp.concatenate(dq_parts, axis=1)
        dq_ref[...] = (_unrope(dq, cos_ref[...], sin_ref[...]) if has_band else dq).astype(BF16)
        dsink_ref[...] += jnp.concatenate(dsink_parts, axis=0)

    q_spec = pl.BlockSpec((AB, GW), lambda kh, n: (n + q_off, kh))
    c_spec = pl.BlockSpec((CTX, HEAD), lambda kh, n: (0, kh))
    ce_spec = pl.BlockSpec((CTX, 2 * HEAD), lambda kh, n: (0, kh))
    s_spec = pl.BlockSpec((None, Q_PER_KV, HEAD), lambda kh, n: (kh, 0, 0))
    in_specs = [q_spec, c_spec, ce_spec, s_spec, q_spec, q_spec]
    args = [qa, kc, vc, sink4, o_all, do_all]
    ct_spec = pl.BlockSpec((HEAD, CTX), lambda kh, n: (kh, 0))
    dq_spec = pl.BlockSpec((AB, GW), lambda kh, n: (n + q_off, COL_Q // GW + kh))
    out_specs = [dq_spec, ct_spec, ct_spec, s_spec]
    out_shape = [_sds((T, DP_W), BF16), _sds((KW, CTX), F32), _sds((KW, CTX), F32), _sds((N_KV, Q_PER_KV, HEAD), F32)]
    if has_band:
        p_spec = pl.BlockSpec((S + 2 * CTX, HEAD), lambda kh, n: (0, kh))
        pt_spec = pl.BlockSpec((HEAD, S + 2 * CTX), lambda kh, n: (kh, 0))
        rope_spec = pl.BlockSpec((AB, HEAD), lambda kh, n: (n, 0))
        in_specs += [p_spec, pl.BlockSpec((S + 2 * CTX, 2 * HEAD), lambda kh, n: (0, kh)), _bias_spec(S), rope_spec,
                     rope_spec]
        args += list(band)
        out_specs += [pt_spec, pt_spec]
        out_shape += [_sds((KW, S + 2 * CTX), F32)] * 2
    alias = {}
    if prev_dq is not None:
        in_specs.append(ANY)
        alias = {len(args): 0}
        args.append(prev_dq)
    ci, ca, co, cs, cscr = _carry_args(carry)
    n_out = len(out_specs)
    res = pl.pallas_call(
        _carried(kern, carry, len(args), n_out, *_grid_ends((N_KV, nq))), name=name, grid=(N_KV, nq),
        in_specs=in_specs + ci, out_specs=out_specs + co, out_shape=out_shape + cs, scratch_shapes=cscr,
        input_output_aliases=alias, compiler_params=_params(("arbitrary", "arbitrary")),
    )(*args, *ca)
    return res if carry is None else (res[:n_out], res[n_out:])


def _dkv_assemble(name, dp, dkp, dvp, dkc_l, dvc_l, dkc_c, dvc_c, cos, sin, S):
    T = CTX + S
    KW = N_KV * HEAD

    def kern(dkp_ref, dvp_ref, dkcl_ref, dvcl_ref, dkcc_ref, dvcc_ref, cos_ref, sin_ref, dp_in, out_ref):
        i = pl.program_id(0)

        @pl.when(i == 0)
        def _():
            out_ref[...] = jnp.concatenate([(dkcl_ref[...] + dkcc_ref[...]).T, (dvcl_ref[...] + dvcc_ref[...]).T],
                                           axis=1).astype(BF16)

        @pl.when(i > 0)
        def _():
            out_ref[...] = jnp.concatenate([_unrope(dkp_ref[...].T, cos_ref[...], sin_ref[...]), dvp_ref[...].T],
                                           axis=1).astype(BF16)

    same = lambda i: (0, i)
    lat_map = lambda i: (jnp.maximum(i - 1, 0), 0)
    ctx_map = lambda i: (0, 0)
    return pl.pallas_call(
        kern, name=name, grid=(T // TR,),
        in_specs=[pl.BlockSpec((KW, TR), same), pl.BlockSpec((KW, TR), same),
                  pl.BlockSpec((KW, CTX), ctx_map), pl.BlockSpec((KW, CTX), ctx_map),
                  pl.BlockSpec((KW, CTX), ctx_map), pl.BlockSpec((KW, CTX), ctx_map),
                  pl.BlockSpec((TR, HEAD), lat_map), pl.BlockSpec((TR, HEAD), lat_map), ANY],
        out_specs=pl.BlockSpec((TR, 2 * KW), lambda i: (i, COL_K // (2 * KW))),
        out_shape=_sds((T, DP_W), BF16), input_output_aliases={8: 0}, compiler_params=_params(),
    )(dkp, dvp, dkc_l, dvc_l, dkc_c, dvc_c, cos, sin, dp)


RB = 128
CH = 256
HALO = 8
SUB = 8
GRP = 8


def _vscan(a, b, reverse):
    row = lax.broadcasted_iota(jnp.int32, a.shape, 0)
    A, H = a, b
    for s in (1, 2, 4):
        sh = SUB - s if reverse else s
        m = (row < SUB - s) if reverse else (row >= s)
        As = pltpu.roll(A, sh, 0)
        Hs = pltpu.roll(H, sh, 0)
        H = jnp.where(m, A * Hs + H, H)
        A = jnp.where(m, A * As, A)
    return A, H


def _scan_rows(a_ref, b_ref, r0, nrows, reverse, carry, emit):
    ngrp = nrows // (SUB * GRP)
    row = lax.broadcasted_iota(jnp.int32, (SUB, RB), 0)

    def grp(gi, carry):
        g = (ngrp - 1 - gi) if reverse else gi
        base = r0 + g * (SUB * GRP)
        for v in (range(GRP - 1, -1, -1) if reverse else range(GRP)):
            rs = pl.multiple_of(base + v * SUB, SUB)
            A, H = _vscan(a_ref[pl.ds(rs, SUB), :], b_ref[pl.ds(rs, SUB), :], reverse)
            hf = H + A * carry
            if reverse:
                before = jnp.where(row == SUB - 1, carry, pltpu.roll(hf, SUB - 1, 0))
                carry = hf[0:1, :]
            else:
                before = jnp.where(row == 0, carry, pltpu.roll(hf, 1, 0))
                carry = hf[SUB - 1:SUB, :]
            emit(rs, hf, before)
        return carry

    return lax.fori_loop(0, ngrp, grp, carry)


def _pad_start(ci):
    return pl.multiple_of(ci * CH + HALO * jnp.minimum(ci, 1), HALO)


def _conv_taps(ext, transpose=False):
    n = CH + 2 * HALO
    taps = []
    for k in range(CONV_W):
        off = CONV_LEFT - k if transpose else k - CONV_LEFT
        taps.append(ext[HALO:HALO + CH, :] if off == 0 else pltpu.roll(ext, (-off) % n, 0)[HALO:HALO + CH, :])
    return taps


def _lru_gates(xl, w4, b4, ls):
    pre = _dot(xl.astype(BF16), w4) + b4
    out = []
    for d in range(2):
        r = _sigmoid(pre[:, d * RB:(d + 1) * RB])
        i = _sigmoid(pre[:, (2 + d) * RB:(3 + d) * RB])
        la = LRU_C * r * ls[d:d + 1, :]
        a = jnp.exp(la)
        q = -jnp.tanh(la) * (1.0 + a * a)
        out.append((r, i, a, q))
    return out


def _rnn_specs(T):
    col = lambda n, *_: (0, n)
    return dict(
        xr=pl.BlockSpec((T, RB), lambda n, *_: (0, COL_XR // RB + n)),
        gr=pl.BlockSpec((T, RB), lambda n, *_: (0, COL_GR // RB + n)),
        act=pl.BlockSpec((T, RB), col),
        cw=pl.BlockSpec((CONV_W, RB), col), cb=pl.BlockSpec((1, RB), col),
        w4=pl.BlockSpec((None, RB, 4 * RB), lambda n, *_: (n, 0, 0)),
        b4=pl.BlockSpec((None, 1, 4 * RB), lambda n, *_: (n, 0, 0)),
        lam=pl.BlockSpec((2, RB), col))


PAD_ROWS = 3 * HALO


def _zero_pads(pad_ref, T):
    for r in (0, HALO + CTX, 2 * HALO + T):
        pad_ref[r:r + HALO, :] = jnp.zeros((HALO, RB), F32)


def _fill_padded(pad_ref, src_ref, T):
    _zero_pads(pad_ref, T)
    pad_ref[HALO:HALO + CTX, :] = src_ref[0:CTX, :].astype(F32)
    pad_ref[2 * HALO + CTX:2 * HALO + T, :] = src_ref[CTX:T, :].astype(F32)


def _pad_rows(ci):
    return pl.ds(pl.multiple_of(ci * CH + HALO + HALO * jnp.minimum(ci, 1), HALO), CH)


def _rnn_fwd(name, p, cw, cb, w4, b4, lam, T, carry=None):
    def kern(xr_ref, gr_ref, cw_ref, cb_ref, w4_ref, b4_ref, lam_ref,
             u_ref, a0, a1, yo_ref, hpf_ref, hpb_ref, r0_ref, r1_ref, i0_ref, i1_ref, xpad, b0, b1, y):
        _fill_padded(xpad, xr_ref, T)
        ls = _log_sigmoid(lam_ref[...])
        w4v, b4v, cwv, cbv = w4_ref[...], b4_ref[...], cw_ref[...], cb_ref[...]

        def chunk(ci, _):
            rows = pl.ds(pl.multiple_of(ci * CH, CH), CH)
            taps = _conv_taps(xpad[pl.ds(_pad_start(ci), CH + 2 * HALO), :])
            xl = cbv + sum(taps[k] * cwv[k:k + 1, :] for k in range(CONV_W))
            for d, (r, i, a, q) in enumerate(_lru_gates(xl, w4v, b4v, ls)):
                (a0, a1)[d][rows, :] = a
                (b0, b1)[d][rows, :] = jnp.sqrt(q) * (i * xl)
                (r0_ref, r1_ref)[d][rows, :] = r.astype(BF16)
                (i0_ref, i1_ref)[d][rows, :] = i.astype(BF16)
            return 0

        lax.fori_loop(0, T // CH, chunk, 0)
        zero = jnp.zeros((1, RB), F32)

        def emit_f(rs, hf, before):
            y[pl.ds(rs, SUB), :] = hf
            b0[pl.ds(rs, SUB), :] = before

        def emit_b(rs, hf, before):
            y[pl.ds(rs, SUB), :] += hf
            b1[pl.ds(rs, SUB), :] = before

        _scan_rows(a0, b0, 0, T, False, zero, emit_f)
        c = _scan_rows(a1, b1, 0, CTX, True, zero, emit_b)
        _scan_rows(a1, b1, CTX, T - CTX, True, c, emit_b)

        def finish(ci, _):
            rows = pl.ds(pl.multiple_of(ci * CH, CH), CH)
            yv = y[rows, :]
            u_ref[rows, :] = (yv * _gelu(gr_ref[rows, :].astype(F32))).astype(BF16)
            yo_ref[rows, :] = yv.astype(BF16)
            hpf_ref[rows, :] = b0[rows, :].astype(BF16)
            hpb_ref[rows, :] = b1[rows, :].astype(BF16)
            return 0

        lax.fori_loop(0, T // CH, finish, 0)

    sp = _rnn_specs(T)
    ci, ca, co, cs, cscr = _carry_args(carry)
    dts = [BF16, F32, F32] + [BF16] * 7
    res = pl.pallas_call(
        _carried(kern, carry, 7, 10, *_grid_ends((N_RNN_BLOCKS,))), name=name, grid=(N_RNN_BLOCKS,),
        in_specs=[sp["xr"], sp["gr"], sp["cw"], sp["cb"], sp["w4"], sp["b4"], sp["lam"]] + ci,
        out_specs=[sp["act"]] * 10 + co,
        out_shape=[_sds((T, D), dt) for dt in dts] + cs,
        scratch_shapes=[pltpu.VMEM((T + PAD_ROWS, RB), F32)] + [pltpu.VMEM((T, RB), F32)] * 3 + cscr,
        compiler_params=_params(),
    )(p, p, cw, cb, w4, b4, lam, *ca)
    return res if carry is None else (res[:10], res[10:])


def _rnn_bwd(name, p, du, saved, dp, cw, cb, w4, b4, lam, T, carry=None):
    def kern(xr_ref, gr_ref, du_ref, a0, a1, y_ref, hpf_ref, hpb_ref, r0_ref, r1_ref, i0_ref, i1_ref,
             cw_ref, cb_ref, w4_ref, b4_ref, lam_ref, dp_in,
             dp_ref, dcw_ref, dcb_ref, dw4_ref, db4_ref, dlam_ref,
             xpad, dxpad, c0, c1, dy):
        j = pl.program_id(1)

        @pl.when(j == 0)
        def _():
            scans(gr_ref, du_ref, a0, a1, y_ref, dp_ref, c0, c1, dy)

        @pl.when(j == 1)
        def _():
            gates(xr_ref, a0, a1, (hpf_ref, hpb_ref), (r0_ref, r1_ref), (i0_ref, i1_ref), cw_ref, cb_ref, w4_ref,
                  lam_ref, dp_ref, dcw_ref, dcb_ref, dw4_ref, db4_ref, dlam_ref, xpad, dxpad, c0, c1)

    def scans(gr_ref, du_ref, a0, a1, y_ref, dgr_ref, c0, c1, dy):
        def phase_a(ci, _):
            rows = pl.ds(pl.multiple_of(ci * CH, CH), CH)
            gr = gr_ref[rows, :].astype(F32)
            duv = du_ref[rows, :].astype(F32)
            dyv = duv * _gelu(gr)
            dgr_ref[rows, :] = (duv * y_ref[rows, :].astype(F32) * _gelu_grad(gr)).astype(BF16)
            dy[rows, :] = dyv
            c0[rows, :] = a0[rows, :] * dyv
            c1[rows, :] = a1[rows, :] * dyv
            return 0

        lax.fori_loop(0, T // CH, phase_a, 0)
        zero = jnp.zeros((1, RB), F32)

        def emit0(rs, hf, before):
            c0[pl.ds(rs, SUB), :] = dy[pl.ds(rs, SUB), :] + before

        def emit1(rs, hf, before):
            c1[pl.ds(rs, SUB), :] = dy[pl.ds(rs, SUB), :] + before

        _scan_rows(a0, c0, 0, T, True, zero, emit0)
        c = _scan_rows(a1, c1, CTX, T - CTX, False, zero, emit1)
        _scan_rows(a1, c1, 0, CTX, False, c, emit1)

    def gates(xr_ref, a0, a1, hp_refs, r_refs, i_refs, cw_ref, cb_ref, w4_ref, lam_ref,
              dxr_ref, dcw_ref, dcb_ref, dw4_ref, db4_ref, dlam_ref, xpad, dxpad, c0, c1):
        _fill_padded(xpad, xr_ref, T)
        _zero_pads(dxpad, T)
        lam_v = lam_ref[...]
        ls = _log_sigmoid(lam_v)
        w4v, cwv, cbv = w4_ref[...], cw_ref[...], cb_ref[...]

        def conv_chunk(ci):
            taps = _conv_taps(xpad[pl.ds(_pad_start(ci), CH + 2 * HALO), :])
            return taps, cbv + sum(taps[k] * cwv[k:k + 1, :] for k in range(CONV_W))

        dw4_ref[...] = jnp.zeros(dw4_ref.shape, F32)
        db4_ref[...] = jnp.zeros(db4_ref.shape, F32)
        dlam_ref[...] = jnp.zeros(dlam_ref.shape, F32)
        dcw_ref[...] = jnp.zeros(dcw_ref.shape, F32)
        dcb_ref[...] = jnp.zeros(dcb_ref.shape, F32)

        def phase_c(ci, _):
            base = pl.multiple_of(ci * CH, CH)
            rows = pl.ds(base, CH)
            _, xl = conv_chunk(ci)
            dxl = jnp.zeros((CH, RB), F32)
            dpre_a, dpre_x, dls = [], [], []
            for d in range(2):
                a = (a0, a1)[d][rows, :]
                r = r_refs[d][rows, :].astype(F32)
                i = i_refs[d][rows, :].astype(F32)
                q = -jnp.tanh(LRU_C * r * ls[d:d + 1, :]) * (1.0 + a * a)
                g = (c0, c1)[d][rows, :]
                hp = hp_refs[d][rows, :].astype(F32)
                gm = g * jnp.sqrt(q)
                di = gm * xl
                dxl = dxl + gm * i
                dla = a * (g * hp - a * (g * (i * xl)) * lax.rsqrt(q))
                dr = dla * (LRU_C * ls[d:d + 1, :])
                dls.append(_colsum(dla * (LRU_C * r)))
                dpre_a.append(dr * r * (1.0 - r))
                dpre_x.append(di * i * (1.0 - i))
            dpre = jnp.concatenate(dpre_a + dpre_x, axis=1)
            dpre_b = dpre.astype(BF16)
            dxl = dxl + _dot(dpre_b, w4v, NT)
            dw4_ref[...] += _dot(xl.astype(BF16), dpre_b, TN)
            db4_ref[...] += _colsum(dpre)
            dlam_ref[...] += jnp.concatenate(dls, axis=0)
            dcb_ref[...] += _colsum(dxl)
            dxpad[_pad_rows(ci), :] = dxl
            return 0

        lax.fori_loop(0, T // CH, phase_c, 0)
        dlam_ref[...] = dlam_ref[...] * _sigmoid(-lam_v)

        def phase_d(ci, _):
            base = pl.multiple_of(ci * CH, CH)
            rows = pl.ds(base, CH)
            xtaps, _ = conv_chunk(ci)
            dtaps = _conv_taps(dxpad[pl.ds(_pad_start(ci), CH + 2 * HALO), :], transpose=True)
            dxl = dxpad[_pad_rows(ci), :]
            dxr_ref[rows, :] = sum(dtaps[k] * cwv[k:k + 1, :] for k in range(CONV_W)).astype(BF16)
            dcw_ref[...] += jnp.concatenate([_colsum(dxl * xtaps[k]) for k in range(CONV_W)], axis=0)
            return 0

        lax.fori_loop(0, T // CH, phase_d, 0)

    sp = _rnn_specs(T)
    dp_spec = pl.BlockSpec((T, RB), lambda n, j: (0, COL_GR // RB + n - j * (COL_GR - COL_XR) // RB))
    ci, ca, co, cs, cscr = _carry_args(carry)
    n_in = 3 + len(saved) + 5 + 1
    res = pl.pallas_call(
        _carried(kern, carry, n_in, 6, *_grid_ends((N_RNN_BLOCKS, 2))), name=name, grid=(N_RNN_BLOCKS, 2),
        in_specs=[sp["xr"], sp["gr"]] + [sp["act"]] * (1 + len(saved)) + [sp["cw"], sp["cb"], sp["w4"], sp["b4"],
                                                                           sp["lam"], ANY] + ci,
        out_specs=[dp_spec, sp["cw"], sp["cb"], sp["w4"], sp["b4"], sp["lam"]] + co,
        out_shape=[_sds((T, DP_W), BF16), _sds((CONV_W, D), F32), _sds((1, D), F32),
                   _sds((N_RNN_BLOCKS, RB, 4 * RB), F32), _sds((N_RNN_BLOCKS, 1, 4 * RB), F32), _sds((2, D), F32)] + cs,
        scratch_shapes=[pltpu.VMEM((T + PAD_ROWS, RB), F32)] * 2 + [pltpu.VMEM((T, RB), F32)] * 3 + cscr,
        input_output_aliases={n_in - 1: 0},
        compiler_params=_params(("arbitrary", "arbitrary")),
    )(p, p, du, *saved, cw, cb, w4, b4, lam, dp, *ca)
    return res if carry is None else (res[:6], res[6:])


class _Plan:
    def __init__(self, shards, Ws):
        L = len(Ws)
        self.shards, self.Ws = shards, Ws
        self.Gs = [None] * L
        self.slots = [dict() for _ in range(L)]
        self.gate_slots = [None] * L
        self.table = {}
        for l in range(L):
            t = f"l{l}_"
            self.table[t + "rnn_fwd"] = [("gather", l, k) for k in ("wffn_in_t", "wo_rnn", "wo_attn", "wout")]
            if l + 1 < L:
                self.table[t + "attn_lat_fwd"] = [("gather", l + 1, "win_t")]
                self.table[t + "ffn_in"] = [("gather", l, "wffn_out")]
            else:
                self.table[t + "attn_lat_fwd"] = [("gather", l, "wffn_out")]
            self.table[t + "ffn_in_dx"] = [("scatter", l, "wffn_out")]
            self.table[t + "attn_lat_bwd"] = [("scatter", l, "wffn_in_t")]
            self.table[t + "proj_dx"] = [("scatter", l, "win_t_a" if l > 0 else "win_t_b")]
            self.table[t + "rnn_bwd"] = ([("scatter", l, k) for k in ("wout", "wo_attn", "wo_rnn")]
                                         + ([("scatter", l + 1, "win_t_b"), ("gates", l + 1, "w4")] if l + 1 < L else []))
        self.table["l0_proj_dw_a"] = [("gates", 0, "w4")]
        self.table["l0_proj_dw_b"] = [("scatter", 0, "win_t_a")]
        self.table["l0_mix_norm"] = [("gather", 0, "win_t")]

    def carry(self, name):
        jobs = []
        for kind, l, k in self.table.get(name, []):
            if kind == "gather":
                jobs.append(("gather", self.shards[l][k]))
            elif kind == "scatter":
                jobs.append(("scatter", self.Gs[l][k].reshape(N_DEV, -1, self.Gs[l][k].shape[-1])))
            else:
                jobs.append(("gather", self.Gs[l]["w4"].reshape(N_RNN_BLOCKS * RB, 4 * RB).astype(BF16)))
        return _Carry(jobs) if jobs else None

    def done(self, name, got):
        for (kind, l, k), res in zip(self.table[name], got):
            if kind == "gather":
                self.Ws[l][k] = res.reshape(-1, D)
            elif kind == "scatter":
                self.slots[l][k] = res
            else:
                self.gate_slots[l] = res


def _run(X, fn, name, *args, **kw):
    carry = None if X is None else X.carry(name)
    if carry is None:
        return fn(name, *args, **kw)
    out, got = fn(name, *args, carry=carry, **kw)
    X.done(name, got)
    return out


def _layer_fwd(l, xa, h, W, rope, S, nxt, X=None):
    T = xa.shape[0]
    tag = f"l{l}_"
    cos, sin, bias = rope
    p = _run(X, _mm_act, tag + "proj", h, W["win_t"], "NT", BF16)
    u, *rnn_saved = _run(X, _rnn_fwd, tag + "rnn_fwd", p, W["cw"], W["cb"], W["w4"], W["b4"], W["lam"], T)
    qa, kp, vp, kc, vc = _qkv_prep(tag + "qkv_prep", p, cos, sin, S)
    o_all = _attn_fwd(tag + "attn_ctx_fwd", qa, kc, vc, W["sink4"], S)
    o_all = _run(X, _attn_fwd, tag + "attn_lat_fwd", qa, kc, vc, W["sink4"], S, band=(kp, vp, bias), prev=o_all)
    ya, yb, z, m, x1, h2 = _out_fused(tag + "out", p, u, o_all, xa, W["wo_rnn"], W["wo_attn"], W["wout"],
                                      W["g_mix_post"], W["mod"], W["g_ffn_pre"])
    fg, fu, s = _run(X, _ffn_in_fused, tag + "ffn_in", h2, W["wffn_in_t"])
    e, *out = _ffn_out_fused(tag + "ffn_out", s, W["wffn_out"], x1, W["g_ffn_post"], W["mod"], nxt)
    saved = dict(xa=xa, h=h, p=p, u=u, rnn=rnn_saved, qa=qa, kp=kp, vp=vp, kc=kc, vc=vc, o_all=o_all,
                 ya=ya, yb=yb, z=z, m=m, x1=x1, h2=h2, fg=fg, fu=fu, s=s, e=e)
    return saved, out


def _layer_bwd(l, dx2, A, W, rope, S, X=None, loss_of=None):
    T = A["xa"].shape[0]
    tag = f"l{l}_"
    cos, sin, bias = rope
    G = {}
    if X is not None:
        X.Gs[l] = G
    if loss_of is None:
        de, df, dga2, G["g_ffn_post"] = _ffn_bwd_fused(tag + "ffn_bwd", A["fg"], A["fu"], W["wffn_out"],
                                                       head=(dx2, A["e"], W["g_ffn_post"], W["mod"]))
    else:
        dx2, de, dga2, G["g_ffn_post"], G["sq"] = _loss_resid_bwd(tag + "loss_ffn_resid_bwd", *loss_of, A["e"],
                                                                  W["g_ffn_post"], W["mod"], GA2)
        df, = _ffn_bwd_fused(tag + "ffn_bwd", A["fg"], A["fu"], W["wffn_out"], de=de)
    G["wffn_out"] = _mm_wgrad(tag + "ffn_out_dw", A["s"], de)
    dx1, dm, dsh2, dsc2, G["g_ffn_pre"], dga1, G["g_mix_post"] = _run(
        X, _ffn_in_bwd_fused, tag + "ffn_in_dx", df, W["wffn_in_t"], A["x1"], dx2, A["m"], W["g_ffn_pre"], W["mod"],
        W["g_mix_post"])
    G["wffn_in_t"] = _run(X, _mm_wgrad, tag + "ffn_in_dw", df, A["h2"])
    G["wout"] = _mm_wgrad(tag + "out_dw", A["z"], dm)
    dya, dyb, dgl, du, do = _out_bwd_fused(tag + "out_dx", dm, W["wout"], W["wo_rnn"], W["wo_attn"], A["p"], A["ya"],
                                           A["yb"])
    G["wo_attn"] = _mm_wgrad(tag + "o_attn_dw", A["o_all"], dyb)
    G["wo_rnn"] = _mm_wgrad(tag + "o_rnn_dw", A["u"], dya)
    dp, dkc_c, dvc_c, dsink_c = _attn_bwd(tag + "attn_ctx_bwd", A["qa"], A["kc"], A["vc"], W["sink4"], A["o_all"], do, S)
    dp, dkc_l, dvc_l, dsink_l, dkp, dvp = _run(
        X, _attn_bwd, tag + "attn_lat_bwd", A["qa"], A["kc"], A["vc"], W["sink4"], A["o_all"], do, S,
        band=(A["kp"], A["vp"], bias, cos, sin), prev_dq=dp)
    G["sink4"] = dsink_c + dsink_l
    dp = _dkv_assemble(tag + "dkv", dp, dkp, dvp, dkc_l, dvc_l, dkc_c, dvc_c, cos, sin, S)
    dp, G["cw"], G["cb"], G["w4"], G["b4"], G["lam"] = _run(
        X, _rnn_bwd, tag + "rnn_bwd", A["p"], du, A["rnn"], dp, W["cw"], W["cb"], W["w4"], W["b4"], W["lam"], T)
    proj_dx = (_proj_bwd_fused, tag + "proj_dx", dp, dgl, W["win_t"], A["xa"], dx1, W["g_mix_pre"], W["mod"])
    if X is not None:
        G["win_t_a"] = _run(X, _proj_wgrad, tag + "proj_dw_a", dp, dgl, A["h"][:, :D // 2])
        if l > 0:
            dxa, dsh1, dsc1, G["g_mix_pre"] = _run(X, *proj_dx)
        G["win_t_b"] = _run(X, _proj_wgrad, tag + "proj_dw_b", dp, dgl, A["h"][:, D // 2:])
        if l == 0:
            dxa, dsh1, dsc1, G["g_mix_pre"] = _run(X, *proj_dx, latent_only=True)
    else:
        dxa, dsh1, dsc1, G["g_mix_pre"] = _run(X, *proj_dx)
        G["win_t"] = _proj_wgrad(tag + "proj_dw", dp, dgl, A["h"])
    G["mod"] = jnp.concatenate([dsh1, dsc1, dga1, dsh2, dsc2, dga2], axis=1)
    return dxa, G


def _local_step(ctx, x, target, Ws, S, X=None):
    rope = (*_rope_tables(S), _band_bias(S))
    L = len(Ws)
    x, h = _run(X, _normmod_fwd, "l0_mix_norm", ctx, x, Ws[0]["g_mix_pre"], Ws[0]["mod"], SH1, SC1)
    saved = []
    for l in range(L):
        nxt = (Ws[l + 1]["g_mix_pre"], Ws[l + 1]["mod"]) if l + 1 < L else None
        A, out = _layer_fwd(l, x, h, Ws[l], rope, S, nxt, X)
        saved.append(A)
        if l + 1 < L:
            x, h = out
    Gs = [None] * L
    dx = None
    for l in reversed(range(L)):
        dx, Gs[l] = _layer_bwd(l, dx, saved[l], Ws[l], rope, S, X, loss_of=(out[0], target) if l == L - 1 else None)
    return Gs[L - 1]["sq"], dx, Gs


MESH = pl.DeviceIdType.MESH


def _place():
    return lax.axis_index("x"), lax.axis_index("y"), lax.axis_index("c")


def _lin(px, py, pc):
    return 4 * px + 2 * py + pc


def _allgather_small(name, blk):
    m, n = blk.shape

    def body(x_ref, out_ref, send_sems, recv_sems, local_sem):
        x, y, c = _place()
        me, sibling = (x, y, c), (x, y, 1 - c)
        chips = [(1 - x, y), (x, 1 - y), (1 - x, 1 - y)]

        def copy(k, block, to, src=None):
            dst = out_ref.at[_lin(*block)]
            return pltpu.make_async_remote_copy(src_ref=dst if src is None else src, dst_ref=dst,
                                                send_sem=send_sems.at[k], recv_sem=recv_sems.at[k],
                                                device_id=to, device_id_type=MESH)

        mine = pltpu.make_async_copy(x_ref, out_ref.at[_lin(*me)], local_sem)
        mine.start()
        first = [copy(0, me, sibling, src=x_ref)]
        first += [copy(1 + j, me, (*chip, c), src=x_ref) for j, chip in enumerate(chips)]
        for cp in first:
            cp.start()
        passed = [copy(4 + j, (*chip, c), sibling) for j, chip in enumerate(chips)]
        for j, chip in enumerate(chips):
            copy(1 + j, (*chip, c), me).wait_recv()
            passed[j].start()
        copy(0, sibling, me).wait_recv()
        for j, chip in enumerate(chips):
            copy(4 + j, (*chip, 1 - c), me).wait_recv()
        for cp in first + passed:
            cp.wait_send()
        mine.wait()

    return pl.pallas_call(
        body, name=name, out_shape=_sds((N_DEV, m, n), blk.dtype),
        in_specs=[pl.BlockSpec(memory_space=pltpu.VMEM)], out_specs=pl.BlockSpec(memory_space=pltpu.VMEM),
        scratch_shapes=[pltpu.SemaphoreType.DMA((7,)), pltpu.SemaphoreType.DMA((7,)), pltpu.SemaphoreType.DMA],
        compiler_params=pltpu.CompilerParams(vmem_limit_bytes=VMEM_LIMIT),
    )(blk)


MOD_ROWS = 16
MOD_SHARD = 6 * D // N_DEV
HI = lax.Precision.HIGHEST


def _mod_fwd(name, c9, w_mod, b_shard):
    L = w_mod.shape[0]

    def kern(c_ref, w_ref, b_ref, o_ref):
        o_ref[...] = lax.dot_general(_silu(c_ref[...]), w_ref[...], NN, precision=HI,
                                     preferred_element_type=F32) + b_ref[...]

    return pl.pallas_call(
        kern, name=name, grid=(L,),
        in_specs=[_full_spec(c9.shape), pl.BlockSpec((None, D, MOD_SHARD), lambda l: (l, 0, 0)),
                  pl.BlockSpec((None, 1, MOD_SHARD), lambda l: (l, 0, 0))],
        out_specs=pl.BlockSpec((None, MOD_ROWS, MOD_SHARD), lambda l: (l, 0, 0)),
        out_shape=_sds((L, MOD_ROWS, MOD_SHARD), F32), compiler_params=_params(),
    )(c9, w_mod, b_shard)


def _mod_bwd(name, c9, w_mod, dmod_all, dmod_cols):
    L = w_mod.shape[0]

    def rows9(ref, l):
        own = jnp.concatenate([ref[j, 2 * l + 1:2 * l + 2, :] for j in range(N_DEV)], axis=0)
        ctx = ref[0, 2 * l:2 * l + 1, :]
        for j in range(1, N_DEV):
            ctx = ctx + ref[j, 2 * l:2 * l + 1, :]
        return own, ctx

    def kern(c_ref, w_ref, all_ref, cols_ref, gw_ref, gb_ref, gc_ref):
        l = pl.program_id(0)
        for ll in range(L):
            @pl.when(l == ll)
            def _():
                own, ctx = rows9(all_ref, ll)
                gb_ref[...] = _colsum(own) + ctx
                own_s, ctx_s = rows9(cols_ref, ll)
                r16 = jnp.concatenate([own_s, ctx_s, jnp.zeros((MOD_ROWS - N_DEV - 1, MOD_SHARD), F32)], axis=0)
                gw_ref[...] = lax.dot_general(_silu(c_ref[...]), r16, TN, precision=HI, preferred_element_type=F32)
                part = lax.dot_general(r16, w_ref[...], NT, precision=HI,
                                       preferred_element_type=F32)[N_DEV:N_DEV + 1, :]
                if ll == 0:
                    gc_ref[...] = part
                else:
                    gc_ref[...] += part

    return pl.pallas_call(
        kern, name=name, grid=(L,),
        in_specs=[_full_spec(c9.shape), pl.BlockSpec((None, D, MOD_SHARD), lambda l: (l, 0, 0)),
                  _full_spec(dmod_all.shape), _full_spec(dmod_cols.shape)],
        out_specs=[pl.BlockSpec((None, D, MOD_SHARD), lambda l: (l, 0, 0)),
                   pl.BlockSpec((None, 1, 6 * D), lambda l: (l, 0, 0)), _full_spec((1, D))],
        out_shape=[_sds((L, D, MOD_SHARD), F32), _sds((L, 1, 6 * D), F32), _sds((1, D), F32)],
        compiler_params=_params(),
    )(c9, w_mod, dmod_all, dmod_cols)


_BC1 = 1.0 - ADAM_B1 ** ADAM_STEP
_BC2 = 1.0 - ADAM_B2 ** ADAM_STEP


def _adamw_vals(w, g, m, v):
    m = ADAM_B1 * m + (1.0 - ADAM_B1) * g
    v = ADAM_B2 * v + (1.0 - ADAM_B2) * (g * g)
    delta = -ADAM_LR * ((m / _BC1) / (jnp.sqrt(v / _BC2) + ADAM_EPS) + ADAM_WD * w)
    return delta, m, v


def _adamw(name, w, g, m, v, tile):
    R, C = w.shape
    blk = ((tile, C), lambda i: (i, 0))

    def body(i, ins, ps, outs, acc):
        d, mm, vv = _adamw_vals(ins[0][...], ins[1][...], ins[2][...], ins[3][...])
        outs[0][...] = d
        outs[1][...] = mm
        outs[2][...] = vv

    return _ew(name, body, R // tile, [(a, *blk) for a in (w, g, m, v)], [], [(_sds((R, C), F32), *blk)] * 3)


def _sum_slots(ref):
    g = ref[0].astype(F32)
    for j in range(1, N_DEV):
        g = g + ref[j].astype(F32)
    return g


def _adamw_slots(name, slots, shape, tile, wmv=None):
    L, R, C = shape
    n = R // tile
    spec = pl.BlockSpec((None, tile, C), lambda l, i: (l, i, 0))
    pieces = [s if isinstance(s, (list, tuple)) else [s] for s in slots]
    layer_of = [ll for ll, ps in enumerate(pieces) for _ in ps]
    flat = [p for ps in pieces for p in ps]
    wmv = list(wmv or [])

    def slot_spec(ll, cols):
        return pl.BlockSpec((N_DEV, tile, cols),
                            lambda l, i: (0, jnp.where(l == ll, i, jnp.where(l < ll, 0, n - 1)), 0))

    def kern(*refs):
        s_refs = refs[:len(flat)]
        rest = refs[len(flat):]
        l = pl.program_id(0)
        for ll in range(L):
            @pl.when(l == ll)
            def _():
                parts = [_sum_slots(r) for r, lr in zip(s_refs, layer_of) if lr == ll]
                g = parts[0] if len(parts) == 1 else jnp.concatenate(parts, axis=1)
                if wmv:
                    w_ref, m_ref, v_ref, g_ref, d_ref, mo_ref, vo_ref = rest
                    d_ref[...], mo_ref[...], vo_ref[...] = _adamw_vals(w_ref[...], g, m_ref[...], v_ref[...])
                else:
                    g_ref, = rest
                g_ref[...] = g

    n_out = 4 if wmv else 1
    return pl.pallas_call(
        kern, name=name, grid=(L, n),
        in_specs=[slot_spec(ll, p.shape[-1]) for ll, p in zip(layer_of, flat)] + [spec] * len(wmv),
        out_specs=[spec] * n_out, out_shape=[_sds((L, R, C), F32)] * n_out,
        compiler_params=_params(("arbitrary", "arbitrary")),
    )(*flat, *wmv)


def _sum_blocks(name, blocks):
    _, R, C = blocks.shape

    def kern(b_ref, o_ref):
        o_ref[...] = _sum_slots(b_ref)

    return pl.pallas_call(kern, name=name, in_specs=[_full_spec(blocks.shape)], out_specs=_full_spec((R, C)),
                          grid=(1,), out_shape=_sds((R, C), F32), compiler_params=_params())(blocks)


BIG = ("win_t", "wo_rnn", "wo_attn", "wout", "wffn_in_t", "wffn_out")
BIG_SRC = ("w_in", "w_o_rnn", "w_o_attn", "w_out", "w_ffn_in", "w_ffn_out")
BIG_T = (True, False, False, False, True, False)
BIG_TILE = (176, 128, 128, 128, 176, 176)


def _chan_full(g8):
    return jnp.transpose(g8, (1, 0, 2)).reshape(g8.shape[1], D)


def kernel(x, c, ctx, c_ctx, w_mod, b_mod, g_mix_pre, g_mix_post, g_ffn_pre, g_ffn_post, w_in, conv_w, conv_b, lru_wa, lru_ba, lru_wx, lru_bx, lru_lam, attn_sink, w_o_rnn, w_o_attn, w_out, w_ffn_in, w_ffn_out, loss_target, m_c_ctx, m_w_mod, m_b_mod, m_g_mix_pre, m_g_mix_post, m_g_ffn_pre, m_g_ffn_post, m_w_in, m_conv_w, m_conv_b, m_lru_wa, m_lru_ba, m_lru_wx, m_lru_bx, m_lru_lam, m_attn_sink, m_w_o_rnn, m_w_o_attn, m_w_out, m_w_ffn_in, m_w_ffn_out, v_c_ctx, v_w_mod, v_b_mod, v_g_mix_pre, v_g_mix_post, v_g_ffn_pre, v_g_ffn_post, v_w_in, v_conv_w, v_conv_b, v_lru_wa, v_lru_ba, v_lru_wx, v_lru_bx, v_lru_lam, v_attn_sink, v_w_o_rnn, v_w_o_attn, v_w_out, v_w_ffn_in, v_w_ffn_out):
    P = dict(c_ctx=c_ctx, w_mod=w_mod, b_mod=b_mod, g_mix_pre=g_mix_pre, g_mix_post=g_mix_post, g_ffn_pre=g_ffn_pre,
             g_ffn_post=g_ffn_post, w_in=w_in, conv_w=conv_w, conv_b=conv_b, lru_wa=lru_wa, lru_ba=lru_ba,
             lru_wx=lru_wx, lru_bx=lru_bx, lru_lam=lru_lam, attn_sink=attn_sink, w_o_rnn=w_o_rnn, w_o_attn=w_o_attn,
             w_out=w_out, w_ffn_in=w_ffn_in, w_ffn_out=w_ffn_out)
    Mo = dict(c_ctx=m_c_ctx, w_mod=m_w_mod, b_mod=m_b_mod, g_mix_pre=m_g_mix_pre, g_mix_post=m_g_mix_post,
              g_ffn_pre=m_g_ffn_pre, g_ffn_post=m_g_ffn_post, w_in=m_w_in, conv_w=m_conv_w, conv_b=m_conv_b,
              lru_wa=m_lru_wa, lru_ba=m_lru_ba, lru_wx=m_lru_wx, lru_bx=m_lru_bx, lru_lam=m_lru_lam,
              attn_sink=m_attn_sink, w_o_rnn=m_w_o_rnn, w_o_attn=m_w_o_attn, w_out=m_w_out, w_ffn_in=m_w_ffn_in,
              w_ffn_out=m_w_ffn_out)
    Vo = dict(c_ctx=v_c_ctx, w_mod=v_w_mod, b_mod=v_b_mod, g_mix_pre=v_g_mix_pre, g_mix_post=v_g_mix_post,
              g_ffn_pre=v_g_ffn_pre, g_ffn_post=v_g_ffn_post, w_in=v_w_in, conv_w=v_conv_w, conv_b=v_conv_b,
              lru_wa=v_lru_wa, lru_ba=v_lru_ba, lru_wx=v_lru_wx, lru_bx=v_lru_bx, lru_lam=v_lru_lam,
              attn_sink=v_attn_sink, w_o_rnn=v_w_o_rnn, w_o_attn=v_w_o_attn, w_out=v_w_out, w_ffn_in=v_w_ffn_in,
              w_ffn_out=v_w_ffn_out)
    L = w_in.shape[0]
    S = x.shape[1]
    me = _lin(*_place())

    small = jnp.concatenate([c.reshape(8, 128), conv_w.reshape(L * CONV_W, 128), lru_ba.reshape(2 * L, 128),
                             lru_bx.reshape(2 * L, 128), lru_lam.reshape(2 * L, 128), jnp.zeros((4, 128), F32)], axis=0)
    small_all = _allgather_small("ag_small", small)
    c_all = small_all[:, 0:8].reshape(N_DEV, D)
    conv_w_f = _chan_full(small_all[:, 8:16]).reshape(L, CONV_W, D)
    lru_ba_f = _chan_full(small_all[:, 16:20]).reshape(L, 2, D)
    lru_bx_f = _chan_full(small_all[:, 20:24]).reshape(L, 2, D)
    lru_lam_f = _chan_full(small_all[:, 24:28]).reshape(L, 2, D)

    c9 = jnp.concatenate([c_all, c_ctx[None], jnp.zeros((MOD_ROWS - N_DEV - 1, D), F32)], axis=0)
    b_shard = lax.dynamic_slice_in_dim(b_mod, me * MOD_SHARD, MOD_SHARD, axis=1)[:, None, :]
    mod_part = _mod_fwd("mod_fwd", c9, w_mod, b_shard)
    mod_all = _allgather_small("ag_mod", mod_part.reshape(L * MOD_ROWS, MOD_SHARD))
    mod_all = jnp.transpose(mod_all.reshape(N_DEV, L, MOD_ROWS, MOD_SHARD), (1, 2, 0, 3)).reshape(L, MOD_ROWS, 6 * D)
    own_row = lax.dynamic_index_in_dim(mod_all, me, axis=1, keepdims=False)
    modrows = jnp.stack([mod_all[:, N_DEV], own_row], axis=1)

    shards = [{k: (P[src][l].T if tr else P[src][l]).astype(BF16) for k, src, tr in zip(BIG, BIG_SRC, BIG_T)}
              for l in range(L)]
    Ws = []
    for l in range(L):
        W = {}
        W.update(
            cw=conv_w_f[l], cb=conv_b[l][None],
            w4=jnp.concatenate([lru_wa[l, 0], lru_wa[l, 1], lru_wx[l, 0], lru_wx[l, 1]], axis=-1).astype(BF16),
            b4=jnp.concatenate([lru_ba_f[l, 0].reshape(N_RNN_BLOCKS, 1, RB), lru_ba_f[l, 1].reshape(N_RNN_BLOCKS, 1, RB),
                                lru_bx_f[l, 0].reshape(N_RNN_BLOCKS, 1, RB), lru_bx_f[l, 1].reshape(N_RNN_BLOCKS, 1, RB)],
                               axis=-1),
            lam=lru_lam_f[l], sink4=jnp.broadcast_to(attn_sink[l].reshape(N_KV, Q_PER_KV, 1), (N_KV, Q_PER_KV, HEAD)),
            g_mix_pre=g_mix_pre[l][None], g_mix_post=g_mix_post[l][None], g_ffn_pre=g_ffn_pre[l][None],
            g_ffn_post=g_ffn_post[l][None], mod=modrows[l])
        Ws.append(W)

    plan = _Plan(shards, Ws)
    sq, dxa, Gs = _local_step(ctx[0], x[0], loss_target[0], Ws, S, plan)
    loss_part = ((0.5 / D) * jnp.sum(sq)).reshape(1, 1)
    grad_x = dxa[None]

    dmod = jnp.concatenate([Gs[l]["mod"] for l in range(L)] + [jnp.zeros((8 - 2 * L, 6 * D), F32)], axis=0)
    dmod_all = _allgather_small("ag_dmod", dmod)
    dmod_cols = lax.dynamic_slice_in_dim(dmod_all, me * MOD_SHARD, MOD_SHARD, axis=2)
    g_w_mod, g_b_mod, dsc_part = _mod_bwd("mod_bwd", c9, w_mod, dmod_all, dmod_cols)
    g_b_mod = g_b_mod[:, 0]

    def rows(name, shape):
        return jnp.concatenate([Gs[l][name].reshape(shape) for l in range(L)], axis=0)

    b4g = [Gs[l]["b4"].reshape(N_RNN_BLOCKS, 4, RB) for l in range(L)]
    sink_row = jnp.concatenate([Gs[l]["sink4"][:, :, 0].reshape(1, N_Q) for l in range(L)]
                               + [loss_part, jnp.zeros((1, D - L * N_Q - 1), F32)], axis=1)
    small_g = jnp.concatenate(
        [rows("g_mix_pre", (1, D)), rows("g_mix_post", (1, D)), rows("g_ffn_pre", (1, D)), rows("g_ffn_post", (1, D)),
         rows("cb", (1, D)), rows("cw", (CONV_W, D))]
        + [b4g[l][:, d].reshape(1, D) for l in range(L) for d in range(2)]
        + [b4g[l][:, 2 + d].reshape(1, D) for l in range(L) for d in range(2)]
        + [rows("lam", (2, D)), sink_row, dsc_part], axis=0)
    n_small = small_g.shape[0]
    small_tot = _sum_blocks("sum_small", _allgather_small("ag_small_grads", small_g))
    o = 0
    G = {}
    for name in ("g_mix_pre", "g_mix_post", "g_ffn_pre", "g_ffn_post", "conv_b"):
        G[name] = small_tot[o:o + L]
        o += L
    G["conv_w"] = small_tot[o:o + L * CONV_W].reshape(L, CONV_W, D)
    o += L * CONV_W
    for name in ("lru_ba", "lru_bx", "lru_lam"):
        G[name] = small_tot[o:o + 2 * L].reshape(L, 2, D)
        o += 2 * L
    G["attn_sink"] = small_tot[o, :L * N_Q].reshape(L, N_Q)
    loss = small_tot[o, L * N_Q]
    sg = jax.nn.sigmoid(c_ctx)
    G["c_ctx"] = small_tot[o + 1] * (sg * (1.0 + c_ctx * (1.0 - sg)))
    G["b_mod"] = g_b_mod
    G["w_mod"] = g_w_mod

    for l in range(L):
        plan.slots[l]["win_t"] = [plan.slots[l]["win_t_a"], plan.slots[l]["win_t_b"]]

    out_g, out_d, out_m, out_v = {}, {}, {}, {}

    def put(name, res, shape=None):
        g, d, m, v = res
        for dst, val in ((out_g, g), (out_d, d), (out_m, m), (out_v, v)):
            dst[name] = val if shape is None else val.reshape(shape)

    for k, src, tr, tile in zip(BIG, BIG_SRC, BIG_T, BIG_TILE):
        lay = (lambda a: jnp.swapaxes(a, 1, 2)) if tr else (lambda a: a)
        wmv = (lay(P[src]), lay(Mo[src]), lay(Vo[src]))
        res = _adamw_slots("adamw_" + src, [plan.slots[l][k] for l in range(L)], wmv[0].shape, tile, wmv)
        put(src, [lay(r) for r in res])
    res = _adamw("adamw_w_mod", w_mod.reshape(L * D, MOD_SHARD), g_w_mod.reshape(L * D, MOD_SHARD),
                 m_w_mod.reshape(L * D, MOD_SHARD), v_w_mod.reshape(L * D, MOD_SHARD), 256)
    put("w_mod", (g_w_mod,) + tuple(res), w_mod.shape)
    def fuse4(wa, wx):
        return jnp.concatenate([wa[:, 0], wa[:, 1], wx[:, 0], wx[:, 1]], axis=-1).reshape(L, N_RNN_BLOCKS * RB, 4 * RB)

    res = _adamw_slots("adamw_gates", plan.gate_slots, (L, N_RNN_BLOCKS * RB, 4 * RB), 256,
                       (fuse4(lru_wa, lru_wx), fuse4(m_lru_wa, m_lru_wx), fuse4(v_lru_wa, v_lru_wx)))
    res = [r.reshape(L, N_RNN_BLOCKS, RB, 4, RB) for r in res]
    put("lru_wa", [jnp.stack([r[:, :, :, 0], r[:, :, :, 1]], axis=1) for r in res])
    put("lru_wx", [jnp.stack([r[:, :, :, 2], r[:, :, :, 3]], axis=1) for r in res])
    rep = ("g_mix_pre", "g_mix_post", "g_ffn_pre", "g_ffn_post", "conv_b", "b_mod")

    def pack_rep(T_):
        sink = jnp.concatenate([T_["attn_sink"].reshape(1, L * N_Q), jnp.zeros((1, D - L * N_Q), F32)], axis=1)
        return jnp.concatenate([T_[n].reshape(-1, D) for n in rep] + [sink, T_["c_ctx"][None]], axis=0)

    pk = [pack_rep(T_) for T_ in (P, G, Mo, Vo)]
    n_rep = pk[0].shape[0]
    res = _adamw("adamw_replicated", *[jnp.pad(a, ((0, 24 - n_rep), (0, 0))) for a in pk], 24)
    res = (pk[1],) + tuple(r[:n_rep] for r in res)
    o = 0
    for n in rep:
        k = P[n].size // D
        put(n, [r[o:o + k] for r in res], P[n].shape)
        o += k
    put("attn_sink", [r[o, :L * N_Q] for r in res], attn_sink.shape)
    put("c_ctx", [r[o + 1] for r in res], c_ctx.shape)
    chan = ("conv_w", "lru_ba", "lru_bx", "lru_lam")
    g_own = {n: lax.dynamic_slice_in_dim(G[n], me * RB, RB, axis=2) for n in chan}

    def pack_chan(T_):
        return jnp.concatenate([T_[n].reshape(-1, RB) for n in chan], axis=0)

    pk = [pack_chan(T_) for T_ in (P, g_own, Mo, Vo)]
    n_ch = pk[0].shape[0]
    res = _adamw("adamw_channels", *[jnp.pad(a, ((0, 24 - n_ch), (0, 0))) for a in pk], 24)
    res = (pk[1],) + tuple(r[:n_ch] for r in res)
    o = 0
    for n in chan:
        k = P[n].size // RB
        put(n, [r[o:o + k] for r in res], P[n].shape)
        o += k

    order = ("c_ctx", "w_mod", "b_mod", "g_mix_pre", "g_mix_post", "g_ffn_pre", "g_ffn_post", "w_in", "conv_w", "conv_b",
             "lru_wa", "lru_ba", "lru_wx", "lru_bx", "lru_lam", "attn_sink", "w_o_rnn", "w_o_attn", "w_out", "w_ffn_in",
             "w_ffn_out")
    return (loss, grad_x, *[out_g[n] for n in order], *[out_d[n] for n in order], *[out_m[n] for n in order],
            *[out_v[n] for n in order])
```

```python
import functools
import math

import numpy as np
import jax
import jax.numpy as jnp
from jax import lax
from jax.experimental import pallas as pl
from jax.experimental.pallas import tpu as pltpu

F32 = jnp.float32
BF16 = jnp.bfloat16

D = 1024
CTX = 256
TR = 256
HEAD = 128
N_Q = 8
N_KV = 2
Q_PER_KV = N_Q // N_KV
GRID_W = 64
N_FREQ = HEAD // 4
ROPE_BASE = 10000.0
N_RNN_BLOCKS = 8
CONV_W = 4
CONV_LEFT = 2
LRU_C = 8.0
D_FF = 2816
IN_W = 5632
P_W = IN_W
DP_W = 3584
COL_XR, COL_GR, COL_Q, COL_K, COL_V, COL_GL = 0, 1024, 2048, 3072, 3328, 3584
GLB = 512
EPS = 1e-6
NEG_INF = -1e30
ATT_SCALE = HEAD ** -0.5
N_DEV = 8
VMEM_LIMIT = 56 * 1024 * 1024

ADAM_LR, ADAM_B1, ADAM_B2, ADAM_EPS, ADAM_WD, ADAM_STEP = 0.001, 0.9, 0.999, 1e-08, 0.01, 10

NN = (((1,), (0,)), ((), ()))
NT = (((1,), (1,)), ((), ()))
TN = (((0,), (0,)), ((), ()))


def _dot(a, b, dims=NN):
    return lax.dot_general(a, b, dims, preferred_element_type=F32)


def _params(sem=("arbitrary",)):
    return pltpu.CompilerParams(dimension_semantics=sem, vmem_limit_bytes=VMEM_LIMIT)


def _full_spec(shape):
    nd = len(shape)
    return pl.BlockSpec(shape, lambda *_: (0,) * nd)


ANY = pl.BlockSpec(memory_space=pl.ANY)


def _ew(name, body, n, row_ins, pars, row_outs, accs=(), alias=None):
    n_ri, n_p, n_ro, n_acc = len(row_ins), len(pars), len(row_outs), len(accs)

    def kern(*refs):
        i = pl.program_id(0)
        ins = refs[:n_ri]
        ps = refs[n_ri:n_ri + n_p]
        outs = refs[n_ri + n_p:n_ri + n_p + n_ro]
        acc = refs[n_ri + n_p + n_ro:]
        if n_acc:
            @pl.when(i == 0)
            def _():
                for a in acc:
                    a[...] = jnp.zeros(a.shape, a.dtype)
        body(i, ins, ps, outs, acc)

    in_specs = [ANY if blk is None else pl.BlockSpec(blk, imap) for (_, blk, imap) in row_ins]
    in_specs += [_full_spec(p.shape) for p in pars]
    out_specs = [pl.BlockSpec(blk, imap) for (_, blk, imap) in row_outs] + [_full_spec(a.shape) for a in accs]
    out_shape = [s for (s, _, _) in row_outs] + list(accs)
    return pl.pallas_call(
        kern, name=name, grid=(n,), in_specs=in_specs, out_specs=out_specs, out_shape=out_shape,
        input_output_aliases=alias or {}, compiler_params=_params(),
    )(*[a for (a, _, _) in row_ins], *pars)


def _rowblk(width, colblk=0, roff=0, tile=TR):
    return (tile, width), (lambda i: (i + roff, colblk))


def _sds(shape, dtype):
    return jax.ShapeDtypeStruct(shape, dtype)


class _Carry:
    SAME_CORE = (1, 3, 5)

    def __init__(self, jobs):
        self.jobs = list(jobs)
        self.arrays = [a for _, a in self.jobs]
        self.out_shapes = [_sds(a.shape if kind == "scatter" else (N_DEV, *a.shape), a.dtype) for kind, a in self.jobs]
        n = len(self.jobs)
        self.scratch = [pltpu.SemaphoreType.DMA((n, 7)), pltpu.SemaphoreType.DMA((n, 7)), pltpu.SemaphoreType.DMA((n,))]

    def _setup(self, sems):
        send_sems, recv_sems, local_sems = sems
        x, y, c = _place()
        me = _lin(x, y, c)
        peers = [(x ^ ((k + 1) >> 2 & 1), y ^ ((k + 1) >> 1 & 1), c ^ ((k + 1) & 1)) for k in range(7)]

        def copy(a, k, sem_k, src, dst):
            return pltpu.make_async_remote_copy(src_ref=src, dst_ref=dst, send_sem=send_sems.at[a, sem_k],
                                                recv_sem=recv_sems.at[a, sem_k], device_id=peers[k], device_id_type=MESH)

        return me, [_lin(*p) for p in peers], copy, local_sems

    def _local(self, a, kind, ins, outs, me, local_sems):
        return pltpu.make_async_copy(ins[a].at[me] if kind == "scatter" else ins[a], outs[a].at[me], local_sems.at[a])

    def start(self, ins, outs, sems):
        me, theirs, copy, local_sems = self._setup(sems)
        for a, (kind, _) in enumerate(self.jobs):
            self._local(a, kind, ins, outs, me, local_sems).start()
            if kind == "scatter":
                for k in range(7):
                    copy(a, k, k, ins[a].at[theirs[k]], outs[a].at[me]).start()
            else:
                for k in (0,) + self.SAME_CORE:
                    copy(a, k, k, ins[a], outs[a].at[me]).start()

    def wait(self, ins, outs, sems):
        me, theirs, copy, local_sems = self._setup(sems)
        for a, (kind, _) in enumerate(self.jobs):
            if kind == "scatter":
                for k in range(7):
                    copy(a, k, k, ins[a].at[me], outs[a].at[theirs[k]]).wait_recv()
                for k in range(7):
                    copy(a, k, k, ins[a].at[theirs[k]], outs[a].at[me]).wait_send()
            else:
                for k in self.SAME_CORE:
                    blk = outs[a].at[theirs[k]]
                    copy(a, k, k, ins[a], blk).wait_recv()
                    copy(a, 0, k + 1, blk, blk).start()
                copy(a, 0, 0, ins[a], outs[a].at[theirs[0]]).wait_recv()
                for k in self.SAME_CORE:
                    copy(a, 0, k + 1, ins[a], outs[a].at[theirs[k + 1]]).wait_recv()
                for k in (0,) + self.SAME_CORE:
                    copy(a, k, k, ins[a], outs[a].at[me]).wait_send()
                for k in self.SAME_CORE:
                    blk = outs[a].at[theirs[k]]
                    copy(a, 0, k + 1, blk, blk).wait_send()
            self._local(a, kind, ins, outs, me, local_sems).wait()


def _carried(kern, carry, n_in, n_out, first, last):
    if carry is None:
        return kern
    nc = len(carry.jobs)

    def wrapped(*refs):
        ins, cin = refs[:n_in], refs[n_in:n_in + nc]
        outs, cout = refs[n_in + nc:n_in + nc + n_out], refs[n_in + nc + n_out:n_in + 2 * nc + n_out]
        scr, sems = refs[n_in + 2 * nc + n_out:-3], refs[-3:]

        @pl.when(first())
        def _():
            carry.start(cin, cout, sems)

        kern(*ins, *outs, *scr)

        @pl.when(last())
        def _():
            carry.wait(cin, cout, sems)

    return wrapped


def _carry_args(carry):
    if carry is None:
        return [], [], [], [], []
    n = len(carry.jobs)
    return [ANY] * n, carry.arrays, [ANY] * n, carry.out_shapes, carry.scratch


def _grid_ends(dims):
    first = lambda: functools.reduce(jnp.logical_and, [pl.program_id(d) == 0 for d in range(len(dims))])
    last = lambda: functools.reduce(jnp.logical_and, [pl.program_id(d) == n - 1 for d, n in enumerate(dims)])
    return first, last


def _mm_call(name, a, b, mode, out_dtype, tm, tn, rows_outer=True, single_b=False, carry=None):
    if mode == "TN":
        (K, M), N = a.shape, b.shape[1]
    else:
        (M, K), N = a.shape, (b.shape[1] if mode == "NN" else b.shape[0])
    assert M % tm == 0 and N % tn == 0, (name, M, N, K, tm, tn)
    ij = (lambda g0, g1: (g0, g1)) if rows_outer else (lambda g0, g1: (g1, g0))
    grid = (M // tm, N // tn) if rows_outer else (N // tn, M // tm)
    if mode == "TN":
        a_spec = pl.BlockSpec((K, tm), lambda g0, g1: (0, ij(g0, g1)[0]))
    else:
        a_spec = pl.BlockSpec((tm, K), lambda g0, g1: (ij(g0, g1)[0], 0))
    b_blk, b_map = ((tn, K), lambda g0, g1: (ij(g0, g1)[1], 0)) if mode == "NT" else \
                   ((K, tn), lambda g0, g1: (0, ij(g0, g1)[1]))
    b_spec = pl.BlockSpec(b_blk, b_map, pipeline_mode=pl.Buffered(1)) if single_b else pl.BlockSpec(b_blk, b_map)
    dims = {"NN": NN, "NT": NT, "TN": TN}[mode]

    def kern(a_ref, b_ref, o_ref):
        o_ref[...] = _dot(a_ref[...], b_ref[...], dims).astype(o_ref.dtype)

    ci, ca, co, cs, cscr = _carry_args(carry)
    res = pl.pallas_call(
        _carried(kern, carry, 2, 1, *_grid_ends(grid)), name=name, grid=grid, in_specs=[a_spec, b_spec] + ci,
        out_specs=[pl.BlockSpec((tm, tn), lambda g0, g1: ij(g0, g1))] + co,
        out_shape=[_sds((M, N), out_dtype)] + cs, scratch_shapes=cscr,
        compiler_params=_params(("arbitrary", "arbitrary")),
    )(a, b, *ca)
    return res[0] if carry is None else (res[0], res[1:])


def _mm_act(name, a, w, mode, out_dtype=BF16, carry=None):
    rows, K = a.shape
    N = w.shape[1] if mode == "NN" else w.shape[0]
    if K > D_FF:
        return _mm_call(name, a, w, mode, out_dtype, rows // 8, N, single_b=True, carry=carry)
    tn = N if N <= 1024 else 1408
    return _mm_call(name, a, w, mode, out_dtype, rows // 4, tn, carry=carry)


def _mm_wgrad(name, x, dy, out_dtype=BF16, carry=None):
    M = x.shape[1]
    tm = 1408 if M == D_FF else 512
    return _mm_call(name, x, dy, "TN", out_dtype, tm, dy.shape[1], single_b=True, carry=carry)


def _sigmoid(x):
    return 0.5 * jnp.tanh(0.5 * x) + 0.5


def _silu(x):
    return x * _sigmoid(x)


def _silu_grad(x):
    s = _sigmoid(x)
    return s * (1.0 + x * (1.0 - s))


_GELU_K = math.sqrt(2.0 / math.pi)


def _gelu(x):
    return 0.5 * x * (1.0 + jnp.tanh(_GELU_K * (x + 0.044715 * x * x * x)))


def _gelu_grad(x):
    t = jnp.tanh(_GELU_K * (x + 0.044715 * x * x * x))
    return 0.5 * (1.0 + t) + 0.5 * x * (1.0 - t * t) * _GELU_K * (1.0 + 3.0 * 0.044715 * x * x)


def _log_sigmoid(x):
    return jnp.minimum(x, 0.0) - jnp.log(1.0 + jnp.exp(-jnp.abs(x)))


def _rms(x):
    x = x.astype(F32)
    r = lax.rsqrt(jnp.mean(x * x, axis=-1, keepdims=True) + EPS)
    return x * r, r


def _rms_bwd(dy, y, r):
    return r * (dy - y * jnp.mean(dy * y, axis=-1, keepdims=True))


def _modrow(mod_ref, i, chunk):
    lo = mod_ref[0:1, chunk * D:(chunk + 1) * D]
    hi = mod_ref[1:2, chunk * D:(chunk + 1) * D]
    return jnp.where(i == 0, lo, hi)


def _acc_seg(acc_ref, i, val):
    zero = jnp.zeros_like(val)
    acc_ref[0:1, :] += jnp.where(i == 0, val, zero)
    acc_ref[1:2, :] += jnp.where(i == 0, zero, val)


def _colsum(x):
    return jnp.sum(x, axis=0, keepdims=True)


SH1, SC1, GA1, SH2, SC2, GA2 = range(6)


def _normmod_fwd(name, ctx, x, g, mod, c_sh, c_sc, carry=None):
    T = ctx.shape[0] + x.shape[0]
    assert ctx.shape[0] == TR
    n = T // TR

    def kern(ctx_ref, x_ref, g_ref, mod_ref, xa_ref, h_ref):
        i = pl.program_id(0)
        v = jnp.where(i == 0, ctx_ref[...], x_ref[...])
        xa_ref[...] = v
        y, _ = _rms(v)
        h = (y * g_ref[...]) * (1.0 + _modrow(mod_ref, i, c_sc)) + _modrow(mod_ref, i, c_sh)
        h_ref[...] = h.astype(BF16)

    row = pl.BlockSpec((TR, D), lambda i: (i, 0))
    ci, ca, co, cs, cscr = _carry_args(carry)
    res = pl.pallas_call(
        _carried(kern, carry, 4, 2, *_grid_ends((n,))), name=name, grid=(n,),
        in_specs=[pl.BlockSpec((TR, D), lambda i: (0, 0)), pl.BlockSpec((TR, D), lambda i: (jnp.maximum(i - 1, 0), 0)),
                  _full_spec(g.shape), _full_spec(mod.shape)] + ci,
        out_specs=[row, row] + co, out_shape=[_sds((T, D), F32), _sds((T, D), BF16)] + cs, scratch_shapes=cscr,
        compiler_params=_params(),
    )(ctx, x, g, mod, *ca)
    return res if carry is None else (res[:2], res[2:])


def _modrows(mod_ref, row0, n, chunk):
    t = row0 + lax.broadcasted_iota(jnp.int32, (n, 1), 0)
    return jnp.where(t < CTX, mod_ref[0:1, chunk * D:(chunk + 1) * D], mod_ref[1:2, chunk * D:(chunk + 1) * D])


def _loss_resid_bwd(name, x_out, target, mat, gpost, mod, c_ga):
    T = x_out.shape[0]

    def body(i, ins, ps, outs, acc):
        err = ins[0][...] - ins[1][...]
        lat = i > 0
        dx = jnp.where(lat, err * (1.0 / D), 0.0)
        outs[0][...] = dx
        acc[2][...] += jnp.where(lat, _colsum(err * err), 0.0)
        outs[1][...] = _resid_bwd_vals(i, dx, ins[2][...], ps[0][...], ps[1], c_ga, acc[0], acc[1]).astype(BF16)

    tgt_blk = ((TR, D), lambda i: (jnp.maximum(i - 1, 0), 0))
    return _ew(name, body, T // TR, [(x_out, *_rowblk(D)), (target, *tgt_blk), (mat, *_rowblk(D))], [gpost, mod],
               [(_sds((T, D), F32), *_rowblk(D)), (_sds((T, D), BF16), *_rowblk(D))],
               [_sds((2, D), F32), _sds((1, D), F32), _sds((1, D), F32)])


def _mod_for(mod_ref, i, chunk, row0, n):
    return _modrow(mod_ref, i, chunk) if row0 is None else _modrows(mod_ref, row0, n, chunk)


def _acc_for(acc_ref, i, v, row0):
    if row0 is None:
        _acc_seg(acc_ref, i, _colsum(v))
        return

    @pl.when(row0 < CTX)
    def _():
        is_ctx = row0 + lax.broadcasted_iota(jnp.int32, (v.shape[0], 1), 0) < CTX
        acc_ref[0:1, :] += _colsum(jnp.where(is_ctx, v, 0.0))
        acc_ref[1:2, :] += _colsum(jnp.where(is_ctx, 0.0, v))

    @pl.when(row0 >= CTX)
    def _():
        acc_ref[1:2, :] += _colsum(v)


def _resid_bwd_vals(i, dout, mat, gpost, mod_ref, c_ga, acc_ga, acc_g, row0=None):
    ym, rm = _rms(mat)
    ga = _mod_for(mod_ref, i, c_ga, row0, dout.shape[0])
    _acc_for(acc_ga, i, dout * (ym * gpost), row0)
    dn = dout * ga
    acc_g[...] += _colsum(dn * ym)
    return _rms_bwd(dn * gpost, ym, rm)


def _normmod_bwd_vals(i, dh, xin, g, mod_ref, c_sh, c_sc, acc_sh, acc_sc, acc_g, row0=None):
    dh = dh.astype(F32)
    y, r = _rms(xin)
    _acc_for(acc_sc, i, dh * (y * g), row0)
    _acc_for(acc_sh, i, dh, row0)
    dyg = dh * (1.0 + _mod_for(mod_ref, i, c_sc, row0, dh.shape[0]))
    acc_g[...] += _colsum(dyg * y)
    return _rms_bwd(dyg * g, y, r)


def _parts(i, tm):
    return [(slice(0, tm), i * tm)]


FT = 1408


def _ffn_in_fused(name, h2, w_t, carry=None):
    T = h2.shape[0]
    tm, nj = T // 4, D_FF // FT

    def kern(a_ref, bg_ref, bu_ref, fg_ref, fu_ref, s_ref):
        for rows, _ in _parts(0, tm):
            a = a_ref[rows, :]
            g = _dot(a, bg_ref[...], NT)
            u = _dot(a, bu_ref[...], NT)
            fg_ref[rows, :] = g.astype(BF16)
            fu_ref[rows, :] = u.astype(BF16)
            s_ref[rows, :] = (_silu(g) * u).astype(BF16)

    o_spec = pl.BlockSpec((tm, FT), lambda i, j: (i, j))
    ci, ca, co, cs, cscr = _carry_args(carry)
    res = pl.pallas_call(
        _carried(kern, carry, 3, 3, *_grid_ends((4, nj))), name=name, grid=(4, nj),
        in_specs=[pl.BlockSpec((tm, D), lambda i, j: (i, 0)), pl.BlockSpec((FT, D), lambda i, j: (j, 0)),
                  pl.BlockSpec((FT, D), lambda i, j: (j + nj, 0))] + ci,
        out_specs=[o_spec] * 3 + co, out_shape=[_sds((T, D_FF), BF16)] * 3 + cs, scratch_shapes=cscr,
        compiler_params=_params(("arbitrary", "arbitrary")),
    )(h2, w_t, w_t, *ca)
    return res if carry is None else (res[:3], res[3:])


def _norm_chain(row0, xin, mat, gpost, mod_ref, c_ga, gnext, modn_ref, c_sh, c_sc):
    n = xin.shape[0]
    ym, _ = _rms(mat.astype(BF16))
    xo = xin + _modrows(mod_ref, row0, n, c_ga) * (ym * gpost)
    y, _ = _rms(xo)
    h = (y * gnext) * (1.0 + _modrows(modn_ref, row0, n, c_sc)) + _modrows(modn_ref, row0, n, c_sh)
    return xo, h.astype(BF16)


def _out_fused(name, p, u, o_all, xa, w_o_rnn, w_o_attn, w_out, gpost, mod, gnext):
    T = u.shape[0]
    tm = T // 8

    def kern(g0, g1, g2, g3, u_ref, o_ref, xa_ref, wr_ref, wa_ref, w_ref, gpost_ref, mod_ref, gnext_ref,
             ya_ref, yb_ref, z_ref, m_ref, x1_ref, h2_ref):
        for rows, row0 in _parts(pl.program_id(0), tm):
            ya = _dot(u_ref[rows, :], wr_ref[...]).astype(BF16)
            yb = _dot(o_ref[rows, :], wa_ref[...]).astype(BF16)
            ya_ref[rows, :] = ya
            yb_ref[rows, :] = yb
            ga = _sigmoid(jnp.concatenate([g0[rows, :], g1[rows, :]], axis=1).astype(F32))
            gb = _sigmoid(jnp.concatenate([g2[rows, :], g3[rows, :]], axis=1).astype(F32))
            z = (ga * ya.astype(F32) + gb * yb.astype(F32)).astype(BF16)
            z_ref[rows, :] = z
            m = _dot(z, w_ref[...])
            m_ref[rows, :] = m.astype(BF16)
            x1_ref[rows, :], h2_ref[rows, :] = _norm_chain(row0, xa_ref[rows, :], m, gpost_ref[...], mod_ref, GA1,
                                                           gnext_ref[...], mod_ref, SH2, SC2)

    row = lambda w: pl.BlockSpec((tm, w), lambda i: (i, 0))
    return pl.pallas_call(
        kern, name=name, grid=(T // tm,),
        in_specs=[pl.BlockSpec((tm, GLB), lambda i, q=q: (i, COL_GL // GLB + q)) for q in range(4)]
                 + [row(D), row(D), row(D)] + [_full_spec(a.shape) for a in (w_o_rnn, w_o_attn, w_out, gpost, mod, gnext)],
        out_specs=[row(D)] * 6,
        out_shape=[_sds((T, D), BF16)] * 4 + [_sds((T, D), F32), _sds((T, D), BF16)],
        compiler_params=_params(),
    )(p, p, p, p, u, o_all, xa, w_o_rnn, w_o_attn, w_out, gpost, mod, gnext)


def _ffn_out_fused(name, s, w, x1, gpost, mod, nxt=None):
    T = s.shape[0]
    tm = T // 8

    def kern(s_ref, w_ref, x1_ref, gpost_ref, mod_ref, *rest):
        for rows, row0 in _parts(pl.program_id(0), tm):
            e = _dot(s_ref[rows, :], w_ref[...])
            if nxt is None:
                e_ref, xo_ref = rest
                ym, _ = _rms(e.astype(BF16))
                xo_ref[rows, :] = x1_ref[rows, :] + _modrows(mod_ref, row0, e.shape[0], GA2) * (ym * gpost_ref[...])
            else:
                gnext_ref, modn_ref, e_ref, xo_ref, h_ref = rest
                xo_ref[rows, :], h_ref[rows, :] = _norm_chain(row0, x1_ref[rows, :], e, gpost_ref[...], mod_ref, GA2,
                                                              gnext_ref[...], modn_ref, SH1, SC1)
            e_ref[rows, :] = e.astype(BF16)

    row = lambda w_: pl.BlockSpec((tm, w_), lambda i: (i, 0))
    extra = [] if nxt is None else list(nxt)
    return pl.pallas_call(
        kern, name=name, grid=(T // tm,),
        in_specs=[row(D_FF), _full_spec(w.shape), row(D), _full_spec(gpost.shape), _full_spec(mod.shape)]
                 + [_full_spec(a.shape) for a in extra],
        out_specs=[row(D)] * (2 if nxt is None else 3),
        out_shape=[_sds((T, D), BF16), _sds((T, D), F32)] + ([] if nxt is None else [_sds((T, D), BF16)]),
        compiler_params=_params(),
    )(s, w, x1, gpost, mod, *extra)


def _ffn_bwd_fused(name, fg, fu, w, de=None, head=None):
    T = fg.shape[0]
    tm = T // 8
    row = lambda w_: pl.BlockSpec((tm, w_), lambda i: (i, 0))
    w_spec = pl.BlockSpec(w.shape, lambda i: (0, 0), pipeline_mode=pl.Buffered(1))

    def tail(rows, de_v, fg_ref, fu_ref, w_ref, df_ref):
        ds = _dot(de_v, w_ref[...], NT)
        g, u = fg_ref[rows, :].astype(F32), fu_ref[rows, :].astype(F32)
        df_ref[rows, :] = jnp.concatenate([ds * u * _silu_grad(g), ds * _silu(g)], axis=1).astype(BF16)

    if head is None:
        def kern(de_ref, fg_ref, fu_ref, w_ref, df_ref):
            for rows, _ in _parts(pl.program_id(0), tm):
                tail(rows, de_ref[rows, :], fg_ref, fu_ref, w_ref, df_ref)

        return pl.pallas_call(
            kern, name=name, grid=(T // tm,), in_specs=[row(D), row(D_FF), row(D_FF), w_spec],
            out_specs=[row(2 * D_FF)], out_shape=[_sds((T, 2 * D_FF), BF16)], compiler_params=_params(),
        )(de, fg, fu, w)

    dx2, e, gpost, mod = head

    def kern(dx_ref, e_ref, fg_ref, fu_ref, w_ref, gpost_ref, mod_ref, de_ref, df_ref, dga_ref, dg_ref):
        i = pl.program_id(0)

        @pl.when(i == 0)
        def _():
            dga_ref[...] = jnp.zeros(dga_ref.shape, F32)
            dg_ref[...] = jnp.zeros(dg_ref.shape, F32)

        for rows, row0 in _parts(i, tm):
            de_v = _resid_bwd_vals(i, dx_ref[rows, :], e_ref[rows, :], gpost_ref[...], mod_ref, GA2, dga_ref, dg_ref,
                                   row0=row0).astype(BF16)
            de_ref[rows, :] = de_v
            tail(rows, de_v, fg_ref, fu_ref, w_ref, df_ref)

    return pl.pallas_call(
        kern, name=name, grid=(T // tm,),
        in_specs=[row(D), row(D), row(D_FF), row(D_FF), w_spec, _full_spec(gpost.shape), _full_spec(mod.shape)],
        out_specs=[row(D), row(2 * D_FF), _full_spec((2, D)), _full_spec((1, D))],
        out_shape=[_sds((T, D), BF16), _sds((T, 2 * D_FF), BF16), _sds((2, D), F32), _sds((1, D), F32)],
        compiler_params=_params(),
    )(dx2, e, fg, fu, w, gpost, mod)


def _zero_at_start(i, refs):
    @pl.when(i == 0)
    def _():
        for r in refs:
            r[...] = jnp.zeros(r.shape, F32)


def _proj_bwd_fused(name, dp, dgl, w_in_t, xa, dx1, gpre, mod, latent_only=False, carry=None):
    T = dp.shape[0]
    tm = T // 8
    n = T // tm
    row = lambda w_: pl.BlockSpec((tm, w_), lambda i: (i, 0))

    def tile_vals(i, dp_ref, dgl_ref, w_ref, xa_ref, dx1_ref, g_ref, mod_ref, dsh_ref, dsc_ref, dg_ref):
        _zero_at_start(i, (dsh_ref, dsc_ref, dg_ref))
        dh = _dot(dp_ref[...], w_ref[0:DP_W, :]) + _dot(dgl_ref[...], w_ref[DP_W:, :])
        return dx1_ref[...] + _normmod_bwd_vals(i, dh, xa_ref[...], g_ref[...], mod_ref, SH1, SC1, dsh_ref, dsc_ref,
                                                dg_ref, row0=i * tm)

    def kern(dp_ref, dgl_ref, w_ref, xa_ref, dx1_ref, g_ref, mod_ref, dxa_ref, dsh_ref, dsc_ref, dg_ref):
        dxa_ref[...] = tile_vals(pl.program_id(0), dp_ref, dgl_ref, w_ref, xa_ref, dx1_ref, g_ref, mod_ref, dsh_ref,
                                 dsc_ref, dg_ref)

    def kern_latent(dp_ref, dgl_ref, w_ref, xa_ref, dx1_ref, g_ref, mod_ref, dx_hbm, dsh_ref, dsc_ref, dg_ref, buf, sems):
        i = pl.program_id(0)

        def send(t, slot):
            head = pltpu.make_async_copy(buf.at[slot, pl.ds(CTX, tm - CTX)], dx_hbm.at[0, pl.ds(0, tm - CTX)],
                                         sems.at[slot])
            start = pl.multiple_of(jnp.maximum(t * tm - CTX, 0), 8)
            body = pltpu.make_async_copy(buf.at[slot], dx_hbm.at[0, pl.ds(start, tm)], sems.at[slot])
            return head, body

        def each(t, slot, fn):
            head, body = send(t, slot)

            @pl.when(t == 0)
            def _():
                fn(head)

            @pl.when(t > 0)
            def _():
                fn(body)

        slot = i % 2
        buf[slot] = tile_vals(i, dp_ref, dgl_ref, w_ref, xa_ref, dx1_ref, g_ref, mod_ref, dsh_ref, dsc_ref, dg_ref)
        each(i, slot, lambda cp: cp.start())

        @pl.when(i > 0)
        def _():
            each(i - 1, 1 - slot, lambda cp: cp.wait())

        @pl.when(i == n - 1)
        def _():
            each(i, slot, lambda cp: cp.wait())

    assert tm > CTX and (tm - CTX) % 8 == 0
    ci, ca, co, cs, cscr = _carry_args(carry)
    own = [pltpu.VMEM((2, tm, D), F32), pltpu.SemaphoreType.DMA((2,))] if latent_only else []
    res = pl.pallas_call(
        _carried(kern_latent if latent_only else kern, carry, 7, 4, *_grid_ends((n,))), name=name, grid=(n,),
        in_specs=[row(DP_W), row(P_W - DP_W),
                  pl.BlockSpec(w_in_t.shape, lambda i: (0, 0), pipeline_mode=pl.Buffered(1)), row(D), row(D),
                  _full_spec(gpre.shape), _full_spec(mod.shape)] + ci,
        out_specs=[ANY if latent_only else row(D), _full_spec((2, D)), _full_spec((2, D)), _full_spec((1, D))] + co,
        out_shape=[_sds((1, T - CTX, D) if latent_only else (T, D), F32), _sds((2, D), F32), _sds((2, D), F32),
                   _sds((1, D), F32)] + cs,
        scratch_shapes=own + cscr, compiler_params=_params(),
    )(dp, dgl, w_in_t, xa, dx1, gpre, mod, *ca)
    return res if carry is None else (res[:4], res[4:])


def _proj_wgrad(name, dp, dgl, h, carry=None):
    T, N = h.shape
    n1, n2 = DP_W // GLB, (P_W - DP_W) // GLB

    def kern(a1_ref, a2_ref, h_ref, o_ref):
        i = pl.program_id(0)

        @pl.when(i < n1)
        def _():
            o_ref[...] = _dot(a1_ref[...], h_ref[...], TN).astype(o_ref.dtype)

        @pl.when(i >= n1)
        def _():
            o_ref[...] = _dot(a2_ref[...], h_ref[...], TN).astype(o_ref.dtype)

    ci, ca, co, cs, cscr = _carry_args(carry)
    res = pl.pallas_call(
        _carried(kern, carry, 3, 1, *_grid_ends((n1 + n2,))), name=name, grid=(n1 + n2,),
        in_specs=[pl.BlockSpec((T, GLB), lambda i: (0, jnp.minimum(i, n1 - 1))),
                  pl.BlockSpec((T, GLB), lambda i: (0, jnp.maximum(i - n1, 0))),
                  pl.BlockSpec((T, N), lambda i: (0, 0), pipeline_mode=pl.Buffered(1))] + ci,
        out_specs=[pl.BlockSpec((GLB, N), lambda i: (i, 0))] + co,
        out_shape=[_sds((P_W, N), BF16)] + cs, scratch_shapes=cscr, compiler_params=_params(),
    )(dp, dgl, h, *ca)
    return res[0] if carry is None else (res[0], res[1:])


def _ffn_in_bwd_fused(name, df, w_t, x1, dres, mat, gpre, mod, gpost, carry=None):
    T = df.shape[0]
    tm = T // 8
    row = lambda w_: pl.BlockSpec((tm, w_), lambda i: (i, 0))

    def kern(df_ref, w_ref, x1_ref, dres_ref, mat_ref, gpre_ref, mod_ref, gpost_ref,
             dx1_ref, dm_ref, dsh_ref, dsc_ref, dgpre_ref, dga_ref, dgpost_ref):
        i = pl.program_id(0)
        _zero_at_start(i, (dsh_ref, dsc_ref, dgpre_ref, dga_ref, dgpost_ref))
        for rows, row0 in _parts(i, tm):
            dh2 = _dot(df_ref[rows, :], w_ref[...])
            dx1 = dres_ref[rows, :] + _normmod_bwd_vals(i, dh2, x1_ref[rows, :], gpre_ref[...], mod_ref, SH2, SC2,
                                                        dsh_ref, dsc_ref, dgpre_ref, row0=row0)
            dx1_ref[rows, :] = dx1
            dm_ref[rows, :] = _resid_bwd_vals(i, dx1, mat_ref[rows, :], gpost_ref[...], mod_ref, GA1, dga_ref,
                                              dgpost_ref, row0=row0).astype(BF16)

    ci, ca, co, cs, cscr = _carry_args(carry)
    res = pl.pallas_call(
        _carried(kern, carry, 8, 7, *_grid_ends((T // tm,))), name=name, grid=(T // tm,),
        in_specs=[row(2 * D_FF), pl.BlockSpec(w_t.shape, lambda i: (0, 0), pipeline_mode=pl.Buffered(1)), row(D),
                  row(D), row(D), _full_spec(gpre.shape), _full_spec(mod.shape), _full_spec(gpost.shape)] + ci,
        out_specs=[row(D), row(D), _full_spec((2, D)), _full_spec((2, D)), _full_spec((1, D)), _full_spec((2, D)),
                   _full_spec((1, D))] + co,
        out_shape=[_sds((T, D), F32), _sds((T, D), BF16), _sds((2, D), F32), _sds((2, D), F32), _sds((1, D), F32),
                   _sds((2, D), F32), _sds((1, D), F32)] + cs,
        scratch_shapes=cscr, compiler_params=_params(),
    )(df, w_t, x1, dres, mat, gpre, mod, gpost, *ca)
    return res if carry is None else (res[:7], res[7:])


def _out_bwd_fused(name, dm, w_out, w_o_rnn, w_o_attn, p, ya, yb):
    T = dm.shape[0]
    tm = T // 8
    row = lambda w_: pl.BlockSpec((tm, w_), lambda i: (i, 0))

    def kern(dm_ref, w_ref, wr_ref, wa_ref, g0, g1, g2, g3, ya_ref, yb_ref, dya_ref, dyb_ref, dgl_ref, du_ref, do_ref):
        for rows, _ in _parts(pl.program_id(0), tm):
            dz = _dot(dm_ref[rows, :], w_ref[...], NT)
            ga = _sigmoid(jnp.concatenate([g0[rows, :], g1[rows, :]], axis=1).astype(F32))
            gb = _sigmoid(jnp.concatenate([g2[rows, :], g3[rows, :]], axis=1).astype(F32))
            dya = (dz * ga).astype(BF16)
            dyb = (dz * gb).astype(BF16)
            dya_ref[rows, :] = dya
            dyb_ref[rows, :] = dyb
            dgl_ref[rows, :] = jnp.concatenate([dz * ya_ref[rows, :].astype(F32) * ga * (1.0 - ga),
                                                dz * yb_ref[rows, :].astype(F32) * gb * (1.0 - gb)],
                                               axis=1).astype(BF16)
            du_ref[rows, :] = _dot(dya, wr_ref[...], NT).astype(BF16)
            do_ref[rows, :] = _dot(dyb, wa_ref[...], NT).astype(BF16)

    return pl.pallas_call(
        kern, name=name, grid=(T // tm,),
        in_specs=[row(D)] + [_full_spec(w.shape) for w in (w_out, w_o_rnn, w_o_attn)]
                 + [pl.BlockSpec((tm, GLB), lambda i, q=q: (i, COL_GL // GLB + q)) for q in range(4)] + [row(D), row(D)],
        out_specs=[row(D), row(D), row(2 * D), row(D), row(D)],
        out_shape=[_sds((T, D), BF16), _sds((T, D), BF16), _sds((T, 2 * D), BF16), _sds((T, D), BF16),
                   _sds((T, D), BF16)],
        compiler_params=_params(),
    )(dm, w_out, w_o_rnn, w_o_attn, p, p, p, p, ya, yb)


AB = 128
CTX_BLKS = CTX // AB


def _rope_tables(S):
    pos = jnp.arange(S, dtype=jnp.int32)
    inv = ROPE_BASE ** (-jnp.arange(N_FREQ, dtype=F32) / N_FREQ)
    ang_r = (pos // GRID_W).astype(F32)[:, None] * inv[None, :]
    ang_c = (pos % GRID_W).astype(F32)[:, None] * inv[None, :]
    cos = jnp.concatenate([jnp.cos(ang_r)] * 2 + [jnp.cos(ang_c)] * 2, axis=1)
    sin = jnp.concatenate([-jnp.sin(ang_r), jnp.sin(ang_r), -jnp.sin(ang_c), jnp.sin(ang_c)], axis=1)
    return cos, sin


def _rope(x, cos, sin):
    w = x.shape[1]
    reps = w // HEAD
    lane = lax.broadcasted_iota(jnp.int32, x.shape, 1)
    partner = jnp.where((lane & 63) < 32, pltpu.roll(x, w - 32, 1), pltpu.roll(x, 32, 1))
    return x * jnp.tile(cos, (1, reps)) + partner * jnp.tile(sin, (1, reps))


def _unrope(dx, cos, sin):
    w = dx.shape[1]
    reps = w // HEAD
    lane = lax.broadcasted_iota(jnp.int32, dx.shape, 1)
    t = dx * jnp.tile(sin, (1, reps))
    partner = jnp.where((lane & 63) < 32, pltpu.roll(t, w - 32, 1), pltpu.roll(t, 32, 1))
    return dx * jnp.tile(cos, (1, reps)) + partner


def _qkv_prep(name, p, cos, sin, S):
    T = CTX + S
    nt = T // TR
    cb = CTX // TR
    KW = N_KV * HEAD

    def with_ones(v):
        ones = jnp.ones((TR, HEAD), BF16)
        return jnp.concatenate([v[:, kh * HEAD:(kh + 1) * HEAD] if part == 0 else ones
                                for kh in range(N_KV) for part in range(2)], axis=1)

    def kern(q_ref, k_ref, v_ref, cos_ref, sin_ref, qa_ref, kp_ref, vp_ref, kc_ref, vc_ref):
        i = pl.program_id(0)
        cos_v, sin_v = cos_ref[...], sin_ref[...]
        @pl.when(i < cb)
        def _():
            qa_ref[...] = (q_ref[...].astype(F32) * ATT_SCALE).astype(BF16)
            kc_ref[...] = k_ref[...]
            vc_ref[...] = with_ones(v_ref[...])

        @pl.when((i < cb) | (i >= nt))
        def _():
            kp_ref[...] = jnp.zeros(kp_ref.shape, BF16)
            vp_ref[...] = jnp.zeros(vp_ref.shape, BF16)

        @pl.when((i >= cb) & (i < nt))
        def _():
            qa_ref[...] = (_rope(q_ref[...].astype(F32), cos_v, sin_v) * ATT_SCALE).astype(BF16)
            kp_ref[...] = _rope(k_ref[...].astype(F32), cos_v, sin_v).astype(BF16)
            vp_ref[...] = with_ones(v_ref[...])

    tok = lambda i: jnp.minimum(i, nt - 1)
    lat_map = lambda i: (jnp.clip(i - cb, 0, nt - cb - 1), 0)
    ctx_map = lambda i: (jnp.minimum(i, cb - 1), 0)
    return pl.pallas_call(
        kern, name=name, grid=(nt + cb,),
        in_specs=[pl.BlockSpec((TR, N_Q * HEAD), lambda i: (tok(i), COL_Q // (N_Q * HEAD))),
                  pl.BlockSpec((TR, KW), lambda i: (tok(i), COL_K // KW)),
                  pl.BlockSpec((TR, KW), lambda i: (tok(i), COL_V // KW)),
                  pl.BlockSpec((TR, HEAD), lat_map), pl.BlockSpec((TR, HEAD), lat_map)],
        out_specs=[pl.BlockSpec((TR, N_Q * HEAD), lambda i: (tok(i), 0)),
                   pl.BlockSpec((TR, KW), lambda i: (i, 0)), pl.BlockSpec((TR, 2 * KW), lambda i: (i, 0)),
                   pl.BlockSpec((TR, KW), ctx_map), pl.BlockSpec((TR, 2 * KW), ctx_map)],
        out_shape=[_sds((T, N_Q * HEAD), BF16), _sds((S + 2 * CTX, KW), BF16), _sds((S + 2 * CTX, 2 * KW), BF16),
                   _sds((CTX, KW), BF16), _sds((CTX, 2 * KW), BF16)],
        compiler_params=_params(),
    )(p, p, p, cos, sin)


GW = Q_PER_KV * HEAD
HG = Q_PER_KV


def _band_bias(S):
    r = jnp.arange(AB, dtype=jnp.int32)[:, None]
    c = jnp.arange(3 * AB, dtype=jnp.int32)[None, :]
    near = jnp.abs(c - AB - r) <= AB
    valid = jnp.stack([near & (c >= AB), near, near & (c < 2 * AB)])
    return jnp.where(valid, 0.0, NEG_INF).astype(F32)


def _bias_spec(S):
    nb = S // AB
    return pl.BlockSpec((None, AB, 3 * AB), lambda kh, n: (jnp.where(n == 0, 0, jnp.where(n == nb - 1, 2, 1)), 0, 0))


def _head_probs(q, sink, kc, vce, kb, vbe, bias):
    s_c = _dot(q, kc, NT)
    m = jnp.maximum(jnp.max(s_c, axis=-1, keepdims=True), sink)
    if kb is not None:
        s_b = _dot(q, kb, NT) + bias
        m = jnp.maximum(m, jnp.max(s_b, axis=-1, keepdims=True))
    p_c = jnp.exp(s_c - m).astype(BF16)
    acc = _dot(p_c, vce)
    p_b = None
    if kb is not None:
        p_b = jnp.exp(s_b - m).astype(BF16)
        acc = acc + _dot(p_b, vbe)
    return p_c, p_b, m, acc


def _attn_fwd(name, qa, kc, vc, sink4, S, band=None, prev=None, carry=None):
    T = qa.shape[0]
    has_band = band is not None
    nq = S // AB if has_band else CTX_BLKS
    q_off = CTX_BLKS if has_band else 0

    def kern(*refs):
        q_ref, kc_ref, vc_ref, sink_ref = refs[:4]
        rest = refs[4:]
        o_ref = rest[-1]
        n = pl.program_id(1)
        kc_v, vce = kc_ref[...], vc_ref[...]
        kb = vbe = bias = None
        if has_band:
            kp_ref, vp_ref, bias_ref = rest[:3]
            start = pl.multiple_of(n * AB + (CTX - AB), AB)
            kb = kp_ref[pl.ds(start, 3 * AB), :]
            vbe = vp_ref[pl.ds(start, 3 * AB), :]
            bias = bias_ref[...]
        outs = []
        for g in range(Q_PER_KV):
            sink = sink_ref[g:g + 1, 0:1]
            _, _, m, acc = _head_probs(q_ref[:, g * HEAD:(g + 1) * HEAD], sink, kc_v, vce, kb, vbe, bias)
            l = acc[:, HEAD:] + jnp.exp(sink - m)
            outs.append(acc[:, :HEAD] / l)
        o_ref[...] = jnp.concatenate(outs, axis=1).astype(BF16)

    in_specs = [pl.BlockSpec((AB, GW), lambda kh, n: (n + q_off, kh)),
                pl.BlockSpec((CTX, HEAD), lambda kh, n: (0, kh)), pl.BlockSpec((CTX, 2 * HEAD), lambda kh, n: (0, kh)),
                pl.BlockSpec((None, Q_PER_KV, HEAD), lambda kh, n: (kh, 0, 0))]
    args = [qa, kc, vc, sink4]
    if has_band:
        in_specs += [pl.BlockSpec((S + 2 * CTX, HEAD), lambda kh, n: (0, kh)),
                     pl.BlockSpec((S + 2 * CTX, 2 * HEAD), lambda kh, n: (0, kh)), _bias_spec(S)]
        args += list(band)
    alias = {}
    if prev is not None:
        in_specs.append(ANY)
        alias = {len(args): 0}
        args.append(prev)
    ci, ca, co, cs, cscr = _carry_args(carry)
    res = pl.pallas_call(
        _carried(kern, carry, len(args), 1, *_grid_ends((N_KV, nq))), name=name, grid=(N_KV, nq),
        in_specs=in_specs + ci,
        out_specs=[pl.BlockSpec((AB, GW), lambda kh, n: (n + q_off, kh))] + co,
        out_shape=[_sds((T, N_Q * HEAD), BF16)] + cs, input_output_aliases=alias, scratch_shapes=cscr,
        compiler_params=_params(("arbitrary", "arbitrary")),
    )(*args, *ca)
    return res[0] if carry is None else (res[0], res[1:])


def _attn_bwd(name, qa, kc, vc, sink4, o_all, do_all, S, band=None, prev_dq=None, carry=None):
    T = qa.shape[0]
    has_band = band is not None
    nq = S // AB if has_band else CTX_BLKS
    q_off = CTX_BLKS if has_band else 0
    KW = N_KV * HEAD

    def kern(*refs):
        q_ref, kc_ref, vc_ref, sink_ref, o_ref, do_ref = refs[:6]
        rest = refs[6:]
        if has_band:
            kp_ref, vp_ref, bias_ref, cos_ref, sin_ref = rest[:5]
            rest = rest[5:]
        if prev_dq is not None:
            rest = rest[1:]
        dq_ref, dkc_ref, dvc_ref, dsink_ref = rest[:4]
        n = pl.program_id(1)

        @pl.when(n == 0)
        def _():
            dkc_ref[...] = jnp.zeros(dkc_ref.shape, F32)
            dvc_ref[...] = jnp.zeros(dvc_ref.shape, F32)
            dsink_ref[...] = jnp.zeros(dsink_ref.shape, F32)
            if has_band:
                rest[4][...] = jnp.zeros(rest[4].shape, F32)
                rest[5][...] = jnp.zeros(rest[5].shape, F32)

        kc_v, vce = kc_ref[...], vc_ref[...]
        vc_v = vce[:, :HEAD]
        kb = vbe = vb = bias = None
        if has_band:
            start = pl.multiple_of(n * AB + (CTX - AB), AB)
            kb = kp_ref[pl.ds(start, 3 * AB), :]
            vbe = vp_ref[pl.ds(start, 3 * AB), :]
            vb = vbe[:, :HEAD]
            bias = bias_ref[...]
        dq_parts, dsink_parts = [], []
        for g0 in range(0, Q_PER_KV, HG):
            heads = range(g0, g0 + HG)
            stack = lambda ref: jnp.concatenate([ref[:, g * HEAD:(g + 1) * HEAD] for g in heads], axis=0)
            q4, do4 = stack(q_ref), stack(do_ref)
            sink = jnp.concatenate([jnp.broadcast_to(sink_ref[g:g + 1, 0:1], (AB, 1)) for g in heads], axis=0)
            s_c = _dot(q4, kc_v, NT)
            m = jnp.maximum(jnp.max(s_c, axis=-1, keepdims=True), sink)
            if has_band:
                s_b = _dot(q4, kb, NT) + jnp.tile(bias, (HG, 1))
                m = jnp.maximum(m, jnp.max(s_b, axis=-1, keepdims=True))
            p_c = jnp.exp(s_c - m).astype(BF16).astype(F32)
            p_sink = jnp.exp(sink - m)
            l = jnp.sum(p_c, axis=-1, keepdims=True) + p_sink
            if has_band:
                p_b = jnp.exp(s_b - m).astype(BF16).astype(F32)
                l = l + jnp.sum(p_b, axis=-1, keepdims=True)
            inv = 1.0 / l
            delta = jnp.sum(do4.astype(F32) * stack(o_ref).astype(F32), axis=-1, keepdims=True)
            do4b = do4.astype(BF16)
            pn_c = (p_c * inv).astype(BF16)
            ds_c = (p_c * inv * (_dot(do4b, vc_v, NT) - delta)).astype(BF16)
            dq4 = _dot(ds_c, kc_v)
            dkc_ref[...] += _dot(q4, ds_c, TN)
            dvc_ref[...] += _dot(do4b, pn_c, TN)
            if has_band:
                pn_b = (p_b * inv).astype(BF16)
                ds_b = (p_b * inv * (_dot(do4b, vb, NT) - delta)).astype(BF16)
                dq4 = dq4 + _dot(ds_b, kb)
                rest[4][:, pl.ds(start, 3 * AB)] += _dot(q4, ds_b, TN)
                rest[5][:, pl.ds(start, 3 * AB)] += _dot(do4b, pn_b, TN)
            dq4 = dq4 * ATT_SCALE
            dq_parts += [dq4[k * AB:(k + 1) * AB, :] for k in range(HG)]
            ps = p_sink * inv * delta
            dsink_parts += [jnp.broadcast_to(-jnp.sum(ps[k * AB:(k + 1) * AB, :], axis=0, keepdims=True), (1, HEAD))
                            for k in range(HG)]
        dq = jnp.concatenate(dq_parts, axis=1)
        dq_ref[...] = (_unrope(dq, cos_ref[...], sin_ref[...]) if has_band else dq).astype(BF16)
        dsink_ref[...] += jnp.concatenate(dsink_parts, axis=0)

    q_spec = pl.BlockSpec((AB, GW), lambda kh, n: (n + q_off, kh))
    c_spec = pl.BlockSpec((CTX, HEAD), lambda kh, n: (0, kh))
    ce_spec = pl.BlockSpec((CTX, 2 * HEAD), lambda kh, n: (0, kh))
    s_spec = pl.BlockSpec((None, Q_PER_KV, HEAD), lambda kh, n: (kh, 0, 0))
    in_specs = [q_spec, c_spec, ce_spec, s_spec, q_spec, q_spec]
    args = [qa, kc, vc, sink4, o_all, do_all]
    ct_spec = pl.BlockSpec((HEAD, CTX), lambda kh, n: (kh, 0))
    dq_spec = pl.BlockSpec((AB, GW), lambda kh, n: (n + q_off, COL_Q // GW + kh))
    out_specs = [dq_spec, ct_spec, ct_spec, s_spec]
    out_shape = [_sds((T, DP_W), BF16), _sds((KW, CTX), F32), _sds((KW, CTX), F32), _sds((N_KV, Q_PER_KV, HEAD), F32)]
    if has_band:
        p_spec = pl.BlockSpec((S + 2 * CTX, HEAD), lambda kh, n: (0, kh))
        pt_spec = pl.BlockSpec((HEAD, S + 2 * CTX), lambda kh, n: (kh, 0))
        rope_spec = pl.BlockSpec((AB, HEAD), lambda kh, n: (n, 0))
        in_specs += [p_spec, pl.BlockSpec((S + 2 * CTX, 2 * HEAD), lambda kh, n: (0, kh)), _bias_spec(S), rope_spec,
                     rope_spec]
        args += list(band)
        out_specs += [pt_spec, pt_spec]
        out_shape += [_sds((KW, S + 2 * CTX), F32)] * 2
    alias = {}
    if prev_dq is not None:
        in_specs.append(ANY)
        alias = {len(args): 0}
        args.append(prev_dq)
    ci, ca, co, cs, cscr = _carry_args(carry)
    n_out = len(out_specs)
    res = pl.pallas_call(
        _carried(kern, carry, len(args), n_out, *_grid_ends((N_KV, nq))), name=name, grid=(N_KV, nq),
        in_specs=in_specs + ci, out_specs=out_specs + co, out_shape=out_shape + cs, scratch_shapes=cscr,
        input_output_aliases=alias, compiler_params=_params(("arbitrary", "arbitrary")),
    )(*args, *ca)
    return res if carry is None else (res[:n_out], res[n_out:])


def _dkv_assemble(name, dp, dkp, dvp, dkc_l, dvc_l, dkc_c, dvc_c, cos, sin, S):
    T = CTX + S
    KW = N_KV * HEAD

    def kern(dkp_ref, dvp_ref, dkcl_ref, dvcl_ref, dkcc_ref, dvcc_ref, cos_ref, sin_ref, dp_in, out_ref):
        i = pl.program_id(0)

        @pl.when(i == 0)
        def _():
            out_ref[...] = jnp.concatenate([(dkcl_ref[...] + dkcc_ref[...]).T, (dvcl_ref[...] + dvcc_ref[...]).T],
                                           axis=1).astype(BF16)

        @pl.when(i > 0)
        def _():
            out_ref[...] = jnp.concatenate([_unrope(dkp_ref[...].T, cos_ref[...], sin_ref[...]), dvp_ref[...].T],
                                           axis=1).astype(BF16)

    same = lambda i: (0, i)
    lat_map = lambda i: (jnp.maximum(i - 1, 0), 0)
    ctx_map = lambda i: (0, 0)
    return pl.pallas_call(
        kern, name=name, grid=(T // TR,),
        in_specs=[pl.BlockSpec((KW, TR), same), pl.BlockSpec((KW, TR), same),
                  pl.BlockSpec((KW, CTX), ctx_map), pl.BlockSpec((KW, CTX), ctx_map),
                  pl.BlockSpec((KW, CTX), ctx_map), pl.BlockSpec((KW, CTX), ctx_map),
                  pl.BlockSpec((TR, HEAD), lat_map), pl.BlockSpec((TR, HEAD), lat_map), ANY],
        out_specs=pl.BlockSpec((TR, 2 * KW), lambda i: (i, COL_K // (2 * KW))),
        out_shape=_sds((T, DP_W), BF16), input_output_aliases={8: 0}, compiler_params=_params(),
    )(dkp, dvp, dkc_l, dvc_l, dkc_c, dvc_c, cos, sin, dp)


RB = 128
CH = 256
HALO = 8
SUB = 8
GRP = 8


def _vscan(a, b, reverse):
    row = lax.broadcasted_iota(jnp.int32, a.shape, 0)
    A, H = a, b
    for s in (1, 2, 4):
        sh = SUB - s if reverse else s
        m = (row < SUB - s) if reverse else (row >= s)
        As = pltpu.roll(A, sh, 0)
        Hs = pltpu.roll(H, sh, 0)
        H = jnp.where(m, A * Hs + H, H)
        A = jnp.where(m, A * As, A)
    return A, H


def _scan_rows(a_ref, b_ref, r0, nrows, reverse, carry, emit):
    ngrp = nrows // (SUB * GRP)
    row = lax.broadcasted_iota(jnp.int32, (SUB, RB), 0)

    def grp(gi, carry):
        g = (ngrp - 1 - gi) if reverse else gi
        base = r0 + g * (SUB * GRP)
        for v in (range(GRP - 1, -1, -1) if reverse else range(GRP)):
            rs = pl.multiple_of(base + v * SUB, SUB)
            A, H = _vscan(a_ref[pl.ds(rs, SUB), :], b_ref[pl.ds(rs, SUB), :], reverse)
            hf = H + A * carry
            if reverse:
                before = jnp.where(row == SUB - 1, carry, pltpu.roll(hf, SUB - 1, 0))
                carry = hf[0:1, :]
            else:
                before = jnp.where(row == 0, carry, pltpu.roll(hf, 1, 0))
                carry = hf[SUB - 1:SUB, :]
            emit(rs, hf, before)
        return carry

    return lax.fori_loop(0, ngrp, grp, carry)


def _pad_start(ci):
    return pl.multiple_of(ci * CH + HALO * jnp.minimum(ci, 1), HALO)


def _conv_taps(ext, transpose=False):
    n = CH + 2 * HALO
    taps = []
    for k in range(CONV_W):
        off = CONV_LEFT - k if transpose else k - CONV_LEFT
        taps.append(ext[HALO:HALO + CH, :] if off == 0 else pltpu.roll(ext, (-off) % n, 0)[HALO:HALO + CH, :])
    return taps


def _lru_gates(xl, w4, b4, ls):
    pre = _dot(xl.astype(BF16), w4) + b4
    out = []
    for d in range(2):
        r = _sigmoid(pre[:, d * RB:(d + 1) * RB])
        i = _sigmoid(pre[:, (2 + d) * RB:(3 + d) * RB])
        la = LRU_C * r * ls[d:d + 1, :]
        a = jnp.exp(la)
        q = -jnp.tanh(la) * (1.0 + a * a)
        out.append((r, i, a, q))
    return out


def _rnn_specs(T):
    col = lambda n, *_: (0, n)
    return dict(
        xr=pl.BlockSpec((T, RB), lambda n, *_: (0, COL_XR // RB + n)),
        gr=pl.BlockSpec((T, RB), lambda n, *_: (0, COL_GR // RB + n)),
        act=pl.BlockSpec((T, RB), col),
        cw=pl.BlockSpec((CONV_W, RB), col), cb=pl.BlockSpec((1, RB), col),
        w4=pl.BlockSpec((None, RB, 4 * RB), lambda n, *_: (n, 0, 0)),
        b4=pl.BlockSpec((None, 1, 4 * RB), lambda n, *_: (n, 0, 0)),
        lam=pl.BlockSpec((2, RB), col))


PAD_ROWS = 3 * HALO


def _zero_pads(pad_ref, T):
    for r in (0, HALO + CTX, 2 * HALO + T):
        pad_ref[r:r + HALO, :] = jnp.zeros((HALO, RB), F32)


def _fill_padded(pad_ref, src_ref, T):
    _zero_pads(pad_ref, T)
    pad_ref[HALO:HALO + CTX, :] = src_ref[0:CTX, :].astype(F32)
    pad_ref[2 * HALO + CTX:2 * HALO + T, :] = src_ref[CTX:T, :].astype(F32)


def _pad_rows(ci):
    return pl.ds(pl.multiple_of(ci * CH + HALO + HALO * jnp.minimum(ci, 1), HALO), CH)


def _rnn_fwd(name, p, cw, cb, w4, b4, lam, T, carry=None):
    def kern(xr_ref, gr_ref, cw_ref, cb_ref, w4_ref, b4_ref, lam_ref,
             u_ref, a0, a1, yo_ref, hpf_ref, hpb_ref, r0_ref, r1_ref, i0_ref, i1_ref, xpad, b0, b1, y):
        _fill_padded(xpad, xr_ref, T)
        ls = _log_sigmoid(lam_ref[...])
        w4v, b4v, cwv, cbv = w4_ref[...], b4_ref[...], cw_ref[...], cb_ref[...]

        def chunk(ci, _):
            rows = pl.ds(pl.multiple_of(ci * CH, CH), CH)
            taps = _conv_taps(xpad[pl.ds(_pad_start(ci), CH + 2 * HALO), :])
            xl = cbv + sum(taps[k] * cwv[k:k + 1, :] for k in range(CONV_W))
            for d, (r, i, a, q) in enumerate(_lru_gates(xl, w4v, b4v, ls)):
                (a0, a1)[d][rows, :] = a
                (b0, b1)[d][rows, :] = jnp.sqrt(q) * (i * xl)
                (r0_ref, r1_ref)[d][rows, :] = r.astype(BF16)
                (i0_ref, i1_ref)[d][rows, :] = i.astype(BF16)
            return 0

        lax.fori_loop(0, T // CH, chunk, 0)
        zero = jnp.zeros((1, RB), F32)

        def emit_f(rs, hf, before):
            y[pl.ds(rs, SUB), :] = hf
            b0[pl.ds(rs, SUB), :] = before

        def emit_b(rs, hf, before):
            y[pl.ds(rs, SUB), :] += hf
            b1[pl.ds(rs, SUB), :] = before

        _scan_rows(a0, b0, 0, T, False, zero, emit_f)
        c = _scan_rows(a1, b1, 0, CTX, True, zero, emit_b)
        _scan_rows(a1, b1, CTX, T - CTX, True, c, emit_b)

        def finish(ci, _):
            rows = pl.ds(pl.multiple_of(ci * CH, CH), CH)
            yv = y[rows, :]
            u_ref[rows, :] = (yv * _gelu(gr_ref[rows, :].astype(F32))).astype(BF16)
            yo_ref[rows, :] = yv.astype(BF16)
            hpf_ref[rows, :] = b0[rows, :].astype(BF16)
            hpb_ref[rows, :] = b1[rows, :].astype(BF16)
            return 0

        lax.fori_loop(0, T // CH, finish, 0)

    sp = _rnn_specs(T)
    ci, ca, co, cs, cscr = _carry_args(carry)
    dts = [BF16, F32, F32] + [BF16] * 7
    res = pl.pallas_call(
        _carried(kern, carry, 7, 10, *_grid_ends((N_RNN_BLOCKS,))), name=name, grid=(N_RNN_BLOCKS,),
        in_specs=[sp["xr"], sp["gr"], sp["cw"], sp["cb"], sp["w4"], sp["b4"], sp["lam"]] + ci,
        out_specs=[sp["act"]] * 10 + co,
        out_shape=[_sds((T, D), dt) for dt in dts] + cs,
        scratch_shapes=[pltpu.VMEM((T + PAD_ROWS, RB), F32)] + [pltpu.VMEM((T, RB), F32)] * 3 + cscr,
        compiler_params=_params(),
    )(p, p, cw, cb, w4, b4, lam, *ca)
    return res if carry is None else (res[:10], res[10:])


def _rnn_bwd(name, p, du, saved, dp, cw, cb, w4, b4, lam, T, carry=None):
    def kern(xr_ref, gr_ref, du_ref, a0, a1, y_ref, hpf_ref, hpb_ref, r0_ref, r1_ref, i0_ref, i1_ref,
             cw_ref, cb_ref, w4_ref, b4_ref, lam_ref, dp_in,
             dp_ref, dcw_ref, dcb_ref, dw4_ref, db4_ref, dlam_ref,
             xpad, dxpad, c0, c1, dy):
        j = pl.program_id(1)

        @pl.when(j == 0)
        def _():
            scans(gr_ref, du_ref, a0, a1, y_ref, dp_ref, c0, c1, dy)

        @pl.when(j == 1)
        def _():
            gates(xr_ref, a0, a1, (hpf_ref, hpb_ref), (r0_ref, r1_ref), (i0_ref, i1_ref), cw_ref, cb_ref, w4_ref,
                  lam_ref, dp_ref, dcw_ref, dcb_ref, dw4_ref, db4_ref, dlam_ref, xpad, dxpad, c0, c1)

    def scans(gr_ref, du_ref, a0, a1, y_ref, dgr_ref, c0, c1, dy):
        def phase_a(ci, _):
            rows = pl.ds(pl.multiple_of(ci * CH, CH), CH)
            gr = gr_ref[rows, :].astype(F32)
            duv = du_ref[rows, :].astype(F32)
            dyv = duv * _gelu(gr)
            dgr_ref[rows, :] = (duv * y_ref[rows, :].astype(F32) * _gelu_grad(gr)).astype(BF16)
            dy[rows, :] = dyv
            c0[rows, :] = a0[rows, :] * dyv
            c1[rows, :] = a1[rows, :] * dyv
            return 0

        lax.fori_loop(0, T // CH, phase_a, 0)
        zero = jnp.zeros((1, RB), F32)

        def emit0(rs, hf, before):
            c0[pl.ds(rs, SUB), :] = dy[pl.ds(rs, SUB), :] + before

        def emit1(rs, hf, before):
            c1[pl.ds(rs, SUB), :] = dy[pl.ds(rs, SUB), :] + before

        _scan_rows(a0, c0, 0, T, True, zero, emit0)
        c = _scan_rows(a1, c1, CTX, T - CTX, False, zero, emit1)
        _scan_rows(a1, c1, 0, CTX, False, c, emit1)

    def gates(xr_ref, a0, a1, hp_refs, r_refs, i_refs, cw_ref, cb_ref, w4_ref, lam_ref,
              dxr_ref, dcw_ref, dcb_ref, dw4_ref, db4_ref, dlam_ref, xpad, dxpad, c0, c1):
        _fill_padded(xpad, xr_ref, T)
        _zero_pads(dxpad, T)
        lam_v = lam_ref[...]
        ls = _log_sigmoid(lam_v)
        w4v, cwv, cbv = w4_ref[...], cw_ref[...], cb_ref[...]

        def conv_chunk(ci):
            taps = _conv_taps(xpad[pl.ds(_pad_start(ci), CH + 2 * HALO), :])
            return taps, cbv + sum(taps[k] * cwv[k:k + 1, :] for k in range(CONV_W))

        dw4_ref[...] = jnp.zeros(dw4_ref.shape, F32)
        db4_ref[...] = jnp.zeros(db4_ref.shape, F32)
        dlam_ref[...] = jnp.zeros(dlam_ref.shape, F32)
        dcw_ref[...] = jnp.zeros(dcw_ref.shape, F32)
        dcb_ref[...] = jnp.zeros(dcb_ref.shape, F32)

        def phase_c(ci, _):
            base = pl.multiple_of(ci * CH, CH)
            rows = pl.ds(base, CH)
            _, xl = conv_chunk(ci)
            dxl = jnp.zeros((CH, RB), F32)
            dpre_a, dpre_x, dls = [], [], []
            for d in range(2):
                a = (a0, a1)[d][rows, :]
                r = r_refs[d][rows, :].astype(F32)
                i = i_refs[d][rows, :].astype(F32)
                q = -jnp.tanh(LRU_C * r * ls[d:d + 1, :]) * (1.0 + a * a)
                g = (c0, c1)[d][rows, :]
                hp = hp_refs[d][rows, :].astype(F32)
                gm = g * jnp.sqrt(q)
                di = gm * xl
                dxl = dxl + gm * i
                dla = a * (g * hp - a * (g * (i * xl)) * lax.rsqrt(q))
                dr = dla * (LRU_C * ls[d:d + 1, :])
                dls.append(_colsum(dla * (LRU_C * r)))
                dpre_a.append(dr * r * (1.0 - r))
                dpre_x.append(di * i * (1.0 - i))
            dpre = jnp.concatenate(dpre_a + dpre_x, axis=1)
            dpre_b = dpre.astype(BF16)
            dxl = dxl + _dot(dpre_b, w4v, NT)
            dw4_ref[...] += _dot(xl.astype(BF16), dpre_b, TN)
            db4_ref[...] += _colsum(dpre)
            dlam_ref[...] += jnp.concatenate(dls, axis=0)
            dcb_ref[...] += _colsum(dxl)
            dxpad[_pad_rows(ci), :] = dxl
            return 0

        lax.fori_loop(0, T // CH, phase_c, 0)
        dlam_ref[...] = dlam_ref[...] * _sigmoid(-lam_v)

        def phase_d(ci, _):
            base = pl.multiple_of(ci * CH, CH)
            rows = pl.ds(base, CH)
            xtaps, _ = conv_chunk(ci)
            dtaps = _conv_taps(dxpad[pl.ds(_pad_start(ci), CH + 2 * HALO), :], transpose=True)
            dxl = dxpad[_pad_rows(ci), :]
            dxr_ref[rows, :] = sum(dtaps[k] * cwv[k:k + 1, :] for k in range(CONV_W)).astype(BF16)
            dcw_ref[...] += jnp.concatenate([_colsum(dxl * xtaps[k]) for k in range(CONV_W)], axis=0)
            return 0

        lax.fori_loop(0, T // CH, phase_d, 0)

    sp = _rnn_specs(T)
    dp_spec = pl.BlockSpec((T, RB), lambda n, j: (0, COL_GR // RB + n - j * (COL_GR - COL_XR) // RB))
    ci, ca, co, cs, cscr = _carry_args(carry)
    n_in = 3 + len(saved) + 5 + 1
    res = pl.pallas_call(
        _carried(kern, carry, n_in, 6, *_grid_ends((N_RNN_BLOCKS, 2))), name=name, grid=(N_RNN_BLOCKS, 2),
        in_specs=[sp["xr"], sp["gr"]] + [sp["act"]] * (1 + len(saved)) + [sp["cw"], sp["cb"], sp["w4"], sp["b4"],
                                                                           sp["lam"], ANY] + ci,
        out_specs=[dp_spec, sp["cw"], sp["cb"], sp["w4"], sp["b4"], sp["lam"]] + co,
        out_shape=[_sds((T, DP_W), BF16), _sds((CONV_W, D), F32), _sds((1, D), F32),
                   _sds((N_RNN_BLOCKS, RB, 4 * RB), F32), _sds((N_RNN_BLOCKS, 1, 4 * RB), F32), _sds((2, D), F32)] + cs,
        scratch_shapes=[pltpu.VMEM((T + PAD_ROWS, RB), F32)] * 2 + [pltpu.VMEM((T, RB), F32)] * 3 + cscr,
        input_output_aliases={n_in - 1: 0},
        compiler_params=_params(("arbitrary", "arbitrary")),
    )(p, p, du, *saved, cw, cb, w4, b4, lam, dp, *ca)
    return res if carry is None else (res[:6], res[6:])


class _Plan:
    def __init__(self, shards, Ws):
        L = len(Ws)
        self.shards, self.Ws = shards, Ws
        self.Gs = [None] * L
        self.slots = [dict() for _ in range(L)]
        self.gate_slots = [None] * L
        self.table = {}
        for l in range(L):
            t = f"l{l}_"
            self.table[t + "rnn_fwd"] = [("gather", l, k) for k in ("wffn_in_t", "wo_rnn", "wo_attn", "wout")]
            if l + 1 < L:
                self.table[t + "attn_lat_fwd"] = [("gather", l + 1, "win_t")]
                self.table[t + "ffn_in"] = [("gather", l, "wffn_out")]
            else:
                self.table[t + "attn_lat_fwd"] = [("gather", l, "wffn_out")]
            self.table[t + "ffn_in_dx"] = [("scatter", l, "wffn_out")]
            self.table[t + "attn_lat_bwd"] = [("scatter", l, "wffn_in_t")]
            self.table[t + "proj_dx"] = [("scatter", l, "win_t_a" if l > 0 else "win_t_b")]
            self.table[t + "rnn_bwd"] = ([("scatter", l, k) for k in ("wout", "wo_attn", "wo_rnn")]
                                         + ([("scatter", l + 1, "win_t_b"), ("gates", l + 1, "w4")] if l + 1 < L else []))
        self.table["l0_proj_dw_a"] = [("gates", 0, "w4")]
        self.table["l0_proj_dw_b"] = [("scatter", 0, "win_t_a")]
        self.table["l0_mix_norm"] = [("gather", 0, "win_t")]

    def carry(self, name):
        jobs = []
        for kind, l, k in self.table.get(name, []):
            if kind == "gather":
                jobs.append(("gather", self.shards[l][k]))
            elif kind == "scatter":
                jobs.append(("scatter", self.Gs[l][k].reshape(N_DEV, -1, self.Gs[l][k].shape[-1])))
            else:
                jobs.append(("gather", self.Gs[l]["w4"].reshape(N_RNN_BLOCKS * RB, 4 * RB).astype(BF16)))
        return _Carry(jobs) if jobs else None

    def done(self, name, got):
        for (kind, l, k), res in zip(self.table[name], got):
            if kind == "gather":
                self.Ws[l][k] = res.reshape(-1, D)
            elif kind == "scatter":
                self.slots[l][k] = res
            else:
                self.gate_slots[l] = res


def _run(X, fn, name, *args, **kw):
    carry = None if X is None else X.carry(name)
    if carry is None:
        return fn(name, *args, **kw)
    out, got = fn(name, *args, carry=carry, **kw)
    X.done(name, got)
    return out


def _layer_fwd(l, xa, h, W, rope, S, nxt, X=None):
    T = xa.shape[0]
    tag = f"l{l}_"
    cos, sin, bias = rope
    p = _run(X, _mm_act, tag + "proj", h, W["win_t"], "NT", BF16)
    u, *rnn_saved = _run(X, _rnn_fwd, tag + "rnn_fwd", p, W["cw"], W["cb"], W["w4"], W["b4"], W["lam"], T)
    qa, kp, vp, kc, vc = _qkv_prep(tag + "qkv_prep", p, cos, sin, S)
    o_all = _attn_fwd(tag + "attn_ctx_fwd", qa, kc, vc, W["sink4"], S)
    o_all = _run(X, _attn_fwd, tag + "attn_lat_fwd", qa, kc, vc, W["sink4"], S, band=(kp, vp, bias), prev=o_all)
    ya, yb, z, m, x1, h2 = _out_fused(tag + "out", p, u, o_all, xa, W["wo_rnn"], W["wo_attn"], W["wout"],
                                      W["g_mix_post"], W["mod"], W["g_ffn_pre"])
    fg, fu, s = _run(X, _ffn_in_fused, tag + "ffn_in", h2, W["wffn_in_t"])
    e, *out = _ffn_out_fused(tag + "ffn_out", s, W["wffn_out"], x1, W["g_ffn_post"], W["mod"], nxt)
    saved = dict(xa=xa, h=h, p=p, u=u, rnn=rnn_saved, qa=qa, kp=kp, vp=vp, kc=kc, vc=vc, o_all=o_all,
                 ya=ya, yb=yb, z=z, m=m, x1=x1, h2=h2, fg=fg, fu=fu, s=s, e=e)
    return saved, out


def _layer_bwd(l, dx2, A, W, rope, S, X=None, loss_of=None):
    T = A["xa"].shape[0]
    tag = f"l{l}_"
    cos, sin, bias = rope
    G = {}
    if X is not None:
        X.Gs[l] = G
    if loss_of is None:
        de, df, dga2, G["g_ffn_post"] = _ffn_bwd_fused(tag + "ffn_bwd", A["fg"], A["fu"], W["wffn_out"],
                                                       head=(dx2, A["e"], W["g_ffn_post"], W["mod"]))
    else:
        dx2, de, dga2, G["g_ffn_post"], G["sq"] = _loss_resid_bwd(tag + "loss_ffn_resid_bwd", *loss_of, A["e"],
                                                                  W["g_ffn_post"], W["mod"], GA2)
        df, = _ffn_bwd_fused(tag + "ffn_bwd", A["fg"], A["fu"], W["wffn_out"], de=de)
    G["wffn_out"] = _mm_wgrad(tag + "ffn_out_dw", A["s"], de)
    dx1, dm, dsh2, dsc2, G["g_ffn_pre"], dga1, G["g_mix_post"] = _run(
        X, _ffn_in_bwd_fused, tag + "ffn_in_dx", df, W["wffn_in_t"], A["x1"], dx2, A["m"], W["g_ffn_pre"], W["mod"],
        W["g_mix_post"])
    G["wffn_in_t"] = _run(X, _mm_wgrad, tag + "ffn_in_dw", df, A["h2"])
    G["wout"] = _mm_wgrad(tag + "out_dw", A["z"], dm)
    dya, dyb, dgl, du, do = _out_bwd_fused(tag + "out_dx", dm, W["wout"], W["wo_rnn"], W["wo_attn"], A["p"], A["ya"],
                                           A["yb"])
    G["wo_attn"] = _mm_wgrad(tag + "o_attn_dw", A["o_all"], dyb)
    G["wo_rnn"] = _mm_wgrad(tag + "o_rnn_dw", A["u"], dya)
    dp, dkc_c, dvc_c, dsink_c = _attn_bwd(tag + "attn_ctx_bwd", A["qa"], A["kc"], A["vc"], W["sink4"], A["o_all"], do, S)
    dp, dkc_l, dvc_l, dsink_l, dkp, dvp = _run(
        X, _attn_bwd, tag + "attn_lat_bwd", A["qa"], A["kc"], A["vc"], W["sink4"], A["o_all"], do, S,
        band=(A["kp"], A["vp"], bias, cos, sin), prev_dq=dp)
    G["sink4"] = dsink_c + dsink_l
    dp = _dkv_assemble(tag + "dkv", dp, dkp, dvp, dkc_l, dvc_l, dkc_c, dvc_c, cos, sin, S)
    dp, G["cw"], G["cb"], G["w4"], G["b4"], G["lam"] = _run(
        X, _rnn_bwd, tag + "rnn_bwd", A["p"], du, A["rnn"], dp, W["cw"], W["cb"], W["w4"], W["b4"], W["lam"], T)
    proj_dx = (_proj_bwd_fused, tag + "proj_dx", dp, dgl, W["win_t"], A["xa"], dx1, W["g_mix_pre"], W["mod"])
    if X is not None:
        G["win_t_a"] = _run(X, _proj_wgrad, tag + "proj_dw_a", dp, dgl, A["h"][:, :D // 2])
        if l > 0:
            dxa, dsh1, dsc1, G["g_mix_pre"] = _run(X, *proj_dx)
        G["win_t_b"] = _run(X, _proj_wgrad, tag + "proj_dw_b", dp, dgl, A["h"][:, D // 2:])
        if l == 0:
            dxa, dsh1, dsc1, G["g_mix_pre"] = _run(X, *proj_dx, latent_only=True)
    else:
        dxa, dsh1, dsc1, G["g_mix_pre"] = _run(X, *proj_dx)
        G["win_t"] = _proj_wgrad(tag + "proj_dw", dp, dgl, A["h"])
    G["mod"] = jnp.concatenate([dsh1, dsc1, dga1, dsh2, dsc2, dga2], axis=1)
    return dxa, G


def _local_step(ctx, x, target, Ws, S, X=None):
    rope = (*_rope_tables(S), _band_bias(S))
    L = len(Ws)
    x, h = _run(X, _normmod_fwd, "l0_mix_norm", ctx, x, Ws[0]["g_mix_pre"], Ws[0]["mod"], SH1, SC1)
    saved = []
    for l in range(L):
        nxt = (Ws[l + 1]["g_mix_pre"], Ws[l + 1]["mod"]) if l + 1 < L else None
        A, out = _layer_fwd(l, x, h, Ws[l], rope, S, nxt, X)
        saved.append(A)
        if l + 1 < L:
            x, h = out
    Gs = [None] * L
    dx = None
    for l in reversed(range(L)):
        dx, Gs[l] = _layer_bwd(l, dx, saved[l], Ws[l], rope, S, X, loss_of=(out[0], target) if l == L - 1 else None)
    return Gs[L - 1]["sq"], dx, Gs


MESH = pl.DeviceIdType.MESH


def _place():
    return lax.axis_index("x"), lax.axis_index("y"), lax.axis_index("c")


def _lin(px, py, pc):
    return 4 * px + 2 * py + pc


def _allgather_small(name, blk):
    m, n = blk.shape

    def body(x_ref, out_ref, send_sems, recv_sems, local_sem):
        x, y, c = _place()
        me, sibling = (x, y, c), (x, y, 1 - c)
        chips = [(1 - x, y), (x, 1 - y), (1 - x, 1 - y)]

        def copy(k, block, to, src=None):
            dst = out_ref.at[_lin(*block)]
            return pltpu.make_async_remote_copy(src_ref=dst if src is None else src, dst_ref=dst,
                                                send_sem=send_sems.at[k], recv_sem=recv_sems.at[k],
                                                device_id=to, device_id_type=MESH)

        mine = pltpu.make_async_copy(x_ref, out_ref.at[_lin(*me)], local_sem)
        mine.start()
        first = [copy(0, me, sibling, src=x_ref)]
        first += [copy(1 + j, me, (*chip, c), src=x_ref) for j, chip in enumerate(chips)]
        for cp in first:
            cp.start()
        passed = [copy(4 + j, (*chip, c), sibling) for j, chip in enumerate(chips)]
        for j, chip in enumerate(chips):
            copy(1 + j, (*chip, c), me).wait_recv()
            passed[j].start()
        copy(0, sibling, me).wait_recv()
        for j, chip in enumerate(chips):
            copy(4 + j, (*chip, 1 - c), me).wait_recv()
        for cp in first + passed:
            cp.wait_send()
        mine.wait()

    return pl.pallas_call(
        body, name=name, out_shape=_sds((N_DEV, m, n), blk.dtype),
        in_specs=[pl.BlockSpec(memory_space=pltpu.VMEM)], out_specs=pl.BlockSpec(memory_space=pltpu.VMEM),
        scratch_shapes=[pltpu.SemaphoreType.DMA((7,)), pltpu.SemaphoreType.DMA((7,)), pltpu.SemaphoreType.DMA],
        compiler_params=pltpu.CompilerParams(vmem_limit_bytes=VMEM_LIMIT),
    )(blk)


MOD_ROWS = 16
MOD_SHARD = 6 * D // N_DEV
HI = lax.Precision.HIGHEST


def _mod_fwd(name, c9, w_mod, b_shard):
    L = w_mod.shape[0]

    def kern(c_ref, w_ref, b_ref, o_ref):
        o_ref[...] = lax.dot_general(_silu(c_ref[...]), w_ref[...], NN, precision=HI,
                                     preferred_element_type=F32) + b_ref[...]

    return pl.pallas_call(
        kern, name=name, grid=(L,),
        in_specs=[_full_spec(c9.shape), pl.BlockSpec((None, D, MOD_SHARD), lambda l: (l, 0, 0)),
                  pl.BlockSpec((None, 1, MOD_SHARD), lambda l: (l, 0, 0))],
        out_specs=pl.BlockSpec((None, MOD_ROWS, MOD_SHARD), lambda l: (l, 0, 0)),
        out_shape=_sds((L, MOD_ROWS, MOD_SHARD), F32), compiler_params=_params(),
    )(c9, w_mod, b_shard)


def _mod_bwd(name, c9, w_mod, dmod_all, dmod_cols):
    L = w_mod.shape[0]

    def rows9(ref, l):
        own = jnp.concatenate([ref[j, 2 * l + 1:2 * l + 2, :] for j in range(N_DEV)], axis=0)
        ctx = ref[0, 2 * l:2 * l + 1, :]
        for j in range(1, N_DEV):
            ctx = ctx + ref[j, 2 * l:2 * l + 1, :]
        return own, ctx

    def kern(c_ref, w_ref, all_ref, cols_ref, gw_ref, gb_ref, gc_ref):
        l = pl.program_id(0)
        for ll in range(L):
            @pl.when(l == ll)
            def _():
                own, ctx = rows9(all_ref, ll)
                gb_ref[...] = _colsum(own) + ctx
                own_s, ctx_s = rows9(cols_ref, ll)
                r16 = jnp.concatenate([own_s, ctx_s, jnp.zeros((MOD_ROWS - N_DEV - 1, MOD_SHARD), F32)], axis=0)
                gw_ref[...] = lax.dot_general(_silu(c_ref[...]), r16, TN, precision=HI, preferred_element_type=F32)
                part = lax.dot_general(r16, w_ref[...], NT, precision=HI,
                                       preferred_element_type=F32)[N_DEV:N_DEV + 1, :]
                if ll == 0:
                    gc_ref[...] = part
                else:
                    gc_ref[...] += part

    return pl.pallas_call(
        kern, name=name, grid=(L,),
        in_specs=[_full_spec(c9.shape), pl.BlockSpec((None, D, MOD_SHARD), lambda l: (l, 0, 0)),
                  _full_spec(dmod_all.shape), _full_spec(dmod_cols.shape)],
        out_specs=[pl.BlockSpec((None, D, MOD_SHARD), lambda l: (l, 0, 0)),
                   pl.BlockSpec((None, 1, 6 * D), lambda l: (l, 0, 0)), _full_spec((1, D))],
        out_shape=[_sds((L, D, MOD_SHARD), F32), _sds((L, 1, 6 * D), F32), _sds((1, D), F32)],
        compiler_params=_params(),
    )(c9, w_mod, dmod_all, dmod_cols)


_BC1 = 1.0 - ADAM_B1 ** ADAM_STEP
_BC2 = 1.0 - ADAM_B2 ** ADAM_STEP


def _adamw_vals(w, g, m, v):
    m = ADAM_B1 * m + (1.0 - ADAM_B1) * g
    v = ADAM_B2 * v + (1.0 - ADAM_B2) * (g * g)
    delta = -ADAM_LR * ((m / _BC1) / (jnp.sqrt(v / _BC2) + ADAM_EPS) + ADAM_WD * w)
    return delta, m, v


def _adamw(name, w, g, m, v, tile):
    R, C = w.shape
    blk = ((tile, C), lambda i: (i, 0))

    def body(i, ins, ps, outs, acc):
        d, mm, vv = _adamw_vals(ins[0][...], ins[1][...], ins[2][...], ins[3][...])
        outs[0][...] = d
        outs[1][...] = mm
        outs[2][...] = vv

    return _ew(name, body, R // tile, [(a, *blk) for a in (w, g, m, v)], [], [(_sds((R, C), F32), *blk)] * 3)


def _sum_slots(ref):
    g = ref[0].astype(F32)
    for j in range(1, N_DEV):
        g = g + ref[j].astype(F32)
    return g


def _adamw_slots(name, slots, shape, tile, wmv=None):
    L, R, C = shape
    n = R // tile
    spec = pl.BlockSpec((None, tile, C), lambda l, i: (l, i, 0))
    pieces = [s if isinstance(s, (list, tuple)) else [s] for s in slots]
    layer_of = [ll for ll, ps in enumerate(pieces) for _ in ps]
    flat = [p for ps in pieces for p in ps]
    wmv = list(wmv or [])

    def slot_spec(ll, cols):
        return pl.BlockSpec((N_DEV, tile, cols),
                            lambda l, i: (0, jnp.where(l == ll, i, jnp.where(l < ll, 0, n - 1)), 0))

    def kern(*refs):
        s_refs = refs[:len(flat)]
        rest = refs[len(flat):]
        l = pl.program_id(0)
        for ll in range(L):
            @pl.when(l == ll)
            def _():
                parts = [_sum_slots(r) for r, lr in zip(s_refs, layer_of) if lr == ll]
                g = parts[0] if len(parts) == 1 else jnp.concatenate(parts, axis=1)
                if wmv:
                    w_ref, m_ref, v_ref, g_ref, d_ref, mo_ref, vo_ref = rest
                    d_ref[...], mo_ref[...], vo_ref[...] = _adamw_vals(w_ref[...], g, m_ref[...], v_ref[...])
                else:
                    g_ref, = rest
                g_ref[...] = g

    n_out = 4 if wmv else 1
    return pl.pallas_call(
        kern, name=name, grid=(L, n),
        in_specs=[slot_spec(ll, p.shape[-1]) for ll, p in zip(layer_of, flat)] + [spec] * len(wmv),
        out_specs=[spec] * n_out, out_shape=[_sds((L, R, C), F32)] * n_out,
        compiler_params=_params(("arbitrary", "arbitrary")),
    )(*flat, *wmv)


def _sum_blocks(name, blocks):
    _, R, C = blocks.shape

    def kern(b_ref, o_ref):
        o_ref[...] = _sum_slots(b_ref)

    return pl.pallas_call(kern, name=name, in_specs=[_full_spec(blocks.shape)], out_specs=_full_spec((R, C)),
                          grid=(1,), out_shape=_sds((R, C), F32), compiler_params=_params())(blocks)


BIG = ("win_t", "wo_rnn", "wo_attn", "wout", "wffn_in_t", "wffn_out")
BIG_SRC = ("w_in", "w_o_rnn", "w_o_attn", "w_out", "w_ffn_in", "w_ffn_out")
BIG_T = (True, False, False, False, True, False)
BIG_TILE = (176, 128, 128, 128, 176, 176)


def _chan_full(g8):
    return jnp.transpose(g8, (1, 0, 2)).reshape(g8.shape[1], D)


def kernel(x, c, ctx, c_ctx, w_mod, b_mod, g_mix_pre, g_mix_post, g_ffn_pre, g_ffn_post, w_in, conv_w, conv_b, lru_wa, lru_ba, lru_wx, lru_bx, lru_lam, attn_sink, w_o_rnn, w_o_attn, w_out, w_ffn_in, w_ffn_out, loss_target, m_c_ctx, m_w_mod, m_b_mod, m_g_mix_pre, m_g_mix_post, m_g_ffn_pre, m_g_ffn_post, m_w_in, m_conv_w, m_conv_b, m_lru_wa, m_lru_ba, m_lru_wx, m_lru_bx, m_lru_lam, m_attn_sink, m_w_o_rnn, m_w_o_attn, m_w_out, m_w_ffn_in, m_w_ffn_out, v_c_ctx, v_w_mod, v_b_mod, v_g_mix_pre, v_g_mix_post, v_g_ffn_pre, v_g_ffn_post, v_w_in, v_conv_w, v_conv_b, v_lru_wa, v_lru_ba, v_lru_wx, v_lru_bx, v_lru_lam, v_attn_sink, v_w_o_rnn, v_w_o_attn, v_w_out, v_w_ffn_in, v_w_ffn_out):
    P = dict(c_ctx=c_ctx, w_mod=w_mod, b_mod=b_mod, g_mix_pre=g_mix_pre, g_mix_post=g_mix_post, g_ffn_pre=g_ffn_pre,
             g_ffn_post=g_ffn_post, w_in=w_in, conv_w=conv_w, conv_b=conv_b, lru_wa=lru_wa, lru_ba=lru_ba,
             lru_wx=lru_wx, lru_bx=lru_bx, lru_lam=lru_lam, attn_sink=attn_sink, w_o_rnn=w_o_rnn, w_o_attn=w_o_attn,
             w_out=w_out, w_ffn_in=w_ffn_in, w_ffn_out=w_ffn_out)
    Mo = dict(c_ctx=m_c_ctx, w_mod=m_w_mod, b_mod=m_b_mod, g_mix_pre=m_g_mix_pre, g_mix_post=m_g_mix_post,
              g_ffn_pre=m_g_ffn_pre, g_ffn_post=m_g_ffn_post, w_in=m_w_in, conv_w=m_conv_w, conv_b=m_conv_b,
              lru_wa=m_lru_wa, lru_ba=m_lru_ba, lru_wx=m_lru_wx, lru_bx=m_lru_bx, lru_lam=m_lru_lam,
              attn_sink=m_attn_sink, w_o_rnn=m_w_o_rnn, w_o_attn=m_w_o_attn, w_out=m_w_out, w_ffn_in=m_w_ffn_in,
              w_ffn_out=m_w_ffn_out)
    Vo = dict(c_ctx=v_c_ctx, w_mod=v_w_mod, b_mod=v_b_mod, g_mix_pre=v_g_mix_pre, g_mix_post=v_g_mix_post,
              g_ffn_pre=v_g_ffn_pre, g_ffn_post=v_g_ffn_post, w_in=v_w_in, conv_w=v_conv_w, conv_b=v_conv_b,
              lru_wa=v_lru_wa, lru_ba=v_lru_ba, lru_wx=v_lru_wx, lru_bx=v_lru_bx, lru_lam=v_lru_lam,
              attn_sink=v_attn_sink, w_o_rnn=v_w_o_rnn, w_o_attn=v_w_o_attn, w_out=v_w_out, w_ffn_in=v_w_ffn_in,
              w_ffn_out=v_w_ffn_out)
    L = w_in.shape[0]
    S = x.shape[1]
    me = _lin(*_place())

    small = jnp.concatenate([c.reshape(8, 128), conv_w.reshape(L * CONV_W, 128), lru_ba.reshape(2 * L, 128),
                             lru_bx.reshape(2 * L, 128), lru_lam.reshape(2 * L, 128), jnp.zeros((4, 128), F32)], axis=0)
    small_all = _allgather_small("ag_small", small)
    c_all = small_all[:, 0:8].reshape(N_DEV, D)
    conv_w_f = _chan_full(small_all[:, 8:16]).reshape(L, CONV_W, D)
    lru_ba_f = _chan_full(small_all[:, 16:20]).reshape(L, 2, D)
    lru_bx_f = _chan_full(small_all[:, 20:24]).reshape(L, 2, D)
    lru_lam_f = _chan_full(small_all[:, 24:28]).reshape(L, 2, D)

    c9 = jnp.concatenate([c_all, c_ctx[None], jnp.zeros((MOD_ROWS - N_DEV - 1, D), F32)], axis=0)
    b_shard = lax.dynamic_slice_in_dim(b_mod, me * MOD_SHARD, MOD_SHARD, axis=1)[:, None, :]
    mod_part = _mod_fwd("mod_fwd", c9, w_mod, b_shard)
    mod_all = _allgather_small("ag_mod", mod_part.reshape(L * MOD_ROWS, MOD_SHARD))
    mod_all = jnp.transpose(mod_all.reshape(N_DEV, L, MOD_ROWS, MOD_SHARD), (1, 2, 0, 3)).reshape(L, MOD_ROWS, 6 * D)
    own_row = lax.dynamic_index_in_dim(mod_all, me, axis=1, keepdims=False)
    modrows = jnp.stack([mod_all[:, N_DEV], own_row], axis=1)

    shards = [{k: (P[src][l].T if tr else P[src][l]).astype(BF16) for k, src, tr in zip(BIG, BIG_SRC, BIG_T)}
              for l in range(L)]
    Ws = []
    for l in range(L):
        W = {}
        W.update(
            cw=conv_w_f[l], cb=conv_b[l][None],
            w4=jnp.concatenate([lru_wa[l, 0], lru_wa[l, 1], lru_wx[l, 0], lru_wx[l, 1]], axis=-1).astype(BF16),
            b4=jnp.concatenate([lru_ba_f[l, 0].reshape(N_RNN_BLOCKS, 1, RB), lru_ba_f[l, 1].reshape(N_RNN_BLOCKS, 1, RB),
                                lru_bx_f[l, 0].reshape(N_RNN_BLOCKS, 1, RB), lru_bx_f[l, 1].reshape(N_RNN_BLOCKS, 1, RB)],
                               axis=-1),
            lam=lru_lam_f[l], sink4=jnp.broadcast_to(attn_sink[l].reshape(N_KV, Q_PER_KV, 1), (N_KV, Q_PER_KV, HEAD)),
            g_mix_pre=g_mix_pre[l][None], g_mix_post=g_mix_post[l][None], g_ffn_pre=g_ffn_pre[l][None],
            g_ffn_post=g_ffn_post[l][None], mod=modrows[l])
        Ws.append(W)

    plan = _Plan(shards, Ws)
    sq, dxa, Gs = _local_step(ctx[0], x[0], loss_target[0], Ws, S, plan)
    loss_part = ((0.5 / D) * jnp.sum(sq)).reshape(1, 1)
    grad_x = dxa

    dmod = jnp.concatenate([Gs[l]["mod"] for l in range(L)] + [jnp.zeros((8 - 2 * L, 6 * D), F32)], axis=0)
    dmod_all = _allgather_small("ag_dmod", dmod)
    dmod_cols = lax.dynamic_slice_in_dim(dmod_all, me * MOD_SHARD, MOD_SHARD, axis=2)
    g_w_mod, g_b_mod, dsc_part = _mod_bwd("mod_bwd", c9, w_mod, dmod_all, dmod_cols)
    g_b_mod = g_b_mod[:, 0]

    def rows(name, shape):
        return jnp.concatenate([Gs[l][name].reshape(shape) for l in range(L)], axis=0)

    b4g = [Gs[l]["b4"].reshape(N_RNN_BLOCKS, 4, RB) for l in range(L)]
    sink_row = jnp.concatenate([Gs[l]["sink4"][:, :, 0].reshape(1, N_Q) for l in range(L)]
                               + [loss_part, jnp.zeros((1, D - L * N_Q - 1), F32)], axis=1)
    small_g = jnp.concatenate(
        [rows("g_mix_pre", (1, D)), rows("g_mix_post", (1, D)), rows("g_ffn_pre", (1, D)), rows("g_ffn_post", (1, D)),
         rows("cb", (1, D)), rows("cw", (CONV_W, D))]
        + [b4g[l][:, d].reshape(1, D) for l in range(L) for d in range(2)]
        + [b4g[l][:, 2 + d].reshape(1, D) for l in range(L) for d in range(2)]
        + [rows("lam", (2, D)), sink_row, dsc_part], axis=0)
    n_small = small_g.shape[0]
    small_tot = _sum_blocks("sum_small", _allgather_small("ag_small_grads", small_g))
    o = 0
    G = {}
    for name in ("g_mix_pre", "g_mix_post", "g_ffn_pre", "g_ffn_post", "conv_b"):
        G[name] = small_tot[o:o + L]
        o += L
    G["conv_w"] = small_tot[o:o + L * CONV_W].reshape(L, CONV_W, D)
    o += L * CONV_W
    for name in ("lru_ba", "lru_bx", "lru_lam"):
        G[name] = small_tot[o:o + 2 * L].reshape(L, 2, D)
        o += 2 * L
    G["attn_sink"] = small_tot[o, :L * N_Q].reshape(L, N_Q)
    loss = small_tot[o, L * N_Q]
    sg = jax.nn.sigmoid(c_ctx)
    G["c_ctx"] = small_tot[o + 1] * (sg * (1.0 + c_ctx * (1.0 - sg)))
    G["b_mod"] = g_b_mod
    G["w_mod"] = g_w_mod

    for l in range(L):
        plan.slots[l]["win_t"] = [plan.slots[l]["win_t_a"], plan.slots[l]["win_t_b"]]

    out_g, out_d, out_m, out_v = {}, {}, {}, {}

    def put(name, res, shape=None):
        g, d, m, v = res
        for dst, val in ((out_g, g), (out_d, d), (out_m, m), (out_v, v)):
            dst[name] = val if shape is None else val.reshape(shape)

    for k, src, tr, tile in zip(BIG, BIG_SRC, BIG_T, BIG_TILE):
        lay = (lambda a: jnp.swapaxes(a, 1, 2)) if tr else (lambda a: a)
        wmv = (lay(P[src]), lay(Mo[src]), lay(Vo[src]))
        res = _adamw_slots("adamw_" + src, [plan.slots[l][k] for l in range(L)], wmv[0].shape, tile, wmv)
        put(src, [lay(r) for r in res])
    res = _adamw("adamw_w_mod", w_mod.reshape(L * D, MOD_SHARD), g_w_mod.reshape(L * D, MOD_SHARD),
                 m_w_mod.reshape(L * D, MOD_SHARD), v_w_mod.reshape(L * D, MOD_SHARD), 256)
    put("w_mod", (g_w_mod,) + tuple(res), w_mod.shape)
    def fuse4(wa, wx):
        return jnp.concatenate([wa[:, 0], wa[:, 1], wx[:, 0], wx[:, 1]], axis=-1).reshape(L, N_RNN_BLOCKS * RB, 4 * RB)

    res = _adamw_slots("adamw_gates", plan.gate_slots, (L, N_RNN_BLOCKS * RB, 4 * RB), 256,
                       (fuse4(lru_wa, lru_wx), fuse4(m_lru_wa, m_lru_wx), fuse4(v_lru_wa, v_lru_wx)))
    res = [r.reshape(L, N_RNN_BLOCKS, RB, 4, RB) for r in res]
    put("lru_wa", [jnp.stack([r[:, :, :, 0], r[:, :, :, 1]], axis=1) for r in res])
    put("lru_wx", [jnp.stack([r[:, :, :, 2], r[:, :, :, 3]], axis=1) for r in res])
    rep = ("g_mix_pre", "g_mix_post", "g_ffn_pre", "g_ffn_post", "conv_b", "b_mod")

    def pack_rep(T_):
        sink = jnp.concatenate([T_["attn_sink"].reshape(1, L * N_Q), jnp.zeros((1, D - L * N_Q), F32)], axis=1)
        return jnp.concatenate([T_[n].reshape(-1, D) for n in rep] + [sink, T_["c_ctx"][None]], axis=0)

    pk = [pack_rep(T_) for T_ in (P, G, Mo, Vo)]
    n_rep = pk[0].shape[0]
    res = _adamw("adamw_replicated", *[jnp.pad(a, ((0, 24 - n_rep), (0, 0))) for a in pk], 24)
    res = (pk[1],) + tuple(r[:n_rep] for r in res)
    o = 0
    for n in rep:
        k = P[n].size // D
        put(n, [r[o:o + k] for r in res], P[n].shape)
        o += k
    put("attn_sink", [r[o, :L * N_Q] for r in res], attn_sink.shape)
    put("c_ctx", [r[o + 1] for r in res], c_ctx.shape)
    chan = ("conv_w", "lru_ba", "lru_bx", "lru_lam")
    g_own = {n: lax.dynamic_slice_in_dim(G[n], me * RB, RB, axis=2) for n in chan}

    def pack_chan(T_):
        return jnp.concatenate([T_[n].reshape(-1, RB) for n in chan], axis=0)

    pk = [pack_chan(T_) for T_ in (P, g_own, Mo, Vo)]
    n_ch = pk[0].shape[0]
    res = _adamw("adamw_channels", *[jnp.pad(a, ((0, 24 - n_ch), (0, 0))) for a in pk], 24)
    res = (pk[1],) + tuple(r[:n_ch] for r in res)
    o = 0
    for n in chan:
        k = P[n].size // RB
        put(n, [r[o:o + k] for r in res], P[n].shape)
        o += k

    order = ("c_ctx", "w_mod", "b_mod", "g_mix_pre", "g_mix_post", "g_ffn_pre", "g_ffn_post", "w_in", "conv_w", "conv_b",
             "lru_wa", "lru_ba", "lru_wx", "lru_bx", "lru_lam", "attn_sink", "w_o_rnn", "w_o_attn", "w_out", "w_ffn_in",
             "w_ffn_out")
    return (loss, grad_x, *[out_g[n] for n in order], *[out_d[n] for n in order], *[out_m[n] for n in order],
            *[out_v[n] for n in order])
```
